```python
import functools
import jax, jax.numpy as jnp
from jax import lax
import numpy as np


D_MODEL = 1024
BATCH = 16
SEQ = 2048
DEPTH = 2

GRID_W = 64
CTX_LEN = 256
N_MIXERS = 2
N_MOD = 6
EPS = 1e-6
GLA_HEADS = 4
GLA_KEY_DIM = D_MODEL // 2
GLA_VAL_DIM = D_MODEL
GLA_HEAD_K = GLA_KEY_DIM // GLA_HEADS
GLA_HEAD_V = GLA_VAL_DIM // GLA_HEADS
GLA_GATE_RANK = 16
GLA_GATE_TAU = 16.0
GLA_CHUNK = 64
GLA_IN_DIM = 2 * GLA_KEY_DIM + 2 * GLA_VAL_DIM + 2 * GLA_GATE_RANK
SC_DIM = D_MODEL
CONV_WIDTH = 3
FFN_HIDDEN = 5 * D_MODEL // 2

kernel_name = 'hybrid_gla_shortconv_convffn_dit'


def rmsnorm(x, gain):
    x32 = x.astype(jnp.float32)
    y = x32 * lax.rsqrt(jnp.mean(x32 * x32, axis=-1, keepdims=True) + EPS)
    return y.astype(x.dtype) * gain


def modulate(x, gain, shift, scale):
    return rmsnorm(x, gain) * (1 + scale) + shift


def dwconv3(u, w, axis):
    n = u.shape[axis]
    pad = [(0, 0)] * u.ndim
    pad[axis] = (1, 1)
    up = jnp.pad(u, pad)
    out = lax.slice_in_dim(up, 0, n, axis=axis) * w[0]
    for tap in range(1, CONV_WIDTH):
        out = out + lax.slice_in_dim(up, tap, tap + n, axis=axis) * w[tap]
    return out


def conv_grid(u, w, rows, axis):
    b, t, ch = u.shape
    return dwconv3(u.reshape(b, rows, GRID_W, ch), w, axis).reshape(b, t, ch)


def conv_seq(u, w):
    return dwconv3(u, w, 1)


def heads(t, dh):
    return t.reshape(t.shape[0], t.shape[1], -1, dh)


def gla_log_decay(a_low, w_a2, b_a):
    z = (a_low @ w_a2 + b_a).astype(jnp.float32)
    return heads(jax.nn.log_sigmoid(z) / GLA_GATE_TAU, GLA_HEAD_K)


def gla_scan(q, k, v, log_a, s0):
    bsz, t, nh, _ = q.shape
    dv = v.shape[-1]
    n = t // GLA_CHUNK

    def to_chunks(a):
        return a.astype(jnp.float32).reshape(bsz, n, GLA_CHUNK, nh, a.shape[-1]).transpose(1, 0, 3, 2, 4)

    xs = tuple(to_chunks(a) for a in (q, k, v, log_a))
    mask = jnp.tril(jnp.ones((GLA_CHUNK, GLA_CHUNK), dtype=bool))

    def step(s, inp):
        qi, ki, vi, gi = inp
        bcum = jnp.cumsum(gi, axis=-2)
        b_last = bcum[..., -1:, :]
        q_s = qi * jnp.exp(bcum)
        k_s = ki * jnp.exp(-bcum)
        k_d = ki * jnp.exp(b_last - bcum)
        att = jnp.where(mask, jnp.einsum('bhik,bhjk->bhij', q_s, k_s), 0.0)
        o = jnp.einsum('bhik,bhkv->bhiv', q_s, s) + jnp.einsum('bhij,bhjv->bhiv', att, vi)
        s_new = jnp.exp(b_last[..., 0, :])[..., None] * s + jnp.einsum('bhjk,bhjv->bhkv', k_d, vi)
        return s_new, o

    s_fin, oc = lax.scan(step, s0.astype(jnp.float32), xs)
    o = oc.transpose(1, 0, 3, 2, 4).reshape(bsz, t, nh, dv)
    return o, s_fin


def gla_state(k, v, log_a):
    bcum = jnp.cumsum(log_a, axis=1)
    k_d = k.astype(jnp.float32) * jnp.exp(bcum[:, -1:] - bcum)
    return jnp.einsum('bthk,bthv->bhkv', k_d, v.astype(jnp.float32))


def gla_split_cols(p):
    kt, vt, r = GLA_KEY_DIM, GLA_VAL_DIM, GLA_GATE_RANK
    return jnp.split(p, [kt, 2 * kt, 2 * kt + vt, 2 * kt + 2 * vt, 2 * kt + 2 * vt + r], axis=-1)


def gla_mixer(h, w_in, w_a2, b_a, head_gain, w_out, s0_f, s0_b):
    q, k, v, g, a_f, a_b = gla_split_cols(h @ w_in)
    q = heads(q, GLA_HEAD_K) * (GLA_HEAD_K ** -0.5)
    k = heads(k, GLA_HEAD_K)
    v = heads(v, GLA_HEAD_V)
    la_f = gla_log_decay(a_f, w_a2[0], b_a[0])
    la_b = gla_log_decay(a_b, w_a2[1], b_a[1])
    o_f, s_f = gla_scan(q, k, v, la_f, s0_f)
    flip = functools.partial(jnp.flip, axis=1)
    o_b, s_b = gla_scan(flip(q), flip(k), flip(v), flip(la_b), s0_b)
    o = o_f + flip(o_b)
    o = o * lax.rsqrt(jnp.mean(o * o, axis=-1, keepdims=True) + EPS)
    o = (o.astype(h.dtype) * head_gain).reshape(h.shape[0], h.shape[1], GLA_VAL_DIM)
    return (o * jax.nn.silu(g)) @ w_out, s_f, s_b


def gla_context_states(h, w_in, w_a2, b_a):
    kt, vt = GLA_KEY_DIM, GLA_VAL_DIM
    k, v = jnp.split(h @ w_in[:, kt:2 * kt + vt], [kt], axis=-1)
    a_f, a_b = jnp.split(h @ w_in[:, 2 * kt + 2 * vt:], 2, axis=-1)
    k = heads(k, GLA_HEAD_K)
    v = heads(v, GLA_HEAD_V)
    s_f = gla_state(k, v, gla_log_decay(a_f, w_a2[0], b_a[0]))
    s_b = gla_state(jnp.flip(k, 1), jnp.flip(v, 1), jnp.flip(gla_log_decay(a_b, w_a2[1], b_a[1]), 1))
    return s_f, s_b


def short_conv_mixer(h, w_in, conv_w, w_out, conv_fn):
    bg, cg, v = jnp.split(h @ w_in, 3, axis=-1)
    return (bg * conv_fn(cg * v, conv_w)) @ w_out


def conv_ffn(h, w_up, conv_w, conv_b, w_down, conv_fn):
    u = conv_fn(h @ w_up, conv_w) + conv_b
    a, gt = jnp.split(u, 2, axis=-1)
    return (a * jax.nn.silu(gt)) @ w_down


def _fwd_setup_inputs(seed: int = 0) -> dict:
    key = jax.random.key(seed)
    ks = jax.random.split(key, 24)
    n_a = (DEPTH + N_MIXERS - 1) // N_MIXERS
    n_b = DEPTH // N_MIXERS
    f32 = jnp.float32

    def nrm(k, shape, scale):
        return jax.random.normal(k, shape, f32) * scale

    return {
        'x': nrm(ks[0], (BATCH, SEQ, D_MODEL), 1.0),
        'c': nrm(ks[1], (BATCH, D_MODEL), 1.0),
        'ctx': nrm(ks[2], (BATCH, CTX_LEN, D_MODEL), 1.0),
        'c_ctx': nrm(ks[3], (D_MODEL,), 1.0),
        'ada_w': nrm(ks[4], (DEPTH, D_MODEL, N_MOD * D_MODEL), 0.5 * D_MODEL ** -0.5),
        'ada_b': nrm(ks[5], (DEPTH, N_MOD * D_MODEL), 0.02),
        'norm_mix': 1.0 + nrm(ks[6], (DEPTH, D_MODEL), 0.02),
        'norm_ffn': 1.0 + nrm(ks[7], (DEPTH, D_MODEL), 0.02),
        'gla_w_in': nrm(ks[8], (n_a, D_MODEL, GLA_IN_DIM), D_MODEL ** -0.5),
        'gla_w_a2': nrm(ks[9], (n_a, 2, GLA_GATE_RANK, GLA_KEY_DIM), GLA_GATE_RANK ** -0.5),
        'gla_b_a': nrm(ks[10], (n_a, 2, GLA_KEY_DIM), 0.1),
        'gla_head_norm': 1.0 + nrm(ks[11], (n_a, GLA_HEAD_V), 0.02),
        'gla_w_out': nrm(ks[12], (n_a, GLA_VAL_DIM, D_MODEL), GLA_VAL_DIM ** -0.5),
        'sc_w_in': nrm(ks[13], (n_b, D_MODEL, 3 * SC_DIM), D_MODEL ** -0.5),
        'sc_conv_w': nrm(ks[14], (n_b, CONV_WIDTH, SC_DIM), CONV_WIDTH ** -0.5),
        'sc_w_out': nrm(ks[15], (n_b, SC_DIM, D_MODEL), SC_DIM ** -0.5),
        'ffn_w_up': nrm(ks[16], (DEPTH, D_MODEL, 2 * FFN_HIDDEN), D_MODEL ** -0.5),
        'ffn_conv_w': nrm(ks[17], (DEPTH, CONV_WIDTH, 2 * FFN_HIDDEN), CONV_WIDTH ** -0.5),
        'ffn_conv_b': nrm(ks[18], (DEPTH, 2 * FFN_HIDDEN), 0.02),
        'ffn_w_down': nrm(ks[19], (DEPTH, FFN_HIDDEN, D_MODEL), FFN_HIDDEN ** -0.5),
        'final_norm': 1.0 + nrm(ks[20], (D_MODEL,), 0.02),
    }


def _fwd_reference(x, c, ctx, c_ctx, ada_w, ada_b, norm_mix, norm_ffn, gla_w_in, gla_w_a2, gla_b_a,
              gla_head_norm, gla_w_out, sc_w_in, sc_conv_w, sc_w_out, ffn_w_up, ffn_conv_w,
              ffn_conv_b, ffn_w_down, final_norm):
    rows = x.shape[1] // GRID_W
    conv_lat_rows = functools.partial(conv_grid, rows=rows, axis=2)
    conv_lat_cols = functools.partial(conv_grid, rows=rows, axis=1)
    h, hc = x, ctx
    sc, scc = jax.nn.silu(c), jax.nn.silu(c_ctx)
    for i in range(DEPTH):
        mixer, j = i % N_MIXERS, i // N_MIXERS
        ctx_later = any(l % N_MIXERS == 0 for l in range(i + 1, DEPTH))
        m = [t[:, None, :] for t in jnp.split(sc @ ada_w[i] + ada_b[i], N_MOD, axis=-1)]
        need_ctx = (mixer == 0) or ctx_later
        if need_ctx:
            mc = jnp.split(scc @ ada_w[i] + ada_b[i], N_MOD, axis=-1)
            hnc = modulate(hc, norm_mix[i], mc[0], mc[1])
        hn = modulate(h, norm_mix[i], m[0], m[1])
        if mixer == 0:
            if ctx_later:
                zero = jnp.zeros((hc.shape[0], GLA_HEADS, GLA_HEAD_K, GLA_HEAD_V), jnp.float32)
                yc, s_f, s_b = gla_mixer(hnc, gla_w_in[j], gla_w_a2[j], gla_b_a[j], gla_head_norm[j],
                                         gla_w_out[j], zero, zero)
            else:
                s_f, s_b = gla_context_states(hnc, gla_w_in[j], gla_w_a2[j], gla_b_a[j])
            y, _, _ = gla_mixer(hn, gla_w_in[j], gla_w_a2[j], gla_b_a[j], gla_head_norm[j],
                                gla_w_out[j], s_f, s_b)
        else:
            y = short_conv_mixer(hn, sc_w_in[j], sc_conv_w[j], sc_w_out[j], conv_lat_rows)
            if ctx_later:
                yc = short_conv_mixer(hnc, sc_w_in[j], sc_conv_w[j], sc_w_out[j], conv_seq)
        h = h + m[2] * y
        h = h + m[5] * conv_ffn(modulate(h, norm_ffn[i], m[3], m[4]), ffn_w_up[i], ffn_conv_w[i],
                                ffn_conv_b[i], ffn_w_down[i], conv_lat_cols)
        if ctx_later:
            hc = hc + mc[2] * yc
            hc = hc + mc[5] * conv_ffn(modulate(hc, norm_ffn[i], mc[3], mc[4]), ffn_w_up[i],
                                       ffn_conv_w[i], ffn_conv_b[i], ffn_w_down[i], conv_seq)
    return rmsnorm(h, final_norm)


import jax as _jax
import jax.numpy as _jnp

TWIN_FORMAT = 'train_step'
FWD_PARAMS = ['x', 'c', 'ctx', 'c_ctx', 'ada_w', 'ada_b', 'norm_mix', 'norm_ffn', 'gla_w_in', 'gla_w_a2', 'gla_b_a', 'gla_head_norm', 'gla_w_out', 'sc_w_in', 'sc_conv_w', 'sc_w_out', 'ffn_w_up', 'ffn_conv_w', 'ffn_conv_b', 'ffn_w_down', 'final_norm']
TWIN_WEIGHTS = ['c_ctx', 'ada_w', 'ada_b', 'norm_mix', 'norm_ffn', 'gla_w_in', 'gla_w_a2', 'gla_b_a', 'gla_head_norm', 'gla_w_out', 'sc_w_in', 'sc_conv_w', 'sc_w_out', 'ffn_w_up', 'ffn_conv_w', 'ffn_conv_b', 'ffn_w_down', 'final_norm']
TWIN_DIFF_INPUT = 'x'
TWIN_INPUTS = ['x', 'c', 'ctx', 'c_ctx', 'ada_w', 'ada_b', 'norm_mix', 'norm_ffn', 'gla_w_in', 'gla_w_a2', 'gla_b_a', 'gla_head_norm', 'gla_w_out', 'sc_w_in', 'sc_conv_w', 'sc_w_out', 'ffn_w_up', 'ffn_conv_w', 'ffn_conv_b', 'ffn_w_down', 'final_norm', 'loss_target', 'm_c_ctx', 'm_ada_w', 'm_ada_b', 'm_norm_mix', 'm_norm_ffn', 'm_gla_w_in', 'm_gla_w_a2', 'm_gla_b_a', 'm_gla_head_norm', 'm_gla_w_out', 'm_sc_w_in', 'm_sc_conv_w', 'm_sc_w_out', 'm_ffn_w_up', 'm_ffn_conv_w', 'm_ffn_conv_b', 'm_ffn_w_down', 'm_final_norm', 'v_c_ctx', 'v_ada_w', 'v_ada_b', 'v_norm_mix', 'v_norm_ffn', 'v_gla_w_in', 'v_gla_w_a2', 'v_gla_b_a', 'v_gla_head_norm', 'v_gla_w_out', 'v_sc_w_in', 'v_sc_conv_w', 'v_sc_w_out', 'v_ffn_w_up', 'v_ffn_conv_w', 'v_ffn_conv_b', 'v_ffn_w_down', 'v_final_norm']
TWIN_OUTPUTS = ['loss', 'grad_x', 'grad_c_ctx', 'grad_ada_w', 'grad_ada_b', 'grad_norm_mix', 'grad_norm_ffn', 'grad_gla_w_in', 'grad_gla_w_a2', 'grad_gla_b_a', 'grad_gla_head_norm', 'grad_gla_w_out', 'grad_sc_w_in', 'grad_sc_conv_w', 'grad_sc_w_out', 'grad_ffn_w_up', 'grad_ffn_conv_w', 'grad_ffn_conv_b', 'grad_ffn_w_down', 'grad_final_norm', 'delta_c_ctx', 'delta_ada_w', 'delta_ada_b', 'delta_norm_mix', 'delta_norm_ffn', 'delta_gla_w_in', 'delta_gla_w_a2', 'delta_gla_b_a', 'delta_gla_head_norm', 'delta_gla_w_out', 'delta_sc_w_in', 'delta_sc_conv_w', 'delta_sc_w_out', 'delta_ffn_w_up', 'delta_ffn_conv_w', 'delta_ffn_conv_b', 'delta_ffn_w_down', 'delta_final_norm', 'new_m_c_ctx', 'new_m_ada_w', 'new_m_ada_b', 'new_m_norm_mix', 'new_m_norm_ffn', 'new_m_gla_w_in', 'new_m_gla_w_a2', 'new_m_gla_b_a', 'new_m_gla_head_norm', 'new_m_gla_w_out', 'new_m_sc_w_in', 'new_m_sc_conv_w', 'new_m_sc_w_out', 'new_m_ffn_w_up', 'new_m_ffn_conv_w', 'new_m_ffn_conv_b', 'new_m_ffn_w_down', 'new_m_final_norm', 'new_v_c_ctx', 'new_v_ada_w', 'new_v_ada_b', 'new_v_norm_mix', 'new_v_norm_ffn', 'new_v_gla_w_in', 'new_v_gla_w_a2', 'new_v_gla_b_a', 'new_v_gla_head_norm', 'new_v_gla_w_out', 'new_v_sc_w_in', 'new_v_sc_conv_w', 'new_v_sc_w_out', 'new_v_ffn_w_up', 'new_v_ffn_conv_w', 'new_v_ffn_conv_b', 'new_v_ffn_w_down', 'new_v_final_norm']
TWIN_LEAF_KINDS = {'loss': 'loss', 'grad_x': 'grad_x', 'grad_c_ctx': 'grad_w', 'grad_ada_w': 'grad_w', 'grad_ada_b': 'grad_w', 'grad_norm_mix': 'grad_w', 'grad_norm_ffn': 'grad_w', 'grad_gla_w_in': 'grad_w', 'grad_gla_w_a2': 'grad_w', 'grad_gla_b_a': 'grad_w', 'grad_gla_head_norm': 'grad_w', 'grad_gla_w_out': 'grad_w', 'grad_sc_w_in': 'grad_w', 'grad_sc_conv_w': 'grad_w', 'grad_sc_w_out': 'grad_w', 'grad_ffn_w_up': 'grad_w', 'grad_ffn_conv_w': 'grad_w', 'grad_ffn_conv_b': 'grad_w', 'grad_ffn_w_down': 'grad_w', 'grad_final_norm': 'grad_w', 'delta_c_ctx': 'delta_w', 'delta_ada_w': 'delta_w', 'delta_ada_b': 'delta_w', 'delta_norm_mix': 'delta_w', 'delta_norm_ffn': 'delta_w', 'delta_gla_w_in': 'delta_w', 'delta_gla_w_a2': 'delta_w', 'delta_gla_b_a': 'delta_w', 'delta_gla_head_norm': 'delta_w', 'delta_gla_w_out': 'delta_w', 'delta_sc_w_in': 'delta_w', 'delta_sc_conv_w': 'delta_w', 'delta_sc_w_out': 'delta_w', 'delta_ffn_w_up': 'delta_w', 'delta_ffn_conv_w': 'delta_w', 'delta_ffn_conv_b': 'delta_w', 'delta_ffn_w_down': 'delta_w', 'delta_final_norm': 'delta_w', 'new_m_c_ctx': 'new_m', 'new_m_ada_w': 'new_m', 'new_m_ada_b': 'new_m', 'new_m_norm_mix': 'new_m', 'new_m_norm_ffn': 'new_m', 'new_m_gla_w_in': 'new_m', 'new_m_gla_w_a2': 'new_m', 'new_m_gla_b_a': 'new_m', 'new_m_gla_head_norm': 'new_m', 'new_m_gla_w_out': 'new_m', 'new_m_sc_w_in': 'new_m', 'new_m_sc_conv_w': 'new_m', 'new_m_sc_w_out': 'new_m', 'new_m_ffn_w_up': 'new_m', 'new_m_ffn_conv_w': 'new_m', 'new_m_ffn_conv_b': 'new_m', 'new_m_ffn_w_down': 'new_m', 'new_m_final_norm': 'new_m', 'new_v_c_ctx': 'new_v', 'new_v_ada_w': 'new_v', 'new_v_ada_b': 'new_v', 'new_v_norm_mix': 'new_v', 'new_v_norm_ffn': 'new_v', 'new_v_gla_w_in': 'new_v', 'new_v_gla_w_a2': 'new_v', 'new_v_gla_b_a': 'new_v', 'new_v_gla_head_norm': 'new_v', 'new_v_gla_w_out': 'new_v', 'new_v_sc_w_in': 'new_v', 'new_v_sc_conv_w': 'new_v', 'new_v_sc_w_out': 'new_v', 'new_v_ffn_w_up': 'new_v', 'new_v_ffn_conv_w': 'new_v', 'new_v_ffn_conv_b': 'new_v', 'new_v_ffn_w_down': 'new_v', 'new_v_final_norm': 'new_v'}


def _forward(args):
    return _fwd_reference(*[args[k] for k in FWD_PARAMS])


def _output_shape():
    out = _jax.eval_shape(lambda: _forward(_fwd_setup_inputs(0)))
    return out.shape, out.dtype

N_MICROBATCH = 1
ADAM_LR = 0.001
ADAM_B1 = 0.9
ADAM_B2 = 0.999
ADAM_EPS = 1e-08
ADAM_WD = 0.01
ADAM_STEP = 10
PER_EXAMPLE_BATCH_AXIS = {'x': 0, 'c': 0, 'ctx': 0, 'loss_target': 0}
SHARED_INPUTS = []
_WEIGHT_DTYPES = {'c_ctx': _jnp.float32, 'ada_w': _jnp.float32, 'ada_b': _jnp.float32, 'norm_mix': _jnp.float32, 'norm_ffn': _jnp.float32, 'gla_w_in': _jnp.float32, 'gla_w_a2': _jnp.float32, 'gla_b_a': _jnp.float32, 'gla_head_norm': _jnp.float32, 'gla_w_out': _jnp.float32, 'sc_w_in': _jnp.float32, 'sc_conv_w': _jnp.float32, 'sc_w_out': _jnp.float32, 'ffn_w_up': _jnp.float32, 'ffn_conv_w': _jnp.float32, 'ffn_conv_b': _jnp.float32, 'ffn_w_down': _jnp.float32, 'final_norm': _jnp.float32}
MOMENT_SCALE = {'c_ctx': 6.100946e-03, 'ada_w': 7.600182e-02, 'ada_b': 1.251452e-01, 'norm_mix': 9.785245e-02, 'norm_ffn': 6.452856e-02, 'gla_w_in': 4.955129e-02, 'gla_w_a2': 7.263015e-03, 'gla_b_a': 1.838260e-02, 'gla_head_norm': 1.008690e-01, 'gla_w_out': 4.108159e-02, 'sc_w_in': 6.451342e-02, 'sc_conv_w': 6.522281e-02, 'sc_w_out': 6.411145e-02, 'ffn_w_up': 2.996889e-02, 'ffn_conv_w': 2.978145e-02, 'ffn_conv_b': 2.492300e-02, 'ffn_w_down': 4.686560e-02, 'final_norm': 3.219384e+01}


def _to_microbatches(a, axis):
    t = _jnp.moveaxis(a, axis, 0)
    t = t.reshape((N_MICROBATCH, t.shape[0] // N_MICROBATCH) + t.shape[1:])
    return _jnp.moveaxis(t, 1, axis + 1)


def setup_inputs(seed: int = 0) -> dict:
    inp = _fwd_setup_inputs(seed)
    key = _jax.random.fold_in(_jax.random.key(seed), 7919)
    shape, _ = _output_shape()
    out = dict(inp)
    out["loss_target"] = _jax.random.normal(_jax.random.fold_in(key, 0), shape, _jnp.float32)
    for i, name in enumerate(TWIN_WEIGHTS):
        w = inp[name].astype(_jnp.float32)
        if MOMENT_SCALE is None:
            s = _jnp.sqrt(_jnp.mean(_jnp.square(w)) + 1e-30)
        else:
            s = MOMENT_SCALE[name]
        km, kv = _jax.random.split(_jax.random.fold_in(key, i + 1))
        out[name] = w
        out["m_" + name] = s * _jax.random.normal(km, w.shape, _jnp.float32)
        out["v_" + name] = (s * s) * _jax.random.uniform(kv, w.shape, _jnp.float32, 0.5, 1.5)
    if N_MICROBATCH > 1:
        for name, axis in PER_EXAMPLE_BATCH_AXIS.items():
            out[name] = _to_microbatches(out[name], axis)
    return {'x': out['x'], 'c': out['c'], 'ctx': out['ctx'], 'c_ctx': out['c_ctx'], 'ada_w': out['ada_w'], 'ada_b': out['ada_b'], 'norm_mix': out['norm_mix'], 'norm_ffn': out['norm_ffn'], 'gla_w_in': out['gla_w_in'], 'gla_w_a2': out['gla_w_a2'], 'gla_b_a': out['gla_b_a'], 'gla_head_norm': out['gla_head_norm'], 'gla_w_out': out['gla_w_out'], 'sc_w_in': out['sc_w_in'], 'sc_conv_w': out['sc_conv_w'], 'sc_w_out': out['sc_w_out'], 'ffn_w_up': out['ffn_w_up'], 'ffn_conv_w': out['ffn_conv_w'], 'ffn_conv_b': out['ffn_conv_b'], 'ffn_w_down': out['ffn_w_down'], 'final_norm': out['final_norm'], 'loss_target': out['loss_target'], 'm_c_ctx': out['m_c_ctx'], 'm_ada_w': out['m_ada_w'], 'm_ada_b': out['m_ada_b'], 'm_norm_mix': out['m_norm_mix'], 'm_norm_ffn': out['m_norm_ffn'], 'm_gla_w_in': out['m_gla_w_in'], 'm_gla_w_a2': out['m_gla_w_a2'], 'm_gla_b_a': out['m_gla_b_a'], 'm_gla_head_norm': out['m_gla_head_norm'], 'm_gla_w_out': out['m_gla_w_out'], 'm_sc_w_in': out['m_sc_w_in'], 'm_sc_conv_w': out['m_sc_conv_w'], 'm_sc_w_out': out['m_sc_w_out'], 'm_ffn_w_up': out['m_ffn_w_up'], 'm_ffn_conv_w': out['m_ffn_conv_w'], 'm_ffn_conv_b': out['m_ffn_conv_b'], 'm_ffn_w_down': out['m_ffn_w_down'], 'm_final_norm': out['m_final_norm'], 'v_c_ctx': out['v_c_ctx'], 'v_ada_w': out['v_ada_w'], 'v_ada_b': out['v_ada_b'], 'v_norm_mix': out['v_norm_mix'], 'v_norm_ffn': out['v_norm_ffn'], 'v_gla_w_in': out['v_gla_w_in'], 'v_gla_w_a2': out['v_gla_w_a2'], 'v_gla_b_a': out['v_gla_b_a'], 'v_gla_head_norm': out['v_gla_head_norm'], 'v_gla_w_out': out['v_gla_w_out'], 'v_sc_w_in': out['v_sc_w_in'], 'v_sc_conv_w': out['v_sc_conv_w'], 'v_sc_w_out': out['v_sc_w_out'], 'v_ffn_w_up': out['v_ffn_w_up'], 'v_ffn_conv_w': out['v_ffn_conv_w'], 'v_ffn_conv_b': out['v_ffn_conv_b'], 'v_ffn_w_down': out['v_ffn_w_down'], 'v_final_norm': out['v_final_norm']}


def _loss(weights, diff, rest, loss_target):
    with _jax.named_scope("forward"):
        args = {**rest, TWIN_DIFF_INPUT: diff, **{k: w.astype(_WEIGHT_DTYPES[k]) for k, w in weights.items()}}
        y = _forward(args)
    with _jax.named_scope("loss_head"):
        err = _jnp.square(y.astype(_jnp.float32) - loss_target)
        return 0.5 * _jnp.sum(_jnp.mean(err, axis=-1)) if err.ndim else 0.5 * err


def _adamw(w, g, m, v):
    m = ADAM_B1 * m + (1.0 - ADAM_B1) * g
    v = ADAM_B2 * v + (1.0 - ADAM_B2) * _jnp.square(g)
    m_hat = m / (1.0 - ADAM_B1 ** ADAM_STEP)
    v_hat = v / (1.0 - ADAM_B2 ** ADAM_STEP)
    delta = -ADAM_LR * (m_hat / (_jnp.sqrt(v_hat) + ADAM_EPS) + ADAM_WD * w)
    return delta, m, v


def reference(x, c, ctx, c_ctx, ada_w, ada_b, norm_mix, norm_ffn, gla_w_in, gla_w_a2, gla_b_a, gla_head_norm, gla_w_out, sc_w_in, sc_conv_w, sc_w_out, ffn_w_up, ffn_conv_w, ffn_conv_b, ffn_w_down, final_norm, loss_target, m_c_ctx, m_ada_w, m_ada_b, m_norm_mix, m_norm_ffn, m_gla_w_in, m_gla_w_a2, m_gla_b_a, m_gla_head_norm, m_gla_w_out, m_sc_w_in, m_sc_conv_w, m_sc_w_out, m_ffn_w_up, m_ffn_conv_w, m_ffn_conv_b, m_ffn_w_down, m_final_norm, v_c_ctx, v_ada_w, v_ada_b, v_norm_mix, v_norm_ffn, v_gla_w_in, v_gla_w_a2, v_gla_b_a, v_gla_head_norm, v_gla_w_out, v_sc_w_in, v_sc_conv_w, v_sc_w_out, v_ffn_w_up, v_ffn_conv_w, v_ffn_conv_b, v_ffn_w_down, v_final_norm):
    given = dict(x=x, c=c, ctx=ctx, c_ctx=c_ctx, ada_w=ada_w, ada_b=ada_b, norm_mix=norm_mix, norm_ffn=norm_ffn, gla_w_in=gla_w_in, gla_w_a2=gla_w_a2, gla_b_a=gla_b_a, gla_head_norm=gla_head_norm, gla_w_out=gla_w_out, sc_w_in=sc_w_in, sc_conv_w=sc_conv_w, sc_w_out=sc_w_out, ffn_w_up=ffn_w_up, ffn_conv_w=ffn_conv_w, ffn_conv_b=ffn_conv_b, ffn_w_down=ffn_w_down, final_norm=final_norm, loss_target=loss_target, m_c_ctx=m_c_ctx, m_ada_w=m_ada_w, m_ada_b=m_ada_b, m_norm_mix=m_norm_mix, m_norm_ffn=m_norm_ffn, m_gla_w_in=m_gla_w_in, m_gla_w_a2=m_gla_w_a2, m_gla_b_a=m_gla_b_a, m_gla_head_norm=m_gla_head_norm, m_gla_w_out=m_gla_w_out, m_sc_w_in=m_sc_w_in, m_sc_conv_w=m_sc_conv_w, m_sc_w_out=m_sc_w_out, m_ffn_w_up=m_ffn_w_up, m_ffn_conv_w=m_ffn_conv_w, m_ffn_conv_b=m_ffn_conv_b, m_ffn_w_down=m_ffn_w_down, m_final_norm=m_final_norm, v_c_ctx=v_c_ctx, v_ada_w=v_ada_w, v_ada_b=v_ada_b, v_norm_mix=v_norm_mix, v_norm_ffn=v_norm_ffn, v_gla_w_in=v_gla_w_in, v_gla_w_a2=v_gla_w_a2, v_gla_b_a=v_gla_b_a, v_gla_head_norm=v_gla_head_norm, v_gla_w_out=v_gla_w_out, v_sc_w_in=v_sc_w_in, v_sc_conv_w=v_sc_conv_w, v_sc_w_out=v_sc_w_out, v_ffn_w_up=v_ffn_w_up, v_ffn_conv_w=v_ffn_conv_w, v_ffn_conv_b=v_ffn_conv_b, v_ffn_w_down=v_ffn_w_down, v_final_norm=v_final_norm)
    weights = {n: given[n] for n in TWIN_WEIGHTS}
    shared = {n: given[n] for n in SHARED_INPUTS}
    per_example = {n: given[n] for n in ['x', 'c', 'ctx']}
    grad_fn = _jax.value_and_grad(_loss, argnums=(0, 1))

    def one_microbatch(ex, loss_target):
        ex = dict(ex)
        diff = ex.pop(TWIN_DIFF_INPUT)
        return grad_fn(weights, diff, {**shared, **ex}, loss_target)

    if N_MICROBATCH == 1:
        loss, (grad_w, grad_x) = one_microbatch(per_example, given["loss_target"])
    else:
        def body(carry, xs):
            loss_sum, grad_sum = carry
            l_k, (gw_k, gx_k) = one_microbatch(xs[0], xs[1])
            with _jax.named_scope("update"):
                return (loss_sum + l_k, _jax.tree.map(_jnp.add, grad_sum, gw_k)), gx_k

        init = (_jnp.zeros((), _jnp.float32), _jax.tree.map(_jnp.zeros_like, weights))
        (loss, grad_w), grad_x = _jax.lax.scan(body, init, (per_example, given["loss_target"]))
    with _jax.named_scope("update"):
        delta_w, new_m, new_v = {}, {}, {}
        for n in TWIN_WEIGHTS:
            delta_w[n], new_m[n], new_v[n] = _adamw(weights[n], grad_w[n], given["m_" + n], given["v_" + n])
    return (loss, grad_x, *[grad_w[n] for n in TWIN_WEIGHTS], *[delta_w[n] for n in TWIN_WEIGHTS],
            *[new_m[n] for n in TWIN_WEIGHTS], *[new_v[n] for n in TWIN_WEIGHTS])
```

```python
import functools

import jax
import jax.numpy as jnp
from jax import lax
from jax.experimental import pallas as pl
from jax.experimental.pallas import tpu as pltpu

F32 = jnp.float32
BF16 = jnp.bfloat16

N_DEV = 8
D = 1024
SEQ = 2048
CTX = 256
TT = CTX + SEQ
GRID_W = 64
CHUNK = 64
HEADS = 4
HK = 128
HV = 256
KD = 512
VD = 1024
RANK = 16
TAU = 16.0
GLA_IN = 3104
GLA_IN_PAD = 3200
FFN_H = 2560
FFN_TC = 256
EPS = 1e-6
LR, B1, B2, AEPS, WD, STEP = 0.001, 0.9, 0.999, 1e-08, 0.01, 10
MESH = pl.DeviceIdType.MESH


def _pick(dim, cap=1024):
    for t in range(cap, 0, -128):
        if dim % t == 0:
            return t
    return dim


def mm(name, a, b, *, ta=False, tb=False, out_dtype=F32):
    K, M = a.shape if ta else a.shape[::-1]
    N = b.shape[0] if tb else b.shape[1]
    assert (b.shape[1] if tb else b.shape[0]) == K, (name, a.shape, b.shape)
    tm, tn, tk = _pick(M), _pick(N), _pick(K)
    nk = K // tk
    dn = (((0 if ta else 1,), (1 if tb else 0,)), ((), ()))

    def body(a_ref, b_ref, o_ref, acc_ref):
        k = pl.program_id(2)
        part = lax.dot_general(a_ref[...].astype(BF16), b_ref[...].astype(BF16), dn, preferred_element_type=F32)

        @pl.when(k == 0)
        def _():
            acc_ref[...] = part

        @pl.when(k > 0)
        def _():
            acc_ref[...] += part

        @pl.when(k == nk - 1)
        def _():
            o_ref[...] = acc_ref[...].astype(out_dtype)

    a_spec = pl.BlockSpec((tk, tm), lambda i, j, k: (k, i)) if ta else pl.BlockSpec((tm, tk), lambda i, j, k: (i, k))
    b_spec = pl.BlockSpec((tn, tk), lambda i, j, k: (j, k)) if tb else pl.BlockSpec((tk, tn), lambda i, j, k: (k, j))
    return pl.pallas_call(
        body, name=name, grid=(M // tm, N // tn, nk),
        in_specs=[a_spec, b_spec], out_specs=pl.BlockSpec((tm, tn), lambda i, j, k: (i, j)),
        out_shape=jax.ShapeDtypeStruct((M, N), out_dtype),
        scratch_shapes=[pltpu.VMEM((tm, tn), F32)],
        compiler_params=pltpu.CompilerParams(dimension_semantics=("parallel", "parallel", "arbitrary")),
    )(a, b)


def mm3(name, a, b, **kw):
    bsz, t, k = a.shape
    return mm(name, a.reshape(bsz * t, k), b, **kw).reshape(bsz, t, -1)


def mm_tn(name, a, b):
    return mm(name, a.reshape(-1, a.shape[-1]), b.reshape(-1, b.shape[-1]), ta=True)


def X(arr, w=None, co=0, ro=0, split=1):
    return dict(a=arr, w=arr.shape[-1] if w is None else w, co=co, ro=ro, split=split)


def P(arr, per_example=False, w=None, split=1):
    return dict(a=arr, e=per_example, w=arr.shape[-1] if w is None else w, split=split)


def _pieces(ref, split):
    w = ref.shape[-1] // split
    return [ref[:, i * w:(i + 1) * w] for i in range(split)]


def _store(ref, pieces, accumulate=False):
    w = ref.shape[-1] // len(pieces)
    for i, p in enumerate(pieces):
        if accumulate:
            ref[:, i * w:(i + 1) * w] += p.astype(ref.dtype)
        else:
            ref[:, i * w:(i + 1) * w] = p.astype(ref.dtype)


def rowwise(name, f, xs, ps, *, tm, nt, nc=1, outs=None, douts=None, dx=None, dp=None):
    bsz = xs[0]["a"].shape[0]
    fwd = douts is None
    nx, np_ = len(xs), len(ps)
    douts = [] if fwd else douts
    dx = {} if fwd else dx
    dp = [] if fwd else dp

    def x_spec(s):
        return pl.BlockSpec((None, tm, s["w"]), lambda c, b, t, s=s: (b, t + s["ro"], c + s["co"]))

    def p_spec(s):
        r = s["a"].shape[-2]
        if s["e"]:
            return pl.BlockSpec((None, r, s["w"]), lambda c, b, t: (b, 0, c))
        return pl.BlockSpec((r, s["w"]), lambda c, b, t: (0, c))

    in_specs = [x_spec(s) for s in xs] + [p_spec(s) for s in ps] + [x_spec(s) for s in douts]
    operands = [s["a"] for s in xs] + [s["a"] for s in ps] + [s["a"] for s in douts]
    if fwd:
        out_shape = [jax.ShapeDtypeStruct((bsz, nt * tm, nc * w), dt) for (w, dt, _) in outs]
        out_specs = [pl.BlockSpec((None, tm, w), lambda c, b, t: (b, t, c)) for (w, _, _) in outs]
    else:
        out_shape = [jax.ShapeDtypeStruct((bsz, nt * tm, nc * xs[i]["w"]), dt) for i, dt in dx.items()]
        out_specs = [pl.BlockSpec((None, tm, xs[i]["w"]), lambda c, b, t: (b, t, c)) for i in dx]
        for j in dp:
            s = ps[j]
            r = s["a"].shape[-2]
            if s["e"]:
                out_shape.append(jax.ShapeDtypeStruct((bsz, r, nc * s["w"]), F32))
                out_specs.append(pl.BlockSpec((None, r, s["w"]), lambda c, b, t: (b, 0, c)))
            else:
                out_shape.append(jax.ShapeDtypeStruct((r, nc * s["w"]), F32))
                out_specs.append(pl.BlockSpec((r, s["w"]), lambda c, b, t: (0, c)))

    def body(*refs):
        x_refs, p_refs = refs[:nx], refs[nx:nx + np_]
        d_refs = refs[nx + np_:nx + np_ + len(douts)]
        o_refs = refs[nx + np_ + len(douts):]
        xv = [[p.astype(F32) for p in _pieces(r, s["split"])] for r, s in zip(x_refs, xs)]
        pv = [[p.astype(F32) for p in _pieces(r, s["split"])] for r, s in zip(p_refs, ps)]
        if fwd:
            for r, pieces in zip(o_refs, f(xv, pv)):
                _store(r, pieces)
            return
        _, vjp = jax.vjp(f, xv, pv)
        cot = [[p.astype(F32) for p in _pieces(r, s["split"])] for r, s in zip(d_refs, douts)]
        dxv, dpv = vjp(cot)
        for r, i in zip(o_refs, dx):
            _store(r, dxv[i])
        b, t = pl.program_id(1), pl.program_id(2)
        for r, j in zip(o_refs[len(dx):], dp):
            first = (t == 0) if ps[j]["e"] else jnp.logical_and(b == 0, t == 0)

            @pl.when(first)
            def _(r=r, j=j):
                _store(r, dpv[j])

            @pl.when(jnp.logical_not(first))
            def _(r=r, j=j):
                _store(r, dpv[j], accumulate=True)

    res = pl.pallas_call(
        body, name=name, grid=(nc, bsz, nt), in_specs=in_specs, out_specs=out_specs, out_shape=out_shape,
        compiler_params=pltpu.CompilerParams(dimension_semantics=("arbitrary", "arbitrary", "arbitrary")),
    )(*operands)
    return res


def _keep_rows(a, shift, keep):
    n = a.shape[0]
    t = lax.broadcasted_iota(jnp.int32, a.shape, 0)
    return jnp.where(keep(t, n), pltpu.roll(a, shift % n, 0), 0.0)


def _shift_pair(step, keep_prev, keep_next):
    @jax.custom_vjp
    def prev(a):
        return _keep_rows(a, step, keep_prev)

    @jax.custom_vjp
    def nxt(a):
        return _keep_rows(a, -step, keep_next)

    prev.defvjp(lambda a: (prev(a), None), lambda _, g: (nxt(g),))
    nxt.defvjp(lambda a: (nxt(a), None), lambda _, g: (prev(g),))
    return prev, nxt


prev_tok, next_tok = _shift_pair(1, lambda t, n: t % GRID_W != 0, lambda t, n: t % GRID_W != GRID_W - 1)
prev_row, next_row = _shift_pair(GRID_W, lambda t, n: t >= GRID_W, lambda t, n: t < n - GRID_W)


@jax.custom_vjp
def bdot(a, w):
    return jnp.dot(a.astype(BF16), w.astype(BF16), preferred_element_type=F32)


def _bdot_bwd(res, g):
    a, w = res
    gb = g.astype(BF16)
    da = lax.dot_general(gb, w.astype(BF16), (((1,), (1,)), ((), ())), preferred_element_type=F32)
    dw = lax.dot_general(a.astype(BF16), gb, (((0,), (0,)), ((), ())), preferred_element_type=F32)
    return da, dw


bdot.defvjp(lambda a, w: (bdot(a, w), (a, w)), _bdot_bwd)


@jax.custom_vjp
def log_sigmoid(z):
    return jnp.minimum(z, 0.0) - jnp.log(1.0 + jnp.exp(-jnp.abs(z)))


def _lsig_bwd(z, g):
    e = jnp.exp(-jnp.abs(z))
    return (g * jnp.where(z >= 0, e, 1.0) / (1.0 + e),)


log_sigmoid.defvjp(lambda z: (log_sigmoid(z), z), _lsig_bwd)


def silu(x):
    return x * jax.nn.sigmoid(x)


def _rms(x):
    return x * lax.rsqrt(jnp.mean(x * x, axis=-1, keepdims=True) + EPS)


def _mod(x, gain, shift, scale):
    return _rms(x) * gain * (1.0 + scale) + shift


def f_mod(xs, ps):
    ((h,),), ((gain,), (shift,), (scale,)) = xs, ps
    return [[_mod(h, gain, shift, scale)], [h]]


def f_res_mod(xs, ps):
    ((h,), (y,)), ((gate,), (gain,), (shift,), (scale,)) = xs, ps
    h1 = h + gate * y
    return [[h1], [_mod(h1, gain, shift, scale)]]


def f_ffn_mid(xs, ps):
    ((ua, ug),), ((w0a, w0g), (w1a, w1g), (w2a, w2g), (ba, bg)) = xs, ps
    a = w0a * prev_row(ua) + w1a * ua + w2a * next_row(ua) + ba
    g = w0g * prev_row(ug) + w1g * ug + w2g * next_row(ug) + bg
    return [[a * silu(g)]]


def f_sc_mid(xs, ps):
    ((bg, cg, v),), ((w0,), (w1,), (w2,)) = xs, ps
    z = cg * v
    return [[bg * (w0 * prev_tok(z) + w1 * z + w2 * next_tok(z))]]


def f_decay(xs, ps):
    ((a,),), ((wd,), (bd,)) = xs, ps
    return [[log_sigmoid(bdot(a, wd) + bd) / TAU]]


def f_gla_post(xs, ps):
    (of, ob, g), ((gain,),) = xs, ps
    return [[_rms(a + b) * gain * silu(c) for a, b, c in zip(of, ob, g)]]


def f_sum_pairs(xs, ps):
    return [[a + b for a, b in zip(xs[0], xs[1])]]


NCH = TT // CHUNK
CTX_CH = CTX // CHUNK
_NT = (((1,), (1,)), ((), ()))
_TN = (((0,), (0,)), ((), ()))
_NN = (((1,), (0,)), ((), ()))


def _chunk_of(d, j):
    return jnp.where(d == 0, j, jnp.where(j < CTX_CH, CTX_CH - 1 - j, NCH + CTX_CH - 1 - j))


def _dot(a, b, dn):
    return lax.dot_general(a, b, dn, preferred_element_type=F32)


def _mask_dot(m, g):
    g0 = g.astype(BF16)
    r1 = g - g0.astype(F32)
    g1 = r1.astype(BF16)
    g2 = (r1 - g1.astype(F32)).astype(BF16)
    return _dot(m, g0, _NN) + _dot(m, g1, _NN) + _dot(m, g2, _NN)


def _causal(d):
    row = lax.broadcasted_iota(jnp.int32, (CHUNK, CHUNK), 0)
    col = lax.broadcasted_iota(jnp.int32, (CHUNK, CHUNK), 1)
    delta = jnp.where(d == 0, col - row, row - col)
    return delta <= 0, delta >= 0


def _gla_in_specs(rev):
    def blk(d, j):
        return _chunk_of(d, (NCH - 1 - j) if rev else j)

    return [
        pl.BlockSpec((None, CHUNK, KD), lambda b, d, j: (b, blk(d, j), 0)),
        pl.BlockSpec((None, CHUNK, KD), lambda b, d, j: (b, blk(d, j), 1)),
        pl.BlockSpec((None, CHUNK, VD), lambda b, d, j: (b, blk(d, j), 1)),
        pl.BlockSpec((None, CHUNK, KD), lambda b, d, j: (b, blk(d, j), d)),
    ], blk


def gla_fwd(pcat, la):
    bsz = pcat.shape[0]
    in_specs, blk = _gla_in_specs(False)

    def body(q_ref, k_ref, v_ref, la_ref, o_ref, s_ref, st):
        d, j = pl.program_id(1), pl.program_id(2)

        @pl.when(j == 0)
        def _():
            st[...] = jnp.zeros_like(st)

        s_ref[...] = st[...]
        causal, _ = _causal(d)
        mf = causal.astype(BF16)
        for h in range(HEADS):
            ks_, vs_ = slice(h * HK, (h + 1) * HK), slice(h * HV, (h + 1) * HV)
            q, k, v, g = q_ref[:, ks_] * (HK ** -0.5), k_ref[:, ks_], v_ref[:, vs_].astype(BF16), la_ref[:, ks_]
            b = _mask_dot(mf, g)
            bl = jnp.sum(g, axis=0, keepdims=True)
            qs = (q * jnp.exp(b)).astype(BF16)
            ks = (k * jnp.exp(-b)).astype(BF16)
            kd = (k * jnp.exp(bl - b)).astype(BF16)
            s = st[h]
            att = jnp.where(causal, _dot(qs, ks, _NT), 0.0).astype(BF16)
            o_ref[:, vs_] = _dot(qs, s.astype(BF16), _NT) + _dot(att, v, _NN)
            st[h] = jnp.exp(bl) * s + _dot(v, kd, _TN)

    return pl.pallas_call(
        body, name="gla_fwd", grid=(bsz, 2, NCH), in_specs=in_specs,
        out_specs=[pl.BlockSpec((None, CHUNK, VD), lambda b, d, j: (b, blk(d, j), d)),
                   pl.BlockSpec((None, None, None, HEADS, HV, HK), lambda b, d, j: (b, d, j, 0, 0, 0))],
        out_shape=[jax.ShapeDtypeStruct((bsz, TT, 2 * VD), F32), jax.ShapeDtypeStruct((bsz, 2, NCH, HEADS, HV, HK), F32)],
        scratch_shapes=[pltpu.VMEM((HEADS, HV, HK), F32)],
        compiler_params=pltpu.CompilerParams(dimension_semantics=("arbitrary", "arbitrary", "arbitrary")),
    )(pcat, pcat, pcat, la)


def gla_bwd(pcat, la, s_all, do):
    bsz = pcat.shape[0]
    in_specs, blk = _gla_in_specs(True)
    in_specs += [
        pl.BlockSpec((None, None, None, HEADS, HV, HK), lambda b, d, j: (b, d, NCH - 1 - j, 0, 0, 0)),
        pl.BlockSpec((None, CHUNK, VD), lambda b, d, j: (b, jnp.maximum(blk(d, j) - CTX_CH, 0), 0)),
    ]

    def body(q_ref, k_ref, v_ref, la_ref, s_ref, do_ref, dq_ref, dk_ref, dv_ref, dla_ref, dst):
        d, j = pl.program_id(1), pl.program_id(2)

        @pl.when(j == 0)
        def _():
            dst[...] = jnp.zeros_like(dst)

        latent = blk(d, j) >= CTX_CH
        causal, causal_t = _causal(d)
        mt = causal_t.astype(BF16)
        mf = causal.astype(BF16)
        scale = HK ** -0.5
        for h in range(HEADS):
            ks_, vs_ = slice(h * HK, (h + 1) * HK), slice(h * HV, (h + 1) * HV)
            q, k, v, g = q_ref[:, ks_] * scale, k_ref[:, ks_], v_ref[:, vs_].astype(BF16), la_ref[:, ks_]
            b = _mask_dot(mf, g)
            bl = jnp.sum(g, axis=0, keepdims=True)
            e, ei, ed, el = jnp.exp(b), jnp.exp(-b), jnp.exp(bl - b), jnp.exp(bl)
            qs, ks, kd = q * e, k * ei, k * ed
            qsb, ksb, kdb = qs.astype(BF16), ks.astype(BF16), kd.astype(BF16)
            s, ds1 = s_ref[h], dst[h]
            sb, ds1b = s.astype(BF16), ds1.astype(BF16)
            dob = jnp.where(latent, do_ref[:, vs_], 0.0).astype(BF16)
            att = jnp.where(causal, _dot(qsb, ksb, _NT), 0.0).astype(BF16)
            datt = jnp.where(causal, _dot(dob, v, _NT), 0.0).astype(BF16)
            dqs = _dot(dob, sb, _NN) + _dot(datt, ksb, _NN)
            dks = _dot(datt, qsb, _TN)
            dv_ref[:, vs_] = _dot(att, dob, _TN) + _dot(kdb, ds1b, _NT)
            dkd = _dot(v, ds1b, _NN)
            dst[h] = _dot(dob, qsb, _TN) + el * ds1
            del_ = jnp.sum(s * ds1, axis=0, keepdims=True)
            dq_ref[:, ks_] = dqs * e * scale
            dk_ref[:, ks_] = dks * ei + dkd * ed
            db = dqs * qs - dks * ks - dkd * kd
            dbl = jnp.sum(dkd * kd, axis=0, keepdims=True) + del_ * el
            dla_ref[:, ks_] = _mask_dot(mt, db) + dbl

    return pl.pallas_call(
        body, name="gla_bwd", grid=(bsz, 2, NCH), in_specs=in_specs,
        out_specs=[pl.BlockSpec((None, None, CHUNK, KD), lambda b, d, j: (d, b, blk(d, j), 0)),
                   pl.BlockSpec((None, None, CHUNK, KD), lambda b, d, j: (d, b, blk(d, j), 0)),
                   pl.BlockSpec((None, None, CHUNK, VD), lambda b, d, j: (d, b, blk(d, j), 0)),
                   pl.BlockSpec((None, CHUNK, KD), lambda b, d, j: (b, blk(d, j), d))],
        out_shape=[jax.ShapeDtypeStruct((2, bsz, TT, KD), F32), jax.ShapeDtypeStruct((2, bsz, TT, KD), F32),
                   jax.ShapeDtypeStruct((2, bsz, TT, VD), F32), jax.ShapeDtypeStruct((bsz, TT, 2 * KD), F32)],
        scratch_shapes=[pltpu.VMEM((HEADS, HV, HK), F32)],
        compiler_params=pltpu.CompilerParams(dimension_semantics=("arbitrary", "arbitrary", "arbitrary")),
    )(pcat, pcat, pcat, la, s_all, do)


def gla_combine(dq2, dk2, dv2, dgate, dpa):
    bsz = dgate.shape[0]
    tm = CTX

    def body(dq_ref, dk_ref, dv_ref, dg_ref, dpa_ref, o_ref):
        t = pl.program_id(1)
        o_ref[:, 0:KD] = (dq_ref[0] + dq_ref[1]).astype(BF16)
        o_ref[:, KD:2 * KD] = (dk_ref[0] + dk_ref[1]).astype(BF16)
        o_ref[:, 2 * KD:2 * KD + VD] = (dv_ref[0] + dv_ref[1]).astype(BF16)
        o_ref[:, 2 * KD + VD:2 * KD + 2 * VD] = jnp.where(t > 0, dg_ref[...], 0).astype(BF16)
        o_ref[:, 2 * KD + 2 * VD:] = dpa_ref[...].astype(BF16)

    return pl.pallas_call(
        body, name="gla_combine", grid=(bsz, TT // tm),
        in_specs=[pl.BlockSpec((2, None, tm, KD), lambda b, t: (0, b, t, 0)),
                  pl.BlockSpec((2, None, tm, KD), lambda b, t: (0, b, t, 0)),
                  pl.BlockSpec((2, None, tm, VD), lambda b, t: (0, b, t, 0)),
                  pl.BlockSpec((None, tm, VD), lambda b, t: (b, jnp.maximum(t - 1, 0), 0)),
                  pl.BlockSpec((None, tm, 128), lambda b, t: (b, t, 0))],
        out_specs=pl.BlockSpec((None, tm, GLA_IN_PAD), lambda b, t: (b, t, 0)),
        out_shape=jax.ShapeDtypeStruct((bsz, TT, GLA_IN_PAD), BF16),
        compiler_params=pltpu.CompilerParams(dimension_semantics=("arbitrary", "arbitrary")),
    )(dq2, dk2, dv2, dgate, dpa)


def final_loss(h1, fo, gate, gain, tgt):
    bsz, t_len, _ = h1.shape
    tm = 256

    def body(h_ref, f_ref, gate_ref, gain_ref, tgt_ref, loss_ref, dh_ref, df_ref, dgate_ref, dgain_ref):
        b, t = pl.program_id(0), pl.program_id(1)
        target = tgt_ref[...]

        def core(h, fo_, gate_, gain_):
            e = _rms(h + gate_ * fo_) * gain_ - target
            return jnp.sum(0.5 * jnp.sum(e * e, axis=-1, keepdims=True) / D, axis=0, keepdims=True)

        loss, vjp = jax.vjp(core, h_ref[...], f_ref[...], gate_ref[...], gain_ref[...])
        dh, df, dgate, dgain = vjp(jnp.ones((1, 1), F32))
        dh_ref[...] = dh
        df_ref[...] = df.astype(BF16)
        first = jnp.logical_and(b == 0, t == 0)

        @pl.when(first)
        def _():
            loss_ref[...] = jnp.broadcast_to(loss, loss_ref.shape)
            dgain_ref[...] = dgain

        @pl.when(jnp.logical_not(first))
        def _():
            loss_ref[...] += jnp.broadcast_to(loss, loss_ref.shape)
            dgain_ref[...] += dgain

        @pl.when(t == 0)
        def _():
            dgate_ref[...] = dgate

        @pl.when(t > 0)
        def _():
            dgate_ref[...] += dgate

    tile = pl.BlockSpec((None, tm, D), lambda b, t: (b, t, 0))
    per_ex = pl.BlockSpec((None, 1, D), lambda b, t: (b, 0, 0))
    shared = pl.BlockSpec((1, D), lambda b, t: (0, 0))
    return pl.pallas_call(
        body, name="final_loss", grid=(bsz, t_len // tm),
        in_specs=[tile, tile, per_ex, shared, tile],
        out_specs=[pl.BlockSpec((8, 128), lambda b, t: (0, 0)), tile, tile, per_ex, shared],
        out_shape=[jax.ShapeDtypeStruct((8, 128), F32), jax.ShapeDtypeStruct(h1.shape, F32),
                   jax.ShapeDtypeStruct(h1.shape, BF16), jax.ShapeDtypeStruct((bsz, 1, D), F32),
                   jax.ShapeDtypeStruct((1, D), F32)],
        compiler_params=pltpu.CompilerParams(dimension_semantics=("arbitrary", "arbitrary")),
    )(h1, fo, gate, gain, tgt)


ADA_ROWS = 24
ADA_CTX_ROW = 16
ADA_COLS = 6 * D // N_DEV


def ada_fwd(cond, w, b):
    def body(c_ref, w_ref, b_ref, o_ref):
        s = silu(c_ref[...]).astype(BF16)
        o_ref[...] = jnp.dot(s, w_ref[...].astype(BF16), preferred_element_type=F32) + b_ref[...]

    return pl.pallas_call(
        body, name="ada_fwd", grid=(2,),
        in_specs=[pl.BlockSpec((ADA_ROWS, D), lambda i: (0, 0)), pl.BlockSpec((None, D, ADA_COLS), lambda i: (i, 0, 0)),
                  pl.BlockSpec((None, 1, ADA_COLS), lambda i: (i, 0, 0))],
        out_specs=pl.BlockSpec((None, ADA_ROWS, ADA_COLS), lambda i: (i, 0, 0)),
        out_shape=jax.ShapeDtypeStruct((2, ADA_ROWS, ADA_COLS), F32),
    )(cond, w, b)


def ada_bwd(cond, dm_mine, dm_full, w):
    def body(c_ref, dm_ref, dmf_ref, w_ref, gw_ref, gb_ref, cp_ref):
        i = pl.program_id(0)
        s = silu(c_ref[...]).astype(BF16)
        dm = dm_ref[...].astype(BF16)
        gw_ref[...] = _dot(s, dm, _TN)
        gb_ref[...] = jnp.sum(dmf_ref[...], axis=0, keepdims=True)

        @pl.when(i == 0)
        def _():
            cp_ref[...] = _dot(dm_ref[ADA_CTX_ROW:, :].astype(BF16), w_ref[...].astype(BF16), _NT)

    return pl.pallas_call(
        body, name="ada_bwd", grid=(2,),
        in_specs=[pl.BlockSpec((ADA_ROWS, D), lambda i: (0, 0)), pl.BlockSpec((None, ADA_ROWS, ADA_COLS), lambda i: (i, 0, 0)),
                  pl.BlockSpec((None, ADA_ROWS, 6 * D), lambda i: (i, 0, 0)), pl.BlockSpec((None, D, ADA_COLS), lambda i: (i, 0, 0))],
        out_specs=[pl.BlockSpec((None, D, ADA_COLS), lambda i: (i, 0, 0)), pl.BlockSpec((None, 1, 6 * D), lambda i: (i, 0, 0)),
                   pl.BlockSpec((ADA_ROWS - ADA_CTX_ROW, D), lambda i: (0, 0))],
        out_shape=[jax.ShapeDtypeStruct((2, D, ADA_COLS), F32), jax.ShapeDtypeStruct((2, 1, 6 * D), F32),
                   jax.ShapeDtypeStruct((ADA_ROWS - ADA_CTX_ROW, D), F32)],
        compiler_params=pltpu.CompilerParams(dimension_semantics=("arbitrary",)),
    )(cond, dm_mine, dm_full, w)


def cctx_grad(parts, c_ctx):
    def body(p_ref, c_ref, o_ref):
        tot = p_ref[0:1, :]
        for i in range(1, N_DEV):
            tot = tot + p_ref[i:i + 1, :]
        c = c_ref[...]
        sg = jax.nn.sigmoid(c)
        o_ref[...] = tot * sg * (1.0 + c * (1.0 - sg))

    return pl.pallas_call(body, name="cctx_grad", out_shape=jax.ShapeDtypeStruct((1, D), F32))(parts, c_ctx)


def _row_tile(r):
    for t in (512, 256, 128, 80, 64, 40, 32, 16, 8):
        if r % t == 0:
            return t
    return r


def _slot_sum(ref):
    tot = ref[0].astype(F32)
    for i in range(1, ref.shape[0]):
        tot = tot + ref[i].astype(F32)
    return tot


def sum_slots(name, x):
    s, r, c = x.shape
    tr = _row_tile(r)

    def body(x_ref, o_ref):
        o_ref[...] = _slot_sum(x_ref)

    return pl.pallas_call(
        body, name=name, grid=(r // tr,), in_specs=[pl.BlockSpec((s, tr, c), lambda i: (0, i, 0))],
        out_specs=pl.BlockSpec((tr, c), lambda i: (i, 0)), out_shape=jax.ShapeDtypeStruct((r, c), F32),
    )(x)


def adamw(name, w, g, m, v):
    r, c = w.shape
    tr = _row_tile(r)
    stacked = g.ndim == 3

    def body(w_ref, g_ref, m_ref, v_ref, go_ref, d_ref, mo_ref, vo_ref):
        gv = _slot_sum(g_ref) if stacked else g_ref[...]
        mn = B1 * m_ref[...] + (1.0 - B1) * gv
        vn = B2 * v_ref[...] + (1.0 - B2) * jnp.square(gv)
        m_hat = mn / (1.0 - B1 ** STEP)
        v_hat = vn / (1.0 - B2 ** STEP)
        go_ref[...] = gv
        d_ref[...] = -LR * (m_hat / (jnp.sqrt(v_hat) + AEPS) + WD * w_ref[...])
        mo_ref[...] = mn
        vo_ref[...] = vn

    tile = pl.BlockSpec((tr, c), lambda i: (i, 0))
    g_spec = pl.BlockSpec((g.shape[0], tr, c), lambda i: (0, i, 0)) if stacked else tile
    return pl.pallas_call(
        body, name=name, grid=(r // tr,), in_specs=[tile, g_spec, tile, tile], out_specs=[tile] * 4,
        out_shape=[jax.ShapeDtypeStruct((r, c), F32)] * 4,
    )(w, g, m, v)


def _place():
    return lax.axis_index("x"), lax.axis_index("y"), lax.axis_index("c")


def all_gather(name, x, in_vmem):
    r, c = x.shape
    space = pltpu.VMEM if in_vmem else pl.ANY

    def body(x_ref, out_ref, send_sems, recv_sems, local_sem):
        px, py, pc = _place()
        me, sibling = (px, py, pc), (px, py, 1 - pc)
        chips = [(1 - px, py), (px, 1 - py), (1 - px, 1 - py)]

        def rows(qx, qy, qc):
            return out_ref.at[pl.ds((4 * qx + 2 * qy + qc) * r, r), :]

        def copy(k, block, to, src=None):
            return pltpu.make_async_remote_copy(
                src_ref=rows(*block) if src is None else src, dst_ref=rows(*block),
                send_sem=send_sems.at[k], recv_sem=recv_sems.at[k], device_id=to, device_id_type=MESH)

        mine = pltpu.make_async_copy(x_ref, rows(*me), local_sem)
        mine.start()
        first = [copy(0, me, sibling, src=x_ref)]
        first += [copy(1 + j, me, (*chip, pc), src=x_ref) for j, chip in enumerate(chips)]
        for cp in first:
            cp.start()
        passed = [copy(4 + j, (*chip, pc), sibling) for j, chip in enumerate(chips)]
        for j, chip in enumerate(chips):
            copy(1 + j, (*chip, pc), me).wait_recv()
            passed[j].start()
        copy(0, sibling, me).wait_recv()
        for j, chip in enumerate(chips):
            copy(4 + j, (*chip, 1 - pc), me).wait_recv()
        for cp in first + passed:
            cp.wait_send()
        mine.wait()

    return pl.pallas_call(
        body, name=name, out_shape=jax.ShapeDtypeStruct((N_DEV * r, c), x.dtype),
        in_specs=[pl.BlockSpec(memory_space=space)], out_specs=pl.BlockSpec(memory_space=space),
        scratch_shapes=[pltpu.SemaphoreType.DMA((7,)), pltpu.SemaphoreType.DMA((7,)), pltpu.SemaphoreType.DMA],
    )(x)


def all_to_all(name, x):
    def body(x_ref, out_ref, send_sems, recv_sems, local_sem):
        px, py, pc = _place()
        me = 4 * px + 2 * py + pc
        mine = pltpu.make_async_copy(x_ref.at[me], out_ref.at[me], local_sem)
        mine.start()
        copies = []
        for k in range(1, N_DEV):
            qx = 1 - px if k & 4 else px
            qy = 1 - py if k & 2 else py
            qc = 1 - pc if k & 1 else pc
            copies.append(pltpu.make_async_remote_copy(
                src_ref=x_ref.at[4 * qx + 2 * qy + qc], dst_ref=out_ref.at[me],
                send_sem=send_sems.at[k - 1], recv_sem=recv_sems.at[k - 1], device_id=(qx, qy, qc), device_id_type=MESH))
        for cp in copies:
            cp.start()
        for cp in copies:
            cp.wait()
        mine.wait()

    return pl.pallas_call(
        body, name=name, out_shape=jax.ShapeDtypeStruct(x.shape, x.dtype),
        in_specs=[pl.BlockSpec(memory_space=pl.ANY)], out_specs=pl.BlockSpec(memory_space=pl.ANY),
        scratch_shapes=[pltpu.SemaphoreType.DMA((7,)), pltpu.SemaphoreType.DMA((7,)), pltpu.SemaphoreType.DMA],
    )(x)


NCF = FFN_H // FFN_TC


def _size(shape):
    n = 1
    for s in shape:
        n *= s
    return n


def _padded_rows(n_elems, row_mult):
    return -(-n_elems // (D * row_mult)) * row_mult


def _pack_rows(arrs, dtype, row_mult):
    rows, offs, r0 = [], [], 0
    for a in arrs:
        flat = a.reshape(-1).astype(dtype)
        n = _padded_rows(flat.shape[0], row_mult)
        rows.append(jnp.pad(flat, (0, n * D - flat.shape[0])).reshape(n, D))
        offs.append(r0)
        r0 += n
    return jnp.concatenate(rows, 0), offs


def _unpack_rows(buf, offs, shapes):
    lead, out = buf.shape[:-2], []
    for o, shp in zip(offs, shapes):
        n = _size(shp)
        nr = -(-n // D)
        out.append(buf[..., o:o + nr, :].reshape(lead + (nr * D,))[..., :n].reshape(lead + tuple(shp)))
    return out


def _interleave(a):
    return a.reshape(a.shape[:-1] + (2, NCF, FFN_TC)).swapaxes(-2, -3).reshape(a.shape)


def _deinterleave(a):
    return a.reshape(a.shape[:-1] + (NCF, 2, FFN_TC)).swapaxes(-2, -3).reshape(a.shape)


def _cols_from_shards(g):
    return g.transpose(1, 0, 2).reshape(g.shape[1], N_DEV * g.shape[2])


def _cols_to_shards(w):
    k, n = w.shape[0], w.shape[1] // N_DEV
    return w.reshape(k, N_DEV, n).transpose(1, 0, 2)


def _rows3(w):
    return [w[i:i + 1] for i in range(3)]


def f_mod1(xs, ps):
    return f_mod(xs, ps)[:1]


def kernel(x, c, ctx, c_ctx, ada_w, ada_b, norm_mix, norm_ffn, gla_w_in, gla_w_a2, gla_b_a, gla_head_norm, gla_w_out, sc_w_in, sc_conv_w, sc_w_out, ffn_w_up, ffn_conv_w, ffn_conv_b, ffn_w_down, final_norm, loss_target, m_c_ctx, m_ada_w, m_ada_b, m_norm_mix, m_norm_ffn, m_gla_w_in, m_gla_w_a2, m_gla_b_a, m_gla_head_norm, m_gla_w_out, m_sc_w_in, m_sc_conv_w, m_sc_w_out, m_ffn_w_up, m_ffn_conv_w, m_ffn_conv_b, m_ffn_w_down, m_final_norm, v_c_ctx, v_ada_w, v_ada_b, v_norm_mix, v_norm_ffn, v_gla_w_in, v_gla_w_a2, v_gla_b_a, v_gla_head_norm, v_gla_w_out, v_sc_w_in, v_sc_conv_w, v_sc_w_out, v_ffn_w_up, v_ffn_conv_w, v_ffn_conv_b, v_ffn_w_down, v_final_norm):
    names = ["c_ctx", "ada_w", "ada_b", "norm_mix", "norm_ffn", "gla_w_in", "gla_w_a2", "gla_b_a", "gla_head_norm",
             "gla_w_out", "sc_w_in", "sc_conv_w", "sc_w_out", "ffn_w_up", "ffn_conv_w", "ffn_conv_b", "ffn_w_down",
             "final_norm"]
    w_ = dict(zip(names, [c_ctx, ada_w, ada_b, norm_mix, norm_ffn, gla_w_in, gla_w_a2, gla_b_a, gla_head_norm, gla_w_out,
                          sc_w_in, sc_conv_w, sc_w_out, ffn_w_up, ffn_conv_w, ffn_conv_b, ffn_w_down, final_norm]))
    m_ = dict(zip(names, [m_c_ctx, m_ada_w, m_ada_b, m_norm_mix, m_norm_ffn, m_gla_w_in, m_gla_w_a2, m_gla_b_a,
                          m_gla_head_norm, m_gla_w_out, m_sc_w_in, m_sc_conv_w, m_sc_w_out, m_ffn_w_up, m_ffn_conv_w,
                          m_ffn_conv_b, m_ffn_w_down, m_final_norm]))
    v_ = dict(zip(names, [v_c_ctx, v_ada_w, v_ada_b, v_norm_mix, v_norm_ffn, v_gla_w_in, v_gla_w_a2, v_gla_b_a,
                          v_gla_head_norm, v_gla_w_out, v_sc_w_in, v_sc_conv_w, v_sc_w_out, v_ffn_w_up, v_ffn_conv_w,
                          v_ffn_conv_b, v_ffn_w_down, v_final_norm]))
    me = 4 * lax.axis_index("x") + 2 * lax.axis_index("y") + lax.axis_index("c")
    bsz = x.shape[0]
    tm = 256
    nt = SEQ // tm
    ctx_tiles = CTX // tm
    pe = functools.partial(P, per_example=True)

    small_sharded = [c, gla_w_a2, gla_b_a, sc_conv_w, ffn_conv_w]
    pack0, offs0 = _pack_rows(small_sharded, F32, 8)
    g0 = all_gather("ag_small", pack0, True).reshape(N_DEV, pack0.shape[0], D)
    c_all, wa2_s, ba_s, scw_s, fcw_s = _unpack_rows(g0, offs0, [a.shape for a in small_sharded])
    w_a2 = wa2_s[:, 0].transpose(1, 2, 0, 3).reshape(2, RANK, KD)
    b_a = ba_s[:, 0].transpose(1, 0, 2).reshape(2, KD)
    sc_cw = scw_s[:, 0].transpose(1, 0, 2).reshape(3, D)
    ffn_cw = fcw_s.transpose(1, 2, 0, 3).reshape(2, 3, 2 * FFN_H)

    cond = jnp.concatenate([c_all.reshape(N_DEV * bsz, D), c_ctx[None], jnp.zeros((ADA_ROWS - N_DEV * bsz - 1, D), F32)], 0)
    b_mine = lax.dynamic_slice(ada_b, (0, me * ADA_COLS), (2, ADA_COLS)).reshape(2, 1, ADA_COLS)
    mod_part = ada_fwd(cond, ada_w, b_mine)
    mod = all_gather("ag_mod", mod_part.reshape(2 * ADA_ROWS, ADA_COLS), True)
    mod = mod.reshape(N_DEV, 2, ADA_ROWS, ADA_COLS).transpose(1, 2, 0, 3).reshape(2, ADA_ROWS, 6 * D)
    mods = lax.dynamic_slice(mod, (0, bsz * me, 0), (2, bsz, 6 * D))
    md = [[mods[i][:, k * D:(k + 1) * D].reshape(bsz, 1, D) for k in range(6)] for i in range(2)]
    mc = [mod[0, ADA_CTX_ROW, k * D:(k + 1) * D][None] for k in range(2)]

    big = ["gla_w_in", "gla_w_out", "sc_w_in", "sc_w_out", "ffn_w_up", "ffn_w_down"]
    wpack, offs_w = _pack_rows([w_[n] for n in big], BF16, 16)
    gw = all_gather("ag_weights", wpack, False).reshape(N_DEV, wpack.shape[0], D)
    s_gin, s_gout, s_sin, s_sout, s_up, s_down = _unpack_rows(gw, offs_w, [w_[n].shape for n in big])
    w_gin = jnp.pad(_cols_from_shards(s_gin[:, 0]), ((0, 0), (0, GLA_IN_PAD - GLA_IN)))
    w_gout = s_gout[:, 0].reshape(VD, D)
    w_sin = _cols_from_shards(s_sin[:, 0])
    w_sout = s_sout[:, 0].reshape(D, D)
    w_up = [_interleave(_cols_from_shards(s_up[:, i])) for i in range(2)]
    w_down = [s_down[:, i].reshape(FFN_H, D) for i in range(2)]
    fcw = [_rows3(_interleave(ffn_cw[i])) for i in range(2)]
    fcb = [_interleave(ffn_conv_b[i])[None] for i in range(2)]
    wd = jnp.zeros((128, 2 * KD), F32).at[:RANK, :KD].set(w_a2[0]).at[RANK:2 * RANK, KD:].set(w_a2[1])
    bd = b_a.reshape(1, 2 * KD)
    scw = _rows3(sc_cw)
    head_gain = gla_head_norm.reshape(1, HV)
    gains_mix = [norm_mix[i][None] for i in range(2)]
    gains_ffn = [norm_ffn[i][None] for i in range(2)]
    ffn_w = 2 * FFN_TC

    def ffn_params(i):
        return [P(a, w=ffn_w, split=2) for a in fcw[i] + [fcb[i]]]

    def ffn_fwd(i, hn2):
        u = mm3(f"ffn_up{i}", hn2, w_up[i], out_dtype=BF16)
        act = rowwise(f"ffn_mid{i}", f_ffn_mid, [X(u, w=ffn_w, split=2)], ffn_params(i), tm=SEQ, nt=1, nc=NCF,
                      outs=[(FFN_TC, BF16, 1)])[0]
        return u, act, mm3(f"ffn_down{i}", act, w_down[i])

    def res_mod_fwd(name, h, y, ps):
        return rowwise(name, f_res_mod, [X(h), X(y)], ps, tm=tm, nt=nt, outs=[(D, F32, 1), (D, BF16, 1)])

    ps_in0 = [P(gains_mix[0]), pe(md[0][0]), pe(md[0][1])]
    ps_ctx = [P(gains_mix[0]), P(mc[0]), P(mc[1])]
    hn0 = rowwise("mod_in0", f_mod, [X(x)], ps_in0, tm=tm, nt=nt, outs=[(D, BF16, 1)])[0]
    hnc = rowwise("mod_ctx", f_mod, [X(ctx)], ps_ctx, tm=tm, nt=ctx_tiles, outs=[(D, BF16, 1)])[0]
    hcat = jnp.concatenate([hnc, hn0], axis=1)
    pcat = mm3("gla_in", hcat, w_gin)
    pa_x = X(pcat, w=128, co=(GLA_IN_PAD - 128) // 128)
    la = rowwise("gla_decay", f_decay, [pa_x], [P(wd), P(bd)], tm=tm, nt=TT // tm, outs=[(2 * KD, F32, 1)])[0]
    o2, s_all = gla_fwd(pcat, la)
    post_xs = [X(o2, w=VD, co=0, ro=ctx_tiles, split=HEADS), X(o2, w=VD, co=1, ro=ctx_tiles, split=HEADS),
               X(pcat, w=VD, co=2, ro=ctx_tiles, split=HEADS)]
    yin0 = rowwise("gla_post", f_gla_post, post_xs, [P(head_gain)], tm=tm, nt=nt, outs=[(VD, BF16, HEADS)])[0]
    y0 = mm3("gla_out", yin0, w_gout)
    ps_mid0 = [pe(md[0][2]), P(gains_ffn[0]), pe(md[0][3]), pe(md[0][4])]
    h1_0, hn2_0 = res_mod_fwd("res_mod_mid0", x, y0, ps_mid0)
    u0, act0, fo0 = ffn_fwd(0, hn2_0)
    ps_in1 = [pe(md[0][5]), P(gains_mix[1]), pe(md[1][0]), pe(md[1][1])]
    h2_0, hn1 = res_mod_fwd("res_mod_in1", h1_0, fo0, ps_in1)

    p1 = mm3("sc_in", hn1, w_sin)
    sc_ps = [P(a) for a in scw]
    yin1 = rowwise("sc_mid", f_sc_mid, [X(p1, split=3)], sc_ps, tm=tm, nt=nt, outs=[(D, BF16, 1)])[0]
    y1 = mm3("sc_out", yin1, w_sout)
    ps_mid1 = [pe(md[1][2]), P(gains_ffn[1]), pe(md[1][3]), pe(md[1][4])]
    h1_1, hn2_1 = res_mod_fwd("res_mod_mid1", h2_0, y1, ps_mid1)
    u1, act1, fo1 = ffn_fwd(1, hn2_1)
    loss8, dh1_1, dfo1, dm5_1, g_final = final_loss(h1_1, fo1, md[1][5], final_norm[None], loss_target)
    loss = lax.psum(loss8[0, 0], ("x", "y", "c"))

    def ffn_bwd(i, u, act, hn2, dfo):
        dact = mm3(f"ffn_down_dx{i}", dfo, w_down[i], tb=True, out_dtype=BF16)
        g_down = mm_tn(f"ffn_down_dw{i}", act, dfo)
        r = rowwise(f"ffn_mid_bwd{i}", f_ffn_mid, [X(u, w=ffn_w, split=2)], ffn_params(i), tm=SEQ, nt=1, nc=NCF,
                    douts=[X(dact, w=FFN_TC)], dx={0: BF16}, dp=[0, 1, 2, 3])
        du, g_cw, g_cb = r[0], jnp.concatenate(r[1:4], 0), r[4]
        dhn2 = mm3(f"ffn_up_dx{i}", du, w_up[i], tb=True, out_dtype=BF16)
        g_up = mm_tn(f"ffn_up_dw{i}", hn2, du)
        return dhn2, _deinterleave(g_up), g_down, _deinterleave(g_cw), _deinterleave(g_cb)

    def res_mod_bwd(name, h, y, ps, dh1, dhn):
        return rowwise(name, f_res_mod, [X(h), X(y)], ps, tm=tm, nt=nt, douts=[X(dh1), X(dhn)],
                       dx={0: F32, 1: BF16}, dp=[0, 1, 2, 3])

    dhn2_1, g_up1, g_down1, g_fcw1, g_fcb1 = ffn_bwd(1, u1, act1, hn2_1, dfo1)
    dh2_0, dy1, dm2_1, g_nffn1, dm3_1, dm4_1 = res_mod_bwd("res_mod_mid1_bwd", h2_0, y1, ps_mid1, dh1_1, dhn2_1)
    dyin1 = mm3("sc_out_dx", dy1, w_sout, tb=True, out_dtype=BF16)
    g_sout = mm_tn("sc_out_dw", yin1, dy1)
    r = rowwise("sc_mid_bwd", f_sc_mid, [X(p1, split=3)], sc_ps, tm=tm, nt=nt, douts=[X(dyin1)], dx={0: BF16}, dp=[0, 1, 2])
    dp1, g_scw = r[0], jnp.concatenate(r[1:4], 0)
    dhn1 = mm3("sc_in_dx", dp1, w_sin, tb=True, out_dtype=BF16)
    g_sin = mm_tn("sc_in_dw", hn1, dp1)
    dh1_0, dfo0, dm5_0, g_nmix1, dm0_1, dm1_1 = res_mod_bwd("res_mod_in1_bwd", h1_0, fo0, ps_in1, dh2_0, dhn1)

    dhn2_0, g_up0, g_down0, g_fcw0, g_fcb0 = ffn_bwd(0, u0, act0, hn2_0, dfo0)
    dx_res, dy0, dm2_0, g_nffn0, dm3_0, dm4_0 = res_mod_bwd("res_mod_mid0_bwd", x, y0, ps_mid0, dh1_0, dhn2_0)
    dyin0 = mm3("gla_out_dx", dy0, w_gout, tb=True, out_dtype=BF16)
    g_gout = mm_tn("gla_out_dw", yin0, dy0)
    do, dgate, g_head = rowwise("gla_post_bwd", f_gla_post, post_xs, [P(head_gain)], tm=tm, nt=nt,
                                douts=[X(dyin0, split=HEADS)], dx={0: F32, 2: BF16}, dp=[0])
    dq2, dk2, dv2, dla = gla_bwd(pcat, la, s_all, do)
    dpa, g_wd, g_bd = rowwise("gla_decay_bwd", f_decay, [pa_x], [P(wd), P(bd)], tm=tm, nt=TT // tm, douts=[X(dla)],
                              dx={0: BF16}, dp=[0, 1])
    dpcat = gla_combine(dq2, dk2, dv2, dgate, dpa)
    dhcat = mm3("gla_in_dx", dpcat, w_gin, tb=True, out_dtype=BF16)
    g_gin = mm_tn("gla_in_dw", hcat, dpcat)[:, :GLA_IN]
    grad_x, g_nmix0, dm0_0, dm1_0 = rowwise("mod_in0_bwd", f_mod, [X(x)], ps_in0, tm=tm, nt=nt,
                                            douts=[X(dhcat, ro=ctx_tiles), X(dx_res)], dx={0: F32}, dp=[0, 1, 2])
    g_nmix0c, dmc0, dmc1 = rowwise("mod_ctx_bwd", f_mod1, [X(ctx)], ps_ctx, tm=tm, nt=ctx_tiles, douts=[X(dhcat)],
                                   dx={}, dp=[0, 1, 2])

    zero_row = jnp.zeros((1, 4 * D), F32)
    dmod = [jnp.concatenate([jnp.concatenate([a.reshape(bsz, D) for a in dms], 1), ctx_row], 0)
            for dms, ctx_row in (([dm0_0, dm1_0, dm2_0, dm3_0, dm4_0, dm5_0], jnp.concatenate([dmc0, dmc1, zero_row], 1)),
                                 ([dm0_1, dm1_1, dm2_1, dm3_1, dm4_1, dm5_1], jnp.zeros((1, 6 * D), F32)))]
    g_wa2 = jnp.stack([g_wd[:RANK, :KD], g_wd[RANK:2 * RANK, KD:]])
    small_grads = [jnp.stack(dmod), jnp.concatenate([g_nmix0 + g_nmix0c, g_nmix1], 0), jnp.concatenate([g_nffn0, g_nffn1], 0),
                   g_head, jnp.concatenate([g_fcb0, g_fcb1], 0), g_final, g_wa2, g_bd.reshape(2, KD), g_scw,
                   jnp.stack([g_fcw0, g_fcw1])]
    pack1, offs1 = _pack_rows(small_grads, F32, 8)
    g1 = all_gather("ag_grads", pack1, True).reshape(N_DEV, pack1.shape[0], D)
    dmod_all = _unpack_rows(g1, offs1[:1], [small_grads[0].shape])[0]
    tot = _unpack_rows(sum_slots("sum_small", g1), offs1, [a.shape for a in small_grads])
    dm_rows = dmod_all[:, :, :bsz].transpose(1, 0, 2, 3).reshape(2, N_DEV * bsz, 6 * D)
    dm_full = jnp.concatenate([dm_rows, tot[0][:, bsz:], jnp.zeros((2, ADA_ROWS - N_DEV * bsz - 1, 6 * D), F32)], 1)
    dm_mine = lax.dynamic_slice(dm_full, (0, 0, me * ADA_COLS), (2, ADA_ROWS, ADA_COLS))
    g_ada_w, g_ada_b, cpart = ada_bwd(cond, dm_mine, dm_full, ada_w)
    cparts = all_gather("ag_cctx", cpart, True).reshape(N_DEV, ADA_ROWS - ADA_CTX_ROW, D)[:, 0]
    g_cctx = cctx_grad(cparts, c_ctx[None])[0]

    def my_cols(full, n):
        return lax.dynamic_slice_in_dim(full, me * n, n, axis=full.ndim - 1)

    grads = {
        "c_ctx": g_cctx, "ada_b": g_ada_b.reshape(2, 6 * D), "norm_mix": tot[1], "norm_ffn": tot[2],
        "gla_head_norm": tot[3], "ffn_conv_b": tot[4], "final_norm": tot[5].reshape(D),
        "gla_w_a2": my_cols(tot[6], KD // N_DEV)[None], "gla_b_a": my_cols(tot[7], KD // N_DEV)[None],
        "sc_conv_w": my_cols(tot[8], D // N_DEV)[None], "ffn_conv_w": my_cols(tot[9], 2 * FFN_H // N_DEV),
    }

    def row_slots(g):
        return g.reshape(N_DEV, -1, D)

    def col_slots(g):
        return _cols_to_shards(g).reshape(N_DEV, -1, D)

    slots = [col_slots(g_gin), row_slots(g_gout), col_slots(g_sin), row_slots(g_sout),
             jnp.concatenate([col_slots(g_up0), col_slots(g_up1)], 1), jnp.concatenate([row_slots(g_down0), row_slots(g_down1)], 1)]
    slots = [jnp.pad(s, ((0, 0), (0, (-s.shape[1]) % 16), (0, 0))).astype(BF16) for s in slots]
    gpack = jnp.concatenate(slots, 1)
    recv = all_to_all("a2a_grads", gpack)

    def local_pack(tree, keys):
        return _pack_rows([tree[n] for n in keys], F32, 16)[0]

    res_big = adamw("adamw_big", local_pack(w_, big), recv, local_pack(m_, big), local_pack(v_, big))
    res_ada = adamw("adamw_ada", *[a.reshape(2 * D, ADA_COLS) for a in (ada_w, g_ada_w, m_ada_w, v_ada_w)])
    small = [n for n in names if n not in big and n != "ada_w"]
    g_small = _pack_rows([grads[n] for n in small], F32, 8)[0]
    res_small = adamw("adamw_small", _pack_rows([w_[n] for n in small], F32, 8)[0], g_small,
                      _pack_rows([m_[n] for n in small], F32, 8)[0], _pack_rows([v_[n] for n in small], F32, 8)[0])
    offs_s = _pack_rows([w_[n] for n in small], F32, 8)[1]

    out = {}
    for kind, idx in (("grad", 0), ("delta", 1), ("new_m", 2), ("new_v", 3)):
        vals = dict(zip(big, _unpack_rows(res_big[idx], offs_w, [w_[n].shape for n in big])))
        vals["ada_w"] = res_ada[idx].reshape(ada_w.shape)
        vals.update(zip(small, _unpack_rows(res_small[idx], offs_s, [w_[n].shape for n in small])))
        out[kind] = [vals[n] for n in names]
    return (loss, grad_x, *out["grad"], *out["delta"], *out["new_m"], *out["new_v"])
```

```python
import functools

import jax
import jax.numpy as jnp
from jax import lax
from jax.experimental import pallas as pl
from jax.experimental.pallas import tpu as pltpu

F32 = jnp.float32
BF16 = jnp.bfloat16

N_DEV = 8
D = 1024
SEQ = 2048
CTX = 256
TT = CTX + SEQ
GRID_W = 64
CHUNK = 64
HEADS = 4
HK = 128
HV = 256
KD = 512
VD = 1024
RANK = 16
TAU = 16.0
GLA_IN = 3104
GLA_IN_PAD = 3200
FFN_H = 2560
FFN_TC = 256
EPS = 1e-6
LR, B1, B2, AEPS, WD, STEP = 0.001, 0.9, 0.999, 1e-08, 0.01, 10
MESH = pl.DeviceIdType.MESH


def _pick(dim, cap=1024):
    for t in range(cap, 0, -128):
        if dim % t == 0:
            return t
    return dim


def mm(name, a, b, *, ta=False, tb=False, out_dtype=F32):
    K, M = a.shape if ta else a.shape[::-1]
    N = b.shape[0] if tb else b.shape[1]
    assert (b.shape[1] if tb else b.shape[0]) == K, (name, a.shape, b.shape)
    tm, tn, tk = _pick(M), _pick(N), _pick(K)
    nk = K // tk
    dn = (((0 if ta else 1,), (1 if tb else 0,)), ((), ()))

    def body(a_ref, b_ref, o_ref, acc_ref):
        k = pl.program_id(2)
        part = lax.dot_general(a_ref[...].astype(BF16), b_ref[...].astype(BF16), dn, preferred_element_type=F32)

        @pl.when(k == 0)
        def _():
            acc_ref[...] = part

        @pl.when(k > 0)
        def _():
            acc_ref[...] += part

        @pl.when(k == nk - 1)
        def _():
            o_ref[...] = acc_ref[...].astype(out_dtype)

    a_spec = pl.BlockSpec((tk, tm), lambda i, j, k: (k, i)) if ta else pl.BlockSpec((tm, tk), lambda i, j, k: (i, k))
    b_spec = pl.BlockSpec((tn, tk), lambda i, j, k: (j, k)) if tb else pl.BlockSpec((tk, tn), lambda i, j, k: (k, j))
    return pl.pallas_call(
        body, name=name, grid=(M // tm, N // tn, nk),
        in_specs=[a_spec, b_spec], out_specs=pl.BlockSpec((tm, tn), lambda i, j, k: (i, j)),
        out_shape=jax.ShapeDtypeStruct((M, N), out_dtype),
        scratch_shapes=[pltpu.VMEM((tm, tn), F32)],
        compiler_params=pltpu.CompilerParams(dimension_semantics=("parallel", "parallel", "arbitrary")),
    )(a, b)


def mm3(name, a, b, **kw):
    bsz, t, k = a.shape
    return mm(name, a.reshape(bsz * t, k), b, **kw).reshape(bsz, t, -1)


def mm_tn(name, a, b):
    return mm(name, a.reshape(-1, a.shape[-1]), b.reshape(-1, b.shape[-1]), ta=True)


def X(arr, w=None, co=0, ro=0, split=1):
    return dict(a=arr, w=arr.shape[-1] if w is None else w, co=co, ro=ro, split=split)


def P(arr, per_example=False, w=None, split=1):
    return dict(a=arr, e=per_example, w=arr.shape[-1] if w is None else w, split=split)


def _pieces(ref, split):
    w = ref.shape[-1] // split
    return [ref[:, i * w:(i + 1) * w] for i in range(split)]


def _store(ref, pieces, accumulate=False):
    w = ref.shape[-1] // len(pieces)
    for i, p in enumerate(pieces):
        if accumulate:
            ref[:, i * w:(i + 1) * w] += p.astype(ref.dtype)
        else:
            ref[:, i * w:(i + 1) * w] = p.astype(ref.dtype)


def rowwise(name, f, xs, ps, *, tm, nt, nc=1, outs=None, douts=None, dx=None, dp=None):
    bsz = xs[0]["a"].shape[0]
    fwd = douts is None
    nx, np_ = len(xs), len(ps)
    douts = [] if fwd else douts
    dx = {} if fwd else dx
    dp = [] if fwd else dp

    def x_spec(s):
        return pl.BlockSpec((None, tm, s["w"]), lambda c, b, t, s=s: (b, t + s["ro"], c + s["co"]))

    def p_spec(s):
        r = s["a"].shape[-2]
        if s["e"]:
            return pl.BlockSpec((None, r, s["w"]), lambda c, b, t: (b, 0, c))
        return pl.BlockSpec((r, s["w"]), lambda c, b, t: (0, c))

    in_specs = [x_spec(s) for s in xs] + [p_spec(s) for s in ps] + [x_spec(s) for s in douts]
    operands = [s["a"] for s in xs] + [s["a"] for s in ps] + [s["a"] for s in douts]
    if fwd:
        out_shape = [jax.ShapeDtypeStruct((bsz, nt * tm, nc * w), dt) for (w, dt, _) in outs]
        out_specs = [pl.BlockSpec((None, tm, w), lambda c, b, t: (b, t, c)) for (w, _, _) in outs]
    else:
        out_shape = [jax.ShapeDtypeStruct((bsz, nt * tm, nc * xs[i]["w"]), dt) for i, dt in dx.items()]
        out_specs = [pl.BlockSpec((None, tm, xs[i]["w"]), lambda c, b, t: (b, t, c)) for i in dx]
        for j in dp:
            s = ps[j]
            r = s["a"].shape[-2]
            if s["e"]:
                out_shape.append(jax.ShapeDtypeStruct((bsz, r, nc * s["w"]), F32))
                out_specs.append(pl.BlockSpec((None, r, s["w"]), lambda c, b, t: (b, 0, c)))
            else:
                out_shape.append(jax.ShapeDtypeStruct((r, nc * s["w"]), F32))
                out_specs.append(pl.BlockSpec((r, s["w"]), lambda c, b, t: (0, c)))

    def body(*refs):
        x_refs, p_refs = refs[:nx], refs[nx:nx + np_]
        d_refs = refs[nx + np_:nx + np_ + len(douts)]
        o_refs = refs[nx + np_ + len(douts):]
        xv = [[p.astype(F32) for p in _pieces(r, s["split"])] for r, s in zip(x_refs, xs)]
        pv = [[p.astype(F32) for p in _pieces(r, s["split"])] for r, s in zip(p_refs, ps)]
        if fwd:
            for r, pieces in zip(o_refs, f(xv, pv)):
                _store(r, pieces)
            return
        _, vjp = jax.vjp(f, xv, pv)
        cot = [[p.astype(F32) for p in _pieces(r, s["split"])] for r, s in zip(d_refs, douts)]
        dxv, dpv = vjp(cot)
        for r, i in zip(o_refs, dx):
            _store(r, dxv[i])
        b, t = pl.program_id(1), pl.program_id(2)
        for r, j in zip(o_refs[len(dx):], dp):
            first = (t == 0) if ps[j]["e"] else jnp.logical_and(b == 0, t == 0)

            @pl.when(first)
            def _(r=r, j=j):
                _store(r, dpv[j])

            @pl.when(jnp.logical_not(first))
            def _(r=r, j=j):
                _store(r, dpv[j], accumulate=True)

    res = pl.pallas_call(
        body, name=name, grid=(nc, bsz, nt), in_specs=in_specs, out_specs=out_specs, out_shape=out_shape,
        compiler_params=pltpu.CompilerParams(dimension_semantics=("arbitrary", "arbitrary", "arbitrary")),
    )(*operands)
    return res


def _keep_rows(a, shift, keep):
    n = a.shape[0]
    t = lax.broadcasted_iota(jnp.int32, a.shape, 0)
    return jnp.where(keep(t, n), pltpu.roll(a, shift % n, 0), 0.0)


def _shift_pair(step, keep_prev, keep_next):
    @jax.custom_vjp
    def prev(a):
        return _keep_rows(a, step, keep_prev)

    @jax.custom_vjp
    def nxt(a):
        return _keep_rows(a, -step, keep_next)

    prev.defvjp(lambda a: (prev(a), None), lambda _, g: (nxt(g),))
    nxt.defvjp(lambda a: (nxt(a), None), lambda _, g: (prev(g),))
    return prev, nxt


prev_tok, next_tok = _shift_pair(1, lambda t, n: t % GRID_W != 0, lambda t, n: t % GRID_W != GRID_W - 1)
prev_row, next_row = _shift_pair(GRID_W, lambda t, n: t >= GRID_W, lambda t, n: t < n - GRID_W)


@jax.custom_vjp
def bdot(a, w):
    return jnp.dot(a.astype(BF16), w.astype(BF16), preferred_element_type=F32)


def _bdot_bwd(res, g):
    a, w = res
    gb = g.astype(BF16)
    da = lax.dot_general(gb, w.astype(BF16), (((1,), (1,)), ((), ())), preferred_element_type=F32)
    dw = lax.dot_general(a.astype(BF16), gb, (((0,), (0,)), ((), ())), preferred_element_type=F32)
    return da, dw


bdot.defvjp(lambda a, w: (bdot(a, w), (a, w)), _bdot_bwd)


@jax.custom_vjp
def log_sigmoid(z):
    return jnp.minimum(z, 0.0) - jnp.log(1.0 + jnp.exp(-jnp.abs(z)))


def _lsig_bwd(z, g):
    e = jnp.exp(-jnp.abs(z))
    return (g * jnp.where(z >= 0, e, 1.0) / (1.0 + e),)


log_sigmoid.defvjp(lambda z: (log_sigmoid(z), z), _lsig_bwd)


def silu(x):
    return x * jax.nn.sigmoid(x)


def _rms(x):
    return x * lax.rsqrt(jnp.mean(x * x, axis=-1, keepdims=True) + EPS)


def _mod(x, gain, shift, scale):
    return _rms(x) * gain * (1.0 + scale) + shift


def f_mod(xs, ps):
    ((h,),), ((gain,), (shift,), (scale,)) = xs, ps
    return [[_mod(h, gain, shift, scale)], [h]]


def f_res_mod(xs, ps):
    ((h,), (y,)), ((gate,), (gain,), (shift,), (scale,)) = xs, ps
    h1 = h + gate * y
    return [[h1], [_mod(h1, gain, shift, scale)]]


def f_ffn_mid(xs, ps):
    ((ua, ug),), ((w0a, w0g), (w1a, w1g), (w2a, w2g), (ba, bg)) = xs, ps
    a = w0a * prev_row(ua) + w1a * ua + w2a * next_row(ua) + ba
    g = w0g * prev_row(ug) + w1g * ug + w2g * next_row(ug) + bg
    return [[a * silu(g)]]


def f_sc_mid(xs, ps):
    ((bg, cg, v),), ((w0,), (w1,), (w2,)) = xs, ps
    z = cg * v
    return [[bg * (w0 * prev_tok(z) + w1 * z + w2 * next_tok(z))]]


def f_decay(xs, ps):
    ((a,),), ((wd,), (bd,)) = xs, ps
    return [[log_sigmoid(bdot(a, wd) + bd) / TAU]]


def f_gla_post(xs, ps):
    (of, ob, g), ((gain,),) = xs, ps
    return [[_rms(a + b) * gain * silu(c) for a, b, c in zip(of, ob, g)]]


def f_sum_pairs(xs, ps):
    return [[a + b for a, b in zip(xs[0], xs[1])]]


NCH = TT // CHUNK
CTX_CH = CTX // CHUNK
_NT = (((1,), (1,)), ((), ()))
_TN = (((0,), (0,)), ((), ()))
_NN = (((1,), (0,)), ((), ()))


def _chunk_of(d, j):
    return jnp.where(d == 0, j, jnp.where(j < CTX_CH, CTX_CH - 1 - j, NCH + CTX_CH - 1 - j))


def _dot(a, b, dn):
    return lax.dot_general(a, b, dn, preferred_element_type=F32)


def _mask_dot(m, g):
    g0 = g.astype(BF16)
    r1 = g - g0.astype(F32)
    g1 = r1.astype(BF16)
    g2 = (r1 - g1.astype(F32)).astype(BF16)
    return _dot(m, g0, _NN) + _dot(m, g1, _NN) + _dot(m, g2, _NN)


def _causal(d):
    row = lax.broadcasted_iota(jnp.int32, (CHUNK, CHUNK), 0)
    col = lax.broadcasted_iota(jnp.int32, (CHUNK, CHUNK), 1)
    delta = jnp.where(d == 0, col - row, row - col)
    return delta <= 0, delta >= 0


def _gla_in_specs(rev):
    def blk(d, j):
        return _chunk_of(d, (NCH - 1 - j) if rev else j)

    return [
        pl.BlockSpec((None, CHUNK, KD), lambda b, d, j: (b, blk(d, j), 0)),
        pl.BlockSpec((None, CHUNK, KD), lambda b, d, j: (b, blk(d, j), 1)),
        pl.BlockSpec((None, CHUNK, VD), lambda b, d, j: (b, blk(d, j), 1)),
        pl.BlockSpec((None, CHUNK, KD), lambda b, d, j: (b, blk(d, j), d)),
    ], blk


def gla_fwd(pcat, la):
    bsz = pcat.shape[0]
    in_specs, blk = _gla_in_specs(False)

    def body(q_ref, k_ref, v_ref, la_ref, o_ref, s_ref, st):
        d, j = pl.program_id(1), pl.program_id(2)

        @pl.when(j == 0)
        def _():
            st[...] = jnp.zeros_like(st)

        s_ref[...] = st[...]
        causal, _ = _causal(d)
        mf = causal.astype(BF16)
        for h in range(HEADS):
            ks_, vs_ = slice(h * HK, (h + 1) * HK), slice(h * HV, (h + 1) * HV)
            q, k, v, g = q_ref[:, ks_] * (HK ** -0.5), k_ref[:, ks_], v_ref[:, vs_].astype(BF16), la_ref[:, ks_]
            b = _mask_dot(mf, g)
            bl = jnp.sum(g, axis=0, keepdims=True)
            qs = (q * jnp.exp(b)).astype(BF16)
            ks = (k * jnp.exp(-b)).astype(BF16)
            kd = (k * jnp.exp(bl - b)).astype(BF16)
            s = st[h]
            att = jnp.where(causal, _dot(qs, ks, _NT), 0.0).astype(BF16)
            o_ref[:, vs_] = _dot(qs, s.astype(BF16), _NT) + _dot(att, v, _NN)
            st[h] = jnp.exp(bl) * s + _dot(v, kd, _TN)

    return pl.pallas_call(
        body, name="gla_fwd", grid=(bsz, 2, NCH), in_specs=in_specs,
        out_specs=[pl.BlockSpec((None, CHUNK, VD), lambda b, d, j: (b, blk(d, j), d)),
                   pl.BlockSpec((None, None, None, HEADS, HV, HK), lambda b, d, j: (b, d, j, 0, 0, 0))],
        out_shape=[jax.ShapeDtypeStruct((bsz, TT, 2 * VD), F32), jax.ShapeDtypeStruct((bsz, 2, NCH, HEADS, HV, HK), F32)],
        scratch_shapes=[pltpu.VMEM((HEADS, HV, HK), F32)],
        compiler_params=pltpu.CompilerParams(dimension_semantics=("arbitrary", "arbitrary", "arbitrary")),
    )(pcat, pcat, pcat, la)


def gla_bwd(pcat, la, s_all, do):
    bsz = pcat.shape[0]
    in_specs, blk = _gla_in_specs(True)
    in_specs += [
        pl.BlockSpec((None, None, None, HEADS, HV, HK), lambda b, d, j: (b, d, NCH - 1 - j, 0, 0, 0)),
        pl.BlockSpec((None, CHUNK, VD), lambda b, d, j: (b, jnp.maximum(blk(d, j) - CTX_CH, 0), 0)),
    ]

    def body(q_ref, k_ref, v_ref, la_ref, s_ref, do_ref, dq_ref, dk_ref, dv_ref, dla_ref, dst):
        d, j = pl.program_id(1), pl.program_id(2)

        @pl.when(j == 0)
        def _():
            dst[...] = jnp.zeros_like(dst)

        latent = blk(d, j) >= CTX_CH
        causal, causal_t = _causal(d)
        mt = causal_t.astype(BF16)
        mf = causal.astype(BF16)
        scale = HK ** -0.5
        for h in range(HEADS):
            ks_, vs_ = slice(h * HK, (h + 1) * HK), slice(h * HV, (h + 1) * HV)
            q, k, v, g = q_ref[:, ks_] * scale, k_ref[:, ks_], v_ref[:, vs_].astype(BF16), la_ref[:, ks_]
            b = _mask_dot(mf, g)
            bl = jnp.sum(g, axis=0, keepdims=True)
            e, ei, ed, el = jnp.exp(b), jnp.exp(-b), jnp.exp(bl - b), jnp.exp(bl)
            qs, ks, kd = q * e, k * ei, k * ed
            qsb, ksb, kdb = qs.astype(BF16), ks.astype(BF16), kd.astype(BF16)
            s, ds1 = s_ref[h], dst[h]
            sb, ds1b = s.astype(BF16), ds1.astype(BF16)
            dob = jnp.where(latent, do_ref[:, vs_], 0.0).astype(BF16)
            att = jnp.where(causal, _dot(qsb, ksb, _NT), 0.0).astype(BF16)
            datt = jnp.where(causal, _dot(dob, v, _NT), 0.0).astype(BF16)
            dqs = _dot(dob, sb, _NN) + _dot(datt, ksb, _NN)
            dks = _dot(datt, qsb, _TN)
            dv_ref[:, vs_] = _dot(att, dob, _TN) + _dot(kdb, ds1b, _NT)
            dkd = _dot(v, ds1b, _NN)
            dst[h] = _dot(dob, qsb, _TN) + el * ds1
            del_ = jnp.sum(s * ds1, axis=0, keepdims=True)
            dq_ref[:, ks_] = dqs * e * scale
            dk_ref[:, ks_] = dks * ei + dkd * ed
            db = dqs * qs - dks * ks - dkd * kd
            dbl = jnp.sum(dkd * kd, axis=0, keepdims=True) + del_ * el
            dla_ref[:, ks_] = _mask_dot(mt, db) + dbl

    return pl.pallas_call(
        body, name="gla_bwd", grid=(bsz, 2, NCH), in_specs=in_specs,
        out_specs=[pl.BlockSpec((None, None, CHUNK, KD), lambda b, d, j: (d, b, blk(d, j), 0)),
                   pl.BlockSpec((None, None, CHUNK, KD), lambda b, d, j: (d, b, blk(d, j), 0)),
                   pl.BlockSpec((None, None, CHUNK, VD), lambda b, d, j: (d, b, blk(d, j), 0)),
                   pl.BlockSpec((None, CHUNK, KD), lambda b, d, j: (b, blk(d, j), d))],
        out_shape=[jax.ShapeDtypeStruct((2, bsz, TT, KD), F32), jax.ShapeDtypeStruct((2, bsz, TT, KD), F32),
                   jax.ShapeDtypeStruct((2, bsz, TT, VD), F32), jax.ShapeDtypeStruct((bsz, TT, 2 * KD), F32)],
        scratch_shapes=[pltpu.VMEM((HEADS, HV, HK), F32)],
        compiler_params=pltpu.CompilerParams(dimension_semantics=("arbitrary", "arbitrary", "arbitrary")),
    )(pcat, pcat, pcat, la, s_all, do)


def gla_combine(dq2, dk2, dv2, dgate, dpa):
    bsz = dgate.shape[0]
    tm = CTX

    def body(dq_ref, dk_ref, dv_ref, dg_ref, dpa_ref, o_ref):
        t = pl.program_id(1)
        o_ref[:, 0:KD] = (dq_ref[0] + dq_ref[1]).astype(BF16)
        o_ref[:, KD:2 * KD] = (dk_ref[0] + dk_ref[1]).astype(BF16)
        o_ref[:, 2 * KD:2 * KD + VD] = (dv_ref[0] + dv_ref[1]).astype(BF16)
        o_ref[:, 2 * KD + VD:2 * KD + 2 * VD] = jnp.where(t > 0, dg_ref[...], 0).astype(BF16)
        o_ref[:, 2 * KD + 2 * VD:] = dpa_ref[...].astype(BF16)

    return pl.pallas_call(
        body, name="gla_combine", grid=(bsz, TT // tm),
        in_specs=[pl.BlockSpec((2, None, tm, KD), lambda b, t: (0, b, t, 0)),
                  pl.BlockSpec((2, None, tm, KD), lambda b, t: (0, b, t, 0)),
                  pl.BlockSpec((2, None, tm, VD), lambda b, t: (0, b, t, 0)),
                  pl.BlockSpec((None, tm, VD), lambda b, t: (b, jnp.maximum(t - 1, 0), 0)),
                  pl.BlockSpec((None, tm, 128), lambda b, t: (b, t, 0))],
        out_specs=pl.BlockSpec((None, tm, GLA_IN_PAD), lambda b, t: (b, t, 0)),
        out_shape=jax.ShapeDtypeStruct((bsz, TT, GLA_IN_PAD), BF16),
        compiler_params=pltpu.CompilerParams(dimension_semantics=("arbitrary", "arbitrary")),
    )(dq2, dk2, dv2, dgate, dpa)


def final_loss(h1, fo, gate, gain, tgt):
    bsz, t_len, _ = h1.shape
    tm = 256

    def body(h_ref, f_ref, gate_ref, gain_ref, tgt_ref, loss_ref, dh_ref, df_ref, dgate_ref, dgain_ref):
        b, t = pl.program_id(0), pl.program_id(1)
        target = tgt_ref[...]

        def core(h, fo_, gate_, gain_):
            e = _rms(h + gate_ * fo_) * gain_ - target
            return jnp.sum(0.5 * jnp.sum(e * e, axis=-1, keepdims=True) / D, axis=0, keepdims=True)

        loss, vjp = jax.vjp(core, h_ref[...], f_ref[...], gate_ref[...], gain_ref[...])
        dh, df, dgate, dgain = vjp(jnp.ones((1, 1), F32))
        dh_ref[...] = dh
        df_ref[...] = df.astype(BF16)
        first = jnp.logical_and(b == 0, t == 0)

        @pl.when(first)
        def _():
            loss_ref[...] = jnp.broadcast_to(loss, loss_ref.shape)
            dgain_ref[...] = dgain

        @pl.when(jnp.logical_not(first))
        def _():
            loss_ref[...] += jnp.broadcast_to(loss, loss_ref.shape)
            dgain_ref[...] += dgain

        @pl.when(t == 0)
        def _():
            dgate_ref[...] = dgate

        @pl.when(t > 0)
        def _():
            dgate_ref[...] += dgate

    tile = pl.BlockSpec((None, tm, D), lambda b, t: (b, t, 0))
    per_ex = pl.BlockSpec((None, 1, D), lambda b, t: (b, 0, 0))
    shared = pl.BlockSpec((1, D), lambda b, t: (0, 0))
    return pl.pallas_call(
        body, name="final_loss", grid=(bsz, t_len // tm),
        in_specs=[tile, tile, per_ex, shared, tile],
        out_specs=[pl.BlockSpec((8, 128), lambda b, t: (0, 0)), tile, tile, per_ex, shared],
        out_shape=[jax.ShapeDtypeStruct((8, 128), F32), jax.ShapeDtypeStruct(h1.shape, F32),
                   jax.ShapeDtypeStruct(h1.shape, BF16), jax.ShapeDtypeStruct((bsz, 1, D), F32),
                   jax.ShapeDtypeStruct((1, D), F32)],
        compiler_params=pltpu.CompilerParams(dimension_semantics=("arbitrary", "arbitrary")),
    )(h1, fo, gate, gain, tgt)


ADA_ROWS = 24
ADA_CTX_ROW = 16
ADA_COLS = 6 * D // N_DEV


def ada_fwd(cond, w, b):
    def body(c_ref, w_ref, b_ref, o_ref):
        s = silu(c_ref[...]).astype(BF16)
        o_ref[...] = jnp.dot(s, w_ref[...].astype(BF16), preferred_element_type=F32) + b_ref[...]

    return pl.pallas_call(
        body, name="ada_fwd", grid=(2,),
        in_specs=[pl.BlockSpec((ADA_ROWS, D), lambda i: (0, 0)), pl.BlockSpec((None, D, ADA_COLS), lambda i: (i, 0, 0)),
                  pl.BlockSpec((None, 1, ADA_COLS), lambda i: (i, 0, 0))],
        out_specs=pl.BlockSpec((None, ADA_ROWS, ADA_COLS), lambda i: (i, 0, 0)),
        out_shape=jax.ShapeDtypeStruct((2, ADA_ROWS, ADA_COLS), F32),
    )(cond, w, b)


def ada_bwd(cond, dm_mine, dm_full, w):
    def body(c_ref, dm_ref, dmf_ref, w_ref, gw_ref, gb_ref, cp_ref):
        i = pl.program_id(0)
        s = silu(c_ref[...]).astype(BF16)
        dm = dm_ref[...].astype(BF16)
        gw_ref[...] = _dot(s, dm, _TN)
        gb_ref[...] = jnp.sum(dmf_ref[...], axis=0, keepdims=True)

        @pl.when(i == 0)
        def _():
            cp_ref[...] = _dot(dm_ref[ADA_CTX_ROW:, :].astype(BF16), w_ref[...].astype(BF16), _NT)

    return pl.pallas_call(
        body, name="ada_bwd", grid=(2,),
        in_specs=[pl.BlockSpec((ADA_ROWS, D), lambda i: (0, 0)), pl.BlockSpec((None, ADA_ROWS, ADA_COLS), lambda i: (i, 0, 0)),
                  pl.BlockSpec((None, ADA_ROWS, 6 * D), lambda i: (i, 0, 0)), pl.BlockSpec((None, D, ADA_COLS), lambda i: (i, 0, 0))],
        out_specs=[pl.BlockSpec((None, D, ADA_COLS), lambda i: (i, 0, 0)), pl.BlockSpec((None, 1, 6 * D), lambda i: (i, 0, 0)),
                   pl.BlockSpec((ADA_ROWS - ADA_CTX_ROW, D), lambda i: (0, 0))],
        out_shape=[jax.ShapeDtypeStruct((2, D, ADA_COLS), F32), jax.ShapeDtypeStruct((2, 1, 6 * D), F32),
                   jax.ShapeDtypeStruct((ADA_ROWS - ADA_CTX_ROW, D), F32)],
        compiler_params=pltpu.CompilerParams(dimension_semantics=("arbitrary",)),
    )(cond, dm_mine, dm_full, w)


def cctx_grad(parts, c_ctx):
    def body(p_ref, c_ref, o_ref):
        tot = p_ref[0:1, :]
        for i in range(1, N_DEV):
            tot = tot + p_ref[i:i + 1, :]
        c = c_ref[...]
        sg = jax.nn.sigmoid(c)
        o_ref[...] = tot * sg * (1.0 + c * (1.0 - sg))

    return pl.pallas_call(body, name="cctx_grad", out_shape=jax.ShapeDtypeStruct((1, D), F32))(parts, c_ctx)


def _row_tile(r):
    for t in (512, 256, 128, 80, 64, 40, 32, 16, 8):
        if r % t == 0:
            return t
    return r


def _slot_sum(ref):
    tot = ref[0].astype(F32)
    for i in range(1, ref.shape[0]):
        tot = tot + ref[i].astype(F32)
    return tot


def sum_slots(name, x):
    s, r, c = x.shape
    tr = _row_tile(r)

    def body(x_ref, o_ref):
        o_ref[...] = _slot_sum(x_ref)

    return pl.pallas_call(
        body, name=name, grid=(r // tr,), in_specs=[pl.BlockSpec((s, tr, c), lambda i: (0, i, 0))],
        out_specs=pl.BlockSpec((tr, c), lambda i: (i, 0)), out_shape=jax.ShapeDtypeStruct((r, c), F32),
    )(x)


def adamw(name, w, g, m, v):
    r, c = w.shape
    tr = _row_tile(r)
    stacked = g.ndim == 3

    def body(w_ref, g_ref, m_ref, v_ref, go_ref, d_ref, mo_ref, vo_ref):
        gv = _slot_sum(g_ref) if stacked else g_ref[...]
        mn = B1 * m_ref[...] + (1.0 - B1) * gv
        vn = B2 * v_ref[...] + (1.0 - B2) * jnp.square(gv)
        m_hat = mn / (1.0 - B1 ** STEP)
        v_hat = vn / (1.0 - B2 ** STEP)
        go_ref[...] = gv
        d_ref[...] = -LR * (m_hat / (jnp.sqrt(v_hat) + AEPS) + WD * w_ref[...])
        mo_ref[...] = mn
        vo_ref[...] = vn

    tile = pl.BlockSpec((tr, c), lambda i: (i, 0))
    g_spec = pl.BlockSpec((g.shape[0], tr, c), lambda i: (0, i, 0)) if stacked else tile
    return pl.pallas_call(
        body, name=name, grid=(r // tr,), in_specs=[tile, g_spec, tile, tile], out_specs=[tile] * 4,
        out_shape=[jax.ShapeDtypeStruct((r, c), F32)] * 4,
    )(w, g, m, v)


def _place():
    return lax.axis_index("x"), lax.axis_index("y"), lax.axis_index("c")


def all_gather(name, x, in_vmem):
    r, c = x.shape
    space = pltpu.VMEM if in_vmem else pl.ANY

    def body(x_ref, out_ref, send_sems, recv_sems, local_sem):
        px, py, pc = _place()
        me, sibling = (px, py, pc), (px, py, 1 - pc)
        chips = [(1 - px, py), (px, 1 - py), (1 - px, 1 - py)]

        def rows(qx, qy, qc):
            return out_ref.at[pl.ds((4 * qx + 2 * qy + qc) * r, r), :]

        def copy(k, block, to, src=None):
            return pltpu.make_async_remote_copy(
                src_ref=rows(*block) if src is None else src, dst_ref=rows(*block),
                send_sem=send_sems.at[k], recv_sem=recv_sems.at[k], device_id=to, device_id_type=MESH)

        mine = pltpu.make_async_copy(x_ref, rows(*me), local_sem)
        mine.start()
        first = [copy(0, me, sibling, src=x_ref)]
        first += [copy(1 + j, me, (*chip, pc), src=x_ref) for j, chip in enumerate(chips)]
        for cp in first:
            cp.start()
        passed = [copy(4 + j, (*chip, pc), sibling) for j, chip in enumerate(chips)]
        for j, chip in enumerate(chips):
            copy(1 + j, (*chip, pc), me).wait_recv()
            passed[j].start()
        copy(0, sibling, me).wait_recv()
        for j, chip in enumerate(chips):
            copy(4 + j, (*chip, 1 - pc), me).wait_recv()
        for cp in first + passed:
            cp.wait_send()
        mine.wait()

    return pl.pallas_call(
        body, name=name, out_shape=jax.ShapeDtypeStruct((N_DEV * r, c), x.dtype),
        in_specs=[pl.BlockSpec(memory_space=space)], out_specs=pl.BlockSpec(memory_space=space),
        scratch_shapes=[pltpu.SemaphoreType.DMA((7,)), pltpu.SemaphoreType.DMA((7,)), pltpu.SemaphoreType.DMA],
    )(x)


def all_to_all(name, x):
    def body(x_ref, out_ref, send_sems, recv_sems, local_sem):
        px, py, pc = _place()
        me = 4 * px + 2 * py + pc
        mine = pltpu.make_async_copy(x_ref.at[me], out_ref.at[me], local_sem)
        mine.start()
        copies = []
        for k in range(1, N_DEV):
            qx = 1 - px if k & 4 else px
            qy = 1 - py if k & 2 else py
            qc = 1 - pc if k & 1 else pc
            copies.append(pltpu.make_async_remote_copy(
                src_ref=x_ref.at[4 * qx + 2 * qy + qc], dst_ref=out_ref.at[me],
                send_sem=send_sems.at[k - 1], recv_sem=recv_sems.at[k - 1], device_id=(qx, qy, qc), device_id_type=MESH))
        for cp in copies:
            cp.start()
        for cp in copies:
            cp.wait()
        mine.wait()

    return pl.pallas_call(
        body, name=name, out_shape=jax.ShapeDtypeStruct(x.shape, x.dtype),
        in_specs=[pl.BlockSpec(memory_space=pl.ANY)], out_specs=pl.BlockSpec(memory_space=pl.ANY),
        scratch_shapes=[pltpu.SemaphoreType.DMA((7,)), pltpu.SemaphoreType.DMA((7,)), pltpu.SemaphoreType.DMA],
    )(x)


_HBM = pl.BlockSpec(memory_space=pltpu.HBM)
_SEM = pl.BlockSpec(memory_space=pltpu.SEMAPHORE)
_EFFECT = pltpu.SideEffectType.DATAFLOW_SIDE_EFFECTING


def _peers():
    px, py, pc = _place()
    return [(1 - px if k & 4 else px, 1 - py if k & 2 else py, 1 - pc if k & 1 else pc) for k in range(1, N_DEV)]


def _slot(dev):
    return 4 * dev[0] + 2 * dev[1] + dev[2]


def _split_copies(src_ref, land_ref, send_sems, recv_sems, gather):
    me = _slot(_place())
    return [pltpu.make_async_remote_copy(
        src_ref=src_ref if gather else src_ref.at[_slot(peer)], dst_ref=land_ref.at[me],
        send_sem=send_sems.at[k], recv_sem=recv_sems.at[k], device_id=peer, device_id_type=MESH)
        for k, peer in enumerate(_peers())]


def exchange_start(name, src, gather):
    land_shape = (N_DEV,) + src.shape if gather else src.shape

    def body(src_ref, land_ref, send_sems, recv_sems, src_thru, land_thru, token):
        for cp in _split_copies(src_ref, land_ref, send_sems, recv_sems, gather):
            cp.start()
        token[...] = jnp.zeros_like(token)

    return pl.pallas_call(
        body, name=name,
        out_shape=(pltpu.SemaphoreType.DMA((N_DEV - 1,)), pltpu.SemaphoreType.DMA((N_DEV - 1,)),
                   pltpu.HBM(src.shape, src.dtype), pltpu.HBM(land_shape, src.dtype), jax.ShapeDtypeStruct((8, 128), F32)),
        in_specs=(_HBM, _HBM), out_specs=(_SEM, _SEM, _HBM, _HBM, pl.BlockSpec(memory_space=pltpu.VMEM)),
        input_output_aliases={0: 2, 1: 3},
        compiler_params=pltpu.CompilerParams(has_side_effects=_EFFECT),
    )(pltpu.with_memory_space_constraint(src, pltpu.HBM),
      pltpu.with_memory_space_constraint(lax.empty(land_shape, src.dtype), pltpu.HBM))


def exchange_wait(name, started, after, gather):
    send_sems, recv_sems, src_thru, land_thru, _ = started

    def body(src_ref, land_ref, send_sems, recv_sems, after_ref, src_out, land_out):
        for cp in _split_copies(src_ref, land_ref, send_sems, recv_sems, gather):
            cp.wait_send()
            cp.wait_recv()

    return pl.pallas_call(
        body, name=name, out_shape=(pltpu.HBM(src_thru.shape, src_thru.dtype), pltpu.HBM(land_thru.shape, land_thru.dtype)),
        in_specs=(_HBM, _HBM, _SEM, _SEM, pl.BlockSpec(memory_space=pl.ANY)), out_specs=(_HBM, _HBM),
        input_output_aliases={0: 0, 1: 1},
        compiler_params=pltpu.CompilerParams(has_side_effects=_EFFECT),
    )(src_thru, land_thru, send_sems, recv_sems, after)


NCF = FFN_H // FFN_TC


def _size(shape):
    n = 1
    for s in shape:
        n *= s
    return n


def _padded_rows(n_elems, row_mult):
    return -(-n_elems // (D * row_mult)) * row_mult


def _pack_rows(arrs, dtype, row_mult):
    rows, offs, r0 = [], [], 0
    for a in arrs:
        flat = a.reshape(-1).astype(dtype)
        n = _padded_rows(flat.shape[0], row_mult)
        rows.append(jnp.pad(flat, (0, n * D - flat.shape[0])).reshape(n, D))
        offs.append(r0)
        r0 += n
    return jnp.concatenate(rows, 0), offs


def _unpack_rows(buf, offs, shapes):
    lead, out = buf.shape[:-2], []
    for o, shp in zip(offs, shapes):
        n = _size(shp)
        nr = -(-n // D)
        out.append(buf[..., o:o + nr, :].reshape(lead + (nr * D,))[..., :n].reshape(lead + tuple(shp)))
    return out


def _interleave(a):
    return a.reshape(a.shape[:-1] + (2, NCF, FFN_TC)).swapaxes(-2, -3).reshape(a.shape)


def _deinterleave(a):
    return a.reshape(a.shape[:-1] + (NCF, 2, FFN_TC)).swapaxes(-2, -3).reshape(a.shape)


def _cols_from_shards(g):
    return g.transpose(1, 0, 2).reshape(g.shape[1], N_DEV * g.shape[2])


def _cols_to_shards(w):
    k, n = w.shape[0], w.shape[1] // N_DEV
    return w.reshape(k, N_DEV, n).transpose(1, 0, 2)


def _rows3(w):
    return [w[i:i + 1] for i in range(3)]


def f_mod1(xs, ps):
    return f_mod(xs, ps)[:1]


def kernel(x, c, ctx, c_ctx, ada_w, ada_b, norm_mix, norm_ffn, gla_w_in, gla_w_a2, gla_b_a, gla_head_norm, gla_w_out, sc_w_in, sc_conv_w, sc_w_out, ffn_w_up, ffn_conv_w, ffn_conv_b, ffn_w_down, final_norm, loss_target, m_c_ctx, m_ada_w, m_ada_b, m_norm_mix, m_norm_ffn, m_gla_w_in, m_gla_w_a2, m_gla_b_a, m_gla_head_norm, m_gla_w_out, m_sc_w_in, m_sc_conv_w, m_sc_w_out, m_ffn_w_up, m_ffn_conv_w, m_ffn_conv_b, m_ffn_w_down, m_final_norm, v_c_ctx, v_ada_w, v_ada_b, v_norm_mix, v_norm_ffn, v_gla_w_in, v_gla_w_a2, v_gla_b_a, v_gla_head_norm, v_gla_w_out, v_sc_w_in, v_sc_conv_w, v_sc_w_out, v_ffn_w_up, v_ffn_conv_w, v_ffn_conv_b, v_ffn_w_down, v_final_norm):
    names = ["c_ctx", "ada_w", "ada_b", "norm_mix", "norm_ffn", "gla_w_in", "gla_w_a2", "gla_b_a", "gla_head_norm",
             "gla_w_out", "sc_w_in", "sc_conv_w", "sc_w_out", "ffn_w_up", "ffn_conv_w", "ffn_conv_b", "ffn_w_down",
             "final_norm"]
    w_ = dict(zip(names, [c_ctx, ada_w, ada_b, norm_mix, norm_ffn, gla_w_in, gla_w_a2, gla_b_a, gla_head_norm, gla_w_out,
                          sc_w_in, sc_conv_w, sc_w_out, ffn_w_up, ffn_conv_w, ffn_conv_b, ffn_w_down, final_norm]))
    m_ = dict(zip(names, [m_c_ctx, m_ada_w, m_ada_b, m_norm_mix, m_norm_ffn, m_gla_w_in, m_gla_w_a2, m_gla_b_a,
                          m_gla_head_norm, m_gla_w_out, m_sc_w_in, m_sc_conv_w, m_sc_w_out, m_ffn_w_up, m_ffn_conv_w,
                          m_ffn_conv_b, m_ffn_w_down, m_final_norm]))
    v_ = dict(zip(names, [v_c_ctx, v_ada_w, v_ada_b, v_norm_mix, v_norm_ffn, v_gla_w_in, v_gla_w_a2, v_gla_b_a,
                          v_gla_head_norm, v_gla_w_out, v_sc_w_in, v_sc_conv_w, v_sc_w_out, v_ffn_w_up, v_ffn_conv_w,
                          v_ffn_conv_b, v_ffn_w_down, v_final_norm]))
    me = 4 * lax.axis_index("x") + 2 * lax.axis_index("y") + lax.axis_index("c")
    bsz = x.shape[0]
    tm = 256
    nt = SEQ // tm
    ctx_tiles = CTX // tm
    pe = functools.partial(P, per_example=True)

    groups = {"gla": [("gla_w_in", 0), ("gla_w_out", 0)], "ffn0": [("ffn_w_up", 0), ("ffn_w_down", 0)],
              "l1": [("sc_w_in", 0), ("sc_w_out", 0), ("ffn_w_up", 1), ("ffn_w_down", 1)]}

    def part(tree, key):
        return tree[key[0]][key[1]:key[1] + 1]

    ag_started, tok = {}, 0.0
    for g, keys in groups.items():
        wp, offs = _pack_rows([part(w_, k) for k in keys], BF16, 16)
        st = exchange_start(f"ag_{g}_start", wp, True)
        ag_started[g] = (st, offs)
        tok = tok + st[4][0, 0]
    c = c + tok

    small_sharded = [c, gla_w_a2, gla_b_a, sc_conv_w, ffn_conv_w]
    pack0, offs0 = _pack_rows(small_sharded, F32, 8)
    g0 = all_gather("ag_small", pack0, True).reshape(N_DEV, pack0.shape[0], D)
    c_all, wa2_s, ba_s, scw_s, fcw_s = _unpack_rows(g0, offs0, [a.shape for a in small_sharded])
    w_a2 = wa2_s[:, 0].transpose(1, 2, 0, 3).reshape(2, RANK, KD)
    b_a = ba_s[:, 0].transpose(1, 0, 2).reshape(2, KD)
    sc_cw = scw_s[:, 0].transpose(1, 0, 2).reshape(3, D)
    ffn_cw = fcw_s.transpose(1, 2, 0, 3).reshape(2, 3, 2 * FFN_H)

    cond = jnp.concatenate([c_all.reshape(N_DEV * bsz, D), c_ctx[None], jnp.zeros((ADA_ROWS - N_DEV * bsz - 1, D), F32)], 0)
    b_mine = lax.dynamic_slice(ada_b, (0, me * ADA_COLS), (2, ADA_COLS)).reshape(2, 1, ADA_COLS)
    mod_part = ada_fwd(cond, ada_w, b_mine)
    mod = all_gather("ag_mod", mod_part.reshape(2 * ADA_ROWS, ADA_COLS), True)
    mod = mod.reshape(N_DEV, 2, ADA_ROWS, ADA_COLS).transpose(1, 2, 0, 3).reshape(2, ADA_ROWS, 6 * D)
    mods = lax.dynamic_slice(mod, (0, bsz * me, 0), (2, bsz, 6 * D))
    md = [[mods[i][:, k * D:(k + 1) * D].reshape(bsz, 1, D) for k in range(6)] for i in range(2)]
    mc = [mod[0, ADA_CTX_ROW, k * D:(k + 1) * D][None] for k in range(2)]

    def gathered(g, after):
        st, offs = ag_started[g]
        mine, land = exchange_wait(f"ag_{g}_wait", st, after, True)
        land = lax.dynamic_update_index_in_dim(land, mine, me, 0)
        return [s[:, 0] for s in _unpack_rows(land, offs, [part(w_, k).shape for k in groups[g]])]

    w_up, w_down = [None, None], [None, None]
    fcw = [_rows3(_interleave(ffn_cw[i])) for i in range(2)]
    fcb = [_interleave(ffn_conv_b[i])[None] for i in range(2)]
    wd = jnp.zeros((128, 2 * KD), F32).at[:RANK, :KD].set(w_a2[0]).at[RANK:2 * RANK, KD:].set(w_a2[1])
    bd = b_a.reshape(1, 2 * KD)
    scw = _rows3(sc_cw)
    head_gain = gla_head_norm.reshape(1, HV)
    gains_mix = [norm_mix[i][None] for i in range(2)]
    gains_ffn = [norm_ffn[i][None] for i in range(2)]
    ffn_w = 2 * FFN_TC

    def ffn_params(i):
        return [P(a, w=ffn_w, split=2) for a in fcw[i] + [fcb[i]]]

    def ffn_fwd(i, hn2):
        u = mm3(f"ffn_up{i}", hn2, w_up[i], out_dtype=BF16)
        act = rowwise(f"ffn_mid{i}", f_ffn_mid, [X(u, w=ffn_w, split=2)], ffn_params(i), tm=SEQ, nt=1, nc=NCF,
                      outs=[(FFN_TC, BF16, 1)])[0]
        return u, act, mm3(f"ffn_down{i}", act, w_down[i])

    def res_mod_fwd(name, h, y, ps):
        return rowwise(name, f_res_mod, [X(h), X(y)], ps, tm=tm, nt=nt, outs=[(D, F32, 1), (D, BF16, 1)])

    ps_in0 = [P(gains_mix[0]), pe(md[0][0]), pe(md[0][1])]
    ps_ctx = [P(gains_mix[0]), P(mc[0]), P(mc[1])]
    hn0 = rowwise("mod_in0", f_mod, [X(x)], ps_in0, tm=tm, nt=nt, outs=[(D, BF16, 1)])[0]
    hnc = rowwise("mod_ctx", f_mod, [X(ctx)], ps_ctx, tm=tm, nt=ctx_tiles, outs=[(D, BF16, 1)])[0]
    hcat = jnp.concatenate([hnc, hn0], axis=1)
    s_gin, s_gout = gathered("gla", hcat)
    w_gin = jnp.pad(_cols_from_shards(s_gin), ((0, 0), (0, GLA_IN_PAD - GLA_IN)))
    w_gout = s_gout.reshape(VD, D)
    pcat = mm3("gla_in", hcat, w_gin)
    pa_x = X(pcat, w=128, co=(GLA_IN_PAD - 128) // 128)
    la = rowwise("gla_decay", f_decay, [pa_x], [P(wd), P(bd)], tm=tm, nt=TT // tm, outs=[(2 * KD, F32, 1)])[0]
    o2, s_all = gla_fwd(pcat, la)
    post_xs = [X(o2, w=VD, co=0, ro=ctx_tiles, split=HEADS), X(o2, w=VD, co=1, ro=ctx_tiles, split=HEADS),
               X(pcat, w=VD, co=2, ro=ctx_tiles, split=HEADS)]
    yin0 = rowwise("gla_post", f_gla_post, post_xs, [P(head_gain)], tm=tm, nt=nt, outs=[(VD, BF16, HEADS)])[0]
    y0 = mm3("gla_out", yin0, w_gout)
    ps_mid0 = [pe(md[0][2]), P(gains_ffn[0]), pe(md[0][3]), pe(md[0][4])]
    h1_0, hn2_0 = res_mod_fwd("res_mod_mid0", x, y0, ps_mid0)
    s_up0, s_down0 = gathered("ffn0", hn2_0)
    w_up[0], w_down[0] = _interleave(_cols_from_shards(s_up0)), s_down0.reshape(FFN_H, D)
    u0, act0, fo0 = ffn_fwd(0, hn2_0)
    ps_in1 = [pe(md[0][5]), P(gains_mix[1]), pe(md[1][0]), pe(md[1][1])]
    h2_0, hn1 = res_mod_fwd("res_mod_in1", h1_0, fo0, ps_in1)

    s_sin, s_sout, s_up1, s_down1 = gathered("l1", hn1)
    w_sin, w_sout = _cols_from_shards(s_sin), s_sout.reshape(D, D)
    w_up[1], w_down[1] = _interleave(_cols_from_shards(s_up1)), s_down1.reshape(FFN_H, D)
    p1 = mm3("sc_in", hn1, w_sin)
    sc_ps = [P(a) for a in scw]
    yin1 = rowwise("sc_mid", f_sc_mid, [X(p1, split=3)], sc_ps, tm=tm, nt=nt, outs=[(D, BF16, 1)])[0]
    y1 = mm3("sc_out", yin1, w_sout)
    ps_mid1 = [pe(md[1][2]), P(gains_ffn[1]), pe(md[1][3]), pe(md[1][4])]
    h1_1, hn2_1 = res_mod_fwd("res_mod_mid1", h2_0, y1, ps_mid1)
    u1, act1, fo1 = ffn_fwd(1, hn2_1)
    loss8, dh1_1, dfo1, dm5_1, g_final = final_loss(h1_1, fo1, md[1][5], final_norm[None], loss_target)
    loss = lax.psum(loss8[0, 0], ("x", "y", "c"))

    def ffn_bwd(i, u, act, hn2, dfo):
        dact = mm3(f"ffn_down_dx{i}", dfo, w_down[i], tb=True, out_dtype=BF16)
        g_down = mm_tn(f"ffn_down_dw{i}", act, dfo)
        r = rowwise(f"ffn_mid_bwd{i}", f_ffn_mid, [X(u, w=ffn_w, split=2)], ffn_params(i), tm=SEQ, nt=1, nc=NCF,
                    douts=[X(dact, w=FFN_TC)], dx={0: BF16}, dp=[0, 1, 2, 3])
        du, g_cw, g_cb = r[0], jnp.concatenate(r[1:4], 0), r[4]
        dhn2 = mm3(f"ffn_up_dx{i}", du, w_up[i], tb=True, out_dtype=BF16)
        g_up = mm_tn(f"ffn_up_dw{i}", hn2, du)
        return dhn2, _deinterleave(g_up), g_down, _deinterleave(g_cw), _deinterleave(g_cb)

    def res_mod_bwd(name, h, y, ps, dh1, dhn):
        return rowwise(name, f_res_mod, [X(h), X(y)], ps, tm=tm, nt=nt, douts=[X(dh1), X(dhn)],
                       dx={0: F32, 1: BF16}, dp=[0, 1, 2, 3])

    def row_slots(g):
        return g.reshape(N_DEV, -1, D)

    def col_slots(g):
        return _cols_to_shards(g).reshape(N_DEV, -1, D)

    a2a_started = {}

    def send_grads(g, slots):
        slots = [jnp.pad(s, ((0, 0), (0, (-s.shape[1]) % 16), (0, 0))).astype(BF16) for s in slots]
        a2a_started[g] = exchange_start(f"a2a_{g}_start", jnp.concatenate(slots, 1), False)
        return a2a_started[g][4][0, 0]

    def after_start(ps, tok):
        return [dict(ps[0], a=ps[0]["a"] + tok)] + ps[1:]

    dhn2_1, g_up1, g_down1, g_fcw1, g_fcb1 = ffn_bwd(1, u1, act1, hn2_1, dfo1)
    dh2_0, dy1, dm2_1, g_nffn1, dm3_1, dm4_1 = res_mod_bwd("res_mod_mid1_bwd", h2_0, y1, ps_mid1, dh1_1, dhn2_1)
    dyin1 = mm3("sc_out_dx", dy1, w_sout, tb=True, out_dtype=BF16)
    g_sout = mm_tn("sc_out_dw", yin1, dy1)
    r = rowwise("sc_mid_bwd", f_sc_mid, [X(p1, split=3)], sc_ps, tm=tm, nt=nt, douts=[X(dyin1)], dx={0: BF16}, dp=[0, 1, 2])
    dp1, g_scw = r[0], jnp.concatenate(r[1:4], 0)
    dhn1 = mm3("sc_in_dx", dp1, w_sin, tb=True, out_dtype=BF16)
    g_sin = mm_tn("sc_in_dw", hn1, dp1)
    tok = send_grads("l1", [col_slots(g_sin), row_slots(g_sout), col_slots(g_up1), row_slots(g_down1)])
    dh1_0, dfo0, dm5_0, g_nmix1, dm0_1, dm1_1 = res_mod_bwd("res_mod_in1_bwd", h1_0, fo0, after_start(ps_in1, tok), dh2_0, dhn1)

    dhn2_0, g_up0, g_down0, g_fcw0, g_fcb0 = ffn_bwd(0, u0, act0, hn2_0, dfo0)
    tok = send_grads("ffn0", [col_slots(g_up0), row_slots(g_down0)])
    dx_res, dy0, dm2_0, g_nffn0, dm3_0, dm4_0 = res_mod_bwd("res_mod_mid0_bwd", x, y0, after_start(ps_mid0, tok), dh1_0, dhn2_0)
    dyin0 = mm3("gla_out_dx", dy0, w_gout, tb=True, out_dtype=BF16)
    g_gout = mm_tn("gla_out_dw", yin0, dy0)
    do, dgate, g_head = rowwise("gla_post_bwd", f_gla_post, post_xs, [P(head_gain)], tm=tm, nt=nt,
                                douts=[X(dyin0, split=HEADS)], dx={0: F32, 2: BF16}, dp=[0])
    dq2, dk2, dv2, dla = gla_bwd(pcat, la, s_all, do)
    dpa, g_wd, g_bd = rowwise("gla_decay_bwd", f_decay, [pa_x], [P(wd), P(bd)], tm=tm, nt=TT // tm, douts=[X(dla)],
                              dx={0: BF16}, dp=[0, 1])
    dpcat = gla_combine(dq2, dk2, dv2, dgate, dpa)
    dhcat = mm3("gla_in_dx", dpcat, w_gin, tb=True, out_dtype=BF16)
    g_gin = mm_tn("gla_in_dw", hcat, dpcat)[:, :GLA_IN]
    tok = send_grads("gla", [col_slots(g_gin), row_slots(g_gout)])
    grad_x, g_nmix0, dm0_0, dm1_0 = rowwise("mod_in0_bwd", f_mod, [X(x)], after_start(ps_in0, tok), tm=tm, nt=nt,
                                            douts=[X(dhcat, ro=ctx_tiles), X(dx_res)], dx={0: F32}, dp=[0, 1, 2])
    g_nmix0c, dmc0, dmc1 = rowwise("mod_ctx_bwd", f_mod1, [X(ctx)], ps_ctx, tm=tm, nt=ctx_tiles, douts=[X(dhcat)],
                                   dx={}, dp=[0, 1, 2])

    zero_row = jnp.zeros((1, 4 * D), F32)
    dmod = [jnp.concatenate([jnp.concatenate([a.reshape(bsz, D) for a in dms], 1), ctx_row], 0)
            for dms, ctx_row in (([dm0_0, dm1_0, dm2_0, dm3_0, dm4_0, dm5_0], jnp.concatenate([dmc0, dmc1, zero_row], 1)),
                                 ([dm0_1, dm1_1, dm2_1, dm3_1, dm4_1, dm5_1], jnp.zeros((1, 6 * D), F32)))]
    g_wa2 = jnp.stack([g_wd[:RANK, :KD], g_wd[RANK:2 * RANK, KD:]])
    small_grads = [jnp.stack(dmod), jnp.concatenate([g_nmix0 + g_nmix0c, g_nmix1], 0), jnp.concatenate([g_nffn0, g_nffn1], 0),
                   g_head, jnp.concatenate([g_fcb0, g_fcb1], 0), g_final, g_wa2, g_bd.reshape(2, KD), g_scw,
                   jnp.stack([g_fcw0, g_fcw1])]
    pack1, offs1 = _pack_rows(small_grads, F32, 8)
    g1 = all_gather("ag_grads", pack1, True).reshape(N_DEV, pack1.shape[0], D)
    dmod_all = _unpack_rows(g1, offs1[:1], [small_grads[0].shape])[0]
    tot = _unpack_rows(sum_slots("sum_small", g1), offs1, [a.shape for a in small_grads])
    dm_rows = dmod_all[:, :, :bsz].transpose(1, 0, 2, 3).reshape(2, N_DEV * bsz, 6 * D)
    dm_full = jnp.concatenate([dm_rows, tot[0][:, bsz:], jnp.zeros((2, ADA_ROWS - N_DEV * bsz - 1, 6 * D), F32)], 1)
    dm_mine = lax.dynamic_slice(dm_full, (0, 0, me * ADA_COLS), (2, ADA_ROWS, ADA_COLS))
    g_ada_w, g_ada_b, cpart = ada_bwd(cond, dm_mine, dm_full, ada_w)
    cparts = all_gather("ag_cctx", cpart, True).reshape(N_DEV, ADA_ROWS - ADA_CTX_ROW, D)[:, 0]
    g_cctx = cctx_grad(cparts, c_ctx[None])[0]

    def my_cols(full, n):
        return lax.dynamic_slice_in_dim(full, me * n, n, axis=full.ndim - 1)

    grads = {
        "c_ctx": g_cctx, "ada_b": g_ada_b.reshape(2, 6 * D), "norm_mix": tot[1], "norm_ffn": tot[2],
        "gla_head_norm": tot[3], "ffn_conv_b": tot[4], "final_norm": tot[5].reshape(D),
        "gla_w_a2": my_cols(tot[6], KD // N_DEV)[None], "gla_b_a": my_cols(tot[7], KD // N_DEV)[None],
        "sc_conv_w": my_cols(tot[8], D // N_DEV)[None], "ffn_conv_w": my_cols(tot[9], 2 * FFN_H // N_DEV),
    }

    res_ada = adamw("adamw_ada", *[a.reshape(2 * D, ADA_COLS) for a in (ada_w, g_ada_w, m_ada_w, v_ada_w)])
    big = ["gla_w_in", "gla_w_out", "sc_w_in", "sc_w_out", "ffn_w_up", "ffn_w_down"]
    small = [n for n in names if n not in big and n != "ada_w"]
    g_small = _pack_rows([grads[n] for n in small], F32, 8)[0]
    res_small = adamw("adamw_small", _pack_rows([w_[n] for n in small], F32, 8)[0], g_small,
                      _pack_rows([m_[n] for n in small], F32, 8)[0], _pack_rows([v_[n] for n in small], F32, 8)[0])
    offs_s = _pack_rows([w_[n] for n in small], F32, 8)[1]

    big_res, after = {}, res_small[0]
    for g in ("l1", "ffn0", "gla"):
        keys = groups[g]
        sent, land = exchange_wait(f"a2a_{g}_wait", a2a_started[g], after, False)
        land = lax.dynamic_update_index_in_dim(land, lax.dynamic_index_in_dim(sent, me, 0, keepdims=False), me, 0)
        packs = [_pack_rows([part(t, k) for k in keys], F32, 16)[0] for t in (w_, m_, v_)]
        res = adamw(f"adamw_{g}", packs[0], land, packs[1], packs[2])
        shapes = [part(w_, k).shape for k in keys]
        for i, k in enumerate(keys):
            big_res[k] = [_unpack_rows(r, ag_started[g][1], shapes)[i] for r in res]
        after = res[0]

    out = {}
    for kind, idx in (("grad", 0), ("delta", 1), ("new_m", 2), ("new_v", 3)):
        vals = {n: jnp.concatenate([big_res[(n, i)][idx] for i in range(w_[n].shape[0])], 0) for n in big}
        vals["ada_w"] = res_ada[idx].reshape(ada_w.shape)
        vals.update(zip(small, _unpack_rows(res_small[idx], offs_s, [w_[n].shape for n in small])))
        out[kind] = [vals[n] for n in names]
    return (loss, grad_x, *out["grad"], *out["delta"], *out["new_m"], *out["new_v"])
```

```python
import functools

import jax
import jax.numpy as jnp
from jax import lax
from jax.experimental import pallas as pl
from jax.experimental.pallas import tpu as pltpu

F32 = jnp.float32
BF16 = jnp.bfloat16

N_DEV = 8
D = 1024
SEQ = 2048
CTX = 256
TT = CTX + SEQ
GRID_W = 64
CHUNK = 64
HEADS = 4
HK = 128
HV = 256
KD = 512
VD = 1024
RANK = 16
TAU = 16.0
GLA_IN = 3104
GLA_IN_PAD = 3200
FFN_H = 2560
FFN_TC = 256
EPS = 1e-6
LR, B1, B2, AEPS, WD, STEP = 0.001, 0.9, 0.999, 1e-08, 0.01, 10
MESH = pl.DeviceIdType.MESH


def _pick(dim, cap=1024):
    for t in range(cap, 0, -128):
        if dim % t == 0:
            return t
    return dim


def mm(name, a, b, *, ta=False, tb=False, out_dtype=F32):
    K, M = a.shape if ta else a.shape[::-1]
    N = b.shape[0] if tb else b.shape[1]
    assert (b.shape[1] if tb else b.shape[0]) == K, (name, a.shape, b.shape)
    tm, tn, tk = _pick(M), _pick(N), _pick(K)
    nk = K // tk
    dn = (((0 if ta else 1,), (1 if tb else 0,)), ((), ()))

    def body(a_ref, b_ref, o_ref, acc_ref):
        k = pl.program_id(2)
        part = lax.dot_general(a_ref[...].astype(BF16), b_ref[...].astype(BF16), dn, preferred_element_type=F32)

        @pl.when(k == 0)
        def _():
            acc_ref[...] = part

        @pl.when(k > 0)
        def _():
            acc_ref[...] += part

        @pl.when(k == nk - 1)
        def _():
            o_ref[...] = acc_ref[...].astype(out_dtype)

    a_spec = pl.BlockSpec((tk, tm), lambda i, j, k: (k, i)) if ta else pl.BlockSpec((tm, tk), lambda i, j, k: (i, k))
    b_spec = pl.BlockSpec((tn, tk), lambda i, j, k: (j, k)) if tb else pl.BlockSpec((tk, tn), lambda i, j, k: (k, j))
    return pl.pallas_call(
        body, name=name, grid=(M // tm, N // tn, nk),
        in_specs=[a_spec, b_spec], out_specs=pl.BlockSpec((tm, tn), lambda i, j, k: (i, j)),
        out_shape=jax.ShapeDtypeStruct((M, N), out_dtype),
        scratch_shapes=[pltpu.VMEM((tm, tn), F32)],
        compiler_params=pltpu.CompilerParams(dimension_semantics=("parallel", "parallel", "arbitrary")),
    )(a, b)


def mm3(name, a, b, **kw):
    bsz, t, k = a.shape
    return mm(name, a.reshape(bsz * t, k), b, **kw).reshape(bsz, t, -1)


def mm_tn(name, a, b):
    return mm(name, a.reshape(-1, a.shape[-1]), b.reshape(-1, b.shape[-1]), ta=True)


def X(arr, w=None, co=0, ro=0, split=1):
    return dict(a=arr, w=arr.shape[-1] if w is None else w, co=co, ro=ro, split=split)


def P(arr, per_example=False, w=None, split=1):
    return dict(a=arr, e=per_example, w=arr.shape[-1] if w is None else w, split=split)


def _pieces(ref, split):
    w = ref.shape[-1] // split
    return [ref[:, i * w:(i + 1) * w] for i in range(split)]


def _store(ref, pieces, accumulate=False):
    w = ref.shape[-1] // len(pieces)
    for i, p in enumerate(pieces):
        if accumulate:
            ref[:, i * w:(i + 1) * w] += p.astype(ref.dtype)
        else:
            ref[:, i * w:(i + 1) * w] = p.astype(ref.dtype)


def rowwise(name, f, xs, ps, *, tm, nt, nc=1, outs=None, douts=None, dx=None, dp=None):
    bsz = xs[0]["a"].shape[0]
    fwd = douts is None
    nx, np_ = len(xs), len(ps)
    douts = [] if fwd else douts
    dx = {} if fwd else dx
    dp = [] if fwd else dp

    def x_spec(s):
        return pl.BlockSpec((None, tm, s["w"]), lambda c, b, t, s=s: (b, t + s["ro"], c + s["co"]))

    def p_spec(s):
        r = s["a"].shape[-2]
        if s["e"]:
            return pl.BlockSpec((None, r, s["w"]), lambda c, b, t: (b, 0, c))
        return pl.BlockSpec((r, s["w"]), lambda c, b, t: (0, c))

    in_specs = [x_spec(s) for s in xs] + [p_spec(s) for s in ps] + [x_spec(s) for s in douts]
    operands = [s["a"] for s in xs] + [s["a"] for s in ps] + [s["a"] for s in douts]
    if fwd:
        out_shape = [jax.ShapeDtypeStruct((bsz, nt * tm, nc * w), dt) for (w, dt, _) in outs]
        out_specs = [pl.BlockSpec((None, tm, w), lambda c, b, t: (b, t, c)) for (w, _, _) in outs]
    else:
        out_shape = [jax.ShapeDtypeStruct((bsz, nt * tm, nc * xs[i]["w"]), dt) for i, dt in dx.items()]
        out_specs = [pl.BlockSpec((None, tm, xs[i]["w"]), lambda c, b, t: (b, t, c)) for i in dx]
        for j in dp:
            s = ps[j]
            r = s["a"].shape[-2]
            if s["e"]:
                out_shape.append(jax.ShapeDtypeStruct((bsz, r, nc * s["w"]), F32))
                out_specs.append(pl.BlockSpec((None, r, s["w"]), lambda c, b, t: (b, 0, c)))
            else:
                out_shape.append(jax.ShapeDtypeStruct((r, nc * s["w"]), F32))
                out_specs.append(pl.BlockSpec((r, s["w"]), lambda c, b, t: (0, c)))

    def body(*refs):
        x_refs, p_refs = refs[:nx], refs[nx:nx + np_]
        d_refs = refs[nx + np_:nx + np_ + len(douts)]
        o_refs = refs[nx + np_ + len(douts):]
        xv = [[p.astype(F32) for p in _pieces(r, s["split"])] for r, s in zip(x_refs, xs)]
        pv = [[p.astype(F32) for p in _pieces(r, s["split"])] for r, s in zip(p_refs, ps)]
        if fwd:
            for r, pieces in zip(o_refs, f(xv, pv)):
                _store(r, pieces)
            return
        _, vjp = jax.vjp(f, xv, pv)
        cot = [[p.astype(F32) for p in _pieces(r, s["split"])] for r, s in zip(d_refs, douts)]
        dxv, dpv = vjp(cot)
        for r, i in zip(o_refs, dx):
            _store(r, dxv[i])
        b, t = pl.program_id(1), pl.program_id(2)
        for r, j in zip(o_refs[len(dx):], dp):
            first = (t == 0) if ps[j]["e"] else jnp.logical_and(b == 0, t == 0)

            @pl.when(first)
            def _(r=r, j=j):
                _store(r, dpv[j])

            @pl.when(jnp.logical_not(first))
            def _(r=r, j=j):
                _store(r, dpv[j], accumulate=True)

    res = pl.pallas_call(
        body, name=name, grid=(nc, bsz, nt), in_specs=in_specs, out_specs=out_specs, out_shape=out_shape,
        compiler_params=pltpu.CompilerParams(dimension_semantics=("arbitrary", "arbitrary", "arbitrary")),
    )(*operands)
    return res


def _keep_rows(a, shift, keep):
    n = a.shape[0]
    t = lax.broadcasted_iota(jnp.int32, a.shape, 0)
    return jnp.where(keep(t, n), pltpu.roll(a, shift % n, 0), 0.0)


def _shift_pair(step, keep_prev, keep_next):
    @jax.custom_vjp
    def prev(a):
        return _keep_rows(a, step, keep_prev)

    @jax.custom_vjp
    def nxt(a):
        return _keep_rows(a, -step, keep_next)

    prev.defvjp(lambda a: (prev(a), None), lambda _, g: (nxt(g),))
    nxt.defvjp(lambda a: (nxt(a), None), lambda _, g: (prev(g),))
    return prev, nxt


prev_tok, next_tok = _shift_pair(1, lambda t, n: t % GRID_W != 0, lambda t, n: t % GRID_W != GRID_W - 1)
prev_row, next_row = _shift_pair(GRID_W, lambda t, n: t >= GRID_W, lambda t, n: t < n - GRID_W)


@jax.custom_vjp
def bdot(a, w):
    return jnp.dot(a.astype(BF16), w.astype(BF16), preferred_element_type=F32)


def _bdot_bwd(res, g):
    a, w = res
    gb = g.astype(BF16)
    da = lax.dot_general(gb, w.astype(BF16), (((1,), (1,)), ((), ())), preferred_element_type=F32)
    dw = lax.dot_general(a.astype(BF16), gb, (((0,), (0,)), ((), ())), preferred_element_type=F32)
    return da, dw


bdot.defvjp(lambda a, w: (bdot(a, w), (a, w)), _bdot_bwd)


@jax.custom_vjp
def log_sigmoid(z):
    return jnp.minimum(z, 0.0) - jnp.log(1.0 + jnp.exp(-jnp.abs(z)))


def _lsig_bwd(z, g):
    e = jnp.exp(-jnp.abs(z))
    return (g * jnp.where(z >= 0, e, 1.0) / (1.0 + e),)


log_sigmoid.defvjp(lambda z: (log_sigmoid(z), z), _lsig_bwd)


def silu(x):
    return x * jax.nn.sigmoid(x)


def _rms(x):
    return x * lax.rsqrt(jnp.mean(x * x, axis=-1, keepdims=True) + EPS)


def _mod(x, gain, shift, scale):
    return _rms(x) * gain * (1.0 + scale) + shift


def f_mod(xs, ps):
    ((h,),), ((gain,), (shift,), (scale,)) = xs, ps
    return [[_mod(h, gain, shift, scale)], [h]]


def f_res_mod(xs, ps):
    ((h,), (y,)), ((gate,), (gain,), (shift,), (scale,)) = xs, ps
    h1 = h + gate * y
    return [[h1], [_mod(h1, gain, shift, scale)]]


def f_ffn_mid(xs, ps):
    ((ua, ug),), ((w0a, w0g), (w1a, w1g), (w2a, w2g), (ba, bg)) = xs, ps
    a = w0a * prev_row(ua) + w1a * ua + w2a * next_row(ua) + ba
    g = w0g * prev_row(ug) + w1g * ug + w2g * next_row(ug) + bg
    return [[a * silu(g)]]


def f_sc_mid(xs, ps):
    ((bg, cg, v),), ((w0,), (w1,), (w2,)) = xs, ps
    z = cg * v
    return [[bg * (w0 * prev_tok(z) + w1 * z + w2 * next_tok(z))]]


def f_decay(xs, ps):
    ((a,),), ((wd,), (bd,)) = xs, ps
    return [[log_sigmoid(bdot(a, wd) + bd) / TAU]]


def f_gla_post(xs, ps):
    (of, ob, g), ((gain,),) = xs, ps
    return [[_rms(a + b) * gain * silu(c) for a, b, c in zip(of, ob, g)]]


def f_sum_pairs(xs, ps):
    return [[a + b for a, b in zip(xs[0], xs[1])]]


NCH = TT // CHUNK
CTX_CH = CTX // CHUNK
_NT = (((1,), (1,)), ((), ()))
_TN = (((0,), (0,)), ((), ()))
_NN = (((1,), (0,)), ((), ()))


def _chunk_of(d, j):
    return jnp.where(d == 0, j, jnp.where(j < CTX_CH, CTX_CH - 1 - j, NCH + CTX_CH - 1 - j))


def _dot(a, b, dn):
    return lax.dot_general(a, b, dn, preferred_element_type=F32)


def _mask_dot(m, g):
    g0 = g.astype(BF16)
    r1 = g - g0.astype(F32)
    g1 = r1.astype(BF16)
    g2 = (r1 - g1.astype(F32)).astype(BF16)
    return _dot(m, g0, _NN) + _dot(m, g1, _NN) + _dot(m, g2, _NN)


def _causal(d):
    row = lax.broadcasted_iota(jnp.int32, (CHUNK, CHUNK), 0)
    col = lax.broadcasted_iota(jnp.int32, (CHUNK, CHUNK), 1)
    delta = jnp.where(d == 0, col - row, row - col)
    return delta <= 0, delta >= 0


def _gla_in_specs(rev):
    def blk(d, j):
        return _chunk_of(d, (NCH - 1 - j) if rev else j)

    return [
        pl.BlockSpec((None, CHUNK, KD), lambda b, d, j: (b, blk(d, j), 0)),
        pl.BlockSpec((None, CHUNK, KD), lambda b, d, j: (b, blk(d, j), 1)),
        pl.BlockSpec((None, CHUNK, VD), lambda b, d, j: (b, blk(d, j), 1)),
        pl.BlockSpec((None, CHUNK, KD), lambda b, d, j: (b, blk(d, j), d)),
    ], blk


def gla_fwd(pcat, la):
    bsz = pcat.shape[0]
    in_specs, blk = _gla_in_specs(False)

    def body(q_ref, k_ref, v_ref, la_ref, o_ref, s_ref, st):
        d, j = pl.program_id(1), pl.program_id(2)

        @pl.when(j == 0)
        def _():
            st[...] = jnp.zeros_like(st)

        s_ref[...] = st[...]
        causal, _ = _causal(d)
        mf = causal.astype(BF16)
        for h in range(HEADS):
            ks_, vs_ = slice(h * HK, (h + 1) * HK), slice(h * HV, (h + 1) * HV)
            q, k, v, g = q_ref[:, ks_] * (HK ** -0.5), k_ref[:, ks_], v_ref[:, vs_].astype(BF16), la_ref[:, ks_]
            b = _mask_dot(mf, g)
            bl = jnp.sum(g, axis=0, keepdims=True)
            qs = (q * jnp.exp(b)).astype(BF16)
            ks = (k * jnp.exp(-b)).astype(BF16)
            kd = (k * jnp.exp(bl - b)).astype(BF16)
            s = st[h]
            att = jnp.where(causal, _dot(qs, ks, _NT), 0.0).astype(BF16)
            o_ref[:, vs_] = _dot(qs, s.astype(BF16), _NT) + _dot(att, v, _NN)
            st[h] = jnp.exp(bl) * s + _dot(v, kd, _TN)

    return pl.pallas_call(
        body, name="gla_fwd", grid=(bsz, 2, NCH), in_specs=in_specs,
        out_specs=[pl.BlockSpec((None, CHUNK, VD), lambda b, d, j: (b, blk(d, j), d)),
                   pl.BlockSpec((None, None, None, HEADS, HV, HK), lambda b, d, j: (b, d, j, 0, 0, 0))],
        out_shape=[jax.ShapeDtypeStruct((bsz, TT, 2 * VD), F32), jax.ShapeDtypeStruct((bsz, 2, NCH, HEADS, HV, HK), F32)],
        scratch_shapes=[pltpu.VMEM((HEADS, HV, HK), F32)],
        compiler_params=pltpu.CompilerParams(dimension_semantics=("arbitrary", "arbitrary", "arbitrary")),
    )(pcat, pcat, pcat, la)


def gla_bwd(pcat, la, s_all, do):
    bsz = pcat.shape[0]
    in_specs, blk = _gla_in_specs(True)
    in_specs += [
        pl.BlockSpec((None, None, None, HEADS, HV, HK), lambda b, d, j: (b, d, NCH - 1 - j, 0, 0, 0)),
        pl.BlockSpec((None, CHUNK, VD), lambda b, d, j: (b, jnp.maximum(blk(d, j) - CTX_CH, 0), 0)),
    ]

    def body(q_ref, k_ref, v_ref, la_ref, s_ref, do_ref, dq_ref, dk_ref, dv_ref, dla_ref, dst):
        d, j = pl.program_id(1), pl.program_id(2)

        @pl.when(j == 0)
        def _():
            dst[...] = jnp.zeros_like(dst)

        latent = blk(d, j) >= CTX_CH
        causal, causal_t = _causal(d)
        mt = causal_t.astype(BF16)
        mf = causal.astype(BF16)
        scale = HK ** -0.5
        for h in range(HEADS):
            ks_, vs_ = slice(h * HK, (h + 1) * HK), slice(h * HV, (h + 1) * HV)
            q, k, v, g = q_ref[:, ks_] * scale, k_ref[:, ks_], v_ref[:, vs_].astype(BF16), la_ref[:, ks_]
            b = _mask_dot(mf, g)
            bl = jnp.sum(g, axis=0, keepdims=True)
            e, ei, ed, el = jnp.exp(b), jnp.exp(-b), jnp.exp(bl - b), jnp.exp(bl)
            qs, ks, kd = q * e, k * ei, k * ed
            qsb, ksb, kdb = qs.astype(BF16), ks.astype(BF16), kd.astype(BF16)
            s, ds1 = s_ref[h], dst[h]
            sb, ds1b = s.astype(BF16), ds1.astype(BF16)
            dob = jnp.where(latent, do_ref[:, vs_], 0.0).astype(BF16)
            att = jnp.where(causal, _dot(qsb, ksb, _NT), 0.0).astype(BF16)
            datt = jnp.where(causal, _dot(dob, v, _NT), 0.0).astype(BF16)
            dqs = _dot(dob, sb, _NN) + _dot(datt, ksb, _NN)
            dks = _dot(datt, qsb, _TN)
            dv_ref[:, vs_] = _dot(att, dob, _TN) + _dot(kdb, ds1b, _NT)
            dkd = _dot(v, ds1b, _NN)
            dst[h] = _dot(dob, qsb, _TN) + el * ds1
            del_ = jnp.sum(s * ds1, axis=0, keepdims=True)
            dq_ref[:, ks_] = dqs * e * scale
            dk_ref[:, ks_] = dks * ei + dkd * ed
            db = dqs * qs - dks * ks - dkd * kd
            dbl = jnp.sum(dkd * kd, axis=0, keepdims=True) + del_ * el
            dla_ref[:, ks_] = _mask_dot(mt, db) + dbl

    return pl.pallas_call(
        body, name="gla_bwd", grid=(bsz, 2, NCH), in_specs=in_specs,
        out_specs=[pl.BlockSpec((None, None, CHUNK, KD), lambda b, d, j: (d, b, blk(d, j), 0)),
                   pl.BlockSpec((None, None, CHUNK, KD), lambda b, d, j: (d, b, blk(d, j), 0)),
                   pl.BlockSpec((None, None, CHUNK, VD), lambda b, d, j: (d, b, blk(d, j), 0)),
                   pl.BlockSpec((None, CHUNK, KD), lambda b, d, j: (b, blk(d, j), d))],
        out_shape=[jax.ShapeDtypeStruct((2, bsz, TT, KD), F32), jax.ShapeDtypeStruct((2, bsz, TT, KD), F32),
                   jax.ShapeDtypeStruct((2, bsz, TT, VD), F32), jax.ShapeDtypeStruct((bsz, TT, 2 * KD), F32)],
        scratch_shapes=[pltpu.VMEM((HEADS, HV, HK), F32)],
        compiler_params=pltpu.CompilerParams(dimension_semantics=("arbitrary", "arbitrary", "arbitrary")),
    )(pcat, pcat, pcat, la, s_all, do)


def gla_combine(dq2, dk2, dv2, dgate, dpa):
    bsz = dgate.shape[0]
    tm = CTX

    def body(dq_ref, dk_ref, dv_ref, dg_ref, dpa_ref, o_ref):
        t = pl.program_id(1)
        o_ref[:, 0:KD] = (dq_ref[0] + dq_ref[1]).astype(BF16)
        o_ref[:, KD:2 * KD] = (dk_ref[0] + dk_ref[1]).astype(BF16)
        o_ref[:, 2 * KD:2 * KD + VD] = (dv_ref[0] + dv_ref[1]).astype(BF16)
        o_ref[:, 2 * KD + VD:2 * KD + 2 * VD] = jnp.where(t > 0, dg_ref[...], 0).astype(BF16)
        o_ref[:, 2 * KD + 2 * VD:] = dpa_ref[...].astype(BF16)

    return pl.pallas_call(
        body, name="gla_combine", grid=(bsz, TT // tm),
        in_specs=[pl.BlockSpec((2, None, tm, KD), lambda b, t: (0, b, t, 0)),
                  pl.BlockSpec((2, None, tm, KD), lambda b, t: (0, b, t, 0)),
                  pl.BlockSpec((2, None, tm, VD), lambda b, t: (0, b, t, 0)),
                  pl.BlockSpec((None, tm, VD), lambda b, t: (b, jnp.maximum(t - 1, 0), 0)),
                  pl.BlockSpec((None, tm, 128), lambda b, t: (b, t, 0))],
        out_specs=pl.BlockSpec((None, tm, GLA_IN_PAD), lambda b, t: (b, t, 0)),
        out_shape=jax.ShapeDtypeStruct((bsz, TT, GLA_IN_PAD), BF16),
        compiler_params=pltpu.CompilerParams(dimension_semantics=("arbitrary", "arbitrary")),
    )(dq2, dk2, dv2, dgate, dpa)


def final_loss(h1, fo, gate, gain, tgt):
    bsz, t_len, _ = h1.shape
    tm = 256

    def body(h_ref, f_ref, gate_ref, gain_ref, tgt_ref, loss_ref, dh_ref, df_ref, dgate_ref, dgain_ref):
        b, t = pl.program_id(0), pl.program_id(1)
        target = tgt_ref[...]

        def core(h, fo_, gate_, gain_):
            e = _rms(h + gate_ * fo_) * gain_ - target
            return jnp.sum(0.5 * jnp.sum(e * e, axis=-1, keepdims=True) / D, axis=0, keepdims=True)

        loss, vjp = jax.vjp(core, h_ref[...], f_ref[...], gate_ref[...], gain_ref[...])
        dh, df, dgate, dgain = vjp(jnp.ones((1, 1), F32))
        dh_ref[...] = dh
        df_ref[...] = df.astype(BF16)
        first = jnp.logical_and(b == 0, t == 0)

        @pl.when(first)
        def _():
            loss_ref[...] = jnp.broadcast_to(loss, loss_ref.shape)
            dgain_ref[...] = dgain

        @pl.when(jnp.logical_not(first))
        def _():
            loss_ref[...] += jnp.broadcast_to(loss, loss_ref.shape)
            dgain_ref[...] += dgain

        @pl.when(t == 0)
        def _():
            dgate_ref[...] = dgate

        @pl.when(t > 0)
        def _():
            dgate_ref[...] += dgate

    tile = pl.BlockSpec((None, tm, D), lambda b, t: (b, t, 0))
    per_ex = pl.BlockSpec((None, 1, D), lambda b, t: (b, 0, 0))
    shared = pl.BlockSpec((1, D), lambda b, t: (0, 0))
    return pl.pallas_call(
        body, name="final_loss", grid=(bsz, t_len // tm),
        in_specs=[tile, tile, per_ex, shared, tile],
        out_specs=[pl.BlockSpec((8, 128), lambda b, t: (0, 0)), tile, tile, per_ex, shared],
        out_shape=[jax.ShapeDtypeStruct((8, 128), F32), jax.ShapeDtypeStruct(h1.shape, F32),
                   jax.ShapeDtypeStruct(h1.shape, BF16), jax.ShapeDtypeStruct((bsz, 1, D), F32),
                   jax.ShapeDtypeStruct((1, D), F32)],
        compiler_params=pltpu.CompilerParams(dimension_semantics=("arbitrary", "arbitrary")),
    )(h1, fo, gate, gain, tgt)


ADA_ROWS = 24
ADA_CTX_ROW = 16
ADA_COLS = 6 * D // N_DEV


def ada_fwd(cond, w, b):
    def body(c_ref, w_ref, b_ref, o_ref):
        s = silu(c_ref[...]).astype(BF16)
        o_ref[...] = jnp.dot(s, w_ref[...].astype(BF16), preferred_element_type=F32) + b_ref[...]

    return pl.pallas_call(
        body, name="ada_fwd", grid=(2,),
        in_specs=[pl.BlockSpec((ADA_ROWS, D), lambda i: (0, 0)), pl.BlockSpec((None, D, ADA_COLS), lambda i: (i, 0, 0)),
                  pl.BlockSpec((None, 1, ADA_COLS), lambda i: (i, 0, 0))],
        out_specs=pl.BlockSpec((None, ADA_ROWS, ADA_COLS), lambda i: (i, 0, 0)),
        out_shape=jax.ShapeDtypeStruct((2, ADA_ROWS, ADA_COLS), F32),
    )(cond, w, b)


def ada_bwd(cond, dm_mine, dm_full, w):
    def body(c_ref, dm_ref, dmf_ref, w_ref, gw_ref, gb_ref, cp_ref):
        i = pl.program_id(0)
        s = silu(c_ref[...]).astype(BF16)
        dm = dm_ref[...].astype(BF16)
        gw_ref[...] = _dot(s, dm, _TN)
        gb_ref[...] = jnp.sum(dmf_ref[...], axis=0, keepdims=True)

        @pl.when(i == 0)
        def _():
            cp_ref[...] = _dot(dm_ref[ADA_CTX_ROW:, :].astype(BF16), w_ref[...].astype(BF16), _NT)

    return pl.pallas_call(
        body, name="ada_bwd", grid=(2,),
        in_specs=[pl.BlockSpec((ADA_ROWS, D), lambda i: (0, 0)), pl.BlockSpec((None, ADA_ROWS, ADA_COLS), lambda i: (i, 0, 0)),
                  pl.BlockSpec((None, ADA_ROWS, 6 * D), lambda i: (i, 0, 0)), pl.BlockSpec((None, D, ADA_COLS), lambda i: (i, 0, 0))],
        out_specs=[pl.BlockSpec((None, D, ADA_COLS), lambda i: (i, 0, 0)), pl.BlockSpec((None, 1, 6 * D), lambda i: (i, 0, 0)),
                   pl.BlockSpec((ADA_ROWS - ADA_CTX_ROW, D), lambda i: (0, 0))],
        out_shape=[jax.ShapeDtypeStruct((2, D, ADA_COLS), F32), jax.ShapeDtypeStruct((2, 1, 6 * D), F32),
                   jax.ShapeDtypeStruct((ADA_ROWS - ADA_CTX_ROW, D), F32)],
        compiler_params=pltpu.CompilerParams(dimension_semantics=("arbitrary",)),
    )(cond, dm_mine, dm_full, w)


def cctx_grad(parts, c_ctx):
    def body(p_ref, c_ref, o_ref):
        tot = p_ref[0:1, :]
        for i in range(1, N_DEV):
            tot = tot + p_ref[i:i + 1, :]
        c = c_ref[...]
        sg = jax.nn.sigmoid(c)
        o_ref[...] = tot * sg * (1.0 + c * (1.0 - sg))

    return pl.pallas_call(body, name="cctx_grad", out_shape=jax.ShapeDtypeStruct((1, D), F32))(parts, c_ctx)


def _row_tile(r):
    for t in (512, 256, 128, 80, 64, 40, 32, 16, 8):
        if r % t == 0:
            return t
    return r


def _slot_sum(ref):
    tot = ref[0].astype(F32)
    for i in range(1, ref.shape[0]):
        tot = tot + ref[i].astype(F32)
    return tot


def sum_slots(name, x):
    s, r, c = x.shape
    tr = _row_tile(r)

    def body(x_ref, o_ref):
        o_ref[...] = _slot_sum(x_ref)

    return pl.pallas_call(
        body, name=name, grid=(r // tr,), in_specs=[pl.BlockSpec((s, tr, c), lambda i: (0, i, 0))],
        out_specs=pl.BlockSpec((tr, c), lambda i: (i, 0)), out_shape=jax.ShapeDtypeStruct((r, c), F32),
    )(x)


def adamw(name, w, g, m, v):
    r, c = w.shape
    tr = _row_tile(r)
    stacked = g.ndim == 3

    def body(w_ref, g_ref, m_ref, v_ref, go_ref, d_ref, mo_ref, vo_ref):
        gv = _slot_sum(g_ref) if stacked else g_ref[...]
        mn = B1 * m_ref[...] + (1.0 - B1) * gv
        vn = B2 * v_ref[...] + (1.0 - B2) * jnp.square(gv)
        m_hat = mn / (1.0 - B1 ** STEP)
        v_hat = vn / (1.0 - B2 ** STEP)
        go_ref[...] = gv
        d_ref[...] = -LR * (m_hat / (jnp.sqrt(v_hat) + AEPS) + WD * w_ref[...])
        mo_ref[...] = mn
        vo_ref[...] = vn

    tile = pl.BlockSpec((tr, c), lambda i: (i, 0))
    g_spec = pl.BlockSpec((g.shape[0], tr, c), lambda i: (0, i, 0)) if stacked else tile
    return pl.pallas_call(
        body, name=name, grid=(r // tr,), in_specs=[tile, g_spec, tile, tile], out_specs=[tile] * 4,
        out_shape=[jax.ShapeDtypeStruct((r, c), F32)] * 4,
    )(w, g, m, v)


def _place():
    return lax.axis_index("x"), lax.axis_index("y"), lax.axis_index("c")


def all_gather(name, x, in_vmem):
    r, c = x.shape
    space = pltpu.VMEM if in_vmem else pl.ANY

    def body(x_ref, out_ref, send_sems, recv_sems, local_sem):
        px, py, pc = _place()
        me, sibling = (px, py, pc), (px, py, 1 - pc)
        chips = [(1 - px, py), (px, 1 - py), (1 - px, 1 - py)]

        def rows(qx, qy, qc):
            return out_ref.at[pl.ds((4 * qx + 2 * qy + qc) * r, r), :]

        def copy(k, block, to, src=None):
            return pltpu.make_async_remote_copy(
                src_ref=rows(*block) if src is None else src, dst_ref=rows(*block),
                send_sem=send_sems.at[k], recv_sem=recv_sems.at[k], device_id=to, device_id_type=MESH)

        mine = pltpu.make_async_copy(x_ref, rows(*me), local_sem)
        mine.start()
        first = [copy(0, me, sibling, src=x_ref)]
        first += [copy(1 + j, me, (*chip, pc), src=x_ref) for j, chip in enumerate(chips)]
        for cp in first:
            cp.start()
        passed = [copy(4 + j, (*chip, pc), sibling) for j, chip in enumerate(chips)]
        for j, chip in enumerate(chips):
            copy(1 + j, (*chip, pc), me).wait_recv()
            passed[j].start()
        copy(0, sibling, me).wait_recv()
        for j, chip in enumerate(chips):
            copy(4 + j, (*chip, 1 - pc), me).wait_recv()
        for cp in first + passed:
            cp.wait_send()
        mine.wait()

    return pl.pallas_call(
        body, name=name, out_shape=jax.ShapeDtypeStruct((N_DEV * r, c), x.dtype),
        in_specs=[pl.BlockSpec(memory_space=space)], out_specs=pl.BlockSpec(memory_space=space),
        scratch_shapes=[pltpu.SemaphoreType.DMA((7,)), pltpu.SemaphoreType.DMA((7,)), pltpu.SemaphoreType.DMA],
    )(x)


def all_to_all(name, x):
    def body(x_ref, out_ref, send_sems, recv_sems, local_sem):
        px, py, pc = _place()
        me = 4 * px + 2 * py + pc
        mine = pltpu.make_async_copy(x_ref.at[me], out_ref.at[me], local_sem)
        mine.start()
        copies = []
        for k in range(1, N_DEV):
            qx = 1 - px if k & 4 else px
            qy = 1 - py if k & 2 else py
            qc = 1 - pc if k & 1 else pc
            copies.append(pltpu.make_async_remote_copy(
                src_ref=x_ref.at[4 * qx + 2 * qy + qc], dst_ref=out_ref.at[me],
                send_sem=send_sems.at[k - 1], recv_sem=recv_sems.at[k - 1], device_id=(qx, qy, qc), device_id_type=MESH))
        for cp in copies:
            cp.start()
        for cp in copies:
            cp.wait()
        mine.wait()

    return pl.pallas_call(
        body, name=name, out_shape=jax.ShapeDtypeStruct(x.shape, x.dtype),
        in_specs=[pl.BlockSpec(memory_space=pl.ANY)], out_specs=pl.BlockSpec(memory_space=pl.ANY),
        scratch_shapes=[pltpu.SemaphoreType.DMA((7,)), pltpu.SemaphoreType.DMA((7,)), pltpu.SemaphoreType.DMA],
    )(x)


_HBM = pl.BlockSpec(memory_space=pltpu.HBM)
_SEM = pl.BlockSpec(memory_space=pltpu.SEMAPHORE)
_EFFECT = pltpu.SideEffectType.DATAFLOW_SIDE_EFFECTING


def _peers():
    px, py, pc = _place()
    return [(1 - px if k & 4 else px, 1 - py if k & 2 else py, 1 - pc if k & 1 else pc) for k in range(1, N_DEV)]


def _slot(dev):
    return 4 * dev[0] + 2 * dev[1] + dev[2]


def _split_copies(src_ref, land_ref, send_sems, recv_sems, gather):
    me = _slot(_place())
    return [pltpu.make_async_remote_copy(
        src_ref=src_ref if gather else src_ref.at[_slot(peer)], dst_ref=land_ref.at[me],
        send_sem=send_sems.at[k], recv_sem=recv_sems.at[k], device_id=peer, device_id_type=MESH)
        for k, peer in enumerate(_peers())]


def exchange_start(name, src, gather):
    land_shape = (N_DEV,) + src.shape if gather else src.shape

    def body(src_ref, land_ref, send_sems, recv_sems, src_thru, land_thru, token):
        for cp in _split_copies(src_ref, land_ref, send_sems, recv_sems, gather):
            cp.start()
        token[...] = jnp.zeros_like(token)

    return pl.pallas_call(
        body, name=name,
        out_shape=(pltpu.SemaphoreType.DMA((N_DEV - 1,)), pltpu.SemaphoreType.DMA((N_DEV - 1,)),
                   pltpu.HBM(src.shape, src.dtype), pltpu.HBM(land_shape, src.dtype), jax.ShapeDtypeStruct((8, 128), F32)),
        in_specs=(_HBM, _HBM), out_specs=(_SEM, _SEM, _HBM, _HBM, pl.BlockSpec(memory_space=pltpu.VMEM)),
        input_output_aliases={0: 2, 1: 3},
        compiler_params=pltpu.CompilerParams(has_side_effects=_EFFECT),
    )(pltpu.with_memory_space_constraint(src, pltpu.HBM),
      pltpu.with_memory_space_constraint(lax.empty(land_shape, src.dtype), pltpu.HBM))


def exchange_wait(name, started, after, gather):
    send_sems, recv_sems, src_thru, land_thru, _ = started

    def body(src_ref, land_ref, send_sems, recv_sems, after_ref, src_out, land_out):
        for cp in _split_copies(src_ref, land_ref, send_sems, recv_sems, gather):
            cp.wait_send()
            cp.wait_recv()

    return pl.pallas_call(
        body, name=name, out_shape=(pltpu.HBM(src_thru.shape, src_thru.dtype), pltpu.HBM(land_thru.shape, land_thru.dtype)),
        in_specs=(_HBM, _HBM, _SEM, _SEM, pl.BlockSpec(memory_space=pl.ANY)), out_specs=(_HBM, _HBM),
        input_output_aliases={0: 0, 1: 1},
        compiler_params=pltpu.CompilerParams(has_side_effects=_EFFECT),
    )(src_thru, land_thru, send_sems, recv_sems, after)


NCF = FFN_H // FFN_TC


def _size(shape):
    n = 1
    for s in shape:
        n *= s
    return n


def _padded_rows(n_elems, row_mult):
    return -(-n_elems // (D * row_mult)) * row_mult


def _pack_rows(arrs, dtype, row_mult):
    rows, offs, r0 = [], [], 0
    for a in arrs:
        flat = a.reshape(-1).astype(dtype)
        n = _padded_rows(flat.shape[0], row_mult)
        rows.append(jnp.pad(flat, (0, n * D - flat.shape[0])).reshape(n, D))
        offs.append(r0)
        r0 += n
    return jnp.concatenate(rows, 0), offs


def _unpack_rows(buf, offs, shapes):
    lead, out = buf.shape[:-2], []
    for o, shp in zip(offs, shapes):
        n = _size(shp)
        nr = -(-n // D)
        out.append(buf[..., o:o + nr, :].reshape(lead + (nr * D,))[..., :n].reshape(lead + tuple(shp)))
    return out


def _interleave(a):
    return a.reshape(a.shape[:-1] + (2, NCF, FFN_TC)).swapaxes(-2, -3).reshape(a.shape)


def _deinterleave(a):
    return a.reshape(a.shape[:-1] + (NCF, 2, FFN_TC)).swapaxes(-2, -3).reshape(a.shape)


def _cols_from_shards(g):
    return g.transpose(1, 0, 2).reshape(g.shape[1], N_DEV * g.shape[2])


def _cols_to_shards(w):
    k, n = w.shape[0], w.shape[1] // N_DEV
    return w.reshape(k, N_DEV, n).transpose(1, 0, 2)


def _rows3(w):
    return [w[i:i + 1] for i in range(3)]


def f_mod1(xs, ps):
    return f_mod(xs, ps)[:1]


def kernel(x, c, ctx, c_ctx, ada_w, ada_b, norm_mix, norm_ffn, gla_w_in, gla_w_a2, gla_b_a, gla_head_norm, gla_w_out, sc_w_in, sc_conv_w, sc_w_out, ffn_w_up, ffn_conv_w, ffn_conv_b, ffn_w_down, final_norm, loss_target, m_c_ctx, m_ada_w, m_ada_b, m_norm_mix, m_norm_ffn, m_gla_w_in, m_gla_w_a2, m_gla_b_a, m_gla_head_norm, m_gla_w_out, m_sc_w_in, m_sc_conv_w, m_sc_w_out, m_ffn_w_up, m_ffn_conv_w, m_ffn_conv_b, m_ffn_w_down, m_final_norm, v_c_ctx, v_ada_w, v_ada_b, v_norm_mix, v_norm_ffn, v_gla_w_in, v_gla_w_a2, v_gla_b_a, v_gla_head_norm, v_gla_w_out, v_sc_w_in, v_sc_conv_w, v_sc_w_out, v_ffn_w_up, v_ffn_conv_w, v_ffn_conv_b, v_ffn_w_down, v_final_norm):
    names = ["c_ctx", "ada_w", "ada_b", "norm_mix", "norm_ffn", "gla_w_in", "gla_w_a2", "gla_b_a", "gla_head_norm",
             "gla_w_out", "sc_w_in", "sc_conv_w", "sc_w_out", "ffn_w_up", "ffn_conv_w", "ffn_conv_b", "ffn_w_down",
             "final_norm"]
    w_ = dict(zip(names, [c_ctx, ada_w, ada_b, norm_mix, norm_ffn, gla_w_in, gla_w_a2, gla_b_a, gla_head_norm, gla_w_out,
                          sc_w_in, sc_conv_w, sc_w_out, ffn_w_up, ffn_conv_w, ffn_conv_b, ffn_w_down, final_norm]))
    m_ = dict(zip(names, [m_c_ctx, m_ada_w, m_ada_b, m_norm_mix, m_norm_ffn, m_gla_w_in, m_gla_w_a2, m_gla_b_a,
                          m_gla_head_norm, m_gla_w_out, m_sc_w_in, m_sc_conv_w, m_sc_w_out, m_ffn_w_up, m_ffn_conv_w,
                          m_ffn_conv_b, m_ffn_w_down, m_final_norm]))
    v_ = dict(zip(names, [v_c_ctx, v_ada_w, v_ada_b, v_norm_mix, v_norm_ffn, v_gla_w_in, v_gla_w_a2, v_gla_b_a,
                          v_gla_head_norm, v_gla_w_out, v_sc_w_in, v_sc_conv_w, v_sc_w_out, v_ffn_w_up, v_ffn_conv_w,
                          v_ffn_conv_b, v_ffn_w_down, v_final_norm]))
    me = 4 * lax.axis_index("x") + 2 * lax.axis_index("y") + lax.axis_index("c")
    bsz = x.shape[0]
    tm = 256
    nt = SEQ // tm
    ctx_tiles = CTX // tm
    pe = functools.partial(P, per_example=True)

    small_sharded = [c, gla_w_a2, gla_b_a, sc_conv_w, ffn_conv_w]
    pack0, offs0 = _pack_rows(small_sharded, F32, 8)
    g0 = all_gather("ag_small", pack0, True).reshape(N_DEV, pack0.shape[0], D)
    c_all, wa2_s, ba_s, scw_s, fcw_s = _unpack_rows(g0, offs0, [a.shape for a in small_sharded])
    w_a2 = wa2_s[:, 0].transpose(1, 2, 0, 3).reshape(2, RANK, KD)
    b_a = ba_s[:, 0].transpose(1, 0, 2).reshape(2, KD)
    sc_cw = scw_s[:, 0].transpose(1, 0, 2).reshape(3, D)
    ffn_cw = fcw_s.transpose(1, 2, 0, 3).reshape(2, 3, 2 * FFN_H)

    cond = jnp.concatenate([c_all.reshape(N_DEV * bsz, D), c_ctx[None], jnp.zeros((ADA_ROWS - N_DEV * bsz - 1, D), F32)], 0)
    b_mine = lax.dynamic_slice(ada_b, (0, me * ADA_COLS), (2, ADA_COLS)).reshape(2, 1, ADA_COLS)
    mod_part = ada_fwd(cond, ada_w, b_mine)
    mod = all_gather("ag_mod", mod_part.reshape(2 * ADA_ROWS, ADA_COLS), True)
    mod = mod.reshape(N_DEV, 2, ADA_ROWS, ADA_COLS).transpose(1, 2, 0, 3).reshape(2, ADA_ROWS, 6 * D)
    mods = lax.dynamic_slice(mod, (0, bsz * me, 0), (2, bsz, 6 * D))
    md = [[mods[i][:, k * D:(k + 1) * D].reshape(bsz, 1, D) for k in range(6)] for i in range(2)]
    mc = [mod[0, ADA_CTX_ROW, k * D:(k + 1) * D][None] for k in range(2)]

    groups = {"gla": [("gla_w_in", 0), ("gla_w_out", 0)], "ffn0": [("ffn_w_up", 0), ("ffn_w_down", 0)],
              "l1": [("sc_w_in", 0), ("sc_w_out", 0), ("ffn_w_up", 1), ("ffn_w_down", 1)]}

    def part(tree, key):
        return tree[key[0]][key[1]:key[1] + 1]

    ag_started, tok = {}, 0.0
    for g, keys in groups.items():
        wp, offs = _pack_rows([part(w_, k) for k in keys], BF16, 16)
        wp, mod = lax.optimization_barrier((wp, mod))
        st = exchange_start(f"ag_{g}_start", wp, True)
        ag_started[g] = (st, offs)
        tok = tok + st[4][0, 0]
    norm_mix = norm_mix + tok

    def gathered(g, after):
        st, offs = ag_started[g]
        mine, land = exchange_wait(f"ag_{g}_wait", st, after, True)
        land = lax.dynamic_update_index_in_dim(land, mine, me, 0)
        return [s[:, 0] for s in _unpack_rows(land, offs, [part(w_, k).shape for k in groups[g]])]

    w_up, w_down = [None, None], [None, None]
    fcw = [_rows3(_interleave(ffn_cw[i])) for i in range(2)]
    fcb = [_interleave(ffn_conv_b[i])[None] for i in range(2)]
    wd = jnp.zeros((128, 2 * KD), F32).at[:RANK, :KD].set(w_a2[0]).at[RANK:2 * RANK, KD:].set(w_a2[1])
    bd = b_a.reshape(1, 2 * KD)
    scw = _rows3(sc_cw)
    head_gain = gla_head_norm.reshape(1, HV)
    gains_mix = [norm_mix[i][None] for i in range(2)]
    gains_ffn = [norm_ffn[i][None] for i in range(2)]
    ffn_w = 2 * FFN_TC

    def ffn_params(i):
        return [P(a, w=ffn_w, split=2) for a in fcw[i] + [fcb[i]]]

    def ffn_fwd(i, hn2):
        u = mm3(f"ffn_up{i}", hn2, w_up[i], out_dtype=BF16)
        act = rowwise(f"ffn_mid{i}", f_ffn_mid, [X(u, w=ffn_w, split=2)], ffn_params(i), tm=SEQ, nt=1, nc=NCF,
                      outs=[(FFN_TC, BF16, 1)])[0]
        return u, act, mm3(f"ffn_down{i}", act, w_down[i])

    def res_mod_fwd(name, h, y, ps):
        return rowwise(name, f_res_mod, [X(h), X(y)], ps, tm=tm, nt=nt, outs=[(D, F32, 1), (D, BF16, 1)])

    ps_in0 = [P(gains_mix[0]), pe(md[0][0]), pe(md[0][1])]
    ps_ctx = [P(gains_mix[0]), P(mc[0]), P(mc[1])]
    hn0 = rowwise("mod_in0", f_mod, [X(x)], ps_in0, tm=tm, nt=nt, outs=[(D, BF16, 1)])[0]
    hnc = rowwise("mod_ctx", f_mod, [X(ctx)], ps_ctx, tm=tm, nt=ctx_tiles, outs=[(D, BF16, 1)])[0]
    hcat = jnp.concatenate([hnc, hn0], axis=1)
    s_gin, s_gout = gathered("gla", hcat)
    w_gin = jnp.pad(_cols_from_shards(s_gin), ((0, 0), (0, GLA_IN_PAD - GLA_IN)))
    w_gout = s_gout.reshape(VD, D)
    pcat = mm3("gla_in", hcat, w_gin)
    pa_x = X(pcat, w=128, co=(GLA_IN_PAD - 128) // 128)
    la = rowwise("gla_decay", f_decay, [pa_x], [P(wd), P(bd)], tm=tm, nt=TT // tm, outs=[(2 * KD, F32, 1)])[0]
    o2, s_all = gla_fwd(pcat, la)
    post_xs = [X(o2, w=VD, co=0, ro=ctx_tiles, split=HEADS), X(o2, w=VD, co=1, ro=ctx_tiles, split=HEADS),
               X(pcat, w=VD, co=2, ro=ctx_tiles, split=HEADS)]
    yin0 = rowwise("gla_post", f_gla_post, post_xs, [P(head_gain)], tm=tm, nt=nt, outs=[(VD, BF16, HEADS)])[0]
    y0 = mm3("gla_out", yin0, w_gout)
    ps_mid0 = [pe(md[0][2]), P(gains_ffn[0]), pe(md[0][3]), pe(md[0][4])]
    h1_0, hn2_0 = res_mod_fwd("res_mod_mid0", x, y0, ps_mid0)
    s_up0, s_down0 = gathered("ffn0", hn2_0)
    w_up[0], w_down[0] = _interleave(_cols_from_shards(s_up0)), s_down0.reshape(FFN_H, D)
    u0, act0, fo0 = ffn_fwd(0, hn2_0)
    ps_in1 = [pe(md[0][5]), P(gains_mix[1]), pe(md[1][0]), pe(md[1][1])]
    h2_0, hn1 = res_mod_fwd("res_mod_in1", h1_0, fo0, ps_in1)

    s_sin, s_sout, s_up1, s_down1 = gathered("l1", hn1)
    w_sin, w_sout = _cols_from_shards(s_sin), s_sout.reshape(D, D)
    w_up[1], w_down[1] = _interleave(_cols_from_shards(s_up1)), s_down1.reshape(FFN_H, D)
    p1 = mm3("sc_in", hn1, w_sin)
    sc_ps = [P(a) for a in scw]
    yin1 = rowwise("sc_mid", f_sc_mid, [X(p1, split=3)], sc_ps, tm=tm, nt=nt, outs=[(D, BF16, 1)])[0]
    y1 = mm3("sc_out", yin1, w_sout)
    ps_mid1 = [pe(md[1][2]), P(gains_ffn[1]), pe(md[1][3]), pe(md[1][4])]
    h1_1, hn2_1 = res_mod_fwd("res_mod_mid1", h2_0, y1, ps_mid1)
    u1, act1, fo1 = ffn_fwd(1, hn2_1)
    loss8, dh1_1, dfo1, dm5_1, g_final = final_loss(h1_1, fo1, md[1][5], final_norm[None], loss_target)
    loss = lax.psum(loss8[0, 0], ("x", "y", "c"))

    def ffn_bwd(i, u, act, hn2, dfo):
        dact = mm3(f"ffn_down_dx{i}", dfo, w_down[i], tb=True, out_dtype=BF16)
        g_down = mm_tn(f"ffn_down_dw{i}", act, dfo)
        r = rowwise(f"ffn_mid_bwd{i}", f_ffn_mid, [X(u, w=ffn_w, split=2)], ffn_params(i), tm=SEQ, nt=1, nc=NCF,
                    douts=[X(dact, w=FFN_TC)], dx={0: BF16}, dp=[0, 1, 2, 3])
        du, g_cw, g_cb = r[0], jnp.concatenate(r[1:4], 0), r[4]
        dhn2 = mm3(f"ffn_up_dx{i}", du, w_up[i], tb=True, out_dtype=BF16)
        g_up = mm_tn(f"ffn_up_dw{i}", hn2, du)
        return dhn2, _deinterleave(g_up), g_down, _deinterleave(g_cw), _deinterleave(g_cb)

    def res_mod_bwd(name, h, y, ps, dh1, dhn):
        return rowwise(name, f_res_mod, [X(h), X(y)], ps, tm=tm, nt=nt, douts=[X(dh1), X(dhn)],
                       dx={0: F32, 1: BF16}, dp=[0, 1, 2, 3])

    def row_slots(g):
        return g.reshape(N_DEV, -1, D)

    def col_slots(g):
        return _cols_to_shards(g).reshape(N_DEV, -1, D)

    a2a_started = {}

    def send_grads(g, slots, after=None):
        slots = [jnp.pad(s, ((0, 0), (0, (-s.shape[1]) % 16), (0, 0))).astype(BF16) for s in slots]
        gpack = jnp.concatenate(slots, 1)
        if after is not None:
            gpack, _ = lax.optimization_barrier((gpack, after))
        a2a_started[g] = exchange_start(f"a2a_{g}_start", gpack, False)
        return a2a_started[g][4][0, 0]

    def after_start(ps, tok):
        return [dict(ps[0], a=ps[0]["a"] + tok)] + ps[1:]

    dhn2_1, g_up1, g_down1, g_fcw1, g_fcb1 = ffn_bwd(1, u1, act1, hn2_1, dfo1)
    dh2_0, dy1, dm2_1, g_nffn1, dm3_1, dm4_1 = res_mod_bwd("res_mod_mid1_bwd", h2_0, y1, ps_mid1, dh1_1, dhn2_1)
    dyin1 = mm3("sc_out_dx", dy1, w_sout, tb=True, out_dtype=BF16)
    g_sout = mm_tn("sc_out_dw", yin1, dy1)
    r = rowwise("sc_mid_bwd", f_sc_mid, [X(p1, split=3)], sc_ps, tm=tm, nt=nt, douts=[X(dyin1)], dx={0: BF16}, dp=[0, 1, 2])
    dp1, g_scw = r[0], jnp.concatenate(r[1:4], 0)
    dhn1 = mm3("sc_in_dx", dp1, w_sin, tb=True, out_dtype=BF16)
    g_sin = mm_tn("sc_in_dw", hn1, dp1)
    tok = send_grads("l1", [col_slots(g_sin), row_slots(g_sout), col_slots(g_up1), row_slots(g_down1)])
    dh1_0, dfo0, dm5_0, g_nmix1, dm0_1, dm1_1 = res_mod_bwd("res_mod_in1_bwd", h1_0, fo0, after_start(ps_in1, tok), dh2_0, dhn1)

    dhn2_0, g_up0, g_down0, g_fcw0, g_fcb0 = ffn_bwd(0, u0, act0, hn2_0, dfo0)
    tok = send_grads("ffn0", [col_slots(g_up0), row_slots(g_down0)])
    dx_res, dy0, dm2_0, g_nffn0, dm3_0, dm4_0 = res_mod_bwd("res_mod_mid0_bwd", x, y0, after_start(ps_mid0, tok), dh1_0, dhn2_0)
    dyin0 = mm3("gla_out_dx", dy0, w_gout, tb=True, out_dtype=BF16)
    g_gout = mm_tn("gla_out_dw", yin0, dy0)
    do, dgate, g_head = rowwise("gla_post_bwd", f_gla_post, post_xs, [P(head_gain)], tm=tm, nt=nt,
                                douts=[X(dyin0, split=HEADS)], dx={0: F32, 2: BF16}, dp=[0])
    dq2, dk2, dv2, dla = gla_bwd(pcat, la, s_all, do)
    dpa, g_wd, g_bd = rowwise("gla_decay_bwd", f_decay, [pa_x], [P(wd), P(bd)], tm=tm, nt=TT // tm, douts=[X(dla)],
                              dx={0: BF16}, dp=[0, 1])
    dpcat = gla_combine(dq2, dk2, dv2, dgate, dpa)
    dhcat = mm3("gla_in_dx", dpcat, w_gin, tb=True, out_dtype=BF16)
    g_gin = mm_tn("gla_in_dw", hcat, dpcat)[:, :GLA_IN]
    grad_x, g_nmix0, dm0_0, dm1_0 = rowwise("mod_in0_bwd", f_mod, [X(x)], ps_in0, tm=tm, nt=nt,
                                            douts=[X(dhcat, ro=ctx_tiles), X(dx_res)], dx={0: F32}, dp=[0, 1, 2])
    g_nmix0c, dmc0, dmc1 = rowwise("mod_ctx_bwd", f_mod1, [X(ctx)], ps_ctx, tm=tm, nt=ctx_tiles, douts=[X(dhcat)],
                                   dx={}, dp=[0, 1, 2])

    zero_row = jnp.zeros((1, 4 * D), F32)
    dmod = [jnp.concatenate([jnp.concatenate([a.reshape(bsz, D) for a in dms], 1), ctx_row], 0)
            for dms, ctx_row in (([dm0_0, dm1_0, dm2_0, dm3_0, dm4_0, dm5_0], jnp.concatenate([dmc0, dmc1, zero_row], 1)),
                                 ([dm0_1, dm1_1, dm2_1, dm3_1, dm4_1, dm5_1], jnp.zeros((1, 6 * D), F32)))]
    g_wa2 = jnp.stack([g_wd[:RANK, :KD], g_wd[RANK:2 * RANK, KD:]])
    small_grads = [jnp.stack(dmod), jnp.concatenate([g_nmix0 + g_nmix0c, g_nmix1], 0), jnp.concatenate([g_nffn0, g_nffn1], 0),
                   g_head, jnp.concatenate([g_fcb0, g_fcb1], 0), g_final, g_wa2, g_bd.reshape(2, KD), g_scw,
                   jnp.stack([g_fcw0, g_fcw1])]
    pack1, offs1 = _pack_rows(small_grads, F32, 8)
    g1 = all_gather("ag_grads", pack1, True).reshape(N_DEV, pack1.shape[0], D)
    dmod_all = _unpack_rows(g1, offs1[:1], [small_grads[0].shape])[0]
    tot = _unpack_rows(sum_slots("sum_small", g1), offs1, [a.shape for a in small_grads])
    dm_rows = dmod_all[:, :, :bsz].transpose(1, 0, 2, 3).reshape(2, N_DEV * bsz, 6 * D)
    dm_full = jnp.concatenate([dm_rows, tot[0][:, bsz:], jnp.zeros((2, ADA_ROWS - N_DEV * bsz - 1, 6 * D), F32)], 1)
    dm_mine = lax.dynamic_slice(dm_full, (0, 0, me * ADA_COLS), (2, ADA_ROWS, ADA_COLS))
    g_ada_w, g_ada_b, cpart = ada_bwd(cond, dm_mine, dm_full, ada_w)
    cparts = all_gather("ag_cctx", cpart, True).reshape(N_DEV, ADA_ROWS - ADA_CTX_ROW, D)[:, 0]
    g_cctx = cctx_grad(cparts, c_ctx[None])[0]
    tok = send_grads("gla", [col_slots(g_gin), row_slots(g_gout)], after=g_cctx)

    def my_cols(full, n):
        return lax.dynamic_slice_in_dim(full, me * n, n, axis=full.ndim - 1)

    grads = {
        "c_ctx": g_cctx, "ada_b": g_ada_b.reshape(2, 6 * D), "norm_mix": tot[1], "norm_ffn": tot[2],
        "gla_head_norm": tot[3], "ffn_conv_b": tot[4], "final_norm": tot[5].reshape(D),
        "gla_w_a2": my_cols(tot[6], KD // N_DEV)[None], "gla_b_a": my_cols(tot[7], KD // N_DEV)[None],
        "sc_conv_w": my_cols(tot[8], D // N_DEV)[None], "ffn_conv_w": my_cols(tot[9], 2 * FFN_H // N_DEV),
    }

    res_ada = adamw("adamw_ada", *[a.reshape(2 * D, ADA_COLS) for a in (ada_w, g_ada_w, m_ada_w, v_ada_w)])
    grads["c_ctx"] = g_cctx + tok
    big = ["gla_w_in", "gla_w_out", "sc_w_in", "sc_w_out", "ffn_w_up", "ffn_w_down"]
    small = [n for n in names if n not in big and n != "ada_w"]
    g_small = _pack_rows([grads[n] for n in small], F32, 8)[0]
    res_small = adamw("adamw_small", _pack_rows([w_[n] for n in small], F32, 8)[0], g_small,
                      _pack_rows([m_[n] for n in small], F32, 8)[0], _pack_rows([v_[n] for n in small], F32, 8)[0])
    offs_s = _pack_rows([w_[n] for n in small], F32, 8)[1]

    big_res, after = {}, res_small[0]
    for g in ("l1", "ffn0", "gla"):
        keys = groups[g]
        sent, land = exchange_wait(f"a2a_{g}_wait", a2a_started[g], after, False)
        land = lax.dynamic_update_index_in_dim(land, lax.dynamic_index_in_dim(sent, me, 0, keepdims=False), me, 0)
        packs = [_pack_rows([part(t, k) for k in keys], F32, 16)[0] for t in (w_, m_, v_)]
        res = adamw(f"adamw_{g}", packs[0], land, packs[1], packs[2])
        shapes = [part(w_, k).shape for k in keys]
        for i, k in enumerate(keys):
            big_res[k] = [_unpack_rows(r, ag_started[g][1], shapes)[i] for r in res]
        after = res[0]

    out = {}
    for kind, idx in (("grad", 0), ("delta", 1), ("new_m", 2), ("new_v", 3)):
        vals = {n: jnp.concatenate([big_res[(n, i)][idx] for i in range(w_[n].shape[0])], 0) for n in big}
        vals["ada_w"] = res_ada[idx].reshape(ada_w.shape)
        vals.update(zip(small, _unpack_rows(res_small[idx], offs_s, [w_[n].shape for n in small])))
        out[kind] = [vals[n] for n in names]
    return (loss, grad_x, *out["grad"], *out["delta"], *out["new_m"], *out["new_v"])
```

```python
import functools

import jax
import jax.numpy as jnp
from jax import lax
from jax.experimental import pallas as pl
from jax.experimental.pallas import tpu as pltpu

F32 = jnp.float32
BF16 = jnp.bfloat16

N_DEV = 8
D = 1024
SEQ = 2048
CTX = 256
TT = CTX + SEQ
GRID_W = 64
CHUNK = 64
HEADS = 4
HK = 128
HV = 256
KD = 512
VD = 1024
RANK = 16
TAU = 16.0
GLA_IN = 3104
GLA_IN_PAD = 3200
FFN_H = 2560
FFN_TC = 256
EPS = 1e-6
LR, B1, B2, AEPS, WD, STEP = 0.001, 0.9, 0.999, 1e-08, 0.01, 10
MESH = pl.DeviceIdType.MESH


def V(arr, kind="flat"):
    if kind == "tok":
        return V(arr.reshape(-1, arr.shape[-1]))
    if kind == "flat":
        r, c = arr.shape
        return dict(a=arr, kind=kind, shape=(r, c), runit=r, cunit=c)
    if kind == "planes":
        bsz, _, t, ch = arr.shape
        return dict(a=arr, kind=kind, shape=(bsz * t, 2 * ch), runit=t, cunit=ch)
    _, r, n = arr.shape
    return dict(a=arr, kind=kind, shape=(r, N_DEV * n), runit=r, cunit=n)


def _view_spec(v, br, bc, idx):
    if v["kind"] == "flat":
        return pl.BlockSpec((br, bc), idx)
    if v["kind"] == "planes":
        nt, nch = v["runit"] // br, v["cunit"] // bc

        def at(i, j, k):
            r, c = idx(i, j, k)
            return r // nt, c // nch, r % nt, c % nch
        return pl.BlockSpec((None, None, br, bc), at)
    per = v["cunit"] // bc

    def at(i, j, k):
        r, c = idx(i, j, k)
        return c // per, r, c % per
    return pl.BlockSpec((None, br, bc), at)


def _tile(*units, cap=1024):
    for t in range(cap, 0, -128):
        if all(u % t == 0 for u in units):
            return t
    raise ValueError(units)


def _out_view(kind, rows, cols, dtype, planes_t=None):
    if kind == "flat":
        shape = (rows, cols)
    elif kind == "planes":
        shape = (rows // planes_t, 2, planes_t, cols // 2)
    else:
        shape = (N_DEV, rows, cols // N_DEV)
    return V(jax.ShapeDtypeStruct(shape, dtype), kind)


def mm(name, a, b, form="nn", out="flat", out_dtype=F32, planes_t=None):
    (m, kk) = a["shape"][::-1] if form == "tn" else a["shape"]
    n = b["shape"][0] if form == "nt" else b["shape"][1]
    assert (b["shape"][1] if form == "nt" else b["shape"][0]) == kk, (name, a["shape"], b["shape"])
    o = _out_view(out, m, n, out_dtype, planes_t)
    a_m, a_k = (a["cunit"], a["runit"]) if form == "tn" else (a["runit"], a["cunit"])
    b_k, b_n = (b["cunit"], b["runit"]) if form == "nt" else (b["runit"], b["cunit"])
    tm, tn, tk = _tile(a_m, o["runit"]), _tile(b_n, o["cunit"]), _tile(a_k, b_k)
    nk = kk // tk
    dn = (((0 if form == "tn" else 1,), (1 if form == "nt" else 0,)), ((), ()))

    def body(a_ref, b_ref, o_ref, acc_ref):
        k = pl.program_id(2)
        part = lax.dot_general(a_ref[...].astype(BF16), b_ref[...].astype(BF16), dn, preferred_element_type=F32)

        @pl.when(k == 0)
        def _():
            acc_ref[...] = part

        @pl.when(k > 0)
        def _():
            acc_ref[...] += part

        @pl.when(k == nk - 1)
        def _():
            o_ref[...] = acc_ref[...].astype(out_dtype)

    if form == "tn":
        a_spec = _view_spec(a, tk, tm, lambda i, j, k: (k, i))
    else:
        a_spec = _view_spec(a, tm, tk, lambda i, j, k: (i, k))
    if form == "nt":
        b_spec = _view_spec(b, tn, tk, lambda i, j, k: (j, k))
    else:
        b_spec = _view_spec(b, tk, tn, lambda i, j, k: (k, j))
    return pl.pallas_call(
        body, name=name, grid=(m // tm, n // tn, nk),
        in_specs=[a_spec, b_spec], out_specs=_view_spec(o, tm, tn, lambda i, j, k: (i, j)), out_shape=o["a"],
        scratch_shapes=[pltpu.VMEM((tm, tn), F32)],
        compiler_params=pltpu.CompilerParams(dimension_semantics=("parallel", "parallel", "arbitrary")),
    )(a["a"], b["a"])


def X(arr, w=None, co=0, ro=0, split=1, planes=False):
    return dict(a=arr, w=arr.shape[-1] if w is None else w, co=co, ro=ro, split=2 if planes else split,
                mode="planes" if planes else "cols")


def P(arr, per_example=False, w=None, split=1, rows=False):
    return dict(a=arr, e=per_example, w=arr.shape[-1] if w is None else w, split=arr.shape[-2] if rows else split,
                mode="rows" if rows else "cols")


def _pieces(ref, s):
    if s["mode"] == "planes":
        return [ref[0], ref[1]]
    if s["mode"] == "rows":
        return [ref[i:i + 1, :] for i in range(s["split"])]
    w = ref.shape[-1] // s["split"]
    return [ref[:, i * w:(i + 1) * w] for i in range(s["split"])]


def _store(ref, pieces, s, accumulate=False):
    w = ref.shape[-1] // len(pieces)
    for i, p in enumerate(pieces):
        at = (i,) if s["mode"] == "planes" else (slice(i, i + 1),) if s["mode"] == "rows" else (slice(None), slice(i * w, (i + 1) * w))
        if accumulate:
            ref[at] += p.astype(ref.dtype)
        else:
            ref[at] = p.astype(ref.dtype)


def rowwise(name, f, xs, ps, *, tm, nt, nc=1, outs=None, douts=None, dx=None, dp=None):
    bsz = xs[0]["a"].shape[0]
    fwd = douts is None
    nx, np_ = len(xs), len(ps)
    douts = [] if fwd else douts
    dx = {} if fwd else dx
    dp = [] if fwd else dp

    def x_spec(s):
        if s["mode"] == "planes":
            return pl.BlockSpec((None, 2, tm, s["w"]), lambda c, b, t, s=s: (b, 0, t + s["ro"], c + s["co"]))
        return pl.BlockSpec((None, tm, s["w"]), lambda c, b, t, s=s: (b, t + s["ro"], c + s["co"]))

    def x_out(s, dt):
        if s["mode"] == "planes":
            return (jax.ShapeDtypeStruct((bsz, 2, nt * tm, nc * s["w"]), dt),
                    pl.BlockSpec((None, 2, tm, s["w"]), lambda c, b, t: (b, 0, t, c)))
        return (jax.ShapeDtypeStruct((bsz, nt * tm, nc * s["w"]), dt), pl.BlockSpec((None, tm, s["w"]), lambda c, b, t: (b, t, c)))

    def p_spec(s):
        r = s["a"].shape[-2]
        if s["e"]:
            return pl.BlockSpec((None, r, s["w"]), lambda c, b, t: (b, 0, c))
        return pl.BlockSpec((r, s["w"]), lambda c, b, t: (0, c))

    in_specs = [x_spec(s) for s in xs] + [p_spec(s) for s in ps] + [x_spec(s) for s in douts]
    operands = [s["a"] for s in xs] + [s["a"] for s in ps] + [s["a"] for s in douts]
    if fwd:
        out_modes = [dict(mode="cols", split=sp) for (_, _, sp) in outs]
        out_shape = [jax.ShapeDtypeStruct((bsz, nt * tm, nc * w), dt) for (w, dt, _) in outs]
        out_specs = [pl.BlockSpec((None, tm, w), lambda c, b, t: (b, t, c)) for (w, _, _) in outs]
    else:
        dx_outs = [x_out(xs[i], dt) for i, dt in dx.items()]
        out_shape, out_specs = [o[0] for o in dx_outs], [o[1] for o in dx_outs]
        for j in dp:
            s = ps[j]
            r = s["a"].shape[-2]
            if s["e"]:
                out_shape.append(jax.ShapeDtypeStruct((bsz, r, nc * s["w"]), F32))
                out_specs.append(pl.BlockSpec((None, r, s["w"]), lambda c, b, t: (b, 0, c)))
            else:
                out_shape.append(jax.ShapeDtypeStruct((r, nc * s["w"]), F32))
                out_specs.append(pl.BlockSpec((r, s["w"]), lambda c, b, t: (0, c)))

    def body(*refs):
        x_refs, p_refs = refs[:nx], refs[nx:nx + np_]
        d_refs = refs[nx + np_:nx + np_ + len(douts)]
        o_refs = refs[nx + np_ + len(douts):]
        xv = [[p.astype(F32) for p in _pieces(r, s)] for r, s in zip(x_refs, xs)]
        pv = [[p.astype(F32) for p in _pieces(r, s)] for r, s in zip(p_refs, ps)]
        if fwd:
            for r, pieces, s in zip(o_refs, f(xv, pv), out_modes):
                _store(r, pieces, s)
            return
        _, vjp = jax.vjp(f, xv, pv)
        cot = [[p.astype(F32) for p in _pieces(r, s)] for r, s in zip(d_refs, douts)]
        dxv, dpv = vjp(cot)
        for r, i in zip(o_refs, dx):
            _store(r, dxv[i], xs[i])
        b, t = pl.program_id(1), pl.program_id(2)
        for r, j in zip(o_refs[len(dx):], dp):
            first = (t == 0) if ps[j]["e"] else jnp.logical_and(b == 0, t == 0)

            @pl.when(first)
            def _(r=r, j=j):
                _store(r, dpv[j], ps[j])

            @pl.when(jnp.logical_not(first))
            def _(r=r, j=j):
                _store(r, dpv[j], ps[j], accumulate=True)

    res = pl.pallas_call(
        body, name=name, grid=(nc, bsz, nt), in_specs=in_specs, out_specs=out_specs, out_shape=out_shape,
        compiler_params=pltpu.CompilerParams(dimension_semantics=("arbitrary", "arbitrary", "arbitrary")),
    )(*operands)
    return res


def _keep_rows(a, shift, keep):
    n = a.shape[0]
    t = lax.broadcasted_iota(jnp.int32, a.shape, 0)
    return jnp.where(keep(t, n), pltpu.roll(a, shift % n, 0), 0.0)


def _shift_pair(step, keep_prev, keep_next):
    @jax.custom_vjp
    def prev(a):
        return _keep_rows(a, step, keep_prev)

    @jax.custom_vjp
    def nxt(a):
        return _keep_rows(a, -step, keep_next)

    prev.defvjp(lambda a: (prev(a), None), lambda _, g: (nxt(g),))
    nxt.defvjp(lambda a: (nxt(a), None), lambda _, g: (prev(g),))
    return prev, nxt


prev_tok, next_tok = _shift_pair(1, lambda t, n: t % GRID_W != 0, lambda t, n: t % GRID_W != GRID_W - 1)
prev_row, next_row = _shift_pair(GRID_W, lambda t, n: t >= GRID_W, lambda t, n: t < n - GRID_W)


@jax.custom_vjp
def bdot(a, w):
    return jnp.dot(a.astype(BF16), w.astype(BF16), preferred_element_type=F32)


def _bdot_bwd(res, g):
    a, w = res
    gb = g.astype(BF16)
    da = lax.dot_general(gb, w.astype(BF16), (((1,), (1,)), ((), ())), preferred_element_type=F32)
    dw = lax.dot_general(a.astype(BF16), gb, (((0,), (0,)), ((), ())), preferred_element_type=F32)
    return da, dw


bdot.defvjp(lambda a, w: (bdot(a, w), (a, w)), _bdot_bwd)


@jax.custom_vjp
def log_sigmoid(z):
    return jnp.minimum(z, 0.0) - jnp.log(1.0 + jnp.exp(-jnp.abs(z)))


def _lsig_bwd(z, g):
    e = jnp.exp(-jnp.abs(z))
    return (g * jnp.where(z >= 0, e, 1.0) / (1.0 + e),)


log_sigmoid.defvjp(lambda z: (log_sigmoid(z), z), _lsig_bwd)


def silu(x):
    return x * jax.nn.sigmoid(x)


def _rms(x):
    return x * lax.rsqrt(jnp.mean(x * x, axis=-1, keepdims=True) + EPS)


def _mod(x, gain, shift, scale):
    return _rms(x) * gain * (1.0 + scale) + shift


def f_mod(xs, ps):
    ((h,),), ((gain,), (shift,), (scale,)) = xs, ps
    return [[_mod(h, gain, shift, scale)], [h]]


def f_res_mod(xs, ps):
    ((h,), (y,)), ((gate,), (gain,), (shift,), (scale,)) = xs, ps
    h1 = h + gate * y
    return [[h1], [_mod(h1, gain, shift, scale)]]


def f_ffn_mid(xs, ps):
    ((ua, ug),), ((w0a, w0g), (w1a, w1g), (w2a, w2g), (ba, bg)) = xs, ps
    a = w0a * prev_row(ua) + w1a * ua + w2a * next_row(ua) + ba
    g = w0g * prev_row(ug) + w1g * ug + w2g * next_row(ug) + bg
    return [[a * silu(g)]]


def f_sc_mid(xs, ps):
    ((bg, cg, v),), ((w0,), (w1,), (w2,)) = xs, ps
    z = cg * v
    return [[bg * (w0 * prev_tok(z) + w1 * z + w2 * next_tok(z))]]


def f_decay(xs, ps):
    ((a,),), ((wd,), (bd,)) = xs, ps
    return [[log_sigmoid(bdot(a, wd) + bd) / TAU]]


def f_gla_post(xs, ps):
    (of, ob, g), ((gain,),) = xs, ps
    return [[_rms(a + b) * gain * silu(c) for a, b, c in zip(of, ob, g)]]


NCH = TT // CHUNK
CTX_CH = CTX // CHUNK
_NT = (((1,), (1,)), ((), ()))
_TN = (((0,), (0,)), ((), ()))
_NN = (((1,), (0,)), ((), ()))


def _chunk_of(d, j):
    return jnp.where(d == 0, j, jnp.where(j < CTX_CH, CTX_CH - 1 - j, NCH + CTX_CH - 1 - j))


def _dot(a, b, dn):
    return lax.dot_general(a, b, dn, preferred_element_type=F32)


def _mask_dot(m, g):
    g0 = g.astype(BF16)
    r1 = g - g0.astype(F32)
    g1 = r1.astype(BF16)
    g2 = (r1 - g1.astype(F32)).astype(BF16)
    return _dot(m, g0, _NN) + _dot(m, g1, _NN) + _dot(m, g2, _NN)


def _causal(d):
    row = lax.broadcasted_iota(jnp.int32, (CHUNK, CHUNK), 0)
    col = lax.broadcasted_iota(jnp.int32, (CHUNK, CHUNK), 1)
    delta = jnp.where(d == 0, col - row, row - col)
    return delta <= 0, delta >= 0


def _gla_in_specs(rev):
    def blk(d, j):
        return _chunk_of(d, (NCH - 1 - j) if rev else j)

    return [
        pl.BlockSpec((None, CHUNK, KD), lambda b, d, j: (b, blk(d, j), 0)),
        pl.BlockSpec((None, CHUNK, KD), lambda b, d, j: (b, blk(d, j), 1)),
        pl.BlockSpec((None, CHUNK, VD), lambda b, d, j: (b, blk(d, j), 1)),
        pl.BlockSpec((None, CHUNK, KD), lambda b, d, j: (b, blk(d, j), d)),
    ], blk


def gla_fwd(pcat, la):
    bsz = pcat.shape[0]
    in_specs, blk = _gla_in_specs(False)

    def body(q_ref, k_ref, v_ref, la_ref, o_ref, s_ref, st):
        d, j = pl.program_id(1), pl.program_id(2)

        @pl.when(j == 0)
        def _():
            st[...] = jnp.zeros_like(st)

        s_ref[...] = st[...]
        causal, _ = _causal(d)
        mf = causal.astype(BF16)
        for h in range(HEADS):
            ks_, vs_ = slice(h * HK, (h + 1) * HK), slice(h * HV, (h + 1) * HV)
            q, k, v, g = q_ref[:, ks_] * (HK ** -0.5), k_ref[:, ks_], v_ref[:, vs_].astype(BF16), la_ref[:, ks_]
            b = _mask_dot(mf, g)
            bl = jnp.sum(g, axis=0, keepdims=True)
            qs = (q * jnp.exp(b)).astype(BF16)
            ks = (k * jnp.exp(-b)).astype(BF16)
            kd = (k * jnp.exp(bl - b)).astype(BF16)
            s = st[h]
            att = jnp.where(causal, _dot(qs, ks, _NT), 0.0).astype(BF16)
            o_ref[:, vs_] = _dot(qs, s.astype(BF16), _NT) + _dot(att, v, _NN)
            st[h] = jnp.exp(bl) * s + _dot(v, kd, _TN)

    return pl.pallas_call(
        body, name="gla_fwd", grid=(bsz, 2, NCH), in_specs=in_specs,
        out_specs=[pl.BlockSpec((None, CHUNK, VD), lambda b, d, j: (b, blk(d, j), d)),
                   pl.BlockSpec((None, None, None, HEADS, HV, HK), lambda b, d, j: (b, d, j, 0, 0, 0))],
        out_shape=[jax.ShapeDtypeStruct((bsz, TT, 2 * VD), F32), jax.ShapeDtypeStruct((bsz, 2, NCH, HEADS, HV, HK), F32)],
        scratch_shapes=[pltpu.VMEM((HEADS, HV, HK), F32)],
        compiler_params=pltpu.CompilerParams(dimension_semantics=("arbitrary", "arbitrary", "arbitrary")),
    )(pcat, pcat, pcat, la)


def gla_bwd(pcat, la, s_all, do):
    bsz = pcat.shape[0]
    in_specs, blk = _gla_in_specs(True)
    in_specs += [
        pl.BlockSpec((None, None, None, HEADS, HV, HK), lambda b, d, j: (b, d, NCH - 1 - j, 0, 0, 0)),
        pl.BlockSpec((None, CHUNK, VD), lambda b, d, j: (b, jnp.maximum(blk(d, j) - CTX_CH, 0), 0)),
    ]

    def body(q_ref, k_ref, v_ref, la_ref, s_ref, do_ref, dq_ref, dk_ref, dv_ref, dla_ref, dst):
        d, j = pl.program_id(1), pl.program_id(2)

        @pl.when(j == 0)
        def _():
            dst[...] = jnp.zeros_like(dst)

        latent = blk(d, j) >= CTX_CH
        causal, causal_t = _causal(d)
        mt = causal_t.astype(BF16)
        mf = causal.astype(BF16)
        scale = HK ** -0.5
        for h in range(HEADS):
            ks_, vs_ = slice(h * HK, (h + 1) * HK), slice(h * HV, (h + 1) * HV)
            q, k, v, g = q_ref[:, ks_] * scale, k_ref[:, ks_], v_ref[:, vs_].astype(BF16), la_ref[:, ks_]
            b = _mask_dot(mf, g)
            bl = jnp.sum(g, axis=0, keepdims=True)
            e, ei, ed, el = jnp.exp(b), jnp.exp(-b), jnp.exp(bl - b), jnp.exp(bl)
            qs, ks, kd = q * e, k * ei, k * ed
            qsb, ksb, kdb = qs.astype(BF16), ks.astype(BF16), kd.astype(BF16)
            s, ds1 = s_ref[h], dst[h]
            sb, ds1b = s.astype(BF16), ds1.astype(BF16)
            dob = jnp.where(latent, do_ref[:, vs_], 0.0).astype(BF16)
            att = jnp.where(causal, _dot(qsb, ksb, _NT), 0.0).astype(BF16)
            datt = jnp.where(causal, _dot(dob, v, _NT), 0.0).astype(BF16)
            dqs = _dot(dob, sb, _NN) + _dot(datt, ksb, _NN)
            dks = _dot(datt, qsb, _TN)
            dv_ref[:, vs_] = _dot(att, dob, _TN) + _dot(kdb, ds1b, _NT)
            dkd = _dot(v, ds1b, _NN)
            dst[h] = _dot(dob, qsb, _TN) + el * ds1
            del_ = jnp.sum(s * ds1, axis=0, keepdims=True)
            dq_ref[:, ks_] = dqs * e * scale
            dk_ref[:, ks_] = dks * ei + dkd * ed
            db = dqs * qs - dks * ks - dkd * kd
            dbl = jnp.sum(dkd * kd, axis=0, keepdims=True) + del_ * el
            dla_ref[:, ks_] = _mask_dot(mt, db) + dbl

    return pl.pallas_call(
        body, name="gla_bwd", grid=(bsz, 2, NCH), in_specs=in_specs,
        out_specs=[pl.BlockSpec((None, None, CHUNK, KD), lambda b, d, j: (d, b, blk(d, j), 0)),
                   pl.BlockSpec((None, None, CHUNK, KD), lambda b, d, j: (d, b, blk(d, j), 0)),
                   pl.BlockSpec((None, None, CHUNK, VD), lambda b, d, j: (d, b, blk(d, j), 0)),
                   pl.BlockSpec((None, CHUNK, KD), lambda b, d, j: (b, blk(d, j), d))],
        out_shape=[jax.ShapeDtypeStruct((2, bsz, TT, KD), F32), jax.ShapeDtypeStruct((2, bsz, TT, KD), F32),
                   jax.ShapeDtypeStruct((2, bsz, TT, VD), F32), jax.ShapeDtypeStruct((bsz, TT, 2 * KD), F32)],
        scratch_shapes=[pltpu.VMEM((HEADS, HV, HK), F32)],
        compiler_params=pltpu.CompilerParams(dimension_semantics=("arbitrary", "arbitrary", "arbitrary")),
    )(pcat, pcat, pcat, la, s_all, do)


def gla_combine(dq2, dk2, dv2, dgate, dpa):
    bsz = dgate.shape[0]
    tm = CTX

    def body(dq_ref, dk_ref, dv_ref, dg_ref, dpa_ref, o_ref):
        t = pl.program_id(1)
        o_ref[:, 0:KD] = (dq_ref[0] + dq_ref[1]).astype(BF16)
        o_ref[:, KD:2 * KD] = (dk_ref[0] + dk_ref[1]).astype(BF16)
        o_ref[:, 2 * KD:2 * KD + VD] = (dv_ref[0] + dv_ref[1]).astype(BF16)
        o_ref[:, 2 * KD + VD:2 * KD + 2 * VD] = jnp.where(t > 0, dg_ref[...], 0).astype(BF16)
        o_ref[:, 2 * KD + 2 * VD:] = dpa_ref[...].astype(BF16)

    return pl.pallas_call(
        body, name="gla_combine", grid=(bsz, TT // tm),
        in_specs=[pl.BlockSpec((2, None, tm, KD), lambda b, t: (0, b, t, 0)),
                  pl.BlockSpec((2, None, tm, KD), lambda b, t: (0, b, t, 0)),
                  pl.BlockSpec((2, None, tm, VD), lambda b, t: (0, b, t, 0)),
                  pl.BlockSpec((None, tm, VD), lambda b, t: (b, jnp.maximum(t - 1, 0), 0)),
                  pl.BlockSpec((None, tm, 128), lambda b, t: (b, t, 0))],
        out_specs=pl.BlockSpec((None, tm, GLA_IN_PAD), lambda b, t: (b, t, 0)),
        out_shape=jax.ShapeDtypeStruct((bsz, TT, GLA_IN_PAD), BF16),
        compiler_params=pltpu.CompilerParams(dimension_semantics=("arbitrary", "arbitrary")),
    )(dq2, dk2, dv2, dgate, dpa)


def final_loss(h1, fo, gate, gain, tgt):
    bsz, t_len, _ = h1.shape
    tm = 256

    def body(h_ref, f_ref, gate_ref, gain_ref, tgt_ref, loss_ref, dh_ref, df_ref, dgate_ref, dgain_ref):
        b, t = pl.program_id(0), pl.program_id(1)
        target = tgt_ref[...]

        def core(h, fo_, gate_, gain_):
            e = _rms(h + gate_ * fo_) * gain_ - target
            return jnp.sum(0.5 * jnp.sum(e * e, axis=-1, keepdims=True) / D, axis=0, keepdims=True)

        loss, vjp = jax.vjp(core, h_ref[...], f_ref[...], gate_ref[...], gain_ref[...])
        dh, df, dgate, dgain = vjp(jnp.ones((1, 1), F32))
        dh_ref[...] = dh
        df_ref[...] = df.astype(BF16)
        first = jnp.logical_and(b == 0, t == 0)

        @pl.when(first)
        def _():
            loss_ref[...] = jnp.broadcast_to(loss, loss_ref.shape)
            dgain_ref[...] = dgain

        @pl.when(jnp.logical_not(first))
        def _():
            loss_ref[...] += jnp.broadcast_to(loss, loss_ref.shape)
            dgain_ref[...] += dgain

        @pl.when(t == 0)
        def _():
            dgate_ref[...] = dgate

        @pl.when(t > 0)
        def _():
            dgate_ref[...] += dgate

    tile = pl.BlockSpec((None, tm, D), lambda b, t: (b, t, 0))
    per_ex = pl.BlockSpec((None, 1, D), lambda b, t: (b, 0, 0))
    shared = pl.BlockSpec((1, D), lambda b, t: (0, 0))
    return pl.pallas_call(
        body, name="final_loss", grid=(bsz, t_len // tm),
        in_specs=[tile, tile, per_ex, shared, tile],
        out_specs=[pl.BlockSpec((8, 128), lambda b, t: (0, 0)), tile, tile, per_ex, shared],
        out_shape=[jax.ShapeDtypeStruct((8, 128), F32), jax.ShapeDtypeStruct(h1.shape, F32),
                   jax.ShapeDtypeStruct(h1.shape, BF16), jax.ShapeDtypeStruct((bsz, 1, D), F32),
                   jax.ShapeDtypeStruct((1, D), F32)],
        compiler_params=pltpu.CompilerParams(dimension_semantics=("arbitrary", "arbitrary")),
    )(h1, fo, gate, gain, tgt)


ADA_ROWS = 24
ADA_CTX_ROW = 16
ADA_COLS = 6 * D // N_DEV


def ada_fwd(cond, w, b):
    def body(c_ref, w_ref, b_ref, o_ref):
        s = silu(c_ref[...]).astype(BF16)
        o_ref[...] = jnp.dot(s, w_ref[...].astype(BF16), preferred_element_type=F32) + b_ref[...]

    return pl.pallas_call(
        body, name="ada_fwd", grid=(2,),
        in_specs=[pl.BlockSpec((ADA_ROWS, D), lambda i: (0, 0)), pl.BlockSpec((None, D, ADA_COLS), lambda i: (i, 0, 0)),
                  pl.BlockSpec((None, 1, ADA_COLS), lambda i: (i, 0, 0))],
        out_specs=pl.BlockSpec((None, ADA_ROWS, ADA_COLS), lambda i: (i, 0, 0)),
        out_shape=jax.ShapeDtypeStruct((2, ADA_ROWS, ADA_COLS), F32),
    )(cond, w, b)


def ada_bwd(cond, dm_mine, dm_full, w):
    def body(c_ref, dm_ref, dmf_ref, w_ref, gw_ref, gb_ref, cp_ref):
        i = pl.program_id(0)
        s = silu(c_ref[...]).astype(BF16)
        dm = dm_ref[...].astype(BF16)
        gw_ref[...] = _dot(s, dm, _TN)
        gb_ref[...] = jnp.sum(dmf_ref[...], axis=0, keepdims=True)

        @pl.when(i == 0)
        def _():
            cp_ref[...] = _dot(dm_ref[ADA_CTX_ROW:, :].astype(BF16), w_ref[...].astype(BF16), _NT)

    return pl.pallas_call(
        body, name="ada_bwd", grid=(2,),
        in_specs=[pl.BlockSpec((ADA_ROWS, D), lambda i: (0, 0)), pl.BlockSpec((None, ADA_ROWS, ADA_COLS), lambda i: (i, 0, 0)),
                  pl.BlockSpec((None, ADA_ROWS, 6 * D), lambda i: (i, 0, 0)), pl.BlockSpec((None, D, ADA_COLS), lambda i: (i, 0, 0))],
        out_specs=[pl.BlockSpec((None, D, ADA_COLS), lambda i: (i, 0, 0)), pl.BlockSpec((None, 1, 6 * D), lambda i: (i, 0, 0)),
                   pl.BlockSpec((ADA_ROWS - ADA_CTX_ROW, D), lambda i: (0, 0))],
        out_shape=[jax.ShapeDtypeStruct((2, D, ADA_COLS), F32), jax.ShapeDtypeStruct((2, 1, 6 * D), F32),
                   jax.ShapeDtypeStruct((ADA_ROWS - ADA_CTX_ROW, D), F32)],
        compiler_params=pltpu.CompilerParams(dimension_semantics=("arbitrary",)),
    )(cond, dm_mine, dm_full, w)


def cctx_grad(parts, c_ctx):
    def body(p_ref, c_ref, o_ref):
        tot = p_ref[0:1, :]
        for i in range(1, N_DEV):
            tot = tot + p_ref[i:i + 1, :]
        c = c_ref[...]
        sg = jax.nn.sigmoid(c)
        o_ref[...] = tot * sg * (1.0 + c * (1.0 - sg))

    return pl.pallas_call(body, name="cctx_grad", out_shape=jax.ShapeDtypeStruct((1, D), F32))(parts, c_ctx)


def _row_tile(r):
    for t in (512, 256, 128, 80, 64, 40, 32, 16, 8):
        if r % t == 0:
            return t
    return r


def _slot_sum(ref):
    tot = ref[0].astype(F32)
    for i in range(1, ref.shape[0]):
        tot = tot + ref[i].astype(F32)
    return tot


def sum_slots(name, x):
    s, r, c = x.shape
    tr = _row_tile(r)

    def body(x_ref, o_ref):
        o_ref[...] = _slot_sum(x_ref)

    return pl.pallas_call(
        body, name=name, grid=(r // tr,), in_specs=[pl.BlockSpec((s, tr, c), lambda i: (0, i, 0))],
        out_specs=pl.BlockSpec((tr, c), lambda i: (i, 0)), out_shape=jax.ShapeDtypeStruct((r, c), F32),
    )(x)


def adamw(name, w, g, m, v, layer=None):
    r, c = w.shape[-2:]
    tr = _row_tile(r)
    stacked = g.ndim == 3

    def body(w_ref, g_ref, m_ref, v_ref, go_ref, d_ref, mo_ref, vo_ref):
        gv = _slot_sum(g_ref) if stacked else g_ref[...]
        mn = B1 * m_ref[...] + (1.0 - B1) * gv
        vn = B2 * v_ref[...] + (1.0 - B2) * jnp.square(gv)
        m_hat = mn / (1.0 - B1 ** STEP)
        v_hat = vn / (1.0 - B2 ** STEP)
        go_ref[...] = gv
        d_ref[...] = -LR * (m_hat / (jnp.sqrt(v_hat) + AEPS) + WD * w_ref[...])
        mo_ref[...] = mn
        vo_ref[...] = vn

    tile = pl.BlockSpec((tr, c), lambda i: (i, 0))
    slab = tile if layer is None else pl.BlockSpec((None, tr, c), lambda i: (layer, i, 0))
    g_spec = pl.BlockSpec((g.shape[0], tr, c), lambda i: (0, i, 0)) if stacked else tile
    return pl.pallas_call(
        body, name=name, grid=(r // tr,), in_specs=[slab, g_spec, slab, slab], out_specs=[tile] * 4,
        out_shape=[jax.ShapeDtypeStruct((r, c), F32)] * 4,
    )(w, g, m, v)


def _place():
    return lax.axis_index("x"), lax.axis_index("y"), lax.axis_index("c")


def all_gather(name, x, in_vmem):
    r, c = x.shape
    space = pltpu.VMEM if in_vmem else pl.ANY

    def body(x_ref, out_ref, send_sems, recv_sems, local_sem):
        px, py, pc = _place()
        me, sibling = (px, py, pc), (px, py, 1 - pc)
        chips = [(1 - px, py), (px, 1 - py), (1 - px, 1 - py)]

        def rows(qx, qy, qc):
            return out_ref.at[pl.ds((4 * qx + 2 * qy + qc) * r, r), :]

        def copy(k, block, to, src=None):
            return pltpu.make_async_remote_copy(
                src_ref=rows(*block) if src is None else src, dst_ref=rows(*block),
                send_sem=send_sems.at[k], recv_sem=recv_sems.at[k], device_id=to, device_id_type=MESH)

        mine = pltpu.make_async_copy(x_ref, rows(*me), local_sem)
        mine.start()
        first = [copy(0, me, sibling, src=x_ref)]
        first += [copy(1 + j, me, (*chip, pc), src=x_ref) for j, chip in enumerate(chips)]
        for cp in first:
            cp.start()
        passed = [copy(4 + j, (*chip, pc), sibling) for j, chip in enumerate(chips)]
        for j, chip in enumerate(chips):
            copy(1 + j, (*chip, pc), me).wait_recv()
            passed[j].start()
        copy(0, sibling, me).wait_recv()
        for j, chip in enumerate(chips):
            copy(4 + j, (*chip, 1 - pc), me).wait_recv()
        for cp in first + passed:
            cp.wait_send()
        mine.wait()

    return pl.pallas_call(
        body, name=name, out_shape=jax.ShapeDtypeStruct((N_DEV * r, c), x.dtype),
        in_specs=[pl.BlockSpec(memory_space=space)], out_specs=pl.BlockSpec(memory_space=space),
        scratch_shapes=[pltpu.SemaphoreType.DMA((7,)), pltpu.SemaphoreType.DMA((7,)), pltpu.SemaphoreType.DMA],
    )(x)


_HBM =pl.BlockSpec(memory_space=pltpu.HBM)
_SEM = pl.BlockSpec(memory_space=pltpu.SEMAPHORE)
_EFFECT = pltpu.SideEffectType.DATAFLOW_SIDE_EFFECTING


def _peers():
    px, py, pc = _place()
    return [(1 - px if k & 4 else px, 1 - py if k & 2 else py, 1 - pc if k & 1 else pc) for k in range(1, N_DEV)]


def _slot(dev):
    return 4 * dev[0] + 2 * dev[1] + dev[2]


def _split_copies(src_refs, land_refs, send_sems, recv_sems, gather):
    me = _slot(_place())
    return [pltpu.make_async_remote_copy(
        src_ref=src if gather else src.at[_slot(peer)], dst_ref=land.at[me],
        send_sem=send_sems.at[a * (N_DEV - 1) + k], recv_sem=recv_sems.at[a * (N_DEV - 1) + k],
        device_id=peer, device_id_type=MESH)
        for a, (src, land) in enumerate(zip(src_refs, land_refs)) for k, peer in enumerate(_peers())]


def exchange_start(name, srcs, gather):
    n = len(srcs)
    lands = [pltpu.HBM((N_DEV,) + s.shape if gather else s.shape, s.dtype) for s in srcs]

    def body(*refs):
        send_sems, recv_sems = refs[2 * n:2 * n + 2]
        for cp in _split_copies(refs[:n], refs[n:2 * n], send_sems, recv_sems, gather):
            cp.start()
        refs[-1][...] = jnp.zeros_like(refs[-1])

    sems = pltpu.SemaphoreType.DMA((n * (N_DEV - 1),))
    res = pl.pallas_call(
        body, name=name,
        out_shape=(sems, sems, *[pltpu.HBM(s.shape, s.dtype) for s in srcs], *lands, jax.ShapeDtypeStruct((8, 128), F32)),
        in_specs=(_HBM,) * (2 * n), out_specs=(_SEM, _SEM) + (_HBM,) * (2 * n) + (pl.BlockSpec(memory_space=pltpu.VMEM),),
        input_output_aliases={i: 2 + i for i in range(2 * n)},
        compiler_params=pltpu.CompilerParams(has_side_effects=_EFFECT),
    )(*[pltpu.with_memory_space_constraint(s, pltpu.HBM) for s in srcs],
      *[pltpu.with_memory_space_constraint(lax.empty(ld.shape, ld.dtype), pltpu.HBM) for ld in lands])
    return res[0], res[1], list(res[2:2 + n]), list(res[2 + n:2 + 2 * n]), res[-1]


def exchange_wait(name, started, after, gather):
    send_sems, recv_sems, srcs, lands, _ = started
    n = len(srcs)

    def body(*refs):
        send_sems, recv_sems = refs[2 * n:2 * n + 2]
        for cp in _split_copies(refs[:n], refs[n:2 * n], send_sems, recv_sems, gather):
            cp.wait_send()
            cp.wait_recv()

    res = pl.pallas_call(
        body, name=name, out_shape=tuple(pltpu.HBM(a.shape, a.dtype) for a in srcs + lands),
        in_specs=(_HBM,) * (2 * n) + (_SEM, _SEM, pl.BlockSpec(memory_space=pl.ANY)), out_specs=(_HBM,) * (2 * n),
        input_output_aliases={i: i for i in range(2 * n)},
        compiler_params=pltpu.CompilerParams(has_side_effects=_EFFECT),
    )(*srcs, *lands, send_sems, recv_sems, after)
    return list(res[:n]), list(res[n:])


NCF = FFN_H // FFN_TC


def _size(shape):
    n = 1
    for s in shape:
        n *= s
    return n


def _padded_rows(n_elems, row_mult):
    return -(-n_elems // (D * row_mult)) * row_mult


def _pack_rows(arrs, dtype, row_mult):
    rows, offs, r0 = [], [], 0
    for a in arrs:
        flat = a.reshape(-1).astype(dtype)
        n = _padded_rows(flat.shape[0], row_mult)
        rows.append(jnp.pad(flat, (0, n * D - flat.shape[0])).reshape(n, D))
        offs.append(r0)
        r0 += n
    return jnp.concatenate(rows, 0), offs


def _unpack_rows(buf, offs, shapes):
    lead, out = buf.shape[:-2], []
    for o, shp in zip(offs, shapes):
        n = _size(shp)
        nr = -(-n // D)
        out.append(buf[..., o:o + nr, :].reshape(lead + (nr * D,))[..., :n].reshape(lead + tuple(shp)))
    return out


def _cols_from_shards(g):
    return g.transpose(1, 0, 2).reshape(g.shape[1], N_DEV * g.shape[2])


def _cols_to_shards(w):
    k, n = w.shape[0], w.shape[1] // N_DEV
    return w.reshape(k, N_DEV, n).transpose(1, 0, 2)


def _rows3(w):
    return [w[i:i + 1] for i in range(3)]


def f_mod1(xs, ps):
    return f_mod(xs, ps)[:1]


def kernel(x, c, ctx, c_ctx, ada_w, ada_b, norm_mix, norm_ffn, gla_w_in, gla_w_a2, gla_b_a, gla_head_norm, gla_w_out, sc_w_in, sc_conv_w, sc_w_out, ffn_w_up, ffn_conv_w, ffn_conv_b, ffn_w_down, final_norm, loss_target, m_c_ctx, m_ada_w, m_ada_b, m_norm_mix, m_norm_ffn, m_gla_w_in, m_gla_w_a2, m_gla_b_a, m_gla_head_norm, m_gla_w_out, m_sc_w_in, m_sc_conv_w, m_sc_w_out, m_ffn_w_up, m_ffn_conv_w, m_ffn_conv_b, m_ffn_w_down, m_final_norm, v_c_ctx, v_ada_w, v_ada_b, v_norm_mix, v_norm_ffn, v_gla_w_in, v_gla_w_a2, v_gla_b_a, v_gla_head_norm, v_gla_w_out, v_sc_w_in, v_sc_conv_w, v_sc_w_out, v_ffn_w_up, v_ffn_conv_w, v_ffn_conv_b, v_ffn_w_down, v_final_norm):
    names = ["c_ctx", "ada_w", "ada_b", "norm_mix", "norm_ffn", "gla_w_in", "gla_w_a2", "gla_b_a", "gla_head_norm",
             "gla_w_out", "sc_w_in", "sc_conv_w", "sc_w_out", "ffn_w_up", "ffn_conv_w", "ffn_conv_b", "ffn_w_down",
             "final_norm"]
    w_ = dict(zip(names, [c_ctx, ada_w, ada_b, norm_mix, norm_ffn, gla_w_in, gla_w_a2, gla_b_a, gla_head_norm, gla_w_out,
                          sc_w_in, sc_conv_w, sc_w_out, ffn_w_up, ffn_conv_w, ffn_conv_b, ffn_w_down, final_norm]))
    m_ = dict(zip(names, [m_c_ctx, m_ada_w, m_ada_b, m_norm_mix, m_norm_ffn, m_gla_w_in, m_gla_w_a2, m_gla_b_a,
                          m_gla_head_norm, m_gla_w_out, m_sc_w_in, m_sc_conv_w, m_sc_w_out, m_ffn_w_up, m_ffn_conv_w,
                          m_ffn_conv_b, m_ffn_w_down, m_final_norm]))
    v_ = dict(zip(names, [v_c_ctx, v_ada_w, v_ada_b, v_norm_mix, v_norm_ffn, v_gla_w_in, v_gla_w_a2, v_gla_b_a,
                          v_gla_head_norm, v_gla_w_out, v_sc_w_in, v_sc_conv_w, v_sc_w_out, v_ffn_w_up, v_ffn_conv_w,
                          v_ffn_conv_b, v_ffn_w_down, v_final_norm]))
    me = 4 * lax.axis_index("x") + 2 * lax.axis_index("y") + lax.axis_index("c")
    bsz = x.shape[0]
    tm = 256
    nt = SEQ // tm
    ctx_tiles = CTX // tm
    pe = functools.partial(P, per_example=True)

    small_sharded = [c, gla_w_a2, gla_b_a, sc_conv_w, ffn_conv_w]
    pack0, offs0 = _pack_rows(small_sharded, F32, 8)
    g0 = all_gather("ag_small", pack0, True).reshape(N_DEV, pack0.shape[0], D)
    c_all, wa2_s, ba_s, scw_s, fcw_s = _unpack_rows(g0, offs0, [a.shape for a in small_sharded])
    w_a2 = wa2_s[:, 0].transpose(1, 2, 0, 3).reshape(2, RANK, KD)
    b_a = ba_s[:, 0].transpose(1, 0, 2).reshape(2, KD)
    sc_cw = scw_s[:, 0].transpose(1, 0, 2).reshape(3, D)
    ffn_cw = fcw_s.transpose(1, 2, 0, 3).reshape(2, 3, 2 * FFN_H)

    cond = jnp.concatenate([c_all.reshape(N_DEV * bsz, D), c_ctx[None], jnp.zeros((ADA_ROWS - N_DEV * bsz - 1, D), F32)], 0)
    b_mine = lax.dynamic_slice(ada_b, (0, me * ADA_COLS), (2, ADA_COLS)).reshape(2, 1, ADA_COLS)
    mod_part = ada_fwd(cond, ada_w, b_mine)
    mod = all_gather("ag_mod", mod_part.reshape(2 * ADA_ROWS, ADA_COLS), True)
    mod = mod.reshape(N_DEV, 2, ADA_ROWS, ADA_COLS).transpose(1, 2, 0, 3).reshape(2, ADA_ROWS, 6 * D)
    mods = lax.dynamic_slice(mod, (0, bsz * me, 0), (2, bsz, 6 * D))
    md = [[mods[i][:, k * D:(k + 1) * D].reshape(bsz, 1, D) for k in range(6)] for i in range(2)]
    mc = [mod[0, ADA_CTX_ROW, k * D:(k + 1) * D][None] for k in range(2)]

    groups = {"gla": [("gla_w_in", 0), ("gla_w_out", 0)], "ffn0": [("ffn_w_up", 0), ("ffn_w_down", 0)],
              "l1": [("sc_w_in", 0), ("sc_w_out", 0), ("ffn_w_up", 1), ("ffn_w_down", 1)]}

    ag_started, tok = {}, 0.0
    for g, keys in groups.items():
        srcs = [w_[n][i].astype(BF16) for n, i in keys]
        *srcs, _ = lax.optimization_barrier((*srcs, mod))
        ag_started[g] = exchange_start(f"ag_{g}_start", srcs, True)
        tok = tok + ag_started[g][4][0, 0]
    norm_mix = norm_mix + tok

    def gathered(g, after):
        mine, lands = exchange_wait(f"ag_{g}_wait", ag_started[g], after, True)
        return [lax.dynamic_update_index_in_dim(ld, mn, me, 0) for ld, mn in zip(lands, mine)]

    s_up, w_down = [None, None], [None, None]
    wd = jnp.zeros((128, 2 * KD), F32).at[:RANK, :KD].set(w_a2[0]).at[RANK:2 * RANK, KD:].set(w_a2[1])
    bd = b_a.reshape(1, 2 * KD)
    scw = _rows3(sc_cw)
    head_gain = gla_head_norm.reshape(1, HV)
    gains_mix = [norm_mix[i][None] for i in range(2)]
    gains_ffn = [norm_ffn[i][None] for i in range(2)]

    def tokens(a2d, t_len):
        return a2d.reshape(bsz, t_len, -1)

    def ffn_params(i):
        rows = [ffn_cw[i][t] for t in range(3)] + [ffn_conv_b[i]]
        return [P(a.reshape(2, FFN_H), w=FFN_TC, rows=True) for a in rows]

    def ffn_fwd(i, hn2):
        u = mm(f"ffn_up{i}", V(hn2, "tok"), V(s_up[i], "cols"), out="planes", out_dtype=BF16, planes_t=SEQ)
        act = rowwise(f"ffn_mid{i}", f_ffn_mid, [X(u, w=FFN_TC, planes=True)], ffn_params(i), tm=SEQ, nt=1, nc=NCF,
                      outs=[(FFN_TC, BF16, 1)])[0]
        return u, act, tokens(mm(f"ffn_down{i}", V(act, "tok"), V(w_down[i])), SEQ)

    def res_mod_fwd(name, h, y, ps):
        return rowwise(name, f_res_mod, [X(h), X(y)], ps, tm=tm, nt=nt, outs=[(D, F32, 1), (D, BF16, 1)])

    ps_in0 = [P(gains_mix[0]), pe(md[0][0]), pe(md[0][1])]
    ps_ctx = [P(gains_mix[0]), P(mc[0]), P(mc[1])]
    hn0 = rowwise("mod_in0", f_mod, [X(x)], ps_in0, tm=tm, nt=nt, outs=[(D, BF16, 1)])[0]
    hnc = rowwise("mod_ctx", f_mod, [X(ctx)], ps_ctx, tm=tm, nt=ctx_tiles, outs=[(D, BF16, 1)])[0]
    hcat = jnp.concatenate([hnc, hn0], axis=1)
    s_gin, s_gout = gathered("gla", hcat)
    w_gin = jnp.pad(_cols_from_shards(s_gin), ((0, 0), (0, GLA_IN_PAD - GLA_IN)))
    w_gout = s_gout.reshape(VD, D)
    pcat = tokens(mm("gla_in", V(hcat, "tok"), V(w_gin)), TT)
    pa_x = X(pcat, w=128, co=(GLA_IN_PAD - 128) // 128)
    la = rowwise("gla_decay", f_decay, [pa_x], [P(wd), P(bd)], tm=tm, nt=TT // tm, outs=[(2 * KD, F32, 1)])[0]
    o2, s_all = gla_fwd(pcat, la)
    post_xs = [X(o2, w=VD, co=0, ro=ctx_tiles, split=HEADS), X(o2, w=VD, co=1, ro=ctx_tiles, split=HEADS),
               X(pcat, w=VD, co=2, ro=ctx_tiles, split=HEADS)]
    yin0 = rowwise("gla_post", f_gla_post, post_xs, [P(head_gain)], tm=tm, nt=nt, outs=[(VD, BF16, HEADS)])[0]
    y0 = tokens(mm("gla_out", V(yin0, "tok"), V(w_gout)), SEQ)
    ps_mid0 = [pe(md[0][2]), P(gains_ffn[0]), pe(md[0][3]), pe(md[0][4])]
    h1_0, hn2_0 = res_mod_fwd("res_mod_mid0", x, y0, ps_mid0)
    s_up[0], s_down0 = gathered("ffn0", hn2_0)
    w_down[0] = s_down0.reshape(FFN_H, D)
    u0, act0, fo0 = ffn_fwd(0, hn2_0)
    ps_in1 = [pe(md[0][5]), P(gains_mix[1]), pe(md[1][0]), pe(md[1][1])]
    h2_0, hn1 = res_mod_fwd("res_mod_in1", h1_0, fo0, ps_in1)

    s_sin, s_sout, s_up[1], s_down1 = gathered("l1", hn1)
    w_sout, w_down[1] = s_sout.reshape(D, D), s_down1.reshape(FFN_H, D)
    p1 = tokens(mm("sc_in", V(hn1, "tok"), V(s_sin, "cols")), SEQ)
    sc_ps = [P(a) for a in scw]
    yin1 = rowwise("sc_mid", f_sc_mid, [X(p1, split=3)], sc_ps, tm=tm, nt=nt, outs=[(D, BF16, 1)])[0]
    y1 = tokens(mm("sc_out", V(yin1, "tok"), V(w_sout)), SEQ)
    ps_mid1 = [pe(md[1][2]), P(gains_ffn[1]), pe(md[1][3]), pe(md[1][4])]
    h1_1, hn2_1 = res_mod_fwd("res_mod_mid1", h2_0, y1, ps_mid1)
    u1, act1, fo1 = ffn_fwd(1, hn2_1)
    loss8, dh1_1, dfo1, dm5_1, g_final = final_loss(h1_1, fo1, md[1][5], final_norm[None], loss_target)
    loss = lax.psum(loss8[0, 0], ("x", "y", "c"))

    def ffn_bwd(i, u, act, hn2, dfo):
        dact = tokens(mm(f"ffn_down_dx{i}", V(dfo, "tok"), V(w_down[i]), form="nt", out_dtype=BF16), SEQ)
        g_down = mm(f"ffn_down_dw{i}", V(act, "tok"), V(dfo, "tok"), form="tn", out_dtype=BF16)
        r = rowwise(f"ffn_mid_bwd{i}", f_ffn_mid, [X(u, w=FFN_TC, planes=True)], ffn_params(i), tm=SEQ, nt=1, nc=NCF,
                    douts=[X(dact, w=FFN_TC)], dx={0: BF16}, dp=[0, 1, 2, 3])
        du, g_cw, g_cb = r[0], jnp.stack([a.reshape(2 * FFN_H) for a in r[1:4]]), r[4].reshape(1, 2 * FFN_H)
        dhn2 = tokens(mm(f"ffn_up_dx{i}", V(du, "planes"), V(s_up[i], "cols"), form="nt", out_dtype=BF16), SEQ)
        g_up = mm(f"ffn_up_dw{i}", V(hn2, "tok"), V(du, "planes"), form="tn", out="cols", out_dtype=BF16)
        return dhn2, g_up, row_slots(g_down), g_cw, g_cb

    def res_mod_bwd(name, h, y, ps, dh1, dhn):
        return rowwise(name, f_res_mod, [X(h), X(y)], ps, tm=tm, nt=nt, douts=[X(dh1), X(dhn)],
                       dx={0: F32, 1: BF16}, dp=[0, 1, 2, 3])

    def row_slots(g):
        return g.reshape(N_DEV, -1, g.shape[-1])

    a2a_started = {}

    def send_grads(g, slots, after=None):
        if after is not None:
            *slots, _ = lax.optimization_barrier((*slots, after))
        a2a_started[g] = exchange_start(f"a2a_{g}_start", list(slots), False)
        return a2a_started[g][4][0, 0]

    def after_start(ps, tok):
        return [dict(ps[0], a=ps[0]["a"] + tok)] + ps[1:]

    dhn2_1, g_up1, g_down1, g_fcw1, g_fcb1 = ffn_bwd(1, u1, act1, hn2_1, dfo1)
    dh2_0, dy1, dm2_1, g_nffn1, dm3_1, dm4_1 = res_mod_bwd("res_mod_mid1_bwd", h2_0, y1, ps_mid1, dh1_1, dhn2_1)
    dyin1 = tokens(mm("sc_out_dx", V(dy1, "tok"), V(w_sout), form="nt", out_dtype=BF16), SEQ)
    g_sout = row_slots(mm("sc_out_dw", V(yin1, "tok"), V(dy1, "tok"), form="tn", out_dtype=BF16))
    r = rowwise("sc_mid_bwd", f_sc_mid, [X(p1, split=3)], sc_ps, tm=tm, nt=nt, douts=[X(dyin1)], dx={0: BF16}, dp=[0, 1, 2])
    dp1, g_scw = r[0], jnp.concatenate(r[1:4], 0)
    dhn1 = tokens(mm("sc_in_dx", V(dp1, "tok"), V(s_sin, "cols"), form="nt", out_dtype=BF16), SEQ)
    g_sin = mm("sc_in_dw", V(hn1, "tok"), V(dp1, "tok"), form="tn", out="cols", out_dtype=BF16)
    tok = send_grads("l1", [g_sin, g_sout, g_up1, g_down1])
    dh1_0, dfo0, dm5_0, g_nmix1, dm0_1, dm1_1 = res_mod_bwd("res_mod_in1_bwd", h1_0, fo0, after_start(ps_in1, tok), dh2_0, dhn1)

    dhn2_0, g_up0, g_down0, g_fcw0, g_fcb0 = ffn_bwd(0, u0, act0, hn2_0, dfo0)
    tok = send_grads("ffn0", [g_up0, g_down0])
    dx_res, dy0, dm2_0, g_nffn0, dm3_0, dm4_0 = res_mod_bwd("res_mod_mid0_bwd", x, y0, after_start(ps_mid0, tok), dh1_0, dhn2_0)
    dyin0 = tokens(mm("gla_out_dx", V(dy0, "tok"), V(w_gout), form="nt", out_dtype=BF16), SEQ)
    g_gout = row_slots(mm("gla_out_dw", V(yin0, "tok"), V(dy0, "tok"), form="tn", out_dtype=BF16))
    do, dgate, g_head = rowwise("gla_post_bwd", f_gla_post, post_xs, [P(head_gain)], tm=tm, nt=nt,
                                douts=[X(dyin0, split=HEADS)], dx={0: F32, 2: BF16}, dp=[0])
    dq2, dk2, dv2, dla = gla_bwd(pcat, la, s_all, do)
    dpa, g_wd, g_bd = rowwise("gla_decay_bwd", f_decay, [pa_x], [P(wd), P(bd)], tm=tm, nt=TT // tm, douts=[X(dla)],
                              dx={0: BF16}, dp=[0, 1])
    dpcat = gla_combine(dq2, dk2, dv2, dgate, dpa)
    dhcat = tokens(mm("gla_in_dx", V(dpcat, "tok"), V(w_gin), form="nt", out_dtype=BF16), TT)
    g_gin = _cols_to_shards(mm("gla_in_dw", V(hcat, "tok"), V(dpcat, "tok"), form="tn", out_dtype=BF16)[:, :GLA_IN])
    grad_x, g_nmix0, dm0_0, dm1_0 = rowwise("mod_in0_bwd", f_mod, [X(x)], ps_in0, tm=tm, nt=nt,
                                            douts=[X(dhcat, ro=ctx_tiles), X(dx_res)], dx={0: F32}, dp=[0, 1, 2])
    g_nmix0c, dmc0, dmc1 = rowwise("mod_ctx_bwd", f_mod1, [X(ctx)], ps_ctx, tm=tm, nt=ctx_tiles, douts=[X(dhcat)],
                                   dx={}, dp=[0, 1, 2])

    zero_row = jnp.zeros((1, 4 * D), F32)
    dmod = [jnp.concatenate([jnp.concatenate([a.reshape(bsz, D) for a in dms], 1), ctx_row], 0)
            for dms, ctx_row in (([dm0_0, dm1_0, dm2_0, dm3_0, dm4_0, dm5_0], jnp.concatenate([dmc0, dmc1, zero_row], 1)),
                                 ([dm0_1, dm1_1, dm2_1, dm3_1, dm4_1, dm5_1], jnp.zeros((1, 6 * D), F32)))]
    g_wa2 = jnp.stack([g_wd[:RANK, :KD], g_wd[RANK:2 * RANK, KD:]])
    small_grads = [jnp.stack(dmod), jnp.concatenate([g_nmix0 + g_nmix0c, g_nmix1], 0), jnp.concatenate([g_nffn0, g_nffn1], 0),
                   g_head, jnp.concatenate([g_fcb0, g_fcb1], 0), g_final, g_wa2, g_bd.reshape(2, KD), g_scw,
                   jnp.stack([g_fcw0, g_fcw1])]
    pack1, offs1 = _pack_rows(small_grads, F32, 8)
    g1 = all_gather("ag_grads", pack1, True).reshape(N_DEV, pack1.shape[0], D)
    dmod_all = _unpack_rows(g1, offs1[:1], [small_grads[0].shape])[0]
    tot = _unpack_rows(sum_slots("sum_small", g1), offs1, [a.shape for a in small_grads])
    dm_rows = dmod_all[:, :, :bsz].transpose(1, 0, 2, 3).reshape(2, N_DEV * bsz, 6 * D)
    dm_full = jnp.concatenate([dm_rows, tot[0][:, bsz:], jnp.zeros((2, ADA_ROWS - N_DEV * bsz - 1, 6 * D), F32)], 1)
    dm_mine = lax.dynamic_slice(dm_full, (0, 0, me * ADA_COLS), (2, ADA_ROWS, ADA_COLS))
    g_ada_w, g_ada_b, cpart = ada_bwd(cond, dm_mine, dm_full, ada_w)
    cparts = all_gather("ag_cctx", cpart, True).reshape(N_DEV, ADA_ROWS - ADA_CTX_ROW, D)[:, 0]
    g_cctx = cctx_grad(cparts, c_ctx[None])[0]
    tok = send_grads("gla", [g_gin, g_gout], after=g_cctx)

    def my_cols(full, n):
        return lax.dynamic_slice_in_dim(full, me * n, n, axis=full.ndim - 1)

    grads = {
        "c_ctx": g_cctx, "ada_b": g_ada_b.reshape(2, 6 * D), "norm_mix": tot[1], "norm_ffn": tot[2],
        "gla_head_norm": tot[3], "ffn_conv_b": tot[4], "final_norm": tot[5].reshape(D),
        "gla_w_a2": my_cols(tot[6], KD // N_DEV)[None], "gla_b_a": my_cols(tot[7], KD // N_DEV)[None],
        "sc_conv_w": my_cols(tot[8], D // N_DEV)[None], "ffn_conv_w": my_cols(tot[9], 2 * FFN_H // N_DEV),
    }

    res_ada = adamw("adamw_ada", *[a.reshape(2 * D, ADA_COLS) for a in (ada_w, g_ada_w, m_ada_w, v_ada_w)])
    grads["c_ctx"] = g_cctx + tok
    big = ["gla_w_in", "gla_w_out", "sc_w_in", "sc_w_out", "ffn_w_up", "ffn_w_down"]
    small = [n for n in names if n not in big and n != "ada_w"]
    g_small = _pack_rows([grads[n] for n in small], F32, 8)[0]
    res_small = adamw("adamw_small", _pack_rows([w_[n] for n in small], F32, 8)[0], g_small,
                      _pack_rows([m_[n] for n in small], F32, 8)[0], _pack_rows([v_[n] for n in small], F32, 8)[0])
    offs_s = _pack_rows([w_[n] for n in small], F32, 8)[1]

    big_res, after = {}, res_small[0]
    for g in ("l1", "ffn0", "gla"):
        sent, lands = exchange_wait(f"a2a_{g}_wait", a2a_started[g], after, False)
        for (n, i), mine, land in zip(groups[g], sent, lands):
            land = lax.dynamic_update_index_in_dim(land, lax.dynamic_index_in_dim(mine, me, 0, keepdims=False), me, 0)
            big_res[(n, i)] = adamw(f"adamw_{n}{i}", w_[n], land, m_[n], v_[n], layer=i)
            after = big_res[(n, i)][0]

    out = {}
    for kind, idx in (("grad", 0), ("delta", 1), ("new_m", 2), ("new_v", 3)):
        vals = {n: jnp.stack([big_res[(n, i)][idx] for i in range(w_[n].shape[0])]) for n in big}
        vals["ada_w"] = res_ada[idx].reshape(ada_w.shape)
        vals.update(zip(small, _unpack_rows(res_small[idx], offs_s, [w_[n].shape for n in small])))
        out[kind] = [vals[n] for n in names]
    return (loss, grad_x, *out["grad"], *out["delta"], *out["new_m"], *out["new_v"])
```

```python
import functools

import jax
import jax.numpy as jnp
from jax import lax
from jax.experimental import pallas as pl
from jax.experimental.pallas import tpu as pltpu

F32 = jnp.float32
BF16 = jnp.bfloat16

N_DEV = 8
D = 1024
SEQ = 2048
CTX = 256
TT = CTX + SEQ
GRID_W = 64
CHUNK = 64
HEADS = 4
HK = 128
HV = 256
KD = 512
VD = 1024
RANK = 16
TAU = 16.0
GLA_IN = 3104
GLA_IN_PAD = 3200
FFN_H = 2560
FFN_TC = 256
EPS = 1e-6
LR, B1, B2, AEPS, WD, STEP = 0.001, 0.9, 0.999, 1e-08, 0.01, 10
MESH = pl.DeviceIdType.MESH


def V(arr, kind="flat"):
    if kind == "tok":
        return V(arr.reshape(-1, arr.shape[-1]))
    if kind == "flat":
        r, c = arr.shape
        return dict(a=arr, kind=kind, shape=(r, c), runit=r, cunit=c)
    if kind == "planes":
        bsz, _, t, ch = arr.shape
        return dict(a=arr, kind=kind, shape=(bsz * t, 2 * ch), runit=t, cunit=ch)
    _, r, n = arr.shape
    return dict(a=arr, kind=kind, shape=(r, N_DEV * n), runit=r, cunit=2 * n)


def _view_spec(v, br, bc, idx):
    if v["kind"] == "flat":
        return pl.BlockSpec((br, bc), idx)
    if v["kind"] == "planes":
        nt, nch = v["runit"] // br, v["cunit"] // bc

        def at(i, j, k):
            r, c = idx(i, j, k)
            return r // nt, c // nch, r % nt, c % nch
        return pl.BlockSpec((None, None, br, bc), at)
    assert bc == v["cunit"], (bc, v["cunit"])

    def at(i, j, k):
        r, c = idx(i, j, k)
        return c, r, 0
    return pl.BlockSpec((2, br, bc // 2), at)


def _tile(*units, cap=1536):
    for t in range(cap, 0, -128):
        if all(u % t == 0 for u in units):
            return t
    raise ValueError(units)


def _out_view(kind, rows, cols, dtype, planes_t=None):
    if kind == "flat":
        shape = (rows, cols)
    elif kind == "planes":
        shape = (rows // planes_t, 2, planes_t, cols // 2)
    else:
        shape = (N_DEV, rows, cols // N_DEV)
    return V(jax.ShapeDtypeStruct(shape, dtype), kind)


def mm(name, a, b, form="nn", out="flat", out_dtype=F32, planes_t=None):
    (m, kk) = a["shape"][::-1] if form == "tn" else a["shape"]
    n = b["shape"][0] if form == "nt" else b["shape"][1]
    assert (b["shape"][1] if form == "nt" else b["shape"][0]) == kk, (name, a["shape"], b["shape"])
    o = _out_view(out, m, n, out_dtype, planes_t)
    a_m, a_k = (a["cunit"], a["runit"]) if form == "tn" else (a["runit"], a["cunit"])
    b_k, b_n = (b["cunit"], b["runit"]) if form == "nt" else (b["runit"], b["cunit"])
    tm, tn, tk = _tile(a_m, o["runit"]), _tile(b_n, o["cunit"]), _tile(a_k, b_k)
    nk = kk // tk
    dn = (((0 if form == "tn" else 1,), (1 if form == "nt" else 0,)), ((), ()))

    def load(ref):
        if len(ref.shape) == 3:
            return jnp.concatenate([ref[0], ref[1]], axis=-1).astype(BF16)
        return ref[...].astype(BF16)

    def store(o_ref, val):
        val = val.astype(out_dtype)
        if len(o_ref.shape) == 3:
            half = val.shape[-1] // 2
            o_ref[0], o_ref[1] = val[:, :half], val[:, half:]
        else:
            o_ref[...] = val

    def body(a_ref, b_ref, o_ref, *acc):
        part = lax.dot_general(load(a_ref), load(b_ref), dn, preferred_element_type=F32)
        if nk == 1:
            store(o_ref, part)
            return
        k, acc_ref = pl.program_id(2), acc[0]

        @pl.when(k == 0)
        def _():
            acc_ref[...] = part

        @pl.when(jnp.logical_and(k > 0, k < nk - 1))
        def _():
            acc_ref[...] += part

        @pl.when(k == nk - 1)
        def _():
            store(o_ref, acc_ref[...] + part)

    if form == "tn":
        a_spec = _view_spec(a, tk, tm, lambda i, j, k: (k, i))
    else:
        a_spec = _view_spec(a, tm, tk, lambda i, j, k: (i, k))
    if form == "nt":
        b_spec = _view_spec(b, tn, tk, lambda i, j, k: (j, k))
    else:
        b_spec = _view_spec(b, tk, tn, lambda i, j, k: (k, j))
    return pl.pallas_call(
        body, name=name, grid=(m // tm, n // tn, nk),
        in_specs=[a_spec, b_spec], out_specs=_view_spec(o, tm, tn, lambda i, j, k: (i, j)), out_shape=o["a"],
        scratch_shapes=[pltpu.VMEM((tm, tn), F32)] if nk > 1 else [],
        compiler_params=pltpu.CompilerParams(dimension_semantics=("parallel", "parallel", "arbitrary")),
    )(a["a"], b["a"])


def X(arr, w=None, co=0, ro=0, split=1, planes=False):
    return dict(a=arr, w=arr.shape[-1] if w is None else w, co=co, ro=ro, split=2 if planes else split,
                mode="planes" if planes else "cols")


def P(arr, per_example=False, w=None, split=1, rows=False):
    return dict(a=arr, e=per_example, w=arr.shape[-1] if w is None else w, split=arr.shape[-2] if rows else split,
                mode="rows" if rows else "cols")


def _pieces(ref, s):
    if s["mode"] == "planes":
        return [ref[0], ref[1]]
    if s["mode"] == "rows":
        return [ref[i:i + 1, :] for i in range(s["split"])]
    w = ref.shape[-1] // s["split"]
    return [ref[:, i * w:(i + 1) * w] for i in range(s["split"])]


def _store(ref, pieces, s, accumulate=False):
    w = ref.shape[-1] // len(pieces)
    for i, p in enumerate(pieces):
        at = (i,) if s["mode"] == "planes" else (slice(i, i + 1),) if s["mode"] == "rows" else (slice(None), slice(i * w, (i + 1) * w))
        if accumulate:
            ref[at] += p.astype(ref.dtype)
        else:
            ref[at] = p.astype(ref.dtype)


def rowwise(name, f, xs, ps, *, tm, nt, nc=1, outs=None, douts=None, dx=None, dp=None):
    bsz = xs[0]["a"].shape[0]
    fwd = douts is None
    nx, np_ = len(xs), len(ps)
    douts = [] if fwd else douts
    dx = {} if fwd else dx
    dp = [] if fwd else dp

    def x_spec(s):
        if s["mode"] == "planes":
            return pl.BlockSpec((None, 2, tm, s["w"]), lambda c, b, t, s=s: (b, 0, t + s["ro"], c + s["co"]))
        return pl.BlockSpec((None, tm, s["w"]), lambda c, b, t, s=s: (b, t + s["ro"], c + s["co"]))

    def x_out(s, dt):
        if s["mode"] == "planes":
            return (jax.ShapeDtypeStruct((bsz, 2, nt * tm, nc * s["w"]), dt),
                    pl.BlockSpec((None, 2, tm, s["w"]), lambda c, b, t: (b, 0, t, c)))
        return (jax.ShapeDtypeStruct((bsz, nt * tm, nc * s["w"]), dt), pl.BlockSpec((None, tm, s["w"]), lambda c, b, t: (b, t, c)))

    def p_spec(s):
        r = s["a"].shape[-2]
        if s["e"]:
            return pl.BlockSpec((None, r, s["w"]), lambda c, b, t: (b, 0, c))
        return pl.BlockSpec((r, s["w"]), lambda c, b, t: (0, c))

    in_specs = [x_spec(s) for s in xs] + [p_spec(s) for s in ps] + [x_spec(s) for s in douts]
    operands = [s["a"] for s in xs] + [s["a"] for s in ps] + [s["a"] for s in douts]
    if fwd:
        out_modes = [dict(mode="cols", split=sp) for (_, _, sp) in outs]
        out_shape = [jax.ShapeDtypeStruct((bsz, nt * tm, nc * w), dt) for (w, dt, _) in outs]
        out_specs = [pl.BlockSpec((None, tm, w), lambda c, b, t: (b, t, c)) for (w, _, _) in outs]
    else:
        dx_outs = [x_out(xs[i], dt) for i, dt in dx.items()]
        out_shape, out_specs = [o[0] for o in dx_outs], [o[1] for o in dx_outs]
        for j in dp:
            s = ps[j]
            r = s["a"].shape[-2]
            if s["e"]:
                out_shape.append(jax.ShapeDtypeStruct((bsz, r, nc * s["w"]), F32))
                out_specs.append(pl.BlockSpec((None, r, s["w"]), lambda c, b, t: (b, 0, c)))
            else:
                out_shape.append(jax.ShapeDtypeStruct((r, nc * s["w"]), F32))
                out_specs.append(pl.BlockSpec((r, s["w"]), lambda c, b, t: (0, c)))

    def body(*refs):
        x_refs, p_refs = refs[:nx], refs[nx:nx + np_]
        d_refs = refs[nx + np_:nx + np_ + len(douts)]
        o_refs = refs[nx + np_ + len(douts):]
        xv = [[p.astype(F32) for p in _pieces(r, s)] for r, s in zip(x_refs, xs)]
        pv = [[p.astype(F32) for p in _pieces(r, s)] for r, s in zip(p_refs, ps)]
        if fwd:
            for r, pieces, s in zip(o_refs, f(xv, pv), out_modes):
                _store(r, pieces, s)
            return
        _, vjp = jax.vjp(f, xv, pv)
        cot = [[p.astype(F32) for p in _pieces(r, s)] for r, s in zip(d_refs, douts)]
        dxv, dpv = vjp(cot)
        for r, i in zip(o_refs, dx):
            _store(r, dxv[i], xs[i])
        b, t = pl.program_id(1), pl.program_id(2)
        for r, j in zip(o_refs[len(dx):], dp):
            first = (t == 0) if ps[j]["e"] else jnp.logical_and(b == 0, t == 0)

            @pl.when(first)
            def _(r=r, j=j):
                _store(r, dpv[j], ps[j])

            @pl.when(jnp.logical_not(first))
            def _(r=r, j=j):
                _store(r, dpv[j], ps[j], accumulate=True)

    res = pl.pallas_call(
        body, name=name, grid=(nc, bsz, nt), in_specs=in_specs, out_specs=out_specs, out_shape=out_shape,
        compiler_params=pltpu.CompilerParams(dimension_semantics=("arbitrary", "arbitrary", "arbitrary")),
    )(*operands)
    return res


def _keep_rows(a, shift, keep):
    n = a.shape[0]
    t = lax.broadcasted_iota(jnp.int32, a.shape, 0)
    return jnp.where(keep(t, n), pltpu.roll(a, shift % n, 0), 0.0)


def _shift_pair(step, keep_prev, keep_next):
    @jax.custom_vjp
    def prev(a):
        return _keep_rows(a, step, keep_prev)

    @jax.custom_vjp
    def nxt(a):
        return _keep_rows(a, -step, keep_next)

    prev.defvjp(lambda a: (prev(a), None), lambda _, g: (nxt(g),))
    nxt.defvjp(lambda a: (nxt(a), None), lambda _, g: (prev(g),))
    return prev, nxt


prev_tok, next_tok = _shift_pair(1, lambda t, n: t % GRID_W != 0, lambda t, n: t % GRID_W != GRID_W - 1)
prev_row, next_row = _shift_pair(GRID_W, lambda t, n: t >= GRID_W, lambda t, n: t < n - GRID_W)


@jax.custom_vjp
def bdot(a, w):
    return jnp.dot(a.astype(BF16), w.astype(BF16), preferred_element_type=F32)


def _bdot_bwd(res, g):
    a, w = res
    gb = g.astype(BF16)
    da = lax.dot_general(gb, w.astype(BF16), (((1,), (1,)), ((), ())), preferred_element_type=F32)
    dw = lax.dot_general(a.astype(BF16), gb, (((0,), (0,)), ((), ())), preferred_element_type=F32)
    return da, dw


bdot.defvjp(lambda a, w: (bdot(a, w), (a, w)), _bdot_bwd)


@jax.custom_vjp
def log_sigmoid(z):
    return jnp.minimum(z, 0.0) - jnp.log(1.0 + jnp.exp(-jnp.abs(z)))


def _lsig_bwd(z, g):
    e = jnp.exp(-jnp.abs(z))
    return (g * jnp.where(z >= 0, e, 1.0) / (1.0 + e),)


log_sigmoid.defvjp(lambda z: (log_sigmoid(z), z), _lsig_bwd)


def silu(x):
    return x * jax.nn.sigmoid(x)


def _rms(x):
    return x * lax.rsqrt(jnp.mean(x * x, axis=-1, keepdims=True) + EPS)


def _mod(x, gain, shift, scale):
    return _rms(x) * gain * (1.0 + scale) + shift


def f_mod(xs, ps):
    ((h,),), ((gain,), (shift,), (scale,)) = xs, ps
    return [[_mod(h, gain, shift, scale)], [h]]


def f_res_mod(xs, ps):
    ((h,), (y,)), ((gate,), (gain,), (shift,), (scale,)) = xs, ps
    h1 = h + gate * y
    return [[h1], [_mod(h1, gain, shift, scale)]]


def f_ffn_mid(xs, ps):
    ((ua, ug),), ((w0a, w0g), (w1a, w1g), (w2a, w2g), (ba, bg)) = xs, ps
    a = w0a * prev_row(ua) + w1a * ua + w2a * next_row(ua) + ba
    g = w0g * prev_row(ug) + w1g * ug + w2g * next_row(ug) + bg
    return [[a * silu(g)]]


def f_sc_mid(xs, ps):
    ((bg, cg, v),), ((w0,), (w1,), (w2,)) = xs, ps
    z = cg * v
    return [[bg * (w0 * prev_tok(z) + w1 * z + w2 * next_tok(z))]]


def f_decay(xs, ps):
    ((a,),), ((wd,), (bd,)) = xs, ps
    return [[log_sigmoid(bdot(a, wd) + bd) / TAU]]


def f_gla_post(xs, ps):
    (of, ob, g), ((gain,),) = xs, ps
    return [[_rms(a + b) * gain * silu(c) for a, b, c in zip(of, ob, g)]]


NCH = TT // CHUNK
CTX_CH = CTX // CHUNK
_NT = (((1,), (1,)), ((), ()))
_TN = (((0,), (0,)), ((), ()))
_NN = (((1,), (0,)), ((), ()))


def _chunk_of(d, j):
    return jnp.where(d == 0, j, jnp.where(j < CTX_CH, CTX_CH - 1 - j, NCH + CTX_CH - 1 - j))


def _dot(a, b, dn):
    return lax.dot_general(a, b, dn, preferred_element_type=F32)


def _mask_dot(m, g):
    g0 = g.astype(BF16)
    r1 = g - g0.astype(F32)
    g1 = r1.astype(BF16)
    g2 = (r1 - g1.astype(F32)).astype(BF16)
    return _dot(m, g0, _NN) + _dot(m, g1, _NN) + _dot(m, g2, _NN)


def _causal(d):
    row = lax.broadcasted_iota(jnp.int32, (CHUNK, CHUNK), 0)
    col = lax.broadcasted_iota(jnp.int32, (CHUNK, CHUNK), 1)
    delta = jnp.where(d == 0, col - row, row - col)
    return delta <= 0, delta >= 0


def _gla_in_specs(rev):
    def blk(d, j):
        return _chunk_of(d, (NCH - 1 - j) if rev else j)

    return [
        pl.BlockSpec((None, CHUNK, KD), lambda b, d, j: (b, blk(d, j), 0)),
        pl.BlockSpec((None, CHUNK, KD), lambda b, d, j: (b, blk(d, j), 1)),
        pl.BlockSpec((None, CHUNK, VD), lambda b, d, j: (b, blk(d, j), 1)),
        pl.BlockSpec((None, CHUNK, KD), lambda b, d, j: (b, blk(d, j), d)),
    ], blk


def gla_fwd(pcat, la):
    bsz = pcat.shape[0]
    in_specs, blk = _gla_in_specs(False)

    def body(q_ref, k_ref, v_ref, la_ref, o_ref, s_ref, st):
        d, j = pl.program_id(1), pl.program_id(2)

        @pl.when(j == 0)
        def _():
            st[...] = jnp.zeros_like(st)

        s_ref[...] = st[...]
        causal, _ = _causal(d)
        mf = causal.astype(BF16)
        for h in range(HEADS):
            ks_, vs_ = slice(h * HK, (h + 1) * HK), slice(h * HV, (h + 1) * HV)
            q, k, v, g = q_ref[:, ks_] * (HK ** -0.5), k_ref[:, ks_], v_ref[:, vs_].astype(BF16), la_ref[:, ks_]
            b = _mask_dot(mf, g)
            bl = jnp.sum(g, axis=0, keepdims=True)
            qs = (q * jnp.exp(b)).astype(BF16)
            ks = (k * jnp.exp(-b)).astype(BF16)
            kd = (k * jnp.exp(bl - b)).astype(BF16)
            s = st[h]
            att = jnp.where(causal, _dot(qs, ks, _NT), 0.0).astype(BF16)
            o_ref[:, vs_] = _dot(qs, s.astype(BF16), _NT) + _dot(att, v, _NN)
            st[h] = jnp.exp(bl) * s + _dot(v, kd, _TN)

    return pl.pallas_call(
        body, name="gla_fwd", grid=(bsz, 2, NCH), in_specs=in_specs,
        out_specs=[pl.BlockSpec((None, CHUNK, VD), lambda b, d, j: (b, blk(d, j), d)),
                   pl.BlockSpec((None, None, None, HEADS, HV, HK), lambda b, d, j: (b, d, j, 0, 0, 0))],
        out_shape=[jax.ShapeDtypeStruct((bsz, TT, 2 * VD), F32), jax.ShapeDtypeStruct((bsz, 2, NCH, HEADS, HV, HK), F32)],
        scratch_shapes=[pltpu.VMEM((HEADS, HV, HK), F32)],
        compiler_params=pltpu.CompilerParams(dimension_semantics=("arbitrary", "arbitrary", "arbitrary")),
    )(pcat, pcat, pcat, la)


def gla_bwd(pcat, la, s_all, do):
    bsz = pcat.shape[0]
    in_specs, blk = _gla_in_specs(True)
    in_specs += [
        pl.BlockSpec((None, None, None, HEADS, HV, HK), lambda b, d, j: (b, d, NCH - 1 - j, 0, 0, 0)),
        pl.BlockSpec((None, CHUNK, VD), lambda b, d, j: (b, jnp.maximum(blk(d, j) - CTX_CH, 0), 0)),
    ]

    def body(q_ref, k_ref, v_ref, la_ref, s_ref, do_ref, dq_ref, dk_ref, dv_ref, dla_ref, dst):
        d, j = pl.program_id(1), pl.program_id(2)

        @pl.when(j == 0)
        def _():
            dst[...] = jnp.zeros_like(dst)

        latent = blk(d, j) >= CTX_CH
        causal, causal_t = _causal(d)
        mt = causal_t.astype(BF16)
        mf = causal.astype(BF16)
        scale = HK ** -0.5
        for h in range(HEADS):
            ks_, vs_ = slice(h * HK, (h + 1) * HK), slice(h * HV, (h + 1) * HV)
            q, k, v, g = q_ref[:, ks_] * scale, k_ref[:, ks_], v_ref[:, vs_].astype(BF16), la_ref[:, ks_]
            b = _mask_dot(mf, g)
            bl = jnp.sum(g, axis=0, keepdims=True)
            e, ei, ed, el = jnp.exp(b), jnp.exp(-b), jnp.exp(bl - b), jnp.exp(bl)
            qs, ks, kd = q * e, k * ei, k * ed
            qsb, ksb, kdb = qs.astype(BF16), ks.astype(BF16), kd.astype(BF16)
            s, ds1 = s_ref[h], dst[h]
            sb, ds1b = s.astype(BF16), ds1.astype(BF16)
            dob = jnp.where(latent, do_ref[:, vs_], 0.0).astype(BF16)
            att = jnp.where(causal, _dot(qsb, ksb, _NT), 0.0).astype(BF16)
            datt = jnp.where(causal, _dot(dob, v, _NT), 0.0).astype(BF16)
            dqs = _dot(dob, sb, _NN) + _dot(datt, ksb, _NN)
            dks = _dot(datt, qsb, _TN)
            dv_ref[:, vs_] = _dot(att, dob, _TN) + _dot(kdb, ds1b, _NT)
            dkd = _dot(v, ds1b, _NN)
            dst[h] = _dot(dob, qsb, _TN) + el * ds1
            del_ = jnp.sum(s * ds1, axis=0, keepdims=True)
            dq_ref[:, ks_] = dqs * e * scale
            dk_ref[:, ks_] = dks * ei + dkd * ed
            db = dqs * qs - dks * ks - dkd * kd
            dbl = jnp.sum(dkd * kd, axis=0, keepdims=True) + del_ * el
            dla_ref[:, ks_] = _mask_dot(mt, db) + dbl

    return pl.pallas_call(
        body, name="gla_bwd", grid=(bsz, 2, NCH), in_specs=in_specs,
        out_specs=[pl.BlockSpec((None, None, CHUNK, KD), lambda b, d, j: (d, b, blk(d, j), 0)),
                   pl.BlockSpec((None, None, CHUNK, KD), lambda b, d, j: (d, b, blk(d, j), 0)),
                   pl.BlockSpec((None, None, CHUNK, VD), lambda b, d, j: (d, b, blk(d, j), 0)),
                   pl.BlockSpec((None, CHUNK, KD), lambda b, d, j: (b, blk(d, j), d))],
        out_shape=[jax.ShapeDtypeStruct((2, bsz, TT, KD), F32), jax.ShapeDtypeStruct((2, bsz, TT, KD), F32),
                   jax.ShapeDtypeStruct((2, bsz, TT, VD), F32), jax.ShapeDtypeStruct((bsz, TT, 2 * KD), F32)],
        scratch_shapes=[pltpu.VMEM((HEADS, HV, HK), F32)],
        compiler_params=pltpu.CompilerParams(dimension_semantics=("arbitrary", "arbitrary", "arbitrary")),
    )(pcat, pcat, pcat, la, s_all, do)


def gla_combine(dq2, dk2, dv2, dgate, dpa):
    bsz = dgate.shape[0]
    tm = CTX

    def body(dq_ref, dk_ref, dv_ref, dg_ref, dpa_ref, o_ref):
        t = pl.program_id(1)
        o_ref[:, 0:KD] = (dq_ref[0] + dq_ref[1]).astype(BF16)
        o_ref[:, KD:2 * KD] = (dk_ref[0] + dk_ref[1]).astype(BF16)
        o_ref[:, 2 * KD:2 * KD + VD] = (dv_ref[0] + dv_ref[1]).astype(BF16)
        o_ref[:, 2 * KD + VD:2 * KD + 2 * VD] = jnp.where(t > 0, dg_ref[...], 0).astype(BF16)
        o_ref[:, 2 * KD + 2 * VD:] = dpa_ref[...].astype(BF16)

    return pl.pallas_call(
        body, name="gla_combine", grid=(bsz, TT // tm),
        in_specs=[pl.BlockSpec((2, None, tm, KD), lambda b, t: (0, b, t, 0)),
                  pl.BlockSpec((2, None, tm, KD), lambda b, t: (0, b, t, 0)),
                  pl.BlockSpec((2, None, tm, VD), lambda b, t: (0, b, t, 0)),
                  pl.BlockSpec((None, tm, VD), lambda b, t: (b, jnp.maximum(t - 1, 0), 0)),
                  pl.BlockSpec((None, tm, 128), lambda b, t: (b, t, 0))],
        out_specs=pl.BlockSpec((None, tm, GLA_IN_PAD), lambda b, t: (b, t, 0)),
        out_shape=jax.ShapeDtypeStruct((bsz, TT, GLA_IN_PAD), BF16),
        compiler_params=pltpu.CompilerParams(dimension_semantics=("arbitrary", "arbitrary")),
    )(dq2, dk2, dv2, dgate, dpa)


def final_loss(h1, fo, gate, gain, tgt):
    bsz, t_len, _ = h1.shape
    tm = 256

    def body(h_ref, f_ref, gate_ref, gain_ref, tgt_ref, loss_ref, dh_ref, df_ref, dgate_ref, dgain_ref):
        b, t = pl.program_id(0), pl.program_id(1)
        target = tgt_ref[...]

        def core(h, fo_, gate_, gain_):
            e = _rms(h + gate_ * fo_) * gain_ - target
            return jnp.sum(0.5 * jnp.sum(e * e, axis=-1, keepdims=True) / D, axis=0, keepdims=True)

        loss, vjp = jax.vjp(core, h_ref[...], f_ref[...], gate_ref[...], gain_ref[...])
        dh, df, dgate, dgain = vjp(jnp.ones((1, 1), F32))
        dh_ref[...] = dh
        df_ref[...] = df.astype(BF16)
        first = jnp.logical_and(b == 0, t == 0)

        @pl.when(first)
        def _():
            loss_ref[...] = jnp.broadcast_to(loss, loss_ref.shape)
            dgain_ref[...] = dgain

        @pl.when(jnp.logical_not(first))
        def _():
            loss_ref[...] += jnp.broadcast_to(loss, loss_ref.shape)
            dgain_ref[...] += dgain

        @pl.when(t == 0)
        def _():
            dgate_ref[...] = dgate

        @pl.when(t > 0)
        def _():
            dgate_ref[...] += dgate

    tile = pl.BlockSpec((None, tm, D), lambda b, t: (b, t, 0))
    per_ex = pl.BlockSpec((None, 1, D), lambda b, t: (b, 0, 0))
    shared = pl.BlockSpec((1, D), lambda b, t: (0, 0))
    return pl.pallas_call(
        body, name="final_loss", grid=(bsz, t_len // tm),
        in_specs=[tile, tile, per_ex, shared, tile],
        out_specs=[pl.BlockSpec((8, 128), lambda b, t: (0, 0)), tile, tile, per_ex, shared],
        out_shape=[jax.ShapeDtypeStruct((8, 128), F32), jax.ShapeDtypeStruct(h1.shape, F32),
                   jax.ShapeDtypeStruct(h1.shape, BF16), jax.ShapeDtypeStruct((bsz, 1, D), F32),
                   jax.ShapeDtypeStruct((1, D), F32)],
        compiler_params=pltpu.CompilerParams(dimension_semantics=("arbitrary", "arbitrary")),
    )(h1, fo, gate, gain, tgt)


ADA_ROWS = 24
ADA_CTX_ROW = 16
ADA_COLS = 6 * D // N_DEV


def ada_fwd(cond, w, b):
    def body(c_ref, w_ref, b_ref, o_ref):
        s = silu(c_ref[...]).astype(BF16)
        o_ref[...] = jnp.dot(s, w_ref[...].astype(BF16), preferred_element_type=F32) + b_ref[...]

    return pl.pallas_call(
        body, name="ada_fwd", grid=(2,),
        in_specs=[pl.BlockSpec((ADA_ROWS, D), lambda i: (0, 0)), pl.BlockSpec((None, D, ADA_COLS), lambda i: (i, 0, 0)),
                  pl.BlockSpec((None, 1, ADA_COLS), lambda i: (i, 0, 0))],
        out_specs=pl.BlockSpec((None, ADA_ROWS, ADA_COLS), lambda i: (i, 0, 0)),
        out_shape=jax.ShapeDtypeStruct((2, ADA_ROWS, ADA_COLS), F32),
    )(cond, w, b)


def ada_bwd(cond, dm_mine, dm_full, w):
    def body(c_ref, dm_ref, dmf_ref, w_ref, gw_ref, gb_ref, cp_ref):
        i = pl.program_id(0)
        s = silu(c_ref[...]).astype(BF16)
        dm = dm_ref[...].astype(BF16)
        gw_ref[...] = _dot(s, dm, _TN)
        gb_ref[...] = jnp.sum(dmf_ref[...], axis=0, keepdims=True)

        @pl.when(i == 0)
        def _():
            cp_ref[...] = _dot(dm_ref[ADA_CTX_ROW:, :].astype(BF16), w_ref[...].astype(BF16), _NT)

    return pl.pallas_call(
        body, name="ada_bwd", grid=(2,),
        in_specs=[pl.BlockSpec((ADA_ROWS, D), lambda i: (0, 0)), pl.BlockSpec((None, ADA_ROWS, ADA_COLS), lambda i: (i, 0, 0)),
                  pl.BlockSpec((None, ADA_ROWS, 6 * D), lambda i: (i, 0, 0)), pl.BlockSpec((None, D, ADA_COLS), lambda i: (i, 0, 0))],
        out_specs=[pl.BlockSpec((None, D, ADA_COLS), lambda i: (i, 0, 0)), pl.BlockSpec((None, 1, 6 * D), lambda i: (i, 0, 0)),
                   pl.BlockSpec((ADA_ROWS - ADA_CTX_ROW, D), lambda i: (0, 0))],
        out_shape=[jax.ShapeDtypeStruct((2, D, ADA_COLS), F32), jax.ShapeDtypeStruct((2, 1, 6 * D), F32),
                   jax.ShapeDtypeStruct((ADA_ROWS - ADA_CTX_ROW, D), F32)],
        compiler_params=pltpu.CompilerParams(dimension_semantics=("arbitrary",)),
    )(cond, dm_mine, dm_full, w)


def cctx_grad(parts, c_ctx):
    def body(p_ref, c_ref, o_ref):
        tot = p_ref[0:1, :]
        for i in range(1, N_DEV):
            tot = tot + p_ref[i:i + 1, :]
        c = c_ref[...]
        sg = jax.nn.sigmoid(c)
        o_ref[...] = tot * sg * (1.0 + c * (1.0 - sg))

    return pl.pallas_call(body, name="cctx_grad", out_shape=jax.ShapeDtypeStruct((1, D), F32))(parts, c_ctx)


def _row_tile(r):
    for t in (512, 256, 128, 80, 64, 40, 32, 16, 8):
        if r % t == 0:
            return t
    return r


def _slot_sum(ref):
    tot = ref[0].astype(F32)
    for i in range(1, ref.shape[0]):
        tot = tot + ref[i].astype(F32)
    return tot


def sum_slots(name, x):
    s, r, c = x.shape
    tr = _row_tile(r)

    def body(x_ref, o_ref):
        o_ref[...] = _slot_sum(x_ref)

    return pl.pallas_call(
        body, name=name, grid=(r // tr,), in_specs=[pl.BlockSpec((s, tr, c), lambda i: (0, i, 0))],
        out_specs=pl.BlockSpec((tr, c), lambda i: (i, 0)), out_shape=jax.ShapeDtypeStruct((r, c), F32),
    )(x)


def adamw(name, w, g, m, v, layer=None):
    r, c = w.shape[-2:]
    tr = _row_tile(r)
    stacked = g.ndim == 3

    def body(w_ref, g_ref, m_ref, v_ref, go_ref, d_ref, mo_ref, vo_ref):
        gv = _slot_sum(g_ref) if stacked else g_ref[...]
        mn = B1 * m_ref[...] + (1.0 - B1) * gv
        vn = B2 * v_ref[...] + (1.0 - B2) * jnp.square(gv)
        m_hat = mn / (1.0 - B1 ** STEP)
        v_hat = vn / (1.0 - B2 ** STEP)
        go_ref[...] = gv
        d_ref[...] = -LR * (m_hat / (jnp.sqrt(v_hat) + AEPS) + WD * w_ref[...])
        mo_ref[...] = mn
        vo_ref[...] = vn

    tile = pl.BlockSpec((tr, c), lambda i: (i, 0))
    slab = tile if layer is None else pl.BlockSpec((None, tr, c), lambda i: (layer, i, 0))
    g_spec = pl.BlockSpec((g.shape[0], tr, c), lambda i: (0, i, 0)) if stacked else tile
    return pl.pallas_call(
        body, name=name, grid=(r // tr,), in_specs=[slab, g_spec, slab, slab], out_specs=[tile] * 4,
        out_shape=[jax.ShapeDtypeStruct((r, c), F32)] * 4,
    )(w, g, m, v)


def _place():
    return lax.axis_index("x"), lax.axis_index("y"), lax.axis_index("c")


def all_gather(name, x, in_vmem):
    r, c = x.shape
    space = pltpu.VMEM if in_vmem else pl.ANY

    def body(x_ref, out_ref, send_sems, recv_sems, local_sem):
        px, py, pc = _place()
        me, sibling = (px, py, pc), (px, py, 1 - pc)
        chips = [(1 - px, py), (px, 1 - py), (1 - px, 1 - py)]

        def rows(qx, qy, qc):
            return out_ref.at[pl.ds((4 * qx + 2 * qy + qc) * r, r), :]

        def copy(k, block, to, src=None):
            return pltpu.make_async_remote_copy(
                src_ref=rows(*block) if src is None else src, dst_ref=rows(*block),
                send_sem=send_sems.at[k], recv_sem=recv_sems.at[k], device_id=to, device_id_type=MESH)

        mine = pltpu.make_async_copy(x_ref, rows(*me), local_sem)
        mine.start()
        first = [copy(0, me, sibling, src=x_ref)]
        first += [copy(1 + j, me, (*chip, pc), src=x_ref) for j, chip in enumerate(chips)]
        for cp in first:
            cp.start()
        passed = [copy(4 + j, (*chip, pc), sibling) for j, chip in enumerate(chips)]
        for j, chip in enumerate(chips):
            copy(1 + j, (*chip, pc), me).wait_recv()
            passed[j].start()
        copy(0, sibling, me).wait_recv()
        for j, chip in enumerate(chips):
            copy(4 + j, (*chip, 1 - pc), me).wait_recv()
        for cp in first + passed:
            cp.wait_send()
        mine.wait()

    return pl.pallas_call(
        body, name=name, out_shape=jax.ShapeDtypeStruct((N_DEV * r, c), x.dtype),
        in_specs=[pl.BlockSpec(memory_space=space)], out_specs=pl.BlockSpec(memory_space=space),
        scratch_shapes=[pltpu.SemaphoreType.DMA((7,)), pltpu.SemaphoreType.DMA((7,)), pltpu.SemaphoreType.DMA],
    )(x)


_HBM =pl.BlockSpec(memory_space=pltpu.HBM)
_SEM = pl.BlockSpec(memory_space=pltpu.SEMAPHORE)
_EFFECT = pltpu.SideEffectType.DATAFLOW_SIDE_EFFECTING


def _peers():
    px, py, pc = _place()
    return [(1 - px if k & 4 else px, 1 - py if k & 2 else py, 1 - pc if k & 1 else pc) for k in range(1, N_DEV)]


def _slot(dev):
    return 4 * dev[0] + 2 * dev[1] + dev[2]


def _split_copies(src_refs, land_refs, send_sems, recv_sems, gather):
    me = _slot(_place())
    return [pltpu.make_async_remote_copy(
        src_ref=src if gather else src.at[_slot(peer)], dst_ref=land.at[me],
        send_sem=send_sems.at[a * (N_DEV - 1) + k], recv_sem=recv_sems.at[a * (N_DEV - 1) + k],
        device_id=peer, device_id_type=MESH)
        for a, (src, land) in enumerate(zip(src_refs, land_refs)) for k, peer in enumerate(_peers())]


def exchange_start(name, srcs, gather):
    n = len(srcs)
    lands = [pltpu.HBM((N_DEV,) + s.shape if gather else s.shape, s.dtype) for s in srcs]

    def body(*refs):
        send_sems, recv_sems = refs[2 * n:2 * n + 2]
        for cp in _split_copies(refs[:n], refs[n:2 * n], send_sems, recv_sems, gather):
            cp.start()
        refs[-1][...] = jnp.zeros_like(refs[-1])

    sems = pltpu.SemaphoreType.DMA((n * (N_DEV - 1),))
    res = pl.pallas_call(
        body, name=name,
        out_shape=(sems, sems, *[pltpu.HBM(s.shape, s.dtype) for s in srcs], *lands, jax.ShapeDtypeStruct((8, 128), F32)),
        in_specs=(_HBM,) * (2 * n), out_specs=(_SEM, _SEM) + (_HBM,) * (2 * n) + (pl.BlockSpec(memory_space=pltpu.VMEM),),
        input_output_aliases={i: 2 + i for i in range(2 * n)},
        compiler_params=pltpu.CompilerParams(has_side_effects=_EFFECT),
    )(*[pltpu.with_memory_space_constraint(s, pltpu.HBM) for s in srcs],
      *[pltpu.with_memory_space_constraint(lax.empty(ld.shape, ld.dtype), pltpu.HBM) for ld in lands])
    return res[0], res[1], list(res[2:2 + n]), list(res[2 + n:2 + 2 * n]), res[-1]


def exchange_wait(name, started, after, gather):
    send_sems, recv_sems, srcs, lands, _ = started
    n = len(srcs)

    def body(*refs):
        send_sems, recv_sems = refs[2 * n:2 * n + 2]
        for cp in _split_copies(refs[:n], refs[n:2 * n], send_sems, recv_sems, gather):
            cp.wait_send()
            cp.wait_recv()

    res = pl.pallas_call(
        body, name=name, out_shape=tuple(pltpu.HBM(a.shape, a.dtype) for a in srcs + lands),
        in_specs=(_HBM,) * (2 * n) + (_SEM, _SEM, pl.BlockSpec(memory_space=pl.ANY)), out_specs=(_HBM,) * (2 * n),
        input_output_aliases={i: i for i in range(2 * n)},
        compiler_params=pltpu.CompilerParams(has_side_effects=_EFFECT),
    )(*srcs, *lands, send_sems, recv_sems, after)
    return list(res[:n]), list(res[n:])


NCF = FFN_H // FFN_TC


def _size(shape):
    n = 1
    for s in shape:
        n *= s
    return n


def _padded_rows(n_elems, row_mult):
    return -(-n_elems // (D * row_mult)) * row_mult


def _pack_rows(arrs, dtype, row_mult):
    rows, offs, r0 = [], [], 0
    for a in arrs:
        flat = a.reshape(-1).astype(dtype)
        n = _padded_rows(flat.shape[0], row_mult)
        rows.append(jnp.pad(flat, (0, n * D - flat.shape[0])).reshape(n, D))
        offs.append(r0)
        r0 += n
    return jnp.concatenate(rows, 0), offs


def _unpack_rows(buf, offs, shapes):
    lead, out = buf.shape[:-2], []
    for o, shp in zip(offs, shapes):
        n = _size(shp)
        nr = -(-n // D)
        out.append(buf[..., o:o + nr, :].reshape(lead + (nr * D,))[..., :n].reshape(lead + tuple(shp)))
    return out


def _cols_from_shards(g):
    return g.transpose(1, 0, 2).reshape(g.shape[1], N_DEV * g.shape[2])


def _cols_to_shards(w):
    k, n = w.shape[0], w.shape[1] // N_DEV
    return w.reshape(k, N_DEV, n).transpose(1, 0, 2)


def _rows3(w):
    return [w[i:i + 1] for i in range(3)]


def f_mod1(xs, ps):
    return f_mod(xs, ps)[:1]


def kernel(x, c, ctx, c_ctx, ada_w, ada_b, norm_mix, norm_ffn, gla_w_in, gla_w_a2, gla_b_a, gla_head_norm, gla_w_out, sc_w_in, sc_conv_w, sc_w_out, ffn_w_up, ffn_conv_w, ffn_conv_b, ffn_w_down, final_norm, loss_target, m_c_ctx, m_ada_w, m_ada_b, m_norm_mix, m_norm_ffn, m_gla_w_in, m_gla_w_a2, m_gla_b_a, m_gla_head_norm, m_gla_w_out, m_sc_w_in, m_sc_conv_w, m_sc_w_out, m_ffn_w_up, m_ffn_conv_w, m_ffn_conv_b, m_ffn_w_down, m_final_norm, v_c_ctx, v_ada_w, v_ada_b, v_norm_mix, v_norm_ffn, v_gla_w_in, v_gla_w_a2, v_gla_b_a, v_gla_head_norm, v_gla_w_out, v_sc_w_in, v_sc_conv_w, v_sc_w_out, v_ffn_w_up, v_ffn_conv_w, v_ffn_conv_b, v_ffn_w_down, v_final_norm):
    names = ["c_ctx", "ada_w", "ada_b", "norm_mix", "norm_ffn", "gla_w_in", "gla_w_a2", "gla_b_a", "gla_head_norm",
             "gla_w_out", "sc_w_in", "sc_conv_w", "sc_w_out", "ffn_w_up", "ffn_conv_w", "ffn_conv_b", "ffn_w_down",
             "final_norm"]
    w_ = dict(zip(names, [c_ctx, ada_w, ada_b, norm_mix, norm_ffn, gla_w_in, gla_w_a2, gla_b_a, gla_head_norm, gla_w_out,
                          sc_w_in, sc_conv_w, sc_w_out, ffn_w_up, ffn_conv_w, ffn_conv_b, ffn_w_down, final_norm]))
    m_ = dict(zip(names, [m_c_ctx, m_ada_w, m_ada_b, m_norm_mix, m_norm_ffn, m_gla_w_in, m_gla_w_a2, m_gla_b_a,
                          m_gla_head_norm, m_gla_w_out, m_sc_w_in, m_sc_conv_w, m_sc_w_out, m_ffn_w_up, m_ffn_conv_w,
                          m_ffn_conv_b, m_ffn_w_down, m_final_norm]))
    v_ = dict(zip(names, [v_c_ctx, v_ada_w, v_ada_b, v_norm_mix, v_norm_ffn, v_gla_w_in, v_gla_w_a2, v_gla_b_a,
                          v_gla_head_norm, v_gla_w_out, v_sc_w_in, v_sc_conv_w, v_sc_w_out, v_ffn_w_up, v_ffn_conv_w,
                          v_ffn_conv_b, v_ffn_w_down, v_final_norm]))
    me = 4 * lax.axis_index("x") + 2 * lax.axis_index("y") + lax.axis_index("c")
    bsz = x.shape[0]
    tm = 256
    nt = SEQ // tm
    ctx_tiles = CTX // tm
    pe = functools.partial(P, per_example=True)

    small_sharded = [c, gla_w_a2, gla_b_a, sc_conv_w, ffn_conv_w]
    pack0, offs0 = _pack_rows(small_sharded, F32, 8)
    g0 = all_gather("ag_small", pack0, True).reshape(N_DEV, pack0.shape[0], D)
    c_all, wa2_s, ba_s, scw_s, fcw_s = _unpack_rows(g0, offs0, [a.shape for a in small_sharded])
    w_a2 = wa2_s[:, 0].transpose(1, 2, 0, 3).reshape(2, RANK, KD)
    b_a = ba_s[:, 0].transpose(1, 0, 2).reshape(2, KD)
    sc_cw = scw_s[:, 0].transpose(1, 0, 2).reshape(3, D)
    ffn_cw = fcw_s.transpose(1, 2, 0, 3).reshape(2, 3, 2 * FFN_H)

    cond = jnp.concatenate([c_all.reshape(N_DEV * bsz, D), c_ctx[None], jnp.zeros((ADA_ROWS - N_DEV * bsz - 1, D), F32)], 0)
    b_mine = lax.dynamic_slice(ada_b, (0, me * ADA_COLS), (2, ADA_COLS)).reshape(2, 1, ADA_COLS)
    mod_part = ada_fwd(cond, ada_w, b_mine)
    mod = all_gather("ag_mod", mod_part.reshape(2 * ADA_ROWS, ADA_COLS), True)
    mod = mod.reshape(N_DEV, 2, ADA_ROWS, ADA_COLS).transpose(1, 2, 0, 3).reshape(2, ADA_ROWS, 6 * D)
    mods = lax.dynamic_slice(mod, (0, bsz * me, 0), (2, bsz, 6 * D))
    md = [[mods[i][:, k * D:(k + 1) * D].reshape(bsz, 1, D) for k in range(6)] for i in range(2)]
    mc = [mod[0, ADA_CTX_ROW, k * D:(k + 1) * D][None] for k in range(2)]

    groups = {"gla": [("gla_w_in", 0), ("gla_w_out", 0)], "ffn0": [("ffn_w_up", 0), ("ffn_w_down", 0)],
              "l1": [("sc_w_in", 0), ("sc_w_out", 0), ("ffn_w_up", 1), ("ffn_w_down", 1)]}

    ag_started, tok = {}, 0.0
    for g, keys in groups.items():
        srcs = [w_[n][i].astype(BF16) for n, i in keys]
        *srcs, _ = lax.optimization_barrier((*srcs, mod))
        ag_started[g] = exchange_start(f"ag_{g}_start", srcs, True)
        tok = tok + ag_started[g][4][0, 0]
    norm_mix = norm_mix + tok

    def gathered(g, after):
        mine, lands = exchange_wait(f"ag_{g}_wait", ag_started[g], after, True)
        return [lax.dynamic_update_index_in_dim(ld, mn, me, 0) for ld, mn in zip(lands, mine)]

    s_up, w_down = [None, None], [None, None]
    wd = jnp.zeros((128, 2 * KD), F32).at[:RANK, :KD].set(w_a2[0]).at[RANK:2 * RANK, KD:].set(w_a2[1])
    bd = b_a.reshape(1, 2 * KD)
    scw = _rows3(sc_cw)
    head_gain = gla_head_norm.reshape(1, HV)
    gains_mix = [norm_mix[i][None] for i in range(2)]
    gains_ffn = [norm_ffn[i][None] for i in range(2)]

    def tokens(a2d, t_len):
        return a2d.reshape(bsz, t_len, -1)

    def ffn_params(i):
        rows = [ffn_cw[i][t] for t in range(3)] + [ffn_conv_b[i]]
        return [P(a.reshape(2, FFN_H), w=FFN_TC, rows=True) for a in rows]

    def ffn_fwd(i, hn2):
        u = mm(f"ffn_up{i}", V(hn2, "tok"), V(s_up[i], "cols"), out="planes", out_dtype=BF16, planes_t=SEQ)
        act = rowwise(f"ffn_mid{i}", f_ffn_mid, [X(u, w=FFN_TC, planes=True)], ffn_params(i), tm=SEQ, nt=1, nc=NCF,
                      outs=[(FFN_TC, BF16, 1)])[0]
        return u, act, tokens(mm(f"ffn_down{i}", V(act, "tok"), V(w_down[i])), SEQ)

    def res_mod_fwd(name, h, y, ps):
        return rowwise(name, f_res_mod, [X(h), X(y)], ps, tm=tm, nt=nt, outs=[(D, F32, 1), (D, BF16, 1)])

    ps_in0 = [P(gains_mix[0]), pe(md[0][0]), pe(md[0][1])]
    ps_ctx = [P(gains_mix[0]), P(mc[0]), P(mc[1])]
    hn0 = rowwise("mod_in0", f_mod, [X(x)], ps_in0, tm=tm, nt=nt, outs=[(D, BF16, 1)])[0]
    hnc = rowwise("mod_ctx", f_mod, [X(ctx)], ps_ctx, tm=tm, nt=ctx_tiles, outs=[(D, BF16, 1)])[0]
    hcat = jnp.concatenate([hnc, hn0], axis=1)
    s_gin, s_gout = gathered("gla", hcat)
    w_gin = jnp.pad(_cols_from_shards(s_gin), ((0, 0), (0, GLA_IN_PAD - GLA_IN)))
    w_gout = s_gout.reshape(VD, D)
    pcat = tokens(mm("gla_in", V(hcat, "tok"), V(w_gin)), TT)
    pa_x = X(pcat, w=128, co=(GLA_IN_PAD - 128) // 128)
    la = rowwise("gla_decay", f_decay, [pa_x], [P(wd), P(bd)], tm=tm, nt=TT // tm, outs=[(2 * KD, F32, 1)])[0]
    o2, s_all = gla_fwd(pcat, la)
    post_xs = [X(o2, w=VD, co=0, ro=ctx_tiles, split=HEADS), X(o2, w=VD, co=1, ro=ctx_tiles, split=HEADS),
               X(pcat, w=VD, co=2, ro=ctx_tiles, split=HEADS)]
    yin0 = rowwise("gla_post", f_gla_post, post_xs, [P(head_gain)], tm=tm, nt=nt, outs=[(VD, BF16, HEADS)])[0]
    y0 = tokens(mm("gla_out", V(yin0, "tok"), V(w_gout)), SEQ)
    ps_mid0 = [pe(md[0][2]), P(gains_ffn[0]), pe(md[0][3]), pe(md[0][4])]
    h1_0, hn2_0 = res_mod_fwd("res_mod_mid0", x, y0, ps_mid0)
    s_up[0], s_down0 = gathered("ffn0", hn2_0)
    w_down[0] = s_down0.reshape(FFN_H, D)
    u0, act0, fo0 = ffn_fwd(0, hn2_0)
    ps_in1 = [pe(md[0][5]), P(gains_mix[1]), pe(md[1][0]), pe(md[1][1])]
    h2_0, hn1 = res_mod_fwd("res_mod_in1", h1_0, fo0, ps_in1)

    s_sin, s_sout, s_up[1], s_down1 = gathered("l1", hn1)
    w_sout, w_down[1] = s_sout.reshape(D, D), s_down1.reshape(FFN_H, D)
    p1 = tokens(mm("sc_in", V(hn1, "tok"), V(s_sin, "cols")), SEQ)
    sc_ps = [P(a) for a in scw]
    yin1 = rowwise("sc_mid", f_sc_mid, [X(p1, split=3)], sc_ps, tm=tm, nt=nt, outs=[(D, BF16, 1)])[0]
    y1 = tokens(mm("sc_out", V(yin1, "tok"), V(w_sout)), SEQ)
    ps_mid1 = [pe(md[1][2]), P(gains_ffn[1]), pe(md[1][3]), pe(md[1][4])]
    h1_1, hn2_1 = res_mod_fwd("res_mod_mid1", h2_0, y1, ps_mid1)
    u1, act1, fo1 = ffn_fwd(1, hn2_1)
    loss8, dh1_1, dfo1, dm5_1, g_final = final_loss(h1_1, fo1, md[1][5], final_norm[None], loss_target)
    loss = lax.psum(loss8[0, 0], ("x", "y", "c"))

    def ffn_bwd(i, u, act, hn2, dfo):
        dact = tokens(mm(f"ffn_down_dx{i}", V(dfo, "tok"), V(w_down[i]), form="nt", out_dtype=BF16), SEQ)
        g_down = mm(f"ffn_down_dw{i}", V(act, "tok"), V(dfo, "tok"), form="tn", out_dtype=BF16)
        r = rowwise(f"ffn_mid_bwd{i}", f_ffn_mid, [X(u, w=FFN_TC, planes=True)], ffn_params(i), tm=SEQ, nt=1, nc=NCF,
                    douts=[X(dact, w=FFN_TC)], dx={0: BF16}, dp=[0, 1, 2, 3])
        du, g_cw, g_cb = r[0], jnp.stack([a.reshape(2 * FFN_H) for a in r[1:4]]), r[4].reshape(1, 2 * FFN_H)
        dhn2 = tokens(mm(f"ffn_up_dx{i}", V(du, "planes"), V(s_up[i], "cols"), form="nt", out_dtype=BF16), SEQ)
        g_up = mm(f"ffn_up_dw{i}", V(hn2, "tok"), V(du, "planes"), form="tn", out="cols", out_dtype=BF16)
        return dhn2, g_up, row_slots(g_down), g_cw, g_cb

    def res_mod_bwd(name, h, y, ps, dh1, dhn):
        return rowwise(name, f_res_mod, [X(h), X(y)], ps, tm=tm, nt=nt, douts=[X(dh1), X(dhn)],
                       dx={0: F32, 1: BF16}, dp=[0, 1, 2, 3])

    def row_slots(g):
        return g.reshape(N_DEV, -1, g.shape[-1])

    a2a_started = {}

    def send_grads(g, slots, after=None):
        if after is not None:
            *slots, _ = lax.optimization_barrier((*slots, after))
        a2a_started[g] = exchange_start(f"a2a_{g}_start", list(slots), False)
        return a2a_started[g][4][0, 0]

    def after_start(ps, tok):
        return [dict(ps[0], a=ps[0]["a"] + tok)] + ps[1:]

    dhn2_1, g_up1, g_down1, g_fcw1, g_fcb1 = ffn_bwd(1, u1, act1, hn2_1, dfo1)
    dh2_0, dy1, dm2_1, g_nffn1, dm3_1, dm4_1 = res_mod_bwd("res_mod_mid1_bwd", h2_0, y1, ps_mid1, dh1_1, dhn2_1)
    dyin1 = tokens(mm("sc_out_dx", V(dy1, "tok"), V(w_sout), form="nt", out_dtype=BF16), SEQ)
    g_sout = row_slots(mm("sc_out_dw", V(yin1, "tok"), V(dy1, "tok"), form="tn", out_dtype=BF16))
    r = rowwise("sc_mid_bwd", f_sc_mid, [X(p1, split=3)], sc_ps, tm=tm, nt=nt, douts=[X(dyin1)], dx={0: BF16}, dp=[0, 1, 2])
    dp1, g_scw = r[0], jnp.concatenate(r[1:4], 0)
    dhn1 = tokens(mm("sc_in_dx", V(dp1, "tok"), V(s_sin, "cols"), form="nt", out_dtype=BF16), SEQ)
    g_sin = mm("sc_in_dw", V(hn1, "tok"), V(dp1, "tok"), form="tn", out="cols", out_dtype=BF16)
    tok = send_grads("l1", [g_sin, g_sout, g_up1, g_down1])
    dh1_0, dfo0, dm5_0, g_nmix1, dm0_1, dm1_1 = res_mod_bwd("res_mod_in1_bwd", h1_0, fo0, after_start(ps_in1, tok), dh2_0, dhn1)

    dhn2_0, g_up0, g_down0, g_fcw0, g_fcb0 = ffn_bwd(0, u0, act0, hn2_0, dfo0)
    tok = send_grads("ffn0", [g_up0, g_down0])
    dx_res, dy0, dm2_0, g_nffn0, dm3_0, dm4_0 = res_mod_bwd("res_mod_mid0_bwd", x, y0, after_start(ps_mid0, tok), dh1_0, dhn2_0)
    dyin0 = tokens(mm("gla_out_dx", V(dy0, "tok"), V(w_gout), form="nt", out_dtype=BF16), SEQ)
    g_gout = row_slots(mm("gla_out_dw", V(yin0, "tok"), V(dy0, "tok"), form="tn", out_dtype=BF16))
    do, dgate, g_head = rowwise("gla_post_bwd", f_gla_post, post_xs, [P(head_gain)], tm=tm, nt=nt,
                                douts=[X(dyin0, split=HEADS)], dx={0: F32, 2: BF16}, dp=[0])
    dq2, dk2, dv2, dla = gla_bwd(pcat, la, s_all, do)
    dpa, g_wd, g_bd = rowwise("gla_decay_bwd", f_decay, [pa_x], [P(wd), P(bd)], tm=tm, nt=TT // tm, douts=[X(dla)],
                              dx={0: BF16}, dp=[0, 1])
    dpcat = gla_combine(dq2, dk2, dv2, dgate, dpa)
    dhcat = tokens(mm("gla_in_dx", V(dpcat, "tok"), V(w_gin), form="nt", out_dtype=BF16), TT)
    g_gin = _cols_to_shards(mm("gla_in_dw", V(hcat, "tok"), V(dpcat, "tok"), form="tn", out_dtype=BF16)[:, :GLA_IN])
    grad_x, g_nmix0, dm0_0, dm1_0 = rowwise("mod_in0_bwd", f_mod, [X(x)], ps_in0, tm=tm, nt=nt,
                                            douts=[X(dhcat, ro=ctx_tiles), X(dx_res)], dx={0: F32}, dp=[0, 1, 2])
    g_nmix0c, dmc0, dmc1 = rowwise("mod_ctx_bwd", f_mod1, [X(ctx)], ps_ctx, tm=tm, nt=ctx_tiles, douts=[X(dhcat)],
                                   dx={}, dp=[0, 1, 2])

    zero_row = jnp.zeros((1, 4 * D), F32)
    dmod = [jnp.concatenate([jnp.concatenate([a.reshape(bsz, D) for a in dms], 1), ctx_row], 0)
            for dms, ctx_row in (([dm0_0, dm1_0, dm2_0, dm3_0, dm4_0, dm5_0], jnp.concatenate([dmc0, dmc1, zero_row], 1)),
                                 ([dm0_1, dm1_1, dm2_1, dm3_1, dm4_1, dm5_1], jnp.zeros((1, 6 * D), F32)))]
    g_wa2 = jnp.stack([g_wd[:RANK, :KD], g_wd[RANK:2 * RANK, KD:]])
    small_grads = [jnp.stack(dmod), jnp.concatenate([g_nmix0 + g_nmix0c, g_nmix1], 0), jnp.concatenate([g_nffn0, g_nffn1], 0),
                   g_head, jnp.concatenate([g_fcb0, g_fcb1], 0), g_final, g_wa2, g_bd.reshape(2, KD), g_scw,
                   jnp.stack([g_fcw0, g_fcw1])]
    pack1, offs1 = _pack_rows(small_grads, F32, 8)
    g1 = all_gather("ag_grads", pack1, True).reshape(N_DEV, pack1.shape[0], D)
    dmod_all = _unpack_rows(g1, offs1[:1], [small_grads[0].shape])[0]
    tot = _unpack_rows(sum_slots("sum_small", g1), offs1, [a.shape for a in small_grads])
    dm_rows = dmod_all[:, :, :bsz].transpose(1, 0, 2, 3).reshape(2, N_DEV * bsz, 6 * D)
    dm_full = jnp.concatenate([dm_rows, tot[0][:, bsz:], jnp.zeros((2, ADA_ROWS - N_DEV * bsz - 1, 6 * D), F32)], 1)
    dm_mine = lax.dynamic_slice(dm_full, (0, 0, me * ADA_COLS), (2, ADA_ROWS, ADA_COLS))
    g_ada_w, g_ada_b, cpart = ada_bwd(cond, dm_mine, dm_full, ada_w)
    cparts = all_gather("ag_cctx", cpart, True).reshape(N_DEV, ADA_ROWS - ADA_CTX_ROW, D)[:, 0]
    g_cctx = cctx_grad(cparts, c_ctx[None])[0]
    tok = send_grads("gla", [g_gin, g_gout], after=g_cctx)

    def my_cols(full, n):
        return lax.dynamic_slice_in_dim(full, me * n, n, axis=full.ndim - 1)

    grads = {
        "c_ctx": g_cctx, "ada_b": g_ada_b.reshape(2, 6 * D), "norm_mix": tot[1], "norm_ffn": tot[2],
        "gla_head_norm": tot[3], "ffn_conv_b": tot[4], "final_norm": tot[5].reshape(D),
        "gla_w_a2": my_cols(tot[6], KD // N_DEV)[None], "gla_b_a": my_cols(tot[7], KD // N_DEV)[None],
        "sc_conv_w": my_cols(tot[8], D // N_DEV)[None], "ffn_conv_w": my_cols(tot[9], 2 * FFN_H // N_DEV),
    }

    res_ada = adamw("adamw_ada", *[a.reshape(2 * D, ADA_COLS) for a in (ada_w, g_ada_w, m_ada_w, v_ada_w)])
    grads["c_ctx"] = g_cctx + tok
    big = ["gla_w_in", "gla_w_out", "sc_w_in", "sc_w_out", "ffn_w_up", "ffn_w_down"]
    small = [n for n in names if n not in big and n != "ada_w"]
    g_small = _pack_rows([grads[n] for n in small], F32, 8)[0]
    res_small = adamw("adamw_small", _pack_rows([w_[n] for n in small], F32, 8)[0], g_small,
                      _pack_rows([m_[n] for n in small], F32, 8)[0], _pack_rows([v_[n] for n in small], F32, 8)[0])
    offs_s = _pack_rows([w_[n] for n in small], F32, 8)[1]

    big_res, after = {}, res_small[0]
    for g in ("l1", "ffn0", "gla"):
        sent, lands = exchange_wait(f"a2a_{g}_wait", a2a_started[g], after, False)
        for (n, i), mine, land in zip(groups[g], sent, lands):
            land = lax.dynamic_update_index_in_dim(land, lax.dynamic_index_in_dim(mine, me, 0, keepdims=False), me, 0)
            big_res[(n, i)] = adamw(f"adamw_{n}{i}", w_[n], land, m_[n], v_[n], layer=i)
            after = big_res[(n, i)][0]

    out = {}
    for kind, idx in (("grad", 0), ("delta", 1), ("new_m", 2), ("new_v", 3)):
        vals = {n: jnp.stack([big_res[(n, i)][idx] for i in range(w_[n].shape[0])]) for n in big}
        vals["ada_w"] = res_ada[idx].reshape(ada_w.shape)
        vals.update(zip(small, _unpack_rows(res_small[idx], offs_s, [w_[n].shape for n in small])))
        out[kind] = [vals[n] for n in names]
    return (loss, grad_x, *out["grad"], *out["delta"], *out["new_m"], *out["new_v"])
```

```python
import functools

import jax
import jax.numpy as jnp
from jax import lax
from jax.experimental import pallas as pl
from jax.experimental.pallas import tpu as pltpu

F32 = jnp.float32
BF16 = jnp.bfloat16

N_DEV = 8
D = 1024
SEQ = 2048
CTX = 256
TT = CTX + SEQ
GRID_W = 64
CHUNK = 64
HEADS = 4
HK = 128
HV = 256
KD = 512
VD = 1024
RANK = 16
TAU = 16.0
GLA_IN = 3104
GLA_IN_PAD = 3200
FFN_H = 2560
FFN_TC = 256
EPS = 1e-6
LR, B1, B2, AEPS, WD, STEP = 0.001, 0.9, 0.999, 1e-08, 0.01, 10
MESH = pl.DeviceIdType.MESH


def V(arr, kind="flat"):
    if kind == "tok":
        return V(arr.reshape(-1, arr.shape[-1]))
    if kind == "flat":
        r, c = arr.shape
        return dict(a=arr, kind=kind, shape=(r, c), runit=r, cunit=c)
    if kind == "planes":
        bsz, _, t, ch = arr.shape
        return dict(a=arr, kind=kind, shape=(bsz * t, 2 * ch), runit=t, cunit=ch)
    _, r, n = arr.shape
    return dict(a=arr, kind=kind, shape=(r, N_DEV * n), runit=r, cunit=2 * n)


def _view_spec(v, br, bc, idx):
    if v["kind"] == "flat":
        return pl.BlockSpec((br, bc), idx)
    if v["kind"] == "planes":
        nt, nch = v["runit"] // br, v["cunit"] // bc

        def at(i, j, k):
            r, c = idx(i, j, k)
            return r // nt, c // nch, r % nt, c % nch
        return pl.BlockSpec((None, None, br, bc), at)
    assert bc == v["cunit"], (bc, v["cunit"])

    def at(i, j, k):
        r, c = idx(i, j, k)
        return c, r, 0
    return pl.BlockSpec((2, br, bc // 2), at)


def _tile(*units, cap=1536):
    for t in range(cap, 0, -128):
        if all(u % t == 0 for u in units):
            return t
    raise ValueError(units)


def _out_view(kind, rows, cols, dtype, planes_t=None):
    if kind == "flat":
        shape = (rows, cols)
    elif kind == "planes":
        shape = (rows // planes_t, 2, planes_t, cols // 2)
    else:
        shape = (N_DEV, rows, cols // N_DEV)
    return V(jax.ShapeDtypeStruct(shape, dtype), kind)


def mm(name, a, b, form="nn", out="flat", out_dtype=F32, planes_t=None):
    (m, kk) = a["shape"][::-1] if form == "tn" else a["shape"]
    n = b["shape"][0] if form == "nt" else b["shape"][1]
    assert (b["shape"][1] if form == "nt" else b["shape"][0]) == kk, (name, a["shape"], b["shape"])
    o = _out_view(out, m, n, out_dtype, planes_t)
    a_m, a_k = (a["cunit"], a["runit"]) if form == "tn" else (a["runit"], a["cunit"])
    b_k, b_n = (b["cunit"], b["runit"]) if form == "nt" else (b["runit"], b["cunit"])
    tm, tn, tk = _tile(a_m, o["runit"]), _tile(b_n, o["cunit"]), _tile(a_k, b_k)
    nk = kk // tk
    dn = (((0 if form == "tn" else 1,), (1 if form == "nt" else 0,)), ((), ()))

    def load(ref):
        if len(ref.shape) == 3:
            return jnp.concatenate([ref[0], ref[1]], axis=-1).astype(BF16)
        return ref[...].astype(BF16)

    def store(o_ref, val):
        val = val.astype(out_dtype)
        if len(o_ref.shape) == 3:
            half = val.shape[-1] // 2
            o_ref[0], o_ref[1] = val[:, :half], val[:, half:]
        else:
            o_ref[...] = val

    def body(a_ref, b_ref, o_ref, *acc):
        part = lax.dot_general(load(a_ref), load(b_ref), dn, preferred_element_type=F32)
        if nk == 1:
            store(o_ref, part)
            return
        k, acc_ref = pl.program_id(2), acc[0]

        @pl.when(k == 0)
        def _():
            acc_ref[...] = part

        @pl.when(jnp.logical_and(k > 0, k < nk - 1))
        def _():
            acc_ref[...] += part

        @pl.when(k == nk - 1)
        def _():
            store(o_ref, acc_ref[...] + part)

    if form == "tn":
        a_spec = _view_spec(a, tk, tm, lambda i, j, k: (k, i))
    else:
        a_spec = _view_spec(a, tm, tk, lambda i, j, k: (i, k))
    if form == "nt":
        b_spec = _view_spec(b, tn, tk, lambda i, j, k: (j, k))
    else:
        b_spec = _view_spec(b, tk, tn, lambda i, j, k: (k, j))
    return pl.pallas_call(
        body, name=name, grid=(m // tm, n // tn, nk),
        in_specs=[a_spec, b_spec], out_specs=_view_spec(o, tm, tn, lambda i, j, k: (i, j)), out_shape=o["a"],
        scratch_shapes=[pltpu.VMEM((tm, tn), F32)] if nk > 1 else [],
        compiler_params=pltpu.CompilerParams(dimension_semantics=("parallel", "parallel", "arbitrary")),
    )(a["a"], b["a"])


def X(arr, w=None, co=0, ro=0, split=1, planes=False):
    return dict(a=arr, w=arr.shape[-1] if w is None else w, co=co, ro=ro, split=2 if planes else split,
                mode="planes" if planes else "cols")


def P(arr, per_example=False, w=None, split=1, rows=False):
    return dict(a=arr, e=per_example, w=arr.shape[-1] if w is None else w, split=arr.shape[-2] if rows else split,
                mode="rows" if rows else "cols")


def _pieces(ref, s):
    if s["mode"] == "planes":
        return [ref[0], ref[1]]
    if s["mode"] == "rows":
        return [ref[i:i + 1, :] for i in range(s["split"])]
    w = ref.shape[-1] // s["split"]
    return [ref[:, i * w:(i + 1) * w] for i in range(s["split"])]


def _store(ref, pieces, s, accumulate=False):
    w = ref.shape[-1] // len(pieces)
    for i, p in enumerate(pieces):
        at = (i,) if s["mode"] == "planes" else (slice(i, i + 1),) if s["mode"] == "rows" else (slice(None), slice(i * w, (i + 1) * w))
        if accumulate:
            ref[at] += p.astype(ref.dtype)
        else:
            ref[at] = p.astype(ref.dtype)


def rowwise(name, f, xs, ps, *, tm, nt, nc=1, outs=None, douts=None, dx=None, dp=None):
    bsz = xs[0]["a"].shape[0]
    fwd = douts is None
    nx, np_ = len(xs), len(ps)
    douts = [] if fwd else douts
    dx = {} if fwd else dx
    dp = [] if fwd else dp

    def x_spec(s):
        if s["mode"] == "planes":
            return pl.BlockSpec((None, 2, tm, s["w"]), lambda c, b, t, s=s: (b, 0, t + s["ro"], c + s["co"]))
        return pl.BlockSpec((None, tm, s["w"]), lambda c, b, t, s=s: (b, t + s["ro"], c + s["co"]))

    def x_out(s, dt):
        if s["mode"] == "planes":
            return (jax.ShapeDtypeStruct((bsz, 2, nt * tm, nc * s["w"]), dt),
                    pl.BlockSpec((None, 2, tm, s["w"]), lambda c, b, t: (b, 0, t, c)))
        return (jax.ShapeDtypeStruct((bsz, nt * tm, nc * s["w"]), dt), pl.BlockSpec((None, tm, s["w"]), lambda c, b, t: (b, t, c)))

    def p_spec(s):
        r = s["a"].shape[-2]
        if s["e"]:
            return pl.BlockSpec((None, r, s["w"]), lambda c, b, t: (b, 0, c))
        return pl.BlockSpec((r, s["w"]), lambda c, b, t: (0, c))

    in_specs = [x_spec(s) for s in xs] + [p_spec(s) for s in ps] + [x_spec(s) for s in douts]
    operands = [s["a"] for s in xs] + [s["a"] for s in ps] + [s["a"] for s in douts]
    if fwd:
        out_modes = [dict(mode="cols", split=sp) for (_, _, sp) in outs]
        out_shape = [jax.ShapeDtypeStruct((bsz, nt * tm, nc * w), dt) for (w, dt, _) in outs]
        out_specs = [pl.BlockSpec((None, tm, w), lambda c, b, t: (b, t, c)) for (w, _, _) in outs]
    else:
        dx_outs = [x_out(xs[i], dt) for i, dt in dx.items()]
        out_shape, out_specs = [o[0] for o in dx_outs], [o[1] for o in dx_outs]
        for j in dp:
            s = ps[j]
            r = s["a"].shape[-2]
            if s["e"]:
                out_shape.append(jax.ShapeDtypeStruct((bsz, r, nc * s["w"]), F32))
                out_specs.append(pl.BlockSpec((None, r, s["w"]), lambda c, b, t: (b, 0, c)))
            else:
                out_shape.append(jax.ShapeDtypeStruct((r, nc * s["w"]), F32))
                out_specs.append(pl.BlockSpec((r, s["w"]), lambda c, b, t: (0, c)))

    def body(*refs):
        x_refs, p_refs = refs[:nx], refs[nx:nx + np_]
        d_refs = refs[nx + np_:nx + np_ + len(douts)]
        o_refs = refs[nx + np_ + len(douts):]
        xv = [[p.astype(F32) for p in _pieces(r, s)] for r, s in zip(x_refs, xs)]
        pv = [[p.astype(F32) for p in _pieces(r, s)] for r, s in zip(p_refs, ps)]
        if fwd:
            for r, pieces, s in zip(o_refs, f(xv, pv), out_modes):
                _store(r, pieces, s)
            return
        _, vjp = jax.vjp(f, xv, pv)
        cot = [[p.astype(F32) for p in _pieces(r, s)] for r, s in zip(d_refs, douts)]
        dxv, dpv = vjp(cot)
        for r, i in zip(o_refs, dx):
            _store(r, dxv[i], xs[i])
        b, t = pl.program_id(1), pl.program_id(2)
        for r, j in zip(o_refs[len(dx):], dp):
            first = (t == 0) if ps[j]["e"] else jnp.logical_and(b == 0, t == 0)

            @pl.when(first)
            def _(r=r, j=j):
                _store(r, dpv[j], ps[j])

            @pl.when(jnp.logical_not(first))
            def _(r=r, j=j):
                _store(r, dpv[j], ps[j], accumulate=True)

    res = pl.pallas_call(
        body, name=name, grid=(nc, bsz, nt), in_specs=in_specs, out_specs=out_specs, out_shape=out_shape,
        compiler_params=pltpu.CompilerParams(dimension_semantics=("arbitrary", "arbitrary", "arbitrary")),
    )(*operands)
    return res


def _keep_rows(a, shift, keep):
    n = a.shape[0]
    t = lax.broadcasted_iota(jnp.int32, a.shape, 0)
    return jnp.where(keep(t, n), pltpu.roll(a, shift % n, 0), 0.0)


def _shift_pair(step, keep_prev, keep_next):
    @jax.custom_vjp
    def prev(a):
        return _keep_rows(a, step, keep_prev)

    @jax.custom_vjp
    def nxt(a):
        return _keep_rows(a, -step, keep_next)

    prev.defvjp(lambda a: (prev(a), None), lambda _, g: (nxt(g),))
    nxt.defvjp(lambda a: (nxt(a), None), lambda _, g: (prev(g),))
    return prev, nxt


prev_tok, next_tok = _shift_pair(1, lambda t, n: t % GRID_W != 0, lambda t, n: t % GRID_W != GRID_W - 1)
prev_row, next_row = _shift_pair(GRID_W, lambda t, n: t >= GRID_W, lambda t, n: t < n - GRID_W)


@jax.custom_vjp
def bdot(a, w):
    return jnp.dot(a.astype(BF16), w.astype(BF16), preferred_element_type=F32)


def _bdot_bwd(res, g):
    a, w = res
    gb = g.astype(BF16)
    da = lax.dot_general(gb, w.astype(BF16), (((1,), (1,)), ((), ())), preferred_element_type=F32)
    dw = lax.dot_general(a.astype(BF16), gb, (((0,), (0,)), ((), ())), preferred_element_type=F32)
    return da, dw


bdot.defvjp(lambda a, w: (bdot(a, w), (a, w)), _bdot_bwd)


@jax.custom_vjp
def log_sigmoid(z):
    return jnp.minimum(z, 0.0) - jnp.log(1.0 + jnp.exp(-jnp.abs(z)))


def _lsig_bwd(z, g):
    e = jnp.exp(-jnp.abs(z))
    return (g * jnp.where(z >= 0, e, 1.0) / (1.0 + e),)


log_sigmoid.defvjp(lambda z: (log_sigmoid(z), z), _lsig_bwd)


def silu(x):
    return x * jax.nn.sigmoid(x)


def _rms(x):
    return x * lax.rsqrt(jnp.mean(x * x, axis=-1, keepdims=True) + EPS)


def _mod(x, gain, shift, scale):
    return _rms(x) * gain * (1.0 + scale) + shift


def f_mod(xs, ps):
    ((h,),), ((gain,), (shift,), (scale,)) = xs, ps
    return [[_mod(h, gain, shift, scale)], [h]]


def f_res_mod(xs, ps):
    ((h,), (y,)), ((gate,), (gain,), (shift,), (scale,)) = xs, ps
    h1 = h + gate * y
    return [[h1], [_mod(h1, gain, shift, scale)]]


def f_ffn_mid(xs, ps):
    ((ua, ug),), ((w0a, w0g), (w1a, w1g), (w2a, w2g), (ba, bg)) = xs, ps
    a = w0a * prev_row(ua) + w1a * ua + w2a * next_row(ua) + ba
    g = w0g * prev_row(ug) + w1g * ug + w2g * next_row(ug) + bg
    return [[a * silu(g)]]


def f_sc_mid(xs, ps):
    ((bg, cg, v),), ((w0,), (w1,), (w2,)) = xs, ps
    z = cg * v
    return [[bg * (w0 * prev_tok(z) + w1 * z + w2 * next_tok(z))]]


def f_decay(xs, ps):
    ((a,),), ((wd,), (bd,)) = xs, ps
    return [[log_sigmoid(bdot(a, wd) + bd) / TAU]]


def f_gla_post(xs, ps):
    (of, ob, g), ((gain,),) = xs, ps
    return [[_rms(a + b) * gain * silu(c) for a, b, c in zip(of, ob, g)]]


NCH = TT // CHUNK
CTX_CH = CTX // CHUNK
_NT = (((1,), (1,)), ((), ()))
_TN = (((0,), (0,)), ((), ()))
_NN = (((1,), (0,)), ((), ()))


def _chunk_of(d, j):
    return jnp.where(d == 0, j, jnp.where(j < CTX_CH, CTX_CH - 1 - j, NCH + CTX_CH - 1 - j))


def _dot(a, b, dn):
    return lax.dot_general(a, b, dn, preferred_element_type=F32)


def _mask_dot(m, g):
    g0 = g.astype(BF16)
    r1 = g - g0.astype(F32)
    g1 = r1.astype(BF16)
    g2 = (r1 - g1.astype(F32)).astype(BF16)
    return _dot(m, g0, _NN) + _dot(m, g1, _NN) + _dot(m, g2, _NN)


def _causal(d):
    row = lax.broadcasted_iota(jnp.int32, (CHUNK, CHUNK), 0)
    col = lax.broadcasted_iota(jnp.int32, (CHUNK, CHUNK), 1)
    delta = jnp.where(d == 0, col - row, row - col)
    return delta <= 0, delta >= 0


def _gla_in_specs(bsz, rev):
    def blk(d, j):
        return _chunk_of(d, (NCH - 1 - j) if rev else j)

    return [
        pl.BlockSpec((bsz, CHUNK, KD), lambda d, j: (0, blk(d, j), 0)),
        pl.BlockSpec((bsz, CHUNK, KD), lambda d, j: (0, blk(d, j), 1)),
        pl.BlockSpec((bsz, CHUNK, VD), lambda d, j: (0, blk(d, j), 1)),
        pl.BlockSpec((bsz, CHUNK, KD), lambda d, j: (0, blk(d, j), d)),
    ], blk


def gla_fwd(pcat, la):
    bsz = pcat.shape[0]
    in_specs, blk = _gla_in_specs(bsz, False)

    def body(q_ref, k_ref, v_ref, la_ref, o_ref, s_ref, st):
        d, j = pl.program_id(0), pl.program_id(1)

        @pl.when(j == 0)
        def _():
            st[...] = jnp.zeros_like(st)

        s_ref[...] = st[...]
        causal, _ = _causal(d)
        mf = causal.astype(BF16)
        for e, h in [(e, h) for e in range(bsz) for h in range(HEADS)]:
            ks_, vs_ = slice(h * HK, (h + 1) * HK), slice(h * HV, (h + 1) * HV)
            q, k, v, g = q_ref[e, :, ks_] * (HK ** -0.5), k_ref[e, :, ks_], v_ref[e, :, vs_].astype(BF16), la_ref[e, :, ks_]
            b = _mask_dot(mf, g)
            bl = jnp.sum(g, axis=0, keepdims=True)
            qs = (q * jnp.exp(b)).astype(BF16)
            ks = (k * jnp.exp(-b)).astype(BF16)
            kd = (k * jnp.exp(bl - b)).astype(BF16)
            s = st[e, h]
            att = jnp.where(causal, _dot(qs, ks, _NT), 0.0).astype(BF16)
            o_ref[e, :, vs_] = _dot(qs, s.astype(BF16), _NT) + _dot(att, v, _NN)
            st[e, h] = jnp.exp(bl) * s + _dot(v, kd, _TN)

    return pl.pallas_call(
        body, name="gla_fwd", grid=(2, NCH), in_specs=in_specs,
        out_specs=[pl.BlockSpec((bsz, CHUNK, VD), lambda d, j: (0, blk(d, j), d)),
                   pl.BlockSpec((bsz, None, None, HEADS, HV, HK), lambda d, j: (0, d, j, 0, 0, 0))],
        out_shape=[jax.ShapeDtypeStruct((bsz, TT, 2 * VD), F32), jax.ShapeDtypeStruct((bsz, 2, NCH, HEADS, HV, HK), F32)],
        scratch_shapes=[pltpu.VMEM((bsz, HEADS, HV, HK), F32)],
        compiler_params=pltpu.CompilerParams(dimension_semantics=("arbitrary", "arbitrary")),
    )(pcat, pcat, pcat, la)


def gla_bwd(pcat, la, s_all, do):
    bsz = pcat.shape[0]
    in_specs, blk = _gla_in_specs(bsz, True)
    in_specs += [
        pl.BlockSpec((bsz, None, None, HEADS, HV, HK), lambda d, j: (0, d, NCH - 1 - j, 0, 0, 0)),
        pl.BlockSpec((bsz, CHUNK, VD), lambda d, j: (0, jnp.maximum(blk(d, j) - CTX_CH, 0), 0)),
    ]

    def body(q_ref, k_ref, v_ref, la_ref, s_ref, do_ref, dq_ref, dk_ref, dv_ref, dla_ref, dst):
        d, j = pl.program_id(0), pl.program_id(1)

        @pl.when(j == 0)
        def _():
            dst[...] = jnp.zeros_like(dst)

        latent = blk(d, j) >= CTX_CH
        causal, causal_t = _causal(d)
        mt = causal_t.astype(BF16)
        mf = causal.astype(BF16)
        scale = HK ** -0.5
        for e, h in [(e, h) for e in range(bsz) for h in range(HEADS)]:
            ks_, vs_ = slice(h * HK, (h + 1) * HK), slice(h * HV, (h + 1) * HV)
            q, k, v, g = q_ref[e, :, ks_] * scale, k_ref[e, :, ks_], v_ref[e, :, vs_].astype(BF16), la_ref[e, :, ks_]
            b = _mask_dot(mf, g)
            bl = jnp.sum(g, axis=0, keepdims=True)
            ex, ei, ed, el = jnp.exp(b), jnp.exp(-b), jnp.exp(bl - b), jnp.exp(bl)
            qs, ks, kd = q * ex, k * ei, k * ed
            qsb, ksb, kdb = qs.astype(BF16), ks.astype(BF16), kd.astype(BF16)
            s, ds1 = s_ref[e, h], dst[e, h]
            sb, ds1b = s.astype(BF16), ds1.astype(BF16)
            dob = jnp.where(latent, do_ref[e, :, vs_], 0.0).astype(BF16)
            att = jnp.where(causal, _dot(qsb, ksb, _NT), 0.0).astype(BF16)
            datt = jnp.where(causal, _dot(dob, v, _NT), 0.0).astype(BF16)
            dqs = _dot(dob, sb, _NN) + _dot(datt, ksb, _NN)
            dks = _dot(datt, qsb, _TN)
            dv_ref[e, :, vs_] = _dot(att, dob, _TN) + _dot(kdb, ds1b, _NT)
            dkd = _dot(v, ds1b, _NN)
            dst[e, h] = _dot(dob, qsb, _TN) + el * ds1
            del_ = jnp.sum(s * ds1, axis=0, keepdims=True)
            dq_ref[e, :, ks_] = dqs * ex * scale
            dk_ref[e, :, ks_] = dks * ei + dkd * ed
            db = dqs * qs - dks * ks - dkd * kd
            dbl = jnp.sum(dkd * kd, axis=0, keepdims=True) + del_ * el
            dla_ref[e, :, ks_] = _mask_dot(mt, db) + dbl

    return pl.pallas_call(
        body, name="gla_bwd", grid=(2, NCH), in_specs=in_specs,
        out_specs=[pl.BlockSpec((None, bsz, CHUNK, KD), lambda d, j: (d, 0, blk(d, j), 0)),
                   pl.BlockSpec((None, bsz, CHUNK, KD), lambda d, j: (d, 0, blk(d, j), 0)),
                   pl.BlockSpec((None, bsz, CHUNK, VD), lambda d, j: (d, 0, blk(d, j), 0)),
                   pl.BlockSpec((bsz, CHUNK, KD), lambda d, j: (0, blk(d, j), d))],
        out_shape=[jax.ShapeDtypeStruct((2, bsz, TT, KD), F32), jax.ShapeDtypeStruct((2, bsz, TT, KD), F32),
                   jax.ShapeDtypeStruct((2, bsz, TT, VD), F32), jax.ShapeDtypeStruct((bsz, TT, 2 * KD), F32)],
        scratch_shapes=[pltpu.VMEM((bsz, HEADS, HV, HK), F32)],
        compiler_params=pltpu.CompilerParams(dimension_semantics=("arbitrary", "arbitrary")),
    )(pcat, pcat, pcat, la, s_all, do)


def gla_combine(dq2, dk2, dv2, dgate, dpa):
    bsz = dgate.shape[0]
    tm = CTX

    def body(dq_ref, dk_ref, dv_ref, dg_ref, dpa_ref, o_ref):
        t = pl.program_id(1)
        o_ref[:, 0:KD] = (dq_ref[0] + dq_ref[1]).astype(BF16)
        o_ref[:, KD:2 * KD] = (dk_ref[0] + dk_ref[1]).astype(BF16)
        o_ref[:, 2 * KD:2 * KD + VD] = (dv_ref[0] + dv_ref[1]).astype(BF16)
        o_ref[:, 2 * KD + VD:2 * KD + 2 * VD] = jnp.where(t > 0, dg_ref[...], 0).astype(BF16)
        o_ref[:, 2 * KD + 2 * VD:] = dpa_ref[...].astype(BF16)

    return pl.pallas_call(
        body, name="gla_combine", grid=(bsz, TT // tm),
        in_specs=[pl.BlockSpec((2, None, tm, KD), lambda b, t: (0, b, t, 0)),
                  pl.BlockSpec((2, None, tm, KD), lambda b, t: (0, b, t, 0)),
                  pl.BlockSpec((2, None, tm, VD), lambda b, t: (0, b, t, 0)),
                  pl.BlockSpec((None, tm, VD), lambda b, t: (b, jnp.maximum(t - 1, 0), 0)),
                  pl.BlockSpec((None, tm, 128), lambda b, t: (b, t, 0))],
        out_specs=pl.BlockSpec((None, tm, GLA_IN_PAD), lambda b, t: (b, t, 0)),
        out_shape=jax.ShapeDtypeStruct((bsz, TT, GLA_IN_PAD), BF16),
        compiler_params=pltpu.CompilerParams(dimension_semantics=("arbitrary", "arbitrary")),
    )(dq2, dk2, dv2, dgate, dpa)


def final_loss(h1, fo, gate, gain, tgt):
    bsz, t_len, _ = h1.shape
    tm = 256

    def body(h_ref, f_ref, gate_ref, gain_ref, tgt_ref, loss_ref, dh_ref, df_ref, dgate_ref, dgain_ref):
        b, t = pl.program_id(0), pl.program_id(1)
        target = tgt_ref[...]

        def core(h, fo_, gate_, gain_):
            e = _rms(h + gate_ * fo_) * gain_ - target
            return jnp.sum(0.5 * jnp.sum(e * e, axis=-1, keepdims=True) / D, axis=0, keepdims=True)

        loss, vjp = jax.vjp(core, h_ref[...], f_ref[...], gate_ref[...], gain_ref[...])
        dh, df, dgate, dgain = vjp(jnp.ones((1, 1), F32))
        dh_ref[...] = dh
        df_ref[...] = df.astype(BF16)
        first = jnp.logical_and(b == 0, t == 0)

        @pl.when(first)
        def _():
            loss_ref[...] = jnp.broadcast_to(loss, loss_ref.shape)
            dgain_ref[...] = dgain

        @pl.when(jnp.logical_not(first))
        def _():
            loss_ref[...] += jnp.broadcast_to(loss, loss_ref.shape)
            dgain_ref[...] += dgain

        @pl.when(t == 0)
        def _():
            dgate_ref[...] = dgate

        @pl.when(t > 0)
        def _():
            dgate_ref[...] += dgate

    tile = pl.BlockSpec((None, tm, D), lambda b, t: (b, t, 0))
    per_ex = pl.BlockSpec((None, 1, D), lambda b, t: (b, 0, 0))
    shared = pl.BlockSpec((1, D), lambda b, t: (0, 0))
    return pl.pallas_call(
        body, name="final_loss", grid=(bsz, t_len // tm),
        in_specs=[tile, tile, per_ex, shared, tile],
        out_specs=[pl.BlockSpec((8, 128), lambda b, t: (0, 0)), tile, tile, per_ex, shared],
        out_shape=[jax.ShapeDtypeStruct((8, 128), F32), jax.ShapeDtypeStruct(h1.shape, F32),
                   jax.ShapeDtypeStruct(h1.shape, BF16), jax.ShapeDtypeStruct((bsz, 1, D), F32),
                   jax.ShapeDtypeStruct((1, D), F32)],
        compiler_params=pltpu.CompilerParams(dimension_semantics=("arbitrary", "arbitrary")),
    )(h1, fo, gate, gain, tgt)


ADA_ROWS = 24
ADA_CTX_ROW = 16
ADA_COLS = 6 * D // N_DEV


def ada_fwd(cond, w, b):
    def body(c_ref, w_ref, b_ref, o_ref):
        s = silu(c_ref[...]).astype(BF16)
        o_ref[...] = jnp.dot(s, w_ref[...].astype(BF16), preferred_element_type=F32) + b_ref[...]

    return pl.pallas_call(
        body, name="ada_fwd", grid=(2,),
        in_specs=[pl.BlockSpec((ADA_ROWS, D), lambda i: (0, 0)), pl.BlockSpec((None, D, ADA_COLS), lambda i: (i, 0, 0)),
                  pl.BlockSpec((None, 1, ADA_COLS), lambda i: (i, 0, 0))],
        out_specs=pl.BlockSpec((None, ADA_ROWS, ADA_COLS), lambda i: (i, 0, 0)),
        out_shape=jax.ShapeDtypeStruct((2, ADA_ROWS, ADA_COLS), F32),
    )(cond, w, b)


def ada_bwd(cond, dm_mine, dm_full, w):
    def body(c_ref, dm_ref, dmf_ref, w_ref, gw_ref, gb_ref, cp_ref):
        i = pl.program_id(0)
        s = silu(c_ref[...]).astype(BF16)
        dm = dm_ref[...].astype(BF16)
        gw_ref[...] = _dot(s, dm, _TN)
        gb_ref[...] = jnp.sum(dmf_ref[...], axis=0, keepdims=True)

        @pl.when(i == 0)
        def _():
            cp_ref[...] = _dot(dm_ref[ADA_CTX_ROW:, :].astype(BF16), w_ref[...].astype(BF16), _NT)

    return pl.pallas_call(
        body, name="ada_bwd", grid=(2,),
        in_specs=[pl.BlockSpec((ADA_ROWS, D), lambda i: (0, 0)), pl.BlockSpec((None, ADA_ROWS, ADA_COLS), lambda i: (i, 0, 0)),
                  pl.BlockSpec((None, ADA_ROWS, 6 * D), lambda i: (i, 0, 0)), pl.BlockSpec((None, D, ADA_COLS), lambda i: (i, 0, 0))],
        out_specs=[pl.BlockSpec((None, D, ADA_COLS), lambda i: (i, 0, 0)), pl.BlockSpec((None, 1, 6 * D), lambda i: (i, 0, 0)),
                   pl.BlockSpec((ADA_ROWS - ADA_CTX_ROW, D), lambda i: (0, 0))],
        out_shape=[jax.ShapeDtypeStruct((2, D, ADA_COLS), F32), jax.ShapeDtypeStruct((2, 1, 6 * D), F32),
                   jax.ShapeDtypeStruct((ADA_ROWS - ADA_CTX_ROW, D), F32)],
        compiler_params=pltpu.CompilerParams(dimension_semantics=("arbitrary",)),
    )(cond, dm_mine, dm_full, w)


def cctx_grad(parts, c_ctx):
    def body(p_ref, c_ref, o_ref):
        tot = p_ref[0:1, :]
        for i in range(1, N_DEV):
            tot = tot + p_ref[i:i + 1, :]
        c = c_ref[...]
        sg = jax.nn.sigmoid(c)
        o_ref[...] = tot * sg * (1.0 + c * (1.0 - sg))

    return pl.pallas_call(body, name="cctx_grad", out_shape=jax.ShapeDtypeStruct((1, D), F32))(parts, c_ctx)


def _row_tile(r):
    for t in (512, 256, 128, 80, 64, 40, 32, 16, 8):
        if r % t == 0:
            return t
    return r


def _slot_sum(ref):
    tot = ref[0].astype(F32)
    for i in range(1, ref.shape[0]):
        tot = tot + ref[i].astype(F32)
    return tot


def sum_slots(name, x):
    s, r, c = x.shape
    tr = _row_tile(r)

    def body(x_ref, o_ref):
        o_ref[...] = _slot_sum(x_ref)

    return pl.pallas_call(
        body, name=name, grid=(r // tr,), in_specs=[pl.BlockSpec((s, tr, c), lambda i: (0, i, 0))],
        out_specs=pl.BlockSpec((tr, c), lambda i: (i, 0)), out_shape=jax.ShapeDtypeStruct((r, c), F32),
    )(x)


def adamw(name, w, g, m, v, layer=None):
    r, c = w.shape[-2:]
    tr = _row_tile(r)
    stacked = g.ndim == 3

    def body(w_ref, g_ref, m_ref, v_ref, go_ref, d_ref, mo_ref, vo_ref):
        gv = _slot_sum(g_ref) if stacked else g_ref[...]
        mn = B1 * m_ref[...] + (1.0 - B1) * gv
        vn = B2 * v_ref[...] + (1.0 - B2) * jnp.square(gv)
        m_hat = mn / (1.0 - B1 ** STEP)
        v_hat = vn / (1.0 - B2 ** STEP)
        go_ref[...] = gv
        d_ref[...] = -LR * (m_hat / (jnp.sqrt(v_hat) + AEPS) + WD * w_ref[...])
        mo_ref[...] = mn
        vo_ref[...] = vn

    tile = pl.BlockSpec((tr, c), lambda i: (i, 0))
    slab = tile if layer is None else pl.BlockSpec((None, tr, c), lambda i: (layer, i, 0))
    g_spec = pl.BlockSpec((g.shape[0], tr, c), lambda i: (0, i, 0)) if stacked else tile
    return pl.pallas_call(
        body, name=name, grid=(r // tr,), in_specs=[slab, g_spec, slab, slab], out_specs=[tile] * 4,
        out_shape=[jax.ShapeDtypeStruct((r, c), F32)] * 4,
    )(w, g, m, v)


def _place():
    return lax.axis_index("x"), lax.axis_index("y"), lax.axis_index("c")


def all_gather(name, x, in_vmem):
    r, c = x.shape
    space = pltpu.VMEM if in_vmem else pl.ANY

    def body(x_ref, out_ref, send_sems, recv_sems, local_sem):
        px, py, pc = _place()
        me, sibling = (px, py, pc), (px, py, 1 - pc)
        chips = [(1 - px, py), (px, 1 - py), (1 - px, 1 - py)]

        def rows(qx, qy, qc):
            return out_ref.at[pl.ds((4 * qx + 2 * qy + qc) * r, r), :]

        def copy(k, block, to, src=None):
            return pltpu.make_async_remote_copy(
                src_ref=rows(*block) if src is None else src, dst_ref=rows(*block),
                send_sem=send_sems.at[k], recv_sem=recv_sems.at[k], device_id=to, device_id_type=MESH)

        mine = pltpu.make_async_copy(x_ref, rows(*me), local_sem)
        mine.start()
        first = [copy(0, me, sibling, src=x_ref)]
        first += [copy(1 + j, me, (*chip, pc), src=x_ref) for j, chip in enumerate(chips)]
        for cp in first:
            cp.start()
        passed = [copy(4 + j, (*chip, pc), sibling) for j, chip in enumerate(chips)]
        for j, chip in enumerate(chips):
            copy(1 + j, (*chip, pc), me).wait_recv()
            passed[j].start()
        copy(0, sibling, me).wait_recv()
        for j, chip in enumerate(chips):
            copy(4 + j, (*chip, 1 - pc), me).wait_recv()
        for cp in first + passed:
            cp.wait_send()
        mine.wait()

    return pl.pallas_call(
        body, name=name, out_shape=jax.ShapeDtypeStruct((N_DEV * r, c), x.dtype),
        in_specs=[pl.BlockSpec(memory_space=space)], out_specs=pl.BlockSpec(memory_space=space),
        scratch_shapes=[pltpu.SemaphoreType.DMA((7,)), pltpu.SemaphoreType.DMA((7,)), pltpu.SemaphoreType.DMA],
    )(x)


_HBM =pl.BlockSpec(memory_space=pltpu.HBM)
_SEM = pl.BlockSpec(memory_space=pltpu.SEMAPHORE)
_EFFECT = pltpu.SideEffectType.DATAFLOW_SIDE_EFFECTING


def _peers():
    px, py, pc = _place()
    return [(1 - px if k & 4 else px, 1 - py if k & 2 else py, 1 - pc if k & 1 else pc) for k in range(1, N_DEV)]


def _slot(dev):
    return 4 * dev[0] + 2 * dev[1] + dev[2]


def _split_copies(src_refs, land_refs, send_sems, recv_sems, gather):
    me = _slot(_place())
    return [pltpu.make_async_remote_copy(
        src_ref=src if gather else src.at[_slot(peer)], dst_ref=land.at[me],
        send_sem=send_sems.at[a * (N_DEV - 1) + k], recv_sem=recv_sems.at[a * (N_DEV - 1) + k],
        device_id=peer, device_id_type=MESH)
        for a, (src, land) in enumerate(zip(src_refs, land_refs)) for k, peer in enumerate(_peers())]


def exchange_start(name, srcs, gather):
    n = len(srcs)
    lands = [pltpu.HBM((N_DEV,) + s.shape if gather else s.shape, s.dtype) for s in srcs]

    def body(*refs):
        send_sems, recv_sems = refs[2 * n:2 * n + 2]
        for cp in _split_copies(refs[:n], refs[n:2 * n], send_sems, recv_sems, gather):
            cp.start()
        refs[-1][...] = jnp.zeros_like(refs[-1])

    sems = pltpu.SemaphoreType.DMA((n * (N_DEV - 1),))
    res = pl.pallas_call(
        body, name=name,
        out_shape=(sems, sems, *[pltpu.HBM(s.shape, s.dtype) for s in srcs], *lands, jax.ShapeDtypeStruct((8, 128), F32)),
        in_specs=(_HBM,) * (2 * n), out_specs=(_SEM, _SEM) + (_HBM,) * (2 * n) + (pl.BlockSpec(memory_space=pltpu.VMEM),),
        input_output_aliases={i: 2 + i for i in range(2 * n)},
        compiler_params=pltpu.CompilerParams(has_side_effects=_EFFECT),
    )(*[pltpu.with_memory_space_constraint(s, pltpu.HBM) for s in srcs],
      *[pltpu.with_memory_space_constraint(lax.empty(ld.shape, ld.dtype), pltpu.HBM) for ld in lands])
    return res[0], res[1], list(res[2:2 + n]), list(res[2 + n:2 + 2 * n]), res[-1]


def exchange_wait(name, started, after, gather):
    send_sems, recv_sems, srcs, lands, _ = started
    n = len(srcs)

    def body(*refs):
        send_sems, recv_sems = refs[2 * n:2 * n + 2]
        for cp in _split_copies(refs[:n], refs[n:2 * n], send_sems, recv_sems, gather):
            cp.wait_send()
            cp.wait_recv()

    res = pl.pallas_call(
        body, name=name, out_shape=tuple(pltpu.HBM(a.shape, a.dtype) for a in srcs + lands),
        in_specs=(_HBM,) * (2 * n) + (_SEM, _SEM, pl.BlockSpec(memory_space=pl.ANY)), out_specs=(_HBM,) * (2 * n),
        input_output_aliases={i: i for i in range(2 * n)},
        compiler_params=pltpu.CompilerParams(has_side_effects=_EFFECT),
    )(*srcs, *lands, send_sems, recv_sems, after)
    return list(res[:n]), list(res[n:])


NCF = FFN_H // FFN_TC


def _size(shape):
    n = 1
    for s in shape:
        n *= s
    return n


def _padded_rows(n_elems, row_mult):
    return -(-n_elems // (D * row_mult)) * row_mult


def _pack_rows(arrs, dtype, row_mult):
    rows, offs, r0 = [], [], 0
    for a in arrs:
        flat = a.reshape(-1).astype(dtype)
        n = _padded_rows(flat.shape[0], row_mult)
        rows.append(jnp.pad(flat, (0, n * D - flat.shape[0])).reshape(n, D))
        offs.append(r0)
        r0 += n
    return jnp.concatenate(rows, 0), offs


def _unpack_rows(buf, offs, shapes):
    lead, out = buf.shape[:-2], []
    for o, shp in zip(offs, shapes):
        n = _size(shp)
        nr = -(-n // D)
        out.append(buf[..., o:o + nr, :].reshape(lead + (nr * D,))[..., :n].reshape(lead + tuple(shp)))
    return out


def _cols_from_shards(g):
    return g.transpose(1, 0, 2).reshape(g.shape[1], N_DEV * g.shape[2])


def _cols_to_shards(w):
    k, n = w.shape[0], w.shape[1] // N_DEV
    return w.reshape(k, N_DEV, n).transpose(1, 0, 2)


def _rows3(w):
    return [w[i:i + 1] for i in range(3)]


def f_mod1(xs, ps):
    return f_mod(xs, ps)[:1]


def kernel(x, c, ctx, c_ctx, ada_w, ada_b, norm_mix, norm_ffn, gla_w_in, gla_w_a2, gla_b_a, gla_head_norm, gla_w_out, sc_w_in, sc_conv_w, sc_w_out, ffn_w_up, ffn_conv_w, ffn_conv_b, ffn_w_down, final_norm, loss_target, m_c_ctx, m_ada_w, m_ada_b, m_norm_mix, m_norm_ffn, m_gla_w_in, m_gla_w_a2, m_gla_b_a, m_gla_head_norm, m_gla_w_out, m_sc_w_in, m_sc_conv_w, m_sc_w_out, m_ffn_w_up, m_ffn_conv_w, m_ffn_conv_b, m_ffn_w_down, m_final_norm, v_c_ctx, v_ada_w, v_ada_b, v_norm_mix, v_norm_ffn, v_gla_w_in, v_gla_w_a2, v_gla_b_a, v_gla_head_norm, v_gla_w_out, v_sc_w_in, v_sc_conv_w, v_sc_w_out, v_ffn_w_up, v_ffn_conv_w, v_ffn_conv_b, v_ffn_w_down, v_final_norm):
    names = ["c_ctx", "ada_w", "ada_b", "norm_mix", "norm_ffn", "gla_w_in", "gla_w_a2", "gla_b_a", "gla_head_norm",
             "gla_w_out", "sc_w_in", "sc_conv_w", "sc_w_out", "ffn_w_up", "ffn_conv_w", "ffn_conv_b", "ffn_w_down",
             "final_norm"]
    w_ = dict(zip(names, [c_ctx, ada_w, ada_b, norm_mix, norm_ffn, gla_w_in, gla_w_a2, gla_b_a, gla_head_norm, gla_w_out,
                          sc_w_in, sc_conv_w, sc_w_out, ffn_w_up, ffn_conv_w, ffn_conv_b, ffn_w_down, final_norm]))
    m_ = dict(zip(names, [m_c_ctx, m_ada_w, m_ada_b, m_norm_mix, m_norm_ffn, m_gla_w_in, m_gla_w_a2, m_gla_b_a,
                          m_gla_head_norm, m_gla_w_out, m_sc_w_in, m_sc_conv_w, m_sc_w_out, m_ffn_w_up, m_ffn_conv_w,
                          m_ffn_conv_b, m_ffn_w_down, m_final_norm]))
    v_ = dict(zip(names, [v_c_ctx, v_ada_w, v_ada_b, v_norm_mix, v_norm_ffn, v_gla_w_in, v_gla_w_a2, v_gla_b_a,
                          v_gla_head_norm, v_gla_w_out, v_sc_w_in, v_sc_conv_w, v_sc_w_out, v_ffn_w_up, v_ffn_conv_w,
                          v_ffn_conv_b, v_ffn_w_down, v_final_norm]))
    me = 4 * lax.axis_index("x") + 2 * lax.axis_index("y") + lax.axis_index("c")
    bsz = x.shape[0]
    tm = 256
    nt = SEQ // tm
    ctx_tiles = CTX // tm
    pe = functools.partial(P, per_example=True)

    small_sharded = [c, gla_w_a2, gla_b_a, sc_conv_w, ffn_conv_w]
    pack0, offs0 = _pack_rows(small_sharded, F32, 8)
    g0 = all_gather("ag_small", pack0, True).reshape(N_DEV, pack0.shape[0], D)
    c_all, wa2_s, ba_s, scw_s, fcw_s = _unpack_rows(g0, offs0, [a.shape for a in small_sharded])
    w_a2 = wa2_s[:, 0].transpose(1, 2, 0, 3).reshape(2, RANK, KD)
    b_a = ba_s[:, 0].transpose(1, 0, 2).reshape(2, KD)
    sc_cw = scw_s[:, 0].transpose(1, 0, 2).reshape(3, D)
    ffn_cw = fcw_s.transpose(1, 2, 0, 3).reshape(2, 3, 2 * FFN_H)

    cond = jnp.concatenate([c_all.reshape(N_DEV * bsz, D), c_ctx[None], jnp.zeros((ADA_ROWS - N_DEV * bsz - 1, D), F32)], 0)
    b_mine = lax.dynamic_slice(ada_b, (0, me * ADA_COLS), (2, ADA_COLS)).reshape(2, 1, ADA_COLS)
    mod_part = ada_fwd(cond, ada_w, b_mine)
    mod = all_gather("ag_mod", mod_part.reshape(2 * ADA_ROWS, ADA_COLS), True)
    mod = mod.reshape(N_DEV, 2, ADA_ROWS, ADA_COLS).transpose(1, 2, 0, 3).reshape(2, ADA_ROWS, 6 * D)
    mods = lax.dynamic_slice(mod, (0, bsz * me, 0), (2, bsz, 6 * D))
    md = [[mods[i][:, k * D:(k + 1) * D].reshape(bsz, 1, D) for k in range(6)] for i in range(2)]
    mc = [mod[0, ADA_CTX_ROW, k * D:(k + 1) * D][None] for k in range(2)]

    groups = {"gla": [("gla_w_in", 0), ("gla_w_out", 0)], "ffn0": [("ffn_w_up", 0), ("ffn_w_down", 0)],
              "l1": [("sc_w_in", 0), ("sc_w_out", 0), ("ffn_w_up", 1), ("ffn_w_down", 1)]}

    ag_started, tok = {}, 0.0
    for g, keys in groups.items():
        srcs = [w_[n][i].astype(BF16) for n, i in keys]
        *srcs, _ = lax.optimization_barrier((*srcs, mod))
        ag_started[g] = exchange_start(f"ag_{g}_start", srcs, True)
        tok = tok + ag_started[g][4][0, 0]
    norm_mix = norm_mix + tok

    def gathered(g, after):
        mine, lands = exchange_wait(f"ag_{g}_wait", ag_started[g], after, True)
        return [lax.dynamic_update_index_in_dim(ld, mn, me, 0) for ld, mn in zip(lands, mine)]

    s_up, w_down = [None, None], [None, None]
    wd = jnp.zeros((128, 2 * KD), F32).at[:RANK, :KD].set(w_a2[0]).at[RANK:2 * RANK, KD:].set(w_a2[1])
    bd = b_a.reshape(1, 2 * KD)
    scw = _rows3(sc_cw)
    head_gain = gla_head_norm.reshape(1, HV)
    gains_mix = [norm_mix[i][None] for i in range(2)]
    gains_ffn = [norm_ffn[i][None] for i in range(2)]

    def tokens(a2d, t_len):
        return a2d.reshape(bsz, t_len, -1)

    def ffn_params(i):
        rows = [ffn_cw[i][t] for t in range(3)] + [ffn_conv_b[i]]
        return [P(a.reshape(2, FFN_H), w=FFN_TC, rows=True) for a in rows]

    def ffn_fwd(i, hn2):
        u = mm(f"ffn_up{i}", V(hn2, "tok"), V(s_up[i], "cols"), out="planes", out_dtype=BF16, planes_t=SEQ)
        act = rowwise(f"ffn_mid{i}", f_ffn_mid, [X(u, w=FFN_TC, planes=True)], ffn_params(i), tm=SEQ, nt=1, nc=NCF,
                      outs=[(FFN_TC, BF16, 1)])[0]
        return u, act, tokens(mm(f"ffn_down{i}", V(act, "tok"), V(w_down[i])), SEQ)

    def res_mod_fwd(name, h, y, ps):
        return rowwise(name, f_res_mod, [X(h), X(y)], ps, tm=tm, nt=nt, outs=[(D, F32, 1), (D, BF16, 1)])

    ps_in0 = [P(gains_mix[0]), pe(md[0][0]), pe(md[0][1])]
    ps_ctx = [P(gains_mix[0]), P(mc[0]), P(mc[1])]
    hn0 = rowwise("mod_in0", f_mod, [X(x)], ps_in0, tm=tm, nt=nt, outs=[(D, BF16, 1)])[0]
    hnc = rowwise("mod_ctx", f_mod, [X(ctx)], ps_ctx, tm=tm, nt=ctx_tiles, outs=[(D, BF16, 1)])[0]
    hcat = jnp.concatenate([hnc, hn0], axis=1)
    s_gin, s_gout = gathered("gla", hcat)
    w_gin = jnp.pad(_cols_from_shards(s_gin), ((0, 0), (0, GLA_IN_PAD - GLA_IN)))
    w_gout = s_gout.reshape(VD, D)
    pcat = tokens(mm("gla_in", V(hcat, "tok"), V(w_gin)), TT)
    pa_x = X(pcat, w=128, co=(GLA_IN_PAD - 128) // 128)
    la = rowwise("gla_decay", f_decay, [pa_x], [P(wd), P(bd)], tm=tm, nt=TT // tm, outs=[(2 * KD, F32, 1)])[0]
    o2, s_all = gla_fwd(pcat, la)
    post_xs = [X(o2, w=VD, co=0, ro=ctx_tiles, split=HEADS), X(o2, w=VD, co=1, ro=ctx_tiles, split=HEADS),
               X(pcat, w=VD, co=2, ro=ctx_tiles, split=HEADS)]
    yin0 = rowwise("gla_post", f_gla_post, post_xs, [P(head_gain)], tm=tm, nt=nt, outs=[(VD, BF16, HEADS)])[0]
    y0 = tokens(mm("gla_out", V(yin0, "tok"), V(w_gout)), SEQ)
    ps_mid0 = [pe(md[0][2]), P(gains_ffn[0]), pe(md[0][3]), pe(md[0][4])]
    h1_0, hn2_0 = res_mod_fwd("res_mod_mid0", x, y0, ps_mid0)
    s_up[0], s_down0 = gathered("ffn0", hn2_0)
    w_down[0] = s_down0.reshape(FFN_H, D)
    u0, act0, fo0 = ffn_fwd(0, hn2_0)
    ps_in1 = [pe(md[0][5]), P(gains_mix[1]), pe(md[1][0]), pe(md[1][1])]
    h2_0, hn1 = res_mod_fwd("res_mod_in1", h1_0, fo0, ps_in1)

    s_sin, s_sout, s_up[1], s_down1 = gathered("l1", hn1)
    w_sout, w_down[1] = s_sout.reshape(D, D), s_down1.reshape(FFN_H, D)
    p1 = tokens(mm("sc_in", V(hn1, "tok"), V(s_sin, "cols")), SEQ)
    sc_ps = [P(a) for a in scw]
    yin1 = rowwise("sc_mid", f_sc_mid, [X(p1, split=3)], sc_ps, tm=tm, nt=nt, outs=[(D, BF16, 1)])[0]
    y1 = tokens(mm("sc_out", V(yin1, "tok"), V(w_sout)), SEQ)
    ps_mid1 = [pe(md[1][2]), P(gains_ffn[1]), pe(md[1][3]), pe(md[1][4])]
    h1_1, hn2_1 = res_mod_fwd("res_mod_mid1", h2_0, y1, ps_mid1)
    u1, act1, fo1 = ffn_fwd(1, hn2_1)
    loss8, dh1_1, dfo1, dm5_1, g_final = final_loss(h1_1, fo1, md[1][5], final_norm[None], loss_target)
    loss = lax.psum(loss8[0, 0], ("x", "y", "c"))

    def ffn_bwd(i, u, act, hn2, dfo):
        dact = tokens(mm(f"ffn_down_dx{i}", V(dfo, "tok"), V(w_down[i]), form="nt", out_dtype=BF16), SEQ)
        g_down = mm(f"ffn_down_dw{i}", V(act, "tok"), V(dfo, "tok"), form="tn", out_dtype=BF16)
        r = rowwise(f"ffn_mid_bwd{i}", f_ffn_mid, [X(u, w=FFN_TC, planes=True)], ffn_params(i), tm=SEQ, nt=1, nc=NCF,
                    douts=[X(dact, w=FFN_TC)], dx={0: BF16}, dp=[0, 1, 2, 3])
        du, g_cw, g_cb = r[0], jnp.stack([a.reshape(2 * FFN_H) for a in r[1:4]]), r[4].reshape(1, 2 * FFN_H)
        dhn2 = tokens(mm(f"ffn_up_dx{i}", V(du, "planes"), V(s_up[i], "cols"), form="nt", out_dtype=BF16), SEQ)
        g_up = mm(f"ffn_up_dw{i}", V(hn2, "tok"), V(du, "planes"), form="tn", out="cols", out_dtype=BF16)
        return dhn2, g_up, row_slots(g_down), g_cw, g_cb

    def res_mod_bwd(name, h, y, ps, dh1, dhn):
        return rowwise(name, f_res_mod, [X(h), X(y)], ps, tm=tm, nt=nt, douts=[X(dh1), X(dhn)],
                       dx={0: F32, 1: BF16}, dp=[0, 1, 2, 3])

    def row_slots(g):
        return g.reshape(N_DEV, -1, g.shape[-1])

    a2a_started = {}

    def send_grads(g, slots, after=None):
        if after is not None:
            *slots, _ = lax.optimization_barrier((*slots, after))
        a2a_started[g] = exchange_start(f"a2a_{g}_start", list(slots), False)
        return a2a_started[g][4][0, 0]

    def after_start(ps, tok):
        return [dict(ps[0], a=ps[0]["a"] + tok)] + ps[1:]

    dhn2_1, g_up1, g_down1, g_fcw1, g_fcb1 = ffn_bwd(1, u1, act1, hn2_1, dfo1)
    dh2_0, dy1, dm2_1, g_nffn1, dm3_1, dm4_1 = res_mod_bwd("res_mod_mid1_bwd", h2_0, y1, ps_mid1, dh1_1, dhn2_1)
    dyin1 = tokens(mm("sc_out_dx", V(dy1, "tok"), V(w_sout), form="nt", out_dtype=BF16), SEQ)
    g_sout = row_slots(mm("sc_out_dw", V(yin1, "tok"), V(dy1, "tok"), form="tn", out_dtype=BF16))
    r = rowwise("sc_mid_bwd", f_sc_mid, [X(p1, split=3)], sc_ps, tm=tm, nt=nt, douts=[X(dyin1)], dx={0: BF16}, dp=[0, 1, 2])
    dp1, g_scw = r[0], jnp.concatenate(r[1:4], 0)
    dhn1 = tokens(mm("sc_in_dx", V(dp1, "tok"), V(s_sin, "cols"), form="nt", out_dtype=BF16), SEQ)
    g_sin = mm("sc_in_dw", V(hn1, "tok"), V(dp1, "tok"), form="tn", out="cols", out_dtype=BF16)
    tok = send_grads("l1", [g_sin, g_sout, g_up1, g_down1])
    dh1_0, dfo0, dm5_0, g_nmix1, dm0_1, dm1_1 = res_mod_bwd("res_mod_in1_bwd", h1_0, fo0, after_start(ps_in1, tok), dh2_0, dhn1)

    dhn2_0, g_up0, g_down0, g_fcw0, g_fcb0 = ffn_bwd(0, u0, act0, hn2_0, dfo0)
    tok = send_grads("ffn0", [g_up0, g_down0])
    dx_res, dy0, dm2_0, g_nffn0, dm3_0, dm4_0 = res_mod_bwd("res_mod_mid0_bwd", x, y0, after_start(ps_mid0, tok), dh1_0, dhn2_0)
    dyin0 = tokens(mm("gla_out_dx", V(dy0, "tok"), V(w_gout), form="nt", out_dtype=BF16), SEQ)
    g_gout = row_slots(mm("gla_out_dw", V(yin0, "tok"), V(dy0, "tok"), form="tn", out_dtype=BF16))
    do, dgate, g_head = rowwise("gla_post_bwd", f_gla_post, post_xs, [P(head_gain)], tm=tm, nt=nt,
                                douts=[X(dyin0, split=HEADS)], dx={0: F32, 2: BF16}, dp=[0])
    dq2, dk2, dv2, dla = gla_bwd(pcat, la, s_all, do)
    dpa, g_wd, g_bd = rowwise("gla_decay_bwd", f_decay, [pa_x], [P(wd), P(bd)], tm=tm, nt=TT // tm, douts=[X(dla)],
                              dx={0: BF16}, dp=[0, 1])
    dpcat = gla_combine(dq2, dk2, dv2, dgate, dpa)
    dhcat = tokens(mm("gla_in_dx", V(dpcat, "tok"), V(w_gin), form="nt", out_dtype=BF16), TT)
    g_gin = _cols_to_shards(mm("gla_in_dw", V(hcat, "tok"), V(dpcat, "tok"), form="tn", out_dtype=BF16)[:, :GLA_IN])
    grad_x, g_nmix0, dm0_0, dm1_0 = rowwise("mod_in0_bwd", f_mod, [X(x)], ps_in0, tm=tm, nt=nt,
                                            douts=[X(dhcat, ro=ctx_tiles), X(dx_res)], dx={0: F32}, dp=[0, 1, 2])
    g_nmix0c, dmc0, dmc1 = rowwise("mod_ctx_bwd", f_mod1, [X(ctx)], ps_ctx, tm=tm, nt=ctx_tiles, douts=[X(dhcat)],
                                   dx={}, dp=[0, 1, 2])

    zero_row = jnp.zeros((1, 4 * D), F32)
    dmod = [jnp.concatenate([jnp.concatenate([a.reshape(bsz, D) for a in dms], 1), ctx_row], 0)
            for dms, ctx_row in (([dm0_0, dm1_0, dm2_0, dm3_0, dm4_0, dm5_0], jnp.concatenate([dmc0, dmc1, zero_row], 1)),
                                 ([dm0_1, dm1_1, dm2_1, dm3_1, dm4_1, dm5_1], jnp.zeros((1, 6 * D), F32)))]
    g_wa2 = jnp.stack([g_wd[:RANK, :KD], g_wd[RANK:2 * RANK, KD:]])
    small_grads = [jnp.stack(dmod), jnp.concatenate([g_nmix0 + g_nmix0c, g_nmix1], 0), jnp.concatenate([g_nffn0, g_nffn1], 0),
                   g_head, jnp.concatenate([g_fcb0, g_fcb1], 0), g_final, g_wa2, g_bd.reshape(2, KD), g_scw,
                   jnp.stack([g_fcw0, g_fcw1])]
    pack1, offs1 = _pack_rows(small_grads, F32, 8)
    g1 = all_gather("ag_grads", pack1, True).reshape(N_DEV, pack1.shape[0], D)
    dmod_all = _unpack_rows(g1, offs1[:1], [small_grads[0].shape])[0]
    tot = _unpack_rows(sum_slots("sum_small", g1), offs1, [a.shape for a in small_grads])
    dm_rows = dmod_all[:, :, :bsz].transpose(1, 0, 2, 3).reshape(2, N_DEV * bsz, 6 * D)
    dm_full = jnp.concatenate([dm_rows, tot[0][:, bsz:], jnp.zeros((2, ADA_ROWS - N_DEV * bsz - 1, 6 * D), F32)], 1)
    dm_mine = lax.dynamic_slice(dm_full, (0, 0, me * ADA_COLS), (2, ADA_ROWS, ADA_COLS))
    g_ada_w, g_ada_b, cpart = ada_bwd(cond, dm_mine, dm_full, ada_w)
    cparts = all_gather("ag_cctx", cpart, True).reshape(N_DEV, ADA_ROWS - ADA_CTX_ROW, D)[:, 0]
    g_cctx = cctx_grad(cparts, c_ctx[None])[0]
    tok = send_grads("gla", [g_gin, g_gout], after=g_cctx)

    def my_cols(full, n):
        return lax.dynamic_slice_in_dim(full, me * n, n, axis=full.ndim - 1)

    grads = {
        "c_ctx": g_cctx, "ada_b": g_ada_b.reshape(2, 6 * D), "norm_mix": tot[1], "norm_ffn": tot[2],
        "gla_head_norm": tot[3], "ffn_conv_b": tot[4], "final_norm": tot[5].reshape(D),
        "gla_w_a2": my_cols(tot[6], KD // N_DEV)[None], "gla_b_a": my_cols(tot[7], KD // N_DEV)[None],
        "sc_conv_w": my_cols(tot[8], D // N_DEV)[None], "ffn_conv_w": my_cols(tot[9], 2 * FFN_H // N_DEV),
    }

    res_ada = adamw("adamw_ada", *[a.reshape(2 * D, ADA_COLS) for a in (ada_w, g_ada_w, m_ada_w, v_ada_w)])
    grads["c_ctx"] = g_cctx + tok
    big = ["gla_w_in", "gla_w_out", "sc_w_in", "sc_w_out", "ffn_w_up", "ffn_w_down"]
    small = [n for n in names if n not in big and n != "ada_w"]
    g_small = _pack_rows([grads[n] for n in small], F32, 8)[0]
    res_small = adamw("adamw_small", _pack_rows([w_[n] for n in small], F32, 8)[0], g_small,
                      _pack_rows([m_[n] for n in small], F32, 8)[0], _pack_rows([v_[n] for n in small], F32, 8)[0])
    offs_s = _pack_rows([w_[n] for n in small], F32, 8)[1]

    big_res, after = {}, res_small[0]
    for g in ("l1", "ffn0", "gla"):
        sent, lands = exchange_wait(f"a2a_{g}_wait", a2a_started[g], after, False)
        for (n, i), mine, land in zip(groups[g], sent, lands):
            land = lax.dynamic_update_index_in_dim(land, lax.dynamic_index_in_dim(mine, me, 0, keepdims=False), me, 0)
            big_res[(n, i)] = adamw(f"adamw_{n}{i}", w_[n], land, m_[n], v_[n], layer=i)
            after = big_res[(n, i)][0]

    out = {}
    for kind, idx in (("grad", 0), ("delta", 1), ("new_m", 2), ("new_v", 3)):
        vals = {n: jnp.stack([big_res[(n, i)][idx] for i in range(w_[n].shape[0])]) for n in big}
        vals["ada_w"] = res_ada[idx].reshape(ada_w.shape)
        vals.update(zip(small, _unpack_rows(res_small[idx], offs_s, [w_[n].shape for n in small])))
        out[kind] = [vals[n] for n in names]
    return (loss, grad_x, *out["grad"], *out["delta"], *out["new_m"], *out["new_v"])
```

```python
import functools

import jax
import jax.numpy as jnp
from jax import lax
from jax.experimental import pallas as pl
from jax.experimental.pallas import tpu as pltpu

F32 = jnp.float32
BF16 = jnp.bfloat16

N_DEV = 8
D = 1024
SEQ = 2048
CTX = 256
TT = CTX + SEQ
GRID_W = 64
CHUNK = 64
HEADS = 4
HK = 128
HV = 256
KD = 512
VD = 1024
RANK = 16
TAU = 16.0
GLA_IN = 3104
GLA_IN_PAD = 3200
FFN_H = 2560
FFN_TC = 256
EPS = 1e-6
LR, B1, B2, AEPS, WD, STEP = 0.001, 0.9, 0.999, 1e-08, 0.01, 10
MESH = pl.DeviceIdType.MESH


def _blocks(n):
    return [n] + [t for t in range(n - n % 128, 0, -128) if n % t == 0 and t != n]


def V(arr, kind="flat"):
    if kind == "tok":
        return V(arr.reshape(-1, arr.shape[-1]))
    if kind == "flat":
        r, c = arr.shape
        return dict(a=arr, kind=kind, shape=(r, c), rows=_blocks(r), cols=_blocks(c))
    if kind == "planes":
        bsz, _, t, ch = arr.shape
        return dict(a=arr, kind=kind, shape=(bsz * t, 2 * ch), rows=_blocks(t), cols=[2 * ch] + _blocks(ch), t=t, ch=ch)
    _, r, n = arr.shape
    return dict(a=arr, kind=kind, shape=(r, N_DEV * n), rows=_blocks(r), cols=[8 * n, 4 * n, 2 * n], n=n)


def _view_spec(v, br, bc, idx):
    if v["kind"] == "flat":
        return pl.BlockSpec((br, bc), idx)
    if v["kind"] == "planes":
        nt = v["t"] // br
        if bc == 2 * v["ch"]:
            return pl.BlockSpec((None, 2, br, v["ch"]), lambda i, j, k: (idx(i, j, k)[0] // nt, 0, idx(i, j, k)[0] % nt, 0))
        nch = v["ch"] // bc

        def at(i, j, k):
            r, c = idx(i, j, k)
            return r // nt, c // nch, r % nt, c % nch
        return pl.BlockSpec((None, None, br, bc), at)
    return pl.BlockSpec((bc // v["n"], br, v["n"]), lambda i, j, k: (idx(i, j, k)[1], idx(i, j, k)[0], 0))


def _out_view(kind, rows, cols, dtype, planes_t=None):
    if kind == "flat":
        shape = (rows, cols)
    elif kind == "planes":
        shape = (rows // planes_t, 2, planes_t, cols // 2)
    else:
        shape = (N_DEV, rows, cols // N_DEV)
    return V(jax.ShapeDtypeStruct(shape, dtype), kind)


MM_VMEM_BUDGET = 40 * 2 ** 20
MM_VMEM_LIMIT = 56 * 2 ** 20
MM_MAX_TILE = 1536


def _mm_tiles(m, n, kk, ms, ns, ks, a_bytes, b_bytes, o_bytes):
    best = None
    for tk in ks:
        for tm in [t for t in ms if t <= MM_MAX_TILE]:
            for tn in [t for t in ns if t <= MM_MAX_TILE]:
                one_k = tk == kk
                need = 2 * (tm * tk * a_bytes + tk * tn * b_bytes + tm * tn * o_bytes) + (0 if one_k else tm * tn * 4)
                if need > MM_VMEM_BUDGET:
                    continue
                steps = (m // tm) * (n // tn) * (kk // tk)
                traffic = (m * kk * a_bytes * (1 if one_k else n // tn)
                           + kk * n * b_bytes * (1 if one_k and n == tn else m // tm) + m * n * o_bytes)
                fill = (tm * tk * a_bytes + tk * tn * b_bytes) / 2.5e12
                cost = max(2.0 * m * n * kk / (9e14 if one_k else 6.5e14), traffic / 2.5e12) + steps * 0.4e-6 + fill
                if best is None or cost < best[0]:
                    best = (cost, tm, tn, tk)
    return best[1:]


def mm(name, a, b, form="nn", out="flat", out_dtype=F32, planes_t=None):
    (m, kk) = a["shape"][::-1] if form == "tn" else a["shape"]
    n = b["shape"][0] if form == "nt" else b["shape"][1]
    assert (b["shape"][1] if form == "nt" else b["shape"][0]) == kk, (name, a["shape"], b["shape"])
    o = _out_view(out, m, n, out_dtype, planes_t)
    a_m, a_k = (a["cols"], a["rows"]) if form == "tn" else (a["rows"], a["cols"])
    b_k, b_n = (b["cols"], b["rows"]) if form == "nt" else (b["rows"], b["cols"])
    tm, tn, tk = _mm_tiles(m, n, kk, [t for t in a_m if t in o["rows"]], [t for t in b_n if t in o["cols"]],
                           [t for t in a_k if t in b_k], a["a"].dtype.itemsize, b["a"].dtype.itemsize,
                           jnp.dtype(out_dtype).itemsize)
    nk = kk // tk
    dn = (((0 if form == "tn" else 1,), (1 if form == "nt" else 0,)), ((), ()))

    def load(ref):
        if len(ref.shape) == 3:
            return jnp.concatenate([ref[p] for p in range(ref.shape[0])], axis=-1).astype(BF16)
        return ref[...].astype(BF16)

    def store(o_ref, val):
        val = val.astype(out_dtype)
        if len(o_ref.shape) == 3:
            w = o_ref.shape[-1]
            for p in range(o_ref.shape[0]):
                o_ref[p] = val[:, p * w:(p + 1) * w]
        else:
            o_ref[...] = val

    def body(a_ref, b_ref, o_ref, *acc):
        if nk == 1:
            store(o_ref, lax.dot_general(load(a_ref), load(b_ref), dn, preferred_element_type=F32))
            return
        k, acc_ref = pl.program_id(2), acc[0]

        @pl.when(k == 0)
        def _():
            acc_ref[...] = jnp.zeros_like(acc_ref)

        acc_ref[...] += lax.dot_general(load(a_ref), load(b_ref), dn, preferred_element_type=F32)

        @pl.when(k == nk - 1)
        def _():
            store(o_ref, acc_ref[...])

    if form == "tn":
        a_spec = _view_spec(a, tk, tm, lambda i, j, k: (k, i))
    else:
        a_spec = _view_spec(a, tm, tk, lambda i, j, k: (i, k))
    if form == "nt":
        b_spec = _view_spec(b, tn, tk, lambda i, j, k: (j, k))
    else:
        b_spec = _view_spec(b, tk, tn, lambda i, j, k: (k, j))
    return pl.pallas_call(
        body, name=name, grid=(m // tm, n // tn, nk),
        in_specs=[a_spec, b_spec], out_specs=_view_spec(o, tm, tn, lambda i, j, k: (i, j)), out_shape=o["a"],
        scratch_shapes=[pltpu.VMEM((tm, tn), F32)] if nk > 1 else [],
        compiler_params=pltpu.CompilerParams(dimension_semantics=("parallel", "parallel", "arbitrary"),
                                             vmem_limit_bytes=MM_VMEM_LIMIT),
    )(a["a"], b["a"])


def X(arr, w=None, co=0, ro=0, split=1, planes=False):
    return dict(a=arr, w=arr.shape[-1] if w is None else w, co=co, ro=ro, split=2 if planes else split,
                mode="planes" if planes else "cols")


def P(arr, per_example=False, w=None, split=1, rows=False):
    return dict(a=arr, e=per_example, w=arr.shape[-1] if w is None else w, split=arr.shape[-2] if rows else split,
                mode="rows" if rows else "cols")


def _pieces(ref, s):
    if s["mode"] == "planes":
        return [ref[0], ref[1]]
    if s["mode"] == "rows":
        return [ref[i:i + 1, :] for i in range(s["split"])]
    w = ref.shape[-1] // s["split"]
    return [ref[:, i * w:(i + 1) * w] for i in range(s["split"])]


def _store(ref, pieces, s, accumulate=False):
    w = ref.shape[-1] // len(pieces)
    for i, p in enumerate(pieces):
        at = (i,) if s["mode"] == "planes" else (slice(i, i + 1),) if s["mode"] == "rows" else (slice(None), slice(i * w, (i + 1) * w))
        if accumulate:
            ref[at] += p.astype(ref.dtype)
        else:
            ref[at] = p.astype(ref.dtype)


def rowwise(name, f, xs, ps, *, tm, nt, nc=1, outs=None, douts=None, dx=None, dp=None):
    bsz = xs[0]["a"].shape[0]
    fwd = douts is None
    nx, np_ = len(xs), len(ps)
    douts = [] if fwd else douts
    dx = {} if fwd else dx
    dp = [] if fwd else dp

    def x_spec(s):
        if s["mode"] == "planes":
            return pl.BlockSpec((None, 2, tm, s["w"]), lambda c, b, t, s=s: (b, 0, t + s["ro"], c + s["co"]))
        return pl.BlockSpec((None, tm, s["w"]), lambda c, b, t, s=s: (b, t + s["ro"], c + s["co"]))

    def x_out(s, dt):
        if s["mode"] == "planes":
            return (jax.ShapeDtypeStruct((bsz, 2, nt * tm, nc * s["w"]), dt),
                    pl.BlockSpec((None, 2, tm, s["w"]), lambda c, b, t: (b, 0, t, c)))
        return (jax.ShapeDtypeStruct((bsz, nt * tm, nc * s["w"]), dt), pl.BlockSpec((None, tm, s["w"]), lambda c, b, t: (b, t, c)))

    def p_spec(s):
        r = s["a"].shape[-2]
        if s["e"]:
            return pl.BlockSpec((None, r, s["w"]), lambda c, b, t: (b, 0, c))
        return pl.BlockSpec((r, s["w"]), lambda c, b, t: (0, c))

    in_specs = [x_spec(s) for s in xs] + [p_spec(s) for s in ps] + [x_spec(s) for s in douts]
    operands = [s["a"] for s in xs] + [s["a"] for s in ps] + [s["a"] for s in douts]
    if fwd:
        out_modes = [dict(mode="cols", split=sp) for (_, _, sp) in outs]
        out_shape = [jax.ShapeDtypeStruct((bsz, nt * tm, nc * w), dt) for (w, dt, _) in outs]
        out_specs = [pl.BlockSpec((None, tm, w), lambda c, b, t: (b, t, c)) for (w, _, _) in outs]
    else:
        dx_outs = [x_out(xs[i], dt) for i, dt in dx.items()]
        out_shape, out_specs = [o[0] for o in dx_outs], [o[1] for o in dx_outs]
        for j in dp:
            s = ps[j]
            r = s["a"].shape[-2]
            if s["e"]:
                out_shape.append(jax.ShapeDtypeStruct((bsz, r, nc * s["w"]), F32))
                out_specs.append(pl.BlockSpec((None, r, s["w"]), lambda c, b, t: (b, 0, c)))
            else:
                out_shape.append(jax.ShapeDtypeStruct((r, nc * s["w"]), F32))
                out_specs.append(pl.BlockSpec((r, s["w"]), lambda c, b, t: (0, c)))

    def body(*refs):
        x_refs, p_refs = refs[:nx], refs[nx:nx + np_]
        d_refs = refs[nx + np_:nx + np_ + len(douts)]
        o_refs = refs[nx + np_ + len(douts):]
        xv = [[p.astype(F32) for p in _pieces(r, s)] for r, s in zip(x_refs, xs)]
        pv = [[p.astype(F32) for p in _pieces(r, s)] for r, s in zip(p_refs, ps)]
        if fwd:
            for r, pieces, s in zip(o_refs, f(xv, pv), out_modes):
                _store(r, pieces, s)
            return
        _, vjp = jax.vjp(f, xv, pv)
        cot = [[p.astype(F32) for p in _pieces(r, s)] for r, s in zip(d_refs, douts)]
        dxv, dpv = vjp(cot)
        for r, i in zip(o_refs, dx):
            _store(r, dxv[i], xs[i])
        b, t = pl.program_id(1), pl.program_id(2)
        for r, j in zip(o_refs[len(dx):], dp):
            first = (t == 0) if ps[j]["e"] else jnp.logical_and(b == 0, t == 0)

            @pl.when(first)
            def _(r=r, j=j):
                _store(r, dpv[j], ps[j])

            @pl.when(jnp.logical_not(first))
            def _(r=r, j=j):
                _store(r, dpv[j], ps[j], accumulate=True)

    res = pl.pallas_call(
        body, name=name, grid=(nc, bsz, nt), in_specs=in_specs, out_specs=out_specs, out_shape=out_shape,
        compiler_params=pltpu.CompilerParams(dimension_semantics=("arbitrary", "arbitrary", "arbitrary")),
    )(*operands)
    return res


def _keep_rows(a, shift, keep):
    n = a.shape[0]
    t = lax.broadcasted_iota(jnp.int32, a.shape, 0)
    return jnp.where(keep(t, n), pltpu.roll(a, shift % n, 0), 0.0)


def _shift_pair(step, keep_prev, keep_next):
    @jax.custom_vjp
    def prev(a):
        return _keep_rows(a, step, keep_prev)

    @jax.custom_vjp
    def nxt(a):
        return _keep_rows(a, -step, keep_next)

    prev.defvjp(lambda a: (prev(a), None), lambda _, g: (nxt(g),))
    nxt.defvjp(lambda a: (nxt(a), None), lambda _, g: (prev(g),))
    return prev, nxt


prev_tok, next_tok = _shift_pair(1, lambda t, n: t % GRID_W != 0, lambda t, n: t % GRID_W != GRID_W - 1)
prev_row, next_row = _shift_pair(GRID_W, lambda t, n: t >= GRID_W, lambda t, n: t < n - GRID_W)


@jax.custom_vjp
def bdot(a, w):
    return jnp.dot(a.astype(BF16), w.astype(BF16), preferred_element_type=F32)


def _bdot_bwd(res, g):
    a, w = res
    gb = g.astype(BF16)
    da = lax.dot_general(gb, w.astype(BF16), (((1,), (1,)), ((), ())), preferred_element_type=F32)
    dw = lax.dot_general(a.astype(BF16), gb, (((0,), (0,)), ((), ())), preferred_element_type=F32)
    return da, dw


bdot.defvjp(lambda a, w: (bdot(a, w), (a, w)), _bdot_bwd)


@jax.custom_vjp
def log_sigmoid(z):
    return jnp.minimum(z, 0.0) - jnp.log(1.0 + jnp.exp(-jnp.abs(z)))


def _lsig_bwd(z, g):
    e = jnp.exp(-jnp.abs(z))
    return (g * jnp.where(z >= 0, e, 1.0) / (1.0 + e),)


log_sigmoid.defvjp(lambda z: (log_sigmoid(z), z), _lsig_bwd)


def silu(x):
    return x * jax.nn.sigmoid(x)


def _rms(x):
    return x * lax.rsqrt(jnp.mean(x * x, axis=-1, keepdims=True) + EPS)


def _mod(x, gain, shift, scale):
    return _rms(x) * gain * (1.0 + scale) + shift


def f_mod(xs, ps):
    ((h,),), ((gain,), (shift,), (scale,)) = xs, ps
    return [[_mod(h, gain, shift, scale)], [h]]


def f_res_mod(xs, ps):
    ((h,), (y,)), ((gate,), (gain,), (shift,), (scale,)) = xs, ps
    h1 = h + gate * y
    return [[h1], [_mod(h1, gain, shift, scale)]]


def f_ffn_mid(xs, ps):
    ((ua, ug),), ((w0a, w0g), (w1a, w1g), (w2a, w2g), (ba, bg)) = xs, ps
    a = w0a * prev_row(ua) + w1a * ua + w2a * next_row(ua) + ba
    g = w0g * prev_row(ug) + w1g * ug + w2g * next_row(ug) + bg
    return [[a * silu(g)]]


def f_sc_mid(xs, ps):
    ((bg, cg, v),), ((w0,), (w1,), (w2,)) = xs, ps
    z = cg * v
    return [[bg * (w0 * prev_tok(z) + w1 * z + w2 * next_tok(z))]]


def f_decay(xs, ps):
    ((a,),), ((wd,), (bd,)) = xs, ps
    return [[log_sigmoid(bdot(a, wd) + bd) / TAU]]


def f_gla_post(xs, ps):
    (of, ob, g), ((gain,),) = xs, ps
    return [[_rms(a + b) * gain * silu(c) for a, b, c in zip(of, ob, g)]]


NCH = TT // CHUNK
CTX_CH = CTX // CHUNK
_NT = (((1,), (1,)), ((), ()))
_TN = (((0,), (0,)), ((), ()))
_NN = (((1,), (0,)), ((), ()))


def _chunk_of(d, j):
    return jnp.where(d == 0, j, jnp.where(j < CTX_CH, CTX_CH - 1 - j, NCH + CTX_CH - 1 - j))


def _dot(a, b, dn):
    return lax.dot_general(a, b, dn, preferred_element_type=F32)


def _mask_dot(m, g):
    g0 = g.astype(BF16)
    r1 = g - g0.astype(F32)
    g1 = r1.astype(BF16)
    g2 = (r1 - g1.astype(F32)).astype(BF16)
    return _dot(m, g0, _NN) + _dot(m, g1, _NN) + _dot(m, g2, _NN)


def _causal(d):
    row = lax.broadcasted_iota(jnp.int32, (CHUNK, CHUNK), 0)
    col = lax.broadcasted_iota(jnp.int32, (CHUNK, CHUNK), 1)
    delta = jnp.where(d == 0, col - row, row - col)
    return delta <= 0, delta >= 0


def _gla_in_specs(bsz, rev):
    def blk(d, j):
        return _chunk_of(d, (NCH - 1 - j) if rev else j)

    return [
        pl.BlockSpec((bsz, CHUNK, KD), lambda d, j: (0, blk(d, j), 0)),
        pl.BlockSpec((bsz, CHUNK, KD), lambda d, j: (0, blk(d, j), 1)),
        pl.BlockSpec((bsz, CHUNK, VD), lambda d, j: (0, blk(d, j), 1)),
        pl.BlockSpec((bsz, CHUNK, KD), lambda d, j: (0, blk(d, j), d)),
    ], blk


def gla_fwd(pcat, la):
    bsz = pcat.shape[0]
    in_specs, blk = _gla_in_specs(bsz, False)

    def body(q_ref, k_ref, v_ref, la_ref, o_ref, s_ref, st):
        d, j = pl.program_id(0), pl.program_id(1)

        @pl.when(j == 0)
        def _():
            st[...] = jnp.zeros_like(st)

        s_ref[...] = st[...]
        causal, _ = _causal(d)
        mf = causal.astype(BF16)
        for e, h in [(e, h) for e in range(bsz) for h in range(HEADS)]:
            ks_, vs_ = slice(h * HK, (h + 1) * HK), slice(h * HV, (h + 1) * HV)
            q, k, v, g = q_ref[e, :, ks_] * (HK ** -0.5), k_ref[e, :, ks_], v_ref[e, :, vs_].astype(BF16), la_ref[e, :, ks_]
            b = _mask_dot(mf, g)
            bl = jnp.sum(g, axis=0, keepdims=True)
            qs = (q * jnp.exp(b)).astype(BF16)
            ks = (k * jnp.exp(-b)).astype(BF16)
            kd = (k * jnp.exp(bl - b)).astype(BF16)
            s = st[e, h]
            att = jnp.where(causal, _dot(qs, ks, _NT), 0.0).astype(BF16)
            o_ref[e, :, vs_] = _dot(qs, s.astype(BF16), _NT) + _dot(att, v, _NN)
            st[e, h] = jnp.exp(bl) * s + _dot(v, kd, _TN)

    return pl.pallas_call(
        body, name="gla_fwd", grid=(2, NCH), in_specs=in_specs,
        out_specs=[pl.BlockSpec((bsz, CHUNK, VD), lambda d, j: (0, blk(d, j), d)),
                   pl.BlockSpec((bsz, None, None, HEADS, HV, HK), lambda d, j: (0, d, j, 0, 0, 0))],
        out_shape=[jax.ShapeDtypeStruct((bsz, TT, 2 * VD), F32), jax.ShapeDtypeStruct((bsz, 2, NCH, HEADS, HV, HK), F32)],
        scratch_shapes=[pltpu.VMEM((bsz, HEADS, HV, HK), F32)],
        compiler_params=pltpu.CompilerParams(dimension_semantics=("arbitrary", "arbitrary")),
    )(pcat, pcat, pcat, la)


def gla_bwd(pcat, la, s_all, do):
    bsz = pcat.shape[0]
    in_specs, blk = _gla_in_specs(bsz, True)
    in_specs += [
        pl.BlockSpec((bsz, None, None, HEADS, HV, HK), lambda d, j: (0, d, NCH - 1 - j, 0, 0, 0)),
        pl.BlockSpec((bsz, CHUNK, VD), lambda d, j: (0, jnp.maximum(blk(d, j) - CTX_CH, 0), 0)),
    ]

    def body(q_ref, k_ref, v_ref, la_ref, s_ref, do_ref, dq_ref, dk_ref, dv_ref, dla_ref, dst):
        d, j = pl.program_id(0), pl.program_id(1)

        @pl.when(j == 0)
        def _():
            dst[...] = jnp.zeros_like(dst)

        latent = blk(d, j) >= CTX_CH
        causal, causal_t = _causal(d)
        mt = causal_t.astype(BF16)
        mf = causal.astype(BF16)
        scale = HK ** -0.5
        for e, h in [(e, h) for e in range(bsz) for h in range(HEADS)]:
            ks_, vs_ = slice(h * HK, (h + 1) * HK), slice(h * HV, (h + 1) * HV)
            q, k, v, g = q_ref[e, :, ks_] * scale, k_ref[e, :, ks_], v_ref[e, :, vs_].astype(BF16), la_ref[e, :, ks_]
            b = _mask_dot(mf, g)
            bl = jnp.sum(g, axis=0, keepdims=True)
            ex, ei, ed, el = jnp.exp(b), jnp.exp(-b), jnp.exp(bl - b), jnp.exp(bl)
            qs, ks, kd = q * ex, k * ei, k * ed
            qsb, ksb, kdb = qs.astype(BF16), ks.astype(BF16), kd.astype(BF16)
            s, ds1 = s_ref[e, h], dst[e, h]
            sb, ds1b = s.astype(BF16), ds1.astype(BF16)
            dob = jnp.where(latent, do_ref[e, :, vs_], 0.0).astype(BF16)
            att = jnp.where(causal, _dot(qsb, ksb, _NT), 0.0).astype(BF16)
            datt = jnp.where(causal, _dot(dob, v, _NT), 0.0).astype(BF16)
            dqs = _dot(dob, sb, _NN) + _dot(datt, ksb, _NN)
            dks = _dot(datt, qsb, _TN)
            dv_ref[e, :, vs_] = _dot(att, dob, _TN) + _dot(kdb, ds1b, _NT)
            dkd = _dot(v, ds1b, _NN)
            dst[e, h] = _dot(dob, qsb, _TN) + el * ds1
            del_ = jnp.sum(s * ds1, axis=0, keepdims=True)
            dq_ref[e, :, ks_] = dqs * ex * scale
            dk_ref[e, :, ks_] = dks * ei + dkd * ed
            db = dqs * qs - dks * ks - dkd * kd
            dbl = jnp.sum(dkd * kd, axis=0, keepdims=True) + del_ * el
            dla_ref[e, :, ks_] = _mask_dot(mt, db) + dbl

    return pl.pallas_call(
        body, name="gla_bwd", grid=(2, NCH), in_specs=in_specs,
        out_specs=[pl.BlockSpec((None, bsz, CHUNK, KD), lambda d, j: (d, 0, blk(d, j), 0)),
                   pl.BlockSpec((None, bsz, CHUNK, KD), lambda d, j: (d, 0, blk(d, j), 0)),
                   pl.BlockSpec((None, bsz, CHUNK, VD), lambda d, j: (d, 0, blk(d, j), 0)),
                   pl.BlockSpec((bsz, CHUNK, KD), lambda d, j: (0, blk(d, j), d))],
        out_shape=[jax.ShapeDtypeStruct((2, bsz, TT, KD), F32), jax.ShapeDtypeStruct((2, bsz, TT, KD), F32),
                   jax.ShapeDtypeStruct((2, bsz, TT, VD), F32), jax.ShapeDtypeStruct((bsz, TT, 2 * KD), F32)],
        scratch_shapes=[pltpu.VMEM((bsz, HEADS, HV, HK), F32)],
        compiler_params=pltpu.CompilerParams(dimension_semantics=("arbitrary", "arbitrary")),
    )(pcat, pcat, pcat, la, s_all, do)


def gla_combine(dq2, dk2, dv2, dgate, dpa):
    bsz = dgate.shape[0]
    tm = CTX

    def body(dq_ref, dk_ref, dv_ref, dg_ref, dpa_ref, o_ref):
        t = pl.program_id(1)
        o_ref[:, 0:KD] = (dq_ref[0] + dq_ref[1]).astype(BF16)
        o_ref[:, KD:2 * KD] = (dk_ref[0] + dk_ref[1]).astype(BF16)
        o_ref[:, 2 * KD:2 * KD + VD] = (dv_ref[0] + dv_ref[1]).astype(BF16)
        o_ref[:, 2 * KD + VD:2 * KD + 2 * VD] = jnp.where(t > 0, dg_ref[...], 0).astype(BF16)
        o_ref[:, 2 * KD + 2 * VD:] = dpa_ref[...].astype(BF16)

    return pl.pallas_call(
        body, name="gla_combine", grid=(bsz, TT // tm),
        in_specs=[pl.BlockSpec((2, None, tm, KD), lambda b, t: (0, b, t, 0)),
                  pl.BlockSpec((2, None, tm, KD), lambda b, t: (0, b, t, 0)),
                  pl.BlockSpec((2, None, tm, VD), lambda b, t: (0, b, t, 0)),
                  pl.BlockSpec((None, tm, VD), lambda b, t: (b, jnp.maximum(t - 1, 0), 0)),
                  pl.BlockSpec((None, tm, 128), lambda b, t: (b, t, 0))],
        out_specs=pl.BlockSpec((None, tm, GLA_IN_PAD), lambda b, t: (b, t, 0)),
        out_shape=jax.ShapeDtypeStruct((bsz, TT, GLA_IN_PAD), BF16),
        compiler_params=pltpu.CompilerParams(dimension_semantics=("arbitrary", "arbitrary")),
    )(dq2, dk2, dv2, dgate, dpa)


def final_loss(h1, fo, gate, gain, tgt):
    bsz, t_len, _ = h1.shape
    tm = 256

    def body(h_ref, f_ref, gate_ref, gain_ref, tgt_ref, loss_ref, dh_ref, df_ref, dgate_ref, dgain_ref):
        b, t = pl.program_id(0), pl.program_id(1)
        target = tgt_ref[...]

        def core(h, fo_, gate_, gain_):
            e = _rms(h + gate_ * fo_) * gain_ - target
            return jnp.sum(0.5 * jnp.sum(e * e, axis=-1, keepdims=True) / D, axis=0, keepdims=True)

        loss, vjp = jax.vjp(core, h_ref[...], f_ref[...], gate_ref[...], gain_ref[...])
        dh, df, dgate, dgain = vjp(jnp.ones((1, 1), F32))
        dh_ref[...] = dh
        df_ref[...] = df.astype(BF16)
        first = jnp.logical_and(b == 0, t == 0)

        @pl.when(first)
        def _():
            loss_ref[...] = jnp.broadcast_to(loss, loss_ref.shape)
            dgain_ref[...] = dgain

        @pl.when(jnp.logical_not(first))
        def _():
            loss_ref[...] += jnp.broadcast_to(loss, loss_ref.shape)
            dgain_ref[...] += dgain

        @pl.when(t == 0)
        def _():
            dgate_ref[...] = dgate

        @pl.when(t > 0)
        def _():
            dgate_ref[...] += dgate

    tile = pl.BlockSpec((None, tm, D), lambda b, t: (b, t, 0))
    per_ex = pl.BlockSpec((None, 1, D), lambda b, t: (b, 0, 0))
    shared = pl.BlockSpec((1, D), lambda b, t: (0, 0))
    return pl.pallas_call(
        body, name="final_loss", grid=(bsz, t_len // tm),
        in_specs=[tile, tile, per_ex, shared, tile],
        out_specs=[pl.BlockSpec((8, 128), lambda b, t: (0, 0)), tile, tile, per_ex, shared],
        out_shape=[jax.ShapeDtypeStruct((8, 128), F32), jax.ShapeDtypeStruct(h1.shape, F32),
                   jax.ShapeDtypeStruct(h1.shape, BF16), jax.ShapeDtypeStruct((bsz, 1, D), F32),
                   jax.ShapeDtypeStruct((1, D), F32)],
        compiler_params=pltpu.CompilerParams(dimension_semantics=("arbitrary", "arbitrary")),
    )(h1, fo, gate, gain, tgt)


ADA_ROWS = 24
ADA_CTX_ROW = 16
ADA_COLS = 6 * D // N_DEV


def ada_fwd(cond, w, b):
    def body(c_ref, w_ref, b_ref, o_ref):
        s = silu(c_ref[...]).astype(BF16)
        o_ref[...] = jnp.dot(s, w_ref[...].astype(BF16), preferred_element_type=F32) + b_ref[...]

    return pl.pallas_call(
        body, name="ada_fwd", grid=(2,),
        in_specs=[pl.BlockSpec((ADA_ROWS, D), lambda i: (0, 0)), pl.BlockSpec((None, D, ADA_COLS), lambda i: (i, 0, 0)),
                  pl.BlockSpec((None, 1, ADA_COLS), lambda i: (i, 0, 0))],
        out_specs=pl.BlockSpec((None, ADA_ROWS, ADA_COLS), lambda i: (i, 0, 0)),
        out_shape=jax.ShapeDtypeStruct((2, ADA_ROWS, ADA_COLS), F32),
    )(cond, w, b)


def ada_bwd(cond, dm_mine, dm_full, w):
    def body(c_ref, dm_ref, dmf_ref, w_ref, gw_ref, gb_ref, cp_ref):
        i = pl.program_id(0)
        s = silu(c_ref[...]).astype(BF16)
        dm = dm_ref[...].astype(BF16)
        gw_ref[...] = _dot(s, dm, _TN)
        gb_ref[...] = jnp.sum(dmf_ref[...], axis=0, keepdims=True)

        @pl.when(i == 0)
        def _():
            cp_ref[...] = _dot(dm_ref[ADA_CTX_ROW:, :].astype(BF16), w_ref[...].astype(BF16), _NT)

    return pl.pallas_call(
        body, name="ada_bwd", grid=(2,),
        in_specs=[pl.BlockSpec((ADA_ROWS, D), lambda i: (0, 0)), pl.BlockSpec((None, ADA_ROWS, ADA_COLS), lambda i: (i, 0, 0)),
                  pl.BlockSpec((None, ADA_ROWS, 6 * D), lambda i: (i, 0, 0)), pl.BlockSpec((None, D, ADA_COLS), lambda i: (i, 0, 0))],
        out_specs=[pl.BlockSpec((None, D, ADA_COLS), lambda i: (i, 0, 0)), pl.BlockSpec((None, 1, 6 * D), lambda i: (i, 0, 0)),
                   pl.BlockSpec((ADA_ROWS - ADA_CTX_ROW, D), lambda i: (0, 0))],
        out_shape=[jax.ShapeDtypeStruct((2, D, ADA_COLS), F32), jax.ShapeDtypeStruct((2, 1, 6 * D), F32),
                   jax.ShapeDtypeStruct((ADA_ROWS - ADA_CTX_ROW, D), F32)],
        compiler_params=pltpu.CompilerParams(dimension_semantics=("arbitrary",)),
    )(cond, dm_mine, dm_full, w)


def cctx_grad(parts, c_ctx):
    def body(p_ref, c_ref, o_ref):
        tot = p_ref[0:1, :]
        for i in range(1, N_DEV):
            tot = tot + p_ref[i:i + 1, :]
        c = c_ref[...]
        sg = jax.nn.sigmoid(c)
        o_ref[...] = tot * sg * (1.0 + c * (1.0 - sg))

    return pl.pallas_call(body, name="cctx_grad", out_shape=jax.ShapeDtypeStruct((1, D), F32))(parts, c_ctx)


def _row_tile(r):
    for t in (512, 256, 128, 80, 64, 40, 32, 16, 8):
        if r % t == 0:
            return t
    return r


def _slot_sum(ref):
    tot = ref[0].astype(F32)
    for i in range(1, ref.shape[0]):
        tot = tot + ref[i].astype(F32)
    return tot


def sum_slots(name, x):
    s, r, c = x.shape
    tr = _row_tile(r)

    def body(x_ref, o_ref):
        o_ref[...] = _slot_sum(x_ref)

    return pl.pallas_call(
        body, name=name, grid=(r // tr,), in_specs=[pl.BlockSpec((s, tr, c), lambda i: (0, i, 0))],
        out_specs=pl.BlockSpec((tr, c), lambda i: (i, 0)), out_shape=jax.ShapeDtypeStruct((r, c), F32),
    )(x)


def adamw(name, w, g, m, v, layer=None):
    r, c = w.shape[-2:]
    tr = _row_tile(r)
    stacked = g.ndim == 3

    def body(w_ref, g_ref, m_ref, v_ref, go_ref, d_ref, mo_ref, vo_ref):
        gv = _slot_sum(g_ref) if stacked else g_ref[...]
        mn = B1 * m_ref[...] + (1.0 - B1) * gv
        vn = B2 * v_ref[...] + (1.0 - B2) * jnp.square(gv)
        m_hat = mn / (1.0 - B1 ** STEP)
        v_hat = vn / (1.0 - B2 ** STEP)
        go_ref[...] = gv
        d_ref[...] = -LR * (m_hat / (jnp.sqrt(v_hat) + AEPS) + WD * w_ref[...])
        mo_ref[...] = mn
        vo_ref[...] = vn

    tile = pl.BlockSpec((tr, c), lambda i: (i, 0))
    slab = tile if layer is None else pl.BlockSpec((None, tr, c), lambda i: (layer, i, 0))
    g_spec = pl.BlockSpec((g.shape[0], tr, c), lambda i: (0, i, 0)) if stacked else tile
    return pl.pallas_call(
        body, name=name, grid=(r // tr,), in_specs=[slab, g_spec, slab, slab], out_specs=[tile] * 4,
        out_shape=[jax.ShapeDtypeStruct((r, c), F32)] * 4,
    )(w, g, m, v)


def _place():
    return lax.axis_index("x"), lax.axis_index("y"), lax.axis_index("c")


def all_gather(name, x, in_vmem):
    r, c = x.shape
    space = pltpu.VMEM if in_vmem else pl.ANY

    def body(x_ref, out_ref, send_sems, recv_sems, local_sem):
        px, py, pc = _place()
        me, sibling = (px, py, pc), (px, py, 1 - pc)
        chips = [(1 - px, py), (px, 1 - py), (1 - px, 1 - py)]

        def rows(qx, qy, qc):
            return out_ref.at[pl.ds((4 * qx + 2 * qy + qc) * r, r), :]

        def copy(k, block, to, src=None):
            return pltpu.make_async_remote_copy(
                src_ref=rows(*block) if src is None else src, dst_ref=rows(*block),
                send_sem=send_sems.at[k], recv_sem=recv_sems.at[k], device_id=to, device_id_type=MESH)

        mine = pltpu.make_async_copy(x_ref, rows(*me), local_sem)
        mine.start()
        first = [copy(0, me, sibling, src=x_ref)]
        first += [copy(1 + j, me, (*chip, pc), src=x_ref) for j, chip in enumerate(chips)]
        for cp in first:
            cp.start()
        passed = [copy(4 + j, (*chip, pc), sibling) for j, chip in enumerate(chips)]
        for j, chip in enumerate(chips):
            copy(1 + j, (*chip, pc), me).wait_recv()
            passed[j].start()
        copy(0, sibling, me).wait_recv()
        for j, chip in enumerate(chips):
            copy(4 + j, (*chip, 1 - pc), me).wait_recv()
        for cp in first + passed:
            cp.wait_send()
        mine.wait()

    return pl.pallas_call(
        body, name=name, out_shape=jax.ShapeDtypeStruct((N_DEV * r, c), x.dtype),
        in_specs=[pl.BlockSpec(memory_space=space)], out_specs=pl.BlockSpec(memory_space=space),
        scratch_shapes=[pltpu.SemaphoreType.DMA((7,)), pltpu.SemaphoreType.DMA((7,)), pltpu.SemaphoreType.DMA],
    )(x)


_HBM =pl.BlockSpec(memory_space=pltpu.HBM)
_SEM = pl.BlockSpec(memory_space=pltpu.SEMAPHORE)
_EFFECT = pltpu.SideEffectType.DATAFLOW_SIDE_EFFECTING


def _peers():
    px, py, pc = _place()
    return [(1 - px if k & 4 else px, 1 - py if k & 2 else py, 1 - pc if k & 1 else pc) for k in range(1, N_DEV)]


def _slot(dev):
    return 4 * dev[0] + 2 * dev[1] + dev[2]


def _split_copies(src_refs, land_refs, send_sems, recv_sems, gather):
    me = _slot(_place())
    return [pltpu.make_async_remote_copy(
        src_ref=src if gather else src.at[_slot(peer)], dst_ref=land.at[me],
        send_sem=send_sems.at[a * (N_DEV - 1) + k], recv_sem=recv_sems.at[a * (N_DEV - 1) + k],
        device_id=peer, device_id_type=MESH)
        for a, (src, land) in enumerate(zip(src_refs, land_refs)) for k, peer in enumerate(_peers())]


def exchange_start(name, srcs, gather):
    n = len(srcs)
    lands = [pltpu.HBM((N_DEV,) + s.shape if gather else s.shape, s.dtype) for s in srcs]

    def body(*refs):
        send_sems, recv_sems = refs[2 * n:2 * n + 2]
        for cp in _split_copies(refs[:n], refs[n:2 * n], send_sems, recv_sems, gather):
            cp.start()
        refs[-1][...] = jnp.zeros_like(refs[-1])

    sems = pltpu.SemaphoreType.DMA((n * (N_DEV - 1),))
    res = pl.pallas_call(
        body, name=name,
        out_shape=(sems, sems, *[pltpu.HBM(s.shape, s.dtype) for s in srcs], *lands, jax.ShapeDtypeStruct((8, 128), F32)),
        in_specs=(_HBM,) * (2 * n), out_specs=(_SEM, _SEM) + (_HBM,) * (2 * n) + (pl.BlockSpec(memory_space=pltpu.VMEM),),
        input_output_aliases={i: 2 + i for i in range(2 * n)},
        compiler_params=pltpu.CompilerParams(has_side_effects=_EFFECT),
    )(*[pltpu.with_memory_space_constraint(s, pltpu.HBM) for s in srcs],
      *[pltpu.with_memory_space_constraint(lax.empty(ld.shape, ld.dtype), pltpu.HBM) for ld in lands])
    return res[0], res[1], list(res[2:2 + n]), list(res[2 + n:2 + 2 * n]), res[-1]


def exchange_wait(name, started, after, gather):
    send_sems, recv_sems, srcs, lands, _ = started
    n = len(srcs)

    def body(*refs):
        send_sems, recv_sems = refs[2 * n:2 * n + 2]
        for cp in _split_copies(refs[:n], refs[n:2 * n], send_sems, recv_sems, gather):
            cp.wait_send()
            cp.wait_recv()

    res = pl.pallas_call(
        body, name=name, out_shape=tuple(pltpu.HBM(a.shape, a.dtype) for a in srcs + lands),
        in_specs=(_HBM,) * (2 * n) + (_SEM, _SEM, pl.BlockSpec(memory_space=pl.ANY)), out_specs=(_HBM,) * (2 * n),
        input_output_aliases={i: i for i in range(2 * n)},
        compiler_params=pltpu.CompilerParams(has_side_effects=_EFFECT),
    )(*srcs, *lands, send_sems, recv_sems, after)
    return list(res[:n]), list(res[n:])


NCF = FFN_H // FFN_TC


def _size(shape):
    n = 1
    for s in shape:
        n *= s
    return n


def _padded_rows(n_elems, row_mult):
    return -(-n_elems // (D * row_mult)) * row_mult


def _pack_rows(arrs, dtype, row_mult):
    rows, offs, r0 = [], [], 0
    for a in arrs:
        flat = a.reshape(-1).astype(dtype)
        n = _padded_rows(flat.shape[0], row_mult)
        rows.append(jnp.pad(flat, (0, n * D - flat.shape[0])).reshape(n, D))
        offs.append(r0)
        r0 += n
    return jnp.concatenate(rows, 0), offs


def _unpack_rows(buf, offs, shapes):
    lead, out = buf.shape[:-2], []
    for o, shp in zip(offs, shapes):
        n = _size(shp)
        nr = -(-n // D)
        out.append(buf[..., o:o + nr, :].reshape(lead + (nr * D,))[..., :n].reshape(lead + tuple(shp)))
    return out


def _cols_from_shards(g):
    return g.transpose(1, 0, 2).reshape(g.shape[1], N_DEV * g.shape[2])


def _cols_to_shards(w):
    k, n = w.shape[0], w.shape[1] // N_DEV
    return w.reshape(k, N_DEV, n).transpose(1, 0, 2)


def _rows3(w):
    return [w[i:i + 1] for i in range(3)]


def f_mod1(xs, ps):
    return f_mod(xs, ps)[:1]


def kernel(x, c, ctx, c_ctx, ada_w, ada_b, norm_mix, norm_ffn, gla_w_in, gla_w_a2, gla_b_a, gla_head_norm, gla_w_out, sc_w_in, sc_conv_w, sc_w_out, ffn_w_up, ffn_conv_w, ffn_conv_b, ffn_w_down, final_norm, loss_target, m_c_ctx, m_ada_w, m_ada_b, m_norm_mix, m_norm_ffn, m_gla_w_in, m_gla_w_a2, m_gla_b_a, m_gla_head_norm, m_gla_w_out, m_sc_w_in, m_sc_conv_w, m_sc_w_out, m_ffn_w_up, m_ffn_conv_w, m_ffn_conv_b, m_ffn_w_down, m_final_norm, v_c_ctx, v_ada_w, v_ada_b, v_norm_mix, v_norm_ffn, v_gla_w_in, v_gla_w_a2, v_gla_b_a, v_gla_head_norm, v_gla_w_out, v_sc_w_in, v_sc_conv_w, v_sc_w_out, v_ffn_w_up, v_ffn_conv_w, v_ffn_conv_b, v_ffn_w_down, v_final_norm):
    names = ["c_ctx", "ada_w", "ada_b", "norm_mix", "norm_ffn", "gla_w_in", "gla_w_a2", "gla_b_a", "gla_head_norm",
             "gla_w_out", "sc_w_in", "sc_conv_w", "sc_w_out", "ffn_w_up", "ffn_conv_w", "ffn_conv_b", "ffn_w_down",
             "final_norm"]
    w_ = dict(zip(names, [c_ctx, ada_w, ada_b, norm_mix, norm_ffn, gla_w_in, gla_w_a2, gla_b_a, gla_head_norm, gla_w_out,
                          sc_w_in, sc_conv_w, sc_w_out, ffn_w_up, ffn_conv_w, ffn_conv_b, ffn_w_down, final_norm]))
    m_ = dict(zip(names, [m_c_ctx, m_ada_w, m_ada_b, m_norm_mix, m_norm_ffn, m_gla_w_in, m_gla_w_a2, m_gla_b_a,
                          m_gla_head_norm, m_gla_w_out, m_sc_w_in, m_sc_conv_w, m_sc_w_out, m_ffn_w_up, m_ffn_conv_w,
                          m_ffn_conv_b, m_ffn_w_down, m_final_norm]))
    v_ = dict(zip(names, [v_c_ctx, v_ada_w, v_ada_b, v_norm_mix, v_norm_ffn, v_gla_w_in, v_gla_w_a2, v_gla_b_a,
                          v_gla_head_norm, v_gla_w_out, v_sc_w_in, v_sc_conv_w, v_sc_w_out, v_ffn_w_up, v_ffn_conv_w,
                          v_ffn_conv_b, v_ffn_w_down, v_final_norm]))
    me = 4 * lax.axis_index("x") + 2 * lax.axis_index("y") + lax.axis_index("c")
    bsz = x.shape[0]
    tm = 256
    nt = SEQ // tm
    ctx_tiles = CTX // tm
    pe = functools.partial(P, per_example=True)

    small_sharded = [c, gla_w_a2, gla_b_a, sc_conv_w, ffn_conv_w]
    pack0, offs0 = _pack_rows(small_sharded, F32, 8)
    g0 = all_gather("ag_small", pack0, True).reshape(N_DEV, pack0.shape[0], D)
    c_all, wa2_s, ba_s, scw_s, fcw_s = _unpack_rows(g0, offs0, [a.shape for a in small_sharded])
    w_a2 = wa2_s[:, 0].transpose(1, 2, 0, 3).reshape(2, RANK, KD)
    b_a = ba_s[:, 0].transpose(1, 0, 2).reshape(2, KD)
    sc_cw = scw_s[:, 0].transpose(1, 0, 2).reshape(3, D)
    ffn_cw = fcw_s.transpose(1, 2, 0, 3).reshape(2, 3, 2 * FFN_H)

    cond = jnp.concatenate([c_all.reshape(N_DEV * bsz, D), c_ctx[None], jnp.zeros((ADA_ROWS - N_DEV * bsz - 1, D), F32)], 0)
    b_mine = lax.dynamic_slice(ada_b, (0, me * ADA_COLS), (2, ADA_COLS)).reshape(2, 1, ADA_COLS)
    mod_part = ada_fwd(cond, ada_w, b_mine)
    mod = all_gather("ag_mod", mod_part.reshape(2 * ADA_ROWS, ADA_COLS), True)
    mod = mod.reshape(N_DEV, 2, ADA_ROWS, ADA_COLS).transpose(1, 2, 0, 3).reshape(2, ADA_ROWS, 6 * D)
    mods = lax.dynamic_slice(mod, (0, bsz * me, 0), (2, bsz, 6 * D))
    md = [[mods[i][:, k * D:(k + 1) * D].reshape(bsz, 1, D) for k in range(6)] for i in range(2)]
    mc = [mod[0, ADA_CTX_ROW, k * D:(k + 1) * D][None] for k in range(2)]

    groups = {"gla": [("gla_w_in", 0), ("gla_w_out", 0)], "ffn0": [("ffn_w_up", 0), ("ffn_w_down", 0)],
              "l1": [("sc_w_in", 0), ("sc_w_out", 0), ("ffn_w_up", 1), ("ffn_w_down", 1)]}

    ag_started, tok = {}, 0.0
    for g, keys in groups.items():
        srcs = [w_[n][i].astype(BF16) for n, i in keys]
        *srcs, _ = lax.optimization_barrier((*srcs, mod))
        ag_started[g] = exchange_start(f"ag_{g}_start", srcs, True)
        tok = tok + ag_started[g][4][0, 0]
    norm_mix = norm_mix + tok

    def gathered(g, after):
        mine, lands = exchange_wait(f"ag_{g}_wait", ag_started[g], after, True)
        return [lax.dynamic_update_index_in_dim(ld, mn, me, 0) for ld, mn in zip(lands, mine)]

    s_up, w_down = [None, None], [None, None]
    wd = jnp.zeros((128, 2 * KD), F32).at[:RANK, :KD].set(w_a2[0]).at[RANK:2 * RANK, KD:].set(w_a2[1])
    bd = b_a.reshape(1, 2 * KD)
    scw = _rows3(sc_cw)
    head_gain = gla_head_norm.reshape(1, HV)
    gains_mix = [norm_mix[i][None] for i in range(2)]
    gains_ffn = [norm_ffn[i][None] for i in range(2)]

    def tokens(a2d, t_len):
        return a2d.reshape(bsz, t_len, -1)

    def ffn_params(i):
        rows = [ffn_cw[i][t] for t in range(3)] + [ffn_conv_b[i]]
        return [P(a.reshape(2, FFN_H), w=FFN_TC, rows=True) for a in rows]

    def ffn_fwd(i, hn2):
        u = mm(f"ffn_up{i}", V(hn2, "tok"), V(s_up[i], "cols"), out="planes", out_dtype=BF16, planes_t=SEQ)
        act = rowwise(f"ffn_mid{i}", f_ffn_mid, [X(u, w=FFN_TC, planes=True)], ffn_params(i), tm=SEQ, nt=1, nc=NCF,
                      outs=[(FFN_TC, BF16, 1)])[0]
        return u, act, tokens(mm(f"ffn_down{i}", V(act, "tok"), V(w_down[i])), SEQ)

    def res_mod_fwd(name, h, y, ps):
        return rowwise(name, f_res_mod, [X(h), X(y)], ps, tm=tm, nt=nt, outs=[(D, F32, 1), (D, BF16, 1)])

    ps_in0 = [P(gains_mix[0]), pe(md[0][0]), pe(md[0][1])]
    ps_ctx = [P(gains_mix[0]), P(mc[0]), P(mc[1])]
    hn0 = rowwise("mod_in0", f_mod, [X(x)], ps_in0, tm=tm, nt=nt, outs=[(D, BF16, 1)])[0]
    hnc = rowwise("mod_ctx", f_mod, [X(ctx)], ps_ctx, tm=tm, nt=ctx_tiles, outs=[(D, BF16, 1)])[0]
    hcat = jnp.concatenate([hnc, hn0], axis=1)
    s_gin, s_gout = gathered("gla", hcat)
    w_gin = jnp.pad(_cols_from_shards(s_gin), ((0, 0), (0, GLA_IN_PAD - GLA_IN)))
    w_gout = s_gout.reshape(VD, D)
    pcat = tokens(mm("gla_in", V(hcat, "tok"), V(w_gin)), TT)
    pa_x = X(pcat, w=128, co=(GLA_IN_PAD - 128) // 128)
    la = rowwise("gla_decay", f_decay, [pa_x], [P(wd), P(bd)], tm=tm, nt=TT // tm, outs=[(2 * KD, F32, 1)])[0]
    o2, s_all = gla_fwd(pcat, la)
    post_xs = [X(o2, w=VD, co=0, ro=ctx_tiles, split=HEADS), X(o2, w=VD, co=1, ro=ctx_tiles, split=HEADS),
               X(pcat, w=VD, co=2, ro=ctx_tiles, split=HEADS)]
    yin0 = rowwise("gla_post", f_gla_post, post_xs, [P(head_gain)], tm=tm, nt=nt, outs=[(VD, BF16, HEADS)])[0]
    y0 = tokens(mm("gla_out", V(yin0, "tok"), V(w_gout)), SEQ)
    ps_mid0 = [pe(md[0][2]), P(gains_ffn[0]), pe(md[0][3]), pe(md[0][4])]
    h1_0, hn2_0 = res_mod_fwd("res_mod_mid0", x, y0, ps_mid0)
    s_up[0], s_down0 = gathered("ffn0", hn2_0)
    w_down[0] = s_down0.reshape(FFN_H, D)
    u0, act0, fo0 = ffn_fwd(0, hn2_0)
    ps_in1 = [pe(md[0][5]), P(gains_mix[1]), pe(md[1][0]), pe(md[1][1])]
    h2_0, hn1 = res_mod_fwd("res_mod_in1", h1_0, fo0, ps_in1)

    s_sin, s_sout, s_up[1], s_down1 = gathered("l1", hn1)
    w_sout, w_down[1] = s_sout.reshape(D, D), s_down1.reshape(FFN_H, D)
    p1 = tokens(mm("sc_in", V(hn1, "tok"), V(s_sin, "cols")), SEQ)
    sc_ps = [P(a) for a in scw]
    yin1 = rowwise("sc_mid", f_sc_mid, [X(p1, split=3)], sc_ps, tm=tm, nt=nt, outs=[(D, BF16, 1)])[0]
    y1 = tokens(mm("sc_out", V(yin1, "tok"), V(w_sout)), SEQ)
    ps_mid1 = [pe(md[1][2]), P(gains_ffn[1]), pe(md[1][3]), pe(md[1][4])]
    h1_1, hn2_1 = res_mod_fwd("res_mod_mid1", h2_0, y1, ps_mid1)
    u1, act1, fo1 = ffn_fwd(1, hn2_1)
    loss8, dh1_1, dfo1, dm5_1, g_final = final_loss(h1_1, fo1, md[1][5], final_norm[None], loss_target)
    loss = lax.psum(loss8[0, 0], ("x", "y", "c"))

    def ffn_bwd(i, u, act, hn2, dfo):
        dact = tokens(mm(f"ffn_down_dx{i}", V(dfo, "tok"), V(w_down[i]), form="nt", out_dtype=BF16), SEQ)
        g_down = mm(f"ffn_down_dw{i}", V(act, "tok"), V(dfo, "tok"), form="tn", out_dtype=BF16)
        r = rowwise(f"ffn_mid_bwd{i}", f_ffn_mid, [X(u, w=FFN_TC, planes=True)], ffn_params(i), tm=SEQ, nt=1, nc=NCF,
                    douts=[X(dact, w=FFN_TC)], dx={0: BF16}, dp=[0, 1, 2, 3])
        du, g_cw, g_cb = r[0], jnp.stack([a.reshape(2 * FFN_H) for a in r[1:4]]), r[4].reshape(1, 2 * FFN_H)
        dhn2 = tokens(mm(f"ffn_up_dx{i}", V(du, "planes"), V(s_up[i], "cols"), form="nt", out_dtype=BF16), SEQ)
        g_up = mm(f"ffn_up_dw{i}", V(hn2, "tok"), V(du, "planes"), form="tn", out="cols", out_dtype=BF16)
        return dhn2, g_up, row_slots(g_down), g_cw, g_cb

    def res_mod_bwd(name, h, y, ps, dh1, dhn):
        return rowwise(name, f_res_mod, [X(h), X(y)], ps, tm=tm, nt=nt, douts=[X(dh1), X(dhn)],
                       dx={0: F32, 1: BF16}, dp=[0, 1, 2, 3])

    def row_slots(g):
        return g.reshape(N_DEV, -1, g.shape[-1])

    a2a_started = {}

    def send_grads(g, slots, after=None):
        if after is not None:
            *slots, _ = lax.optimization_barrier((*slots, after))
        a2a_started[g] = exchange_start(f"a2a_{g}_start", list(slots), False)
        return a2a_started[g][4][0, 0]

    def after_start(ps, tok):
        return [dict(ps[0], a=ps[0]["a"] + tok)] + ps[1:]

    dhn2_1, g_up1, g_down1, g_fcw1, g_fcb1 = ffn_bwd(1, u1, act1, hn2_1, dfo1)
    dh2_0, dy1, dm2_1, g_nffn1, dm3_1, dm4_1 = res_mod_bwd("res_mod_mid1_bwd", h2_0, y1, ps_mid1, dh1_1, dhn2_1)
    dyin1 = tokens(mm("sc_out_dx", V(dy1, "tok"), V(w_sout), form="nt", out_dtype=BF16), SEQ)
    g_sout = row_slots(mm("sc_out_dw", V(yin1, "tok"), V(dy1, "tok"), form="tn", out_dtype=BF16))
    r = rowwise("sc_mid_bwd", f_sc_mid, [X(p1, split=3)], sc_ps, tm=tm, nt=nt, douts=[X(dyin1)], dx={0: BF16}, dp=[0, 1, 2])
    dp1, g_scw = r[0], jnp.concatenate(r[1:4], 0)
    dhn1 = tokens(mm("sc_in_dx", V(dp1, "tok"), V(s_sin, "cols"), form="nt", out_dtype=BF16), SEQ)
    g_sin = mm("sc_in_dw", V(hn1, "tok"), V(dp1, "tok"), form="tn", out="cols", out_dtype=BF16)
    tok = send_grads("l1", [g_sin, g_sout, g_up1, g_down1])
    dh1_0, dfo0, dm5_0, g_nmix1, dm0_1, dm1_1 = res_mod_bwd("res_mod_in1_bwd", h1_0, fo0, after_start(ps_in1, tok), dh2_0, dhn1)

    dhn2_0, g_up0, g_down0, g_fcw0, g_fcb0 = ffn_bwd(0, u0, act0, hn2_0, dfo0)
    tok = send_grads("ffn0", [g_up0, g_down0])
    dx_res, dy0, dm2_0, g_nffn0, dm3_0, dm4_0 = res_mod_bwd("res_mod_mid0_bwd", x, y0, after_start(ps_mid0, tok), dh1_0, dhn2_0)
    dyin0 = tokens(mm("gla_out_dx", V(dy0, "tok"), V(w_gout), form="nt", out_dtype=BF16), SEQ)
    g_gout = row_slots(mm("gla_out_dw", V(yin0, "tok"), V(dy0, "tok"), form="tn", out_dtype=BF16))
    do, dgate, g_head = rowwise("gla_post_bwd", f_gla_post, post_xs, [P(head_gain)], tm=tm, nt=nt,
                                douts=[X(dyin0, split=HEADS)], dx={0: F32, 2: BF16}, dp=[0])
    dq2, dk2, dv2, dla = gla_bwd(pcat, la, s_all, do)
    dpa, g_wd, g_bd = rowwise("gla_decay_bwd", f_decay, [pa_x], [P(wd), P(bd)], tm=tm, nt=TT // tm, douts=[X(dla)],
                              dx={0: BF16}, dp=[0, 1])
    dpcat = gla_combine(dq2, dk2, dv2, dgate, dpa)
    dhcat = tokens(mm("gla_in_dx", V(dpcat, "tok"), V(w_gin), form="nt", out_dtype=BF16), TT)
    g_gin = _cols_to_shards(mm("gla_in_dw", V(hcat, "tok"), V(dpcat, "tok"), form="tn", out_dtype=BF16)[:, :GLA_IN])
    grad_x, g_nmix0, dm0_0, dm1_0 = rowwise("mod_in0_bwd", f_mod, [X(x)], ps_in0, tm=tm, nt=nt,
                                            douts=[X(dhcat, ro=ctx_tiles), X(dx_res)], dx={0: F32}, dp=[0, 1, 2])
    g_nmix0c, dmc0, dmc1 = rowwise("mod_ctx_bwd", f_mod1, [X(ctx)], ps_ctx, tm=tm, nt=ctx_tiles, douts=[X(dhcat)],
                                   dx={}, dp=[0, 1, 2])

    zero_row = jnp.zeros((1, 4 * D), F32)
    dmod = [jnp.concatenate([jnp.concatenate([a.reshape(bsz, D) for a in dms], 1), ctx_row], 0)
            for dms, ctx_row in (([dm0_0, dm1_0, dm2_0, dm3_0, dm4_0, dm5_0], jnp.concatenate([dmc0, dmc1, zero_row], 1)),
                                 ([dm0_1, dm1_1, dm2_1, dm3_1, dm4_1, dm5_1], jnp.zeros((1, 6 * D), F32)))]
    g_wa2 = jnp.stack([g_wd[:RANK, :KD], g_wd[RANK:2 * RANK, KD:]])
    small_grads = [jnp.stack(dmod), jnp.concatenate([g_nmix0 + g_nmix0c, g_nmix1], 0), jnp.concatenate([g_nffn0, g_nffn1], 0),
                   g_head, jnp.concatenate([g_fcb0, g_fcb1], 0), g_final, g_wa2, g_bd.reshape(2, KD), g_scw,
                   jnp.stack([g_fcw0, g_fcw1])]
    pack1, offs1 = _pack_rows(small_grads, F32, 8)
    g1 = all_gather("ag_grads", pack1, True).reshape(N_DEV, pack1.shape[0], D)
    dmod_all = _unpack_rows(g1, offs1[:1], [small_grads[0].shape])[0]
    tot = _unpack_rows(sum_slots("sum_small", g1), offs1, [a.shape for a in small_grads])
    dm_rows = dmod_all[:, :, :bsz].transpose(1, 0, 2, 3).reshape(2, N_DEV * bsz, 6 * D)
    dm_full = jnp.concatenate([dm_rows, tot[0][:, bsz:], jnp.zeros((2, ADA_ROWS - N_DEV * bsz - 1, 6 * D), F32)], 1)
    dm_mine = lax.dynamic_slice(dm_full, (0, 0, me * ADA_COLS), (2, ADA_ROWS, ADA_COLS))
    g_ada_w, g_ada_b, cpart = ada_bwd(cond, dm_mine, dm_full, ada_w)
    cparts = all_gather("ag_cctx", cpart, True).reshape(N_DEV, ADA_ROWS - ADA_CTX_ROW, D)[:, 0]
    g_cctx = cctx_grad(cparts, c_ctx[None])[0]
    tok = send_grads("gla", [g_gin, g_gout], after=g_cctx)

    def my_cols(full, n):
        return lax.dynamic_slice_in_dim(full, me * n, n, axis=full.ndim - 1)

    grads = {
        "c_ctx": g_cctx, "ada_b": g_ada_b.reshape(2, 6 * D), "norm_mix": tot[1], "norm_ffn": tot[2],
        "gla_head_norm": tot[3], "ffn_conv_b": tot[4], "final_norm": tot[5].reshape(D),
        "gla_w_a2": my_cols(tot[6], KD // N_DEV)[None], "gla_b_a": my_cols(tot[7], KD // N_DEV)[None],
        "sc_conv_w": my_cols(tot[8], D // N_DEV)[None], "ffn_conv_w": my_cols(tot[9], 2 * FFN_H // N_DEV),
    }

    res_ada = adamw("adamw_ada", *[a.reshape(2 * D, ADA_COLS) for a in (ada_w, g_ada_w, m_ada_w, v_ada_w)])
    grads["c_ctx"] = g_cctx + tok
    big = ["gla_w_in", "gla_w_out", "sc_w_in", "sc_w_out", "ffn_w_up", "ffn_w_down"]
    small = [n for n in names if n not in big and n != "ada_w"]
    g_small = _pack_rows([grads[n] for n in small], F32, 8)[0]
    res_small = adamw("adamw_small", _pack_rows([w_[n] for n in small], F32, 8)[0], g_small,
                      _pack_rows([m_[n] for n in small], F32, 8)[0], _pack_rows([v_[n] for n in small], F32, 8)[0])
    offs_s = _pack_rows([w_[n] for n in small], F32, 8)[1]

    big_res, after = {}, res_small[0]
    for g in ("l1", "ffn0", "gla"):
        sent, lands = exchange_wait(f"a2a_{g}_wait", a2a_started[g], after, False)
        for (n, i), mine, land in zip(groups[g], sent, lands):
            land = lax.dynamic_update_index_in_dim(land, lax.dynamic_index_in_dim(mine, me, 0, keepdims=False), me, 0)
            big_res[(n, i)] = adamw(f"adamw_{n}{i}", w_[n], land, m_[n], v_[n], layer=i)
            after = big_res[(n, i)][0]

    out = {}
    for kind, idx in (("grad", 0), ("delta", 1), ("new_m", 2), ("new_v", 3)):
        vals = {n: jnp.stack([big_res[(n, i)][idx] for i in range(w_[n].shape[0])]) for n in big}
        vals["ada_w"] = res_ada[idx].reshape(ada_w.shape)
        vals.update(zip(small, _unpack_rows(res_small[idx], offs_s, [w_[n].shape for n in small])))
        out[kind] = [vals[n] for n in names]
    return (loss, grad_x, *out["grad"], *out["delta"], *out["new_m"], *out["new_v"])
```

```python
import functools

import jax
import jax.numpy as jnp
from jax import lax
from jax.experimental import pallas as pl
from jax.experimental.pallas import tpu as pltpu

F32 = jnp.float32
BF16 = jnp.bfloat16

N_DEV = 8
D = 1024
SEQ = 2048
CTX = 256
TT = CTX + SEQ
GRID_W = 64
CHUNK = 64
HEADS = 4
HK = 128
HV = 256
KD = 512
VD = 1024
RANK = 16
TAU = 16.0
GLA_IN = 3104
GLA_IN_PAD = 3200
FFN_H = 2560
FFN_TC = 256
EPS = 1e-6
LR, B1, B2, AEPS, WD, STEP = 0.001, 0.9, 0.999, 1e-08, 0.01, 10
MESH = pl.DeviceIdType.MESH


def _blocks(n):
    return [n] + [t for t in range(n - n % 128, 0, -128) if n % t == 0 and t != n]


def V(arr, kind="flat"):
    if kind == "tok":
        return V(arr.reshape(-1, arr.shape[-1]))
    if kind == "flat":
        r, c = arr.shape
        return dict(a=arr, kind=kind, shape=(r, c), rows=_blocks(r), cols=_blocks(c))
    if kind == "planes":
        bsz, _, t, ch = arr.shape
        return dict(a=arr, kind=kind, shape=(bsz * t, 2 * ch), rows=_blocks(t), cols=[2 * ch] + _blocks(ch), t=t, ch=ch)
    _, r, n = arr.shape
    return dict(a=arr, kind=kind, shape=(r, N_DEV * n), rows=_blocks(r), cols=[8 * n, 4 * n, 2 * n], n=n)


def _view_spec(v, br, bc, idx):
    if v["kind"] == "flat":
        return pl.BlockSpec((br, bc), idx)
    if v["kind"] == "planes":
        nt = v["t"] // br
        if bc == 2 * v["ch"]:
            return pl.BlockSpec((None, 2, br, v["ch"]), lambda i, j, k: (idx(i, j, k)[0] // nt, 0, idx(i, j, k)[0] % nt, 0))
        nch = v["ch"] // bc

        def at(i, j, k):
            r, c = idx(i, j, k)
            return r // nt, c // nch, r % nt, c % nch
        return pl.BlockSpec((None, None, br, bc), at)
    return pl.BlockSpec((bc // v["n"], br, v["n"]), lambda i, j, k: (idx(i, j, k)[1], idx(i, j, k)[0], 0))


def _out_view(kind, rows, cols, dtype, planes_t=None):
    if kind == "flat":
        shape = (rows, cols)
    elif kind == "planes":
        shape = (rows // planes_t, 2, planes_t, cols // 2)
    else:
        shape = (N_DEV, rows, cols // N_DEV)
    return V(jax.ShapeDtypeStruct(shape, dtype), kind)


MM_VMEM_BUDGET = 40 * 2 ** 20
MM_VMEM_LIMIT = 56 * 2 ** 20
MM_MAX_TILE = 1536


def _mm_tiles(m, n, kk, ms, ns, ks, a_bytes, b_bytes, o_bytes):
    best = None
    for tk in ks:
        for tm in [t for t in ms if t <= MM_MAX_TILE]:
            for tn in [t for t in ns if t <= MM_MAX_TILE]:
                one_k = tk == kk
                need = 2 * (tm * tk * a_bytes + tk * tn * b_bytes + tm * tn * o_bytes) + (0 if one_k else tm * tn * 4)
                if need > MM_VMEM_BUDGET:
                    continue
                steps = (m // tm) * (n // tn) * (kk // tk)
                traffic = (m * kk * a_bytes * (1 if one_k else n // tn)
                           + kk * n * b_bytes * (1 if one_k and n == tn else m // tm) + m * n * o_bytes)
                fill = (tm * tk * a_bytes + tk * tn * b_bytes) / 2.5e12
                cost = max(2.0 * m * n * kk / (9e14 if one_k else 6.5e14), traffic / 2.5e12) + steps * 0.4e-6 + fill
                if best is None or cost < best[0]:
                    best = (cost, tm, tn, tk)
    return best[1:]


def mm(name, a, b, form="nn", out="flat", out_dtype=F32, planes_t=None):
    (m, kk) = a["shape"][::-1] if form == "tn" else a["shape"]
    n = b["shape"][0] if form == "nt" else b["shape"][1]
    assert (b["shape"][1] if form == "nt" else b["shape"][0]) == kk, (name, a["shape"], b["shape"])
    o = _out_view(out, m, n, out_dtype, planes_t)
    a_m, a_k = (a["cols"], a["rows"]) if form == "tn" else (a["rows"], a["cols"])
    b_k, b_n = (b["cols"], b["rows"]) if form == "nt" else (b["rows"], b["cols"])
    tm, tn, tk = _mm_tiles(m, n, kk, [t for t in a_m if t in o["rows"]], [t for t in b_n if t in o["cols"]],
                           [t for t in a_k if t in b_k], a["a"].dtype.itemsize, b["a"].dtype.itemsize,
                           jnp.dtype(out_dtype).itemsize)
    nk = kk // tk
    dn = (((0 if form == "tn" else 1,), (1 if form == "nt" else 0,)), ((), ()))

    def load(ref):
        if len(ref.shape) == 3:
            return jnp.concatenate([ref[p] for p in range(ref.shape[0])], axis=-1).astype(BF16)
        return ref[...].astype(BF16)

    def store(o_ref, val):
        val = val.astype(out_dtype)
        if len(o_ref.shape) == 3:
            w = o_ref.shape[-1]
            for p in range(o_ref.shape[0]):
                o_ref[p] = val[:, p * w:(p + 1) * w]
        else:
            o_ref[...] = val

    def body(a_ref, b_ref, o_ref, *acc):
        if nk == 1:
            store(o_ref, lax.dot_general(load(a_ref), load(b_ref), dn, preferred_element_type=F32))
            return
        k, acc_ref = pl.program_id(2), acc[0]

        @pl.when(k == 0)
        def _():
            acc_ref[...] = jnp.zeros_like(acc_ref)

        acc_ref[...] += lax.dot_general(load(a_ref), load(b_ref), dn, preferred_element_type=F32)

        @pl.when(k == nk - 1)
        def _():
            store(o_ref, acc_ref[...])

    if form == "tn":
        a_spec = _view_spec(a, tk, tm, lambda i, j, k: (k, i))
    else:
        a_spec = _view_spec(a, tm, tk, lambda i, j, k: (i, k))
    if form == "nt":
        b_spec = _view_spec(b, tn, tk, lambda i, j, k: (j, k))
    else:
        b_spec = _view_spec(b, tk, tn, lambda i, j, k: (k, j))
    return pl.pallas_call(
        body, name=name, grid=(m // tm, n // tn, nk),
        in_specs=[a_spec, b_spec], out_specs=_view_spec(o, tm, tn, lambda i, j, k: (i, j)), out_shape=o["a"],
        scratch_shapes=[pltpu.VMEM((tm, tn), F32)] if nk > 1 else [],
        compiler_params=pltpu.CompilerParams(dimension_semantics=("parallel", "parallel", "arbitrary"),
                                             vmem_limit_bytes=MM_VMEM_LIMIT),
    )(a["a"], b["a"])


def X(arr, w=None, co=0, ro=0, split=1, planes=False):
    return dict(a=arr, w=arr.shape[-1] if w is None else w, co=co, ro=ro, split=2 if planes else split,
                mode="planes" if planes else "cols")


def P(arr, per_example=False, w=None, split=1, rows=False):
    return dict(a=arr, e=per_example, w=arr.shape[-1] if w is None else w, split=arr.shape[-2] if rows else split,
                mode="rows" if rows else "cols")


def _pieces(ref, s):
    if s["mode"] == "planes":
        return [ref[0], ref[1]]
    if s["mode"] == "rows":
        return [ref[i:i + 1, :] for i in range(s["split"])]
    w = ref.shape[-1] // s["split"]
    return [ref[:, i * w:(i + 1) * w] for i in range(s["split"])]


def _store(ref, pieces, s, accumulate=False):
    w = ref.shape[-1] // len(pieces)
    for i, p in enumerate(pieces):
        at = (i,) if s["mode"] == "planes" else (slice(i, i + 1),) if s["mode"] == "rows" else (slice(None), slice(i * w, (i + 1) * w))
        if accumulate:
            ref[at] += p.astype(ref.dtype)
        else:
            ref[at] = p.astype(ref.dtype)


def rowwise(name, f, xs, ps, *, tm, nt, nc=1, outs=None, douts=None, dx=None, dp=None):
    bsz = xs[0]["a"].shape[0]
    fwd = douts is None
    nx, np_ = len(xs), len(ps)
    douts = [] if fwd else douts
    dx = {} if fwd else dx
    dp = [] if fwd else dp

    def x_spec(s):
        if s["mode"] == "planes":
            return pl.BlockSpec((None, 2, tm, s["w"]), lambda c, b, t, s=s: (b, 0, t + s["ro"], c + s["co"]))
        return pl.BlockSpec((None, tm, s["w"]), lambda c, b, t, s=s: (b, t + s["ro"], c + s["co"]))

    def x_out(s, dt):
        if s["mode"] == "planes":
            return (jax.ShapeDtypeStruct((bsz, 2, nt * tm, nc * s["w"]), dt),
                    pl.BlockSpec((None, 2, tm, s["w"]), lambda c, b, t: (b, 0, t, c)))
        return (jax.ShapeDtypeStruct((bsz, nt * tm, nc * s["w"]), dt), pl.BlockSpec((None, tm, s["w"]), lambda c, b, t: (b, t, c)))

    def p_spec(s):
        r = s["a"].shape[-2]
        if s["e"]:
            return pl.BlockSpec((None, r, s["w"]), lambda c, b, t: (b, 0, c))
        return pl.BlockSpec((r, s["w"]), lambda c, b, t: (0, c))

    in_specs = [x_spec(s) for s in xs] + [p_spec(s) for s in ps] + [x_spec(s) for s in douts]
    operands = [s["a"] for s in xs] + [s["a"] for s in ps] + [s["a"] for s in douts]
    if fwd:
        out_modes = [dict(mode="cols", split=sp) for (_, _, sp) in outs]
        out_shape = [jax.ShapeDtypeStruct((bsz, nt * tm, nc * w), dt) for (w, dt, _) in outs]
        out_specs = [pl.BlockSpec((None, tm, w), lambda c, b, t: (b, t, c)) for (w, _, _) in outs]
    else:
        dx_outs = [x_out(xs[i], dt) for i, dt in dx.items()]
        out_shape, out_specs = [o[0] for o in dx_outs], [o[1] for o in dx_outs]
        for j in dp:
            s = ps[j]
            r = s["a"].shape[-2]
            if s["e"]:
                out_shape.append(jax.ShapeDtypeStruct((bsz, r, nc * s["w"]), F32))
                out_specs.append(pl.BlockSpec((None, r, s["w"]), lambda c, b, t: (b, 0, c)))
            else:
                out_shape.append(jax.ShapeDtypeStruct((r, nc * s["w"]), F32))
                out_specs.append(pl.BlockSpec((r, s["w"]), lambda c, b, t: (0, c)))

    def body(*refs):
        x_refs, p_refs = refs[:nx], refs[nx:nx + np_]
        d_refs = refs[nx + np_:nx + np_ + len(douts)]
        o_refs = refs[nx + np_ + len(douts):]
        xv = [[p.astype(F32) for p in _pieces(r, s)] for r, s in zip(x_refs, xs)]
        pv = [[p.astype(F32) for p in _pieces(r, s)] for r, s in zip(p_refs, ps)]
        if fwd:
            for r, pieces, s in zip(o_refs, f(xv, pv), out_modes):
                _store(r, pieces, s)
            return
        _, vjp = jax.vjp(f, xv, pv)
        cot = [[p.astype(F32) for p in _pieces(r, s)] for r, s in zip(d_refs, douts)]
        dxv, dpv = vjp(cot)
        for r, i in zip(o_refs, dx):
            _store(r, dxv[i], xs[i])
        b, t = pl.program_id(1), pl.program_id(2)
        for r, j in zip(o_refs[len(dx):], dp):
            first = (t == 0) if ps[j]["e"] else jnp.logical_and(b == 0, t == 0)

            @pl.when(first)
            def _(r=r, j=j):
                _store(r, dpv[j], ps[j])

            @pl.when(jnp.logical_not(first))
            def _(r=r, j=j):
                _store(r, dpv[j], ps[j], accumulate=True)

    res = pl.pallas_call(
        body, name=name, grid=(nc, bsz, nt), in_specs=in_specs, out_specs=out_specs, out_shape=out_shape,
        compiler_params=pltpu.CompilerParams(dimension_semantics=("arbitrary", "arbitrary", "arbitrary")),
    )(*operands)
    return res


def _keep_rows(a, shift, keep):
    n = a.shape[0]
    t = lax.broadcasted_iota(jnp.int32, a.shape, 0)
    return jnp.where(keep(t, n), pltpu.roll(a, shift % n, 0), 0.0)


def _shift_pair(step, keep_prev, keep_next):
    @jax.custom_vjp
    def prev(a):
        return _keep_rows(a, step, keep_prev)

    @jax.custom_vjp
    def nxt(a):
        return _keep_rows(a, -step, keep_next)

    prev.defvjp(lambda a: (prev(a), None), lambda _, g: (nxt(g),))
    nxt.defvjp(lambda a: (nxt(a), None), lambda _, g: (prev(g),))
    return prev, nxt


prev_tok, next_tok = _shift_pair(1, lambda t, n: t % GRID_W != 0, lambda t, n: t % GRID_W != GRID_W - 1)
prev_row, next_row = _shift_pair(GRID_W, lambda t, n: t >= GRID_W, lambda t, n: t < n - GRID_W)


@jax.custom_vjp
def bdot(a, w):
    return jnp.dot(a.astype(BF16), w.astype(BF16), preferred_element_type=F32)


def _bdot_bwd(res, g):
    a, w = res
    gb = g.astype(BF16)
    da = lax.dot_general(gb, w.astype(BF16), (((1,), (1,)), ((), ())), preferred_element_type=F32)
    dw = lax.dot_general(a.astype(BF16), gb, (((0,), (0,)), ((), ())), preferred_element_type=F32)
    return da, dw


bdot.defvjp(lambda a, w: (bdot(a, w), (a, w)), _bdot_bwd)


@jax.custom_vjp
def log_sigmoid(z):
    return jnp.minimum(z, 0.0) - jnp.log(1.0 + jnp.exp(-jnp.abs(z)))


def _lsig_bwd(z, g):
    e = jnp.exp(-jnp.abs(z))
    return (g * jnp.where(z >= 0, e, 1.0) / (1.0 + e),)


log_sigmoid.defvjp(lambda z: (log_sigmoid(z), z), _lsig_bwd)


def silu(x):
    return x * jax.nn.sigmoid(x)


def _rms(x):
    return x * lax.rsqrt(jnp.mean(x * x, axis=-1, keepdims=True) + EPS)


def _mod(x, gain, shift, scale):
    return _rms(x) * gain * (1.0 + scale) + shift


def f_mod(xs, ps):
    ((h,),), ((gain,), (shift,), (scale,)) = xs, ps
    return [[_mod(h, gain, shift, scale)], [h]]


def f_res_mod(xs, ps):
    ((h,), (y,)), ((gate,), (gain,), (shift,), (scale,)) = xs, ps
    h1 = h + gate * y
    return [[h1], [_mod(h1, gain, shift, scale)]]


def f_ffn_mid(xs, ps):
    ((ua, ug),), ((w0a, w0g), (w1a, w1g), (w2a, w2g), (ba, bg)) = xs, ps
    a = w0a * prev_row(ua) + w1a * ua + w2a * next_row(ua) + ba
    g = w0g * prev_row(ug) + w1g * ug + w2g * next_row(ug) + bg
    return [[a * silu(g)]]


def f_sc_mid(xs, ps):
    ((bg, cg, v),), ((w0,), (w1,), (w2,)) = xs, ps
    z = cg * v
    return [[bg * (w0 * prev_tok(z) + w1 * z + w2 * next_tok(z))]]


def f_decay(xs, ps):
    ((a,),), ((wd,), (bd,)) = xs, ps
    return [[log_sigmoid(bdot(a, wd) + bd) / TAU]]


def f_gla_post(xs, ps):
    (of, ob, g), ((gain,),) = xs, ps
    return [[_rms(a + b) * gain * silu(c) for a, b, c in zip(of, ob, g)]]


NCH = TT // CHUNK
CTX_CH = CTX // CHUNK
_NT = (((1,), (1,)), ((), ()))
_TN = (((0,), (0,)), ((), ()))
_NN = (((1,), (0,)), ((), ()))


def _chunk_of(d, j):
    return jnp.where(d == 0, j, jnp.where(j < CTX_CH, CTX_CH - 1 - j, NCH + CTX_CH - 1 - j))


def _dot(a, b, dn):
    return lax.dot_general(a, b, dn, preferred_element_type=F32)


def _mask_dot(m, g):
    g0 = g.astype(BF16)
    r1 = g - g0.astype(F32)
    g1 = r1.astype(BF16)
    g2 = (r1 - g1.astype(F32)).astype(BF16)
    return _dot(m, g0, _NN) + _dot(m, g1, _NN) + _dot(m, g2, _NN)


def _causal(d):
    row = lax.broadcasted_iota(jnp.int32, (CHUNK, CHUNK), 0)
    col = lax.broadcasted_iota(jnp.int32, (CHUNK, CHUNK), 1)
    delta = jnp.where(d == 0, col - row, row - col)
    return delta <= 0, delta >= 0


def _gla_in_specs(bsz, rev):
    def blk(d, j):
        return _chunk_of(d, (NCH - 1 - j) if rev else j)

    return [
        pl.BlockSpec((bsz, CHUNK, KD), lambda d, j: (0, blk(d, j), 0)),
        pl.BlockSpec((bsz, CHUNK, KD), lambda d, j: (0, blk(d, j), 1)),
        pl.BlockSpec((bsz, CHUNK, VD), lambda d, j: (0, blk(d, j), 1)),
        pl.BlockSpec((bsz, CHUNK, KD), lambda d, j: (0, blk(d, j), d)),
    ], blk


def gla_fwd(pcat, la):
    bsz = pcat.shape[0]
    in_specs, blk = _gla_in_specs(bsz, False)

    def body(q_ref, k_ref, v_ref, la_ref, o_ref, s_ref, st):
        d, j = pl.program_id(0), pl.program_id(1)

        @pl.when(j == 0)
        def _():
            st[...] = jnp.zeros_like(st)

        s_ref[...] = st[...]
        causal, _ = _causal(d)
        mf = causal.astype(BF16)
        for e, h in [(e, h) for e in range(bsz) for h in range(HEADS)]:
            ks_, vs_ = slice(h * HK, (h + 1) * HK), slice(h * HV, (h + 1) * HV)
            q, k, v, g = q_ref[e, :, ks_] * (HK ** -0.5), k_ref[e, :, ks_], v_ref[e, :, vs_].astype(BF16), la_ref[e, :, ks_]
            b = _mask_dot(mf, g)
            bl = jnp.sum(g, axis=0, keepdims=True)
            qs = (q * jnp.exp(b)).astype(BF16)
            ks = (k * jnp.exp(-b)).astype(BF16)
            kd = (k * jnp.exp(bl - b)).astype(BF16)
            s = st[e, h]
            att = jnp.where(causal, _dot(qs, ks, _NT), 0.0).astype(BF16)
            o_ref[e, :, vs_] = _dot(qs, s.astype(BF16), _NT) + _dot(att, v, _NN)
            st[e, h] = jnp.exp(bl) * s + _dot(v, kd, _TN)

    return pl.pallas_call(
        body, name="gla_fwd", grid=(2, NCH), in_specs=in_specs,
        out_specs=[pl.BlockSpec((bsz, CHUNK, VD), lambda d, j: (0, blk(d, j), d)),
                   pl.BlockSpec((bsz, None, None, HEADS, HV, HK), lambda d, j: (0, d, j, 0, 0, 0))],
        out_shape=[jax.ShapeDtypeStruct((bsz, TT, 2 * VD), F32), jax.ShapeDtypeStruct((bsz, 2, NCH, HEADS, HV, HK), F32)],
        scratch_shapes=[pltpu.VMEM((bsz, HEADS, HV, HK), F32)],
        compiler_params=pltpu.CompilerParams(dimension_semantics=("arbitrary", "arbitrary")),
    )(pcat, pcat, pcat, la)


def gla_bwd(pcat, la, s_all, do):
    bsz = pcat.shape[0]
    in_specs, blk = _gla_in_specs(bsz, True)
    in_specs += [
        pl.BlockSpec((bsz, None, None, HEADS, HV, HK), lambda d, j: (0, d, NCH - 1 - j, 0, 0, 0)),
        pl.BlockSpec((bsz, CHUNK, VD), lambda d, j: (0, jnp.maximum(blk(d, j) - CTX_CH, 0), 0)),
    ]

    def body(q_ref, k_ref, v_ref, la_ref, s_ref, do_ref, dq_ref, dk_ref, dv_ref, dla_ref, dst):
        d, j = pl.program_id(0), pl.program_id(1)

        @pl.when(j == 0)
        def _():
            dst[...] = jnp.zeros_like(dst)

        latent = blk(d, j) >= CTX_CH
        causal, causal_t = _causal(d)
        mt = causal_t.astype(BF16)
        mf = causal.astype(BF16)
        scale = HK ** -0.5
        for e, h in [(e, h) for e in range(bsz) for h in range(HEADS)]:
            ks_, vs_ = slice(h * HK, (h + 1) * HK), slice(h * HV, (h + 1) * HV)
            q, k, v, g = q_ref[e, :, ks_] * scale, k_ref[e, :, ks_], v_ref[e, :, vs_].astype(BF16), la_ref[e, :, ks_]
            b = _mask_dot(mf, g)
            bl = jnp.sum(g, axis=0, keepdims=True)
            ex, ei, ed, el = jnp.exp(b), jnp.exp(-b), jnp.exp(bl - b), jnp.exp(bl)
            qs, ks, kd = q * ex, k * ei, k * ed
            qsb, ksb, kdb = qs.astype(BF16), ks.astype(BF16), kd.astype(BF16)
            s, ds1 = s_ref[e, h], dst[e, h]
            sb, ds1b = s.astype(BF16), ds1.astype(BF16)
            dob = jnp.where(latent, do_ref[e, :, vs_], 0.0).astype(BF16)
            att = jnp.where(causal, _dot(qsb, ksb, _NT), 0.0).astype(BF16)
            datt = jnp.where(causal, _dot(dob, v, _NT), 0.0).astype(BF16)
            dqs = _dot(dob, sb, _NN) + _dot(datt, ksb, _NN)
            dks = _dot(datt, qsb, _TN)
            dv_ref[e, :, vs_] = _dot(att, dob, _TN) + _dot(kdb, ds1b, _NT)
            dkd = _dot(v, ds1b, _NN)
            dst[e, h] = _dot(dob, qsb, _TN) + el * ds1
            del_ = jnp.sum(s * ds1, axis=0, keepdims=True)
            dq_ref[e, :, ks_] = dqs * ex * scale
            dk_ref[e, :, ks_] = dks * ei + dkd * ed
            db = dqs * qs - dks * ks - dkd * kd
            dbl = jnp.sum(dkd * kd, axis=0, keepdims=True) + del_ * el
            dla_ref[e, :, ks_] = _mask_dot(mt, db) + dbl

    return pl.pallas_call(
        body, name="gla_bwd", grid=(2, NCH), in_specs=in_specs,
        out_specs=[pl.BlockSpec((None, bsz, CHUNK, KD), lambda d, j: (d, 0, blk(d, j), 0)),
                   pl.BlockSpec((None, bsz, CHUNK, KD), lambda d, j: (d, 0, blk(d, j), 0)),
                   pl.BlockSpec((None, bsz, CHUNK, VD), lambda d, j: (d, 0, blk(d, j), 0)),
                   pl.BlockSpec((bsz, CHUNK, KD), lambda d, j: (0, blk(d, j), d))],
        out_shape=[jax.ShapeDtypeStruct((2, bsz, TT, KD), F32), jax.ShapeDtypeStruct((2, bsz, TT, KD), F32),
                   jax.ShapeDtypeStruct((2, bsz, TT, VD), F32), jax.ShapeDtypeStruct((bsz, TT, 2 * KD), F32)],
        scratch_shapes=[pltpu.VMEM((bsz, HEADS, HV, HK), F32)],
        compiler_params=pltpu.CompilerParams(dimension_semantics=("arbitrary", "arbitrary")),
    )(pcat, pcat, pcat, la, s_all, do)


def gla_combine(dq2, dk2, dv2, dgate, dpa):
    bsz = dgate.shape[0]
    tm = CTX

    def body(dq_ref, dk_ref, dv_ref, dg_ref, dpa_ref, o_ref):
        t = pl.program_id(1)
        o_ref[:, 0:KD] = (dq_ref[0] + dq_ref[1]).astype(BF16)
        o_ref[:, KD:2 * KD] = (dk_ref[0] + dk_ref[1]).astype(BF16)
        o_ref[:, 2 * KD:2 * KD + VD] = (dv_ref[0] + dv_ref[1]).astype(BF16)
        o_ref[:, 2 * KD + VD:2 * KD + 2 * VD] = jnp.where(t > 0, dg_ref[...], 0).astype(BF16)
        o_ref[:, 2 * KD + 2 * VD:] = dpa_ref[...].astype(BF16)

    return pl.pallas_call(
        body, name="gla_combine", grid=(bsz, TT // tm),
        in_specs=[pl.BlockSpec((2, None, tm, KD), lambda b, t: (0, b, t, 0)),
                  pl.BlockSpec((2, None, tm, KD), lambda b, t: (0, b, t, 0)),
                  pl.BlockSpec((2, None, tm, VD), lambda b, t: (0, b, t, 0)),
                  pl.BlockSpec((None, tm, VD), lambda b, t: (b, jnp.maximum(t - 1, 0), 0)),
                  pl.BlockSpec((None, tm, 128), lambda b, t: (b, t, 0))],
        out_specs=pl.BlockSpec((None, tm, GLA_IN_PAD), lambda b, t: (b, t, 0)),
        out_shape=jax.ShapeDtypeStruct((bsz, TT, GLA_IN_PAD), BF16),
        compiler_params=pltpu.CompilerParams(dimension_semantics=("arbitrary", "arbitrary")),
    )(dq2, dk2, dv2, dgate, dpa)


def final_loss(h1, fo, gate, gain, tgt):
    bsz, t_len, _ = h1.shape
    tm = 256

    def body(h_ref, f_ref, gate_ref, gain_ref, tgt_ref, loss_ref, dh_ref, df_ref, dgate_ref, dgain_ref):
        b, t = pl.program_id(0), pl.program_id(1)
        target = tgt_ref[...]

        def core(h, fo_, gate_, gain_):
            e = _rms(h + gate_ * fo_) * gain_ - target
            return jnp.sum(0.5 * jnp.sum(e * e, axis=-1, keepdims=True) / D, axis=0, keepdims=True)

        loss, vjp = jax.vjp(core, h_ref[...], f_ref[...], gate_ref[...], gain_ref[...])
        dh, df, dgate, dgain = vjp(jnp.ones((1, 1), F32))
        dh_ref[...] = dh
        df_ref[...] = df.astype(BF16)
        first = jnp.logical_and(b == 0, t == 0)

        @pl.when(first)
        def _():
            loss_ref[...] = jnp.broadcast_to(loss, loss_ref.shape)
            dgain_ref[...] = dgain

        @pl.when(jnp.logical_not(first))
        def _():
            loss_ref[...] += jnp.broadcast_to(loss, loss_ref.shape)
            dgain_ref[...] += dgain

        @pl.when(t == 0)
        def _():
            dgate_ref[...] = dgate

        @pl.when(t > 0)
        def _():
            dgate_ref[...] += dgate

    tile = pl.BlockSpec((None, tm, D), lambda b, t: (b, t, 0))
    per_ex = pl.BlockSpec((None, 1, D), lambda b, t: (b, 0, 0))
    shared = pl.BlockSpec((1, D), lambda b, t: (0, 0))
    return pl.pallas_call(
        body, name="final_loss", grid=(bsz, t_len // tm),
        in_specs=[tile, tile, per_ex, shared, tile],
        out_specs=[pl.BlockSpec((8, 128), lambda b, t: (0, 0)), tile, tile, per_ex, shared],
        out_shape=[jax.ShapeDtypeStruct((8, 128), F32), jax.ShapeDtypeStruct(h1.shape, F32),
                   jax.ShapeDtypeStruct(h1.shape, BF16), jax.ShapeDtypeStruct((bsz, 1, D), F32),
                   jax.ShapeDtypeStruct((1, D), F32)],
        compiler_params=pltpu.CompilerParams(dimension_semantics=("arbitrary", "arbitrary")),
    )(h1, fo, gate, gain, tgt)


ADA_ROWS = 24
ADA_CTX_ROW = 16
ADA_COLS = 6 * D // N_DEV


def ada_fwd(cond, w, b):
    def body(c_ref, w_ref, b_ref, o_ref):
        s = silu(c_ref[...]).astype(BF16)
        o_ref[...] = jnp.dot(s, w_ref[...].astype(BF16), preferred_element_type=F32) + b_ref[...]

    return pl.pallas_call(
        body, name="ada_fwd", grid=(2,),
        in_specs=[pl.BlockSpec((ADA_ROWS, D), lambda i: (0, 0)), pl.BlockSpec((None, D, ADA_COLS), lambda i: (i, 0, 0)),
                  pl.BlockSpec((None, 1, ADA_COLS), lambda i: (i, 0, 0))],
        out_specs=pl.BlockSpec((None, ADA_ROWS, ADA_COLS), lambda i: (i, 0, 0)),
        out_shape=jax.ShapeDtypeStruct((2, ADA_ROWS, ADA_COLS), F32),
    )(cond, w, b)


def ada_bwd(cond, dm_mine, dm_full, w):
    def body(c_ref, dm_ref, dmf_ref, w_ref, gw_ref, gb_ref, cp_ref):
        i = pl.program_id(0)
        s = silu(c_ref[...]).astype(BF16)
        dm = dm_ref[...].astype(BF16)
        gw_ref[...] = _dot(s, dm, _TN)
        gb_ref[...] = jnp.sum(dmf_ref[...], axis=0, keepdims=True)

        @pl.when(i == 0)
        def _():
            cp_ref[...] = _dot(dm_ref[ADA_CTX_ROW:, :].astype(BF16), w_ref[...].astype(BF16), _NT)

    return pl.pallas_call(
        body, name="ada_bwd", grid=(2,),
        in_specs=[pl.BlockSpec((ADA_ROWS, D), lambda i: (0, 0)), pl.BlockSpec((None, ADA_ROWS, ADA_COLS), lambda i: (i, 0, 0)),
                  pl.BlockSpec((None, ADA_ROWS, 6 * D), lambda i: (i, 0, 0)), pl.BlockSpec((None, D, ADA_COLS), lambda i: (i, 0, 0))],
        out_specs=[pl.BlockSpec((None, D, ADA_COLS), lambda i: (i, 0, 0)), pl.BlockSpec((None, 1, 6 * D), lambda i: (i, 0, 0)),
                   pl.BlockSpec((ADA_ROWS - ADA_CTX_ROW, D), lambda i: (0, 0))],
        out_shape=[jax.ShapeDtypeStruct((2, D, ADA_COLS), F32), jax.ShapeDtypeStruct((2, 1, 6 * D), F32),
                   jax.ShapeDtypeStruct((ADA_ROWS - ADA_CTX_ROW, D), F32)],
        compiler_params=pltpu.CompilerParams(dimension_semantics=("arbitrary",)),
    )(cond, dm_mine, dm_full, w)


def cctx_grad(parts, c_ctx):
    def body(p_ref, c_ref, o_ref):
        tot = p_ref[0:1, :]
        for i in range(1, N_DEV):
            tot = tot + p_ref[i:i + 1, :]
        c = c_ref[...]
        sg = jax.nn.sigmoid(c)
        o_ref[...] = tot * sg * (1.0 + c * (1.0 - sg))

    return pl.pallas_call(body, name="cctx_grad", out_shape=jax.ShapeDtypeStruct((1, D), F32))(parts, c_ctx)


def _row_tile(r):
    for t in (512, 256, 128, 80, 64, 40, 32, 16, 8):
        if r % t == 0:
            return t
    return r


def _slot_sum(ref):
    tot = ref[0].astype(F32)
    for i in range(1, ref.shape[0]):
        tot = tot + ref[i].astype(F32)
    return tot


def sum_slots(name, x):
    s, r, c = x.shape
    tr = _row_tile(r)

    def body(x_ref, o_ref):
        o_ref[...] = _slot_sum(x_ref)

    return pl.pallas_call(
        body, name=name, grid=(r // tr,), in_specs=[pl.BlockSpec((s, tr, c), lambda i: (0, i, 0))],
        out_specs=pl.BlockSpec((tr, c), lambda i: (i, 0)), out_shape=jax.ShapeDtypeStruct((r, c), F32),
    )(x)


def adamw(name, w, g, m, v, layer=None):
    r, c = w.shape[-2:]
    tr = _row_tile(r)
    stacked = g.ndim == 3

    def body(w_ref, g_ref, m_ref, v_ref, go_ref, d_ref, mo_ref, vo_ref):
        gv = _slot_sum(g_ref) if stacked else g_ref[...]
        mn = B1 * m_ref[...] + (1.0 - B1) * gv
        vn = B2 * v_ref[...] + (1.0 - B2) * jnp.square(gv)
        m_hat = mn / (1.0 - B1 ** STEP)
        v_hat = vn / (1.0 - B2 ** STEP)
        go_ref[...] = gv
        d_ref[...] = -LR * (m_hat / (jnp.sqrt(v_hat) + AEPS) + WD * w_ref[...])
        mo_ref[...] = mn
        vo_ref[...] = vn

    tile = pl.BlockSpec((tr, c), lambda i: (i, 0))
    slab = tile if layer is None else pl.BlockSpec((None, tr, c), lambda i: (layer, i, 0))
    g_spec = pl.BlockSpec((g.shape[0], tr, c), lambda i: (0, i, 0)) if stacked else tile
    return pl.pallas_call(
        body, name=name, grid=(r // tr,), in_specs=[slab, g_spec, slab, slab], out_specs=[tile] * 4,
        out_shape=[jax.ShapeDtypeStruct((r, c), F32)] * 4,
    )(w, g, m, v)


def _place():
    return lax.axis_index("x"), lax.axis_index("y"), lax.axis_index("c")


def all_gather(name, x, in_vmem):
    r, c = x.shape
    space = pltpu.VMEM if in_vmem else pl.ANY

    def body(x_ref, out_ref, send_sems, recv_sems, local_sem):
        px, py, pc = _place()
        me, sibling = (px, py, pc), (px, py, 1 - pc)
        chips = [(1 - px, py), (px, 1 - py), (1 - px, 1 - py)]

        def rows(qx, qy, qc):
            return out_ref.at[pl.ds((4 * qx + 2 * qy + qc) * r, r), :]

        def copy(k, block, to, src=None):
            return pltpu.make_async_remote_copy(
                src_ref=rows(*block) if src is None else src, dst_ref=rows(*block),
                send_sem=send_sems.at[k], recv_sem=recv_sems.at[k], device_id=to, device_id_type=MESH)

        mine = pltpu.make_async_copy(x_ref, rows(*me), local_sem)
        mine.start()
        first = [copy(0, me, sibling, src=x_ref)]
        first += [copy(1 + j, me, (*chip, pc), src=x_ref) for j, chip in enumerate(chips)]
        for cp in first:
            cp.start()
        passed = [copy(4 + j, (*chip, pc), sibling) for j, chip in enumerate(chips)]
        for j, chip in enumerate(chips):
            copy(1 + j, (*chip, pc), me).wait_recv()
            passed[j].start()
        copy(0, sibling, me).wait_recv()
        for j, chip in enumerate(chips):
            copy(4 + j, (*chip, 1 - pc), me).wait_recv()
        for cp in first + passed:
            cp.wait_send()
        mine.wait()

    return pl.pallas_call(
        body, name=name, out_shape=jax.ShapeDtypeStruct((N_DEV * r, c), x.dtype),
        in_specs=[pl.BlockSpec(memory_space=space)], out_specs=pl.BlockSpec(memory_space=space),
        scratch_shapes=[pltpu.SemaphoreType.DMA((7,)), pltpu.SemaphoreType.DMA((7,)), pltpu.SemaphoreType.DMA],
    )(x)


_HBM =pl.BlockSpec(memory_space=pltpu.HBM)
_SEM = pl.BlockSpec(memory_space=pltpu.SEMAPHORE)
_EFFECT = pltpu.SideEffectType.DATAFLOW_SIDE_EFFECTING


def _peers():
    px, py, pc = _place()
    return [(1 - px if k & 4 else px, 1 - py if k & 2 else py, 1 - pc if k & 1 else pc) for k in range(1, N_DEV)]


def _slot(dev):
    return 4 * dev[0] + 2 * dev[1] + dev[2]


def _split_copies(src_refs, land_refs, send_sems, recv_sems, gather):
    me = _slot(_place())
    return [pltpu.make_async_remote_copy(
        src_ref=src if gather else src.at[_slot(peer)], dst_ref=land.at[me],
        send_sem=send_sems.at[a * (N_DEV - 1) + k], recv_sem=recv_sems.at[a * (N_DEV - 1) + k],
        device_id=peer, device_id_type=MESH)
        for a, (src, land) in enumerate(zip(src_refs, land_refs)) for k, peer in enumerate(_peers())]


def exchange_start(name, srcs, gather):
    n = len(srcs)
    lands = [pltpu.HBM((N_DEV,) + s.shape if gather else s.shape, s.dtype) for s in srcs]

    def body(*refs):
        send_sems, recv_sems = refs[2 * n:2 * n + 2]
        for cp in _split_copies(refs[:n], refs[n:2 * n], send_sems, recv_sems, gather):
            cp.start()
        refs[-1][...] = jnp.zeros_like(refs[-1])

    sems = pltpu.SemaphoreType.DMA((n * (N_DEV - 1),))
    res = pl.pallas_call(
        body, name=name,
        out_shape=(sems, sems, *[pltpu.HBM(s.shape, s.dtype) for s in srcs], *lands, jax.ShapeDtypeStruct((8, 128), F32)),
        in_specs=(_HBM,) * (2 * n), out_specs=(_SEM, _SEM) + (_HBM,) * (2 * n) + (pl.BlockSpec(memory_space=pltpu.VMEM),),
        input_output_aliases={i: 2 + i for i in range(2 * n)},
        compiler_params=pltpu.CompilerParams(has_side_effects=_EFFECT),
    )(*[pltpu.with_memory_space_constraint(s, pltpu.HBM) for s in srcs],
      *[pltpu.with_memory_space_constraint(lax.empty(ld.shape, ld.dtype), pltpu.HBM) for ld in lands])
    return res[0], res[1], list(res[2:2 + n]), list(res[2 + n:2 + 2 * n]), res[-1]


def exchange_wait(name, started, after, gather):
    send_sems, recv_sems, srcs, lands, _ = started
    n = len(srcs)

    def body(*refs):
        send_sems, recv_sems = refs[2 * n:2 * n + 2]
        for cp in _split_copies(refs[:n], refs[n:2 * n], send_sems, recv_sems, gather):
            cp.wait_send()
            cp.wait_recv()

    res = pl.pallas_call(
        body, name=name, out_shape=tuple(pltpu.HBM(a.shape, a.dtype) for a in srcs + lands),
        in_specs=(_HBM,) * (2 * n) + (_SEM, _SEM, pl.BlockSpec(memory_space=pl.ANY)), out_specs=(_HBM,) * (2 * n),
        input_output_aliases={i: i for i in range(2 * n)},
        compiler_params=pltpu.CompilerParams(has_side_effects=_EFFECT),
    )(*srcs, *lands, send_sems, recv_sems, after)
    return list(res[:n]), list(res[n:])


NCF = FFN_H // FFN_TC


def _size(shape):
    n = 1
    for s in shape:
        n *= s
    return n


def _padded_rows(n_elems, row_mult):
    return -(-n_elems // (D * row_mult)) * row_mult


def _pack_rows(arrs, dtype, row_mult):
    rows, offs, r0 = [], [], 0
    for a in arrs:
        flat = a.reshape(-1).astype(dtype)
        n = _padded_rows(flat.shape[0], row_mult)
        rows.append(jnp.pad(flat, (0, n * D - flat.shape[0])).reshape(n, D))
        offs.append(r0)
        r0 += n
    return jnp.concatenate(rows, 0), offs


def _unpack_rows(buf, offs, shapes):
    lead, out = buf.shape[:-2], []
    for o, shp in zip(offs, shapes):
        n = _size(shp)
        nr = -(-n // D)
        out.append(buf[..., o:o + nr, :].reshape(lead + (nr * D,))[..., :n].reshape(lead + tuple(shp)))
    return out


def _cols_from_shards(g):
    return g.transpose(1, 0, 2).reshape(g.shape[1], N_DEV * g.shape[2])


def _cols_to_shards(w):
    k, n = w.shape[0], w.shape[1] // N_DEV
    return w.reshape(k, N_DEV, n).transpose(1, 0, 2)


def _rows3(w):
    return [w[i:i + 1] for i in range(3)]


def f_mod1(xs, ps):
    return f_mod(xs, ps)[:1]


def kernel(x, c, ctx, c_ctx, ada_w, ada_b, norm_mix, norm_ffn, gla_w_in, gla_w_a2, gla_b_a, gla_head_norm, gla_w_out, sc_w_in, sc_conv_w, sc_w_out, ffn_w_up, ffn_conv_w, ffn_conv_b, ffn_w_down, final_norm, loss_target, m_c_ctx, m_ada_w, m_ada_b, m_norm_mix, m_norm_ffn, m_gla_w_in, m_gla_w_a2, m_gla_b_a, m_gla_head_norm, m_gla_w_out, m_sc_w_in, m_sc_conv_w, m_sc_w_out, m_ffn_w_up, m_ffn_conv_w, m_ffn_conv_b, m_ffn_w_down, m_final_norm, v_c_ctx, v_ada_w, v_ada_b, v_norm_mix, v_norm_ffn, v_gla_w_in, v_gla_w_a2, v_gla_b_a, v_gla_head_norm, v_gla_w_out, v_sc_w_in, v_sc_conv_w, v_sc_w_out, v_ffn_w_up, v_ffn_conv_w, v_ffn_conv_b, v_ffn_w_down, v_final_norm):
    names = ["c_ctx", "ada_w", "ada_b", "norm_mix", "norm_ffn", "gla_w_in", "gla_w_a2", "gla_b_a", "gla_head_norm",
             "gla_w_out", "sc_w_in", "sc_conv_w", "sc_w_out", "ffn_w_up", "ffn_conv_w", "ffn_conv_b", "ffn_w_down",
             "final_norm"]
    w_ = dict(zip(names, [c_ctx, ada_w, ada_b, norm_mix, norm_ffn, gla_w_in, gla_w_a2, gla_b_a, gla_head_norm, gla_w_out,
                          sc_w_in, sc_conv_w, sc_w_out, ffn_w_up, ffn_conv_w, ffn_conv_b, ffn_w_down, final_norm]))
    m_ = dict(zip(names, [m_c_ctx, m_ada_w, m_ada_b, m_norm_mix, m_norm_ffn, m_gla_w_in, m_gla_w_a2, m_gla_b_a,
                          m_gla_head_norm, m_gla_w_out, m_sc_w_in, m_sc_conv_w, m_sc_w_out, m_ffn_w_up, m_ffn_conv_w,
                          m_ffn_conv_b, m_ffn_w_down, m_final_norm]))
    v_ = dict(zip(names, [v_c_ctx, v_ada_w, v_ada_b, v_norm_mix, v_norm_ffn, v_gla_w_in, v_gla_w_a2, v_gla_b_a,
                          v_gla_head_norm, v_gla_w_out, v_sc_w_in, v_sc_conv_w, v_sc_w_out, v_ffn_w_up, v_ffn_conv_w,
                          v_ffn_conv_b, v_ffn_w_down, v_final_norm]))
    me = 4 * lax.axis_index("x") + 2 * lax.axis_index("y") + lax.axis_index("c")
    bsz = x.shape[0]
    tm = 256
    nt = SEQ // tm
    ctx_tiles = CTX // tm
    pe = functools.partial(P, per_example=True)

    groups = {"gla": [("gla_w_in", 0), ("gla_w_out", 0)], "ffn0": [("ffn_w_up", 0), ("ffn_w_down", 0)],
              "l1": [("sc_w_in", 0), ("sc_w_out", 0), ("ffn_w_up", 1), ("ffn_w_down", 1)]}
    ag_started = {}

    def start_gather(g, after=None):
        srcs = [w_[n][i].astype(BF16) for n, i in groups[g]]
        if after is not None:
            *srcs, _ = lax.optimization_barrier((*srcs, after))
        ag_started[g] = exchange_start(f"ag_{g}_start", srcs, True)
        return ag_started[g][4][0, 0]

    c = c + start_gather("gla")

    small_sharded = [c, gla_w_a2, gla_b_a, sc_conv_w, ffn_conv_w]
    pack0, offs0 = _pack_rows(small_sharded, F32, 8)
    g0 = all_gather("ag_small", pack0, True).reshape(N_DEV, pack0.shape[0], D)
    c_all, wa2_s, ba_s, scw_s, fcw_s = _unpack_rows(g0, offs0, [a.shape for a in small_sharded])
    w_a2 = wa2_s[:, 0].transpose(1, 2, 0, 3).reshape(2, RANK, KD)
    b_a = ba_s[:, 0].transpose(1, 0, 2).reshape(2, KD)
    sc_cw = scw_s[:, 0].transpose(1, 0, 2).reshape(3, D)
    ffn_cw = fcw_s.transpose(1, 2, 0, 3).reshape(2, 3, 2 * FFN_H)

    cond = jnp.concatenate([c_all.reshape(N_DEV * bsz, D), c_ctx[None], jnp.zeros((ADA_ROWS - N_DEV * bsz - 1, D), F32)], 0)
    b_mine = lax.dynamic_slice(ada_b, (0, me * ADA_COLS), (2, ADA_COLS)).reshape(2, 1, ADA_COLS)
    mod_part = ada_fwd(cond, ada_w, b_mine)
    mod = all_gather("ag_mod", mod_part.reshape(2 * ADA_ROWS, ADA_COLS), True)
    mod = mod.reshape(N_DEV, 2, ADA_ROWS, ADA_COLS).transpose(1, 2, 0, 3).reshape(2, ADA_ROWS, 6 * D)
    mods = lax.dynamic_slice(mod, (0, bsz * me, 0), (2, bsz, 6 * D))
    md = [[mods[i][:, k * D:(k + 1) * D].reshape(bsz, 1, D) for k in range(6)] for i in range(2)]
    mc = [mod[0, ADA_CTX_ROW, k * D:(k + 1) * D][None] for k in range(2)]

    norm_mix = norm_mix + start_gather("ffn0", after=mod) + start_gather("l1", after=mod)

    def gathered(g, after):
        mine, lands = exchange_wait(f"ag_{g}_wait", ag_started[g], after, True)
        return [lax.dynamic_update_index_in_dim(ld, mn, me, 0) for ld, mn in zip(lands, mine)]

    s_up, w_down = [None, None], [None, None]
    wd = jnp.zeros((128, 2 * KD), F32).at[:RANK, :KD].set(w_a2[0]).at[RANK:2 * RANK, KD:].set(w_a2[1])
    bd = b_a.reshape(1, 2 * KD)
    scw = _rows3(sc_cw)
    head_gain = gla_head_norm.reshape(1, HV)
    gains_mix = [norm_mix[i][None] for i in range(2)]
    gains_ffn = [norm_ffn[i][None] for i in range(2)]

    def tokens(a2d, t_len):
        return a2d.reshape(bsz, t_len, -1)

    def ffn_params(i):
        rows = [ffn_cw[i][t] for t in range(3)] + [ffn_conv_b[i]]
        return [P(a.reshape(2, FFN_H), w=FFN_TC, rows=True) for a in rows]

    def ffn_fwd(i, hn2):
        u = mm(f"ffn_up{i}", V(hn2, "tok"), V(s_up[i], "cols"), out="planes", out_dtype=BF16, planes_t=SEQ)
        act = rowwise(f"ffn_mid{i}", f_ffn_mid, [X(u, w=FFN_TC, planes=True)], ffn_params(i), tm=SEQ, nt=1, nc=NCF,
                      outs=[(FFN_TC, BF16, 1)])[0]
        return u, act, tokens(mm(f"ffn_down{i}", V(act, "tok"), V(w_down[i])), SEQ)

    def res_mod_fwd(name, h, y, ps):
        return rowwise(name, f_res_mod, [X(h), X(y)], ps, tm=tm, nt=nt, outs=[(D, F32, 1), (D, BF16, 1)])

    ps_in0 = [P(gains_mix[0]), pe(md[0][0]), pe(md[0][1])]
    ps_ctx = [P(gains_mix[0]), P(mc[0]), P(mc[1])]
    hn0 = rowwise("mod_in0", f_mod, [X(x)], ps_in0, tm=tm, nt=nt, outs=[(D, BF16, 1)])[0]
    hnc = rowwise("mod_ctx", f_mod, [X(ctx)], ps_ctx, tm=tm, nt=ctx_tiles, outs=[(D, BF16, 1)])[0]
    hcat = jnp.concatenate([hnc, hn0], axis=1)
    s_gin, s_gout = gathered("gla", hcat)
    w_gin = jnp.pad(_cols_from_shards(s_gin), ((0, 0), (0, GLA_IN_PAD - GLA_IN)))
    w_gout = s_gout.reshape(VD, D)
    pcat = tokens(mm("gla_in", V(hcat, "tok"), V(w_gin)), TT)
    pa_x = X(pcat, w=128, co=(GLA_IN_PAD - 128) // 128)
    la = rowwise("gla_decay", f_decay, [pa_x], [P(wd), P(bd)], tm=tm, nt=TT // tm, outs=[(2 * KD, F32, 1)])[0]
    o2, s_all = gla_fwd(pcat, la)
    post_xs = [X(o2, w=VD, co=0, ro=ctx_tiles, split=HEADS), X(o2, w=VD, co=1, ro=ctx_tiles, split=HEADS),
               X(pcat, w=VD, co=2, ro=ctx_tiles, split=HEADS)]
    yin0 = rowwise("gla_post", f_gla_post, post_xs, [P(head_gain)], tm=tm, nt=nt, outs=[(VD, BF16, HEADS)])[0]
    y0 = tokens(mm("gla_out", V(yin0, "tok"), V(w_gout)), SEQ)
    ps_mid0 = [pe(md[0][2]), P(gains_ffn[0]), pe(md[0][3]), pe(md[0][4])]
    h1_0, hn2_0 = res_mod_fwd("res_mod_mid0", x, y0, ps_mid0)
    s_up[0], s_down0 = gathered("ffn0", hn2_0)
    w_down[0] = s_down0.reshape(FFN_H, D)
    u0, act0, fo0 = ffn_fwd(0, hn2_0)
    ps_in1 = [pe(md[0][5]), P(gains_mix[1]), pe(md[1][0]), pe(md[1][1])]
    h2_0, hn1 = res_mod_fwd("res_mod_in1", h1_0, fo0, ps_in1)

    s_sin, s_sout, s_up[1], s_down1 = gathered("l1", hn1)
    w_sout, w_down[1] = s_sout.reshape(D, D), s_down1.reshape(FFN_H, D)
    p1 = tokens(mm("sc_in", V(hn1, "tok"), V(s_sin, "cols")), SEQ)
    sc_ps = [P(a) for a in scw]
    yin1 = rowwise("sc_mid", f_sc_mid, [X(p1, split=3)], sc_ps, tm=tm, nt=nt, outs=[(D, BF16, 1)])[0]
    y1 = tokens(mm("sc_out", V(yin1, "tok"), V(w_sout)), SEQ)
    ps_mid1 = [pe(md[1][2]), P(gains_ffn[1]), pe(md[1][3]), pe(md[1][4])]
    h1_1, hn2_1 = res_mod_fwd("res_mod_mid1", h2_0, y1, ps_mid1)
    u1, act1, fo1 = ffn_fwd(1, hn2_1)
    loss8, dh1_1, dfo1, dm5_1, g_final = final_loss(h1_1, fo1, md[1][5], final_norm[None], loss_target)

    def ffn_bwd(i, u, act, hn2, dfo):
        dact = tokens(mm(f"ffn_down_dx{i}", V(dfo, "tok"), V(w_down[i]), form="nt", out_dtype=BF16), SEQ)
        g_down = mm(f"ffn_down_dw{i}", V(act, "tok"), V(dfo, "tok"), form="tn", out_dtype=BF16)
        r = rowwise(f"ffn_mid_bwd{i}", f_ffn_mid, [X(u, w=FFN_TC, planes=True)], ffn_params(i), tm=SEQ, nt=1, nc=NCF,
                    douts=[X(dact, w=FFN_TC)], dx={0: BF16}, dp=[0, 1, 2, 3])
        du, g_cw, g_cb = r[0], jnp.stack([a.reshape(2 * FFN_H) for a in r[1:4]]), r[4].reshape(1, 2 * FFN_H)
        dhn2 = tokens(mm(f"ffn_up_dx{i}", V(du, "planes"), V(s_up[i], "cols"), form="nt", out_dtype=BF16), SEQ)
        g_up = mm(f"ffn_up_dw{i}", V(hn2, "tok"), V(du, "planes"), form="tn", out="cols", out_dtype=BF16)
        return dhn2, g_up, row_slots(g_down), g_cw, g_cb

    def res_mod_bwd(name, h, y, ps, dh1, dhn):
        return rowwise(name, f_res_mod, [X(h), X(y)], ps, tm=tm, nt=nt, douts=[X(dh1), X(dhn)],
                       dx={0: F32, 1: BF16}, dp=[0, 1, 2, 3])

    def row_slots(g):
        return g.reshape(N_DEV, -1, g.shape[-1])

    a2a_started = {}

    def send_grads(g, slots, after=None):
        if after is not None:
            *slots, _ = lax.optimization_barrier((*slots, after))
        a2a_started[g] = exchange_start(f"a2a_{g}_start", list(slots), False)
        return a2a_started[g][4][0, 0]

    def after_start(ps, tok):
        return [dict(ps[0], a=ps[0]["a"] + tok)] + ps[1:]

    dhn2_1, g_up1, g_down1, g_fcw1, g_fcb1 = ffn_bwd(1, u1, act1, hn2_1, dfo1)
    dh2_0, dy1, dm2_1, g_nffn1, dm3_1, dm4_1 = res_mod_bwd("res_mod_mid1_bwd", h2_0, y1, ps_mid1, dh1_1, dhn2_1)
    dyin1 = tokens(mm("sc_out_dx", V(dy1, "tok"), V(w_sout), form="nt", out_dtype=BF16), SEQ)
    g_sout = row_slots(mm("sc_out_dw", V(yin1, "tok"), V(dy1, "tok"), form="tn", out_dtype=BF16))
    r = rowwise("sc_mid_bwd", f_sc_mid, [X(p1, split=3)], sc_ps, tm=tm, nt=nt, douts=[X(dyin1)], dx={0: BF16}, dp=[0, 1, 2])
    dp1, g_scw = r[0], jnp.concatenate(r[1:4], 0)
    dhn1 = tokens(mm("sc_in_dx", V(dp1, "tok"), V(s_sin, "cols"), form="nt", out_dtype=BF16), SEQ)
    g_sin = mm("sc_in_dw", V(hn1, "tok"), V(dp1, "tok"), form="tn", out="cols", out_dtype=BF16)
    tok = send_grads("l1", [g_sin, g_sout, g_up1, g_down1])
    dh1_0, dfo0, dm5_0, g_nmix1, dm0_1, dm1_1 = res_mod_bwd("res_mod_in1_bwd", h1_0, fo0, after_start(ps_in1, tok), dh2_0, dhn1)

    dhn2_0, g_up0, g_down0, g_fcw0, g_fcb0 = ffn_bwd(0, u0, act0, hn2_0, dfo0)
    tok = send_grads("ffn0", [g_up0, g_down0])
    dx_res, dy0, dm2_0, g_nffn0, dm3_0, dm4_0 = res_mod_bwd("res_mod_mid0_bwd", x, y0, after_start(ps_mid0, tok), dh1_0, dhn2_0)
    dyin0 = tokens(mm("gla_out_dx", V(dy0, "tok"), V(w_gout), form="nt", out_dtype=BF16), SEQ)
    g_gout = row_slots(mm("gla_out_dw", V(yin0, "tok"), V(dy0, "tok"), form="tn", out_dtype=BF16))
    do, dgate, g_head = rowwise("gla_post_bwd", f_gla_post, post_xs, [P(head_gain)], tm=tm, nt=nt,
                                douts=[X(dyin0, split=HEADS)], dx={0: F32, 2: BF16}, dp=[0])
    dq2, dk2, dv2, dla = gla_bwd(pcat, la, s_all, do)
    dpa, g_wd, g_bd = rowwise("gla_decay_bwd", f_decay, [pa_x], [P(wd), P(bd)], tm=tm, nt=TT // tm, douts=[X(dla)],
                              dx={0: BF16}, dp=[0, 1])
    dpcat = gla_combine(dq2, dk2, dv2, dgate, dpa)
    dhcat = tokens(mm("gla_in_dx", V(dpcat, "tok"), V(w_gin), form="nt", out_dtype=BF16), TT)
    g_gin = _cols_to_shards(mm("gla_in_dw", V(hcat, "tok"), V(dpcat, "tok"), form="tn", out_dtype=BF16)[:, :GLA_IN])
    grad_x, g_nmix0, dm0_0, dm1_0 = rowwise("mod_in0_bwd", f_mod, [X(x)], ps_in0, tm=tm, nt=nt,
                                            douts=[X(dhcat, ro=ctx_tiles), X(dx_res)], dx={0: F32}, dp=[0, 1, 2])
    g_nmix0c, dmc0, dmc1 = rowwise("mod_ctx_bwd", f_mod1, [X(ctx)], ps_ctx, tm=tm, nt=ctx_tiles, douts=[X(dhcat)],
                                   dx={}, dp=[0, 1, 2])

    zero_row = jnp.zeros((1, 4 * D), F32)
    dmod = [jnp.concatenate([jnp.concatenate([a.reshape(bsz, D) for a in dms], 1), ctx_row], 0)
            for dms, ctx_row in (([dm0_0, dm1_0, dm2_0, dm3_0, dm4_0, dm5_0], jnp.concatenate([dmc0, dmc1, zero_row], 1)),
                                 ([dm0_1, dm1_1, dm2_1, dm3_1, dm4_1, dm5_1], jnp.zeros((1, 6 * D), F32)))]
    g_wa2 = jnp.stack([g_wd[:RANK, :KD], g_wd[RANK:2 * RANK, KD:]])
    small_grads = [jnp.stack(dmod), jnp.concatenate([g_nmix0 + g_nmix0c, g_nmix1], 0), jnp.concatenate([g_nffn0, g_nffn1], 0),
                   g_head, jnp.concatenate([g_fcb0, g_fcb1], 0), g_final, g_wa2, g_bd.reshape(2, KD), g_scw,
                   jnp.stack([g_fcw0, g_fcw1]), loss8[:1]]
    pack1, offs1 = _pack_rows(small_grads, F32, 8)
    g1 = all_gather("ag_grads", pack1, True).reshape(N_DEV, pack1.shape[0], D)
    dmod_all = _unpack_rows(g1, offs1[:1], [small_grads[0].shape])[0]
    tot = _unpack_rows(sum_slots("sum_small", g1), offs1, [a.shape for a in small_grads])
    loss = tot[10][0, 0]
    dm_rows = dmod_all[:, :, :bsz].transpose(1, 0, 2, 3).reshape(2, N_DEV * bsz, 6 * D)
    dm_full = jnp.concatenate([dm_rows, tot[0][:, bsz:], jnp.zeros((2, ADA_ROWS - N_DEV * bsz - 1, 6 * D), F32)], 1)
    dm_mine = lax.dynamic_slice(dm_full, (0, 0, me * ADA_COLS), (2, ADA_ROWS, ADA_COLS))
    g_ada_w, g_ada_b, cpart = ada_bwd(cond, dm_mine, dm_full, ada_w)
    cparts = all_gather("ag_cctx", cpart, True).reshape(N_DEV, ADA_ROWS - ADA_CTX_ROW, D)[:, 0]
    g_cctx = cctx_grad(cparts, c_ctx[None])[0]
    tok = send_grads("gla", [g_gin, g_gout], after=g_cctx)

    def my_cols(full, n):
        return lax.dynamic_slice_in_dim(full, me * n, n, axis=full.ndim - 1)

    grads = {
        "c_ctx": g_cctx, "ada_b": g_ada_b.reshape(2, 6 * D), "norm_mix": tot[1], "norm_ffn": tot[2],
        "gla_head_norm": tot[3], "ffn_conv_b": tot[4], "final_norm": tot[5].reshape(D),
        "gla_w_a2": my_cols(tot[6], KD // N_DEV)[None], "gla_b_a": my_cols(tot[7], KD // N_DEV)[None],
        "sc_conv_w": my_cols(tot[8], D // N_DEV)[None], "ffn_conv_w": my_cols(tot[9], 2 * FFN_H // N_DEV),
    }

    res_ada = adamw("adamw_ada", *[a.reshape(2 * D, ADA_COLS) for a in (ada_w, g_ada_w, m_ada_w, v_ada_w)])
    grads["c_ctx"] = g_cctx + tok
    big = ["gla_w_in", "gla_w_out", "sc_w_in", "sc_w_out", "ffn_w_up", "ffn_w_down"]
    small = [n for n in names if n not in big and n != "ada_w"]
    g_small = _pack_rows([grads[n] for n in small], F32, 8)[0]
    res_small = adamw("adamw_small", _pack_rows([w_[n] for n in small], F32, 8)[0], g_small,
                      _pack_rows([m_[n] for n in small], F32, 8)[0], _pack_rows([v_[n] for n in small], F32, 8)[0])
    offs_s = _pack_rows([w_[n] for n in small], F32, 8)[1]

    big_res, done = {}, [res_small[0][0, 0], res_ada[0][0, 0]]
    for g in ("l1", "ffn0", "gla"):
        sent, lands = exchange_wait(f"a2a_{g}_wait", a2a_started[g], jnp.stack(done), False)
        for (n, i), mine, land in zip(groups[g], sent, lands):
            land = lax.dynamic_update_index_in_dim(land, lax.dynamic_index_in_dim(mine, me, 0, keepdims=False), me, 0)
            big_res[(n, i)] = adamw(f"adamw_{n}{i}", w_[n], land, m_[n], v_[n], layer=i)
            done.append(big_res[(n, i)][0][0, 0])

    out = {}
    for kind, idx in (("grad", 0), ("delta", 1), ("new_m", 2), ("new_v", 3)):
        vals = {n: jnp.stack([big_res[(n, i)][idx] for i in range(w_[n].shape[0])]) for n in big}
        vals["ada_w"] = res_ada[idx].reshape(ada_w.shape)
        vals.update(zip(small, _unpack_rows(res_small[idx], offs_s, [w_[n].shape for n in small])))
        out[kind] = [vals[n] for n in names]
    return (loss, grad_x, *out["grad"], *out["delta"], *out["new_m"], *out["new_v"])
```

```python
import functools

import jax
import jax.numpy as jnp
from jax import lax
from jax.experimental import pallas as pl
from jax.experimental.pallas import tpu as pltpu

F32 = jnp.float32
BF16 = jnp.bfloat16

N_DEV = 8
D = 1024
SEQ = 2048
CTX = 256
TT = CTX + SEQ
GRID_W = 64
CHUNK = 64
HEADS = 4
HK = 128
HV = 256
KD = 512
VD = 1024
RANK = 16
TAU = 16.0
GLA_IN = 3104
GLA_IN_PAD = 3200
FFN_H = 2560
FFN_TC = 256
EPS = 1e-6
LR, B1, B2, AEPS, WD, STEP = 0.001, 0.9, 0.999, 1e-08, 0.01, 10
MESH = pl.DeviceIdType.MESH


def _blocks(n):
    return [n] + [t for t in range(n - n % 128, 0, -128) if n % t == 0 and t != n]


def V(arr, kind="flat"):
    if kind == "tok":
        return V(arr.reshape(-1, arr.shape[-1]))
    if kind == "flat":
        r, c = arr.shape
        return dict(a=arr, kind=kind, shape=(r, c), rows=_blocks(r), cols=_blocks(c))
    if kind == "planes":
        bsz, _, t, ch = arr.shape
        return dict(a=arr, kind=kind, shape=(bsz * t, 2 * ch), rows=_blocks(t), cols=[2 * ch] + _blocks(ch), t=t, ch=ch)
    _, r, n = arr.shape
    return dict(a=arr, kind=kind, shape=(r, N_DEV * n), rows=_blocks(r), cols=[8 * n, 4 * n, 2 * n], n=n)


def _view_spec(v, br, bc, idx):
    if v["kind"] == "flat":
        return pl.BlockSpec((br, bc), idx)
    if v["kind"] == "planes":
        nt = v["t"] // br
        if bc == 2 * v["ch"]:
            return pl.BlockSpec((None, 2, br, v["ch"]), lambda i, j, k: (idx(i, j, k)[0] // nt, 0, idx(i, j, k)[0] % nt, 0))
        nch = v["ch"] // bc

        def at(i, j, k):
            r, c = idx(i, j, k)
            return r // nt, c // nch, r % nt, c % nch
        return pl.BlockSpec((None, None, br, bc), at)
    return pl.BlockSpec((bc // v["n"], br, v["n"]), lambda i, j, k: (idx(i, j, k)[1], idx(i, j, k)[0], 0))


def _out_view(kind, rows, cols, dtype, planes_t=None):
    if kind == "flat":
        shape = (rows, cols)
    elif kind == "planes":
        shape = (rows // planes_t, 2, planes_t, cols // 2)
    else:
        shape = (N_DEV, rows, cols // N_DEV)
    return V(jax.ShapeDtypeStruct(shape, dtype), kind)


MM_VMEM_BUDGET = 40 * 2 ** 20
MM_VMEM_LIMIT = 56 * 2 ** 20
MM_MAX_TILE = 1536


def _mm_tiles(m, n, kk, ms, ns, ks, a_bytes, b_bytes, o_bytes):
    best = None
    for tk in ks:
        for tm in [t for t in ms if t <= MM_MAX_TILE]:
            for tn in [t for t in ns if t <= MM_MAX_TILE]:
                one_k = tk == kk
                need = 2 * (tm * tk * a_bytes + tk * tn * b_bytes + tm * tn * o_bytes) + (0 if one_k else tm * tn * 4)
                if need > MM_VMEM_BUDGET:
                    continue
                steps = (m // tm) * (n // tn) * (kk // tk)
                traffic = (m * kk * a_bytes * (1 if one_k else n // tn)
                           + kk * n * b_bytes * (1 if one_k and n == tn else m // tm) + m * n * o_bytes)
                fill = (tm * tk * a_bytes + tk * tn * b_bytes) / 2.5e12
                cost = max(2.0 * m * n * kk / (9e14 if one_k else 6.5e14), traffic / 2.5e12) + steps * 0.4e-6 + fill
                if best is None or cost < best[0]:
                    best = (cost, tm, tn, tk)
    return best[1:]


def mm(name, a, b, form="nn", out="flat", out_dtype=F32, planes_t=None):
    (m, kk) = a["shape"][::-1] if form == "tn" else a["shape"]
    n = b["shape"][0] if form == "nt" else b["shape"][1]
    assert (b["shape"][1] if form == "nt" else b["shape"][0]) == kk, (name, a["shape"], b["shape"])
    o = _out_view(out, m, n, out_dtype, planes_t)
    a_m, a_k = (a["cols"], a["rows"]) if form == "tn" else (a["rows"], a["cols"])
    b_k, b_n = (b["cols"], b["rows"]) if form == "nt" else (b["rows"], b["cols"])
    tm, tn, tk = _mm_tiles(m, n, kk, [t for t in a_m if t in o["rows"]], [t for t in b_n if t in o["cols"]],
                           [t for t in a_k if t in b_k], a["a"].dtype.itemsize, b["a"].dtype.itemsize,
                           jnp.dtype(out_dtype).itemsize)
    nk = kk // tk
    dn = (((0 if form == "tn" else 1,), (1 if form == "nt" else 0,)), ((), ()))

    def load(ref):
        if len(ref.shape) == 3:
            return jnp.concatenate([ref[p] for p in range(ref.shape[0])], axis=-1).astype(BF16)
        return ref[...].astype(BF16)

    def store(o_ref, val):
        val = val.astype(out_dtype)
        if len(o_ref.shape) == 3:
            w = o_ref.shape[-1]
            for p in range(o_ref.shape[0]):
                o_ref[p] = val[:, p * w:(p + 1) * w]
        else:
            o_ref[...] = val

    def body(a_ref, b_ref, o_ref, *acc):
        if nk == 1:
            store(o_ref, lax.dot_general(load(a_ref), load(b_ref), dn, preferred_element_type=F32))
            return
        k, acc_ref = pl.program_id(2), acc[0]

        @pl.when(k == 0)
        def _():
            acc_ref[...] = jnp.zeros_like(acc_ref)

        acc_ref[...] += lax.dot_general(load(a_ref), load(b_ref), dn, preferred_element_type=F32)

        @pl.when(k == nk - 1)
        def _():
            store(o_ref, acc_ref[...])

    if form == "tn":
        a_spec = _view_spec(a, tk, tm, lambda i, j, k: (k, i))
    else:
        a_spec = _view_spec(a, tm, tk, lambda i, j, k: (i, k))
    if form == "nt":
        b_spec = _view_spec(b, tn, tk, lambda i, j, k: (j, k))
    else:
        b_spec = _view_spec(b, tk, tn, lambda i, j, k: (k, j))
    return pl.pallas_call(
        body, name=name, grid=(m // tm, n // tn, nk),
        in_specs=[a_spec, b_spec], out_specs=_view_spec(o, tm, tn, lambda i, j, k: (i, j)), out_shape=o["a"],
        scratch_shapes=[pltpu.VMEM((tm, tn), F32)] if nk > 1 else [],
        compiler_params=pltpu.CompilerParams(dimension_semantics=("parallel", "parallel", "arbitrary"),
                                             vmem_limit_bytes=MM_VMEM_LIMIT),
    )(a["a"], b["a"])


def X(arr, w=None, co=0, ro=0, split=1, planes=False):
    return dict(a=arr, w=arr.shape[-1] if w is None else w, co=co, ro=ro, split=2 if planes else split,
                mode="planes" if planes else "cols")


def P(arr, per_example=False, w=None, split=1, rows=False):
    return dict(a=arr, e=per_example, w=arr.shape[-1] if w is None else w, split=arr.shape[-2] if rows else split,
                mode="rows" if rows else "cols")


def _pieces(ref, s):
    if s["mode"] == "planes":
        return [ref[0], ref[1]]
    if s["mode"] == "rows":
        return [ref[i:i + 1, :] for i in range(s["split"])]
    w = ref.shape[-1] // s["split"]
    return [ref[:, i * w:(i + 1) * w] for i in range(s["split"])]


def _store(ref, pieces, s, accumulate=False):
    w = ref.shape[-1] // len(pieces)
    for i, p in enumerate(pieces):
        at = (i,) if s["mode"] == "planes" else (slice(i, i + 1),) if s["mode"] == "rows" else (slice(None), slice(i * w, (i + 1) * w))
        if accumulate:
            ref[at] += p.astype(ref.dtype)
        else:
            ref[at] = p.astype(ref.dtype)


def rowwise(name, f, xs, ps, *, tm, nt, nc=1, outs=None, douts=None, dx=None, dp=None):
    bsz = xs[0]["a"].shape[0]
    fwd = douts is None
    nx, np_ = len(xs), len(ps)
    douts = [] if fwd else douts
    dx = {} if fwd else dx
    dp = [] if fwd else dp

    def x_spec(s):
        if s["mode"] == "planes":
            return pl.BlockSpec((None, 2, tm, s["w"]), lambda c, b, t, s=s: (b, 0, t + s["ro"], c + s["co"]))
        return pl.BlockSpec((None, tm, s["w"]), lambda c, b, t, s=s: (b, t + s["ro"], c + s["co"]))

    def x_out(s, dt):
        if s["mode"] == "planes":
            return (jax.ShapeDtypeStruct((bsz, 2, nt * tm, nc * s["w"]), dt),
                    pl.BlockSpec((None, 2, tm, s["w"]), lambda c, b, t: (b, 0, t, c)))
        return (jax.ShapeDtypeStruct((bsz, nt * tm, nc * s["w"]), dt), pl.BlockSpec((None, tm, s["w"]), lambda c, b, t: (b, t, c)))

    def p_spec(s):
        r = s["a"].shape[-2]
        if s["e"]:
            return pl.BlockSpec((None, r, s["w"]), lambda c, b, t: (b, 0, c))
        return pl.BlockSpec((r, s["w"]), lambda c, b, t: (0, c))

    in_specs = [x_spec(s) for s in xs] + [p_spec(s) for s in ps] + [x_spec(s) for s in douts]
    operands = [s["a"] for s in xs] + [s["a"] for s in ps] + [s["a"] for s in douts]
    if fwd:
        out_modes = [dict(mode="cols", split=sp) for (_, _, sp) in outs]
        out_shape = [jax.ShapeDtypeStruct((bsz, nt * tm, nc * w), dt) for (w, dt, _) in outs]
        out_specs = [pl.BlockSpec((None, tm, w), lambda c, b, t: (b, t, c)) for (w, _, _) in outs]
    else:
        dx_outs = [x_out(xs[i], dt) for i, dt in dx.items()]
        out_shape, out_specs = [o[0] for o in dx_outs], [o[1] for o in dx_outs]
        for j in dp:
            s = ps[j]
            r = s["a"].shape[-2]
            if s["e"]:
                out_shape.append(jax.ShapeDtypeStruct((bsz, r, nc * s["w"]), F32))
                out_specs.append(pl.BlockSpec((None, r, s["w"]), lambda c, b, t: (b, 0, c)))
            else:
                out_shape.append(jax.ShapeDtypeStruct((r, nc * s["w"]), F32))
                out_specs.append(pl.BlockSpec((r, s["w"]), lambda c, b, t: (0, c)))

    def body(*refs):
        x_refs, p_refs = refs[:nx], refs[nx:nx + np_]
        d_refs = refs[nx + np_:nx + np_ + len(douts)]
        o_refs = refs[nx + np_ + len(douts):]
        xv = [[p.astype(F32) for p in _pieces(r, s)] for r, s in zip(x_refs, xs)]
        pv = [[p.astype(F32) for p in _pieces(r, s)] for r, s in zip(p_refs, ps)]
        if fwd:
            for r, pieces, s in zip(o_refs, f(xv, pv), out_modes):
                _store(r, pieces, s)
            return
        _, vjp = jax.vjp(f, xv, pv)
        cot = [[p.astype(F32) for p in _pieces(r, s)] for r, s in zip(d_refs, douts)]
        dxv, dpv = vjp(cot)
        for r, i in zip(o_refs, dx):
            _store(r, dxv[i], xs[i])
        b, t = pl.program_id(1), pl.program_id(2)
        for r, j in zip(o_refs[len(dx):], dp):
            first = (t == 0) if ps[j]["e"] else jnp.logical_and(b == 0, t == 0)

            @pl.when(first)
            def _(r=r, j=j):
                _store(r, dpv[j], ps[j])

            @pl.when(jnp.logical_not(first))
            def _(r=r, j=j):
                _store(r, dpv[j], ps[j], accumulate=True)

    res = pl.pallas_call(
        body, name=name, grid=(nc, bsz, nt), in_specs=in_specs, out_specs=out_specs, out_shape=out_shape,
        compiler_params=pltpu.CompilerParams(dimension_semantics=("arbitrary", "arbitrary", "arbitrary")),
    )(*operands)
    return res


def _keep_rows(a, shift, keep):
    n = a.shape[0]
    t = lax.broadcasted_iota(jnp.int32, a.shape, 0)
    return jnp.where(keep(t, n), pltpu.roll(a, shift % n, 0), 0.0)


def _shift_pair(step, keep_prev, keep_next):
    @jax.custom_vjp
    def prev(a):
        return _keep_rows(a, step, keep_prev)

    @jax.custom_vjp
    def nxt(a):
        return _keep_rows(a, -step, keep_next)

    prev.defvjp(lambda a: (prev(a), None), lambda _, g: (nxt(g),))
    nxt.defvjp(lambda a: (nxt(a), None), lambda _, g: (prev(g),))
    return prev, nxt


prev_tok, next_tok = _shift_pair(1, lambda t, n: t % GRID_W != 0, lambda t, n: t % GRID_W != GRID_W - 1)
prev_row, next_row = _shift_pair(GRID_W, lambda t, n: t >= GRID_W, lambda t, n: t < n - GRID_W)


@jax.custom_vjp
def bdot(a, w):
    return jnp.dot(a.astype(BF16), w.astype(BF16), preferred_element_type=F32)


def _bdot_bwd(res, g):
    a, w = res
    gb = g.astype(BF16)
    da = lax.dot_general(gb, w.astype(BF16), (((1,), (1,)), ((), ())), preferred_element_type=F32)
    dw = lax.dot_general(a.astype(BF16), gb, (((0,), (0,)), ((), ())), preferred_element_type=F32)
    return da, dw


bdot.defvjp(lambda a, w: (bdot(a, w), (a, w)), _bdot_bwd)


@jax.custom_vjp
def log_sigmoid(z):
    return jnp.minimum(z, 0.0) - jnp.log(1.0 + jnp.exp(-jnp.abs(z)))


def _lsig_bwd(z, g):
    e = jnp.exp(-jnp.abs(z))
    return (g * jnp.where(z >= 0, e, 1.0) / (1.0 + e),)


log_sigmoid.defvjp(lambda z: (log_sigmoid(z), z), _lsig_bwd)


def silu(x):
    return x * jax.nn.sigmoid(x)


def _rms(x):
    return x * lax.rsqrt(jnp.mean(x * x, axis=-1, keepdims=True) + EPS)


def _mod(x, gain, shift, scale):
    return _rms(x) * gain * (1.0 + scale) + shift


def f_mod(xs, ps):
    ((h,),), ((gain,), (shift,), (scale,)) = xs, ps
    return [[_mod(h, gain, shift, scale)], [h]]


def f_res_mod(xs, ps):
    ((h,), (y,)), ((gate,), (gain,), (shift,), (scale,)) = xs, ps
    h1 = h + gate * y
    return [[h1], [_mod(h1, gain, shift, scale)]]


def f_ffn_mid(xs, ps):
    ((ua, ug),), ((w0a, w0g), (w1a, w1g), (w2a, w2g), (ba, bg)) = xs, ps
    a = w0a * prev_row(ua) + w1a * ua + w2a * next_row(ua) + ba
    g = w0g * prev_row(ug) + w1g * ug + w2g * next_row(ug) + bg
    return [[a * silu(g)]]


def f_sc_mid(xs, ps):
    ((bg, cg, v),), ((w0,), (w1,), (w2,)) = xs, ps
    z = cg * v
    return [[bg * (w0 * prev_tok(z) + w1 * z + w2 * next_tok(z))]]


def f_decay(xs, ps):
    ((a,),), ((wd,), (bd,)) = xs, ps
    return [[log_sigmoid(bdot(a, wd) + bd) / TAU]]


def f_gla_post(xs, ps):
    (of, ob, g), ((gain,),) = xs, ps
    return [[_rms(a + b) * gain * silu(c) for a, b, c in zip(of, ob, g)]]


NCH = TT // CHUNK
CTX_CH = CTX // CHUNK
_NT = (((1,), (1,)), ((), ()))
_TN = (((0,), (0,)), ((), ()))
_NN = (((1,), (0,)), ((), ()))


def _chunk_of(d, j):
    return jnp.where(d == 0, j, jnp.where(j < CTX_CH, CTX_CH - 1 - j, NCH + CTX_CH - 1 - j))


def _dot(a, b, dn):
    return lax.dot_general(a, b, dn, preferred_element_type=F32)


def _mask_dot(m, g):
    g0 = g.astype(BF16)
    r1 = g - g0.astype(F32)
    g1 = r1.astype(BF16)
    g2 = (r1 - g1.astype(F32)).astype(BF16)
    return _dot(m, g0, _NN) + _dot(m, g1, _NN) + _dot(m, g2, _NN)


def _causal(d):
    row = lax.broadcasted_iota(jnp.int32, (CHUNK, CHUNK), 0)
    col = lax.broadcasted_iota(jnp.int32, (CHUNK, CHUNK), 1)
    delta = jnp.where(d == 0, col - row, row - col)
    return delta <= 0, delta >= 0


def _gla_in_specs(bsz, rev):
    def blk(d, j):
        return _chunk_of(d, (NCH - 1 - j) if rev else j)

    return [
        pl.BlockSpec((bsz, CHUNK, KD), lambda d, j: (0, blk(d, j), 0)),
        pl.BlockSpec((bsz, CHUNK, KD), lambda d, j: (0, blk(d, j), 1)),
        pl.BlockSpec((bsz, CHUNK, VD), lambda d, j: (0, blk(d, j), 1)),
        pl.BlockSpec((bsz, CHUNK, KD), lambda d, j: (0, blk(d, j), d)),
    ], blk


def gla_fwd(pcat, la):
    bsz = pcat.shape[0]
    in_specs, blk = _gla_in_specs(bsz, False)

    def body(q_ref, k_ref, v_ref, la_ref, o_ref, s_ref, st):
        d, j = pl.program_id(0), pl.program_id(1)

        @pl.when(j == 0)
        def _():
            st[...] = jnp.zeros_like(st)

        s_ref[...] = st[...]
        causal, _ = _causal(d)
        mf = causal.astype(BF16)
        for e, h in [(e, h) for e in range(bsz) for h in range(HEADS)]:
            ks_, vs_ = slice(h * HK, (h + 1) * HK), slice(h * HV, (h + 1) * HV)
            q, k, v, g = q_ref[e, :, ks_] * (HK ** -0.5), k_ref[e, :, ks_], v_ref[e, :, vs_].astype(BF16), la_ref[e, :, ks_]
            b = _mask_dot(mf, g)
            bl = jnp.sum(g, axis=0, keepdims=True)
            qs = (q * jnp.exp(b)).astype(BF16)
            ks = (k * jnp.exp(-b)).astype(BF16)
            kd = (k * jnp.exp(bl - b)).astype(BF16)
            s = st[e, h]
            att = jnp.where(causal, _dot(qs, ks, _NT), 0.0).astype(BF16)
            o_ref[e, :, vs_] = _dot(qs, s.astype(BF16), _NT) + _dot(att, v, _NN)
            st[e, h] = jnp.exp(bl) * s + _dot(v, kd, _TN)

    return pl.pallas_call(
        body, name="gla_fwd", grid=(2, NCH), in_specs=in_specs,
        out_specs=[pl.BlockSpec((bsz, CHUNK, VD), lambda d, j: (0, blk(d, j), d)),
                   pl.BlockSpec((bsz, None, None, HEADS, HV, HK), lambda d, j: (0, d, j, 0, 0, 0))],
        out_shape=[jax.ShapeDtypeStruct((bsz, TT, 2 * VD), F32), jax.ShapeDtypeStruct((bsz, 2, NCH, HEADS, HV, HK), F32)],
        scratch_shapes=[pltpu.VMEM((bsz, HEADS, HV, HK), F32)],
        compiler_params=pltpu.CompilerParams(dimension_semantics=("arbitrary", "arbitrary")),
    )(pcat, pcat, pcat, la)


def gla_bwd(pcat, la, s_all, do):
    bsz = pcat.shape[0]
    in_specs, blk = _gla_in_specs(bsz, True)
    in_specs += [
        pl.BlockSpec((bsz, None, None, HEADS, HV, HK), lambda d, j: (0, d, NCH - 1 - j, 0, 0, 0)),
        pl.BlockSpec((bsz, CHUNK, VD), lambda d, j: (0, jnp.maximum(blk(d, j) - CTX_CH, 0), 0)),
    ]

    def body(q_ref, k_ref, v_ref, la_ref, s_ref, do_ref, dq_ref, dk_ref, dv_ref, dla_ref, dst):
        d, j = pl.program_id(0), pl.program_id(1)

        @pl.when(j == 0)
        def _():
            dst[...] = jnp.zeros_like(dst)

        latent = blk(d, j) >= CTX_CH
        causal, causal_t = _causal(d)
        mt = causal_t.astype(BF16)
        mf = causal.astype(BF16)
        scale = HK ** -0.5
        for e, h in [(e, h) for e in range(bsz) for h in range(HEADS)]:
            ks_, vs_ = slice(h * HK, (h + 1) * HK), slice(h * HV, (h + 1) * HV)
            q, k, v, g = q_ref[e, :, ks_] * scale, k_ref[e, :, ks_], v_ref[e, :, vs_].astype(BF16), la_ref[e, :, ks_]
            b = _mask_dot(mf, g)
            bl = jnp.sum(g, axis=0, keepdims=True)
            ex, ei, ed, el = jnp.exp(b), jnp.exp(-b), jnp.exp(bl - b), jnp.exp(bl)
            qs, ks, kd = q * ex, k * ei, k * ed
            qsb, ksb, kdb = qs.astype(BF16), ks.astype(BF16), kd.astype(BF16)
            s, ds1 = s_ref[e, h], dst[e, h]
            sb, ds1b = s.astype(BF16), ds1.astype(BF16)
            dob = jnp.where(latent, do_ref[e, :, vs_], 0.0).astype(BF16)
            att = jnp.where(causal, _dot(qsb, ksb, _NT), 0.0).astype(BF16)
            datt = jnp.where(causal, _dot(dob, v, _NT), 0.0).astype(BF16)
            dqs = _dot(dob, sb, _NN) + _dot(datt, ksb, _NN)
            dks = _dot(datt, qsb, _TN)
            dv_ref[e, :, vs_] = _dot(att, dob, _TN) + _dot(kdb, ds1b, _NT)
            dkd = _dot(v, ds1b, _NN)
            dst[e, h] = _dot(dob, qsb, _TN) + el * ds1
            del_ = jnp.sum(s * ds1, axis=0, keepdims=True)
            dq_ref[e, :, ks_] = dqs * ex * scale
            dk_ref[e, :, ks_] = dks * ei + dkd * ed
            db = dqs * qs - dks * ks - dkd * kd
            dbl = jnp.sum(dkd * kd, axis=0, keepdims=True) + del_ * el
            dla_ref[e, :, ks_] = _mask_dot(mt, db) + dbl

    return pl.pallas_call(
        body, name="gla_bwd", grid=(2, NCH), in_specs=in_specs,
        out_specs=[pl.BlockSpec((None, bsz, CHUNK, KD), lambda d, j: (d, 0, blk(d, j), 0)),
                   pl.BlockSpec((None, bsz, CHUNK, KD), lambda d, j: (d, 0, blk(d, j), 0)),
                   pl.BlockSpec((None, bsz, CHUNK, VD), lambda d, j: (d, 0, blk(d, j), 0)),
                   pl.BlockSpec((bsz, CHUNK, KD), lambda d, j: (0, blk(d, j), d))],
        out_shape=[jax.ShapeDtypeStruct((2, bsz, TT, KD), F32), jax.ShapeDtypeStruct((2, bsz, TT, KD), F32),
                   jax.ShapeDtypeStruct((2, bsz, TT, VD), F32), jax.ShapeDtypeStruct((bsz, TT, 2 * KD), F32)],
        scratch_shapes=[pltpu.VMEM((bsz, HEADS, HV, HK), F32)],
        compiler_params=pltpu.CompilerParams(dimension_semantics=("arbitrary", "arbitrary")),
    )(pcat, pcat, pcat, la, s_all, do)


def gla_combine(dq2, dk2, dv2, dgate, dpa):
    bsz = dgate.shape[0]
    tm = CTX

    def body(dq_ref, dk_ref, dv_ref, dg_ref, dpa_ref, o_ref):
        t = pl.program_id(1)
        o_ref[:, 0:KD] = (dq_ref[0] + dq_ref[1]).astype(BF16)
        o_ref[:, KD:2 * KD] = (dk_ref[0] + dk_ref[1]).astype(BF16)
        o_ref[:, 2 * KD:2 * KD + VD] = (dv_ref[0] + dv_ref[1]).astype(BF16)
        o_ref[:, 2 * KD + VD:2 * KD + 2 * VD] = jnp.where(t > 0, dg_ref[...], 0).astype(BF16)
        o_ref[:, 2 * KD + 2 * VD:] = dpa_ref[...].astype(BF16)

    return pl.pallas_call(
        body, name="gla_combine", grid=(bsz, TT // tm),
        in_specs=[pl.BlockSpec((2, None, tm, KD), lambda b, t: (0, b, t, 0)),
                  pl.BlockSpec((2, None, tm, KD), lambda b, t: (0, b, t, 0)),
                  pl.BlockSpec((2, None, tm, VD), lambda b, t: (0, b, t, 0)),
                  pl.BlockSpec((None, tm, VD), lambda b, t: (b, jnp.maximum(t - 1, 0), 0)),
                  pl.BlockSpec((None, tm, 128), lambda b, t: (b, t, 0))],
        out_specs=pl.BlockSpec((None, tm, GLA_IN_PAD), lambda b, t: (b, t, 0)),
        out_shape=jax.ShapeDtypeStruct((bsz, TT, GLA_IN_PAD), BF16),
        compiler_params=pltpu.CompilerParams(dimension_semantics=("arbitrary", "arbitrary")),
    )(dq2, dk2, dv2, dgate, dpa)


def final_loss(h1, fo, gate, gain, tgt):
    bsz, t_len, _ = h1.shape
    tm = 256

    def body(h_ref, f_ref, gate_ref, gain_ref, tgt_ref, loss_ref, dh_ref, df_ref, dgate_ref, dgain_ref):
        b, t = pl.program_id(0), pl.program_id(1)
        target = tgt_ref[...]

        def core(h, fo_, gate_, gain_):
            e = _rms(h + gate_ * fo_) * gain_ - target
            return jnp.sum(0.5 * jnp.sum(e * e, axis=-1, keepdims=True) / D, axis=0, keepdims=True)

        loss, vjp = jax.vjp(core, h_ref[...], f_ref[...], gate_ref[...], gain_ref[...])
        dh, df, dgate, dgain = vjp(jnp.ones((1, 1), F32))
        dh_ref[...] = dh
        df_ref[...] = df.astype(BF16)
        first = jnp.logical_and(b == 0, t == 0)

        @pl.when(first)
        def _():
            loss_ref[...] = jnp.broadcast_to(loss, loss_ref.shape)
            dgain_ref[...] = dgain

        @pl.when(jnp.logical_not(first))
        def _():
            loss_ref[...] += jnp.broadcast_to(loss, loss_ref.shape)
            dgain_ref[...] += dgain

        @pl.when(t == 0)
        def _():
            dgate_ref[...] = dgate

        @pl.when(t > 0)
        def _():
            dgate_ref[...] += dgate

    tile = pl.BlockSpec((None, tm, D), lambda b, t: (b, t, 0))
    per_ex = pl.BlockSpec((None, 1, D), lambda b, t: (b, 0, 0))
    shared = pl.BlockSpec((1, D), lambda b, t: (0, 0))
    return pl.pallas_call(
        body, name="final_loss", grid=(bsz, t_len // tm),
        in_specs=[tile, tile, per_ex, shared, tile],
        out_specs=[pl.BlockSpec((8, 128), lambda b, t: (0, 0)), tile, tile, per_ex, shared],
        out_shape=[jax.ShapeDtypeStruct((8, 128), F32), jax.ShapeDtypeStruct(h1.shape, F32),
                   jax.ShapeDtypeStruct(h1.shape, BF16), jax.ShapeDtypeStruct((bsz, 1, D), F32),
                   jax.ShapeDtypeStruct((1, D), F32)],
        compiler_params=pltpu.CompilerParams(dimension_semantics=("arbitrary", "arbitrary")),
    )(h1, fo, gate, gain, tgt)


ADA_ROWS = 24
ADA_CTX_ROW = 16
ADA_COLS = 6 * D // N_DEV


def ada_fwd(cond, w, b):
    def body(c_ref, w_ref, b_ref, o_ref):
        s = silu(c_ref[...]).astype(BF16)
        o_ref[...] = jnp.dot(s, w_ref[...].astype(BF16), preferred_element_type=F32) + b_ref[...]

    return pl.pallas_call(
        body, name="ada_fwd", grid=(2,),
        in_specs=[pl.BlockSpec((ADA_ROWS, D), lambda i: (0, 0)), pl.BlockSpec((None, D, ADA_COLS), lambda i: (i, 0, 0)),
                  pl.BlockSpec((None, 1, ADA_COLS), lambda i: (i, 0, 0))],
        out_specs=pl.BlockSpec((None, ADA_ROWS, ADA_COLS), lambda i: (i, 0, 0)),
        out_shape=jax.ShapeDtypeStruct((2, ADA_ROWS, ADA_COLS), F32),
    )(cond, w, b)


def ada_bwd(cond, dm_mine, dm_full, w):
    def body(c_ref, dm_ref, dmf_ref, w_ref, gw_ref, gb_ref, cp_ref):
        i = pl.program_id(0)
        s = silu(c_ref[...]).astype(BF16)
        dm = dm_ref[...].astype(BF16)
        gw_ref[...] = _dot(s, dm, _TN)
        gb_ref[...] = jnp.sum(dmf_ref[...], axis=0, keepdims=True)

        @pl.when(i == 0)
        def _():
            cp_ref[...] = _dot(dm_ref[ADA_CTX_ROW:, :].astype(BF16), w_ref[...].astype(BF16), _NT)

    return pl.pallas_call(
        body, name="ada_bwd", grid=(2,),
        in_specs=[pl.BlockSpec((ADA_ROWS, D), lambda i: (0, 0)), pl.BlockSpec((None, ADA_ROWS, ADA_COLS), lambda i: (i, 0, 0)),
                  pl.BlockSpec((None, ADA_ROWS, 6 * D), lambda i: (i, 0, 0)), pl.BlockSpec((None, D, ADA_COLS), lambda i: (i, 0, 0))],
        out_specs=[pl.BlockSpec((None, D, ADA_COLS), lambda i: (i, 0, 0)), pl.BlockSpec((None, 1, 6 * D), lambda i: (i, 0, 0)),
                   pl.BlockSpec((ADA_ROWS - ADA_CTX_ROW, D), lambda i: (0, 0))],
        out_shape=[jax.ShapeDtypeStruct((2, D, ADA_COLS), F32), jax.ShapeDtypeStruct((2, 1, 6 * D), F32),
                   jax.ShapeDtypeStruct((ADA_ROWS - ADA_CTX_ROW, D), F32)],
        compiler_params=pltpu.CompilerParams(dimension_semantics=("arbitrary",)),
    )(cond, dm_mine, dm_full, w)


def cctx_grad(parts, c_ctx):
    def body(p_ref, c_ref, o_ref):
        tot = p_ref[0:1, :]
        for i in range(1, N_DEV):
            tot = tot + p_ref[i:i + 1, :]
        c = c_ref[...]
        sg = jax.nn.sigmoid(c)
        o_ref[...] = tot * sg * (1.0 + c * (1.0 - sg))

    return pl.pallas_call(body, name="cctx_grad", out_shape=jax.ShapeDtypeStruct((1, D), F32))(parts, c_ctx)


def _row_tile(r):
    for t in (512, 256, 128, 80, 64, 40, 32, 16, 8):
        if r % t == 0:
            return t
    return r


def _slot_sum(ref):
    tot = ref[0].astype(F32)
    for i in range(1, ref.shape[0]):
        tot = tot + ref[i].astype(F32)
    return tot


def sum_slots(name, x):
    s, r, c = x.shape
    tr = _row_tile(r)

    def body(x_ref, o_ref):
        o_ref[...] = _slot_sum(x_ref)

    return pl.pallas_call(
        body, name=name, grid=(r // tr,), in_specs=[pl.BlockSpec((s, tr, c), lambda i: (0, i, 0))],
        out_specs=pl.BlockSpec((tr, c), lambda i: (i, 0)), out_shape=jax.ShapeDtypeStruct((r, c), F32),
    )(x)


def adamw(name, w, g, m, v, layer=None):
    r, c = w.shape[-2:]
    tr = _row_tile(r)
    stacked = g.ndim == 3

    def body(w_ref, g_ref, m_ref, v_ref, go_ref, d_ref, mo_ref, vo_ref):
        gv = _slot_sum(g_ref) if stacked else g_ref[...]
        mn = B1 * m_ref[...] + (1.0 - B1) * gv
        vn = B2 * v_ref[...] + (1.0 - B2) * jnp.square(gv)
        m_hat = mn / (1.0 - B1 ** STEP)
        v_hat = vn / (1.0 - B2 ** STEP)
        go_ref[...] = gv
        d_ref[...] = -LR * (m_hat / (jnp.sqrt(v_hat) + AEPS) + WD * w_ref[...])
        mo_ref[...] = mn
        vo_ref[...] = vn

    tile = pl.BlockSpec((tr, c), lambda i: (i, 0))
    slab = tile if layer is None else pl.BlockSpec((None, tr, c), lambda i: (layer, i, 0))
    g_spec = pl.BlockSpec((g.shape[0], tr, c), lambda i: (0, i, 0)) if stacked else tile
    return pl.pallas_call(
        body, name=name, grid=(r // tr,), in_specs=[slab, g_spec, slab, slab], out_specs=[tile] * 4,
        out_shape=[jax.ShapeDtypeStruct((r, c), F32)] * 4,
    )(w, g, m, v)


def _place():
    return lax.axis_index("x"), lax.axis_index("y"), lax.axis_index("c")


def all_gather(name, x, in_vmem):
    r, c = x.shape
    space = pltpu.VMEM if in_vmem else pl.ANY

    def body(x_ref, out_ref, send_sems, recv_sems, local_sem):
        px, py, pc = _place()
        me, sibling = (px, py, pc), (px, py, 1 - pc)
        chips = [(1 - px, py), (px, 1 - py), (1 - px, 1 - py)]

        def rows(qx, qy, qc):
            return out_ref.at[pl.ds((4 * qx + 2 * qy + qc) * r, r), :]

        def copy(k, block, to, src=None):
            return pltpu.make_async_remote_copy(
                src_ref=rows(*block) if src is None else src, dst_ref=rows(*block),
                send_sem=send_sems.at[k], recv_sem=recv_sems.at[k], device_id=to, device_id_type=MESH)

        mine = pltpu.make_async_copy(x_ref, rows(*me), local_sem)
        mine.start()
        first = [copy(0, me, sibling, src=x_ref)]
        first += [copy(1 + j, me, (*chip, pc), src=x_ref) for j, chip in enumerate(chips)]
        for cp in first:
            cp.start()
        passed = [copy(4 + j, (*chip, pc), sibling) for j, chip in enumerate(chips)]
        for j, chip in enumerate(chips):
            copy(1 + j, (*chip, pc), me).wait_recv()
            passed[j].start()
        copy(0, sibling, me).wait_recv()
        for j, chip in enumerate(chips):
            copy(4 + j, (*chip, 1 - pc), me).wait_recv()
        for cp in first + passed:
            cp.wait_send()
        mine.wait()

    return pl.pallas_call(
        body, name=name, out_shape=jax.ShapeDtypeStruct((N_DEV * r, c), x.dtype),
        in_specs=[pl.BlockSpec(memory_space=space)], out_specs=pl.BlockSpec(memory_space=space),
        scratch_shapes=[pltpu.SemaphoreType.DMA((7,)), pltpu.SemaphoreType.DMA((7,)), pltpu.SemaphoreType.DMA],
    )(x)


_HBM =pl.BlockSpec(memory_space=pltpu.HBM)
_SEM = pl.BlockSpec(memory_space=pltpu.SEMAPHORE)
_EFFECT = pltpu.SideEffectType.DATAFLOW_SIDE_EFFECTING


def _peers():
    px, py, pc = _place()
    return [(1 - px if k & 4 else px, 1 - py if k & 2 else py, 1 - pc if k & 1 else pc) for k in range(1, N_DEV)]


def _slot(dev):
    return 4 * dev[0] + 2 * dev[1] + dev[2]


def _split_copies(src_refs, land_refs, send_sems, recv_sems, gather):
    me = _slot(_place())
    return [pltpu.make_async_remote_copy(
        src_ref=src if gather else src.at[_slot(peer)], dst_ref=land.at[me],
        send_sem=send_sems.at[a * (N_DEV - 1) + k], recv_sem=recv_sems.at[a * (N_DEV - 1) + k],
        device_id=peer, device_id_type=MESH)
        for a, (src, land) in enumerate(zip(src_refs, land_refs)) for k, peer in enumerate(_peers())]


def exchange_start(name, srcs, gather):
    n = len(srcs)
    lands = [pltpu.HBM((N_DEV,) + s.shape if gather else s.shape, s.dtype) for s in srcs]

    def body(*refs):
        send_sems, recv_sems = refs[2 * n:2 * n + 2]
        for cp in _split_copies(refs[:n], refs[n:2 * n], send_sems, recv_sems, gather):
            cp.start()
        refs[-1][...] = jnp.zeros_like(refs[-1])

    sems = pltpu.SemaphoreType.DMA((n * (N_DEV - 1),))
    res = pl.pallas_call(
        body, name=name,
        out_shape=(sems, sems, *[pltpu.HBM(s.shape, s.dtype) for s in srcs], *lands, jax.ShapeDtypeStruct((8, 128), F32)),
        in_specs=(_HBM,) * (2 * n), out_specs=(_SEM, _SEM) + (_HBM,) * (2 * n) + (pl.BlockSpec(memory_space=pltpu.VMEM),),
        input_output_aliases={i: 2 + i for i in range(2 * n)},
        compiler_params=pltpu.CompilerParams(has_side_effects=_EFFECT),
    )(*[pltpu.with_memory_space_constraint(s, pltpu.HBM) for s in srcs],
      *[pltpu.with_memory_space_constraint(lax.empty(ld.shape, ld.dtype), pltpu.HBM) for ld in lands])
    return res[0], res[1], list(res[2:2 + n]), list(res[2 + n:2 + 2 * n]), res[-1]


def exchange_wait(name, started, after, gather):
    send_sems, recv_sems, srcs, lands, _ = started
    n = len(srcs)

    def body(*refs):
        send_sems, recv_sems = refs[2 * n:2 * n + 2]
        for cp in _split_copies(refs[:n], refs[n:2 * n], send_sems, recv_sems, gather):
            cp.wait_send()
            cp.wait_recv()

    res = pl.pallas_call(
        body, name=name, out_shape=tuple(pltpu.HBM(a.shape, a.dtype) for a in srcs + lands),
        in_specs=(_HBM,) * (2 * n) + (_SEM, _SEM, pl.BlockSpec(memory_space=pl.ANY)), out_specs=(_HBM,) * (2 * n),
        input_output_aliases={i: i for i in range(2 * n)},
        compiler_params=pltpu.CompilerParams(has_side_effects=_EFFECT),
    )(*srcs, *lands, send_sems, recv_sems, after)
    return list(res[:n]), list(res[n:])


NCF = FFN_H // FFN_TC


def _size(shape):
    n = 1
    for s in shape:
        n *= s
    return n


def _padded_rows(n_elems, row_mult):
    return -(-n_elems // (D * row_mult)) * row_mult


def _pack_rows(arrs, dtype, row_mult):
    rows, offs, r0 = [], [], 0
    for a in arrs:
        flat = a.reshape(-1).astype(dtype)
        n = _padded_rows(flat.shape[0], row_mult)
        rows.append(jnp.pad(flat, (0, n * D - flat.shape[0])).reshape(n, D))
        offs.append(r0)
        r0 += n
    return jnp.concatenate(rows, 0), offs


def _unpack_rows(buf, offs, shapes):
    lead, out = buf.shape[:-2], []
    for o, shp in zip(offs, shapes):
        n = _size(shp)
        nr = -(-n // D)
        out.append(buf[..., o:o + nr, :].reshape(lead + (nr * D,))[..., :n].reshape(lead + tuple(shp)))
    return out


def _cols_from_shards(g):
    return g.transpose(1, 0, 2).reshape(g.shape[1], N_DEV * g.shape[2])


def _cols_to_shards(w):
    k, n = w.shape[0], w.shape[1] // N_DEV
    return w.reshape(k, N_DEV, n).transpose(1, 0, 2)


def _rows3(w):
    return [w[i:i + 1] for i in range(3)]


def f_mod1(xs, ps):
    return f_mod(xs, ps)[:1]


def kernel(x, c, ctx, c_ctx, ada_w, ada_b, norm_mix, norm_ffn, gla_w_in, gla_w_a2, gla_b_a, gla_head_norm, gla_w_out, sc_w_in, sc_conv_w, sc_w_out, ffn_w_up, ffn_conv_w, ffn_conv_b, ffn_w_down, final_norm, loss_target, m_c_ctx, m_ada_w, m_ada_b, m_norm_mix, m_norm_ffn, m_gla_w_in, m_gla_w_a2, m_gla_b_a, m_gla_head_norm, m_gla_w_out, m_sc_w_in, m_sc_conv_w, m_sc_w_out, m_ffn_w_up, m_ffn_conv_w, m_ffn_conv_b, m_ffn_w_down, m_final_norm, v_c_ctx, v_ada_w, v_ada_b, v_norm_mix, v_norm_ffn, v_gla_w_in, v_gla_w_a2, v_gla_b_a, v_gla_head_norm, v_gla_w_out, v_sc_w_in, v_sc_conv_w, v_sc_w_out, v_ffn_w_up, v_ffn_conv_w, v_ffn_conv_b, v_ffn_w_down, v_final_norm):
    names = ["c_ctx", "ada_w", "ada_b", "norm_mix", "norm_ffn", "gla_w_in", "gla_w_a2", "gla_b_a", "gla_head_norm",
             "gla_w_out", "sc_w_in", "sc_conv_w", "sc_w_out", "ffn_w_up", "ffn_conv_w", "ffn_conv_b", "ffn_w_down",
             "final_norm"]
    w_ = dict(zip(names, [c_ctx, ada_w, ada_b, norm_mix, norm_ffn, gla_w_in, gla_w_a2, gla_b_a, gla_head_norm, gla_w_out,
                          sc_w_in, sc_conv_w, sc_w_out, ffn_w_up, ffn_conv_w, ffn_conv_b, ffn_w_down, final_norm]))
    m_ = dict(zip(names, [m_c_ctx, m_ada_w, m_ada_b, m_norm_mix, m_norm_ffn, m_gla_w_in, m_gla_w_a2, m_gla_b_a,
                          m_gla_head_norm, m_gla_w_out, m_sc_w_in, m_sc_conv_w, m_sc_w_out, m_ffn_w_up, m_ffn_conv_w,
                          m_ffn_conv_b, m_ffn_w_down, m_final_norm]))
    v_ = dict(zip(names, [v_c_ctx, v_ada_w, v_ada_b, v_norm_mix, v_norm_ffn, v_gla_w_in, v_gla_w_a2, v_gla_b_a,
                          v_gla_head_norm, v_gla_w_out, v_sc_w_in, v_sc_conv_w, v_sc_w_out, v_ffn_w_up, v_ffn_conv_w,
                          v_ffn_conv_b, v_ffn_w_down, v_final_norm]))
    me = 4 * lax.axis_index("x") + 2 * lax.axis_index("y") + lax.axis_index("c")
    bsz = x.shape[0]
    tm = 256
    nt = SEQ // tm
    ctx_tiles = CTX // tm
    pe = functools.partial(P, per_example=True)

    groups = {"gla": [("gla_w_in", 0), ("gla_w_out", 0)], "ffn0": [("ffn_w_up", 0), ("ffn_w_down", 0)],
              "l1": [("sc_w_in", 0), ("sc_w_out", 0), ("ffn_w_up", 1), ("ffn_w_down", 1)]}
    ag_started = {}

    def start_gather(g, after=None):
        srcs = [w_[n][i].astype(BF16) for n, i in groups[g]]
        if after is not None:
            *srcs, _ = lax.optimization_barrier((*srcs, after))
        ag_started[g] = exchange_start(f"ag_{g}_start", srcs, True)
        return ag_started[g][4][0, 0]

    small_sharded = [c, gla_w_a2, gla_b_a, sc_conv_w, ffn_conv_w]
    pack0, offs0 = _pack_rows(small_sharded, F32, 8)
    g0 = all_gather("ag_small", pack0, True).reshape(N_DEV, pack0.shape[0], D)
    c_all, wa2_s, ba_s, scw_s, fcw_s = _unpack_rows(g0, offs0, [a.shape for a in small_sharded])
    w_a2 = wa2_s[:, 0].transpose(1, 2, 0, 3).reshape(2, RANK, KD)
    b_a = ba_s[:, 0].transpose(1, 0, 2).reshape(2, KD)
    sc_cw = scw_s[:, 0].transpose(1, 0, 2).reshape(3, D)
    ffn_cw = fcw_s.transpose(1, 2, 0, 3).reshape(2, 3, 2 * FFN_H)

    cond = jnp.concatenate([c_all.reshape(N_DEV * bsz, D), c_ctx[None], jnp.zeros((ADA_ROWS - N_DEV * bsz - 1, D), F32)], 0)
    b_mine = lax.dynamic_slice(ada_b, (0, me * ADA_COLS), (2, ADA_COLS)).reshape(2, 1, ADA_COLS)
    mod_part = ada_fwd(cond, ada_w, b_mine)
    mod = all_gather("ag_mod", mod_part.reshape(2 * ADA_ROWS, ADA_COLS), True)
    mod = mod.reshape(N_DEV, 2, ADA_ROWS, ADA_COLS).transpose(1, 2, 0, 3).reshape(2, ADA_ROWS, 6 * D)
    mods = lax.dynamic_slice(mod, (0, bsz * me, 0), (2, bsz, 6 * D))
    md = [[mods[i][:, k * D:(k + 1) * D].reshape(bsz, 1, D) for k in range(6)] for i in range(2)]
    mc = [mod[0, ADA_CTX_ROW, k * D:(k + 1) * D][None] for k in range(2)]

    norm_mix = norm_mix + sum(start_gather(g, after=mod) for g in groups)

    def gathered(g, after):
        mine, lands = exchange_wait(f"ag_{g}_wait", ag_started[g], after, True)
        return [lax.dynamic_update_index_in_dim(ld, mn, me, 0) for ld, mn in zip(lands, mine)]

    s_up, w_down = [None, None], [None, None]
    wd = jnp.zeros((128, 2 * KD), F32).at[:RANK, :KD].set(w_a2[0]).at[RANK:2 * RANK, KD:].set(w_a2[1])
    bd = b_a.reshape(1, 2 * KD)
    scw = _rows3(sc_cw)
    head_gain = gla_head_norm.reshape(1, HV)
    gains_mix = [norm_mix[i][None] for i in range(2)]
    gains_ffn = [norm_ffn[i][None] for i in range(2)]

    def tokens(a2d, t_len):
        return a2d.reshape(bsz, t_len, -1)

    def ffn_params(i):
        rows = [ffn_cw[i][t] for t in range(3)] + [ffn_conv_b[i]]
        return [P(a.reshape(2, FFN_H), w=FFN_TC, rows=True) for a in rows]

    def ffn_fwd(i, hn2):
        u = mm(f"ffn_up{i}", V(hn2, "tok"), V(s_up[i], "cols"), out="planes", out_dtype=BF16, planes_t=SEQ)
        act = rowwise(f"ffn_mid{i}", f_ffn_mid, [X(u, w=FFN_TC, planes=True)], ffn_params(i), tm=SEQ, nt=1, nc=NCF,
                      outs=[(FFN_TC, BF16, 1)])[0]
        return u, act, tokens(mm(f"ffn_down{i}", V(act, "tok"), V(w_down[i])), SEQ)

    def res_mod_fwd(name, h, y, ps):
        return rowwise(name, f_res_mod, [X(h), X(y)], ps, tm=tm, nt=nt, outs=[(D, F32, 1), (D, BF16, 1)])

    ps_in0 = [P(gains_mix[0]), pe(md[0][0]), pe(md[0][1])]
    ps_ctx = [P(gains_mix[0]), P(mc[0]), P(mc[1])]
    hn0 = rowwise("mod_in0", f_mod, [X(x)], ps_in0, tm=tm, nt=nt, outs=[(D, BF16, 1)])[0]
    hnc = rowwise("mod_ctx", f_mod, [X(ctx)], ps_ctx, tm=tm, nt=ctx_tiles, outs=[(D, BF16, 1)])[0]
    hcat = jnp.concatenate([hnc, hn0], axis=1)
    s_gin, s_gout = gathered("gla", hcat)
    w_gin = jnp.pad(_cols_from_shards(s_gin), ((0, 0), (0, GLA_IN_PAD - GLA_IN)))
    w_gout = s_gout.reshape(VD, D)
    pcat = tokens(mm("gla_in", V(hcat, "tok"), V(w_gin)), TT)
    pa_x = X(pcat, w=128, co=(GLA_IN_PAD - 128) // 128)
    la = rowwise("gla_decay", f_decay, [pa_x], [P(wd), P(bd)], tm=tm, nt=TT // tm, outs=[(2 * KD, F32, 1)])[0]
    o2, s_all = gla_fwd(pcat, la)
    post_xs = [X(o2, w=VD, co=0, ro=ctx_tiles, split=HEADS), X(o2, w=VD, co=1, ro=ctx_tiles, split=HEADS),
               X(pcat, w=VD, co=2, ro=ctx_tiles, split=HEADS)]
    yin0 = rowwise("gla_post", f_gla_post, post_xs, [P(head_gain)], tm=tm, nt=nt, outs=[(VD, BF16, HEADS)])[0]
    y0 = tokens(mm("gla_out", V(yin0, "tok"), V(w_gout)), SEQ)
    ps_mid0 = [pe(md[0][2]), P(gains_ffn[0]), pe(md[0][3]), pe(md[0][4])]
    h1_0, hn2_0 = res_mod_fwd("res_mod_mid0", x, y0, ps_mid0)
    s_up[0], s_down0 = gathered("ffn0", hn2_0)
    w_down[0] = s_down0.reshape(FFN_H, D)
    u0, act0, fo0 = ffn_fwd(0, hn2_0)
    ps_in1 = [pe(md[0][5]), P(gains_mix[1]), pe(md[1][0]), pe(md[1][1])]
    h2_0, hn1 = res_mod_fwd("res_mod_in1", h1_0, fo0, ps_in1)

    s_sin, s_sout, s_up[1], s_down1 = gathered("l1", hn1)
    w_sout, w_down[1] = s_sout.reshape(D, D), s_down1.reshape(FFN_H, D)
    p1 = tokens(mm("sc_in", V(hn1, "tok"), V(s_sin, "cols")), SEQ)
    sc_ps = [P(a) for a in scw]
    yin1 = rowwise("sc_mid", f_sc_mid, [X(p1, split=3)], sc_ps, tm=tm, nt=nt, outs=[(D, BF16, 1)])[0]
    y1 = tokens(mm("sc_out", V(yin1, "tok"), V(w_sout)), SEQ)
    ps_mid1 = [pe(md[1][2]), P(gains_ffn[1]), pe(md[1][3]), pe(md[1][4])]
    h1_1, hn2_1 = res_mod_fwd("res_mod_mid1", h2_0, y1, ps_mid1)
    u1, act1, fo1 = ffn_fwd(1, hn2_1)
    loss8, dh1_1, dfo1, dm5_1, g_final = final_loss(h1_1, fo1, md[1][5], final_norm[None], loss_target)

    def ffn_bwd(i, u, act, hn2, dfo):
        dact = tokens(mm(f"ffn_down_dx{i}", V(dfo, "tok"), V(w_down[i]), form="nt", out_dtype=BF16), SEQ)
        g_down = mm(f"ffn_down_dw{i}", V(act, "tok"), V(dfo, "tok"), form="tn", out_dtype=BF16)
        r = rowwise(f"ffn_mid_bwd{i}", f_ffn_mid, [X(u, w=FFN_TC, planes=True)], ffn_params(i), tm=SEQ, nt=1, nc=NCF,
                    douts=[X(dact, w=FFN_TC)], dx={0: BF16}, dp=[0, 1, 2, 3])
        du, g_cw, g_cb = r[0], jnp.stack([a.reshape(2 * FFN_H) for a in r[1:4]]), r[4].reshape(1, 2 * FFN_H)
        dhn2 = tokens(mm(f"ffn_up_dx{i}", V(du, "planes"), V(s_up[i], "cols"), form="nt", out_dtype=BF16), SEQ)
        g_up = mm(f"ffn_up_dw{i}", V(hn2, "tok"), V(du, "planes"), form="tn", out="cols", out_dtype=BF16)
        return dhn2, g_up, row_slots(g_down), g_cw, g_cb

    def res_mod_bwd(name, h, y, ps, dh1, dhn):
        return rowwise(name, f_res_mod, [X(h), X(y)], ps, tm=tm, nt=nt, douts=[X(dh1), X(dhn)],
                       dx={0: F32, 1: BF16}, dp=[0, 1, 2, 3])

    def row_slots(g):
        return g.reshape(N_DEV, -1, g.shape[-1])

    a2a_started = {}

    def send_grads(g, slots, after=None):
        if after is not None:
            *slots, _ = lax.optimization_barrier((*slots, after))
        a2a_started[g] = exchange_start(f"a2a_{g}_start", list(slots), False)
        return a2a_started[g][4][0, 0]

    def after_start(ps, tok):
        return [dict(ps[0], a=ps[0]["a"] + tok)] + ps[1:]

    dhn2_1, g_up1, g_down1, g_fcw1, g_fcb1 = ffn_bwd(1, u1, act1, hn2_1, dfo1)
    dh2_0, dy1, dm2_1, g_nffn1, dm3_1, dm4_1 = res_mod_bwd("res_mod_mid1_bwd", h2_0, y1, ps_mid1, dh1_1, dhn2_1)
    dyin1 = tokens(mm("sc_out_dx", V(dy1, "tok"), V(w_sout), form="nt", out_dtype=BF16), SEQ)
    g_sout = row_slots(mm("sc_out_dw", V(yin1, "tok"), V(dy1, "tok"), form="tn", out_dtype=BF16))
    r = rowwise("sc_mid_bwd", f_sc_mid, [X(p1, split=3)], sc_ps, tm=tm, nt=nt, douts=[X(dyin1)], dx={0: BF16}, dp=[0, 1, 2])
    dp1, g_scw = r[0], jnp.concatenate(r[1:4], 0)
    dhn1 = tokens(mm("sc_in_dx", V(dp1, "tok"), V(s_sin, "cols"), form="nt", out_dtype=BF16), SEQ)
    g_sin = mm("sc_in_dw", V(hn1, "tok"), V(dp1, "tok"), form="tn", out="cols", out_dtype=BF16)
    tok = send_grads("l1", [g_sin, g_sout, g_up1, g_down1])
    dh1_0, dfo0, dm5_0, g_nmix1, dm0_1, dm1_1 = res_mod_bwd("res_mod_in1_bwd", h1_0, fo0, after_start(ps_in1, tok), dh2_0, dhn1)

    dhn2_0, g_up0, g_down0, g_fcw0, g_fcb0 = ffn_bwd(0, u0, act0, hn2_0, dfo0)
    tok = send_grads("ffn0", [g_up0, g_down0])
    dx_res, dy0, dm2_0, g_nffn0, dm3_0, dm4_0 = res_mod_bwd("res_mod_mid0_bwd", x, y0, after_start(ps_mid0, tok), dh1_0, dhn2_0)
    dyin0 = tokens(mm("gla_out_dx", V(dy0, "tok"), V(w_gout), form="nt", out_dtype=BF16), SEQ)
    g_gout = row_slots(mm("gla_out_dw", V(yin0, "tok"), V(dy0, "tok"), form="tn", out_dtype=BF16))
    do, dgate, g_head = rowwise("gla_post_bwd", f_gla_post, post_xs, [P(head_gain)], tm=tm, nt=nt,
                                douts=[X(dyin0, split=HEADS)], dx={0: F32, 2: BF16}, dp=[0])
    dq2, dk2, dv2, dla = gla_bwd(pcat, la, s_all, do)
    dpa, g_wd, g_bd = rowwise("gla_decay_bwd", f_decay, [pa_x], [P(wd), P(bd)], tm=tm, nt=TT // tm, douts=[X(dla)],
                              dx={0: BF16}, dp=[0, 1])
    dpcat = gla_combine(dq2, dk2, dv2, dgate, dpa)
    dhcat = tokens(mm("gla_in_dx", V(dpcat, "tok"), V(w_gin), form="nt", out_dtype=BF16), TT)
    g_gin = _cols_to_shards(mm("gla_in_dw", V(hcat, "tok"), V(dpcat, "tok"), form="tn", out_dtype=BF16)[:, :GLA_IN])
    grad_x, g_nmix0, dm0_0, dm1_0 = rowwise("mod_in0_bwd", f_mod, [X(x)], ps_in0, tm=tm, nt=nt,
                                            douts=[X(dhcat, ro=ctx_tiles), X(dx_res)], dx={0: F32}, dp=[0, 1, 2])
    g_nmix0c, dmc0, dmc1 = rowwise("mod_ctx_bwd", f_mod1, [X(ctx)], ps_ctx, tm=tm, nt=ctx_tiles, douts=[X(dhcat)],
                                   dx={}, dp=[0, 1, 2])

    zero_row = jnp.zeros((1, 4 * D), F32)
    dmod = [jnp.concatenate([jnp.concatenate([a.reshape(bsz, D) for a in dms], 1), ctx_row], 0)
            for dms, ctx_row in (([dm0_0, dm1_0, dm2_0, dm3_0, dm4_0, dm5_0], jnp.concatenate([dmc0, dmc1, zero_row], 1)),
                                 ([dm0_1, dm1_1, dm2_1, dm3_1, dm4_1, dm5_1], jnp.zeros((1, 6 * D), F32)))]
    g_wa2 = jnp.stack([g_wd[:RANK, :KD], g_wd[RANK:2 * RANK, KD:]])
    small_grads = [jnp.stack(dmod), jnp.concatenate([g_nmix0 + g_nmix0c, g_nmix1], 0), jnp.concatenate([g_nffn0, g_nffn1], 0),
                   g_head, jnp.concatenate([g_fcb0, g_fcb1], 0), g_final, g_wa2, g_bd.reshape(2, KD), g_scw,
                   jnp.stack([g_fcw0, g_fcw1]), loss8[:1]]
    pack1, offs1 = _pack_rows(small_grads, F32, 8)
    g1 = all_gather("ag_grads", pack1, True).reshape(N_DEV, pack1.shape[0], D)
    dmod_all = _unpack_rows(g1, offs1[:1], [small_grads[0].shape])[0]
    tot = _unpack_rows(sum_slots("sum_small", g1), offs1, [a.shape for a in small_grads])
    loss = tot[10][0, 0]
    dm_rows = dmod_all[:, :, :bsz].transpose(1, 0, 2, 3).reshape(2, N_DEV * bsz, 6 * D)
    dm_full = jnp.concatenate([dm_rows, tot[0][:, bsz:], jnp.zeros((2, ADA_ROWS - N_DEV * bsz - 1, 6 * D), F32)], 1)
    dm_mine = lax.dynamic_slice(dm_full, (0, 0, me * ADA_COLS), (2, ADA_ROWS, ADA_COLS))
    g_ada_w, g_ada_b, cpart = ada_bwd(cond, dm_mine, dm_full, ada_w)
    cparts = all_gather("ag_cctx", cpart, True).reshape(N_DEV, ADA_ROWS - ADA_CTX_ROW, D)[:, 0]
    g_cctx = cctx_grad(cparts, c_ctx[None])[0]
    tok = send_grads("gla", [g_gin, g_gout], after=g_cctx)

    def my_cols(full, n):
        return lax.dynamic_slice_in_dim(full, me * n, n, axis=full.ndim - 1)

    grads = {
        "c_ctx": g_cctx, "ada_b": g_ada_b.reshape(2, 6 * D), "norm_mix": tot[1], "norm_ffn": tot[2],
        "gla_head_norm": tot[3], "ffn_conv_b": tot[4], "final_norm": tot[5].reshape(D),
        "gla_w_a2": my_cols(tot[6], KD // N_DEV)[None], "gla_b_a": my_cols(tot[7], KD // N_DEV)[None],
        "sc_conv_w": my_cols(tot[8], D // N_DEV)[None], "ffn_conv_w": my_cols(tot[9], 2 * FFN_H // N_DEV),
    }

    res_ada = adamw("adamw_ada", *[a.reshape(2 * D, ADA_COLS) for a in (ada_w, g_ada_w, m_ada_w, v_ada_w)])
    grads["c_ctx"] = g_cctx + tok
    big = ["gla_w_in", "gla_w_out", "sc_w_in", "sc_w_out", "ffn_w_up", "ffn_w_down"]
    small = [n for n in names if n not in big and n != "ada_w"]
    g_small = _pack_rows([grads[n] for n in small], F32, 8)[0]
    res_small = adamw("adamw_small", _pack_rows([w_[n] for n in small], F32, 8)[0], g_small,
                      _pack_rows([m_[n] for n in small], F32, 8)[0], _pack_rows([v_[n] for n in small], F32, 8)[0])
    offs_s = _pack_rows([w_[n] for n in small], F32, 8)[1]

    big_res, done = {}, [res_small[0][0, 0], res_ada[0][0, 0]]
    for g in ("l1", "ffn0", "gla"):
        sent, lands = exchange_wait(f"a2a_{g}_wait", a2a_started[g], jnp.stack(done), False)
        for (n, i), mine, land in zip(groups[g], sent, lands):
            land = lax.dynamic_update_index_in_dim(land, lax.dynamic_index_in_dim(mine, me, 0, keepdims=False), me, 0)
            big_res[(n, i)] = adamw(f"adamw_{n}{i}", w_[n], land, m_[n], v_[n], layer=i)
            done.append(big_res[(n, i)][0][0, 0])

    out = {}
    for kind, idx in (("grad", 0), ("delta", 1), ("new_m", 2), ("new_v", 3)):
        vals = {n: jnp.stack([big_res[(n, i)][idx] for i in range(w_[n].shape[0])]) for n in big}
        vals["ada_w"] = res_ada[idx].reshape(ada_w.shape)
        vals.update(zip(small, _unpack_rows(res_small[idx], offs_s, [w_[n].shape for n in small])))
        out[kind] = [vals[n] for n in names]
    return (loss, grad_x, *out["grad"], *out["delta"], *out["new_m"], *out["new_v"])
```

```python
import functools

import jax
import jax.numpy as jnp
from jax import lax
from jax.experimental import pallas as pl
from jax.experimental.pallas import tpu as pltpu

F32 = jnp.float32
BF16 = jnp.bfloat16

N_DEV = 8
D = 1024
SEQ = 2048
CTX = 256
TT = CTX + SEQ
GRID_W = 64
CHUNK = 64
HEADS = 4
HK = 128
HV = 256
KD = 512
VD = 1024
RANK = 16
TAU = 16.0
GLA_IN = 3104
GLA_IN_PAD = 3200
FFN_H = 2560
FFN_TC = 256
EPS = 1e-6
LR, B1, B2, AEPS, WD, STEP = 0.001, 0.9, 0.999, 1e-08, 0.01, 10
MESH = pl.DeviceIdType.MESH


def _blocks(n):
    return [n] + [t for t in range(n - n % 128, 0, -128) if n % t == 0 and t != n]


def V(arr, kind="flat"):
    if kind == "tok":
        return V(arr.reshape(-1, arr.shape[-1]))
    if kind == "flat":
        r, c = arr.shape
        return dict(a=arr, kind=kind, shape=(r, c), rows=_blocks(r), cols=_blocks(c))
    if kind == "planes":
        bsz, _, t, ch = arr.shape
        return dict(a=arr, kind=kind, shape=(bsz * t, 2 * ch), rows=_blocks(t), cols=[2 * ch] + _blocks(ch), t=t, ch=ch)
    _, r, n = arr.shape
    return dict(a=arr, kind=kind, shape=(r, N_DEV * n), rows=_blocks(r), cols=[8 * n, 4 * n, 2 * n], n=n)


def _view_spec(v, br, bc, idx):
    if v["kind"] == "flat":
        return pl.BlockSpec((br, bc), idx)
    if v["kind"] == "planes":
        nt = v["t"] // br
        if bc == 2 * v["ch"]:
            return pl.BlockSpec((None, 2, br, v["ch"]), lambda i, j, k: (idx(i, j, k)[0] // nt, 0, idx(i, j, k)[0] % nt, 0))
        nch = v["ch"] // bc

        def at(i, j, k):
            r, c = idx(i, j, k)
            return r // nt, c // nch, r % nt, c % nch
        return pl.BlockSpec((None, None, br, bc), at)
    return pl.BlockSpec((bc // v["n"], br, v["n"]), lambda i, j, k: (idx(i, j, k)[1], idx(i, j, k)[0], 0))


def _out_view(kind, rows, cols, dtype, planes_t=None):
    if kind == "flat":
        shape = (rows, cols)
    elif kind == "planes":
        shape = (rows // planes_t, 2, planes_t, cols // 2)
    else:
        shape = (N_DEV, rows, cols // N_DEV)
    return V(jax.ShapeDtypeStruct(shape, dtype), kind)


MM_VMEM_BUDGET = 40 * 2 ** 20
MM_VMEM_LIMIT = 56 * 2 ** 20
MM_MAX_TILE = 1536


def _mm_tiles(m, n, kk, ms, ns, ks, a_bytes, b_bytes, o_bytes):
    best = None
    for tk in ks:
        for tm in [t for t in ms if t <= MM_MAX_TILE]:
            for tn in [t for t in ns if t <= MM_MAX_TILE]:
                one_k = tk == kk
                need = 2 * (tm * tk * a_bytes + tk * tn * b_bytes + tm * tn * o_bytes) + (0 if one_k else tm * tn * 4)
                if need > MM_VMEM_BUDGET:
                    continue
                steps = (m // tm) * (n // tn) * (kk // tk)
                traffic = (m * kk * a_bytes * (1 if one_k else n // tn)
                           + kk * n * b_bytes * (1 if one_k and n == tn else m // tm) + m * n * o_bytes)
                fill = (tm * tk * a_bytes + tk * tn * b_bytes) / 2.5e12
                cost = max(2.0 * m * n * kk / (9e14 if one_k else 6.5e14), traffic / 2.5e12) + steps * 0.4e-6 + fill
                if best is None or cost < best[0]:
                    best = (cost, tm, tn, tk)
    return best[1:]


def mm(name, a, b, form="nn", out="flat", out_dtype=F32, planes_t=None):
    (m, kk) = a["shape"][::-1] if form == "tn" else a["shape"]
    n = b["shape"][0] if form == "nt" else b["shape"][1]
    assert (b["shape"][1] if form == "nt" else b["shape"][0]) == kk, (name, a["shape"], b["shape"])
    o = _out_view(out, m, n, out_dtype, planes_t)
    a_m, a_k = (a["cols"], a["rows"]) if form == "tn" else (a["rows"], a["cols"])
    b_k, b_n = (b["cols"], b["rows"]) if form == "nt" else (b["rows"], b["cols"])
    tm, tn, tk = _mm_tiles(m, n, kk, [t for t in a_m if t in o["rows"]], [t for t in b_n if t in o["cols"]],
                           [t for t in a_k if t in b_k], a["a"].dtype.itemsize, b["a"].dtype.itemsize,
                           jnp.dtype(out_dtype).itemsize)
    nk = kk // tk
    dn = (((0 if form == "tn" else 1,), (1 if form == "nt" else 0,)), ((), ()))

    def load(ref):
        if len(ref.shape) == 3:
            return jnp.concatenate([ref[p] for p in range(ref.shape[0])], axis=-1).astype(BF16)
        return ref[...].astype(BF16)

    def store(o_ref, val):
        val = val.astype(out_dtype)
        if len(o_ref.shape) == 3:
            w = o_ref.shape[-1]
            for p in range(o_ref.shape[0]):
                o_ref[p] = val[:, p * w:(p + 1) * w]
        else:
            o_ref[...] = val

    def body(a_ref, b_ref, o_ref, *acc):
        if nk == 1:
            store(o_ref, lax.dot_general(load(a_ref), load(b_ref), dn, preferred_element_type=F32))
            return
        k, acc_ref = pl.program_id(2), acc[0]

        @pl.when(k == 0)
        def _():
            acc_ref[...] = jnp.zeros_like(acc_ref)

        acc_ref[...] += lax.dot_general(load(a_ref), load(b_ref), dn, preferred_element_type=F32)

        @pl.when(k == nk - 1)
        def _():
            store(o_ref, acc_ref[...])

    if form == "tn":
        a_spec = _view_spec(a, tk, tm, lambda i, j, k: (k, i))
    else:
        a_spec = _view_spec(a, tm, tk, lambda i, j, k: (i, k))
    if form == "nt":
        b_spec = _view_spec(b, tn, tk, lambda i, j, k: (j, k))
    else:
        b_spec = _view_spec(b, tk, tn, lambda i, j, k: (k, j))
    return pl.pallas_call(
        body, name=name, grid=(m // tm, n // tn, nk),
        in_specs=[a_spec, b_spec], out_specs=_view_spec(o, tm, tn, lambda i, j, k: (i, j)), out_shape=o["a"],
        scratch_shapes=[pltpu.VMEM((tm, tn), F32)] if nk > 1 else [],
        compiler_params=pltpu.CompilerParams(dimension_semantics=("parallel", "parallel", "arbitrary"),
                                             vmem_limit_bytes=MM_VMEM_LIMIT),
    )(a["a"], b["a"])


def X(arr, w=None, co=0, ro=0, split=1, planes=False):
    return dict(a=arr, w=arr.shape[-1] if w is None else w, co=co, ro=ro, split=2 if planes else split,
                mode="planes" if planes else "cols")


def P(arr, per_example=False, w=None, split=1, rows=False):
    return dict(a=arr, e=per_example, w=arr.shape[-1] if w is None else w, split=arr.shape[-2] if rows else split,
                mode="rows" if rows else "cols")


def _pieces(ref, s):
    if s["mode"] == "planes":
        return [ref[0], ref[1]]
    if s["mode"] == "rows":
        return [ref[i:i + 1, :] for i in range(s["split"])]
    w = ref.shape[-1] // s["split"]
    return [ref[:, i * w:(i + 1) * w] for i in range(s["split"])]


def _store(ref, pieces, s, accumulate=False):
    w = ref.shape[-1] // len(pieces)
    for i, p in enumerate(pieces):
        at = (i,) if s["mode"] == "planes" else (slice(i, i + 1),) if s["mode"] == "rows" else (slice(None), slice(i * w, (i + 1) * w))
        if accumulate:
            ref[at] += p.astype(ref.dtype)
        else:
            ref[at] = p.astype(ref.dtype)


def rowwise(name, f, xs, ps, *, tm, nt, nc=1, outs=None, douts=None, dx=None, dp=None):
    bsz = xs[0]["a"].shape[0]
    fwd = douts is None
    nx, np_ = len(xs), len(ps)
    douts = [] if fwd else douts
    dx = {} if fwd else dx
    dp = [] if fwd else dp

    def x_spec(s):
        if s["mode"] == "planes":
            return pl.BlockSpec((None, 2, tm, s["w"]), lambda c, b, t, s=s: (b, 0, t + s["ro"], c + s["co"]))
        return pl.BlockSpec((None, tm, s["w"]), lambda c, b, t, s=s: (b, t + s["ro"], c + s["co"]))

    def x_out(s, dt):
        if s["mode"] == "planes":
            return (jax.ShapeDtypeStruct((bsz, 2, nt * tm, nc * s["w"]), dt),
                    pl.BlockSpec((None, 2, tm, s["w"]), lambda c, b, t: (b, 0, t, c)))
        return (jax.ShapeDtypeStruct((bsz, nt * tm, nc * s["w"]), dt), pl.BlockSpec((None, tm, s["w"]), lambda c, b, t: (b, t, c)))

    def p_spec(s):
        r = s["a"].shape[-2]
        if s["e"]:
            return pl.BlockSpec((None, r, s["w"]), lambda c, b, t: (b, 0, c))
        return pl.BlockSpec((r, s["w"]), lambda c, b, t: (0, c))

    in_specs = [x_spec(s) for s in xs] + [p_spec(s) for s in ps] + [x_spec(s) for s in douts]
    operands = [s["a"] for s in xs] + [s["a"] for s in ps] + [s["a"] for s in douts]
    if fwd:
        out_modes = [dict(mode="cols", split=sp) for (_, _, sp) in outs]
        out_shape = [jax.ShapeDtypeStruct((bsz, nt * tm, nc * w), dt) for (w, dt, _) in outs]
        out_specs = [pl.BlockSpec((None, tm, w), lambda c, b, t: (b, t, c)) for (w, _, _) in outs]
    else:
        dx_outs = [x_out(xs[i], dt) for i, dt in dx.items()]
        out_shape, out_specs = [o[0] for o in dx_outs], [o[1] for o in dx_outs]
        for j in dp:
            s = ps[j]
            r = s["a"].shape[-2]
            if s["e"]:
                out_shape.append(jax.ShapeDtypeStruct((bsz, r, nc * s["w"]), F32))
                out_specs.append(pl.BlockSpec((None, r, s["w"]), lambda c, b, t: (b, 0, c)))
            else:
                out_shape.append(jax.ShapeDtypeStruct((r, nc * s["w"]), F32))
                out_specs.append(pl.BlockSpec((r, s["w"]), lambda c, b, t: (0, c)))

    def body(*refs):
        x_refs, p_refs = refs[:nx], refs[nx:nx + np_]
        d_refs = refs[nx + np_:nx + np_ + len(douts)]
        o_refs = refs[nx + np_ + len(douts):]
        xv = [[p.astype(F32) for p in _pieces(r, s)] for r, s in zip(x_refs, xs)]
        pv = [[p.astype(F32) for p in _pieces(r, s)] for r, s in zip(p_refs, ps)]
        if fwd:
            for r, pieces, s in zip(o_refs, f(xv, pv), out_modes):
                _store(r, pieces, s)
            return
        _, vjp = jax.vjp(f, xv, pv)
        cot = [[p.astype(F32) for p in _pieces(r, s)] for r, s in zip(d_refs, douts)]
        dxv, dpv = vjp(cot)
        for r, i in zip(o_refs, dx):
            _store(r, dxv[i], xs[i])
        b, t = pl.program_id(1), pl.program_id(2)
        for r, j in zip(o_refs[len(dx):], dp):
            first = (t == 0) if ps[j]["e"] else jnp.logical_and(b == 0, t == 0)

            @pl.when(first)
            def _(r=r, j=j):
                _store(r, dpv[j], ps[j])

            @pl.when(jnp.logical_not(first))
            def _(r=r, j=j):
                _store(r, dpv[j], ps[j], accumulate=True)

    res = pl.pallas_call(
        body, name=name, grid=(nc, bsz, nt), in_specs=in_specs, out_specs=out_specs, out_shape=out_shape,
        compiler_params=pltpu.CompilerParams(dimension_semantics=("arbitrary", "arbitrary", "arbitrary")),
    )(*operands)
    return res


def _keep_rows(a, shift, keep):
    n = a.shape[0]
    t = lax.broadcasted_iota(jnp.int32, a.shape, 0)
    return jnp.where(keep(t, n), pltpu.roll(a, shift % n, 0), 0.0)


def _shift_pair(step, keep_prev, keep_next):
    @jax.custom_vjp
    def prev(a):
        return _keep_rows(a, step, keep_prev)

    @jax.custom_vjp
    def nxt(a):
        return _keep_rows(a, -step, keep_next)

    prev.defvjp(lambda a: (prev(a), None), lambda _, g: (nxt(g),))
    nxt.defvjp(lambda a: (nxt(a), None), lambda _, g: (prev(g),))
    return prev, nxt


prev_tok, next_tok = _shift_pair(1, lambda t, n: t % GRID_W != 0, lambda t, n: t % GRID_W != GRID_W - 1)
prev_row, next_row = _shift_pair(GRID_W, lambda t, n: t >= GRID_W, lambda t, n: t < n - GRID_W)


@jax.custom_vjp
def bdot(a, w):
    return jnp.dot(a.astype(BF16), w.astype(BF16), preferred_element_type=F32)


def _bdot_bwd(res, g):
    a, w = res
    gb = g.astype(BF16)
    da = lax.dot_general(gb, w.astype(BF16), (((1,), (1,)), ((), ())), preferred_element_type=F32)
    dw = lax.dot_general(a.astype(BF16), gb, (((0,), (0,)), ((), ())), preferred_element_type=F32)
    return da, dw


bdot.defvjp(lambda a, w: (bdot(a, w), (a, w)), _bdot_bwd)


@jax.custom_vjp
def log_sigmoid(z):
    return jnp.minimum(z, 0.0) - jnp.log(1.0 + jnp.exp(-jnp.abs(z)))


def _lsig_bwd(z, g):
    e = jnp.exp(-jnp.abs(z))
    return (g * jnp.where(z >= 0, e, 1.0) / (1.0 + e),)


log_sigmoid.defvjp(lambda z: (log_sigmoid(z), z), _lsig_bwd)


def silu(x):
    return x * jax.nn.sigmoid(x)


def _rms(x):
    return x * lax.rsqrt(jnp.mean(x * x, axis=-1, keepdims=True) + EPS)


def _mod(x, gain, shift, scale):
    return _rms(x) * gain * (1.0 + scale) + shift


def f_mod(xs, ps):
    ((h,),), ((gain,), (shift,), (scale,)) = xs, ps
    return [[_mod(h, gain, shift, scale)], [h]]


def f_res_mod(xs, ps):
    ((h,), (y,)), ((gate,), (gain,), (shift,), (scale,)) = xs, ps
    h1 = h + gate * y
    return [[h1], [_mod(h1, gain, shift, scale)]]


def f_ffn_mid(xs, ps):
    ((ua, ug),), ((w0a, w0g), (w1a, w1g), (w2a, w2g), (ba, bg)) = xs, ps
    a = w0a * prev_row(ua) + w1a * ua + w2a * next_row(ua) + ba
    g = w0g * prev_row(ug) + w1g * ug + w2g * next_row(ug) + bg
    return [[a * silu(g)]]


def f_sc_mid(xs, ps):
    ((bg, cg, v),), ((w0,), (w1,), (w2,)) = xs, ps
    z = cg * v
    return [[bg * (w0 * prev_tok(z) + w1 * z + w2 * next_tok(z))]]


def f_decay(xs, ps):
    ((a,),), ((wd,), (bd,)) = xs, ps
    return [[log_sigmoid(bdot(a, wd) + bd) / TAU]]


def f_gla_post(xs, ps):
    (of, ob, g), ((gain,),) = xs, ps
    return [[_rms(a + b) * gain * silu(c) for a, b, c in zip(of, ob, g)]]


NCH = TT // CHUNK
CTX_CH = CTX // CHUNK
_NT = (((1,), (1,)), ((), ()))
_TN = (((0,), (0,)), ((), ()))
_NN = (((1,), (0,)), ((), ()))


def _chunk_of(d, j):
    return jnp.where(d == 0, j, jnp.where(j < CTX_CH, CTX_CH - 1 - j, NCH + CTX_CH - 1 - j))


def _dot(a, b, dn):
    return lax.dot_general(a, b, dn, preferred_element_type=F32)


def _mask_dot(m, g):
    g0 = g.astype(BF16)
    r1 = g - g0.astype(F32)
    g1 = r1.astype(BF16)
    g2 = (r1 - g1.astype(F32)).astype(BF16)
    return _dot(m, g0, _NN) + _dot(m, g1, _NN) + _dot(m, g2, _NN)


def _causal(d):
    row = lax.broadcasted_iota(jnp.int32, (CHUNK, CHUNK), 0)
    col = lax.broadcasted_iota(jnp.int32, (CHUNK, CHUNK), 1)
    delta = jnp.where(d == 0, col - row, row - col)
    return delta <= 0, delta >= 0


def _gla_in_specs(bsz, rev):
    def blk(d, j):
        return _chunk_of(d, (NCH - 1 - j) if rev else j)

    return [
        pl.BlockSpec((bsz, CHUNK, KD), lambda d, j: (0, blk(d, j), 0)),
        pl.BlockSpec((bsz, CHUNK, KD), lambda d, j: (0, blk(d, j), 1)),
        pl.BlockSpec((bsz, CHUNK, VD), lambda d, j: (0, blk(d, j), 1)),
        pl.BlockSpec((bsz, CHUNK, KD), lambda d, j: (0, blk(d, j), d)),
    ], blk


def gla_fwd(pcat, la):
    bsz = pcat.shape[0]
    in_specs, blk = _gla_in_specs(bsz, False)

    def body(q_ref, k_ref, v_ref, la_ref, o_ref, s_ref, st):
        d, j = pl.program_id(0), pl.program_id(1)

        @pl.when(j == 0)
        def _():
            st[...] = jnp.zeros_like(st)

        s_ref[...] = st[...]
        causal, _ = _causal(d)
        mf = causal.astype(BF16)
        for e, h in [(e, h) for e in range(bsz) for h in range(HEADS)]:
            ks_, vs_ = slice(h * HK, (h + 1) * HK), slice(h * HV, (h + 1) * HV)
            q, k, v, g = q_ref[e, :, ks_] * (HK ** -0.5), k_ref[e, :, ks_], v_ref[e, :, vs_].astype(BF16), la_ref[e, :, ks_]
            b = _mask_dot(mf, g)
            bl = jnp.sum(g, axis=0, keepdims=True)
            qs = (q * jnp.exp(b)).astype(BF16)
            ks = (k * jnp.exp(-b)).astype(BF16)
            kd = (k * jnp.exp(bl - b)).astype(BF16)
            s = st[e, h]
            att = jnp.where(causal, _dot(qs, ks, _NT), 0.0).astype(BF16)
            o_ref[e, :, vs_] = _dot(qs, s.astype(BF16), _NT) + _dot(att, v, _NN)
            st[e, h] = jnp.exp(bl) * s + _dot(v, kd, _TN)

    return pl.pallas_call(
        body, name="gla_fwd", grid=(2, NCH), in_specs=in_specs,
        out_specs=[pl.BlockSpec((bsz, CHUNK, VD), lambda d, j: (0, blk(d, j), d)),
                   pl.BlockSpec((bsz, None, None, HEADS, HV, HK), lambda d, j: (0, d, j, 0, 0, 0))],
        out_shape=[jax.ShapeDtypeStruct((bsz, TT, 2 * VD), F32), jax.ShapeDtypeStruct((bsz, 2, NCH, HEADS, HV, HK), F32)],
        scratch_shapes=[pltpu.VMEM((bsz, HEADS, HV, HK), F32)],
        compiler_params=pltpu.CompilerParams(dimension_semantics=("arbitrary", "arbitrary")),
    )(pcat, pcat, pcat, la)


def gla_bwd(pcat, la, s_all, do):
    bsz = pcat.shape[0]
    in_specs, blk = _gla_in_specs(bsz, True)
    in_specs += [
        pl.BlockSpec((bsz, None, None, HEADS, HV, HK), lambda d, j: (0, d, NCH - 1 - j, 0, 0, 0)),
        pl.BlockSpec((bsz, CHUNK, VD), lambda d, j: (0, jnp.maximum(blk(d, j) - CTX_CH, 0), 0)),
    ]

    def body(q_ref, k_ref, v_ref, la_ref, s_ref, do_ref, dq_ref, dk_ref, dv_ref, dla_ref, dst):
        d, j = pl.program_id(0), pl.program_id(1)

        @pl.when(j == 0)
        def _():
            dst[...] = jnp.zeros_like(dst)

        latent = blk(d, j) >= CTX_CH
        causal, causal_t = _causal(d)
        mt = causal_t.astype(BF16)
        mf = causal.astype(BF16)
        scale = HK ** -0.5
        for e, h in [(e, h) for e in range(bsz) for h in range(HEADS)]:
            ks_, vs_ = slice(h * HK, (h + 1) * HK), slice(h * HV, (h + 1) * HV)
            q, k, v, g = q_ref[e, :, ks_] * scale, k_ref[e, :, ks_], v_ref[e, :, vs_].astype(BF16), la_ref[e, :, ks_]
            b = _mask_dot(mf, g)
            bl = jnp.sum(g, axis=0, keepdims=True)
            ex, ei, ed, el = jnp.exp(b), jnp.exp(-b), jnp.exp(bl - b), jnp.exp(bl)
            qs, ks, kd = q * ex, k * ei, k * ed
            qsb, ksb, kdb = qs.astype(BF16), ks.astype(BF16), kd.astype(BF16)
            s, ds1 = s_ref[e, h], dst[e, h]
            sb, ds1b = s.astype(BF16), ds1.astype(BF16)
            dob = jnp.where(latent, do_ref[e, :, vs_], 0.0).astype(BF16)
            att = jnp.where(causal, _dot(qsb, ksb, _NT), 0.0).astype(BF16)
            datt = jnp.where(causal, _dot(dob, v, _NT), 0.0).astype(BF16)
            dqs = _dot(dob, sb, _NN) + _dot(datt, ksb, _NN)
            dks = _dot(datt, qsb, _TN)
            dv_ref[e, :, vs_] = _dot(att, dob, _TN) + _dot(kdb, ds1b, _NT)
            dkd = _dot(v, ds1b, _NN)
            dst[e, h] = _dot(dob, qsb, _TN) + el * ds1
            del_ = jnp.sum(s * ds1, axis=0, keepdims=True)
            dq_ref[e, :, ks_] = dqs * ex * scale
            dk_ref[e, :, ks_] = dks * ei + dkd * ed
            db = dqs * qs - dks * ks - dkd * kd
            dbl = jnp.sum(dkd * kd, axis=0, keepdims=True) + del_ * el
            dla_ref[e, :, ks_] = _mask_dot(mt, db) + dbl

    return pl.pallas_call(
        body, name="gla_bwd", grid=(2, NCH), in_specs=in_specs,
        out_specs=[pl.BlockSpec((None, bsz, CHUNK, KD), lambda d, j: (d, 0, blk(d, j), 0)),
                   pl.BlockSpec((None, bsz, CHUNK, KD), lambda d, j: (d, 0, blk(d, j), 0)),
                   pl.BlockSpec((None, bsz, CHUNK, VD), lambda d, j: (d, 0, blk(d, j), 0)),
                   pl.BlockSpec((bsz, CHUNK, KD), lambda d, j: (0, blk(d, j), d))],
        out_shape=[jax.ShapeDtypeStruct((2, bsz, TT, KD), F32), jax.ShapeDtypeStruct((2, bsz, TT, KD), F32),
                   jax.ShapeDtypeStruct((2, bsz, TT, VD), F32), jax.ShapeDtypeStruct((bsz, TT, 2 * KD), F32)],
        scratch_shapes=[pltpu.VMEM((bsz, HEADS, HV, HK), F32)],
        compiler_params=pltpu.CompilerParams(dimension_semantics=("arbitrary", "arbitrary")),
    )(pcat, pcat, pcat, la, s_all, do)


def gla_combine(dq2, dk2, dv2, dgate, dpa):
    bsz = dgate.shape[0]
    tm = CTX

    def body(dq_ref, dk_ref, dv_ref, dg_ref, dpa_ref, o_ref):
        t = pl.program_id(1)
        o_ref[:, 0:KD] = (dq_ref[0] + dq_ref[1]).astype(BF16)
        o_ref[:, KD:2 * KD] = (dk_ref[0] + dk_ref[1]).astype(BF16)
        o_ref[:, 2 * KD:2 * KD + VD] = (dv_ref[0] + dv_ref[1]).astype(BF16)
        o_ref[:, 2 * KD + VD:2 * KD + 2 * VD] = jnp.where(t > 0, dg_ref[...], 0).astype(BF16)
        o_ref[:, 2 * KD + 2 * VD:] = dpa_ref[...].astype(BF16)

    return pl.pallas_call(
        body, name="gla_combine", grid=(bsz, TT // tm),
        in_specs=[pl.BlockSpec((2, None, tm, KD), lambda b, t: (0, b, t, 0)),
                  pl.BlockSpec((2, None, tm, KD), lambda b, t: (0, b, t, 0)),
                  pl.BlockSpec((2, None, tm, VD), lambda b, t: (0, b, t, 0)),
                  pl.BlockSpec((None, tm, VD), lambda b, t: (b, jnp.maximum(t - 1, 0), 0)),
                  pl.BlockSpec((None, tm, 128), lambda b, t: (b, t, 0))],
        out_specs=pl.BlockSpec((None, tm, GLA_IN_PAD), lambda b, t: (b, t, 0)),
        out_shape=jax.ShapeDtypeStruct((bsz, TT, GLA_IN_PAD), BF16),
        compiler_params=pltpu.CompilerParams(dimension_semantics=("arbitrary", "arbitrary")),
    )(dq2, dk2, dv2, dgate, dpa)


def final_loss(h1, fo, gate, gain, tgt):
    bsz, t_len, _ = h1.shape
    tm = 256

    def body(h_ref, f_ref, gate_ref, gain_ref, tgt_ref, loss_ref, dh_ref, df_ref, dgate_ref, dgain_ref):
        b, t = pl.program_id(0), pl.program_id(1)
        target = tgt_ref[...]

        def core(h, fo_, gate_, gain_):
            e = _rms(h + gate_ * fo_) * gain_ - target
            return jnp.sum(0.5 * jnp.sum(e * e, axis=-1, keepdims=True) / D, axis=0, keepdims=True)

        loss, vjp = jax.vjp(core, h_ref[...], f_ref[...], gate_ref[...], gain_ref[...])
        dh, df, dgate, dgain = vjp(jnp.ones((1, 1), F32))
        dh_ref[...] = dh
        df_ref[...] = df.astype(BF16)
        first = jnp.logical_and(b == 0, t == 0)

        @pl.when(first)
        def _():
            loss_ref[...] = jnp.broadcast_to(loss, loss_ref.shape)
            dgain_ref[...] = dgain

        @pl.when(jnp.logical_not(first))
        def _():
            loss_ref[...] += jnp.broadcast_to(loss, loss_ref.shape)
            dgain_ref[...] += dgain

        @pl.when(t == 0)
        def _():
            dgate_ref[...] = dgate

        @pl.when(t > 0)
        def _():
            dgate_ref[...] += dgate

    tile = pl.BlockSpec((None, tm, D), lambda b, t: (b, t, 0))
    per_ex = pl.BlockSpec((None, 1, D), lambda b, t: (b, 0, 0))
    shared = pl.BlockSpec((1, D), lambda b, t: (0, 0))
    return pl.pallas_call(
        body, name="final_loss", grid=(bsz, t_len // tm),
        in_specs=[tile, tile, per_ex, shared, tile],
        out_specs=[pl.BlockSpec((8, 128), lambda b, t: (0, 0)), tile, tile, per_ex, shared],
        out_shape=[jax.ShapeDtypeStruct((8, 128), F32), jax.ShapeDtypeStruct(h1.shape, F32),
                   jax.ShapeDtypeStruct(h1.shape, BF16), jax.ShapeDtypeStruct((bsz, 1, D), F32),
                   jax.ShapeDtypeStruct((1, D), F32)],
        compiler_params=pltpu.CompilerParams(dimension_semantics=("arbitrary", "arbitrary")),
    )(h1, fo, gate, gain, tgt)


ADA_ROWS = 24
ADA_CTX_ROW = 16
ADA_COLS = 6 * D // N_DEV


def ada_fwd(cond, w, b):
    def body(c_ref, w_ref, b_ref, o_ref):
        s = silu(c_ref[...]).astype(BF16)
        o_ref[...] = jnp.dot(s, w_ref[...].astype(BF16), preferred_element_type=F32) + b_ref[...]

    return pl.pallas_call(
        body, name="ada_fwd", grid=(2,),
        in_specs=[pl.BlockSpec((ADA_ROWS, D), lambda i: (0, 0)), pl.BlockSpec((None, D, ADA_COLS), lambda i: (i, 0, 0)),
                  pl.BlockSpec((None, 1, ADA_COLS), lambda i: (i, 0, 0))],
        out_specs=pl.BlockSpec((None, ADA_ROWS, ADA_COLS), lambda i: (i, 0, 0)),
        out_shape=jax.ShapeDtypeStruct((2, ADA_ROWS, ADA_COLS), F32),
    )(cond, w, b)


def ada_bwd(cond, dm_mine, dm_full, w):
    def body(c_ref, dm_ref, dmf_ref, w_ref, gw_ref, gb_ref, cp_ref):
        i = pl.program_id(0)
        s = silu(c_ref[...]).astype(BF16)
        dm = dm_ref[...].astype(BF16)
        gw_ref[...] = _dot(s, dm, _TN)
        gb_ref[...] = jnp.sum(dmf_ref[...], axis=0, keepdims=True)

        @pl.when(i == 0)
        def _():
            cp_ref[...] = _dot(dm_ref[ADA_CTX_ROW:, :].astype(BF16), w_ref[...].astype(BF16), _NT)

    return pl.pallas_call(
        body, name="ada_bwd", grid=(2,),
        in_specs=[pl.BlockSpec((ADA_ROWS, D), lambda i: (0, 0)), pl.BlockSpec((None, ADA_ROWS, ADA_COLS), lambda i: (i, 0, 0)),
                  pl.BlockSpec((None, ADA_ROWS, 6 * D), lambda i: (i, 0, 0)), pl.BlockSpec((None, D, ADA_COLS), lambda i: (i, 0, 0))],
        out_specs=[pl.BlockSpec((None, D, ADA_COLS), lambda i: (i, 0, 0)), pl.BlockSpec((None, 1, 6 * D), lambda i: (i, 0, 0)),
                   pl.BlockSpec((ADA_ROWS - ADA_CTX_ROW, D), lambda i: (0, 0))],
        out_shape=[jax.ShapeDtypeStruct((2, D, ADA_COLS), F32), jax.ShapeDtypeStruct((2, 1, 6 * D), F32),
                   jax.ShapeDtypeStruct((ADA_ROWS - ADA_CTX_ROW, D), F32)],
        compiler_params=pltpu.CompilerParams(dimension_semantics=("arbitrary",)),
    )(cond, dm_mine, dm_full, w)


def cctx_grad(parts, c_ctx):
    def body(p_ref, c_ref, o_ref):
        tot = p_ref[0:1, :]
        for i in range(1, N_DEV):
            tot = tot + p_ref[i:i + 1, :]
        c = c_ref[...]
        sg = jax.nn.sigmoid(c)
        o_ref[...] = tot * sg * (1.0 + c * (1.0 - sg))

    return pl.pallas_call(body, name="cctx_grad", out_shape=jax.ShapeDtypeStruct((1, D), F32))(parts, c_ctx)


def _row_tile(r):
    for t in (512, 256, 128, 80, 64, 40, 32, 16, 8):
        if r % t == 0:
            return t
    return r


def _slot_sum(ref):
    tot = ref[0].astype(F32)
    for i in range(1, ref.shape[0]):
        tot = tot + ref[i].astype(F32)
    return tot


def sum_slots(name, x):
    s, r, c = x.shape
    tr = _row_tile(r)

    def body(x_ref, o_ref):
        o_ref[...] = _slot_sum(x_ref)

    return pl.pallas_call(
        body, name=name, grid=(r // tr,), in_specs=[pl.BlockSpec((s, tr, c), lambda i: (0, i, 0))],
        out_specs=pl.BlockSpec((tr, c), lambda i: (i, 0)), out_shape=jax.ShapeDtypeStruct((r, c), F32),
    )(x)


def adamw(name, w, g, m, v, layer=None):
    r, c = w.shape[-2:]
    tr = _row_tile(r)
    stacked = g.ndim == 3

    def body(w_ref, g_ref, m_ref, v_ref, go_ref, d_ref, mo_ref, vo_ref):
        gv = _slot_sum(g_ref) if stacked else g_ref[...]
        mn = B1 * m_ref[...] + (1.0 - B1) * gv
        vn = B2 * v_ref[...] + (1.0 - B2) * jnp.square(gv)
        m_hat = mn / (1.0 - B1 ** STEP)
        v_hat = vn / (1.0 - B2 ** STEP)
        go_ref[...] = gv
        d_ref[...] = -LR * (m_hat / (jnp.sqrt(v_hat) + AEPS) + WD * w_ref[...])
        mo_ref[...] = mn
        vo_ref[...] = vn

    tile = pl.BlockSpec((tr, c), lambda i: (i, 0))
    slab = tile if layer is None else pl.BlockSpec((None, tr, c), lambda i: (layer, i, 0))
    g_spec = pl.BlockSpec((g.shape[0], tr, c), lambda i: (0, i, 0)) if stacked else tile
    return pl.pallas_call(
        body, name=name, grid=(r // tr,), in_specs=[slab, g_spec, slab, slab], out_specs=[tile] * 4,
        out_shape=[jax.ShapeDtypeStruct((r, c), F32)] * 4,
    )(w, g, m, v)


def _place():
    return lax.axis_index("x"), lax.axis_index("y"), lax.axis_index("c")


def all_gather(name, x, in_vmem):
    r, c = x.shape
    space = pltpu.VMEM if in_vmem else pl.ANY

    def body(x_ref, out_ref, send_sems, recv_sems, local_sem):
        px, py, pc = _place()
        me, sibling = (px, py, pc), (px, py, 1 - pc)
        chips = [(1 - px, py), (px, 1 - py), (1 - px, 1 - py)]

        def rows(qx, qy, qc):
            return out_ref.at[pl.ds((4 * qx + 2 * qy + qc) * r, r), :]

        def copy(k, block, to, src=None):
            return pltpu.make_async_remote_copy(
                src_ref=rows(*block) if src is None else src, dst_ref=rows(*block),
                send_sem=send_sems.at[k], recv_sem=recv_sems.at[k], device_id=to, device_id_type=MESH)

        mine = pltpu.make_async_copy(x_ref, rows(*me), local_sem)
        mine.start()
        first = [copy(0, me, sibling, src=x_ref)]
        first += [copy(1 + j, me, (*chip, pc), src=x_ref) for j, chip in enumerate(chips)]
        for cp in first:
            cp.start()
        passed = [copy(4 + j, (*chip, pc), sibling) for j, chip in enumerate(chips)]
        for j, chip in enumerate(chips):
            copy(1 + j, (*chip, pc), me).wait_recv()
            passed[j].start()
        copy(0, sibling, me).wait_recv()
        for j, chip in enumerate(chips):
            copy(4 + j, (*chip, 1 - pc), me).wait_recv()
        for cp in first + passed:
            cp.wait_send()
        mine.wait()

    return pl.pallas_call(
        body, name=name, out_shape=jax.ShapeDtypeStruct((N_DEV * r, c), x.dtype),
        in_specs=[pl.BlockSpec(memory_space=space)], out_specs=pl.BlockSpec(memory_space=space),
        scratch_shapes=[pltpu.SemaphoreType.DMA((7,)), pltpu.SemaphoreType.DMA((7,)), pltpu.SemaphoreType.DMA],
    )(x)


_HBM =pl.BlockSpec(memory_space=pltpu.HBM)
_SEM = pl.BlockSpec(memory_space=pltpu.SEMAPHORE)
_EFFECT = pltpu.SideEffectType.DATAFLOW_SIDE_EFFECTING


def _peers():
    px, py, pc = _place()
    return [(1 - px if k & 4 else px, 1 - py if k & 2 else py, 1 - pc if k & 1 else pc) for k in range(1, N_DEV)]


def _slot(dev):
    return 4 * dev[0] + 2 * dev[1] + dev[2]


def _split_copies(src_refs, land_refs, send_sems, recv_sems, gather):
    me = _slot(_place())
    return [pltpu.make_async_remote_copy(
        src_ref=src if gather else src.at[_slot(peer)], dst_ref=land.at[me],
        send_sem=send_sems.at[a * (N_DEV - 1) + k], recv_sem=recv_sems.at[a * (N_DEV - 1) + k],
        device_id=peer, device_id_type=MESH)
        for a, (src, land) in enumerate(zip(src_refs, land_refs)) for k, peer in enumerate(_peers())]


def exchange_start(name, srcs, gather):
    n = len(srcs)
    lands = [pltpu.HBM((N_DEV,) + s.shape if gather else s.shape, s.dtype) for s in srcs]

    def body(*refs):
        send_sems, recv_sems = refs[2 * n:2 * n + 2]
        for cp in _split_copies(refs[:n], refs[n:2 * n], send_sems, recv_sems, gather):
            cp.start()
        refs[-1][...] = jnp.zeros_like(refs[-1])

    sems = pltpu.SemaphoreType.DMA((n * (N_DEV - 1),))
    res = pl.pallas_call(
        body, name=name,
        out_shape=(sems, sems, *[pltpu.HBM(s.shape, s.dtype) for s in srcs], *lands, jax.ShapeDtypeStruct((8, 128), F32)),
        in_specs=(_HBM,) * (2 * n), out_specs=(_SEM, _SEM) + (_HBM,) * (2 * n) + (pl.BlockSpec(memory_space=pltpu.VMEM),),
        input_output_aliases={i: 2 + i for i in range(2 * n)},
        compiler_params=pltpu.CompilerParams(has_side_effects=_EFFECT),
    )(*[pltpu.with_memory_space_constraint(s, pltpu.HBM) for s in srcs],
      *[pltpu.with_memory_space_constraint(lax.empty(ld.shape, ld.dtype), pltpu.HBM) for ld in lands])
    return res[0], res[1], list(res[2:2 + n]), list(res[2 + n:2 + 2 * n]), res[-1]


def exchange_wait(name, started, after, gather):
    send_sems, recv_sems, srcs, lands, _ = started
    n = len(srcs)

    def body(*refs):
        send_sems, recv_sems = refs[2 * n:2 * n + 2]
        for cp in _split_copies(refs[:n], refs[n:2 * n], send_sems, recv_sems, gather):
            cp.wait_send()
            cp.wait_recv()

    res = pl.pallas_call(
        body, name=name, out_shape=tuple(pltpu.HBM(a.shape, a.dtype) for a in srcs + lands),
        in_specs=(_HBM,) * (2 * n) + (_SEM, _SEM, pl.BlockSpec(memory_space=pl.ANY)), out_specs=(_HBM,) * (2 * n),
        input_output_aliases={i: i for i in range(2 * n)},
        compiler_params=pltpu.CompilerParams(has_side_effects=_EFFECT),
    )(*srcs, *lands, send_sems, recv_sems, after)
    return list(res[:n]), list(res[n:])


NCF = FFN_H // FFN_TC


def _size(shape):
    n = 1
    for s in shape:
        n *= s
    return n


def _padded_rows(n_elems, row_mult):
    return -(-n_elems // (D * row_mult)) * row_mult


def _pack_rows(arrs, dtype, row_mult):
    rows, offs, r0 = [], [], 0
    for a in arrs:
        flat = a.reshape(-1).astype(dtype)
        n = _padded_rows(flat.shape[0], row_mult)
        rows.append(jnp.pad(flat, (0, n * D - flat.shape[0])).reshape(n, D))
        offs.append(r0)
        r0 += n
    return jnp.concatenate(rows, 0), offs


def _unpack_rows(buf, offs, shapes):
    lead, out = buf.shape[:-2], []
    for o, shp in zip(offs, shapes):
        n = _size(shp)
        nr = -(-n // D)
        out.append(buf[..., o:o + nr, :].reshape(lead + (nr * D,))[..., :n].reshape(lead + tuple(shp)))
    return out


def _cols_from_shards(g):
    return g.transpose(1, 0, 2).reshape(g.shape[1], N_DEV * g.shape[2])


def _cols_to_shards(w):
    k, n = w.shape[0], w.shape[1] // N_DEV
    return w.reshape(k, N_DEV, n).transpose(1, 0, 2)


def _rows3(w):
    return [w[i:i + 1] for i in range(3)]


def f_mod1(xs, ps):
    return f_mod(xs, ps)[:1]


def kernel(x, c, ctx, c_ctx, ada_w, ada_b, norm_mix, norm_ffn, gla_w_in, gla_w_a2, gla_b_a, gla_head_norm, gla_w_out, sc_w_in, sc_conv_w, sc_w_out, ffn_w_up, ffn_conv_w, ffn_conv_b, ffn_w_down, final_norm, loss_target, m_c_ctx, m_ada_w, m_ada_b, m_norm_mix, m_norm_ffn, m_gla_w_in, m_gla_w_a2, m_gla_b_a, m_gla_head_norm, m_gla_w_out, m_sc_w_in, m_sc_conv_w, m_sc_w_out, m_ffn_w_up, m_ffn_conv_w, m_ffn_conv_b, m_ffn_w_down, m_final_norm, v_c_ctx, v_ada_w, v_ada_b, v_norm_mix, v_norm_ffn, v_gla_w_in, v_gla_w_a2, v_gla_b_a, v_gla_head_norm, v_gla_w_out, v_sc_w_in, v_sc_conv_w, v_sc_w_out, v_ffn_w_up, v_ffn_conv_w, v_ffn_conv_b, v_ffn_w_down, v_final_norm):
    names = ["c_ctx", "ada_w", "ada_b", "norm_mix", "norm_ffn", "gla_w_in", "gla_w_a2", "gla_b_a", "gla_head_norm",
             "gla_w_out", "sc_w_in", "sc_conv_w", "sc_w_out", "ffn_w_up", "ffn_conv_w", "ffn_conv_b", "ffn_w_down",
             "final_norm"]
    w_ = dict(zip(names, [c_ctx, ada_w, ada_b, norm_mix, norm_ffn, gla_w_in, gla_w_a2, gla_b_a, gla_head_norm, gla_w_out,
                          sc_w_in, sc_conv_w, sc_w_out, ffn_w_up, ffn_conv_w, ffn_conv_b, ffn_w_down, final_norm]))
    m_ = dict(zip(names, [m_c_ctx, m_ada_w, m_ada_b, m_norm_mix, m_norm_ffn, m_gla_w_in, m_gla_w_a2, m_gla_b_a,
                          m_gla_head_norm, m_gla_w_out, m_sc_w_in, m_sc_conv_w, m_sc_w_out, m_ffn_w_up, m_ffn_conv_w,
                          m_ffn_conv_b, m_ffn_w_down, m_final_norm]))
    v_ = dict(zip(names, [v_c_ctx, v_ada_w, v_ada_b, v_norm_mix, v_norm_ffn, v_gla_w_in, v_gla_w_a2, v_gla_b_a,
                          v_gla_head_norm, v_gla_w_out, v_sc_w_in, v_sc_conv_w, v_sc_w_out, v_ffn_w_up, v_ffn_conv_w,
                          v_ffn_conv_b, v_ffn_w_down, v_final_norm]))
    me = 4 * lax.axis_index("x") + 2 * lax.axis_index("y") + lax.axis_index("c")
    bsz = x.shape[0]
    tm = 256
    nt = SEQ // tm
    ctx_tiles = CTX // tm
    pe = functools.partial(P, per_example=True)

    groups = {"gla": [("gla_w_in", 0), ("gla_w_out", 0)], "ffn0": [("ffn_w_up", 0), ("ffn_w_down", 0)],
              "l1": [("sc_w_in", 0), ("sc_w_out", 0), ("ffn_w_up", 1), ("ffn_w_down", 1)]}
    ag_started = {}

    def start_gather(g, after=None):
        srcs = [w_[n][i].astype(BF16) for n, i in groups[g]]
        if after is not None:
            *srcs, _ = lax.optimization_barrier((*srcs, after))
        ag_started[g] = exchange_start(f"ag_{g}_start", srcs, True)
        return ag_started[g][4][0, 0]

    small_sharded = [c, gla_w_a2, gla_b_a, sc_conv_w, ffn_conv_w]
    pack0, offs0 = _pack_rows(small_sharded, F32, 8)
    g0 = all_gather("ag_small", pack0, True).reshape(N_DEV, pack0.shape[0], D)
    c_all, wa2_s, ba_s, scw_s, fcw_s = _unpack_rows(g0, offs0, [a.shape for a in small_sharded])
    w_a2 = wa2_s[:, 0].transpose(1, 2, 0, 3).reshape(2, RANK, KD)
    b_a = ba_s[:, 0].transpose(1, 0, 2).reshape(2, KD)
    sc_cw = scw_s[:, 0].transpose(1, 0, 2).reshape(3, D)
    ffn_cw = fcw_s.transpose(1, 2, 0, 3).reshape(2, 3, 2 * FFN_H)

    cond = jnp.concatenate([c_all.reshape(N_DEV * bsz, D), c_ctx[None], jnp.zeros((ADA_ROWS - N_DEV * bsz - 1, D), F32)], 0)
    b_mine = lax.dynamic_slice(ada_b, (0, me * ADA_COLS), (2, ADA_COLS)).reshape(2, 1, ADA_COLS)
    mod_part = ada_fwd(cond, ada_w, b_mine)
    mod = all_gather("ag_mod", mod_part.reshape(2 * ADA_ROWS, ADA_COLS), True)
    mod = mod.reshape(N_DEV, 2, ADA_ROWS, ADA_COLS).transpose(1, 2, 0, 3).reshape(2, ADA_ROWS, 6 * D)
    mods = lax.dynamic_slice(mod, (0, bsz * me, 0), (2, bsz, 6 * D))
    md = [[mods[i][:, k * D:(k + 1) * D].reshape(bsz, 1, D) for k in range(6)] for i in range(2)]
    mc = [mod[0, ADA_CTX_ROW, k * D:(k + 1) * D][None] for k in range(2)]

    tok = mod[0, 0, 0]
    for g in groups:
        tok = start_gather(g, after=tok)
    norm_mix = norm_mix + tok

    def gathered(g, after):
        mine, lands = exchange_wait(f"ag_{g}_wait", ag_started[g], after, True)
        return [lax.dynamic_update_index_in_dim(ld, mn, me, 0) for ld, mn in zip(lands, mine)]

    s_up, w_down = [None, None], [None, None]
    wd = jnp.zeros((128, 2 * KD), F32).at[:RANK, :KD].set(w_a2[0]).at[RANK:2 * RANK, KD:].set(w_a2[1])
    bd = b_a.reshape(1, 2 * KD)
    scw = _rows3(sc_cw)
    head_gain = gla_head_norm.reshape(1, HV)
    gains_mix = [norm_mix[i][None] for i in range(2)]
    gains_ffn = [norm_ffn[i][None] for i in range(2)]

    def tokens(a2d, t_len):
        return a2d.reshape(bsz, t_len, -1)

    def ffn_params(i):
        rows = [ffn_cw[i][t] for t in range(3)] + [ffn_conv_b[i]]
        return [P(a.reshape(2, FFN_H), w=FFN_TC, rows=True) for a in rows]

    def ffn_fwd(i, hn2):
        u = mm(f"ffn_up{i}", V(hn2, "tok"), V(s_up[i], "cols"), out="planes", out_dtype=BF16, planes_t=SEQ)
        act = rowwise(f"ffn_mid{i}", f_ffn_mid, [X(u, w=FFN_TC, planes=True)], ffn_params(i), tm=SEQ, nt=1, nc=NCF,
                      outs=[(FFN_TC, BF16, 1)])[0]
        return u, act, tokens(mm(f"ffn_down{i}", V(act, "tok"), V(w_down[i])), SEQ)

    def res_mod_fwd(name, h, y, ps):
        return rowwise(name, f_res_mod, [X(h), X(y)], ps, tm=tm, nt=nt, outs=[(D, F32, 1), (D, BF16, 1)])

    ps_in0 = [P(gains_mix[0]), pe(md[0][0]), pe(md[0][1])]
    ps_ctx = [P(gains_mix[0]), P(mc[0]), P(mc[1])]
    hn0 = rowwise("mod_in0", f_mod, [X(x)], ps_in0, tm=tm, nt=nt, outs=[(D, BF16, 1)])[0]
    hnc = rowwise("mod_ctx", f_mod, [X(ctx)], ps_ctx, tm=tm, nt=ctx_tiles, outs=[(D, BF16, 1)])[0]
    hcat = jnp.concatenate([hnc, hn0], axis=1)
    s_gin, s_gout = gathered("gla", hcat)
    w_gin = jnp.pad(_cols_from_shards(s_gin), ((0, 0), (0, GLA_IN_PAD - GLA_IN)))
    w_gout = s_gout.reshape(VD, D)
    pcat = tokens(mm("gla_in", V(hcat, "tok"), V(w_gin)), TT)
    pa_x = X(pcat, w=128, co=(GLA_IN_PAD - 128) // 128)
    la = rowwise("gla_decay", f_decay, [pa_x], [P(wd), P(bd)], tm=tm, nt=TT // tm, outs=[(2 * KD, F32, 1)])[0]
    o2, s_all = gla_fwd(pcat, la)
    post_xs = [X(o2, w=VD, co=0, ro=ctx_tiles, split=HEADS), X(o2, w=VD, co=1, ro=ctx_tiles, split=HEADS),
               X(pcat, w=VD, co=2, ro=ctx_tiles, split=HEADS)]
    yin0 = rowwise("gla_post", f_gla_post, post_xs, [P(head_gain)], tm=tm, nt=nt, outs=[(VD, BF16, HEADS)])[0]
    y0 = tokens(mm("gla_out", V(yin0, "tok"), V(w_gout)), SEQ)
    ps_mid0 = [pe(md[0][2]), P(gains_ffn[0]), pe(md[0][3]), pe(md[0][4])]
    h1_0, hn2_0 = res_mod_fwd("res_mod_mid0", x, y0, ps_mid0)
    s_up[0], s_down0 = gathered("ffn0", hn2_0)
    w_down[0] = s_down0.reshape(FFN_H, D)
    u0, act0, fo0 = ffn_fwd(0, hn2_0)
    ps_in1 = [pe(md[0][5]), P(gains_mix[1]), pe(md[1][0]), pe(md[1][1])]
    h2_0, hn1 = res_mod_fwd("res_mod_in1", h1_0, fo0, ps_in1)

    s_sin, s_sout, s_up[1], s_down1 = gathered("l1", hn1)
    w_sout, w_down[1] = s_sout.reshape(D, D), s_down1.reshape(FFN_H, D)
    p1 = tokens(mm("sc_in", V(hn1, "tok"), V(s_sin, "cols")), SEQ)
    sc_ps = [P(a) for a in scw]
    yin1 = rowwise("sc_mid", f_sc_mid, [X(p1, split=3)], sc_ps, tm=tm, nt=nt, outs=[(D, BF16, 1)])[0]
    y1 = tokens(mm("sc_out", V(yin1, "tok"), V(w_sout)), SEQ)
    ps_mid1 = [pe(md[1][2]), P(gains_ffn[1]), pe(md[1][3]), pe(md[1][4])]
    h1_1, hn2_1 = res_mod_fwd("res_mod_mid1", h2_0, y1, ps_mid1)
    u1, act1, fo1 = ffn_fwd(1, hn2_1)
    loss8, dh1_1, dfo1, dm5_1, g_final = final_loss(h1_1, fo1, md[1][5], final_norm[None], loss_target)

    def ffn_bwd(i, u, act, hn2, dfo):
        dact = tokens(mm(f"ffn_down_dx{i}", V(dfo, "tok"), V(w_down[i]), form="nt", out_dtype=BF16), SEQ)
        g_down = mm(f"ffn_down_dw{i}", V(act, "tok"), V(dfo, "tok"), form="tn", out_dtype=BF16)
        r = rowwise(f"ffn_mid_bwd{i}", f_ffn_mid, [X(u, w=FFN_TC, planes=True)], ffn_params(i), tm=SEQ, nt=1, nc=NCF,
                    douts=[X(dact, w=FFN_TC)], dx={0: BF16}, dp=[0, 1, 2, 3])
        du, g_cw, g_cb = r[0], jnp.stack([a.reshape(2 * FFN_H) for a in r[1:4]]), r[4].reshape(1, 2 * FFN_H)
        dhn2 = tokens(mm(f"ffn_up_dx{i}", V(du, "planes"), V(s_up[i], "cols"), form="nt", out_dtype=BF16), SEQ)
        g_up = mm(f"ffn_up_dw{i}", V(hn2, "tok"), V(du, "planes"), form="tn", out="cols", out_dtype=BF16)
        return dhn2, g_up, row_slots(g_down), g_cw, g_cb

    def res_mod_bwd(name, h, y, ps, dh1, dhn):
        return rowwise(name, f_res_mod, [X(h), X(y)], ps, tm=tm, nt=nt, douts=[X(dh1), X(dhn)],
                       dx={0: F32, 1: BF16}, dp=[0, 1, 2, 3])

    def row_slots(g):
        return g.reshape(N_DEV, -1, g.shape[-1])

    a2a_started = {}

    def send_grads(g, slots, after=None):
        if after is not None:
            *slots, _ = lax.optimization_barrier((*slots, after))
        a2a_started[g] = exchange_start(f"a2a_{g}_start", list(slots), False)
        return a2a_started[g][4][0, 0]

    def after_start(ps, tok):
        return [dict(ps[0], a=ps[0]["a"] + tok)] + ps[1:]

    dhn2_1, g_up1, g_down1, g_fcw1, g_fcb1 = ffn_bwd(1, u1, act1, hn2_1, dfo1)
    dh2_0, dy1, dm2_1, g_nffn1, dm3_1, dm4_1 = res_mod_bwd("res_mod_mid1_bwd", h2_0, y1, ps_mid1, dh1_1, dhn2_1)
    dyin1 = tokens(mm("sc_out_dx", V(dy1, "tok"), V(w_sout), form="nt", out_dtype=BF16), SEQ)
    g_sout = row_slots(mm("sc_out_dw", V(yin1, "tok"), V(dy1, "tok"), form="tn", out_dtype=BF16))
    r = rowwise("sc_mid_bwd", f_sc_mid, [X(p1, split=3)], sc_ps, tm=tm, nt=nt, douts=[X(dyin1)], dx={0: BF16}, dp=[0, 1, 2])
    dp1, g_scw = r[0], jnp.concatenate(r[1:4], 0)
    dhn1 = tokens(mm("sc_in_dx", V(dp1, "tok"), V(s_sin, "cols"), form="nt", out_dtype=BF16), SEQ)
    g_sin = mm("sc_in_dw", V(hn1, "tok"), V(dp1, "tok"), form="tn", out="cols", out_dtype=BF16)
    tok = send_grads("l1", [g_sin, g_sout, g_up1, g_down1])
    dh1_0, dfo0, dm5_0, g_nmix1, dm0_1, dm1_1 = res_mod_bwd("res_mod_in1_bwd", h1_0, fo0, after_start(ps_in1, tok), dh2_0, dhn1)

    dhn2_0, g_up0, g_down0, g_fcw0, g_fcb0 = ffn_bwd(0, u0, act0, hn2_0, dfo0)
    tok = send_grads("ffn0", [g_up0, g_down0])
    dx_res, dy0, dm2_0, g_nffn0, dm3_0, dm4_0 = res_mod_bwd("res_mod_mid0_bwd", x, y0, after_start(ps_mid0, tok), dh1_0, dhn2_0)
    dyin0 = tokens(mm("gla_out_dx", V(dy0, "tok"), V(w_gout), form="nt", out_dtype=BF16), SEQ)
    g_gout = row_slots(mm("gla_out_dw", V(yin0, "tok"), V(dy0, "tok"), form="tn", out_dtype=BF16))
    do, dgate, g_head = rowwise("gla_post_bwd", f_gla_post, post_xs, [P(head_gain)], tm=tm, nt=nt,
                                douts=[X(dyin0, split=HEADS)], dx={0: F32, 2: BF16}, dp=[0])
    dq2, dk2, dv2, dla = gla_bwd(pcat, la, s_all, do)
    dpa, g_wd, g_bd = rowwise("gla_decay_bwd", f_decay, [pa_x], [P(wd), P(bd)], tm=tm, nt=TT // tm, douts=[X(dla)],
                              dx={0: BF16}, dp=[0, 1])
    dpcat = gla_combine(dq2, dk2, dv2, dgate, dpa)
    dhcat = tokens(mm("gla_in_dx", V(dpcat, "tok"), V(w_gin), form="nt", out_dtype=BF16), TT)
    g_gin = _cols_to_shards(mm("gla_in_dw", V(hcat, "tok"), V(dpcat, "tok"), form="tn", out_dtype=BF16)[:, :GLA_IN])
    grad_x, g_nmix0, dm0_0, dm1_0 = rowwise("mod_in0_bwd", f_mod, [X(x)], ps_in0, tm=tm, nt=nt,
                                            douts=[X(dhcat, ro=ctx_tiles), X(dx_res)], dx={0: F32}, dp=[0, 1, 2])
    g_nmix0c, dmc0, dmc1 = rowwise("mod_ctx_bwd", f_mod1, [X(ctx)], ps_ctx, tm=tm, nt=ctx_tiles, douts=[X(dhcat)],
                                   dx={}, dp=[0, 1, 2])

    zero_row = jnp.zeros((1, 4 * D), F32)
    dmod = [jnp.concatenate([jnp.concatenate([a.reshape(bsz, D) for a in dms], 1), ctx_row], 0)
            for dms, ctx_row in (([dm0_0, dm1_0, dm2_0, dm3_0, dm4_0, dm5_0], jnp.concatenate([dmc0, dmc1, zero_row], 1)),
                                 ([dm0_1, dm1_1, dm2_1, dm3_1, dm4_1, dm5_1], jnp.zeros((1, 6 * D), F32)))]
    g_wa2 = jnp.stack([g_wd[:RANK, :KD], g_wd[RANK:2 * RANK, KD:]])
    small_grads = [jnp.stack(dmod), jnp.concatenate([g_nmix0 + g_nmix0c, g_nmix1], 0), jnp.concatenate([g_nffn0, g_nffn1], 0),
                   g_head, jnp.concatenate([g_fcb0, g_fcb1], 0), g_final, g_wa2, g_bd.reshape(2, KD), g_scw,
                   jnp.stack([g_fcw0, g_fcw1]), loss8[:1]]
    pack1, offs1 = _pack_rows(small_grads, F32, 8)
    g1 = all_gather("ag_grads", pack1, True).reshape(N_DEV, pack1.shape[0], D)
    dmod_all = _unpack_rows(g1, offs1[:1], [small_grads[0].shape])[0]
    tot = _unpack_rows(sum_slots("sum_small", g1), offs1, [a.shape for a in small_grads])
    loss = tot[10][0, 0]
    dm_rows = dmod_all[:, :, :bsz].transpose(1, 0, 2, 3).reshape(2, N_DEV * bsz, 6 * D)
    dm_full = jnp.concatenate([dm_rows, tot[0][:, bsz:], jnp.zeros((2, ADA_ROWS - N_DEV * bsz - 1, 6 * D), F32)], 1)
    dm_mine = lax.dynamic_slice(dm_full, (0, 0, me * ADA_COLS), (2, ADA_ROWS, ADA_COLS))
    g_ada_w, g_ada_b, cpart = ada_bwd(cond, dm_mine, dm_full, ada_w)
    cparts = all_gather("ag_cctx", cpart, True).reshape(N_DEV, ADA_ROWS - ADA_CTX_ROW, D)[:, 0]
    g_cctx = cctx_grad(cparts, c_ctx[None])[0]
    tok = send_grads("gla", [g_gin, g_gout], after=g_cctx)

    def my_cols(full, n):
        return lax.dynamic_slice_in_dim(full, me * n, n, axis=full.ndim - 1)

    grads = {
        "c_ctx": g_cctx, "ada_b": g_ada_b.reshape(2, 6 * D), "norm_mix": tot[1], "norm_ffn": tot[2],
        "gla_head_norm": tot[3], "ffn_conv_b": tot[4], "final_norm": tot[5].reshape(D),
        "gla_w_a2": my_cols(tot[6], KD // N_DEV)[None], "gla_b_a": my_cols(tot[7], KD // N_DEV)[None],
        "sc_conv_w": my_cols(tot[8], D // N_DEV)[None], "ffn_conv_w": my_cols(tot[9], 2 * FFN_H // N_DEV),
    }

    res_ada = adamw("adamw_ada", *[a.reshape(2 * D, ADA_COLS) for a in (ada_w, g_ada_w, m_ada_w, v_ada_w)])
    grads["c_ctx"] = g_cctx + tok
    big = ["gla_w_in", "gla_w_out", "sc_w_in", "sc_w_out", "ffn_w_up", "ffn_w_down"]
    small = [n for n in names if n not in big and n != "ada_w"]
    g_small = _pack_rows([grads[n] for n in small], F32, 8)[0]
    res_small = adamw("adamw_small", _pack_rows([w_[n] for n in small], F32, 8)[0], g_small,
                      _pack_rows([m_[n] for n in small], F32, 8)[0], _pack_rows([v_[n] for n in small], F32, 8)[0])
    offs_s = _pack_rows([w_[n] for n in small], F32, 8)[1]

    big_res, done = {}, [res_small[0][0, 0], res_ada[0][0, 0]]
    for g in ("l1", "ffn0", "gla"):
        sent, lands = exchange_wait(f"a2a_{g}_wait", a2a_started[g], jnp.stack(done), False)
        for (n, i), mine, land in zip(groups[g], sent, lands):
            land = lax.dynamic_update_index_in_dim(land, lax.dynamic_index_in_dim(mine, me, 0, keepdims=False), me, 0)
            big_res[(n, i)] = adamw(f"adamw_{n}{i}", w_[n], land, m_[n], v_[n], layer=i)
            done.append(big_res[(n, i)][0][0, 0])

    out = {}
    for kind, idx in (("grad", 0), ("delta", 1), ("new_m", 2), ("new_v", 3)):
        vals = {n: jnp.stack([big_res[(n, i)][idx] for i in range(w_[n].shape[0])]) for n in big}
        vals["ada_w"] = res_ada[idx].reshape(ada_w.shape)
        vals.update(zip(small, _unpack_rows(res_small[idx], offs_s, [w_[n].shape for n in small])))
        out[kind] = [vals[n] for n in names]
    return (loss, grad_x, *out["grad"], *out["delta"], *out["new_m"], *out["new_v"])
```

```python
import functools

import jax
import jax.numpy as jnp
from jax import lax
from jax.experimental import pallas as pl
from jax.experimental.pallas import tpu as pltpu

F32 = jnp.float32
BF16 = jnp.bfloat16

N_DEV = 8
D = 1024
SEQ = 2048
CTX = 256
TT = CTX + SEQ
GRID_W = 64
CHUNK = 64
HEADS = 4
HK = 128
HV = 256
KD = 512
VD = 1024
RANK = 16
TAU = 16.0
GLA_IN = 3104
GLA_IN_PAD = 3200
FFN_H = 2560
FFN_TC = 256
EPS = 1e-6
LR, B1, B2, AEPS, WD, STEP = 0.001, 0.9, 0.999, 1e-08, 0.01, 10
MESH = pl.DeviceIdType.MESH


def _blocks(n):
    return [n] + [t for t in range(n - n % 128, 0, -128) if n % t == 0 and t != n]


def V(arr, kind="flat"):
    if kind == "tok":
        return V(arr.reshape(-1, arr.shape[-1]))
    if kind == "flat":
        r, c = arr.shape
        return dict(a=arr, kind=kind, shape=(r, c), rows=_blocks(r), cols=_blocks(c))
    if kind == "planes":
        bsz, _, t, ch = arr.shape
        return dict(a=arr, kind=kind, shape=(bsz * t, 2 * ch), rows=_blocks(t), cols=[2 * ch] + _blocks(ch), t=t, ch=ch)
    _, r, n = arr.shape
    return dict(a=arr, kind=kind, shape=(r, N_DEV * n), rows=_blocks(r), cols=[8 * n, 4 * n, 2 * n], n=n)


def _view_spec(v, br, bc, idx):
    if v["kind"] == "flat":
        return pl.BlockSpec((br, bc), idx)
    if v["kind"] == "planes":
        nt = v["t"] // br
        if bc == 2 * v["ch"]:
            return pl.BlockSpec((None, 2, br, v["ch"]), lambda i, j, k: (idx(i, j, k)[0] // nt, 0, idx(i, j, k)[0] % nt, 0))
        nch = v["ch"] // bc

        def at(i, j, k):
            r, c = idx(i, j, k)
            return r // nt, c // nch, r % nt, c % nch
        return pl.BlockSpec((None, None, br, bc), at)
    return pl.BlockSpec((bc // v["n"], br, v["n"]), lambda i, j, k: (idx(i, j, k)[1], idx(i, j, k)[0], 0))


def _out_view(kind, rows, cols, dtype, planes_t=None):
    if kind == "flat":
        shape = (rows, cols)
    elif kind == "planes":
        shape = (rows // planes_t, 2, planes_t, cols // 2)
    else:
        shape = (N_DEV, rows, cols // N_DEV)
    return V(jax.ShapeDtypeStruct(shape, dtype), kind)


MM_VMEM_BUDGET = 40 * 2 ** 20
MM_VMEM_LIMIT = 56 * 2 ** 20
MM_MAX_TILE = 1536


def _mm_tiles(m, n, kk, ms, ns, ks, a_bytes, b_bytes, o_bytes):
    best = None
    for tk in ks:
        for tm in [t for t in ms if t <= MM_MAX_TILE]:
            for tn in [t for t in ns if t <= MM_MAX_TILE]:
                one_k = tk == kk
                need = 2 * (tm * tk * a_bytes + tk * tn * b_bytes + tm * tn * o_bytes) + (0 if one_k else tm * tn * 4)
                if need > MM_VMEM_BUDGET:
                    continue
                steps = (m // tm) * (n // tn) * (kk // tk)
                traffic = (m * kk * a_bytes * (1 if one_k else n // tn)
                           + kk * n * b_bytes * (1 if one_k and n == tn else m // tm) + m * n * o_bytes)
                fill = (tm * tk * a_bytes + tk * tn * b_bytes) / 2.5e12
                cost = max(2.0 * m * n * kk / (9e14 if one_k else 6.5e14), traffic / 2.5e12) + steps * 0.4e-6 + fill
                if best is None or cost < best[0]:
                    best = (cost, tm, tn, tk)
    return best[1:]


def mm(name, a, b, form="nn", out="flat", out_dtype=F32, planes_t=None):
    (m, kk) = a["shape"][::-1] if form == "tn" else a["shape"]
    n = b["shape"][0] if form == "nt" else b["shape"][1]
    assert (b["shape"][1] if form == "nt" else b["shape"][0]) == kk, (name, a["shape"], b["shape"])
    o = _out_view(out, m, n, out_dtype, planes_t)
    a_m, a_k = (a["cols"], a["rows"]) if form == "tn" else (a["rows"], a["cols"])
    b_k, b_n = (b["cols"], b["rows"]) if form == "nt" else (b["rows"], b["cols"])
    tm, tn, tk = _mm_tiles(m, n, kk, [t for t in a_m if t in o["rows"]], [t for t in b_n if t in o["cols"]],
                           [t for t in a_k if t in b_k], a["a"].dtype.itemsize, b["a"].dtype.itemsize,
                           jnp.dtype(out_dtype).itemsize)
    nk = kk // tk
    dn = (((0 if form == "tn" else 1,), (1 if form == "nt" else 0,)), ((), ()))

    def load(ref):
        if len(ref.shape) == 3:
            return jnp.concatenate([ref[p] for p in range(ref.shape[0])], axis=-1).astype(BF16)
        return ref[...].astype(BF16)

    def store(o_ref, val):
        val = val.astype(out_dtype)
        if len(o_ref.shape) == 3:
            w = o_ref.shape[-1]
            for p in range(o_ref.shape[0]):
                o_ref[p] = val[:, p * w:(p + 1) * w]
        else:
            o_ref[...] = val

    def body(a_ref, b_ref, o_ref, *acc):
        if nk == 1:
            store(o_ref, lax.dot_general(load(a_ref), load(b_ref), dn, preferred_element_type=F32))
            return
        k, acc_ref = pl.program_id(2), acc[0]

        @pl.when(k == 0)
        def _():
            acc_ref[...] = jnp.zeros_like(acc_ref)

        acc_ref[...] += lax.dot_general(load(a_ref), load(b_ref), dn, preferred_element_type=F32)

        @pl.when(k == nk - 1)
        def _():
            store(o_ref, acc_ref[...])

    if form == "tn":
        a_spec = _view_spec(a, tk, tm, lambda i, j, k: (k, i))
    else:
        a_spec = _view_spec(a, tm, tk, lambda i, j, k: (i, k))
    if form == "nt":
        b_spec = _view_spec(b, tn, tk, lambda i, j, k: (j, k))
    else:
        b_spec = _view_spec(b, tk, tn, lambda i, j, k: (k, j))
    return pl.pallas_call(
        body, name=name, grid=(m // tm, n // tn, nk),
        in_specs=[a_spec, b_spec], out_specs=_view_spec(o, tm, tn, lambda i, j, k: (i, j)), out_shape=o["a"],
        scratch_shapes=[pltpu.VMEM((tm, tn), F32)] if nk > 1 else [],
        compiler_params=pltpu.CompilerParams(dimension_semantics=("parallel", "parallel", "arbitrary"),
                                             vmem_limit_bytes=MM_VMEM_LIMIT),
    )(a["a"], b["a"])


def X(arr, w=None, co=0, ro=0, split=1, planes=False):
    return dict(a=arr, w=arr.shape[-1] if w is None else w, co=co, ro=ro, split=2 if planes else split,
                mode="planes" if planes else "cols")


def P(arr, per_example=False, w=None, split=1, rows=False):
    return dict(a=arr, e=per_example, w=arr.shape[-1] if w is None else w, split=arr.shape[-2] if rows else split,
                mode="rows" if rows else "cols")


def _pieces(ref, s):
    if s["mode"] == "planes":
        return [ref[0], ref[1]]
    if s["mode"] == "rows":
        return [ref[i:i + 1, :] for i in range(s["split"])]
    w = ref.shape[-1] // s["split"]
    return [ref[:, i * w:(i + 1) * w] for i in range(s["split"])]


def _store(ref, pieces, s, accumulate=False):
    w = ref.shape[-1] // len(pieces)
    for i, p in enumerate(pieces):
        at = (i,) if s["mode"] == "planes" else (slice(i, i + 1),) if s["mode"] == "rows" else (slice(None), slice(i * w, (i + 1) * w))
        if accumulate:
            ref[at] += p.astype(ref.dtype)
        else:
            ref[at] = p.astype(ref.dtype)


def rowwise(name, f, xs, ps, *, tm, nt, nc=1, outs=None, douts=None, dx=None, dp=None):
    bsz = xs[0]["a"].shape[0]
    fwd = douts is None
    nx, np_ = len(xs), len(ps)
    douts = [] if fwd else douts
    dx = {} if fwd else dx
    dp = [] if fwd else dp

    def x_spec(s):
        if s["mode"] == "planes":
            return pl.BlockSpec((None, 2, tm, s["w"]), lambda c, b, t, s=s: (b, 0, t + s["ro"], c + s["co"]))
        return pl.BlockSpec((None, tm, s["w"]), lambda c, b, t, s=s: (b, t + s["ro"], c + s["co"]))

    def x_out(s, dt):
        if s["mode"] == "planes":
            return (jax.ShapeDtypeStruct((bsz, 2, nt * tm, nc * s["w"]), dt),
                    pl.BlockSpec((None, 2, tm, s["w"]), lambda c, b, t: (b, 0, t, c)))
        return (jax.ShapeDtypeStruct((bsz, nt * tm, nc * s["w"]), dt), pl.BlockSpec((None, tm, s["w"]), lambda c, b, t: (b, t, c)))

    def p_spec(s):
        r = s["a"].shape[-2]
        if s["e"]:
            return pl.BlockSpec((None, r, s["w"]), lambda c, b, t: (b, 0, c))
        return pl.BlockSpec((r, s["w"]), lambda c, b, t: (0, c))

    in_specs = [x_spec(s) for s in xs] + [p_spec(s) for s in ps] + [x_spec(s) for s in douts]
    operands = [s["a"] for s in xs] + [s["a"] for s in ps] + [s["a"] for s in douts]
    if fwd:
        out_modes = [dict(mode="cols", split=sp) for (_, _, sp) in outs]
        out_shape = [jax.ShapeDtypeStruct((bsz, nt * tm, nc * w), dt) for (w, dt, _) in outs]
        out_specs = [pl.BlockSpec((None, tm, w), lambda c, b, t: (b, t, c)) for (w, _, _) in outs]
    else:
        dx_outs = [x_out(xs[i], dt) for i, dt in dx.items()]
        out_shape, out_specs = [o[0] for o in dx_outs], [o[1] for o in dx_outs]
        for j in dp:
            s = ps[j]
            r = s["a"].shape[-2]
            if s["e"]:
                out_shape.append(jax.ShapeDtypeStruct((bsz, r, nc * s["w"]), F32))
                out_specs.append(pl.BlockSpec((None, r, s["w"]), lambda c, b, t: (b, 0, c)))
            else:
                out_shape.append(jax.ShapeDtypeStruct((r, nc * s["w"]), F32))
                out_specs.append(pl.BlockSpec((r, s["w"]), lambda c, b, t: (0, c)))

    def body(*refs):
        x_refs, p_refs = refs[:nx], refs[nx:nx + np_]
        d_refs = refs[nx + np_:nx + np_ + len(douts)]
        o_refs = refs[nx + np_ + len(douts):]
        xv = [[p.astype(F32) for p in _pieces(r, s)] for r, s in zip(x_refs, xs)]
        pv = [[p.astype(F32) for p in _pieces(r, s)] for r, s in zip(p_refs, ps)]
        if fwd:
            for r, pieces, s in zip(o_refs, f(xv, pv), out_modes):
                _store(r, pieces, s)
            return
        _, vjp = jax.vjp(f, xv, pv)
        cot = [[p.astype(F32) for p in _pieces(r, s)] for r, s in zip(d_refs, douts)]
        dxv, dpv = vjp(cot)
        for r, i in zip(o_refs, dx):
            _store(r, dxv[i], xs[i])
        b, t = pl.program_id(1), pl.program_id(2)
        for r, j in zip(o_refs[len(dx):], dp):
            first = (t == 0) if ps[j]["e"] else jnp.logical_and(b == 0, t == 0)

            @pl.when(first)
            def _(r=r, j=j):
                _store(r, dpv[j], ps[j])

            @pl.when(jnp.logical_not(first))
            def _(r=r, j=j):
                _store(r, dpv[j], ps[j], accumulate=True)

    res = pl.pallas_call(
        body, name=name, grid=(nc, bsz, nt), in_specs=in_specs, out_specs=out_specs, out_shape=out_shape,
        compiler_params=pltpu.CompilerParams(dimension_semantics=("arbitrary", "arbitrary", "arbitrary")),
    )(*operands)
    return res


def _keep_rows(a, shift, keep):
    n = a.shape[0]
    t = lax.broadcasted_iota(jnp.int32, a.shape, 0)
    return jnp.where(keep(t, n), pltpu.roll(a, shift % n, 0), 0.0)


def _shift_pair(step, keep_prev, keep_next):
    @jax.custom_vjp
    def prev(a):
        return _keep_rows(a, step, keep_prev)

    @jax.custom_vjp
    def nxt(a):
        return _keep_rows(a, -step, keep_next)

    prev.defvjp(lambda a: (prev(a), None), lambda _, g: (nxt(g),))
    nxt.defvjp(lambda a: (nxt(a), None), lambda _, g: (prev(g),))
    return prev, nxt


prev_tok, next_tok = _shift_pair(1, lambda t, n: t % GRID_W != 0, lambda t, n: t % GRID_W != GRID_W - 1)
prev_row, next_row = _shift_pair(GRID_W, lambda t, n: t >= GRID_W, lambda t, n: t < n - GRID_W)


@jax.custom_vjp
def bdot(a, w):
    return jnp.dot(a.astype(BF16), w.astype(BF16), preferred_element_type=F32)


def _bdot_bwd(res, g):
    a, w = res
    gb = g.astype(BF16)
    da = lax.dot_general(gb, w.astype(BF16), (((1,), (1,)), ((), ())), preferred_element_type=F32)
    dw = lax.dot_general(a.astype(BF16), gb, (((0,), (0,)), ((), ())), preferred_element_type=F32)
    return da, dw


bdot.defvjp(lambda a, w: (bdot(a, w), (a, w)), _bdot_bwd)


@jax.custom_vjp
def log_sigmoid(z):
    return jnp.minimum(z, 0.0) - jnp.log(1.0 + jnp.exp(-jnp.abs(z)))


def _lsig_bwd(z, g):
    e = jnp.exp(-jnp.abs(z))
    return (g * jnp.where(z >= 0, e, 1.0) / (1.0 + e),)


log_sigmoid.defvjp(lambda z: (log_sigmoid(z), z), _lsig_bwd)


def silu(x):
    return x * jax.nn.sigmoid(x)


def _rms(x):
    return x * lax.rsqrt(jnp.mean(x * x, axis=-1, keepdims=True) + EPS)


def _mod(x, gain, shift, scale):
    return _rms(x) * gain * (1.0 + scale) + shift


def f_mod(xs, ps):
    ((h,),), ((gain,), (shift,), (scale,)) = xs, ps
    return [[_mod(h, gain, shift, scale)], [h]]


def f_res_mod(xs, ps):
    ((h,), (y,)), ((gate,), (gain,), (shift,), (scale,)) = xs, ps
    h1 = h + gate * y
    return [[h1], [_mod(h1, gain, shift, scale)]]


def f_ffn_mid(xs, ps):
    ((ua, ug),), ((w0a, w0g), (w1a, w1g), (w2a, w2g), (ba, bg)) = xs, ps
    a = w0a * prev_row(ua) + w1a * ua + w2a * next_row(ua) + ba
    g = w0g * prev_row(ug) + w1g * ug + w2g * next_row(ug) + bg
    return [[a * silu(g)]]


def f_sc_mid(xs, ps):
    ((bg, cg, v),), ((w0,), (w1,), (w2,)) = xs, ps
    z = cg * v
    return [[bg * (w0 * prev_tok(z) + w1 * z + w2 * next_tok(z))]]


def f_decay(xs, ps):
    ((a,),), ((wd,), (bd,)) = xs, ps
    return [[log_sigmoid(bdot(a, wd) + bd) / TAU]]


def f_gla_post(xs, ps):
    (of, ob, g), ((gain,),) = xs, ps
    return [[_rms(a + b) * gain * silu(c) for a, b, c in zip(of, ob, g)]]


NCH = TT // CHUNK
CTX_CH = CTX // CHUNK
_NT = (((1,), (1,)), ((), ()))
_TN = (((0,), (0,)), ((), ()))
_NN = (((1,), (0,)), ((), ()))


def _chunk_of(d, j):
    return jnp.where(d == 0, j, jnp.where(j < CTX_CH, CTX_CH - 1 - j, NCH + CTX_CH - 1 - j))


def _dot(a, b, dn):
    return lax.dot_general(a, b, dn, preferred_element_type=F32)


def _mask_dot(m, g):
    g0 = g.astype(BF16)
    r1 = g - g0.astype(F32)
    g1 = r1.astype(BF16)
    g2 = (r1 - g1.astype(F32)).astype(BF16)
    return _dot(m, g0, _NN) + _dot(m, g1, _NN) + _dot(m, g2, _NN)


def _causal(d):
    row = lax.broadcasted_iota(jnp.int32, (CHUNK, CHUNK), 0)
    col = lax.broadcasted_iota(jnp.int32, (CHUNK, CHUNK), 1)
    delta = jnp.where(d == 0, col - row, row - col)
    return delta <= 0, delta >= 0


def _gla_in_specs(bsz, rev):
    def blk(d, j):
        return _chunk_of(d, (NCH - 1 - j) if rev else j)

    return [
        pl.BlockSpec((bsz, CHUNK, KD), lambda d, j: (0, blk(d, j), 0)),
        pl.BlockSpec((bsz, CHUNK, KD), lambda d, j: (0, blk(d, j), 1)),
        pl.BlockSpec((bsz, CHUNK, VD), lambda d, j: (0, blk(d, j), 1)),
        pl.BlockSpec((bsz, CHUNK, KD), lambda d, j: (0, blk(d, j), d)),
    ], blk


def gla_fwd(pcat, la):
    bsz = pcat.shape[0]
    in_specs, blk = _gla_in_specs(bsz, False)

    def body(q_ref, k_ref, v_ref, la_ref, o_ref, s_ref, st):
        d, j = pl.program_id(0), pl.program_id(1)

        @pl.when(j == 0)
        def _():
            st[...] = jnp.zeros_like(st)

        s_ref[...] = st[...]
        causal, _ = _causal(d)
        mf = causal.astype(BF16)
        for e, h in [(e, h) for e in range(bsz) for h in range(HEADS)]:
            ks_, vs_ = slice(h * HK, (h + 1) * HK), slice(h * HV, (h + 1) * HV)
            q, k, v, g = q_ref[e, :, ks_] * (HK ** -0.5), k_ref[e, :, ks_], v_ref[e, :, vs_].astype(BF16), la_ref[e, :, ks_]
            b = _mask_dot(mf, g)
            bl = jnp.sum(g, axis=0, keepdims=True)
            qs = (q * jnp.exp(b)).astype(BF16)
            ks = (k * jnp.exp(-b)).astype(BF16)
            kd = (k * jnp.exp(bl - b)).astype(BF16)
            s = st[e, h]
            att = jnp.where(causal, _dot(qs, ks, _NT), 0.0).astype(BF16)
            o_ref[e, :, vs_] = _dot(qs, s.astype(BF16), _NT) + _dot(att, v, _NN)
            st[e, h] = jnp.exp(bl) * s + _dot(v, kd, _TN)

    return pl.pallas_call(
        body, name="gla_fwd", grid=(2, NCH), in_specs=in_specs,
        out_specs=[pl.BlockSpec((bsz, CHUNK, VD), lambda d, j: (0, blk(d, j), d)),
                   pl.BlockSpec((bsz, None, None, HEADS, HV, HK), lambda d, j: (0, d, j, 0, 0, 0))],
        out_shape=[jax.ShapeDtypeStruct((bsz, TT, 2 * VD), F32), jax.ShapeDtypeStruct((bsz, 2, NCH, HEADS, HV, HK), F32)],
        scratch_shapes=[pltpu.VMEM((bsz, HEADS, HV, HK), F32)],
        compiler_params=pltpu.CompilerParams(dimension_semantics=("arbitrary", "arbitrary")),
    )(pcat, pcat, pcat, la)


def gla_bwd(pcat, la, s_all, do):
    bsz = pcat.shape[0]
    in_specs, blk = _gla_in_specs(bsz, True)
    in_specs += [
        pl.BlockSpec((bsz, None, None, HEADS, HV, HK), lambda d, j: (0, d, NCH - 1 - j, 0, 0, 0)),
        pl.BlockSpec((bsz, CHUNK, VD), lambda d, j: (0, jnp.maximum(blk(d, j) - CTX_CH, 0), 0)),
    ]

    def body(q_ref, k_ref, v_ref, la_ref, s_ref, do_ref, dq_ref, dk_ref, dv_ref, dla_ref, dst):
        d, j = pl.program_id(0), pl.program_id(1)

        @pl.when(j == 0)
        def _():
            dst[...] = jnp.zeros_like(dst)

        latent = blk(d, j) >= CTX_CH
        causal, causal_t = _causal(d)
        mt = causal_t.astype(BF16)
        mf = causal.astype(BF16)
        scale = HK ** -0.5
        for e, h in [(e, h) for e in range(bsz) for h in range(HEADS)]:
            ks_, vs_ = slice(h * HK, (h + 1) * HK), slice(h * HV, (h + 1) * HV)
            q, k, v, g = q_ref[e, :, ks_] * scale, k_ref[e, :, ks_], v_ref[e, :, vs_].astype(BF16), la_ref[e, :, ks_]
            b = _mask_dot(mf, g)
            bl = jnp.sum(g, axis=0, keepdims=True)
            ex, ei, ed, el = jnp.exp(b), jnp.exp(-b), jnp.exp(bl - b), jnp.exp(bl)
            qs, ks, kd = q * ex, k * ei, k * ed
            qsb, ksb, kdb = qs.astype(BF16), ks.astype(BF16), kd.astype(BF16)
            s, ds1 = s_ref[e, h], dst[e, h]
            sb, ds1b = s.astype(BF16), ds1.astype(BF16)
            dob = jnp.where(latent, do_ref[e, :, vs_], 0.0).astype(BF16)
            att = jnp.where(causal, _dot(qsb, ksb, _NT), 0.0).astype(BF16)
            datt = jnp.where(causal, _dot(dob, v, _NT), 0.0).astype(BF16)
            dqs = _dot(dob, sb, _NN) + _dot(datt, ksb, _NN)
            dks = _dot(datt, qsb, _TN)
            dv_ref[e, :, vs_] = _dot(att, dob, _TN) + _dot(kdb, ds1b, _NT)
            dkd = _dot(v, ds1b, _NN)
            dst[e, h] = _dot(dob, qsb, _TN) + el * ds1
            del_ = jnp.sum(s * ds1, axis=0, keepdims=True)
            dq_ref[e, :, ks_] = dqs * ex * scale
            dk_ref[e, :, ks_] = dks * ei + dkd * ed
            db = dqs * qs - dks * ks - dkd * kd
            dbl = jnp.sum(dkd * kd, axis=0, keepdims=True) + del_ * el
            dla_ref[e, :, ks_] = _mask_dot(mt, db) + dbl

    return pl.pallas_call(
        body, name="gla_bwd", grid=(2, NCH), in_specs=in_specs,
        out_specs=[pl.BlockSpec((None, bsz, CHUNK, KD), lambda d, j: (d, 0, blk(d, j), 0)),
                   pl.BlockSpec((None, bsz, CHUNK, KD), lambda d, j: (d, 0, blk(d, j), 0)),
                   pl.BlockSpec((None, bsz, CHUNK, VD), lambda d, j: (d, 0, blk(d, j), 0)),
                   pl.BlockSpec((bsz, CHUNK, KD), lambda d, j: (0, blk(d, j), d))],
        out_shape=[jax.ShapeDtypeStruct((2, bsz, TT, KD), F32), jax.ShapeDtypeStruct((2, bsz, TT, KD), F32),
                   jax.ShapeDtypeStruct((2, bsz, TT, VD), F32), jax.ShapeDtypeStruct((bsz, TT, 2 * KD), F32)],
        scratch_shapes=[pltpu.VMEM((bsz, HEADS, HV, HK), F32)],
        compiler_params=pltpu.CompilerParams(dimension_semantics=("arbitrary", "arbitrary")),
    )(pcat, pcat, pcat, la, s_all, do)


def gla_combine(dq2, dk2, dv2, dgate, dpa):
    bsz = dgate.shape[0]
    tm = CTX

    def body(dq_ref, dk_ref, dv_ref, dg_ref, dpa_ref, o_ref):
        t = pl.program_id(1)
        o_ref[:, 0:KD] = (dq_ref[0] + dq_ref[1]).astype(BF16)
        o_ref[:, KD:2 * KD] = (dk_ref[0] + dk_ref[1]).astype(BF16)
        o_ref[:, 2 * KD:2 * KD + VD] = (dv_ref[0] + dv_ref[1]).astype(BF16)
        o_ref[:, 2 * KD + VD:2 * KD + 2 * VD] = jnp.where(t > 0, dg_ref[...], 0).astype(BF16)
        o_ref[:, 2 * KD + 2 * VD:] = dpa_ref[...].astype(BF16)

    return pl.pallas_call(
        body, name="gla_combine", grid=(bsz, TT // tm),
        in_specs=[pl.BlockSpec((2, None, tm, KD), lambda b, t: (0, b, t, 0)),
                  pl.BlockSpec((2, None, tm, KD), lambda b, t: (0, b, t, 0)),
                  pl.BlockSpec((2, None, tm, VD), lambda b, t: (0, b, t, 0)),
                  pl.BlockSpec((None, tm, VD), lambda b, t: (b, jnp.maximum(t - 1, 0), 0)),
                  pl.BlockSpec((None, tm, 128), lambda b, t: (b, t, 0))],
        out_specs=pl.BlockSpec((None, tm, GLA_IN_PAD), lambda b, t: (b, t, 0)),
        out_shape=jax.ShapeDtypeStruct((bsz, TT, GLA_IN_PAD), BF16),
        compiler_params=pltpu.CompilerParams(dimension_semantics=("arbitrary", "arbitrary")),
    )(dq2, dk2, dv2, dgate, dpa)


def final_loss(h1, fo, gate, gain, tgt):
    bsz, t_len, _ = h1.shape
    tm = 256

    def body(h_ref, f_ref, gate_ref, gain_ref, tgt_ref, loss_ref, dh_ref, df_ref, dgate_ref, dgain_ref):
        b, t = pl.program_id(0), pl.program_id(1)
        target = tgt_ref[...]

        def core(h, fo_, gate_, gain_):
            e = _rms(h + gate_ * fo_) * gain_ - target
            return jnp.sum(0.5 * jnp.sum(e * e, axis=-1, keepdims=True) / D, axis=0, keepdims=True)

        loss, vjp = jax.vjp(core, h_ref[...], f_ref[...], gate_ref[...], gain_ref[...])
        dh, df, dgate, dgain = vjp(jnp.ones((1, 1), F32))
        dh_ref[...] = dh
        df_ref[...] = df.astype(BF16)
        first = jnp.logical_and(b == 0, t == 0)

        @pl.when(first)
        def _():
            loss_ref[...] = jnp.broadcast_to(loss, loss_ref.shape)
            dgain_ref[...] = dgain

        @pl.when(jnp.logical_not(first))
        def _():
            loss_ref[...] += jnp.broadcast_to(loss, loss_ref.shape)
            dgain_ref[...] += dgain

        @pl.when(t == 0)
        def _():
            dgate_ref[...] = dgate

        @pl.when(t > 0)
        def _():
            dgate_ref[...] += dgate

    tile = pl.BlockSpec((None, tm, D), lambda b, t: (b, t, 0))
    per_ex = pl.BlockSpec((None, 1, D), lambda b, t: (b, 0, 0))
    shared = pl.BlockSpec((1, D), lambda b, t: (0, 0))
    return pl.pallas_call(
        body, name="final_loss", grid=(bsz, t_len // tm),
        in_specs=[tile, tile, per_ex, shared, tile],
        out_specs=[pl.BlockSpec((8, 128), lambda b, t: (0, 0)), tile, tile, per_ex, shared],
        out_shape=[jax.ShapeDtypeStruct((8, 128), F32), jax.ShapeDtypeStruct(h1.shape, F32),
                   jax.ShapeDtypeStruct(h1.shape, BF16), jax.ShapeDtypeStruct((bsz, 1, D), F32),
                   jax.ShapeDtypeStruct((1, D), F32)],
        compiler_params=pltpu.CompilerParams(dimension_semantics=("arbitrary", "arbitrary")),
    )(h1, fo, gate, gain, tgt)


ADA_ROWS = 24
ADA_CTX_ROW = 16
ADA_COLS = 6 * D // N_DEV


def ada_fwd(cond, w, b):
    def body(c_ref, w_ref, b_ref, o_ref):
        s = silu(c_ref[...]).astype(BF16)
        o_ref[...] = jnp.dot(s, w_ref[...].astype(BF16), preferred_element_type=F32) + b_ref[...]

    return pl.pallas_call(
        body, name="ada_fwd", grid=(2,),
        in_specs=[pl.BlockSpec((ADA_ROWS, D), lambda i: (0, 0)), pl.BlockSpec((None, D, ADA_COLS), lambda i: (i, 0, 0)),
                  pl.BlockSpec((None, 1, ADA_COLS), lambda i: (i, 0, 0))],
        out_specs=pl.BlockSpec((None, ADA_ROWS, ADA_COLS), lambda i: (i, 0, 0)),
        out_shape=jax.ShapeDtypeStruct((2, ADA_ROWS, ADA_COLS), F32),
    )(cond, w, b)


def ada_bwd(cond, dm_mine, dm_full, w):
    def body(c_ref, dm_ref, dmf_ref, w_ref, gw_ref, gb_ref, cp_ref):
        i = pl.program_id(0)
        s = silu(c_ref[...]).astype(BF16)
        dm = dm_ref[...].astype(BF16)
        gw_ref[...] = _dot(s, dm, _TN)
        gb_ref[...] = jnp.sum(dmf_ref[...], axis=0, keepdims=True)

        @pl.when(i == 0)
        def _():
            cp_ref[...] = _dot(dm_ref[ADA_CTX_ROW:, :].astype(BF16), w_ref[...].astype(BF16), _NT)

    return pl.pallas_call(
        body, name="ada_bwd", grid=(2,),
        in_specs=[pl.BlockSpec((ADA_ROWS, D), lambda i: (0, 0)), pl.BlockSpec((None, ADA_ROWS, ADA_COLS), lambda i: (i, 0, 0)),
                  pl.BlockSpec((None, ADA_ROWS, 6 * D), lambda i: (i, 0, 0)), pl.BlockSpec((None, D, ADA_COLS), lambda i: (i, 0, 0))],
        out_specs=[pl.BlockSpec((None, D, ADA_COLS), lambda i: (i, 0, 0)), pl.BlockSpec((None, 1, 6 * D), lambda i: (i, 0, 0)),
                   pl.BlockSpec((ADA_ROWS - ADA_CTX_ROW, D), lambda i: (0, 0))],
        out_shape=[jax.ShapeDtypeStruct((2, D, ADA_COLS), F32), jax.ShapeDtypeStruct((2, 1, 6 * D), F32),
                   jax.ShapeDtypeStruct((ADA_ROWS - ADA_CTX_ROW, D), F32)],
        compiler_params=pltpu.CompilerParams(dimension_semantics=("arbitrary",)),
    )(cond, dm_mine, dm_full, w)


def cctx_grad(parts, c_ctx):
    def body(p_ref, c_ref, o_ref):
        tot = p_ref[0:1, :]
        for i in range(1, N_DEV):
            tot = tot + p_ref[i:i + 1, :]
        c = c_ref[...]
        sg = jax.nn.sigmoid(c)
        o_ref[...] = tot * sg * (1.0 + c * (1.0 - sg))

    return pl.pallas_call(body, name="cctx_grad", out_shape=jax.ShapeDtypeStruct((1, D), F32))(parts, c_ctx)


def _row_tile(r):
    for t in (512, 256, 128, 80, 64, 40, 32, 16, 8):
        if r % t == 0:
            return t
    return r


def _slot_sum(ref):
    tot = ref[0].astype(F32)
    for i in range(1, ref.shape[0]):
        tot = tot + ref[i].astype(F32)
    return tot


def sum_slots(name, x):
    s, r, c = x.shape
    tr = _row_tile(r)

    def body(x_ref, o_ref):
        o_ref[...] = _slot_sum(x_ref)

    return pl.pallas_call(
        body, name=name, grid=(r // tr,), in_specs=[pl.BlockSpec((s, tr, c), lambda i: (0, i, 0))],
        out_specs=pl.BlockSpec((tr, c), lambda i: (i, 0)), out_shape=jax.ShapeDtypeStruct((r, c), F32),
    )(x)


def adamw(name, w, g, m, v, layer=None):
    r, c = w.shape[-2:]
    tr = _row_tile(r)
    stacked = g.ndim == 3

    def body(w_ref, g_ref, m_ref, v_ref, go_ref, d_ref, mo_ref, vo_ref):
        gv = _slot_sum(g_ref) if stacked else g_ref[...]
        mn = B1 * m_ref[...] + (1.0 - B1) * gv
        vn = B2 * v_ref[...] + (1.0 - B2) * jnp.square(gv)
        m_hat = mn / (1.0 - B1 ** STEP)
        v_hat = vn / (1.0 - B2 ** STEP)
        go_ref[...] = gv
        d_ref[...] = -LR * (m_hat / (jnp.sqrt(v_hat) + AEPS) + WD * w_ref[...])
        mo_ref[...] = mn
        vo_ref[...] = vn

    tile = pl.BlockSpec((tr, c), lambda i: (i, 0))
    slab = tile if layer is None else pl.BlockSpec((None, tr, c), lambda i: (layer, i, 0))
    g_spec = pl.BlockSpec((g.shape[0], tr, c), lambda i: (0, i, 0)) if stacked else tile
    return pl.pallas_call(
        body, name=name, grid=(r // tr,), in_specs=[slab, g_spec, slab, slab], out_specs=[tile] * 4,
        out_shape=[jax.ShapeDtypeStruct((r, c), F32)] * 4,
    )(w, g, m, v)


def _place():
    return lax.axis_index("x"), lax.axis_index("y"), lax.axis_index("c")


def all_gather(name, x, in_vmem):
    r, c = x.shape
    space = pltpu.VMEM if in_vmem else pl.ANY

    def body(x_ref, out_ref, send_sems, recv_sems, local_sem):
        px, py, pc = _place()
        me, sibling = (px, py, pc), (px, py, 1 - pc)
        chips = [(1 - px, py), (px, 1 - py), (1 - px, 1 - py)]

        def rows(qx, qy, qc):
            return out_ref.at[pl.ds((4 * qx + 2 * qy + qc) * r, r), :]

        def copy(k, block, to, src=None):
            return pltpu.make_async_remote_copy(
                src_ref=rows(*block) if src is None else src, dst_ref=rows(*block),
                send_sem=send_sems.at[k], recv_sem=recv_sems.at[k], device_id=to, device_id_type=MESH)

        mine = pltpu.make_async_copy(x_ref, rows(*me), local_sem)
        mine.start()
        first = [copy(0, me, sibling, src=x_ref)]
        first += [copy(1 + j, me, (*chip, pc), src=x_ref) for j, chip in enumerate(chips)]
        for cp in first:
            cp.start()
        passed = [copy(4 + j, (*chip, pc), sibling) for j, chip in enumerate(chips)]
        for j, chip in enumerate(chips):
            copy(1 + j, (*chip, pc), me).wait_recv()
            passed[j].start()
        copy(0, sibling, me).wait_recv()
        for j, chip in enumerate(chips):
            copy(4 + j, (*chip, 1 - pc), me).wait_recv()
        for cp in first + passed:
            cp.wait_send()
        mine.wait()

    return pl.pallas_call(
        body, name=name, out_shape=jax.ShapeDtypeStruct((N_DEV * r, c), x.dtype),
        in_specs=[pl.BlockSpec(memory_space=space)], out_specs=pl.BlockSpec(memory_space=space),
        scratch_shapes=[pltpu.SemaphoreType.DMA((7,)), pltpu.SemaphoreType.DMA((7,)), pltpu.SemaphoreType.DMA],
    )(x)


_HBM =pl.BlockSpec(memory_space=pltpu.HBM)
_SEM = pl.BlockSpec(memory_space=pltpu.SEMAPHORE)
_EFFECT = pltpu.SideEffectType.DATAFLOW_SIDE_EFFECTING


def _peers():
    px, py, pc = _place()
    return [(1 - px if k & 4 else px, 1 - py if k & 2 else py, 1 - pc if k & 1 else pc) for k in range(1, N_DEV)]


def _slot(dev):
    return 4 * dev[0] + 2 * dev[1] + dev[2]


def _split_copies(src_refs, land_refs, send_sems, recv_sems, gather):
    me = _slot(_place())
    return [pltpu.make_async_remote_copy(
        src_ref=src if gather else src.at[_slot(peer)], dst_ref=land.at[me],
        send_sem=send_sems.at[a * (N_DEV - 1) + k], recv_sem=recv_sems.at[a * (N_DEV - 1) + k],
        device_id=peer, device_id_type=MESH)
        for a, (src, land) in enumerate(zip(src_refs, land_refs)) for k, peer in enumerate(_peers())]


def exchange_start(name, srcs, gather, after):
    n = len(srcs)
    lands = [pltpu.HBM((N_DEV,) + s.shape if gather else s.shape, s.dtype) for s in srcs]

    def body(*refs):
        send_sems, recv_sems = refs[2 * n + 1:2 * n + 3]
        for cp in _split_copies(refs[:n], refs[n:2 * n], send_sems, recv_sems, gather):
            cp.start()
        refs[-1][...] = jnp.zeros_like(refs[-1])

    sems = pltpu.SemaphoreType.DMA((n * (N_DEV - 1),))
    res = pl.pallas_call(
        body, name=name,
        out_shape=(sems, sems, *[pltpu.HBM(s.shape, s.dtype) for s in srcs], *lands, jax.ShapeDtypeStruct((8, 128), F32)),
        in_specs=(_HBM,) * (2 * n) + (pl.BlockSpec(memory_space=pl.ANY),),
        out_specs=(_SEM, _SEM) + (_HBM,) * (2 * n) + (pl.BlockSpec(memory_space=pltpu.VMEM),),
        input_output_aliases={i: 2 + i for i in range(2 * n)},
        compiler_params=pltpu.CompilerParams(has_side_effects=_EFFECT),
    )(*[pltpu.with_memory_space_constraint(s, pltpu.HBM) for s in srcs],
      *[pltpu.with_memory_space_constraint(lax.empty(ld.shape, ld.dtype), pltpu.HBM) for ld in lands], after)
    return res[0], res[1], list(res[2:2 + n]), list(res[2 + n:2 + 2 * n]), res[-1]


def exchange_wait(name, started, after, gather):
    send_sems, recv_sems, srcs, lands, _ = started
    n = len(srcs)

    def body(*refs):
        send_sems, recv_sems = refs[2 * n:2 * n + 2]
        for cp in _split_copies(refs[:n], refs[n:2 * n], send_sems, recv_sems, gather):
            cp.wait_send()
            cp.wait_recv()

    res = pl.pallas_call(
        body, name=name, out_shape=tuple(pltpu.HBM(a.shape, a.dtype) for a in srcs + lands),
        in_specs=(_HBM,) * (2 * n) + (_SEM, _SEM, pl.BlockSpec(memory_space=pl.ANY)), out_specs=(_HBM,) * (2 * n),
        input_output_aliases={i: i for i in range(2 * n)},
        compiler_params=pltpu.CompilerParams(has_side_effects=_EFFECT),
    )(*srcs, *lands, send_sems, recv_sems, after)
    return list(res[:n]), list(res[n:])


NCF = FFN_H // FFN_TC


def _size(shape):
    n = 1
    for s in shape:
        n *= s
    return n


def _padded_rows(n_elems, row_mult):
    return -(-n_elems // (D * row_mult)) * row_mult


def _pack_rows(arrs, dtype, row_mult):
    rows, offs, r0 = [], [], 0
    for a in arrs:
        flat = a.reshape(-1).astype(dtype)
        n = _padded_rows(flat.shape[0], row_mult)
        rows.append(jnp.pad(flat, (0, n * D - flat.shape[0])).reshape(n, D))
        offs.append(r0)
        r0 += n
    return jnp.concatenate(rows, 0), offs


def _unpack_rows(buf, offs, shapes):
    lead, out = buf.shape[:-2], []
    for o, shp in zip(offs, shapes):
        n = _size(shp)
        nr = -(-n // D)
        out.append(buf[..., o:o + nr, :].reshape(lead + (nr * D,))[..., :n].reshape(lead + tuple(shp)))
    return out


def _cols_from_shards(g):
    return g.transpose(1, 0, 2).reshape(g.shape[1], N_DEV * g.shape[2])


def _cols_to_shards(w):
    k, n = w.shape[0], w.shape[1] // N_DEV
    return w.reshape(k, N_DEV, n).transpose(1, 0, 2)


def _rows3(w):
    return [w[i:i + 1] for i in range(3)]


def f_mod1(xs, ps):
    return f_mod(xs, ps)[:1]


def kernel(x, c, ctx, c_ctx, ada_w, ada_b, norm_mix, norm_ffn, gla_w_in, gla_w_a2, gla_b_a, gla_head_norm, gla_w_out, sc_w_in, sc_conv_w, sc_w_out, ffn_w_up, ffn_conv_w, ffn_conv_b, ffn_w_down, final_norm, loss_target, m_c_ctx, m_ada_w, m_ada_b, m_norm_mix, m_norm_ffn, m_gla_w_in, m_gla_w_a2, m_gla_b_a, m_gla_head_norm, m_gla_w_out, m_sc_w_in, m_sc_conv_w, m_sc_w_out, m_ffn_w_up, m_ffn_conv_w, m_ffn_conv_b, m_ffn_w_down, m_final_norm, v_c_ctx, v_ada_w, v_ada_b, v_norm_mix, v_norm_ffn, v_gla_w_in, v_gla_w_a2, v_gla_b_a, v_gla_head_norm, v_gla_w_out, v_sc_w_in, v_sc_conv_w, v_sc_w_out, v_ffn_w_up, v_ffn_conv_w, v_ffn_conv_b, v_ffn_w_down, v_final_norm):
    names = ["c_ctx", "ada_w", "ada_b", "norm_mix", "norm_ffn", "gla_w_in", "gla_w_a2", "gla_b_a", "gla_head_norm",
             "gla_w_out", "sc_w_in", "sc_conv_w", "sc_w_out", "ffn_w_up", "ffn_conv_w", "ffn_conv_b", "ffn_w_down",
             "final_norm"]
    w_ = dict(zip(names, [c_ctx, ada_w, ada_b, norm_mix, norm_ffn, gla_w_in, gla_w_a2, gla_b_a, gla_head_norm, gla_w_out,
                          sc_w_in, sc_conv_w, sc_w_out, ffn_w_up, ffn_conv_w, ffn_conv_b, ffn_w_down, final_norm]))
    m_ = dict(zip(names, [m_c_ctx, m_ada_w, m_ada_b, m_norm_mix, m_norm_ffn, m_gla_w_in, m_gla_w_a2, m_gla_b_a,
                          m_gla_head_norm, m_gla_w_out, m_sc_w_in, m_sc_conv_w, m_sc_w_out, m_ffn_w_up, m_ffn_conv_w,
                          m_ffn_conv_b, m_ffn_w_down, m_final_norm]))
    v_ = dict(zip(names, [v_c_ctx, v_ada_w, v_ada_b, v_norm_mix, v_norm_ffn, v_gla_w_in, v_gla_w_a2, v_gla_b_a,
                          v_gla_head_norm, v_gla_w_out, v_sc_w_in, v_sc_conv_w, v_sc_w_out, v_ffn_w_up, v_ffn_conv_w,
                          v_ffn_conv_b, v_ffn_w_down, v_final_norm]))
    me = 4 * lax.axis_index("x") + 2 * lax.axis_index("y") + lax.axis_index("c")
    bsz = x.shape[0]
    tm = 256
    nt = SEQ // tm
    ctx_tiles = CTX // tm
    pe = functools.partial(P, per_example=True)

    groups = {"gla": [("gla_w_in", 0), ("gla_w_out", 0)], "ffn0": [("ffn_w_up", 0), ("ffn_w_down", 0)],
              "l1": [("sc_w_in", 0), ("sc_w_out", 0), ("ffn_w_up", 1), ("ffn_w_down", 1)]}
    ag_started = {}

    def start_gather(g, after):
        ag_started[g] = exchange_start(f"ag_{g}_start", [w_[n][i].astype(BF16) for n, i in groups[g]], True, after)
        return ag_started[g][4]

    small_sharded = [c, gla_w_a2, gla_b_a, sc_conv_w, ffn_conv_w]
    pack0, offs0 = _pack_rows(small_sharded, F32, 8)
    g0 = all_gather("ag_small", pack0, True).reshape(N_DEV, pack0.shape[0], D)
    c_all, wa2_s, ba_s, scw_s, fcw_s = _unpack_rows(g0, offs0, [a.shape for a in small_sharded])
    w_a2 = wa2_s[:, 0].transpose(1, 2, 0, 3).reshape(2, RANK, KD)
    b_a = ba_s[:, 0].transpose(1, 0, 2).reshape(2, KD)
    sc_cw = scw_s[:, 0].transpose(1, 0, 2).reshape(3, D)
    ffn_cw = fcw_s.transpose(1, 2, 0, 3).reshape(2, 3, 2 * FFN_H)

    cond = jnp.concatenate([c_all.reshape(N_DEV * bsz, D), c_ctx[None], jnp.zeros((ADA_ROWS - N_DEV * bsz - 1, D), F32)], 0)
    b_mine = lax.dynamic_slice(ada_b, (0, me * ADA_COLS), (2, ADA_COLS)).reshape(2, 1, ADA_COLS)
    mod_part = ada_fwd(cond, ada_w, b_mine)
    mod = all_gather("ag_mod", mod_part.reshape(2 * ADA_ROWS, ADA_COLS), True)
    mod = mod.reshape(N_DEV, 2, ADA_ROWS, ADA_COLS).transpose(1, 2, 0, 3).reshape(2, ADA_ROWS, 6 * D)
    mods = lax.dynamic_slice(mod, (0, bsz * me, 0), (2, bsz, 6 * D))
    md = [[mods[i][:, k * D:(k + 1) * D].reshape(bsz, 1, D) for k in range(6)] for i in range(2)]
    mc = [mod[0, ADA_CTX_ROW, k * D:(k + 1) * D][None] for k in range(2)]

    tok = mod
    for g in groups:
        tok = start_gather(g, tok)
    norm_mix = norm_mix + tok[0, 0]

    def gathered(g, after):
        mine, lands = exchange_wait(f"ag_{g}_wait", ag_started[g], after, True)
        return [lax.dynamic_update_index_in_dim(ld, mn, me, 0) for ld, mn in zip(lands, mine)]

    s_up, w_down = [None, None], [None, None]
    wd = jnp.zeros((128, 2 * KD), F32).at[:RANK, :KD].set(w_a2[0]).at[RANK:2 * RANK, KD:].set(w_a2[1])
    bd = b_a.reshape(1, 2 * KD)
    scw = _rows3(sc_cw)
    head_gain = gla_head_norm.reshape(1, HV)
    gains_mix = [norm_mix[i][None] for i in range(2)]
    gains_ffn = [norm_ffn[i][None] for i in range(2)]

    def tokens(a2d, t_len):
        return a2d.reshape(bsz, t_len, -1)

    def ffn_params(i):
        rows = [ffn_cw[i][t] for t in range(3)] + [ffn_conv_b[i]]
        return [P(a.reshape(2, FFN_H), w=FFN_TC, rows=True) for a in rows]

    def ffn_fwd(i, hn2):
        u = mm(f"ffn_up{i}", V(hn2, "tok"), V(s_up[i], "cols"), out="planes", out_dtype=BF16, planes_t=SEQ)
        act = rowwise(f"ffn_mid{i}", f_ffn_mid, [X(u, w=FFN_TC, planes=True)], ffn_params(i), tm=SEQ, nt=1, nc=NCF,
                      outs=[(FFN_TC, BF16, 1)])[0]
        return u, act, tokens(mm(f"ffn_down{i}", V(act, "tok"), V(w_down[i])), SEQ)

    def res_mod_fwd(name, h, y, ps):
        return rowwise(name, f_res_mod, [X(h), X(y)], ps, tm=tm, nt=nt, outs=[(D, F32, 1), (D, BF16, 1)])

    ps_in0 = [P(gains_mix[0]), pe(md[0][0]), pe(md[0][1])]
    ps_ctx = [P(gains_mix[0]), P(mc[0]), P(mc[1])]
    hn0 = rowwise("mod_in0", f_mod, [X(x)], ps_in0, tm=tm, nt=nt, outs=[(D, BF16, 1)])[0]
    hnc = rowwise("mod_ctx", f_mod, [X(ctx)], ps_ctx, tm=tm, nt=ctx_tiles, outs=[(D, BF16, 1)])[0]
    hcat = jnp.concatenate([hnc, hn0], axis=1)
    s_gin, s_gout = gathered("gla", hcat)
    w_gin = jnp.pad(_cols_from_shards(s_gin), ((0, 0), (0, GLA_IN_PAD - GLA_IN)))
    w_gout = s_gout.reshape(VD, D)
    pcat = tokens(mm("gla_in", V(hcat, "tok"), V(w_gin)), TT)
    pa_x = X(pcat, w=128, co=(GLA_IN_PAD - 128) // 128)
    la = rowwise("gla_decay", f_decay, [pa_x], [P(wd), P(bd)], tm=tm, nt=TT // tm, outs=[(2 * KD, F32, 1)])[0]
    o2, s_all = gla_fwd(pcat, la)
    post_xs = [X(o2, w=VD, co=0, ro=ctx_tiles, split=HEADS), X(o2, w=VD, co=1, ro=ctx_tiles, split=HEADS),
               X(pcat, w=VD, co=2, ro=ctx_tiles, split=HEADS)]
    yin0 = rowwise("gla_post", f_gla_post, post_xs, [P(head_gain)], tm=tm, nt=nt, outs=[(VD, BF16, HEADS)])[0]
    y0 = tokens(mm("gla_out", V(yin0, "tok"), V(w_gout)), SEQ)
    ps_mid0 = [pe(md[0][2]), P(gains_ffn[0]), pe(md[0][3]), pe(md[0][4])]
    h1_0, hn2_0 = res_mod_fwd("res_mod_mid0", x, y0, ps_mid0)
    s_up[0], s_down0 = gathered("ffn0", hn2_0)
    w_down[0] = s_down0.reshape(FFN_H, D)
    u0, act0, fo0 = ffn_fwd(0, hn2_0)
    ps_in1 = [pe(md[0][5]), P(gains_mix[1]), pe(md[1][0]), pe(md[1][1])]
    h2_0, hn1 = res_mod_fwd("res_mod_in1", h1_0, fo0, ps_in1)

    s_sin, s_sout, s_up[1], s_down1 = gathered("l1", hn1)
    w_sout, w_down[1] = s_sout.reshape(D, D), s_down1.reshape(FFN_H, D)
    p1 = tokens(mm("sc_in", V(hn1, "tok"), V(s_sin, "cols")), SEQ)
    sc_ps = [P(a) for a in scw]
    yin1 = rowwise("sc_mid", f_sc_mid, [X(p1, split=3)], sc_ps, tm=tm, nt=nt, outs=[(D, BF16, 1)])[0]
    y1 = tokens(mm("sc_out", V(yin1, "tok"), V(w_sout)), SEQ)
    ps_mid1 = [pe(md[1][2]), P(gains_ffn[1]), pe(md[1][3]), pe(md[1][4])]
    h1_1, hn2_1 = res_mod_fwd("res_mod_mid1", h2_0, y1, ps_mid1)
    u1, act1, fo1 = ffn_fwd(1, hn2_1)
    loss8, dh1_1, dfo1, dm5_1, g_final = final_loss(h1_1, fo1, md[1][5], final_norm[None], loss_target)

    def ffn_bwd(i, u, act, hn2, dfo):
        dact = tokens(mm(f"ffn_down_dx{i}", V(dfo, "tok"), V(w_down[i]), form="nt", out_dtype=BF16), SEQ)
        g_down = mm(f"ffn_down_dw{i}", V(act, "tok"), V(dfo, "tok"), form="tn", out_dtype=BF16)
        r = rowwise(f"ffn_mid_bwd{i}", f_ffn_mid, [X(u, w=FFN_TC, planes=True)], ffn_params(i), tm=SEQ, nt=1, nc=NCF,
                    douts=[X(dact, w=FFN_TC)], dx={0: BF16}, dp=[0, 1, 2, 3])
        du, g_cw, g_cb = r[0], jnp.stack([a.reshape(2 * FFN_H) for a in r[1:4]]), r[4].reshape(1, 2 * FFN_H)
        dhn2 = tokens(mm(f"ffn_up_dx{i}", V(du, "planes"), V(s_up[i], "cols"), form="nt", out_dtype=BF16), SEQ)
        g_up = mm(f"ffn_up_dw{i}", V(hn2, "tok"), V(du, "planes"), form="tn", out="cols", out_dtype=BF16)
        return dhn2, g_up, row_slots(g_down), g_cw, g_cb

    def res_mod_bwd(name, h, y, ps, dh1, dhn):
        return rowwise(name, f_res_mod, [X(h), X(y)], ps, tm=tm, nt=nt, douts=[X(dh1), X(dhn)],
                       dx={0: F32, 1: BF16}, dp=[0, 1, 2, 3])

    def row_slots(g):
        return g.reshape(N_DEV, -1, g.shape[-1])

    a2a_started = {}

    def send_grads(g, slots, after=None):
        a2a_started[g] = exchange_start(f"a2a_{g}_start", list(slots), False, loss8 if after is None else after)
        return a2a_started[g][4][0, 0]

    def after_start(ps, tok):
        return [dict(ps[0], a=ps[0]["a"] + tok)] + ps[1:]

    dhn2_1, g_up1, g_down1, g_fcw1, g_fcb1 = ffn_bwd(1, u1, act1, hn2_1, dfo1)
    dh2_0, dy1, dm2_1, g_nffn1, dm3_1, dm4_1 = res_mod_bwd("res_mod_mid1_bwd", h2_0, y1, ps_mid1, dh1_1, dhn2_1)
    dyin1 = tokens(mm("sc_out_dx", V(dy1, "tok"), V(w_sout), form="nt", out_dtype=BF16), SEQ)
    g_sout = row_slots(mm("sc_out_dw", V(yin1, "tok"), V(dy1, "tok"), form="tn", out_dtype=BF16))
    r = rowwise("sc_mid_bwd", f_sc_mid, [X(p1, split=3)], sc_ps, tm=tm, nt=nt, douts=[X(dyin1)], dx={0: BF16}, dp=[0, 1, 2])
    dp1, g_scw = r[0], jnp.concatenate(r[1:4], 0)
    dhn1 = tokens(mm("sc_in_dx", V(dp1, "tok"), V(s_sin, "cols"), form="nt", out_dtype=BF16), SEQ)
    g_sin = mm("sc_in_dw", V(hn1, "tok"), V(dp1, "tok"), form="tn", out="cols", out_dtype=BF16)
    tok = send_grads("l1", [g_sin, g_sout, g_up1, g_down1])
    dh1_0, dfo0, dm5_0, g_nmix1, dm0_1, dm1_1 = res_mod_bwd("res_mod_in1_bwd", h1_0, fo0, after_start(ps_in1, tok), dh2_0, dhn1)

    dhn2_0, g_up0, g_down0, g_fcw0, g_fcb0 = ffn_bwd(0, u0, act0, hn2_0, dfo0)
    tok = send_grads("ffn0", [g_up0, g_down0])
    dx_res, dy0, dm2_0, g_nffn0, dm3_0, dm4_0 = res_mod_bwd("res_mod_mid0_bwd", x, y0, after_start(ps_mid0, tok), dh1_0, dhn2_0)
    dyin0 = tokens(mm("gla_out_dx", V(dy0, "tok"), V(w_gout), form="nt", out_dtype=BF16), SEQ)
    g_gout = row_slots(mm("gla_out_dw", V(yin0, "tok"), V(dy0, "tok"), form="tn", out_dtype=BF16))
    do, dgate, g_head = rowwise("gla_post_bwd", f_gla_post, post_xs, [P(head_gain)], tm=tm, nt=nt,
                                douts=[X(dyin0, split=HEADS)], dx={0: F32, 2: BF16}, dp=[0])
    dq2, dk2, dv2, dla = gla_bwd(pcat, la, s_all, do)
    dpa, g_wd, g_bd = rowwise("gla_decay_bwd", f_decay, [pa_x], [P(wd), P(bd)], tm=tm, nt=TT // tm, douts=[X(dla)],
                              dx={0: BF16}, dp=[0, 1])
    dpcat = gla_combine(dq2, dk2, dv2, dgate, dpa)
    dhcat = tokens(mm("gla_in_dx", V(dpcat, "tok"), V(w_gin), form="nt", out_dtype=BF16), TT)
    g_gin = _cols_to_shards(mm("gla_in_dw", V(hcat, "tok"), V(dpcat, "tok"), form="tn", out_dtype=BF16)[:, :GLA_IN])
    grad_x, g_nmix0, dm0_0, dm1_0 = rowwise("mod_in0_bwd", f_mod, [X(x)], ps_in0, tm=tm, nt=nt,
                                            douts=[X(dhcat, ro=ctx_tiles), X(dx_res)], dx={0: F32}, dp=[0, 1, 2])
    g_nmix0c, dmc0, dmc1 = rowwise("mod_ctx_bwd", f_mod1, [X(ctx)], ps_ctx, tm=tm, nt=ctx_tiles, douts=[X(dhcat)],
                                   dx={}, dp=[0, 1, 2])

    zero_row = jnp.zeros((1, 4 * D), F32)
    dmod = [jnp.concatenate([jnp.concatenate([a.reshape(bsz, D) for a in dms], 1), ctx_row], 0)
            for dms, ctx_row in (([dm0_0, dm1_0, dm2_0, dm3_0, dm4_0, dm5_0], jnp.concatenate([dmc0, dmc1, zero_row], 1)),
                                 ([dm0_1, dm1_1, dm2_1, dm3_1, dm4_1, dm5_1], jnp.zeros((1, 6 * D), F32)))]
    g_wa2 = jnp.stack([g_wd[:RANK, :KD], g_wd[RANK:2 * RANK, KD:]])
    small_grads = [jnp.stack(dmod), jnp.concatenate([g_nmix0 + g_nmix0c, g_nmix1], 0), jnp.concatenate([g_nffn0, g_nffn1], 0),
                   g_head, jnp.concatenate([g_fcb0, g_fcb1], 0), g_final, g_wa2, g_bd.reshape(2, KD), g_scw,
                   jnp.stack([g_fcw0, g_fcw1]), loss8[:1]]
    pack1, offs1 = _pack_rows(small_grads, F32, 8)
    g1 = all_gather("ag_grads", pack1, True).reshape(N_DEV, pack1.shape[0], D)
    dmod_all = _unpack_rows(g1, offs1[:1], [small_grads[0].shape])[0]
    tot = _unpack_rows(sum_slots("sum_small", g1), offs1, [a.shape for a in small_grads])
    loss = tot[10][0, 0]
    dm_rows = dmod_all[:, :, :bsz].transpose(1, 0, 2, 3).reshape(2, N_DEV * bsz, 6 * D)
    dm_full = jnp.concatenate([dm_rows, tot[0][:, bsz:], jnp.zeros((2, ADA_ROWS - N_DEV * bsz - 1, 6 * D), F32)], 1)
    dm_mine = lax.dynamic_slice(dm_full, (0, 0, me * ADA_COLS), (2, ADA_ROWS, ADA_COLS))
    g_ada_w, g_ada_b, cpart = ada_bwd(cond, dm_mine, dm_full, ada_w)
    cparts = all_gather("ag_cctx", cpart, True).reshape(N_DEV, ADA_ROWS - ADA_CTX_ROW, D)[:, 0]
    g_cctx = cctx_grad(cparts, c_ctx[None])[0]
    tok = send_grads("gla", [g_gin, g_gout], after=g_cctx)

    def my_cols(full, n):
        return lax.dynamic_slice_in_dim(full, me * n, n, axis=full.ndim - 1)

    grads = {
        "c_ctx": g_cctx, "ada_b": g_ada_b.reshape(2, 6 * D), "norm_mix": tot[1], "norm_ffn": tot[2],
        "gla_head_norm": tot[3], "ffn_conv_b": tot[4], "final_norm": tot[5].reshape(D),
        "gla_w_a2": my_cols(tot[6], KD // N_DEV)[None], "gla_b_a": my_cols(tot[7], KD // N_DEV)[None],
        "sc_conv_w": my_cols(tot[8], D // N_DEV)[None], "ffn_conv_w": my_cols(tot[9], 2 * FFN_H // N_DEV),
    }

    res_ada = adamw("adamw_ada", *[a.reshape(2 * D, ADA_COLS) for a in (ada_w, g_ada_w, m_ada_w, v_ada_w)])
    grads["c_ctx"] = g_cctx + tok
    big = ["gla_w_in", "gla_w_out", "sc_w_in", "sc_w_out", "ffn_w_up", "ffn_w_down"]
    small = [n for n in names if n not in big and n != "ada_w"]
    g_small = _pack_rows([grads[n] for n in small], F32, 8)[0]
    res_small = adamw("adamw_small", _pack_rows([w_[n] for n in small], F32, 8)[0], g_small,
                      _pack_rows([m_[n] for n in small], F32, 8)[0], _pack_rows([v_[n] for n in small], F32, 8)[0])
    offs_s = _pack_rows([w_[n] for n in small], F32, 8)[1]

    big_res, done = {}, [res_small[0][0, 0], res_ada[0][0, 0]]
    for g in ("l1", "ffn0", "gla"):
        sent, lands = exchange_wait(f"a2a_{g}_wait", a2a_started[g], jnp.stack(done), False)
        for (n, i), mine, land in zip(groups[g], sent, lands):
            land = lax.dynamic_update_index_in_dim(land, lax.dynamic_index_in_dim(mine, me, 0, keepdims=False), me, 0)
            big_res[(n, i)] = adamw(f"adamw_{n}{i}", w_[n], land, m_[n], v_[n], layer=i)
            done.append(big_res[(n, i)][0][0, 0])

    out = {}
    for kind, idx in (("grad", 0), ("delta", 1), ("new_m", 2), ("new_v", 3)):
        vals = {n: jnp.stack([big_res[(n, i)][idx] for i in range(w_[n].shape[0])]) for n in big}
        vals["ada_w"] = res_ada[idx].reshape(ada_w.shape)
        vals.update(zip(small, _unpack_rows(res_small[idx], offs_s, [w_[n].shape for n in small])))
        out[kind] = [vals[n] for n in names]
    return (loss, grad_x, *out["grad"], *out["delta"], *out["new_m"], *out["new_v"])
```

```python
import functools

import jax
import jax.numpy as jnp
from jax import lax
from jax.experimental import pallas as pl
from jax.experimental.pallas import tpu as pltpu

F32 = jnp.float32
BF16 = jnp.bfloat16

N_DEV = 8
D = 1024
SEQ = 2048
CTX = 256
TT = CTX + SEQ
GRID_W = 64
CHUNK = 64
HEADS = 4
HK = 128
HV = 256
KD = 512
VD = 1024
RANK = 16
TAU = 16.0
GLA_IN = 3104
GLA_IN_PAD = 3200
FFN_H = 2560
FFN_TC = 256
EPS = 1e-6
LR, B1, B2, AEPS, WD, STEP = 0.001, 0.9, 0.999, 1e-08, 0.01, 10
MESH = pl.DeviceIdType.MESH


def _blocks(n):
    return [n] + [t for t in range(n - n % 128, 0, -128) if n % t == 0 and t != n]


def V(arr, kind="flat", width=None):
    if kind == "tok":
        return V(arr.reshape(-1, arr.shape[-1]))
    if kind == "flat":
        r, c = arr.shape
        return dict(a=arr, kind=kind, shape=(r, c), rows=_blocks(r), cols=_blocks(c))
    if kind == "planes":
        bsz, _, t, ch = arr.shape
        return dict(a=arr, kind=kind, shape=(bsz * t, 2 * ch), rows=_blocks(t), cols=[2 * ch] + _blocks(ch), t=t, ch=ch)
    _, r, n = arr.shape
    if width is not None:
        return dict(a=arr, kind=kind, shape=(r, width), rows=_blocks(r), cols=[width], n=n, pad=width - N_DEV * n)
    return dict(a=arr, kind=kind, shape=(r, N_DEV * n), rows=_blocks(r), cols=[8 * n, 4 * n, 2 * n], n=n, pad=0)


def _view_spec(v, br, bc, idx):
    if v["kind"] == "flat":
        return pl.BlockSpec((br, bc), idx)
    if v["kind"] == "planes":
        nt = v["t"] // br
        if bc == 2 * v["ch"]:
            return pl.BlockSpec((None, 2, br, v["ch"]), lambda i, j, k: (idx(i, j, k)[0] // nt, 0, idx(i, j, k)[0] % nt, 0))
        nch = v["ch"] // bc

        def at(i, j, k):
            r, c = idx(i, j, k)
            return r // nt, c // nch, r % nt, c % nch
        return pl.BlockSpec((None, None, br, bc), at)
    return pl.BlockSpec(((bc - v["pad"]) // v["n"], br, v["n"]), lambda i, j, k: (idx(i, j, k)[1], idx(i, j, k)[0], 0))


def _out_view(kind, rows, cols, dtype, planes_t=None, shard_n=None):
    if kind == "flat":
        shape = (rows, cols)
    elif kind == "planes":
        shape = (rows // planes_t, 2, planes_t, cols // 2)
    elif shard_n is not None:
        return V(jax.ShapeDtypeStruct((N_DEV, rows, shard_n), dtype), kind, width=cols)
    else:
        shape = (N_DEV, rows, cols // N_DEV)
    return V(jax.ShapeDtypeStruct(shape, dtype), kind)


MM_VMEM_BUDGET = 40 * 2 ** 20
MM_VMEM_LIMIT = 56 * 2 ** 20
MM_MAX_TILE = 1536


def _mm_tiles(m, n, kk, ms, ns, ks, a_bytes, b_bytes, o_bytes):
    best = None
    for tk in ks:
        for tm in [t for t in ms if t <= MM_MAX_TILE] or ms:
            for tn in [t for t in ns if t <= MM_MAX_TILE] or ns:
                one_k = tk == kk
                need = 2 * (tm * tk * a_bytes + tk * tn * b_bytes + tm * tn * o_bytes) + (0 if one_k else tm * tn * 4)
                if need > MM_VMEM_BUDGET:
                    continue
                steps = (m // tm) * (n // tn) * (kk // tk)
                traffic = (m * kk * a_bytes * (1 if one_k else n // tn)
                           + kk * n * b_bytes * (1 if one_k and n == tn else m // tm) + m * n * o_bytes)
                fill = (tm * tk * a_bytes + tk * tn * b_bytes) / 2.5e12
                cost = max(2.0 * m * n * kk / (9e14 if one_k else 6.5e14), traffic / 2.5e12) + steps * 0.4e-6 + fill
                if best is None or cost < best[0]:
                    best = (cost, tm, tn, tk)
    return best[1:]


def mm(name, a, b, form="nn", out="flat", out_dtype=F32, planes_t=None, shard_n=None):
    (m, kk) = a["shape"][::-1] if form == "tn" else a["shape"]
    n = b["shape"][0] if form == "nt" else b["shape"][1]
    assert (b["shape"][1] if form == "nt" else b["shape"][0]) == kk, (name, a["shape"], b["shape"])
    o = _out_view(out, m, n, out_dtype, planes_t, shard_n)
    a_m, a_k = (a["cols"], a["rows"]) if form == "tn" else (a["rows"], a["cols"])
    b_k, b_n = (b["cols"], b["rows"]) if form == "nt" else (b["rows"], b["cols"])
    tm, tn, tk = _mm_tiles(m, n, kk, [t for t in a_m if t in o["rows"]], [t for t in b_n if t in o["cols"]],
                           [t for t in a_k if t in b_k], a["a"].dtype.itemsize, b["a"].dtype.itemsize,
                           jnp.dtype(out_dtype).itemsize)
    nk = kk // tk
    dn = (((0 if form == "tn" else 1,), (1 if form == "nt" else 0,)), ((), ()))

    def load(ref, v):
        if len(ref.shape) == 3:
            pieces = [ref[p].astype(BF16) for p in range(ref.shape[0])]
            if v.get("pad"):
                pieces.append(jnp.zeros(ref.shape[1:2] + (v["pad"],), BF16))
            return jnp.concatenate(pieces, axis=-1)
        return ref[...].astype(BF16)

    def store(o_ref, val):
        val = val.astype(out_dtype)
        if len(o_ref.shape) == 3:
            w = o_ref.shape[-1]
            for p in range(o_ref.shape[0]):
                o_ref[p] = val[:, p * w:(p + 1) * w]
        else:
            o_ref[...] = val

    def body(a_ref, b_ref, o_ref, *acc):
        if nk == 1:
            store(o_ref, lax.dot_general(load(a_ref, a), load(b_ref, b), dn, preferred_element_type=F32))
            return
        k, acc_ref = pl.program_id(2), acc[0]

        @pl.when(k == 0)
        def _():
            acc_ref[...] = jnp.zeros_like(acc_ref)

        acc_ref[...] += lax.dot_general(load(a_ref, a), load(b_ref, b), dn, preferred_element_type=F32)

        @pl.when(k == nk - 1)
        def _():
            store(o_ref, acc_ref[...])

    if form == "tn":
        a_spec = _view_spec(a, tk, tm, lambda i, j, k: (k, i))
    else:
        a_spec = _view_spec(a, tm, tk, lambda i, j, k: (i, k))
    if form == "nt":
        b_spec = _view_spec(b, tn, tk, lambda i, j, k: (j, k))
    else:
        b_spec = _view_spec(b, tk, tn, lambda i, j, k: (k, j))
    return pl.pallas_call(
        body, name=name, grid=(m // tm, n // tn, nk),
        in_specs=[a_spec, b_spec], out_specs=_view_spec(o, tm, tn, lambda i, j, k: (i, j)), out_shape=o["a"],
        scratch_shapes=[pltpu.VMEM((tm, tn), F32)] if nk > 1 else [],
        compiler_params=pltpu.CompilerParams(dimension_semantics=("parallel", "parallel", "arbitrary"),
                                             vmem_limit_bytes=MM_VMEM_LIMIT),
    )(a["a"], b["a"])


def X(arr, w=None, co=0, ro=0, split=1, planes=False):
    return dict(a=arr, w=arr.shape[-1] if w is None else w, co=co, ro=ro, split=2 if planes else split,
                mode="planes" if planes else "cols")


def P(arr, per_example=False, w=None, split=1, rows=False):
    return dict(a=arr, e=per_example, w=arr.shape[-1] if w is None else w, split=arr.shape[-2] if rows else split,
                mode="rows" if rows else "cols")


def _pieces(ref, s):
    if s["mode"] == "planes":
        return [ref[0], ref[1]]
    if s["mode"] == "rows":
        return [ref[i:i + 1, :] for i in range(s["split"])]
    w = ref.shape[-1] // s["split"]
    return [ref[:, i * w:(i + 1) * w] for i in range(s["split"])]


def _store(ref, pieces, s, accumulate=False):
    w = ref.shape[-1] // len(pieces)
    for i, p in enumerate(pieces):
        at = (i,) if s["mode"] == "planes" else (slice(i, i + 1),) if s["mode"] == "rows" else (slice(None), slice(i * w, (i + 1) * w))
        if accumulate:
            ref[at] += p.astype(ref.dtype)
        else:
            ref[at] = p.astype(ref.dtype)


def rowwise(name, f, xs, ps, *, tm, nt, nc=1, outs=None, douts=None, dx=None, dp=None):
    bsz = xs[0]["a"].shape[0]
    fwd = douts is None
    nx, np_ = len(xs), len(ps)
    douts = [] if fwd else douts
    dx = {} if fwd else dx
    dp = [] if fwd else dp

    def x_spec(s):
        if s["mode"] == "planes":
            return pl.BlockSpec((None, 2, tm, s["w"]), lambda c, b, t, s=s: (b, 0, t + s["ro"], c + s["co"]))
        return pl.BlockSpec((None, tm, s["w"]), lambda c, b, t, s=s: (b, t + s["ro"], c + s["co"]))

    def x_out(s, dt):
        if s["mode"] == "planes":
            return (jax.ShapeDtypeStruct((bsz, 2, nt * tm, nc * s["w"]), dt),
                    pl.BlockSpec((None, 2, tm, s["w"]), lambda c, b, t: (b, 0, t, c)))
        return (jax.ShapeDtypeStruct((bsz, nt * tm, nc * s["w"]), dt), pl.BlockSpec((None, tm, s["w"]), lambda c, b, t: (b, t, c)))

    def p_spec(s):
        r = s["a"].shape[-2]
        if s["e"]:
            return pl.BlockSpec((None, r, s["w"]), lambda c, b, t: (b, 0, c))
        return pl.BlockSpec((r, s["w"]), lambda c, b, t: (0, c))

    in_specs = [x_spec(s) for s in xs] + [p_spec(s) for s in ps] + [x_spec(s) for s in douts]
    operands = [s["a"] for s in xs] + [s["a"] for s in ps] + [s["a"] for s in douts]
    if fwd:
        out_modes = [dict(mode="cols", split=sp) for (_, _, sp) in outs]
        out_shape = [jax.ShapeDtypeStruct((bsz, nt * tm, nc * w), dt) for (w, dt, _) in outs]
        out_specs = [pl.BlockSpec((None, tm, w), lambda c, b, t: (b, t, c)) for (w, _, _) in outs]
    else:
        dx_outs = [x_out(xs[i], dt) for i, dt in dx.items()]
        out_shape, out_specs = [o[0] for o in dx_outs], [o[1] for o in dx_outs]
        for j in dp:
            s = ps[j]
            r = s["a"].shape[-2]
            if s["e"]:
                out_shape.append(jax.ShapeDtypeStruct((bsz, r, nc * s["w"]), F32))
                out_specs.append(pl.BlockSpec((None, r, s["w"]), lambda c, b, t: (b, 0, c)))
            else:
                out_shape.append(jax.ShapeDtypeStruct((r, nc * s["w"]), F32))
                out_specs.append(pl.BlockSpec((r, s["w"]), lambda c, b, t: (0, c)))

    def body(*refs):
        x_refs, p_refs = refs[:nx], refs[nx:nx + np_]
        d_refs = refs[nx + np_:nx + np_ + len(douts)]
        o_refs = refs[nx + np_ + len(douts):]
        xv = [[p.astype(F32) for p in _pieces(r, s)] for r, s in zip(x_refs, xs)]
        pv = [[p.astype(F32) for p in _pieces(r, s)] for r, s in zip(p_refs, ps)]
        if fwd:
            for r, pieces, s in zip(o_refs, f(xv, pv), out_modes):
                _store(r, pieces, s)
            return
        _, vjp = jax.vjp(f, xv, pv)
        cot = [[p.astype(F32) for p in _pieces(r, s)] for r, s in zip(d_refs, douts)]
        dxv, dpv = vjp(cot)
        for r, i in zip(o_refs, dx):
            _store(r, dxv[i], xs[i])
        b, t = pl.program_id(1), pl.program_id(2)
        for r, j in zip(o_refs[len(dx):], dp):
            first = (t == 0) if ps[j]["e"] else jnp.logical_and(b == 0, t == 0)

            @pl.when(first)
            def _(r=r, j=j):
                _store(r, dpv[j], ps[j])

            @pl.when(jnp.logical_not(first))
            def _(r=r, j=j):
                _store(r, dpv[j], ps[j], accumulate=True)

    res = pl.pallas_call(
        body, name=name, grid=(nc, bsz, nt), in_specs=in_specs, out_specs=out_specs, out_shape=out_shape,
        compiler_params=pltpu.CompilerParams(dimension_semantics=("arbitrary", "arbitrary", "arbitrary")),
    )(*operands)
    return res


def _keep_rows(a, shift, keep):
    n = a.shape[0]
    t = lax.broadcasted_iota(jnp.int32, a.shape, 0)
    return jnp.where(keep(t, n), pltpu.roll(a, shift % n, 0), 0.0)


def _shift_pair(step, keep_prev, keep_next):
    @jax.custom_vjp
    def prev(a):
        return _keep_rows(a, step, keep_prev)

    @jax.custom_vjp
    def nxt(a):
        return _keep_rows(a, -step, keep_next)

    prev.defvjp(lambda a: (prev(a), None), lambda _, g: (nxt(g),))
    nxt.defvjp(lambda a: (nxt(a), None), lambda _, g: (prev(g),))
    return prev, nxt


prev_tok, next_tok = _shift_pair(1, lambda t, n: t % GRID_W != 0, lambda t, n: t % GRID_W != GRID_W - 1)
prev_row, next_row = _shift_pair(GRID_W, lambda t, n: t >= GRID_W, lambda t, n: t < n - GRID_W)


@jax.custom_vjp
def bdot(a, w):
    return jnp.dot(a.astype(BF16), w.astype(BF16), preferred_element_type=F32)


def _bdot_bwd(res, g):
    a, w = res
    gb = g.astype(BF16)
    da = lax.dot_general(gb, w.astype(BF16), (((1,), (1,)), ((), ())), preferred_element_type=F32)
    dw = lax.dot_general(a.astype(BF16), gb, (((0,), (0,)), ((), ())), preferred_element_type=F32)
    return da, dw


bdot.defvjp(lambda a, w: (bdot(a, w), (a, w)), _bdot_bwd)


@jax.custom_vjp
def log_sigmoid(z):
    return jnp.minimum(z, 0.0) - jnp.log(1.0 + jnp.exp(-jnp.abs(z)))


def _lsig_bwd(z, g):
    e = jnp.exp(-jnp.abs(z))
    return (g * jnp.where(z >= 0, e, 1.0) / (1.0 + e),)


log_sigmoid.defvjp(lambda z: (log_sigmoid(z), z), _lsig_bwd)


def silu(x):
    return x * jax.nn.sigmoid(x)


def _rms(x):
    return x * lax.rsqrt(jnp.mean(x * x, axis=-1, keepdims=True) + EPS)


def _mod(x, gain, shift, scale):
    return _rms(x) * gain * (1.0 + scale) + shift


def f_mod(xs, ps):
    ((h,),), ((gain,), (shift,), (scale,)) = xs, ps
    return [[_mod(h, gain, shift, scale)], [h]]


def f_res_mod(xs, ps):
    ((h,), (y,)), ((gate,), (gain,), (shift,), (scale,)) = xs, ps
    h1 = h + gate * y
    return [[h1], [_mod(h1, gain, shift, scale)]]


def f_ffn_mid(xs, ps):
    ((ua, ug),), ((w0a, w0g), (w1a, w1g), (w2a, w2g), (ba, bg)) = xs, ps
    a = w0a * prev_row(ua) + w1a * ua + w2a * next_row(ua) + ba
    g = w0g * prev_row(ug) + w1g * ug + w2g * next_row(ug) + bg
    return [[a * silu(g)]]


def f_sc_mid(xs, ps):
    ((bg, cg, v),), ((w0,), (w1,), (w2,)) = xs, ps
    z = cg * v
    return [[bg * (w0 * prev_tok(z) + w1 * z + w2 * next_tok(z))]]


def f_decay(xs, ps):
    ((a,),), ((wd,), (bd,)) = xs, ps
    return [[log_sigmoid(bdot(a, wd) + bd) / TAU]]


def f_gla_post(xs, ps):
    (of, ob, g), ((gain,),) = xs, ps
    return [[_rms(a + b) * gain * silu(c) for a, b, c in zip(of, ob, g)]]


NCH = TT // CHUNK
CTX_CH = CTX // CHUNK
_NT = (((1,), (1,)), ((), ()))
_TN = (((0,), (0,)), ((), ()))
_NN = (((1,), (0,)), ((), ()))


def _chunk_of(d, j):
    return jnp.where(d == 0, j, jnp.where(j < CTX_CH, CTX_CH - 1 - j, NCH + CTX_CH - 1 - j))


def _dot(a, b, dn):
    return lax.dot_general(a, b, dn, preferred_element_type=F32)


def _mask_dot(m, g):
    g0 = g.astype(BF16)
    r1 = g - g0.astype(F32)
    g1 = r1.astype(BF16)
    g2 = (r1 - g1.astype(F32)).astype(BF16)
    return _dot(m, g0, _NN) + _dot(m, g1, _NN) + _dot(m, g2, _NN)


def _causal(d):
    row = lax.broadcasted_iota(jnp.int32, (CHUNK, CHUNK), 0)
    col = lax.broadcasted_iota(jnp.int32, (CHUNK, CHUNK), 1)
    delta = jnp.where(d == 0, col - row, row - col)
    return delta <= 0, delta >= 0


def _gla_in_specs(bsz, rev):
    def blk(d, j):
        return _chunk_of(d, (NCH - 1 - j) if rev else j)

    return [
        pl.BlockSpec((bsz, CHUNK, KD), lambda d, j: (0, blk(d, j), 0)),
        pl.BlockSpec((bsz, CHUNK, KD), lambda d, j: (0, blk(d, j), 1)),
        pl.BlockSpec((bsz, CHUNK, VD), lambda d, j: (0, blk(d, j), 1)),
        pl.BlockSpec((bsz, CHUNK, KD), lambda d, j: (0, blk(d, j), d)),
    ], blk


def gla_fwd(pcat, la):
    bsz = pcat.shape[0]
    in_specs, blk = _gla_in_specs(bsz, False)

    def body(q_ref, k_ref, v_ref, la_ref, o_ref, s_ref, st):
        d, j = pl.program_id(0), pl.program_id(1)

        @pl.when(j == 0)
        def _():
            st[...] = jnp.zeros_like(st)

        s_ref[...] = st[...]
        causal, _ = _causal(d)
        mf = causal.astype(BF16)
        for e, h in [(e, h) for e in range(bsz) for h in range(HEADS)]:
            ks_, vs_ = slice(h * HK, (h + 1) * HK), slice(h * HV, (h + 1) * HV)
            q, k, v, g = q_ref[e, :, ks_] * (HK ** -0.5), k_ref[e, :, ks_], v_ref[e, :, vs_].astype(BF16), la_ref[e, :, ks_]
            b = _mask_dot(mf, g)
            bl = jnp.sum(g, axis=0, keepdims=True)
            qs = (q * jnp.exp(b)).astype(BF16)
            ks = (k * jnp.exp(-b)).astype(BF16)
            kd = (k * jnp.exp(bl - b)).astype(BF16)
            s = st[e, h]
            att = jnp.where(causal, _dot(qs, ks, _NT), 0.0).astype(BF16)
            o_ref[e, :, vs_] = _dot(qs, s.astype(BF16), _NT) + _dot(att, v, _NN)
            st[e, h] = jnp.exp(bl) * s + _dot(v, kd, _TN)

    return pl.pallas_call(
        body, name="gla_fwd", grid=(2, NCH), in_specs=in_specs,
        out_specs=[pl.BlockSpec((bsz, CHUNK, VD), lambda d, j: (0, blk(d, j), d)),
                   pl.BlockSpec((bsz, None, None, HEADS, HV, HK), lambda d, j: (0, d, j, 0, 0, 0))],
        out_shape=[jax.ShapeDtypeStruct((bsz, TT, 2 * VD), F32), jax.ShapeDtypeStruct((bsz, 2, NCH, HEADS, HV, HK), F32)],
        scratch_shapes=[pltpu.VMEM((bsz, HEADS, HV, HK), F32)],
        compiler_params=pltpu.CompilerParams(dimension_semantics=("arbitrary", "arbitrary")),
    )(pcat, pcat, pcat, la)


def gla_bwd(pcat, la, s_all, do):
    bsz = pcat.shape[0]
    in_specs, blk = _gla_in_specs(bsz, True)
    in_specs += [
        pl.BlockSpec((bsz, None, None, HEADS, HV, HK), lambda d, j: (0, d, NCH - 1 - j, 0, 0, 0)),
        pl.BlockSpec((bsz, CHUNK, VD), lambda d, j: (0, jnp.maximum(blk(d, j) - CTX_CH, 0), 0)),
    ]

    def body(q_ref, k_ref, v_ref, la_ref, s_ref, do_ref, dq_ref, dk_ref, dv_ref, dla_ref, dst):
        d, j = pl.program_id(0), pl.program_id(1)

        @pl.when(j == 0)
        def _():
            dst[...] = jnp.zeros_like(dst)

        latent = blk(d, j) >= CTX_CH
        causal, causal_t = _causal(d)
        mt = causal_t.astype(BF16)
        mf = causal.astype(BF16)
        scale = HK ** -0.5
        for e, h in [(e, h) for e in range(bsz) for h in range(HEADS)]:
            ks_, vs_ = slice(h * HK, (h + 1) * HK), slice(h * HV, (h + 1) * HV)
            q, k, v, g = q_ref[e, :, ks_] * scale, k_ref[e, :, ks_], v_ref[e, :, vs_].astype(BF16), la_ref[e, :, ks_]
            b = _mask_dot(mf, g)
            bl = jnp.sum(g, axis=0, keepdims=True)
            ex, ei, ed, el = jnp.exp(b), jnp.exp(-b), jnp.exp(bl - b), jnp.exp(bl)
            qs, ks, kd = q * ex, k * ei, k * ed
            qsb, ksb, kdb = qs.astype(BF16), ks.astype(BF16), kd.astype(BF16)
            s, ds1 = s_ref[e, h], dst[e, h]
            sb, ds1b = s.astype(BF16), ds1.astype(BF16)
            dob = jnp.where(latent, do_ref[e, :, vs_], 0.0).astype(BF16)
            att = jnp.where(causal, _dot(qsb, ksb, _NT), 0.0).astype(BF16)
            datt = jnp.where(causal, _dot(dob, v, _NT), 0.0).astype(BF16)
            dqs = _dot(dob, sb, _NN) + _dot(datt, ksb, _NN)
            dks = _dot(datt, qsb, _TN)
            dv_ref[e, :, vs_] = _dot(att, dob, _TN) + _dot(kdb, ds1b, _NT)
            dkd = _dot(v, ds1b, _NN)
            dst[e, h] = _dot(dob, qsb, _TN) + el * ds1
            del_ = jnp.sum(s * ds1, axis=0, keepdims=True)
            dq_ref[e, :, ks_] = dqs * ex * scale
            dk_ref[e, :, ks_] = dks * ei + dkd * ed
            db = dqs * qs - dks * ks - dkd * kd
            dbl = jnp.sum(dkd * kd, axis=0, keepdims=True) + del_ * el
            dla_ref[e, :, ks_] = _mask_dot(mt, db) + dbl

    return pl.pallas_call(
        body, name="gla_bwd", grid=(2, NCH), in_specs=in_specs,
        out_specs=[pl.BlockSpec((None, bsz, CHUNK, KD), lambda d, j: (d, 0, blk(d, j), 0)),
                   pl.BlockSpec((None, bsz, CHUNK, KD), lambda d, j: (d, 0, blk(d, j), 0)),
                   pl.BlockSpec((None, bsz, CHUNK, VD), lambda d, j: (d, 0, blk(d, j), 0)),
                   pl.BlockSpec((bsz, CHUNK, KD), lambda d, j: (0, blk(d, j), d))],
        out_shape=[jax.ShapeDtypeStruct((2, bsz, TT, KD), F32), jax.ShapeDtypeStruct((2, bsz, TT, KD), F32),
                   jax.ShapeDtypeStruct((2, bsz, TT, VD), F32), jax.ShapeDtypeStruct((bsz, TT, 2 * KD), F32)],
        scratch_shapes=[pltpu.VMEM((bsz, HEADS, HV, HK), F32)],
        compiler_params=pltpu.CompilerParams(dimension_semantics=("arbitrary", "arbitrary")),
    )(pcat, pcat, pcat, la, s_all, do)


def gla_combine(dq2, dk2, dv2, dgate, dpa):
    bsz = dgate.shape[0]
    tm = CTX

    def body(dq_ref, dk_ref, dv_ref, dg_ref, dpa_ref, o_ref):
        t = pl.program_id(1)
        o_ref[:, 0:KD] = (dq_ref[0] + dq_ref[1]).astype(BF16)
        o_ref[:, KD:2 * KD] = (dk_ref[0] + dk_ref[1]).astype(BF16)
        o_ref[:, 2 * KD:2 * KD + VD] = (dv_ref[0] + dv_ref[1]).astype(BF16)
        o_ref[:, 2 * KD + VD:2 * KD + 2 * VD] = jnp.where(t > 0, dg_ref[...], 0).astype(BF16)
        o_ref[:, 2 * KD + 2 * VD:] = dpa_ref[...].astype(BF16)

    return pl.pallas_call(
        body, name="gla_combine", grid=(bsz, TT // tm),
        in_specs=[pl.BlockSpec((2, None, tm, KD), lambda b, t: (0, b, t, 0)),
                  pl.BlockSpec((2, None, tm, KD), lambda b, t: (0, b, t, 0)),
                  pl.BlockSpec((2, None, tm, VD), lambda b, t: (0, b, t, 0)),
                  pl.BlockSpec((None, tm, VD), lambda b, t: (b, jnp.maximum(t - 1, 0), 0)),
                  pl.BlockSpec((None, tm, 128), lambda b, t: (b, t, 0))],
        out_specs=pl.BlockSpec((None, tm, GLA_IN_PAD), lambda b, t: (b, t, 0)),
        out_shape=jax.ShapeDtypeStruct((bsz, TT, GLA_IN_PAD), BF16),
        compiler_params=pltpu.CompilerParams(dimension_semantics=("arbitrary", "arbitrary")),
    )(dq2, dk2, dv2, dgate, dpa)


def final_loss(h1, fo, gate, gain, tgt):
    bsz, t_len, _ = h1.shape
    tm = 256

    def body(h_ref, f_ref, gate_ref, gain_ref, tgt_ref, loss_ref, dh_ref, df_ref, dgate_ref, dgain_ref):
        b, t = pl.program_id(0), pl.program_id(1)
        target = tgt_ref[...]

        def core(h, fo_, gate_, gain_):
            e = _rms(h + gate_ * fo_) * gain_ - target
            return jnp.sum(0.5 * jnp.sum(e * e, axis=-1, keepdims=True) / D, axis=0, keepdims=True)

        loss, vjp = jax.vjp(core, h_ref[...], f_ref[...], gate_ref[...], gain_ref[...])
        dh, df, dgate, dgain = vjp(jnp.ones((1, 1), F32))
        dh_ref[...] = dh
        df_ref[...] = df.astype(BF16)
        first = jnp.logical_and(b == 0, t == 0)

        @pl.when(first)
        def _():
            loss_ref[...] = jnp.broadcast_to(loss, loss_ref.shape)
            dgain_ref[...] = dgain

        @pl.when(jnp.logical_not(first))
        def _():
            loss_ref[...] += jnp.broadcast_to(loss, loss_ref.shape)
            dgain_ref[...] += dgain

        @pl.when(t == 0)
        def _():
            dgate_ref[...] = dgate

        @pl.when(t > 0)
        def _():
            dgate_ref[...] += dgate

    tile = pl.BlockSpec((None, tm, D), lambda b, t: (b, t, 0))
    per_ex = pl.BlockSpec((None, 1, D), lambda b, t: (b, 0, 0))
    shared = pl.BlockSpec((1, D), lambda b, t: (0, 0))
    return pl.pallas_call(
        body, name="final_loss", grid=(bsz, t_len // tm),
        in_specs=[tile, tile, per_ex, shared, tile],
        out_specs=[pl.BlockSpec((8, 128), lambda b, t: (0, 0)), tile, tile, per_ex, shared],
        out_shape=[jax.ShapeDtypeStruct((8, 128), F32), jax.ShapeDtypeStruct(h1.shape, F32),
                   jax.ShapeDtypeStruct(h1.shape, BF16), jax.ShapeDtypeStruct((bsz, 1, D), F32),
                   jax.ShapeDtypeStruct((1, D), F32)],
        compiler_params=pltpu.CompilerParams(dimension_semantics=("arbitrary", "arbitrary")),
    )(h1, fo, gate, gain, tgt)


ADA_ROWS = 24
ADA_CTX_ROW = 16
ADA_COLS = 6 * D // N_DEV


def ada_fwd(cond, w, b):
    def body(c_ref, w_ref, b_ref, o_ref):
        s = silu(c_ref[...]).astype(BF16)
        o_ref[...] = jnp.dot(s, w_ref[...].astype(BF16), preferred_element_type=F32) + b_ref[...]

    return pl.pallas_call(
        body, name="ada_fwd", grid=(2,),
        in_specs=[pl.BlockSpec((ADA_ROWS, D), lambda i: (0, 0)), pl.BlockSpec((None, D, ADA_COLS), lambda i: (i, 0, 0)),
                  pl.BlockSpec((None, 1, ADA_COLS), lambda i: (i, 0, 0))],
        out_specs=pl.BlockSpec((None, ADA_ROWS, ADA_COLS), lambda i: (i, 0, 0)),
        out_shape=jax.ShapeDtypeStruct((2, ADA_ROWS, ADA_COLS), F32),
    )(cond, w, b)


def ada_bwd(cond, dm_mine, dm_full, w):
    def body(c_ref, dm_ref, dmf_ref, w_ref, gw_ref, gb_ref, cp_ref):
        i = pl.program_id(0)
        s = silu(c_ref[...]).astype(BF16)
        dm = dm_ref[...].astype(BF16)
        gw_ref[...] = _dot(s, dm, _TN)
        gb_ref[...] = jnp.sum(dmf_ref[...], axis=0, keepdims=True)

        @pl.when(i == 0)
        def _():
            cp_ref[...] = _dot(dm_ref[ADA_CTX_ROW:, :].astype(BF16), w_ref[...].astype(BF16), _NT)

    return pl.pallas_call(
        body, name="ada_bwd", grid=(2,),
        in_specs=[pl.BlockSpec((ADA_ROWS, D), lambda i: (0, 0)), pl.BlockSpec((None, ADA_ROWS, ADA_COLS), lambda i: (i, 0, 0)),
                  pl.BlockSpec((None, ADA_ROWS, 6 * D), lambda i: (i, 0, 0)), pl.BlockSpec((None, D, ADA_COLS), lambda i: (i, 0, 0))],
        out_specs=[pl.BlockSpec((None, D, ADA_COLS), lambda i: (i, 0, 0)), pl.BlockSpec((None, 1, 6 * D), lambda i: (i, 0, 0)),
                   pl.BlockSpec((ADA_ROWS - ADA_CTX_ROW, D), lambda i: (0, 0))],
        out_shape=[jax.ShapeDtypeStruct((2, D, ADA_COLS), F32), jax.ShapeDtypeStruct((2, 1, 6 * D), F32),
                   jax.ShapeDtypeStruct((ADA_ROWS - ADA_CTX_ROW, D), F32)],
        compiler_params=pltpu.CompilerParams(dimension_semantics=("arbitrary",)),
    )(cond, dm_mine, dm_full, w)


def cctx_grad(parts, c_ctx):
    def body(p_ref, c_ref, o_ref):
        tot = p_ref[0:1, :]
        for i in range(1, N_DEV):
            tot = tot + p_ref[i:i + 1, :]
        c = c_ref[...]
        sg = jax.nn.sigmoid(c)
        o_ref[...] = tot * sg * (1.0 + c * (1.0 - sg))

    return pl.pallas_call(body, name="cctx_grad", out_shape=jax.ShapeDtypeStruct((1, D), F32))(parts, c_ctx)


def _row_tile(r):
    for t in (512, 256, 128, 80, 64, 40, 32, 16, 8):
        if r % t == 0:
            return t
    return r


def _slot_sum(ref):
    tot = ref[0].astype(F32)
    for i in range(1, ref.shape[0]):
        tot = tot + ref[i].astype(F32)
    return tot


def sum_slots(name, x):
    s, r, c = x.shape
    tr = _row_tile(r)

    def body(x_ref, o_ref):
        o_ref[...] = _slot_sum(x_ref)

    return pl.pallas_call(
        body, name=name, grid=(r // tr,), in_specs=[pl.BlockSpec((s, tr, c), lambda i: (0, i, 0))],
        out_specs=pl.BlockSpec((tr, c), lambda i: (i, 0)), out_shape=jax.ShapeDtypeStruct((r, c), F32),
    )(x)


def adamw(name, w, g, m, v, layer=None):
    r, c = w.shape[-2:]
    tr = _row_tile(r)
    stacked = g.ndim == 3

    def body(w_ref, g_ref, m_ref, v_ref, go_ref, d_ref, mo_ref, vo_ref):
        gv = _slot_sum(g_ref) if stacked else g_ref[...]
        mn = B1 * m_ref[...] + (1.0 - B1) * gv
        vn = B2 * v_ref[...] + (1.0 - B2) * jnp.square(gv)
        m_hat = mn / (1.0 - B1 ** STEP)
        v_hat = vn / (1.0 - B2 ** STEP)
        go_ref[...] = gv
        d_ref[...] = -LR * (m_hat / (jnp.sqrt(v_hat) + AEPS) + WD * w_ref[...])
        mo_ref[...] = mn
        vo_ref[...] = vn

    tile = pl.BlockSpec((tr, c), lambda i: (i, 0))
    slab = tile if layer is None else pl.BlockSpec((None, tr, c), lambda i: (layer, i, 0))
    g_spec = pl.BlockSpec((g.shape[0], tr, c), lambda i: (0, i, 0)) if stacked else tile
    return pl.pallas_call(
        body, name=name, grid=(r // tr,), in_specs=[slab, g_spec, slab, slab], out_specs=[tile] * 4,
        out_shape=[jax.ShapeDtypeStruct((r, c), F32)] * 4,
    )(w, g, m, v)


def _place():
    return lax.axis_index("x"), lax.axis_index("y"), lax.axis_index("c")


def all_gather(name, x, in_vmem):
    r, c = x.shape
    space = pltpu.VMEM if in_vmem else pl.ANY

    def body(x_ref, out_ref, send_sems, recv_sems, local_sem):
        px, py, pc = _place()
        me, sibling = (px, py, pc), (px, py, 1 - pc)
        chips = [(1 - px, py), (px, 1 - py), (1 - px, 1 - py)]

        def rows(qx, qy, qc):
            return out_ref.at[pl.ds((4 * qx + 2 * qy + qc) * r, r), :]

        def copy(k, block, to, src=None):
            return pltpu.make_async_remote_copy(
                src_ref=rows(*block) if src is None else src, dst_ref=rows(*block),
                send_sem=send_sems.at[k], recv_sem=recv_sems.at[k], device_id=to, device_id_type=MESH)

        mine = pltpu.make_async_copy(x_ref, rows(*me), local_sem)
        mine.start()
        first = [copy(0, me, sibling, src=x_ref)]
        first += [copy(1 + j, me, (*chip, pc), src=x_ref) for j, chip in enumerate(chips)]
        for cp in first:
            cp.start()
        passed = [copy(4 + j, (*chip, pc), sibling) for j, chip in enumerate(chips)]
        for j, chip in enumerate(chips):
            copy(1 + j, (*chip, pc), me).wait_recv()
            passed[j].start()
        copy(0, sibling, me).wait_recv()
        for j, chip in enumerate(chips):
            copy(4 + j, (*chip, 1 - pc), me).wait_recv()
        for cp in first + passed:
            cp.wait_send()
        mine.wait()

    return pl.pallas_call(
        body, name=name, out_shape=jax.ShapeDtypeStruct((N_DEV * r, c), x.dtype),
        in_specs=[pl.BlockSpec(memory_space=space)], out_specs=pl.BlockSpec(memory_space=space),
        scratch_shapes=[pltpu.SemaphoreType.DMA((7,)), pltpu.SemaphoreType.DMA((7,)), pltpu.SemaphoreType.DMA],
    )(x)


_HBM =pl.BlockSpec(memory_space=pltpu.HBM)
_SEM = pl.BlockSpec(memory_space=pltpu.SEMAPHORE)
_EFFECT = pltpu.SideEffectType.DATAFLOW_SIDE_EFFECTING


def _peers():
    px, py, pc = _place()
    return [(1 - px if k & 4 else px, 1 - py if k & 2 else py, 1 - pc if k & 1 else pc) for k in range(1, N_DEV)]


def _slot(dev):
    return 4 * dev[0] + 2 * dev[1] + dev[2]


def _split_copies(src_refs, land_refs, send_sems, recv_sems, gather):
    me = _slot(_place())
    return [pltpu.make_async_remote_copy(
        src_ref=src if gather else src.at[_slot(peer)], dst_ref=land.at[me],
        send_sem=send_sems.at[a * (N_DEV - 1) + k], recv_sem=recv_sems.at[a * (N_DEV - 1) + k],
        device_id=peer, device_id_type=MESH)
        for a, (src, land) in enumerate(zip(src_refs, land_refs)) for k, peer in enumerate(_peers())]


def exchange_start(name, srcs, gather, after):
    n = len(srcs)
    lands = [pltpu.HBM((N_DEV,) + s.shape if gather else s.shape, s.dtype) for s in srcs]

    def body(*refs):
        send_sems, recv_sems = refs[2 * n + 1:2 * n + 3]
        for cp in _split_copies(refs[:n], refs[n:2 * n], send_sems, recv_sems, gather):
            cp.start()
        refs[-1][...] = jnp.zeros_like(refs[-1])

    sems = pltpu.SemaphoreType.DMA((n * (N_DEV - 1),))
    res = pl.pallas_call(
        body, name=name,
        out_shape=(sems, sems, *[pltpu.HBM(s.shape, s.dtype) for s in srcs], *lands, jax.ShapeDtypeStruct((8, 128), F32)),
        in_specs=(_HBM,) * (2 * n) + (pl.BlockSpec(memory_space=pl.ANY),),
        out_specs=(_SEM, _SEM) + (_HBM,) * (2 * n) + (pl.BlockSpec(memory_space=pltpu.VMEM),),
        input_output_aliases={i: 2 + i for i in range(2 * n)},
        compiler_params=pltpu.CompilerParams(has_side_effects=_EFFECT),
    )(*[pltpu.with_memory_space_constraint(s, pltpu.HBM) for s in srcs],
      *[pltpu.with_memory_space_constraint(lax.empty(ld.shape, ld.dtype), pltpu.HBM) for ld in lands], after)
    return res[0], res[1], list(res[2:2 + n]), list(res[2 + n:2 + 2 * n]), res[-1]


def exchange_wait(name, started, after, gather):
    send_sems, recv_sems, srcs, lands, _ = started
    n = len(srcs)
    after = list(after) if isinstance(after, (list, tuple)) else [after]

    def body(*refs):
        send_sems, recv_sems = refs[2 * n:2 * n + 2]
        for cp in _split_copies(refs[:n], refs[n:2 * n], send_sems, recv_sems, gather):
            cp.wait_send()
            cp.wait_recv()

    res = pl.pallas_call(
        body, name=name, out_shape=tuple(pltpu.HBM(a.shape, a.dtype) for a in srcs + lands),
        in_specs=(_HBM,) * (2 * n) + (_SEM, _SEM) + (pl.BlockSpec(memory_space=pl.ANY),) * len(after),
        out_specs=(_HBM,) * (2 * n), input_output_aliases={i: i for i in range(2 * n)},
        compiler_params=pltpu.CompilerParams(has_side_effects=_EFFECT),
    )(*srcs, *lands, send_sems, recv_sems, *after)
    return list(res[:n]), list(res[n:])


NCF = FFN_H // FFN_TC


def _size(shape):
    n = 1
    for s in shape:
        n *= s
    return n


def _padded_rows(n_elems, row_mult):
    return -(-n_elems // (D * row_mult)) * row_mult


def _pack_rows(arrs, dtype, row_mult):
    rows, offs, r0 = [], [], 0
    for a in arrs:
        flat = a.reshape(-1).astype(dtype)
        n = _padded_rows(flat.shape[0], row_mult)
        rows.append(jnp.pad(flat, (0, n * D - flat.shape[0])).reshape(n, D))
        offs.append(r0)
        r0 += n
    return jnp.concatenate(rows, 0), offs


def _unpack_rows(buf, offs, shapes):
    lead, out = buf.shape[:-2], []
    for o, shp in zip(offs, shapes):
        n = _size(shp)
        nr = -(-n // D)
        out.append(buf[..., o:o + nr, :].reshape(lead + (nr * D,))[..., :n].reshape(lead + tuple(shp)))
    return out


def _rows3(w):
    return [w[i:i + 1] for i in range(3)]


def f_mod1(xs, ps):
    return f_mod(xs, ps)[:1]


def kernel(x, c, ctx, c_ctx, ada_w, ada_b, norm_mix, norm_ffn, gla_w_in, gla_w_a2, gla_b_a, gla_head_norm, gla_w_out, sc_w_in, sc_conv_w, sc_w_out, ffn_w_up, ffn_conv_w, ffn_conv_b, ffn_w_down, final_norm, loss_target, m_c_ctx, m_ada_w, m_ada_b, m_norm_mix, m_norm_ffn, m_gla_w_in, m_gla_w_a2, m_gla_b_a, m_gla_head_norm, m_gla_w_out, m_sc_w_in, m_sc_conv_w, m_sc_w_out, m_ffn_w_up, m_ffn_conv_w, m_ffn_conv_b, m_ffn_w_down, m_final_norm, v_c_ctx, v_ada_w, v_ada_b, v_norm_mix, v_norm_ffn, v_gla_w_in, v_gla_w_a2, v_gla_b_a, v_gla_head_norm, v_gla_w_out, v_sc_w_in, v_sc_conv_w, v_sc_w_out, v_ffn_w_up, v_ffn_conv_w, v_ffn_conv_b, v_ffn_w_down, v_final_norm):
    names = ["c_ctx", "ada_w", "ada_b", "norm_mix", "norm_ffn", "gla_w_in", "gla_w_a2", "gla_b_a", "gla_head_norm",
             "gla_w_out", "sc_w_in", "sc_conv_w", "sc_w_out", "ffn_w_up", "ffn_conv_w", "ffn_conv_b", "ffn_w_down",
             "final_norm"]
    w_ = dict(zip(names, [c_ctx, ada_w, ada_b, norm_mix, norm_ffn, gla_w_in, gla_w_a2, gla_b_a, gla_head_norm, gla_w_out,
                          sc_w_in, sc_conv_w, sc_w_out, ffn_w_up, ffn_conv_w, ffn_conv_b, ffn_w_down, final_norm]))
    m_ = dict(zip(names, [m_c_ctx, m_ada_w, m_ada_b, m_norm_mix, m_norm_ffn, m_gla_w_in, m_gla_w_a2, m_gla_b_a,
                          m_gla_head_norm, m_gla_w_out, m_sc_w_in, m_sc_conv_w, m_sc_w_out, m_ffn_w_up, m_ffn_conv_w,
                          m_ffn_conv_b, m_ffn_w_down, m_final_norm]))
    v_ = dict(zip(names, [v_c_ctx, v_ada_w, v_ada_b, v_norm_mix, v_norm_ffn, v_gla_w_in, v_gla_w_a2, v_gla_b_a,
                          v_gla_head_norm, v_gla_w_out, v_sc_w_in, v_sc_conv_w, v_sc_w_out, v_ffn_w_up, v_ffn_conv_w,
                          v_ffn_conv_b, v_ffn_w_down, v_final_norm]))
    me = 4 * lax.axis_index("x") + 2 * lax.axis_index("y") + lax.axis_index("c")
    bsz = x.shape[0]
    tm = 256
    nt = SEQ // tm
    ctx_tiles = CTX // tm
    pe = functools.partial(P, per_example=True)

    groups = {"gla": [("gla_w_in", 0), ("gla_w_out", 0)], "ffn0": [("ffn_w_up", 0), ("ffn_w_down", 0)],
              "l1": [("sc_w_in", 0), ("sc_w_out", 0), ("ffn_w_up", 1), ("ffn_w_down", 1)]}
    ag_started = {}

    def start_gather(g, after):
        ag_started[g] = exchange_start(f"ag_{g}_start", [w_[n][i].astype(BF16) for n, i in groups[g]], True, after)
        return ag_started[g][4]

    small_sharded = [c, gla_w_a2, gla_b_a, sc_conv_w, ffn_conv_w]
    pack0, offs0 = _pack_rows(small_sharded, F32, 8)
    g0 = all_gather("ag_small", pack0, True).reshape(N_DEV, pack0.shape[0], D)
    c_all, wa2_s, ba_s, scw_s, fcw_s = _unpack_rows(g0, offs0, [a.shape for a in small_sharded])
    w_a2 = wa2_s[:, 0].transpose(1, 2, 0, 3).reshape(2, RANK, KD)
    b_a = ba_s[:, 0].transpose(1, 0, 2).reshape(2, KD)
    sc_cw = scw_s[:, 0].transpose(1, 0, 2).reshape(3, D)
    ffn_cw = fcw_s.transpose(1, 2, 0, 3).reshape(2, 3, 2 * FFN_H)

    cond = jnp.concatenate([c_all.reshape(N_DEV * bsz, D), c_ctx[None], jnp.zeros((ADA_ROWS - N_DEV * bsz - 1, D), F32)], 0)
    b_mine = lax.dynamic_slice(ada_b, (0, me * ADA_COLS), (2, ADA_COLS)).reshape(2, 1, ADA_COLS)
    mod_part = ada_fwd(cond, ada_w, b_mine)
    mod = all_gather("ag_mod", mod_part.reshape(2 * ADA_ROWS, ADA_COLS), True)
    mod = mod.reshape(N_DEV, 2, ADA_ROWS, ADA_COLS).transpose(1, 2, 0, 3).reshape(2, ADA_ROWS, 6 * D)
    mods = lax.dynamic_slice(mod, (0, bsz * me, 0), (2, bsz, 6 * D))
    md = [[mods[i][:, k * D:(k + 1) * D].reshape(bsz, 1, D) for k in range(6)] for i in range(2)]
    mc = [mod[0, ADA_CTX_ROW, k * D:(k + 1) * D][None] for k in range(2)]

    tok = mod
    for g in groups:
        tok = start_gather(g, tok)
    norm_mix = norm_mix + tok[0, 0]

    def gathered(g, after):
        mine, lands = exchange_wait(f"ag_{g}_wait", ag_started[g], after, True)
        return [lax.dynamic_update_index_in_dim(ld, mn, me, 0) for ld, mn in zip(lands, mine)]

    s_up, w_down = [None, None], [None, None]
    wd = jnp.zeros((128, 2 * KD), F32).at[:RANK, :KD].set(w_a2[0]).at[RANK:2 * RANK, KD:].set(w_a2[1])
    bd = b_a.reshape(1, 2 * KD)
    scw = _rows3(sc_cw)
    head_gain = gla_head_norm.reshape(1, HV)
    gains_mix = [norm_mix[i][None] for i in range(2)]
    gains_ffn = [norm_ffn[i][None] for i in range(2)]

    def tokens(a2d, t_len):
        return a2d.reshape(bsz, t_len, -1)

    def ffn_params(i):
        rows = [ffn_cw[i][t] for t in range(3)] + [ffn_conv_b[i]]
        return [P(a.reshape(2, FFN_H), w=FFN_TC, rows=True) for a in rows]

    def ffn_fwd(i, hn2):
        u = mm(f"ffn_up{i}", V(hn2, "tok"), V(s_up[i], "cols"), out="planes", out_dtype=BF16, planes_t=SEQ)
        act = rowwise(f"ffn_mid{i}", f_ffn_mid, [X(u, w=FFN_TC, planes=True)], ffn_params(i), tm=SEQ, nt=1, nc=NCF,
                      outs=[(FFN_TC, BF16, 1)])[0]
        return u, act, tokens(mm(f"ffn_down{i}", V(act, "tok"), V(w_down[i])), SEQ)

    def res_mod_fwd(name, h, y, ps):
        return rowwise(name, f_res_mod, [X(h), X(y)], ps, tm=tm, nt=nt, outs=[(D, F32, 1), (D, BF16, 1)])

    ps_in0 = [P(gains_mix[0]), pe(md[0][0]), pe(md[0][1])]
    ps_ctx = [P(gains_mix[0]), P(mc[0]), P(mc[1])]
    hn0 = rowwise("mod_in0", f_mod, [X(x)], ps_in0, tm=tm, nt=nt, outs=[(D, BF16, 1)])[0]
    hnc = rowwise("mod_ctx", f_mod, [X(ctx)], ps_ctx, tm=tm, nt=ctx_tiles, outs=[(D, BF16, 1)])[0]
    hcat = jnp.concatenate([hnc, hn0], axis=1)
    s_gin, s_gout = gathered("gla", hcat)
    w_gin = V(s_gin, "cols", width=GLA_IN_PAD)
    w_gout = s_gout.reshape(VD, D)
    pcat = tokens(mm("gla_in", V(hcat, "tok"), w_gin), TT)
    pa_x = X(pcat, w=128, co=(GLA_IN_PAD - 128) // 128)
    la = rowwise("gla_decay", f_decay, [pa_x], [P(wd), P(bd)], tm=tm, nt=TT // tm, outs=[(2 * KD, F32, 1)])[0]
    o2, s_all = gla_fwd(pcat, la)
    post_xs = [X(o2, w=VD, co=0, ro=ctx_tiles, split=HEADS), X(o2, w=VD, co=1, ro=ctx_tiles, split=HEADS),
               X(pcat, w=VD, co=2, ro=ctx_tiles, split=HEADS)]
    yin0 = rowwise("gla_post", f_gla_post, post_xs, [P(head_gain)], tm=tm, nt=nt, outs=[(VD, BF16, HEADS)])[0]
    y0 = tokens(mm("gla_out", V(yin0, "tok"), V(w_gout)), SEQ)
    ps_mid0 = [pe(md[0][2]), P(gains_ffn[0]), pe(md[0][3]), pe(md[0][4])]
    h1_0, hn2_0 = res_mod_fwd("res_mod_mid0", x, y0, ps_mid0)
    s_up[0], s_down0 = gathered("ffn0", hn2_0)
    w_down[0] = s_down0.reshape(FFN_H, D)
    u0, act0, fo0 = ffn_fwd(0, hn2_0)
    ps_in1 = [pe(md[0][5]), P(gains_mix[1]), pe(md[1][0]), pe(md[1][1])]
    h2_0, hn1 = res_mod_fwd("res_mod_in1", h1_0, fo0, ps_in1)

    s_sin, s_sout, s_up[1], s_down1 = gathered("l1", hn1)
    w_sout, w_down[1] = s_sout.reshape(D, D), s_down1.reshape(FFN_H, D)
    p1 = tokens(mm("sc_in", V(hn1, "tok"), V(s_sin, "cols")), SEQ)
    sc_ps = [P(a) for a in scw]
    yin1 = rowwise("sc_mid", f_sc_mid, [X(p1, split=3)], sc_ps, tm=tm, nt=nt, outs=[(D, BF16, 1)])[0]
    y1 = tokens(mm("sc_out", V(yin1, "tok"), V(w_sout)), SEQ)
    ps_mid1 = [pe(md[1][2]), P(gains_ffn[1]), pe(md[1][3]), pe(md[1][4])]
    h1_1, hn2_1 = res_mod_fwd("res_mod_mid1", h2_0, y1, ps_mid1)
    u1, act1, fo1 = ffn_fwd(1, hn2_1)
    loss8, dh1_1, dfo1, dm5_1, g_final = final_loss(h1_1, fo1, md[1][5], final_norm[None], loss_target)

    def ffn_bwd(i, u, act, hn2, dfo):
        dact = tokens(mm(f"ffn_down_dx{i}", V(dfo, "tok"), V(w_down[i]), form="nt", out_dtype=BF16), SEQ)
        g_down = mm(f"ffn_down_dw{i}", V(act, "tok"), V(dfo, "tok"), form="tn", out_dtype=BF16)
        r = rowwise(f"ffn_mid_bwd{i}", f_ffn_mid, [X(u, w=FFN_TC, planes=True)], ffn_params(i), tm=SEQ, nt=1, nc=NCF,
                    douts=[X(dact, w=FFN_TC)], dx={0: BF16}, dp=[0, 1, 2, 3])
        du, g_cw, g_cb = r[0], jnp.stack([a.reshape(2 * FFN_H) for a in r[1:4]]), r[4].reshape(1, 2 * FFN_H)
        dhn2 = tokens(mm(f"ffn_up_dx{i}", V(du, "planes"), V(s_up[i], "cols"), form="nt", out_dtype=BF16), SEQ)
        g_up = mm(f"ffn_up_dw{i}", V(hn2, "tok"), V(du, "planes"), form="tn", out="cols", out_dtype=BF16)
        return dhn2, g_up, row_slots(g_down), g_cw, g_cb

    def res_mod_bwd(name, h, y, ps, dh1, dhn):
        return rowwise(name, f_res_mod, [X(h), X(y)], ps, tm=tm, nt=nt, douts=[X(dh1), X(dhn)],
                       dx={0: F32, 1: BF16}, dp=[0, 1, 2, 3])

    def row_slots(g):
        return g.reshape(N_DEV, -1, g.shape[-1])

    a2a_started = {}

    def send_grads(g, slots, after=None):
        a2a_started[g] = exchange_start(f"a2a_{g}_start", list(slots), False, loss8 if after is None else after)
        return a2a_started[g][4][0, 0]

    def after_start(ps, tok):
        return [dict(ps[0], a=ps[0]["a"] + tok)] + ps[1:]

    dhn2_1, g_up1, g_down1, g_fcw1, g_fcb1 = ffn_bwd(1, u1, act1, hn2_1, dfo1)
    dh2_0, dy1, dm2_1, g_nffn1, dm3_1, dm4_1 = res_mod_bwd("res_mod_mid1_bwd", h2_0, y1, ps_mid1, dh1_1, dhn2_1)
    dyin1 = tokens(mm("sc_out_dx", V(dy1, "tok"), V(w_sout), form="nt", out_dtype=BF16), SEQ)
    g_sout = row_slots(mm("sc_out_dw", V(yin1, "tok"), V(dy1, "tok"), form="tn", out_dtype=BF16))
    r = rowwise("sc_mid_bwd", f_sc_mid, [X(p1, split=3)], sc_ps, tm=tm, nt=nt, douts=[X(dyin1)], dx={0: BF16}, dp=[0, 1, 2])
    dp1, g_scw = r[0], jnp.concatenate(r[1:4], 0)
    dhn1 = tokens(mm("sc_in_dx", V(dp1, "tok"), V(s_sin, "cols"), form="nt", out_dtype=BF16), SEQ)
    g_sin = mm("sc_in_dw", V(hn1, "tok"), V(dp1, "tok"), form="tn", out="cols", out_dtype=BF16)
    tok = send_grads("l1", [g_sin, g_sout, g_up1, g_down1])
    dh1_0, dfo0, dm5_0, g_nmix1, dm0_1, dm1_1 = res_mod_bwd("res_mod_in1_bwd", h1_0, fo0, after_start(ps_in1, tok), dh2_0, dhn1)

    dhn2_0, g_up0, g_down0, g_fcw0, g_fcb0 = ffn_bwd(0, u0, act0, hn2_0, dfo0)
    tok = send_grads("ffn0", [g_up0, g_down0])
    dx_res, dy0, dm2_0, g_nffn0, dm3_0, dm4_0 = res_mod_bwd("res_mod_mid0_bwd", x, y0, after_start(ps_mid0, tok), dh1_0, dhn2_0)
    dyin0 = tokens(mm("gla_out_dx", V(dy0, "tok"), V(w_gout), form="nt", out_dtype=BF16), SEQ)
    g_gout = row_slots(mm("gla_out_dw", V(yin0, "tok"), V(dy0, "tok"), form="tn", out_dtype=BF16))
    do, dgate, g_head = rowwise("gla_post_bwd", f_gla_post, post_xs, [P(head_gain)], tm=tm, nt=nt,
                                douts=[X(dyin0, split=HEADS)], dx={0: F32, 2: BF16}, dp=[0])
    dq2, dk2, dv2, dla = gla_bwd(pcat, la, s_all, do)
    dpa, g_wd, g_bd = rowwise("gla_decay_bwd", f_decay, [pa_x], [P(wd), P(bd)], tm=tm, nt=TT // tm, douts=[X(dla)],
                              dx={0: BF16}, dp=[0, 1])
    dpcat = gla_combine(dq2, dk2, dv2, dgate, dpa)
    dhcat = tokens(mm("gla_in_dx", V(dpcat, "tok"), w_gin, form="nt", out_dtype=BF16), TT)
    g_gin = mm("gla_in_dw", V(hcat, "tok"), V(dpcat, "tok"), form="tn", out="cols", out_dtype=BF16, shard_n=GLA_IN // N_DEV)
    grad_x, g_nmix0, dm0_0, dm1_0 = rowwise("mod_in0_bwd", f_mod, [X(x)], ps_in0, tm=tm, nt=nt,
                                            douts=[X(dhcat, ro=ctx_tiles), X(dx_res)], dx={0: F32}, dp=[0, 1, 2])
    g_nmix0c, dmc0, dmc1 = rowwise("mod_ctx_bwd", f_mod1, [X(ctx)], ps_ctx, tm=tm, nt=ctx_tiles, douts=[X(dhcat)],
                                   dx={}, dp=[0, 1, 2])

    zero_row = jnp.zeros((1, 4 * D), F32)
    dmod = [jnp.concatenate([jnp.concatenate([a.reshape(bsz, D) for a in dms], 1), ctx_row], 0)
            for dms, ctx_row in (([dm0_0, dm1_0, dm2_0, dm3_0, dm4_0, dm5_0], jnp.concatenate([dmc0, dmc1, zero_row], 1)),
                                 ([dm0_1, dm1_1, dm2_1, dm3_1, dm4_1, dm5_1], jnp.zeros((1, 6 * D), F32)))]
    g_wa2 = jnp.stack([g_wd[:RANK, :KD], g_wd[RANK:2 * RANK, KD:]])
    small_grads = [jnp.stack(dmod), jnp.concatenate([g_nmix0 + g_nmix0c, g_nmix1], 0), jnp.concatenate([g_nffn0, g_nffn1], 0),
                   g_head, jnp.concatenate([g_fcb0, g_fcb1], 0), g_final, g_wa2, g_bd.reshape(2, KD), g_scw,
                   jnp.stack([g_fcw0, g_fcw1]), loss8[:1]]
    pack1, offs1 = _pack_rows(small_grads, F32, 8)
    g1 = all_gather("ag_grads", pack1, True).reshape(N_DEV, pack1.shape[0], D)
    dmod_all = _unpack_rows(g1, offs1[:1], [small_grads[0].shape])[0]
    tot = _unpack_rows(sum_slots("sum_small", g1), offs1, [a.shape for a in small_grads])
    loss = tot[10][0, 0]
    dm_rows = dmod_all[:, :, :bsz].transpose(1, 0, 2, 3).reshape(2, N_DEV * bsz, 6 * D)
    dm_full = jnp.concatenate([dm_rows, tot[0][:, bsz:], jnp.zeros((2, ADA_ROWS - N_DEV * bsz - 1, 6 * D), F32)], 1)
    dm_mine = lax.dynamic_slice(dm_full, (0, 0, me * ADA_COLS), (2, ADA_ROWS, ADA_COLS))
    g_ada_w, g_ada_b, cpart = ada_bwd(cond, dm_mine, dm_full, ada_w)
    cparts = all_gather("ag_cctx", cpart, True).reshape(N_DEV, ADA_ROWS - ADA_CTX_ROW, D)[:, 0]
    g_cctx = cctx_grad(cparts, c_ctx[None])[0]
    tok = send_grads("gla", [g_gin, g_gout], after=g_cctx)

    def my_cols(full, n):
        return lax.dynamic_slice_in_dim(full, me * n, n, axis=full.ndim - 1)

    grads = {
        "c_ctx": g_cctx, "ada_b": g_ada_b.reshape(2, 6 * D), "norm_mix": tot[1], "norm_ffn": tot[2],
        "gla_head_norm": tot[3], "ffn_conv_b": tot[4], "final_norm": tot[5].reshape(D),
        "gla_w_a2": my_cols(tot[6], KD // N_DEV)[None], "gla_b_a": my_cols(tot[7], KD // N_DEV)[None],
        "sc_conv_w": my_cols(tot[8], D // N_DEV)[None], "ffn_conv_w": my_cols(tot[9], 2 * FFN_H // N_DEV),
    }

    res_ada = adamw("adamw_ada", *[a.reshape(2 * D, ADA_COLS) for a in (ada_w, g_ada_w, m_ada_w, v_ada_w)])
    grads["c_ctx"] = g_cctx + tok
    big = ["gla_w_in", "gla_w_out", "sc_w_in", "sc_w_out", "ffn_w_up", "ffn_w_down"]
    small = [n for n in names if n not in big and n != "ada_w"]
    g_small = _pack_rows([grads[n] for n in small], F32, 8)[0]
    res_small = adamw("adamw_small", _pack_rows([w_[n] for n in small], F32, 8)[0], g_small,
                      _pack_rows([m_[n] for n in small], F32, 8)[0], _pack_rows([v_[n] for n in small], F32, 8)[0])
    offs_s = _pack_rows([w_[n] for n in small], F32, 8)[1]

    big_res, done = {}, [res_small[0], res_ada[0]]
    for g in ("l1", "ffn0", "gla"):
        sent, lands = exchange_wait(f"a2a_{g}_wait", a2a_started[g], done, False)
        for (n, i), mine, land in zip(groups[g], sent, lands):
            land = lax.dynamic_update_index_in_dim(land, lax.dynamic_index_in_dim(mine, me, 0, keepdims=False), me, 0)
            big_res[(n, i)] = adamw(f"adamw_{n}{i}", w_[n], land, m_[n], v_[n], layer=i)
            done.append(big_res[(n, i)][0])

    out = {}
    for kind, idx in (("grad", 0), ("delta", 1), ("new_m", 2), ("new_v", 3)):
        vals = {n: jnp.stack([big_res[(n, i)][idx] for i in range(w_[n].shape[0])]) for n in big}
        vals["ada_w"] = res_ada[idx].reshape(ada_w.shape)
        vals.update(zip(small, _unpack_rows(res_small[idx], offs_s, [w_[n].shape for n in small])))
        out[kind] = [vals[n] for n in names]
    return (loss, grad_x, *out["grad"], *out["delta"], *out["new_m"], *out["new_v"])
```

```python
import functools

import jax
import jax.numpy as jnp
from jax import lax
from jax.experimental import pallas as pl
from jax.experimental.pallas import tpu as pltpu

F32 = jnp.float32
BF16 = jnp.bfloat16

N_DEV = 8
D = 1024
SEQ = 2048
CTX = 256
TT = CTX + SEQ
GRID_W = 64
CHUNK = 64
HEADS = 4
HK = 128
HV = 256
KD = 512
VD = 1024
RANK = 16
TAU = 16.0
GLA_IN = 3104
GLA_IN_PAD = 3200
FFN_H = 2560
FFN_TC = 256
EPS = 1e-6
LR, B1, B2, AEPS, WD, STEP = 0.001, 0.9, 0.999, 1e-08, 0.01, 10
MESH = pl.DeviceIdType.MESH


def _blocks(n):
    return [n] + [t for t in range(n - n % 128, 0, -128) if n % t == 0 and t != n]


def V(arr, kind="flat", width=None):
    if kind == "tok":
        return V(arr.reshape(-1, arr.shape[-1]))
    if kind == "flat":
        r, c = arr.shape
        return dict(a=arr, kind=kind, shape=(r, c), rows=_blocks(r), cols=_blocks(c))
    if kind == "planes":
        bsz, _, t, ch = arr.shape
        return dict(a=arr, kind=kind, shape=(bsz * t, 2 * ch), rows=_blocks(t), cols=[2 * ch] + _blocks(ch), t=t, ch=ch)
    _, r, n = arr.shape
    if width is not None:
        return dict(a=arr, kind=kind, shape=(r, width), rows=_blocks(r), cols=[width], n=n, pad=width - N_DEV * n)
    return dict(a=arr, kind=kind, shape=(r, N_DEV * n), rows=_blocks(r), cols=[8 * n, 4 * n, 2 * n], n=n, pad=0)


def _view_spec(v, br, bc, idx):
    if v["kind"] == "flat":
        return pl.BlockSpec((br, bc), idx)
    if v["kind"] == "planes":
        nt = v["t"] // br
        if bc == 2 * v["ch"]:
            return pl.BlockSpec((None, 2, br, v["ch"]), lambda i, j, k: (idx(i, j, k)[0] // nt, 0, idx(i, j, k)[0] % nt, 0))
        nch = v["ch"] // bc

        def at(i, j, k):
            r, c = idx(i, j, k)
            return r // nt, c // nch, r % nt, c % nch
        return pl.BlockSpec((None, None, br, bc), at)
    return pl.BlockSpec(((bc - v["pad"]) // v["n"], br, v["n"]), lambda i, j, k: (idx(i, j, k)[1], idx(i, j, k)[0], 0))


def _out_view(kind, rows, cols, dtype, planes_t=None, shard_n=None):
    if kind == "flat":
        shape = (rows, cols)
    elif kind == "planes":
        shape = (rows // planes_t, 2, planes_t, cols // 2)
    elif shard_n is not None:
        return V(jax.ShapeDtypeStruct((N_DEV, rows, shard_n), dtype), kind, width=cols)
    else:
        shape = (N_DEV, rows, cols // N_DEV)
    return V(jax.ShapeDtypeStruct(shape, dtype), kind)


MM_VMEM_BUDGET = 40 * 2 ** 20
MM_VMEM_LIMIT = 56 * 2 ** 20
MM_MAX_TILE = 1536


def _mm_tiles(m, n, kk, ms, ns, ks, a_bytes, b_bytes, o_bytes):
    best = None
    for tk in ks:
        for tm in [t for t in ms if t <= MM_MAX_TILE] or ms:
            for tn in [t for t in ns if t <= MM_MAX_TILE] or ns:
                one_k = tk == kk
                need = 2 * (tm * tk * a_bytes + tk * tn * b_bytes + tm * tn * o_bytes) + (0 if one_k else tm * tn * 4)
                if need > MM_VMEM_BUDGET:
                    continue
                steps = (m // tm) * (n // tn) * (kk // tk)
                traffic = (m * kk * a_bytes * (1 if one_k else n // tn)
                           + kk * n * b_bytes * (1 if one_k and n == tn else m // tm) + m * n * o_bytes)
                fill = (tm * tk * a_bytes + tk * tn * b_bytes) / 2.5e12
                cost = max(2.0 * m * n * kk / (9e14 if one_k else 6.5e14), traffic / 2.5e12) + steps * 0.4e-6 + fill
                if best is None or cost < best[0]:
                    best = (cost, tm, tn, tk)
    return best[1:]


def mm(name, a, b, form="nn", out="flat", out_dtype=F32, planes_t=None, shard_n=None):
    (m, kk) = a["shape"][::-1] if form == "tn" else a["shape"]
    n = b["shape"][0] if form == "nt" else b["shape"][1]
    assert (b["shape"][1] if form == "nt" else b["shape"][0]) == kk, (name, a["shape"], b["shape"])
    o = _out_view(out, m, n, out_dtype, planes_t, shard_n)
    a_m, a_k = (a["cols"], a["rows"]) if form == "tn" else (a["rows"], a["cols"])
    b_k, b_n = (b["cols"], b["rows"]) if form == "nt" else (b["rows"], b["cols"])
    tm, tn, tk = _mm_tiles(m, n, kk, [t for t in a_m if t in o["rows"]], [t for t in b_n if t in o["cols"]],
                           [t for t in a_k if t in b_k], a["a"].dtype.itemsize, b["a"].dtype.itemsize,
                           jnp.dtype(out_dtype).itemsize)
    nk = kk // tk
    dn = (((0 if form == "tn" else 1,), (1 if form == "nt" else 0,)), ((), ()))

    def load(ref, v):
        if len(ref.shape) == 3:
            pieces = [ref[p].astype(BF16) for p in range(ref.shape[0])]
            if v.get("pad"):
                pieces.append(jnp.zeros(ref.shape[1:2] + (v["pad"],), BF16))
            return jnp.concatenate(pieces, axis=-1)
        return ref[...].astype(BF16)

    def store(o_ref, val):
        val = val.astype(out_dtype)
        if len(o_ref.shape) == 3:
            w = o_ref.shape[-1]
            for p in range(o_ref.shape[0]):
                o_ref[p] = val[:, p * w:(p + 1) * w]
        else:
            o_ref[...] = val

    def body(a_ref, b_ref, o_ref, *acc):
        if nk == 1:
            store(o_ref, lax.dot_general(load(a_ref, a), load(b_ref, b), dn, preferred_element_type=F32))
            return
        k, acc_ref = pl.program_id(2), acc[0]

        @pl.when(k == 0)
        def _():
            acc_ref[...] = jnp.zeros_like(acc_ref)

        acc_ref[...] += lax.dot_general(load(a_ref, a), load(b_ref, b), dn, preferred_element_type=F32)

        @pl.when(k == nk - 1)
        def _():
            store(o_ref, acc_ref[...])

    if form == "tn":
        a_spec = _view_spec(a, tk, tm, lambda i, j, k: (k, i))
    else:
        a_spec = _view_spec(a, tm, tk, lambda i, j, k: (i, k))
    if form == "nt":
        b_spec = _view_spec(b, tn, tk, lambda i, j, k: (j, k))
    else:
        b_spec = _view_spec(b, tk, tn, lambda i, j, k: (k, j))
    return pl.pallas_call(
        body, name=name, grid=(m // tm, n // tn, nk),
        in_specs=[a_spec, b_spec], out_specs=_view_spec(o, tm, tn, lambda i, j, k: (i, j)), out_shape=o["a"],
        scratch_shapes=[pltpu.VMEM((tm, tn), F32)] if nk > 1 else [],
        compiler_params=pltpu.CompilerParams(dimension_semantics=("parallel", "parallel", "arbitrary"),
                                             vmem_limit_bytes=MM_VMEM_LIMIT),
    )(a["a"], b["a"])


def X(arr, w=None, co=0, ro=0, split=1, planes=False):
    return dict(a=arr, w=arr.shape[-1] if w is None else w, co=co, ro=ro, split=2 if planes else split,
                mode="planes" if planes else "cols")


def P(arr, per_example=False, w=None, split=1, rows=False):
    return dict(a=arr, e=per_example, w=arr.shape[-1] if w is None else w, split=arr.shape[-2] if rows else split,
                mode="rows" if rows else "cols")


def _pieces(ref, s):
    if s["mode"] == "planes":
        return [ref[0], ref[1]]
    if s["mode"] == "rows":
        return [ref[i:i + 1, :] for i in range(s["split"])]
    w = ref.shape[-1] // s["split"]
    return [ref[:, i * w:(i + 1) * w] for i in range(s["split"])]


def _store(ref, pieces, s, accumulate=False):
    w = ref.shape[-1] // len(pieces)
    for i, p in enumerate(pieces):
        at = (i,) if s["mode"] == "planes" else (slice(i, i + 1),) if s["mode"] == "rows" else (slice(None), slice(i * w, (i + 1) * w))
        if accumulate:
            ref[at] += p.astype(ref.dtype)
        else:
            ref[at] = p.astype(ref.dtype)


def rowwise(name, f, xs, ps, *, tm, nt, nc=1, outs=None, douts=None, dx=None, dp=None):
    bsz = xs[0]["a"].shape[0]
    fwd = douts is None
    nx, np_ = len(xs), len(ps)
    douts = [] if fwd else douts
    dx = {} if fwd else dx
    dp = [] if fwd else dp

    def x_spec(s):
        if s["mode"] == "planes":
            return pl.BlockSpec((None, 2, tm, s["w"]), lambda c, b, t, s=s: (b, 0, t + s["ro"], c + s["co"]))
        return pl.BlockSpec((None, tm, s["w"]), lambda c, b, t, s=s: (b, t + s["ro"], c + s["co"]))

    def x_out(s, dt):
        if s["mode"] == "planes":
            return (jax.ShapeDtypeStruct((bsz, 2, nt * tm, nc * s["w"]), dt),
                    pl.BlockSpec((None, 2, tm, s["w"]), lambda c, b, t: (b, 0, t, c)))
        return (jax.ShapeDtypeStruct((bsz, nt * tm, nc * s["w"]), dt), pl.BlockSpec((None, tm, s["w"]), lambda c, b, t: (b, t, c)))

    def p_spec(s):
        r = s["a"].shape[-2]
        if s["e"]:
            return pl.BlockSpec((None, r, s["w"]), lambda c, b, t: (b, 0, c))
        return pl.BlockSpec((r, s["w"]), lambda c, b, t: (0, c))

    in_specs = [x_spec(s) for s in xs] + [p_spec(s) for s in ps] + [x_spec(s) for s in douts]
    operands = [s["a"] for s in xs] + [s["a"] for s in ps] + [s["a"] for s in douts]
    if fwd:
        out_modes = [dict(mode="cols", split=sp) for (_, _, sp) in outs]
        out_shape = [jax.ShapeDtypeStruct((bsz, nt * tm, nc * w), dt) for (w, dt, _) in outs]
        out_specs = [pl.BlockSpec((None, tm, w), lambda c, b, t: (b, t, c)) for (w, _, _) in outs]
    else:
        dx_outs = [x_out(xs[i], dt) for i, dt in dx.items()]
        out_shape, out_specs = [o[0] for o in dx_outs], [o[1] for o in dx_outs]
        for j in dp:
            s = ps[j]
            r = s["a"].shape[-2]
            if s["e"]:
                out_shape.append(jax.ShapeDtypeStruct((bsz, r, nc * s["w"]), F32))
                out_specs.append(pl.BlockSpec((None, r, s["w"]), lambda c, b, t: (b, 0, c)))
            else:
                out_shape.append(jax.ShapeDtypeStruct((r, nc * s["w"]), F32))
                out_specs.append(pl.BlockSpec((r, s["w"]), lambda c, b, t: (0, c)))

    def body(*refs):
        x_refs, p_refs = refs[:nx], refs[nx:nx + np_]
        d_refs = refs[nx + np_:nx + np_ + len(douts)]
        o_refs = refs[nx + np_ + len(douts):]
        xv = [[p.astype(F32) for p in _pieces(r, s)] for r, s in zip(x_refs, xs)]
        pv = [[p.astype(F32) for p in _pieces(r, s)] for r, s in zip(p_refs, ps)]
        if fwd:
            for r, pieces, s in zip(o_refs, f(xv, pv), out_modes):
                _store(r, pieces, s)
            return
        _, vjp = jax.vjp(f, xv, pv)
        cot = [[p.astype(F32) for p in _pieces(r, s)] for r, s in zip(d_refs, douts)]
        dxv, dpv = vjp(cot)
        for r, i in zip(o_refs, dx):
            _store(r, dxv[i], xs[i])
        b, t = pl.program_id(1), pl.program_id(2)
        for r, j in zip(o_refs[len(dx):], dp):
            first = (t == 0) if ps[j]["e"] else jnp.logical_and(b == 0, t == 0)

            @pl.when(first)
            def _(r=r, j=j):
                _store(r, dpv[j], ps[j])

            @pl.when(jnp.logical_not(first))
            def _(r=r, j=j):
                _store(r, dpv[j], ps[j], accumulate=True)

    res = pl.pallas_call(
        body, name=name, grid=(nc, bsz, nt), in_specs=in_specs, out_specs=out_specs, out_shape=out_shape,
        compiler_params=pltpu.CompilerParams(dimension_semantics=("arbitrary", "arbitrary", "arbitrary")),
    )(*operands)
    return res


def _keep_rows(a, shift, keep):
    n = a.shape[0]
    t = lax.broadcasted_iota(jnp.int32, a.shape, 0)
    return jnp.where(keep(t, n), pltpu.roll(a, shift % n, 0), 0.0)


def _shift_pair(step, keep_prev, keep_next):
    @jax.custom_vjp
    def prev(a):
        return _keep_rows(a, step, keep_prev)

    @jax.custom_vjp
    def nxt(a):
        return _keep_rows(a, -step, keep_next)

    prev.defvjp(lambda a: (prev(a), None), lambda _, g: (nxt(g),))
    nxt.defvjp(lambda a: (nxt(a), None), lambda _, g: (prev(g),))
    return prev, nxt


prev_tok, next_tok = _shift_pair(1, lambda t, n: t % GRID_W != 0, lambda t, n: t % GRID_W != GRID_W - 1)
prev_row, next_row = _shift_pair(GRID_W, lambda t, n: t >= GRID_W, lambda t, n: t < n - GRID_W)


@jax.custom_vjp
def bdot(a, w):
    return jnp.dot(a.astype(BF16), w.astype(BF16), preferred_element_type=F32)


def _bdot_bwd(res, g):
    a, w = res
    gb = g.astype(BF16)
    da = lax.dot_general(gb, w.astype(BF16), (((1,), (1,)), ((), ())), preferred_element_type=F32)
    dw = lax.dot_general(a.astype(BF16), gb, (((0,), (0,)), ((), ())), preferred_element_type=F32)
    return da, dw


bdot.defvjp(lambda a, w: (bdot(a, w), (a, w)), _bdot_bwd)


@jax.custom_vjp
def log_sigmoid(z):
    return jnp.minimum(z, 0.0) - jnp.log(1.0 + jnp.exp(-jnp.abs(z)))


def _lsig_bwd(z, g):
    e = jnp.exp(-jnp.abs(z))
    return (g * jnp.where(z >= 0, e, 1.0) / (1.0 + e),)


log_sigmoid.defvjp(lambda z: (log_sigmoid(z), z), _lsig_bwd)


def silu(x):
    return x * jax.nn.sigmoid(x)


def _rms(x):
    return x * lax.rsqrt(jnp.mean(x * x, axis=-1, keepdims=True) + EPS)


def _mod(x, gain, shift, scale):
    return _rms(x) * gain * (1.0 + scale) + shift


def f_mod(xs, ps):
    ((h,),), ((gain,), (shift,), (scale,)) = xs, ps
    return [[_mod(h, gain, shift, scale)], [h]]


def f_res_mod(xs, ps):
    ((h,), (y,)), ((gate,), (gain,), (shift,), (scale,)) = xs, ps
    h1 = h + gate * y
    return [[h1], [_mod(h1, gain, shift, scale)]]


def f_ffn_mid(xs, ps):
    ((ua, ug),), ((w0a, w0g), (w1a, w1g), (w2a, w2g), (ba, bg)) = xs, ps
    a = w0a * prev_row(ua) + w1a * ua + w2a * next_row(ua) + ba
    g = w0g * prev_row(ug) + w1g * ug + w2g * next_row(ug) + bg
    return [[a * silu(g)]]


def f_sc_mid(xs, ps):
    ((bg, cg, v),), ((w0,), (w1,), (w2,)) = xs, ps
    z = cg * v
    return [[bg * (w0 * prev_tok(z) + w1 * z + w2 * next_tok(z))]]


def f_decay(xs, ps):
    ((a,),), ((wd,), (bd,)) = xs, ps
    return [[log_sigmoid(bdot(a, wd) + bd) / TAU]]


def f_gla_post(xs, ps):
    (of, ob, g), ((gain,),) = xs, ps
    return [[_rms(a + b) * gain * silu(c) for a, b, c in zip(of, ob, g)]]


NCH = TT // CHUNK
CTX_CH = CTX // CHUNK
_NT = (((1,), (1,)), ((), ()))
_TN = (((0,), (0,)), ((), ()))
_NN = (((1,), (0,)), ((), ()))


def _chunk_of(d, j):
    return jnp.where(d == 0, j, jnp.where(j < CTX_CH, CTX_CH - 1 - j, NCH + CTX_CH - 1 - j))


def _dot(a, b, dn):
    return lax.dot_general(a, b, dn, preferred_element_type=F32)


def _cumsum_rows(g, suffix):
    n = g.shape[0]
    row = lax.broadcasted_iota(jnp.int32, g.shape, 0)
    s = 1
    while s < n:
        if suffix:
            g = g + jnp.where(row < n - s, pltpu.roll(g, n - s, 0), 0.0)
        else:
            g = g + jnp.where(row >= s, pltpu.roll(g, s, 0), 0.0)
        s *= 2
    return g


def _causal(backward):
    row = lax.broadcasted_iota(jnp.int32, (CHUNK, CHUNK), 0)
    col = lax.broadcasted_iota(jnp.int32, (CHUNK, CHUNK), 1)
    return col >= row if backward else col <= row


def _gla_in_specs(bsz, rev):
    def blk(d, j):
        return _chunk_of(d, (NCH - 1 - j) if rev else j)

    return [
        pl.BlockSpec((bsz, CHUNK, KD), lambda d, j: (0, blk(d, j), 0)),
        pl.BlockSpec((bsz, CHUNK, KD), lambda d, j: (0, blk(d, j), 1)),
        pl.BlockSpec((bsz, CHUNK, VD), lambda d, j: (0, blk(d, j), 1)),
        pl.BlockSpec((bsz, CHUNK, KD), lambda d, j: (0, blk(d, j), d)),
    ], blk


def gla_fwd(pcat, la):
    bsz = pcat.shape[0]
    in_specs, blk = _gla_in_specs(bsz, False)

    def body(q_ref, k_ref, v_ref, la_ref, o_ref, s_ref, st):
        d, j = pl.program_id(0), pl.program_id(1)

        @pl.when(j == 0)
        def _():
            st[...] = jnp.zeros_like(st)

        s_ref[...] = st[...]

        def scan(backward):
            causal = _causal(backward)
            for e in range(bsz):
                g_all = la_ref[e]
                b_all = _cumsum_rows(g_all, backward)
                bl_all = jnp.sum(g_all, axis=0, keepdims=True)
                qs_all = (q_ref[e] * (HK ** -0.5) * jnp.exp(b_all)).astype(BF16)
                ks_all = (k_ref[e] * jnp.exp(-b_all)).astype(BF16)
                kd_all = (k_ref[e] * jnp.exp(bl_all - b_all)).astype(BF16)
                el_all = jnp.exp(bl_all)
                for h in range(HEADS):
                    ks_, vs_ = slice(h * HK, (h + 1) * HK), slice(h * HV, (h + 1) * HV)
                    qs, ks, kd, v = qs_all[:, ks_], ks_all[:, ks_], kd_all[:, ks_], v_ref[e, :, vs_].astype(BF16)
                    s = st[e, h]
                    att = jnp.where(causal, _dot(qs, ks, _NT), 0.0).astype(BF16)
                    o_ref[e, :, vs_] = _dot(qs, s.astype(BF16), _NT) + _dot(att, v, _NN)
                    st[e, h] = el_all[:, ks_] * s + _dot(v, kd, _TN)

        @pl.when(d == 0)
        def _():
            scan(False)

        @pl.when(d == 1)
        def _():
            scan(True)

    return pl.pallas_call(
        body, name="gla_fwd", grid=(2, NCH), in_specs=in_specs,
        out_specs=[pl.BlockSpec((bsz, CHUNK, VD), lambda d, j: (0, blk(d, j), d)),
                   pl.BlockSpec((bsz, None, None, HEADS, HV, HK), lambda d, j: (0, d, j, 0, 0, 0))],
        out_shape=[jax.ShapeDtypeStruct((bsz, TT, 2 * VD), F32), jax.ShapeDtypeStruct((bsz, 2, NCH, HEADS, HV, HK), F32)],
        scratch_shapes=[pltpu.VMEM((bsz, HEADS, HV, HK), F32)],
        compiler_params=pltpu.CompilerParams(dimension_semantics=("arbitrary", "arbitrary")),
    )(pcat, pcat, pcat, la)


def gla_bwd(pcat, la, s_all, do):
    bsz = pcat.shape[0]
    in_specs, blk = _gla_in_specs(bsz, True)
    in_specs += [
        pl.BlockSpec((bsz, None, None, HEADS, HV, HK), lambda d, j: (0, d, NCH - 1 - j, 0, 0, 0)),
        pl.BlockSpec((bsz, CHUNK, VD), lambda d, j: (0, jnp.maximum(blk(d, j) - CTX_CH, 0), 0)),
    ]

    def body(q_ref, k_ref, v_ref, la_ref, s_ref, do_ref, dq_ref, dk_ref, dv_ref, dla_ref, dst):
        d, j = pl.program_id(0), pl.program_id(1)

        @pl.when(j == 0)
        def _():
            dst[...] = jnp.zeros_like(dst)

        latent = blk(d, j) >= CTX_CH
        scale = HK ** -0.5

        def scan(backward):
            causal = _causal(backward)
            for e in range(bsz):
                g_all = la_ref[e]
                b_all = _cumsum_rows(g_all, backward)
                bl_all = jnp.sum(g_all, axis=0, keepdims=True)
                ex_all, ei_all, ed_all, el_all = jnp.exp(b_all), jnp.exp(-b_all), jnp.exp(bl_all - b_all), jnp.exp(bl_all)
                qs_all, ks_all, kd_all = q_ref[e] * scale * ex_all, k_ref[e] * ei_all, k_ref[e] * ed_all
                qsb_all, ksb_all, kdb_all = qs_all.astype(BF16), ks_all.astype(BF16), kd_all.astype(BF16)
                db_parts, dbl_parts = [], []
                for h in range(HEADS):
                    ks_, vs_ = slice(h * HK, (h + 1) * HK), slice(h * HV, (h + 1) * HV)
                    qs, ks, kd, el = qs_all[:, ks_], ks_all[:, ks_], kd_all[:, ks_], el_all[:, ks_]
                    qsb, ksb, kdb, v = qsb_all[:, ks_], ksb_all[:, ks_], kdb_all[:, ks_], v_ref[e, :, vs_].astype(BF16)
                    s, ds1 = s_ref[e, h], dst[e, h]
                    sb, ds1b = s.astype(BF16), ds1.astype(BF16)
                    dob = jnp.where(latent, do_ref[e, :, vs_], 0.0).astype(BF16)
                    att = jnp.where(causal, _dot(qsb, ksb, _NT), 0.0).astype(BF16)
                    datt = jnp.where(causal, _dot(dob, v, _NT), 0.0).astype(BF16)
                    dqs = _dot(dob, sb, _NN) + _dot(datt, ksb, _NN)
                    dks = _dot(datt, qsb, _TN)
                    dv_ref[e, :, vs_] = _dot(att, dob, _TN) + _dot(kdb, ds1b, _NT)
                    dkd = _dot(v, ds1b, _NN)
                    dst[e, h] = _dot(dob, qsb, _TN) + el * ds1
                    del_ = jnp.sum(s * ds1, axis=0, keepdims=True)
                    dq_ref[e, :, ks_] = dqs * ex_all[:, ks_] * scale
                    dk_ref[e, :, ks_] = dks * ei_all[:, ks_] + dkd * ed_all[:, ks_]
                    db_parts.append(dqs * qs - dks * ks - dkd * kd)
                    dbl_parts.append(jnp.sum(dkd * kd, axis=0, keepdims=True) + del_ * el)
                dla_ref[e] = _cumsum_rows(jnp.concatenate(db_parts, -1), not backward) + jnp.concatenate(dbl_parts, -1)

        @pl.when(d == 0)
        def _():
            scan(False)

        @pl.when(d == 1)
        def _():
            scan(True)

    return pl.pallas_call(
        body, name="gla_bwd", grid=(2, NCH), in_specs=in_specs,
        out_specs=[pl.BlockSpec((None, bsz, CHUNK, KD), lambda d, j: (d, 0, blk(d, j), 0)),
                   pl.BlockSpec((None, bsz, CHUNK, KD), lambda d, j: (d, 0, blk(d, j), 0)),
                   pl.BlockSpec((None, bsz, CHUNK, VD), lambda d, j: (d, 0, blk(d, j), 0)),
                   pl.BlockSpec((bsz, CHUNK, KD), lambda d, j: (0, blk(d, j), d))],
        out_shape=[jax.ShapeDtypeStruct((2, bsz, TT, KD), F32), jax.ShapeDtypeStruct((2, bsz, TT, KD), F32),
                   jax.ShapeDtypeStruct((2, bsz, TT, VD), F32), jax.ShapeDtypeStruct((bsz, TT, 2 * KD), F32)],
        scratch_shapes=[pltpu.VMEM((bsz, HEADS, HV, HK), F32)],
        compiler_params=pltpu.CompilerParams(dimension_semantics=("arbitrary", "arbitrary")),
    )(pcat, pcat, pcat, la, s_all, do)


def gla_combine(dq2, dk2, dv2, dgate, dpa):
    bsz = dgate.shape[0]
    tm = CTX

    def body(dq_ref, dk_ref, dv_ref, dg_ref, dpa_ref, o_ref):
        t = pl.program_id(1)
        o_ref[:, 0:KD] = (dq_ref[0] + dq_ref[1]).astype(BF16)
        o_ref[:, KD:2 * KD] = (dk_ref[0] + dk_ref[1]).astype(BF16)
        o_ref[:, 2 * KD:2 * KD + VD] = (dv_ref[0] + dv_ref[1]).astype(BF16)
        o_ref[:, 2 * KD + VD:2 * KD + 2 * VD] = jnp.where(t > 0, dg_ref[...], 0).astype(BF16)
        o_ref[:, 2 * KD + 2 * VD:] = dpa_ref[...].astype(BF16)

    return pl.pallas_call(
        body, name="gla_combine", grid=(bsz, TT // tm),
        in_specs=[pl.BlockSpec((2, None, tm, KD), lambda b, t: (0, b, t, 0)),
                  pl.BlockSpec((2, None, tm, KD), lambda b, t: (0, b, t, 0)),
                  pl.BlockSpec((2, None, tm, VD), lambda b, t: (0, b, t, 0)),
                  pl.BlockSpec((None, tm, VD), lambda b, t: (b, jnp.maximum(t - 1, 0), 0)),
                  pl.BlockSpec((None, tm, 128), lambda b, t: (b, t, 0))],
        out_specs=pl.BlockSpec((None, tm, GLA_IN_PAD), lambda b, t: (b, t, 0)),
        out_shape=jax.ShapeDtypeStruct((bsz, TT, GLA_IN_PAD), BF16),
        compiler_params=pltpu.CompilerParams(dimension_semantics=("arbitrary", "arbitrary")),
    )(dq2, dk2, dv2, dgate, dpa)


def final_loss(h1, fo, gate, gain, tgt):
    bsz, t_len, _ = h1.shape
    tm = 256

    def body(h_ref, f_ref, gate_ref, gain_ref, tgt_ref, loss_ref, dh_ref, df_ref, dgate_ref, dgain_ref):
        b, t = pl.program_id(0), pl.program_id(1)
        target = tgt_ref[...]

        def core(h, fo_, gate_, gain_):
            e = _rms(h + gate_ * fo_) * gain_ - target
            return jnp.sum(0.5 * jnp.sum(e * e, axis=-1, keepdims=True) / D, axis=0, keepdims=True)

        loss, vjp = jax.vjp(core, h_ref[...], f_ref[...], gate_ref[...], gain_ref[...])
        dh, df, dgate, dgain = vjp(jnp.ones((1, 1), F32))
        dh_ref[...] = dh
        df_ref[...] = df.astype(BF16)
        first = jnp.logical_and(b == 0, t == 0)

        @pl.when(first)
        def _():
            loss_ref[...] = jnp.broadcast_to(loss, loss_ref.shape)
            dgain_ref[...] = dgain

        @pl.when(jnp.logical_not(first))
        def _():
            loss_ref[...] += jnp.broadcast_to(loss, loss_ref.shape)
            dgain_ref[...] += dgain

        @pl.when(t == 0)
        def _():
            dgate_ref[...] = dgate

        @pl.when(t > 0)
        def _():
            dgate_ref[...] += dgate

    tile = pl.BlockSpec((None, tm, D), lambda b, t: (b, t, 0))
    per_ex = pl.BlockSpec((None, 1, D), lambda b, t: (b, 0, 0))
    shared = pl.BlockSpec((1, D), lambda b, t: (0, 0))
    return pl.pallas_call(
        body, name="final_loss", grid=(bsz, t_len // tm),
        in_specs=[tile, tile, per_ex, shared, tile],
        out_specs=[pl.BlockSpec((8, 128), lambda b, t: (0, 0)), tile, tile, per_ex, shared],
        out_shape=[jax.ShapeDtypeStruct((8, 128), F32), jax.ShapeDtypeStruct(h1.shape, F32),
                   jax.ShapeDtypeStruct(h1.shape, BF16), jax.ShapeDtypeStruct((bsz, 1, D), F32),
                   jax.ShapeDtypeStruct((1, D), F32)],
        compiler_params=pltpu.CompilerParams(dimension_semantics=("arbitrary", "arbitrary")),
    )(h1, fo, gate, gain, tgt)


ADA_ROWS = 24
ADA_CTX_ROW = 16
ADA_COLS = 6 * D // N_DEV


def ada_fwd(cond, w, b):
    def body(c_ref, w_ref, b_ref, o_ref):
        s = silu(c_ref[...]).astype(BF16)
        o_ref[...] = jnp.dot(s, w_ref[...].astype(BF16), preferred_element_type=F32) + b_ref[...]

    return pl.pallas_call(
        body, name="ada_fwd", grid=(2,),
        in_specs=[pl.BlockSpec((ADA_ROWS, D), lambda i: (0, 0)), pl.BlockSpec((None, D, ADA_COLS), lambda i: (i, 0, 0)),
                  pl.BlockSpec((None, 1, ADA_COLS), lambda i: (i, 0, 0))],
        out_specs=pl.BlockSpec((None, ADA_ROWS, ADA_COLS), lambda i: (i, 0, 0)),
        out_shape=jax.ShapeDtypeStruct((2, ADA_ROWS, ADA_COLS), F32),
    )(cond, w, b)


def ada_bwd(cond, dm_mine, dm_full, w):
    def body(c_ref, dm_ref, dmf_ref, w_ref, gw_ref, gb_ref, cp_ref):
        i = pl.program_id(0)
        s = silu(c_ref[...]).astype(BF16)
        dm = dm_ref[...].astype(BF16)
        gw_ref[...] = _dot(s, dm, _TN)
        gb_ref[...] = jnp.sum(dmf_ref[...], axis=0, keepdims=True)

        @pl.when(i == 0)
        def _():
            cp_ref[...] = _dot(dm_ref[ADA_CTX_ROW:, :].astype(BF16), w_ref[...].astype(BF16), _NT)

    return pl.pallas_call(
        body, name="ada_bwd", grid=(2,),
        in_specs=[pl.BlockSpec((ADA_ROWS, D), lambda i: (0, 0)), pl.BlockSpec((None, ADA_ROWS, ADA_COLS), lambda i: (i, 0, 0)),
                  pl.BlockSpec((None, ADA_ROWS, 6 * D), lambda i: (i, 0, 0)), pl.BlockSpec((None, D, ADA_COLS), lambda i: (i, 0, 0))],
        out_specs=[pl.BlockSpec((None, D, ADA_COLS), lambda i: (i, 0, 0)), pl.BlockSpec((None, 1, 6 * D), lambda i: (i, 0, 0)),
                   pl.BlockSpec((ADA_ROWS - ADA_CTX_ROW, D), lambda i: (0, 0))],
        out_shape=[jax.ShapeDtypeStruct((2, D, ADA_COLS), F32), jax.ShapeDtypeStruct((2, 1, 6 * D), F32),
                   jax.ShapeDtypeStruct((ADA_ROWS - ADA_CTX_ROW, D), F32)],
        compiler_params=pltpu.CompilerParams(dimension_semantics=("arbitrary",)),
    )(cond, dm_mine, dm_full, w)


def cctx_grad(parts, c_ctx):
    def body(p_ref, c_ref, o_ref):
        tot = p_ref[0:1, :]
        for i in range(1, N_DEV):
            tot = tot + p_ref[i:i + 1, :]
        c = c_ref[...]
        sg = jax.nn.sigmoid(c)
        o_ref[...] = tot * sg * (1.0 + c * (1.0 - sg))

    return pl.pallas_call(body, name="cctx_grad", out_shape=jax.ShapeDtypeStruct((1, D), F32))(parts, c_ctx)


def _row_tile(r):
    for t in (512, 256, 128, 80, 64, 40, 32, 16, 8):
        if r % t == 0:
            return t
    return r


def _slot_sum(ref):
    tot = ref[0].astype(F32)
    for i in range(1, ref.shape[0]):
        tot = tot + ref[i].astype(F32)
    return tot


def sum_slots(name, x):
    s, r, c = x.shape
    tr = _row_tile(r)

    def body(x_ref, o_ref):
        o_ref[...] = _slot_sum(x_ref)

    return pl.pallas_call(
        body, name=name, grid=(r // tr,), in_specs=[pl.BlockSpec((s, tr, c), lambda i: (0, i, 0))],
        out_specs=pl.BlockSpec((tr, c), lambda i: (i, 0)), out_shape=jax.ShapeDtypeStruct((r, c), F32),
    )(x)


def adamw(name, w, g, m, v, layer=None):
    r, c = w.shape[-2:]
    tr = _row_tile(r)
    stacked = g.ndim == 3

    def body(w_ref, g_ref, m_ref, v_ref, go_ref, d_ref, mo_ref, vo_ref):
        gv = _slot_sum(g_ref) if stacked else g_ref[...]
        mn = B1 * m_ref[...] + (1.0 - B1) * gv
        vn = B2 * v_ref[...] + (1.0 - B2) * jnp.square(gv)
        m_hat = mn / (1.0 - B1 ** STEP)
        v_hat = vn / (1.0 - B2 ** STEP)
        go_ref[...] = gv
        d_ref[...] = -LR * (m_hat / (jnp.sqrt(v_hat) + AEPS) + WD * w_ref[...])
        mo_ref[...] = mn
        vo_ref[...] = vn

    tile = pl.BlockSpec((tr, c), lambda i: (i, 0))
    slab = tile if layer is None else pl.BlockSpec((None, tr, c), lambda i: (layer, i, 0))
    g_spec = pl.BlockSpec((g.shape[0], tr, c), lambda i: (0, i, 0)) if stacked else tile
    return pl.pallas_call(
        body, name=name, grid=(r // tr,), in_specs=[slab, g_spec, slab, slab], out_specs=[tile] * 4,
        out_shape=[jax.ShapeDtypeStruct((r, c), F32)] * 4,
    )(w, g, m, v)


def _place():
    return lax.axis_index("x"), lax.axis_index("y"), lax.axis_index("c")


def all_gather(name, x, in_vmem):
    r, c = x.shape
    space = pltpu.VMEM if in_vmem else pl.ANY

    def body(x_ref, out_ref, send_sems, recv_sems, local_sem):
        px, py, pc = _place()
        me, sibling = (px, py, pc), (px, py, 1 - pc)
        chips = [(1 - px, py), (px, 1 - py), (1 - px, 1 - py)]

        def rows(qx, qy, qc):
            return out_ref.at[pl.ds((4 * qx + 2 * qy + qc) * r, r), :]

        def copy(k, block, to, src=None):
            return pltpu.make_async_remote_copy(
                src_ref=rows(*block) if src is None else src, dst_ref=rows(*block),
                send_sem=send_sems.at[k], recv_sem=recv_sems.at[k], device_id=to, device_id_type=MESH)

        mine = pltpu.make_async_copy(x_ref, rows(*me), local_sem)
        mine.start()
        first = [copy(0, me, sibling, src=x_ref)]
        first += [copy(1 + j, me, (*chip, pc), src=x_ref) for j, chip in enumerate(chips)]
        for cp in first:
            cp.start()
        passed = [copy(4 + j, (*chip, pc), sibling) for j, chip in enumerate(chips)]
        for j, chip in enumerate(chips):
            copy(1 + j, (*chip, pc), me).wait_recv()
            passed[j].start()
        copy(0, sibling, me).wait_recv()
        for j, chip in enumerate(chips):
            copy(4 + j, (*chip, 1 - pc), me).wait_recv()
        for cp in first + passed:
            cp.wait_send()
        mine.wait()

    return pl.pallas_call(
        body, name=name, out_shape=jax.ShapeDtypeStruct((N_DEV * r, c), x.dtype),
        in_specs=[pl.BlockSpec(memory_space=space)], out_specs=pl.BlockSpec(memory_space=space),
        scratch_shapes=[pltpu.SemaphoreType.DMA((7,)), pltpu.SemaphoreType.DMA((7,)), pltpu.SemaphoreType.DMA],
    )(x)


_HBM =pl.BlockSpec(memory_space=pltpu.HBM)
_SEM = pl.BlockSpec(memory_space=pltpu.SEMAPHORE)
_EFFECT = pltpu.SideEffectType.DATAFLOW_SIDE_EFFECTING


def _peers():
    px, py, pc = _place()
    return [(1 - px if k & 4 else px, 1 - py if k & 2 else py, 1 - pc if k & 1 else pc) for k in range(1, N_DEV)]


def _slot(dev):
    return 4 * dev[0] + 2 * dev[1] + dev[2]


def _split_copies(src_refs, land_refs, send_sems, recv_sems, gather):
    me = _slot(_place())
    return [pltpu.make_async_remote_copy(
        src_ref=src if gather else src.at[_slot(peer)], dst_ref=land.at[me],
        send_sem=send_sems.at[a * (N_DEV - 1) + k], recv_sem=recv_sems.at[a * (N_DEV - 1) + k],
        device_id=peer, device_id_type=MESH)
        for a, (src, land) in enumerate(zip(src_refs, land_refs)) for k, peer in enumerate(_peers())]


def exchange_start(name, srcs, gather, after):
    n = len(srcs)
    lands = [pltpu.HBM((N_DEV,) + s.shape if gather else s.shape, s.dtype) for s in srcs]

    def body(*refs):
        send_sems, recv_sems = refs[2 * n + 1:2 * n + 3]
        for cp in _split_copies(refs[:n], refs[n:2 * n], send_sems, recv_sems, gather):
            cp.start()
        refs[-1][...] = jnp.zeros_like(refs[-1])

    sems = pltpu.SemaphoreType.DMA((n * (N_DEV - 1),))
    res = pl.pallas_call(
        body, name=name,
        out_shape=(sems, sems, *[pltpu.HBM(s.shape, s.dtype) for s in srcs], *lands, jax.ShapeDtypeStruct((8, 128), F32)),
        in_specs=(_HBM,) * (2 * n) + (pl.BlockSpec(memory_space=pl.ANY),),
        out_specs=(_SEM, _SEM) + (_HBM,) * (2 * n) + (pl.BlockSpec(memory_space=pltpu.VMEM),),
        input_output_aliases={i: 2 + i for i in range(2 * n)},
        compiler_params=pltpu.CompilerParams(has_side_effects=_EFFECT),
    )(*[pltpu.with_memory_space_constraint(s, pltpu.HBM) for s in srcs],
      *[pltpu.with_memory_space_constraint(lax.empty(ld.shape, ld.dtype), pltpu.HBM) for ld in lands], after)
    return res[0], res[1], list(res[2:2 + n]), list(res[2 + n:2 + 2 * n]), res[-1]


def exchange_wait(name, started, after, gather):
    send_sems, recv_sems, srcs, lands, _ = started
    n = len(srcs)
    after = list(after) if isinstance(after, (list, tuple)) else [after]

    def body(*refs):
        send_sems, recv_sems = refs[2 * n:2 * n + 2]
        for cp in _split_copies(refs[:n], refs[n:2 * n], send_sems, recv_sems, gather):
            cp.wait_send()
            cp.wait_recv()

    res = pl.pallas_call(
        body, name=name, out_shape=tuple(pltpu.HBM(a.shape, a.dtype) for a in srcs + lands),
        in_specs=(_HBM,) * (2 * n) + (_SEM, _SEM) + (pl.BlockSpec(memory_space=pl.ANY),) * len(after),
        out_specs=(_HBM,) * (2 * n), input_output_aliases={i: i for i in range(2 * n)},
        compiler_params=pltpu.CompilerParams(has_side_effects=_EFFECT),
    )(*srcs, *lands, send_sems, recv_sems, *after)
    return list(res[:n]), list(res[n:])


NCF = FFN_H // FFN_TC


def _size(shape):
    n = 1
    for s in shape:
        n *= s
    return n


def _padded_rows(n_elems, row_mult):
    return -(-n_elems // (D * row_mult)) * row_mult


def _pack_rows(arrs, dtype, row_mult):
    rows, offs, r0 = [], [], 0
    for a in arrs:
        flat = a.reshape(-1).astype(dtype)
        n = _padded_rows(flat.shape[0], row_mult)
        rows.append(jnp.pad(flat, (0, n * D - flat.shape[0])).reshape(n, D))
        offs.append(r0)
        r0 += n
    return jnp.concatenate(rows, 0), offs


def _unpack_rows(buf, offs, shapes):
    lead, out = buf.shape[:-2], []
    for o, shp in zip(offs, shapes):
        n = _size(shp)
        nr = -(-n // D)
        out.append(buf[..., o:o + nr, :].reshape(lead + (nr * D,))[..., :n].reshape(lead + tuple(shp)))
    return out


def _rows3(w):
    return [w[i:i + 1] for i in range(3)]


def f_mod1(xs, ps):
    return f_mod(xs, ps)[:1]


def kernel(x, c, ctx, c_ctx, ada_w, ada_b, norm_mix, norm_ffn, gla_w_in, gla_w_a2, gla_b_a, gla_head_norm, gla_w_out, sc_w_in, sc_conv_w, sc_w_out, ffn_w_up, ffn_conv_w, ffn_conv_b, ffn_w_down, final_norm, loss_target, m_c_ctx, m_ada_w, m_ada_b, m_norm_mix, m_norm_ffn, m_gla_w_in, m_gla_w_a2, m_gla_b_a, m_gla_head_norm, m_gla_w_out, m_sc_w_in, m_sc_conv_w, m_sc_w_out, m_ffn_w_up, m_ffn_conv_w, m_ffn_conv_b, m_ffn_w_down, m_final_norm, v_c_ctx, v_ada_w, v_ada_b, v_norm_mix, v_norm_ffn, v_gla_w_in, v_gla_w_a2, v_gla_b_a, v_gla_head_norm, v_gla_w_out, v_sc_w_in, v_sc_conv_w, v_sc_w_out, v_ffn_w_up, v_ffn_conv_w, v_ffn_conv_b, v_ffn_w_down, v_final_norm):
    names = ["c_ctx", "ada_w", "ada_b", "norm_mix", "norm_ffn", "gla_w_in", "gla_w_a2", "gla_b_a", "gla_head_norm",
             "gla_w_out", "sc_w_in", "sc_conv_w", "sc_w_out", "ffn_w_up", "ffn_conv_w", "ffn_conv_b", "ffn_w_down",
             "final_norm"]
    w_ = dict(zip(names, [c_ctx, ada_w, ada_b, norm_mix, norm_ffn, gla_w_in, gla_w_a2, gla_b_a, gla_head_norm, gla_w_out,
                          sc_w_in, sc_conv_w, sc_w_out, ffn_w_up, ffn_conv_w, ffn_conv_b, ffn_w_down, final_norm]))
    m_ = dict(zip(names, [m_c_ctx, m_ada_w, m_ada_b, m_norm_mix, m_norm_ffn, m_gla_w_in, m_gla_w_a2, m_gla_b_a,
                          m_gla_head_norm, m_gla_w_out, m_sc_w_in, m_sc_conv_w, m_sc_w_out, m_ffn_w_up, m_ffn_conv_w,
                          m_ffn_conv_b, m_ffn_w_down, m_final_norm]))
    v_ = dict(zip(names, [v_c_ctx, v_ada_w, v_ada_b, v_norm_mix, v_norm_ffn, v_gla_w_in, v_gla_w_a2, v_gla_b_a,
                          v_gla_head_norm, v_gla_w_out, v_sc_w_in, v_sc_conv_w, v_sc_w_out, v_ffn_w_up, v_ffn_conv_w,
                          v_ffn_conv_b, v_ffn_w_down, v_final_norm]))
    me = 4 * lax.axis_index("x") + 2 * lax.axis_index("y") + lax.axis_index("c")
    bsz = x.shape[0]
    tm = 256
    nt = SEQ // tm
    ctx_tiles = CTX // tm
    pe = functools.partial(P, per_example=True)

    groups = {"gla": [("gla_w_in", 0), ("gla_w_out", 0)], "ffn0": [("ffn_w_up", 0), ("ffn_w_down", 0)],
              "l1": [("sc_w_in", 0), ("sc_w_out", 0), ("ffn_w_up", 1), ("ffn_w_down", 1)]}
    ag_started = {}

    def start_gather(g, after):
        ag_started[g] = exchange_start(f"ag_{g}_start", [w_[n][i].astype(BF16) for n, i in groups[g]], True, after)
        return ag_started[g][4]

    small_sharded = [c, gla_w_a2, gla_b_a, sc_conv_w, ffn_conv_w]
    pack0, offs0 = _pack_rows(small_sharded, F32, 8)
    g0 = all_gather("ag_small", pack0, True).reshape(N_DEV, pack0.shape[0], D)
    c_all, wa2_s, ba_s, scw_s, fcw_s = _unpack_rows(g0, offs0, [a.shape for a in small_sharded])
    w_a2 = wa2_s[:, 0].transpose(1, 2, 0, 3).reshape(2, RANK, KD)
    b_a = ba_s[:, 0].transpose(1, 0, 2).reshape(2, KD)
    sc_cw = scw_s[:, 0].transpose(1, 0, 2).reshape(3, D)
    ffn_cw = fcw_s.transpose(1, 2, 0, 3).reshape(2, 3, 2 * FFN_H)

    cond = jnp.concatenate([c_all.reshape(N_DEV * bsz, D), c_ctx[None], jnp.zeros((ADA_ROWS - N_DEV * bsz - 1, D), F32)], 0)
    b_mine = lax.dynamic_slice(ada_b, (0, me * ADA_COLS), (2, ADA_COLS)).reshape(2, 1, ADA_COLS)
    mod_part = ada_fwd(cond, ada_w, b_mine)
    mod = all_gather("ag_mod", mod_part.reshape(2 * ADA_ROWS, ADA_COLS), True)
    mod = mod.reshape(N_DEV, 2, ADA_ROWS, ADA_COLS).transpose(1, 2, 0, 3).reshape(2, ADA_ROWS, 6 * D)
    mods = lax.dynamic_slice(mod, (0, bsz * me, 0), (2, bsz, 6 * D))
    md = [[mods[i][:, k * D:(k + 1) * D].reshape(bsz, 1, D) for k in range(6)] for i in range(2)]
    mc = [mod[0, ADA_CTX_ROW, k * D:(k + 1) * D][None] for k in range(2)]

    tok = mod
    for g in groups:
        tok = start_gather(g, tok)
    norm_mix = norm_mix + tok[0, 0]

    def gathered(g, after):
        mine, lands = exchange_wait(f"ag_{g}_wait", ag_started[g], after, True)
        return [lax.dynamic_update_index_in_dim(ld, mn, me, 0) for ld, mn in zip(lands, mine)]

    s_up, w_down = [None, None], [None, None]
    wd = jnp.zeros((128, 2 * KD), F32).at[:RANK, :KD].set(w_a2[0]).at[RANK:2 * RANK, KD:].set(w_a2[1])
    bd = b_a.reshape(1, 2 * KD)
    scw = _rows3(sc_cw)
    head_gain = gla_head_norm.reshape(1, HV)
    gains_mix = [norm_mix[i][None] for i in range(2)]
    gains_ffn = [norm_ffn[i][None] for i in range(2)]

    def tokens(a2d, t_len):
        return a2d.reshape(bsz, t_len, -1)

    def ffn_params(i):
        rows = [ffn_cw[i][t] for t in range(3)] + [ffn_conv_b[i]]
        return [P(a.reshape(2, FFN_H), w=FFN_TC, rows=True) for a in rows]

    def ffn_fwd(i, hn2):
        u = mm(f"ffn_up{i}", V(hn2, "tok"), V(s_up[i], "cols"), out="planes", out_dtype=BF16, planes_t=SEQ)
        act = rowwise(f"ffn_mid{i}", f_ffn_mid, [X(u, w=FFN_TC, planes=True)], ffn_params(i), tm=SEQ, nt=1, nc=NCF,
                      outs=[(FFN_TC, BF16, 1)])[0]
        return u, act, tokens(mm(f"ffn_down{i}", V(act, "tok"), V(w_down[i])), SEQ)

    def res_mod_fwd(name, h, y, ps):
        return rowwise(name, f_res_mod, [X(h), X(y)], ps, tm=tm, nt=nt, outs=[(D, F32, 1), (D, BF16, 1)])

    ps_in0 = [P(gains_mix[0]), pe(md[0][0]), pe(md[0][1])]
    ps_ctx = [P(gains_mix[0]), P(mc[0]), P(mc[1])]
    hn0 = rowwise("mod_in0", f_mod, [X(x)], ps_in0, tm=tm, nt=nt, outs=[(D, BF16, 1)])[0]
    hnc = rowwise("mod_ctx", f_mod, [X(ctx)], ps_ctx, tm=tm, nt=ctx_tiles, outs=[(D, BF16, 1)])[0]
    hcat = jnp.concatenate([hnc, hn0], axis=1)
    s_gin, s_gout = gathered("gla", hcat)
    w_gin = V(s_gin, "cols", width=GLA_IN_PAD)
    w_gout = s_gout.reshape(VD, D)
    pcat = tokens(mm("gla_in", V(hcat, "tok"), w_gin), TT)
    pa_x = X(pcat, w=128, co=(GLA_IN_PAD - 128) // 128)
    la = rowwise("gla_decay", f_decay, [pa_x], [P(wd), P(bd)], tm=tm, nt=TT // tm, outs=[(2 * KD, F32, 1)])[0]
    o2, s_all = gla_fwd(pcat, la)
    post_xs = [X(o2, w=VD, co=0, ro=ctx_tiles, split=HEADS), X(o2, w=VD, co=1, ro=ctx_tiles, split=HEADS),
               X(pcat, w=VD, co=2, ro=ctx_tiles, split=HEADS)]
    yin0 = rowwise("gla_post", f_gla_post, post_xs, [P(head_gain)], tm=tm, nt=nt, outs=[(VD, BF16, HEADS)])[0]
    y0 = tokens(mm("gla_out", V(yin0, "tok"), V(w_gout)), SEQ)
    ps_mid0 = [pe(md[0][2]), P(gains_ffn[0]), pe(md[0][3]), pe(md[0][4])]
    h1_0, hn2_0 = res_mod_fwd("res_mod_mid0", x, y0, ps_mid0)
    s_up[0], s_down0 = gathered("ffn0", hn2_0)
    w_down[0] = s_down0.reshape(FFN_H, D)
    u0, act0, fo0 = ffn_fwd(0, hn2_0)
    ps_in1 = [pe(md[0][5]), P(gains_mix[1]), pe(md[1][0]), pe(md[1][1])]
    h2_0, hn1 = res_mod_fwd("res_mod_in1", h1_0, fo0, ps_in1)

    s_sin, s_sout, s_up[1], s_down1 = gathered("l1", hn1)
    w_sout, w_down[1] = s_sout.reshape(D, D), s_down1.reshape(FFN_H, D)
    p1 = tokens(mm("sc_in", V(hn1, "tok"), V(s_sin, "cols")), SEQ)
    sc_ps = [P(a) for a in scw]
    yin1 = rowwise("sc_mid", f_sc_mid, [X(p1, split=3)], sc_ps, tm=tm, nt=nt, outs=[(D, BF16, 1)])[0]
    y1 = tokens(mm("sc_out", V(yin1, "tok"), V(w_sout)), SEQ)
    ps_mid1 = [pe(md[1][2]), P(gains_ffn[1]), pe(md[1][3]), pe(md[1][4])]
    h1_1, hn2_1 = res_mod_fwd("res_mod_mid1", h2_0, y1, ps_mid1)
    u1, act1, fo1 = ffn_fwd(1, hn2_1)
    loss8, dh1_1, dfo1, dm5_1, g_final = final_loss(h1_1, fo1, md[1][5], final_norm[None], loss_target)

    def ffn_bwd(i, u, act, hn2, dfo):
        dact = tokens(mm(f"ffn_down_dx{i}", V(dfo, "tok"), V(w_down[i]), form="nt", out_dtype=BF16), SEQ)
        g_down = mm(f"ffn_down_dw{i}", V(act, "tok"), V(dfo, "tok"), form="tn", out_dtype=BF16)
        r = rowwise(f"ffn_mid_bwd{i}", f_ffn_mid, [X(u, w=FFN_TC, planes=True)], ffn_params(i), tm=SEQ, nt=1, nc=NCF,
                    douts=[X(dact, w=FFN_TC)], dx={0: BF16}, dp=[0, 1, 2, 3])
        du, g_cw, g_cb = r[0], jnp.stack([a.reshape(2 * FFN_H) for a in r[1:4]]), r[4].reshape(1, 2 * FFN_H)
        dhn2 = tokens(mm(f"ffn_up_dx{i}", V(du, "planes"), V(s_up[i], "cols"), form="nt", out_dtype=BF16), SEQ)
        g_up = mm(f"ffn_up_dw{i}", V(hn2, "tok"), V(du, "planes"), form="tn", out="cols", out_dtype=BF16)
        return dhn2, g_up, row_slots(g_down), g_cw, g_cb

    def res_mod_bwd(name, h, y, ps, dh1, dhn):
        return rowwise(name, f_res_mod, [X(h), X(y)], ps, tm=tm, nt=nt, douts=[X(dh1), X(dhn)],
                       dx={0: F32, 1: BF16}, dp=[0, 1, 2, 3])

    def row_slots(g):
        return g.reshape(N_DEV, -1, g.shape[-1])

    a2a_started = {}

    def send_grads(g, slots, after=None):
        a2a_started[g] = exchange_start(f"a2a_{g}_start", list(slots), False, loss8 if after is None else after)
        return a2a_started[g][4][0, 0]

    def after_start(ps, tok):
        return [dict(ps[0], a=ps[0]["a"] + tok)] + ps[1:]

    dhn2_1, g_up1, g_down1, g_fcw1, g_fcb1 = ffn_bwd(1, u1, act1, hn2_1, dfo1)
    dh2_0, dy1, dm2_1, g_nffn1, dm3_1, dm4_1 = res_mod_bwd("res_mod_mid1_bwd", h2_0, y1, ps_mid1, dh1_1, dhn2_1)
    dyin1 = tokens(mm("sc_out_dx", V(dy1, "tok"), V(w_sout), form="nt", out_dtype=BF16), SEQ)
    g_sout = row_slots(mm("sc_out_dw", V(yin1, "tok"), V(dy1, "tok"), form="tn", out_dtype=BF16))
    r = rowwise("sc_mid_bwd", f_sc_mid, [X(p1, split=3)], sc_ps, tm=tm, nt=nt, douts=[X(dyin1)], dx={0: BF16}, dp=[0, 1, 2])
    dp1, g_scw = r[0], jnp.concatenate(r[1:4], 0)
    dhn1 = tokens(mm("sc_in_dx", V(dp1, "tok"), V(s_sin, "cols"), form="nt", out_dtype=BF16), SEQ)
    g_sin = mm("sc_in_dw", V(hn1, "tok"), V(dp1, "tok"), form="tn", out="cols", out_dtype=BF16)
    tok = send_grads("l1", [g_sin, g_sout, g_up1, g_down1])
    dh1_0, dfo0, dm5_0, g_nmix1, dm0_1, dm1_1 = res_mod_bwd("res_mod_in1_bwd", h1_0, fo0, after_start(ps_in1, tok), dh2_0, dhn1)

    dhn2_0, g_up0, g_down0, g_fcw0, g_fcb0 = ffn_bwd(0, u0, act0, hn2_0, dfo0)
    tok = send_grads("ffn0", [g_up0, g_down0])
    dx_res, dy0, dm2_0, g_nffn0, dm3_0, dm4_0 = res_mod_bwd("res_mod_mid0_bwd", x, y0, after_start(ps_mid0, tok), dh1_0, dhn2_0)
    dyin0 = tokens(mm("gla_out_dx", V(dy0, "tok"), V(w_gout), form="nt", out_dtype=BF16), SEQ)
    g_gout = row_slots(mm("gla_out_dw", V(yin0, "tok"), V(dy0, "tok"), form="tn", out_dtype=BF16))
    do, dgate, g_head = rowwise("gla_post_bwd", f_gla_post, post_xs, [P(head_gain)], tm=tm, nt=nt,
                                douts=[X(dyin0, split=HEADS)], dx={0: F32, 2: BF16}, dp=[0])
    dq2, dk2, dv2, dla = gla_bwd(pcat, la, s_all, do)
    dpa, g_wd, g_bd = rowwise("gla_decay_bwd", f_decay, [pa_x], [P(wd), P(bd)], tm=tm, nt=TT // tm, douts=[X(dla)],
                              dx={0: BF16}, dp=[0, 1])
    dpcat = gla_combine(dq2, dk2, dv2, dgate, dpa)
    dhcat = tokens(mm("gla_in_dx", V(dpcat, "tok"), w_gin, form="nt", out_dtype=BF16), TT)
    g_gin = mm("gla_in_dw", V(hcat, "tok"), V(dpcat, "tok"), form="tn", out="cols", out_dtype=BF16, shard_n=GLA_IN // N_DEV)
    grad_x, g_nmix0, dm0_0, dm1_0 = rowwise("mod_in0_bwd", f_mod, [X(x)], ps_in0, tm=tm, nt=nt,
                                            douts=[X(dhcat, ro=ctx_tiles), X(dx_res)], dx={0: F32}, dp=[0, 1, 2])
    g_nmix0c, dmc0, dmc1 = rowwise("mod_ctx_bwd", f_mod1, [X(ctx)], ps_ctx, tm=tm, nt=ctx_tiles, douts=[X(dhcat)],
                                   dx={}, dp=[0, 1, 2])

    zero_row = jnp.zeros((1, 4 * D), F32)
    dmod = [jnp.concatenate([jnp.concatenate([a.reshape(bsz, D) for a in dms], 1), ctx_row], 0)
            for dms, ctx_row in (([dm0_0, dm1_0, dm2_0, dm3_0, dm4_0, dm5_0], jnp.concatenate([dmc0, dmc1, zero_row], 1)),
                                 ([dm0_1, dm1_1, dm2_1, dm3_1, dm4_1, dm5_1], jnp.zeros((1, 6 * D), F32)))]
    g_wa2 = jnp.stack([g_wd[:RANK, :KD], g_wd[RANK:2 * RANK, KD:]])
    small_grads = [jnp.stack(dmod), jnp.concatenate([g_nmix0 + g_nmix0c, g_nmix1], 0), jnp.concatenate([g_nffn0, g_nffn1], 0),
                   g_head, jnp.concatenate([g_fcb0, g_fcb1], 0), g_final, g_wa2, g_bd.reshape(2, KD), g_scw,
                   jnp.stack([g_fcw0, g_fcw1]), loss8[:1]]
    pack1, offs1 = _pack_rows(small_grads, F32, 8)
    g1 = all_gather("ag_grads", pack1, True).reshape(N_DEV, pack1.shape[0], D)
    dmod_all = _unpack_rows(g1, offs1[:1], [small_grads[0].shape])[0]
    tot = _unpack_rows(sum_slots("sum_small", g1), offs1, [a.shape for a in small_grads])
    loss = tot[10][0, 0]
    dm_rows = dmod_all[:, :, :bsz].transpose(1, 0, 2, 3).reshape(2, N_DEV * bsz, 6 * D)
    dm_full = jnp.concatenate([dm_rows, tot[0][:, bsz:], jnp.zeros((2, ADA_ROWS - N_DEV * bsz - 1, 6 * D), F32)], 1)
    dm_mine = lax.dynamic_slice(dm_full, (0, 0, me * ADA_COLS), (2, ADA_ROWS, ADA_COLS))
    g_ada_w, g_ada_b, cpart = ada_bwd(cond, dm_mine, dm_full, ada_w)
    cparts = all_gather("ag_cctx", cpart, True).reshape(N_DEV, ADA_ROWS - ADA_CTX_ROW, D)[:, 0]
    g_cctx = cctx_grad(cparts, c_ctx[None])[0]
    tok = send_grads("gla", [g_gin, g_gout], after=g_cctx)

    def my_cols(full, n):
        return lax.dynamic_slice_in_dim(full, me * n, n, axis=full.ndim - 1)

    grads = {
        "c_ctx": g_cctx, "ada_b": g_ada_b.reshape(2, 6 * D), "norm_mix": tot[1], "norm_ffn": tot[2],
        "gla_head_norm": tot[3], "ffn_conv_b": tot[4], "final_norm": tot[5].reshape(D),
        "gla_w_a2": my_cols(tot[6], KD // N_DEV)[None], "gla_b_a": my_cols(tot[7], KD // N_DEV)[None],
        "sc_conv_w": my_cols(tot[8], D // N_DEV)[None], "ffn_conv_w": my_cols(tot[9], 2 * FFN_H // N_DEV),
    }

    res_ada = adamw("adamw_ada", *[a.reshape(2 * D, ADA_COLS) for a in (ada_w, g_ada_w, m_ada_w, v_ada_w)])
    grads["c_ctx"] = g_cctx + tok
    big = ["gla_w_in", "gla_w_out", "sc_w_in", "sc_w_out", "ffn_w_up", "ffn_w_down"]
    small = [n for n in names if n not in big and n != "ada_w"]
    g_small = _pack_rows([grads[n] for n in small], F32, 8)[0]
    res_small = adamw("adamw_small", _pack_rows([w_[n] for n in small], F32, 8)[0], g_small,
                      _pack_rows([m_[n] for n in small], F32, 8)[0], _pack_rows([v_[n] for n in small], F32, 8)[0])
    offs_s = _pack_rows([w_[n] for n in small], F32, 8)[1]

    big_res, done = {}, [res_small[0], res_ada[0]]
    for g in ("l1", "ffn0", "gla"):
        sent, lands = exchange_wait(f"a2a_{g}_wait", a2a_started[g], done, False)
        for (n, i), mine, land in zip(groups[g], sent, lands):
            land = lax.dynamic_update_index_in_dim(land, lax.dynamic_index_in_dim(mine, me, 0, keepdims=False), me, 0)
            big_res[(n, i)] = adamw(f"adamw_{n}{i}", w_[n], land, m_[n], v_[n], layer=i)
            done.append(big_res[(n, i)][0])

    out = {}
    for kind, idx in (("grad", 0), ("delta", 1), ("new_m", 2), ("new_v", 3)):
        vals = {n: jnp.stack([big_res[(n, i)][idx] for i in range(w_[n].shape[0])]) for n in big}
        vals["ada_w"] = res_ada[idx].reshape(ada_w.shape)
        vals.update(zip(small, _unpack_rows(res_small[idx], offs_s, [w_[n].shape for n in small])))
        out[kind] = [vals[n] for n in names]
    return (loss, grad_x, *out["grad"], *out["delta"], *out["new_m"], *out["new_v"])
```

```python
import functools

import jax
import jax.numpy as jnp
from jax import lax
from jax.experimental import pallas as pl
from jax.experimental.pallas import tpu as pltpu

F32 = jnp.float32
BF16 = jnp.bfloat16

N_DEV = 8
D = 1024
SEQ = 2048
CTX = 256
TT = CTX + SEQ
GRID_W = 64
CHUNK = 64
HEADS = 4
HK = 128
HV = 256
KD = 512
VD = 1024
RANK = 16
TAU = 16.0
GLA_IN = 3104
GLA_IN_PAD = 3200
FFN_H = 2560
FFN_TC = 256
EPS = 1e-6
LR, B1, B2, AEPS, WD, STEP = 0.001, 0.9, 0.999, 1e-08, 0.01, 10
MESH = pl.DeviceIdType.MESH


def _blocks(n):
    return [n] + [t for t in range(n - n % 128, 0, -128) if n % t == 0 and t != n]


def V(arr, kind="flat", width=None):
    if kind == "tok":
        return V(arr.reshape(-1, arr.shape[-1]))
    if kind == "flat":
        r, c = arr.shape
        return dict(a=arr, kind=kind, shape=(r, c), rows=_blocks(r), cols=_blocks(c))
    if kind == "planes":
        bsz, _, t, ch = arr.shape
        return dict(a=arr, kind=kind, shape=(bsz * t, 2 * ch), rows=_blocks(t), cols=[2 * ch] + _blocks(ch), t=t, ch=ch)
    _, r, n = arr.shape
    if width is not None:
        return dict(a=arr, kind=kind, shape=(r, width), rows=_blocks(r), cols=[width], n=n, pad=width - N_DEV * n)
    return dict(a=arr, kind=kind, shape=(r, N_DEV * n), rows=_blocks(r), cols=[8 * n, 4 * n, 2 * n], n=n, pad=0)


def _view_spec(v, br, bc, idx):
    if v["kind"] == "flat":
        return pl.BlockSpec((br, bc), idx)
    if v["kind"] == "planes":
        nt = v["t"] // br
        if bc == 2 * v["ch"]:
            return pl.BlockSpec((None, 2, br, v["ch"]), lambda i, j, k: (idx(i, j, k)[0] // nt, 0, idx(i, j, k)[0] % nt, 0))
        nch = v["ch"] // bc

        def at(i, j, k):
            r, c = idx(i, j, k)
            return r // nt, c // nch, r % nt, c % nch
        return pl.BlockSpec((None, None, br, bc), at)
    return pl.BlockSpec(((bc - v["pad"]) // v["n"], br, v["n"]), lambda i, j, k: (idx(i, j, k)[1], idx(i, j, k)[0], 0))


def _out_view(kind, rows, cols, dtype, planes_t=None, shard_n=None):
    if kind == "flat":
        shape = (rows, cols)
    elif kind == "planes":
        shape = (rows // planes_t, 2, planes_t, cols // 2)
    elif shard_n is not None:
        return V(jax.ShapeDtypeStruct((N_DEV, rows, shard_n), dtype), kind, width=cols)
    else:
        shape = (N_DEV, rows, cols // N_DEV)
    return V(jax.ShapeDtypeStruct(shape, dtype), kind)


MM_VMEM_BUDGET = 40 * 2 ** 20
MM_VMEM_LIMIT = 56 * 2 ** 20
MM_MAX_TILE = 1536


def _mm_tiles(m, n, kk, ms, ns, ks, a_bytes, b_bytes, o_bytes):
    best = None
    for tk in ks:
        for tm in [t for t in ms if t <= MM_MAX_TILE] or ms:
            for tn in [t for t in ns if t <= MM_MAX_TILE] or ns:
                one_k = tk == kk
                need = 2 * (tm * tk * a_bytes + tk * tn * b_bytes + tm * tn * o_bytes) + (0 if one_k else tm * tn * 4)
                if need > MM_VMEM_BUDGET:
                    continue
                steps = (m // tm) * (n // tn) * (kk // tk)
                traffic = (m * kk * a_bytes * (1 if one_k else n // tn)
                           + kk * n * b_bytes * (1 if one_k and n == tn else m // tm) + m * n * o_bytes)
                fill = (tm * tk * a_bytes + tk * tn * b_bytes) / 2.5e12
                cost = max(2.0 * m * n * kk / (9e14 if one_k else 6.5e14), traffic / 2.5e12) + steps * 0.4e-6 + fill
                if best is None or cost < best[0]:
                    best = (cost, tm, tn, tk)
    return best[1:]


def mm(name, a, b, form="nn", out="flat", out_dtype=F32, planes_t=None, shard_n=None):
    (m, kk) = a["shape"][::-1] if form == "tn" else a["shape"]
    n = b["shape"][0] if form == "nt" else b["shape"][1]
    assert (b["shape"][1] if form == "nt" else b["shape"][0]) == kk, (name, a["shape"], b["shape"])
    o = _out_view(out, m, n, out_dtype, planes_t, shard_n)
    a_m, a_k = (a["cols"], a["rows"]) if form == "tn" else (a["rows"], a["cols"])
    b_k, b_n = (b["cols"], b["rows"]) if form == "nt" else (b["rows"], b["cols"])
    tm, tn, tk = _mm_tiles(m, n, kk, [t for t in a_m if t in o["rows"]], [t for t in b_n if t in o["cols"]],
                           [t for t in a_k if t in b_k], a["a"].dtype.itemsize, b["a"].dtype.itemsize,
                           jnp.dtype(out_dtype).itemsize)
    nk = kk // tk
    dn = (((0 if form == "tn" else 1,), (1 if form == "nt" else 0,)), ((), ()))

    def load(ref, v):
        if len(ref.shape) == 3:
            pieces = [ref[p].astype(BF16) for p in range(ref.shape[0])]
            if v.get("pad"):
                pieces.append(jnp.zeros(ref.shape[1:2] + (v["pad"],), BF16))
            return jnp.concatenate(pieces, axis=-1)
        return ref[...].astype(BF16)

    def store(o_ref, val):
        val = val.astype(out_dtype)
        if len(o_ref.shape) == 3:
            w = o_ref.shape[-1]
            for p in range(o_ref.shape[0]):
                o_ref[p] = val[:, p * w:(p + 1) * w]
        else:
            o_ref[...] = val

    def body(a_ref, b_ref, o_ref, *acc):
        if nk == 1:
            store(o_ref, lax.dot_general(load(a_ref, a), load(b_ref, b), dn, preferred_element_type=F32))
            return
        k, acc_ref = pl.program_id(2), acc[0]

        @pl.when(k == 0)
        def _():
            acc_ref[...] = jnp.zeros_like(acc_ref)

        acc_ref[...] += lax.dot_general(load(a_ref, a), load(b_ref, b), dn, preferred_element_type=F32)

        @pl.when(k == nk - 1)
        def _():
            store(o_ref, acc_ref[...])

    if form == "tn":
        a_spec = _view_spec(a, tk, tm, lambda i, j, k: (k, i))
    else:
        a_spec = _view_spec(a, tm, tk, lambda i, j, k: (i, k))
    if form == "nt":
        b_spec = _view_spec(b, tn, tk, lambda i, j, k: (j, k))
    else:
        b_spec = _view_spec(b, tk, tn, lambda i, j, k: (k, j))
    return pl.pallas_call(
        body, name=name, grid=(m // tm, n // tn, nk),
        in_specs=[a_spec, b_spec], out_specs=_view_spec(o, tm, tn, lambda i, j, k: (i, j)), out_shape=o["a"],
        scratch_shapes=[pltpu.VMEM((tm, tn), F32)] if nk > 1 else [],
        compiler_params=pltpu.CompilerParams(dimension_semantics=("parallel", "parallel", "arbitrary"),
                                             vmem_limit_bytes=MM_VMEM_LIMIT),
    )(a["a"], b["a"])


def mm_res_mod(name, a, w, h, gate, gain, shift, scale):
    bsz, t_len, kk = a.shape
    tm = 512
    per = t_len // tm

    def body(a_ref, w_ref, h_ref, gate_ref, gain_ref, shift_ref, scale_ref, y_ref, h1_ref, hn_ref):
        y = jnp.dot(a_ref[...].astype(BF16), w_ref[...].astype(BF16), preferred_element_type=F32)
        h1 = h_ref[...] + gate_ref[...] * y
        y_ref[...] = y.astype(BF16)
        h1_ref[...] = h1
        hn_ref[...] = _mod(h1, gain_ref[...], shift_ref[...], scale_ref[...]).astype(BF16)

    def tile(width):
        return pl.BlockSpec((None, tm, width), lambda i: (i // per, i % per, 0))

    per_ex = pl.BlockSpec((None, 1, D), lambda i: (i // per, 0, 0))
    return pl.pallas_call(
        body, name=name, grid=(bsz * per,),
        in_specs=[tile(kk), pl.BlockSpec((kk, D), lambda i: (0, 0)), tile(D), per_ex, pl.BlockSpec((1, D), lambda i: (0, 0)),
                  per_ex, per_ex],
        out_specs=[tile(D)] * 3,
        out_shape=[jax.ShapeDtypeStruct((bsz, t_len, D), BF16), jax.ShapeDtypeStruct((bsz, t_len, D), F32),
                   jax.ShapeDtypeStruct((bsz, t_len, D), BF16)],
        compiler_params=pltpu.CompilerParams(dimension_semantics=("parallel",), vmem_limit_bytes=MM_VMEM_LIMIT),
    )(a, w, h, gate, gain, shift, scale)


def X(arr, w=None, co=0, ro=0, split=1, planes=False):
    return dict(a=arr, w=arr.shape[-1] if w is None else w, co=co, ro=ro, split=2 if planes else split,
                mode="planes" if planes else "cols")


def P(arr, per_example=False, w=None, split=1, rows=False):
    return dict(a=arr, e=per_example, w=arr.shape[-1] if w is None else w, split=arr.shape[-2] if rows else split,
                mode="rows" if rows else "cols")


def _pieces(ref, s):
    if s["mode"] == "planes":
        return [ref[0], ref[1]]
    if s["mode"] == "rows":
        return [ref[i:i + 1, :] for i in range(s["split"])]
    w = ref.shape[-1] // s["split"]
    return [ref[:, i * w:(i + 1) * w] for i in range(s["split"])]


def _store(ref, pieces, s, accumulate=False):
    w = ref.shape[-1] // len(pieces)
    for i, p in enumerate(pieces):
        at = (i,) if s["mode"] == "planes" else (slice(i, i + 1),) if s["mode"] == "rows" else (slice(None), slice(i * w, (i + 1) * w))
        if accumulate:
            ref[at] += p.astype(ref.dtype)
        else:
            ref[at] = p.astype(ref.dtype)


def rowwise(name, f, xs, ps, *, tm, nt, nc=1, outs=None, douts=None, dx=None, dp=None):
    bsz = xs[0]["a"].shape[0]
    fwd = douts is None
    nx, np_ = len(xs), len(ps)
    douts = [] if fwd else douts
    dx = {} if fwd else dx
    dp = [] if fwd else dp

    def x_spec(s):
        if s["mode"] == "planes":
            return pl.BlockSpec((None, 2, tm, s["w"]), lambda c, b, t, s=s: (b, 0, t + s["ro"], c + s["co"]))
        return pl.BlockSpec((None, tm, s["w"]), lambda c, b, t, s=s: (b, t + s["ro"], c + s["co"]))

    def x_out(s, dt):
        if s["mode"] == "planes":
            return (jax.ShapeDtypeStruct((bsz, 2, nt * tm, nc * s["w"]), dt),
                    pl.BlockSpec((None, 2, tm, s["w"]), lambda c, b, t: (b, 0, t, c)))
        return (jax.ShapeDtypeStruct((bsz, nt * tm, nc * s["w"]), dt), pl.BlockSpec((None, tm, s["w"]), lambda c, b, t: (b, t, c)))

    def p_spec(s):
        r = s["a"].shape[-2]
        if s["e"]:
            return pl.BlockSpec((None, r, s["w"]), lambda c, b, t: (b, 0, c))
        return pl.BlockSpec((r, s["w"]), lambda c, b, t: (0, c))

    in_specs = [x_spec(s) for s in xs] + [p_spec(s) for s in ps] + [x_spec(s) for s in douts]
    operands = [s["a"] for s in xs] + [s["a"] for s in ps] + [s["a"] for s in douts]
    if fwd:
        out_modes = [dict(mode="cols", split=sp) for (_, _, sp) in outs]
        out_shape = [jax.ShapeDtypeStruct((bsz, nt * tm, nc * w), dt) for (w, dt, _) in outs]
        out_specs = [pl.BlockSpec((None, tm, w), lambda c, b, t: (b, t, c)) for (w, _, _) in outs]
    else:
        dx_outs = [x_out(xs[i], dt) for i, dt in dx.items()]
        out_shape, out_specs = [o[0] for o in dx_outs], [o[1] for o in dx_outs]
        for j in dp:
            s = ps[j]
            r = s["a"].shape[-2]
            if s["e"]:
                out_shape.append(jax.ShapeDtypeStruct((bsz, r, nc * s["w"]), F32))
                out_specs.append(pl.BlockSpec((None, r, s["w"]), lambda c, b, t: (b, 0, c)))
            else:
                out_shape.append(jax.ShapeDtypeStruct((r, nc * s["w"]), F32))
                out_specs.append(pl.BlockSpec((r, s["w"]), lambda c, b, t: (0, c)))

    def body(*refs):
        x_refs, p_refs = refs[:nx], refs[nx:nx + np_]
        d_refs = refs[nx + np_:nx + np_ + len(douts)]
        o_refs = refs[nx + np_ + len(douts):]
        xv = [[p.astype(F32) for p in _pieces(r, s)] for r, s in zip(x_refs, xs)]
        pv = [[p.astype(F32) for p in _pieces(r, s)] for r, s in zip(p_refs, ps)]
        if fwd:
            for r, pieces, s in zip(o_refs, f(xv, pv), out_modes):
                _store(r, pieces, s)
            return
        _, vjp = jax.vjp(f, xv, pv)
        cot = [[p.astype(F32) for p in _pieces(r, s)] for r, s in zip(d_refs, douts)]
        dxv, dpv = vjp(cot)
        for r, i in zip(o_refs, dx):
            _store(r, dxv[i], xs[i])
        b, t = pl.program_id(1), pl.program_id(2)
        for r, j in zip(o_refs[len(dx):], dp):
            first = (t == 0) if ps[j]["e"] else jnp.logical_and(b == 0, t == 0)

            @pl.when(first)
            def _(r=r, j=j):
                _store(r, dpv[j], ps[j])

            @pl.when(jnp.logical_not(first))
            def _(r=r, j=j):
                _store(r, dpv[j], ps[j], accumulate=True)

    res = pl.pallas_call(
        body, name=name, grid=(nc, bsz, nt), in_specs=in_specs, out_specs=out_specs, out_shape=out_shape,
        compiler_params=pltpu.CompilerParams(dimension_semantics=("arbitrary", "arbitrary", "arbitrary")),
    )(*operands)
    return res


def _keep_rows(a, shift, keep):
    n = a.shape[0]
    t = lax.broadcasted_iota(jnp.int32, a.shape, 0)
    return jnp.where(keep(t, n), pltpu.roll(a, shift % n, 0), 0.0)


def _shift_pair(step, keep_prev, keep_next):
    @jax.custom_vjp
    def prev(a):
        return _keep_rows(a, step, keep_prev)

    @jax.custom_vjp
    def nxt(a):
        return _keep_rows(a, -step, keep_next)

    prev.defvjp(lambda a: (prev(a), None), lambda _, g: (nxt(g),))
    nxt.defvjp(lambda a: (nxt(a), None), lambda _, g: (prev(g),))
    return prev, nxt


prev_tok, next_tok = _shift_pair(1, lambda t, n: t % GRID_W != 0, lambda t, n: t % GRID_W != GRID_W - 1)
prev_row, next_row = _shift_pair(GRID_W, lambda t, n: t >= GRID_W, lambda t, n: t < n - GRID_W)


@jax.custom_vjp
def bdot(a, w):
    return jnp.dot(a.astype(BF16), w.astype(BF16), preferred_element_type=F32)


def _bdot_bwd(res, g):
    a, w = res
    gb = g.astype(BF16)
    da = lax.dot_general(gb, w.astype(BF16), (((1,), (1,)), ((), ())), preferred_element_type=F32)
    dw = lax.dot_general(a.astype(BF16), gb, (((0,), (0,)), ((), ())), preferred_element_type=F32)
    return da, dw


bdot.defvjp(lambda a, w: (bdot(a, w), (a, w)), _bdot_bwd)


@jax.custom_vjp
def log_sigmoid(z):
    return jnp.minimum(z, 0.0) - jnp.log(1.0 + jnp.exp(-jnp.abs(z)))


def _lsig_bwd(z, g):
    e = jnp.exp(-jnp.abs(z))
    return (g * jnp.where(z >= 0, e, 1.0) / (1.0 + e),)


log_sigmoid.defvjp(lambda z: (log_sigmoid(z), z), _lsig_bwd)


def silu(x):
    return x * jax.nn.sigmoid(x)


def _rms(x):
    return x * lax.rsqrt(jnp.mean(x * x, axis=-1, keepdims=True) + EPS)


def _mod(x, gain, shift, scale):
    return _rms(x) * gain * (1.0 + scale) + shift


def f_mod(xs, ps):
    ((h,),), ((gain,), (shift,), (scale,)) = xs, ps
    return [[_mod(h, gain, shift, scale)], [h]]


def f_res_mod(xs, ps):
    ((h,), (y,)), ((gate,), (gain,), (shift,), (scale,)) = xs, ps
    h1 = h + gate * y
    return [[h1], [_mod(h1, gain, shift, scale)]]


def f_ffn_mid(xs, ps):
    ((ua, ug),), ((w0a, w0g), (w1a, w1g), (w2a, w2g), (ba, bg)) = xs, ps
    a = w0a * prev_row(ua) + w1a * ua + w2a * next_row(ua) + ba
    g = w0g * prev_row(ug) + w1g * ug + w2g * next_row(ug) + bg
    return [[a * silu(g)]]


def f_sc_mid(xs, ps):
    ((bg, cg, v),), ((w0,), (w1,), (w2,)) = xs, ps
    z = cg * v
    return [[bg * (w0 * prev_tok(z) + w1 * z + w2 * next_tok(z))]]


def f_decay(xs, ps):
    ((a,),), ((wd,), (bd,)) = xs, ps
    return [[log_sigmoid(bdot(a, wd) + bd) / TAU]]


def f_gla_post(xs, ps):
    (of, ob, g), ((gain,),) = xs, ps
    return [[_rms(a + b) * gain * silu(c) for a, b, c in zip(of, ob, g)]]


NCH = TT // CHUNK
CTX_CH = CTX // CHUNK
_NT = (((1,), (1,)), ((), ()))
_TN = (((0,), (0,)), ((), ()))
_NN = (((1,), (0,)), ((), ()))


def _chunk_of(d, j):
    return jnp.where(d == 0, j, jnp.where(j < CTX_CH, CTX_CH - 1 - j, NCH + CTX_CH - 1 - j))


def _dot(a, b, dn):
    return lax.dot_general(a, b, dn, preferred_element_type=F32)


def _cumsum_rows(g, suffix):
    n = g.shape[0]
    row = lax.broadcasted_iota(jnp.int32, g.shape, 0)
    s = 1
    while s < n:
        if suffix:
            g = g + jnp.where(row < n - s, pltpu.roll(g, n - s, 0), 0.0)
        else:
            g = g + jnp.where(row >= s, pltpu.roll(g, s, 0), 0.0)
        s *= 2
    return g


def _causal(backward):
    row = lax.broadcasted_iota(jnp.int32, (CHUNK, CHUNK), 0)
    col = lax.broadcasted_iota(jnp.int32, (CHUNK, CHUNK), 1)
    return col >= row if backward else col <= row


def _gla_in_specs(bsz, rev):
    def blk(d, j):
        return _chunk_of(d, (NCH - 1 - j) if rev else j)

    return [
        pl.BlockSpec((bsz, CHUNK, KD), lambda d, j: (0, blk(d, j), 0)),
        pl.BlockSpec((bsz, CHUNK, KD), lambda d, j: (0, blk(d, j), 1)),
        pl.BlockSpec((bsz, CHUNK, VD), lambda d, j: (0, blk(d, j), 1)),
        pl.BlockSpec((bsz, CHUNK, KD), lambda d, j: (0, blk(d, j), d)),
    ], blk


def gla_fwd(pcat, la):
    bsz = pcat.shape[0]
    in_specs, blk = _gla_in_specs(bsz, False)

    def body(q_ref, k_ref, v_ref, la_ref, o_ref, s_ref, st):
        d, j = pl.program_id(0), pl.program_id(1)

        @pl.when(j == 0)
        def _():
            st[...] = jnp.zeros_like(st)

        s_ref[...] = st[...]

        def scan(backward):
            causal = _causal(backward)
            for e in range(bsz):
                g_all = la_ref[e]
                b_all = _cumsum_rows(g_all, backward)
                bl_all = jnp.sum(g_all, axis=0, keepdims=True)
                qs_all = (q_ref[e] * (HK ** -0.5) * jnp.exp(b_all)).astype(BF16)
                ks_all = (k_ref[e] * jnp.exp(-b_all)).astype(BF16)
                kd_all = (k_ref[e] * jnp.exp(bl_all - b_all)).astype(BF16)
                el_all = jnp.exp(bl_all)
                for h in range(HEADS):
                    ks_, vs_ = slice(h * HK, (h + 1) * HK), slice(h * HV, (h + 1) * HV)
                    qs, ks, kd, v = qs_all[:, ks_], ks_all[:, ks_], kd_all[:, ks_], v_ref[e, :, vs_].astype(BF16)
                    s = st[e, h]
                    att = jnp.where(causal, _dot(qs, ks, _NT), 0.0).astype(BF16)
                    o_ref[e, :, vs_] = _dot(qs, s.astype(BF16), _NT) + _dot(att, v, _NN)
                    st[e, h] = el_all[:, ks_] * s + _dot(v, kd, _TN)

        @pl.when(d == 0)
        def _():
            scan(False)

        @pl.when(d == 1)
        def _():
            scan(True)

    return pl.pallas_call(
        body, name="gla_fwd", grid=(2, NCH), in_specs=in_specs,
        out_specs=[pl.BlockSpec((bsz, CHUNK, VD), lambda d, j: (0, blk(d, j), d)),
                   pl.BlockSpec((bsz, None, None, HEADS, HV, HK), lambda d, j: (0, d, j, 0, 0, 0))],
        out_shape=[jax.ShapeDtypeStruct((bsz, TT, 2 * VD), F32), jax.ShapeDtypeStruct((bsz, 2, NCH, HEADS, HV, HK), F32)],
        scratch_shapes=[pltpu.VMEM((bsz, HEADS, HV, HK), F32)],
        compiler_params=pltpu.CompilerParams(dimension_semantics=("arbitrary", "arbitrary")),
    )(pcat, pcat, pcat, la)


def gla_bwd(pcat, la, s_all, do):
    bsz = pcat.shape[0]
    in_specs, blk = _gla_in_specs(bsz, True)
    in_specs += [
        pl.BlockSpec((bsz, None, None, HEADS, HV, HK), lambda d, j: (0, d, NCH - 1 - j, 0, 0, 0)),
        pl.BlockSpec((bsz, CHUNK, VD), lambda d, j: (0, jnp.maximum(blk(d, j) - CTX_CH, 0), 0)),
    ]

    def body(q_ref, k_ref, v_ref, la_ref, s_ref, do_ref, dq_ref, dk_ref, dv_ref, dla_ref, dst):
        d, j = pl.program_id(0), pl.program_id(1)

        @pl.when(j == 0)
        def _():
            dst[...] = jnp.zeros_like(dst)

        latent = blk(d, j) >= CTX_CH
        scale = HK ** -0.5

        def scan(backward):
            causal = _causal(backward)
            for e in range(bsz):
                g_all = la_ref[e]
                b_all = _cumsum_rows(g_all, backward)
                bl_all = jnp.sum(g_all, axis=0, keepdims=True)
                ex_all, ei_all, ed_all, el_all = jnp.exp(b_all), jnp.exp(-b_all), jnp.exp(bl_all - b_all), jnp.exp(bl_all)
                qs_all, ks_all, kd_all = q_ref[e] * scale * ex_all, k_ref[e] * ei_all, k_ref[e] * ed_all
                qsb_all, ksb_all, kdb_all = qs_all.astype(BF16), ks_all.astype(BF16), kd_all.astype(BF16)
                db_parts, dbl_parts = [], []
                for h in range(HEADS):
                    ks_, vs_ = slice(h * HK, (h + 1) * HK), slice(h * HV, (h + 1) * HV)
                    qs, ks, kd, el = qs_all[:, ks_], ks_all[:, ks_], kd_all[:, ks_], el_all[:, ks_]
                    qsb, ksb, kdb, v = qsb_all[:, ks_], ksb_all[:, ks_], kdb_all[:, ks_], v_ref[e, :, vs_].astype(BF16)
                    s, ds1 = s_ref[e, h], dst[e, h]
                    sb, ds1b = s.astype(BF16), ds1.astype(BF16)
                    dob = jnp.where(latent, do_ref[e, :, vs_], 0.0).astype(BF16)
                    att = jnp.where(causal, _dot(qsb, ksb, _NT), 0.0).astype(BF16)
                    datt = jnp.where(causal, _dot(dob, v, _NT), 0.0).astype(BF16)
                    dqs = _dot(dob, sb, _NN) + _dot(datt, ksb, _NN)
                    dks = _dot(datt, qsb, _TN)
                    dv_ref[e, :, vs_] = _dot(att, dob, _TN) + _dot(kdb, ds1b, _NT)
                    dkd = _dot(v, ds1b, _NN)
                    dst[e, h] = _dot(dob, qsb, _TN) + el * ds1
                    del_ = jnp.sum(s * ds1, axis=0, keepdims=True)
                    dq_ref[e, :, ks_] = dqs * ex_all[:, ks_] * scale
                    dk_ref[e, :, ks_] = dks * ei_all[:, ks_] + dkd * ed_all[:, ks_]
                    db_parts.append(dqs * qs - dks * ks - dkd * kd)
                    dbl_parts.append(jnp.sum(dkd * kd, axis=0, keepdims=True) + del_ * el)
                dla_ref[e] = _cumsum_rows(jnp.concatenate(db_parts, -1), not backward) + jnp.concatenate(dbl_parts, -1)

        @pl.when(d == 0)
        def _():
            scan(False)

        @pl.when(d == 1)
        def _():
            scan(True)

    return pl.pallas_call(
        body, name="gla_bwd", grid=(2, NCH), in_specs=in_specs,
        out_specs=[pl.BlockSpec((None, bsz, CHUNK, KD), lambda d, j: (d, 0, blk(d, j), 0)),
                   pl.BlockSpec((None, bsz, CHUNK, KD), lambda d, j: (d, 0, blk(d, j), 0)),
                   pl.BlockSpec((None, bsz, CHUNK, VD), lambda d, j: (d, 0, blk(d, j), 0)),
                   pl.BlockSpec((bsz, CHUNK, KD), lambda d, j: (0, blk(d, j), d))],
        out_shape=[jax.ShapeDtypeStruct((2, bsz, TT, KD), F32), jax.ShapeDtypeStruct((2, bsz, TT, KD), F32),
                   jax.ShapeDtypeStruct((2, bsz, TT, VD), F32), jax.ShapeDtypeStruct((bsz, TT, 2 * KD), F32)],
        scratch_shapes=[pltpu.VMEM((bsz, HEADS, HV, HK), F32)],
        compiler_params=pltpu.CompilerParams(dimension_semantics=("arbitrary", "arbitrary")),
    )(pcat, pcat, pcat, la, s_all, do)


def gla_combine(dq2, dk2, dv2, dgate, dpa):
    bsz = dgate.shape[0]
    tm = CTX

    def body(dq_ref, dk_ref, dv_ref, dg_ref, dpa_ref, o_ref):
        t = pl.program_id(1)
        o_ref[:, 0:KD] = (dq_ref[0] + dq_ref[1]).astype(BF16)
        o_ref[:, KD:2 * KD] = (dk_ref[0] + dk_ref[1]).astype(BF16)
        o_ref[:, 2 * KD:2 * KD + VD] = (dv_ref[0] + dv_ref[1]).astype(BF16)
        o_ref[:, 2 * KD + VD:2 * KD + 2 * VD] = jnp.where(t > 0, dg_ref[...], 0).astype(BF16)
        o_ref[:, 2 * KD + 2 * VD:] = dpa_ref[...].astype(BF16)

    return pl.pallas_call(
        body, name="gla_combine", grid=(bsz, TT // tm),
        in_specs=[pl.BlockSpec((2, None, tm, KD), lambda b, t: (0, b, t, 0)),
                  pl.BlockSpec((2, None, tm, KD), lambda b, t: (0, b, t, 0)),
                  pl.BlockSpec((2, None, tm, VD), lambda b, t: (0, b, t, 0)),
                  pl.BlockSpec((None, tm, VD), lambda b, t: (b, jnp.maximum(t - 1, 0), 0)),
                  pl.BlockSpec((None, tm, 128), lambda b, t: (b, t, 0))],
        out_specs=pl.BlockSpec((None, tm, GLA_IN_PAD), lambda b, t: (b, t, 0)),
        out_shape=jax.ShapeDtypeStruct((bsz, TT, GLA_IN_PAD), BF16),
        compiler_params=pltpu.CompilerParams(dimension_semantics=("arbitrary", "arbitrary")),
    )(dq2, dk2, dv2, dgate, dpa)


def final_loss(h1, fo, gate, gain, tgt):
    bsz, t_len, _ = h1.shape
    tm = 256

    def body(h_ref, f_ref, gate_ref, gain_ref, tgt_ref, loss_ref, dh_ref, df_ref, dgate_ref, dgain_ref):
        b, t = pl.program_id(0), pl.program_id(1)
        target = tgt_ref[...]

        def core(h, fo_, gate_, gain_):
            e = _rms(h + gate_ * fo_) * gain_ - target
            return jnp.sum(0.5 * jnp.sum(e * e, axis=-1, keepdims=True) / D, axis=0, keepdims=True)

        loss, vjp = jax.vjp(core, h_ref[...], f_ref[...], gate_ref[...], gain_ref[...])
        dh, df, dgate, dgain = vjp(jnp.ones((1, 1), F32))
        dh_ref[...] = dh
        df_ref[...] = df.astype(BF16)
        first = jnp.logical_and(b == 0, t == 0)

        @pl.when(first)
        def _():
            loss_ref[...] = jnp.broadcast_to(loss, loss_ref.shape)
            dgain_ref[...] = dgain

        @pl.when(jnp.logical_not(first))
        def _():
            loss_ref[...] += jnp.broadcast_to(loss, loss_ref.shape)
            dgain_ref[...] += dgain

        @pl.when(t == 0)
        def _():
            dgate_ref[...] = dgate

        @pl.when(t > 0)
        def _():
            dgate_ref[...] += dgate

    tile = pl.BlockSpec((None, tm, D), lambda b, t: (b, t, 0))
    per_ex = pl.BlockSpec((None, 1, D), lambda b, t: (b, 0, 0))
    shared = pl.BlockSpec((1, D), lambda b, t: (0, 0))
    return pl.pallas_call(
        body, name="final_loss", grid=(bsz, t_len // tm),
        in_specs=[tile, tile, per_ex, shared, tile],
        out_specs=[pl.BlockSpec((8, 128), lambda b, t: (0, 0)), tile, tile, per_ex, shared],
        out_shape=[jax.ShapeDtypeStruct((8, 128), F32), jax.ShapeDtypeStruct(h1.shape, F32),
                   jax.ShapeDtypeStruct(h1.shape, BF16), jax.ShapeDtypeStruct((bsz, 1, D), F32),
                   jax.ShapeDtypeStruct((1, D), F32)],
        compiler_params=pltpu.CompilerParams(dimension_semantics=("arbitrary", "arbitrary")),
    )(h1, fo, gate, gain, tgt)


ADA_ROWS = 24
ADA_CTX_ROW = 16
ADA_COLS = 6 * D // N_DEV


def ada_fwd(cond, w, b):
    def body(c_ref, w_ref, b_ref, o_ref):
        s = silu(c_ref[...]).astype(BF16)
        o_ref[...] = jnp.dot(s, w_ref[...].astype(BF16), preferred_element_type=F32) + b_ref[...]

    return pl.pallas_call(
        body, name="ada_fwd", grid=(2,),
        in_specs=[pl.BlockSpec((ADA_ROWS, D), lambda i: (0, 0)), pl.BlockSpec((None, D, ADA_COLS), lambda i: (i, 0, 0)),
                  pl.BlockSpec((None, 1, ADA_COLS), lambda i: (i, 0, 0))],
        out_specs=pl.BlockSpec((None, ADA_ROWS, ADA_COLS), lambda i: (i, 0, 0)),
        out_shape=jax.ShapeDtypeStruct((2, ADA_ROWS, ADA_COLS), F32),
    )(cond, w, b)


def ada_bwd(cond, dm_mine, dm_full, w):
    def body(c_ref, dm_ref, dmf_ref, w_ref, gw_ref, gb_ref, cp_ref):
        i = pl.program_id(0)
        s = silu(c_ref[...]).astype(BF16)
        dm = dm_ref[...].astype(BF16)
        gw_ref[...] = _dot(s, dm, _TN)
        gb_ref[...] = jnp.sum(dmf_ref[...], axis=0, keepdims=True)

        @pl.when(i == 0)
        def _():
            cp_ref[...] = _dot(dm_ref[ADA_CTX_ROW:, :].astype(BF16), w_ref[...].astype(BF16), _NT)

    return pl.pallas_call(
        body, name="ada_bwd", grid=(2,),
        in_specs=[pl.BlockSpec((ADA_ROWS, D), lambda i: (0, 0)), pl.BlockSpec((None, ADA_ROWS, ADA_COLS), lambda i: (i, 0, 0)),
                  pl.BlockSpec((None, ADA_ROWS, 6 * D), lambda i: (i, 0, 0)), pl.BlockSpec((None, D, ADA_COLS), lambda i: (i, 0, 0))],
        out_specs=[pl.BlockSpec((None, D, ADA_COLS), lambda i: (i, 0, 0)), pl.BlockSpec((None, 1, 6 * D), lambda i: (i, 0, 0)),
                   pl.BlockSpec((ADA_ROWS - ADA_CTX_ROW, D), lambda i: (0, 0))],
        out_shape=[jax.ShapeDtypeStruct((2, D, ADA_COLS), F32), jax.ShapeDtypeStruct((2, 1, 6 * D), F32),
                   jax.ShapeDtypeStruct((ADA_ROWS - ADA_CTX_ROW, D), F32)],
        compiler_params=pltpu.CompilerParams(dimension_semantics=("arbitrary",)),
    )(cond, dm_mine, dm_full, w)


def cctx_grad(parts, c_ctx):
    def body(p_ref, c_ref, o_ref):
        tot = p_ref[0:1, :]
        for i in range(1, N_DEV):
            tot = tot + p_ref[i:i + 1, :]
        c = c_ref[...]
        sg = jax.nn.sigmoid(c)
        o_ref[...] = tot * sg * (1.0 + c * (1.0 - sg))

    return pl.pallas_call(body, name="cctx_grad", out_shape=jax.ShapeDtypeStruct((1, D), F32))(parts, c_ctx)


def _row_tile(r):
    for t in (512, 256, 128, 80, 64, 40, 32, 16, 8):
        if r % t == 0:
            return t
    return r


def _slot_sum(ref):
    tot = ref[0].astype(F32)
    for i in range(1, ref.shape[0]):
        tot = tot + ref[i].astype(F32)
    return tot


def sum_slots(name, x):
    s, r, c = x.shape
    tr = _row_tile(r)

    def body(x_ref, o_ref):
        o_ref[...] = _slot_sum(x_ref)

    return pl.pallas_call(
        body, name=name, grid=(r // tr,), in_specs=[pl.BlockSpec((s, tr, c), lambda i: (0, i, 0))],
        out_specs=pl.BlockSpec((tr, c), lambda i: (i, 0)), out_shape=jax.ShapeDtypeStruct((r, c), F32),
    )(x)


def adamw(name, w, g, m, v, layer=None):
    r, c = w.shape[-2:]
    tr = _row_tile(r)
    stacked = g.ndim == 3

    def body(w_ref, g_ref, m_ref, v_ref, go_ref, d_ref, mo_ref, vo_ref):
        gv = _slot_sum(g_ref) if stacked else g_ref[...]
        mn = B1 * m_ref[...] + (1.0 - B1) * gv
        vn = B2 * v_ref[...] + (1.0 - B2) * jnp.square(gv)
        m_hat = mn / (1.0 - B1 ** STEP)
        v_hat = vn / (1.0 - B2 ** STEP)
        go_ref[...] = gv
        d_ref[...] = -LR * (m_hat / (jnp.sqrt(v_hat) + AEPS) + WD * w_ref[...])
        mo_ref[...] = mn
        vo_ref[...] = vn

    tile = pl.BlockSpec((tr, c), lambda i: (i, 0))
    slab = tile if layer is None else pl.BlockSpec((None, tr, c), lambda i: (layer, i, 0))
    g_spec = pl.BlockSpec((g.shape[0], tr, c), lambda i: (0, i, 0)) if stacked else tile
    return pl.pallas_call(
        body, name=name, grid=(r // tr,), in_specs=[slab, g_spec, slab, slab], out_specs=[tile] * 4,
        out_shape=[jax.ShapeDtypeStruct((r, c), F32)] * 4,
    )(w, g, m, v)


def _place():
    return lax.axis_index("x"), lax.axis_index("y"), lax.axis_index("c")


def all_gather(name, x, in_vmem):
    r, c = x.shape
    space = pltpu.VMEM if in_vmem else pl.ANY

    def body(x_ref, out_ref, send_sems, recv_sems, local_sem):
        px, py, pc = _place()
        me, sibling = (px, py, pc), (px, py, 1 - pc)
        chips = [(1 - px, py), (px, 1 - py), (1 - px, 1 - py)]

        def rows(qx, qy, qc):
            return out_ref.at[pl.ds((4 * qx + 2 * qy + qc) * r, r), :]

        def copy(k, block, to, src=None):
            return pltpu.make_async_remote_copy(
                src_ref=rows(*block) if src is None else src, dst_ref=rows(*block),
                send_sem=send_sems.at[k], recv_sem=recv_sems.at[k], device_id=to, device_id_type=MESH)

        mine = pltpu.make_async_copy(x_ref, rows(*me), local_sem)
        mine.start()
        first = [copy(0, me, sibling, src=x_ref)]
        first += [copy(1 + j, me, (*chip, pc), src=x_ref) for j, chip in enumerate(chips)]
        for cp in first:
            cp.start()
        passed = [copy(4 + j, (*chip, pc), sibling) for j, chip in enumerate(chips)]
        for j, chip in enumerate(chips):
            copy(1 + j, (*chip, pc), me).wait_recv()
            passed[j].start()
        copy(0, sibling, me).wait_recv()
        for j, chip in enumerate(chips):
            copy(4 + j, (*chip, 1 - pc), me).wait_recv()
        for cp in first + passed:
            cp.wait_send()
        mine.wait()

    return pl.pallas_call(
        body, name=name, out_shape=jax.ShapeDtypeStruct((N_DEV * r, c), x.dtype),
        in_specs=[pl.BlockSpec(memory_space=space)], out_specs=pl.BlockSpec(memory_space=space),
        scratch_shapes=[pltpu.SemaphoreType.DMA((7,)), pltpu.SemaphoreType.DMA((7,)), pltpu.SemaphoreType.DMA],
    )(x)


_HBM =pl.BlockSpec(memory_space=pltpu.HBM)
_SEM = pl.BlockSpec(memory_space=pltpu.SEMAPHORE)
_EFFECT = pltpu.SideEffectType.DATAFLOW_SIDE_EFFECTING


def _peers():
    px, py, pc = _place()
    return [(1 - px if k & 4 else px, 1 - py if k & 2 else py, 1 - pc if k & 1 else pc) for k in range(1, N_DEV)]


def _slot(dev):
    return 4 * dev[0] + 2 * dev[1] + dev[2]


def _split_copies(src_refs, land_refs, send_sems, recv_sems, gather):
    me = _slot(_place())
    return [pltpu.make_async_remote_copy(
        src_ref=src if gather else src.at[_slot(peer)], dst_ref=land.at[me],
        send_sem=send_sems.at[a * (N_DEV - 1) + k], recv_sem=recv_sems.at[a * (N_DEV - 1) + k],
        device_id=peer, device_id_type=MESH)
        for a, (src, land) in enumerate(zip(src_refs, land_refs)) for k, peer in enumerate(_peers())]


def exchange_start(name, srcs, gather, after):
    n = len(srcs)
    lands = [pltpu.HBM((N_DEV,) + s.shape if gather else s.shape, s.dtype) for s in srcs]

    def body(*refs):
        send_sems, recv_sems = refs[2 * n + 1:2 * n + 3]
        for cp in _split_copies(refs[:n], refs[n:2 * n], send_sems, recv_sems, gather):
            cp.start()
        refs[-1][...] = jnp.zeros_like(refs[-1])

    sems = pltpu.SemaphoreType.DMA((n * (N_DEV - 1),))
    res = pl.pallas_call(
        body, name=name,
        out_shape=(sems, sems, *[pltpu.HBM(s.shape, s.dtype) for s in srcs], *lands, jax.ShapeDtypeStruct((8, 128), F32)),
        in_specs=(_HBM,) * (2 * n) + (pl.BlockSpec(memory_space=pl.ANY),),
        out_specs=(_SEM, _SEM) + (_HBM,) * (2 * n) + (pl.BlockSpec(memory_space=pltpu.VMEM),),
        input_output_aliases={i: 2 + i for i in range(2 * n)},
        compiler_params=pltpu.CompilerParams(has_side_effects=_EFFECT),
    )(*[pltpu.with_memory_space_constraint(s, pltpu.HBM) for s in srcs],
      *[pltpu.with_memory_space_constraint(lax.empty(ld.shape, ld.dtype), pltpu.HBM) for ld in lands], after)
    return res[0], res[1], list(res[2:2 + n]), list(res[2 + n:2 + 2 * n]), res[-1]


def exchange_wait(name, started, after, gather):
    send_sems, recv_sems, srcs, lands, _ = started
    n = len(srcs)
    after = list(after) if isinstance(after, (list, tuple)) else [after]

    def body(*refs):
        send_sems, recv_sems = refs[2 * n:2 * n + 2]
        for cp in _split_copies(refs[:n], refs[n:2 * n], send_sems, recv_sems, gather):
            cp.wait_send()
            cp.wait_recv()

    res = pl.pallas_call(
        body, name=name, out_shape=tuple(pltpu.HBM(a.shape, a.dtype) for a in srcs + lands),
        in_specs=(_HBM,) * (2 * n) + (_SEM, _SEM) + (pl.BlockSpec(memory_space=pl.ANY),) * len(after),
        out_specs=(_HBM,) * (2 * n), input_output_aliases={i: i for i in range(2 * n)},
        compiler_params=pltpu.CompilerParams(has_side_effects=_EFFECT),
    )(*srcs, *lands, send_sems, recv_sems, *after)
    return list(res[:n]), list(res[n:])


NCF = FFN_H // FFN_TC


def _size(shape):
    n = 1
    for s in shape:
        n *= s
    return n


def _padded_rows(n_elems, row_mult):
    return -(-n_elems // (D * row_mult)) * row_mult


def _pack_rows(arrs, dtype, row_mult):
    rows, offs, r0 = [], [], 0
    for a in arrs:
        flat = a.reshape(-1).astype(dtype)
        n = _padded_rows(flat.shape[0], row_mult)
        rows.append(jnp.pad(flat, (0, n * D - flat.shape[0])).reshape(n, D))
        offs.append(r0)
        r0 += n
    return jnp.concatenate(rows, 0), offs


def _unpack_rows(buf, offs, shapes):
    lead, out = buf.shape[:-2], []
    for o, shp in zip(offs, shapes):
        n = _size(shp)
        nr = -(-n // D)
        out.append(buf[..., o:o + nr, :].reshape(lead + (nr * D,))[..., :n].reshape(lead + tuple(shp)))
    return out


def _rows3(w):
    return [w[i:i + 1] for i in range(3)]


def f_mod1(xs, ps):
    return f_mod(xs, ps)[:1]


def kernel(x, c, ctx, c_ctx, ada_w, ada_b, norm_mix, norm_ffn, gla_w_in, gla_w_a2, gla_b_a, gla_head_norm, gla_w_out, sc_w_in, sc_conv_w, sc_w_out, ffn_w_up, ffn_conv_w, ffn_conv_b, ffn_w_down, final_norm, loss_target, m_c_ctx, m_ada_w, m_ada_b, m_norm_mix, m_norm_ffn, m_gla_w_in, m_gla_w_a2, m_gla_b_a, m_gla_head_norm, m_gla_w_out, m_sc_w_in, m_sc_conv_w, m_sc_w_out, m_ffn_w_up, m_ffn_conv_w, m_ffn_conv_b, m_ffn_w_down, m_final_norm, v_c_ctx, v_ada_w, v_ada_b, v_norm_mix, v_norm_ffn, v_gla_w_in, v_gla_w_a2, v_gla_b_a, v_gla_head_norm, v_gla_w_out, v_sc_w_in, v_sc_conv_w, v_sc_w_out, v_ffn_w_up, v_ffn_conv_w, v_ffn_conv_b, v_ffn_w_down, v_final_norm):
    names = ["c_ctx", "ada_w", "ada_b", "norm_mix", "norm_ffn", "gla_w_in", "gla_w_a2", "gla_b_a", "gla_head_norm",
             "gla_w_out", "sc_w_in", "sc_conv_w", "sc_w_out", "ffn_w_up", "ffn_conv_w", "ffn_conv_b", "ffn_w_down",
             "final_norm"]
    w_ = dict(zip(names, [c_ctx, ada_w, ada_b, norm_mix, norm_ffn, gla_w_in, gla_w_a2, gla_b_a, gla_head_norm, gla_w_out,
                          sc_w_in, sc_conv_w, sc_w_out, ffn_w_up, ffn_conv_w, ffn_conv_b, ffn_w_down, final_norm]))
    m_ = dict(zip(names, [m_c_ctx, m_ada_w, m_ada_b, m_norm_mix, m_norm_ffn, m_gla_w_in, m_gla_w_a2, m_gla_b_a,
                          m_gla_head_norm, m_gla_w_out, m_sc_w_in, m_sc_conv_w, m_sc_w_out, m_ffn_w_up, m_ffn_conv_w,
                          m_ffn_conv_b, m_ffn_w_down, m_final_norm]))
    v_ = dict(zip(names, [v_c_ctx, v_ada_w, v_ada_b, v_norm_mix, v_norm_ffn, v_gla_w_in, v_gla_w_a2, v_gla_b_a,
                          v_gla_head_norm, v_gla_w_out, v_sc_w_in, v_sc_conv_w, v_sc_w_out, v_ffn_w_up, v_ffn_conv_w,
                          v_ffn_conv_b, v_ffn_w_down, v_final_norm]))
    me = 4 * lax.axis_index("x") + 2 * lax.axis_index("y") + lax.axis_index("c")
    bsz = x.shape[0]
    tm = 256
    nt = SEQ // tm
    ctx_tiles = CTX // tm
    pe = functools.partial(P, per_example=True)

    groups = {"gla": [("gla_w_in", 0), ("gla_w_out", 0)], "ffn0": [("ffn_w_up", 0), ("ffn_w_down", 0)],
              "l1": [("sc_w_in", 0), ("sc_w_out", 0), ("ffn_w_up", 1), ("ffn_w_down", 1)]}
    ag_groups = {"gin": [("gla_w_in", 0)], "ffn0": [("gla_w_out", 0), ("ffn_w_up", 0), ("ffn_w_down", 0)], "l1": groups["l1"]}
    ag_started = {}

    def start_gather(g, after):
        ag_started[g] = exchange_start(f"ag_{g}_start", [w_[n][i].astype(BF16) for n, i in ag_groups[g]], True, after)
        return ag_started[g][4]

    small_sharded = [c, gla_w_a2, gla_b_a, sc_conv_w, ffn_conv_w]
    pack0, offs0 = _pack_rows(small_sharded, F32, 8)
    g0 = all_gather("ag_small", pack0, True).reshape(N_DEV, pack0.shape[0], D)
    c_all, wa2_s, ba_s, scw_s, fcw_s = _unpack_rows(g0, offs0, [a.shape for a in small_sharded])
    w_a2 = wa2_s[:, 0].transpose(1, 2, 0, 3).reshape(2, RANK, KD)
    b_a = ba_s[:, 0].transpose(1, 0, 2).reshape(2, KD)
    sc_cw = scw_s[:, 0].transpose(1, 0, 2).reshape(3, D)
    ffn_cw = fcw_s.transpose(1, 2, 0, 3).reshape(2, 3, 2 * FFN_H)

    cond = jnp.concatenate([c_all.reshape(N_DEV * bsz, D), c_ctx[None], jnp.zeros((ADA_ROWS - N_DEV * bsz - 1, D), F32)], 0)
    b_mine = lax.dynamic_slice(ada_b, (0, me * ADA_COLS), (2, ADA_COLS)).reshape(2, 1, ADA_COLS)
    mod_part = ada_fwd(cond, ada_w, b_mine)
    mod = all_gather("ag_mod", mod_part.reshape(2 * ADA_ROWS, ADA_COLS), True)
    mod = mod.reshape(N_DEV, 2, ADA_ROWS, ADA_COLS).transpose(1, 2, 0, 3).reshape(2, ADA_ROWS, 6 * D)
    mods = lax.dynamic_slice(mod, (0, bsz * me, 0), (2, bsz, 6 * D))
    md = [[mods[i][:, k * D:(k + 1) * D].reshape(bsz, 1, D) for k in range(6)] for i in range(2)]
    mc = [mod[0, ADA_CTX_ROW, k * D:(k + 1) * D][None] for k in range(2)]

    tok = mod
    for g in ag_groups:
        tok = start_gather(g, tok)
    norm_mix = norm_mix + tok[0, 0]

    def gathered(g, after):
        mine, lands = exchange_wait(f"ag_{g}_wait", ag_started[g], after, True)
        return [lax.dynamic_update_index_in_dim(ld, mn, me, 0) for ld, mn in zip(lands, mine)]

    s_up, w_down = [None, None], [None, None]
    wd = jnp.zeros((128, 2 * KD), F32).at[:RANK, :KD].set(w_a2[0]).at[RANK:2 * RANK, KD:].set(w_a2[1])
    bd = b_a.reshape(1, 2 * KD)
    scw = _rows3(sc_cw)
    head_gain = gla_head_norm.reshape(1, HV)
    gains_mix = [norm_mix[i][None] for i in range(2)]
    gains_ffn = [norm_ffn[i][None] for i in range(2)]

    def tokens(a2d, t_len):
        return a2d.reshape(bsz, t_len, -1)

    def ffn_params(i):
        rows = [ffn_cw[i][t] for t in range(3)] + [ffn_conv_b[i]]
        return [P(a.reshape(2, FFN_H), w=FFN_TC, rows=True) for a in rows]

    def ffn_fwd(i, hn2):
        u = mm(f"ffn_up{i}", V(hn2, "tok"), V(s_up[i], "cols"), out="planes", out_dtype=BF16, planes_t=SEQ)
        act = rowwise(f"ffn_mid{i}", f_ffn_mid, [X(u, w=FFN_TC, planes=True)], ffn_params(i), tm=SEQ, nt=1, nc=NCF,
                      outs=[(FFN_TC, BF16, 1)])[0]
        return u, act

    def arrays(ps):
        return [p["a"] for p in ps]

    ps_in0 = [P(gains_mix[0]), pe(md[0][0]), pe(md[0][1])]
    ps_ctx = [P(gains_mix[0]), P(mc[0]), P(mc[1])]
    hn0 = rowwise("mod_in0", f_mod, [X(x)], ps_in0, tm=tm, nt=nt, outs=[(D, BF16, 1)])[0]
    hnc = rowwise("mod_ctx", f_mod, [X(ctx)], ps_ctx, tm=tm, nt=ctx_tiles, outs=[(D, BF16, 1)])[0]
    hcat = jnp.concatenate([hnc, hn0], axis=1)
    (s_gin,) = gathered("gin", hcat)
    w_gin = V(s_gin, "cols", width=GLA_IN_PAD)
    pcat = tokens(mm("gla_in", V(hcat, "tok"), w_gin), TT)
    pa_x = X(pcat, w=128, co=(GLA_IN_PAD - 128) // 128)
    la = rowwise("gla_decay", f_decay, [pa_x], [P(wd), P(bd)], tm=tm, nt=TT // tm, outs=[(2 * KD, F32, 1)])[0]
    o2, s_all = gla_fwd(pcat, la)
    post_xs = [X(o2, w=VD, co=0, ro=ctx_tiles, split=HEADS), X(o2, w=VD, co=1, ro=ctx_tiles, split=HEADS),
               X(pcat, w=VD, co=2, ro=ctx_tiles, split=HEADS)]
    yin0 = rowwise("gla_post", f_gla_post, post_xs, [P(head_gain)], tm=tm, nt=nt, outs=[(VD, BF16, HEADS)])[0]
    s_gout, s_up[0], s_down0 = gathered("ffn0", yin0)
    w_gout, w_down[0] = s_gout.reshape(VD, D), s_down0.reshape(FFN_H, D)
    ps_mid0 = [pe(md[0][2]), P(gains_ffn[0]), pe(md[0][3]), pe(md[0][4])]
    y0, h1_0, hn2_0 = mm_res_mod("gla_out", yin0, w_gout, x, *arrays(ps_mid0))
    u0, act0 = ffn_fwd(0, hn2_0)
    ps_in1 = [pe(md[0][5]), P(gains_mix[1]), pe(md[1][0]), pe(md[1][1])]
    fo0, h2_0, hn1 = mm_res_mod("ffn_down0", act0, w_down[0], h1_0, *arrays(ps_in1))

    s_sin, s_sout, s_up[1], s_down1 = gathered("l1", hn1)
    w_sout, w_down[1] = s_sout.reshape(D, D), s_down1.reshape(FFN_H, D)
    p1 = tokens(mm("sc_in", V(hn1, "tok"), V(s_sin, "cols")), SEQ)
    sc_ps = [P(a) for a in scw]
    yin1 = rowwise("sc_mid", f_sc_mid, [X(p1, split=3)], sc_ps, tm=tm, nt=nt, outs=[(D, BF16, 1)])[0]
    ps_mid1 = [pe(md[1][2]), P(gains_ffn[1]), pe(md[1][3]), pe(md[1][4])]
    y1, h1_1, hn2_1 = mm_res_mod("sc_out", yin1, w_sout, h2_0, *arrays(ps_mid1))
    u1, act1 = ffn_fwd(1, hn2_1)
    fo1 = tokens(mm("ffn_down1", V(act1, "tok"), V(w_down[1])), SEQ)
    loss8, dh1_1, dfo1, dm5_1, g_final = final_loss(h1_1, fo1, md[1][5], final_norm[None], loss_target)

    def ffn_bwd(i, u, act, hn2, dfo):
        dact = tokens(mm(f"ffn_down_dx{i}", V(dfo, "tok"), V(w_down[i]), form="nt", out_dtype=BF16), SEQ)
        g_down = mm(f"ffn_down_dw{i}", V(act, "tok"), V(dfo, "tok"), form="tn", out_dtype=BF16)
        r = rowwise(f"ffn_mid_bwd{i}", f_ffn_mid, [X(u, w=FFN_TC, planes=True)], ffn_params(i), tm=SEQ, nt=1, nc=NCF,
                    douts=[X(dact, w=FFN_TC)], dx={0: BF16}, dp=[0, 1, 2, 3])
        du, g_cw, g_cb = r[0], jnp.stack([a.reshape(2 * FFN_H) for a in r[1:4]]), r[4].reshape(1, 2 * FFN_H)
        dhn2 = tokens(mm(f"ffn_up_dx{i}", V(du, "planes"), V(s_up[i], "cols"), form="nt", out_dtype=BF16), SEQ)
        g_up = mm(f"ffn_up_dw{i}", V(hn2, "tok"), V(du, "planes"), form="tn", out="cols", out_dtype=BF16)
        return dhn2, g_up, row_slots(g_down), g_cw, g_cb

    def res_mod_bwd(name, h, y, ps, dh1, dhn):
        return rowwise(name, f_res_mod, [X(h), X(y)], ps, tm=tm, nt=nt, douts=[X(dh1), X(dhn)],
                       dx={0: F32, 1: BF16}, dp=[0, 1, 2, 3])

    def row_slots(g):
        return g.reshape(N_DEV, -1, g.shape[-1])

    a2a_started = {}

    def send_grads(g, slots, after=None):
        a2a_started[g] = exchange_start(f"a2a_{g}_start", list(slots), False, loss8 if after is None else after)
        return a2a_started[g][4][0, 0]

    def after_start(ps, tok):
        return [dict(ps[0], a=ps[0]["a"] + tok)] + ps[1:]

    dhn2_1, g_up1, g_down1, g_fcw1, g_fcb1 = ffn_bwd(1, u1, act1, hn2_1, dfo1)
    dh2_0, dy1, dm2_1, g_nffn1, dm3_1, dm4_1 = res_mod_bwd("res_mod_mid1_bwd", h2_0, y1, ps_mid1, dh1_1, dhn2_1)
    dyin1 = tokens(mm("sc_out_dx", V(dy1, "tok"), V(w_sout), form="nt", out_dtype=BF16), SEQ)
    g_sout = row_slots(mm("sc_out_dw", V(yin1, "tok"), V(dy1, "tok"), form="tn", out_dtype=BF16))
    r = rowwise("sc_mid_bwd", f_sc_mid, [X(p1, split=3)], sc_ps, tm=tm, nt=nt, douts=[X(dyin1)], dx={0: BF16}, dp=[0, 1, 2])
    dp1, g_scw = r[0], jnp.concatenate(r[1:4], 0)
    dhn1 = tokens(mm("sc_in_dx", V(dp1, "tok"), V(s_sin, "cols"), form="nt", out_dtype=BF16), SEQ)
    g_sin = mm("sc_in_dw", V(hn1, "tok"), V(dp1, "tok"), form="tn", out="cols", out_dtype=BF16)
    tok = send_grads("l1", [g_sin, g_sout, g_up1, g_down1])
    dh1_0, dfo0, dm5_0, g_nmix1, dm0_1, dm1_1 = res_mod_bwd("res_mod_in1_bwd", h1_0, fo0, after_start(ps_in1, tok), dh2_0, dhn1)

    dhn2_0, g_up0, g_down0, g_fcw0, g_fcb0 = ffn_bwd(0, u0, act0, hn2_0, dfo0)
    tok = send_grads("ffn0", [g_up0, g_down0])
    dx_res, dy0, dm2_0, g_nffn0, dm3_0, dm4_0 = res_mod_bwd("res_mod_mid0_bwd", x, y0, after_start(ps_mid0, tok), dh1_0, dhn2_0)
    dyin0 = tokens(mm("gla_out_dx", V(dy0, "tok"), V(w_gout), form="nt", out_dtype=BF16), SEQ)
    g_gout = row_slots(mm("gla_out_dw", V(yin0, "tok"), V(dy0, "tok"), form="tn", out_dtype=BF16))
    do, dgate, g_head = rowwise("gla_post_bwd", f_gla_post, post_xs, [P(head_gain)], tm=tm, nt=nt,
                                douts=[X(dyin0, split=HEADS)], dx={0: F32, 2: BF16}, dp=[0])
    dq2, dk2, dv2, dla = gla_bwd(pcat, la, s_all, do)
    dpa, g_wd, g_bd = rowwise("gla_decay_bwd", f_decay, [pa_x], [P(wd), P(bd)], tm=tm, nt=TT // tm, douts=[X(dla)],
                              dx={0: BF16}, dp=[0, 1])
    dpcat = gla_combine(dq2, dk2, dv2, dgate, dpa)
    dhcat = tokens(mm("gla_in_dx", V(dpcat, "tok"), w_gin, form="nt", out_dtype=BF16), TT)
    g_gin = mm("gla_in_dw", V(hcat, "tok"), V(dpcat, "tok"), form="tn", out="cols", out_dtype=BF16, shard_n=GLA_IN // N_DEV)
    grad_x, g_nmix0, dm0_0, dm1_0 = rowwise("mod_in0_bwd", f_mod, [X(x)], ps_in0, tm=tm, nt=nt,
                                            douts=[X(dhcat, ro=ctx_tiles), X(dx_res)], dx={0: F32}, dp=[0, 1, 2])
    g_nmix0c, dmc0, dmc1 = rowwise("mod_ctx_bwd", f_mod1, [X(ctx)], ps_ctx, tm=tm, nt=ctx_tiles, douts=[X(dhcat)],
                                   dx={}, dp=[0, 1, 2])

    zero_row = jnp.zeros((1, 4 * D), F32)
    dmod = [jnp.concatenate([jnp.concatenate([a.reshape(bsz, D) for a in dms], 1), ctx_row], 0)
            for dms, ctx_row in (([dm0_0, dm1_0, dm2_0, dm3_0, dm4_0, dm5_0], jnp.concatenate([dmc0, dmc1, zero_row], 1)),
                                 ([dm0_1, dm1_1, dm2_1, dm3_1, dm4_1, dm5_1], jnp.zeros((1, 6 * D), F32)))]
    g_wa2 = jnp.stack([g_wd[:RANK, :KD], g_wd[RANK:2 * RANK, KD:]])
    small_grads = [jnp.stack(dmod), jnp.concatenate([g_nmix0 + g_nmix0c, g_nmix1], 0), jnp.concatenate([g_nffn0, g_nffn1], 0),
                   g_head, jnp.concatenate([g_fcb0, g_fcb1], 0), g_final, g_wa2, g_bd.reshape(2, KD), g_scw,
                   jnp.stack([g_fcw0, g_fcw1]), loss8[:1]]
    pack1, offs1 = _pack_rows(small_grads, F32, 8)
    g1 = all_gather("ag_grads", pack1, True).reshape(N_DEV, pack1.shape[0], D)
    dmod_all = _unpack_rows(g1, offs1[:1], [small_grads[0].shape])[0]
    tot = _unpack_rows(sum_slots("sum_small", g1), offs1, [a.shape for a in small_grads])
    loss = tot[10][0, 0]
    dm_rows = dmod_all[:, :, :bsz].transpose(1, 0, 2, 3).reshape(2, N_DEV * bsz, 6 * D)
    dm_full = jnp.concatenate([dm_rows, tot[0][:, bsz:], jnp.zeros((2, ADA_ROWS - N_DEV * bsz - 1, 6 * D), F32)], 1)
    dm_mine = lax.dynamic_slice(dm_full, (0, 0, me * ADA_COLS), (2, ADA_ROWS, ADA_COLS))
    g_ada_w, g_ada_b, cpart = ada_bwd(cond, dm_mine, dm_full, ada_w)
    cparts = all_gather("ag_cctx", cpart, True).reshape(N_DEV, ADA_ROWS - ADA_CTX_ROW, D)[:, 0]
    g_cctx = cctx_grad(cparts, c_ctx[None])[0]
    tok = send_grads("gla", [g_gin, g_gout], after=g_cctx)

    def my_cols(full, n):
        return lax.dynamic_slice_in_dim(full, me * n, n, axis=full.ndim - 1)

    grads = {
        "c_ctx": g_cctx, "ada_b": g_ada_b.reshape(2, 6 * D), "norm_mix": tot[1], "norm_ffn": tot[2],
        "gla_head_norm": tot[3], "ffn_conv_b": tot[4], "final_norm": tot[5].reshape(D),
        "gla_w_a2": my_cols(tot[6], KD // N_DEV)[None], "gla_b_a": my_cols(tot[7], KD // N_DEV)[None],
        "sc_conv_w": my_cols(tot[8], D // N_DEV)[None], "ffn_conv_w": my_cols(tot[9], 2 * FFN_H // N_DEV),
    }

    res_ada = adamw("adamw_ada", *[a.reshape(2 * D, ADA_COLS) for a in (ada_w, g_ada_w, m_ada_w, v_ada_w)])
    grads["c_ctx"] = g_cctx + tok
    big = ["gla_w_in", "gla_w_out", "sc_w_in", "sc_w_out", "ffn_w_up", "ffn_w_down"]
    small = [n for n in names if n not in big and n != "ada_w"]
    g_small = _pack_rows([grads[n] for n in small], F32, 8)[0]
    res_small = adamw("adamw_small", _pack_rows([w_[n] for n in small], F32, 8)[0], g_small,
                      _pack_rows([m_[n] for n in small], F32, 8)[0], _pack_rows([v_[n] for n in small], F32, 8)[0])
    offs_s = _pack_rows([w_[n] for n in small], F32, 8)[1]

    big_res, done = {}, [res_small[0], res_ada[0]]
    for g in ("l1", "ffn0", "gla"):
        sent, lands = exchange_wait(f"a2a_{g}_wait", a2a_started[g], done, False)
        for (n, i), mine, land in zip(groups[g], sent, lands):
            land = lax.dynamic_update_index_in_dim(land, lax.dynamic_index_in_dim(mine, me, 0, keepdims=False), me, 0)
            big_res[(n, i)] = adamw(f"adamw_{n}{i}", w_[n], land, m_[n], v_[n], layer=i)
            done.append(big_res[(n, i)][0])

    out = {}
    for kind, idx in (("grad", 0), ("delta", 1), ("new_m", 2), ("new_v", 3)):
        vals = {n: jnp.stack([big_res[(n, i)][idx] for i in range(w_[n].shape[0])]) for n in big}
        vals["ada_w"] = res_ada[idx].reshape(ada_w.shape)
        vals.update(zip(small, _unpack_rows(res_small[idx], offs_s, [w_[n].shape for n in small])))
        out[kind] = [vals[n] for n in names]
    return (loss, grad_x, *out["grad"], *out["delta"], *out["new_m"], *out["new_v"])
```

```python
import functools

import jax
import jax.numpy as jnp
from jax import lax
from jax.experimental import pallas as pl
from jax.experimental.pallas import tpu as pltpu

F32 = jnp.float32
BF16 = jnp.bfloat16

N_DEV = 8
D = 1024
SEQ = 2048
CTX = 256
TT = CTX + SEQ
GRID_W = 64
CHUNK = 64
HEADS = 4
HK = 128
HV = 256
KD = 512
VD = 1024
RANK = 16
TAU = 16.0
GLA_IN = 3104
GLA_IN_PAD = 3200
FFN_H = 2560
FFN_TC = 256
EPS = 1e-6
LR, B1, B2, AEPS, WD, STEP = 0.001, 0.9, 0.999, 1e-08, 0.01, 10
MESH = pl.DeviceIdType.MESH


def _blocks(n):
    return [n] + [t for t in range(n - n % 128, 0, -128) if n % t == 0 and t != n]


def V(arr, kind="flat", width=None):
    if kind == "tok":
        return V(arr.reshape(-1, arr.shape[-1]))
    if kind == "flat":
        r, c = arr.shape
        return dict(a=arr, kind=kind, shape=(r, c), rows=_blocks(r), cols=_blocks(c))
    if kind == "planes":
        bsz, _, t, ch = arr.shape
        return dict(a=arr, kind=kind, shape=(bsz * t, 2 * ch), rows=_blocks(t), cols=[2 * ch] + _blocks(ch), t=t, ch=ch)
    _, r, n = arr.shape
    if width is not None:
        return dict(a=arr, kind=kind, shape=(r, width), rows=_blocks(r), cols=[width], n=n, pad=width - N_DEV * n)
    return dict(a=arr, kind=kind, shape=(r, N_DEV * n), rows=_blocks(r), cols=[8 * n, 4 * n, 2 * n], n=n, pad=0)


def _view_spec(v, br, bc, idx):
    if v["kind"] == "flat":
        return pl.BlockSpec((br, bc), idx)
    if v["kind"] == "planes":
        nt = v["t"] // br
        if bc == 2 * v["ch"]:
            return pl.BlockSpec((None, 2, br, v["ch"]), lambda i, j, k: (idx(i, j, k)[0] // nt, 0, idx(i, j, k)[0] % nt, 0))
        nch = v["ch"] // bc

        def at(i, j, k):
            r, c = idx(i, j, k)
            return r // nt, c // nch, r % nt, c % nch
        return pl.BlockSpec((None, None, br, bc), at)
    return pl.BlockSpec(((bc - v["pad"]) // v["n"], br, v["n"]), lambda i, j, k: (idx(i, j, k)[1], idx(i, j, k)[0], 0))


def _out_view(kind, rows, cols, dtype, planes_t=None, shard_n=None):
    if kind == "flat":
        shape = (rows, cols)
    elif kind == "planes":
        shape = (rows // planes_t, 2, planes_t, cols // 2)
    elif shard_n is not None:
        return V(jax.ShapeDtypeStruct((N_DEV, rows, shard_n), dtype), kind, width=cols)
    else:
        shape = (N_DEV, rows, cols // N_DEV)
    return V(jax.ShapeDtypeStruct(shape, dtype), kind)


MM_VMEM_BUDGET = 40 * 2 ** 20
MM_VMEM_LIMIT = 56 * 2 ** 20
MM_MAX_TILE = 1536


def _mm_tiles(m, n, kk, ms, ns, ks, a_bytes, b_bytes, o_bytes):
    best = None
    for tk in ks:
        for tm in [t for t in ms if t <= MM_MAX_TILE] or ms:
            for tn in [t for t in ns if t <= MM_MAX_TILE] or ns:
                one_k = tk == kk
                need = 2 * (tm * tk * a_bytes + tk * tn * b_bytes + tm * tn * o_bytes) + (0 if one_k else tm * tn * 4)
                if need > MM_VMEM_BUDGET:
                    continue
                steps = (m // tm) * (n // tn) * (kk // tk)
                traffic = (m * kk * a_bytes * (1 if one_k else n // tn)
                           + kk * n * b_bytes * (1 if one_k and n == tn else m // tm) + m * n * o_bytes)
                fill = (tm * tk * a_bytes + tk * tn * b_bytes) / 2.5e12
                cost = max(2.0 * m * n * kk / (9e14 if one_k else 6.5e14), traffic / 2.5e12) + steps * 0.4e-6 + fill
                if best is None or cost < best[0]:
                    best = (cost, tm, tn, tk)
    return best[1:]


def mm(name, a, b, form="nn", out="flat", out_dtype=F32, planes_t=None, shard_n=None):
    (m, kk) = a["shape"][::-1] if form == "tn" else a["shape"]
    n = b["shape"][0] if form == "nt" else b["shape"][1]
    assert (b["shape"][1] if form == "nt" else b["shape"][0]) == kk, (name, a["shape"], b["shape"])
    o = _out_view(out, m, n, out_dtype, planes_t, shard_n)
    a_m, a_k = (a["cols"], a["rows"]) if form == "tn" else (a["rows"], a["cols"])
    b_k, b_n = (b["cols"], b["rows"]) if form == "nt" else (b["rows"], b["cols"])
    tm, tn, tk = _mm_tiles(m, n, kk, [t for t in a_m if t in o["rows"]], [t for t in b_n if t in o["cols"]],
                           [t for t in a_k if t in b_k], a["a"].dtype.itemsize, b["a"].dtype.itemsize,
                           jnp.dtype(out_dtype).itemsize)
    nk = kk // tk
    dn = (((0 if form == "tn" else 1,), (1 if form == "nt" else 0,)), ((), ()))

    def load(ref, v):
        if len(ref.shape) == 3:
            pieces = [ref[p].astype(BF16) for p in range(ref.shape[0])]
            if v.get("pad"):
                pieces.append(jnp.zeros(ref.shape[1:2] + (v["pad"],), BF16))
            return jnp.concatenate(pieces, axis=-1)
        return ref[...].astype(BF16)

    def store(o_ref, val):
        val = val.astype(out_dtype)
        if len(o_ref.shape) == 3:
            w = o_ref.shape[-1]
            for p in range(o_ref.shape[0]):
                o_ref[p] = val[:, p * w:(p + 1) * w]
        else:
            o_ref[...] = val

    def body(a_ref, b_ref, o_ref, *acc):
        if nk == 1:
            store(o_ref, lax.dot_general(load(a_ref, a), load(b_ref, b), dn, preferred_element_type=F32))
            return
        k, acc_ref = pl.program_id(2), acc[0]

        @pl.when(k == 0)
        def _():
            acc_ref[...] = jnp.zeros_like(acc_ref)

        acc_ref[...] += lax.dot_general(load(a_ref, a), load(b_ref, b), dn, preferred_element_type=F32)

        @pl.when(k == nk - 1)
        def _():
            store(o_ref, acc_ref[...])

    if form == "tn":
        a_spec = _view_spec(a, tk, tm, lambda i, j, k: (k, i))
    else:
        a_spec = _view_spec(a, tm, tk, lambda i, j, k: (i, k))
    if form == "nt":
        b_spec = _view_spec(b, tn, tk, lambda i, j, k: (j, k))
    else:
        b_spec = _view_spec(b, tk, tn, lambda i, j, k: (k, j))
    return pl.pallas_call(
        body, name=name, grid=(m // tm, n // tn, nk),
        in_specs=[a_spec, b_spec], out_specs=_view_spec(o, tm, tn, lambda i, j, k: (i, j)), out_shape=o["a"],
        scratch_shapes=[pltpu.VMEM((tm, tn), F32)] if nk > 1 else [],
        compiler_params=pltpu.CompilerParams(dimension_semantics=("parallel", "parallel", "arbitrary"),
                                             vmem_limit_bytes=MM_VMEM_LIMIT),
    )(a["a"], b["a"])


def mm_res_mod(name, a, w, h, gate, gain, shift, scale):
    bsz, t_len, kk = a.shape
    tm = 512
    per = t_len // tm

    def body(a_ref, w_ref, h_ref, gate_ref, gain_ref, shift_ref, scale_ref, y_ref, h1_ref, hn_ref):
        y = jnp.dot(a_ref[...].astype(BF16), w_ref[...].astype(BF16), preferred_element_type=F32)
        h1 = h_ref[...] + gate_ref[...] * y
        y_ref[...] = y.astype(BF16)
        h1_ref[...] = h1
        hn_ref[...] = _mod(h1, gain_ref[...], shift_ref[...], scale_ref[...]).astype(BF16)

    def tile(width):
        return pl.BlockSpec((None, tm, width), lambda i: (i // per, i % per, 0))

    per_ex = pl.BlockSpec((None, 1, D), lambda i: (i // per, 0, 0))
    return pl.pallas_call(
        body, name=name, grid=(bsz * per,),
        in_specs=[tile(kk), pl.BlockSpec((kk, D), lambda i: (0, 0)), tile(D), per_ex, pl.BlockSpec((1, D), lambda i: (0, 0)),
                  per_ex, per_ex],
        out_specs=[tile(D)] * 3,
        out_shape=[jax.ShapeDtypeStruct((bsz, t_len, D), BF16), jax.ShapeDtypeStruct((bsz, t_len, D), F32),
                   jax.ShapeDtypeStruct((bsz, t_len, D), BF16)],
        compiler_params=pltpu.CompilerParams(dimension_semantics=("parallel",), vmem_limit_bytes=MM_VMEM_LIMIT),
    )(a, w, h, gate, gain, shift, scale)


def X(arr, w=None, co=0, ro=0, split=1, planes=False):
    return dict(a=arr, w=arr.shape[-1] if w is None else w, co=co, ro=ro, split=2 if planes else split,
                mode="planes" if planes else "cols")


def P(arr, per_example=False, w=None, split=1, rows=False):
    return dict(a=arr, e=per_example, w=arr.shape[-1] if w is None else w, split=arr.shape[-2] if rows else split,
                mode="rows" if rows else "cols")


def _pieces(ref, s):
    if s["mode"] == "planes":
        return [ref[0], ref[1]]
    if s["mode"] == "rows":
        return [ref[i:i + 1, :] for i in range(s["split"])]
    w = ref.shape[-1] // s["split"]
    return [ref[:, i * w:(i + 1) * w] for i in range(s["split"])]


def _store(ref, pieces, s, accumulate=False):
    w = ref.shape[-1] // len(pieces)
    for i, p in enumerate(pieces):
        at = (i,) if s["mode"] == "planes" else (slice(i, i + 1),) if s["mode"] == "rows" else (slice(None), slice(i * w, (i + 1) * w))
        if accumulate:
            ref[at] += p.astype(ref.dtype)
        else:
            ref[at] = p.astype(ref.dtype)


def rowwise(name, f, xs, ps, *, tm, nt, nc=1, outs=None, douts=None, dx=None, dp=None):
    bsz = xs[0]["a"].shape[0]
    fwd = douts is None
    nx, np_ = len(xs), len(ps)
    douts = [] if fwd else douts
    dx = {} if fwd else dx
    dp = [] if fwd else dp

    def x_spec(s):
        if s["mode"] == "planes":
            return pl.BlockSpec((None, 2, tm, s["w"]), lambda c, b, t, s=s: (b, 0, t + s["ro"], c + s["co"]))
        return pl.BlockSpec((None, tm, s["w"]), lambda c, b, t, s=s: (b, t + s["ro"], c + s["co"]))

    def x_out(s, dt):
        if s["mode"] == "planes":
            return (jax.ShapeDtypeStruct((bsz, 2, nt * tm, nc * s["w"]), dt),
                    pl.BlockSpec((None, 2, tm, s["w"]), lambda c, b, t: (b, 0, t, c)))
        return (jax.ShapeDtypeStruct((bsz, nt * tm, nc * s["w"]), dt), pl.BlockSpec((None, tm, s["w"]), lambda c, b, t: (b, t, c)))

    def p_spec(s):
        r = s["a"].shape[-2]
        if s["e"]:
            return pl.BlockSpec((None, r, s["w"]), lambda c, b, t: (b, 0, c))
        return pl.BlockSpec((r, s["w"]), lambda c, b, t: (0, c))

    in_specs = [x_spec(s) for s in xs] + [p_spec(s) for s in ps] + [x_spec(s) for s in douts]
    operands = [s["a"] for s in xs] + [s["a"] for s in ps] + [s["a"] for s in douts]
    if fwd:
        out_modes = [dict(mode="cols", split=sp) for (_, _, sp) in outs]
        out_shape = [jax.ShapeDtypeStruct((bsz, nt * tm, nc * w), dt) for (w, dt, _) in outs]
        out_specs = [pl.BlockSpec((None, tm, w), lambda c, b, t: (b, t, c)) for (w, _, _) in outs]
    else:
        dx_outs = [x_out(xs[i], dt) for i, dt in dx.items()]
        out_shape, out_specs = [o[0] for o in dx_outs], [o[1] for o in dx_outs]
        for j in dp:
            s = ps[j]
            r = s["a"].shape[-2]
            if s["e"]:
                out_shape.append(jax.ShapeDtypeStruct((bsz, r, nc * s["w"]), F32))
                out_specs.append(pl.BlockSpec((None, r, s["w"]), lambda c, b, t: (b, 0, c)))
            else:
                out_shape.append(jax.ShapeDtypeStruct((r, nc * s["w"]), F32))
                out_specs.append(pl.BlockSpec((r, s["w"]), lambda c, b, t: (0, c)))

    def body(*refs):
        x_refs, p_refs = refs[:nx], refs[nx:nx + np_]
        d_refs = refs[nx + np_:nx + np_ + len(douts)]
        o_refs = refs[nx + np_ + len(douts):]
        xv = [[p.astype(F32) for p in _pieces(r, s)] for r, s in zip(x_refs, xs)]
        pv = [[p.astype(F32) for p in _pieces(r, s)] for r, s in zip(p_refs, ps)]
        if fwd:
            for r, pieces, s in zip(o_refs, f(xv, pv), out_modes):
                _store(r, pieces, s)
            return
        _, vjp = jax.vjp(f, xv, pv)
        cot = [[p.astype(F32) for p in _pieces(r, s)] for r, s in zip(d_refs, douts)]
        dxv, dpv = vjp(cot)
        for r, i in zip(o_refs, dx):
            _store(r, dxv[i], xs[i])
        b, t = pl.program_id(1), pl.program_id(2)
        for r, j in zip(o_refs[len(dx):], dp):
            first = (t == 0) if ps[j]["e"] else jnp.logical_and(b == 0, t == 0)

            @pl.when(first)
            def _(r=r, j=j):
                _store(r, dpv[j], ps[j])

            @pl.when(jnp.logical_not(first))
            def _(r=r, j=j):
                _store(r, dpv[j], ps[j], accumulate=True)

    res = pl.pallas_call(
        body, name=name, grid=(nc, bsz, nt), in_specs=in_specs, out_specs=out_specs, out_shape=out_shape,
        compiler_params=pltpu.CompilerParams(dimension_semantics=("arbitrary", "arbitrary", "arbitrary")),
    )(*operands)
    return res


def _keep_rows(a, shift, keep):
    n = a.shape[0]
    t = lax.broadcasted_iota(jnp.int32, a.shape, 0)
    return jnp.where(keep(t, n), pltpu.roll(a, shift % n, 0), 0.0)


def _shift_pair(step, keep_prev, keep_next):
    @jax.custom_vjp
    def prev(a):
        return _keep_rows(a, step, keep_prev)

    @jax.custom_vjp
    def nxt(a):
        return _keep_rows(a, -step, keep_next)

    prev.defvjp(lambda a: (prev(a), None), lambda _, g: (nxt(g),))
    nxt.defvjp(lambda a: (nxt(a), None), lambda _, g: (prev(g),))
    return prev, nxt


prev_tok, next_tok = _shift_pair(1, lambda t, n: t % GRID_W != 0, lambda t, n: t % GRID_W != GRID_W - 1)
prev_row, next_row = _shift_pair(GRID_W, lambda t, n: t >= GRID_W, lambda t, n: t < n - GRID_W)


@jax.custom_vjp
def bdot(a, w):
    return jnp.dot(a.astype(BF16), w.astype(BF16), preferred_element_type=F32)


def _bdot_bwd(res, g):
    a, w = res
    gb = g.astype(BF16)
    da = lax.dot_general(gb, w.astype(BF16), (((1,), (1,)), ((), ())), preferred_element_type=F32)
    dw = lax.dot_general(a.astype(BF16), gb, (((0,), (0,)), ((), ())), preferred_element_type=F32)
    return da, dw


bdot.defvjp(lambda a, w: (bdot(a, w), (a, w)), _bdot_bwd)


@jax.custom_vjp
def log_sigmoid(z):
    return jnp.minimum(z, 0.0) - jnp.log(1.0 + jnp.exp(-jnp.abs(z)))


def _lsig_bwd(z, g):
    e = jnp.exp(-jnp.abs(z))
    return (g * jnp.where(z >= 0, e, 1.0) / (1.0 + e),)


log_sigmoid.defvjp(lambda z: (log_sigmoid(z), z), _lsig_bwd)


def silu(x):
    return x * jax.nn.sigmoid(x)


def _rms(x):
    return x * lax.rsqrt(jnp.mean(x * x, axis=-1, keepdims=True) + EPS)


def _mod(x, gain, shift, scale):
    return _rms(x) * gain * (1.0 + scale) + shift


def f_mod(xs, ps):
    ((h,),), ((gain,), (shift,), (scale,)) = xs, ps
    return [[_mod(h, gain, shift, scale)], [h]]


def f_res_mod(xs, ps):
    ((h,), (y,)), ((gate,), (gain,), (shift,), (scale,)) = xs, ps
    h1 = h + gate * y
    return [[h1], [_mod(h1, gain, shift, scale)]]


def f_ffn_mid(xs, ps):
    ((ua, ug),), ((w0a, w0g), (w1a, w1g), (w2a, w2g), (ba, bg)) = xs, ps
    a = w0a * prev_row(ua) + w1a * ua + w2a * next_row(ua) + ba
    g = w0g * prev_row(ug) + w1g * ug + w2g * next_row(ug) + bg
    return [[a * silu(g)]]


def f_sc_mid(xs, ps):
    ((bg, cg, v),), ((w0,), (w1,), (w2,)) = xs, ps
    z = cg * v
    return [[bg * (w0 * prev_tok(z) + w1 * z + w2 * next_tok(z))]]


def f_decay(xs, ps):
    ((a,),), ((wd,), (bd,)) = xs, ps
    return [[log_sigmoid(bdot(a, wd) + bd) / TAU]]


def f_gla_post(xs, ps):
    (of, ob, g), ((gain,),) = xs, ps
    return [[_rms(a + b) * gain * silu(c) for a, b, c in zip(of, ob, g)]]


NCH = TT // CHUNK
CTX_CH = CTX // CHUNK
_NT = (((1,), (1,)), ((), ()))
_TN = (((0,), (0,)), ((), ()))
_NN = (((1,), (0,)), ((), ()))


def _chunk_of(d, j):
    return jnp.where(d == 0, j, jnp.where(j < CTX_CH, CTX_CH - 1 - j, NCH + CTX_CH - 1 - j))


def _dot(a, b, dn):
    return lax.dot_general(a, b, dn, preferred_element_type=F32)


def _cumsum_rows(g, suffix):
    n = g.shape[0]
    row = lax.broadcasted_iota(jnp.int32, g.shape, 0)
    s = 1
    while s < n:
        if suffix:
            g = g + jnp.where(row < n - s, pltpu.roll(g, n - s, 0), 0.0)
        else:
            g = g + jnp.where(row >= s, pltpu.roll(g, s, 0), 0.0)
        s *= 2
    return g


def _causal(backward):
    row = lax.broadcasted_iota(jnp.int32, (CHUNK, CHUNK), 0)
    col = lax.broadcasted_iota(jnp.int32, (CHUNK, CHUNK), 1)
    return col >= row if backward else col <= row


def _gla_in_specs(bsz, rev):
    def blk(d, j):
        return _chunk_of(d, (NCH - 1 - j) if rev else j)

    return [
        pl.BlockSpec((bsz, CHUNK, KD), lambda d, j: (0, blk(d, j), 0)),
        pl.BlockSpec((bsz, CHUNK, KD), lambda d, j: (0, blk(d, j), 1)),
        pl.BlockSpec((bsz, CHUNK, VD), lambda d, j: (0, blk(d, j), 1)),
        pl.BlockSpec((bsz, CHUNK, KD), lambda d, j: (0, blk(d, j), d)),
    ], blk


def gla_fwd(pcat, la):
    bsz = pcat.shape[0]
    in_specs, blk = _gla_in_specs(bsz, False)

    def body(q_ref, k_ref, v_ref, la_ref, o_ref, s_ref, st):
        d, j = pl.program_id(0), pl.program_id(1)

        @pl.when(j == 0)
        def _():
            st[...] = jnp.zeros_like(st)

        s_ref[...] = st[...]

        def scan(backward):
            causal = _causal(backward)
            for e in range(bsz):
                g_all = la_ref[e]
                b_all = _cumsum_rows(g_all, backward)
                bl_all = jnp.sum(g_all, axis=0, keepdims=True)
                qs_all = (q_ref[e] * (HK ** -0.5) * jnp.exp(b_all)).astype(BF16)
                ks_all = (k_ref[e] * jnp.exp(-b_all)).astype(BF16)
                kd_all = (k_ref[e] * jnp.exp(bl_all - b_all)).astype(BF16)
                el_all = jnp.exp(bl_all)
                for h in range(HEADS):
                    ks_, vs_ = slice(h * HK, (h + 1) * HK), slice(h * HV, (h + 1) * HV)
                    qs, ks, kd, v = qs_all[:, ks_], ks_all[:, ks_], kd_all[:, ks_], v_ref[e, :, vs_].astype(BF16)
                    s = st[e, h]
                    att = jnp.where(causal, _dot(qs, ks, _NT), 0.0).astype(BF16)
                    o_ref[e, :, vs_] = _dot(qs, s.astype(BF16), _NT) + _dot(att, v, _NN)
                    st[e, h] = el_all[:, ks_] * s + _dot(v, kd, _TN)

        @pl.when(d == 0)
        def _():
            scan(False)

        @pl.when(d == 1)
        def _():
            scan(True)

    return pl.pallas_call(
        body, name="gla_fwd", grid=(2, NCH), in_specs=in_specs,
        out_specs=[pl.BlockSpec((bsz, CHUNK, VD), lambda d, j: (0, blk(d, j), d)),
                   pl.BlockSpec((bsz, None, None, HEADS, HV, HK), lambda d, j: (0, d, j, 0, 0, 0))],
        out_shape=[jax.ShapeDtypeStruct((bsz, TT, 2 * VD), F32), jax.ShapeDtypeStruct((bsz, 2, NCH, HEADS, HV, HK), F32)],
        scratch_shapes=[pltpu.VMEM((bsz, HEADS, HV, HK), F32)],
        compiler_params=pltpu.CompilerParams(dimension_semantics=("arbitrary", "arbitrary")),
    )(pcat, pcat, pcat, la)


def gla_bwd(pcat, la, s_all, do):
    bsz = pcat.shape[0]
    in_specs, blk = _gla_in_specs(bsz, True)
    in_specs += [
        pl.BlockSpec((bsz, None, None, HEADS, HV, HK), lambda d, j: (0, d, NCH - 1 - j, 0, 0, 0)),
        pl.BlockSpec((bsz, CHUNK, VD), lambda d, j: (0, jnp.maximum(blk(d, j) - CTX_CH, 0), 0)),
    ]

    def body(q_ref, k_ref, v_ref, la_ref, s_ref, do_ref, dq_ref, dk_ref, dv_ref, dla_ref, dst):
        d, j = pl.program_id(0), pl.program_id(1)

        @pl.when(j == 0)
        def _():
            dst[...] = jnp.zeros_like(dst)

        latent = blk(d, j) >= CTX_CH
        scale = HK ** -0.5

        def scan(backward):
            causal = _causal(backward)
            for e in range(bsz):
                g_all = la_ref[e]
                b_all = _cumsum_rows(g_all, backward)
                bl_all = jnp.sum(g_all, axis=0, keepdims=True)
                ex_all, ei_all, ed_all, el_all = jnp.exp(b_all), jnp.exp(-b_all), jnp.exp(bl_all - b_all), jnp.exp(bl_all)
                qs_all, ks_all, kd_all = q_ref[e] * scale * ex_all, k_ref[e] * ei_all, k_ref[e] * ed_all
                qsb_all, ksb_all, kdb_all = qs_all.astype(BF16), ks_all.astype(BF16), kd_all.astype(BF16)
                db_parts, dbl_parts = [], []
                for h in range(HEADS):
                    ks_, vs_ = slice(h * HK, (h + 1) * HK), slice(h * HV, (h + 1) * HV)
                    qs, ks, kd, el = qs_all[:, ks_], ks_all[:, ks_], kd_all[:, ks_], el_all[:, ks_]
                    qsb, ksb, kdb, v = qsb_all[:, ks_], ksb_all[:, ks_], kdb_all[:, ks_], v_ref[e, :, vs_].astype(BF16)
                    s, ds1 = s_ref[e, h], dst[e, h]
                    sb, ds1b = s.astype(BF16), ds1.astype(BF16)
                    dob = jnp.where(latent, do_ref[e, :, vs_], 0.0).astype(BF16)
                    att = jnp.where(causal, _dot(qsb, ksb, _NT), 0.0).astype(BF16)
                    datt = jnp.where(causal, _dot(dob, v, _NT), 0.0).astype(BF16)
                    dqs = _dot(dob, sb, _NN) + _dot(datt, ksb, _NN)
                    dks = _dot(datt, qsb, _TN)
                    dv_ref[e, :, vs_] = _dot(att, dob, _TN) + _dot(kdb, ds1b, _NT)
                    dkd = _dot(v, ds1b, _NN)
                    dst[e, h] = _dot(dob, qsb, _TN) + el * ds1
                    del_ = jnp.sum(s * ds1, axis=0, keepdims=True)
                    dq_ref[e, :, ks_] = dqs * ex_all[:, ks_] * scale
                    dk_ref[e, :, ks_] = dks * ei_all[:, ks_] + dkd * ed_all[:, ks_]
                    db_parts.append(dqs * qs - dks * ks - dkd * kd)
                    dbl_parts.append(jnp.sum(dkd * kd, axis=0, keepdims=True) + del_ * el)
                dla_ref[e] = _cumsum_rows(jnp.concatenate(db_parts, -1), not backward) + jnp.concatenate(dbl_parts, -1)

        @pl.when(d == 0)
        def _():
            scan(False)

        @pl.when(d == 1)
        def _():
            scan(True)

    return pl.pallas_call(
        body, name="gla_bwd", grid=(2, NCH), in_specs=in_specs,
        out_specs=[pl.BlockSpec((None, bsz, CHUNK, KD), lambda d, j: (d, 0, blk(d, j), 0)),
                   pl.BlockSpec((None, bsz, CHUNK, KD), lambda d, j: (d, 0, blk(d, j), 0)),
                   pl.BlockSpec((None, bsz, CHUNK, VD), lambda d, j: (d, 0, blk(d, j), 0)),
                   pl.BlockSpec((bsz, CHUNK, KD), lambda d, j: (0, blk(d, j), d))],
        out_shape=[jax.ShapeDtypeStruct((2, bsz, TT, KD), F32), jax.ShapeDtypeStruct((2, bsz, TT, KD), F32),
                   jax.ShapeDtypeStruct((2, bsz, TT, VD), F32), jax.ShapeDtypeStruct((bsz, TT, 2 * KD), F32)],
        scratch_shapes=[pltpu.VMEM((bsz, HEADS, HV, HK), F32)],
        compiler_params=pltpu.CompilerParams(dimension_semantics=("arbitrary", "arbitrary")),
    )(pcat, pcat, pcat, la, s_all, do)


def gla_combine(dq2, dk2, dv2, dgate, dpa):
    bsz = dgate.shape[0]
    tm = CTX

    def body(dq_ref, dk_ref, dv_ref, dg_ref, dpa_ref, o_ref):
        t = pl.program_id(1)
        o_ref[:, 0:KD] = (dq_ref[0] + dq_ref[1]).astype(BF16)
        o_ref[:, KD:2 * KD] = (dk_ref[0] + dk_ref[1]).astype(BF16)
        o_ref[:, 2 * KD:2 * KD + VD] = (dv_ref[0] + dv_ref[1]).astype(BF16)
        o_ref[:, 2 * KD + VD:2 * KD + 2 * VD] = jnp.where(t > 0, dg_ref[...], 0).astype(BF16)
        o_ref[:, 2 * KD + 2 * VD:] = dpa_ref[...].astype(BF16)

    return pl.pallas_call(
        body, name="gla_combine", grid=(bsz, TT // tm),
        in_specs=[pl.BlockSpec((2, None, tm, KD), lambda b, t: (0, b, t, 0)),
                  pl.BlockSpec((2, None, tm, KD), lambda b, t: (0, b, t, 0)),
                  pl.BlockSpec((2, None, tm, VD), lambda b, t: (0, b, t, 0)),
                  pl.BlockSpec((None, tm, VD), lambda b, t: (b, jnp.maximum(t - 1, 0), 0)),
                  pl.BlockSpec((None, tm, 128), lambda b, t: (b, t, 0))],
        out_specs=pl.BlockSpec((None, tm, GLA_IN_PAD), lambda b, t: (b, t, 0)),
        out_shape=jax.ShapeDtypeStruct((bsz, TT, GLA_IN_PAD), BF16),
        compiler_params=pltpu.CompilerParams(dimension_semantics=("arbitrary", "arbitrary")),
    )(dq2, dk2, dv2, dgate, dpa)


def final_loss(h1, fo, gate, gain, tgt):
    bsz, t_len, _ = h1.shape
    tm = 256

    def body(h_ref, f_ref, gate_ref, gain_ref, tgt_ref, loss_ref, dh_ref, df_ref, dgate_ref, dgain_ref):
        b, t = pl.program_id(0), pl.program_id(1)
        target = tgt_ref[...]

        def core(h, fo_, gate_, gain_):
            e = _rms(h + gate_ * fo_) * gain_ - target
            return jnp.sum(0.5 * jnp.sum(e * e, axis=-1, keepdims=True) / D, axis=0, keepdims=True)

        loss, vjp = jax.vjp(core, h_ref[...], f_ref[...], gate_ref[...], gain_ref[...])
        dh, df, dgate, dgain = vjp(jnp.ones((1, 1), F32))
        dh_ref[...] = dh
        df_ref[...] = df.astype(BF16)
        first = jnp.logical_and(b == 0, t == 0)

        @pl.when(first)
        def _():
            loss_ref[...] = jnp.broadcast_to(loss, loss_ref.shape)
            dgain_ref[...] = dgain

        @pl.when(jnp.logical_not(first))
        def _():
            loss_ref[...] += jnp.broadcast_to(loss, loss_ref.shape)
            dgain_ref[...] += dgain

        @pl.when(t == 0)
        def _():
            dgate_ref[...] = dgate

        @pl.when(t > 0)
        def _():
            dgate_ref[...] += dgate

    tile = pl.BlockSpec((None, tm, D), lambda b, t: (b, t, 0))
    per_ex = pl.BlockSpec((None, 1, D), lambda b, t: (b, 0, 0))
    shared = pl.BlockSpec((1, D), lambda b, t: (0, 0))
    return pl.pallas_call(
        body, name="final_loss", grid=(bsz, t_len // tm),
        in_specs=[tile, tile, per_ex, shared, tile],
        out_specs=[pl.BlockSpec((8, 128), lambda b, t: (0, 0)), tile, tile, per_ex, shared],
        out_shape=[jax.ShapeDtypeStruct((8, 128), F32), jax.ShapeDtypeStruct(h1.shape, F32),
                   jax.ShapeDtypeStruct(h1.shape, BF16), jax.ShapeDtypeStruct((bsz, 1, D), F32),
                   jax.ShapeDtypeStruct((1, D), F32)],
        compiler_params=pltpu.CompilerParams(dimension_semantics=("arbitrary", "arbitrary")),
    )(h1, fo, gate, gain, tgt)


ADA_ROWS = 24
ADA_CTX_ROW = 16
ADA_COLS = 6 * D // N_DEV


def ada_fwd(cond, w, b):
    def body(c_ref, w_ref, b_ref, o_ref):
        s = silu(c_ref[...]).astype(BF16)
        o_ref[...] = jnp.dot(s, w_ref[...].astype(BF16), preferred_element_type=F32) + b_ref[...]

    return pl.pallas_call(
        body, name="ada_fwd", grid=(2,),
        in_specs=[pl.BlockSpec((ADA_ROWS, D), lambda i: (0, 0)), pl.BlockSpec((None, D, ADA_COLS), lambda i: (i, 0, 0)),
                  pl.BlockSpec((None, 1, ADA_COLS), lambda i: (i, 0, 0))],
        out_specs=pl.BlockSpec((None, ADA_ROWS, ADA_COLS), lambda i: (i, 0, 0)),
        out_shape=jax.ShapeDtypeStruct((2, ADA_ROWS, ADA_COLS), F32),
    )(cond, w, b)


def ada_bwd(cond, dm_mine, dm_full, w):
    def body(c_ref, dm_ref, dmf_ref, w_ref, gw_ref, gb_ref, cp_ref):
        i = pl.program_id(0)
        s = silu(c_ref[...]).astype(BF16)
        dm = dm_ref[...].astype(BF16)
        gw_ref[...] = _dot(s, dm, _TN)
        gb_ref[...] = jnp.sum(dmf_ref[...], axis=0, keepdims=True)

        @pl.when(i == 0)
        def _():
            cp_ref[...] = _dot(dm_ref[ADA_CTX_ROW:, :].astype(BF16), w_ref[...].astype(BF16), _NT)

    return pl.pallas_call(
        body, name="ada_bwd", grid=(2,),
        in_specs=[pl.BlockSpec((ADA_ROWS, D), lambda i: (0, 0)), pl.BlockSpec((None, ADA_ROWS, ADA_COLS), lambda i: (i, 0, 0)),
                  pl.BlockSpec((None, ADA_ROWS, 6 * D), lambda i: (i, 0, 0)), pl.BlockSpec((None, D, ADA_COLS), lambda i: (i, 0, 0))],
        out_specs=[pl.BlockSpec((None, D, ADA_COLS), lambda i: (i, 0, 0)), pl.BlockSpec((None, 1, 6 * D), lambda i: (i, 0, 0)),
                   pl.BlockSpec((ADA_ROWS - ADA_CTX_ROW, D), lambda i: (0, 0))],
        out_shape=[jax.ShapeDtypeStruct((2, D, ADA_COLS), F32), jax.ShapeDtypeStruct((2, 1, 6 * D), F32),
                   jax.ShapeDtypeStruct((ADA_ROWS - ADA_CTX_ROW, D), F32)],
        compiler_params=pltpu.CompilerParams(dimension_semantics=("arbitrary",)),
    )(cond, dm_mine, dm_full, w)


def cctx_grad(parts, c_ctx):
    def body(p_ref, c_ref, o_ref):
        tot = p_ref[0:1, :]
        for i in range(1, N_DEV):
            tot = tot + p_ref[i:i + 1, :]
        c = c_ref[...]
        sg = jax.nn.sigmoid(c)
        o_ref[...] = tot * sg * (1.0 + c * (1.0 - sg))

    return pl.pallas_call(body, name="cctx_grad", out_shape=jax.ShapeDtypeStruct((1, D), F32))(parts, c_ctx)


def _row_tile(r):
    for t in (512, 256, 128, 80, 64, 40, 32, 16, 8):
        if r % t == 0:
            return t
    return r


def _slot_sum(ref):
    tot = ref[0].astype(F32)
    for i in range(1, ref.shape[0]):
        tot = tot + ref[i].astype(F32)
    return tot


def sum_slots(name, x):
    s, r, c = x.shape
    tr = _row_tile(r)

    def body(x_ref, o_ref):
        o_ref[...] = _slot_sum(x_ref)

    return pl.pallas_call(
        body, name=name, grid=(r // tr,), in_specs=[pl.BlockSpec((s, tr, c), lambda i: (0, i, 0))],
        out_specs=pl.BlockSpec((tr, c), lambda i: (i, 0)), out_shape=jax.ShapeDtypeStruct((r, c), F32),
    )(x)


def adamw(name, w, g, m, v, layer=None):
    r, c = w.shape[-2:]
    tr = _row_tile(r)
    stacked = g.ndim == 3

    def body(w_ref, g_ref, m_ref, v_ref, go_ref, d_ref, mo_ref, vo_ref):
        gv = _slot_sum(g_ref) if stacked else g_ref[...]
        mn = B1 * m_ref[...] + (1.0 - B1) * gv
        vn = B2 * v_ref[...] + (1.0 - B2) * jnp.square(gv)
        m_hat = mn / (1.0 - B1 ** STEP)
        v_hat = vn / (1.0 - B2 ** STEP)
        go_ref[...] = gv
        d_ref[...] = -LR * (m_hat / (jnp.sqrt(v_hat) + AEPS) + WD * w_ref[...])
        mo_ref[...] = mn
        vo_ref[...] = vn

    tile = pl.BlockSpec((tr, c), lambda i: (i, 0))
    slab = tile if layer is None else pl.BlockSpec((None, tr, c), lambda i: (layer, i, 0))
    g_spec = pl.BlockSpec((g.shape[0], tr, c), lambda i: (0, i, 0)) if stacked else tile
    return pl.pallas_call(
        body, name=name, grid=(r // tr,), in_specs=[slab, g_spec, slab, slab], out_specs=[tile] * 4,
        out_shape=[jax.ShapeDtypeStruct((r, c), F32)] * 4,
    )(w, g, m, v)


def _place():
    return lax.axis_index("x"), lax.axis_index("y"), lax.axis_index("c")


def all_gather(name, x, in_vmem):
    r, c = x.shape
    space = pltpu.VMEM if in_vmem else pl.ANY

    def body(x_ref, out_ref, send_sems, recv_sems, local_sem):
        px, py, pc = _place()
        me, sibling = (px, py, pc), (px, py, 1 - pc)
        chips = [(1 - px, py), (px, 1 - py), (1 - px, 1 - py)]

        def rows(qx, qy, qc):
            return out_ref.at[pl.ds((4 * qx + 2 * qy + qc) * r, r), :]

        def copy(k, block, to, src=None):
            return pltpu.make_async_remote_copy(
                src_ref=rows(*block) if src is None else src, dst_ref=rows(*block),
                send_sem=send_sems.at[k], recv_sem=recv_sems.at[k], device_id=to, device_id_type=MESH)

        mine = pltpu.make_async_copy(x_ref, rows(*me), local_sem)
        mine.start()
        first = [copy(0, me, sibling, src=x_ref)]
        first += [copy(1 + j, me, (*chip, pc), src=x_ref) for j, chip in enumerate(chips)]
        for cp in first:
            cp.start()
        passed = [copy(4 + j, (*chip, pc), sibling) for j, chip in enumerate(chips)]
        for j, chip in enumerate(chips):
            copy(1 + j, (*chip, pc), me).wait_recv()
            passed[j].start()
        copy(0, sibling, me).wait_recv()
        for j, chip in enumerate(chips):
            copy(4 + j, (*chip, 1 - pc), me).wait_recv()
        for cp in first + passed:
            cp.wait_send()
        mine.wait()

    return pl.pallas_call(
        body, name=name, out_shape=jax.ShapeDtypeStruct((N_DEV * r, c), x.dtype),
        in_specs=[pl.BlockSpec(memory_space=space)], out_specs=pl.BlockSpec(memory_space=space),
        scratch_shapes=[pltpu.SemaphoreType.DMA((7,)), pltpu.SemaphoreType.DMA((7,)), pltpu.SemaphoreType.DMA],
    )(x)


_HBM =pl.BlockSpec(memory_space=pltpu.HBM)
_SEM = pl.BlockSpec(memory_space=pltpu.SEMAPHORE)
_EFFECT = pltpu.SideEffectType.DATAFLOW_SIDE_EFFECTING


def _peers():
    px, py, pc = _place()
    return [(1 - px if k & 4 else px, 1 - py if k & 2 else py, 1 - pc if k & 1 else pc) for k in range(1, N_DEV)]


def _slot(dev):
    return 4 * dev[0] + 2 * dev[1] + dev[2]


def _split_copies(src_refs, land_refs, send_sems, recv_sems, gather):
    me = _slot(_place())
    return [pltpu.make_async_remote_copy(
        src_ref=src if gather else src.at[_slot(peer)], dst_ref=land.at[me],
        send_sem=send_sems.at[a * (N_DEV - 1) + k], recv_sem=recv_sems.at[a * (N_DEV - 1) + k],
        device_id=peer, device_id_type=MESH)
        for a, (src, land) in enumerate(zip(src_refs, land_refs)) for k, peer in enumerate(_peers())]


def exchange_start(name, srcs, gather, after):
    n = len(srcs)
    lands = [pltpu.HBM((N_DEV,) + s.shape if gather else s.shape, s.dtype) for s in srcs]

    def body(*refs):
        send_sems, recv_sems = refs[2 * n + 1:2 * n + 3]
        for cp in _split_copies(refs[:n], refs[n:2 * n], send_sems, recv_sems, gather):
            cp.start()
        refs[-1][...] = jnp.zeros_like(refs[-1])

    sems = pltpu.SemaphoreType.DMA((n * (N_DEV - 1),))
    res = pl.pallas_call(
        body, name=name,
        out_shape=(sems, sems, *[pltpu.HBM(s.shape, s.dtype) for s in srcs], *lands, jax.ShapeDtypeStruct((8, 128), F32)),
        in_specs=(_HBM,) * (2 * n) + (pl.BlockSpec(memory_space=pl.ANY),),
        out_specs=(_SEM, _SEM) + (_HBM,) * (2 * n) + (pl.BlockSpec(memory_space=pltpu.VMEM),),
        input_output_aliases={i: 2 + i for i in range(2 * n)},
        compiler_params=pltpu.CompilerParams(has_side_effects=_EFFECT),
    )(*[pltpu.with_memory_space_constraint(s, pltpu.HBM) for s in srcs],
      *[pltpu.with_memory_space_constraint(lax.empty(ld.shape, ld.dtype), pltpu.HBM) for ld in lands], after)
    return res[0], res[1], list(res[2:2 + n]), list(res[2 + n:2 + 2 * n]), res[-1]


def exchange_wait(name, started, after, gather):
    send_sems, recv_sems, srcs, lands, _ = started
    n = len(srcs)
    after = list(after) if isinstance(after, (list, tuple)) else [after]

    def body(*refs):
        send_sems, recv_sems = refs[2 * n:2 * n + 2]
        for cp in _split_copies(refs[:n], refs[n:2 * n], send_sems, recv_sems, gather):
            cp.wait_send()
            cp.wait_recv()

    res = pl.pallas_call(
        body, name=name, out_shape=tuple(pltpu.HBM(a.shape, a.dtype) for a in srcs + lands),
        in_specs=(_HBM,) * (2 * n) + (_SEM, _SEM) + (pl.BlockSpec(memory_space=pl.ANY),) * len(after),
        out_specs=(_HBM,) * (2 * n), input_output_aliases={i: i for i in range(2 * n)},
        compiler_params=pltpu.CompilerParams(has_side_effects=_EFFECT),
    )(*srcs, *lands, send_sems, recv_sems, *after)
    return list(res[:n]), list(res[n:])


NCF = FFN_H // FFN_TC


def _size(shape):
    n = 1
    for s in shape:
        n *= s
    return n


def _padded_rows(n_elems, row_mult):
    return -(-n_elems // (D * row_mult)) * row_mult


def _pack_rows(arrs, dtype, row_mult):
    rows, offs, r0 = [], [], 0
    for a in arrs:
        flat = a.reshape(-1).astype(dtype)
        n = _padded_rows(flat.shape[0], row_mult)
        rows.append(jnp.pad(flat, (0, n * D - flat.shape[0])).reshape(n, D))
        offs.append(r0)
        r0 += n
    return jnp.concatenate(rows, 0), offs


def _unpack_rows(buf, offs, shapes):
    lead, out = buf.shape[:-2], []
    for o, shp in zip(offs, shapes):
        n = _size(shp)
        nr = -(-n // D)
        out.append(buf[..., o:o + nr, :].reshape(lead + (nr * D,))[..., :n].reshape(lead + tuple(shp)))
    return out


def _rows3(w):
    return [w[i:i + 1] for i in range(3)]


def f_mod1(xs, ps):
    return f_mod(xs, ps)[:1]


def kernel(x, c, ctx, c_ctx, ada_w, ada_b, norm_mix, norm_ffn, gla_w_in, gla_w_a2, gla_b_a, gla_head_norm, gla_w_out, sc_w_in, sc_conv_w, sc_w_out, ffn_w_up, ffn_conv_w, ffn_conv_b, ffn_w_down, final_norm, loss_target, m_c_ctx, m_ada_w, m_ada_b, m_norm_mix, m_norm_ffn, m_gla_w_in, m_gla_w_a2, m_gla_b_a, m_gla_head_norm, m_gla_w_out, m_sc_w_in, m_sc_conv_w, m_sc_w_out, m_ffn_w_up, m_ffn_conv_w, m_ffn_conv_b, m_ffn_w_down, m_final_norm, v_c_ctx, v_ada_w, v_ada_b, v_norm_mix, v_norm_ffn, v_gla_w_in, v_gla_w_a2, v_gla_b_a, v_gla_head_norm, v_gla_w_out, v_sc_w_in, v_sc_conv_w, v_sc_w_out, v_ffn_w_up, v_ffn_conv_w, v_ffn_conv_b, v_ffn_w_down, v_final_norm):
    names = ["c_ctx", "ada_w", "ada_b", "norm_mix", "norm_ffn", "gla_w_in", "gla_w_a2", "gla_b_a", "gla_head_norm",
             "gla_w_out", "sc_w_in", "sc_conv_w", "sc_w_out", "ffn_w_up", "ffn_conv_w", "ffn_conv_b", "ffn_w_down",
             "final_norm"]
    w_ = dict(zip(names, [c_ctx, ada_w, ada_b, norm_mix, norm_ffn, gla_w_in, gla_w_a2, gla_b_a, gla_head_norm, gla_w_out,
                          sc_w_in, sc_conv_w, sc_w_out, ffn_w_up, ffn_conv_w, ffn_conv_b, ffn_w_down, final_norm]))
    m_ = dict(zip(names, [m_c_ctx, m_ada_w, m_ada_b, m_norm_mix, m_norm_ffn, m_gla_w_in, m_gla_w_a2, m_gla_b_a,
                          m_gla_head_norm, m_gla_w_out, m_sc_w_in, m_sc_conv_w, m_sc_w_out, m_ffn_w_up, m_ffn_conv_w,
                          m_ffn_conv_b, m_ffn_w_down, m_final_norm]))
    v_ = dict(zip(names, [v_c_ctx, v_ada_w, v_ada_b, v_norm_mix, v_norm_ffn, v_gla_w_in, v_gla_w_a2, v_gla_b_a,
                          v_gla_head_norm, v_gla_w_out, v_sc_w_in, v_sc_conv_w, v_sc_w_out, v_ffn_w_up, v_ffn_conv_w,
                          v_ffn_conv_b, v_ffn_w_down, v_final_norm]))
    me = 4 * lax.axis_index("x") + 2 * lax.axis_index("y") + lax.axis_index("c")
    bsz = x.shape[0]
    tm = 256
    nt = SEQ // tm
    ctx_tiles = CTX // tm
    pe = functools.partial(P, per_example=True)

    groups = {"ffn1": [("ffn_w_up", 1), ("ffn_w_down", 1)], "sc": [("sc_w_in", 0), ("sc_w_out", 0)],
              "ffn0": [("ffn_w_up", 0), ("ffn_w_down", 0)], "gla": [("gla_w_in", 0), ("gla_w_out", 0)]}
    ag_groups = {"gin": [("gla_w_in", 0)], "ffn0": [("gla_w_out", 0), ("ffn_w_up", 0), ("ffn_w_down", 0)],
                 "sc": groups["sc"], "ffn1": groups["ffn1"]}
    ag_started = {}

    def start_gather(g, after):
        ag_started[g] = exchange_start(f"ag_{g}_start", [w_[n][i].astype(BF16) for n, i in ag_groups[g]], True, after)
        return ag_started[g][4]

    small_sharded = [c, gla_w_a2, gla_b_a, sc_conv_w, ffn_conv_w]
    pack0, offs0 = _pack_rows(small_sharded, F32, 8)
    g0 = all_gather("ag_small", pack0, True).reshape(N_DEV, pack0.shape[0], D)
    c_all, wa2_s, ba_s, scw_s, fcw_s = _unpack_rows(g0, offs0, [a.shape for a in small_sharded])
    w_a2 = wa2_s[:, 0].transpose(1, 2, 0, 3).reshape(2, RANK, KD)
    b_a = ba_s[:, 0].transpose(1, 0, 2).reshape(2, KD)
    sc_cw = scw_s[:, 0].transpose(1, 0, 2).reshape(3, D)
    ffn_cw = fcw_s.transpose(1, 2, 0, 3).reshape(2, 3, 2 * FFN_H)

    cond = jnp.concatenate([c_all.reshape(N_DEV * bsz, D), c_ctx[None], jnp.zeros((ADA_ROWS - N_DEV * bsz - 1, D), F32)], 0)
    b_mine = lax.dynamic_slice(ada_b, (0, me * ADA_COLS), (2, ADA_COLS)).reshape(2, 1, ADA_COLS)
    mod_part = ada_fwd(cond, ada_w, b_mine)
    mod = all_gather("ag_mod", mod_part.reshape(2 * ADA_ROWS, ADA_COLS), True)
    mod = mod.reshape(N_DEV, 2, ADA_ROWS, ADA_COLS).transpose(1, 2, 0, 3).reshape(2, ADA_ROWS, 6 * D)
    mods = lax.dynamic_slice(mod, (0, bsz * me, 0), (2, bsz, 6 * D))
    md = [[mods[i][:, k * D:(k + 1) * D].reshape(bsz, 1, D) for k in range(6)] for i in range(2)]
    mc = [mod[0, ADA_CTX_ROW, k * D:(k + 1) * D][None] for k in range(2)]

    tok = mod
    for g in ag_groups:
        tok = start_gather(g, tok)
    norm_mix = norm_mix + tok[0, 0]

    def gathered(g, after):
        mine, lands = exchange_wait(f"ag_{g}_wait", ag_started[g], after, True)
        return [lax.dynamic_update_index_in_dim(ld, mn, me, 0) for ld, mn in zip(lands, mine)]

    s_up, w_down = [None, None], [None, None]
    wd = jnp.zeros((128, 2 * KD), F32).at[:RANK, :KD].set(w_a2[0]).at[RANK:2 * RANK, KD:].set(w_a2[1])
    bd = b_a.reshape(1, 2 * KD)
    scw = _rows3(sc_cw)
    head_gain = gla_head_norm.reshape(1, HV)
    gains_mix = [norm_mix[i][None] for i in range(2)]
    gains_ffn = [norm_ffn[i][None] for i in range(2)]

    def tokens(a2d, t_len):
        return a2d.reshape(bsz, t_len, -1)

    def ffn_params(i):
        rows = [ffn_cw[i][t] for t in range(3)] + [ffn_conv_b[i]]
        return [P(a.reshape(2, FFN_H), w=FFN_TC, rows=True) for a in rows]

    def ffn_fwd(i, hn2):
        u = mm(f"ffn_up{i}", V(hn2, "tok"), V(s_up[i], "cols"), out="planes", out_dtype=BF16, planes_t=SEQ)
        act = rowwise(f"ffn_mid{i}", f_ffn_mid, [X(u, w=FFN_TC, planes=True)], ffn_params(i), tm=SEQ, nt=1, nc=NCF,
                      outs=[(FFN_TC, BF16, 1)])[0]
        return u, act

    def arrays(ps):
        return [p["a"] for p in ps]

    ps_in0 = [P(gains_mix[0]), pe(md[0][0]), pe(md[0][1])]
    ps_ctx = [P(gains_mix[0]), P(mc[0]), P(mc[1])]
    hn0 = rowwise("mod_in0", f_mod, [X(x)], ps_in0, tm=tm, nt=nt, outs=[(D, BF16, 1)])[0]
    hnc = rowwise("mod_ctx", f_mod, [X(ctx)], ps_ctx, tm=tm, nt=ctx_tiles, outs=[(D, BF16, 1)])[0]
    hcat = jnp.concatenate([hnc, hn0], axis=1)
    (s_gin,) = gathered("gin", hcat)
    w_gin = V(s_gin, "cols", width=GLA_IN_PAD)
    pcat = tokens(mm("gla_in", V(hcat, "tok"), w_gin), TT)
    pa_x = X(pcat, w=128, co=(GLA_IN_PAD - 128) // 128)
    la = rowwise("gla_decay", f_decay, [pa_x], [P(wd), P(bd)], tm=tm, nt=TT // tm, outs=[(2 * KD, F32, 1)])[0]
    o2, s_all = gla_fwd(pcat, la)
    post_xs = [X(o2, w=VD, co=0, ro=ctx_tiles, split=HEADS), X(o2, w=VD, co=1, ro=ctx_tiles, split=HEADS),
               X(pcat, w=VD, co=2, ro=ctx_tiles, split=HEADS)]
    yin0 = rowwise("gla_post", f_gla_post, post_xs, [P(head_gain)], tm=tm, nt=nt, outs=[(VD, BF16, HEADS)])[0]
    s_gout, s_up[0], s_down0 = gathered("ffn0", yin0)
    w_gout, w_down[0] = s_gout.reshape(VD, D), s_down0.reshape(FFN_H, D)
    ps_mid0 = [pe(md[0][2]), P(gains_ffn[0]), pe(md[0][3]), pe(md[0][4])]
    y0, h1_0, hn2_0 = mm_res_mod("gla_out", yin0, w_gout, x, *arrays(ps_mid0))
    u0, act0 = ffn_fwd(0, hn2_0)
    ps_in1 = [pe(md[0][5]), P(gains_mix[1]), pe(md[1][0]), pe(md[1][1])]
    fo0, h2_0, hn1 = mm_res_mod("ffn_down0", act0, w_down[0], h1_0, *arrays(ps_in1))

    s_sin, s_sout = gathered("sc", hn1)
    w_sout = s_sout.reshape(D, D)
    p1 = tokens(mm("sc_in", V(hn1, "tok"), V(s_sin, "cols")), SEQ)
    sc_ps = [P(a) for a in scw]
    yin1 = rowwise("sc_mid", f_sc_mid, [X(p1, split=3)], sc_ps, tm=tm, nt=nt, outs=[(D, BF16, 1)])[0]
    ps_mid1 = [pe(md[1][2]), P(gains_ffn[1]), pe(md[1][3]), pe(md[1][4])]
    y1, h1_1, hn2_1 = mm_res_mod("sc_out", yin1, w_sout, h2_0, *arrays(ps_mid1))
    s_up[1], s_down1 = gathered("ffn1", hn2_1)
    w_down[1] = s_down1.reshape(FFN_H, D)
    u1, act1 = ffn_fwd(1, hn2_1)
    fo1 = tokens(mm("ffn_down1", V(act1, "tok"), V(w_down[1])), SEQ)
    loss8, dh1_1, dfo1, dm5_1, g_final = final_loss(h1_1, fo1, md[1][5], final_norm[None], loss_target)

    def ffn_bwd(i, u, act, hn2, dfo):
        dact = tokens(mm(f"ffn_down_dx{i}", V(dfo, "tok"), V(w_down[i]), form="nt", out_dtype=BF16), SEQ)
        g_down = mm(f"ffn_down_dw{i}", V(act, "tok"), V(dfo, "tok"), form="tn", out_dtype=BF16)
        r = rowwise(f"ffn_mid_bwd{i}", f_ffn_mid, [X(u, w=FFN_TC, planes=True)], ffn_params(i), tm=SEQ, nt=1, nc=NCF,
                    douts=[X(dact, w=FFN_TC)], dx={0: BF16}, dp=[0, 1, 2, 3])
        du, g_cw, g_cb = r[0], jnp.stack([a.reshape(2 * FFN_H) for a in r[1:4]]), r[4].reshape(1, 2 * FFN_H)
        dhn2 = tokens(mm(f"ffn_up_dx{i}", V(du, "planes"), V(s_up[i], "cols"), form="nt", out_dtype=BF16), SEQ)
        g_up = mm(f"ffn_up_dw{i}", V(hn2, "tok"), V(du, "planes"), form="tn", out="cols", out_dtype=BF16)
        return dhn2, g_up, row_slots(g_down), g_cw, g_cb

    def res_mod_bwd(name, h, y, ps, dh1, dhn):
        return rowwise(name, f_res_mod, [X(h), X(y)], ps, tm=tm, nt=nt, douts=[X(dh1), X(dhn)],
                       dx={0: F32, 1: BF16}, dp=[0, 1, 2, 3])

    def row_slots(g):
        return g.reshape(N_DEV, -1, g.shape[-1])

    a2a_started = {}

    def send_grads(g, slots, after=None):
        a2a_started[g] = exchange_start(f"a2a_{g}_start", list(slots), False, loss8 if after is None else after)
        return a2a_started[g][4][0, 0]

    def after_start(ps, tok):
        return [dict(ps[0], a=ps[0]["a"] + tok)] + ps[1:]

    dhn2_1, g_up1, g_down1, g_fcw1, g_fcb1 = ffn_bwd(1, u1, act1, hn2_1, dfo1)
    tok = send_grads("ffn1", [g_up1, g_down1])
    dh2_0, dy1, dm2_1, g_nffn1, dm3_1, dm4_1 = res_mod_bwd("res_mod_mid1_bwd", h2_0, y1, after_start(ps_mid1, tok), dh1_1, dhn2_1)
    dyin1 = tokens(mm("sc_out_dx", V(dy1, "tok"), V(w_sout), form="nt", out_dtype=BF16), SEQ)
    g_sout = row_slots(mm("sc_out_dw", V(yin1, "tok"), V(dy1, "tok"), form="tn", out_dtype=BF16))
    r = rowwise("sc_mid_bwd", f_sc_mid, [X(p1, split=3)], sc_ps, tm=tm, nt=nt, douts=[X(dyin1)], dx={0: BF16}, dp=[0, 1, 2])
    dp1, g_scw = r[0], jnp.concatenate(r[1:4], 0)
    dhn1 = tokens(mm("sc_in_dx", V(dp1, "tok"), V(s_sin, "cols"), form="nt", out_dtype=BF16), SEQ)
    g_sin = mm("sc_in_dw", V(hn1, "tok"), V(dp1, "tok"), form="tn", out="cols", out_dtype=BF16)
    tok = send_grads("sc", [g_sin, g_sout])
    dh1_0, dfo0, dm5_0, g_nmix1, dm0_1, dm1_1 = res_mod_bwd("res_mod_in1_bwd", h1_0, fo0, after_start(ps_in1, tok), dh2_0, dhn1)

    dhn2_0, g_up0, g_down0, g_fcw0, g_fcb0 = ffn_bwd(0, u0, act0, hn2_0, dfo0)
    tok = send_grads("ffn0", [g_up0, g_down0])
    dx_res, dy0, dm2_0, g_nffn0, dm3_0, dm4_0 = res_mod_bwd("res_mod_mid0_bwd", x, y0, after_start(ps_mid0, tok), dh1_0, dhn2_0)
    dyin0 = tokens(mm("gla_out_dx", V(dy0, "tok"), V(w_gout), form="nt", out_dtype=BF16), SEQ)
    g_gout = row_slots(mm("gla_out_dw", V(yin0, "tok"), V(dy0, "tok"), form="tn", out_dtype=BF16))
    do, dgate, g_head = rowwise("gla_post_bwd", f_gla_post, post_xs, [P(head_gain)], tm=tm, nt=nt,
                                douts=[X(dyin0, split=HEADS)], dx={0: F32, 2: BF16}, dp=[0])
    dq2, dk2, dv2, dla = gla_bwd(pcat, la, s_all, do)
    dpa, g_wd, g_bd = rowwise("gla_decay_bwd", f_decay, [pa_x], [P(wd), P(bd)], tm=tm, nt=TT // tm, douts=[X(dla)],
                              dx={0: BF16}, dp=[0, 1])
    dpcat = gla_combine(dq2, dk2, dv2, dgate, dpa)
    dhcat = tokens(mm("gla_in_dx", V(dpcat, "tok"), w_gin, form="nt", out_dtype=BF16), TT)
    g_gin = mm("gla_in_dw", V(hcat, "tok"), V(dpcat, "tok"), form="tn", out="cols", out_dtype=BF16, shard_n=GLA_IN // N_DEV)
    grad_x, g_nmix0, dm0_0, dm1_0 = rowwise("mod_in0_bwd", f_mod, [X(x)], ps_in0, tm=tm, nt=nt,
                                            douts=[X(dhcat, ro=ctx_tiles), X(dx_res)], dx={0: F32}, dp=[0, 1, 2])
    g_nmix0c, dmc0, dmc1 = rowwise("mod_ctx_bwd", f_mod1, [X(ctx)], ps_ctx, tm=tm, nt=ctx_tiles, douts=[X(dhcat)],
                                   dx={}, dp=[0, 1, 2])

    zero_row = jnp.zeros((1, 4 * D), F32)
    dmod = [jnp.concatenate([jnp.concatenate([a.reshape(bsz, D) for a in dms], 1), ctx_row], 0)
            for dms, ctx_row in (([dm0_0, dm1_0, dm2_0, dm3_0, dm4_0, dm5_0], jnp.concatenate([dmc0, dmc1, zero_row], 1)),
                                 ([dm0_1, dm1_1, dm2_1, dm3_1, dm4_1, dm5_1], jnp.zeros((1, 6 * D), F32)))]
    g_wa2 = jnp.stack([g_wd[:RANK, :KD], g_wd[RANK:2 * RANK, KD:]])
    small_grads = [jnp.stack(dmod), jnp.concatenate([g_nmix0 + g_nmix0c, g_nmix1], 0), jnp.concatenate([g_nffn0, g_nffn1], 0),
                   g_head, jnp.concatenate([g_fcb0, g_fcb1], 0), g_final, g_wa2, g_bd.reshape(2, KD), g_scw,
                   jnp.stack([g_fcw0, g_fcw1]), loss8[:1]]
    pack1, offs1 = _pack_rows(small_grads, F32, 8)
    g1 = all_gather("ag_grads", pack1, True).reshape(N_DEV, pack1.shape[0], D)
    dmod_all = _unpack_rows(g1, offs1[:1], [small_grads[0].shape])[0]
    tot = _unpack_rows(sum_slots("sum_small", g1), offs1, [a.shape for a in small_grads])
    loss = tot[10][0, 0]
    dm_rows = dmod_all[:, :, :bsz].transpose(1, 0, 2, 3).reshape(2, N_DEV * bsz, 6 * D)
    dm_full = jnp.concatenate([dm_rows, tot[0][:, bsz:], jnp.zeros((2, ADA_ROWS - N_DEV * bsz - 1, 6 * D), F32)], 1)
    dm_mine = lax.dynamic_slice(dm_full, (0, 0, me * ADA_COLS), (2, ADA_ROWS, ADA_COLS))
    g_ada_w, g_ada_b, cpart = ada_bwd(cond, dm_mine, dm_full, ada_w)
    cparts = all_gather("ag_cctx", cpart, True).reshape(N_DEV, ADA_ROWS - ADA_CTX_ROW, D)[:, 0]
    g_cctx = cctx_grad(cparts, c_ctx[None])[0]
    tok = send_grads("gla", [g_gin, g_gout], after=g_cctx)

    def my_cols(full, n):
        return lax.dynamic_slice_in_dim(full, me * n, n, axis=full.ndim - 1)

    grads = {
        "c_ctx": g_cctx, "ada_b": g_ada_b.reshape(2, 6 * D), "norm_mix": tot[1], "norm_ffn": tot[2],
        "gla_head_norm": tot[3], "ffn_conv_b": tot[4], "final_norm": tot[5].reshape(D),
        "gla_w_a2": my_cols(tot[6], KD // N_DEV)[None], "gla_b_a": my_cols(tot[7], KD // N_DEV)[None],
        "sc_conv_w": my_cols(tot[8], D // N_DEV)[None], "ffn_conv_w": my_cols(tot[9], 2 * FFN_H // N_DEV),
    }

    res_ada = adamw("adamw_ada", *[a.reshape(2 * D, ADA_COLS) for a in (ada_w, g_ada_w, m_ada_w, v_ada_w)])
    grads["c_ctx"] = g_cctx + tok
    big = ["gla_w_in", "gla_w_out", "sc_w_in", "sc_w_out", "ffn_w_up", "ffn_w_down"]
    small = [n for n in names if n not in big and n != "ada_w"]
    g_small = _pack_rows([grads[n] for n in small], F32, 8)[0]
    res_small = adamw("adamw_small", _pack_rows([w_[n] for n in small], F32, 8)[0], g_small,
                      _pack_rows([m_[n] for n in small], F32, 8)[0], _pack_rows([v_[n] for n in small], F32, 8)[0])
    offs_s = _pack_rows([w_[n] for n in small], F32, 8)[1]

    big_res, done = {}, [res_small[0], res_ada[0]]
    for g in groups:
        sent, lands = exchange_wait(f"a2a_{g}_wait", a2a_started[g], done, False)
        for (n, i), mine, land in zip(groups[g], sent, lands):
            land = lax.dynamic_update_index_in_dim(land, lax.dynamic_index_in_dim(mine, me, 0, keepdims=False), me, 0)
            big_res[(n, i)] = adamw(f"adamw_{n}{i}", w_[n], land, m_[n], v_[n], layer=i)
            done.append(big_res[(n, i)][0])

    out = {}
    for kind, idx in (("grad", 0), ("delta", 1), ("new_m", 2), ("new_v", 3)):
        vals = {n: jnp.stack([big_res[(n, i)][idx] for i in range(w_[n].shape[0])]) for n in big}
        vals["ada_w"] = res_ada[idx].reshape(ada_w.shape)
        vals.update(zip(small, _unpack_rows(res_small[idx], offs_s, [w_[n].shape for n in small])))
        out[kind] = [vals[n] for n in names]
    return (loss, grad_x, *out["grad"], *out["delta"], *out["new_m"], *out["new_v"])
```

```python
import functools

import jax
import jax.numpy as jnp
from jax import lax
from jax.experimental import pallas as pl
from jax.experimental.pallas import tpu as pltpu

F32 = jnp.float32
BF16 = jnp.bfloat16

N_DEV = 8
D = 1024
SEQ = 2048
CTX = 256
TT = CTX + SEQ
GRID_W = 64
CHUNK = 64
HEADS = 4
HK = 128
HV = 256
KD = 512
VD = 1024
RANK = 16
TAU = 16.0
GLA_IN = 3104
GLA_IN_PAD = 3200
FFN_H = 2560
FFN_TC = 256
EPS = 1e-6
LR, B1, B2, AEPS, WD, STEP = 0.001, 0.9, 0.999, 1e-08, 0.01, 10
MESH = pl.DeviceIdType.MESH


def _blocks(n):
    return [n] + [t for t in range(n - n % 128, 0, -128) if n % t == 0 and t != n]


def V(arr, kind="flat", width=None):
    if kind == "tok":
        return V(arr.reshape(-1, arr.shape[-1]))
    if kind == "flat":
        r, c = arr.shape
        return dict(a=arr, kind=kind, shape=(r, c), rows=_blocks(r), cols=_blocks(c))
    if kind == "planes":
        bsz, _, t, ch = arr.shape
        return dict(a=arr, kind=kind, shape=(bsz * t, 2 * ch), rows=_blocks(t), cols=[2 * ch] + _blocks(ch), t=t, ch=ch)
    _, r, n = arr.shape
    if width is not None:
        return dict(a=arr, kind=kind, shape=(r, width), rows=_blocks(r), cols=[width], n=n, pad=width - N_DEV * n)
    return dict(a=arr, kind=kind, shape=(r, N_DEV * n), rows=_blocks(r), cols=[8 * n, 4 * n, 2 * n], n=n, pad=0)


def _view_spec(v, br, bc, idx):
    if v["kind"] == "flat":
        return pl.BlockSpec((br, bc), idx)
    if v["kind"] == "planes":
        nt = v["t"] // br
        if bc == 2 * v["ch"]:
            return pl.BlockSpec((None, 2, br, v["ch"]), lambda i, j, k: (idx(i, j, k)[0] // nt, 0, idx(i, j, k)[0] % nt, 0))
        nch = v["ch"] // bc

        def at(i, j, k):
            r, c = idx(i, j, k)
            return r // nt, c // nch, r % nt, c % nch
        return pl.BlockSpec((None, None, br, bc), at)
    return pl.BlockSpec(((bc - v["pad"]) // v["n"], br, v["n"]), lambda i, j, k: (idx(i, j, k)[1], idx(i, j, k)[0], 0))


def _out_view(kind, rows, cols, dtype, planes_t=None, shard_n=None):
    if kind == "flat":
        shape = (rows, cols)
    elif kind == "planes":
        shape = (rows // planes_t, 2, planes_t, cols // 2)
    elif shard_n is not None:
        return V(jax.ShapeDtypeStruct((N_DEV, rows, shard_n), dtype), kind, width=cols)
    else:
        shape = (N_DEV, rows, cols // N_DEV)
    return V(jax.ShapeDtypeStruct(shape, dtype), kind)


MM_VMEM_BUDGET = 40 * 2 ** 20
MM_VMEM_LIMIT = 56 * 2 ** 20
MM_MAX_TILE = 1536


def _mm_tiles(m, n, kk, ms, ns, ks, a_bytes, b_bytes, o_bytes):
    best = None
    for tk in ks:
        for tm in [t for t in ms if t <= MM_MAX_TILE] or ms:
            for tn in [t for t in ns if t <= MM_MAX_TILE] or ns:
                one_k = tk == kk
                need = 2 * (tm * tk * a_bytes + tk * tn * b_bytes + tm * tn * o_bytes) + (0 if one_k else tm * tn * 4)
                if need > MM_VMEM_BUDGET:
                    continue
                steps = (m // tm) * (n // tn) * (kk // tk)
                traffic = (m * kk * a_bytes * (1 if one_k else n // tn)
                           + kk * n * b_bytes * (1 if one_k and n == tn else m // tm) + m * n * o_bytes)
                fill = (tm * tk * a_bytes + tk * tn * b_bytes) / 2.5e12
                cost = max(2.0 * m * n * kk / (9e14 if one_k else 6.5e14), traffic / 2.5e12) + steps * 0.4e-6 + fill
                if best is None or cost < best[0]:
                    best = (cost, tm, tn, tk)
    return best[1:]


def mm(name, a, b, form="nn", out="flat", out_dtype=F32, planes_t=None, shard_n=None):
    (m, kk) = a["shape"][::-1] if form == "tn" else a["shape"]
    n = b["shape"][0] if form == "nt" else b["shape"][1]
    assert (b["shape"][1] if form == "nt" else b["shape"][0]) == kk, (name, a["shape"], b["shape"])
    o = _out_view(out, m, n, out_dtype, planes_t, shard_n)
    a_m, a_k = (a["cols"], a["rows"]) if form == "tn" else (a["rows"], a["cols"])
    b_k, b_n = (b["cols"], b["rows"]) if form == "nt" else (b["rows"], b["cols"])
    tm, tn, tk = _mm_tiles(m, n, kk, [t for t in a_m if t in o["rows"]], [t for t in b_n if t in o["cols"]],
                           [t for t in a_k if t in b_k], a["a"].dtype.itemsize, b["a"].dtype.itemsize,
                           jnp.dtype(out_dtype).itemsize)
    nk = kk // tk
    dn = (((0 if form == "tn" else 1,), (1 if form == "nt" else 0,)), ((), ()))

    def load(ref, v):
        if len(ref.shape) == 3:
            pieces = [ref[p].astype(BF16) for p in range(ref.shape[0])]
            if v.get("pad"):
                pieces.append(jnp.zeros(ref.shape[1:2] + (v["pad"],), BF16))
            return jnp.concatenate(pieces, axis=-1)
        return ref[...].astype(BF16)

    def store(o_ref, val):
        val = val.astype(out_dtype)
        if len(o_ref.shape) == 3:
            w = o_ref.shape[-1]
            for p in range(o_ref.shape[0]):
                o_ref[p] = val[:, p * w:(p + 1) * w]
        else:
            o_ref[...] = val

    def body(a_ref, b_ref, o_ref, *acc):
        if nk == 1:
            store(o_ref, lax.dot_general(load(a_ref, a), load(b_ref, b), dn, preferred_element_type=F32))
            return
        k, acc_ref = pl.program_id(2), acc[0]

        @pl.when(k == 0)
        def _():
            acc_ref[...] = jnp.zeros_like(acc_ref)

        acc_ref[...] += lax.dot_general(load(a_ref, a), load(b_ref, b), dn, preferred_element_type=F32)

        @pl.when(k == nk - 1)
        def _():
            store(o_ref, acc_ref[...])

    if form == "tn":
        a_spec = _view_spec(a, tk, tm, lambda i, j, k: (k, i))
    else:
        a_spec = _view_spec(a, tm, tk, lambda i, j, k: (i, k))
    if form == "nt":
        b_spec = _view_spec(b, tn, tk, lambda i, j, k: (j, k))
    else:
        b_spec = _view_spec(b, tk, tn, lambda i, j, k: (k, j))
    return pl.pallas_call(
        body, name=name, grid=(m // tm, n // tn, nk),
        in_specs=[a_spec, b_spec], out_specs=_view_spec(o, tm, tn, lambda i, j, k: (i, j)), out_shape=o["a"],
        scratch_shapes=[pltpu.VMEM((tm, tn), F32)] if nk > 1 else [],
        compiler_params=pltpu.CompilerParams(dimension_semantics=("parallel", "parallel", "arbitrary"),
                                             vmem_limit_bytes=MM_VMEM_LIMIT),
    )(a["a"], b["a"])


def mm_res_mod(name, a, w, h, gate, gain, shift, scale):
    bsz, t_len, kk = a.shape
    tm = 512
    per = t_len // tm

    def body(a_ref, w_ref, h_ref, gate_ref, gain_ref, shift_ref, scale_ref, y_ref, h1_ref, hn_ref):
        y = jnp.dot(a_ref[...].astype(BF16), w_ref[...].astype(BF16), preferred_element_type=F32)
        h1 = h_ref[...] + gate_ref[...] * y
        y_ref[...] = y.astype(BF16)
        h1_ref[...] = h1
        hn_ref[...] = _mod(h1, gain_ref[...], shift_ref[...], scale_ref[...]).astype(BF16)

    def tile(width):
        return pl.BlockSpec((None, tm, width), lambda i: (i // per, i % per, 0))

    per_ex = pl.BlockSpec((None, 1, D), lambda i: (i // per, 0, 0))
    return pl.pallas_call(
        body, name=name, grid=(bsz * per,),
        in_specs=[tile(kk), pl.BlockSpec((kk, D), lambda i: (0, 0)), tile(D), per_ex, pl.BlockSpec((1, D), lambda i: (0, 0)),
                  per_ex, per_ex],
        out_specs=[tile(D)] * 3,
        out_shape=[jax.ShapeDtypeStruct((bsz, t_len, D), BF16), jax.ShapeDtypeStruct((bsz, t_len, D), F32),
                   jax.ShapeDtypeStruct((bsz, t_len, D), BF16)],
        compiler_params=pltpu.CompilerParams(dimension_semantics=("parallel",), vmem_limit_bytes=MM_VMEM_LIMIT),
    )(a, w, h, gate, gain, shift, scale)


def X(arr, w=None, co=0, ro=0, split=1, planes=False):
    return dict(a=arr, w=arr.shape[-1] if w is None else w, co=co, ro=ro, split=2 if planes else split,
                mode="planes" if planes else "cols")


def P(arr, per_example=False, w=None, split=1, rows=False):
    return dict(a=arr, e=per_example, w=arr.shape[-1] if w is None else w, split=arr.shape[-2] if rows else split,
                mode="rows" if rows else "cols")


def _pieces(ref, s):
    if s["mode"] == "planes":
        return [ref[0], ref[1]]
    if s["mode"] == "rows":
        return [ref[i:i + 1, :] for i in range(s["split"])]
    w = ref.shape[-1] // s["split"]
    return [ref[:, i * w:(i + 1) * w] for i in range(s["split"])]


def _store(ref, pieces, s, accumulate=False):
    w = ref.shape[-1] // len(pieces)
    for i, p in enumerate(pieces):
        at = (i,) if s["mode"] == "planes" else (slice(i, i + 1),) if s["mode"] == "rows" else (slice(None), slice(i * w, (i + 1) * w))
        if accumulate:
            ref[at] += p.astype(ref.dtype)
        else:
            ref[at] = p.astype(ref.dtype)


def rowwise(name, f, xs, ps, *, tm, nt, nc=1, outs=None, douts=None, dx=None, dp=None):
    bsz = xs[0]["a"].shape[0]
    fwd = douts is None
    nx, np_ = len(xs), len(ps)
    douts = [] if fwd else douts
    dx = {} if fwd else dx
    dp = [] if fwd else dp

    def x_spec(s):
        if s["mode"] == "planes":
            return pl.BlockSpec((None, 2, tm, s["w"]), lambda c, b, t, s=s: (b, 0, t + s["ro"], c + s["co"]))
        return pl.BlockSpec((None, tm, s["w"]), lambda c, b, t, s=s: (b, t + s["ro"], c + s["co"]))

    def x_out(s, dt):
        if s["mode"] == "planes":
            return (jax.ShapeDtypeStruct((bsz, 2, nt * tm, nc * s["w"]), dt),
                    pl.BlockSpec((None, 2, tm, s["w"]), lambda c, b, t: (b, 0, t, c)))
        return (jax.ShapeDtypeStruct((bsz, nt * tm, nc * s["w"]), dt), pl.BlockSpec((None, tm, s["w"]), lambda c, b, t: (b, t, c)))

    def p_spec(s):
        r = s["a"].shape[-2]
        if s["e"]:
            return pl.BlockSpec((None, r, s["w"]), lambda c, b, t: (b, 0, c))
        return pl.BlockSpec((r, s["w"]), lambda c, b, t: (0, c))

    in_specs = [x_spec(s) for s in xs] + [p_spec(s) for s in ps] + [x_spec(s) for s in douts]
    operands = [s["a"] for s in xs] + [s["a"] for s in ps] + [s["a"] for s in douts]
    if fwd:
        out_modes = [dict(mode="cols", split=sp) for (_, _, sp) in outs]
        out_shape = [jax.ShapeDtypeStruct((bsz, nt * tm, nc * w), dt) for (w, dt, _) in outs]
        out_specs = [pl.BlockSpec((None, tm, w), lambda c, b, t: (b, t, c)) for (w, _, _) in outs]
    else:
        dx_outs = [x_out(xs[i], dt) for i, dt in dx.items()]
        out_shape, out_specs = [o[0] for o in dx_outs], [o[1] for o in dx_outs]
        for j in dp:
            s = ps[j]
            r = s["a"].shape[-2]
            if s["e"]:
                out_shape.append(jax.ShapeDtypeStruct((bsz, r, nc * s["w"]), F32))
                out_specs.append(pl.BlockSpec((None, r, s["w"]), lambda c, b, t: (b, 0, c)))
            else:
                out_shape.append(jax.ShapeDtypeStruct((r, nc * s["w"]), F32))
                out_specs.append(pl.BlockSpec((r, s["w"]), lambda c, b, t: (0, c)))

    def body(*refs):
        x_refs, p_refs = refs[:nx], refs[nx:nx + np_]
        d_refs = refs[nx + np_:nx + np_ + len(douts)]
        o_refs = refs[nx + np_ + len(douts):]
        xv = [[p.astype(F32) for p in _pieces(r, s)] for r, s in zip(x_refs, xs)]
        pv = [[p.astype(F32) for p in _pieces(r, s)] for r, s in zip(p_refs, ps)]
        if fwd:
            for r, pieces, s in zip(o_refs, f(xv, pv), out_modes):
                _store(r, pieces, s)
            return
        _, vjp = jax.vjp(f, xv, pv)
        cot = [[p.astype(F32) for p in _pieces(r, s)] for r, s in zip(d_refs, douts)]
        dxv, dpv = vjp(cot)
        for r, i in zip(o_refs, dx):
            _store(r, dxv[i], xs[i])
        b, t = pl.program_id(1), pl.program_id(2)
        for r, j in zip(o_refs[len(dx):], dp):
            first = (t == 0) if ps[j]["e"] else jnp.logical_and(b == 0, t == 0)

            @pl.when(first)
            def _(r=r, j=j):
                _store(r, dpv[j], ps[j])

            @pl.when(jnp.logical_not(first))
            def _(r=r, j=j):
                _store(r, dpv[j], ps[j], accumulate=True)

    res = pl.pallas_call(
        body, name=name, grid=(nc, bsz, nt), in_specs=in_specs, out_specs=out_specs, out_shape=out_shape,
        compiler_params=pltpu.CompilerParams(dimension_semantics=("arbitrary", "arbitrary", "arbitrary")),
    )(*operands)
    return res


def _keep_rows(a, shift, keep):
    n = a.shape[0]
    t = lax.broadcasted_iota(jnp.int32, a.shape, 0)
    return jnp.where(keep(t, n), pltpu.roll(a, shift % n, 0), 0.0)


def _shift_pair(step, keep_prev, keep_next):
    @jax.custom_vjp
    def prev(a):
        return _keep_rows(a, step, keep_prev)

    @jax.custom_vjp
    def nxt(a):
        return _keep_rows(a, -step, keep_next)

    prev.defvjp(lambda a: (prev(a), None), lambda _, g: (nxt(g),))
    nxt.defvjp(lambda a: (nxt(a), None), lambda _, g: (prev(g),))
    return prev, nxt


prev_tok, next_tok = _shift_pair(1, lambda t, n: t % GRID_W != 0, lambda t, n: t % GRID_W != GRID_W - 1)
prev_row, next_row = _shift_pair(GRID_W, lambda t, n: t >= GRID_W, lambda t, n: t < n - GRID_W)


@jax.custom_vjp
def bdot(a, w):
    return jnp.dot(a.astype(BF16), w.astype(BF16), preferred_element_type=F32)


def _bdot_bwd(res, g):
    a, w = res
    gb = g.astype(BF16)
    da = lax.dot_general(gb, w.astype(BF16), (((1,), (1,)), ((), ())), preferred_element_type=F32)
    dw = lax.dot_general(a.astype(BF16), gb, (((0,), (0,)), ((), ())), preferred_element_type=F32)
    return da, dw


bdot.defvjp(lambda a, w: (bdot(a, w), (a, w)), _bdot_bwd)


@jax.custom_vjp
def log_sigmoid(z):
    return jnp.minimum(z, 0.0) - jnp.log(1.0 + jnp.exp(-jnp.abs(z)))


def _lsig_bwd(z, g):
    e = jnp.exp(-jnp.abs(z))
    return (g * jnp.where(z >= 0, e, 1.0) / (1.0 + e),)


log_sigmoid.defvjp(lambda z: (log_sigmoid(z), z), _lsig_bwd)


def silu(x):
    return x * jax.nn.sigmoid(x)


def _rms(x):
    return x * lax.rsqrt(jnp.mean(x * x, axis=-1, keepdims=True) + EPS)


def _mod(x, gain, shift, scale):
    return _rms(x) * gain * (1.0 + scale) + shift


def f_mod(xs, ps):
    ((h,),), ((gain,), (shift,), (scale,)) = xs, ps
    return [[_mod(h, gain, shift, scale)], [h]]


def f_res_mod(xs, ps):
    ((h,), (y,)), ((gate,), (gain,), (shift,), (scale,)) = xs, ps
    h1 = h + gate * y
    return [[h1], [_mod(h1, gain, shift, scale)]]


def f_ffn_mid(xs, ps):
    ((ua, ug),), ((w0a, w0g), (w1a, w1g), (w2a, w2g), (ba, bg)) = xs, ps
    a = w0a * prev_row(ua) + w1a * ua + w2a * next_row(ua) + ba
    g = w0g * prev_row(ug) + w1g * ug + w2g * next_row(ug) + bg
    return [[a * silu(g)]]


def f_sc_mid(xs, ps):
    ((bg, cg, v),), ((w0,), (w1,), (w2,)) = xs, ps
    z = cg * v
    return [[bg * (w0 * prev_tok(z) + w1 * z + w2 * next_tok(z))]]


def f_decay(xs, ps):
    ((a,),), ((wd,), (bd,)) = xs, ps
    return [[log_sigmoid(bdot(a, wd) + bd) / TAU]]


def f_gla_post(xs, ps):
    (of, ob, g), ((gain,),) = xs, ps
    return [[_rms(a + b) * gain * silu(c) for a, b, c in zip(of, ob, g)]]


NCH = TT // CHUNK
CTX_CH = CTX // CHUNK
_NT = (((1,), (1,)), ((), ()))
_TN = (((0,), (0,)), ((), ()))
_NN = (((1,), (0,)), ((), ()))


def _chunk_of(d, j):
    return jnp.where(d == 0, j, jnp.where(j < CTX_CH, CTX_CH - 1 - j, NCH + CTX_CH - 1 - j))


def _dot(a, b, dn):
    return lax.dot_general(a, b, dn, preferred_element_type=F32)


def _cumsum_rows(g, suffix):
    n = g.shape[0]
    row = lax.broadcasted_iota(jnp.int32, g.shape, 0)
    s = 1
    while s < n:
        if suffix:
            g = g + jnp.where(row < n - s, pltpu.roll(g, n - s, 0), 0.0)
        else:
            g = g + jnp.where(row >= s, pltpu.roll(g, s, 0), 0.0)
        s *= 2
    return g


def _causal(backward):
    row = lax.broadcasted_iota(jnp.int32, (CHUNK, CHUNK), 0)
    col = lax.broadcasted_iota(jnp.int32, (CHUNK, CHUNK), 1)
    return col >= row if backward else col <= row


def _gla_in_specs(bsz, rev):
    def blk(d, j):
        return _chunk_of(d, (NCH - 1 - j) if rev else j)

    return [
        pl.BlockSpec((bsz, CHUNK, KD), lambda d, j: (0, blk(d, j), 0)),
        pl.BlockSpec((bsz, CHUNK, KD), lambda d, j: (0, blk(d, j), 1)),
        pl.BlockSpec((bsz, CHUNK, VD), lambda d, j: (0, blk(d, j), 1)),
        pl.BlockSpec((bsz, CHUNK, KD), lambda d, j: (0, blk(d, j), d)),
    ], blk


def gla_fwd(pcat, la):
    bsz = pcat.shape[0]
    in_specs, blk = _gla_in_specs(bsz, False)

    def body(q_ref, k_ref, v_ref, la_ref, o_ref, s_ref, st):
        d, j = pl.program_id(0), pl.program_id(1)

        @pl.when(j == 0)
        def _():
            st[...] = jnp.zeros_like(st)

        s_ref[...] = st[...]

        def scan(backward):
            causal = _causal(backward)
            for e in range(bsz):
                g_all = la_ref[e]
                b_all = _cumsum_rows(g_all, backward)
                bl_all = jnp.sum(g_all, axis=0, keepdims=True)
                qs_all = (q_ref[e].astype(F32) * (HK ** -0.5) * jnp.exp(b_all)).astype(BF16)
                ks_all = (k_ref[e] * jnp.exp(-b_all)).astype(BF16)
                kd_all = (k_ref[e] * jnp.exp(bl_all - b_all)).astype(BF16)
                el_all = jnp.exp(bl_all)
                for h in range(HEADS):
                    ks_, vs_ = slice(h * HK, (h + 1) * HK), slice(h * HV, (h + 1) * HV)
                    qs, ks, kd, v = qs_all[:, ks_], ks_all[:, ks_], kd_all[:, ks_], v_ref[e, :, vs_].astype(BF16)
                    s = st[e, h]
                    att = jnp.where(causal, _dot(qs, ks, _NT), 0.0).astype(BF16)
                    o_ref[e, :, vs_] = _dot(qs, s.astype(BF16), _NT) + _dot(att, v, _NN)
                    st[e, h] = el_all[:, ks_] * s + _dot(v, kd, _TN)

        @pl.when(d == 0)
        def _():
            scan(False)

        @pl.when(d == 1)
        def _():
            scan(True)

    return pl.pallas_call(
        body, name="gla_fwd", grid=(2, NCH), in_specs=in_specs,
        out_specs=[pl.BlockSpec((bsz, CHUNK, VD), lambda d, j: (0, blk(d, j), d)),
                   pl.BlockSpec((bsz, None, None, HEADS, HV, HK), lambda d, j: (0, d, j, 0, 0, 0))],
        out_shape=[jax.ShapeDtypeStruct((bsz, TT, 2 * VD), F32), jax.ShapeDtypeStruct((bsz, 2, NCH, HEADS, HV, HK), F32)],
        scratch_shapes=[pltpu.VMEM((bsz, HEADS, HV, HK), F32)],
        compiler_params=pltpu.CompilerParams(dimension_semantics=("arbitrary", "arbitrary")),
    )(pcat, pcat, pcat, la)


def gla_bwd(pcat, la, s_all, do):
    bsz = pcat.shape[0]
    in_specs, blk = _gla_in_specs(bsz, True)
    in_specs += [
        pl.BlockSpec((bsz, None, None, HEADS, HV, HK), lambda d, j: (0, d, NCH - 1 - j, 0, 0, 0)),
        pl.BlockSpec((bsz, CHUNK, VD), lambda d, j: (0, jnp.maximum(blk(d, j) - CTX_CH, 0), 0)),
    ]

    def body(q_ref, k_ref, v_ref, la_ref, s_ref, do_ref, dq_ref, dk_ref, dv_ref, dla_ref, dst):
        d, j = pl.program_id(0), pl.program_id(1)

        @pl.when(j == 0)
        def _():
            dst[...] = jnp.zeros_like(dst)

        latent = blk(d, j) >= CTX_CH
        scale = HK ** -0.5

        def scan(backward):
            causal = _causal(backward)
            for e in range(bsz):
                g_all = la_ref[e]
                b_all = _cumsum_rows(g_all, backward)
                bl_all = jnp.sum(g_all, axis=0, keepdims=True)
                ex_all, ei_all, ed_all, el_all = jnp.exp(b_all), jnp.exp(-b_all), jnp.exp(bl_all - b_all), jnp.exp(bl_all)
                qs_all, ks_all, kd_all = q_ref[e].astype(F32) * scale * ex_all, k_ref[e] * ei_all, k_ref[e] * ed_all
                qsb_all, ksb_all, kdb_all = qs_all.astype(BF16), ks_all.astype(BF16), kd_all.astype(BF16)
                db_parts, dbl_parts = [], []
                for h in range(HEADS):
                    ks_, vs_ = slice(h * HK, (h + 1) * HK), slice(h * HV, (h + 1) * HV)
                    qs, ks, kd, el = qs_all[:, ks_], ks_all[:, ks_], kd_all[:, ks_], el_all[:, ks_]
                    qsb, ksb, kdb, v = qsb_all[:, ks_], ksb_all[:, ks_], kdb_all[:, ks_], v_ref[e, :, vs_].astype(BF16)
                    s, ds1 = s_ref[e, h], dst[e, h]
                    sb, ds1b = s.astype(BF16), ds1.astype(BF16)
                    dob = jnp.where(latent, do_ref[e, :, vs_], 0.0).astype(BF16)
                    att = jnp.where(causal, _dot(qsb, ksb, _NT), 0.0).astype(BF16)
                    datt = jnp.where(causal, _dot(dob, v, _NT), 0.0).astype(BF16)
                    dqs = _dot(dob, sb, _NN) + _dot(datt, ksb, _NN)
                    dks = _dot(datt, qsb, _TN)
                    dv_ref[e, :, vs_] = (_dot(att, dob, _TN) + _dot(kdb, ds1b, _NT)).astype(BF16)
                    dkd = _dot(v, ds1b, _NN)
                    dst[e, h] = _dot(dob, qsb, _TN) + el * ds1
                    del_ = jnp.sum(s * ds1, axis=0, keepdims=True)
                    dq_ref[e, :, ks_] = (dqs * ex_all[:, ks_] * scale).astype(BF16)
                    dk_ref[e, :, ks_] = (dks * ei_all[:, ks_] + dkd * ed_all[:, ks_]).astype(BF16)
                    db_parts.append(dqs * qs - dks * ks - dkd * kd)
                    dbl_parts.append(jnp.sum(dkd * kd, axis=0, keepdims=True) + del_ * el)
                dla_ref[e] = _cumsum_rows(jnp.concatenate(db_parts, -1), not backward) + jnp.concatenate(dbl_parts, -1)

        @pl.when(d == 0)
        def _():
            scan(False)

        @pl.when(d == 1)
        def _():
            scan(True)

    return pl.pallas_call(
        body, name="gla_bwd", grid=(2, NCH), in_specs=in_specs,
        out_specs=[pl.BlockSpec((None, bsz, CHUNK, KD), lambda d, j: (d, 0, blk(d, j), 0)),
                   pl.BlockSpec((None, bsz, CHUNK, KD), lambda d, j: (d, 0, blk(d, j), 0)),
                   pl.BlockSpec((None, bsz, CHUNK, VD), lambda d, j: (d, 0, blk(d, j), 0)),
                   pl.BlockSpec((bsz, CHUNK, KD), lambda d, j: (0, blk(d, j), d))],
        out_shape=[jax.ShapeDtypeStruct((2, bsz, TT, KD), BF16), jax.ShapeDtypeStruct((2, bsz, TT, KD), BF16),
                   jax.ShapeDtypeStruct((2, bsz, TT, VD), BF16), jax.ShapeDtypeStruct((bsz, TT, 2 * KD), F32)],
        scratch_shapes=[pltpu.VMEM((bsz, HEADS, HV, HK), F32)],
        compiler_params=pltpu.CompilerParams(dimension_semantics=("arbitrary", "arbitrary")),
    )(pcat, pcat, pcat, la, s_all, do)


def gla_combine(dq2, dk2, dv2, dgate, dpa):
    bsz = dgate.shape[0]
    tm = CTX

    def body(dq_ref, dk_ref, dv_ref, dg_ref, dpa_ref, o_ref):
        t = pl.program_id(1)
        o_ref[:, 0:KD] = (dq_ref[0].astype(F32) + dq_ref[1].astype(F32)).astype(BF16)
        o_ref[:, KD:2 * KD] = (dk_ref[0].astype(F32) + dk_ref[1].astype(F32)).astype(BF16)
        o_ref[:, 2 * KD:2 * KD + VD] = (dv_ref[0].astype(F32) + dv_ref[1].astype(F32)).astype(BF16)
        o_ref[:, 2 * KD + VD:2 * KD + 2 * VD] = jnp.where(t > 0, dg_ref[...], 0).astype(BF16)
        o_ref[:, 2 * KD + 2 * VD:] = dpa_ref[...].astype(BF16)

    return pl.pallas_call(
        body, name="gla_combine", grid=(bsz, TT // tm),
        in_specs=[pl.BlockSpec((2, None, tm, KD), lambda b, t: (0, b, t, 0)),
                  pl.BlockSpec((2, None, tm, KD), lambda b, t: (0, b, t, 0)),
                  pl.BlockSpec((2, None, tm, VD), lambda b, t: (0, b, t, 0)),
                  pl.BlockSpec((None, tm, VD), lambda b, t: (b, jnp.maximum(t - 1, 0), 0)),
                  pl.BlockSpec((None, tm, 128), lambda b, t: (b, t, 0))],
        out_specs=pl.BlockSpec((None, tm, GLA_IN_PAD), lambda b, t: (b, t, 0)),
        out_shape=jax.ShapeDtypeStruct((bsz, TT, GLA_IN_PAD), BF16),
        compiler_params=pltpu.CompilerParams(dimension_semantics=("arbitrary", "arbitrary")),
    )(dq2, dk2, dv2, dgate, dpa)


def final_loss(h1, fo, gate, gain, tgt):
    bsz, t_len, _ = h1.shape
    tm = 256

    def body(h_ref, f_ref, gate_ref, gain_ref, tgt_ref, loss_ref, dh_ref, df_ref, dgate_ref, dgain_ref):
        b, t = pl.program_id(0), pl.program_id(1)
        target = tgt_ref[...]

        def core(h, fo_, gate_, gain_):
            e = _rms(h + gate_ * fo_) * gain_ - target
            return jnp.sum(0.5 * jnp.sum(e * e, axis=-1, keepdims=True) / D, axis=0, keepdims=True)

        loss, vjp = jax.vjp(core, h_ref[...], f_ref[...], gate_ref[...], gain_ref[...])
        dh, df, dgate, dgain = vjp(jnp.ones((1, 1), F32))
        dh_ref[...] = dh
        df_ref[...] = df.astype(BF16)
        first = jnp.logical_and(b == 0, t == 0)

        @pl.when(first)
        def _():
            loss_ref[...] = jnp.broadcast_to(loss, loss_ref.shape)
            dgain_ref[...] = dgain

        @pl.when(jnp.logical_not(first))
        def _():
            loss_ref[...] += jnp.broadcast_to(loss, loss_ref.shape)
            dgain_ref[...] += dgain

        @pl.when(t == 0)
        def _():
            dgate_ref[...] = dgate

        @pl.when(t > 0)
        def _():
            dgate_ref[...] += dgate

    tile = pl.BlockSpec((None, tm, D), lambda b, t: (b, t, 0))
    per_ex = pl.BlockSpec((None, 1, D), lambda b, t: (b, 0, 0))
    shared = pl.BlockSpec((1, D), lambda b, t: (0, 0))
    return pl.pallas_call(
        body, name="final_loss", grid=(bsz, t_len // tm),
        in_specs=[tile, tile, per_ex, shared, tile],
        out_specs=[pl.BlockSpec((8, 128), lambda b, t: (0, 0)), tile, tile, per_ex, shared],
        out_shape=[jax.ShapeDtypeStruct((8, 128), F32), jax.ShapeDtypeStruct(h1.shape, F32),
                   jax.ShapeDtypeStruct(h1.shape, BF16), jax.ShapeDtypeStruct((bsz, 1, D), F32),
                   jax.ShapeDtypeStruct((1, D), F32)],
        compiler_params=pltpu.CompilerParams(dimension_semantics=("arbitrary", "arbitrary")),
    )(h1, fo, gate, gain, tgt)


ADA_ROWS = 24
ADA_CTX_ROW = 16
ADA_COLS = 6 * D // N_DEV


def ada_fwd(cond, w, b):
    def body(c_ref, w_ref, b_ref, o_ref):
        s = silu(c_ref[...]).astype(BF16)
        o_ref[...] = jnp.dot(s, w_ref[...].astype(BF16), preferred_element_type=F32) + b_ref[...]

    return pl.pallas_call(
        body, name="ada_fwd", grid=(2,),
        in_specs=[pl.BlockSpec((ADA_ROWS, D), lambda i: (0, 0)), pl.BlockSpec((None, D, ADA_COLS), lambda i: (i, 0, 0)),
                  pl.BlockSpec((None, 1, ADA_COLS), lambda i: (i, 0, 0))],
        out_specs=pl.BlockSpec((None, ADA_ROWS, ADA_COLS), lambda i: (i, 0, 0)),
        out_shape=jax.ShapeDtypeStruct((2, ADA_ROWS, ADA_COLS), F32),
    )(cond, w, b)


def ada_bwd(cond, dm_mine, dm_full, w):
    def body(c_ref, dm_ref, dmf_ref, w_ref, gw_ref, gb_ref, cp_ref):
        i = pl.program_id(0)
        s = silu(c_ref[...]).astype(BF16)
        dm = dm_ref[...].astype(BF16)
        gw_ref[...] = _dot(s, dm, _TN)
        gb_ref[...] = jnp.sum(dmf_ref[...], axis=0, keepdims=True)

        @pl.when(i == 0)
        def _():
            cp_ref[...] = _dot(dm_ref[ADA_CTX_ROW:, :].astype(BF16), w_ref[...].astype(BF16), _NT)

    return pl.pallas_call(
        body, name="ada_bwd", grid=(2,),
        in_specs=[pl.BlockSpec((ADA_ROWS, D), lambda i: (0, 0)), pl.BlockSpec((None, ADA_ROWS, ADA_COLS), lambda i: (i, 0, 0)),
                  pl.BlockSpec((None, ADA_ROWS, 6 * D), lambda i: (i, 0, 0)), pl.BlockSpec((None, D, ADA_COLS), lambda i: (i, 0, 0))],
        out_specs=[pl.BlockSpec((None, D, ADA_COLS), lambda i: (i, 0, 0)), pl.BlockSpec((None, 1, 6 * D), lambda i: (i, 0, 0)),
                   pl.BlockSpec((ADA_ROWS - ADA_CTX_ROW, D), lambda i: (0, 0))],
        out_shape=[jax.ShapeDtypeStruct((2, D, ADA_COLS), F32), jax.ShapeDtypeStruct((2, 1, 6 * D), F32),
                   jax.ShapeDtypeStruct((ADA_ROWS - ADA_CTX_ROW, D), F32)],
        compiler_params=pltpu.CompilerParams(dimension_semantics=("arbitrary",)),
    )(cond, dm_mine, dm_full, w)


def cctx_grad(parts, c_ctx):
    def body(p_ref, c_ref, o_ref):
        tot = p_ref[0:1, :]
        for i in range(1, N_DEV):
            tot = tot + p_ref[i:i + 1, :]
        c = c_ref[...]
        sg = jax.nn.sigmoid(c)
        o_ref[...] = tot * sg * (1.0 + c * (1.0 - sg))

    return pl.pallas_call(body, name="cctx_grad", out_shape=jax.ShapeDtypeStruct((1, D), F32))(parts, c_ctx)


def _row_tile(r):
    for t in (512, 256, 128, 80, 64, 40, 32, 16, 8):
        if r % t == 0:
            return t
    return r


def _slot_sum(ref):
    tot = ref[0].astype(F32)
    for i in range(1, ref.shape[0]):
        tot = tot + ref[i].astype(F32)
    return tot


def sum_slots(name, x):
    s, r, c = x.shape
    tr = _row_tile(r)

    def body(x_ref, o_ref):
        o_ref[...] = _slot_sum(x_ref)

    return pl.pallas_call(
        body, name=name, grid=(r // tr,), in_specs=[pl.BlockSpec((s, tr, c), lambda i: (0, i, 0))],
        out_specs=pl.BlockSpec((tr, c), lambda i: (i, 0)), out_shape=jax.ShapeDtypeStruct((r, c), F32),
    )(x)


def adamw(name, w, g, m, v, layer=None):
    r, c = w.shape[-2:]
    tr = _row_tile(r)
    stacked = g.ndim == 3

    def body(w_ref, g_ref, m_ref, v_ref, go_ref, d_ref, mo_ref, vo_ref):
        gv = _slot_sum(g_ref) if stacked else g_ref[...]
        mn = B1 * m_ref[...] + (1.0 - B1) * gv
        vn = B2 * v_ref[...] + (1.0 - B2) * jnp.square(gv)
        m_hat = mn / (1.0 - B1 ** STEP)
        v_hat = vn / (1.0 - B2 ** STEP)
        go_ref[...] = gv
        d_ref[...] = -LR * (m_hat / (jnp.sqrt(v_hat) + AEPS) + WD * w_ref[...])
        mo_ref[...] = mn
        vo_ref[...] = vn

    tile = pl.BlockSpec((tr, c), lambda i: (i, 0))
    slab = tile if layer is None else pl.BlockSpec((None, tr, c), lambda i: (layer, i, 0))
    g_spec = pl.BlockSpec((g.shape[0], tr, c), lambda i: (0, i, 0)) if stacked else tile
    return pl.pallas_call(
        body, name=name, grid=(r // tr,), in_specs=[slab, g_spec, slab, slab], out_specs=[tile] * 4,
        out_shape=[jax.ShapeDtypeStruct((r, c), F32)] * 4,
    )(w, g, m, v)


def _place():
    return lax.axis_index("x"), lax.axis_index("y"), lax.axis_index("c")


def all_gather(name, x, in_vmem):
    r, c = x.shape
    space = pltpu.VMEM if in_vmem else pl.ANY

    def body(x_ref, out_ref, send_sems, recv_sems, local_sem):
        px, py, pc = _place()
        me, sibling = (px, py, pc), (px, py, 1 - pc)
        chips = [(1 - px, py), (px, 1 - py), (1 - px, 1 - py)]

        def rows(qx, qy, qc):
            return out_ref.at[pl.ds((4 * qx + 2 * qy + qc) * r, r), :]

        def copy(k, block, to, src=None):
            return pltpu.make_async_remote_copy(
                src_ref=rows(*block) if src is None else src, dst_ref=rows(*block),
                send_sem=send_sems.at[k], recv_sem=recv_sems.at[k], device_id=to, device_id_type=MESH)

        mine = pltpu.make_async_copy(x_ref, rows(*me), local_sem)
        mine.start()
        first = [copy(0, me, sibling, src=x_ref)]
        first += [copy(1 + j, me, (*chip, pc), src=x_ref) for j, chip in enumerate(chips)]
        for cp in first:
            cp.start()
        passed = [copy(4 + j, (*chip, pc), sibling) for j, chip in enumerate(chips)]
        for j, chip in enumerate(chips):
            copy(1 + j, (*chip, pc), me).wait_recv()
            passed[j].start()
        copy(0, sibling, me).wait_recv()
        for j, chip in enumerate(chips):
            copy(4 + j, (*chip, 1 - pc), me).wait_recv()
        for cp in first + passed:
            cp.wait_send()
        mine.wait()

    return pl.pallas_call(
        body, name=name, out_shape=jax.ShapeDtypeStruct((N_DEV * r, c), x.dtype),
        in_specs=[pl.BlockSpec(memory_space=space)], out_specs=pl.BlockSpec(memory_space=space),
        scratch_shapes=[pltpu.SemaphoreType.DMA((7,)), pltpu.SemaphoreType.DMA((7,)), pltpu.SemaphoreType.DMA],
    )(x)


_HBM =pl.BlockSpec(memory_space=pltpu.HBM)
_SEM = pl.BlockSpec(memory_space=pltpu.SEMAPHORE)
_EFFECT = pltpu.SideEffectType.DATAFLOW_SIDE_EFFECTING


def _peers():
    px, py, pc = _place()
    return [(1 - px if k & 4 else px, 1 - py if k & 2 else py, 1 - pc if k & 1 else pc) for k in range(1, N_DEV)]


def _slot(dev):
    return 4 * dev[0] + 2 * dev[1] + dev[2]


def _split_copies(src_refs, land_refs, send_sems, recv_sems, gather):
    me = _slot(_place())
    return [pltpu.make_async_remote_copy(
        src_ref=src if gather else src.at[_slot(peer)], dst_ref=land.at[me],
        send_sem=send_sems.at[a * (N_DEV - 1) + k], recv_sem=recv_sems.at[a * (N_DEV - 1) + k],
        device_id=peer, device_id_type=MESH)
        for a, (src, land) in enumerate(zip(src_refs, land_refs)) for k, peer in enumerate(_peers())]


def exchange_start(name, srcs, gather, after):
    n = len(srcs)
    lands = [pltpu.HBM((N_DEV,) + s.shape if gather else s.shape, s.dtype) for s in srcs]

    def body(*refs):
        send_sems, recv_sems = refs[2 * n + 1:2 * n + 3]
        for cp in _split_copies(refs[:n], refs[n:2 * n], send_sems, recv_sems, gather):
            cp.start()
        refs[-1][...] = jnp.zeros_like(refs[-1])

    sems = pltpu.SemaphoreType.DMA((n * (N_DEV - 1),))
    res = pl.pallas_call(
        body, name=name,
        out_shape=(sems, sems, *[pltpu.HBM(s.shape, s.dtype) for s in srcs], *lands, jax.ShapeDtypeStruct((8, 128), F32)),
        in_specs=(_HBM,) * (2 * n) + (pl.BlockSpec(memory_space=pl.ANY),),
        out_specs=(_SEM, _SEM) + (_HBM,) * (2 * n) + (pl.BlockSpec(memory_space=pltpu.VMEM),),
        input_output_aliases={i: 2 + i for i in range(2 * n)},
        compiler_params=pltpu.CompilerParams(has_side_effects=_EFFECT),
    )(*[pltpu.with_memory_space_constraint(s, pltpu.HBM) for s in srcs],
      *[pltpu.with_memory_space_constraint(lax.empty(ld.shape, ld.dtype), pltpu.HBM) for ld in lands], after)
    return res[0], res[1], list(res[2:2 + n]), list(res[2 + n:2 + 2 * n]), res[-1]


def exchange_wait(name, started, after, gather):
    send_sems, recv_sems, srcs, lands, _ = started
    n = len(srcs)
    after = list(after) if isinstance(after, (list, tuple)) else [after]

    def body(*refs):
        send_sems, recv_sems = refs[2 * n:2 * n + 2]
        for cp in _split_copies(refs[:n], refs[n:2 * n], send_sems, recv_sems, gather):
            cp.wait_send()
            cp.wait_recv()

    res = pl.pallas_call(
        body, name=name, out_shape=tuple(pltpu.HBM(a.shape, a.dtype) for a in srcs + lands),
        in_specs=(_HBM,) * (2 * n) + (_SEM, _SEM) + (pl.BlockSpec(memory_space=pl.ANY),) * len(after),
        out_specs=(_HBM,) * (2 * n), input_output_aliases={i: i for i in range(2 * n)},
        compiler_params=pltpu.CompilerParams(has_side_effects=_EFFECT),
    )(*srcs, *lands, send_sems, recv_sems, *after)
    return list(res[:n]), list(res[n:])


NCF = FFN_H // FFN_TC


def _size(shape):
    n = 1
    for s in shape:
        n *= s
    return n


def _padded_rows(n_elems, row_mult):
    return -(-n_elems // (D * row_mult)) * row_mult


def _pack_rows(arrs, dtype, row_mult):
    rows, offs, r0 = [], [], 0
    for a in arrs:
        flat = a.reshape(-1).astype(dtype)
        n = _padded_rows(flat.shape[0], row_mult)
        rows.append(jnp.pad(flat, (0, n * D - flat.shape[0])).reshape(n, D))
        offs.append(r0)
        r0 += n
    return jnp.concatenate(rows, 0), offs


def _unpack_rows(buf, offs, shapes):
    lead, out = buf.shape[:-2], []
    for o, shp in zip(offs, shapes):
        n = _size(shp)
        nr = -(-n // D)
        out.append(buf[..., o:o + nr, :].reshape(lead + (nr * D,))[..., :n].reshape(lead + tuple(shp)))
    return out


def _rows3(w):
    return [w[i:i + 1] for i in range(3)]


def f_mod1(xs, ps):
    return f_mod(xs, ps)[:1]


def kernel(x, c, ctx, c_ctx, ada_w, ada_b, norm_mix, norm_ffn, gla_w_in, gla_w_a2, gla_b_a, gla_head_norm, gla_w_out, sc_w_in, sc_conv_w, sc_w_out, ffn_w_up, ffn_conv_w, ffn_conv_b, ffn_w_down, final_norm, loss_target, m_c_ctx, m_ada_w, m_ada_b, m_norm_mix, m_norm_ffn, m_gla_w_in, m_gla_w_a2, m_gla_b_a, m_gla_head_norm, m_gla_w_out, m_sc_w_in, m_sc_conv_w, m_sc_w_out, m_ffn_w_up, m_ffn_conv_w, m_ffn_conv_b, m_ffn_w_down, m_final_norm, v_c_ctx, v_ada_w, v_ada_b, v_norm_mix, v_norm_ffn, v_gla_w_in, v_gla_w_a2, v_gla_b_a, v_gla_head_norm, v_gla_w_out, v_sc_w_in, v_sc_conv_w, v_sc_w_out, v_ffn_w_up, v_ffn_conv_w, v_ffn_conv_b, v_ffn_w_down, v_final_norm):
    names = ["c_ctx", "ada_w", "ada_b", "norm_mix", "norm_ffn", "gla_w_in", "gla_w_a2", "gla_b_a", "gla_head_norm",
             "gla_w_out", "sc_w_in", "sc_conv_w", "sc_w_out", "ffn_w_up", "ffn_conv_w", "ffn_conv_b", "ffn_w_down",
             "final_norm"]
    w_ = dict(zip(names, [c_ctx, ada_w, ada_b, norm_mix, norm_ffn, gla_w_in, gla_w_a2, gla_b_a, gla_head_norm, gla_w_out,
                          sc_w_in, sc_conv_w, sc_w_out, ffn_w_up, ffn_conv_w, ffn_conv_b, ffn_w_down, final_norm]))
    m_ = dict(zip(names, [m_c_ctx, m_ada_w, m_ada_b, m_norm_mix, m_norm_ffn, m_gla_w_in, m_gla_w_a2, m_gla_b_a,
                          m_gla_head_norm, m_gla_w_out, m_sc_w_in, m_sc_conv_w, m_sc_w_out, m_ffn_w_up, m_ffn_conv_w,
                          m_ffn_conv_b, m_ffn_w_down, m_final_norm]))
    v_ = dict(zip(names, [v_c_ctx, v_ada_w, v_ada_b, v_norm_mix, v_norm_ffn, v_gla_w_in, v_gla_w_a2, v_gla_b_a,
                          v_gla_head_norm, v_gla_w_out, v_sc_w_in, v_sc_conv_w, v_sc_w_out, v_ffn_w_up, v_ffn_conv_w,
                          v_ffn_conv_b, v_ffn_w_down, v_final_norm]))
    me = 4 * lax.axis_index("x") + 2 * lax.axis_index("y") + lax.axis_index("c")
    bsz = x.shape[0]
    tm = 256
    nt = SEQ // tm
    ctx_tiles = CTX // tm
    pe = functools.partial(P, per_example=True)

    groups = {"ffn1": [("ffn_w_up", 1), ("ffn_w_down", 1)], "sc": [("sc_w_in", 0), ("sc_w_out", 0)],
              "ffn0": [("ffn_w_up", 0), ("ffn_w_down", 0)], "gla": [("gla_w_in", 0), ("gla_w_out", 0)]}
    ag_groups = {"gin": [("gla_w_in", 0)], "ffn0": [("gla_w_out", 0), ("ffn_w_up", 0), ("ffn_w_down", 0)],
                 "sc": groups["sc"], "ffn1": groups["ffn1"]}
    ag_started = {}

    def start_gather(g, after):
        ag_started[g] = exchange_start(f"ag_{g}_start", [w_[n][i].astype(BF16) for n, i in ag_groups[g]], True, after)
        return ag_started[g][4]

    small_sharded = [c, gla_w_a2, gla_b_a, sc_conv_w, ffn_conv_w]
    pack0, offs0 = _pack_rows(small_sharded, F32, 8)
    g0 = all_gather("ag_small", pack0, True).reshape(N_DEV, pack0.shape[0], D)
    c_all, wa2_s, ba_s, scw_s, fcw_s = _unpack_rows(g0, offs0, [a.shape for a in small_sharded])
    w_a2 = wa2_s[:, 0].transpose(1, 2, 0, 3).reshape(2, RANK, KD)
    b_a = ba_s[:, 0].transpose(1, 0, 2).reshape(2, KD)
    sc_cw = scw_s[:, 0].transpose(1, 0, 2).reshape(3, D)
    ffn_cw = fcw_s.transpose(1, 2, 0, 3).reshape(2, 3, 2 * FFN_H)

    cond = jnp.concatenate([c_all.reshape(N_DEV * bsz, D), c_ctx[None], jnp.zeros((ADA_ROWS - N_DEV * bsz - 1, D), F32)], 0)
    b_mine = lax.dynamic_slice(ada_b, (0, me * ADA_COLS), (2, ADA_COLS)).reshape(2, 1, ADA_COLS)
    mod_part = ada_fwd(cond, ada_w, b_mine)
    mod = all_gather("ag_mod", mod_part.reshape(2 * ADA_ROWS, ADA_COLS), True)
    mod = mod.reshape(N_DEV, 2, ADA_ROWS, ADA_COLS).transpose(1, 2, 0, 3).reshape(2, ADA_ROWS, 6 * D)
    mods = lax.dynamic_slice(mod, (0, bsz * me, 0), (2, bsz, 6 * D))
    md = [[mods[i][:, k * D:(k + 1) * D].reshape(bsz, 1, D) for k in range(6)] for i in range(2)]
    mc = [mod[0, ADA_CTX_ROW, k * D:(k + 1) * D][None] for k in range(2)]

    tok = mod
    for g in ag_groups:
        tok = start_gather(g, tok)
    norm_mix = norm_mix + tok[0, 0]

    def gathered(g, after):
        mine, lands = exchange_wait(f"ag_{g}_wait", ag_started[g], after, True)
        return [lax.dynamic_update_index_in_dim(ld, mn, me, 0) for ld, mn in zip(lands, mine)]

    s_up, w_down = [None, None], [None, None]
    wd = jnp.zeros((128, 2 * KD), F32).at[:RANK, :KD].set(w_a2[0]).at[RANK:2 * RANK, KD:].set(w_a2[1])
    bd = b_a.reshape(1, 2 * KD)
    scw = _rows3(sc_cw)
    head_gain = gla_head_norm.reshape(1, HV)
    gains_mix = [norm_mix[i][None] for i in range(2)]
    gains_ffn = [norm_ffn[i][None] for i in range(2)]

    def tokens(a2d, t_len):
        return a2d.reshape(bsz, t_len, -1)

    def ffn_params(i):
        rows = [ffn_cw[i][t] for t in range(3)] + [ffn_conv_b[i]]
        return [P(a.reshape(2, FFN_H), w=FFN_TC, rows=True) for a in rows]

    def ffn_fwd(i, hn2):
        u = mm(f"ffn_up{i}", V(hn2, "tok"), V(s_up[i], "cols"), out="planes", out_dtype=BF16, planes_t=SEQ)
        act = rowwise(f"ffn_mid{i}", f_ffn_mid, [X(u, w=FFN_TC, planes=True)], ffn_params(i), tm=SEQ, nt=1, nc=NCF,
                      outs=[(FFN_TC, BF16, 1)])[0]
        return u, act

    def arrays(ps):
        return [p["a"] for p in ps]

    ps_in0 = [P(gains_mix[0]), pe(md[0][0]), pe(md[0][1])]
    ps_ctx = [P(gains_mix[0]), P(mc[0]), P(mc[1])]
    hn0 = rowwise("mod_in0", f_mod, [X(x)], ps_in0, tm=tm, nt=nt, outs=[(D, BF16, 1)])[0]
    hnc = rowwise("mod_ctx", f_mod, [X(ctx)], ps_ctx, tm=tm, nt=ctx_tiles, outs=[(D, BF16, 1)])[0]
    hcat = jnp.concatenate([hnc, hn0], axis=1)
    (s_gin,) = gathered("gin", hcat)
    w_gin = V(s_gin, "cols", width=GLA_IN_PAD)
    pcat = tokens(mm("gla_in", V(hcat, "tok"), w_gin, out_dtype=BF16), TT)
    pa_x = X(pcat, w=128, co=(GLA_IN_PAD - 128) // 128)
    la = rowwise("gla_decay", f_decay, [pa_x], [P(wd), P(bd)], tm=tm, nt=TT // tm, outs=[(2 * KD, F32, 1)])[0]
    o2, s_all = gla_fwd(pcat, la)
    post_xs = [X(o2, w=VD, co=0, ro=ctx_tiles, split=HEADS), X(o2, w=VD, co=1, ro=ctx_tiles, split=HEADS),
               X(pcat, w=VD, co=2, ro=ctx_tiles, split=HEADS)]
    yin0 = rowwise("gla_post", f_gla_post, post_xs, [P(head_gain)], tm=tm, nt=nt, outs=[(VD, BF16, HEADS)])[0]
    s_gout, s_up[0], s_down0 = gathered("ffn0", yin0)
    w_gout, w_down[0] = s_gout.reshape(VD, D), s_down0.reshape(FFN_H, D)
    ps_mid0 = [pe(md[0][2]), P(gains_ffn[0]), pe(md[0][3]), pe(md[0][4])]
    y0, h1_0, hn2_0 = mm_res_mod("gla_out", yin0, w_gout, x, *arrays(ps_mid0))
    u0, act0 = ffn_fwd(0, hn2_0)
    ps_in1 = [pe(md[0][5]), P(gains_mix[1]), pe(md[1][0]), pe(md[1][1])]
    fo0, h2_0, hn1 = mm_res_mod("ffn_down0", act0, w_down[0], h1_0, *arrays(ps_in1))

    s_sin, s_sout = gathered("sc", hn1)
    w_sout = s_sout.reshape(D, D)
    p1 = tokens(mm("sc_in", V(hn1, "tok"), V(s_sin, "cols")), SEQ)
    sc_ps = [P(a) for a in scw]
    yin1 = rowwise("sc_mid", f_sc_mid, [X(p1, split=3)], sc_ps, tm=tm, nt=nt, outs=[(D, BF16, 1)])[0]
    ps_mid1 = [pe(md[1][2]), P(gains_ffn[1]), pe(md[1][3]), pe(md[1][4])]
    y1, h1_1, hn2_1 = mm_res_mod("sc_out", yin1, w_sout, h2_0, *arrays(ps_mid1))
    s_up[1], s_down1 = gathered("ffn1", hn2_1)
    w_down[1] = s_down1.reshape(FFN_H, D)
    u1, act1 = ffn_fwd(1, hn2_1)
    fo1 = tokens(mm("ffn_down1", V(act1, "tok"), V(w_down[1])), SEQ)
    loss8, dh1_1, dfo1, dm5_1, g_final = final_loss(h1_1, fo1, md[1][5], final_norm[None], loss_target)

    def ffn_bwd(i, u, act, hn2, dfo):
        dact = tokens(mm(f"ffn_down_dx{i}", V(dfo, "tok"), V(w_down[i]), form="nt", out_dtype=BF16), SEQ)
        g_down = mm(f"ffn_down_dw{i}", V(act, "tok"), V(dfo, "tok"), form="tn", out_dtype=BF16)
        r = rowwise(f"ffn_mid_bwd{i}", f_ffn_mid, [X(u, w=FFN_TC, planes=True)], ffn_params(i), tm=SEQ, nt=1, nc=NCF,
                    douts=[X(dact, w=FFN_TC)], dx={0: BF16}, dp=[0, 1, 2, 3])
        du, g_cw, g_cb = r[0], jnp.stack([a.reshape(2 * FFN_H) for a in r[1:4]]), r[4].reshape(1, 2 * FFN_H)
        dhn2 = tokens(mm(f"ffn_up_dx{i}", V(du, "planes"), V(s_up[i], "cols"), form="nt", out_dtype=BF16), SEQ)
        g_up = mm(f"ffn_up_dw{i}", V(hn2, "tok"), V(du, "planes"), form="tn", out="cols", out_dtype=BF16)
        return dhn2, g_up, row_slots(g_down), g_cw, g_cb

    def res_mod_bwd(name, h, y, ps, dh1, dhn):
        return rowwise(name, f_res_mod, [X(h), X(y)], ps, tm=tm, nt=nt, douts=[X(dh1), X(dhn)],
                       dx={0: F32, 1: BF16}, dp=[0, 1, 2, 3])

    def row_slots(g):
        return g.reshape(N_DEV, -1, g.shape[-1])

    a2a_started = {}

    def send_grads(g, slots, after=None):
        a2a_started[g] = exchange_start(f"a2a_{g}_start", list(slots), False, loss8 if after is None else after)
        return a2a_started[g][4][0, 0]

    def after_start(ps, tok):
        return [dict(ps[0], a=ps[0]["a"] + tok)] + ps[1:]

    dhn2_1, g_up1, g_down1, g_fcw1, g_fcb1 = ffn_bwd(1, u1, act1, hn2_1, dfo1)
    tok = send_grads("ffn1", [g_up1, g_down1])
    dh2_0, dy1, dm2_1, g_nffn1, dm3_1, dm4_1 = res_mod_bwd("res_mod_mid1_bwd", h2_0, y1, after_start(ps_mid1, tok), dh1_1, dhn2_1)
    dyin1 = tokens(mm("sc_out_dx", V(dy1, "tok"), V(w_sout), form="nt", out_dtype=BF16), SEQ)
    g_sout = row_slots(mm("sc_out_dw", V(yin1, "tok"), V(dy1, "tok"), form="tn", out_dtype=BF16))
    r = rowwise("sc_mid_bwd", f_sc_mid, [X(p1, split=3)], sc_ps, tm=tm, nt=nt, douts=[X(dyin1)], dx={0: BF16}, dp=[0, 1, 2])
    dp1, g_scw = r[0], jnp.concatenate(r[1:4], 0)
    dhn1 = tokens(mm("sc_in_dx", V(dp1, "tok"), V(s_sin, "cols"), form="nt", out_dtype=BF16), SEQ)
    g_sin = mm("sc_in_dw", V(hn1, "tok"), V(dp1, "tok"), form="tn", out="cols", out_dtype=BF16)
    tok = send_grads("sc", [g_sin, g_sout])
    dh1_0, dfo0, dm5_0, g_nmix1, dm0_1, dm1_1 = res_mod_bwd("res_mod_in1_bwd", h1_0, fo0, after_start(ps_in1, tok), dh2_0, dhn1)

    dhn2_0, g_up0, g_down0, g_fcw0, g_fcb0 = ffn_bwd(0, u0, act0, hn2_0, dfo0)
    tok = send_grads("ffn0", [g_up0, g_down0])
    dx_res, dy0, dm2_0, g_nffn0, dm3_0, dm4_0 = res_mod_bwd("res_mod_mid0_bwd", x, y0, after_start(ps_mid0, tok), dh1_0, dhn2_0)
    dyin0 = tokens(mm("gla_out_dx", V(dy0, "tok"), V(w_gout), form="nt", out_dtype=BF16), SEQ)
    g_gout = row_slots(mm("gla_out_dw", V(yin0, "tok"), V(dy0, "tok"), form="tn", out_dtype=BF16))
    do, dgate, g_head = rowwise("gla_post_bwd", f_gla_post, post_xs, [P(head_gain)], tm=tm, nt=nt,
                                douts=[X(dyin0, split=HEADS)], dx={0: BF16, 2: BF16}, dp=[0])
    dq2, dk2, dv2, dla = gla_bwd(pcat, la, s_all, do)
    dpa, g_wd, g_bd = rowwise("gla_decay_bwd", f_decay, [pa_x], [P(wd), P(bd)], tm=tm, nt=TT // tm, douts=[X(dla)],
                              dx={0: BF16}, dp=[0, 1])
    dpcat = gla_combine(dq2, dk2, dv2, dgate, dpa)
    dhcat = tokens(mm("gla_in_dx", V(dpcat, "tok"), w_gin, form="nt", out_dtype=BF16), TT)
    g_gin = mm("gla_in_dw", V(hcat, "tok"), V(dpcat, "tok"), form="tn", out="cols", out_dtype=BF16, shard_n=GLA_IN // N_DEV)
    grad_x, g_nmix0, dm0_0, dm1_0 = rowwise("mod_in0_bwd", f_mod, [X(x)], ps_in0, tm=tm, nt=nt,
                                            douts=[X(dhcat, ro=ctx_tiles), X(dx_res)], dx={0: F32}, dp=[0, 1, 2])
    g_nmix0c, dmc0, dmc1 = rowwise("mod_ctx_bwd", f_mod1, [X(ctx)], ps_ctx, tm=tm, nt=ctx_tiles, douts=[X(dhcat)],
                                   dx={}, dp=[0, 1, 2])

    zero_row = jnp.zeros((1, 4 * D), F32)
    dmod = [jnp.concatenate([jnp.concatenate([a.reshape(bsz, D) for a in dms], 1), ctx_row], 0)
            for dms, ctx_row in (([dm0_0, dm1_0, dm2_0, dm3_0, dm4_0, dm5_0], jnp.concatenate([dmc0, dmc1, zero_row], 1)),
                                 ([dm0_1, dm1_1, dm2_1, dm3_1, dm4_1, dm5_1], jnp.zeros((1, 6 * D), F32)))]
    g_wa2 = jnp.stack([g_wd[:RANK, :KD], g_wd[RANK:2 * RANK, KD:]])
    small_grads = [jnp.stack(dmod), jnp.concatenate([g_nmix0 + g_nmix0c, g_nmix1], 0), jnp.concatenate([g_nffn0, g_nffn1], 0),
                   g_head, jnp.concatenate([g_fcb0, g_fcb1], 0), g_final, g_wa2, g_bd.reshape(2, KD), g_scw,
                   jnp.stack([g_fcw0, g_fcw1]), loss8[:1]]
    pack1, offs1 = _pack_rows(small_grads, F32, 8)
    g1 = all_gather("ag_grads", pack1, True).reshape(N_DEV, pack1.shape[0], D)
    dmod_all = _unpack_rows(g1, offs1[:1], [small_grads[0].shape])[0]
    tot = _unpack_rows(sum_slots("sum_small", g1), offs1, [a.shape for a in small_grads])
    loss = tot[10][0, 0]
    dm_rows = dmod_all[:, :, :bsz].transpose(1, 0, 2, 3).reshape(2, N_DEV * bsz, 6 * D)
    dm_full = jnp.concatenate([dm_rows, tot[0][:, bsz:], jnp.zeros((2, ADA_ROWS - N_DEV * bsz - 1, 6 * D), F32)], 1)
    dm_mine = lax.dynamic_slice(dm_full, (0, 0, me * ADA_COLS), (2, ADA_ROWS, ADA_COLS))
    g_ada_w, g_ada_b, cpart = ada_bwd(cond, dm_mine, dm_full, ada_w)
    cparts = all_gather("ag_cctx", cpart, True).reshape(N_DEV, ADA_ROWS - ADA_CTX_ROW, D)[:, 0]
    g_cctx = cctx_grad(cparts, c_ctx[None])[0]
    tok = send_grads("gla", [g_gin, g_gout], after=g_cctx)

    def my_cols(full, n):
        return lax.dynamic_slice_in_dim(full, me * n, n, axis=full.ndim - 1)

    grads = {
        "c_ctx": g_cctx, "ada_b": g_ada_b.reshape(2, 6 * D), "norm_mix": tot[1], "norm_ffn": tot[2],
        "gla_head_norm": tot[3], "ffn_conv_b": tot[4], "final_norm": tot[5].reshape(D),
        "gla_w_a2": my_cols(tot[6], KD // N_DEV)[None], "gla_b_a": my_cols(tot[7], KD // N_DEV)[None],
        "sc_conv_w": my_cols(tot[8], D // N_DEV)[None], "ffn_conv_w": my_cols(tot[9], 2 * FFN_H // N_DEV),
    }

    res_ada = adamw("adamw_ada", *[a.reshape(2 * D, ADA_COLS) for a in (ada_w, g_ada_w, m_ada_w, v_ada_w)])
    grads["c_ctx"] = g_cctx + tok
    big = ["gla_w_in", "gla_w_out", "sc_w_in", "sc_w_out", "ffn_w_up", "ffn_w_down"]
    small = [n for n in names if n not in big and n != "ada_w"]
    g_small = _pack_rows([grads[n] for n in small], F32, 8)[0]
    res_small = adamw("adamw_small", _pack_rows([w_[n] for n in small], F32, 8)[0], g_small,
                      _pack_rows([m_[n] for n in small], F32, 8)[0], _pack_rows([v_[n] for n in small], F32, 8)[0])
    offs_s = _pack_rows([w_[n] for n in small], F32, 8)[1]

    big_res, done = {}, [res_small[0], res_ada[0]]
    for g in groups:
        sent, lands = exchange_wait(f"a2a_{g}_wait", a2a_started[g], done, False)
        for (n, i), mine, land in zip(groups[g], sent, lands):
            land = lax.dynamic_update_index_in_dim(land, lax.dynamic_index_in_dim(mine, me, 0, keepdims=False), me, 0)
            big_res[(n, i)] = adamw(f"adamw_{n}{i}", w_[n], land, m_[n], v_[n], layer=i)
            done.append(big_res[(n, i)][0])

    out = {}
    for kind, idx in (("grad", 0), ("delta", 1), ("new_m", 2), ("new_v", 3)):
        vals = {n: jnp.stack([big_res[(n, i)][idx] for i in range(w_[n].shape[0])]) for n in big}
        vals["ada_w"] = res_ada[idx].reshape(ada_w.shape)
        vals.update(zip(small, _unpack_rows(res_small[idx], offs_s, [w_[n].shape for n in small])))
        out[kind] = [vals[n] for n in names]
    return (loss, grad_x, *out["grad"], *out["delta"], *out["new_m"], *out["new_v"])
```

```python
import functools

import jax
import jax.numpy as jnp
from jax import lax
from jax.experimental import pallas as pl
from jax.experimental.pallas import tpu as pltpu

F32 = jnp.float32
BF16 = jnp.bfloat16

N_DEV = 8
D = 1024
SEQ = 2048
CTX = 256
TT = CTX + SEQ
GRID_W = 64
CHUNK = 64
HEADS = 4
HK = 128
HV = 256
KD = 512
VD = 1024
RANK = 16
TAU = 16.0
GLA_IN = 3104
GLA_IN_PAD = 3200
FFN_H = 2560
FFN_TC = 256
EPS = 1e-6
LR, B1, B2, AEPS, WD, STEP = 0.001, 0.9, 0.999, 1e-08, 0.01, 10
MESH = pl.DeviceIdType.MESH


def _blocks(n):
    return [n] + [t for t in range(n - n % 128, 0, -128) if n % t == 0 and t != n]


def V(arr, kind="flat", width=None):
    if kind == "tok":
        return V(arr.reshape(-1, arr.shape[-1]))
    if kind == "flat":
        r, c = arr.shape
        return dict(a=arr, kind=kind, shape=(r, c), rows=_blocks(r), cols=_blocks(c))
    if kind == "planes":
        bsz, _, t, ch = arr.shape
        return dict(a=arr, kind=kind, shape=(bsz * t, 2 * ch), rows=_blocks(t), cols=[2 * ch] + _blocks(ch), t=t, ch=ch)
    _, r, n = arr.shape
    if width is not None:
        return dict(a=arr, kind=kind, shape=(r, width), rows=_blocks(r), cols=[width], n=n, pad=width - N_DEV * n)
    return dict(a=arr, kind=kind, shape=(r, N_DEV * n), rows=_blocks(r), cols=[8 * n, 4 * n, 2 * n], n=n, pad=0)


def _view_spec(v, br, bc, idx):
    if v["kind"] == "flat":
        return pl.BlockSpec((br, bc), idx)
    if v["kind"] == "planes":
        nt = v["t"] // br
        if bc == 2 * v["ch"]:
            return pl.BlockSpec((None, 2, br, v["ch"]), lambda i, j, k: (idx(i, j, k)[0] // nt, 0, idx(i, j, k)[0] % nt, 0))
        nch = v["ch"] // bc

        def at(i, j, k):
            r, c = idx(i, j, k)
            return r // nt, c // nch, r % nt, c % nch
        return pl.BlockSpec((None, None, br, bc), at)
    return pl.BlockSpec(((bc - v["pad"]) // v["n"], br, v["n"]), lambda i, j, k: (idx(i, j, k)[1], idx(i, j, k)[0], 0))


def _out_view(kind, rows, cols, dtype, planes_t=None, shard_n=None):
    if kind == "flat":
        shape = (rows, cols)
    elif kind == "planes":
        shape = (rows // planes_t, 2, planes_t, cols // 2)
    elif shard_n is not None:
        return V(jax.ShapeDtypeStruct((N_DEV, rows, shard_n), dtype), kind, width=cols)
    else:
        shape = (N_DEV, rows, cols // N_DEV)
    return V(jax.ShapeDtypeStruct(shape, dtype), kind)


MM_VMEM_BUDGET = 40 * 2 ** 20
MM_VMEM_LIMIT = 56 * 2 ** 20
MM_MAX_TILE = 1536


def _mm_tiles(m, n, kk, ms, ns, ks, a_bytes, b_bytes, o_bytes):
    best = None
    for tk in ks:
        for tm in [t for t in ms if t <= MM_MAX_TILE] or ms:
            for tn in [t for t in ns if t <= MM_MAX_TILE] or ns:
                one_k = tk == kk
                need = 2 * (tm * tk * a_bytes + tk * tn * b_bytes + tm * tn * o_bytes) + (0 if one_k else tm * tn * 4)
                if need > MM_VMEM_BUDGET:
                    continue
                steps = (m // tm) * (n // tn) * (kk // tk)
                traffic = (m * kk * a_bytes * (1 if one_k else n // tn)
                           + kk * n * b_bytes * (1 if one_k and n == tn else m // tm) + m * n * o_bytes)
                fill = (tm * tk * a_bytes + tk * tn * b_bytes) / 2.5e12
                cost = max(2.0 * m * n * kk / (9e14 if one_k else 6.5e14), traffic / 2.5e12) + steps * 0.4e-6 + fill
                if best is None or cost < best[0]:
                    best = (cost, tm, tn, tk)
    return best[1:]


def mm(name, a, b, form="nn", out="flat", out_dtype=F32, planes_t=None, shard_n=None, after=None):
    (m, kk) = a["shape"][::-1] if form == "tn" else a["shape"]
    n = b["shape"][0] if form == "nt" else b["shape"][1]
    assert (b["shape"][1] if form == "nt" else b["shape"][0]) == kk, (name, a["shape"], b["shape"])
    o = _out_view(out, m, n, out_dtype, planes_t, shard_n)
    a_m, a_k = (a["cols"], a["rows"]) if form == "tn" else (a["rows"], a["cols"])
    b_k, b_n = (b["cols"], b["rows"]) if form == "nt" else (b["rows"], b["cols"])
    tm, tn, tk = _mm_tiles(m, n, kk, [t for t in a_m if t in o["rows"]], [t for t in b_n if t in o["cols"]],
                           [t for t in a_k if t in b_k], a["a"].dtype.itemsize, b["a"].dtype.itemsize,
                           jnp.dtype(out_dtype).itemsize)
    nk = kk // tk
    dn = (((0 if form == "tn" else 1,), (1 if form == "nt" else 0,)), ((), ()))

    def load(ref, v):
        if len(ref.shape) == 3:
            pieces = [ref[p].astype(BF16) for p in range(ref.shape[0])]
            if v.get("pad"):
                pieces.append(jnp.zeros(ref.shape[1:2] + (v["pad"],), BF16))
            return jnp.concatenate(pieces, axis=-1)
        return ref[...].astype(BF16)

    def store(o_ref, val):
        val = val.astype(out_dtype)
        if len(o_ref.shape) == 3:
            w = o_ref.shape[-1]
            for p in range(o_ref.shape[0]):
                o_ref[p] = val[:, p * w:(p + 1) * w]
        else:
            o_ref[...] = val

    def body(a_ref, b_ref, *rest):
        o_ref, acc = rest[0 if after is None else 1], rest[1 if after is None else 2:]
        if nk == 1:
            store(o_ref, lax.dot_general(load(a_ref, a), load(b_ref, b), dn, preferred_element_type=F32))
            return
        k, acc_ref = pl.program_id(2), acc[0]

        @pl.when(k == 0)
        def _():
            acc_ref[...] = jnp.zeros_like(acc_ref)

        acc_ref[...] += lax.dot_general(load(a_ref, a), load(b_ref, b), dn, preferred_element_type=F32)

        @pl.when(k == nk - 1)
        def _():
            store(o_ref, acc_ref[...])

    if form == "tn":
        a_spec = _view_spec(a, tk, tm, lambda i, j, k: (k, i))
    else:
        a_spec = _view_spec(a, tm, tk, lambda i, j, k: (i, k))
    if form == "nt":
        b_spec = _view_spec(b, tn, tk, lambda i, j, k: (j, k))
    else:
        b_spec = _view_spec(b, tk, tn, lambda i, j, k: (k, j))
    return pl.pallas_call(
        body, name=name, grid=(m // tm, n // tn, nk),
        in_specs=[a_spec, b_spec] + ([] if after is None else [pl.BlockSpec(memory_space=pl.ANY)]),
        out_specs=_view_spec(o, tm, tn, lambda i, j, k: (i, j)), out_shape=o["a"],
        scratch_shapes=[pltpu.VMEM((tm, tn), F32)] if nk > 1 else [],
        compiler_params=pltpu.CompilerParams(dimension_semantics=("parallel", "parallel", "arbitrary"),
                                             vmem_limit_bytes=MM_VMEM_LIMIT),
    )(a["a"], b["a"], *([] if after is None else [after]))


def mm_res_mod(name, a, w, h, gate, gain, shift, scale):
    bsz, t_len, kk = a.shape
    tm = 512
    per = t_len // tm

    def body(a_ref, w_ref, h_ref, gate_ref, gain_ref, shift_ref, scale_ref, y_ref, h1_ref, hn_ref):
        y = jnp.dot(a_ref[...].astype(BF16), w_ref[...].astype(BF16), preferred_element_type=F32)
        h1 = h_ref[...] + gate_ref[...] * y
        y_ref[...] = y.astype(BF16)
        h1_ref[...] = h1
        hn_ref[...] = _mod(h1, gain_ref[...], shift_ref[...], scale_ref[...]).astype(BF16)

    def tile(width):
        return pl.BlockSpec((None, tm, width), lambda i: (i // per, i % per, 0))

    per_ex = pl.BlockSpec((None, 1, D), lambda i: (i // per, 0, 0))
    return pl.pallas_call(
        body, name=name, grid=(bsz * per,),
        in_specs=[tile(kk), pl.BlockSpec((kk, D), lambda i: (0, 0)), tile(D), per_ex, pl.BlockSpec((1, D), lambda i: (0, 0)),
                  per_ex, per_ex],
        out_specs=[tile(D)] * 3,
        out_shape=[jax.ShapeDtypeStruct((bsz, t_len, D), BF16), jax.ShapeDtypeStruct((bsz, t_len, D), F32),
                   jax.ShapeDtypeStruct((bsz, t_len, D), BF16)],
        compiler_params=pltpu.CompilerParams(dimension_semantics=("parallel",), vmem_limit_bytes=MM_VMEM_LIMIT),
    )(a, w, h, gate, gain, shift, scale)


def X(arr, w=None, co=0, ro=0, split=1, planes=False):
    return dict(a=arr, w=arr.shape[-1] if w is None else w, co=co, ro=ro, split=2 if planes else split,
                mode="planes" if planes else "cols")


def P(arr, per_example=False, w=None, split=1, rows=False):
    return dict(a=arr, e=per_example, w=arr.shape[-1] if w is None else w, split=arr.shape[-2] if rows else split,
                mode="rows" if rows else "cols")


def _pieces(ref, s):
    if s["mode"] == "planes":
        return [ref[0], ref[1]]
    if s["mode"] == "rows":
        return [ref[i:i + 1, :] for i in range(s["split"])]
    w = ref.shape[-1] // s["split"]
    return [ref[:, i * w:(i + 1) * w] for i in range(s["split"])]


def _store(ref, pieces, s, accumulate=False):
    w = ref.shape[-1] // len(pieces)
    for i, p in enumerate(pieces):
        at = (i,) if s["mode"] == "planes" else (slice(i, i + 1),) if s["mode"] == "rows" else (slice(None), slice(i * w, (i + 1) * w))
        if accumulate:
            ref[at] += p.astype(ref.dtype)
        else:
            ref[at] = p.astype(ref.dtype)


def rowwise(name, f, xs, ps, *, tm, nt, nc=1, outs=None, douts=None, dx=None, dp=None):
    bsz = xs[0]["a"].shape[0]
    fwd = douts is None
    nx, np_ = len(xs), len(ps)
    douts = [] if fwd else douts
    dx = {} if fwd else dx
    dp = [] if fwd else dp

    def x_spec(s):
        if s["mode"] == "planes":
            return pl.BlockSpec((None, 2, tm, s["w"]), lambda c, b, t, s=s: (b, 0, t + s["ro"], c + s["co"]))
        return pl.BlockSpec((None, tm, s["w"]), lambda c, b, t, s=s: (b, t + s["ro"], c + s["co"]))

    def x_out(s, dt):
        if s["mode"] == "planes":
            return (jax.ShapeDtypeStruct((bsz, 2, nt * tm, nc * s["w"]), dt),
                    pl.BlockSpec((None, 2, tm, s["w"]), lambda c, b, t: (b, 0, t, c)))
        return (jax.ShapeDtypeStruct((bsz, nt * tm, nc * s["w"]), dt), pl.BlockSpec((None, tm, s["w"]), lambda c, b, t: (b, t, c)))

    def p_spec(s):
        r = s["a"].shape[-2]
        if s["e"]:
            return pl.BlockSpec((None, r, s["w"]), lambda c, b, t: (b, 0, c))
        return pl.BlockSpec((r, s["w"]), lambda c, b, t: (0, c))

    in_specs = [x_spec(s) for s in xs] + [p_spec(s) for s in ps] + [x_spec(s) for s in douts]
    operands = [s["a"] for s in xs] + [s["a"] for s in ps] + [s["a"] for s in douts]
    if fwd:
        out_modes = [dict(mode="cols", split=sp) for (_, _, sp) in outs]
        out_shape = [jax.ShapeDtypeStruct((bsz, nt * tm, nc * w), dt) for (w, dt, _) in outs]
        out_specs = [pl.BlockSpec((None, tm, w), lambda c, b, t: (b, t, c)) for (w, _, _) in outs]
    else:
        dx_outs = [x_out(xs[i], dt) for i, dt in dx.items()]
        out_shape, out_specs = [o[0] for o in dx_outs], [o[1] for o in dx_outs]
        for j in dp:
            s = ps[j]
            r = s["a"].shape[-2]
            if s["e"]:
                out_shape.append(jax.ShapeDtypeStruct((bsz, r, nc * s["w"]), F32))
                out_specs.append(pl.BlockSpec((None, r, s["w"]), lambda c, b, t: (b, 0, c)))
            else:
                out_shape.append(jax.ShapeDtypeStruct((r, nc * s["w"]), F32))
                out_specs.append(pl.BlockSpec((r, s["w"]), lambda c, b, t: (0, c)))

    def body(*refs):
        x_refs, p_refs = refs[:nx], refs[nx:nx + np_]
        d_refs = refs[nx + np_:nx + np_ + len(douts)]
        o_refs = refs[nx + np_ + len(douts):]
        xv = [[p.astype(F32) for p in _pieces(r, s)] for r, s in zip(x_refs, xs)]
        pv = [[p.astype(F32) for p in _pieces(r, s)] for r, s in zip(p_refs, ps)]
        if fwd:
            for r, pieces, s in zip(o_refs, f(xv, pv), out_modes):
                _store(r, pieces, s)
            return
        _, vjp = jax.vjp(f, xv, pv)
        cot = [[p.astype(F32) for p in _pieces(r, s)] for r, s in zip(d_refs, douts)]
        dxv, dpv = vjp(cot)
        for r, i in zip(o_refs, dx):
            _store(r, dxv[i], xs[i])
        b, t = pl.program_id(1), pl.program_id(2)
        for r, j in zip(o_refs[len(dx):], dp):
            first = (t == 0) if ps[j]["e"] else jnp.logical_and(b == 0, t == 0)

            @pl.when(first)
            def _(r=r, j=j):
                _store(r, dpv[j], ps[j])

            @pl.when(jnp.logical_not(first))
            def _(r=r, j=j):
                _store(r, dpv[j], ps[j], accumulate=True)

    res = pl.pallas_call(
        body, name=name, grid=(nc, bsz, nt), in_specs=in_specs, out_specs=out_specs, out_shape=out_shape,
        compiler_params=pltpu.CompilerParams(dimension_semantics=("arbitrary", "arbitrary", "arbitrary")),
    )(*operands)
    return res


def _keep_rows(a, shift, keep):
    n = a.shape[0]
    t = lax.broadcasted_iota(jnp.int32, a.shape, 0)
    return jnp.where(keep(t, n), pltpu.roll(a, shift % n, 0), 0.0)


def _shift_pair(step, keep_prev, keep_next):
    @jax.custom_vjp
    def prev(a):
        return _keep_rows(a, step, keep_prev)

    @jax.custom_vjp
    def nxt(a):
        return _keep_rows(a, -step, keep_next)

    prev.defvjp(lambda a: (prev(a), None), lambda _, g: (nxt(g),))
    nxt.defvjp(lambda a: (nxt(a), None), lambda _, g: (prev(g),))
    return prev, nxt


prev_tok, next_tok = _shift_pair(1, lambda t, n: t % GRID_W != 0, lambda t, n: t % GRID_W != GRID_W - 1)
prev_row, next_row = _shift_pair(GRID_W, lambda t, n: t >= GRID_W, lambda t, n: t < n - GRID_W)


@jax.custom_vjp
def bdot(a, w):
    return jnp.dot(a.astype(BF16), w.astype(BF16), preferred_element_type=F32)


def _bdot_bwd(res, g):
    a, w = res
    gb = g.astype(BF16)
    da = lax.dot_general(gb, w.astype(BF16), (((1,), (1,)), ((), ())), preferred_element_type=F32)
    dw = lax.dot_general(a.astype(BF16), gb, (((0,), (0,)), ((), ())), preferred_element_type=F32)
    return da, dw


bdot.defvjp(lambda a, w: (bdot(a, w), (a, w)), _bdot_bwd)


@jax.custom_vjp
def log_sigmoid(z):
    return jnp.minimum(z, 0.0) - jnp.log(1.0 + jnp.exp(-jnp.abs(z)))


def _lsig_bwd(z, g):
    e = jnp.exp(-jnp.abs(z))
    return (g * jnp.where(z >= 0, e, 1.0) / (1.0 + e),)


log_sigmoid.defvjp(lambda z: (log_sigmoid(z), z), _lsig_bwd)


def silu(x):
    return x * jax.nn.sigmoid(x)


def _rms(x):
    return x * lax.rsqrt(jnp.mean(x * x, axis=-1, keepdims=True) + EPS)


def _mod(x, gain, shift, scale):
    return _rms(x) * gain * (1.0 + scale) + shift


def f_mod(xs, ps):
    ((h,),), ((gain,), (shift,), (scale,)) = xs, ps
    return [[_mod(h, gain, shift, scale)], [h]]


def f_res_mod(xs, ps):
    ((h,), (y,)), ((gate,), (gain,), (shift,), (scale,)) = xs, ps
    h1 = h + gate * y
    return [[h1], [_mod(h1, gain, shift, scale)]]


def f_ffn_mid(xs, ps):
    ((ua, ug),), ((w0a, w0g), (w1a, w1g), (w2a, w2g), (ba, bg)) = xs, ps
    a = w0a * prev_row(ua) + w1a * ua + w2a * next_row(ua) + ba
    g = w0g * prev_row(ug) + w1g * ug + w2g * next_row(ug) + bg
    return [[a * silu(g)]]


def f_sc_mid(xs, ps):
    ((bg, cg, v),), ((w0,), (w1,), (w2,)) = xs, ps
    z = cg * v
    return [[bg * (w0 * prev_tok(z) + w1 * z + w2 * next_tok(z))]]


def f_decay(xs, ps):
    ((a,),), ((wd,), (bd,)) = xs, ps
    return [[log_sigmoid(bdot(a, wd) + bd) / TAU]]


def f_gla_post(xs, ps):
    (of, ob, g), ((gain,),) = xs, ps
    return [[_rms(a + b) * gain * silu(c) for a, b, c in zip(of, ob, g)]]


NCH = TT // CHUNK
CTX_CH = CTX // CHUNK
_NT = (((1,), (1,)), ((), ()))
_TN = (((0,), (0,)), ((), ()))
_NN = (((1,), (0,)), ((), ()))


def _chunk_of(d, j):
    return jnp.where(d == 0, j, jnp.where(j < CTX_CH, CTX_CH - 1 - j, NCH + CTX_CH - 1 - j))


def _dot(a, b, dn):
    return lax.dot_general(a, b, dn, preferred_element_type=F32)


def _cumsum_rows(g, suffix):
    n = g.shape[0]
    row = lax.broadcasted_iota(jnp.int32, g.shape, 0)
    s = 1
    while s < n:
        if suffix:
            g = g + jnp.where(row < n - s, pltpu.roll(g, n - s, 0), 0.0)
        else:
            g = g + jnp.where(row >= s, pltpu.roll(g, s, 0), 0.0)
        s *= 2
    return g


def _causal(backward):
    row = lax.broadcasted_iota(jnp.int32, (CHUNK, CHUNK), 0)
    col = lax.broadcasted_iota(jnp.int32, (CHUNK, CHUNK), 1)
    return col >= row if backward else col <= row


def _gla_in_specs(bsz, rev):
    def blk(d, j):
        return _chunk_of(d, (NCH - 1 - j) if rev else j)

    return [
        pl.BlockSpec((bsz, CHUNK, KD), lambda d, j: (0, blk(d, j), 0)),
        pl.BlockSpec((bsz, CHUNK, KD), lambda d, j: (0, blk(d, j), 1)),
        pl.BlockSpec((bsz, CHUNK, VD), lambda d, j: (0, blk(d, j), 1)),
        pl.BlockSpec((bsz, CHUNK, KD), lambda d, j: (0, blk(d, j), d)),
    ], blk


def gla_fwd(pcat, la):
    bsz = pcat.shape[0]
    in_specs, blk = _gla_in_specs(bsz, False)

    def body(q_ref, k_ref, v_ref, la_ref, o_ref, s_ref, st):
        d, j = pl.program_id(0), pl.program_id(1)

        @pl.when(j == 0)
        def _():
            st[...] = jnp.zeros_like(st)

        s_ref[...] = st[...]

        def scan(backward):
            causal = _causal(backward)
            for e in range(bsz):
                g_all = la_ref[e]
                b_all = _cumsum_rows(g_all, backward)
                bl_all = jnp.sum(g_all, axis=0, keepdims=True)
                qs_all = (q_ref[e].astype(F32) * (HK ** -0.5) * jnp.exp(b_all)).astype(BF16)
                ks_all = (k_ref[e] * jnp.exp(-b_all)).astype(BF16)
                kd_all = (k_ref[e] * jnp.exp(bl_all - b_all)).astype(BF16)
                el_all = jnp.exp(bl_all)
                for h in range(HEADS):
                    ks_, vs_ = slice(h * HK, (h + 1) * HK), slice(h * HV, (h + 1) * HV)
                    qs, ks, kd, v = qs_all[:, ks_], ks_all[:, ks_], kd_all[:, ks_], v_ref[e, :, vs_].astype(BF16)
                    s = st[e, h]
                    att = jnp.where(causal, _dot(qs, ks, _NT), 0.0).astype(BF16)
                    o_ref[e, :, vs_] = _dot(qs, s.astype(BF16), _NT) + _dot(att, v, _NN)
                    st[e, h] = el_all[:, ks_] * s + _dot(v, kd, _TN)

        @pl.when(d == 0)
        def _():
            scan(False)

        @pl.when(d == 1)
        def _():
            scan(True)

    return pl.pallas_call(
        body, name="gla_fwd", grid=(2, NCH), in_specs=in_specs,
        out_specs=[pl.BlockSpec((bsz, CHUNK, VD), lambda d, j: (0, blk(d, j), d)),
                   pl.BlockSpec((bsz, None, None, HEADS, HV, HK), lambda d, j: (0, d, j, 0, 0, 0))],
        out_shape=[jax.ShapeDtypeStruct((bsz, TT, 2 * VD), F32), jax.ShapeDtypeStruct((bsz, 2, NCH, HEADS, HV, HK), F32)],
        scratch_shapes=[pltpu.VMEM((bsz, HEADS, HV, HK), F32)],
        compiler_params=pltpu.CompilerParams(dimension_semantics=("arbitrary", "arbitrary")),
    )(pcat, pcat, pcat, la)


def gla_bwd(pcat, la, s_all, do):
    bsz = pcat.shape[0]
    in_specs, blk = _gla_in_specs(bsz, True)
    in_specs += [
        pl.BlockSpec((bsz, None, None, HEADS, HV, HK), lambda d, j: (0, d, NCH - 1 - j, 0, 0, 0)),
        pl.BlockSpec((bsz, CHUNK, VD), lambda d, j: (0, jnp.maximum(blk(d, j) - CTX_CH, 0), 0)),
    ]

    def body(q_ref, k_ref, v_ref, la_ref, s_ref, do_ref, dq_ref, dk_ref, dv_ref, dla_ref, dst):
        d, j = pl.program_id(0), pl.program_id(1)

        @pl.when(j == 0)
        def _():
            dst[...] = jnp.zeros_like(dst)

        latent = blk(d, j) >= CTX_CH
        scale = HK ** -0.5

        def scan(backward):
            causal = _causal(backward)
            for e in range(bsz):
                g_all = la_ref[e]
                b_all = _cumsum_rows(g_all, backward)
                bl_all = jnp.sum(g_all, axis=0, keepdims=True)
                ex_all, ei_all, ed_all, el_all = jnp.exp(b_all), jnp.exp(-b_all), jnp.exp(bl_all - b_all), jnp.exp(bl_all)
                qs_all, ks_all, kd_all = q_ref[e].astype(F32) * scale * ex_all, k_ref[e] * ei_all, k_ref[e] * ed_all
                qsb_all, ksb_all, kdb_all = qs_all.astype(BF16), ks_all.astype(BF16), kd_all.astype(BF16)
                db_parts, dbl_parts = [], []
                for h in range(HEADS):
                    ks_, vs_ = slice(h * HK, (h + 1) * HK), slice(h * HV, (h + 1) * HV)
                    qs, ks, kd, el = qs_all[:, ks_], ks_all[:, ks_], kd_all[:, ks_], el_all[:, ks_]
                    qsb, ksb, kdb, v = qsb_all[:, ks_], ksb_all[:, ks_], kdb_all[:, ks_], v_ref[e, :, vs_].astype(BF16)
                    s, ds1 = s_ref[e, h], dst[e, h]
                    sb, ds1b = s.astype(BF16), ds1.astype(BF16)
                    dob = jnp.where(latent, do_ref[e, :, vs_], 0.0).astype(BF16)
                    att = jnp.where(causal, _dot(qsb, ksb, _NT), 0.0).astype(BF16)
                    datt = jnp.where(causal, _dot(dob, v, _NT), 0.0).astype(BF16)
                    dqs = _dot(dob, sb, _NN) + _dot(datt, ksb, _NN)
                    dks = _dot(datt, qsb, _TN)
                    dv_ref[e, :, vs_] = (_dot(att, dob, _TN) + _dot(kdb, ds1b, _NT)).astype(BF16)
                    dkd = _dot(v, ds1b, _NN)
                    dst[e, h] = _dot(dob, qsb, _TN) + el * ds1
                    del_ = jnp.sum(s * ds1, axis=0, keepdims=True)
                    dq_ref[e, :, ks_] = (dqs * ex_all[:, ks_] * scale).astype(BF16)
                    dk_ref[e, :, ks_] = (dks * ei_all[:, ks_] + dkd * ed_all[:, ks_]).astype(BF16)
                    db_parts.append(dqs * qs - dks * ks - dkd * kd)
                    dbl_parts.append(jnp.sum(dkd * kd, axis=0, keepdims=True) + del_ * el)
                dla_ref[e] = _cumsum_rows(jnp.concatenate(db_parts, -1), not backward) + jnp.concatenate(dbl_parts, -1)

        @pl.when(d == 0)
        def _():
            scan(False)

        @pl.when(d == 1)
        def _():
            scan(True)

    return pl.pallas_call(
        body, name="gla_bwd", grid=(2, NCH), in_specs=in_specs,
        out_specs=[pl.BlockSpec((None, bsz, CHUNK, KD), lambda d, j: (d, 0, blk(d, j), 0)),
                   pl.BlockSpec((None, bsz, CHUNK, KD), lambda d, j: (d, 0, blk(d, j), 0)),
                   pl.BlockSpec((None, bsz, CHUNK, VD), lambda d, j: (d, 0, blk(d, j), 0)),
                   pl.BlockSpec((bsz, CHUNK, KD), lambda d, j: (0, blk(d, j), d))],
        out_shape=[jax.ShapeDtypeStruct((2, bsz, TT, KD), BF16), jax.ShapeDtypeStruct((2, bsz, TT, KD), BF16),
                   jax.ShapeDtypeStruct((2, bsz, TT, VD), BF16), jax.ShapeDtypeStruct((bsz, TT, 2 * KD), F32)],
        scratch_shapes=[pltpu.VMEM((bsz, HEADS, HV, HK), F32)],
        compiler_params=pltpu.CompilerParams(dimension_semantics=("arbitrary", "arbitrary")),
    )(pcat, pcat, pcat, la, s_all, do)


def gla_combine(dq2, dk2, dv2, dgate, dpa):
    bsz = dgate.shape[0]
    tm = CTX

    def body(dq_ref, dk_ref, dv_ref, dg_ref, dpa_ref, o_ref):
        t = pl.program_id(1)
        o_ref[:, 0:KD] = (dq_ref[0].astype(F32) + dq_ref[1].astype(F32)).astype(BF16)
        o_ref[:, KD:2 * KD] = (dk_ref[0].astype(F32) + dk_ref[1].astype(F32)).astype(BF16)
        o_ref[:, 2 * KD:2 * KD + VD] = (dv_ref[0].astype(F32) + dv_ref[1].astype(F32)).astype(BF16)
        o_ref[:, 2 * KD + VD:2 * KD + 2 * VD] = jnp.where(t > 0, dg_ref[...], 0).astype(BF16)
        o_ref[:, 2 * KD + 2 * VD:] = dpa_ref[...].astype(BF16)

    return pl.pallas_call(
        body, name="gla_combine", grid=(bsz, TT // tm),
        in_specs=[pl.BlockSpec((2, None, tm, KD), lambda b, t: (0, b, t, 0)),
                  pl.BlockSpec((2, None, tm, KD), lambda b, t: (0, b, t, 0)),
                  pl.BlockSpec((2, None, tm, VD), lambda b, t: (0, b, t, 0)),
                  pl.BlockSpec((None, tm, VD), lambda b, t: (b, jnp.maximum(t - 1, 0), 0)),
                  pl.BlockSpec((None, tm, 128), lambda b, t: (b, t, 0))],
        out_specs=pl.BlockSpec((None, tm, GLA_IN_PAD), lambda b, t: (b, t, 0)),
        out_shape=jax.ShapeDtypeStruct((bsz, TT, GLA_IN_PAD), BF16),
        compiler_params=pltpu.CompilerParams(dimension_semantics=("arbitrary", "arbitrary")),
    )(dq2, dk2, dv2, dgate, dpa)


def final_loss(h1, fo, gate, gain, tgt):
    bsz, t_len, _ = h1.shape
    tm = 256

    def body(h_ref, f_ref, gate_ref, gain_ref, tgt_ref, loss_ref, dh_ref, df_ref, dgate_ref, dgain_ref):
        b, t = pl.program_id(0), pl.program_id(1)
        target = tgt_ref[...]

        def core(h, fo_, gate_, gain_):
            e = _rms(h + gate_ * fo_) * gain_ - target
            return jnp.sum(0.5 * jnp.sum(e * e, axis=-1, keepdims=True) / D, axis=0, keepdims=True)

        loss, vjp = jax.vjp(core, h_ref[...], f_ref[...], gate_ref[...], gain_ref[...])
        dh, df, dgate, dgain = vjp(jnp.ones((1, 1), F32))
        dh_ref[...] = dh
        df_ref[...] = df.astype(BF16)
        first = jnp.logical_and(b == 0, t == 0)

        @pl.when(first)
        def _():
            loss_ref[...] = jnp.broadcast_to(loss, loss_ref.shape)
            dgain_ref[...] = dgain

        @pl.when(jnp.logical_not(first))
        def _():
            loss_ref[...] += jnp.broadcast_to(loss, loss_ref.shape)
            dgain_ref[...] += dgain

        @pl.when(t == 0)
        def _():
            dgate_ref[...] = dgate

        @pl.when(t > 0)
        def _():
            dgate_ref[...] += dgate

    tile = pl.BlockSpec((None, tm, D), lambda b, t: (b, t, 0))
    per_ex = pl.BlockSpec((None, 1, D), lambda b, t: (b, 0, 0))
    shared = pl.BlockSpec((1, D), lambda b, t: (0, 0))
    return pl.pallas_call(
        body, name="final_loss", grid=(bsz, t_len // tm),
        in_specs=[tile, tile, per_ex, shared, tile],
        out_specs=[pl.BlockSpec((8, 128), lambda b, t: (0, 0)), tile, tile, per_ex, shared],
        out_shape=[jax.ShapeDtypeStruct((8, 128), F32), jax.ShapeDtypeStruct(h1.shape, F32),
                   jax.ShapeDtypeStruct(h1.shape, BF16), jax.ShapeDtypeStruct((bsz, 1, D), F32),
                   jax.ShapeDtypeStruct((1, D), F32)],
        compiler_params=pltpu.CompilerParams(dimension_semantics=("arbitrary", "arbitrary")),
    )(h1, fo, gate, gain, tgt)


ADA_ROWS = 24
ADA_CTX_ROW = 16
ADA_COLS = 6 * D // N_DEV


def ada_fwd(cond, w, b):
    def body(c_ref, w_ref, b_ref, o_ref):
        s = silu(c_ref[...]).astype(BF16)
        o_ref[...] = jnp.dot(s, w_ref[...].astype(BF16), preferred_element_type=F32) + b_ref[...]

    return pl.pallas_call(
        body, name="ada_fwd", grid=(2,),
        in_specs=[pl.BlockSpec((ADA_ROWS, D), lambda i: (0, 0)), pl.BlockSpec((None, D, ADA_COLS), lambda i: (i, 0, 0)),
                  pl.BlockSpec((None, 1, ADA_COLS), lambda i: (i, 0, 0))],
        out_specs=pl.BlockSpec((None, ADA_ROWS, ADA_COLS), lambda i: (i, 0, 0)),
        out_shape=jax.ShapeDtypeStruct((2, ADA_ROWS, ADA_COLS), F32),
    )(cond, w, b)


def ada_bwd(cond, dm_mine, dm_full, w):
    def body(c_ref, dm_ref, dmf_ref, w_ref, gw_ref, gb_ref, cp_ref):
        i = pl.program_id(0)
        s = silu(c_ref[...]).astype(BF16)
        dm = dm_ref[...].astype(BF16)
        gw_ref[...] = _dot(s, dm, _TN)
        gb_ref[...] = jnp.sum(dmf_ref[...], axis=0, keepdims=True)

        @pl.when(i == 0)
        def _():
            cp_ref[...] = _dot(dm_ref[ADA_CTX_ROW:, :].astype(BF16), w_ref[...].astype(BF16), _NT)

    return pl.pallas_call(
        body, name="ada_bwd", grid=(2,),
        in_specs=[pl.BlockSpec((ADA_ROWS, D), lambda i: (0, 0)), pl.BlockSpec((None, ADA_ROWS, ADA_COLS), lambda i: (i, 0, 0)),
                  pl.BlockSpec((None, ADA_ROWS, 6 * D), lambda i: (i, 0, 0)), pl.BlockSpec((None, D, ADA_COLS), lambda i: (i, 0, 0))],
        out_specs=[pl.BlockSpec((None, D, ADA_COLS), lambda i: (i, 0, 0)), pl.BlockSpec((None, 1, 6 * D), lambda i: (i, 0, 0)),
                   pl.BlockSpec((ADA_ROWS - ADA_CTX_ROW, D), lambda i: (0, 0))],
        out_shape=[jax.ShapeDtypeStruct((2, D, ADA_COLS), F32), jax.ShapeDtypeStruct((2, 1, 6 * D), F32),
                   jax.ShapeDtypeStruct((ADA_ROWS - ADA_CTX_ROW, D), F32)],
        compiler_params=pltpu.CompilerParams(dimension_semantics=("arbitrary",)),
    )(cond, dm_mine, dm_full, w)


def cctx_grad(parts, c_ctx):
    def body(p_ref, c_ref, o_ref):
        tot = p_ref[0:1, :]
        for i in range(1, N_DEV):
            tot = tot + p_ref[i:i + 1, :]
        c = c_ref[...]
        sg = jax.nn.sigmoid(c)
        o_ref[...] = tot * sg * (1.0 + c * (1.0 - sg))

    return pl.pallas_call(body, name="cctx_grad", out_shape=jax.ShapeDtypeStruct((1, D), F32))(parts, c_ctx)


def _row_tile(r):
    for t in (512, 256, 128, 80, 64, 40, 32, 16, 8):
        if r % t == 0:
            return t
    return r


def _slot_sum(ref):
    tot = ref[0].astype(F32)
    for i in range(1, ref.shape[0]):
        tot = tot + ref[i].astype(F32)
    return tot


def sum_slots(name, x):
    s, r, c = x.shape
    tr = _row_tile(r)

    def body(x_ref, o_ref):
        o_ref[...] = _slot_sum(x_ref)

    return pl.pallas_call(
        body, name=name, grid=(r // tr,), in_specs=[pl.BlockSpec((s, tr, c), lambda i: (0, i, 0))],
        out_specs=pl.BlockSpec((tr, c), lambda i: (i, 0)), out_shape=jax.ShapeDtypeStruct((r, c), F32),
    )(x)


def adamw(name, w, g, m, v, layer=None):
    r, c = w.shape[-2:]
    tr = _row_tile(r)
    stacked = g.ndim == 3

    def body(w_ref, g_ref, m_ref, v_ref, go_ref, d_ref, mo_ref, vo_ref):
        gv = _slot_sum(g_ref) if stacked else g_ref[...]
        mn = B1 * m_ref[...] + (1.0 - B1) * gv
        vn = B2 * v_ref[...] + (1.0 - B2) * jnp.square(gv)
        m_hat = mn / (1.0 - B1 ** STEP)
        v_hat = vn / (1.0 - B2 ** STEP)
        go_ref[...] = gv
        d_ref[...] = -LR * (m_hat / (jnp.sqrt(v_hat) + AEPS) + WD * w_ref[...])
        mo_ref[...] = mn
        vo_ref[...] = vn

    tile = pl.BlockSpec((tr, c), lambda i: (i, 0))
    slab = tile if layer is None else pl.BlockSpec((None, tr, c), lambda i: (layer, i, 0))
    g_spec = pl.BlockSpec((g.shape[0], tr, c), lambda i: (0, i, 0)) if stacked else tile
    return pl.pallas_call(
        body, name=name, grid=(r // tr,), in_specs=[slab, g_spec, slab, slab], out_specs=[tile] * 4,
        out_shape=[jax.ShapeDtypeStruct((r, c), F32)] * 4,
    )(w, g, m, v)


def _place():
    return lax.axis_index("x"), lax.axis_index("y"), lax.axis_index("c")


def all_gather(name, x, in_vmem):
    r, c = x.shape
    space = pltpu.VMEM if in_vmem else pl.ANY

    def body(x_ref, out_ref, send_sems, recv_sems, local_sem):
        px, py, pc = _place()
        me, sibling = (px, py, pc), (px, py, 1 - pc)
        chips = [(1 - px, py), (px, 1 - py), (1 - px, 1 - py)]

        def rows(qx, qy, qc):
            return out_ref.at[pl.ds((4 * qx + 2 * qy + qc) * r, r), :]

        def copy(k, block, to, src=None):
            return pltpu.make_async_remote_copy(
                src_ref=rows(*block) if src is None else src, dst_ref=rows(*block),
                send_sem=send_sems.at[k], recv_sem=recv_sems.at[k], device_id=to, device_id_type=MESH)

        mine = pltpu.make_async_copy(x_ref, rows(*me), local_sem)
        mine.start()
        first = [copy(0, me, sibling, src=x_ref)]
        first += [copy(1 + j, me, (*chip, pc), src=x_ref) for j, chip in enumerate(chips)]
        for cp in first:
            cp.start()
        passed = [copy(4 + j, (*chip, pc), sibling) for j, chip in enumerate(chips)]
        for j, chip in enumerate(chips):
            copy(1 + j, (*chip, pc), me).wait_recv()
            passed[j].start()
        copy(0, sibling, me).wait_recv()
        for j, chip in enumerate(chips):
            copy(4 + j, (*chip, 1 - pc), me).wait_recv()
        for cp in first + passed:
            cp.wait_send()
        mine.wait()

    return pl.pallas_call(
        body, name=name, out_shape=jax.ShapeDtypeStruct((N_DEV * r, c), x.dtype),
        in_specs=[pl.BlockSpec(memory_space=space)], out_specs=pl.BlockSpec(memory_space=space),
        scratch_shapes=[pltpu.SemaphoreType.DMA((7,)), pltpu.SemaphoreType.DMA((7,)), pltpu.SemaphoreType.DMA],
    )(x)


_HBM =pl.BlockSpec(memory_space=pltpu.HBM)
_SEM = pl.BlockSpec(memory_space=pltpu.SEMAPHORE)
_EFFECT = pltpu.SideEffectType.DATAFLOW_SIDE_EFFECTING


def _peers():
    px, py, pc = _place()
    return [(1 - px if k & 4 else px, 1 - py if k & 2 else py, 1 - pc if k & 1 else pc) for k in range(1, N_DEV)]


def _slot(dev):
    return 4 * dev[0] + 2 * dev[1] + dev[2]


def _split_copies(src_refs, land_refs, send_sems, recv_sems, gather):
    me = _slot(_place())
    return [pltpu.make_async_remote_copy(
        src_ref=src if gather else src.at[_slot(peer)], dst_ref=land.at[me],
        send_sem=send_sems.at[a * (N_DEV - 1) + k], recv_sem=recv_sems.at[a * (N_DEV - 1) + k],
        device_id=peer, device_id_type=MESH)
        for a, (src, land) in enumerate(zip(src_refs, land_refs)) for k, peer in enumerate(_peers())]


def exchange_start(name, srcs, gather, after):
    n = len(srcs)
    lands = [pltpu.HBM((N_DEV,) + s.shape if gather else s.shape, s.dtype) for s in srcs]

    def body(*refs):
        send_sems, recv_sems = refs[2 * n + 1:2 * n + 3]
        for cp in _split_copies(refs[:n], refs[n:2 * n], send_sems, recv_sems, gather):
            cp.start()
        refs[-1][...] = jnp.zeros_like(refs[-1])

    sems = pltpu.SemaphoreType.DMA((n * (N_DEV - 1),))
    res = pl.pallas_call(
        body, name=name,
        out_shape=(sems, sems, *[pltpu.HBM(s.shape, s.dtype) for s in srcs], *lands, jax.ShapeDtypeStruct((8, 128), F32)),
        in_specs=(_HBM,) * (2 * n) + (pl.BlockSpec(memory_space=pl.ANY),),
        out_specs=(_SEM, _SEM) + (_HBM,) * (2 * n) + (pl.BlockSpec(memory_space=pltpu.VMEM),),
        input_output_aliases={i: 2 + i for i in range(2 * n)},
        compiler_params=pltpu.CompilerParams(has_side_effects=_EFFECT),
    )(*[pltpu.with_memory_space_constraint(s, pltpu.HBM) for s in srcs],
      *[pltpu.with_memory_space_constraint(lax.empty(ld.shape, ld.dtype), pltpu.HBM) for ld in lands], after)
    return res[0], res[1], list(res[2:2 + n]), list(res[2 + n:2 + 2 * n]), res[-1]


def exchange_wait(name, started, after, gather):
    send_sems, recv_sems, srcs, lands, _ = started
    n = len(srcs)
    after = list(after) if isinstance(after, (list, tuple)) else [after]

    def body(*refs):
        send_sems, recv_sems = refs[2 * n:2 * n + 2]
        for cp in _split_copies(refs[:n], refs[n:2 * n], send_sems, recv_sems, gather):
            cp.wait_send()
            cp.wait_recv()

    res = pl.pallas_call(
        body, name=name, out_shape=tuple(pltpu.HBM(a.shape, a.dtype) for a in srcs + lands),
        in_specs=(_HBM,) * (2 * n) + (_SEM, _SEM) + (pl.BlockSpec(memory_space=pl.ANY),) * len(after),
        out_specs=(_HBM,) * (2 * n), input_output_aliases={i: i for i in range(2 * n)},
        compiler_params=pltpu.CompilerParams(has_side_effects=_EFFECT),
    )(*srcs, *lands, send_sems, recv_sems, *after)
    return list(res[:n]), list(res[n:])


NCF = FFN_H // FFN_TC


def _size(shape):
    n = 1
    for s in shape:
        n *= s
    return n


def _padded_rows(n_elems, row_mult):
    return -(-n_elems // (D * row_mult)) * row_mult


def _pack_rows(arrs, dtype, row_mult):
    rows, offs, r0 = [], [], 0
    for a in arrs:
        flat = a.reshape(-1).astype(dtype)
        n = _padded_rows(flat.shape[0], row_mult)
        rows.append(jnp.pad(flat, (0, n * D - flat.shape[0])).reshape(n, D))
        offs.append(r0)
        r0 += n
    return jnp.concatenate(rows, 0), offs


def _unpack_rows(buf, offs, shapes):
    lead, out = buf.shape[:-2], []
    for o, shp in zip(offs, shapes):
        n = _size(shp)
        nr = -(-n // D)
        out.append(buf[..., o:o + nr, :].reshape(lead + (nr * D,))[..., :n].reshape(lead + tuple(shp)))
    return out


def _rows3(w):
    return [w[i:i + 1] for i in range(3)]


def f_mod1(xs, ps):
    return f_mod(xs, ps)[:1]


def kernel(x, c, ctx, c_ctx, ada_w, ada_b, norm_mix, norm_ffn, gla_w_in, gla_w_a2, gla_b_a, gla_head_norm, gla_w_out, sc_w_in, sc_conv_w, sc_w_out, ffn_w_up, ffn_conv_w, ffn_conv_b, ffn_w_down, final_norm, loss_target, m_c_ctx, m_ada_w, m_ada_b, m_norm_mix, m_norm_ffn, m_gla_w_in, m_gla_w_a2, m_gla_b_a, m_gla_head_norm, m_gla_w_out, m_sc_w_in, m_sc_conv_w, m_sc_w_out, m_ffn_w_up, m_ffn_conv_w, m_ffn_conv_b, m_ffn_w_down, m_final_norm, v_c_ctx, v_ada_w, v_ada_b, v_norm_mix, v_norm_ffn, v_gla_w_in, v_gla_w_a2, v_gla_b_a, v_gla_head_norm, v_gla_w_out, v_sc_w_in, v_sc_conv_w, v_sc_w_out, v_ffn_w_up, v_ffn_conv_w, v_ffn_conv_b, v_ffn_w_down, v_final_norm):
    names = ["c_ctx", "ada_w", "ada_b", "norm_mix", "norm_ffn", "gla_w_in", "gla_w_a2", "gla_b_a", "gla_head_norm",
             "gla_w_out", "sc_w_in", "sc_conv_w", "sc_w_out", "ffn_w_up", "ffn_conv_w", "ffn_conv_b", "ffn_w_down",
             "final_norm"]
    w_ = dict(zip(names, [c_ctx, ada_w, ada_b, norm_mix, norm_ffn, gla_w_in, gla_w_a2, gla_b_a, gla_head_norm, gla_w_out,
                          sc_w_in, sc_conv_w, sc_w_out, ffn_w_up, ffn_conv_w, ffn_conv_b, ffn_w_down, final_norm]))
    m_ = dict(zip(names, [m_c_ctx, m_ada_w, m_ada_b, m_norm_mix, m_norm_ffn, m_gla_w_in, m_gla_w_a2, m_gla_b_a,
                          m_gla_head_norm, m_gla_w_out, m_sc_w_in, m_sc_conv_w, m_sc_w_out, m_ffn_w_up, m_ffn_conv_w,
                          m_ffn_conv_b, m_ffn_w_down, m_final_norm]))
    v_ = dict(zip(names, [v_c_ctx, v_ada_w, v_ada_b, v_norm_mix, v_norm_ffn, v_gla_w_in, v_gla_w_a2, v_gla_b_a,
                          v_gla_head_norm, v_gla_w_out, v_sc_w_in, v_sc_conv_w, v_sc_w_out, v_ffn_w_up, v_ffn_conv_w,
                          v_ffn_conv_b, v_ffn_w_down, v_final_norm]))
    me = 4 * lax.axis_index("x") + 2 * lax.axis_index("y") + lax.axis_index("c")
    bsz = x.shape[0]
    tm = 256
    nt = SEQ // tm
    ctx_tiles = CTX // tm
    pe = functools.partial(P, per_example=True)

    groups = {"ffn1": [("ffn_w_up", 1), ("ffn_w_down", 1)], "sc": [("sc_w_in", 0), ("sc_w_out", 0)],
              "ffn0": [("ffn_w_up", 0), ("ffn_w_down", 0)], "gla": [("gla_w_in", 0), ("gla_w_out", 0)]}
    ag_groups = {"gin": [("gla_w_in", 0)], "ffn0": [("gla_w_out", 0), ("ffn_w_up", 0), ("ffn_w_down", 0)],
                 "sc": groups["sc"], "ffn1": groups["ffn1"]}
    ag_started = {}

    def start_gather(g, after):
        ag_started[g] = exchange_start(f"ag_{g}_start", [w_[n][i].astype(BF16) for n, i in ag_groups[g]], True, after)
        return ag_started[g][4]

    small_sharded = [c, gla_w_a2, gla_b_a, sc_conv_w, ffn_conv_w]
    pack0, offs0 = _pack_rows(small_sharded, F32, 8)
    g0 = all_gather("ag_small", pack0, True).reshape(N_DEV, pack0.shape[0], D)
    c_all, wa2_s, ba_s, scw_s, fcw_s = _unpack_rows(g0, offs0, [a.shape for a in small_sharded])
    w_a2 = wa2_s[:, 0].transpose(1, 2, 0, 3).reshape(2, RANK, KD)
    b_a = ba_s[:, 0].transpose(1, 0, 2).reshape(2, KD)
    sc_cw = scw_s[:, 0].transpose(1, 0, 2).reshape(3, D)
    ffn_cw = fcw_s.transpose(1, 2, 0, 3).reshape(2, 3, 2 * FFN_H)

    cond = jnp.concatenate([c_all.reshape(N_DEV * bsz, D), c_ctx[None], jnp.zeros((ADA_ROWS - N_DEV * bsz - 1, D), F32)], 0)
    b_mine = lax.dynamic_slice(ada_b, (0, me * ADA_COLS), (2, ADA_COLS)).reshape(2, 1, ADA_COLS)
    mod_part = ada_fwd(cond, ada_w, b_mine)
    mod = all_gather("ag_mod", mod_part.reshape(2 * ADA_ROWS, ADA_COLS), True)
    mod = mod.reshape(N_DEV, 2, ADA_ROWS, ADA_COLS).transpose(1, 2, 0, 3).reshape(2, ADA_ROWS, 6 * D)
    mods = lax.dynamic_slice(mod, (0, bsz * me, 0), (2, bsz, 6 * D))
    md = [[mods[i][:, k * D:(k + 1) * D].reshape(bsz, 1, D) for k in range(6)] for i in range(2)]
    mc = [mod[0, ADA_CTX_ROW, k * D:(k + 1) * D][None] for k in range(2)]

    tok = mod
    for g in ag_groups:
        tok = start_gather(g, tok)
    norm_mix = norm_mix + tok[0, 0]

    def gathered(g, after):
        mine, lands = exchange_wait(f"ag_{g}_wait", ag_started[g], after, True)
        return [lax.dynamic_update_index_in_dim(ld, mn, me, 0) for ld, mn in zip(lands, mine)]

    s_up, w_down = [None, None], [None, None]
    wd = jnp.zeros((128, 2 * KD), F32).at[:RANK, :KD].set(w_a2[0]).at[RANK:2 * RANK, KD:].set(w_a2[1])
    bd = b_a.reshape(1, 2 * KD)
    scw = _rows3(sc_cw)
    head_gain = gla_head_norm.reshape(1, HV)
    gains_mix = [norm_mix[i][None] for i in range(2)]
    gains_ffn = [norm_ffn[i][None] for i in range(2)]

    def tokens(a2d, t_len):
        return a2d.reshape(bsz, t_len, -1)

    def ffn_params(i):
        rows = [ffn_cw[i][t] for t in range(3)] + [ffn_conv_b[i]]
        return [P(a.reshape(2, FFN_H), w=FFN_TC, rows=True) for a in rows]

    def ffn_fwd(i, hn2):
        u = mm(f"ffn_up{i}", V(hn2, "tok"), V(s_up[i], "cols"), out="planes", out_dtype=BF16, planes_t=SEQ)
        act = rowwise(f"ffn_mid{i}", f_ffn_mid, [X(u, w=FFN_TC, planes=True)], ffn_params(i), tm=SEQ, nt=1, nc=NCF,
                      outs=[(FFN_TC, BF16, 1)])[0]
        return u, act

    def arrays(ps):
        return [p["a"] for p in ps]

    ps_in0 = [P(gains_mix[0]), pe(md[0][0]), pe(md[0][1])]
    ps_ctx = [P(gains_mix[0]), P(mc[0]), P(mc[1])]
    hn0 = rowwise("mod_in0", f_mod, [X(x)], ps_in0, tm=tm, nt=nt, outs=[(D, BF16, 1)])[0]
    hnc = rowwise("mod_ctx", f_mod, [X(ctx)], ps_ctx, tm=tm, nt=ctx_tiles, outs=[(D, BF16, 1)])[0]
    hcat = jnp.concatenate([hnc, hn0], axis=1)
    (s_gin,) = gathered("gin", hcat)
    w_gin = V(s_gin, "cols", width=GLA_IN_PAD)
    pcat = tokens(mm("gla_in", V(hcat, "tok"), w_gin, out_dtype=BF16), TT)
    pa_x = X(pcat, w=128, co=(GLA_IN_PAD - 128) // 128)
    la = rowwise("gla_decay", f_decay, [pa_x], [P(wd), P(bd)], tm=tm, nt=TT // tm, outs=[(2 * KD, F32, 1)])[0]
    o2, s_all = gla_fwd(pcat, la)
    post_xs = [X(o2, w=VD, co=0, ro=ctx_tiles, split=HEADS), X(o2, w=VD, co=1, ro=ctx_tiles, split=HEADS),
               X(pcat, w=VD, co=2, ro=ctx_tiles, split=HEADS)]
    yin0 = rowwise("gla_post", f_gla_post, post_xs, [P(head_gain)], tm=tm, nt=nt, outs=[(VD, BF16, HEADS)])[0]
    s_gout, s_up[0], s_down0 = gathered("ffn0", yin0)
    w_gout, w_down[0] = s_gout.reshape(VD, D), s_down0.reshape(FFN_H, D)
    ps_mid0 = [pe(md[0][2]), P(gains_ffn[0]), pe(md[0][3]), pe(md[0][4])]
    y0, h1_0, hn2_0 = mm_res_mod("gla_out", yin0, w_gout, x, *arrays(ps_mid0))
    u0, act0 = ffn_fwd(0, hn2_0)
    ps_in1 = [pe(md[0][5]), P(gains_mix[1]), pe(md[1][0]), pe(md[1][1])]
    fo0, h2_0, hn1 = mm_res_mod("ffn_down0", act0, w_down[0], h1_0, *arrays(ps_in1))

    s_sin, s_sout = gathered("sc", hn1)
    w_sout = s_sout.reshape(D, D)
    p1 = tokens(mm("sc_in", V(hn1, "tok"), V(s_sin, "cols")), SEQ)
    sc_ps = [P(a) for a in scw]
    yin1 = rowwise("sc_mid", f_sc_mid, [X(p1, split=3)], sc_ps, tm=tm, nt=nt, outs=[(D, BF16, 1)])[0]
    ps_mid1 = [pe(md[1][2]), P(gains_ffn[1]), pe(md[1][3]), pe(md[1][4])]
    y1, h1_1, hn2_1 = mm_res_mod("sc_out", yin1, w_sout, h2_0, *arrays(ps_mid1))
    s_up[1], s_down1 = gathered("ffn1", hn2_1)
    w_down[1] = s_down1.reshape(FFN_H, D)
    u1, act1 = ffn_fwd(1, hn2_1)
    fo1 = tokens(mm("ffn_down1", V(act1, "tok"), V(w_down[1])), SEQ)
    loss8, dh1_1, dfo1, dm5_1, g_final = final_loss(h1_1, fo1, md[1][5], final_norm[None], loss_target)

    def ffn_bwd(i, u, act, hn2, dfo):
        dact = tokens(mm(f"ffn_down_dx{i}", V(dfo, "tok"), V(w_down[i]), form="nt", out_dtype=BF16), SEQ)
        g_down = mm(f"ffn_down_dw{i}", V(act, "tok"), V(dfo, "tok"), form="tn", out_dtype=BF16)
        r = rowwise(f"ffn_mid_bwd{i}", f_ffn_mid, [X(u, w=FFN_TC, planes=True)], ffn_params(i), tm=SEQ, nt=1, nc=NCF,
                    douts=[X(dact, w=FFN_TC)], dx={0: BF16}, dp=[0, 1, 2, 3])
        du, g_cw, g_cb = r[0], jnp.stack([a.reshape(2 * FFN_H) for a in r[1:4]]), r[4].reshape(1, 2 * FFN_H)
        dhn2 = tokens(mm(f"ffn_up_dx{i}", V(du, "planes"), V(s_up[i], "cols"), form="nt", out_dtype=BF16), SEQ)
        g_up = mm(f"ffn_up_dw{i}", V(hn2, "tok"), V(du, "planes"), form="tn", out="cols", out_dtype=BF16)
        return dhn2, g_up, row_slots(g_down), g_cw, g_cb

    def res_mod_bwd(name, h, y, ps, dh1, dhn):
        return rowwise(name, f_res_mod, [X(h), X(y)], ps, tm=tm, nt=nt, douts=[X(dh1), X(dhn)],
                       dx={0: F32, 1: BF16}, dp=[0, 1, 2, 3])

    def row_slots(g):
        return g.reshape(N_DEV, -1, g.shape[-1])

    a2a_started = {}

    def send_grads(g, slots, after=None):
        a2a_started[g] = exchange_start(f"a2a_{g}_start", list(slots), False, loss8 if after is None else after)
        return a2a_started[g][4][0, 0]

    def after_start(ps, tok):
        return [dict(ps[0], a=ps[0]["a"] + tok)] + ps[1:]

    dhn2_1, g_up1, g_down1, g_fcw1, g_fcb1 = ffn_bwd(1, u1, act1, hn2_1, dfo1)
    tok = send_grads("ffn1", [g_up1, g_down1])
    dh2_0, dy1, dm2_1, g_nffn1, dm3_1, dm4_1 = res_mod_bwd("res_mod_mid1_bwd", h2_0, y1, after_start(ps_mid1, tok), dh1_1, dhn2_1)
    dyin1 = tokens(mm("sc_out_dx", V(dy1, "tok"), V(w_sout), form="nt", out_dtype=BF16), SEQ)
    g_sout = row_slots(mm("sc_out_dw", V(yin1, "tok"), V(dy1, "tok"), form="tn", out_dtype=BF16))
    r = rowwise("sc_mid_bwd", f_sc_mid, [X(p1, split=3)], sc_ps, tm=tm, nt=nt, douts=[X(dyin1)], dx={0: BF16}, dp=[0, 1, 2])
    dp1, g_scw = r[0], jnp.concatenate(r[1:4], 0)
    dhn1 = tokens(mm("sc_in_dx", V(dp1, "tok"), V(s_sin, "cols"), form="nt", out_dtype=BF16), SEQ)
    g_sin = mm("sc_in_dw", V(hn1, "tok"), V(dp1, "tok"), form="tn", out="cols", out_dtype=BF16)
    tok = send_grads("sc", [g_sin, g_sout])
    dh1_0, dfo0, dm5_0, g_nmix1, dm0_1, dm1_1 = res_mod_bwd("res_mod_in1_bwd", h1_0, fo0, after_start(ps_in1, tok), dh2_0, dhn1)

    dhn2_0, g_up0, g_down0, g_fcw0, g_fcb0 = ffn_bwd(0, u0, act0, hn2_0, dfo0)
    tok = send_grads("ffn0", [g_up0, g_down0])
    dx_res, dy0, dm2_0, g_nffn0, dm3_0, dm4_0 = res_mod_bwd("res_mod_mid0_bwd", x, y0, after_start(ps_mid0, tok), dh1_0, dhn2_0)
    dyin0 = tokens(mm("gla_out_dx", V(dy0, "tok"), V(w_gout), form="nt", out_dtype=BF16), SEQ)
    g_gout = row_slots(mm("gla_out_dw", V(yin0, "tok"), V(dy0, "tok"), form="tn", out_dtype=BF16))
    do, dgate, g_head = rowwise("gla_post_bwd", f_gla_post, post_xs, [P(head_gain)], tm=tm, nt=nt,
                                douts=[X(dyin0, split=HEADS)], dx={0: BF16, 2: BF16}, dp=[0])
    dq2, dk2, dv2, dla = gla_bwd(pcat, la, s_all, do)
    dpa, g_wd, g_bd = rowwise("gla_decay_bwd", f_decay, [pa_x], [P(wd), P(bd)], tm=tm, nt=TT // tm, douts=[X(dla)],
                              dx={0: BF16}, dp=[0, 1])
    dpcat = gla_combine(dq2, dk2, dv2, dgate, dpa)
    dhcat = tokens(mm("gla_in_dx", V(dpcat, "tok"), w_gin, form="nt", out_dtype=BF16), TT)
    grad_x, g_nmix0, dm0_0, dm1_0 = rowwise("mod_in0_bwd", f_mod, [X(x)], ps_in0, tm=tm, nt=nt,
                                            douts=[X(dhcat, ro=ctx_tiles), X(dx_res)], dx={0: F32}, dp=[0, 1, 2])
    g_nmix0c, dmc0, dmc1 = rowwise("mod_ctx_bwd", f_mod1, [X(ctx)], ps_ctx, tm=tm, nt=ctx_tiles, douts=[X(dhcat)],
                                   dx={}, dp=[0, 1, 2])

    zero_row = jnp.zeros((1, 4 * D), F32)
    dmod = [jnp.concatenate([jnp.concatenate([a.reshape(bsz, D) for a in dms], 1), ctx_row], 0)
            for dms, ctx_row in (([dm0_0, dm1_0, dm2_0, dm3_0, dm4_0, dm5_0], jnp.concatenate([dmc0, dmc1, zero_row], 1)),
                                 ([dm0_1, dm1_1, dm2_1, dm3_1, dm4_1, dm5_1], jnp.zeros((1, 6 * D), F32)))]
    g_wa2 = jnp.stack([g_wd[:RANK, :KD], g_wd[RANK:2 * RANK, KD:]])
    small_grads = [jnp.stack(dmod), jnp.concatenate([g_nmix0 + g_nmix0c, g_nmix1], 0), jnp.concatenate([g_nffn0, g_nffn1], 0),
                   g_head, jnp.concatenate([g_fcb0, g_fcb1], 0), g_final, g_wa2, g_bd.reshape(2, KD), g_scw,
                   jnp.stack([g_fcw0, g_fcw1]), loss8[:1]]
    pack1, offs1 = _pack_rows(small_grads, F32, 8)
    ag1 = exchange_start("ag_grads_start", [pack1], True, loss8)
    g_gin = mm("gla_in_dw", V(hcat, "tok"), V(dpcat, "tok"), form="tn", out="cols", out_dtype=BF16, shard_n=GLA_IN // N_DEV,
               after=ag1[4])
    mine1, land1 = exchange_wait("ag_grads_wait", ag1, [g_gin], True)
    g1 = lax.dynamic_update_index_in_dim(land1[0], mine1[0], me, 0)
    dmod_all = _unpack_rows(g1, offs1[:1], [small_grads[0].shape])[0]
    tot = _unpack_rows(sum_slots("sum_small", g1), offs1, [a.shape for a in small_grads])
    loss = tot[10][0, 0]
    dm_rows = dmod_all[:, :, :bsz].transpose(1, 0, 2, 3).reshape(2, N_DEV * bsz, 6 * D)
    dm_full = jnp.concatenate([dm_rows, tot[0][:, bsz:], jnp.zeros((2, ADA_ROWS - N_DEV * bsz - 1, 6 * D), F32)], 1)
    dm_mine = lax.dynamic_slice(dm_full, (0, 0, me * ADA_COLS), (2, ADA_ROWS, ADA_COLS))
    g_ada_w, g_ada_b, cpart = ada_bwd(cond, dm_mine, dm_full, ada_w)
    cparts = all_gather("ag_cctx", cpart, True).reshape(N_DEV, ADA_ROWS - ADA_CTX_ROW, D)[:, 0]
    g_cctx = cctx_grad(cparts, c_ctx[None])[0]
    tok = send_grads("gla", [g_gin, g_gout], after=g_cctx)

    def my_cols(full, n):
        return lax.dynamic_slice_in_dim(full, me * n, n, axis=full.ndim - 1)

    grads = {
        "c_ctx": g_cctx, "ada_b": g_ada_b.reshape(2, 6 * D), "norm_mix": tot[1], "norm_ffn": tot[2],
        "gla_head_norm": tot[3], "ffn_conv_b": tot[4], "final_norm": tot[5].reshape(D),
        "gla_w_a2": my_cols(tot[6], KD // N_DEV)[None], "gla_b_a": my_cols(tot[7], KD // N_DEV)[None],
        "sc_conv_w": my_cols(tot[8], D // N_DEV)[None], "ffn_conv_w": my_cols(tot[9], 2 * FFN_H // N_DEV),
    }

    res_ada = adamw("adamw_ada", *[a.reshape(2 * D, ADA_COLS) for a in (ada_w, g_ada_w, m_ada_w, v_ada_w)])
    grads["c_ctx"] = g_cctx + tok
    big = ["gla_w_in", "gla_w_out", "sc_w_in", "sc_w_out", "ffn_w_up", "ffn_w_down"]
    small = [n for n in names if n not in big and n != "ada_w"]
    g_small = _pack_rows([grads[n] for n in small], F32, 8)[0]
    res_small = adamw("adamw_small", _pack_rows([w_[n] for n in small], F32, 8)[0], g_small,
                      _pack_rows([m_[n] for n in small], F32, 8)[0], _pack_rows([v_[n] for n in small], F32, 8)[0])
    offs_s = _pack_rows([w_[n] for n in small], F32, 8)[1]

    big_res, done = {}, [res_small[0], res_ada[0]]
    for g in groups:
        sent, lands = exchange_wait(f"a2a_{g}_wait", a2a_started[g], done, False)
        for (n, i), mine, land in zip(groups[g], sent, lands):
            land = lax.dynamic_update_index_in_dim(land, lax.dynamic_index_in_dim(mine, me, 0, keepdims=False), me, 0)
            big_res[(n, i)] = adamw(f"adamw_{n}{i}", w_[n], land, m_[n], v_[n], layer=i)
            done.append(big_res[(n, i)][0])

    out = {}
    for kind, idx in (("grad", 0), ("delta", 1), ("new_m", 2), ("new_v", 3)):
        vals = {n: jnp.stack([big_res[(n, i)][idx] for i in range(w_[n].shape[0])]) for n in big}
        vals["ada_w"] = res_ada[idx].reshape(ada_w.shape)
        vals.update(zip(small, _unpack_rows(res_small[idx], offs_s, [w_[n].shape for n in small])))
        out[kind] = [vals[n] for n in names]
    return (loss, grad_x, *out["grad"], *out["delta"], *out["new_m"], *out["new_v"])
```

```python
import functools

import jax
import jax.numpy as jnp
from jax import lax
from jax.experimental import pallas as pl
from jax.experimental.pallas import tpu as pltpu

F32 = jnp.float32
BF16 = jnp.bfloat16

N_DEV = 8
D = 1024
SEQ = 2048
CTX = 256
TT = CTX + SEQ
GRID_W = 64
CHUNK = 64
HEADS = 4
HK = 128
HV = 256
KD = 512
VD = 1024
RANK = 16
TAU = 16.0
GLA_IN = 3104
GLA_IN_PAD = 3200
FFN_H = 2560
FFN_TC = 256
EPS = 1e-6
LR, B1, B2, AEPS, WD, STEP = 0.001, 0.9, 0.999, 1e-08, 0.01, 10
MESH = pl.DeviceIdType.MESH


def _blocks(n):
    return [n] + [t for t in range(n - n % 128, 0, -128) if n % t == 0 and t != n]


def V(arr, kind="flat", width=None):
    if kind == "tok":
        return V(arr.reshape(-1, arr.shape[-1]))
    if kind == "flat":
        r, c = arr.shape
        return dict(a=arr, kind=kind, shape=(r, c), rows=_blocks(r), cols=_blocks(c))
    if kind == "planes":
        bsz, _, t, ch = arr.shape
        return dict(a=arr, kind=kind, shape=(bsz * t, 2 * ch), rows=_blocks(t), cols=[2 * ch] + _blocks(ch), t=t, ch=ch)
    _, r, n = arr.shape
    if width is not None:
        return dict(a=arr, kind=kind, shape=(r, width), rows=_blocks(r), cols=[width], n=n, pad=width - N_DEV * n)
    return dict(a=arr, kind=kind, shape=(r, N_DEV * n), rows=_blocks(r), cols=[8 * n, 4 * n, 2 * n], n=n, pad=0)


def _view_spec(v, br, bc, idx):
    if v["kind"] == "flat":
        return pl.BlockSpec((br, bc), idx)
    if v["kind"] == "planes":
        nt = v["t"] // br
        if bc == 2 * v["ch"]:
            return pl.BlockSpec((None, 2, br, v["ch"]), lambda i, j, k: (idx(i, j, k)[0] // nt, 0, idx(i, j, k)[0] % nt, 0))
        nch = v["ch"] // bc

        def at(i, j, k):
            r, c = idx(i, j, k)
            return r // nt, c // nch, r % nt, c % nch
        return pl.BlockSpec((None, None, br, bc), at)
    return pl.BlockSpec(((bc - v["pad"]) // v["n"], br, v["n"]), lambda i, j, k: (idx(i, j, k)[1], idx(i, j, k)[0], 0))


def _out_view(kind, rows, cols, dtype, planes_t=None, shard_n=None):
    if kind == "flat":
        shape = (rows, cols)
    elif kind == "planes":
        shape = (rows // planes_t, 2, planes_t, cols // 2)
    elif shard_n is not None:
        return V(jax.ShapeDtypeStruct((N_DEV, rows, shard_n), dtype), kind, width=cols)
    else:
        shape = (N_DEV, rows, cols // N_DEV)
    return V(jax.ShapeDtypeStruct(shape, dtype), kind)


MM_VMEM_BUDGET = 40 * 2 ** 20
MM_VMEM_LIMIT = 56 * 2 ** 20
MM_MAX_TILE = 1536


def _mm_tiles(m, n, kk, ms, ns, ks, a_bytes, b_bytes, o_bytes):
    best = None
    for tk in ks:
        for tm in [t for t in ms if t <= MM_MAX_TILE] or ms:
            for tn in [t for t in ns if t <= MM_MAX_TILE] or ns:
                one_k = tk == kk
                need = 2 * (tm * tk * a_bytes + tk * tn * b_bytes + tm * tn * o_bytes) + (0 if one_k else tm * tn * 4)
                if need > MM_VMEM_BUDGET:
                    continue
                steps = (m // tm) * (n // tn) * (kk // tk)
                traffic = (m * kk * a_bytes * (1 if one_k else n // tn)
                           + kk * n * b_bytes * (1 if one_k and n == tn else m // tm) + m * n * o_bytes)
                fill = (tm * tk * a_bytes + tk * tn * b_bytes) / 2.5e12
                cost = max(2.0 * m * n * kk / (9e14 if one_k else 6.5e14), traffic / 2.5e12) + steps * 0.4e-6 + fill
                if best is None or cost < best[0]:
                    best = (cost, tm, tn, tk)
    return best[1:]


def mm(name, a, b, form="nn", out="flat", out_dtype=F32, planes_t=None, shard_n=None, after=None):
    (m, kk) = a["shape"][::-1] if form == "tn" else a["shape"]
    n = b["shape"][0] if form == "nt" else b["shape"][1]
    assert (b["shape"][1] if form == "nt" else b["shape"][0]) == kk, (name, a["shape"], b["shape"])
    o = _out_view(out, m, n, out_dtype, planes_t, shard_n)
    a_m, a_k = (a["cols"], a["rows"]) if form == "tn" else (a["rows"], a["cols"])
    b_k, b_n = (b["cols"], b["rows"]) if form == "nt" else (b["rows"], b["cols"])
    tm, tn, tk = _mm_tiles(m, n, kk, [t for t in a_m if t in o["rows"]], [t for t in b_n if t in o["cols"]],
                           [t for t in a_k if t in b_k], a["a"].dtype.itemsize, b["a"].dtype.itemsize,
                           jnp.dtype(out_dtype).itemsize)
    nk = kk // tk
    dn = (((0 if form == "tn" else 1,), (1 if form == "nt" else 0,)), ((), ()))

    def load(ref, v):
        if len(ref.shape) == 3:
            pieces = [ref[p].astype(BF16) for p in range(ref.shape[0])]
            if v.get("pad"):
                pieces.append(jnp.zeros(ref.shape[1:2] + (v["pad"],), BF16))
            return jnp.concatenate(pieces, axis=-1)
        return ref[...].astype(BF16)

    def store(o_ref, val):
        val = val.astype(out_dtype)
        if len(o_ref.shape) == 3:
            w = o_ref.shape[-1]
            for p in range(o_ref.shape[0]):
                o_ref[p] = val[:, p * w:(p + 1) * w]
        else:
            o_ref[...] = val

    def body(a_ref, b_ref, *rest):
        o_ref, acc = rest[0 if after is None else 1], rest[1 if after is None else 2:]
        if nk == 1:
            store(o_ref, lax.dot_general(load(a_ref, a), load(b_ref, b), dn, preferred_element_type=F32))
            return
        k, acc_ref = pl.program_id(2), acc[0]

        @pl.when(k == 0)
        def _():
            acc_ref[...] = jnp.zeros_like(acc_ref)

        acc_ref[...] += lax.dot_general(load(a_ref, a), load(b_ref, b), dn, preferred_element_type=F32)

        @pl.when(k == nk - 1)
        def _():
            store(o_ref, acc_ref[...])

    if form == "tn":
        a_spec = _view_spec(a, tk, tm, lambda i, j, k: (k, i))
    else:
        a_spec = _view_spec(a, tm, tk, lambda i, j, k: (i, k))
    if form == "nt":
        b_spec = _view_spec(b, tn, tk, lambda i, j, k: (j, k))
    else:
        b_spec = _view_spec(b, tk, tn, lambda i, j, k: (k, j))
    return pl.pallas_call(
        body, name=name, grid=(m // tm, n // tn, nk),
        in_specs=[a_spec, b_spec] + ([] if after is None else [pl.BlockSpec(memory_space=pl.ANY)]),
        out_specs=_view_spec(o, tm, tn, lambda i, j, k: (i, j)), out_shape=o["a"],
        scratch_shapes=[pltpu.VMEM((tm, tn), F32)] if nk > 1 else [],
        compiler_params=pltpu.CompilerParams(dimension_semantics=("parallel", "parallel", "arbitrary"),
                                             vmem_limit_bytes=MM_VMEM_LIMIT),
    )(a["a"], b["a"], *([] if after is None else [after]))


def mm_res_mod(name, a, w, h, gate, gain, shift, scale):
    bsz, t_len, kk = a.shape
    tm = 512
    per = t_len // tm

    def body(a_ref, w_ref, h_ref, gate_ref, gain_ref, shift_ref, scale_ref, y_ref, h1_ref, hn_ref):
        y = jnp.dot(a_ref[...].astype(BF16), w_ref[...].astype(BF16), preferred_element_type=F32)
        h1 = h_ref[...] + gate_ref[...] * y
        y_ref[...] = y.astype(BF16)
        h1_ref[...] = h1
        hn_ref[...] = _mod(h1, gain_ref[...], shift_ref[...], scale_ref[...]).astype(BF16)

    def tile(width):
        return pl.BlockSpec((None, tm, width), lambda i: (i // per, i % per, 0))

    per_ex = pl.BlockSpec((None, 1, D), lambda i: (i // per, 0, 0))
    return pl.pallas_call(
        body, name=name, grid=(bsz * per,),
        in_specs=[tile(kk), pl.BlockSpec((kk, D), lambda i: (0, 0)), tile(D), per_ex, pl.BlockSpec((1, D), lambda i: (0, 0)),
                  per_ex, per_ex],
        out_specs=[tile(D)] * 3,
        out_shape=[jax.ShapeDtypeStruct((bsz, t_len, D), BF16), jax.ShapeDtypeStruct((bsz, t_len, D), F32),
                   jax.ShapeDtypeStruct((bsz, t_len, D), BF16)],
        compiler_params=pltpu.CompilerParams(dimension_semantics=("parallel",), vmem_limit_bytes=MM_VMEM_LIMIT),
    )(a, w, h, gate, gain, shift, scale)


def X(arr, w=None, co=0, ro=0, split=1, planes=False):
    return dict(a=arr, w=arr.shape[-1] if w is None else w, co=co, ro=ro, split=2 if planes else split,
                mode="planes" if planes else "cols")


def P(arr, per_example=False, w=None, split=1, rows=False):
    return dict(a=arr, e=per_example, w=arr.shape[-1] if w is None else w, split=arr.shape[-2] if rows else split,
                mode="rows" if rows else "cols")


def _pieces(ref, s):
    if s["mode"] == "planes":
        return [ref[0], ref[1]]
    if s["mode"] == "rows":
        return [ref[i:i + 1, :] for i in range(s["split"])]
    w = ref.shape[-1] // s["split"]
    return [ref[:, i * w:(i + 1) * w] for i in range(s["split"])]


def _store(ref, pieces, s, accumulate=False):
    w = ref.shape[-1] // len(pieces)
    for i, p in enumerate(pieces):
        at = (i,) if s["mode"] == "planes" else (slice(i, i + 1),) if s["mode"] == "rows" else (slice(None), slice(i * w, (i + 1) * w))
        if accumulate:
            ref[at] += p.astype(ref.dtype)
        else:
            ref[at] = p.astype(ref.dtype)


def rowwise(name, f, xs, ps, *, tm, nt, nc=1, outs=None, douts=None, dx=None, dp=None):
    bsz = xs[0]["a"].shape[0]
    fwd = douts is None
    nx, np_ = len(xs), len(ps)
    douts = [] if fwd else douts
    dx = {} if fwd else dx
    dp = [] if fwd else dp

    def x_spec(s):
        if s["mode"] == "planes":
            return pl.BlockSpec((None, 2, tm, s["w"]), lambda c, b, t, s=s: (b, 0, t + s["ro"], c + s["co"]))
        return pl.BlockSpec((None, tm, s["w"]), lambda c, b, t, s=s: (b, t + s["ro"], c + s["co"]))

    def x_out(s, dt):
        if s["mode"] == "planes":
            return (jax.ShapeDtypeStruct((bsz, 2, nt * tm, nc * s["w"]), dt),
                    pl.BlockSpec((None, 2, tm, s["w"]), lambda c, b, t: (b, 0, t, c)))
        return (jax.ShapeDtypeStruct((bsz, nt * tm, nc * s["w"]), dt), pl.BlockSpec((None, tm, s["w"]), lambda c, b, t: (b, t, c)))

    def p_spec(s):
        r = s["a"].shape[-2]
        if s["e"]:
            return pl.BlockSpec((None, r, s["w"]), lambda c, b, t: (b, 0, c))
        return pl.BlockSpec((r, s["w"]), lambda c, b, t: (0, c))

    in_specs = [x_spec(s) for s in xs] + [p_spec(s) for s in ps] + [x_spec(s) for s in douts]
    operands = [s["a"] for s in xs] + [s["a"] for s in ps] + [s["a"] for s in douts]
    if fwd:
        out_modes = [dict(mode="cols", split=sp) for (_, _, sp) in outs]
        out_shape = [jax.ShapeDtypeStruct((bsz, nt * tm, nc * w), dt) for (w, dt, _) in outs]
        out_specs = [pl.BlockSpec((None, tm, w), lambda c, b, t: (b, t, c)) for (w, _, _) in outs]
    else:
        dx_outs = [x_out(xs[i], dt) for i, dt in dx.items()]
        out_shape, out_specs = [o[0] for o in dx_outs], [o[1] for o in dx_outs]
        for j in dp:
            s = ps[j]
            r = s["a"].shape[-2]
            if s["e"]:
                out_shape.append(jax.ShapeDtypeStruct((bsz, r, nc * s["w"]), F32))
                out_specs.append(pl.BlockSpec((None, r, s["w"]), lambda c, b, t: (b, 0, c)))
            else:
                out_shape.append(jax.ShapeDtypeStruct((r, nc * s["w"]), F32))
                out_specs.append(pl.BlockSpec((r, s["w"]), lambda c, b, t: (0, c)))

    def body(*refs):
        x_refs, p_refs = refs[:nx], refs[nx:nx + np_]
        d_refs = refs[nx + np_:nx + np_ + len(douts)]
        o_refs = refs[nx + np_ + len(douts):]
        xv = [[p.astype(F32) for p in _pieces(r, s)] for r, s in zip(x_refs, xs)]
        pv = [[p.astype(F32) for p in _pieces(r, s)] for r, s in zip(p_refs, ps)]
        if fwd:
            for r, pieces, s in zip(o_refs, f(xv, pv), out_modes):
                _store(r, pieces, s)
            return
        _, vjp = jax.vjp(f, xv, pv)
        cot = [[p.astype(F32) for p in _pieces(r, s)] for r, s in zip(d_refs, douts)]
        dxv, dpv = vjp(cot)
        for r, i in zip(o_refs, dx):
            _store(r, dxv[i], xs[i])
        b, t = pl.program_id(1), pl.program_id(2)
        for r, j in zip(o_refs[len(dx):], dp):
            first = (t == 0) if ps[j]["e"] else jnp.logical_and(b == 0, t == 0)

            @pl.when(first)
            def _(r=r, j=j):
                _store(r, dpv[j], ps[j])

            @pl.when(jnp.logical_not(first))
            def _(r=r, j=j):
                _store(r, dpv[j], ps[j], accumulate=True)

    res = pl.pallas_call(
        body, name=name, grid=(nc, bsz, nt), in_specs=in_specs, out_specs=out_specs, out_shape=out_shape,
        compiler_params=pltpu.CompilerParams(dimension_semantics=("arbitrary", "arbitrary", "arbitrary")),
    )(*operands)
    return res


def _keep_rows(a, shift, keep):
    n = a.shape[0]
    t = lax.broadcasted_iota(jnp.int32, a.shape, 0)
    return jnp.where(keep(t, n), pltpu.roll(a, shift % n, 0), 0.0)


def _shift_pair(step, keep_prev, keep_next):
    @jax.custom_vjp
    def prev(a):
        return _keep_rows(a, step, keep_prev)

    @jax.custom_vjp
    def nxt(a):
        return _keep_rows(a, -step, keep_next)

    prev.defvjp(lambda a: (prev(a), None), lambda _, g: (nxt(g),))
    nxt.defvjp(lambda a: (nxt(a), None), lambda _, g: (prev(g),))
    return prev, nxt


prev_tok, next_tok = _shift_pair(1, lambda t, n: t % GRID_W != 0, lambda t, n: t % GRID_W != GRID_W - 1)
prev_row, next_row = _shift_pair(GRID_W, lambda t, n: t >= GRID_W, lambda t, n: t < n - GRID_W)


@jax.custom_vjp
def bdot(a, w):
    return jnp.dot(a.astype(BF16), w.astype(BF16), preferred_element_type=F32)


def _bdot_bwd(res, g):
    a, w = res
    gb = g.astype(BF16)
    da = lax.dot_general(gb, w.astype(BF16), (((1,), (1,)), ((), ())), preferred_element_type=F32)
    dw = lax.dot_general(a.astype(BF16), gb, (((0,), (0,)), ((), ())), preferred_element_type=F32)
    return da, dw


bdot.defvjp(lambda a, w: (bdot(a, w), (a, w)), _bdot_bwd)


@jax.custom_vjp
def log_sigmoid(z):
    return jnp.minimum(z, 0.0) - jnp.log(1.0 + jnp.exp(-jnp.abs(z)))


def _lsig_bwd(z, g):
    e = jnp.exp(-jnp.abs(z))
    return (g * jnp.where(z >= 0, e, 1.0) / (1.0 + e),)


log_sigmoid.defvjp(lambda z: (log_sigmoid(z), z), _lsig_bwd)


def silu(x):
    return x * jax.nn.sigmoid(x)


def _rms(x):
    return x * lax.rsqrt(jnp.mean(x * x, axis=-1, keepdims=True) + EPS)


def _mod(x, gain, shift, scale):
    return _rms(x) * gain * (1.0 + scale) + shift


def f_mod(xs, ps):
    ((h,),), ((gain,), (shift,), (scale,)) = xs, ps
    return [[_mod(h, gain, shift, scale)], [h]]


def f_res_mod(xs, ps):
    ((h,), (y,)), ((gate,), (gain,), (shift,), (scale,)) = xs, ps
    h1 = h + gate * y
    return [[h1], [_mod(h1, gain, shift, scale)]]


def f_ffn_mid(xs, ps):
    ((ua, ug),), ((w0a, w0g), (w1a, w1g), (w2a, w2g), (ba, bg)) = xs, ps
    a = w0a * prev_row(ua) + w1a * ua + w2a * next_row(ua) + ba
    g = w0g * prev_row(ug) + w1g * ug + w2g * next_row(ug) + bg
    return [[a * silu(g)]]


def f_sc_mid(xs, ps):
    ((bg, cg, v),), ((w0,), (w1,), (w2,)) = xs, ps
    z = cg * v
    return [[bg * (w0 * prev_tok(z) + w1 * z + w2 * next_tok(z))]]


def f_decay(xs, ps):
    ((a,),), ((wd,), (bd,)) = xs, ps
    return [[log_sigmoid(bdot(a, wd) + bd) / TAU]]


def f_gla_post(xs, ps):
    (of, ob, g), ((gain,),) = xs, ps
    return [[_rms(a + b) * gain * silu(c) for a, b, c in zip(of, ob, g)]]


NCH = TT // CHUNK
CTX_CH = CTX // CHUNK
_NT = (((1,), (1,)), ((), ()))
_TN = (((0,), (0,)), ((), ()))
_NN = (((1,), (0,)), ((), ()))


def _chunk_of(d, j):
    return jnp.where(d == 0, j, jnp.where(j < CTX_CH, CTX_CH - 1 - j, NCH + CTX_CH - 1 - j))


def _dot(a, b, dn):
    return lax.dot_general(a, b, dn, preferred_element_type=F32)


def _cumsum_rows(g, suffix):
    n = g.shape[0]
    row = lax.broadcasted_iota(jnp.int32, g.shape, 0)
    s = 1
    while s < n:
        if suffix:
            g = g + jnp.where(row < n - s, pltpu.roll(g, n - s, 0), 0.0)
        else:
            g = g + jnp.where(row >= s, pltpu.roll(g, s, 0), 0.0)
        s *= 2
    return g


def _causal(backward):
    row = lax.broadcasted_iota(jnp.int32, (CHUNK, CHUNK), 0)
    col = lax.broadcasted_iota(jnp.int32, (CHUNK, CHUNK), 1)
    return col >= row if backward else col <= row


def _gla_in_specs(bsz, rev):
    def blk(d, j):
        return _chunk_of(d, (NCH - 1 - j) if rev else j)

    return [
        pl.BlockSpec((bsz, CHUNK, KD), lambda d, j: (0, blk(d, j), 0)),
        pl.BlockSpec((bsz, CHUNK, KD), lambda d, j: (0, blk(d, j), 1)),
        pl.BlockSpec((bsz, CHUNK, VD), lambda d, j: (0, blk(d, j), 1)),
        pl.BlockSpec((bsz, CHUNK, KD), lambda d, j: (0, blk(d, j), d)),
    ], blk


def gla_fwd(pcat, la):
    bsz = pcat.shape[0]
    in_specs, blk = _gla_in_specs(bsz, False)

    def body(q_ref, k_ref, v_ref, la_ref, o_ref, s_ref, st):
        d, j = pl.program_id(0), pl.program_id(1)

        @pl.when(j == 0)
        def _():
            st[...] = jnp.zeros_like(st)

        s_ref[...] = st[...]

        def scan(backward):
            causal = _causal(backward)
            for e in range(bsz):
                g_all = la_ref[e]
                b_all = _cumsum_rows(g_all, backward)
                bl_all = jnp.sum(g_all, axis=0, keepdims=True)
                qs_all = (q_ref[e].astype(F32) * (HK ** -0.5) * jnp.exp(b_all)).astype(BF16)
                ks_all = (k_ref[e] * jnp.exp(-b_all)).astype(BF16)
                kd_all = (k_ref[e] * jnp.exp(bl_all - b_all)).astype(BF16)
                el_all = jnp.exp(bl_all)
                for h in range(HEADS):
                    ks_, vs_ = slice(h * HK, (h + 1) * HK), slice(h * HV, (h + 1) * HV)
                    qs, ks, kd, v = qs_all[:, ks_], ks_all[:, ks_], kd_all[:, ks_], v_ref[e, :, vs_].astype(BF16)
                    s = st[e, h]
                    att = jnp.where(causal, _dot(qs, ks, _NT), 0.0).astype(BF16)
                    o_ref[e, :, vs_] = _dot(qs, s.astype(BF16), _NT) + _dot(att, v, _NN)
                    st[e, h] = el_all[:, ks_] * s + _dot(v, kd, _TN)

        @pl.when(d == 0)
        def _():
            scan(False)

        @pl.when(d == 1)
        def _():
            scan(True)

    return pl.pallas_call(
        body, name="gla_fwd", grid=(2, NCH), in_specs=in_specs,
        out_specs=[pl.BlockSpec((bsz, CHUNK, VD), lambda d, j: (0, blk(d, j), d)),
                   pl.BlockSpec((bsz, None, None, HEADS, HV, HK), lambda d, j: (0, d, j, 0, 0, 0))],
        out_shape=[jax.ShapeDtypeStruct((bsz, TT, 2 * VD), F32), jax.ShapeDtypeStruct((bsz, 2, NCH, HEADS, HV, HK), F32)],
        scratch_shapes=[pltpu.VMEM((bsz, HEADS, HV, HK), F32)],
        compiler_params=pltpu.CompilerParams(dimension_semantics=("arbitrary", "arbitrary")),
    )(pcat, pcat, pcat, la)


def gla_bwd(pcat, la, s_all, do):
    bsz = pcat.shape[0]
    in_specs, blk = _gla_in_specs(bsz, True)
    in_specs += [
        pl.BlockSpec((bsz, None, None, HEADS, HV, HK), lambda d, j: (0, d, NCH - 1 - j, 0, 0, 0)),
        pl.BlockSpec((bsz, CHUNK, VD), lambda d, j: (0, jnp.maximum(blk(d, j) - CTX_CH, 0), 0)),
    ]

    def body(q_ref, k_ref, v_ref, la_ref, s_ref, do_ref, dq_ref, dk_ref, dv_ref, dla_ref, dst):
        d, j = pl.program_id(0), pl.program_id(1)

        @pl.when(j == 0)
        def _():
            dst[...] = jnp.zeros_like(dst)

        latent = blk(d, j) >= CTX_CH
        scale = HK ** -0.5

        def scan(backward):
            causal = _causal(backward)
            for e in range(bsz):
                g_all = la_ref[e]
                b_all = _cumsum_rows(g_all, backward)
                bl_all = jnp.sum(g_all, axis=0, keepdims=True)
                ex_all, ei_all, ed_all, el_all = jnp.exp(b_all), jnp.exp(-b_all), jnp.exp(bl_all - b_all), jnp.exp(bl_all)
                qs_all, ks_all, kd_all = q_ref[e].astype(F32) * scale * ex_all, k_ref[e] * ei_all, k_ref[e] * ed_all
                qsb_all, ksb_all, kdb_all = qs_all.astype(BF16), ks_all.astype(BF16), kd_all.astype(BF16)
                db_parts, dbl_parts = [], []
                for h in range(HEADS):
                    ks_, vs_ = slice(h * HK, (h + 1) * HK), slice(h * HV, (h + 1) * HV)
                    qs, ks, kd, el = qs_all[:, ks_], ks_all[:, ks_], kd_all[:, ks_], el_all[:, ks_]
                    qsb, ksb, kdb, v = qsb_all[:, ks_], ksb_all[:, ks_], kdb_all[:, ks_], v_ref[e, :, vs_].astype(BF16)
                    s, ds1 = s_ref[e, h], dst[e, h]
                    sb, ds1b = s.astype(BF16), ds1.astype(BF16)
                    dob = jnp.where(latent, do_ref[e, :, vs_], 0.0).astype(BF16)
                    att = jnp.where(causal, _dot(qsb, ksb, _NT), 0.0).astype(BF16)
                    datt = jnp.where(causal, _dot(dob, v, _NT), 0.0).astype(BF16)
                    dqs = _dot(dob, sb, _NN) + _dot(datt, ksb, _NN)
                    dks = _dot(datt, qsb, _TN)
                    dv_ref[e, :, vs_] = (_dot(att, dob, _TN) + _dot(kdb, ds1b, _NT)).astype(BF16)
                    dkd = _dot(v, ds1b, _NN)
                    dst[e, h] = _dot(dob, qsb, _TN) + el * ds1
                    del_ = jnp.sum(s * ds1, axis=0, keepdims=True)
                    dq_ref[e, :, ks_] = (dqs * ex_all[:, ks_] * scale).astype(BF16)
                    dk_ref[e, :, ks_] = (dks * ei_all[:, ks_] + dkd * ed_all[:, ks_]).astype(BF16)
                    db_parts.append(dqs * qs - dks * ks - dkd * kd)
                    dbl_parts.append(jnp.sum(dkd * kd, axis=0, keepdims=True) + del_ * el)
                dla_ref[e] = _cumsum_rows(jnp.concatenate(db_parts, -1), not backward) + jnp.concatenate(dbl_parts, -1)

        @pl.when(d == 0)
        def _():
            scan(False)

        @pl.when(d == 1)
        def _():
            scan(True)

    return pl.pallas_call(
        body, name="gla_bwd", grid=(2, NCH), in_specs=in_specs,
        out_specs=[pl.BlockSpec((None, bsz, CHUNK, KD), lambda d, j: (d, 0, blk(d, j), 0)),
                   pl.BlockSpec((None, bsz, CHUNK, KD), lambda d, j: (d, 0, blk(d, j), 0)),
                   pl.BlockSpec((None, bsz, CHUNK, VD), lambda d, j: (d, 0, blk(d, j), 0)),
                   pl.BlockSpec((bsz, CHUNK, KD), lambda d, j: (0, blk(d, j), d))],
        out_shape=[jax.ShapeDtypeStruct((2, bsz, TT, KD), BF16), jax.ShapeDtypeStruct((2, bsz, TT, KD), BF16),
                   jax.ShapeDtypeStruct((2, bsz, TT, VD), BF16), jax.ShapeDtypeStruct((bsz, TT, 2 * KD), F32)],
        scratch_shapes=[pltpu.VMEM((bsz, HEADS, HV, HK), F32)],
        compiler_params=pltpu.CompilerParams(dimension_semantics=("arbitrary", "arbitrary")),
    )(pcat, pcat, pcat, la, s_all, do)


def gla_combine(dq2, dk2, dv2, dgate, dpa):
    bsz = dgate.shape[0]
    tm = CTX

    def body(dq_ref, dk_ref, dv_ref, dg_ref, dpa_ref, o_ref):
        t = pl.program_id(1)
        o_ref[:, 0:KD] = (dq_ref[0].astype(F32) + dq_ref[1].astype(F32)).astype(BF16)
        o_ref[:, KD:2 * KD] = (dk_ref[0].astype(F32) + dk_ref[1].astype(F32)).astype(BF16)
        o_ref[:, 2 * KD:2 * KD + VD] = (dv_ref[0].astype(F32) + dv_ref[1].astype(F32)).astype(BF16)
        o_ref[:, 2 * KD + VD:2 * KD + 2 * VD] = jnp.where(t > 0, dg_ref[...], 0).astype(BF16)
        o_ref[:, 2 * KD + 2 * VD:] = dpa_ref[...].astype(BF16)

    return pl.pallas_call(
        body, name="gla_combine", grid=(bsz, TT // tm),
        in_specs=[pl.BlockSpec((2, None, tm, KD), lambda b, t: (0, b, t, 0)),
                  pl.BlockSpec((2, None, tm, KD), lambda b, t: (0, b, t, 0)),
                  pl.BlockSpec((2, None, tm, VD), lambda b, t: (0, b, t, 0)),
                  pl.BlockSpec((None, tm, VD), lambda b, t: (b, jnp.maximum(t - 1, 0), 0)),
                  pl.BlockSpec((None, tm, 128), lambda b, t: (b, t, 0))],
        out_specs=pl.BlockSpec((None, tm, GLA_IN_PAD), lambda b, t: (b, t, 0)),
        out_shape=jax.ShapeDtypeStruct((bsz, TT, GLA_IN_PAD), BF16),
        compiler_params=pltpu.CompilerParams(dimension_semantics=("arbitrary", "arbitrary")),
    )(dq2, dk2, dv2, dgate, dpa)


def final_loss(h1, fo, gate, gain, tgt):
    bsz, t_len, _ = h1.shape
    tm = 256

    def body(h_ref, f_ref, gate_ref, gain_ref, tgt_ref, loss_ref, dh_ref, df_ref, dgate_ref, dgain_ref):
        b, t = pl.program_id(0), pl.program_id(1)
        target = tgt_ref[...]

        def core(h, fo_, gate_, gain_):
            e = _rms(h + gate_ * fo_) * gain_ - target
            return jnp.sum(0.5 * jnp.sum(e * e, axis=-1, keepdims=True) / D, axis=0, keepdims=True)

        loss, vjp = jax.vjp(core, h_ref[...], f_ref[...], gate_ref[...], gain_ref[...])
        dh, df, dgate, dgain = vjp(jnp.ones((1, 1), F32))
        dh_ref[...] = dh
        df_ref[...] = df.astype(BF16)
        first = jnp.logical_and(b == 0, t == 0)

        @pl.when(first)
        def _():
            loss_ref[...] = jnp.broadcast_to(loss, loss_ref.shape)
            dgain_ref[...] = dgain

        @pl.when(jnp.logical_not(first))
        def _():
            loss_ref[...] += jnp.broadcast_to(loss, loss_ref.shape)
            dgain_ref[...] += dgain

        @pl.when(t == 0)
        def _():
            dgate_ref[...] = dgate

        @pl.when(t > 0)
        def _():
            dgate_ref[...] += dgate

    tile = pl.BlockSpec((None, tm, D), lambda b, t: (b, t, 0))
    per_ex = pl.BlockSpec((None, 1, D), lambda b, t: (b, 0, 0))
    shared = pl.BlockSpec((1, D), lambda b, t: (0, 0))
    return pl.pallas_call(
        body, name="final_loss", grid=(bsz, t_len // tm),
        in_specs=[tile, tile, per_ex, shared, tile],
        out_specs=[pl.BlockSpec((8, 128), lambda b, t: (0, 0)), tile, tile, per_ex, shared],
        out_shape=[jax.ShapeDtypeStruct((8, 128), F32), jax.ShapeDtypeStruct(h1.shape, F32),
                   jax.ShapeDtypeStruct(h1.shape, BF16), jax.ShapeDtypeStruct((bsz, 1, D), F32),
                   jax.ShapeDtypeStruct((1, D), F32)],
        compiler_params=pltpu.CompilerParams(dimension_semantics=("arbitrary", "arbitrary")),
    )(h1, fo, gate, gain, tgt)


ADA_ROWS = 24
ADA_CTX_ROW = 16
ADA_COLS = 6 * D // N_DEV


def ada_fwd(cond, w, b):
    def body(c_ref, w_ref, b_ref, o_ref):
        s = silu(c_ref[...]).astype(BF16)
        o_ref[...] = jnp.dot(s, w_ref[...].astype(BF16), preferred_element_type=F32) + b_ref[...]

    return pl.pallas_call(
        body, name="ada_fwd", grid=(2,),
        in_specs=[pl.BlockSpec((ADA_ROWS, D), lambda i: (0, 0)), pl.BlockSpec((None, D, ADA_COLS), lambda i: (i, 0, 0)),
                  pl.BlockSpec((None, 1, ADA_COLS), lambda i: (i, 0, 0))],
        out_specs=pl.BlockSpec((None, ADA_ROWS, ADA_COLS), lambda i: (i, 0, 0)),
        out_shape=jax.ShapeDtypeStruct((2, ADA_ROWS, ADA_COLS), F32),
    )(cond, w, b)


def ada_bwd(cond, dm_mine, dm_full, w):
    def body(c_ref, dm_ref, dmf_ref, w_ref, gw_ref, gb_ref, cp_ref):
        i = pl.program_id(0)
        s = silu(c_ref[...]).astype(BF16)
        dm = dm_ref[...].astype(BF16)
        gw_ref[...] = _dot(s, dm, _TN)
        gb_ref[...] = jnp.sum(dmf_ref[...], axis=0, keepdims=True)

        @pl.when(i == 0)
        def _():
            cp_ref[...] = _dot(dm_ref[ADA_CTX_ROW:, :].astype(BF16), w_ref[...].astype(BF16), _NT)

    return pl.pallas_call(
        body, name="ada_bwd", grid=(2,),
        in_specs=[pl.BlockSpec((ADA_ROWS, D), lambda i: (0, 0)), pl.BlockSpec((None, ADA_ROWS, ADA_COLS), lambda i: (i, 0, 0)),
                  pl.BlockSpec((None, ADA_ROWS, 6 * D), lambda i: (i, 0, 0)), pl.BlockSpec((None, D, ADA_COLS), lambda i: (i, 0, 0))],
        out_specs=[pl.BlockSpec((None, D, ADA_COLS), lambda i: (i, 0, 0)), pl.BlockSpec((None, 1, 6 * D), lambda i: (i, 0, 0)),
                   pl.BlockSpec((ADA_ROWS - ADA_CTX_ROW, D), lambda i: (0, 0))],
        out_shape=[jax.ShapeDtypeStruct((2, D, ADA_COLS), F32), jax.ShapeDtypeStruct((2, 1, 6 * D), F32),
                   jax.ShapeDtypeStruct((ADA_ROWS - ADA_CTX_ROW, D), F32)],
        compiler_params=pltpu.CompilerParams(dimension_semantics=("arbitrary",)),
    )(cond, dm_mine, dm_full, w)


def cctx_grad(parts, c_ctx):
    def body(p_ref, c_ref, o_ref):
        tot = p_ref[0:1, :]
        for i in range(1, N_DEV):
            tot = tot + p_ref[i:i + 1, :]
        c = c_ref[...]
        sg = jax.nn.sigmoid(c)
        o_ref[...] = tot * sg * (1.0 + c * (1.0 - sg))

    return pl.pallas_call(body, name="cctx_grad", out_shape=jax.ShapeDtypeStruct((1, D), F32))(parts, c_ctx)


def _row_tile(r):
    for t in (512, 256, 128, 80, 64, 40, 32, 16, 8):
        if r % t == 0:
            return t
    return r


def _slot_sum(ref):
    tot = ref[0].astype(F32)
    for i in range(1, ref.shape[0]):
        tot = tot + ref[i].astype(F32)
    return tot


def sum_slots(name, x):
    s, r, c = x.shape
    tr = _row_tile(r)

    def body(x_ref, o_ref):
        o_ref[...] = _slot_sum(x_ref)

    return pl.pallas_call(
        body, name=name, grid=(r // tr,), in_specs=[pl.BlockSpec((s, tr, c), lambda i: (0, i, 0))],
        out_specs=pl.BlockSpec((tr, c), lambda i: (i, 0)), out_shape=jax.ShapeDtypeStruct((r, c), F32),
    )(x)


def adamw(name, w, g, m, v, layer=None):
    r, c = w.shape[-2:]
    tr = _row_tile(r)
    stacked = g.ndim == 3

    def body(w_ref, g_ref, m_ref, v_ref, go_ref, d_ref, mo_ref, vo_ref):
        gv = _slot_sum(g_ref) if stacked else g_ref[...]
        mn = B1 * m_ref[...] + (1.0 - B1) * gv
        vn = B2 * v_ref[...] + (1.0 - B2) * jnp.square(gv)
        m_hat = mn / (1.0 - B1 ** STEP)
        v_hat = vn / (1.0 - B2 ** STEP)
        go_ref[...] = gv
        d_ref[...] = -LR * (m_hat / (jnp.sqrt(v_hat) + AEPS) + WD * w_ref[...])
        mo_ref[...] = mn
        vo_ref[...] = vn

    tile = pl.BlockSpec((tr, c), lambda i: (i, 0))
    slab = tile if layer is None else pl.BlockSpec((None, tr, c), lambda i: (layer, i, 0))
    g_spec = pl.BlockSpec((g.shape[0], tr, c), lambda i: (0, i, 0)) if stacked else tile
    return pl.pallas_call(
        body, name=name, grid=(r // tr,), in_specs=[slab, g_spec, slab, slab], out_specs=[tile] * 4,
        out_shape=[jax.ShapeDtypeStruct((r, c), F32)] * 4,
    )(w, g, m, v)


def _place():
    return lax.axis_index("x"), lax.axis_index("y"), lax.axis_index("c")


def all_gather(name, x, in_vmem):
    r, c = x.shape
    space = pltpu.VMEM if in_vmem else pl.ANY

    def body(x_ref, out_ref, send_sems, recv_sems, local_sem):
        px, py, pc = _place()
        me, sibling = (px, py, pc), (px, py, 1 - pc)
        chips = [(1 - px, py), (px, 1 - py), (1 - px, 1 - py)]

        def rows(qx, qy, qc):
            return out_ref.at[pl.ds((4 * qx + 2 * qy + qc) * r, r), :]

        def copy(k, block, to, src=None):
            return pltpu.make_async_remote_copy(
                src_ref=rows(*block) if src is None else src, dst_ref=rows(*block),
                send_sem=send_sems.at[k], recv_sem=recv_sems.at[k], device_id=to, device_id_type=MESH)

        mine = pltpu.make_async_copy(x_ref, rows(*me), local_sem)
        mine.start()
        first = [copy(0, me, sibling, src=x_ref)]
        first += [copy(1 + j, me, (*chip, pc), src=x_ref) for j, chip in enumerate(chips)]
        for cp in first:
            cp.start()
        passed = [copy(4 + j, (*chip, pc), sibling) for j, chip in enumerate(chips)]
        for j, chip in enumerate(chips):
            copy(1 + j, (*chip, pc), me).wait_recv()
            passed[j].start()
        copy(0, sibling, me).wait_recv()
        for j, chip in enumerate(chips):
            copy(4 + j, (*chip, 1 - pc), me).wait_recv()
        for cp in first + passed:
            cp.wait_send()
        mine.wait()

    return pl.pallas_call(
        body, name=name, out_shape=jax.ShapeDtypeStruct((N_DEV * r, c), x.dtype),
        in_specs=[pl.BlockSpec(memory_space=space)], out_specs=pl.BlockSpec(memory_space=space),
        scratch_shapes=[pltpu.SemaphoreType.DMA((7,)), pltpu.SemaphoreType.DMA((7,)), pltpu.SemaphoreType.DMA],
    )(x)


_HBM =pl.BlockSpec(memory_space=pltpu.HBM)
_SEM = pl.BlockSpec(memory_space=pltpu.SEMAPHORE)
_EFFECT = pltpu.SideEffectType.DATAFLOW_SIDE_EFFECTING


def _peers():
    px, py, pc = _place()
    return [(1 - px if k & 4 else px, 1 - py if k & 2 else py, 1 - pc if k & 1 else pc) for k in range(1, N_DEV)]


def _slot(dev):
    return 4 * dev[0] + 2 * dev[1] + dev[2]


def _split_copies(src_refs, land_refs, send_sems, recv_sems, gather):
    me = _slot(_place())
    return [pltpu.make_async_remote_copy(
        src_ref=src if gather else src.at[_slot(peer)], dst_ref=land.at[me],
        send_sem=send_sems.at[a * (N_DEV - 1) + k], recv_sem=recv_sems.at[a * (N_DEV - 1) + k],
        device_id=peer, device_id_type=MESH)
        for a, (src, land) in enumerate(zip(src_refs, land_refs)) for k, peer in enumerate(_peers())]


def exchange_start(name, srcs, gather, after):
    n = len(srcs)
    lands = [pltpu.HBM((N_DEV,) + s.shape if gather else s.shape, s.dtype) for s in srcs]

    def body(*refs):
        send_sems, recv_sems = refs[2 * n + 1:2 * n + 3]
        for cp in _split_copies(refs[:n], refs[n:2 * n], send_sems, recv_sems, gather):
            cp.start()
        refs[-1][...] = jnp.zeros_like(refs[-1])

    sems = pltpu.SemaphoreType.DMA((n * (N_DEV - 1),))
    res = pl.pallas_call(
        body, name=name,
        out_shape=(sems, sems, *[pltpu.HBM(s.shape, s.dtype) for s in srcs], *lands, jax.ShapeDtypeStruct((8, 128), F32)),
        in_specs=(_HBM,) * (2 * n) + (pl.BlockSpec(memory_space=pl.ANY),),
        out_specs=(_SEM, _SEM) + (_HBM,) * (2 * n) + (pl.BlockSpec(memory_space=pltpu.VMEM),),
        input_output_aliases={i: 2 + i for i in range(2 * n)},
        compiler_params=pltpu.CompilerParams(has_side_effects=_EFFECT),
    )(*[pltpu.with_memory_space_constraint(s, pltpu.HBM) for s in srcs],
      *[pltpu.with_memory_space_constraint(lax.empty(ld.shape, ld.dtype), pltpu.HBM) for ld in lands], after)
    return res[0], res[1], list(res[2:2 + n]), list(res[2 + n:2 + 2 * n]), res[-1]


def exchange_wait(name, started, after, gather):
    send_sems, recv_sems, srcs, lands, _ = started
    n = len(srcs)
    after = list(after) if isinstance(after, (list, tuple)) else [after]

    def body(*refs):
        send_sems, recv_sems = refs[2 * n:2 * n + 2]
        for cp in _split_copies(refs[:n], refs[n:2 * n], send_sems, recv_sems, gather):
            cp.wait_send()
            cp.wait_recv()

    res = pl.pallas_call(
        body, name=name, out_shape=tuple(pltpu.HBM(a.shape, a.dtype) for a in srcs + lands),
        in_specs=(_HBM,) * (2 * n) + (_SEM, _SEM) + (pl.BlockSpec(memory_space=pl.ANY),) * len(after),
        out_specs=(_HBM,) * (2 * n), input_output_aliases={i: i for i in range(2 * n)},
        compiler_params=pltpu.CompilerParams(has_side_effects=_EFFECT),
    )(*srcs, *lands, send_sems, recv_sems, *after)
    return list(res[:n]), list(res[n:])


NCF = FFN_H // FFN_TC


def _size(shape):
    n = 1
    for s in shape:
        n *= s
    return n


def _padded_rows(n_elems, row_mult):
    return -(-n_elems // (D * row_mult)) * row_mult


def _pack_rows(arrs, dtype, row_mult):
    rows, offs, r0 = [], [], 0
    for a in arrs:
        flat = a.reshape(-1).astype(dtype)
        n = _padded_rows(flat.shape[0], row_mult)
        rows.append(jnp.pad(flat, (0, n * D - flat.shape[0])).reshape(n, D))
        offs.append(r0)
        r0 += n
    return jnp.concatenate(rows, 0), offs


def _unpack_rows(buf, offs, shapes):
    lead, out = buf.shape[:-2], []
    for o, shp in zip(offs, shapes):
        n = _size(shp)
        nr = -(-n // D)
        out.append(buf[..., o:o + nr, :].reshape(lead + (nr * D,))[..., :n].reshape(lead + tuple(shp)))
    return out


def _rows3(w):
    return [w[i:i + 1] for i in range(3)]


def f_mod1(xs, ps):
    return f_mod(xs, ps)[:1]


def kernel(x, c, ctx, c_ctx, ada_w, ada_b, norm_mix, norm_ffn, gla_w_in, gla_w_a2, gla_b_a, gla_head_norm, gla_w_out, sc_w_in, sc_conv_w, sc_w_out, ffn_w_up, ffn_conv_w, ffn_conv_b, ffn_w_down, final_norm, loss_target, m_c_ctx, m_ada_w, m_ada_b, m_norm_mix, m_norm_ffn, m_gla_w_in, m_gla_w_a2, m_gla_b_a, m_gla_head_norm, m_gla_w_out, m_sc_w_in, m_sc_conv_w, m_sc_w_out, m_ffn_w_up, m_ffn_conv_w, m_ffn_conv_b, m_ffn_w_down, m_final_norm, v_c_ctx, v_ada_w, v_ada_b, v_norm_mix, v_norm_ffn, v_gla_w_in, v_gla_w_a2, v_gla_b_a, v_gla_head_norm, v_gla_w_out, v_sc_w_in, v_sc_conv_w, v_sc_w_out, v_ffn_w_up, v_ffn_conv_w, v_ffn_conv_b, v_ffn_w_down, v_final_norm):
    names = ["c_ctx", "ada_w", "ada_b", "norm_mix", "norm_ffn", "gla_w_in", "gla_w_a2", "gla_b_a", "gla_head_norm",
             "gla_w_out", "sc_w_in", "sc_conv_w", "sc_w_out", "ffn_w_up", "ffn_conv_w", "ffn_conv_b", "ffn_w_down",
             "final_norm"]
    w_ = dict(zip(names, [c_ctx, ada_w, ada_b, norm_mix, norm_ffn, gla_w_in, gla_w_a2, gla_b_a, gla_head_norm, gla_w_out,
                          sc_w_in, sc_conv_w, sc_w_out, ffn_w_up, ffn_conv_w, ffn_conv_b, ffn_w_down, final_norm]))
    m_ = dict(zip(names, [m_c_ctx, m_ada_w, m_ada_b, m_norm_mix, m_norm_ffn, m_gla_w_in, m_gla_w_a2, m_gla_b_a,
                          m_gla_head_norm, m_gla_w_out, m_sc_w_in, m_sc_conv_w, m_sc_w_out, m_ffn_w_up, m_ffn_conv_w,
                          m_ffn_conv_b, m_ffn_w_down, m_final_norm]))
    v_ = dict(zip(names, [v_c_ctx, v_ada_w, v_ada_b, v_norm_mix, v_norm_ffn, v_gla_w_in, v_gla_w_a2, v_gla_b_a,
                          v_gla_head_norm, v_gla_w_out, v_sc_w_in, v_sc_conv_w, v_sc_w_out, v_ffn_w_up, v_ffn_conv_w,
                          v_ffn_conv_b, v_ffn_w_down, v_final_norm]))
    me = 4 * lax.axis_index("x") + 2 * lax.axis_index("y") + lax.axis_index("c")
    bsz = x.shape[0]
    tm = 256
    nt = SEQ // tm
    ctx_tiles = CTX // tm
    pe = functools.partial(P, per_example=True)

    groups = {"ffn1": [("ffn_w_up", 1), ("ffn_w_down", 1)], "sc": [("sc_w_in", 0), ("sc_w_out", 0)],
              "ffn0": [("ffn_w_up", 0), ("ffn_w_down", 0)], "gla": [("gla_w_in", 0), ("gla_w_out", 0)]}
    ag_groups = {"gin": [("gla_w_in", 0)], "ffn0": [("gla_w_out", 0), ("ffn_w_up", 0), ("ffn_w_down", 0)],
                 "sc": groups["sc"], "ffn1": groups["ffn1"]}
    ag_started = {}

    def start_gather(g, after):
        ag_started[g] = exchange_start(f"ag_{g}_start", [w_[n][i].astype(BF16) for n, i in ag_groups[g]], True, after)
        return ag_started[g][4]

    small_sharded = [c, gla_w_a2, gla_b_a, sc_conv_w, ffn_conv_w]
    pack0, offs0 = _pack_rows(small_sharded, F32, 8)
    g0 = all_gather("ag_small", pack0, True).reshape(N_DEV, pack0.shape[0], D)
    c_all, wa2_s, ba_s, scw_s, fcw_s = _unpack_rows(g0, offs0, [a.shape for a in small_sharded])
    w_a2 = wa2_s[:, 0].transpose(1, 2, 0, 3).reshape(2, RANK, KD)
    b_a = ba_s[:, 0].transpose(1, 0, 2).reshape(2, KD)
    sc_cw = scw_s[:, 0].transpose(1, 0, 2).reshape(3, D)
    ffn_cw = fcw_s.transpose(1, 2, 0, 3).reshape(2, 3, 2 * FFN_H)

    cond = jnp.concatenate([c_all.reshape(N_DEV * bsz, D), c_ctx[None], jnp.zeros((ADA_ROWS - N_DEV * bsz - 1, D), F32)], 0)
    b_mine = lax.dynamic_slice(ada_b, (0, me * ADA_COLS), (2, ADA_COLS)).reshape(2, 1, ADA_COLS)
    mod_part = ada_fwd(cond, ada_w, b_mine)
    mod = all_gather("ag_mod", mod_part.reshape(2 * ADA_ROWS, ADA_COLS), True)
    mod = mod.reshape(N_DEV, 2, ADA_ROWS, ADA_COLS).transpose(1, 2, 0, 3).reshape(2, ADA_ROWS, 6 * D)
    mods = lax.dynamic_slice(mod, (0, bsz * me, 0), (2, bsz, 6 * D))
    md = [[mods[i][:, k * D:(k + 1) * D].reshape(bsz, 1, D) for k in range(6)] for i in range(2)]
    mc = [mod[0, ADA_CTX_ROW, k * D:(k + 1) * D][None] for k in range(2)]

    tok = mod
    for g in ag_groups:
        tok = start_gather(g, tok)
    norm_mix = norm_mix + tok[0, 0]

    def gathered(g, after):
        mine, lands = exchange_wait(f"ag_{g}_wait", ag_started[g], after, True)
        return [lax.dynamic_update_index_in_dim(ld, mn, me, 0) for ld, mn in zip(lands, mine)]

    s_up, w_down = [None, None], [None, None]
    wd = jnp.zeros((128, 2 * KD), F32).at[:RANK, :KD].set(w_a2[0]).at[RANK:2 * RANK, KD:].set(w_a2[1])
    bd = b_a.reshape(1, 2 * KD)
    scw = _rows3(sc_cw)
    head_gain = gla_head_norm.reshape(1, HV)
    gains_mix = [norm_mix[i][None] for i in range(2)]
    gains_ffn = [norm_ffn[i][None] for i in range(2)]

    def tokens(a2d, t_len):
        return a2d.reshape(bsz, t_len, -1)

    def ffn_params(i):
        rows = [ffn_cw[i][t] for t in range(3)] + [ffn_conv_b[i]]
        return [P(a.reshape(2, FFN_H), w=FFN_TC, rows=True) for a in rows]

    def ffn_fwd(i, hn2):
        u = mm(f"ffn_up{i}", V(hn2, "tok"), V(s_up[i], "cols"), out="planes", out_dtype=BF16, planes_t=SEQ)
        act = rowwise(f"ffn_mid{i}", f_ffn_mid, [X(u, w=FFN_TC, planes=True)], ffn_params(i), tm=SEQ, nt=1, nc=NCF,
                      outs=[(FFN_TC, BF16, 1)])[0]
        return u, act

    def arrays(ps):
        return [p["a"] for p in ps]

    ps_in0 = [P(gains_mix[0]), pe(md[0][0]), pe(md[0][1])]
    ps_ctx = [P(gains_mix[0]), P(mc[0]), P(mc[1])]
    hn0 = rowwise("mod_in0", f_mod, [X(x)], ps_in0, tm=tm, nt=nt, outs=[(D, BF16, 1)])[0]
    hnc = rowwise("mod_ctx", f_mod, [X(ctx)], ps_ctx, tm=tm, nt=ctx_tiles, outs=[(D, BF16, 1)])[0]
    hcat = jnp.concatenate([hnc, hn0], axis=1)
    (s_gin,) = gathered("gin", hcat)
    w_gin = V(s_gin, "cols", width=GLA_IN_PAD)
    pcat = tokens(mm("gla_in", V(hcat, "tok"), w_gin, out_dtype=BF16), TT)
    pa_x = X(pcat, w=128, co=(GLA_IN_PAD - 128) // 128)
    la = rowwise("gla_decay", f_decay, [pa_x], [P(wd), P(bd)], tm=tm, nt=TT // tm, outs=[(2 * KD, F32, 1)])[0]
    o2, s_all = gla_fwd(pcat, la)
    post_xs = [X(o2, w=VD, co=0, ro=ctx_tiles, split=HEADS), X(o2, w=VD, co=1, ro=ctx_tiles, split=HEADS),
               X(pcat, w=VD, co=2, ro=ctx_tiles, split=HEADS)]
    yin0 = rowwise("gla_post", f_gla_post, post_xs, [P(head_gain)], tm=tm, nt=nt, outs=[(VD, BF16, HEADS)])[0]
    s_gout, s_up[0], s_down0 = gathered("ffn0", yin0)
    w_gout, w_down[0] = s_gout.reshape(VD, D), s_down0.reshape(FFN_H, D)
    ps_mid0 = [pe(md[0][2]), P(gains_ffn[0]), pe(md[0][3]), pe(md[0][4])]
    y0, h1_0, hn2_0 = mm_res_mod("gla_out", yin0, w_gout, x, *arrays(ps_mid0))
    u0, act0 = ffn_fwd(0, hn2_0)
    ps_in1 = [pe(md[0][5]), P(gains_mix[1]), pe(md[1][0]), pe(md[1][1])]
    fo0, h2_0, hn1 = mm_res_mod("ffn_down0", act0, w_down[0], h1_0, *arrays(ps_in1))

    s_sin, s_sout = gathered("sc", hn1)
    w_sout = s_sout.reshape(D, D)
    p1 = tokens(mm("sc_in", V(hn1, "tok"), V(s_sin, "cols")), SEQ)
    sc_ps = [P(a) for a in scw]
    yin1 = rowwise("sc_mid", f_sc_mid, [X(p1, split=3)], sc_ps, tm=tm, nt=nt, outs=[(D, BF16, 1)])[0]
    ps_mid1 = [pe(md[1][2]), P(gains_ffn[1]), pe(md[1][3]), pe(md[1][4])]
    y1, h1_1, hn2_1 = mm_res_mod("sc_out", yin1, w_sout, h2_0, *arrays(ps_mid1))
    s_up[1], s_down1 = gathered("ffn1", hn2_1)
    w_down[1] = s_down1.reshape(FFN_H, D)
    u1, act1 = ffn_fwd(1, hn2_1)
    fo1 = tokens(mm("ffn_down1", V(act1, "tok"), V(w_down[1])), SEQ)
    loss8, dh1_1, dfo1, dm5_1, g_final = final_loss(h1_1, fo1, md[1][5], final_norm[None], loss_target)

    def ffn_bwd(i, u, act, hn2, dfo):
        dact = tokens(mm(f"ffn_down_dx{i}", V(dfo, "tok"), V(w_down[i]), form="nt", out_dtype=BF16), SEQ)
        g_down = mm(f"ffn_down_dw{i}", V(act, "tok"), V(dfo, "tok"), form="tn", out_dtype=BF16)
        r = rowwise(f"ffn_mid_bwd{i}", f_ffn_mid, [X(u, w=FFN_TC, planes=True)], ffn_params(i), tm=SEQ, nt=1, nc=NCF,
                    douts=[X(dact, w=FFN_TC)], dx={0: BF16}, dp=[0, 1, 2, 3])
        du, g_cw, g_cb = r[0], jnp.stack([a.reshape(2 * FFN_H) for a in r[1:4]]), r[4].reshape(1, 2 * FFN_H)
        dhn2 = tokens(mm(f"ffn_up_dx{i}", V(du, "planes"), V(s_up[i], "cols"), form="nt", out_dtype=BF16), SEQ)
        g_up = mm(f"ffn_up_dw{i}", V(hn2, "tok"), V(du, "planes"), form="tn", out="cols", out_dtype=BF16)
        return dhn2, g_up, row_slots(g_down), g_cw, g_cb

    def res_mod_bwd(name, h, y, ps, dh1, dhn):
        return rowwise(name, f_res_mod, [X(h), X(y)], ps, tm=tm, nt=nt, douts=[X(dh1), X(dhn)],
                       dx={0: F32, 1: BF16}, dp=[0, 1, 2, 3])

    def row_slots(g):
        return g.reshape(N_DEV, -1, g.shape[-1])

    a2a_started = {}

    def send_grads(g, slots, after=None):
        a2a_started[g] = exchange_start(f"a2a_{g}_start", list(slots), False, loss8 if after is None else after)
        return a2a_started[g][4][0, 0]

    def after_start(ps, tok):
        return [dict(ps[0], a=ps[0]["a"] + tok)] + ps[1:]

    dhn2_1, g_up1, g_down1, g_fcw1, g_fcb1 = ffn_bwd(1, u1, act1, hn2_1, dfo1)
    tok = send_grads("ffn1", [g_up1, g_down1])
    dh2_0, dy1, dm2_1, g_nffn1, dm3_1, dm4_1 = res_mod_bwd("res_mod_mid1_bwd", h2_0, y1, after_start(ps_mid1, tok), dh1_1, dhn2_1)
    dyin1 = tokens(mm("sc_out_dx", V(dy1, "tok"), V(w_sout), form="nt", out_dtype=BF16), SEQ)
    g_sout = row_slots(mm("sc_out_dw", V(yin1, "tok"), V(dy1, "tok"), form="tn", out_dtype=BF16))
    r = rowwise("sc_mid_bwd", f_sc_mid, [X(p1, split=3)], sc_ps, tm=tm, nt=nt, douts=[X(dyin1)], dx={0: BF16}, dp=[0, 1, 2])
    dp1, g_scw = r[0], jnp.concatenate(r[1:4], 0)
    dhn1 = tokens(mm("sc_in_dx", V(dp1, "tok"), V(s_sin, "cols"), form="nt", out_dtype=BF16), SEQ)
    g_sin = mm("sc_in_dw", V(hn1, "tok"), V(dp1, "tok"), form="tn", out="cols", out_dtype=BF16)
    tok = send_grads("sc", [g_sin, g_sout])
    dh1_0, dfo0, dm5_0, g_nmix1, dm0_1, dm1_1 = res_mod_bwd("res_mod_in1_bwd", h1_0, fo0, after_start(ps_in1, tok), dh2_0, dhn1)

    dhn2_0, g_up0, g_down0, g_fcw0, g_fcb0 = ffn_bwd(0, u0, act0, hn2_0, dfo0)
    tok = send_grads("ffn0", [g_up0, g_down0])
    dx_res, dy0, dm2_0, g_nffn0, dm3_0, dm4_0 = res_mod_bwd("res_mod_mid0_bwd", x, y0, after_start(ps_mid0, tok), dh1_0, dhn2_0)
    dyin0 = tokens(mm("gla_out_dx", V(dy0, "tok"), V(w_gout), form="nt", out_dtype=BF16), SEQ)
    do, dgate, g_head = rowwise("gla_post_bwd", f_gla_post, post_xs, [P(head_gain)], tm=tm, nt=nt,
                                douts=[X(dyin0, split=HEADS)], dx={0: BF16, 2: BF16}, dp=[0])
    dq2, dk2, dv2, dla = gla_bwd(pcat, la, s_all, do)
    dpa, g_wd, g_bd = rowwise("gla_decay_bwd", f_decay, [pa_x], [P(wd), P(bd)], tm=tm, nt=TT // tm, douts=[X(dla)],
                              dx={0: BF16}, dp=[0, 1])
    dpcat = gla_combine(dq2, dk2, dv2, dgate, dpa)
    dhcat = tokens(mm("gla_in_dx", V(dpcat, "tok"), w_gin, form="nt", out_dtype=BF16), TT)
    grad_x, g_nmix0, dm0_0, dm1_0 = rowwise("mod_in0_bwd", f_mod, [X(x)], ps_in0, tm=tm, nt=nt,
                                            douts=[X(dhcat, ro=ctx_tiles), X(dx_res)], dx={0: F32}, dp=[0, 1, 2])
    g_nmix0c, dmc0, dmc1 = rowwise("mod_ctx_bwd", f_mod1, [X(ctx)], ps_ctx, tm=tm, nt=ctx_tiles, douts=[X(dhcat)],
                                   dx={}, dp=[0, 1, 2])

    zero_row = jnp.zeros((1, 4 * D), F32)
    dmod = [jnp.concatenate([jnp.concatenate([a.reshape(bsz, D) for a in dms], 1), ctx_row], 0)
            for dms, ctx_row in (([dm0_0, dm1_0, dm2_0, dm3_0, dm4_0, dm5_0], jnp.concatenate([dmc0, dmc1, zero_row], 1)),
                                 ([dm0_1, dm1_1, dm2_1, dm3_1, dm4_1, dm5_1], jnp.zeros((1, 6 * D), F32)))]
    g_wa2 = jnp.stack([g_wd[:RANK, :KD], g_wd[RANK:2 * RANK, KD:]])
    small_grads = [jnp.stack(dmod), jnp.concatenate([g_nmix0 + g_nmix0c, g_nmix1], 0), jnp.concatenate([g_nffn0, g_nffn1], 0),
                   g_head, jnp.concatenate([g_fcb0, g_fcb1], 0), g_final, g_wa2, g_bd.reshape(2, KD), g_scw,
                   jnp.stack([g_fcw0, g_fcw1]), loss8[:1]]
    pack1, offs1 = _pack_rows(small_grads, F32, 8)
    ag1 = exchange_start("ag_grads_start", [pack1], True, loss8)
    g_gin = mm("gla_in_dw", V(hcat, "tok"), V(dpcat, "tok"), form="tn", out="cols", out_dtype=BF16, shard_n=GLA_IN // N_DEV,
               after=ag1[4])
    g_gout = row_slots(mm("gla_out_dw", V(yin0, "tok"), V(dy0, "tok"), form="tn", out_dtype=BF16, after=ag1[4]))
    mine1, land1 = exchange_wait("ag_grads_wait", ag1, [g_gin, g_gout], True)
    g1 = lax.dynamic_update_index_in_dim(land1[0], mine1[0], me, 0)
    dmod_all = _unpack_rows(g1, offs1[:1], [small_grads[0].shape])[0]
    tot = _unpack_rows(sum_slots("sum_small", g1), offs1, [a.shape for a in small_grads])
    loss = tot[10][0, 0]
    dm_rows = dmod_all[:, :, :bsz].transpose(1, 0, 2, 3).reshape(2, N_DEV * bsz, 6 * D)
    dm_full = jnp.concatenate([dm_rows, tot[0][:, bsz:], jnp.zeros((2, ADA_ROWS - N_DEV * bsz - 1, 6 * D), F32)], 1)
    dm_mine = lax.dynamic_slice(dm_full, (0, 0, me * ADA_COLS), (2, ADA_ROWS, ADA_COLS))
    g_ada_w, g_ada_b, cpart = ada_bwd(cond, dm_mine, dm_full, ada_w)
    cparts = all_gather("ag_cctx", cpart, True).reshape(N_DEV, ADA_ROWS - ADA_CTX_ROW, D)[:, 0]
    g_cctx = cctx_grad(cparts, c_ctx[None])[0]
    tok = send_grads("gla", [g_gin, g_gout], after=g_cctx)

    def my_cols(full, n):
        return lax.dynamic_slice_in_dim(full, me * n, n, axis=full.ndim - 1)

    grads = {
        "c_ctx": g_cctx, "ada_b": g_ada_b.reshape(2, 6 * D), "norm_mix": tot[1], "norm_ffn": tot[2],
        "gla_head_norm": tot[3], "ffn_conv_b": tot[4], "final_norm": tot[5].reshape(D),
        "gla_w_a2": my_cols(tot[6], KD // N_DEV)[None], "gla_b_a": my_cols(tot[7], KD // N_DEV)[None],
        "sc_conv_w": my_cols(tot[8], D // N_DEV)[None], "ffn_conv_w": my_cols(tot[9], 2 * FFN_H // N_DEV),
    }

    res_ada = adamw("adamw_ada", *[a.reshape(2 * D, ADA_COLS) for a in (ada_w, g_ada_w, m_ada_w, v_ada_w)])
    grads["c_ctx"] = g_cctx + tok
    big = ["gla_w_in", "gla_w_out", "sc_w_in", "sc_w_out", "ffn_w_up", "ffn_w_down"]
    small = [n for n in names if n not in big and n != "ada_w"]
    g_small = _pack_rows([grads[n] for n in small], F32, 8)[0]
    res_small = adamw("adamw_small", _pack_rows([w_[n] for n in small], F32, 8)[0], g_small,
                      _pack_rows([m_[n] for n in small], F32, 8)[0], _pack_rows([v_[n] for n in small], F32, 8)[0])
    offs_s = _pack_rows([w_[n] for n in small], F32, 8)[1]

    big_res, done = {}, [res_small[0], res_ada[0]]
    for g in groups:
        sent, lands = exchange_wait(f"a2a_{g}_wait", a2a_started[g], done, False)
        for (n, i), mine, land in zip(groups[g], sent, lands):
            land = lax.dynamic_update_index_in_dim(land, lax.dynamic_index_in_dim(mine, me, 0, keepdims=False), me, 0)
            big_res[(n, i)] = adamw(f"adamw_{n}{i}", w_[n], land, m_[n], v_[n], layer=i)
            done.append(big_res[(n, i)][0])

    out = {}
    for kind, idx in (("grad", 0), ("delta", 1), ("new_m", 2), ("new_v", 3)):
        vals = {n: jnp.stack([big_res[(n, i)][idx] for i in range(w_[n].shape[0])]) for n in big}
        vals["ada_w"] = res_ada[idx].reshape(ada_w.shape)
        vals.update(zip(small, _unpack_rows(res_small[idx], offs_s, [w_[n].shape for n in small])))
        out[kind] = [vals[n] for n in names]
    return (loss, grad_x, *out["grad"], *out["delta"], *out["new_m"], *out["new_v"])
```

```python
import functools

import jax
import jax.numpy as jnp
from jax import lax
from jax.experimental import pallas as pl
from jax.experimental.pallas import tpu as pltpu

F32 = jnp.float32
BF16 = jnp.bfloat16

N_DEV = 8
D = 1024
SEQ = 2048
CTX = 256
TT = CTX + SEQ
GRID_W = 64
CHUNK = 64
HEADS = 4
HK = 128
HV = 256
KD = 512
VD = 1024
RANK = 16
TAU = 16.0
GLA_IN = 3104
GLA_IN_PAD = 3200
FFN_H = 2560
FFN_TC = 256
EPS = 1e-6
LR, B1, B2, AEPS, WD, STEP = 0.001, 0.9, 0.999, 1e-08, 0.01, 10
MESH = pl.DeviceIdType.MESH


def _blocks(n):
    return [n] + [t for t in range(n - n % 128, 0, -128) if n % t == 0 and t != n]


def V(arr, kind="flat", width=None):
    if kind == "tok":
        return V(arr.reshape(-1, arr.shape[-1]))
    if kind == "flat":
        r, c = arr.shape
        return dict(a=arr, kind=kind, shape=(r, c), rows=_blocks(r), cols=_blocks(c))
    if kind == "planes":
        bsz, _, t, ch = arr.shape
        return dict(a=arr, kind=kind, shape=(bsz * t, 2 * ch), rows=_blocks(t), cols=[2 * ch] + _blocks(ch), t=t, ch=ch)
    _, r, n = arr.shape
    if width is not None:
        return dict(a=arr, kind=kind, shape=(r, width), rows=_blocks(r), cols=[width], n=n, pad=width - N_DEV * n)
    return dict(a=arr, kind=kind, shape=(r, N_DEV * n), rows=_blocks(r), cols=[8 * n, 4 * n, 2 * n], n=n, pad=0)


def _view_spec(v, br, bc, idx):
    if v["kind"] == "flat":
        return pl.BlockSpec((br, bc), idx)
    if v["kind"] == "planes":
        nt = v["t"] // br
        if bc == 2 * v["ch"]:
            return pl.BlockSpec((None, 2, br, v["ch"]), lambda i, j, k: (idx(i, j, k)[0] // nt, 0, idx(i, j, k)[0] % nt, 0))
        nch = v["ch"] // bc

        def at(i, j, k):
            r, c = idx(i, j, k)
            return r // nt, c // nch, r % nt, c % nch
        return pl.BlockSpec((None, None, br, bc), at)
    return pl.BlockSpec(((bc - v["pad"]) // v["n"], br, v["n"]), lambda i, j, k: (idx(i, j, k)[1], idx(i, j, k)[0], 0))


def _out_view(kind, rows, cols, dtype, planes_t=None, shard_n=None):
    if kind == "flat":
        shape = (rows, cols)
    elif kind == "planes":
        shape = (rows // planes_t, 2, planes_t, cols // 2)
    elif shard_n is not None:
        return V(jax.ShapeDtypeStruct((N_DEV, rows, shard_n), dtype), kind, width=cols)
    else:
        shape = (N_DEV, rows, cols // N_DEV)
    return V(jax.ShapeDtypeStruct(shape, dtype), kind)


MM_VMEM_BUDGET = 40 * 2 ** 20
MM_VMEM_LIMIT = 56 * 2 ** 20
MM_MAX_TILE = 1536


def _mm_tiles(m, n, kk, ms, ns, ks, a_bytes, b_bytes, o_bytes):
    best = None
    for tk in ks:
        for tm in [t for t in ms if t <= MM_MAX_TILE] or ms:
            for tn in [t for t in ns if t <= MM_MAX_TILE] or ns:
                one_k = tk == kk
                need = 2 * (tm * tk * a_bytes + tk * tn * b_bytes + tm * tn * o_bytes) + (0 if one_k else tm * tn * 4)
                if need > MM_VMEM_BUDGET:
                    continue
                steps = (m // tm) * (n // tn) * (kk // tk)
                traffic = (m * kk * a_bytes * (1 if one_k else n // tn)
                           + kk * n * b_bytes * (1 if one_k and n == tn else m // tm) + m * n * o_bytes)
                fill = (tm * tk * a_bytes + tk * tn * b_bytes) / 2.5e12
                cost = max(2.0 * m * n * kk / (9e14 if one_k else 6.5e14), traffic / 2.5e12) + steps * 0.4e-6 + fill
                if best is None or cost < best[0]:
                    best = (cost, tm, tn, tk)
    return best[1:]


def mm(name, a, b, form="nn", out="flat", out_dtype=F32, planes_t=None, shard_n=None, after=None):
    (m, kk) = a["shape"][::-1] if form == "tn" else a["shape"]
    n = b["shape"][0] if form == "nt" else b["shape"][1]
    assert (b["shape"][1] if form == "nt" else b["shape"][0]) == kk, (name, a["shape"], b["shape"])
    o = _out_view(out, m, n, out_dtype, planes_t, shard_n)
    a_m, a_k = (a["cols"], a["rows"]) if form == "tn" else (a["rows"], a["cols"])
    b_k, b_n = (b["cols"], b["rows"]) if form == "nt" else (b["rows"], b["cols"])
    tm, tn, tk = _mm_tiles(m, n, kk, [t for t in a_m if t in o["rows"]], [t for t in b_n if t in o["cols"]],
                           [t for t in a_k if t in b_k], a["a"].dtype.itemsize, b["a"].dtype.itemsize,
                           jnp.dtype(out_dtype).itemsize)
    nk = kk // tk
    dn = (((0 if form == "tn" else 1,), (1 if form == "nt" else 0,)), ((), ()))

    def load(ref, v):
        if len(ref.shape) == 3:
            pieces = [ref[p].astype(BF16) for p in range(ref.shape[0])]
            if v.get("pad"):
                pieces.append(jnp.zeros(ref.shape[1:2] + (v["pad"],), BF16))
            return jnp.concatenate(pieces, axis=-1)
        return ref[...].astype(BF16)

    def store(o_ref, val):
        val = val.astype(out_dtype)
        if len(o_ref.shape) == 3:
            w = o_ref.shape[-1]
            for p in range(o_ref.shape[0]):
                o_ref[p] = val[:, p * w:(p + 1) * w]
        else:
            o_ref[...] = val

    def body(a_ref, b_ref, *rest):
        o_ref, acc = rest[0 if after is None else 1], rest[1 if after is None else 2:]
        if nk == 1:
            store(o_ref, lax.dot_general(load(a_ref, a), load(b_ref, b), dn, preferred_element_type=F32))
            return
        k, acc_ref = pl.program_id(2), acc[0]

        @pl.when(k == 0)
        def _():
            acc_ref[...] = jnp.zeros_like(acc_ref)

        acc_ref[...] += lax.dot_general(load(a_ref, a), load(b_ref, b), dn, preferred_element_type=F32)

        @pl.when(k == nk - 1)
        def _():
            store(o_ref, acc_ref[...])

    if form == "tn":
        a_spec = _view_spec(a, tk, tm, lambda i, j, k: (k, i))
    else:
        a_spec = _view_spec(a, tm, tk, lambda i, j, k: (i, k))
    if form == "nt":
        b_spec = _view_spec(b, tn, tk, lambda i, j, k: (j, k))
    else:
        b_spec = _view_spec(b, tk, tn, lambda i, j, k: (k, j))
    return pl.pallas_call(
        body, name=name, grid=(m // tm, n // tn, nk),
        in_specs=[a_spec, b_spec] + ([] if after is None else [pl.BlockSpec(memory_space=pl.ANY)]),
        out_specs=_view_spec(o, tm, tn, lambda i, j, k: (i, j)), out_shape=o["a"],
        scratch_shapes=[pltpu.VMEM((tm, tn), F32)] if nk > 1 else [],
        compiler_params=pltpu.CompilerParams(dimension_semantics=("parallel", "parallel", "arbitrary"),
                                             vmem_limit_bytes=MM_VMEM_LIMIT),
    )(a["a"], b["a"], *([] if after is None else [after]))


def mm_res_mod(name, a, w, h, gate, gain, shift, scale):
    bsz, t_len, kk = a.shape
    tm = 512
    per = t_len // tm

    def body(a_ref, w_ref, h_ref, gate_ref, gain_ref, shift_ref, scale_ref, y_ref, h1_ref, hn_ref):
        y = jnp.dot(a_ref[...].astype(BF16), w_ref[...].astype(BF16), preferred_element_type=F32)
        h1 = h_ref[...] + gate_ref[...] * y
        y_ref[...] = y.astype(BF16)
        h1_ref[...] = h1
        hn_ref[...] = _mod(h1, gain_ref[...], shift_ref[...], scale_ref[...]).astype(BF16)

    def tile(width):
        return pl.BlockSpec((None, tm, width), lambda i: (i // per, i % per, 0))

    per_ex = pl.BlockSpec((None, 1, D), lambda i: (i // per, 0, 0))
    return pl.pallas_call(
        body, name=name, grid=(bsz * per,),
        in_specs=[tile(kk), pl.BlockSpec((kk, D), lambda i: (0, 0)), tile(D), per_ex, pl.BlockSpec((1, D), lambda i: (0, 0)),
                  per_ex, per_ex],
        out_specs=[tile(D)] * 3,
        out_shape=[jax.ShapeDtypeStruct((bsz, t_len, D), BF16), jax.ShapeDtypeStruct((bsz, t_len, D), F32),
                   jax.ShapeDtypeStruct((bsz, t_len, D), BF16)],
        compiler_params=pltpu.CompilerParams(dimension_semantics=("parallel",), vmem_limit_bytes=MM_VMEM_LIMIT),
    )(a, w, h, gate, gain, shift, scale)


def X(arr, w=None, co=0, ro=0, split=1, planes=False):
    return dict(a=arr, w=arr.shape[-1] if w is None else w, co=co, ro=ro, split=2 if planes else split,
                mode="planes" if planes else "cols")


def P(arr, per_example=False, w=None, split=1, rows=False):
    return dict(a=arr, e=per_example, w=arr.shape[-1] if w is None else w, split=arr.shape[-2] if rows else split,
                mode="rows" if rows else "cols")


def _pieces(ref, s):
    if s["mode"] == "planes":
        return [ref[0], ref[1]]
    if s["mode"] == "rows":
        return [ref[i:i + 1, :] for i in range(s["split"])]
    w = ref.shape[-1] // s["split"]
    return [ref[:, i * w:(i + 1) * w] for i in range(s["split"])]


def _store(ref, pieces, s, accumulate=False):
    w = ref.shape[-1] // len(pieces)
    for i, p in enumerate(pieces):
        at = (i,) if s["mode"] == "planes" else (slice(i, i + 1),) if s["mode"] == "rows" else (slice(None), slice(i * w, (i + 1) * w))
        if accumulate:
            ref[at] += p.astype(ref.dtype)
        else:
            ref[at] = p.astype(ref.dtype)


def rowwise(name, f, xs, ps, *, tm, nt, nc=1, outs=None, douts=None, dx=None, dp=None):
    bsz = xs[0]["a"].shape[0]
    fwd = douts is None
    nx, np_ = len(xs), len(ps)
    douts = [] if fwd else douts
    dx = {} if fwd else dx
    dp = [] if fwd else dp

    def x_spec(s):
        if s["mode"] == "planes":
            return pl.BlockSpec((None, 2, tm, s["w"]), lambda c, b, t, s=s: (b, 0, t + s["ro"], c + s["co"]))
        return pl.BlockSpec((None, tm, s["w"]), lambda c, b, t, s=s: (b, t + s["ro"], c + s["co"]))

    def x_out(s, dt):
        if s["mode"] == "planes":
            return (jax.ShapeDtypeStruct((bsz, 2, nt * tm, nc * s["w"]), dt),
                    pl.BlockSpec((None, 2, tm, s["w"]), lambda c, b, t: (b, 0, t, c)))
        return (jax.ShapeDtypeStruct((bsz, nt * tm, nc * s["w"]), dt), pl.BlockSpec((None, tm, s["w"]), lambda c, b, t: (b, t, c)))

    def p_spec(s):
        r = s["a"].shape[-2]
        if s["e"]:
            return pl.BlockSpec((None, r, s["w"]), lambda c, b, t: (b, 0, c))
        return pl.BlockSpec((r, s["w"]), lambda c, b, t: (0, c))

    in_specs = [x_spec(s) for s in xs] + [p_spec(s) for s in ps] + [x_spec(s) for s in douts]
    operands = [s["a"] for s in xs] + [s["a"] for s in ps] + [s["a"] for s in douts]
    if fwd:
        out_modes = [dict(mode="cols", split=sp) for (_, _, sp) in outs]
        out_shape = [jax.ShapeDtypeStruct((bsz, nt * tm, nc * w), dt) for (w, dt, _) in outs]
        out_specs = [pl.BlockSpec((None, tm, w), lambda c, b, t: (b, t, c)) for (w, _, _) in outs]
    else:
        dx_outs = [x_out(xs[i], dt) for i, dt in dx.items()]
        out_shape, out_specs = [o[0] for o in dx_outs], [o[1] for o in dx_outs]
        for j in dp:
            s = ps[j]
            r = s["a"].shape[-2]
            if s["e"]:
                out_shape.append(jax.ShapeDtypeStruct((bsz, r, nc * s["w"]), F32))
                out_specs.append(pl.BlockSpec((None, r, s["w"]), lambda c, b, t: (b, 0, c)))
            else:
                out_shape.append(jax.ShapeDtypeStruct((r, nc * s["w"]), F32))
                out_specs.append(pl.BlockSpec((r, s["w"]), lambda c, b, t: (0, c)))

    def body(*refs):
        x_refs, p_refs = refs[:nx], refs[nx:nx + np_]
        d_refs = refs[nx + np_:nx + np_ + len(douts)]
        o_refs = refs[nx + np_ + len(douts):]
        xv = [[p.astype(F32) for p in _pieces(r, s)] for r, s in zip(x_refs, xs)]
        pv = [[p.astype(F32) for p in _pieces(r, s)] for r, s in zip(p_refs, ps)]
        if fwd:
            for r, pieces, s in zip(o_refs, f(xv, pv), out_modes):
                _store(r, pieces, s)
            return
        _, vjp = jax.vjp(f, xv, pv)
        cot = [[p.astype(F32) for p in _pieces(r, s)] for r, s in zip(d_refs, douts)]
        dxv, dpv = vjp(cot)
        for r, i in zip(o_refs, dx):
            _store(r, dxv[i], xs[i])
        b, t = pl.program_id(1), pl.program_id(2)
        for r, j in zip(o_refs[len(dx):], dp):
            first = (t == 0) if ps[j]["e"] else jnp.logical_and(b == 0, t == 0)

            @pl.when(first)
            def _(r=r, j=j):
                _store(r, dpv[j], ps[j])

            @pl.when(jnp.logical_not(first))
            def _(r=r, j=j):
                _store(r, dpv[j], ps[j], accumulate=True)

    res = pl.pallas_call(
        body, name=name, grid=(nc, bsz, nt), in_specs=in_specs, out_specs=out_specs, out_shape=out_shape,
        compiler_params=pltpu.CompilerParams(dimension_semantics=("arbitrary", "arbitrary", "arbitrary")),
    )(*operands)
    return res


def _keep_rows(a, shift, keep):
    n = a.shape[0]
    t = lax.broadcasted_iota(jnp.int32, a.shape, 0)
    return jnp.where(keep(t, n), pltpu.roll(a, shift % n, 0), 0.0)


def _shift_pair(step, keep_prev, keep_next):
    @jax.custom_vjp
    def prev(a):
        return _keep_rows(a, step, keep_prev)

    @jax.custom_vjp
    def nxt(a):
        return _keep_rows(a, -step, keep_next)

    prev.defvjp(lambda a: (prev(a), None), lambda _, g: (nxt(g),))
    nxt.defvjp(lambda a: (nxt(a), None), lambda _, g: (prev(g),))
    return prev, nxt


prev_tok, next_tok = _shift_pair(1, lambda t, n: t % GRID_W != 0, lambda t, n: t % GRID_W != GRID_W - 1)
prev_row, next_row = _shift_pair(GRID_W, lambda t, n: t >= GRID_W, lambda t, n: t < n - GRID_W)


@jax.custom_vjp
def bdot(a, w):
    return jnp.dot(a.astype(BF16), w.astype(BF16), preferred_element_type=F32)


def _bdot_bwd(res, g):
    a, w = res
    gb = g.astype(BF16)
    da = lax.dot_general(gb, w.astype(BF16), (((1,), (1,)), ((), ())), preferred_element_type=F32)
    dw = lax.dot_general(a.astype(BF16), gb, (((0,), (0,)), ((), ())), preferred_element_type=F32)
    return da, dw


bdot.defvjp(lambda a, w: (bdot(a, w), (a, w)), _bdot_bwd)


@jax.custom_vjp
def log_sigmoid(z):
    return jnp.minimum(z, 0.0) - jnp.log(1.0 + jnp.exp(-jnp.abs(z)))


def _lsig_bwd(z, g):
    e = jnp.exp(-jnp.abs(z))
    return (g * jnp.where(z >= 0, e, 1.0) / (1.0 + e),)


log_sigmoid.defvjp(lambda z: (log_sigmoid(z), z), _lsig_bwd)


def silu(x):
    return x * jax.nn.sigmoid(x)


def _rms(x):
    return x * lax.rsqrt(jnp.mean(x * x, axis=-1, keepdims=True) + EPS)


def _mod(x, gain, shift, scale):
    return _rms(x) * gain * (1.0 + scale) + shift


def f_mod(xs, ps):
    ((h,),), ((gain,), (shift,), (scale,)) = xs, ps
    return [[_mod(h, gain, shift, scale)], [h]]


def f_res_mod(xs, ps):
    ((h,), (y,)), ((gate,), (gain,), (shift,), (scale,)) = xs, ps
    h1 = h + gate * y
    return [[h1], [_mod(h1, gain, shift, scale)]]


def f_ffn_mid(xs, ps):
    ((ua, ug),), ((w0a, w0g), (w1a, w1g), (w2a, w2g), (ba, bg)) = xs, ps
    a = w0a * prev_row(ua) + w1a * ua + w2a * next_row(ua) + ba
    g = w0g * prev_row(ug) + w1g * ug + w2g * next_row(ug) + bg
    return [[a * silu(g)]]


def f_sc_mid(xs, ps):
    ((bg, cg, v),), ((w0,), (w1,), (w2,)) = xs, ps
    z = cg * v
    return [[bg * (w0 * prev_tok(z) + w1 * z + w2 * next_tok(z))]]


def f_decay(xs, ps):
    ((a,),), ((wd,), (bd,)) = xs, ps
    return [[log_sigmoid(bdot(a, wd) + bd) / TAU]]


def f_gla_post(xs, ps):
    (of, ob, g), ((gain,),) = xs, ps
    return [[_rms(a + b) * gain * silu(c) for a, b, c in zip(of, ob, g)]]


NCH = TT // CHUNK
CTX_CH = CTX // CHUNK
_NT = (((1,), (1,)), ((), ()))
_TN = (((0,), (0,)), ((), ()))
_NN = (((1,), (0,)), ((), ()))


def _chunk_of(d, j):
    return jnp.where(d == 0, j, jnp.where(j < CTX_CH, CTX_CH - 1 - j, NCH + CTX_CH - 1 - j))


def _dot(a, b, dn):
    return lax.dot_general(a, b, dn, preferred_element_type=F32)


def _cumsum_rows(g, suffix):
    n = g.shape[0]
    row = lax.broadcasted_iota(jnp.int32, g.shape, 0)
    s = 1
    while s < n:
        if suffix:
            g = g + jnp.where(row < n - s, pltpu.roll(g, n - s, 0), 0.0)
        else:
            g = g + jnp.where(row >= s, pltpu.roll(g, s, 0), 0.0)
        s *= 2
    return g


def _causal(backward):
    row = lax.broadcasted_iota(jnp.int32, (CHUNK, CHUNK), 0)
    col = lax.broadcasted_iota(jnp.int32, (CHUNK, CHUNK), 1)
    return col >= row if backward else col <= row


def _gla_in_specs(bsz, rev):
    def blk(d, j):
        return _chunk_of(d, (NCH - 1 - j) if rev else j)

    return [
        pl.BlockSpec((bsz, CHUNK, KD), lambda d, j: (0, blk(d, j), 0)),
        pl.BlockSpec((bsz, CHUNK, KD), lambda d, j: (0, blk(d, j), 1)),
        pl.BlockSpec((bsz, CHUNK, VD), lambda d, j: (0, blk(d, j), 1)),
        pl.BlockSpec((bsz, CHUNK, KD), lambda d, j: (0, blk(d, j), d)),
    ], blk


def gla_fwd(pcat, la):
    bsz = pcat.shape[0]
    in_specs, blk = _gla_in_specs(bsz, False)

    def body(q_ref, k_ref, v_ref, la_ref, o_ref, s_ref, st):
        d, j = pl.program_id(0), pl.program_id(1)

        @pl.when(j == 0)
        def _():
            st[...] = jnp.zeros_like(st)

        s_ref[...] = st[...]

        def scan(backward):
            causal = _causal(backward)
            for e in range(bsz):
                g_all = la_ref[e]
                b_all = _cumsum_rows(g_all, backward)
                bl_all = jnp.sum(g_all, axis=0, keepdims=True)
                qs_all = (q_ref[e].astype(F32) * (HK ** -0.5) * jnp.exp(b_all)).astype(BF16)
                ks_all = (k_ref[e] * jnp.exp(-b_all)).astype(BF16)
                kd_all = (k_ref[e] * jnp.exp(bl_all - b_all)).astype(BF16)
                el_all = jnp.exp(bl_all)
                for h in range(HEADS):
                    ks_, vs_ = slice(h * HK, (h + 1) * HK), slice(h * HV, (h + 1) * HV)
                    qs, ks, kd, v = qs_all[:, ks_], ks_all[:, ks_], kd_all[:, ks_], v_ref[e, :, vs_].astype(BF16)
                    s = st[e, h]
                    att = jnp.where(causal, _dot(qs, ks, _NT), 0.0).astype(BF16)
                    o_ref[e, :, vs_] = _dot(qs, s.astype(BF16), _NT) + _dot(att, v, _NN)
                    st[e, h] = el_all[:, ks_] * s + _dot(v, kd, _TN)

        @pl.when(d == 0)
        def _():
            scan(False)

        @pl.when(d == 1)
        def _():
            scan(True)

    return pl.pallas_call(
        body, name="gla_fwd", grid=(2, NCH), in_specs=in_specs,
        out_specs=[pl.BlockSpec((bsz, CHUNK, VD), lambda d, j: (0, blk(d, j), d)),
                   pl.BlockSpec((bsz, None, None, HEADS, HV, HK), lambda d, j: (0, d, j, 0, 0, 0))],
        out_shape=[jax.ShapeDtypeStruct((bsz, TT, 2 * VD), F32), jax.ShapeDtypeStruct((bsz, 2, NCH, HEADS, HV, HK), F32)],
        scratch_shapes=[pltpu.VMEM((bsz, HEADS, HV, HK), F32)],
        compiler_params=pltpu.CompilerParams(dimension_semantics=("arbitrary", "arbitrary")),
    )(pcat, pcat, pcat, la)


def gla_bwd(pcat, la, s_all, do):
    bsz = pcat.shape[0]
    in_specs, blk = _gla_in_specs(bsz, True)
    in_specs += [
        pl.BlockSpec((bsz, None, None, HEADS, HV, HK), lambda d, j: (0, d, NCH - 1 - j, 0, 0, 0)),
        pl.BlockSpec((bsz, CHUNK, VD), lambda d, j: (0, jnp.maximum(blk(d, j) - CTX_CH, 0), 0)),
    ]

    def body(q_ref, k_ref, v_ref, la_ref, s_ref, do_ref, dq_ref, dk_ref, dv_ref, dla_ref, dst):
        d, j = pl.program_id(0), pl.program_id(1)

        @pl.when(j == 0)
        def _():
            dst[...] = jnp.zeros_like(dst)

        latent = blk(d, j) >= CTX_CH
        scale = HK ** -0.5

        def scan(backward):
            causal = _causal(backward)
            for e in range(bsz):
                g_all = la_ref[e]
                b_all = _cumsum_rows(g_all, backward)
                bl_all = jnp.sum(g_all, axis=0, keepdims=True)
                ex_all, ei_all, ed_all, el_all = jnp.exp(b_all), jnp.exp(-b_all), jnp.exp(bl_all - b_all), jnp.exp(bl_all)
                qs_all, ks_all, kd_all = q_ref[e].astype(F32) * scale * ex_all, k_ref[e] * ei_all, k_ref[e] * ed_all
                qsb_all, ksb_all, kdb_all = qs_all.astype(BF16), ks_all.astype(BF16), kd_all.astype(BF16)
                db_parts, dbl_parts = [], []
                for h in range(HEADS):
                    ks_, vs_ = slice(h * HK, (h + 1) * HK), slice(h * HV, (h + 1) * HV)
                    qs, ks, kd, el = qs_all[:, ks_], ks_all[:, ks_], kd_all[:, ks_], el_all[:, ks_]
                    qsb, ksb, kdb, v = qsb_all[:, ks_], ksb_all[:, ks_], kdb_all[:, ks_], v_ref[e, :, vs_].astype(BF16)
                    s, ds1 = s_ref[e, h], dst[e, h]
                    sb, ds1b = s.astype(BF16), ds1.astype(BF16)
                    dob = jnp.where(latent, do_ref[e, :, vs_], 0.0).astype(BF16)
                    att = jnp.where(causal, _dot(qsb, ksb, _NT), 0.0).astype(BF16)
                    datt = jnp.where(causal, _dot(dob, v, _NT), 0.0).astype(BF16)
                    dqs = _dot(dob, sb, _NN) + _dot(datt, ksb, _NN)
                    dks = _dot(datt, qsb, _TN)
                    dv_ref[e, :, vs_] = (_dot(att, dob, _TN) + _dot(kdb, ds1b, _NT)).astype(BF16)
                    dkd = _dot(v, ds1b, _NN)
                    dst[e, h] = _dot(dob, qsb, _TN) + el * ds1
                    del_ = jnp.sum(s * ds1, axis=0, keepdims=True)
                    dq_ref[e, :, ks_] = (dqs * ex_all[:, ks_] * scale).astype(BF16)
                    dk_ref[e, :, ks_] = (dks * ei_all[:, ks_] + dkd * ed_all[:, ks_]).astype(BF16)
                    db_parts.append(dqs * qs - dks * ks - dkd * kd)
                    dbl_parts.append(jnp.sum(dkd * kd, axis=0, keepdims=True) + del_ * el)
                dla_ref[e] = _cumsum_rows(jnp.concatenate(db_parts, -1), not backward) + jnp.concatenate(dbl_parts, -1)

        @pl.when(d == 0)
        def _():
            scan(False)

        @pl.when(d == 1)
        def _():
            scan(True)

    return pl.pallas_call(
        body, name="gla_bwd", grid=(2, NCH), in_specs=in_specs,
        out_specs=[pl.BlockSpec((None, bsz, CHUNK, KD), lambda d, j: (d, 0, blk(d, j), 0)),
                   pl.BlockSpec((None, bsz, CHUNK, KD), lambda d, j: (d, 0, blk(d, j), 0)),
                   pl.BlockSpec((None, bsz, CHUNK, VD), lambda d, j: (d, 0, blk(d, j), 0)),
                   pl.BlockSpec((bsz, CHUNK, KD), lambda d, j: (0, blk(d, j), d))],
        out_shape=[jax.ShapeDtypeStruct((2, bsz, TT, KD), BF16), jax.ShapeDtypeStruct((2, bsz, TT, KD), BF16),
                   jax.ShapeDtypeStruct((2, bsz, TT, VD), BF16), jax.ShapeDtypeStruct((bsz, TT, 2 * KD), F32)],
        scratch_shapes=[pltpu.VMEM((bsz, HEADS, HV, HK), F32)],
        compiler_params=pltpu.CompilerParams(dimension_semantics=("arbitrary", "arbitrary")),
    )(pcat, pcat, pcat, la, s_all, do)


def gla_combine(dq2, dk2, dv2, dgate, dpa):
    bsz = dgate.shape[0]
    tm = CTX

    def body(dq_ref, dk_ref, dv_ref, dg_ref, dpa_ref, o_ref):
        t = pl.program_id(1)
        o_ref[:, 0:KD] = (dq_ref[0].astype(F32) + dq_ref[1].astype(F32)).astype(BF16)
        o_ref[:, KD:2 * KD] = (dk_ref[0].astype(F32) + dk_ref[1].astype(F32)).astype(BF16)
        o_ref[:, 2 * KD:2 * KD + VD] = (dv_ref[0].astype(F32) + dv_ref[1].astype(F32)).astype(BF16)
        o_ref[:, 2 * KD + VD:2 * KD + 2 * VD] = jnp.where(t > 0, dg_ref[...], 0).astype(BF16)
        o_ref[:, 2 * KD + 2 * VD:] = dpa_ref[...].astype(BF16)

    return pl.pallas_call(
        body, name="gla_combine", grid=(bsz, TT // tm),
        in_specs=[pl.BlockSpec((2, None, tm, KD), lambda b, t: (0, b, t, 0)),
                  pl.BlockSpec((2, None, tm, KD), lambda b, t: (0, b, t, 0)),
                  pl.BlockSpec((2, None, tm, VD), lambda b, t: (0, b, t, 0)),
                  pl.BlockSpec((None, tm, VD), lambda b, t: (b, jnp.maximum(t - 1, 0), 0)),
                  pl.BlockSpec((None, tm, 128), lambda b, t: (b, t, 0))],
        out_specs=pl.BlockSpec((None, tm, GLA_IN_PAD), lambda b, t: (b, t, 0)),
        out_shape=jax.ShapeDtypeStruct((bsz, TT, GLA_IN_PAD), BF16),
        compiler_params=pltpu.CompilerParams(dimension_semantics=("arbitrary", "arbitrary")),
    )(dq2, dk2, dv2, dgate, dpa)


def final_loss(h1, fo, gate, gain, tgt):
    bsz, t_len, _ = h1.shape
    tm = 256

    def body(h_ref, f_ref, gate_ref, gain_ref, tgt_ref, loss_ref, dh_ref, df_ref, dgate_ref, dgain_ref):
        b, t = pl.program_id(0), pl.program_id(1)
        target = tgt_ref[...]

        def core(h, fo_, gate_, gain_):
            e = _rms(h + gate_ * fo_) * gain_ - target
            return jnp.sum(0.5 * jnp.sum(e * e, axis=-1, keepdims=True) / D, axis=0, keepdims=True)

        loss, vjp = jax.vjp(core, h_ref[...], f_ref[...], gate_ref[...], gain_ref[...])
        dh, df, dgate, dgain = vjp(jnp.ones((1, 1), F32))
        dh_ref[...] = dh
        df_ref[...] = df.astype(BF16)
        first = jnp.logical_and(b == 0, t == 0)

        @pl.when(first)
        def _():
            loss_ref[...] = jnp.broadcast_to(loss, loss_ref.shape)
            dgain_ref[...] = dgain

        @pl.when(jnp.logical_not(first))
        def _():
            loss_ref[...] += jnp.broadcast_to(loss, loss_ref.shape)
            dgain_ref[...] += dgain

        @pl.when(t == 0)
        def _():
            dgate_ref[...] = dgate

        @pl.when(t > 0)
        def _():
            dgate_ref[...] += dgate

    tile = pl.BlockSpec((None, tm, D), lambda b, t: (b, t, 0))
    per_ex = pl.BlockSpec((None, 1, D), lambda b, t: (b, 0, 0))
    shared = pl.BlockSpec((1, D), lambda b, t: (0, 0))
    return pl.pallas_call(
        body, name="final_loss", grid=(bsz, t_len // tm),
        in_specs=[tile, tile, per_ex, shared, tile],
        out_specs=[pl.BlockSpec((8, 128), lambda b, t: (0, 0)), tile, tile, per_ex, shared],
        out_shape=[jax.ShapeDtypeStruct((8, 128), F32), jax.ShapeDtypeStruct(h1.shape, F32),
                   jax.ShapeDtypeStruct(h1.shape, BF16), jax.ShapeDtypeStruct((bsz, 1, D), F32),
                   jax.ShapeDtypeStruct((1, D), F32)],
        compiler_params=pltpu.CompilerParams(dimension_semantics=("arbitrary", "arbitrary")),
    )(h1, fo, gate, gain, tgt)


ADA_ROWS = 24
ADA_CTX_ROW = 16
ADA_COLS = 6 * D // N_DEV


def ada_fwd(cond, w, b):
    def body(c_ref, w_ref, b_ref, o_ref):
        s = silu(c_ref[...]).astype(BF16)
        o_ref[...] = jnp.dot(s, w_ref[...].astype(BF16), preferred_element_type=F32) + b_ref[...]

    return pl.pallas_call(
        body, name="ada_fwd", grid=(2,),
        in_specs=[pl.BlockSpec((ADA_ROWS, D), lambda i: (0, 0)), pl.BlockSpec((None, D, ADA_COLS), lambda i: (i, 0, 0)),
                  pl.BlockSpec((None, 1, ADA_COLS), lambda i: (i, 0, 0))],
        out_specs=pl.BlockSpec((None, ADA_ROWS, ADA_COLS), lambda i: (i, 0, 0)),
        out_shape=jax.ShapeDtypeStruct((2, ADA_ROWS, ADA_COLS), F32),
    )(cond, w, b)


def ada_bwd(cond, dm_mine, dm_full, w):
    def body(c_ref, dm_ref, dmf_ref, w_ref, gw_ref, gb_ref, cp_ref):
        i = pl.program_id(0)
        s = silu(c_ref[...]).astype(BF16)
        dm = dm_ref[...].astype(BF16)
        gw_ref[...] = _dot(s, dm, _TN)
        gb_ref[...] = jnp.sum(dmf_ref[...], axis=0, keepdims=True)

        @pl.when(i == 0)
        def _():
            cp_ref[...] = _dot(dm_ref[ADA_CTX_ROW:, :].astype(BF16), w_ref[...].astype(BF16), _NT)

    return pl.pallas_call(
        body, name="ada_bwd", grid=(2,),
        in_specs=[pl.BlockSpec((ADA_ROWS, D), lambda i: (0, 0)), pl.BlockSpec((None, ADA_ROWS, ADA_COLS), lambda i: (i, 0, 0)),
                  pl.BlockSpec((None, ADA_ROWS, 6 * D), lambda i: (i, 0, 0)), pl.BlockSpec((None, D, ADA_COLS), lambda i: (i, 0, 0))],
        out_specs=[pl.BlockSpec((None, D, ADA_COLS), lambda i: (i, 0, 0)), pl.BlockSpec((None, 1, 6 * D), lambda i: (i, 0, 0)),
                   pl.BlockSpec((ADA_ROWS - ADA_CTX_ROW, D), lambda i: (0, 0))],
        out_shape=[jax.ShapeDtypeStruct((2, D, ADA_COLS), F32), jax.ShapeDtypeStruct((2, 1, 6 * D), F32),
                   jax.ShapeDtypeStruct((ADA_ROWS - ADA_CTX_ROW, D), F32)],
        compiler_params=pltpu.CompilerParams(dimension_semantics=("arbitrary",)),
    )(cond, dm_mine, dm_full, w)


def cctx_grad(parts, c_ctx):
    def body(p_ref, c_ref, o_ref):
        tot = p_ref[0:1, :]
        for i in range(1, N_DEV):
            tot = tot + p_ref[i:i + 1, :]
        c = c_ref[...]
        sg = jax.nn.sigmoid(c)
        o_ref[...] = tot * sg * (1.0 + c * (1.0 - sg))

    return pl.pallas_call(body, name="cctx_grad", out_shape=jax.ShapeDtypeStruct((1, D), F32))(parts, c_ctx)


def _row_tile(r):
    for t in (512, 256, 128, 80, 64, 40, 32, 16, 8):
        if r % t == 0:
            return t
    return r


def _slot_sum(ref):
    tot = ref[0].astype(F32)
    for i in range(1, ref.shape[0]):
        tot = tot + ref[i].astype(F32)
    return tot


def sum_slots(name, x):
    s, r, c = x.shape
    tr = _row_tile(r)

    def body(x_ref, o_ref):
        o_ref[...] = _slot_sum(x_ref)

    return pl.pallas_call(
        body, name=name, grid=(r // tr,), in_specs=[pl.BlockSpec((s, tr, c), lambda i: (0, i, 0))],
        out_specs=pl.BlockSpec((tr, c), lambda i: (i, 0)), out_shape=jax.ShapeDtypeStruct((r, c), F32),
    )(x)


def adamw(name, w, g, m, v, layer=None):
    r, c = w.shape[-2:]
    tr = _row_tile(r)
    stacked = g.ndim == 3

    def body(w_ref, g_ref, m_ref, v_ref, go_ref, d_ref, mo_ref, vo_ref):
        gv = _slot_sum(g_ref) if stacked else g_ref[...]
        mn = B1 * m_ref[...] + (1.0 - B1) * gv
        vn = B2 * v_ref[...] + (1.0 - B2) * jnp.square(gv)
        m_hat = mn / (1.0 - B1 ** STEP)
        v_hat = vn / (1.0 - B2 ** STEP)
        go_ref[...] = gv
        d_ref[...] = -LR * (m_hat / (jnp.sqrt(v_hat) + AEPS) + WD * w_ref[...])
        mo_ref[...] = mn
        vo_ref[...] = vn

    tile = pl.BlockSpec((tr, c), lambda i: (i, 0))
    slab = tile if layer is None else pl.BlockSpec((None, tr, c), lambda i: (layer, i, 0))
    g_spec = pl.BlockSpec((g.shape[0], tr, c), lambda i: (0, i, 0)) if stacked else tile
    return pl.pallas_call(
        body, name=name, grid=(r // tr,), in_specs=[slab, g_spec, slab, slab], out_specs=[tile] * 4,
        out_shape=[jax.ShapeDtypeStruct((r, c), F32)] * 4,
    )(w, g, m, v)


def _place():
    return lax.axis_index("x"), lax.axis_index("y"), lax.axis_index("c")


def all_gather(name, x, in_vmem):
    r, c = x.shape
    space = pltpu.VMEM if in_vmem else pl.ANY

    def body(x_ref, out_ref, send_sems, recv_sems, local_sem):
        px, py, pc = _place()
        me, sibling = (px, py, pc), (px, py, 1 - pc)
        chips = [(1 - px, py), (px, 1 - py), (1 - px, 1 - py)]

        def rows(qx, qy, qc):
            return out_ref.at[pl.ds((4 * qx + 2 * qy + qc) * r, r), :]

        def copy(k, block, to, src=None):
            return pltpu.make_async_remote_copy(
                src_ref=rows(*block) if src is None else src, dst_ref=rows(*block),
                send_sem=send_sems.at[k], recv_sem=recv_sems.at[k], device_id=to, device_id_type=MESH)

        mine = pltpu.make_async_copy(x_ref, rows(*me), local_sem)
        mine.start()
        first = [copy(0, me, sibling, src=x_ref)]
        first += [copy(1 + j, me, (*chip, pc), src=x_ref) for j, chip in enumerate(chips)]
        for cp in first:
            cp.start()
        passed = [copy(4 + j, (*chip, pc), sibling) for j, chip in enumerate(chips)]
        for j, chip in enumerate(chips):
            copy(1 + j, (*chip, pc), me).wait_recv()
            passed[j].start()
        copy(0, sibling, me).wait_recv()
        for j, chip in enumerate(chips):
            copy(4 + j, (*chip, 1 - pc), me).wait_recv()
        for cp in first + passed:
            cp.wait_send()
        mine.wait()

    return pl.pallas_call(
        body, name=name, out_shape=jax.ShapeDtypeStruct((N_DEV * r, c), x.dtype),
        in_specs=[pl.BlockSpec(memory_space=space)], out_specs=pl.BlockSpec(memory_space=space),
        scratch_shapes=[pltpu.SemaphoreType.DMA((7,)), pltpu.SemaphoreType.DMA((7,)), pltpu.SemaphoreType.DMA],
    )(x)


_HBM =pl.BlockSpec(memory_space=pltpu.HBM)
_SEM = pl.BlockSpec(memory_space=pltpu.SEMAPHORE)
_EFFECT = pltpu.SideEffectType.DATAFLOW_SIDE_EFFECTING


def _peers():
    px, py, pc = _place()
    return [(1 - px if k & 4 else px, 1 - py if k & 2 else py, 1 - pc if k & 1 else pc) for k in range(1, N_DEV)]


def _slot(dev):
    return 4 * dev[0] + 2 * dev[1] + dev[2]


def _split_copies(src_refs, land_refs, send_sems, recv_sems, gather):
    me = _slot(_place())
    return [pltpu.make_async_remote_copy(
        src_ref=src if gather else src.at[_slot(peer)], dst_ref=land.at[me],
        send_sem=send_sems.at[a * (N_DEV - 1) + k], recv_sem=recv_sems.at[a * (N_DEV - 1) + k],
        device_id=peer, device_id_type=MESH)
        for a, (src, land) in enumerate(zip(src_refs, land_refs)) for k, peer in enumerate(_peers())]


def exchange_start(name, srcs, gather, after):
    n = len(srcs)
    lands = [pltpu.HBM((N_DEV,) + s.shape if gather else s.shape, s.dtype) for s in srcs]

    def body(*refs):
        send_sems, recv_sems = refs[2 * n + 1:2 * n + 3]
        for cp in _split_copies(refs[:n], refs[n:2 * n], send_sems, recv_sems, gather):
            cp.start()
        me, local_sems = _slot(_place()), refs[-1]
        own = [pltpu.make_async_copy(src if gather else src.at[me], land.at[me], local_sems.at[a])
               for a, (src, land) in enumerate(zip(refs[:n], refs[n:2 * n]))]
        for cp in own:
            cp.start()
        for cp in own:
            cp.wait()
        refs[-2][...] = jnp.zeros_like(refs[-2])

    sems = pltpu.SemaphoreType.DMA((n * (N_DEV - 1),))
    res = pl.pallas_call(
        body, name=name,
        out_shape=(sems, sems, *[pltpu.HBM(s.shape, s.dtype) for s in srcs], *lands, jax.ShapeDtypeStruct((8, 128), F32)),
        in_specs=(_HBM,) * (2 * n) + (pl.BlockSpec(memory_space=pl.ANY),),
        out_specs=(_SEM, _SEM) + (_HBM,) * (2 * n) + (pl.BlockSpec(memory_space=pltpu.VMEM),),
        input_output_aliases={i: 2 + i for i in range(2 * n)}, scratch_shapes=[pltpu.SemaphoreType.DMA((n,))],
        compiler_params=pltpu.CompilerParams(has_side_effects=_EFFECT),
    )(*[pltpu.with_memory_space_constraint(s, pltpu.HBM) for s in srcs],
      *[pltpu.with_memory_space_constraint(lax.empty(ld.shape, ld.dtype), pltpu.HBM) for ld in lands], after)
    return res[0], res[1], list(res[2:2 + n]), list(res[2 + n:2 + 2 * n]), res[-1]


def exchange_wait(name, started, after, gather):
    send_sems, recv_sems, srcs, lands, _ = started
    n = len(srcs)
    after = list(after) if isinstance(after, (list, tuple)) else [after]

    def body(*refs):
        send_sems, recv_sems = refs[2 * n:2 * n + 2]
        for cp in _split_copies(refs[:n], refs[n:2 * n], send_sems, recv_sems, gather):
            cp.wait_send()
            cp.wait_recv()

    res = pl.pallas_call(
        body, name=name, out_shape=tuple(pltpu.HBM(a.shape, a.dtype) for a in srcs + lands),
        in_specs=(_HBM,) * (2 * n) + (_SEM, _SEM) + (pl.BlockSpec(memory_space=pl.ANY),) * len(after),
        out_specs=(_HBM,) * (2 * n), input_output_aliases={i: i for i in range(2 * n)},
        compiler_params=pltpu.CompilerParams(has_side_effects=_EFFECT),
    )(*srcs, *lands, send_sems, recv_sems, *after)
    return list(res[:n]), list(res[n:])


NCF = FFN_H // FFN_TC


def _size(shape):
    n = 1
    for s in shape:
        n *= s
    return n


def _padded_rows(n_elems, row_mult):
    return -(-n_elems // (D * row_mult)) * row_mult


def _pack_rows(arrs, dtype, row_mult):
    rows, offs, r0 = [], [], 0
    for a in arrs:
        flat = a.reshape(-1).astype(dtype)
        n = _padded_rows(flat.shape[0], row_mult)
        rows.append(jnp.pad(flat, (0, n * D - flat.shape[0])).reshape(n, D))
        offs.append(r0)
        r0 += n
    return jnp.concatenate(rows, 0), offs


def _unpack_rows(buf, offs, shapes):
    lead, out = buf.shape[:-2], []
    for o, shp in zip(offs, shapes):
        n = _size(shp)
        nr = -(-n // D)
        out.append(buf[..., o:o + nr, :].reshape(lead + (nr * D,))[..., :n].reshape(lead + tuple(shp)))
    return out


def _rows3(w):
    return [w[i:i + 1] for i in range(3)]


def f_mod1(xs, ps):
    return f_mod(xs, ps)[:1]


def kernel(x, c, ctx, c_ctx, ada_w, ada_b, norm_mix, norm_ffn, gla_w_in, gla_w_a2, gla_b_a, gla_head_norm, gla_w_out, sc_w_in, sc_conv_w, sc_w_out, ffn_w_up, ffn_conv_w, ffn_conv_b, ffn_w_down, final_norm, loss_target, m_c_ctx, m_ada_w, m_ada_b, m_norm_mix, m_norm_ffn, m_gla_w_in, m_gla_w_a2, m_gla_b_a, m_gla_head_norm, m_gla_w_out, m_sc_w_in, m_sc_conv_w, m_sc_w_out, m_ffn_w_up, m_ffn_conv_w, m_ffn_conv_b, m_ffn_w_down, m_final_norm, v_c_ctx, v_ada_w, v_ada_b, v_norm_mix, v_norm_ffn, v_gla_w_in, v_gla_w_a2, v_gla_b_a, v_gla_head_norm, v_gla_w_out, v_sc_w_in, v_sc_conv_w, v_sc_w_out, v_ffn_w_up, v_ffn_conv_w, v_ffn_conv_b, v_ffn_w_down, v_final_norm):
    names = ["c_ctx", "ada_w", "ada_b", "norm_mix", "norm_ffn", "gla_w_in", "gla_w_a2", "gla_b_a", "gla_head_norm",
             "gla_w_out", "sc_w_in", "sc_conv_w", "sc_w_out", "ffn_w_up", "ffn_conv_w", "ffn_conv_b", "ffn_w_down",
             "final_norm"]
    w_ = dict(zip(names, [c_ctx, ada_w, ada_b, norm_mix, norm_ffn, gla_w_in, gla_w_a2, gla_b_a, gla_head_norm, gla_w_out,
                          sc_w_in, sc_conv_w, sc_w_out, ffn_w_up, ffn_conv_w, ffn_conv_b, ffn_w_down, final_norm]))
    m_ = dict(zip(names, [m_c_ctx, m_ada_w, m_ada_b, m_norm_mix, m_norm_ffn, m_gla_w_in, m_gla_w_a2, m_gla_b_a,
                          m_gla_head_norm, m_gla_w_out, m_sc_w_in, m_sc_conv_w, m_sc_w_out, m_ffn_w_up, m_ffn_conv_w,
                          m_ffn_conv_b, m_ffn_w_down, m_final_norm]))
    v_ = dict(zip(names, [v_c_ctx, v_ada_w, v_ada_b, v_norm_mix, v_norm_ffn, v_gla_w_in, v_gla_w_a2, v_gla_b_a,
                          v_gla_head_norm, v_gla_w_out, v_sc_w_in, v_sc_conv_w, v_sc_w_out, v_ffn_w_up, v_ffn_conv_w,
                          v_ffn_conv_b, v_ffn_w_down, v_final_norm]))
    me = 4 * lax.axis_index("x") + 2 * lax.axis_index("y") + lax.axis_index("c")
    bsz = x.shape[0]
    tm = 256
    nt = SEQ // tm
    ctx_tiles = CTX // tm
    pe = functools.partial(P, per_example=True)

    groups = {"ffn1": [("ffn_w_up", 1), ("ffn_w_down", 1)], "sc": [("sc_w_in", 0), ("sc_w_out", 0)],
              "ffn0": [("ffn_w_up", 0), ("ffn_w_down", 0)], "gla": [("gla_w_in", 0), ("gla_w_out", 0)]}
    ag_groups = {"gin": [("gla_w_in", 0)], "ffn0": [("gla_w_out", 0), ("ffn_w_up", 0), ("ffn_w_down", 0)],
                 "sc": groups["sc"], "ffn1": groups["ffn1"]}
    ag_started = {}

    def start_gather(g, after):
        ag_started[g] = exchange_start(f"ag_{g}_start", [w_[n][i].astype(BF16) for n, i in ag_groups[g]], True, after)
        return ag_started[g][4]

    small_sharded = [c, gla_w_a2, gla_b_a, sc_conv_w, ffn_conv_w]
    pack0, offs0 = _pack_rows(small_sharded, F32, 8)
    g0 = all_gather("ag_small", pack0, True).reshape(N_DEV, pack0.shape[0], D)
    c_all, wa2_s, ba_s, scw_s, fcw_s = _unpack_rows(g0, offs0, [a.shape for a in small_sharded])
    w_a2 = wa2_s[:, 0].transpose(1, 2, 0, 3).reshape(2, RANK, KD)
    b_a = ba_s[:, 0].transpose(1, 0, 2).reshape(2, KD)
    sc_cw = scw_s[:, 0].transpose(1, 0, 2).reshape(3, D)
    ffn_cw = fcw_s.transpose(1, 2, 0, 3).reshape(2, 3, 2 * FFN_H)

    cond = jnp.concatenate([c_all.reshape(N_DEV * bsz, D), c_ctx[None], jnp.zeros((ADA_ROWS - N_DEV * bsz - 1, D), F32)], 0)
    b_mine = lax.dynamic_slice(ada_b, (0, me * ADA_COLS), (2, ADA_COLS)).reshape(2, 1, ADA_COLS)
    mod_part = ada_fwd(cond, ada_w, b_mine)
    mod = all_gather("ag_mod", mod_part.reshape(2 * ADA_ROWS, ADA_COLS), True)
    mod = mod.reshape(N_DEV, 2, ADA_ROWS, ADA_COLS).transpose(1, 2, 0, 3).reshape(2, ADA_ROWS, 6 * D)
    mods = lax.dynamic_slice(mod, (0, bsz * me, 0), (2, bsz, 6 * D))
    md = [[mods[i][:, k * D:(k + 1) * D].reshape(bsz, 1, D) for k in range(6)] for i in range(2)]
    mc = [mod[0, ADA_CTX_ROW, k * D:(k + 1) * D][None] for k in range(2)]

    tok = mod
    for g in ag_groups:
        tok = start_gather(g, tok)
    norm_mix = norm_mix + tok[0, 0]

    def gathered(g, after):
        mine, lands = exchange_wait(f"ag_{g}_wait", ag_started[g], after, True)
        return lands

    s_up, w_down = [None, None], [None, None]
    wd = jnp.zeros((128, 2 * KD), F32).at[:RANK, :KD].set(w_a2[0]).at[RANK:2 * RANK, KD:].set(w_a2[1])
    bd = b_a.reshape(1, 2 * KD)
    scw = _rows3(sc_cw)
    head_gain = gla_head_norm.reshape(1, HV)
    gains_mix = [norm_mix[i][None] for i in range(2)]
    gains_ffn = [norm_ffn[i][None] for i in range(2)]

    def tokens(a2d, t_len):
        return a2d.reshape(bsz, t_len, -1)

    def ffn_params(i):
        rows = [ffn_cw[i][t] for t in range(3)] + [ffn_conv_b[i]]
        return [P(a.reshape(2, FFN_H), w=FFN_TC, rows=True) for a in rows]

    def ffn_fwd(i, hn2):
        u = mm(f"ffn_up{i}", V(hn2, "tok"), V(s_up[i], "cols"), out="planes", out_dtype=BF16, planes_t=SEQ)
        act = rowwise(f"ffn_mid{i}", f_ffn_mid, [X(u, w=FFN_TC, planes=True)], ffn_params(i), tm=SEQ, nt=1, nc=NCF,
                      outs=[(FFN_TC, BF16, 1)])[0]
        return u, act

    def arrays(ps):
        return [p["a"] for p in ps]

    ps_in0 = [P(gains_mix[0]), pe(md[0][0]), pe(md[0][1])]
    ps_ctx = [P(gains_mix[0]), P(mc[0]), P(mc[1])]
    hn0 = rowwise("mod_in0", f_mod, [X(x)], ps_in0, tm=tm, nt=nt, outs=[(D, BF16, 1)])[0]
    hnc = rowwise("mod_ctx", f_mod, [X(ctx)], ps_ctx, tm=tm, nt=ctx_tiles, outs=[(D, BF16, 1)])[0]
    hcat = jnp.concatenate([hnc, hn0], axis=1)
    (s_gin,) = gathered("gin", hcat)
    w_gin = V(s_gin, "cols", width=GLA_IN_PAD)
    pcat = tokens(mm("gla_in", V(hcat, "tok"), w_gin, out_dtype=BF16), TT)
    pa_x = X(pcat, w=128, co=(GLA_IN_PAD - 128) // 128)
    la = rowwise("gla_decay", f_decay, [pa_x], [P(wd), P(bd)], tm=tm, nt=TT // tm, outs=[(2 * KD, F32, 1)])[0]
    o2, s_all = gla_fwd(pcat, la)
    post_xs = [X(o2, w=VD, co=0, ro=ctx_tiles, split=HEADS), X(o2, w=VD, co=1, ro=ctx_tiles, split=HEADS),
               X(pcat, w=VD, co=2, ro=ctx_tiles, split=HEADS)]
    yin0 = rowwise("gla_post", f_gla_post, post_xs, [P(head_gain)], tm=tm, nt=nt, outs=[(VD, BF16, HEADS)])[0]
    s_gout, s_up[0], s_down0 = gathered("ffn0", yin0)
    w_gout, w_down[0] = s_gout.reshape(VD, D), s_down0.reshape(FFN_H, D)
    ps_mid0 = [pe(md[0][2]), P(gains_ffn[0]), pe(md[0][3]), pe(md[0][4])]
    y0, h1_0, hn2_0 = mm_res_mod("gla_out", yin0, w_gout, x, *arrays(ps_mid0))
    u0, act0 = ffn_fwd(0, hn2_0)
    ps_in1 = [pe(md[0][5]), P(gains_mix[1]), pe(md[1][0]), pe(md[1][1])]
    fo0, h2_0, hn1 = mm_res_mod("ffn_down0", act0, w_down[0], h1_0, *arrays(ps_in1))

    s_sin, s_sout = gathered("sc", hn1)
    w_sout = s_sout.reshape(D, D)
    p1 = tokens(mm("sc_in", V(hn1, "tok"), V(s_sin, "cols")), SEQ)
    sc_ps = [P(a) for a in scw]
    yin1 = rowwise("sc_mid", f_sc_mid, [X(p1, split=3)], sc_ps, tm=tm, nt=nt, outs=[(D, BF16, 1)])[0]
    ps_mid1 = [pe(md[1][2]), P(gains_ffn[1]), pe(md[1][3]), pe(md[1][4])]
    y1, h1_1, hn2_1 = mm_res_mod("sc_out", yin1, w_sout, h2_0, *arrays(ps_mid1))
    s_up[1], s_down1 = gathered("ffn1", hn2_1)
    w_down[1] = s_down1.reshape(FFN_H, D)
    u1, act1 = ffn_fwd(1, hn2_1)
    fo1 = tokens(mm("ffn_down1", V(act1, "tok"), V(w_down[1])), SEQ)
    loss8, dh1_1, dfo1, dm5_1, g_final = final_loss(h1_1, fo1, md[1][5], final_norm[None], loss_target)

    def ffn_bwd(i, u, act, hn2, dfo):
        dact = tokens(mm(f"ffn_down_dx{i}", V(dfo, "tok"), V(w_down[i]), form="nt", out_dtype=BF16), SEQ)
        g_down = mm(f"ffn_down_dw{i}", V(act, "tok"), V(dfo, "tok"), form="tn", out_dtype=BF16)
        r = rowwise(f"ffn_mid_bwd{i}", f_ffn_mid, [X(u, w=FFN_TC, planes=True)], ffn_params(i), tm=SEQ, nt=1, nc=NCF,
                    douts=[X(dact, w=FFN_TC)], dx={0: BF16}, dp=[0, 1, 2, 3])
        du, g_cw, g_cb = r[0], jnp.stack([a.reshape(2 * FFN_H) for a in r[1:4]]), r[4].reshape(1, 2 * FFN_H)
        dhn2 = tokens(mm(f"ffn_up_dx{i}", V(du, "planes"), V(s_up[i], "cols"), form="nt", out_dtype=BF16), SEQ)
        g_up = mm(f"ffn_up_dw{i}", V(hn2, "tok"), V(du, "planes"), form="tn", out="cols", out_dtype=BF16)
        return dhn2, g_up, row_slots(g_down), g_cw, g_cb

    def res_mod_bwd(name, h, y, ps, dh1, dhn):
        return rowwise(name, f_res_mod, [X(h), X(y)], ps, tm=tm, nt=nt, douts=[X(dh1), X(dhn)],
                       dx={0: F32, 1: BF16}, dp=[0, 1, 2, 3])

    def row_slots(g):
        return g.reshape(N_DEV, -1, g.shape[-1])

    a2a_started = {}

    def send_grads(g, slots, after=None):
        a2a_started[g] = exchange_start(f"a2a_{g}_start", list(slots), False, loss8 if after is None else after)
        return a2a_started[g][4][0, 0]

    def after_start(ps, tok):
        return [dict(ps[0], a=ps[0]["a"] + tok)] + ps[1:]

    dhn2_1, g_up1, g_down1, g_fcw1, g_fcb1 = ffn_bwd(1, u1, act1, hn2_1, dfo1)
    tok = send_grads("ffn1", [g_up1, g_down1])
    dh2_0, dy1, dm2_1, g_nffn1, dm3_1, dm4_1 = res_mod_bwd("res_mod_mid1_bwd", h2_0, y1, after_start(ps_mid1, tok), dh1_1, dhn2_1)
    dyin1 = tokens(mm("sc_out_dx", V(dy1, "tok"), V(w_sout), form="nt", out_dtype=BF16), SEQ)
    g_sout = row_slots(mm("sc_out_dw", V(yin1, "tok"), V(dy1, "tok"), form="tn", out_dtype=BF16))
    r = rowwise("sc_mid_bwd", f_sc_mid, [X(p1, split=3)], sc_ps, tm=tm, nt=nt, douts=[X(dyin1)], dx={0: BF16}, dp=[0, 1, 2])
    dp1, g_scw = r[0], jnp.concatenate(r[1:4], 0)
    dhn1 = tokens(mm("sc_in_dx", V(dp1, "tok"), V(s_sin, "cols"), form="nt", out_dtype=BF16), SEQ)
    g_sin = mm("sc_in_dw", V(hn1, "tok"), V(dp1, "tok"), form="tn", out="cols", out_dtype=BF16)
    tok = send_grads("sc", [g_sin, g_sout])
    dh1_0, dfo0, dm5_0, g_nmix1, dm0_1, dm1_1 = res_mod_bwd("res_mod_in1_bwd", h1_0, fo0, after_start(ps_in1, tok), dh2_0, dhn1)

    dhn2_0, g_up0, g_down0, g_fcw0, g_fcb0 = ffn_bwd(0, u0, act0, hn2_0, dfo0)
    tok = send_grads("ffn0", [g_up0, g_down0])
    dx_res, dy0, dm2_0, g_nffn0, dm3_0, dm4_0 = res_mod_bwd("res_mod_mid0_bwd", x, y0, after_start(ps_mid0, tok), dh1_0, dhn2_0)
    dyin0 = tokens(mm("gla_out_dx", V(dy0, "tok"), V(w_gout), form="nt", out_dtype=BF16), SEQ)
    do, dgate, g_head = rowwise("gla_post_bwd", f_gla_post, post_xs, [P(head_gain)], tm=tm, nt=nt,
                                douts=[X(dyin0, split=HEADS)], dx={0: BF16, 2: BF16}, dp=[0])
    dq2, dk2, dv2, dla = gla_bwd(pcat, la, s_all, do)
    dpa, g_wd, g_bd = rowwise("gla_decay_bwd", f_decay, [pa_x], [P(wd), P(bd)], tm=tm, nt=TT // tm, douts=[X(dla)],
                              dx={0: BF16}, dp=[0, 1])
    dpcat = gla_combine(dq2, dk2, dv2, dgate, dpa)
    dhcat = tokens(mm("gla_in_dx", V(dpcat, "tok"), w_gin, form="nt", out_dtype=BF16), TT)
    grad_x, g_nmix0, dm0_0, dm1_0 = rowwise("mod_in0_bwd", f_mod, [X(x)], ps_in0, tm=tm, nt=nt,
                                            douts=[X(dhcat, ro=ctx_tiles), X(dx_res)], dx={0: F32}, dp=[0, 1, 2])
    g_nmix0c, dmc0, dmc1 = rowwise("mod_ctx_bwd", f_mod1, [X(ctx)], ps_ctx, tm=tm, nt=ctx_tiles, douts=[X(dhcat)],
                                   dx={}, dp=[0, 1, 2])

    zero_row = jnp.zeros((1, 4 * D), F32)
    dmod = [jnp.concatenate([jnp.concatenate([a.reshape(bsz, D) for a in dms], 1), ctx_row], 0)
            for dms, ctx_row in (([dm0_0, dm1_0, dm2_0, dm3_0, dm4_0, dm5_0], jnp.concatenate([dmc0, dmc1, zero_row], 1)),
                                 ([dm0_1, dm1_1, dm2_1, dm3_1, dm4_1, dm5_1], jnp.zeros((1, 6 * D), F32)))]
    g_wa2 = jnp.stack([g_wd[:RANK, :KD], g_wd[RANK:2 * RANK, KD:]])
    small_grads = [jnp.stack(dmod), jnp.concatenate([g_nmix0 + g_nmix0c, g_nmix1], 0), jnp.concatenate([g_nffn0, g_nffn1], 0),
                   g_head, jnp.concatenate([g_fcb0, g_fcb1], 0), g_final, g_wa2, g_bd.reshape(2, KD), g_scw,
                   jnp.stack([g_fcw0, g_fcw1]), loss8[:1]]
    pack1, offs1 = _pack_rows(small_grads, F32, 8)
    ag1 = exchange_start("ag_grads_start", [pack1], True, loss8)
    g_gin = mm("gla_in_dw", V(hcat, "tok"), V(dpcat, "tok"), form="tn", out="cols", out_dtype=BF16, shard_n=GLA_IN // N_DEV,
               after=ag1[4])
    g_gout = row_slots(mm("gla_out_dw", V(yin0, "tok"), V(dy0, "tok"), form="tn", out_dtype=BF16, after=ag1[4]))
    mine1, land1 = exchange_wait("ag_grads_wait", ag1, [g_gin, g_gout], True)
    g1 = land1[0]
    dmod_all = _unpack_rows(g1, offs1[:1], [small_grads[0].shape])[0]
    tot = _unpack_rows(sum_slots("sum_small", g1), offs1, [a.shape for a in small_grads])
    loss = tot[10][0, 0]
    dm_rows = dmod_all[:, :, :bsz].transpose(1, 0, 2, 3).reshape(2, N_DEV * bsz, 6 * D)
    dm_full = jnp.concatenate([dm_rows, tot[0][:, bsz:], jnp.zeros((2, ADA_ROWS - N_DEV * bsz - 1, 6 * D), F32)], 1)
    dm_mine = lax.dynamic_slice(dm_full, (0, 0, me * ADA_COLS), (2, ADA_ROWS, ADA_COLS))
    g_ada_w, g_ada_b, cpart = ada_bwd(cond, dm_mine, dm_full, ada_w)
    cparts = all_gather("ag_cctx", cpart, True).reshape(N_DEV, ADA_ROWS - ADA_CTX_ROW, D)[:, 0]
    g_cctx = cctx_grad(cparts, c_ctx[None])[0]
    tok = send_grads("gla", [g_gin, g_gout], after=g_cctx)

    def my_cols(full, n):
        return lax.dynamic_slice_in_dim(full, me * n, n, axis=full.ndim - 1)

    grads = {
        "c_ctx": g_cctx, "ada_b": g_ada_b.reshape(2, 6 * D), "norm_mix": tot[1], "norm_ffn": tot[2],
        "gla_head_norm": tot[3], "ffn_conv_b": tot[4], "final_norm": tot[5].reshape(D),
        "gla_w_a2": my_cols(tot[6], KD // N_DEV)[None], "gla_b_a": my_cols(tot[7], KD // N_DEV)[None],
        "sc_conv_w": my_cols(tot[8], D // N_DEV)[None], "ffn_conv_w": my_cols(tot[9], 2 * FFN_H // N_DEV),
    }

    res_ada = adamw("adamw_ada", *[a.reshape(2 * D, ADA_COLS) for a in (ada_w, g_ada_w, m_ada_w, v_ada_w)])
    grads["c_ctx"] = g_cctx + tok
    big = ["gla_w_in", "gla_w_out", "sc_w_in", "sc_w_out", "ffn_w_up", "ffn_w_down"]
    small = [n for n in names if n not in big and n != "ada_w"]
    g_small = _pack_rows([grads[n] for n in small], F32, 8)[0]
    res_small = adamw("adamw_small", _pack_rows([w_[n] for n in small], F32, 8)[0], g_small,
                      _pack_rows([m_[n] for n in small], F32, 8)[0], _pack_rows([v_[n] for n in small], F32, 8)[0])
    offs_s = _pack_rows([w_[n] for n in small], F32, 8)[1]

    big_res, done = {}, [res_small[0], res_ada[0]]
    for g in groups:
        sent, lands = exchange_wait(f"a2a_{g}_wait", a2a_started[g], done, False)
        for (n, i), mine, land in zip(groups[g], sent, lands):
            big_res[(n, i)] = adamw(f"adamw_{n}{i}", w_[n], land, m_[n], v_[n], layer=i)
            done.append(big_res[(n, i)][0])

    out = {}
    for kind, idx in (("grad", 0), ("delta", 1), ("new_m", 2), ("new_v", 3)):
        vals = {n: jnp.stack([big_res[(n, i)][idx] for i in range(w_[n].shape[0])]) for n in big}
        vals["ada_w"] = res_ada[idx].reshape(ada_w.shape)
        vals.update(zip(small, _unpack_rows(res_small[idx], offs_s, [w_[n].shape for n in small])))
        out[kind] = [vals[n] for n in names]
    return (loss, grad_x, *out["grad"], *out["delta"], *out["new_m"], *out["new_v"])
```

```python
import functools

import jax
import jax.numpy as jnp
from jax import lax
from jax.experimental import pallas as pl
from jax.experimental.pallas import tpu as pltpu

F32 = jnp.float32
BF16 = jnp.bfloat16

N_DEV = 8
D = 1024
SEQ = 2048
CTX = 256
TT = CTX + SEQ
GRID_W = 64
CHUNK = 64
HEADS = 4
HK = 128
HV = 256
KD = 512
VD = 1024
RANK = 16
TAU = 16.0
GLA_IN = 3104
GLA_IN_PAD = 3200
FFN_H = 2560
FFN_TC = 256
EPS = 1e-6
LR, B1, B2, AEPS, WD, STEP = 0.001, 0.9, 0.999, 1e-08, 0.01, 10
MESH = pl.DeviceIdType.MESH


def _blocks(n):
    return [n] + [t for t in range(n - n % 128, 0, -128) if n % t == 0 and t != n]


def V(arr, kind="flat", width=None):
    if kind == "tok":
        return V(arr.reshape(-1, arr.shape[-1]))
    if kind == "flat":
        r, c = arr.shape
        return dict(a=arr, kind=kind, shape=(r, c), rows=_blocks(r), cols=_blocks(c))
    if kind == "planes":
        bsz, _, t, ch = arr.shape
        return dict(a=arr, kind=kind, shape=(bsz * t, 2 * ch), rows=_blocks(t), cols=[2 * ch] + _blocks(ch), t=t, ch=ch)
    _, r, n = arr.shape
    if width is not None:
        return dict(a=arr, kind=kind, shape=(r, width), rows=_blocks(r), cols=[width], n=n, pad=width - N_DEV * n)
    return dict(a=arr, kind=kind, shape=(r, N_DEV * n), rows=_blocks(r), cols=[8 * n, 4 * n, 2 * n], n=n, pad=0)


def _view_spec(v, br, bc, idx):
    if v["kind"] == "flat":
        return pl.BlockSpec((br, bc), idx)
    if v["kind"] == "planes":
        nt = v["t"] // br
        if bc == 2 * v["ch"]:
            return pl.BlockSpec((None, 2, br, v["ch"]), lambda i, j, k: (idx(i, j, k)[0] // nt, 0, idx(i, j, k)[0] % nt, 0))
        nch = v["ch"] // bc

        def at(i, j, k):
            r, c = idx(i, j, k)
            return r // nt, c // nch, r % nt, c % nch
        return pl.BlockSpec((None, None, br, bc), at)
    return pl.BlockSpec(((bc - v["pad"]) // v["n"], br, v["n"]), lambda i, j, k: (idx(i, j, k)[1], idx(i, j, k)[0], 0))


def _out_view(kind, rows, cols, dtype, planes_t=None, shard_n=None):
    if kind == "flat":
        shape = (rows, cols)
    elif kind == "planes":
        shape = (rows // planes_t, 2, planes_t, cols // 2)
    elif shard_n is not None:
        return V(jax.ShapeDtypeStruct((N_DEV, rows, shard_n), dtype), kind, width=cols)
    else:
        shape = (N_DEV, rows, cols // N_DEV)
    return V(jax.ShapeDtypeStruct(shape, dtype), kind)


MM_VMEM_BUDGET = 40 * 2 ** 20
MM_VMEM_LIMIT = 56 * 2 ** 20
MM_MAX_TILE = 1536


def _mm_tiles(m, n, kk, ms, ns, ks, a_bytes, b_bytes, o_bytes):
    best = None
    for tk in ks:
        for tm in [t for t in ms if t <= MM_MAX_TILE] or ms:
            for tn in [t for t in ns if t <= MM_MAX_TILE] or ns:
                one_k = tk == kk
                need = 2 * (tm * tk * a_bytes + tk * tn * b_bytes + tm * tn * o_bytes) + (0 if one_k else tm * tn * 4)
                if need > MM_VMEM_BUDGET:
                    continue
                steps = (m // tm) * (n // tn) * (kk // tk)
                traffic = (m * kk * a_bytes * (1 if one_k else n // tn)
                           + kk * n * b_bytes * (1 if one_k and n == tn else m // tm) + m * n * o_bytes)
                fill = (tm * tk * a_bytes + tk * tn * b_bytes) / 2.5e12
                cost = max(2.0 * m * n * kk / (9e14 if one_k else 6.5e14), traffic / 2.5e12) + steps * 0.4e-6 + fill
                if best is None or cost < best[0]:
                    best = (cost, tm, tn, tk)
    return best[1:]


def mm(name, a, b, form="nn", out="flat", out_dtype=F32, planes_t=None, shard_n=None, after=None):
    (m, kk) = a["shape"][::-1] if form == "tn" else a["shape"]
    n = b["shape"][0] if form == "nt" else b["shape"][1]
    assert (b["shape"][1] if form == "nt" else b["shape"][0]) == kk, (name, a["shape"], b["shape"])
    o = _out_view(out, m, n, out_dtype, planes_t, shard_n)
    a_m, a_k = (a["cols"], a["rows"]) if form == "tn" else (a["rows"], a["cols"])
    b_k, b_n = (b["cols"], b["rows"]) if form == "nt" else (b["rows"], b["cols"])
    tm, tn, tk = _mm_tiles(m, n, kk, [t for t in a_m if t in o["rows"]], [t for t in b_n if t in o["cols"]],
                           [t for t in a_k if t in b_k], a["a"].dtype.itemsize, b["a"].dtype.itemsize,
                           jnp.dtype(out_dtype).itemsize)
    nk = kk // tk
    dn = (((0 if form == "tn" else 1,), (1 if form == "nt" else 0,)), ((), ()))

    def load(ref, v):
        if len(ref.shape) == 3:
            pieces = [ref[p].astype(BF16) for p in range(ref.shape[0])]
            if v.get("pad"):
                pieces.append(jnp.zeros(ref.shape[1:2] + (v["pad"],), BF16))
            return jnp.concatenate(pieces, axis=-1)
        return ref[...].astype(BF16)

    def store(o_ref, val):
        val = val.astype(out_dtype)
        if len(o_ref.shape) == 3:
            w = o_ref.shape[-1]
            for p in range(o_ref.shape[0]):
                o_ref[p] = val[:, p * w:(p + 1) * w]
        else:
            o_ref[...] = val

    def body(a_ref, b_ref, *rest):
        o_ref, acc = rest[0 if after is None else 1], rest[1 if after is None else 2:]
        if nk == 1:
            store(o_ref, lax.dot_general(load(a_ref, a), load(b_ref, b), dn, preferred_element_type=F32))
            return
        k, acc_ref = pl.program_id(2), acc[0]

        @pl.when(k == 0)
        def _():
            acc_ref[...] = jnp.zeros_like(acc_ref)

        acc_ref[...] += lax.dot_general(load(a_ref, a), load(b_ref, b), dn, preferred_element_type=F32)

        @pl.when(k == nk - 1)
        def _():
            store(o_ref, acc_ref[...])

    if form == "tn":
        a_spec = _view_spec(a, tk, tm, lambda i, j, k: (k, i))
    else:
        a_spec = _view_spec(a, tm, tk, lambda i, j, k: (i, k))
    if form == "nt":
        b_spec = _view_spec(b, tn, tk, lambda i, j, k: (j, k))
    else:
        b_spec = _view_spec(b, tk, tn, lambda i, j, k: (k, j))
    return pl.pallas_call(
        body, name=name, grid=(m // tm, n // tn, nk),
        in_specs=[a_spec, b_spec] + ([] if after is None else [pl.BlockSpec(memory_space=pl.ANY)]),
        out_specs=_view_spec(o, tm, tn, lambda i, j, k: (i, j)), out_shape=o["a"],
        scratch_shapes=[pltpu.VMEM((tm, tn), F32)] if nk > 1 else [],
        compiler_params=pltpu.CompilerParams(dimension_semantics=("parallel", "parallel", "arbitrary"),
                                             vmem_limit_bytes=MM_VMEM_LIMIT),
    )(a["a"], b["a"], *([] if after is None else [after]))


def mm_res_mod(name, a, w, h, gate, gain, shift, scale):
    bsz, t_len, kk = a.shape
    tm = 512
    per = t_len // tm

    def body(a_ref, w_ref, h_ref, gate_ref, gain_ref, shift_ref, scale_ref, y_ref, h1_ref, hn_ref):
        y = jnp.dot(a_ref[...].astype(BF16), w_ref[...].astype(BF16), preferred_element_type=F32)
        h1 = h_ref[...] + gate_ref[...] * y
        y_ref[...] = y.astype(BF16)
        h1_ref[...] = h1
        hn_ref[...] = _mod(h1, gain_ref[...], shift_ref[...], scale_ref[...]).astype(BF16)

    def tile(width):
        return pl.BlockSpec((None, tm, width), lambda i: (i // per, i % per, 0))

    per_ex = pl.BlockSpec((None, 1, D), lambda i: (i // per, 0, 0))
    return pl.pallas_call(
        body, name=name, grid=(bsz * per,),
        in_specs=[tile(kk), pl.BlockSpec((kk, D), lambda i: (0, 0)), tile(D), per_ex, pl.BlockSpec((1, D), lambda i: (0, 0)),
                  per_ex, per_ex],
        out_specs=[tile(D)] * 3,
        out_shape=[jax.ShapeDtypeStruct((bsz, t_len, D), BF16), jax.ShapeDtypeStruct((bsz, t_len, D), F32),
                   jax.ShapeDtypeStruct((bsz, t_len, D), BF16)],
        compiler_params=pltpu.CompilerParams(dimension_semantics=("parallel",), vmem_limit_bytes=MM_VMEM_LIMIT),
    )(a, w, h, gate, gain, shift, scale)


def X(arr, w=None, co=0, ro=0, split=1, planes=False):
    return dict(a=arr, w=arr.shape[-1] if w is None else w, co=co, ro=ro, split=2 if planes else split,
                mode="planes" if planes else "cols")


def P(arr, per_example=False, w=None, split=1, rows=False):
    return dict(a=arr, e=per_example, w=arr.shape[-1] if w is None else w, split=arr.shape[-2] if rows else split,
                mode="rows" if rows else "cols")


def _pieces(ref, s):
    if s["mode"] == "planes":
        return [ref[0], ref[1]]
    if s["mode"] == "rows":
        return [ref[i:i + 1, :] for i in range(s["split"])]
    w = ref.shape[-1] // s["split"]
    return [ref[:, i * w:(i + 1) * w] for i in range(s["split"])]


def _store(ref, pieces, s, accumulate=False):
    w = ref.shape[-1] // len(pieces)
    for i, p in enumerate(pieces):
        at = (i,) if s["mode"] == "planes" else (slice(i, i + 1),) if s["mode"] == "rows" else (slice(None), slice(i * w, (i + 1) * w))
        if accumulate:
            ref[at] += p.astype(ref.dtype)
        else:
            ref[at] = p.astype(ref.dtype)


def rowwise(name, f, xs, ps, *, tm, nt, nc=1, outs=None, douts=None, dx=None, dp=None):
    bsz = xs[0]["a"].shape[0]
    fwd = douts is None
    nx, np_ = len(xs), len(ps)
    douts = [] if fwd else douts
    dx = {} if fwd else dx
    dp = [] if fwd else dp

    def x_spec(s):
        if s["mode"] == "planes":
            return pl.BlockSpec((None, 2, tm, s["w"]), lambda c, b, t, s=s: (b, 0, t + s["ro"], c + s["co"]))
        return pl.BlockSpec((None, tm, s["w"]), lambda c, b, t, s=s: (b, t + s["ro"], c + s["co"]))

    def x_out(s, dt):
        if s["mode"] == "planes":
            return (jax.ShapeDtypeStruct((bsz, 2, nt * tm, nc * s["w"]), dt),
                    pl.BlockSpec((None, 2, tm, s["w"]), lambda c, b, t: (b, 0, t, c)))
        return (jax.ShapeDtypeStruct((bsz, nt * tm, nc * s["w"]), dt), pl.BlockSpec((None, tm, s["w"]), lambda c, b, t: (b, t, c)))

    def p_spec(s):
        r = s["a"].shape[-2]
        if s["e"]:
            return pl.BlockSpec((None, r, s["w"]), lambda c, b, t: (b, 0, c))
        return pl.BlockSpec((r, s["w"]), lambda c, b, t: (0, c))

    in_specs = [x_spec(s) for s in xs] + [p_spec(s) for s in ps] + [x_spec(s) for s in douts]
    operands = [s["a"] for s in xs] + [s["a"] for s in ps] + [s["a"] for s in douts]
    if fwd:
        out_modes = [dict(mode="cols", split=sp) for (_, _, sp) in outs]
        out_shape = [jax.ShapeDtypeStruct((bsz, nt * tm, nc * w), dt) for (w, dt, _) in outs]
        out_specs = [pl.BlockSpec((None, tm, w), lambda c, b, t: (b, t, c)) for (w, _, _) in outs]
    else:
        dx_outs = [x_out(xs[i], dt) for i, dt in dx.items()]
        out_shape, out_specs = [o[0] for o in dx_outs], [o[1] for o in dx_outs]
        for j in dp:
            s = ps[j]
            r = s["a"].shape[-2]
            if s["e"]:
                out_shape.append(jax.ShapeDtypeStruct((bsz, r, nc * s["w"]), F32))
                out_specs.append(pl.BlockSpec((None, r, s["w"]), lambda c, b, t: (b, 0, c)))
            else:
                out_shape.append(jax.ShapeDtypeStruct((r, nc * s["w"]), F32))
                out_specs.append(pl.BlockSpec((r, s["w"]), lambda c, b, t: (0, c)))

    def body(*refs):
        x_refs, p_refs = refs[:nx], refs[nx:nx + np_]
        d_refs = refs[nx + np_:nx + np_ + len(douts)]
        o_refs = refs[nx + np_ + len(douts):]
        xv = [[p.astype(F32) for p in _pieces(r, s)] for r, s in zip(x_refs, xs)]
        pv = [[p.astype(F32) for p in _pieces(r, s)] for r, s in zip(p_refs, ps)]
        if fwd:
            for r, pieces, s in zip(o_refs, f(xv, pv), out_modes):
                _store(r, pieces, s)
            return
        _, vjp = jax.vjp(f, xv, pv)
        cot = [[p.astype(F32) for p in _pieces(r, s)] for r, s in zip(d_refs, douts)]
        dxv, dpv = vjp(cot)
        for r, i in zip(o_refs, dx):
            _store(r, dxv[i], xs[i])
        b, t = pl.program_id(1), pl.program_id(2)
        for r, j in zip(o_refs[len(dx):], dp):
            first = (t == 0) if ps[j]["e"] else jnp.logical_and(b == 0, t == 0)

            @pl.when(first)
            def _(r=r, j=j):
                _store(r, dpv[j], ps[j])

            @pl.when(jnp.logical_not(first))
            def _(r=r, j=j):
                _store(r, dpv[j], ps[j], accumulate=True)

    res = pl.pallas_call(
        body, name=name, grid=(nc, bsz, nt), in_specs=in_specs, out_specs=out_specs, out_shape=out_shape,
        compiler_params=pltpu.CompilerParams(dimension_semantics=("arbitrary", "arbitrary", "arbitrary")),
    )(*operands)
    return res


def _keep_rows(a, shift, keep):
    n = a.shape[0]
    t = lax.broadcasted_iota(jnp.int32, a.shape, 0)
    return jnp.where(keep(t, n), pltpu.roll(a, shift % n, 0), 0.0)


def _shift_pair(step, keep_prev, keep_next):
    @jax.custom_vjp
    def prev(a):
        return _keep_rows(a, step, keep_prev)

    @jax.custom_vjp
    def nxt(a):
        return _keep_rows(a, -step, keep_next)

    prev.defvjp(lambda a: (prev(a), None), lambda _, g: (nxt(g),))
    nxt.defvjp(lambda a: (nxt(a), None), lambda _, g: (prev(g),))
    return prev, nxt


prev_tok, next_tok = _shift_pair(1, lambda t, n: t % GRID_W != 0, lambda t, n: t % GRID_W != GRID_W - 1)
prev_row, next_row = _shift_pair(GRID_W, lambda t, n: t >= GRID_W, lambda t, n: t < n - GRID_W)


@jax.custom_vjp
def bdot(a, w):
    return jnp.dot(a.astype(BF16), w.astype(BF16), preferred_element_type=F32)


def _bdot_bwd(res, g):
    a, w = res
    gb = g.astype(BF16)
    da = lax.dot_general(gb, w.astype(BF16), (((1,), (1,)), ((), ())), preferred_element_type=F32)
    dw = lax.dot_general(a.astype(BF16), gb, (((0,), (0,)), ((), ())), preferred_element_type=F32)
    return da, dw


bdot.defvjp(lambda a, w: (bdot(a, w), (a, w)), _bdot_bwd)


@jax.custom_vjp
def log_sigmoid(z):
    return jnp.minimum(z, 0.0) - jnp.log(1.0 + jnp.exp(-jnp.abs(z)))


def _lsig_bwd(z, g):
    e = jnp.exp(-jnp.abs(z))
    return (g * jnp.where(z >= 0, e, 1.0) / (1.0 + e),)


log_sigmoid.defvjp(lambda z: (log_sigmoid(z), z), _lsig_bwd)


def silu(x):
    return x * jax.nn.sigmoid(x)


def _rms(x):
    return x * lax.rsqrt(jnp.mean(x * x, axis=-1, keepdims=True) + EPS)


def _mod(x, gain, shift, scale):
    return _rms(x) * gain * (1.0 + scale) + shift


def f_mod(xs, ps):
    ((h,),), ((gain,), (shift,), (scale,)) = xs, ps
    return [[_mod(h, gain, shift, scale)], [h]]


def f_res_mod(xs, ps):
    ((h,), (y,)), ((gate,), (gain,), (shift,), (scale,)) = xs, ps
    h1 = h + gate * y
    return [[h1], [_mod(h1, gain, shift, scale)]]


def f_ffn_mid(xs, ps):
    ((ua, ug),), ((w0a, w0g), (w1a, w1g), (w2a, w2g), (ba, bg)) = xs, ps
    a = w0a * prev_row(ua) + w1a * ua + w2a * next_row(ua) + ba
    g = w0g * prev_row(ug) + w1g * ug + w2g * next_row(ug) + bg
    return [[a * silu(g)]]


def f_sc_mid(xs, ps):
    ((bg, cg, v),), ((w0,), (w1,), (w2,)) = xs, ps
    z = cg * v
    return [[bg * (w0 * prev_tok(z) + w1 * z + w2 * next_tok(z))]]


def f_decay(xs, ps):
    ((a,),), ((wd,), (bd,)) = xs, ps
    return [[log_sigmoid(bdot(a, wd) + bd) / TAU]]


def f_gla_post(xs, ps):
    (of, ob, g), ((gain,),) = xs, ps
    return [[_rms(a + b) * gain * silu(c) for a, b, c in zip(of, ob, g)]]


NCH = TT // CHUNK
CTX_CH = CTX // CHUNK
_NT = (((1,), (1,)), ((), ()))
_TN = (((0,), (0,)), ((), ()))
_NN = (((1,), (0,)), ((), ()))


def _chunk_of(d, j):
    return jnp.where(d == 0, j, jnp.where(j < CTX_CH, CTX_CH - 1 - j, NCH + CTX_CH - 1 - j))


def _dot(a, b, dn):
    return lax.dot_general(a, b, dn, preferred_element_type=F32)


def _cumsum_rows(g, suffix):
    n = g.shape[0]
    row = lax.broadcasted_iota(jnp.int32, g.shape, 0)
    s = 1
    while s < n:
        if suffix:
            g = g + jnp.where(row < n - s, pltpu.roll(g, n - s, 0), 0.0)
        else:
            g = g + jnp.where(row >= s, pltpu.roll(g, s, 0), 0.0)
        s *= 2
    return g


def _causal(backward):
    row = lax.broadcasted_iota(jnp.int32, (CHUNK, CHUNK), 0)
    col = lax.broadcasted_iota(jnp.int32, (CHUNK, CHUNK), 1)
    return col >= row if backward else col <= row


def _gla_in_specs(bsz, rev):
    def blk(d, j):
        return _chunk_of(d, (NCH - 1 - j) if rev else j)

    return [
        pl.BlockSpec((bsz, CHUNK, KD), lambda d, j: (0, blk(d, j), 0)),
        pl.BlockSpec((bsz, CHUNK, KD), lambda d, j: (0, blk(d, j), 1)),
        pl.BlockSpec((bsz, CHUNK, VD), lambda d, j: (0, blk(d, j), 1)),
        pl.BlockSpec((bsz, CHUNK, KD), lambda d, j: (0, blk(d, j), d)),
    ], blk


def gla_fwd(pcat, la):
    bsz = pcat.shape[0]
    in_specs, blk = _gla_in_specs(bsz, False)

    def body(q_ref, k_ref, v_ref, la_ref, o_ref, s_ref, st):
        d, j = pl.program_id(0), pl.program_id(1)

        @pl.when(j == 0)
        def _():
            st[...] = jnp.zeros_like(st)

        s_ref[...] = st[...]

        def scan(backward):
            causal = _causal(backward)
            for e in range(bsz):
                g_all = la_ref[e]
                b_all = _cumsum_rows(g_all, backward)
                bl_all = jnp.sum(g_all, axis=0, keepdims=True)
                qs_all = (q_ref[e].astype(F32) * (HK ** -0.5) * jnp.exp(b_all)).astype(BF16)
                ks_all = (k_ref[e] * jnp.exp(-b_all)).astype(BF16)
                kd_all = (k_ref[e] * jnp.exp(bl_all - b_all)).astype(BF16)
                el_all = jnp.exp(bl_all)
                for h in range(HEADS):
                    ks_, vs_ = slice(h * HK, (h + 1) * HK), slice(h * HV, (h + 1) * HV)
                    qs, ks, kd, v = qs_all[:, ks_], ks_all[:, ks_], kd_all[:, ks_], v_ref[e, :, vs_].astype(BF16)
                    s = st[e, h]
                    att = jnp.where(causal, _dot(qs, ks, _NT), 0.0).astype(BF16)
                    o_ref[e, :, vs_] = _dot(qs, s.astype(BF16), _NT) + _dot(att, v, _NN)
                    st[e, h] = el_all[:, ks_] * s + _dot(v, kd, _TN)

        @pl.when(d == 0)
        def _():
            scan(False)

        @pl.when(d == 1)
        def _():
            scan(True)

    return pl.pallas_call(
        body, name="gla_fwd", grid=(2, NCH), in_specs=in_specs,
        out_specs=[pl.BlockSpec((bsz, CHUNK, VD), lambda d, j: (0, blk(d, j), d)),
                   pl.BlockSpec((bsz, None, None, HEADS, HV, HK), lambda d, j: (0, d, j, 0, 0, 0))],
        out_shape=[jax.ShapeDtypeStruct((bsz, TT, 2 * VD), F32), jax.ShapeDtypeStruct((bsz, 2, NCH, HEADS, HV, HK), F32)],
        scratch_shapes=[pltpu.VMEM((bsz, HEADS, HV, HK), F32)],
        compiler_params=pltpu.CompilerParams(dimension_semantics=("arbitrary", "arbitrary")),
    )(pcat, pcat, pcat, la)


def gla_bwd(pcat, la, s_all, do):
    bsz = pcat.shape[0]
    in_specs, blk = _gla_in_specs(bsz, True)
    in_specs += [
        pl.BlockSpec((bsz, None, None, HEADS, HV, HK), lambda d, j: (0, d, NCH - 1 - j, 0, 0, 0)),
        pl.BlockSpec((bsz, CHUNK, VD), lambda d, j: (0, jnp.maximum(blk(d, j) - CTX_CH, 0), 0)),
    ]

    def body(q_ref, k_ref, v_ref, la_ref, s_ref, do_ref, dq_ref, dk_ref, dv_ref, dla_ref, dst):
        d, j = pl.program_id(0), pl.program_id(1)

        @pl.when(j == 0)
        def _():
            dst[...] = jnp.zeros_like(dst)

        latent = blk(d, j) >= CTX_CH
        scale = HK ** -0.5

        def scan(backward):
            causal = _causal(backward)
            for e in range(bsz):
                g_all = la_ref[e]
                b_all = _cumsum_rows(g_all, backward)
                bl_all = jnp.sum(g_all, axis=0, keepdims=True)
                ex_all, ei_all, ed_all, el_all = jnp.exp(b_all), jnp.exp(-b_all), jnp.exp(bl_all - b_all), jnp.exp(bl_all)
                qs_all, ks_all, kd_all = q_ref[e].astype(F32) * scale * ex_all, k_ref[e] * ei_all, k_ref[e] * ed_all
                qsb_all, ksb_all, kdb_all = qs_all.astype(BF16), ks_all.astype(BF16), kd_all.astype(BF16)
                db_parts, dbl_parts = [], []
                for h in range(HEADS):
                    ks_, vs_ = slice(h * HK, (h + 1) * HK), slice(h * HV, (h + 1) * HV)
                    qs, ks, kd, el = qs_all[:, ks_], ks_all[:, ks_], kd_all[:, ks_], el_all[:, ks_]
                    qsb, ksb, kdb, v = qsb_all[:, ks_], ksb_all[:, ks_], kdb_all[:, ks_], v_ref[e, :, vs_].astype(BF16)
                    s, ds1 = s_ref[e, h], dst[e, h]
                    sb, ds1b = s.astype(BF16), ds1.astype(BF16)
                    dob = jnp.where(latent, do_ref[e, :, vs_], 0.0).astype(BF16)
                    att = jnp.where(causal, _dot(qsb, ksb, _NT), 0.0).astype(BF16)
                    datt = jnp.where(causal, _dot(dob, v, _NT), 0.0).astype(BF16)
                    dqs = _dot(dob, sb, _NN) + _dot(datt, ksb, _NN)
                    dks = _dot(datt, qsb, _TN)
                    dv_ref[e, :, vs_] = (_dot(att, dob, _TN) + _dot(kdb, ds1b, _NT)).astype(BF16)
                    dkd = _dot(v, ds1b, _NN)
                    dst[e, h] = _dot(dob, qsb, _TN) + el * ds1
                    del_ = jnp.sum(s * ds1, axis=0, keepdims=True)
                    dq_ref[e, :, ks_] = (dqs * ex_all[:, ks_] * scale).astype(BF16)
                    dk_ref[e, :, ks_] = (dks * ei_all[:, ks_] + dkd * ed_all[:, ks_]).astype(BF16)
                    db_parts.append(dqs * qs - dks * ks - dkd * kd)
                    dbl_parts.append(jnp.sum(dkd * kd, axis=0, keepdims=True) + del_ * el)
                dla_ref[e] = _cumsum_rows(jnp.concatenate(db_parts, -1), not backward) + jnp.concatenate(dbl_parts, -1)

        @pl.when(d == 0)
        def _():
            scan(False)

        @pl.when(d == 1)
        def _():
            scan(True)

    return pl.pallas_call(
        body, name="gla_bwd", grid=(2, NCH), in_specs=in_specs,
        out_specs=[pl.BlockSpec((None, bsz, CHUNK, KD), lambda d, j: (d, 0, blk(d, j), 0)),
                   pl.BlockSpec((None, bsz, CHUNK, KD), lambda d, j: (d, 0, blk(d, j), 0)),
                   pl.BlockSpec((None, bsz, CHUNK, VD), lambda d, j: (d, 0, blk(d, j), 0)),
                   pl.BlockSpec((bsz, CHUNK, KD), lambda d, j: (0, blk(d, j), d))],
        out_shape=[jax.ShapeDtypeStruct((2, bsz, TT, KD), BF16), jax.ShapeDtypeStruct((2, bsz, TT, KD), BF16),
                   jax.ShapeDtypeStruct((2, bsz, TT, VD), BF16), jax.ShapeDtypeStruct((bsz, TT, 2 * KD), F32)],
        scratch_shapes=[pltpu.VMEM((bsz, HEADS, HV, HK), F32)],
        compiler_params=pltpu.CompilerParams(dimension_semantics=("arbitrary", "arbitrary")),
    )(pcat, pcat, pcat, la, s_all, do)


def gla_combine(dq2, dk2, dv2, dgate, dpa):
    bsz = dgate.shape[0]
    tm = CTX

    def body(dq_ref, dk_ref, dv_ref, dg_ref, dpa_ref, o_ref):
        t = pl.program_id(1)
        o_ref[:, 0:KD] = (dq_ref[0].astype(F32) + dq_ref[1].astype(F32)).astype(BF16)
        o_ref[:, KD:2 * KD] = (dk_ref[0].astype(F32) + dk_ref[1].astype(F32)).astype(BF16)
        o_ref[:, 2 * KD:2 * KD + VD] = (dv_ref[0].astype(F32) + dv_ref[1].astype(F32)).astype(BF16)
        o_ref[:, 2 * KD + VD:2 * KD + 2 * VD] = jnp.where(t > 0, dg_ref[...], 0).astype(BF16)
        o_ref[:, 2 * KD + 2 * VD:] = dpa_ref[...].astype(BF16)

    return pl.pallas_call(
        body, name="gla_combine", grid=(bsz, TT // tm),
        in_specs=[pl.BlockSpec((2, None, tm, KD), lambda b, t: (0, b, t, 0)),
                  pl.BlockSpec((2, None, tm, KD), lambda b, t: (0, b, t, 0)),
                  pl.BlockSpec((2, None, tm, VD), lambda b, t: (0, b, t, 0)),
                  pl.BlockSpec((None, tm, VD), lambda b, t: (b, jnp.maximum(t - 1, 0), 0)),
                  pl.BlockSpec((None, tm, 128), lambda b, t: (b, t, 0))],
        out_specs=pl.BlockSpec((None, tm, GLA_IN_PAD), lambda b, t: (b, t, 0)),
        out_shape=jax.ShapeDtypeStruct((bsz, TT, GLA_IN_PAD), BF16),
        compiler_params=pltpu.CompilerParams(dimension_semantics=("arbitrary", "arbitrary")),
    )(dq2, dk2, dv2, dgate, dpa)


def final_loss(h1, fo, gate, gain, tgt):
    bsz, t_len, _ = h1.shape
    tm = 256

    def body(h_ref, f_ref, gate_ref, gain_ref, tgt_ref, loss_ref, dh_ref, df_ref, dgate_ref, dgain_ref):
        b, t = pl.program_id(0), pl.program_id(1)
        target = tgt_ref[...]

        def core(h, fo_, gate_, gain_):
            e = _rms(h + gate_ * fo_) * gain_ - target
            return jnp.sum(0.5 * jnp.sum(e * e, axis=-1, keepdims=True) / D, axis=0, keepdims=True)

        loss, vjp = jax.vjp(core, h_ref[...], f_ref[...], gate_ref[...], gain_ref[...])
        dh, df, dgate, dgain = vjp(jnp.ones((1, 1), F32))
        dh_ref[...] = dh
        df_ref[...] = df.astype(BF16)
        first = jnp.logical_and(b == 0, t == 0)

        @pl.when(first)
        def _():
            loss_ref[...] = jnp.broadcast_to(loss, loss_ref.shape)
            dgain_ref[...] = dgain

        @pl.when(jnp.logical_not(first))
        def _():
            loss_ref[...] += jnp.broadcast_to(loss, loss_ref.shape)
            dgain_ref[...] += dgain

        @pl.when(t == 0)
        def _():
            dgate_ref[...] = dgate

        @pl.when(t > 0)
        def _():
            dgate_ref[...] += dgate

    tile = pl.BlockSpec((None, tm, D), lambda b, t: (b, t, 0))
    per_ex = pl.BlockSpec((None, 1, D), lambda b, t: (b, 0, 0))
    shared = pl.BlockSpec((1, D), lambda b, t: (0, 0))
    return pl.pallas_call(
        body, name="final_loss", grid=(bsz, t_len // tm),
        in_specs=[tile, tile, per_ex, shared, tile],
        out_specs=[pl.BlockSpec((8, 128), lambda b, t: (0, 0)), tile, tile, per_ex, shared],
        out_shape=[jax.ShapeDtypeStruct((8, 128), F32), jax.ShapeDtypeStruct(h1.shape, F32),
                   jax.ShapeDtypeStruct(h1.shape, BF16), jax.ShapeDtypeStruct((bsz, 1, D), F32),
                   jax.ShapeDtypeStruct((1, D), F32)],
        compiler_params=pltpu.CompilerParams(dimension_semantics=("arbitrary", "arbitrary")),
    )(h1, fo, gate, gain, tgt)


ADA_ROWS = 24
ADA_CTX_ROW = 16
ADA_COLS = 6 * D // N_DEV


def ada_fwd(cond, w, b):
    def body(c_ref, w_ref, b_ref, o_ref):
        s = silu(c_ref[...]).astype(BF16)
        o_ref[...] = jnp.dot(s, w_ref[...].astype(BF16), preferred_element_type=F32) + b_ref[...]

    return pl.pallas_call(
        body, name="ada_fwd", grid=(2,),
        in_specs=[pl.BlockSpec((ADA_ROWS, D), lambda i: (0, 0)), pl.BlockSpec((None, D, ADA_COLS), lambda i: (i, 0, 0)),
                  pl.BlockSpec((None, 1, ADA_COLS), lambda i: (i, 0, 0))],
        out_specs=pl.BlockSpec((None, ADA_ROWS, ADA_COLS), lambda i: (i, 0, 0)),
        out_shape=jax.ShapeDtypeStruct((2, ADA_ROWS, ADA_COLS), F32),
    )(cond, w, b)


def ada_bwd(cond, dm_mine, dm_full, w):
    def body(c_ref, dm_ref, dmf_ref, w_ref, gw_ref, gb_ref, cp_ref):
        i = pl.program_id(0)
        s = silu(c_ref[...]).astype(BF16)
        dm = dm_ref[...].astype(BF16)
        gw_ref[...] = _dot(s, dm, _TN)
        gb_ref[...] = jnp.sum(dmf_ref[...], axis=0, keepdims=True)

        @pl.when(i == 0)
        def _():
            cp_ref[...] = _dot(dm_ref[ADA_CTX_ROW:, :].astype(BF16), w_ref[...].astype(BF16), _NT)

    return pl.pallas_call(
        body, name="ada_bwd", grid=(2,),
        in_specs=[pl.BlockSpec((ADA_ROWS, D), lambda i: (0, 0)), pl.BlockSpec((None, ADA_ROWS, ADA_COLS), lambda i: (i, 0, 0)),
                  pl.BlockSpec((None, ADA_ROWS, 6 * D), lambda i: (i, 0, 0)), pl.BlockSpec((None, D, ADA_COLS), lambda i: (i, 0, 0))],
        out_specs=[pl.BlockSpec((None, D, ADA_COLS), lambda i: (i, 0, 0)), pl.BlockSpec((None, 1, 6 * D), lambda i: (i, 0, 0)),
                   pl.BlockSpec((ADA_ROWS - ADA_CTX_ROW, D), lambda i: (0, 0))],
        out_shape=[jax.ShapeDtypeStruct((2, D, ADA_COLS), F32), jax.ShapeDtypeStruct((2, 1, 6 * D), F32),
                   jax.ShapeDtypeStruct((ADA_ROWS - ADA_CTX_ROW, D), F32)],
        compiler_params=pltpu.CompilerParams(dimension_semantics=("arbitrary",)),
    )(cond, dm_mine, dm_full, w)


def cctx_grad(parts, c_ctx):
    def body(p_ref, c_ref, o_ref):
        tot = p_ref[0:1, :]
        for i in range(1, N_DEV):
            tot = tot + p_ref[i:i + 1, :]
        c = c_ref[...]
        sg = jax.nn.sigmoid(c)
        o_ref[...] = tot * sg * (1.0 + c * (1.0 - sg))

    return pl.pallas_call(body, name="cctx_grad", out_shape=jax.ShapeDtypeStruct((1, D), F32))(parts, c_ctx)


def _row_tile(r):
    for t in (512, 256, 128, 80, 64, 40, 32, 16, 8):
        if r % t == 0:
            return t
    return r


def _slot_sum(ref):
    tot = ref[0].astype(F32)
    for i in range(1, ref.shape[0]):
        tot = tot + ref[i].astype(F32)
    return tot


def sum_slots(name, x):
    s, r, c = x.shape
    tr = _row_tile(r)

    def body(x_ref, o_ref):
        o_ref[...] = _slot_sum(x_ref)

    return pl.pallas_call(
        body, name=name, grid=(r // tr,), in_specs=[pl.BlockSpec((s, tr, c), lambda i: (0, i, 0))],
        out_specs=pl.BlockSpec((tr, c), lambda i: (i, 0)), out_shape=jax.ShapeDtypeStruct((r, c), F32),
    )(x)


def adamw(name, w, g, m, v, layer=None):
    r, c = w.shape[-2:]
    tr = _row_tile(r)
    stacked = g.ndim == 3

    def body(w_ref, g_ref, m_ref, v_ref, go_ref, d_ref, mo_ref, vo_ref):
        gv = _slot_sum(g_ref) if stacked else g_ref[...]
        mn = B1 * m_ref[...] + (1.0 - B1) * gv
        vn = B2 * v_ref[...] + (1.0 - B2) * jnp.square(gv)
        m_hat = mn / (1.0 - B1 ** STEP)
        v_hat = vn / (1.0 - B2 ** STEP)
        go_ref[...] = gv
        d_ref[...] = -LR * (m_hat / (jnp.sqrt(v_hat) + AEPS) + WD * w_ref[...])
        mo_ref[...] = mn
        vo_ref[...] = vn

    tile = pl.BlockSpec((tr, c), lambda i: (i, 0))
    slab = tile if layer is None else pl.BlockSpec((None, tr, c), lambda i: (layer, i, 0))
    g_spec = pl.BlockSpec((g.shape[0], tr, c), lambda i: (0, i, 0)) if stacked else tile
    return pl.pallas_call(
        body, name=name, grid=(r // tr,), in_specs=[slab, g_spec, slab, slab], out_specs=[tile] * 4,
        out_shape=[jax.ShapeDtypeStruct((r, c), F32)] * 4,
    )(w, g, m, v)


def _place():
    return lax.axis_index("x"), lax.axis_index("y"), lax.axis_index("c")


def all_gather(name, x, in_vmem):
    r, c = x.shape
    space = pltpu.VMEM if in_vmem else pl.ANY

    def body(x_ref, out_ref, send_sems, recv_sems, local_sem):
        px, py, pc = _place()
        me, sibling = (px, py, pc), (px, py, 1 - pc)
        chips = [(1 - px, py), (px, 1 - py), (1 - px, 1 - py)]

        def rows(qx, qy, qc):
            return out_ref.at[pl.ds((4 * qx + 2 * qy + qc) * r, r), :]

        def copy(k, block, to, src=None):
            return pltpu.make_async_remote_copy(
                src_ref=rows(*block) if src is None else src, dst_ref=rows(*block),
                send_sem=send_sems.at[k], recv_sem=recv_sems.at[k], device_id=to, device_id_type=MESH)

        mine = pltpu.make_async_copy(x_ref, rows(*me), local_sem)
        mine.start()
        first = [copy(0, me, sibling, src=x_ref)]
        first += [copy(1 + j, me, (*chip, pc), src=x_ref) for j, chip in enumerate(chips)]
        for cp in first:
            cp.start()
        passed = [copy(4 + j, (*chip, pc), sibling) for j, chip in enumerate(chips)]
        for j, chip in enumerate(chips):
            copy(1 + j, (*chip, pc), me).wait_recv()
            passed[j].start()
        copy(0, sibling, me).wait_recv()
        for j, chip in enumerate(chips):
            copy(4 + j, (*chip, 1 - pc), me).wait_recv()
        for cp in first + passed:
            cp.wait_send()
        mine.wait()

    return pl.pallas_call(
        body, name=name, out_shape=jax.ShapeDtypeStruct((N_DEV * r, c), x.dtype),
        in_specs=[pl.BlockSpec(memory_space=space)], out_specs=pl.BlockSpec(memory_space=space),
        scratch_shapes=[pltpu.SemaphoreType.DMA((7,)), pltpu.SemaphoreType.DMA((7,)), pltpu.SemaphoreType.DMA],
    )(x)


_HBM =pl.BlockSpec(memory_space=pltpu.HBM)
_SEM = pl.BlockSpec(memory_space=pltpu.SEMAPHORE)
_EFFECT = pltpu.SideEffectType.DATAFLOW_SIDE_EFFECTING


def _peers():
    px, py, pc = _place()
    return [(1 - px if k & 4 else px, 1 - py if k & 2 else py, 1 - pc if k & 1 else pc) for k in range(1, N_DEV)]


def _slot(dev):
    return 4 * dev[0] + 2 * dev[1] + dev[2]


def _split_copies(src_refs, land_refs, send_sems, recv_sems, gather):
    me = _slot(_place())
    return [pltpu.make_async_remote_copy(
        src_ref=src if gather else src.at[_slot(peer)], dst_ref=land.at[me],
        send_sem=send_sems.at[a * (N_DEV - 1) + k], recv_sem=recv_sems.at[a * (N_DEV - 1) + k],
        device_id=peer, device_id_type=MESH)
        for a, (src, land) in enumerate(zip(src_refs, land_refs)) for k, peer in enumerate(_peers())]


def exchange_start(name, srcs, gather, after):
    n = len(srcs)
    lands = [pltpu.HBM((N_DEV,) + s.shape if gather else s.shape, s.dtype) for s in srcs]

    def body(*refs):
        send_sems, recv_sems = refs[2 * n + 1:2 * n + 3]
        me, local_sems = _slot(_place()), refs[-1]
        own = [pltpu.make_async_copy(src if gather else src.at[me], land.at[me], local_sems.at[a])
               for a, (src, land) in enumerate(zip(refs[:n], refs[n:2 * n]))]
        for cp in own:
            cp.start()
        for cp in _split_copies(refs[:n], refs[n:2 * n], send_sems, recv_sems, gather):
            cp.start()
        for cp in own:
            cp.wait()
        refs[-2][...] = jnp.zeros_like(refs[-2])

    sems = pltpu.SemaphoreType.DMA((n * (N_DEV - 1),))
    res = pl.pallas_call(
        body, name=name,
        out_shape=(sems, sems, *[pltpu.HBM(s.shape, s.dtype) for s in srcs], *lands, jax.ShapeDtypeStruct((8, 128), F32)),
        in_specs=(_HBM,) * (2 * n) + (pl.BlockSpec(memory_space=pl.ANY),),
        out_specs=(_SEM, _SEM) + (_HBM,) * (2 * n) + (pl.BlockSpec(memory_space=pltpu.VMEM),),
        input_output_aliases={i: 2 + i for i in range(2 * n)}, scratch_shapes=[pltpu.SemaphoreType.DMA((n,))],
        compiler_params=pltpu.CompilerParams(has_side_effects=_EFFECT),
    )(*[pltpu.with_memory_space_constraint(s, pltpu.HBM) for s in srcs],
      *[pltpu.with_memory_space_constraint(lax.empty(ld.shape, ld.dtype), pltpu.HBM) for ld in lands], after)
    return res[0], res[1], list(res[2:2 + n]), list(res[2 + n:2 + 2 * n]), res[-1]


def exchange_wait(name, started, after, gather):
    send_sems, recv_sems, srcs, lands, _ = started
    n = len(srcs)
    after = list(after) if isinstance(after, (list, tuple)) else [after]

    def body(*refs):
        send_sems, recv_sems = refs[2 * n:2 * n + 2]
        for cp in _split_copies(refs[:n], refs[n:2 * n], send_sems, recv_sems, gather):
            cp.wait_send()
            cp.wait_recv()

    res = pl.pallas_call(
        body, name=name, out_shape=tuple(pltpu.HBM(a.shape, a.dtype) for a in srcs + lands),
        in_specs=(_HBM,) * (2 * n) + (_SEM, _SEM) + (pl.BlockSpec(memory_space=pl.ANY),) * len(after),
        out_specs=(_HBM,) * (2 * n), input_output_aliases={i: i for i in range(2 * n)},
        compiler_params=pltpu.CompilerParams(has_side_effects=_EFFECT),
    )(*srcs, *lands, send_sems, recv_sems, *after)
    return list(res[:n]), list(res[n:])


NCF = FFN_H // FFN_TC


def _size(shape):
    n = 1
    for s in shape:
        n *= s
    return n


def _padded_rows(n_elems, row_mult):
    return -(-n_elems // (D * row_mult)) * row_mult


def _pack_rows(arrs, dtype, row_mult):
    rows, offs, r0 = [], [], 0
    for a in arrs:
        flat = a.reshape(-1).astype(dtype)
        n = _padded_rows(flat.shape[0], row_mult)
        rows.append(jnp.pad(flat, (0, n * D - flat.shape[0])).reshape(n, D))
        offs.append(r0)
        r0 += n
    return jnp.concatenate(rows, 0), offs


def _unpack_rows(buf, offs, shapes):
    lead, out = buf.shape[:-2], []
    for o, shp in zip(offs, shapes):
        n = _size(shp)
        nr = -(-n // D)
        out.append(buf[..., o:o + nr, :].reshape(lead + (nr * D,))[..., :n].reshape(lead + tuple(shp)))
    return out


def _rows3(w):
    return [w[i:i + 1] for i in range(3)]


def f_mod1(xs, ps):
    return f_mod(xs, ps)[:1]


def kernel(x, c, ctx, c_ctx, ada_w, ada_b, norm_mix, norm_ffn, gla_w_in, gla_w_a2, gla_b_a, gla_head_norm, gla_w_out, sc_w_in, sc_conv_w, sc_w_out, ffn_w_up, ffn_conv_w, ffn_conv_b, ffn_w_down, final_norm, loss_target, m_c_ctx, m_ada_w, m_ada_b, m_norm_mix, m_norm_ffn, m_gla_w_in, m_gla_w_a2, m_gla_b_a, m_gla_head_norm, m_gla_w_out, m_sc_w_in, m_sc_conv_w, m_sc_w_out, m_ffn_w_up, m_ffn_conv_w, m_ffn_conv_b, m_ffn_w_down, m_final_norm, v_c_ctx, v_ada_w, v_ada_b, v_norm_mix, v_norm_ffn, v_gla_w_in, v_gla_w_a2, v_gla_b_a, v_gla_head_norm, v_gla_w_out, v_sc_w_in, v_sc_conv_w, v_sc_w_out, v_ffn_w_up, v_ffn_conv_w, v_ffn_conv_b, v_ffn_w_down, v_final_norm):
    names = ["c_ctx", "ada_w", "ada_b", "norm_mix", "norm_ffn", "gla_w_in", "gla_w_a2", "gla_b_a", "gla_head_norm",
             "gla_w_out", "sc_w_in", "sc_conv_w", "sc_w_out", "ffn_w_up", "ffn_conv_w", "ffn_conv_b", "ffn_w_down",
             "final_norm"]
    w_ = dict(zip(names, [c_ctx, ada_w, ada_b, norm_mix, norm_ffn, gla_w_in, gla_w_a2, gla_b_a, gla_head_norm, gla_w_out,
                          sc_w_in, sc_conv_w, sc_w_out, ffn_w_up, ffn_conv_w, ffn_conv_b, ffn_w_down, final_norm]))
    m_ = dict(zip(names, [m_c_ctx, m_ada_w, m_ada_b, m_norm_mix, m_norm_ffn, m_gla_w_in, m_gla_w_a2, m_gla_b_a,
                          m_gla_head_norm, m_gla_w_out, m_sc_w_in, m_sc_conv_w, m_sc_w_out, m_ffn_w_up, m_ffn_conv_w,
                          m_ffn_conv_b, m_ffn_w_down, m_final_norm]))
    v_ = dict(zip(names, [v_c_ctx, v_ada_w, v_ada_b, v_norm_mix, v_norm_ffn, v_gla_w_in, v_gla_w_a2, v_gla_b_a,
                          v_gla_head_norm, v_gla_w_out, v_sc_w_in, v_sc_conv_w, v_sc_w_out, v_ffn_w_up, v_ffn_conv_w,
                          v_ffn_conv_b, v_ffn_w_down, v_final_norm]))
    me = 4 * lax.axis_index("x") + 2 * lax.axis_index("y") + lax.axis_index("c")
    bsz = x.shape[0]
    tm = 256
    nt = SEQ // tm
    ctx_tiles = CTX // tm
    pe = functools.partial(P, per_example=True)

    groups = {"ffn1": [("ffn_w_up", 1), ("ffn_w_down", 1)], "sc": [("sc_w_in", 0), ("sc_w_out", 0)],
              "ffn0": [("ffn_w_up", 0), ("ffn_w_down", 0)], "gla": [("gla_w_in", 0), ("gla_w_out", 0)]}
    ag_groups = {"gin": [("gla_w_in", 0)], "ffn0": [("gla_w_out", 0), ("ffn_w_up", 0), ("ffn_w_down", 0)],
                 "sc": groups["sc"], "ffn1": groups["ffn1"]}
    ag_started = {}

    def start_gather(g, after):
        ag_started[g] = exchange_start(f"ag_{g}_start", [w_[n][i].astype(BF16) for n, i in ag_groups[g]], True, after)
        return ag_started[g][4]

    small_sharded = [c, gla_w_a2, gla_b_a, sc_conv_w, ffn_conv_w]
    pack0, offs0 = _pack_rows(small_sharded, F32, 8)
    g0 = all_gather("ag_small", pack0, True).reshape(N_DEV, pack0.shape[0], D)
    c_all, wa2_s, ba_s, scw_s, fcw_s = _unpack_rows(g0, offs0, [a.shape for a in small_sharded])
    w_a2 = wa2_s[:, 0].transpose(1, 2, 0, 3).reshape(2, RANK, KD)
    b_a = ba_s[:, 0].transpose(1, 0, 2).reshape(2, KD)
    sc_cw = scw_s[:, 0].transpose(1, 0, 2).reshape(3, D)
    ffn_cw = fcw_s.transpose(1, 2, 0, 3).reshape(2, 3, 2 * FFN_H)

    cond = jnp.concatenate([c_all.reshape(N_DEV * bsz, D), c_ctx[None], jnp.zeros((ADA_ROWS - N_DEV * bsz - 1, D), F32)], 0)
    b_mine = lax.dynamic_slice(ada_b, (0, me * ADA_COLS), (2, ADA_COLS)).reshape(2, 1, ADA_COLS)
    mod_part = ada_fwd(cond, ada_w, b_mine)
    mod = all_gather("ag_mod", mod_part.reshape(2 * ADA_ROWS, ADA_COLS), True)
    mod = mod.reshape(N_DEV, 2, ADA_ROWS, ADA_COLS).transpose(1, 2, 0, 3).reshape(2, ADA_ROWS, 6 * D)
    mods = lax.dynamic_slice(mod, (0, bsz * me, 0), (2, bsz, 6 * D))
    md = [[mods[i][:, k * D:(k + 1) * D].reshape(bsz, 1, D) for k in range(6)] for i in range(2)]
    mc = [mod[0, ADA_CTX_ROW, k * D:(k + 1) * D][None] for k in range(2)]

    tok = mod
    for g in ag_groups:
        tok = start_gather(g, tok)
    norm_mix = norm_mix + tok[0, 0]

    def gathered(g, after):
        mine, lands = exchange_wait(f"ag_{g}_wait", ag_started[g], after, True)
        return lands

    s_up, w_down = [None, None], [None, None]
    wd = jnp.zeros((128, 2 * KD), F32).at[:RANK, :KD].set(w_a2[0]).at[RANK:2 * RANK, KD:].set(w_a2[1])
    bd = b_a.reshape(1, 2 * KD)
    scw = _rows3(sc_cw)
    head_gain = gla_head_norm.reshape(1, HV)
    gains_mix = [norm_mix[i][None] for i in range(2)]
    gains_ffn = [norm_ffn[i][None] for i in range(2)]

    def tokens(a2d, t_len):
        return a2d.reshape(bsz, t_len, -1)

    def ffn_params(i):
        rows = [ffn_cw[i][t] for t in range(3)] + [ffn_conv_b[i]]
        return [P(a.reshape(2, FFN_H), w=FFN_TC, rows=True) for a in rows]

    def ffn_fwd(i, hn2):
        u = mm(f"ffn_up{i}", V(hn2, "tok"), V(s_up[i], "cols"), out="planes", out_dtype=BF16, planes_t=SEQ)
        act = rowwise(f"ffn_mid{i}", f_ffn_mid, [X(u, w=FFN_TC, planes=True)], ffn_params(i), tm=SEQ, nt=1, nc=NCF,
                      outs=[(FFN_TC, BF16, 1)])[0]
        return u, act

    def arrays(ps):
        return [p["a"] for p in ps]

    ps_in0 = [P(gains_mix[0]), pe(md[0][0]), pe(md[0][1])]
    ps_ctx = [P(gains_mix[0]), P(mc[0]), P(mc[1])]
    hn0 = rowwise("mod_in0", f_mod, [X(x)], ps_in0, tm=tm, nt=nt, outs=[(D, BF16, 1)])[0]
    hnc = rowwise("mod_ctx", f_mod, [X(ctx)], ps_ctx, tm=tm, nt=ctx_tiles, outs=[(D, BF16, 1)])[0]
    hcat = jnp.concatenate([hnc, hn0], axis=1)
    (s_gin,) = gathered("gin", hcat)
    w_gin = V(s_gin, "cols", width=GLA_IN_PAD)
    pcat = tokens(mm("gla_in", V(hcat, "tok"), w_gin, out_dtype=BF16), TT)
    pa_x = X(pcat, w=128, co=(GLA_IN_PAD - 128) // 128)
    la = rowwise("gla_decay", f_decay, [pa_x], [P(wd), P(bd)], tm=tm, nt=TT // tm, outs=[(2 * KD, F32, 1)])[0]
    o2, s_all = gla_fwd(pcat, la)
    post_xs = [X(o2, w=VD, co=0, ro=ctx_tiles, split=HEADS), X(o2, w=VD, co=1, ro=ctx_tiles, split=HEADS),
               X(pcat, w=VD, co=2, ro=ctx_tiles, split=HEADS)]
    yin0 = rowwise("gla_post", f_gla_post, post_xs, [P(head_gain)], tm=tm, nt=nt, outs=[(VD, BF16, HEADS)])[0]
    s_gout, s_up[0], s_down0 = gathered("ffn0", yin0)
    w_gout, w_down[0] = s_gout.reshape(VD, D), s_down0.reshape(FFN_H, D)
    ps_mid0 = [pe(md[0][2]), P(gains_ffn[0]), pe(md[0][3]), pe(md[0][4])]
    y0, h1_0, hn2_0 = mm_res_mod("gla_out", yin0, w_gout, x, *arrays(ps_mid0))
    u0, act0 = ffn_fwd(0, hn2_0)
    ps_in1 = [pe(md[0][5]), P(gains_mix[1]), pe(md[1][0]), pe(md[1][1])]
    fo0, h2_0, hn1 = mm_res_mod("ffn_down0", act0, w_down[0], h1_0, *arrays(ps_in1))

    s_sin, s_sout = gathered("sc", hn1)
    w_sout = s_sout.reshape(D, D)
    p1 = tokens(mm("sc_in", V(hn1, "tok"), V(s_sin, "cols")), SEQ)
    sc_ps = [P(a) for a in scw]
    yin1 = rowwise("sc_mid", f_sc_mid, [X(p1, split=3)], sc_ps, tm=tm, nt=nt, outs=[(D, BF16, 1)])[0]
    ps_mid1 = [pe(md[1][2]), P(gains_ffn[1]), pe(md[1][3]), pe(md[1][4])]
    y1, h1_1, hn2_1 = mm_res_mod("sc_out", yin1, w_sout, h2_0, *arrays(ps_mid1))
    s_up[1], s_down1 = gathered("ffn1", hn2_1)
    w_down[1] = s_down1.reshape(FFN_H, D)
    u1, act1 = ffn_fwd(1, hn2_1)
    fo1 = tokens(mm("ffn_down1", V(act1, "tok"), V(w_down[1])), SEQ)
    loss8, dh1_1, dfo1, dm5_1, g_final = final_loss(h1_1, fo1, md[1][5], final_norm[None], loss_target)

    def ffn_bwd(i, u, act, hn2, dfo):
        dact = tokens(mm(f"ffn_down_dx{i}", V(dfo, "tok"), V(w_down[i]), form="nt", out_dtype=BF16), SEQ)
        g_down = mm(f"ffn_down_dw{i}", V(act, "tok"), V(dfo, "tok"), form="tn", out_dtype=BF16)
        r = rowwise(f"ffn_mid_bwd{i}", f_ffn_mid, [X(u, w=FFN_TC, planes=True)], ffn_params(i), tm=SEQ, nt=1, nc=NCF,
                    douts=[X(dact, w=FFN_TC)], dx={0: BF16}, dp=[0, 1, 2, 3])
        du, g_cw, g_cb = r[0], jnp.stack([a.reshape(2 * FFN_H) for a in r[1:4]]), r[4].reshape(1, 2 * FFN_H)
        dhn2 = tokens(mm(f"ffn_up_dx{i}", V(du, "planes"), V(s_up[i], "cols"), form="nt", out_dtype=BF16), SEQ)
        g_up = mm(f"ffn_up_dw{i}", V(hn2, "tok"), V(du, "planes"), form="tn", out="cols", out_dtype=BF16)
        return dhn2, g_up, row_slots(g_down), g_cw, g_cb

    def res_mod_bwd(name, h, y, ps, dh1, dhn):
        return rowwise(name, f_res_mod, [X(h), X(y)], ps, tm=tm, nt=nt, douts=[X(dh1), X(dhn)],
                       dx={0: F32, 1: BF16}, dp=[0, 1, 2, 3])

    def row_slots(g):
        return g.reshape(N_DEV, -1, g.shape[-1])

    a2a_started = {}

    def send_grads(g, slots, after=None):
        a2a_started[g] = exchange_start(f"a2a_{g}_start", list(slots), False, loss8 if after is None else after)
        return a2a_started[g][4][0, 0]

    def after_start(ps, tok):
        return [dict(ps[0], a=ps[0]["a"] + tok)] + ps[1:]

    dhn2_1, g_up1, g_down1, g_fcw1, g_fcb1 = ffn_bwd(1, u1, act1, hn2_1, dfo1)
    tok = send_grads("ffn1", [g_up1, g_down1])
    dh2_0, dy1, dm2_1, g_nffn1, dm3_1, dm4_1 = res_mod_bwd("res_mod_mid1_bwd", h2_0, y1, after_start(ps_mid1, tok), dh1_1, dhn2_1)
    dyin1 = tokens(mm("sc_out_dx", V(dy1, "tok"), V(w_sout), form="nt", out_dtype=BF16), SEQ)
    g_sout = row_slots(mm("sc_out_dw", V(yin1, "tok"), V(dy1, "tok"), form="tn", out_dtype=BF16))
    r = rowwise("sc_mid_bwd", f_sc_mid, [X(p1, split=3)], sc_ps, tm=tm, nt=nt, douts=[X(dyin1)], dx={0: BF16}, dp=[0, 1, 2])
    dp1, g_scw = r[0], jnp.concatenate(r[1:4], 0)
    dhn1 = tokens(mm("sc_in_dx", V(dp1, "tok"), V(s_sin, "cols"), form="nt", out_dtype=BF16), SEQ)
    g_sin = mm("sc_in_dw", V(hn1, "tok"), V(dp1, "tok"), form="tn", out="cols", out_dtype=BF16)
    tok = send_grads("sc", [g_sin, g_sout])
    dh1_0, dfo0, dm5_0, g_nmix1, dm0_1, dm1_1 = res_mod_bwd("res_mod_in1_bwd", h1_0, fo0, after_start(ps_in1, tok), dh2_0, dhn1)

    dhn2_0, g_up0, g_down0, g_fcw0, g_fcb0 = ffn_bwd(0, u0, act0, hn2_0, dfo0)
    tok = send_grads("ffn0", [g_up0, g_down0])
    dx_res, dy0, dm2_0, g_nffn0, dm3_0, dm4_0 = res_mod_bwd("res_mod_mid0_bwd", x, y0, after_start(ps_mid0, tok), dh1_0, dhn2_0)
    dyin0 = tokens(mm("gla_out_dx", V(dy0, "tok"), V(w_gout), form="nt", out_dtype=BF16), SEQ)
    do, dgate, g_head = rowwise("gla_post_bwd", f_gla_post, post_xs, [P(head_gain)], tm=tm, nt=nt,
                                douts=[X(dyin0, split=HEADS)], dx={0: BF16, 2: BF16}, dp=[0])
    dq2, dk2, dv2, dla = gla_bwd(pcat, la, s_all, do)
    dpa, g_wd, g_bd = rowwise("gla_decay_bwd", f_decay, [pa_x], [P(wd), P(bd)], tm=tm, nt=TT // tm, douts=[X(dla)],
                              dx={0: BF16}, dp=[0, 1])
    dpcat = gla_combine(dq2, dk2, dv2, dgate, dpa)
    dhcat = tokens(mm("gla_in_dx", V(dpcat, "tok"), w_gin, form="nt", out_dtype=BF16), TT)
    grad_x, g_nmix0, dm0_0, dm1_0 = rowwise("mod_in0_bwd", f_mod, [X(x)], ps_in0, tm=tm, nt=nt,
                                            douts=[X(dhcat, ro=ctx_tiles), X(dx_res)], dx={0: F32}, dp=[0, 1, 2])
    g_nmix0c, dmc0, dmc1 = rowwise("mod_ctx_bwd", f_mod1, [X(ctx)], ps_ctx, tm=tm, nt=ctx_tiles, douts=[X(dhcat)],
                                   dx={}, dp=[0, 1, 2])

    zero_row = jnp.zeros((1, 4 * D), F32)
    dmod = [jnp.concatenate([jnp.concatenate([a.reshape(bsz, D) for a in dms], 1), ctx_row], 0)
            for dms, ctx_row in (([dm0_0, dm1_0, dm2_0, dm3_0, dm4_0, dm5_0], jnp.concatenate([dmc0, dmc1, zero_row], 1)),
                                 ([dm0_1, dm1_1, dm2_1, dm3_1, dm4_1, dm5_1], jnp.zeros((1, 6 * D), F32)))]
    g_wa2 = jnp.stack([g_wd[:RANK, :KD], g_wd[RANK:2 * RANK, KD:]])
    small_grads = [jnp.stack(dmod), jnp.concatenate([g_nmix0 + g_nmix0c, g_nmix1], 0), jnp.concatenate([g_nffn0, g_nffn1], 0),
                   g_head, jnp.concatenate([g_fcb0, g_fcb1], 0), g_final, g_wa2, g_bd.reshape(2, KD), g_scw,
                   jnp.stack([g_fcw0, g_fcw1]), loss8[:1]]
    pack1, offs1 = _pack_rows(small_grads, F32, 8)
    ag1 = exchange_start("ag_grads_start", [pack1], True, loss8)
    g_gin = mm("gla_in_dw", V(hcat, "tok"), V(dpcat, "tok"), form="tn", out="cols", out_dtype=BF16, shard_n=GLA_IN // N_DEV,
               after=ag1[4])
    g_gout = row_slots(mm("gla_out_dw", V(yin0, "tok"), V(dy0, "tok"), form="tn", out_dtype=BF16, after=ag1[4]))
    mine1, land1 = exchange_wait("ag_grads_wait", ag1, [g_gin, g_gout], True)
    g1 = land1[0]
    dmod_all = _unpack_rows(g1, offs1[:1], [small_grads[0].shape])[0]
    tot = _unpack_rows(sum_slots("sum_small", g1), offs1, [a.shape for a in small_grads])
    loss = tot[10][0, 0]
    dm_rows = dmod_all[:, :, :bsz].transpose(1, 0, 2, 3).reshape(2, N_DEV * bsz, 6 * D)
    dm_full = jnp.concatenate([dm_rows, tot[0][:, bsz:], jnp.zeros((2, ADA_ROWS - N_DEV * bsz - 1, 6 * D), F32)], 1)
    dm_mine = lax.dynamic_slice(dm_full, (0, 0, me * ADA_COLS), (2, ADA_ROWS, ADA_COLS))
    g_ada_w, g_ada_b, cpart = ada_bwd(cond, dm_mine, dm_full, ada_w)
    cparts = all_gather("ag_cctx", cpart, True).reshape(N_DEV, ADA_ROWS - ADA_CTX_ROW, D)[:, 0]
    g_cctx = cctx_grad(cparts, c_ctx[None])[0]
    tok = send_grads("gla", [g_gin, g_gout], after=g_cctx)

    def my_cols(full, n):
        return lax.dynamic_slice_in_dim(full, me * n, n, axis=full.ndim - 1)

    grads = {
        "c_ctx": g_cctx, "ada_b": g_ada_b.reshape(2, 6 * D), "norm_mix": tot[1], "norm_ffn": tot[2],
        "gla_head_norm": tot[3], "ffn_conv_b": tot[4], "final_norm": tot[5].reshape(D),
        "gla_w_a2": my_cols(tot[6], KD // N_DEV)[None], "gla_b_a": my_cols(tot[7], KD // N_DEV)[None],
        "sc_conv_w": my_cols(tot[8], D // N_DEV)[None], "ffn_conv_w": my_cols(tot[9], 2 * FFN_H // N_DEV),
    }

    res_ada = adamw("adamw_ada", *[a.reshape(2 * D, ADA_COLS) for a in (ada_w, g_ada_w, m_ada_w, v_ada_w)])
    grads["c_ctx"] = g_cctx + tok
    big = ["gla_w_in", "gla_w_out", "sc_w_in", "sc_w_out", "ffn_w_up", "ffn_w_down"]
    small = [n for n in names if n not in big and n != "ada_w"]
    g_small = _pack_rows([grads[n] for n in small], F32, 8)[0]
    res_small = adamw("adamw_small", _pack_rows([w_[n] for n in small], F32, 8)[0], g_small,
                      _pack_rows([m_[n] for n in small], F32, 8)[0], _pack_rows([v_[n] for n in small], F32, 8)[0])
    offs_s = _pack_rows([w_[n] for n in small], F32, 8)[1]

    big_res, done = {}, [res_small[0], res_ada[0]]
    for g in groups:
        sent, lands = exchange_wait(f"a2a_{g}_wait", a2a_started[g], done, False)
        for (n, i), mine, land in zip(groups[g], sent, lands):
            big_res[(n, i)] = adamw(f"adamw_{n}{i}", w_[n], land, m_[n], v_[n], layer=i)
            done.append(big_res[(n, i)][0])

    out = {}
    for kind, idx in (("grad", 0), ("delta", 1), ("new_m", 2), ("new_v", 3)):
        vals = {n: jnp.stack([big_res[(n, i)][idx] for i in range(w_[n].shape[0])]) for n in big}
        vals["ada_w"] = res_ada[idx].reshape(ada_w.shape)
        vals.update(zip(small, _unpack_rows(res_small[idx], offs_s, [w_[n].shape for n in small])))
        out[kind] = [vals[n] for n in names]
    return (loss, grad_x, *out["grad"], *out["delta"], *out["new_m"], *out["new_v"])
```

```python
import functools

import jax
import jax.numpy as jnp
from jax import lax
from jax.experimental import pallas as pl
from jax.experimental.pallas import tpu as pltpu

F32 = jnp.float32
BF16 = jnp.bfloat16

N_DEV = 8
D = 1024
SEQ = 2048
CTX = 256
TT = CTX + SEQ
GRID_W = 64
CHUNK = 64
HEADS = 4
HK = 128
HV = 256
KD = 512
VD = 1024
RANK = 16
TAU = 16.0
GLA_IN = 3104
GLA_IN_PAD = 3200
FFN_H = 2560
FFN_TC = 256
EPS = 1e-6
LR, B1, B2, AEPS, WD, STEP = 0.001, 0.9, 0.999, 1e-08, 0.01, 10
MESH = pl.DeviceIdType.MESH


def _blocks(n):
    return [n] + [t for t in range(n - n % 128, 0, -128) if n % t == 0 and t != n]


def V(arr, kind="flat", width=None):
    if kind == "tok":
        return V(arr.reshape(-1, arr.shape[-1]))
    if kind == "flat":
        r, c = arr.shape
        return dict(a=arr, kind=kind, shape=(r, c), rows=_blocks(r), cols=_blocks(c))
    if kind == "planes":
        bsz, _, t, ch = arr.shape
        return dict(a=arr, kind=kind, shape=(bsz * t, 2 * ch), rows=_blocks(t), cols=[2 * ch] + _blocks(ch), t=t, ch=ch)
    _, r, n = arr.shape
    if width is not None:
        return dict(a=arr, kind=kind, shape=(r, width), rows=_blocks(r), cols=[width], n=n, pad=width - N_DEV * n)
    return dict(a=arr, kind=kind, shape=(r, N_DEV * n), rows=_blocks(r), cols=[8 * n, 4 * n, 2 * n], n=n, pad=0)


def _view_spec(v, br, bc, idx):
    if v["kind"] == "flat":
        return pl.BlockSpec((br, bc), idx)
    if v["kind"] == "planes":
        nt = v["t"] // br
        if bc == 2 * v["ch"]:
            return pl.BlockSpec((None, 2, br, v["ch"]), lambda i, j, k: (idx(i, j, k)[0] // nt, 0, idx(i, j, k)[0] % nt, 0))
        nch = v["ch"] // bc

        def at(i, j, k):
            r, c = idx(i, j, k)
            return r // nt, c // nch, r % nt, c % nch
        return pl.BlockSpec((None, None, br, bc), at)
    return pl.BlockSpec(((bc - v["pad"]) // v["n"], br, v["n"]), lambda i, j, k: (idx(i, j, k)[1], idx(i, j, k)[0], 0))


def _out_view(kind, rows, cols, dtype, planes_t=None, shard_n=None):
    if kind == "flat":
        shape = (rows, cols)
    elif kind == "planes":
        shape = (rows // planes_t, 2, planes_t, cols // 2)
    elif shard_n is not None:
        return V(jax.ShapeDtypeStruct((N_DEV, rows, shard_n), dtype), kind, width=cols)
    else:
        shape = (N_DEV, rows, cols // N_DEV)
    return V(jax.ShapeDtypeStruct(shape, dtype), kind)


MM_VMEM_BUDGET = 40 * 2 ** 20
MM_VMEM_LIMIT = 56 * 2 ** 20
MM_MAX_TILE = 1536


def _mm_tiles(m, n, kk, ms, ns, ks, a_bytes, b_bytes, o_bytes):
    best = None
    for tk in ks:
        for tm in [t for t in ms if t <= MM_MAX_TILE] or ms:
            for tn in [t for t in ns if t <= MM_MAX_TILE] or ns:
                one_k = tk == kk
                need = 2 * (tm * tk * a_bytes + tk * tn * b_bytes + tm * tn * o_bytes) + (0 if one_k else tm * tn * 4)
                if need > MM_VMEM_BUDGET:
                    continue
                steps = (m // tm) * (n // tn) * (kk // tk)
                traffic = (m * kk * a_bytes * (1 if one_k else n // tn)
                           + kk * n * b_bytes * (1 if one_k and n == tn else m // tm) + m * n * o_bytes)
                fill = (tm * tk * a_bytes + tk * tn * b_bytes) / 2.5e12
                cost = max(2.0 * m * n * kk / (9e14 if one_k else 6.5e14), traffic / 2.5e12) + steps * 0.4e-6 + fill
                if best is None or cost < best[0]:
                    best = (cost, tm, tn, tk)
    return best[1:]


def mm(name, a, b, form="nn", out="flat", out_dtype=F32, planes_t=None, shard_n=None, after=None):
    (m, kk) = a["shape"][::-1] if form == "tn" else a["shape"]
    n = b["shape"][0] if form == "nt" else b["shape"][1]
    assert (b["shape"][1] if form == "nt" else b["shape"][0]) == kk, (name, a["shape"], b["shape"])
    o = _out_view(out, m, n, out_dtype, planes_t, shard_n)
    a_m, a_k = (a["cols"], a["rows"]) if form == "tn" else (a["rows"], a["cols"])
    b_k, b_n = (b["cols"], b["rows"]) if form == "nt" else (b["rows"], b["cols"])
    tm, tn, tk = _mm_tiles(m, n, kk, [t for t in a_m if t in o["rows"]], [t for t in b_n if t in o["cols"]],
                           [t for t in a_k if t in b_k], a["a"].dtype.itemsize, b["a"].dtype.itemsize,
                           jnp.dtype(out_dtype).itemsize)
    nk = kk // tk
    dn = (((0 if form == "tn" else 1,), (1 if form == "nt" else 0,)), ((), ()))

    def load(ref, v):
        if len(ref.shape) == 3:
            pieces = [ref[p].astype(BF16) for p in range(ref.shape[0])]
            if v.get("pad"):
                pieces.append(jnp.zeros(ref.shape[1:2] + (v["pad"],), BF16))
            return jnp.concatenate(pieces, axis=-1)
        return ref[...].astype(BF16)

    def store(o_ref, val):
        val = val.astype(out_dtype)
        if len(o_ref.shape) == 3:
            w = o_ref.shape[-1]
            for p in range(o_ref.shape[0]):
                o_ref[p] = val[:, p * w:(p + 1) * w]
        else:
            o_ref[...] = val

    def body(a_ref, b_ref, *rest):
        o_ref, acc = rest[0 if after is None else 1], rest[1 if after is None else 2:]
        if nk == 1:
            store(o_ref, lax.dot_general(load(a_ref, a), load(b_ref, b), dn, preferred_element_type=F32))
            return
        k, acc_ref = pl.program_id(2), acc[0]

        @pl.when(k == 0)
        def _():
            acc_ref[...] = jnp.zeros_like(acc_ref)

        acc_ref[...] += lax.dot_general(load(a_ref, a), load(b_ref, b), dn, preferred_element_type=F32)

        @pl.when(k == nk - 1)
        def _():
            store(o_ref, acc_ref[...])

    if form == "tn":
        a_spec = _view_spec(a, tk, tm, lambda i, j, k: (k, i))
    else:
        a_spec = _view_spec(a, tm, tk, lambda i, j, k: (i, k))
    if form == "nt":
        b_spec = _view_spec(b, tn, tk, lambda i, j, k: (j, k))
    else:
        b_spec = _view_spec(b, tk, tn, lambda i, j, k: (k, j))
    return pl.pallas_call(
        body, name=name, grid=(m // tm, n // tn, nk),
        in_specs=[a_spec, b_spec] + ([] if after is None else [pl.BlockSpec(memory_space=pl.ANY)]),
        out_specs=_view_spec(o, tm, tn, lambda i, j, k: (i, j)), out_shape=o["a"],
        scratch_shapes=[pltpu.VMEM((tm, tn), F32)] if nk > 1 else [],
        compiler_params=pltpu.CompilerParams(dimension_semantics=("parallel", "parallel", "arbitrary"),
                                             vmem_limit_bytes=MM_VMEM_LIMIT),
    )(a["a"], b["a"], *([] if after is None else [after]))


def mm_res_mod(name, a, w, h, gate, gain, shift, scale):
    bsz, t_len, kk = a.shape
    tm = 512
    per = t_len // tm

    def body(a_ref, w_ref, h_ref, gate_ref, gain_ref, shift_ref, scale_ref, y_ref, h1_ref, hn_ref):
        y = jnp.dot(a_ref[...].astype(BF16), w_ref[...].astype(BF16), preferred_element_type=F32)
        h1 = h_ref[...] + gate_ref[...] * y
        y_ref[...] = y.astype(BF16)
        h1_ref[...] = h1
        hn_ref[...] = _mod(h1, gain_ref[...], shift_ref[...], scale_ref[...]).astype(BF16)

    def tile(width):
        return pl.BlockSpec((None, tm, width), lambda i: (i // per, i % per, 0))

    per_ex = pl.BlockSpec((None, 1, D), lambda i: (i // per, 0, 0))
    return pl.pallas_call(
        body, name=name, grid=(bsz * per,),
        in_specs=[tile(kk), pl.BlockSpec((kk, D), lambda i: (0, 0)), tile(D), per_ex, pl.BlockSpec((1, D), lambda i: (0, 0)),
                  per_ex, per_ex],
        out_specs=[tile(D)] * 3,
        out_shape=[jax.ShapeDtypeStruct((bsz, t_len, D), BF16), jax.ShapeDtypeStruct((bsz, t_len, D), F32),
                   jax.ShapeDtypeStruct((bsz, t_len, D), BF16)],
        compiler_params=pltpu.CompilerParams(dimension_semantics=("parallel",), vmem_limit_bytes=MM_VMEM_LIMIT),
    )(a, w, h, gate, gain, shift, scale)


def X(arr, w=None, co=0, ro=0, split=1, planes=False):
    return dict(a=arr, w=arr.shape[-1] if w is None else w, co=co, ro=ro, split=2 if planes else split,
                mode="planes" if planes else "cols")


def P(arr, per_example=False, w=None, split=1, rows=False):
    return dict(a=arr, e=per_example, w=arr.shape[-1] if w is None else w, split=arr.shape[-2] if rows else split,
                mode="rows" if rows else "cols")


def _pieces(ref, s):
    if s["mode"] == "planes":
        return [ref[0], ref[1]]
    if s["mode"] == "rows":
        return [ref[i:i + 1, :] for i in range(s["split"])]
    w = ref.shape[-1] // s["split"]
    return [ref[:, i * w:(i + 1) * w] for i in range(s["split"])]


def _store(ref, pieces, s, accumulate=False):
    w = ref.shape[-1] // len(pieces)
    for i, p in enumerate(pieces):
        at = (i,) if s["mode"] == "planes" else (slice(i, i + 1),) if s["mode"] == "rows" else (slice(None), slice(i * w, (i + 1) * w))
        if accumulate:
            ref[at] += p.astype(ref.dtype)
        else:
            ref[at] = p.astype(ref.dtype)


def rowwise(name, f, xs, ps, *, tm, nt, nc=1, outs=None, douts=None, dx=None, dp=None):
    bsz = xs[0]["a"].shape[0]
    fwd = douts is None
    nx, np_ = len(xs), len(ps)
    douts = [] if fwd else douts
    dx = {} if fwd else dx
    dp = [] if fwd else dp

    def x_spec(s):
        if s["mode"] == "planes":
            return pl.BlockSpec((None, 2, tm, s["w"]), lambda c, b, t, s=s: (b, 0, t + s["ro"], c + s["co"]))
        return pl.BlockSpec((None, tm, s["w"]), lambda c, b, t, s=s: (b, t + s["ro"], c + s["co"]))

    def x_out(s, dt):
        if s["mode"] == "planes":
            return (jax.ShapeDtypeStruct((bsz, 2, nt * tm, nc * s["w"]), dt),
                    pl.BlockSpec((None, 2, tm, s["w"]), lambda c, b, t: (b, 0, t, c)))
        return (jax.ShapeDtypeStruct((bsz, nt * tm, nc * s["w"]), dt), pl.BlockSpec((None, tm, s["w"]), lambda c, b, t: (b, t, c)))

    def p_spec(s):
        r = s["a"].shape[-2]
        if s["e"]:
            return pl.BlockSpec((None, r, s["w"]), lambda c, b, t: (b, 0, c))
        return pl.BlockSpec((r, s["w"]), lambda c, b, t: (0, c))

    in_specs = [x_spec(s) for s in xs] + [p_spec(s) for s in ps] + [x_spec(s) for s in douts]
    operands = [s["a"] for s in xs] + [s["a"] for s in ps] + [s["a"] for s in douts]
    if fwd:
        out_modes = [dict(mode="cols", split=sp) for (_, _, sp) in outs]
        out_shape = [jax.ShapeDtypeStruct((bsz, nt * tm, nc * w), dt) for (w, dt, _) in outs]
        out_specs = [pl.BlockSpec((None, tm, w), lambda c, b, t: (b, t, c)) for (w, _, _) in outs]
    else:
        dx_outs = [x_out(xs[i], dt) for i, dt in dx.items()]
        out_shape, out_specs = [o[0] for o in dx_outs], [o[1] for o in dx_outs]
        for j in dp:
            s = ps[j]
            r = s["a"].shape[-2]
            if s["e"]:
                out_shape.append(jax.ShapeDtypeStruct((bsz, r, nc * s["w"]), F32))
                out_specs.append(pl.BlockSpec((None, r, s["w"]), lambda c, b, t: (b, 0, c)))
            else:
                out_shape.append(jax.ShapeDtypeStruct((r, nc * s["w"]), F32))
                out_specs.append(pl.BlockSpec((r, s["w"]), lambda c, b, t: (0, c)))

    def body(*refs):
        x_refs, p_refs = refs[:nx], refs[nx:nx + np_]
        d_refs = refs[nx + np_:nx + np_ + len(douts)]
        o_refs = refs[nx + np_ + len(douts):]
        xv = [[p.astype(F32) for p in _pieces(r, s)] for r, s in zip(x_refs, xs)]
        pv = [[p.astype(F32) for p in _pieces(r, s)] for r, s in zip(p_refs, ps)]
        if fwd:
            for r, pieces, s in zip(o_refs, f(xv, pv), out_modes):
                _store(r, pieces, s)
            return
        _, vjp = jax.vjp(f, xv, pv)
        cot = [[p.astype(F32) for p in _pieces(r, s)] for r, s in zip(d_refs, douts)]
        dxv, dpv = vjp(cot)
        for r, i in zip(o_refs, dx):
            _store(r, dxv[i], xs[i])
        b, t = pl.program_id(1), pl.program_id(2)
        for r, j in zip(o_refs[len(dx):], dp):
            first = (t == 0) if ps[j]["e"] else jnp.logical_and(b == 0, t == 0)

            @pl.when(first)
            def _(r=r, j=j):
                _store(r, dpv[j], ps[j])

            @pl.when(jnp.logical_not(first))
            def _(r=r, j=j):
                _store(r, dpv[j], ps[j], accumulate=True)

    res = pl.pallas_call(
        body, name=name, grid=(nc, bsz, nt), in_specs=in_specs, out_specs=out_specs, out_shape=out_shape,
        compiler_params=pltpu.CompilerParams(dimension_semantics=("arbitrary", "arbitrary", "arbitrary")),
    )(*operands)
    return res


def _keep_rows(a, shift, keep):
    n = a.shape[0]
    t = lax.broadcasted_iota(jnp.int32, a.shape, 0)
    return jnp.where(keep(t, n), pltpu.roll(a, shift % n, 0), 0.0)


def _shift_pair(step, keep_prev, keep_next):
    @jax.custom_vjp
    def prev(a):
        return _keep_rows(a, step, keep_prev)

    @jax.custom_vjp
    def nxt(a):
        return _keep_rows(a, -step, keep_next)

    prev.defvjp(lambda a: (prev(a), None), lambda _, g: (nxt(g),))
    nxt.defvjp(lambda a: (nxt(a), None), lambda _, g: (prev(g),))
    return prev, nxt


prev_tok, next_tok = _shift_pair(1, lambda t, n: t % GRID_W != 0, lambda t, n: t % GRID_W != GRID_W - 1)
prev_row, next_row = _shift_pair(GRID_W, lambda t, n: t >= GRID_W, lambda t, n: t < n - GRID_W)


@jax.custom_vjp
def bdot(a, w):
    return jnp.dot(a.astype(BF16), w.astype(BF16), preferred_element_type=F32)


def _bdot_bwd(res, g):
    a, w = res
    gb = g.astype(BF16)
    da = lax.dot_general(gb, w.astype(BF16), (((1,), (1,)), ((), ())), preferred_element_type=F32)
    dw = lax.dot_general(a.astype(BF16), gb, (((0,), (0,)), ((), ())), preferred_element_type=F32)
    return da, dw


bdot.defvjp(lambda a, w: (bdot(a, w), (a, w)), _bdot_bwd)


@jax.custom_vjp
def log_sigmoid(z):
    return jnp.minimum(z, 0.0) - jnp.log(1.0 + jnp.exp(-jnp.abs(z)))


def _lsig_bwd(z, g):
    e = jnp.exp(-jnp.abs(z))
    return (g * jnp.where(z >= 0, e, 1.0) / (1.0 + e),)


log_sigmoid.defvjp(lambda z: (log_sigmoid(z), z), _lsig_bwd)


def silu(x):
    return x * jax.nn.sigmoid(x)


def _rms(x):
    return x * lax.rsqrt(jnp.mean(x * x, axis=-1, keepdims=True) + EPS)


def _mod(x, gain, shift, scale):
    return _rms(x) * gain * (1.0 + scale) + shift


def f_mod(xs, ps):
    ((h,),), ((gain,), (shift,), (scale,)) = xs, ps
    return [[_mod(h, gain, shift, scale)], [h]]


def f_res_mod(xs, ps):
    ((h,), (y,)), ((gate,), (gain,), (shift,), (scale,)) = xs, ps
    h1 = h + gate * y
    return [[h1], [_mod(h1, gain, shift, scale)]]


def f_ffn_mid(xs, ps):
    ((ua, ug),), ((w0a, w0g), (w1a, w1g), (w2a, w2g), (ba, bg)) = xs, ps
    a = w0a * prev_row(ua) + w1a * ua + w2a * next_row(ua) + ba
    g = w0g * prev_row(ug) + w1g * ug + w2g * next_row(ug) + bg
    return [[a * silu(g)]]


def f_sc_mid(xs, ps):
    ((bg, cg, v),), ((w0,), (w1,), (w2,)) = xs, ps
    z = cg * v
    return [[bg * (w0 * prev_tok(z) + w1 * z + w2 * next_tok(z))]]


def f_decay(xs, ps):
    ((a,),), ((wd,), (bd,)) = xs, ps
    return [[log_sigmoid(bdot(a, wd) + bd) / TAU]]


def f_gla_post(xs, ps):
    (of, ob, g), ((gain,),) = xs, ps
    return [[_rms(a + b) * gain * silu(c) for a, b, c in zip(of, ob, g)]]


NCH = TT // CHUNK
CTX_CH = CTX // CHUNK
_NT = (((1,), (1,)), ((), ()))
_TN = (((0,), (0,)), ((), ()))
_NN = (((1,), (0,)), ((), ()))


def _chunk_of(d, j):
    return jnp.where(d == 0, j, jnp.where(j < CTX_CH, CTX_CH - 1 - j, NCH + CTX_CH - 1 - j))


def _dot(a, b, dn):
    return lax.dot_general(a, b, dn, preferred_element_type=F32)


def _cumsum_rows(g, suffix):
    n = g.shape[0]
    row = lax.broadcasted_iota(jnp.int32, g.shape, 0)
    s = 1
    while s < n:
        if suffix:
            g = g + jnp.where(row < n - s, pltpu.roll(g, n - s, 0), 0.0)
        else:
            g = g + jnp.where(row >= s, pltpu.roll(g, s, 0), 0.0)
        s *= 2
    return g


def _causal(backward):
    row = lax.broadcasted_iota(jnp.int32, (CHUNK, CHUNK), 0)
    col = lax.broadcasted_iota(jnp.int32, (CHUNK, CHUNK), 1)
    return col >= row if backward else col <= row


def _gla_in_specs(bsz, rev):
    def blk(d, j):
        return _chunk_of(d, (NCH - 1 - j) if rev else j)

    return [
        pl.BlockSpec((bsz, CHUNK, KD), lambda d, j: (0, blk(d, j), 0)),
        pl.BlockSpec((bsz, CHUNK, KD), lambda d, j: (0, blk(d, j), 1)),
        pl.BlockSpec((bsz, CHUNK, VD), lambda d, j: (0, blk(d, j), 1)),
        pl.BlockSpec((bsz, CHUNK, KD), lambda d, j: (0, blk(d, j), d)),
    ], blk


def gla_fwd(pcat, la):
    bsz = pcat.shape[0]
    in_specs, blk = _gla_in_specs(bsz, False)

    def body(q_ref, k_ref, v_ref, la_ref, o_ref, s_ref, st):
        d, j = pl.program_id(0), pl.program_id(1)

        @pl.when(j == 0)
        def _():
            st[...] = jnp.zeros_like(st)

        s_ref[...] = st[...]

        def scan(backward):
            causal = _causal(backward)
            for e in range(bsz):
                g_all = la_ref[e]
                b_all = _cumsum_rows(g_all, backward)
                bl_all = jnp.sum(g_all, axis=0, keepdims=True)
                qs_all = (q_ref[e].astype(F32) * (HK ** -0.5) * jnp.exp(b_all)).astype(BF16)
                ks_all = (k_ref[e] * jnp.exp(-b_all)).astype(BF16)
                kd_all = (k_ref[e] * jnp.exp(bl_all - b_all)).astype(BF16)
                el_all = jnp.exp(bl_all)
                for h in range(HEADS):
                    ks_, vs_ = slice(h * HK, (h + 1) * HK), slice(h * HV, (h + 1) * HV)
                    qs, ks, kd, v = qs_all[:, ks_], ks_all[:, ks_], kd_all[:, ks_], v_ref[e, :, vs_].astype(BF16)
                    s = st[e, h]
                    att = jnp.where(causal, _dot(qs, ks, _NT), 0.0).astype(BF16)
                    o_ref[e, :, vs_] = _dot(qs, s.astype(BF16), _NT) + _dot(att, v, _NN)
                    st[e, h] = el_all[:, ks_] * s + _dot(v, kd, _TN)

        @pl.when(d == 0)
        def _():
            scan(False)

        @pl.when(d == 1)
        def _():
            scan(True)

    return pl.pallas_call(
        body, name="gla_fwd", grid=(2, NCH), in_specs=in_specs,
        out_specs=[pl.BlockSpec((bsz, CHUNK, VD), lambda d, j: (0, blk(d, j), d)),
                   pl.BlockSpec((bsz, None, None, HEADS, HV, HK), lambda d, j: (0, d, j, 0, 0, 0))],
        out_shape=[jax.ShapeDtypeStruct((bsz, TT, 2 * VD), F32), jax.ShapeDtypeStruct((bsz, 2, NCH, HEADS, HV, HK), F32)],
        scratch_shapes=[pltpu.VMEM((bsz, HEADS, HV, HK), F32)],
        compiler_params=pltpu.CompilerParams(dimension_semantics=("arbitrary", "arbitrary")),
    )(pcat, pcat, pcat, la)


def gla_bwd(pcat, la, s_all, do):
    bsz = pcat.shape[0]
    in_specs, blk = _gla_in_specs(bsz, True)
    in_specs += [
        pl.BlockSpec((bsz, None, None, HEADS, HV, HK), lambda d, j: (0, d, NCH - 1 - j, 0, 0, 0)),
        pl.BlockSpec((bsz, CHUNK, VD), lambda d, j: (0, jnp.maximum(blk(d, j) - CTX_CH, 0), 0)),
    ]

    def body(q_ref, k_ref, v_ref, la_ref, s_ref, do_ref, dq_ref, dk_ref, dv_ref, dla_ref, dst):
        d, j = pl.program_id(0), pl.program_id(1)

        @pl.when(j == 0)
        def _():
            dst[...] = jnp.zeros_like(dst)

        latent = blk(d, j) >= CTX_CH
        scale = HK ** -0.5

        def scan(backward):
            causal = _causal(backward)
            for e in range(bsz):
                g_all = la_ref[e]
                b_all = _cumsum_rows(g_all, backward)
                bl_all = jnp.sum(g_all, axis=0, keepdims=True)
                ex_all, ei_all, ed_all, el_all = jnp.exp(b_all), jnp.exp(-b_all), jnp.exp(bl_all - b_all), jnp.exp(bl_all)
                qs_all, ks_all, kd_all = q_ref[e].astype(F32) * scale * ex_all, k_ref[e] * ei_all, k_ref[e] * ed_all
                qsb_all, ksb_all, kdb_all = qs_all.astype(BF16), ks_all.astype(BF16), kd_all.astype(BF16)
                db_parts, dbl_parts = [], []
                for h in range(HEADS):
                    ks_, vs_ = slice(h * HK, (h + 1) * HK), slice(h * HV, (h + 1) * HV)
                    qs, ks, kd, el = qs_all[:, ks_], ks_all[:, ks_], kd_all[:, ks_], el_all[:, ks_]
                    qsb, ksb, kdb, v = qsb_all[:, ks_], ksb_all[:, ks_], kdb_all[:, ks_], v_ref[e, :, vs_].astype(BF16)
                    s, ds1 = s_ref[e, h], dst[e, h]
                    sb, ds1b = s.astype(BF16), ds1.astype(BF16)
                    dob = jnp.where(latent, do_ref[e, :, vs_], 0.0).astype(BF16)
                    att = jnp.where(causal, _dot(qsb, ksb, _NT), 0.0).astype(BF16)
                    datt = jnp.where(causal, _dot(dob, v, _NT), 0.0).astype(BF16)
                    dqs = _dot(dob, sb, _NN) + _dot(datt, ksb, _NN)
                    dks = _dot(datt, qsb, _TN)
                    dv_ref[e, :, vs_] = (_dot(att, dob, _TN) + _dot(kdb, ds1b, _NT)).astype(BF16)
                    dkd = _dot(v, ds1b, _NN)
                    dst[e, h] = _dot(dob, qsb, _TN) + el * ds1
                    del_ = jnp.sum(s * ds1, axis=0, keepdims=True)
                    dq_ref[e, :, ks_] = (dqs * ex_all[:, ks_] * scale).astype(BF16)
                    dk_ref[e, :, ks_] = (dks * ei_all[:, ks_] + dkd * ed_all[:, ks_]).astype(BF16)
                    db_parts.append(dqs * qs - dks * ks - dkd * kd)
                    dbl_parts.append(jnp.sum(dkd * kd, axis=0, keepdims=True) + del_ * el)
                dla_ref[e] = _cumsum_rows(jnp.concatenate(db_parts, -1), not backward) + jnp.concatenate(dbl_parts, -1)

        @pl.when(d == 0)
        def _():
            scan(False)

        @pl.when(d == 1)
        def _():
            scan(True)

    return pl.pallas_call(
        body, name="gla_bwd", grid=(2, NCH), in_specs=in_specs,
        out_specs=[pl.BlockSpec((None, bsz, CHUNK, KD), lambda d, j: (d, 0, blk(d, j), 0)),
                   pl.BlockSpec((None, bsz, CHUNK, KD), lambda d, j: (d, 0, blk(d, j), 0)),
                   pl.BlockSpec((None, bsz, CHUNK, VD), lambda d, j: (d, 0, blk(d, j), 0)),
                   pl.BlockSpec((bsz, CHUNK, KD), lambda d, j: (0, blk(d, j), d))],
        out_shape=[jax.ShapeDtypeStruct((2, bsz, TT, KD), BF16), jax.ShapeDtypeStruct((2, bsz, TT, KD), BF16),
                   jax.ShapeDtypeStruct((2, bsz, TT, VD), BF16), jax.ShapeDtypeStruct((bsz, TT, 2 * KD), F32)],
        scratch_shapes=[pltpu.VMEM((bsz, HEADS, HV, HK), F32)],
        compiler_params=pltpu.CompilerParams(dimension_semantics=("arbitrary", "arbitrary")),
    )(pcat, pcat, pcat, la, s_all, do)


def gla_combine(dq2, dk2, dv2, dgate, dpa):
    bsz = dgate.shape[0]
    tm = CTX

    def body(dq_ref, dk_ref, dv_ref, dg_ref, dpa_ref, o_ref):
        t = pl.program_id(1)
        o_ref[:, 0:KD] = (dq_ref[0].astype(F32) + dq_ref[1].astype(F32)).astype(BF16)
        o_ref[:, KD:2 * KD] = (dk_ref[0].astype(F32) + dk_ref[1].astype(F32)).astype(BF16)
        o_ref[:, 2 * KD:2 * KD + VD] = (dv_ref[0].astype(F32) + dv_ref[1].astype(F32)).astype(BF16)
        o_ref[:, 2 * KD + VD:2 * KD + 2 * VD] = jnp.where(t > 0, dg_ref[...], 0).astype(BF16)
        o_ref[:, 2 * KD + 2 * VD:] = dpa_ref[...].astype(BF16)

    return pl.pallas_call(
        body, name="gla_combine", grid=(bsz, TT // tm),
        in_specs=[pl.BlockSpec((2, None, tm, KD), lambda b, t: (0, b, t, 0)),
                  pl.BlockSpec((2, None, tm, KD), lambda b, t: (0, b, t, 0)),
                  pl.BlockSpec((2, None, tm, VD), lambda b, t: (0, b, t, 0)),
                  pl.BlockSpec((None, tm, VD), lambda b, t: (b, jnp.maximum(t - 1, 0), 0)),
                  pl.BlockSpec((None, tm, 128), lambda b, t: (b, t, 0))],
        out_specs=pl.BlockSpec((None, tm, GLA_IN_PAD), lambda b, t: (b, t, 0)),
        out_shape=jax.ShapeDtypeStruct((bsz, TT, GLA_IN_PAD), BF16),
        compiler_params=pltpu.CompilerParams(dimension_semantics=("arbitrary", "arbitrary")),
    )(dq2, dk2, dv2, dgate, dpa)


def final_loss(h1, fo, gate, gain, tgt):
    bsz, t_len, _ = h1.shape
    tm = 256

    def body(h_ref, f_ref, gate_ref, gain_ref, tgt_ref, loss_ref, dh_ref, df_ref, dgate_ref, dgain_ref):
        b, t = pl.program_id(0), pl.program_id(1)
        target = tgt_ref[...]

        def core(h, fo_, gate_, gain_):
            e = _rms(h + gate_ * fo_) * gain_ - target
            return jnp.sum(0.5 * jnp.sum(e * e, axis=-1, keepdims=True) / D, axis=0, keepdims=True)

        loss, vjp = jax.vjp(core, h_ref[...], f_ref[...], gate_ref[...], gain_ref[...])
        dh, df, dgate, dgain = vjp(jnp.ones((1, 1), F32))
        dh_ref[...] = dh
        df_ref[...] = df.astype(BF16)
        first = jnp.logical_and(b == 0, t == 0)

        @pl.when(first)
        def _():
            loss_ref[...] = jnp.broadcast_to(loss, loss_ref.shape)
            dgain_ref[...] = dgain

        @pl.when(jnp.logical_not(first))
        def _():
            loss_ref[...] += jnp.broadcast_to(loss, loss_ref.shape)
            dgain_ref[...] += dgain

        @pl.when(t == 0)
        def _():
            dgate_ref[...] = dgate

        @pl.when(t > 0)
        def _():
            dgate_ref[...] += dgate

    tile = pl.BlockSpec((None, tm, D), lambda b, t: (b, t, 0))
    per_ex = pl.BlockSpec((None, 1, D), lambda b, t: (b, 0, 0))
    shared = pl.BlockSpec((1, D), lambda b, t: (0, 0))
    return pl.pallas_call(
        body, name="final_loss", grid=(bsz, t_len // tm),
        in_specs=[tile, tile, per_ex, shared, tile],
        out_specs=[pl.BlockSpec((8, 128), lambda b, t: (0, 0)), tile, tile, per_ex, shared],
        out_shape=[jax.ShapeDtypeStruct((8, 128), F32), jax.ShapeDtypeStruct(h1.shape, F32),
                   jax.ShapeDtypeStruct(h1.shape, BF16), jax.ShapeDtypeStruct((bsz, 1, D), F32),
                   jax.ShapeDtypeStruct((1, D), F32)],
        compiler_params=pltpu.CompilerParams(dimension_semantics=("arbitrary", "arbitrary")),
    )(h1, fo, gate, gain, tgt)


ADA_ROWS = 24
ADA_CTX_ROW = 16
ADA_COLS = 6 * D // N_DEV


def ada_fwd(cond, w, b):
    def body(c_ref, w_ref, b_ref, o_ref):
        s = silu(c_ref[...]).astype(BF16)
        o_ref[...] = jnp.dot(s, w_ref[...].astype(BF16), preferred_element_type=F32) + b_ref[...]

    return pl.pallas_call(
        body, name="ada_fwd", grid=(2,),
        in_specs=[pl.BlockSpec((ADA_ROWS, D), lambda i: (0, 0)), pl.BlockSpec((None, D, ADA_COLS), lambda i: (i, 0, 0)),
                  pl.BlockSpec((None, 1, ADA_COLS), lambda i: (i, 0, 0))],
        out_specs=pl.BlockSpec((None, ADA_ROWS, ADA_COLS), lambda i: (i, 0, 0)),
        out_shape=jax.ShapeDtypeStruct((2, ADA_ROWS, ADA_COLS), F32),
    )(cond, w, b)


def ada_bwd(cond, dm_mine, dm_full, w):
    def body(c_ref, dm_ref, dmf_ref, w_ref, gw_ref, gb_ref, cp_ref):
        i = pl.program_id(0)
        s = silu(c_ref[...]).astype(BF16)
        dm = dm_ref[...].astype(BF16)
        gw_ref[...] = _dot(s, dm, _TN)
        gb_ref[...] = jnp.sum(dmf_ref[...], axis=0, keepdims=True)

        @pl.when(i == 0)
        def _():
            cp_ref[...] = _dot(dm_ref[ADA_CTX_ROW:, :].astype(BF16), w_ref[...].astype(BF16), _NT)

    return pl.pallas_call(
        body, name="ada_bwd", grid=(2,),
        in_specs=[pl.BlockSpec((ADA_ROWS, D), lambda i: (0, 0)), pl.BlockSpec((None, ADA_ROWS, ADA_COLS), lambda i: (i, 0, 0)),
                  pl.BlockSpec((None, ADA_ROWS, 6 * D), lambda i: (i, 0, 0)), pl.BlockSpec((None, D, ADA_COLS), lambda i: (i, 0, 0))],
        out_specs=[pl.BlockSpec((None, D, ADA_COLS), lambda i: (i, 0, 0)), pl.BlockSpec((None, 1, 6 * D), lambda i: (i, 0, 0)),
                   pl.BlockSpec((ADA_ROWS - ADA_CTX_ROW, D), lambda i: (0, 0))],
        out_shape=[jax.ShapeDtypeStruct((2, D, ADA_COLS), F32), jax.ShapeDtypeStruct((2, 1, 6 * D), F32),
                   jax.ShapeDtypeStruct((ADA_ROWS - ADA_CTX_ROW, D), F32)],
        compiler_params=pltpu.CompilerParams(dimension_semantics=("arbitrary",)),
    )(cond, dm_mine, dm_full, w)


def cctx_grad(parts, c_ctx):
    def body(p_ref, c_ref, o_ref):
        tot = p_ref[0:1, :]
        for i in range(1, N_DEV):
            tot = tot + p_ref[i:i + 1, :]
        c = c_ref[...]
        sg = jax.nn.sigmoid(c)
        o_ref[...] = tot * sg * (1.0 + c * (1.0 - sg))

    return pl.pallas_call(body, name="cctx_grad", out_shape=jax.ShapeDtypeStruct((1, D), F32))(parts, c_ctx)


def _row_tile(r):
    for t in (512, 256, 128, 80, 64, 40, 32, 16, 8):
        if r % t == 0:
            return t
    return r


def _slot_sum(ref):
    tot = ref[0].astype(F32)
    for i in range(1, ref.shape[0]):
        tot = tot + ref[i].astype(F32)
    return tot


def sum_slots(name, x):
    s, r, c = x.shape
    tr = _row_tile(r)

    def body(x_ref, o_ref):
        o_ref[...] = _slot_sum(x_ref)

    return pl.pallas_call(
        body, name=name, grid=(r // tr,), in_specs=[pl.BlockSpec((s, tr, c), lambda i: (0, i, 0))],
        out_specs=pl.BlockSpec((tr, c), lambda i: (i, 0)), out_shape=jax.ShapeDtypeStruct((r, c), F32),
    )(x)


def _adamw_update(gv, w_ref, m_ref, v_ref, go_ref, d_ref, mo_ref, vo_ref):
    mn = B1 * m_ref[...] + (1.0 - B1) * gv
    vn = B2 * v_ref[...] + (1.0 - B2) * jnp.square(gv)
    m_hat = mn / (1.0 - B1 ** STEP)
    v_hat = vn / (1.0 - B2 ** STEP)
    go_ref[...] = gv
    d_ref[...] = -LR * (m_hat / (jnp.sqrt(v_hat) + AEPS) + WD * w_ref[...])
    mo_ref[...] = mn
    vo_ref[...] = vn


def adamw_slots(name, w, land, sent, me1, m, v, layer):
    r, c = w.shape[-2:]
    tr = _row_tile(r)

    def body(me_ref, w_ref, land_ref, own_ref, m_ref, v_ref, *outs):
        own = own_ref[...].astype(F32)
        gv = jnp.where(me_ref[0] == 0, own, land_ref[0].astype(F32))
        for s in range(1, N_DEV):
            gv = gv + jnp.where(me_ref[0] == s, own, land_ref[s].astype(F32))
        _adamw_update(gv, w_ref, m_ref, v_ref, *outs)

    tile = pl.BlockSpec((tr, c), lambda i, me: (i, 0))
    slab = pl.BlockSpec((None, tr, c), lambda i, me: (layer, i, 0))
    return pl.pallas_call(
        body, name=name, out_shape=[jax.ShapeDtypeStruct((r, c), F32)] * 4,
        grid_spec=pltpu.PrefetchScalarGridSpec(
            num_scalar_prefetch=1, grid=(r // tr,),
            in_specs=[slab, pl.BlockSpec((N_DEV, tr, c), lambda i, me: (0, i, 0)),
                      pl.BlockSpec((None, tr, c), lambda i, me: (me[0], i, 0)), slab, slab],
            out_specs=[tile] * 4),
    )(me1, w, land, sent, m, v)


def adamw(name, w, g, m, v, layer=None):
    r, c = w.shape[-2:]
    tr = _row_tile(r)
    stacked = g.ndim == 3

    def body(w_ref, g_ref, m_ref, v_ref, *outs):
        _adamw_update(_slot_sum(g_ref) if stacked else g_ref[...], w_ref, m_ref, v_ref, *outs)

    tile = pl.BlockSpec((tr, c), lambda i: (i, 0))
    slab = tile if layer is None else pl.BlockSpec((None, tr, c), lambda i: (layer, i, 0))
    g_spec = pl.BlockSpec((g.shape[0], tr, c), lambda i: (0, i, 0)) if stacked else tile
    return pl.pallas_call(
        body, name=name, grid=(r // tr,), in_specs=[slab, g_spec, slab, slab], out_specs=[tile] * 4,
        out_shape=[jax.ShapeDtypeStruct((r, c), F32)] * 4,
    )(w, g, m, v)


def _place():
    return lax.axis_index("x"), lax.axis_index("y"), lax.axis_index("c")


def all_gather(name, x, in_vmem):
    r, c = x.shape
    space = pltpu.VMEM if in_vmem else pl.ANY

    def body(x_ref, out_ref, send_sems, recv_sems, local_sem):
        px, py, pc = _place()
        me, sibling = (px, py, pc), (px, py, 1 - pc)
        chips = [(1 - px, py), (px, 1 - py), (1 - px, 1 - py)]

        def rows(qx, qy, qc):
            return out_ref.at[pl.ds((4 * qx + 2 * qy + qc) * r, r), :]

        def copy(k, block, to, src=None):
            return pltpu.make_async_remote_copy(
                src_ref=rows(*block) if src is None else src, dst_ref=rows(*block),
                send_sem=send_sems.at[k], recv_sem=recv_sems.at[k], device_id=to, device_id_type=MESH)

        mine = pltpu.make_async_copy(x_ref, rows(*me), local_sem)
        mine.start()
        first = [copy(0, me, sibling, src=x_ref)]
        first += [copy(1 + j, me, (*chip, pc), src=x_ref) for j, chip in enumerate(chips)]
        for cp in first:
            cp.start()
        passed = [copy(4 + j, (*chip, pc), sibling) for j, chip in enumerate(chips)]
        for j, chip in enumerate(chips):
            copy(1 + j, (*chip, pc), me).wait_recv()
            passed[j].start()
        copy(0, sibling, me).wait_recv()
        for j, chip in enumerate(chips):
            copy(4 + j, (*chip, 1 - pc), me).wait_recv()
        for cp in first + passed:
            cp.wait_send()
        mine.wait()

    return pl.pallas_call(
        body, name=name, out_shape=jax.ShapeDtypeStruct((N_DEV * r, c), x.dtype),
        in_specs=[pl.BlockSpec(memory_space=space)], out_specs=pl.BlockSpec(memory_space=space),
        scratch_shapes=[pltpu.SemaphoreType.DMA((7,)), pltpu.SemaphoreType.DMA((7,)), pltpu.SemaphoreType.DMA],
    )(x)


_HBM =pl.BlockSpec(memory_space=pltpu.HBM)
_SEM = pl.BlockSpec(memory_space=pltpu.SEMAPHORE)
_EFFECT = pltpu.SideEffectType.DATAFLOW_SIDE_EFFECTING


def _peers():
    px, py, pc = _place()
    return [(1 - px if k & 4 else px, 1 - py if k & 2 else py, 1 - pc if k & 1 else pc) for k in range(1, N_DEV)]


def _slot(dev):
    return 4 * dev[0] + 2 * dev[1] + dev[2]


def _split_copies(src_refs, land_refs, send_sems, recv_sems, gather):
    me = _slot(_place())
    return [pltpu.make_async_remote_copy(
        src_ref=src if gather else src.at[_slot(peer)], dst_ref=land.at[me],
        send_sem=send_sems.at[a * (N_DEV - 1) + k], recv_sem=recv_sems.at[a * (N_DEV - 1) + k],
        device_id=peer, device_id_type=MESH)
        for a, (src, land) in enumerate(zip(src_refs, land_refs)) for k, peer in enumerate(_peers())]


def exchange_start(name, srcs, gather, after):
    n = len(srcs)
    lands = [pltpu.HBM((N_DEV,) + s.shape if gather else s.shape, s.dtype) for s in srcs]

    def body(*refs):
        send_sems, recv_sems = refs[2 * n + 1:2 * n + 3]
        for cp in _split_copies(refs[:n], refs[n:2 * n], send_sems, recv_sems, gather):
            cp.start()
        refs[-1][...] = jnp.zeros_like(refs[-1])

    sems = pltpu.SemaphoreType.DMA((n * (N_DEV - 1),))
    res = pl.pallas_call(
        body, name=name,
        out_shape=(sems, sems, *[pltpu.HBM(s.shape, s.dtype) for s in srcs], *lands, jax.ShapeDtypeStruct((8, 128), F32)),
        in_specs=(_HBM,) * (2 * n) + (pl.BlockSpec(memory_space=pl.ANY),),
        out_specs=(_SEM, _SEM) + (_HBM,) * (2 * n) + (pl.BlockSpec(memory_space=pltpu.VMEM),),
        input_output_aliases={i: 2 + i for i in range(2 * n)},
        compiler_params=pltpu.CompilerParams(has_side_effects=_EFFECT),
    )(*[pltpu.with_memory_space_constraint(s, pltpu.HBM) for s in srcs],
      *[pltpu.with_memory_space_constraint(lax.empty(ld.shape, ld.dtype), pltpu.HBM) for ld in lands], after)
    return res[0], res[1], list(res[2:2 + n]), list(res[2 + n:2 + 2 * n]), res[-1]


def exchange_wait(name, started, after, gather):
    send_sems, recv_sems, srcs, lands, _ = started
    n = len(srcs)
    after = list(after) if isinstance(after, (list, tuple)) else [after]

    def body(*refs):
        send_sems, recv_sems = refs[2 * n:2 * n + 2]
        for cp in _split_copies(refs[:n], refs[n:2 * n], send_sems, recv_sems, gather):
            cp.wait_send()
            cp.wait_recv()

    res = pl.pallas_call(
        body, name=name, out_shape=tuple(pltpu.HBM(a.shape, a.dtype) for a in srcs + lands),
        in_specs=(_HBM,) * (2 * n) + (_SEM, _SEM) + (pl.BlockSpec(memory_space=pl.ANY),) * len(after),
        out_specs=(_HBM,) * (2 * n), input_output_aliases={i: i for i in range(2 * n)},
        compiler_params=pltpu.CompilerParams(has_side_effects=_EFFECT),
    )(*srcs, *lands, send_sems, recv_sems, *after)
    return list(res[:n]), list(res[n:])


NCF = FFN_H // FFN_TC


def _size(shape):
    n = 1
    for s in shape:
        n *= s
    return n


def _padded_rows(n_elems, row_mult):
    return -(-n_elems // (D * row_mult)) * row_mult


def _pack_rows(arrs, dtype, row_mult):
    rows, offs, r0 = [], [], 0
    for a in arrs:
        flat = a.reshape(-1).astype(dtype)
        n = _padded_rows(flat.shape[0], row_mult)
        rows.append(jnp.pad(flat, (0, n * D - flat.shape[0])).reshape(n, D))
        offs.append(r0)
        r0 += n
    return jnp.concatenate(rows, 0), offs


def _unpack_rows(buf, offs, shapes):
    lead, out = buf.shape[:-2], []
    for o, shp in zip(offs, shapes):
        n = _size(shp)
        nr = -(-n // D)
        out.append(buf[..., o:o + nr, :].reshape(lead + (nr * D,))[..., :n].reshape(lead + tuple(shp)))
    return out


def _rows3(w):
    return [w[i:i + 1] for i in range(3)]


def f_mod1(xs, ps):
    return f_mod(xs, ps)[:1]


def kernel(x, c, ctx, c_ctx, ada_w, ada_b, norm_mix, norm_ffn, gla_w_in, gla_w_a2, gla_b_a, gla_head_norm, gla_w_out, sc_w_in, sc_conv_w, sc_w_out, ffn_w_up, ffn_conv_w, ffn_conv_b, ffn_w_down, final_norm, loss_target, m_c_ctx, m_ada_w, m_ada_b, m_norm_mix, m_norm_ffn, m_gla_w_in, m_gla_w_a2, m_gla_b_a, m_gla_head_norm, m_gla_w_out, m_sc_w_in, m_sc_conv_w, m_sc_w_out, m_ffn_w_up, m_ffn_conv_w, m_ffn_conv_b, m_ffn_w_down, m_final_norm, v_c_ctx, v_ada_w, v_ada_b, v_norm_mix, v_norm_ffn, v_gla_w_in, v_gla_w_a2, v_gla_b_a, v_gla_head_norm, v_gla_w_out, v_sc_w_in, v_sc_conv_w, v_sc_w_out, v_ffn_w_up, v_ffn_conv_w, v_ffn_conv_b, v_ffn_w_down, v_final_norm):
    names = ["c_ctx", "ada_w", "ada_b", "norm_mix", "norm_ffn", "gla_w_in", "gla_w_a2", "gla_b_a", "gla_head_norm",
             "gla_w_out", "sc_w_in", "sc_conv_w", "sc_w_out", "ffn_w_up", "ffn_conv_w", "ffn_conv_b", "ffn_w_down",
             "final_norm"]
    w_ = dict(zip(names, [c_ctx, ada_w, ada_b, norm_mix, norm_ffn, gla_w_in, gla_w_a2, gla_b_a, gla_head_norm, gla_w_out,
                          sc_w_in, sc_conv_w, sc_w_out, ffn_w_up, ffn_conv_w, ffn_conv_b, ffn_w_down, final_norm]))
    m_ = dict(zip(names, [m_c_ctx, m_ada_w, m_ada_b, m_norm_mix, m_norm_ffn, m_gla_w_in, m_gla_w_a2, m_gla_b_a,
                          m_gla_head_norm, m_gla_w_out, m_sc_w_in, m_sc_conv_w, m_sc_w_out, m_ffn_w_up, m_ffn_conv_w,
                          m_ffn_conv_b, m_ffn_w_down, m_final_norm]))
    v_ = dict(zip(names, [v_c_ctx, v_ada_w, v_ada_b, v_norm_mix, v_norm_ffn, v_gla_w_in, v_gla_w_a2, v_gla_b_a,
                          v_gla_head_norm, v_gla_w_out, v_sc_w_in, v_sc_conv_w, v_sc_w_out, v_ffn_w_up, v_ffn_conv_w,
                          v_ffn_conv_b, v_ffn_w_down, v_final_norm]))
    me = 4 * lax.axis_index("x") + 2 * lax.axis_index("y") + lax.axis_index("c")
    bsz = x.shape[0]
    tm = 256
    nt = SEQ // tm
    ctx_tiles = CTX // tm
    pe = functools.partial(P, per_example=True)

    groups = {"ffn1": [("ffn_w_up", 1), ("ffn_w_down", 1)], "sc": [("sc_w_in", 0), ("sc_w_out", 0)],
              "ffn0": [("ffn_w_up", 0), ("ffn_w_down", 0)], "gla": [("gla_w_in", 0), ("gla_w_out", 0)]}
    ag_groups = {"gin": [("gla_w_in", 0)], "ffn0": [("gla_w_out", 0), ("ffn_w_up", 0), ("ffn_w_down", 0)],
                 "sc": groups["sc"], "ffn1": groups["ffn1"]}
    ag_started = {}

    def start_gather(g, after):
        ag_started[g] = exchange_start(f"ag_{g}_start", [w_[n][i].astype(BF16) for n, i in ag_groups[g]], True, after)
        return ag_started[g][4]

    small_sharded = [c, gla_w_a2, gla_b_a, sc_conv_w, ffn_conv_w]
    pack0, offs0 = _pack_rows(small_sharded, F32, 8)
    g0 = all_gather("ag_small", pack0, True).reshape(N_DEV, pack0.shape[0], D)
    c_all, wa2_s, ba_s, scw_s, fcw_s = _unpack_rows(g0, offs0, [a.shape for a in small_sharded])
    w_a2 = wa2_s[:, 0].transpose(1, 2, 0, 3).reshape(2, RANK, KD)
    b_a = ba_s[:, 0].transpose(1, 0, 2).reshape(2, KD)
    sc_cw = scw_s[:, 0].transpose(1, 0, 2).reshape(3, D)
    ffn_cw = fcw_s.transpose(1, 2, 0, 3).reshape(2, 3, 2 * FFN_H)

    cond = jnp.concatenate([c_all.reshape(N_DEV * bsz, D), c_ctx[None], jnp.zeros((ADA_ROWS - N_DEV * bsz - 1, D), F32)], 0)
    b_mine = lax.dynamic_slice(ada_b, (0, me * ADA_COLS), (2, ADA_COLS)).reshape(2, 1, ADA_COLS)
    mod_part = ada_fwd(cond, ada_w, b_mine)
    mod = all_gather("ag_mod", mod_part.reshape(2 * ADA_ROWS, ADA_COLS), True)
    mod = mod.reshape(N_DEV, 2, ADA_ROWS, ADA_COLS).transpose(1, 2, 0, 3).reshape(2, ADA_ROWS, 6 * D)
    mods = lax.dynamic_slice(mod, (0, bsz * me, 0), (2, bsz, 6 * D))
    md = [[mods[i][:, k * D:(k + 1) * D].reshape(bsz, 1, D) for k in range(6)] for i in range(2)]
    mc = [mod[0, ADA_CTX_ROW, k * D:(k + 1) * D][None] for k in range(2)]

    tok = mod
    for g in ag_groups:
        tok = start_gather(g, tok)
    norm_mix = norm_mix + tok[0, 0]

    def gathered(g, after):
        mine, lands = exchange_wait(f"ag_{g}_wait", ag_started[g], after, True)
        return [lax.dynamic_update_index_in_dim(ld, mn, me, 0) for ld, mn in zip(lands, mine)]

    s_up, w_down = [None, None], [None, None]
    wd = jnp.zeros((128, 2 * KD), F32).at[:RANK, :KD].set(w_a2[0]).at[RANK:2 * RANK, KD:].set(w_a2[1])
    bd = b_a.reshape(1, 2 * KD)
    scw = _rows3(sc_cw)
    head_gain = gla_head_norm.reshape(1, HV)
    gains_mix = [norm_mix[i][None] for i in range(2)]
    gains_ffn = [norm_ffn[i][None] for i in range(2)]

    def tokens(a2d, t_len):
        return a2d.reshape(bsz, t_len, -1)

    def ffn_params(i):
        rows = [ffn_cw[i][t] for t in range(3)] + [ffn_conv_b[i]]
        return [P(a.reshape(2, FFN_H), w=FFN_TC, rows=True) for a in rows]

    def ffn_fwd(i, hn2):
        u = mm(f"ffn_up{i}", V(hn2, "tok"), V(s_up[i], "cols"), out="planes", out_dtype=BF16, planes_t=SEQ)
        act = rowwise(f"ffn_mid{i}", f_ffn_mid, [X(u, w=FFN_TC, planes=True)], ffn_params(i), tm=SEQ, nt=1, nc=NCF,
                      outs=[(FFN_TC, BF16, 1)])[0]
        return u, act

    def arrays(ps):
        return [p["a"] for p in ps]

    ps_in0 = [P(gains_mix[0]), pe(md[0][0]), pe(md[0][1])]
    ps_ctx = [P(gains_mix[0]), P(mc[0]), P(mc[1])]
    hn0 = rowwise("mod_in0", f_mod, [X(x)], ps_in0, tm=tm, nt=nt, outs=[(D, BF16, 1)])[0]
    hnc = rowwise("mod_ctx", f_mod, [X(ctx)], ps_ctx, tm=tm, nt=ctx_tiles, outs=[(D, BF16, 1)])[0]
    hcat = jnp.concatenate([hnc, hn0], axis=1)
    (s_gin,) = gathered("gin", hcat)
    w_gin = V(s_gin, "cols", width=GLA_IN_PAD)
    pcat = tokens(mm("gla_in", V(hcat, "tok"), w_gin, out_dtype=BF16), TT)
    pa_x = X(pcat, w=128, co=(GLA_IN_PAD - 128) // 128)
    la = rowwise("gla_decay", f_decay, [pa_x], [P(wd), P(bd)], tm=tm, nt=TT // tm, outs=[(2 * KD, F32, 1)])[0]
    o2, s_all = gla_fwd(pcat, la)
    post_xs = [X(o2, w=VD, co=0, ro=ctx_tiles, split=HEADS), X(o2, w=VD, co=1, ro=ctx_tiles, split=HEADS),
               X(pcat, w=VD, co=2, ro=ctx_tiles, split=HEADS)]
    yin0 = rowwise("gla_post", f_gla_post, post_xs, [P(head_gain)], tm=tm, nt=nt, outs=[(VD, BF16, HEADS)])[0]
    s_gout, s_up[0], s_down0 = gathered("ffn0", yin0)
    w_gout, w_down[0] = s_gout.reshape(VD, D), s_down0.reshape(FFN_H, D)
    ps_mid0 = [pe(md[0][2]), P(gains_ffn[0]), pe(md[0][3]), pe(md[0][4])]
    y0, h1_0, hn2_0 = mm_res_mod("gla_out", yin0, w_gout, x, *arrays(ps_mid0))
    u0, act0 = ffn_fwd(0, hn2_0)
    ps_in1 = [pe(md[0][5]), P(gains_mix[1]), pe(md[1][0]), pe(md[1][1])]
    fo0, h2_0, hn1 = mm_res_mod("ffn_down0", act0, w_down[0], h1_0, *arrays(ps_in1))

    s_sin, s_sout = gathered("sc", hn1)
    w_sout = s_sout.reshape(D, D)
    p1 = tokens(mm("sc_in", V(hn1, "tok"), V(s_sin, "cols")), SEQ)
    sc_ps = [P(a) for a in scw]
    yin1 = rowwise("sc_mid", f_sc_mid, [X(p1, split=3)], sc_ps, tm=tm, nt=nt, outs=[(D, BF16, 1)])[0]
    ps_mid1 = [pe(md[1][2]), P(gains_ffn[1]), pe(md[1][3]), pe(md[1][4])]
    y1, h1_1, hn2_1 = mm_res_mod("sc_out", yin1, w_sout, h2_0, *arrays(ps_mid1))
    s_up[1], s_down1 = gathered("ffn1", hn2_1)
    w_down[1] = s_down1.reshape(FFN_H, D)
    u1, act1 = ffn_fwd(1, hn2_1)
    fo1 = tokens(mm("ffn_down1", V(act1, "tok"), V(w_down[1])), SEQ)
    loss8, dh1_1, dfo1, dm5_1, g_final = final_loss(h1_1, fo1, md[1][5], final_norm[None], loss_target)

    def ffn_bwd(i, u, act, hn2, dfo):
        dact = tokens(mm(f"ffn_down_dx{i}", V(dfo, "tok"), V(w_down[i]), form="nt", out_dtype=BF16), SEQ)
        g_down = mm(f"ffn_down_dw{i}", V(act, "tok"), V(dfo, "tok"), form="tn", out_dtype=BF16)
        r = rowwise(f"ffn_mid_bwd{i}", f_ffn_mid, [X(u, w=FFN_TC, planes=True)], ffn_params(i), tm=SEQ, nt=1, nc=NCF,
                    douts=[X(dact, w=FFN_TC)], dx={0: BF16}, dp=[0, 1, 2, 3])
        du, g_cw, g_cb = r[0], jnp.stack([a.reshape(2 * FFN_H) for a in r[1:4]]), r[4].reshape(1, 2 * FFN_H)
        dhn2 = tokens(mm(f"ffn_up_dx{i}", V(du, "planes"), V(s_up[i], "cols"), form="nt", out_dtype=BF16), SEQ)
        g_up = mm(f"ffn_up_dw{i}", V(hn2, "tok"), V(du, "planes"), form="tn", out="cols", out_dtype=BF16)
        return dhn2, g_up, row_slots(g_down), g_cw, g_cb

    def res_mod_bwd(name, h, y, ps, dh1, dhn):
        return rowwise(name, f_res_mod, [X(h), X(y)], ps, tm=tm, nt=nt, douts=[X(dh1), X(dhn)],
                       dx={0: F32, 1: BF16}, dp=[0, 1, 2, 3])

    def row_slots(g):
        return g.reshape(N_DEV, -1, g.shape[-1])

    a2a_started = {}

    def send_grads(g, slots, after=None):
        a2a_started[g] = exchange_start(f"a2a_{g}_start", list(slots), False, loss8 if after is None else after)
        return a2a_started[g][4][0, 0]

    def after_start(ps, tok):
        return [dict(ps[0], a=ps[0]["a"] + tok)] + ps[1:]

    dhn2_1, g_up1, g_down1, g_fcw1, g_fcb1 = ffn_bwd(1, u1, act1, hn2_1, dfo1)
    tok = send_grads("ffn1", [g_up1, g_down1])
    dh2_0, dy1, dm2_1, g_nffn1, dm3_1, dm4_1 = res_mod_bwd("res_mod_mid1_bwd", h2_0, y1, after_start(ps_mid1, tok), dh1_1, dhn2_1)
    dyin1 = tokens(mm("sc_out_dx", V(dy1, "tok"), V(w_sout), form="nt", out_dtype=BF16), SEQ)
    g_sout = row_slots(mm("sc_out_dw", V(yin1, "tok"), V(dy1, "tok"), form="tn", out_dtype=BF16))
    r = rowwise("sc_mid_bwd", f_sc_mid, [X(p1, split=3)], sc_ps, tm=tm, nt=nt, douts=[X(dyin1)], dx={0: BF16}, dp=[0, 1, 2])
    dp1, g_scw = r[0], jnp.concatenate(r[1:4], 0)
    dhn1 = tokens(mm("sc_in_dx", V(dp1, "tok"), V(s_sin, "cols"), form="nt", out_dtype=BF16), SEQ)
    g_sin = mm("sc_in_dw", V(hn1, "tok"), V(dp1, "tok"), form="tn", out="cols", out_dtype=BF16)
    tok = send_grads("sc", [g_sin, g_sout])
    dh1_0, dfo0, dm5_0, g_nmix1, dm0_1, dm1_1 = res_mod_bwd("res_mod_in1_bwd", h1_0, fo0, after_start(ps_in1, tok), dh2_0, dhn1)

    dhn2_0, g_up0, g_down0, g_fcw0, g_fcb0 = ffn_bwd(0, u0, act0, hn2_0, dfo0)
    tok = send_grads("ffn0", [g_up0, g_down0])
    dx_res, dy0, dm2_0, g_nffn0, dm3_0, dm4_0 = res_mod_bwd("res_mod_mid0_bwd", x, y0, after_start(ps_mid0, tok), dh1_0, dhn2_0)
    dyin0 = tokens(mm("gla_out_dx", V(dy0, "tok"), V(w_gout), form="nt", out_dtype=BF16), SEQ)
    do, dgate, g_head = rowwise("gla_post_bwd", f_gla_post, post_xs, [P(head_gain)], tm=tm, nt=nt,
                                douts=[X(dyin0, split=HEADS)], dx={0: BF16, 2: BF16}, dp=[0])
    dq2, dk2, dv2, dla = gla_bwd(pcat, la, s_all, do)
    dpa, g_wd, g_bd = rowwise("gla_decay_bwd", f_decay, [pa_x], [P(wd), P(bd)], tm=tm, nt=TT // tm, douts=[X(dla)],
                              dx={0: BF16}, dp=[0, 1])
    dpcat = gla_combine(dq2, dk2, dv2, dgate, dpa)
    dhcat = tokens(mm("gla_in_dx", V(dpcat, "tok"), w_gin, form="nt", out_dtype=BF16), TT)
    grad_x, g_nmix0, dm0_0, dm1_0 = rowwise("mod_in0_bwd", f_mod, [X(x)], ps_in0, tm=tm, nt=nt,
                                            douts=[X(dhcat, ro=ctx_tiles), X(dx_res)], dx={0: F32}, dp=[0, 1, 2])
    g_nmix0c, dmc0, dmc1 = rowwise("mod_ctx_bwd", f_mod1, [X(ctx)], ps_ctx, tm=tm, nt=ctx_tiles, douts=[X(dhcat)],
                                   dx={}, dp=[0, 1, 2])

    zero_row = jnp.zeros((1, 4 * D), F32)
    dmod = [jnp.concatenate([jnp.concatenate([a.reshape(bsz, D) for a in dms], 1), ctx_row], 0)
            for dms, ctx_row in (([dm0_0, dm1_0, dm2_0, dm3_0, dm4_0, dm5_0], jnp.concatenate([dmc0, dmc1, zero_row], 1)),
                                 ([dm0_1, dm1_1, dm2_1, dm3_1, dm4_1, dm5_1], jnp.zeros((1, 6 * D), F32)))]
    g_wa2 = jnp.stack([g_wd[:RANK, :KD], g_wd[RANK:2 * RANK, KD:]])
    small_grads = [jnp.stack(dmod), jnp.concatenate([g_nmix0 + g_nmix0c, g_nmix1], 0), jnp.concatenate([g_nffn0, g_nffn1], 0),
                   g_head, jnp.concatenate([g_fcb0, g_fcb1], 0), g_final, g_wa2, g_bd.reshape(2, KD), g_scw,
                   jnp.stack([g_fcw0, g_fcw1]), loss8[:1]]
    pack1, offs1 = _pack_rows(small_grads, F32, 8)
    ag1 = exchange_start("ag_grads_start", [pack1], True, loss8)
    g_gin = mm("gla_in_dw", V(hcat, "tok"), V(dpcat, "tok"), form="tn", out="cols", out_dtype=BF16, shard_n=GLA_IN // N_DEV,
               after=ag1[4])
    g_gout = row_slots(mm("gla_out_dw", V(yin0, "tok"), V(dy0, "tok"), form="tn", out_dtype=BF16, after=ag1[4]))
    mine1, land1 = exchange_wait("ag_grads_wait", ag1, [g_gin, g_gout], True)
    g1 = lax.dynamic_update_index_in_dim(land1[0], mine1[0], me, 0)
    dmod_all = _unpack_rows(g1, offs1[:1], [small_grads[0].shape])[0]
    tot = _unpack_rows(sum_slots("sum_small", g1), offs1, [a.shape for a in small_grads])
    loss = tot[10][0, 0]
    dm_rows = dmod_all[:, :, :bsz].transpose(1, 0, 2, 3).reshape(2, N_DEV * bsz, 6 * D)
    dm_full = jnp.concatenate([dm_rows, tot[0][:, bsz:], jnp.zeros((2, ADA_ROWS - N_DEV * bsz - 1, 6 * D), F32)], 1)
    dm_mine = lax.dynamic_slice(dm_full, (0, 0, me * ADA_COLS), (2, ADA_ROWS, ADA_COLS))
    g_ada_w, g_ada_b, cpart = ada_bwd(cond, dm_mine, dm_full, ada_w)
    cparts = all_gather("ag_cctx", cpart, True).reshape(N_DEV, ADA_ROWS - ADA_CTX_ROW, D)[:, 0]
    g_cctx = cctx_grad(cparts, c_ctx[None])[0]
    tok = send_grads("gla", [g_gin, g_gout], after=g_cctx)

    def my_cols(full, n):
        return lax.dynamic_slice_in_dim(full, me * n, n, axis=full.ndim - 1)

    grads = {
        "c_ctx": g_cctx, "ada_b": g_ada_b.reshape(2, 6 * D), "norm_mix": tot[1], "norm_ffn": tot[2],
        "gla_head_norm": tot[3], "ffn_conv_b": tot[4], "final_norm": tot[5].reshape(D),
        "gla_w_a2": my_cols(tot[6], KD // N_DEV)[None], "gla_b_a": my_cols(tot[7], KD // N_DEV)[None],
        "sc_conv_w": my_cols(tot[8], D // N_DEV)[None], "ffn_conv_w": my_cols(tot[9], 2 * FFN_H // N_DEV),
    }

    res_ada = adamw("adamw_ada", *[a.reshape(2 * D, ADA_COLS) for a in (ada_w, g_ada_w, m_ada_w, v_ada_w)])
    grads["c_ctx"] = g_cctx + tok
    big = ["gla_w_in", "gla_w_out", "sc_w_in", "sc_w_out", "ffn_w_up", "ffn_w_down"]
    small = [n for n in names if n not in big and n != "ada_w"]
    g_small = _pack_rows([grads[n] for n in small], F32, 8)[0]
    res_small = adamw("adamw_small", _pack_rows([w_[n] for n in small], F32, 8)[0], g_small,
                      _pack_rows([m_[n] for n in small], F32, 8)[0], _pack_rows([v_[n] for n in small], F32, 8)[0])
    offs_s = _pack_rows([w_[n] for n in small], F32, 8)[1]

    big_res, done, me1 = {}, [res_small[0], res_ada[0]], jnp.reshape(me, (1,)).astype(jnp.int32)
    for g in groups:
        sent, lands = exchange_wait(f"a2a_{g}_wait", a2a_started[g], done, False)
        for (n, i), mine, land in zip(groups[g], sent, lands):
            big_res[(n, i)] = adamw_slots(f"adamw_{n}{i}", w_[n], land, mine, me1, m_[n], v_[n], i)
            done.append(big_res[(n, i)][0])

    out = {}
    for kind, idx in (("grad", 0), ("delta", 1), ("new_m", 2), ("new_v", 3)):
        vals = {n: jnp.stack([big_res[(n, i)][idx] for i in range(w_[n].shape[0])]) for n in big}
        vals["ada_w"] = res_ada[idx].reshape(ada_w.shape)
        vals.update(zip(small, _unpack_rows(res_small[idx], offs_s, [w_[n].shape for n in small])))
        out[kind] = [vals[n] for n in names]
    return (loss, grad_x, *out["grad"], *out["delta"], *out["new_m"], *out["new_v"])
```

```python
import functools

import jax
import jax.numpy as jnp
from jax import lax
from jax.experimental import pallas as pl
from jax.experimental.pallas import tpu as pltpu

F32 = jnp.float32
BF16 = jnp.bfloat16

N_DEV = 8
D = 1024
SEQ = 2048
CTX = 256
TT = CTX + SEQ
GRID_W = 64
CHUNK = 64
HEADS = 4
HK = 128
HV = 256
KD = 512
VD = 1024
RANK = 16
TAU = 16.0
GLA_IN = 3104
GLA_IN_PAD = 3200
FFN_H = 2560
FFN_TC = 256
EPS = 1e-6
LR, B1, B2, AEPS, WD, STEP = 0.001, 0.9, 0.999, 1e-08, 0.01, 10
MESH = pl.DeviceIdType.MESH


def _blocks(n):
    return [n] + [t for t in range(n - n % 128, 0, -128) if n % t == 0 and t != n]


def V(arr, kind="flat", width=None):
    if kind == "tok":
        return V(arr.reshape(-1, arr.shape[-1]))
    if kind == "flat":
        r, c = arr.shape
        return dict(a=arr, kind=kind, shape=(r, c), rows=_blocks(r), cols=_blocks(c))
    if kind == "planes":
        bsz, _, t, ch = arr.shape
        return dict(a=arr, kind=kind, shape=(bsz * t, 2 * ch), rows=_blocks(t), cols=[2 * ch] + _blocks(ch), t=t, ch=ch)
    _, r, n = arr.shape
    if width is not None:
        return dict(a=arr, kind=kind, shape=(r, width), rows=_blocks(r), cols=[width], n=n, pad=width - N_DEV * n)
    return dict(a=arr, kind=kind, shape=(r, N_DEV * n), rows=_blocks(r), cols=[8 * n, 4 * n, 2 * n], n=n, pad=0)


def _view_spec(v, br, bc, idx):
    if v["kind"] == "flat":
        return pl.BlockSpec((br, bc), idx)
    if v["kind"] == "planes":
        nt = v["t"] // br
        if bc == 2 * v["ch"]:
            return pl.BlockSpec((None, 2, br, v["ch"]), lambda i, j, k: (idx(i, j, k)[0] // nt, 0, idx(i, j, k)[0] % nt, 0))
        nch = v["ch"] // bc

        def at(i, j, k):
            r, c = idx(i, j, k)
            return r // nt, c // nch, r % nt, c % nch
        return pl.BlockSpec((None, None, br, bc), at)
    return pl.BlockSpec(((bc - v["pad"]) // v["n"], br, v["n"]), lambda i, j, k: (idx(i, j, k)[1], idx(i, j, k)[0], 0))


def _out_view(kind, rows, cols, dtype, planes_t=None, shard_n=None):
    if kind == "flat":
        shape = (rows, cols)
    elif kind == "planes":
        shape = (rows // planes_t, 2, planes_t, cols // 2)
    elif shard_n is not None:
        return V(jax.ShapeDtypeStruct((N_DEV, rows, shard_n), dtype), kind, width=cols)
    else:
        shape = (N_DEV, rows, cols // N_DEV)
    return V(jax.ShapeDtypeStruct(shape, dtype), kind)


MM_VMEM_BUDGET = 40 * 2 ** 20
MM_VMEM_LIMIT = 56 * 2 ** 20
MM_MAX_TILE = 1536


def _mm_tiles(m, n, kk, ms, ns, ks, a_bytes, b_bytes, o_bytes):
    best = None
    for tk in ks:
        for tm in [t for t in ms if t <= MM_MAX_TILE] or ms:
            for tn in [t for t in ns if t <= MM_MAX_TILE] or ns:
                one_k = tk == kk
                need = 2 * (tm * tk * a_bytes + tk * tn * b_bytes + tm * tn * o_bytes) + (0 if one_k else tm * tn * 4)
                if need > MM_VMEM_BUDGET:
                    continue
                steps = (m // tm) * (n // tn) * (kk // tk)
                traffic = (m * kk * a_bytes * (1 if one_k else n // tn)
                           + kk * n * b_bytes * (1 if one_k and n == tn else m // tm) + m * n * o_bytes)
                fill = (tm * tk * a_bytes + tk * tn * b_bytes) / 2.5e12
                cost = max(2.0 * m * n * kk / (9e14 if one_k else 6.5e14), traffic / 2.5e12) + steps * 0.4e-6 + fill
                if best is None or cost < best[0]:
                    best = (cost, tm, tn, tk)
    return best[1:]


def mm(name, a, b, form="nn", out="flat", out_dtype=F32, planes_t=None, shard_n=None, after=None):
    (m, kk) = a["shape"][::-1] if form == "tn" else a["shape"]
    n = b["shape"][0] if form == "nt" else b["shape"][1]
    assert (b["shape"][1] if form == "nt" else b["shape"][0]) == kk, (name, a["shape"], b["shape"])
    o = _out_view(out, m, n, out_dtype, planes_t, shard_n)
    a_m, a_k = (a["cols"], a["rows"]) if form == "tn" else (a["rows"], a["cols"])
    b_k, b_n = (b["cols"], b["rows"]) if form == "nt" else (b["rows"], b["cols"])
    tm, tn, tk = _mm_tiles(m, n, kk, [t for t in a_m if t in o["rows"]], [t for t in b_n if t in o["cols"]],
                           [t for t in a_k if t in b_k], a["a"].dtype.itemsize, b["a"].dtype.itemsize,
                           jnp.dtype(out_dtype).itemsize)
    nk = kk // tk
    dn = (((0 if form == "tn" else 1,), (1 if form == "nt" else 0,)), ((), ()))

    def load(ref, v):
        if len(ref.shape) == 3:
            pieces = [ref[p].astype(BF16) for p in range(ref.shape[0])]
            if v.get("pad"):
                pieces.append(jnp.zeros(ref.shape[1:2] + (v["pad"],), BF16))
            return jnp.concatenate(pieces, axis=-1)
        return ref[...].astype(BF16)

    def store(o_ref, val):
        val = val.astype(out_dtype)
        if len(o_ref.shape) == 3:
            w = o_ref.shape[-1]
            for p in range(o_ref.shape[0]):
                o_ref[p] = val[:, p * w:(p + 1) * w]
        else:
            o_ref[...] = val

    def body(a_ref, b_ref, *rest):
        o_ref, acc = rest[0 if after is None else 1], rest[1 if after is None else 2:]
        if nk == 1:
            store(o_ref, lax.dot_general(load(a_ref, a), load(b_ref, b), dn, preferred_element_type=F32))
            return
        k, acc_ref = pl.program_id(2), acc[0]

        @pl.when(k == 0)
        def _():
            acc_ref[...] = jnp.zeros_like(acc_ref)

        acc_ref[...] += lax.dot_general(load(a_ref, a), load(b_ref, b), dn, preferred_element_type=F32)

        @pl.when(k == nk - 1)
        def _():
            store(o_ref, acc_ref[...])

    if form == "tn":
        a_spec = _view_spec(a, tk, tm, lambda i, j, k: (k, i))
    else:
        a_spec = _view_spec(a, tm, tk, lambda i, j, k: (i, k))
    if form == "nt":
        b_spec = _view_spec(b, tn, tk, lambda i, j, k: (j, k))
    else:
        b_spec = _view_spec(b, tk, tn, lambda i, j, k: (k, j))
    return pl.pallas_call(
        body, name=name, grid=(m // tm, n // tn, nk),
        in_specs=[a_spec, b_spec] + ([] if after is None else [pl.BlockSpec(memory_space=pl.ANY)]),
        out_specs=_view_spec(o, tm, tn, lambda i, j, k: (i, j)), out_shape=o["a"],
        scratch_shapes=[pltpu.VMEM((tm, tn), F32)] if nk > 1 else [],
        compiler_params=pltpu.CompilerParams(dimension_semantics=("parallel", "parallel", "arbitrary"),
                                             vmem_limit_bytes=MM_VMEM_LIMIT),
    )(a["a"], b["a"], *([] if after is None else [after]))


def mm_res_mod(name, a, w, h, gate, gain, shift, scale):
    bsz, t_len, kk = a.shape
    tm = 512
    per = t_len // tm

    def body(a_ref, w_ref, h_ref, gate_ref, gain_ref, shift_ref, scale_ref, y_ref, h1_ref, hn_ref):
        y = jnp.dot(a_ref[...].astype(BF16), w_ref[...].astype(BF16), preferred_element_type=F32)
        h1 = h_ref[...] + gate_ref[...] * y
        y_ref[...] = y.astype(BF16)
        h1_ref[...] = h1
        hn_ref[...] = _mod(h1, gain_ref[...], shift_ref[...], scale_ref[...]).astype(BF16)

    def tile(width):
        return pl.BlockSpec((None, tm, width), lambda i: (i // per, i % per, 0))

    per_ex = pl.BlockSpec((None, 1, D), lambda i: (i // per, 0, 0))
    return pl.pallas_call(
        body, name=name, grid=(bsz * per,),
        in_specs=[tile(kk), pl.BlockSpec((kk, D), lambda i: (0, 0)), tile(D), per_ex, pl.BlockSpec((1, D), lambda i: (0, 0)),
                  per_ex, per_ex],
        out_specs=[tile(D)] * 3,
        out_shape=[jax.ShapeDtypeStruct((bsz, t_len, D), BF16), jax.ShapeDtypeStruct((bsz, t_len, D), F32),
                   jax.ShapeDtypeStruct((bsz, t_len, D), BF16)],
        compiler_params=pltpu.CompilerParams(dimension_semantics=("parallel",), vmem_limit_bytes=MM_VMEM_LIMIT),
    )(a, w, h, gate, gain, shift, scale)


def X(arr, w=None, co=0, ro=0, split=1, planes=False):
    return dict(a=arr, w=arr.shape[-1] if w is None else w, co=co, ro=ro, split=2 if planes else split,
                mode="planes" if planes else "cols")


def P(arr, per_example=False, w=None, split=1, rows=False):
    return dict(a=arr, e=per_example, w=arr.shape[-1] if w is None else w, split=arr.shape[-2] if rows else split,
                mode="rows" if rows else "cols")


def _pieces(ref, s):
    if s["mode"] == "planes":
        return [ref[0], ref[1]]
    if s["mode"] == "rows":
        return [ref[i:i + 1, :] for i in range(s["split"])]
    w = ref.shape[-1] // s["split"]
    return [ref[:, i * w:(i + 1) * w] for i in range(s["split"])]


def _store(ref, pieces, s, accumulate=False):
    w = ref.shape[-1] // len(pieces)
    for i, p in enumerate(pieces):
        at = (i,) if s["mode"] == "planes" else (slice(i, i + 1),) if s["mode"] == "rows" else (slice(None), slice(i * w, (i + 1) * w))
        if accumulate:
            ref[at] += p.astype(ref.dtype)
        else:
            ref[at] = p.astype(ref.dtype)


def rowwise(name, f, xs, ps, *, tm, nt, nc=1, outs=None, douts=None, dx=None, dp=None):
    bsz = xs[0]["a"].shape[0]
    fwd = douts is None
    nx, np_ = len(xs), len(ps)
    douts = [] if fwd else douts
    dx = {} if fwd else dx
    dp = [] if fwd else dp

    def x_spec(s):
        if s["mode"] == "planes":
            return pl.BlockSpec((None, 2, tm, s["w"]), lambda c, b, t, s=s: (b, 0, t + s["ro"], c + s["co"]))
        return pl.BlockSpec((None, tm, s["w"]), lambda c, b, t, s=s: (b, t + s["ro"], c + s["co"]))

    def x_out(s, dt):
        if s["mode"] == "planes":
            return (jax.ShapeDtypeStruct((bsz, 2, nt * tm, nc * s["w"]), dt),
                    pl.BlockSpec((None, 2, tm, s["w"]), lambda c, b, t: (b, 0, t, c)))
        return (jax.ShapeDtypeStruct((bsz, nt * tm, nc * s["w"]), dt), pl.BlockSpec((None, tm, s["w"]), lambda c, b, t: (b, t, c)))

    def p_spec(s):
        r = s["a"].shape[-2]
        if s["e"]:
            return pl.BlockSpec((None, r, s["w"]), lambda c, b, t: (b, 0, c))
        return pl.BlockSpec((r, s["w"]), lambda c, b, t: (0, c))

    in_specs = [x_spec(s) for s in xs] + [p_spec(s) for s in ps] + [x_spec(s) for s in douts]
    operands = [s["a"] for s in xs] + [s["a"] for s in ps] + [s["a"] for s in douts]
    if fwd:
        out_modes = [dict(mode="cols", split=sp) for (_, _, sp) in outs]
        out_shape = [jax.ShapeDtypeStruct((bsz, nt * tm, nc * w), dt) for (w, dt, _) in outs]
        out_specs = [pl.BlockSpec((None, tm, w), lambda c, b, t: (b, t, c)) for (w, _, _) in outs]
    else:
        dx_outs = [x_out(xs[i], dt) for i, dt in dx.items()]
        out_shape, out_specs = [o[0] for o in dx_outs], [o[1] for o in dx_outs]
        for j in dp:
            s = ps[j]
            r = s["a"].shape[-2]
            if s["e"]:
                out_shape.append(jax.ShapeDtypeStruct((bsz, r, nc * s["w"]), F32))
                out_specs.append(pl.BlockSpec((None, r, s["w"]), lambda c, b, t: (b, 0, c)))
            else:
                out_shape.append(jax.ShapeDtypeStruct((r, nc * s["w"]), F32))
                out_specs.append(pl.BlockSpec((r, s["w"]), lambda c, b, t: (0, c)))

    def body(*refs):
        x_refs, p_refs = refs[:nx], refs[nx:nx + np_]
        d_refs = refs[nx + np_:nx + np_ + len(douts)]
        o_refs = refs[nx + np_ + len(douts):]
        xv = [[p.astype(F32) for p in _pieces(r, s)] for r, s in zip(x_refs, xs)]
        pv = [[p.astype(F32) for p in _pieces(r, s)] for r, s in zip(p_refs, ps)]
        if fwd:
            for r, pieces, s in zip(o_refs, f(xv, pv), out_modes):
                _store(r, pieces, s)
            return
        _, vjp = jax.vjp(f, xv, pv)
        cot = [[p.astype(F32) for p in _pieces(r, s)] for r, s in zip(d_refs, douts)]
        dxv, dpv = vjp(cot)
        for r, i in zip(o_refs, dx):
            _store(r, dxv[i], xs[i])
        b, t = pl.program_id(1), pl.program_id(2)
        for r, j in zip(o_refs[len(dx):], dp):
            first = (t == 0) if ps[j]["e"] else jnp.logical_and(b == 0, t == 0)

            @pl.when(first)
            def _(r=r, j=j):
                _store(r, dpv[j], ps[j])

            @pl.when(jnp.logical_not(first))
            def _(r=r, j=j):
                _store(r, dpv[j], ps[j], accumulate=True)

    res = pl.pallas_call(
        body, name=name, grid=(nc, bsz, nt), in_specs=in_specs, out_specs=out_specs, out_shape=out_shape,
        compiler_params=pltpu.CompilerParams(dimension_semantics=("arbitrary", "arbitrary", "arbitrary")),
    )(*operands)
    return res


def _keep_rows(a, shift, keep):
    n = a.shape[0]
    t = lax.broadcasted_iota(jnp.int32, a.shape, 0)
    return jnp.where(keep(t, n), pltpu.roll(a, shift % n, 0), 0.0)


def _shift_pair(step, keep_prev, keep_next):
    @jax.custom_vjp
    def prev(a):
        return _keep_rows(a, step, keep_prev)

    @jax.custom_vjp
    def nxt(a):
        return _keep_rows(a, -step, keep_next)

    prev.defvjp(lambda a: (prev(a), None), lambda _, g: (nxt(g),))
    nxt.defvjp(lambda a: (nxt(a), None), lambda _, g: (prev(g),))
    return prev, nxt


prev_tok, next_tok = _shift_pair(1, lambda t, n: t % GRID_W != 0, lambda t, n: t % GRID_W != GRID_W - 1)
prev_row, next_row = _shift_pair(GRID_W, lambda t, n: t >= GRID_W, lambda t, n: t < n - GRID_W)


@jax.custom_vjp
def bdot(a, w):
    return jnp.dot(a.astype(BF16), w.astype(BF16), preferred_element_type=F32)


def _bdot_bwd(res, g):
    a, w = res
    gb = g.astype(BF16)
    da = lax.dot_general(gb, w.astype(BF16), (((1,), (1,)), ((), ())), preferred_element_type=F32)
    dw = lax.dot_general(a.astype(BF16), gb, (((0,), (0,)), ((), ())), preferred_element_type=F32)
    return da, dw


bdot.defvjp(lambda a, w: (bdot(a, w), (a, w)), _bdot_bwd)


@jax.custom_vjp
def log_sigmoid(z):
    return jnp.minimum(z, 0.0) - jnp.log(1.0 + jnp.exp(-jnp.abs(z)))


def _lsig_bwd(z, g):
    e = jnp.exp(-jnp.abs(z))
    return (g * jnp.where(z >= 0, e, 1.0) / (1.0 + e),)


log_sigmoid.defvjp(lambda z: (log_sigmoid(z), z), _lsig_bwd)


def silu(x):
    return x * jax.nn.sigmoid(x)


def _rms(x):
    return x * lax.rsqrt(jnp.mean(x * x, axis=-1, keepdims=True) + EPS)


def _mod(x, gain, shift, scale):
    return _rms(x) * gain * (1.0 + scale) + shift


def f_mod(xs, ps):
    ((h,),), ((gain,), (shift,), (scale,)) = xs, ps
    return [[_mod(h, gain, shift, scale)], [h]]


def f_res_mod(xs, ps):
    ((h,), (y,)), ((gate,), (gain,), (shift,), (scale,)) = xs, ps
    h1 = h + gate * y
    return [[h1], [_mod(h1, gain, shift, scale)]]


def f_ffn_mid(xs, ps):
    ((ua, ug),), ((w0a, w0g), (w1a, w1g), (w2a, w2g), (ba, bg)) = xs, ps
    a = w0a * prev_row(ua) + w1a * ua + w2a * next_row(ua) + ba
    g = w0g * prev_row(ug) + w1g * ug + w2g * next_row(ug) + bg
    return [[a * silu(g)]]


def f_sc_mid(xs, ps):
    ((bg, cg, v),), ((w0,), (w1,), (w2,)) = xs, ps
    z = cg * v
    return [[bg * (w0 * prev_tok(z) + w1 * z + w2 * next_tok(z))]]


def f_decay(xs, ps):
    ((a,),), ((wd,), (bd,)) = xs, ps
    return [[log_sigmoid(bdot(a, wd) + bd) / TAU]]


def f_gla_post(xs, ps):
    (of, ob, g), ((gain,),) = xs, ps
    return [[_rms(a + b) * gain * silu(c) for a, b, c in zip(of, ob, g)]]


NCH = TT // CHUNK
CTX_CH = CTX // CHUNK
_NT = (((1,), (1,)), ((), ()))
_TN = (((0,), (0,)), ((), ()))
_NN = (((1,), (0,)), ((), ()))


def _chunk_of(d, j):
    return jnp.where(d == 0, j, jnp.where(j < CTX_CH, CTX_CH - 1 - j, NCH + CTX_CH - 1 - j))


def _dot(a, b, dn):
    return lax.dot_general(a, b, dn, preferred_element_type=F32)


def _cumsum_rows(g, suffix):
    n = g.shape[0]
    row = lax.broadcasted_iota(jnp.int32, g.shape, 0)
    s = 1
    while s < n:
        if suffix:
            g = g + jnp.where(row < n - s, pltpu.roll(g, n - s, 0), 0.0)
        else:
            g = g + jnp.where(row >= s, pltpu.roll(g, s, 0), 0.0)
        s *= 2
    return g


def _causal(backward):
    row = lax.broadcasted_iota(jnp.int32, (CHUNK, CHUNK), 0)
    col = lax.broadcasted_iota(jnp.int32, (CHUNK, CHUNK), 1)
    return col >= row if backward else col <= row


def _gla_in_specs(bsz, rev):
    def blk(d, j):
        return _chunk_of(d, (NCH - 1 - j) if rev else j)

    return [
        pl.BlockSpec((bsz, CHUNK, KD), lambda d, j: (0, blk(d, j), 0)),
        pl.BlockSpec((bsz, CHUNK, KD), lambda d, j: (0, blk(d, j), 1)),
        pl.BlockSpec((bsz, CHUNK, VD), lambda d, j: (0, blk(d, j), 1)),
        pl.BlockSpec((bsz, CHUNK, KD), lambda d, j: (0, blk(d, j), d)),
    ], blk


def gla_fwd(pcat, la):
    bsz = pcat.shape[0]
    in_specs, blk = _gla_in_specs(bsz, False)

    def body(q_ref, k_ref, v_ref, la_ref, o_ref, s_ref, st):
        d, j = pl.program_id(0), pl.program_id(1)

        @pl.when(j == 0)
        def _():
            st[...] = jnp.zeros_like(st)

        s_ref[...] = st[...]

        def scan(backward):
            causal = _causal(backward)
            for e in range(bsz):
                g_all = la_ref[e]
                b_all = _cumsum_rows(g_all, backward)
                bl_all = jnp.sum(g_all, axis=0, keepdims=True)
                qs_all = (q_ref[e].astype(F32) * (HK ** -0.5) * jnp.exp(b_all)).astype(BF16)
                ks_all = (k_ref[e] * jnp.exp(-b_all)).astype(BF16)
                kd_all = (k_ref[e] * jnp.exp(bl_all - b_all)).astype(BF16)
                el_all = jnp.exp(bl_all)
                for h in range(HEADS):
                    ks_, vs_ = slice(h * HK, (h + 1) * HK), slice(h * HV, (h + 1) * HV)
                    qs, ks, kd, v = qs_all[:, ks_], ks_all[:, ks_], kd_all[:, ks_], v_ref[e, :, vs_].astype(BF16)
                    s = st[e, h]
                    att = jnp.where(causal, _dot(qs, ks, _NT), 0.0).astype(BF16)
                    o_ref[e, :, vs_] = _dot(qs, s.astype(BF16), _NT) + _dot(att, v, _NN)
                    st[e, h] = el_all[:, ks_] * s + _dot(v, kd, _TN)

        @pl.when(d == 0)
        def _():
            scan(False)

        @pl.when(d == 1)
        def _():
            scan(True)

    return pl.pallas_call(
        body, name="gla_fwd", grid=(2, NCH), in_specs=in_specs,
        out_specs=[pl.BlockSpec((bsz, CHUNK, VD), lambda d, j: (0, blk(d, j), d)),
                   pl.BlockSpec((bsz, None, None, HEADS, HV, HK), lambda d, j: (0, d, j, 0, 0, 0))],
        out_shape=[jax.ShapeDtypeStruct((bsz, TT, 2 * VD), F32), jax.ShapeDtypeStruct((bsz, 2, NCH, HEADS, HV, HK), F32)],
        scratch_shapes=[pltpu.VMEM((bsz, HEADS, HV, HK), F32)],
        compiler_params=pltpu.CompilerParams(dimension_semantics=("arbitrary", "arbitrary")),
    )(pcat, pcat, pcat, la)


def gla_bwd(pcat, la, s_all, do):
    bsz = pcat.shape[0]
    in_specs, blk = _gla_in_specs(bsz, True)
    in_specs += [
        pl.BlockSpec((bsz, None, None, HEADS, HV, HK), lambda d, j: (0, d, NCH - 1 - j, 0, 0, 0)),
        pl.BlockSpec((bsz, CHUNK, VD), lambda d, j: (0, jnp.maximum(blk(d, j) - CTX_CH, 0), 0)),
    ]

    def body(q_ref, k_ref, v_ref, la_ref, s_ref, do_ref, dq_ref, dk_ref, dv_ref, dla_ref, dst):
        d, j = pl.program_id(0), pl.program_id(1)

        @pl.when(j == 0)
        def _():
            dst[...] = jnp.zeros_like(dst)

        latent = blk(d, j) >= CTX_CH
        scale = HK ** -0.5

        def scan(backward):
            causal = _causal(backward)
            for e in range(bsz):
                g_all = la_ref[e]
                b_all = _cumsum_rows(g_all, backward)
                bl_all = jnp.sum(g_all, axis=0, keepdims=True)
                ex_all, ei_all, ed_all, el_all = jnp.exp(b_all), jnp.exp(-b_all), jnp.exp(bl_all - b_all), jnp.exp(bl_all)
                qs_all, ks_all, kd_all = q_ref[e].astype(F32) * scale * ex_all, k_ref[e] * ei_all, k_ref[e] * ed_all
                qsb_all, ksb_all, kdb_all = qs_all.astype(BF16), ks_all.astype(BF16), kd_all.astype(BF16)
                db_parts, dbl_parts = [], []
                for h in range(HEADS):
                    ks_, vs_ = slice(h * HK, (h + 1) * HK), slice(h * HV, (h + 1) * HV)
                    qs, ks, kd, el = qs_all[:, ks_], ks_all[:, ks_], kd_all[:, ks_], el_all[:, ks_]
                    qsb, ksb, kdb, v = qsb_all[:, ks_], ksb_all[:, ks_], kdb_all[:, ks_], v_ref[e, :, vs_].astype(BF16)
                    s, ds1 = s_ref[e, h], dst[e, h]
                    sb, ds1b = s.astype(BF16), ds1.astype(BF16)
                    dob = jnp.where(latent, do_ref[e, :, vs_], 0.0).astype(BF16)
                    att = jnp.where(causal, _dot(qsb, ksb, _NT), 0.0).astype(BF16)
                    datt = jnp.where(causal, _dot(dob, v, _NT), 0.0).astype(BF16)
                    dqs = _dot(dob, sb, _NN) + _dot(datt, ksb, _NN)
                    dks = _dot(datt, qsb, _TN)
                    dv_ref[e, :, vs_] = (_dot(att, dob, _TN) + _dot(kdb, ds1b, _NT)).astype(BF16)
                    dkd = _dot(v, ds1b, _NN)
                    dst[e, h] = _dot(dob, qsb, _TN) + el * ds1
                    del_ = jnp.sum(s * ds1, axis=0, keepdims=True)
                    dq_ref[e, :, ks_] = (dqs * ex_all[:, ks_] * scale).astype(BF16)
                    dk_ref[e, :, ks_] = (dks * ei_all[:, ks_] + dkd * ed_all[:, ks_]).astype(BF16)
                    db_parts.append(dqs * qs - dks * ks - dkd * kd)
                    dbl_parts.append(jnp.sum(dkd * kd, axis=0, keepdims=True) + del_ * el)
                dla_ref[e] = _cumsum_rows(jnp.concatenate(db_parts, -1), not backward) + jnp.concatenate(dbl_parts, -1)

        @pl.when(d == 0)
        def _():
            scan(False)

        @pl.when(d == 1)
        def _():
            scan(True)

    return pl.pallas_call(
        body, name="gla_bwd", grid=(2, NCH), in_specs=in_specs,
        out_specs=[pl.BlockSpec((None, bsz, CHUNK, KD), lambda d, j: (d, 0, blk(d, j), 0)),
                   pl.BlockSpec((None, bsz, CHUNK, KD), lambda d, j: (d, 0, blk(d, j), 0)),
                   pl.BlockSpec((None, bsz, CHUNK, VD), lambda d, j: (d, 0, blk(d, j), 0)),
                   pl.BlockSpec((bsz, CHUNK, KD), lambda d, j: (0, blk(d, j), d))],
        out_shape=[jax.ShapeDtypeStruct((2, bsz, TT, KD), BF16), jax.ShapeDtypeStruct((2, bsz, TT, KD), BF16),
                   jax.ShapeDtypeStruct((2, bsz, TT, VD), BF16), jax.ShapeDtypeStruct((bsz, TT, 2 * KD), F32)],
        scratch_shapes=[pltpu.VMEM((bsz, HEADS, HV, HK), F32)],
        compiler_params=pltpu.CompilerParams(dimension_semantics=("arbitrary", "arbitrary")),
    )(pcat, pcat, pcat, la, s_all, do)


def gla_combine(dq2, dk2, dv2, dgate, dpa):
    bsz = dgate.shape[0]
    tm = CTX

    def body(dq_ref, dk_ref, dv_ref, dg_ref, dpa_ref, o_ref):
        t = pl.program_id(1)
        o_ref[:, 0:KD] = (dq_ref[0].astype(F32) + dq_ref[1].astype(F32)).astype(BF16)
        o_ref[:, KD:2 * KD] = (dk_ref[0].astype(F32) + dk_ref[1].astype(F32)).astype(BF16)
        o_ref[:, 2 * KD:2 * KD + VD] = (dv_ref[0].astype(F32) + dv_ref[1].astype(F32)).astype(BF16)
        o_ref[:, 2 * KD + VD:2 * KD + 2 * VD] = jnp.where(t > 0, dg_ref[...], 0).astype(BF16)
        o_ref[:, 2 * KD + 2 * VD:] = dpa_ref[...].astype(BF16)

    return pl.pallas_call(
        body, name="gla_combine", grid=(bsz, TT // tm),
        in_specs=[pl.BlockSpec((2, None, tm, KD), lambda b, t: (0, b, t, 0)),
                  pl.BlockSpec((2, None, tm, KD), lambda b, t: (0, b, t, 0)),
                  pl.BlockSpec((2, None, tm, VD), lambda b, t: (0, b, t, 0)),
                  pl.BlockSpec((None, tm, VD), lambda b, t: (b, jnp.maximum(t - 1, 0), 0)),
                  pl.BlockSpec((None, tm, 128), lambda b, t: (b, t, 0))],
        out_specs=pl.BlockSpec((None, tm, GLA_IN_PAD), lambda b, t: (b, t, 0)),
        out_shape=jax.ShapeDtypeStruct((bsz, TT, GLA_IN_PAD), BF16),
        compiler_params=pltpu.CompilerParams(dimension_semantics=("arbitrary", "arbitrary")),
    )(dq2, dk2, dv2, dgate, dpa)


def final_loss(h1, fo, gate, gain, tgt):
    bsz, t_len, _ = h1.shape
    tm = 256

    def body(h_ref, f_ref, gate_ref, gain_ref, tgt_ref, loss_ref, dh_ref, df_ref, dgate_ref, dgain_ref):
        b, t = pl.program_id(0), pl.program_id(1)
        target = tgt_ref[...]

        def core(h, fo_, gate_, gain_):
            e = _rms(h + gate_ * fo_) * gain_ - target
            return jnp.sum(0.5 * jnp.sum(e * e, axis=-1, keepdims=True) / D, axis=0, keepdims=True)

        loss, vjp = jax.vjp(core, h_ref[...], f_ref[...], gate_ref[...], gain_ref[...])
        dh, df, dgate, dgain = vjp(jnp.ones((1, 1), F32))
        dh_ref[...] = dh
        df_ref[...] = df.astype(BF16)
        first = jnp.logical_and(b == 0, t == 0)

        @pl.when(first)
        def _():
            loss_ref[...] = jnp.broadcast_to(loss, loss_ref.shape)
            dgain_ref[...] = dgain

        @pl.when(jnp.logical_not(first))
        def _():
            loss_ref[...] += jnp.broadcast_to(loss, loss_ref.shape)
            dgain_ref[...] += dgain

        @pl.when(t == 0)
        def _():
            dgate_ref[...] = dgate

        @pl.when(t > 0)
        def _():
            dgate_ref[...] += dgate

    tile = pl.BlockSpec((None, tm, D), lambda b, t: (b, t, 0))
    per_ex = pl.BlockSpec((None, 1, D), lambda b, t: (b, 0, 0))
    shared = pl.BlockSpec((1, D), lambda b, t: (0, 0))
    return pl.pallas_call(
        body, name="final_loss", grid=(bsz, t_len // tm),
        in_specs=[tile, tile, per_ex, shared, tile],
        out_specs=[pl.BlockSpec((8, 128), lambda b, t: (0, 0)), tile, tile, per_ex, shared],
        out_shape=[jax.ShapeDtypeStruct((8, 128), F32), jax.ShapeDtypeStruct(h1.shape, F32),
                   jax.ShapeDtypeStruct(h1.shape, BF16), jax.ShapeDtypeStruct((bsz, 1, D), F32),
                   jax.ShapeDtypeStruct((1, D), F32)],
        compiler_params=pltpu.CompilerParams(dimension_semantics=("arbitrary", "arbitrary")),
    )(h1, fo, gate, gain, tgt)


ADA_ROWS = 24
ADA_CTX_ROW = 16
ADA_COLS = 6 * D // N_DEV


def ada_fwd(cond, w, b):
    def body(c_ref, w_ref, b_ref, o_ref):
        s = silu(c_ref[...]).astype(BF16)
        o_ref[...] = jnp.dot(s, w_ref[...].astype(BF16), preferred_element_type=F32) + b_ref[...]

    return pl.pallas_call(
        body, name="ada_fwd", grid=(2,),
        in_specs=[pl.BlockSpec((ADA_ROWS, D), lambda i: (0, 0)), pl.BlockSpec((None, D, ADA_COLS), lambda i: (i, 0, 0)),
                  pl.BlockSpec((None, 1, ADA_COLS), lambda i: (i, 0, 0))],
        out_specs=pl.BlockSpec((None, ADA_ROWS, ADA_COLS), lambda i: (i, 0, 0)),
        out_shape=jax.ShapeDtypeStruct((2, ADA_ROWS, ADA_COLS), F32),
    )(cond, w, b)


def ada_bwd(cond, dm_mine, dm_full, w):
    def body(c_ref, dm_ref, dmf_ref, w_ref, gw_ref, gb_ref, cp_ref):
        i = pl.program_id(0)
        s = silu(c_ref[...]).astype(BF16)
        dm = dm_ref[...].astype(BF16)
        gw_ref[...] = _dot(s, dm, _TN)
        gb_ref[...] = jnp.sum(dmf_ref[...], axis=0, keepdims=True)

        @pl.when(i == 0)
        def _():
            cp_ref[...] = _dot(dm_ref[ADA_CTX_ROW:, :].astype(BF16), w_ref[...].astype(BF16), _NT)

    return pl.pallas_call(
        body, name="ada_bwd", grid=(2,),
        in_specs=[pl.BlockSpec((ADA_ROWS, D), lambda i: (0, 0)), pl.BlockSpec((None, ADA_ROWS, ADA_COLS), lambda i: (i, 0, 0)),
                  pl.BlockSpec((None, ADA_ROWS, 6 * D), lambda i: (i, 0, 0)), pl.BlockSpec((None, D, ADA_COLS), lambda i: (i, 0, 0))],
        out_specs=[pl.BlockSpec((None, D, ADA_COLS), lambda i: (i, 0, 0)), pl.BlockSpec((None, 1, 6 * D), lambda i: (i, 0, 0)),
                   pl.BlockSpec((ADA_ROWS - ADA_CTX_ROW, D), lambda i: (0, 0))],
        out_shape=[jax.ShapeDtypeStruct((2, D, ADA_COLS), F32), jax.ShapeDtypeStruct((2, 1, 6 * D), F32),
                   jax.ShapeDtypeStruct((ADA_ROWS - ADA_CTX_ROW, D), F32)],
        compiler_params=pltpu.CompilerParams(dimension_semantics=("arbitrary",)),
    )(cond, dm_mine, dm_full, w)


def cctx_grad(parts, c_ctx):
    def body(p_ref, c_ref, o_ref):
        tot = p_ref[0:1, :]
        for i in range(1, N_DEV):
            tot = tot + p_ref[i:i + 1, :]
        c = c_ref[...]
        sg = jax.nn.sigmoid(c)
        o_ref[...] = tot * sg * (1.0 + c * (1.0 - sg))

    return pl.pallas_call(body, name="cctx_grad", out_shape=jax.ShapeDtypeStruct((1, D), F32))(parts, c_ctx)


def _row_tile(r):
    for t in (512, 256, 128, 80, 64, 40, 32, 16, 8):
        if r % t == 0:
            return t
    return r


def _slot_sum(ref):
    tot = ref[0].astype(F32)
    for i in range(1, ref.shape[0]):
        tot = tot + ref[i].astype(F32)
    return tot


def sum_slots(name, x):
    s, r, c = x.shape
    tr = _row_tile(r)

    def body(x_ref, o_ref):
        o_ref[...] = _slot_sum(x_ref)

    return pl.pallas_call(
        body, name=name, grid=(r // tr,), in_specs=[pl.BlockSpec((s, tr, c), lambda i: (0, i, 0))],
        out_specs=pl.BlockSpec((tr, c), lambda i: (i, 0)), out_shape=jax.ShapeDtypeStruct((r, c), F32),
    )(x)


def _adamw_update(gv, w_ref, m_ref, v_ref, go_ref, d_ref, mo_ref, vo_ref):
    mn = B1 * m_ref[...] + (1.0 - B1) * gv
    vn = B2 * v_ref[...] + (1.0 - B2) * jnp.square(gv)
    m_hat = mn / (1.0 - B1 ** STEP)
    v_hat = vn / (1.0 - B2 ** STEP)
    go_ref[...] = gv
    d_ref[...] = -LR * (m_hat / (jnp.sqrt(v_hat) + AEPS) + WD * w_ref[...])
    mo_ref[...] = mn
    vo_ref[...] = vn


def adamw_slots(name, w, land, sent, me1, m, v, layer, into=None):
    r, c = w.shape[-2:]
    tr = _row_tile(r)
    into = [] if into is None else list(into)

    def body(me_ref, w_ref, land_ref, own_ref, m_ref, v_ref, *rest):
        own = own_ref[...].astype(F32)
        gv = jnp.where(me_ref[0] == 0, own, land_ref[0].astype(F32))
        for s in range(1, N_DEV):
            gv = gv + jnp.where(me_ref[0] == s, own, land_ref[s].astype(F32))
        _adamw_update(gv, w_ref, m_ref, v_ref, *rest[len(into):])

    slab = pl.BlockSpec((None, tr, c), lambda i, me: (layer, i, 0))
    return pl.pallas_call(
        body, name=name, out_shape=[jax.ShapeDtypeStruct(w.shape, F32)] * 4,
        grid_spec=pltpu.PrefetchScalarGridSpec(
            num_scalar_prefetch=1, grid=(r // tr,),
            in_specs=[slab, pl.BlockSpec((N_DEV, tr, c), lambda i, me: (0, i, 0)),
                      pl.BlockSpec((None, tr, c), lambda i, me: (me[0], i, 0)), slab, slab]
            + [pl.BlockSpec(memory_space=pl.ANY)] * len(into),
            out_specs=[slab] * 4),
        input_output_aliases={6 + k: k for k in range(len(into))},
    )(me1, w, land, sent, m, v, *into)


def adamw(name, w, g, m, v, layer=None):
    r, c = w.shape[-2:]
    tr = _row_tile(r)
    stacked = g.ndim == 3

    def body(w_ref, g_ref, m_ref, v_ref, *outs):
        _adamw_update(_slot_sum(g_ref) if stacked else g_ref[...], w_ref, m_ref, v_ref, *outs)

    tile = pl.BlockSpec((tr, c), lambda i: (i, 0))
    slab = tile if layer is None else pl.BlockSpec((None, tr, c), lambda i: (layer, i, 0))
    g_spec = pl.BlockSpec((g.shape[0], tr, c), lambda i: (0, i, 0)) if stacked else tile
    return pl.pallas_call(
        body, name=name, grid=(r // tr,), in_specs=[slab, g_spec, slab, slab], out_specs=[tile] * 4,
        out_shape=[jax.ShapeDtypeStruct((r, c), F32)] * 4,
    )(w, g, m, v)


def _place():
    return lax.axis_index("x"), lax.axis_index("y"), lax.axis_index("c")


def all_gather(name, x, in_vmem):
    r, c = x.shape
    space = pltpu.VMEM if in_vmem else pl.ANY

    def body(x_ref, out_ref, send_sems, recv_sems, local_sem):
        px, py, pc = _place()
        me, sibling = (px, py, pc), (px, py, 1 - pc)
        chips = [(1 - px, py), (px, 1 - py), (1 - px, 1 - py)]

        def rows(qx, qy, qc):
            return out_ref.at[pl.ds((4 * qx + 2 * qy + qc) * r, r), :]

        def copy(k, block, to, src=None):
            return pltpu.make_async_remote_copy(
                src_ref=rows(*block) if src is None else src, dst_ref=rows(*block),
                send_sem=send_sems.at[k], recv_sem=recv_sems.at[k], device_id=to, device_id_type=MESH)

        mine = pltpu.make_async_copy(x_ref, rows(*me), local_sem)
        mine.start()
        first = [copy(0, me, sibling, src=x_ref)]
        first += [copy(1 + j, me, (*chip, pc), src=x_ref) for j, chip in enumerate(chips)]
        for cp in first:
            cp.start()
        passed = [copy(4 + j, (*chip, pc), sibling) for j, chip in enumerate(chips)]
        for j, chip in enumerate(chips):
            copy(1 + j, (*chip, pc), me).wait_recv()
            passed[j].start()
        copy(0, sibling, me).wait_recv()
        for j, chip in enumerate(chips):
            copy(4 + j, (*chip, 1 - pc), me).wait_recv()
        for cp in first + passed:
            cp.wait_send()
        mine.wait()

    return pl.pallas_call(
        body, name=name, out_shape=jax.ShapeDtypeStruct((N_DEV * r, c), x.dtype),
        in_specs=[pl.BlockSpec(memory_space=space)], out_specs=pl.BlockSpec(memory_space=space),
        scratch_shapes=[pltpu.SemaphoreType.DMA((7,)), pltpu.SemaphoreType.DMA((7,)), pltpu.SemaphoreType.DMA],
    )(x)


_HBM =pl.BlockSpec(memory_space=pltpu.HBM)
_SEM = pl.BlockSpec(memory_space=pltpu.SEMAPHORE)
_EFFECT = pltpu.SideEffectType.DATAFLOW_SIDE_EFFECTING


def _peers():
    px, py, pc = _place()
    return [(1 - px if k & 4 else px, 1 - py if k & 2 else py, 1 - pc if k & 1 else pc) for k in range(1, N_DEV)]


def _slot(dev):
    return 4 * dev[0] + 2 * dev[1] + dev[2]


def _split_copies(src_refs, land_refs, send_sems, recv_sems, gather):
    me = _slot(_place())
    return [pltpu.make_async_remote_copy(
        src_ref=src if gather else src.at[_slot(peer)], dst_ref=land.at[me],
        send_sem=send_sems.at[a * (N_DEV - 1) + k], recv_sem=recv_sems.at[a * (N_DEV - 1) + k],
        device_id=peer, device_id_type=MESH)
        for a, (src, land) in enumerate(zip(src_refs, land_refs)) for k, peer in enumerate(_peers())]


def exchange_start(name, srcs, gather, after):
    n = len(srcs)
    lands = [pltpu.HBM((N_DEV,) + s.shape if gather else s.shape, s.dtype) for s in srcs]

    def body(*refs):
        send_sems, recv_sems = refs[2 * n + 1:2 * n + 3]
        for cp in _split_copies(refs[:n], refs[n:2 * n], send_sems, recv_sems, gather):
            cp.start()
        refs[-1][...] = jnp.zeros_like(refs[-1])

    sems = pltpu.SemaphoreType.DMA((n * (N_DEV - 1),))
    res = pl.pallas_call(
        body, name=name,
        out_shape=(sems, sems, *[pltpu.HBM(s.shape, s.dtype) for s in srcs], *lands, jax.ShapeDtypeStruct((8, 128), F32)),
        in_specs=(_HBM,) * (2 * n) + (pl.BlockSpec(memory_space=pl.ANY),),
        out_specs=(_SEM, _SEM) + (_HBM,) * (2 * n) + (pl.BlockSpec(memory_space=pltpu.VMEM),),
        input_output_aliases={i: 2 + i for i in range(2 * n)},
        compiler_params=pltpu.CompilerParams(has_side_effects=_EFFECT),
    )(*[pltpu.with_memory_space_constraint(s, pltpu.HBM) for s in srcs],
      *[pltpu.with_memory_space_constraint(lax.empty(ld.shape, ld.dtype), pltpu.HBM) for ld in lands], after)
    return res[0], res[1], list(res[2:2 + n]), list(res[2 + n:2 + 2 * n]), res[-1]


def exchange_wait(name, started, after, gather):
    send_sems, recv_sems, srcs, lands, _ = started
    n = len(srcs)
    after = list(after) if isinstance(after, (list, tuple)) else [after]

    def body(*refs):
        send_sems, recv_sems = refs[2 * n:2 * n + 2]
        for cp in _split_copies(refs[:n], refs[n:2 * n], send_sems, recv_sems, gather):
            cp.wait_send()
            cp.wait_recv()

    res = pl.pallas_call(
        body, name=name, out_shape=tuple(pltpu.HBM(a.shape, a.dtype) for a in srcs + lands),
        in_specs=(_HBM,) * (2 * n) + (_SEM, _SEM) + (pl.BlockSpec(memory_space=pl.ANY),) * len(after),
        out_specs=(_HBM,) * (2 * n), input_output_aliases={i: i for i in range(2 * n)},
        compiler_params=pltpu.CompilerParams(has_side_effects=_EFFECT),
    )(*srcs, *lands, send_sems, recv_sems, *after)
    return list(res[:n]), list(res[n:])


NCF = FFN_H // FFN_TC


def _size(shape):
    n = 1
    for s in shape:
        n *= s
    return n


def _padded_rows(n_elems, row_mult):
    return -(-n_elems // (D * row_mult)) * row_mult


def _pack_rows(arrs, dtype, row_mult):
    rows, offs, r0 = [], [], 0
    for a in arrs:
        flat = a.reshape(-1).astype(dtype)
        n = _padded_rows(flat.shape[0], row_mult)
        rows.append(jnp.pad(flat, (0, n * D - flat.shape[0])).reshape(n, D))
        offs.append(r0)
        r0 += n
    return jnp.concatenate(rows, 0), offs


def _unpack_rows(buf, offs, shapes):
    lead, out = buf.shape[:-2], []
    for o, shp in zip(offs, shapes):
        n = _size(shp)
        nr = -(-n // D)
        out.append(buf[..., o:o + nr, :].reshape(lead + (nr * D,))[..., :n].reshape(lead + tuple(shp)))
    return out


def _rows3(w):
    return [w[i:i + 1] for i in range(3)]


def f_mod1(xs, ps):
    return f_mod(xs, ps)[:1]


def kernel(x, c, ctx, c_ctx, ada_w, ada_b, norm_mix, norm_ffn, gla_w_in, gla_w_a2, gla_b_a, gla_head_norm, gla_w_out, sc_w_in, sc_conv_w, sc_w_out, ffn_w_up, ffn_conv_w, ffn_conv_b, ffn_w_down, final_norm, loss_target, m_c_ctx, m_ada_w, m_ada_b, m_norm_mix, m_norm_ffn, m_gla_w_in, m_gla_w_a2, m_gla_b_a, m_gla_head_norm, m_gla_w_out, m_sc_w_in, m_sc_conv_w, m_sc_w_out, m_ffn_w_up, m_ffn_conv_w, m_ffn_conv_b, m_ffn_w_down, m_final_norm, v_c_ctx, v_ada_w, v_ada_b, v_norm_mix, v_norm_ffn, v_gla_w_in, v_gla_w_a2, v_gla_b_a, v_gla_head_norm, v_gla_w_out, v_sc_w_in, v_sc_conv_w, v_sc_w_out, v_ffn_w_up, v_ffn_conv_w, v_ffn_conv_b, v_ffn_w_down, v_final_norm):
    names = ["c_ctx", "ada_w", "ada_b", "norm_mix", "norm_ffn", "gla_w_in", "gla_w_a2", "gla_b_a", "gla_head_norm",
             "gla_w_out", "sc_w_in", "sc_conv_w", "sc_w_out", "ffn_w_up", "ffn_conv_w", "ffn_conv_b", "ffn_w_down",
             "final_norm"]
    w_ = dict(zip(names, [c_ctx, ada_w, ada_b, norm_mix, norm_ffn, gla_w_in, gla_w_a2, gla_b_a, gla_head_norm, gla_w_out,
                          sc_w_in, sc_conv_w, sc_w_out, ffn_w_up, ffn_conv_w, ffn_conv_b, ffn_w_down, final_norm]))
    m_ = dict(zip(names, [m_c_ctx, m_ada_w, m_ada_b, m_norm_mix, m_norm_ffn, m_gla_w_in, m_gla_w_a2, m_gla_b_a,
                          m_gla_head_norm, m_gla_w_out, m_sc_w_in, m_sc_conv_w, m_sc_w_out, m_ffn_w_up, m_ffn_conv_w,
                          m_ffn_conv_b, m_ffn_w_down, m_final_norm]))
    v_ = dict(zip(names, [v_c_ctx, v_ada_w, v_ada_b, v_norm_mix, v_norm_ffn, v_gla_w_in, v_gla_w_a2, v_gla_b_a,
                          v_gla_head_norm, v_gla_w_out, v_sc_w_in, v_sc_conv_w, v_sc_w_out, v_ffn_w_up, v_ffn_conv_w,
                          v_ffn_conv_b, v_ffn_w_down, v_final_norm]))
    me = 4 * lax.axis_index("x") + 2 * lax.axis_index("y") + lax.axis_index("c")
    bsz = x.shape[0]
    tm = 256
    nt = SEQ // tm
    ctx_tiles = CTX // tm
    pe = functools.partial(P, per_example=True)

    groups = {"ffn1": [("ffn_w_up", 1), ("ffn_w_down", 1)], "sc": [("sc_w_in", 0), ("sc_w_out", 0)],
              "ffn0": [("ffn_w_up", 0), ("ffn_w_down", 0)], "gla": [("gla_w_in", 0), ("gla_w_out", 0)]}
    ag_groups = {"gin": [("gla_w_in", 0)], "ffn0": [("gla_w_out", 0), ("ffn_w_up", 0), ("ffn_w_down", 0)],
                 "sc": groups["sc"], "ffn1": groups["ffn1"]}
    ag_started = {}

    def start_gather(g, after):
        ag_started[g] = exchange_start(f"ag_{g}_start", [w_[n][i].astype(BF16) for n, i in ag_groups[g]], True, after)
        return ag_started[g][4]

    small_sharded = [c, gla_w_a2, gla_b_a, sc_conv_w, ffn_conv_w]
    pack0, offs0 = _pack_rows(small_sharded, F32, 8)
    g0 = all_gather("ag_small", pack0, True).reshape(N_DEV, pack0.shape[0], D)
    c_all, wa2_s, ba_s, scw_s, fcw_s = _unpack_rows(g0, offs0, [a.shape for a in small_sharded])
    w_a2 = wa2_s[:, 0].transpose(1, 2, 0, 3).reshape(2, RANK, KD)
    b_a = ba_s[:, 0].transpose(1, 0, 2).reshape(2, KD)
    sc_cw = scw_s[:, 0].transpose(1, 0, 2).reshape(3, D)
    ffn_cw = fcw_s.transpose(1, 2, 0, 3).reshape(2, 3, 2 * FFN_H)

    cond = jnp.concatenate([c_all.reshape(N_DEV * bsz, D), c_ctx[None], jnp.zeros((ADA_ROWS - N_DEV * bsz - 1, D), F32)], 0)
    b_mine = lax.dynamic_slice(ada_b, (0, me * ADA_COLS), (2, ADA_COLS)).reshape(2, 1, ADA_COLS)
    mod_part = ada_fwd(cond, ada_w, b_mine)
    mod = all_gather("ag_mod", mod_part.reshape(2 * ADA_ROWS, ADA_COLS), True)
    mod = mod.reshape(N_DEV, 2, ADA_ROWS, ADA_COLS).transpose(1, 2, 0, 3).reshape(2, ADA_ROWS, 6 * D)
    mods = lax.dynamic_slice(mod, (0, bsz * me, 0), (2, bsz, 6 * D))
    md = [[mods[i][:, k * D:(k + 1) * D].reshape(bsz, 1, D) for k in range(6)] for i in range(2)]
    mc = [mod[0, ADA_CTX_ROW, k * D:(k + 1) * D][None] for k in range(2)]

    tok = mod
    for g in ag_groups:
        tok = start_gather(g, tok)
    norm_mix = norm_mix + tok[0, 0]

    def gathered(g, after):
        mine, lands = exchange_wait(f"ag_{g}_wait", ag_started[g], after, True)
        return [lax.dynamic_update_index_in_dim(ld, mn, me, 0) for ld, mn in zip(lands, mine)]

    s_up, w_down = [None, None], [None, None]
    wd = jnp.zeros((128, 2 * KD), F32).at[:RANK, :KD].set(w_a2[0]).at[RANK:2 * RANK, KD:].set(w_a2[1])
    bd = b_a.reshape(1, 2 * KD)
    scw = _rows3(sc_cw)
    head_gain = gla_head_norm.reshape(1, HV)
    gains_mix = [norm_mix[i][None] for i in range(2)]
    gains_ffn = [norm_ffn[i][None] for i in range(2)]

    def tokens(a2d, t_len):
        return a2d.reshape(bsz, t_len, -1)

    def ffn_params(i):
        rows = [ffn_cw[i][t] for t in range(3)] + [ffn_conv_b[i]]
        return [P(a.reshape(2, FFN_H), w=FFN_TC, rows=True) for a in rows]

    def ffn_fwd(i, hn2):
        u = mm(f"ffn_up{i}", V(hn2, "tok"), V(s_up[i], "cols"), out="planes", out_dtype=BF16, planes_t=SEQ)
        act = rowwise(f"ffn_mid{i}", f_ffn_mid, [X(u, w=FFN_TC, planes=True)], ffn_params(i), tm=SEQ, nt=1, nc=NCF,
                      outs=[(FFN_TC, BF16, 1)])[0]
        return u, act

    def arrays(ps):
        return [p["a"] for p in ps]

    ps_in0 = [P(gains_mix[0]), pe(md[0][0]), pe(md[0][1])]
    ps_ctx = [P(gains_mix[0]), P(mc[0]), P(mc[1])]
    hn0 = rowwise("mod_in0", f_mod, [X(x)], ps_in0, tm=tm, nt=nt, outs=[(D, BF16, 1)])[0]
    hnc = rowwise("mod_ctx", f_mod, [X(ctx)], ps_ctx, tm=tm, nt=ctx_tiles, outs=[(D, BF16, 1)])[0]
    hcat = jnp.concatenate([hnc, hn0], axis=1)
    (s_gin,) = gathered("gin", hcat)
    w_gin = V(s_gin, "cols", width=GLA_IN_PAD)
    pcat = tokens(mm("gla_in", V(hcat, "tok"), w_gin, out_dtype=BF16), TT)
    pa_x = X(pcat, w=128, co=(GLA_IN_PAD - 128) // 128)
    la = rowwise("gla_decay", f_decay, [pa_x], [P(wd), P(bd)], tm=tm, nt=TT // tm, outs=[(2 * KD, F32, 1)])[0]
    o2, s_all = gla_fwd(pcat, la)
    post_xs = [X(o2, w=VD, co=0, ro=ctx_tiles, split=HEADS), X(o2, w=VD, co=1, ro=ctx_tiles, split=HEADS),
               X(pcat, w=VD, co=2, ro=ctx_tiles, split=HEADS)]
    yin0 = rowwise("gla_post", f_gla_post, post_xs, [P(head_gain)], tm=tm, nt=nt, outs=[(VD, BF16, HEADS)])[0]
    s_gout, s_up[0], s_down0 = gathered("ffn0", yin0)
    w_gout, w_down[0] = s_gout.reshape(VD, D), s_down0.reshape(FFN_H, D)
    ps_mid0 = [pe(md[0][2]), P(gains_ffn[0]), pe(md[0][3]), pe(md[0][4])]
    y0, h1_0, hn2_0 = mm_res_mod("gla_out", yin0, w_gout, x, *arrays(ps_mid0))
    u0, act0 = ffn_fwd(0, hn2_0)
    ps_in1 = [pe(md[0][5]), P(gains_mix[1]), pe(md[1][0]), pe(md[1][1])]
    fo0, h2_0, hn1 = mm_res_mod("ffn_down0", act0, w_down[0], h1_0, *arrays(ps_in1))

    s_sin, s_sout = gathered("sc", hn1)
    w_sout = s_sout.reshape(D, D)
    p1 = tokens(mm("sc_in", V(hn1, "tok"), V(s_sin, "cols")), SEQ)
    sc_ps = [P(a) for a in scw]
    yin1 = rowwise("sc_mid", f_sc_mid, [X(p1, split=3)], sc_ps, tm=tm, nt=nt, outs=[(D, BF16, 1)])[0]
    ps_mid1 = [pe(md[1][2]), P(gains_ffn[1]), pe(md[1][3]), pe(md[1][4])]
    y1, h1_1, hn2_1 = mm_res_mod("sc_out", yin1, w_sout, h2_0, *arrays(ps_mid1))
    s_up[1], s_down1 = gathered("ffn1", hn2_1)
    w_down[1] = s_down1.reshape(FFN_H, D)
    u1, act1 = ffn_fwd(1, hn2_1)
    fo1 = tokens(mm("ffn_down1", V(act1, "tok"), V(w_down[1])), SEQ)
    loss8, dh1_1, dfo1, dm5_1, g_final = final_loss(h1_1, fo1, md[1][5], final_norm[None], loss_target)

    def ffn_bwd(i, u, act, hn2, dfo):
        dact = tokens(mm(f"ffn_down_dx{i}", V(dfo, "tok"), V(w_down[i]), form="nt", out_dtype=BF16), SEQ)
        g_down = mm(f"ffn_down_dw{i}", V(act, "tok"), V(dfo, "tok"), form="tn", out_dtype=BF16)
        r = rowwise(f"ffn_mid_bwd{i}", f_ffn_mid, [X(u, w=FFN_TC, planes=True)], ffn_params(i), tm=SEQ, nt=1, nc=NCF,
                    douts=[X(dact, w=FFN_TC)], dx={0: BF16}, dp=[0, 1, 2, 3])
        du, g_cw, g_cb = r[0], jnp.stack([a.reshape(2 * FFN_H) for a in r[1:4]]), r[4].reshape(1, 2 * FFN_H)
        dhn2 = tokens(mm(f"ffn_up_dx{i}", V(du, "planes"), V(s_up[i], "cols"), form="nt", out_dtype=BF16), SEQ)
        g_up = mm(f"ffn_up_dw{i}", V(hn2, "tok"), V(du, "planes"), form="tn", out="cols", out_dtype=BF16)
        return dhn2, g_up, row_slots(g_down), g_cw, g_cb

    def res_mod_bwd(name, h, y, ps, dh1, dhn):
        return rowwise(name, f_res_mod, [X(h), X(y)], ps, tm=tm, nt=nt, douts=[X(dh1), X(dhn)],
                       dx={0: F32, 1: BF16}, dp=[0, 1, 2, 3])

    def row_slots(g):
        return g.reshape(N_DEV, -1, g.shape[-1])

    a2a_started = {}

    def send_grads(g, slots, after=None):
        a2a_started[g] = exchange_start(f"a2a_{g}_start", list(slots), False, loss8 if after is None else after)
        return a2a_started[g][4][0, 0]

    def after_start(ps, tok):
        return [dict(ps[0], a=ps[0]["a"] + tok)] + ps[1:]

    dhn2_1, g_up1, g_down1, g_fcw1, g_fcb1 = ffn_bwd(1, u1, act1, hn2_1, dfo1)
    tok = send_grads("ffn1", [g_up1, g_down1])
    dh2_0, dy1, dm2_1, g_nffn1, dm3_1, dm4_1 = res_mod_bwd("res_mod_mid1_bwd", h2_0, y1, after_start(ps_mid1, tok), dh1_1, dhn2_1)
    dyin1 = tokens(mm("sc_out_dx", V(dy1, "tok"), V(w_sout), form="nt", out_dtype=BF16), SEQ)
    g_sout = row_slots(mm("sc_out_dw", V(yin1, "tok"), V(dy1, "tok"), form="tn", out_dtype=BF16))
    r = rowwise("sc_mid_bwd", f_sc_mid, [X(p1, split=3)], sc_ps, tm=tm, nt=nt, douts=[X(dyin1)], dx={0: BF16}, dp=[0, 1, 2])
    dp1, g_scw = r[0], jnp.concatenate(r[1:4], 0)
    dhn1 = tokens(mm("sc_in_dx", V(dp1, "tok"), V(s_sin, "cols"), form="nt", out_dtype=BF16), SEQ)
    g_sin = mm("sc_in_dw", V(hn1, "tok"), V(dp1, "tok"), form="tn", out="cols", out_dtype=BF16)
    tok = send_grads("sc", [g_sin, g_sout])
    dh1_0, dfo0, dm5_0, g_nmix1, dm0_1, dm1_1 = res_mod_bwd("res_mod_in1_bwd", h1_0, fo0, after_start(ps_in1, tok), dh2_0, dhn1)

    dhn2_0, g_up0, g_down0, g_fcw0, g_fcb0 = ffn_bwd(0, u0, act0, hn2_0, dfo0)
    tok = send_grads("ffn0", [g_up0, g_down0])
    dx_res, dy0, dm2_0, g_nffn0, dm3_0, dm4_0 = res_mod_bwd("res_mod_mid0_bwd", x, y0, after_start(ps_mid0, tok), dh1_0, dhn2_0)
    dyin0 = tokens(mm("gla_out_dx", V(dy0, "tok"), V(w_gout), form="nt", out_dtype=BF16), SEQ)
    do, dgate, g_head = rowwise("gla_post_bwd", f_gla_post, post_xs, [P(head_gain)], tm=tm, nt=nt,
                                douts=[X(dyin0, split=HEADS)], dx={0: BF16, 2: BF16}, dp=[0])
    dq2, dk2, dv2, dla = gla_bwd(pcat, la, s_all, do)
    dpa, g_wd, g_bd = rowwise("gla_decay_bwd", f_decay, [pa_x], [P(wd), P(bd)], tm=tm, nt=TT // tm, douts=[X(dla)],
                              dx={0: BF16}, dp=[0, 1])
    dpcat = gla_combine(dq2, dk2, dv2, dgate, dpa)
    dhcat = tokens(mm("gla_in_dx", V(dpcat, "tok"), w_gin, form="nt", out_dtype=BF16), TT)
    grad_x, g_nmix0, dm0_0, dm1_0 = rowwise("mod_in0_bwd", f_mod, [X(x)], ps_in0, tm=tm, nt=nt,
                                            douts=[X(dhcat, ro=ctx_tiles), X(dx_res)], dx={0: F32}, dp=[0, 1, 2])
    g_nmix0c, dmc0, dmc1 = rowwise("mod_ctx_bwd", f_mod1, [X(ctx)], ps_ctx, tm=tm, nt=ctx_tiles, douts=[X(dhcat)],
                                   dx={}, dp=[0, 1, 2])

    zero_row = jnp.zeros((1, 4 * D), F32)
    dmod = [jnp.concatenate([jnp.concatenate([a.reshape(bsz, D) for a in dms], 1), ctx_row], 0)
            for dms, ctx_row in (([dm0_0, dm1_0, dm2_0, dm3_0, dm4_0, dm5_0], jnp.concatenate([dmc0, dmc1, zero_row], 1)),
                                 ([dm0_1, dm1_1, dm2_1, dm3_1, dm4_1, dm5_1], jnp.zeros((1, 6 * D), F32)))]
    g_wa2 = jnp.stack([g_wd[:RANK, :KD], g_wd[RANK:2 * RANK, KD:]])
    small_grads = [jnp.stack(dmod), jnp.concatenate([g_nmix0 + g_nmix0c, g_nmix1], 0), jnp.concatenate([g_nffn0, g_nffn1], 0),
                   g_head, jnp.concatenate([g_fcb0, g_fcb1], 0), g_final, g_wa2, g_bd.reshape(2, KD), g_scw,
                   jnp.stack([g_fcw0, g_fcw1]), loss8[:1]]
    pack1, offs1 = _pack_rows(small_grads, F32, 8)
    ag1 = exchange_start("ag_grads_start", [pack1], True, loss8)
    g_gin = mm("gla_in_dw", V(hcat, "tok"), V(dpcat, "tok"), form="tn", out="cols", out_dtype=BF16, shard_n=GLA_IN // N_DEV,
               after=ag1[4])
    g_gout = row_slots(mm("gla_out_dw", V(yin0, "tok"), V(dy0, "tok"), form="tn", out_dtype=BF16, after=ag1[4]))
    mine1, land1 = exchange_wait("ag_grads_wait", ag1, [g_gin, g_gout], True)
    g1 = lax.dynamic_update_index_in_dim(land1[0], mine1[0], me, 0)
    dmod_all = _unpack_rows(g1, offs1[:1], [small_grads[0].shape])[0]
    tot = _unpack_rows(sum_slots("sum_small", g1), offs1, [a.shape for a in small_grads])
    loss = tot[10][0, 0]
    dm_rows = dmod_all[:, :, :bsz].transpose(1, 0, 2, 3).reshape(2, N_DEV * bsz, 6 * D)
    dm_full = jnp.concatenate([dm_rows, tot[0][:, bsz:], jnp.zeros((2, ADA_ROWS - N_DEV * bsz - 1, 6 * D), F32)], 1)
    dm_mine = lax.dynamic_slice(dm_full, (0, 0, me * ADA_COLS), (2, ADA_ROWS, ADA_COLS))
    g_ada_w, g_ada_b, cpart = ada_bwd(cond, dm_mine, dm_full, ada_w)
    cparts = all_gather("ag_cctx", cpart, True).reshape(N_DEV, ADA_ROWS - ADA_CTX_ROW, D)[:, 0]
    g_cctx = cctx_grad(cparts, c_ctx[None])[0]
    tok = send_grads("gla", [g_gin, g_gout], after=g_cctx)

    def my_cols(full, n):
        return lax.dynamic_slice_in_dim(full, me * n, n, axis=full.ndim - 1)

    grads = {
        "c_ctx": g_cctx, "ada_b": g_ada_b.reshape(2, 6 * D), "norm_mix": tot[1], "norm_ffn": tot[2],
        "gla_head_norm": tot[3], "ffn_conv_b": tot[4], "final_norm": tot[5].reshape(D),
        "gla_w_a2": my_cols(tot[6], KD // N_DEV)[None], "gla_b_a": my_cols(tot[7], KD // N_DEV)[None],
        "sc_conv_w": my_cols(tot[8], D // N_DEV)[None], "ffn_conv_w": my_cols(tot[9], 2 * FFN_H // N_DEV),
    }

    res_ada = adamw("adamw_ada", *[a.reshape(2 * D, ADA_COLS) for a in (ada_w, g_ada_w, m_ada_w, v_ada_w)])
    grads["c_ctx"] = g_cctx + tok
    big = ["gla_w_in", "gla_w_out", "sc_w_in", "sc_w_out", "ffn_w_up", "ffn_w_down"]
    small = [n for n in names if n not in big and n != "ada_w"]
    g_small = _pack_rows([grads[n] for n in small], F32, 8)[0]
    res_small = adamw("adamw_small", _pack_rows([w_[n] for n in small], F32, 8)[0], g_small,
                      _pack_rows([m_[n] for n in small], F32, 8)[0], _pack_rows([v_[n] for n in small], F32, 8)[0])
    offs_s = _pack_rows([w_[n] for n in small], F32, 8)[1]

    big_res, done, me1 = {}, {"small": res_small[0], "ada_w": res_ada[0]}, jnp.reshape(me, (1,)).astype(jnp.int32)
    for g in groups:
        sent, lands = exchange_wait(f"a2a_{g}_wait", a2a_started[g], list(done.values()), False)
        for (n, i), mine, land in zip(groups[g], sent, lands):
            big_res[n] = adamw_slots(f"adamw_{n}{i}", w_[n], land, mine, me1, m_[n], v_[n], i, into=big_res.get(n))
            done[n] = big_res[n][0]

    out = {}
    for kind, idx in (("grad", 0), ("delta", 1), ("new_m", 2), ("new_v", 3)):
        vals = {n: big_res[n][idx] for n in big}
        vals["ada_w"] = res_ada[idx].reshape(ada_w.shape)
        vals.update(zip(small, _unpack_rows(res_small[idx], offs_s, [w_[n].shape for n in small])))
        out[kind] = [vals[n] for n in names]
    return (loss, grad_x, *out["grad"], *out["delta"], *out["new_m"], *out["new_v"])
```

```python
import functools

import jax
import jax.numpy as jnp
from jax import lax
from jax.experimental import pallas as pl
from jax.experimental.pallas import tpu as pltpu

F32 = jnp.float32
BF16 = jnp.bfloat16

N_DEV = 8
D = 1024
SEQ = 2048
CTX = 256
TT = CTX + SEQ
GRID_W = 64
CHUNK = 64
HEADS = 4
HK = 128
HV = 256
KD = 512
VD = 1024
RANK = 16
TAU = 16.0
GLA_IN = 3104
GLA_IN_PAD = 3200
FFN_H = 2560
FFN_TC = 256
EPS = 1e-6
LR, B1, B2, AEPS, WD, STEP = 0.001, 0.9, 0.999, 1e-08, 0.01, 10
MESH = pl.DeviceIdType.MESH


def _blocks(n):
    return [n] + [t for t in range(n - n % 128, 0, -128) if n % t == 0 and t != n]


def V(arr, kind="flat", width=None):
    if kind == "tok":
        return V(arr.reshape(-1, arr.shape[-1]))
    if kind == "flat":
        r, c = arr.shape
        return dict(a=arr, kind=kind, shape=(r, c), rows=_blocks(r), cols=_blocks(c))
    if kind == "planes":
        bsz, _, t, ch = arr.shape
        return dict(a=arr, kind=kind, shape=(bsz * t, 2 * ch), rows=_blocks(t), cols=[2 * ch] + _blocks(ch), t=t, ch=ch)
    _, r, n = arr.shape
    if width is not None:
        return dict(a=arr, kind=kind, shape=(r, width), rows=_blocks(r), cols=[width], n=n, pad=width - N_DEV * n)
    return dict(a=arr, kind=kind, shape=(r, N_DEV * n), rows=_blocks(r), cols=[8 * n, 4 * n, 2 * n], n=n, pad=0)


def _view_spec(v, br, bc, idx):
    if v["kind"] == "flat":
        return pl.BlockSpec((br, bc), idx)
    if v["kind"] == "planes":
        nt = v["t"] // br
        if bc == 2 * v["ch"]:
            return pl.BlockSpec((None, 2, br, v["ch"]), lambda i, j, k: (idx(i, j, k)[0] // nt, 0, idx(i, j, k)[0] % nt, 0))
        nch = v["ch"] // bc

        def at(i, j, k):
            r, c = idx(i, j, k)
            return r // nt, c // nch, r % nt, c % nch
        return pl.BlockSpec((None, None, br, bc), at)
    return pl.BlockSpec(((bc - v["pad"]) // v["n"], br, v["n"]), lambda i, j, k: (idx(i, j, k)[1], idx(i, j, k)[0], 0))


def _out_view(kind, rows, cols, dtype, planes_t=None, shard_n=None):
    if kind == "flat":
        shape = (rows, cols)
    elif kind == "planes":
        shape = (rows // planes_t, 2, planes_t, cols // 2)
    elif shard_n is not None:
        return V(jax.ShapeDtypeStruct((N_DEV, rows, shard_n), dtype), kind, width=cols)
    else:
        shape = (N_DEV, rows, cols // N_DEV)
    return V(jax.ShapeDtypeStruct(shape, dtype), kind)


MM_VMEM_BUDGET = 40 * 2 ** 20
MM_VMEM_LIMIT = 56 * 2 ** 20
MM_MAX_TILE = 1536


def _mm_tiles(m, n, kk, ms, ns, ks, a_bytes, b_bytes, o_bytes):
    best = None
    for tk in ks:
        for tm in [t for t in ms if t <= MM_MAX_TILE] or ms:
            for tn in [t for t in ns if t <= MM_MAX_TILE] or ns:
                one_k = tk == kk
                need = 2 * (tm * tk * a_bytes + tk * tn * b_bytes + tm * tn * o_bytes) + (0 if one_k else tm * tn * 4)
                if need > MM_VMEM_BUDGET:
                    continue
                steps = (m // tm) * (n // tn) * (kk // tk)
                traffic = (m * kk * a_bytes * (1 if one_k else n // tn)
                           + kk * n * b_bytes * (1 if one_k and n == tn else m // tm) + m * n * o_bytes)
                fill = (tm * tk * a_bytes + tk * tn * b_bytes) / 2.5e12
                cost = max(2.0 * m * n * kk / (9e14 if one_k else 6.5e14), traffic / 2.5e12) + steps * 0.4e-6 + fill
                if best is None or cost < best[0]:
                    best = (cost, tm, tn, tk)
    return best[1:]


def mm(name, a, b, form="nn", out="flat", out_dtype=F32, planes_t=None, shard_n=None, after=None):
    (m, kk) = a["shape"][::-1] if form == "tn" else a["shape"]
    n = b["shape"][0] if form == "nt" else b["shape"][1]
    assert (b["shape"][1] if form == "nt" else b["shape"][0]) == kk, (name, a["shape"], b["shape"])
    o = _out_view(out, m, n, out_dtype, planes_t, shard_n)
    a_m, a_k = (a["cols"], a["rows"]) if form == "tn" else (a["rows"], a["cols"])
    b_k, b_n = (b["cols"], b["rows"]) if form == "nt" else (b["rows"], b["cols"])
    tm, tn, tk = _mm_tiles(m, n, kk, [t for t in a_m if t in o["rows"]], [t for t in b_n if t in o["cols"]],
                           [t for t in a_k if t in b_k], a["a"].dtype.itemsize, b["a"].dtype.itemsize,
                           jnp.dtype(out_dtype).itemsize)
    nk = kk // tk
    dn = (((0 if form == "tn" else 1,), (1 if form == "nt" else 0,)), ((), ()))

    def load(ref, v):
        if len(ref.shape) == 3:
            pieces = [ref[p].astype(BF16) for p in range(ref.shape[0])]
            if v.get("pad"):
                pieces.append(jnp.zeros(ref.shape[1:2] + (v["pad"],), BF16))
            return jnp.concatenate(pieces, axis=-1)
        return ref[...].astype(BF16)

    def store(o_ref, val):
        val = val.astype(out_dtype)
        if len(o_ref.shape) == 3:
            w = o_ref.shape[-1]
            for p in range(o_ref.shape[0]):
                o_ref[p] = val[:, p * w:(p + 1) * w]
        else:
            o_ref[...] = val

    def body(a_ref, b_ref, *rest):
        o_ref, acc = rest[0 if after is None else 1], rest[1 if after is None else 2:]
        if nk == 1:
            store(o_ref, lax.dot_general(load(a_ref, a), load(b_ref, b), dn, preferred_element_type=F32))
            return
        k, acc_ref = pl.program_id(2), acc[0]

        @pl.when(k == 0)
        def _():
            acc_ref[...] = jnp.zeros_like(acc_ref)

        acc_ref[...] += lax.dot_general(load(a_ref, a), load(b_ref, b), dn, preferred_element_type=F32)

        @pl.when(k == nk - 1)
        def _():
            store(o_ref, acc_ref[...])

    if form == "tn":
        a_spec = _view_spec(a, tk, tm, lambda i, j, k: (k, i))
    else:
        a_spec = _view_spec(a, tm, tk, lambda i, j, k: (i, k))
    if form == "nt":
        b_spec = _view_spec(b, tn, tk, lambda i, j, k: (j, k))
    else:
        b_spec = _view_spec(b, tk, tn, lambda i, j, k: (k, j))
    return pl.pallas_call(
        body, name=name, grid=(m // tm, n // tn, nk),
        in_specs=[a_spec, b_spec] + ([] if after is None else [pl.BlockSpec(memory_space=pl.ANY)]),
        out_specs=_view_spec(o, tm, tn, lambda i, j, k: (i, j)), out_shape=o["a"],
        scratch_shapes=[pltpu.VMEM((tm, tn), F32)] if nk > 1 else [],
        compiler_params=pltpu.CompilerParams(dimension_semantics=("parallel", "parallel", "arbitrary"),
                                             vmem_limit_bytes=MM_VMEM_LIMIT),
    )(a["a"], b["a"], *([] if after is None else [after]))


def mm_res_mod(name, a, w, h, gate, gain, shift, scale):
    bsz, t_len, kk = a.shape
    tm = 512
    per = t_len // tm

    def body(a_ref, w_ref, h_ref, gate_ref, gain_ref, shift_ref, scale_ref, y_ref, h1_ref, hn_ref):
        y = jnp.dot(a_ref[...].astype(BF16), w_ref[...].astype(BF16), preferred_element_type=F32)
        h1 = h_ref[...] + gate_ref[...] * y
        y_ref[...] = y.astype(BF16)
        h1_ref[...] = h1
        hn_ref[...] = _mod(h1, gain_ref[...], shift_ref[...], scale_ref[...]).astype(BF16)

    def tile(width):
        return pl.BlockSpec((None, tm, width), lambda i: (i // per, i % per, 0))

    per_ex = pl.BlockSpec((None, 1, D), lambda i: (i // per, 0, 0))
    return pl.pallas_call(
        body, name=name, grid=(bsz * per,),
        in_specs=[tile(kk), pl.BlockSpec((kk, D), lambda i: (0, 0)), tile(D), per_ex, pl.BlockSpec((1, D), lambda i: (0, 0)),
                  per_ex, per_ex],
        out_specs=[tile(D)] * 3,
        out_shape=[jax.ShapeDtypeStruct((bsz, t_len, D), BF16), jax.ShapeDtypeStruct((bsz, t_len, D), F32),
                   jax.ShapeDtypeStruct((bsz, t_len, D), BF16)],
        compiler_params=pltpu.CompilerParams(dimension_semantics=("parallel",), vmem_limit_bytes=MM_VMEM_LIMIT),
    )(a, w, h, gate, gain, shift, scale)


def X(arr, w=None, co=0, ro=0, split=1, planes=False):
    return dict(a=arr, w=arr.shape[-1] if w is None else w, co=co, ro=ro, split=2 if planes else split,
                mode="planes" if planes else "cols")


def P(arr, per_example=False, w=None, split=1, rows=False):
    return dict(a=arr, e=per_example, w=arr.shape[-1] if w is None else w, split=arr.shape[-2] if rows else split,
                mode="rows" if rows else "cols")


def _pieces(ref, s):
    if s["mode"] == "planes":
        return [ref[0], ref[1]]
    if s["mode"] == "rows":
        return [ref[i:i + 1, :] for i in range(s["split"])]
    w = ref.shape[-1] // s["split"]
    return [ref[:, i * w:(i + 1) * w] for i in range(s["split"])]


def _store(ref, pieces, s, accumulate=False):
    w = ref.shape[-1] // len(pieces)
    for i, p in enumerate(pieces):
        at = (i,) if s["mode"] == "planes" else (slice(i, i + 1),) if s["mode"] == "rows" else (slice(None), slice(i * w, (i + 1) * w))
        if accumulate:
            ref[at] += p.astype(ref.dtype)
        else:
            ref[at] = p.astype(ref.dtype)


def rowwise(name, f, xs, ps, *, tm, nt, nc=1, outs=None, douts=None, dx=None, dp=None):
    bsz = xs[0]["a"].shape[0]
    fwd = douts is None
    nx, np_ = len(xs), len(ps)
    douts = [] if fwd else douts
    dx = {} if fwd else dx
    dp = [] if fwd else dp

    def x_spec(s):
        if s["mode"] == "planes":
            return pl.BlockSpec((None, 2, tm, s["w"]), lambda c, b, t, s=s: (b, 0, t + s["ro"], c + s["co"]))
        return pl.BlockSpec((None, tm, s["w"]), lambda c, b, t, s=s: (b, t + s["ro"], c + s["co"]))

    def x_out(s, dt):
        if s["mode"] == "planes":
            return (jax.ShapeDtypeStruct((bsz, 2, nt * tm, nc * s["w"]), dt),
                    pl.BlockSpec((None, 2, tm, s["w"]), lambda c, b, t: (b, 0, t, c)))
        return (jax.ShapeDtypeStruct((bsz, nt * tm, nc * s["w"]), dt), pl.BlockSpec((None, tm, s["w"]), lambda c, b, t: (b, t, c)))

    def p_spec(s):
        r = s["a"].shape[-2]
        if s["e"]:
            return pl.BlockSpec((None, r, s["w"]), lambda c, b, t: (b, 0, c))
        return pl.BlockSpec((r, s["w"]), lambda c, b, t: (0, c))

    in_specs = [x_spec(s) for s in xs] + [p_spec(s) for s in ps] + [x_spec(s) for s in douts]
    operands = [s["a"] for s in xs] + [s["a"] for s in ps] + [s["a"] for s in douts]
    if fwd:
        out_modes = [dict(mode="cols", split=sp) for (_, _, sp) in outs]
        out_shape = [jax.ShapeDtypeStruct((bsz, nt * tm, nc * w), dt) for (w, dt, _) in outs]
        out_specs = [pl.BlockSpec((None, tm, w), lambda c, b, t: (b, t, c)) for (w, _, _) in outs]
    else:
        dx_outs = [x_out(xs[i], dt) for i, dt in dx.items()]
        out_shape, out_specs = [o[0] for o in dx_outs], [o[1] for o in dx_outs]
        for j in dp:
            s = ps[j]
            r = s["a"].shape[-2]
            if s["e"]:
                out_shape.append(jax.ShapeDtypeStruct((bsz, r, nc * s["w"]), F32))
                out_specs.append(pl.BlockSpec((None, r, s["w"]), lambda c, b, t: (b, 0, c)))
            else:
                out_shape.append(jax.ShapeDtypeStruct((r, nc * s["w"]), F32))
                out_specs.append(pl.BlockSpec((r, s["w"]), lambda c, b, t: (0, c)))

    def body(*refs):
        x_refs, p_refs = refs[:nx], refs[nx:nx + np_]
        d_refs = refs[nx + np_:nx + np_ + len(douts)]
        o_refs = refs[nx + np_ + len(douts):]
        xv = [[p.astype(F32) for p in _pieces(r, s)] for r, s in zip(x_refs, xs)]
        pv = [[p.astype(F32) for p in _pieces(r, s)] for r, s in zip(p_refs, ps)]
        if fwd:
            for r, pieces, s in zip(o_refs, f(xv, pv), out_modes):
                _store(r, pieces, s)
            return
        _, vjp = jax.vjp(f, xv, pv)
        cot = [[p.astype(F32) for p in _pieces(r, s)] for r, s in zip(d_refs, douts)]
        dxv, dpv = vjp(cot)
        for r, i in zip(o_refs, dx):
            _store(r, dxv[i], xs[i])
        b, t = pl.program_id(1), pl.program_id(2)
        for r, j in zip(o_refs[len(dx):], dp):
            first = (t == 0) if ps[j]["e"] else jnp.logical_and(b == 0, t == 0)

            @pl.when(first)
            def _(r=r, j=j):
                _store(r, dpv[j], ps[j])

            @pl.when(jnp.logical_not(first))
            def _(r=r, j=j):
                _store(r, dpv[j], ps[j], accumulate=True)

    res = pl.pallas_call(
        body, name=name, grid=(nc, bsz, nt), in_specs=in_specs, out_specs=out_specs, out_shape=out_shape,
        compiler_params=pltpu.CompilerParams(dimension_semantics=("arbitrary", "arbitrary", "arbitrary")),
    )(*operands)
    return res


def _keep_rows(a, shift, keep):
    n = a.shape[0]
    t = lax.broadcasted_iota(jnp.int32, a.shape, 0)
    return jnp.where(keep(t, n), pltpu.roll(a, shift % n, 0), 0.0)


def _shift_pair(step, keep_prev=None, keep_next=None):
    @jax.custom_vjp
    def prev(a):
        if keep_prev is None:
            return jnp.concatenate([jnp.zeros((step,) + a.shape[1:], a.dtype), a[:a.shape[0] - step]], axis=0)
        return _keep_rows(a, step, keep_prev)

    @jax.custom_vjp
    def nxt(a):
        if keep_next is None:
            return jnp.concatenate([a[step:], jnp.zeros((step,) + a.shape[1:], a.dtype)], axis=0)
        return _keep_rows(a, -step, keep_next)

    prev.defvjp(lambda a: (prev(a), None), lambda _, g: (nxt(g),))
    nxt.defvjp(lambda a: (nxt(a), None), lambda _, g: (prev(g),))
    return prev, nxt


prev_tok, next_tok = _shift_pair(1, lambda t, n: t % GRID_W != 0, lambda t, n: t % GRID_W != GRID_W - 1)
prev_row, next_row = _shift_pair(GRID_W)


@jax.custom_vjp
def bdot(a, w):
    return jnp.dot(a.astype(BF16), w.astype(BF16), preferred_element_type=F32)


def _bdot_bwd(res, g):
    a, w = res
    gb = g.astype(BF16)
    da = lax.dot_general(gb, w.astype(BF16), (((1,), (1,)), ((), ())), preferred_element_type=F32)
    dw = lax.dot_general(a.astype(BF16), gb, (((0,), (0,)), ((), ())), preferred_element_type=F32)
    return da, dw


bdot.defvjp(lambda a, w: (bdot(a, w), (a, w)), _bdot_bwd)


@jax.custom_vjp
def log_sigmoid(z):
    return jnp.minimum(z, 0.0) - jnp.log(1.0 + jnp.exp(-jnp.abs(z)))


def _lsig_bwd(z, g):
    e = jnp.exp(-jnp.abs(z))
    return (g * jnp.where(z >= 0, e, 1.0) / (1.0 + e),)


log_sigmoid.defvjp(lambda z: (log_sigmoid(z), z), _lsig_bwd)


def silu(x):
    return x * jax.nn.sigmoid(x)


def _rms(x):
    return x * lax.rsqrt(jnp.mean(x * x, axis=-1, keepdims=True) + EPS)


def _mod(x, gain, shift, scale):
    return _rms(x) * gain * (1.0 + scale) + shift


def f_mod(xs, ps):
    ((h,),), ((gain,), (shift,), (scale,)) = xs, ps
    return [[_mod(h, gain, shift, scale)], [h]]


def f_res_mod(xs, ps):
    ((h,), (y,)), ((gate,), (gain,), (shift,), (scale,)) = xs, ps
    h1 = h + gate * y
    return [[h1], [_mod(h1, gain, shift, scale)]]


def f_ffn_mid(xs, ps):
    ((ua, ug),), ((w0a, w0g), (w1a, w1g), (w2a, w2g), (ba, bg)) = xs, ps
    a = w0a * prev_row(ua) + w1a * ua + w2a * next_row(ua) + ba
    g = w0g * prev_row(ug) + w1g * ug + w2g * next_row(ug) + bg
    return [[a * silu(g)]]


def f_sc_mid(xs, ps):
    ((bg, cg, v),), ((w0,), (w1,), (w2,)) = xs, ps
    z = cg * v
    return [[bg * (w0 * prev_tok(z) + w1 * z + w2 * next_tok(z))]]


def f_decay(xs, ps):
    ((a,),), ((wd,), (bd,)) = xs, ps
    return [[log_sigmoid(bdot(a, wd) + bd) / TAU]]


def f_gla_post(xs, ps):
    (of, ob, g), ((gain,),) = xs, ps
    return [[_rms(a + b) * gain * silu(c) for a, b, c in zip(of, ob, g)]]


NCH = TT // CHUNK
CTX_CH = CTX // CHUNK
_NT = (((1,), (1,)), ((), ()))
_TN = (((0,), (0,)), ((), ()))
_NN = (((1,), (0,)), ((), ()))


def _chunk_of(d, j):
    return jnp.where(d == 0, j, jnp.where(j < CTX_CH, CTX_CH - 1 - j, NCH + CTX_CH - 1 - j))


def _dot(a, b, dn):
    return lax.dot_general(a, b, dn, preferred_element_type=F32)


def _cumsum_rows(g, suffix):
    n = g.shape[0]
    row = lax.broadcasted_iota(jnp.int32, g.shape, 0)
    s = 1
    while s < n:
        if suffix:
            g = g + jnp.where(row < n - s, pltpu.roll(g, n - s, 0), 0.0)
        else:
            g = g + jnp.where(row >= s, pltpu.roll(g, s, 0), 0.0)
        s *= 2
    return g


def _causal(backward):
    row = lax.broadcasted_iota(jnp.int32, (CHUNK, CHUNK), 0)
    col = lax.broadcasted_iota(jnp.int32, (CHUNK, CHUNK), 1)
    return col >= row if backward else col <= row


def _gla_in_specs(bsz, rev):
    def blk(d, j):
        return _chunk_of(d, (NCH - 1 - j) if rev else j)

    return [
        pl.BlockSpec((bsz, CHUNK, KD), lambda d, j: (0, blk(d, j), 0)),
        pl.BlockSpec((bsz, CHUNK, KD), lambda d, j: (0, blk(d, j), 1)),
        pl.BlockSpec((bsz, CHUNK, VD), lambda d, j: (0, blk(d, j), 1)),
        pl.BlockSpec((bsz, CHUNK, KD), lambda d, j: (0, blk(d, j), d)),
    ], blk


def gla_fwd(pcat, la):
    bsz = pcat.shape[0]
    in_specs, blk = _gla_in_specs(bsz, False)

    def body(q_ref, k_ref, v_ref, la_ref, o_ref, s_ref, st):
        d, j = pl.program_id(0), pl.program_id(1)

        @pl.when(j == 0)
        def _():
            st[...] = jnp.zeros_like(st)

        s_ref[...] = st[...]

        def scan(backward):
            causal = _causal(backward)
            for e in range(bsz):
                g_all = la_ref[e]
                b_all = _cumsum_rows(g_all, backward)
                bl_all = jnp.sum(g_all, axis=0, keepdims=True)
                qs_all = (q_ref[e].astype(F32) * (HK ** -0.5) * jnp.exp(b_all)).astype(BF16)
                ks_all = (k_ref[e] * jnp.exp(-b_all)).astype(BF16)
                kd_all = (k_ref[e] * jnp.exp(bl_all - b_all)).astype(BF16)
                el_all = jnp.exp(bl_all)
                for h in range(HEADS):
                    ks_, vs_ = slice(h * HK, (h + 1) * HK), slice(h * HV, (h + 1) * HV)
                    qs, ks, kd, v = qs_all[:, ks_], ks_all[:, ks_], kd_all[:, ks_], v_ref[e, :, vs_].astype(BF16)
                    s = st[e, h]
                    att = jnp.where(causal, _dot(qs, ks, _NT), 0.0).astype(BF16)
                    o_ref[e, :, vs_] = _dot(qs, s.astype(BF16), _NT) + _dot(att, v, _NN)
                    st[e, h] = el_all[:, ks_] * s + _dot(v, kd, _TN)

        @pl.when(d == 0)
        def _():
            scan(False)

        @pl.when(d == 1)
        def _():
            scan(True)

    return pl.pallas_call(
        body, name="gla_fwd", grid=(2, NCH), in_specs=in_specs,
        out_specs=[pl.BlockSpec((bsz, CHUNK, VD), lambda d, j: (0, blk(d, j), d)),
                   pl.BlockSpec((bsz, None, None, HEADS, HV, HK), lambda d, j: (0, d, j, 0, 0, 0))],
        out_shape=[jax.ShapeDtypeStruct((bsz, TT, 2 * VD), F32), jax.ShapeDtypeStruct((bsz, 2, NCH, HEADS, HV, HK), F32)],
        scratch_shapes=[pltpu.VMEM((bsz, HEADS, HV, HK), F32)],
        compiler_params=pltpu.CompilerParams(dimension_semantics=("arbitrary", "arbitrary")),
    )(pcat, pcat, pcat, la)


def gla_bwd(pcat, la, s_all, do):
    bsz = pcat.shape[0]
    in_specs, blk = _gla_in_specs(bsz, True)
    in_specs += [
        pl.BlockSpec((bsz, None, None, HEADS, HV, HK), lambda d, j: (0, d, NCH - 1 - j, 0, 0, 0)),
        pl.BlockSpec((bsz, CHUNK, VD), lambda d, j: (0, jnp.maximum(blk(d, j) - CTX_CH, 0), 0)),
    ]

    def body(q_ref, k_ref, v_ref, la_ref, s_ref, do_ref, dq_ref, dk_ref, dv_ref, dla_ref, dst):
        d, j = pl.program_id(0), pl.program_id(1)

        @pl.when(j == 0)
        def _():
            dst[...] = jnp.zeros_like(dst)

        latent = blk(d, j) >= CTX_CH
        scale = HK ** -0.5

        def scan(backward):
            causal = _causal(backward)
            for e in range(bsz):
                g_all = la_ref[e]
                b_all = _cumsum_rows(g_all, backward)
                bl_all = jnp.sum(g_all, axis=0, keepdims=True)
                ex_all, ei_all, ed_all, el_all = jnp.exp(b_all), jnp.exp(-b_all), jnp.exp(bl_all - b_all), jnp.exp(bl_all)
                qs_all, ks_all, kd_all = q_ref[e].astype(F32) * scale * ex_all, k_ref[e] * ei_all, k_ref[e] * ed_all
                qsb_all, ksb_all, kdb_all = qs_all.astype(BF16), ks_all.astype(BF16), kd_all.astype(BF16)
                db_parts, dbl_parts = [], []
                for h in range(HEADS):
                    ks_, vs_ = slice(h * HK, (h + 1) * HK), slice(h * HV, (h + 1) * HV)
                    qs, ks, kd, el = qs_all[:, ks_], ks_all[:, ks_], kd_all[:, ks_], el_all[:, ks_]
                    qsb, ksb, kdb, v = qsb_all[:, ks_], ksb_all[:, ks_], kdb_all[:, ks_], v_ref[e, :, vs_].astype(BF16)
                    s, ds1 = s_ref[e, h], dst[e, h]
                    sb, ds1b = s.astype(BF16), ds1.astype(BF16)
                    dob = jnp.where(latent, do_ref[e, :, vs_], 0.0).astype(BF16)
                    att = jnp.where(causal, _dot(qsb, ksb, _NT), 0.0).astype(BF16)
                    datt = jnp.where(causal, _dot(dob, v, _NT), 0.0).astype(BF16)
                    dqs = _dot(dob, sb, _NN) + _dot(datt, ksb, _NN)
                    dks = _dot(datt, qsb, _TN)
                    dv_ref[e, :, vs_] = (_dot(att, dob, _TN) + _dot(kdb, ds1b, _NT)).astype(BF16)
                    dkd = _dot(v, ds1b, _NN)
                    dst[e, h] = _dot(dob, qsb, _TN) + el * ds1
                    del_ = jnp.sum(s * ds1, axis=0, keepdims=True)
                    dq_ref[e, :, ks_] = (dqs * ex_all[:, ks_] * scale).astype(BF16)
                    dk_ref[e, :, ks_] = (dks * ei_all[:, ks_] + dkd * ed_all[:, ks_]).astype(BF16)
                    db_parts.append(dqs * qs - dks * ks - dkd * kd)
                    dbl_parts.append(jnp.sum(dkd * kd, axis=0, keepdims=True) + del_ * el)
                dla_ref[e] = _cumsum_rows(jnp.concatenate(db_parts, -1), not backward) + jnp.concatenate(dbl_parts, -1)

        @pl.when(d == 0)
        def _():
            scan(False)

        @pl.when(d == 1)
        def _():
            scan(True)

    return pl.pallas_call(
        body, name="gla_bwd", grid=(2, NCH), in_specs=in_specs,
        out_specs=[pl.BlockSpec((None, bsz, CHUNK, KD), lambda d, j: (d, 0, blk(d, j), 0)),
                   pl.BlockSpec((None, bsz, CHUNK, KD), lambda d, j: (d, 0, blk(d, j), 0)),
                   pl.BlockSpec((None, bsz, CHUNK, VD), lambda d, j: (d, 0, blk(d, j), 0)),
                   pl.BlockSpec((bsz, CHUNK, KD), lambda d, j: (0, blk(d, j), d))],
        out_shape=[jax.ShapeDtypeStruct((2, bsz, TT, KD), BF16), jax.ShapeDtypeStruct((2, bsz, TT, KD), BF16),
                   jax.ShapeDtypeStruct((2, bsz, TT, VD), BF16), jax.ShapeDtypeStruct((bsz, TT, 2 * KD), F32)],
        scratch_shapes=[pltpu.VMEM((bsz, HEADS, HV, HK), F32)],
        compiler_params=pltpu.CompilerParams(dimension_semantics=("arbitrary", "arbitrary")),
    )(pcat, pcat, pcat, la, s_all, do)


def gla_combine(dq2, dk2, dv2, dgate, dpa):
    bsz = dgate.shape[0]
    tm = CTX

    def body(dq_ref, dk_ref, dv_ref, dg_ref, dpa_ref, o_ref):
        t = pl.program_id(1)
        o_ref[:, 0:KD] = (dq_ref[0].astype(F32) + dq_ref[1].astype(F32)).astype(BF16)
        o_ref[:, KD:2 * KD] = (dk_ref[0].astype(F32) + dk_ref[1].astype(F32)).astype(BF16)
        o_ref[:, 2 * KD:2 * KD + VD] = (dv_ref[0].astype(F32) + dv_ref[1].astype(F32)).astype(BF16)
        o_ref[:, 2 * KD + VD:2 * KD + 2 * VD] = jnp.where(t > 0, dg_ref[...], 0).astype(BF16)
        o_ref[:, 2 * KD + 2 * VD:] = dpa_ref[...].astype(BF16)

    return pl.pallas_call(
        body, name="gla_combine", grid=(bsz, TT // tm),
        in_specs=[pl.BlockSpec((2, None, tm, KD), lambda b, t: (0, b, t, 0)),
                  pl.BlockSpec((2, None, tm, KD), lambda b, t: (0, b, t, 0)),
                  pl.BlockSpec((2, None, tm, VD), lambda b, t: (0, b, t, 0)),
                  pl.BlockSpec((None, tm, VD), lambda b, t: (b, jnp.maximum(t - 1, 0), 0)),
                  pl.BlockSpec((None, tm, 128), lambda b, t: (b, t, 0))],
        out_specs=pl.BlockSpec((None, tm, GLA_IN_PAD), lambda b, t: (b, t, 0)),
        out_shape=jax.ShapeDtypeStruct((bsz, TT, GLA_IN_PAD), BF16),
        compiler_params=pltpu.CompilerParams(dimension_semantics=("arbitrary", "arbitrary")),
    )(dq2, dk2, dv2, dgate, dpa)


def final_loss(h1, fo, gate, gain, tgt):
    bsz, t_len, _ = h1.shape
    tm = 256

    def body(h_ref, f_ref, gate_ref, gain_ref, tgt_ref, loss_ref, dh_ref, df_ref, dgate_ref, dgain_ref):
        b, t = pl.program_id(0), pl.program_id(1)
        target = tgt_ref[...]

        def core(h, fo_, gate_, gain_):
            e = _rms(h + gate_ * fo_) * gain_ - target
            return jnp.sum(0.5 * jnp.sum(e * e, axis=-1, keepdims=True) / D, axis=0, keepdims=True)

        loss, vjp = jax.vjp(core, h_ref[...], f_ref[...], gate_ref[...], gain_ref[...])
        dh, df, dgate, dgain = vjp(jnp.ones((1, 1), F32))
        dh_ref[...] = dh
        df_ref[...] = df.astype(BF16)
        first = jnp.logical_and(b == 0, t == 0)

        @pl.when(first)
        def _():
            loss_ref[...] = jnp.broadcast_to(loss, loss_ref.shape)
            dgain_ref[...] = dgain

        @pl.when(jnp.logical_not(first))
        def _():
            loss_ref[...] += jnp.broadcast_to(loss, loss_ref.shape)
            dgain_ref[...] += dgain

        @pl.when(t == 0)
        def _():
            dgate_ref[...] = dgate

        @pl.when(t > 0)
        def _():
            dgate_ref[...] += dgate

    tile = pl.BlockSpec((None, tm, D), lambda b, t: (b, t, 0))
    per_ex = pl.BlockSpec((None, 1, D), lambda b, t: (b, 0, 0))
    shared = pl.BlockSpec((1, D), lambda b, t: (0, 0))
    return pl.pallas_call(
        body, name="final_loss", grid=(bsz, t_len // tm),
        in_specs=[tile, tile, per_ex, shared, tile],
        out_specs=[pl.BlockSpec((8, 128), lambda b, t: (0, 0)), tile, tile, per_ex, shared],
        out_shape=[jax.ShapeDtypeStruct((8, 128), F32), jax.ShapeDtypeStruct(h1.shape, F32),
                   jax.ShapeDtypeStruct(h1.shape, BF16), jax.ShapeDtypeStruct((bsz, 1, D), F32),
                   jax.ShapeDtypeStruct((1, D), F32)],
        compiler_params=pltpu.CompilerParams(dimension_semantics=("arbitrary", "arbitrary")),
    )(h1, fo, gate, gain, tgt)


ADA_ROWS = 24
ADA_CTX_ROW = 16
ADA_COLS = 6 * D // N_DEV


def ada_fwd(cond, w, b):
    def body(c_ref, w_ref, b_ref, o_ref):
        s = silu(c_ref[...]).astype(BF16)
        o_ref[...] = jnp.dot(s, w_ref[...].astype(BF16), preferred_element_type=F32) + b_ref[...]

    return pl.pallas_call(
        body, name="ada_fwd", grid=(2,),
        in_specs=[pl.BlockSpec((ADA_ROWS, D), lambda i: (0, 0)), pl.BlockSpec((None, D, ADA_COLS), lambda i: (i, 0, 0)),
                  pl.BlockSpec((None, 1, ADA_COLS), lambda i: (i, 0, 0))],
        out_specs=pl.BlockSpec((None, ADA_ROWS, ADA_COLS), lambda i: (i, 0, 0)),
        out_shape=jax.ShapeDtypeStruct((2, ADA_ROWS, ADA_COLS), F32),
    )(cond, w, b)


def ada_bwd(cond, dm_mine, dm_full, w):
    def body(c_ref, dm_ref, dmf_ref, w_ref, gw_ref, gb_ref, cp_ref):
        i = pl.program_id(0)
        s = silu(c_ref[...]).astype(BF16)
        dm = dm_ref[...].astype(BF16)
        gw_ref[...] = _dot(s, dm, _TN)
        gb_ref[...] = jnp.sum(dmf_ref[...], axis=0, keepdims=True)

        @pl.when(i == 0)
        def _():
            cp_ref[...] = _dot(dm_ref[ADA_CTX_ROW:, :].astype(BF16), w_ref[...].astype(BF16), _NT)

    return pl.pallas_call(
        body, name="ada_bwd", grid=(2,),
        in_specs=[pl.BlockSpec((ADA_ROWS, D), lambda i: (0, 0)), pl.BlockSpec((None, ADA_ROWS, ADA_COLS), lambda i: (i, 0, 0)),
                  pl.BlockSpec((None, ADA_ROWS, 6 * D), lambda i: (i, 0, 0)), pl.BlockSpec((None, D, ADA_COLS), lambda i: (i, 0, 0))],
        out_specs=[pl.BlockSpec((None, D, ADA_COLS), lambda i: (i, 0, 0)), pl.BlockSpec((None, 1, 6 * D), lambda i: (i, 0, 0)),
                   pl.BlockSpec((ADA_ROWS - ADA_CTX_ROW, D), lambda i: (0, 0))],
        out_shape=[jax.ShapeDtypeStruct((2, D, ADA_COLS), F32), jax.ShapeDtypeStruct((2, 1, 6 * D), F32),
                   jax.ShapeDtypeStruct((ADA_ROWS - ADA_CTX_ROW, D), F32)],
        compiler_params=pltpu.CompilerParams(dimension_semantics=("arbitrary",)),
    )(cond, dm_mine, dm_full, w)


def cctx_grad(parts, c_ctx):
    def body(p_ref, c_ref, o_ref):
        tot = p_ref[0:1, :]
        for i in range(1, N_DEV):
            tot = tot + p_ref[i:i + 1, :]
        c = c_ref[...]
        sg = jax.nn.sigmoid(c)
        o_ref[...] = tot * sg * (1.0 + c * (1.0 - sg))

    return pl.pallas_call(body, name="cctx_grad", out_shape=jax.ShapeDtypeStruct((1, D), F32))(parts, c_ctx)


def _row_tile(r):
    for t in (512, 256, 128, 80, 64, 40, 32, 16, 8):
        if r % t == 0:
            return t
    return r


def _slot_sum(ref):
    tot = ref[0].astype(F32)
    for i in range(1, ref.shape[0]):
        tot = tot + ref[i].astype(F32)
    return tot


def sum_slots(name, x):
    s, r, c = x.shape
    tr = _row_tile(r)

    def body(x_ref, o_ref):
        o_ref[...] = _slot_sum(x_ref)

    return pl.pallas_call(
        body, name=name, grid=(r // tr,), in_specs=[pl.BlockSpec((s, tr, c), lambda i: (0, i, 0))],
        out_specs=pl.BlockSpec((tr, c), lambda i: (i, 0)), out_shape=jax.ShapeDtypeStruct((r, c), F32),
    )(x)


def _adamw_update(gv, w_ref, m_ref, v_ref, go_ref, d_ref, mo_ref, vo_ref):
    mn = B1 * m_ref[...] + (1.0 - B1) * gv
    vn = B2 * v_ref[...] + (1.0 - B2) * jnp.square(gv)
    m_hat = mn / (1.0 - B1 ** STEP)
    v_hat = vn / (1.0 - B2 ** STEP)
    go_ref[...] = gv
    d_ref[...] = -LR * (m_hat / (jnp.sqrt(v_hat) + AEPS) + WD * w_ref[...])
    mo_ref[...] = mn
    vo_ref[...] = vn


def adamw_slots(name, w, land, sent, me1, m, v, layer, into=None):
    r, c = w.shape[-2:]
    tr = _row_tile(r)
    into = [] if into is None else list(into)

    def body(me_ref, w_ref, land_ref, own_ref, m_ref, v_ref, *rest):
        own = own_ref[...].astype(F32)
        gv = jnp.where(me_ref[0] == 0, own, land_ref[0].astype(F32))
        for s in range(1, N_DEV):
            gv = gv + jnp.where(me_ref[0] == s, own, land_ref[s].astype(F32))
        _adamw_update(gv, w_ref, m_ref, v_ref, *rest[len(into):])

    slab = pl.BlockSpec((None, tr, c), lambda i, me: (layer, i, 0))
    return pl.pallas_call(
        body, name=name, out_shape=[jax.ShapeDtypeStruct(w.shape, F32)] * 4,
        grid_spec=pltpu.PrefetchScalarGridSpec(
            num_scalar_prefetch=1, grid=(r // tr,),
            in_specs=[slab, pl.BlockSpec((N_DEV, tr, c), lambda i, me: (0, i, 0)),
                      pl.BlockSpec((None, tr, c), lambda i, me: (me[0], i, 0)), slab, slab]
            + [pl.BlockSpec(memory_space=pl.ANY)] * len(into),
            out_specs=[slab] * 4),
        input_output_aliases={6 + k: k for k in range(len(into))},
    )(me1, w, land, sent, m, v, *into)


def adamw(name, w, g, m, v, layer=None):
    r, c = w.shape[-2:]
    tr = _row_tile(r)
    stacked = g.ndim == 3

    def body(w_ref, g_ref, m_ref, v_ref, *outs):
        _adamw_update(_slot_sum(g_ref) if stacked else g_ref[...], w_ref, m_ref, v_ref, *outs)

    tile = pl.BlockSpec((tr, c), lambda i: (i, 0))
    slab = tile if layer is None else pl.BlockSpec((None, tr, c), lambda i: (layer, i, 0))
    g_spec = pl.BlockSpec((g.shape[0], tr, c), lambda i: (0, i, 0)) if stacked else tile
    return pl.pallas_call(
        body, name=name, grid=(r // tr,), in_specs=[slab, g_spec, slab, slab], out_specs=[tile] * 4,
        out_shape=[jax.ShapeDtypeStruct((r, c), F32)] * 4,
    )(w, g, m, v)


def _place():
    return lax.axis_index("x"), lax.axis_index("y"), lax.axis_index("c")


def all_gather(name, x):
    r, c = x.shape
    space = pltpu.VMEM

    def body(x_ref, out_ref, send_sems, recv_sems, local_sem):
        px, py, pc = _place()
        me, sibling = (px, py, pc), (px, py, 1 - pc)
        chips = [(1 - px, py), (px, 1 - py), (1 - px, 1 - py)]

        def rows(qx, qy, qc):
            return out_ref.at[pl.ds((4 * qx + 2 * qy + qc) * r, r), :]

        def copy(k, block, to, src=None):
            return pltpu.make_async_remote_copy(
                src_ref=rows(*block) if src is None else src, dst_ref=rows(*block),
                send_sem=send_sems.at[k], recv_sem=recv_sems.at[k], device_id=to, device_id_type=MESH)

        mine = pltpu.make_async_copy(x_ref, rows(*me), local_sem)
        mine.start()
        first = [copy(0, me, sibling, src=x_ref)]
        first += [copy(1 + j, me, (*chip, pc), src=x_ref) for j, chip in enumerate(chips)]
        for cp in first:
            cp.start()
        passed = [copy(4 + j, (*chip, pc), sibling) for j, chip in enumerate(chips)]
        for j, chip in enumerate(chips):
            copy(1 + j, (*chip, pc), me).wait_recv()
            passed[j].start()
        copy(0, sibling, me).wait_recv()
        for j, chip in enumerate(chips):
            copy(4 + j, (*chip, 1 - pc), me).wait_recv()
        for cp in first + passed:
            cp.wait_send()
        mine.wait()

    return pl.pallas_call(
        body, name=name, out_shape=jax.ShapeDtypeStruct((N_DEV * r, c), x.dtype),
        in_specs=[pl.BlockSpec(memory_space=space)], out_specs=pl.BlockSpec(memory_space=space),
        scratch_shapes=[pltpu.SemaphoreType.DMA((7,)), pltpu.SemaphoreType.DMA((7,)), pltpu.SemaphoreType.DMA],
    )(x)


_HBM = pl.BlockSpec(memory_space=pltpu.HBM)
_SEM = pl.BlockSpec(memory_space=pltpu.SEMAPHORE)
_EFFECT = pltpu.SideEffectType.DATAFLOW_SIDE_EFFECTING


def _peers():
    px, py, pc = _place()
    return [(1 - px if k & 4 else px, 1 - py if k & 2 else py, 1 - pc if k & 1 else pc) for k in range(1, N_DEV)]


def _slot(dev):
    return 4 * dev[0] + 2 * dev[1] + dev[2]


def _split_copies(src_refs, land_refs, send_sems, recv_sems, gather):
    me = _slot(_place())
    return [pltpu.make_async_remote_copy(
        src_ref=src if gather else src.at[_slot(peer)], dst_ref=land.at[me],
        send_sem=send_sems.at[a * (N_DEV - 1) + k], recv_sem=recv_sems.at[a * (N_DEV - 1) + k],
        device_id=peer, device_id_type=MESH)
        for a, (src, land) in enumerate(zip(src_refs, land_refs)) for k, peer in enumerate(_peers())]


def exchange_start(name, srcs, gather, after):
    n = len(srcs)
    lands = [pltpu.HBM((N_DEV,) + s.shape if gather else s.shape, s.dtype) for s in srcs]

    def body(*refs):
        send_sems, recv_sems = refs[2 * n + 1:2 * n + 3]
        for cp in _split_copies(refs[:n], refs[n:2 * n], send_sems, recv_sems, gather):
            cp.start()
        refs[-1][...] = jnp.zeros_like(refs[-1])

    sems = pltpu.SemaphoreType.DMA((n * (N_DEV - 1),))
    res = pl.pallas_call(
        body, name=name,
        out_shape=(sems, sems, *[pltpu.HBM(s.shape, s.dtype) for s in srcs], *lands, jax.ShapeDtypeStruct((8, 128), F32)),
        in_specs=(_HBM,) * (2 * n) + (pl.BlockSpec(memory_space=pl.ANY),),
        out_specs=(_SEM, _SEM) + (_HBM,) * (2 * n) + (pl.BlockSpec(memory_space=pltpu.VMEM),),
        input_output_aliases={i: 2 + i for i in range(2 * n)},
        compiler_params=pltpu.CompilerParams(has_side_effects=_EFFECT),
    )(*[pltpu.with_memory_space_constraint(s, pltpu.HBM) for s in srcs],
      *[pltpu.with_memory_space_constraint(lax.empty(ld.shape, ld.dtype), pltpu.HBM) for ld in lands], after)
    return res[0], res[1], list(res[2:2 + n]), list(res[2 + n:2 + 2 * n]), res[-1]


def exchange_wait(name, started, after, gather):
    send_sems, recv_sems, srcs, lands, _ = started
    n = len(srcs)
    after = list(after) if isinstance(after, (list, tuple)) else [after]

    def body(*refs):
        send_sems, recv_sems = refs[2 * n:2 * n + 2]
        for cp in _split_copies(refs[:n], refs[n:2 * n], send_sems, recv_sems, gather):
            cp.wait_send()
            cp.wait_recv()

    res = pl.pallas_call(
        body, name=name, out_shape=tuple(pltpu.HBM(a.shape, a.dtype) for a in srcs + lands),
        in_specs=(_HBM,) * (2 * n) + (_SEM, _SEM) + (pl.BlockSpec(memory_space=pl.ANY),) * len(after),
        out_specs=(_HBM,) * (2 * n), input_output_aliases={i: i for i in range(2 * n)},
        compiler_params=pltpu.CompilerParams(has_side_effects=_EFFECT),
    )(*srcs, *lands, send_sems, recv_sems, *after)
    return list(res[:n]), list(res[n:])


NCF = FFN_H // FFN_TC


def _size(shape):
    n = 1
    for s in shape:
        n *= s
    return n


def _padded_rows(n_elems, row_mult):
    return -(-n_elems // (D * row_mult)) * row_mult


def _pack_rows(arrs, dtype, row_mult):
    rows, offs, r0 = [], [], 0
    for a in arrs:
        flat = a.reshape(-1).astype(dtype)
        n = _padded_rows(flat.shape[0], row_mult)
        rows.append(jnp.pad(flat, (0, n * D - flat.shape[0])).reshape(n, D))
        offs.append(r0)
        r0 += n
    return jnp.concatenate(rows, 0), offs


def _unpack_rows(buf, offs, shapes):
    lead, out = buf.shape[:-2], []
    for o, shp in zip(offs, shapes):
        n = _size(shp)
        nr = -(-n // D)
        out.append(buf[..., o:o + nr, :].reshape(lead + (nr * D,))[..., :n].reshape(lead + tuple(shp)))
    return out


def _rows3(w):
    return [w[i:i + 1] for i in range(3)]


def f_mod1(xs, ps):
    return f_mod(xs, ps)[:1]


def kernel(x, c, ctx, c_ctx, ada_w, ada_b, norm_mix, norm_ffn, gla_w_in, gla_w_a2, gla_b_a, gla_head_norm, gla_w_out, sc_w_in, sc_conv_w, sc_w_out, ffn_w_up, ffn_conv_w, ffn_conv_b, ffn_w_down, final_norm, loss_target, m_c_ctx, m_ada_w, m_ada_b, m_norm_mix, m_norm_ffn, m_gla_w_in, m_gla_w_a2, m_gla_b_a, m_gla_head_norm, m_gla_w_out, m_sc_w_in, m_sc_conv_w, m_sc_w_out, m_ffn_w_up, m_ffn_conv_w, m_ffn_conv_b, m_ffn_w_down, m_final_norm, v_c_ctx, v_ada_w, v_ada_b, v_norm_mix, v_norm_ffn, v_gla_w_in, v_gla_w_a2, v_gla_b_a, v_gla_head_norm, v_gla_w_out, v_sc_w_in, v_sc_conv_w, v_sc_w_out, v_ffn_w_up, v_ffn_conv_w, v_ffn_conv_b, v_ffn_w_down, v_final_norm):
    names = ["c_ctx", "ada_w", "ada_b", "norm_mix", "norm_ffn", "gla_w_in", "gla_w_a2", "gla_b_a", "gla_head_norm",
             "gla_w_out", "sc_w_in", "sc_conv_w", "sc_w_out", "ffn_w_up", "ffn_conv_w", "ffn_conv_b", "ffn_w_down",
             "final_norm"]
    w_ = dict(zip(names, [c_ctx, ada_w, ada_b, norm_mix, norm_ffn, gla_w_in, gla_w_a2, gla_b_a, gla_head_norm, gla_w_out,
                          sc_w_in, sc_conv_w, sc_w_out, ffn_w_up, ffn_conv_w, ffn_conv_b, ffn_w_down, final_norm]))
    m_ = dict(zip(names, [m_c_ctx, m_ada_w, m_ada_b, m_norm_mix, m_norm_ffn, m_gla_w_in, m_gla_w_a2, m_gla_b_a,
                          m_gla_head_norm, m_gla_w_out, m_sc_w_in, m_sc_conv_w, m_sc_w_out, m_ffn_w_up, m_ffn_conv_w,
                          m_ffn_conv_b, m_ffn_w_down, m_final_norm]))
    v_ = dict(zip(names, [v_c_ctx, v_ada_w, v_ada_b, v_norm_mix, v_norm_ffn, v_gla_w_in, v_gla_w_a2, v_gla_b_a,
                          v_gla_head_norm, v_gla_w_out, v_sc_w_in, v_sc_conv_w, v_sc_w_out, v_ffn_w_up, v_ffn_conv_w,
                          v_ffn_conv_b, v_ffn_w_down, v_final_norm]))
    me = 4 * lax.axis_index("x") + 2 * lax.axis_index("y") + lax.axis_index("c")
    bsz = x.shape[0]
    tm = 256
    nt = SEQ // tm
    ctx_tiles = CTX // tm
    pe = functools.partial(P, per_example=True)

    groups = {"ffn1": [("ffn_w_up", 1), ("ffn_w_down", 1)], "sc": [("sc_w_in", 0), ("sc_w_out", 0)],
              "ffn0": [("ffn_w_up", 0), ("ffn_w_down", 0)], "gla": [("gla_w_in", 0), ("gla_w_out", 0)]}
    ag_groups = {"gin": [("gla_w_in", 0)], "ffn0": [("gla_w_out", 0), ("ffn_w_up", 0), ("ffn_w_down", 0)],
                 "sc": groups["sc"], "ffn1": groups["ffn1"]}
    ag_started = {}

    def start_gather(g, after):
        ag_started[g] = exchange_start(f"ag_{g}_start", [w_[n][i].astype(BF16) for n, i in ag_groups[g]], True, after)
        return ag_started[g][4]

    small_sharded = [c, gla_w_a2, gla_b_a, sc_conv_w, ffn_conv_w]
    pack0, offs0 = _pack_rows(small_sharded, F32, 8)
    g0 = all_gather("ag_small", pack0).reshape(N_DEV, pack0.shape[0], D)
    c_all, wa2_s, ba_s, scw_s, fcw_s = _unpack_rows(g0, offs0, [a.shape for a in small_sharded])
    w_a2 = wa2_s[:, 0].transpose(1, 2, 0, 3).reshape(2, RANK, KD)
    b_a = ba_s[:, 0].transpose(1, 0, 2).reshape(2, KD)
    sc_cw = scw_s[:, 0].transpose(1, 0, 2).reshape(3, D)
    ffn_cw = fcw_s.transpose(1, 2, 0, 3).reshape(2, 3, 2 * FFN_H)

    cond = jnp.concatenate([c_all.reshape(N_DEV * bsz, D), c_ctx[None], jnp.zeros((ADA_ROWS - N_DEV * bsz - 1, D), F32)], 0)
    b_mine = lax.dynamic_slice(ada_b, (0, me * ADA_COLS), (2, ADA_COLS)).reshape(2, 1, ADA_COLS)
    mod_part = ada_fwd(cond, ada_w, b_mine)
    mod = all_gather("ag_mod", mod_part.reshape(2 * ADA_ROWS, ADA_COLS))
    mod = mod.reshape(N_DEV, 2, ADA_ROWS, ADA_COLS).transpose(1, 2, 0, 3).reshape(2, ADA_ROWS, 6 * D)
    mods = lax.dynamic_slice(mod, (0, bsz * me, 0), (2, bsz, 6 * D))
    md = [[mods[i][:, k * D:(k + 1) * D].reshape(bsz, 1, D) for k in range(6)] for i in range(2)]
    mc = [mod[0, ADA_CTX_ROW, k * D:(k + 1) * D][None] for k in range(2)]

    tok = mod
    for g in ag_groups:
        tok = start_gather(g, tok)
    norm_mix = norm_mix + tok[0, 0]

    def gathered(g, after):
        mine, lands = exchange_wait(f"ag_{g}_wait", ag_started[g], after, True)
        return [lax.dynamic_update_index_in_dim(ld, mn, me, 0) for ld, mn in zip(lands, mine)]

    s_up, w_down = [None, None], [None, None]
    wd = jnp.zeros((128, 2 * KD), F32).at[:RANK, :KD].set(w_a2[0]).at[RANK:2 * RANK, KD:].set(w_a2[1])
    bd = b_a.reshape(1, 2 * KD)
    scw = _rows3(sc_cw)
    head_gain = gla_head_norm.reshape(1, HV)
    gains_mix = [norm_mix[i][None] for i in range(2)]
    gains_ffn = [norm_ffn[i][None] for i in range(2)]

    def tokens(a2d, t_len):
        return a2d.reshape(bsz, t_len, -1)

    def ffn_params(i):
        rows = [ffn_cw[i][t] for t in range(3)] + [ffn_conv_b[i]]
        return [P(a.reshape(2, FFN_H), w=FFN_TC, rows=True) for a in rows]

    def ffn_fwd(i, hn2):
        u = mm(f"ffn_up{i}", V(hn2, "tok"), V(s_up[i], "cols"), out="planes", out_dtype=BF16, planes_t=SEQ)
        act = rowwise(f"ffn_mid{i}", f_ffn_mid, [X(u, w=FFN_TC, planes=True)], ffn_params(i), tm=SEQ, nt=1, nc=NCF,
                      outs=[(FFN_TC, BF16, 1)])[0]
        return u, act

    def arrays(ps):
        return [p["a"] for p in ps]

    ps_in0 = [P(gains_mix[0]), pe(md[0][0]), pe(md[0][1])]
    ps_ctx = [P(gains_mix[0]), P(mc[0]), P(mc[1])]
    hn0 = rowwise("mod_in0", f_mod, [X(x)], ps_in0, tm=tm, nt=nt, outs=[(D, BF16, 1)])[0]
    hnc = rowwise("mod_ctx", f_mod, [X(ctx)], ps_ctx, tm=tm, nt=ctx_tiles, outs=[(D, BF16, 1)])[0]
    hcat = jnp.concatenate([hnc, hn0], axis=1)
    (s_gin,) = gathered("gin", hcat)
    w_gin = V(s_gin, "cols", width=GLA_IN_PAD)
    pcat = tokens(mm("gla_in", V(hcat, "tok"), w_gin, out_dtype=BF16), TT)
    pa_x = X(pcat, w=128, co=(GLA_IN_PAD - 128) // 128)
    la = rowwise("gla_decay", f_decay, [pa_x], [P(wd), P(bd)], tm=tm, nt=TT // tm, outs=[(2 * KD, F32, 1)])[0]
    o2, s_all = gla_fwd(pcat, la)
    post_xs = [X(o2, w=VD, co=0, ro=ctx_tiles, split=HEADS), X(o2, w=VD, co=1, ro=ctx_tiles, split=HEADS),
               X(pcat, w=VD, co=2, ro=ctx_tiles, split=HEADS)]
    yin0 = rowwise("gla_post", f_gla_post, post_xs, [P(head_gain)], tm=tm, nt=nt, outs=[(VD, BF16, HEADS)])[0]
    s_gout, s_up[0], s_down0 = gathered("ffn0", yin0)
    w_gout, w_down[0] = s_gout.reshape(VD, D), s_down0.reshape(FFN_H, D)
    ps_mid0 = [pe(md[0][2]), P(gains_ffn[0]), pe(md[0][3]), pe(md[0][4])]
    y0, h1_0, hn2_0 = mm_res_mod("gla_out", yin0, w_gout, x, *arrays(ps_mid0))
    u0, act0 = ffn_fwd(0, hn2_0)
    ps_in1 = [pe(md[0][5]), P(gains_mix[1]), pe(md[1][0]), pe(md[1][1])]
    fo0, h2_0, hn1 = mm_res_mod("ffn_down0", act0, w_down[0], h1_0, *arrays(ps_in1))

    s_sin, s_sout = gathered("sc", hn1)
    w_sout = s_sout.reshape(D, D)
    p1 = tokens(mm("sc_in", V(hn1, "tok"), V(s_sin, "cols")), SEQ)
    sc_ps = [P(a) for a in scw]
    yin1 = rowwise("sc_mid", f_sc_mid, [X(p1, split=3)], sc_ps, tm=tm, nt=nt, outs=[(D, BF16, 1)])[0]
    ps_mid1 = [pe(md[1][2]), P(gains_ffn[1]), pe(md[1][3]), pe(md[1][4])]
    y1, h1_1, hn2_1 = mm_res_mod("sc_out", yin1, w_sout, h2_0, *arrays(ps_mid1))
    s_up[1], s_down1 = gathered("ffn1", hn2_1)
    w_down[1] = s_down1.reshape(FFN_H, D)
    u1, act1 = ffn_fwd(1, hn2_1)
    fo1 = tokens(mm("ffn_down1", V(act1, "tok"), V(w_down[1])), SEQ)
    loss8, dh1_1, dfo1, dm5_1, g_final = final_loss(h1_1, fo1, md[1][5], final_norm[None], loss_target)

    def ffn_bwd(i, u, act, hn2, dfo):
        dact = tokens(mm(f"ffn_down_dx{i}", V(dfo, "tok"), V(w_down[i]), form="nt", out_dtype=BF16), SEQ)
        g_down = mm(f"ffn_down_dw{i}", V(act, "tok"), V(dfo, "tok"), form="tn", out_dtype=BF16)
        r = rowwise(f"ffn_mid_bwd{i}", f_ffn_mid, [X(u, w=FFN_TC, planes=True)], ffn_params(i), tm=SEQ, nt=1, nc=NCF,
                    douts=[X(dact, w=FFN_TC)], dx={0: BF16}, dp=[0, 1, 2, 3])
        du, g_cw, g_cb = r[0], jnp.stack([a.reshape(2 * FFN_H) for a in r[1:4]]), r[4].reshape(1, 2 * FFN_H)
        dhn2 = tokens(mm(f"ffn_up_dx{i}", V(du, "planes"), V(s_up[i], "cols"), form="nt", out_dtype=BF16), SEQ)
        g_up = mm(f"ffn_up_dw{i}", V(hn2, "tok"), V(du, "planes"), form="tn", out="cols", out_dtype=BF16)
        return dhn2, g_up, row_slots(g_down), g_cw, g_cb

    def res_mod_bwd(name, h, y, ps, dh1, dhn):
        return rowwise(name, f_res_mod, [X(h), X(y)], ps, tm=tm, nt=nt, douts=[X(dh1), X(dhn)],
                       dx={0: F32, 1: BF16}, dp=[0, 1, 2, 3])

    def row_slots(g):
        return g.reshape(N_DEV, -1, g.shape[-1])

    a2a_started = {}

    def send_grads(g, slots, after=None):
        a2a_started[g] = exchange_start(f"a2a_{g}_start", list(slots), False, loss8 if after is None else after)
        return a2a_started[g][4][0, 0]

    def after_start(ps, tok):
        return [dict(ps[0], a=ps[0]["a"] + tok)] + ps[1:]

    dhn2_1, g_up1, g_down1, g_fcw1, g_fcb1 = ffn_bwd(1, u1, act1, hn2_1, dfo1)
    tok = send_grads("ffn1", [g_up1, g_down1])
    dh2_0, dy1, dm2_1, g_nffn1, dm3_1, dm4_1 = res_mod_bwd("res_mod_mid1_bwd", h2_0, y1, after_start(ps_mid1, tok), dh1_1, dhn2_1)
    dyin1 = tokens(mm("sc_out_dx", V(dy1, "tok"), V(w_sout), form="nt", out_dtype=BF16), SEQ)
    g_sout = row_slots(mm("sc_out_dw", V(yin1, "tok"), V(dy1, "tok"), form="tn", out_dtype=BF16))
    r = rowwise("sc_mid_bwd", f_sc_mid, [X(p1, split=3)], sc_ps, tm=tm, nt=nt, douts=[X(dyin1)], dx={0: BF16}, dp=[0, 1, 2])
    dp1, g_scw = r[0], jnp.concatenate(r[1:4], 0)
    dhn1 = tokens(mm("sc_in_dx", V(dp1, "tok"), V(s_sin, "cols"), form="nt", out_dtype=BF16), SEQ)
    g_sin = mm("sc_in_dw", V(hn1, "tok"), V(dp1, "tok"), form="tn", out="cols", out_dtype=BF16)
    tok = send_grads("sc", [g_sin, g_sout])
    dh1_0, dfo0, dm5_0, g_nmix1, dm0_1, dm1_1 = res_mod_bwd("res_mod_in1_bwd", h1_0, fo0, after_start(ps_in1, tok), dh2_0, dhn1)

    dhn2_0, g_up0, g_down0, g_fcw0, g_fcb0 = ffn_bwd(0, u0, act0, hn2_0, dfo0)
    tok = send_grads("ffn0", [g_up0, g_down0])
    dx_res, dy0, dm2_0, g_nffn0, dm3_0, dm4_0 = res_mod_bwd("res_mod_mid0_bwd", x, y0, after_start(ps_mid0, tok), dh1_0, dhn2_0)
    dyin0 = tokens(mm("gla_out_dx", V(dy0, "tok"), V(w_gout), form="nt", out_dtype=BF16), SEQ)
    do, dgate, g_head = rowwise("gla_post_bwd", f_gla_post, post_xs, [P(head_gain)], tm=tm, nt=nt,
                                douts=[X(dyin0, split=HEADS)], dx={0: BF16, 2: BF16}, dp=[0])
    dq2, dk2, dv2, dla = gla_bwd(pcat, la, s_all, do)
    dpa, g_wd, g_bd = rowwise("gla_decay_bwd", f_decay, [pa_x], [P(wd), P(bd)], tm=tm, nt=TT // tm, douts=[X(dla)],
                              dx={0: BF16}, dp=[0, 1])
    dpcat = gla_combine(dq2, dk2, dv2, dgate, dpa)
    dhcat = tokens(mm("gla_in_dx", V(dpcat, "tok"), w_gin, form="nt", out_dtype=BF16), TT)
    grad_x, g_nmix0, dm0_0, dm1_0 = rowwise("mod_in0_bwd", f_mod, [X(x)], ps_in0, tm=tm, nt=nt,
                                            douts=[X(dhcat, ro=ctx_tiles), X(dx_res)], dx={0: F32}, dp=[0, 1, 2])
    g_nmix0c, dmc0, dmc1 = rowwise("mod_ctx_bwd", f_mod1, [X(ctx)], ps_ctx, tm=tm, nt=ctx_tiles, douts=[X(dhcat)],
                                   dx={}, dp=[0, 1, 2])

    zero_row = jnp.zeros((1, 4 * D), F32)
    dmod = [jnp.concatenate([jnp.concatenate([a.reshape(bsz, D) for a in dms], 1), ctx_row], 0)
            for dms, ctx_row in (([dm0_0, dm1_0, dm2_0, dm3_0, dm4_0, dm5_0], jnp.concatenate([dmc0, dmc1, zero_row], 1)),
                                 ([dm0_1, dm1_1, dm2_1, dm3_1, dm4_1, dm5_1], jnp.zeros((1, 6 * D), F32)))]
    g_wa2 = jnp.stack([g_wd[:RANK, :KD], g_wd[RANK:2 * RANK, KD:]])
    small_grads = [jnp.stack(dmod), jnp.concatenate([g_nmix0 + g_nmix0c, g_nmix1], 0), jnp.concatenate([g_nffn0, g_nffn1], 0),
                   g_head, jnp.concatenate([g_fcb0, g_fcb1], 0), g_final, g_wa2, g_bd.reshape(2, KD), g_scw,
                   jnp.stack([g_fcw0, g_fcw1]), loss8[:1]]
    pack1, offs1 = _pack_rows(small_grads, F32, 8)
    ag1 = exchange_start("ag_grads_start", [pack1], True, loss8)
    g_gin = mm("gla_in_dw", V(hcat, "tok"), V(dpcat, "tok"), form="tn", out="cols", out_dtype=BF16, shard_n=GLA_IN // N_DEV,
               after=ag1[4])
    g_gout = row_slots(mm("gla_out_dw", V(yin0, "tok"), V(dy0, "tok"), form="tn", out_dtype=BF16, after=ag1[4]))
    mine1, land1 = exchange_wait("ag_grads_wait", ag1, [g_gin, g_gout], True)
    g1 = lax.dynamic_update_index_in_dim(land1[0], mine1[0], me, 0)
    dmod_all = _unpack_rows(g1, offs1[:1], [small_grads[0].shape])[0]
    tot = _unpack_rows(sum_slots("sum_small", g1), offs1, [a.shape for a in small_grads])
    loss = tot[10][0, 0]
    dm_rows = dmod_all[:, :, :bsz].transpose(1, 0, 2, 3).reshape(2, N_DEV * bsz, 6 * D)
    dm_full = jnp.concatenate([dm_rows, tot[0][:, bsz:], jnp.zeros((2, ADA_ROWS - N_DEV * bsz - 1, 6 * D), F32)], 1)
    dm_mine = lax.dynamic_slice(dm_full, (0, 0, me * ADA_COLS), (2, ADA_ROWS, ADA_COLS))
    g_ada_w, g_ada_b, cpart = ada_bwd(cond, dm_mine, dm_full, ada_w)
    cparts = all_gather("ag_cctx", cpart).reshape(N_DEV, ADA_ROWS - ADA_CTX_ROW, D)[:, 0]
    g_cctx = cctx_grad(cparts, c_ctx[None])[0]
    tok = send_grads("gla", [g_gin, g_gout], after=g_cctx)

    def my_cols(full, n):
        return lax.dynamic_slice_in_dim(full, me * n, n, axis=full.ndim - 1)

    grads = {
        "c_ctx": g_cctx, "ada_b": g_ada_b.reshape(2, 6 * D), "norm_mix": tot[1], "norm_ffn": tot[2],
        "gla_head_norm": tot[3], "ffn_conv_b": tot[4], "final_norm": tot[5].reshape(D),
        "gla_w_a2": my_cols(tot[6], KD // N_DEV)[None], "gla_b_a": my_cols(tot[7], KD // N_DEV)[None],
        "sc_conv_w": my_cols(tot[8], D // N_DEV)[None], "ffn_conv_w": my_cols(tot[9], 2 * FFN_H // N_DEV),
    }

    res_ada = adamw("adamw_ada", *[a.reshape(2 * D, ADA_COLS) for a in (ada_w, g_ada_w, m_ada_w, v_ada_w)])
    grads["c_ctx"] = g_cctx + tok
    big = ["gla_w_in", "gla_w_out", "sc_w_in", "sc_w_out", "ffn_w_up", "ffn_w_down"]
    small = [n for n in names if n not in big and n != "ada_w"]
    g_small = _pack_rows([grads[n] for n in small], F32, 8)[0]
    res_small = adamw("adamw_small", _pack_rows([w_[n] for n in small], F32, 8)[0], g_small,
                      _pack_rows([m_[n] for n in small], F32, 8)[0], _pack_rows([v_[n] for n in small], F32, 8)[0])
    offs_s = _pack_rows([w_[n] for n in small], F32, 8)[1]

    big_res, done, me1 = {}, {"small": res_small[0], "ada_w": res_ada[0]}, jnp.reshape(me, (1,)).astype(jnp.int32)
    for g in groups:
        sent, lands = exchange_wait(f"a2a_{g}_wait", a2a_started[g], list(done.values()), False)
        for (n, i), mine, land in zip(groups[g], sent, lands):
            big_res[n] = adamw_slots(f"adamw_{n}{i}", w_[n], land, mine, me1, m_[n], v_[n], i, into=big_res.get(n))
            done[n] = big_res[n][0]

    out = {}
    for kind, idx in (("grad", 0), ("delta", 1), ("new_m", 2), ("new_v", 3)):
        vals = {n: big_res[n][idx] for n in big}
        vals["ada_w"] = res_ada[idx].reshape(ada_w.shape)
        vals.update(zip(small, _unpack_rows(res_small[idx], offs_s, [w_[n].shape for n in small])))
        out[kind] = [vals[n] for n in names]
    return (loss, grad_x, *out["grad"], *out["delta"], *out["new_m"], *out["new_v"])
```

```python
import functools

import jax
import jax.numpy as jnp
from jax import lax
from jax.experimental import pallas as pl
from jax.experimental.pallas import tpu as pltpu

F32 = jnp.float32
BF16 = jnp.bfloat16

N_DEV = 8
D = 1024
SEQ = 2048
CTX = 256
TT = CTX + SEQ
GRID_W = 64
CHUNK = 64
HEADS = 4
HK = 128
HV = 256
KD = 512
VD = 1024
RANK = 16
TAU = 16.0
GLA_IN = 3104
GLA_IN_PAD = 3200
FFN_H = 2560
FFN_TC = 256
EPS = 1e-6
LR, B1, B2, AEPS, WD, STEP = 0.001, 0.9, 0.999, 1e-08, 0.01, 10
MESH = pl.DeviceIdType.MESH


def _blocks(n):
    return [n] + [t for t in range(n - n % 128, 0, -128) if n % t == 0 and t != n]


def V(arr, kind="flat", width=None):
    if kind == "tok":
        return V(arr.reshape(-1, arr.shape[-1]))
    if kind == "flat":
        r, c = arr.shape
        return dict(a=arr, kind=kind, shape=(r, c), rows=_blocks(r), cols=_blocks(c))
    if kind == "planes":
        bsz, _, t, ch = arr.shape
        return dict(a=arr, kind=kind, shape=(bsz * t, 2 * ch), rows=_blocks(t), cols=[2 * ch] + _blocks(ch), t=t, ch=ch)
    _, r, n = arr.shape
    if width is not None:
        return dict(a=arr, kind=kind, shape=(r, width), rows=_blocks(r), cols=[width], n=n, pad=width - N_DEV * n)
    return dict(a=arr, kind=kind, shape=(r, N_DEV * n), rows=_blocks(r), cols=[8 * n, 4 * n, 2 * n], n=n, pad=0)


def _view_spec(v, br, bc, idx):
    if v["kind"] == "flat":
        return pl.BlockSpec((br, bc), idx)
    if v["kind"] == "planes":
        nt = v["t"] // br
        if bc == 2 * v["ch"]:
            return pl.BlockSpec((None, 2, br, v["ch"]), lambda i, j, k: (idx(i, j, k)[0] // nt, 0, idx(i, j, k)[0] % nt, 0))
        nch = v["ch"] // bc

        def at(i, j, k):
            r, c = idx(i, j, k)
            return r // nt, c // nch, r % nt, c % nch
        return pl.BlockSpec((None, None, br, bc), at)
    return pl.BlockSpec(((bc - v["pad"]) // v["n"], br, v["n"]), lambda i, j, k: (idx(i, j, k)[1], idx(i, j, k)[0], 0))


def _out_view(kind, rows, cols, dtype, planes_t=None, shard_n=None):
    if kind == "flat":
        shape = (rows, cols)
    elif kind == "planes":
        shape = (rows // planes_t, 2, planes_t, cols // 2)
    elif shard_n is not None:
        return V(jax.ShapeDtypeStruct((N_DEV, rows, shard_n), dtype), kind, width=cols)
    else:
        shape = (N_DEV, rows, cols // N_DEV)
    return V(jax.ShapeDtypeStruct(shape, dtype), kind)


MM_VMEM_BUDGET = 40 * 2 ** 20
MM_VMEM_LIMIT = 56 * 2 ** 20
MM_MAX_TILE = 1536


def _mm_tiles(m, n, kk, ms, ns, ks, a_bytes, b_bytes, o_bytes):
    best = None
    for tk in ks:
        for tm in [t for t in ms if t <= MM_MAX_TILE] or ms:
            for tn in [t for t in ns if t <= MM_MAX_TILE] or ns:
                one_k = tk == kk
                need = 2 * (tm * tk * a_bytes + tk * tn * b_bytes + tm * tn * o_bytes) + (0 if one_k else tm * tn * 4)
                if need > MM_VMEM_BUDGET:
                    continue
                steps = (m // tm) * (n // tn) * (kk // tk)
                traffic = (m * kk * a_bytes * (1 if one_k else n // tn)
                           + kk * n * b_bytes * (1 if one_k and n == tn else m // tm) + m * n * o_bytes)
                fill = (tm * tk * a_bytes + tk * tn * b_bytes) / 2.5e12
                cost = max(2.0 * m * n * kk / (9e14 if one_k else 6.5e14), traffic / 2.5e12) + steps * 0.4e-6 + fill
                if best is None or cost < best[0]:
                    best = (cost, tm, tn, tk)
    return best[1:]


def mm(name, a, b, form="nn", out="flat", out_dtype=F32, planes_t=None, shard_n=None, after=None):
    (m, kk) = a["shape"][::-1] if form == "tn" else a["shape"]
    n = b["shape"][0] if form == "nt" else b["shape"][1]
    assert (b["shape"][1] if form == "nt" else b["shape"][0]) == kk, (name, a["shape"], b["shape"])
    o = _out_view(out, m, n, out_dtype, planes_t, shard_n)
    a_m, a_k = (a["cols"], a["rows"]) if form == "tn" else (a["rows"], a["cols"])
    b_k, b_n = (b["cols"], b["rows"]) if form == "nt" else (b["rows"], b["cols"])
    tm, tn, tk = _mm_tiles(m, n, kk, [t for t in a_m if t in o["rows"]], [t for t in b_n if t in o["cols"]],
                           [t for t in a_k if t in b_k], a["a"].dtype.itemsize, b["a"].dtype.itemsize,
                           jnp.dtype(out_dtype).itemsize)
    nk = kk // tk
    dn = (((0 if form == "tn" else 1,), (1 if form == "nt" else 0,)), ((), ()))

    def load(ref, v):
        if len(ref.shape) == 3:
            pieces = [ref[p].astype(BF16) for p in range(ref.shape[0])]
            if v.get("pad"):
                pieces.append(jnp.zeros(ref.shape[1:2] + (v["pad"],), BF16))
            return jnp.concatenate(pieces, axis=-1)
        return ref[...].astype(BF16)

    def store(o_ref, val):
        val = val.astype(out_dtype)
        if len(o_ref.shape) == 3:
            w = o_ref.shape[-1]
            for p in range(o_ref.shape[0]):
                o_ref[p] = val[:, p * w:(p + 1) * w]
        else:
            o_ref[...] = val

    def body(a_ref, b_ref, *rest):
        o_ref, acc = rest[0 if after is None else 1], rest[1 if after is None else 2:]
        if nk == 1:
            store(o_ref, lax.dot_general(load(a_ref, a), load(b_ref, b), dn, preferred_element_type=F32))
            return
        k, acc_ref = pl.program_id(2), acc[0]

        @pl.when(k == 0)
        def _():
            acc_ref[...] = jnp.zeros_like(acc_ref)

        acc_ref[...] += lax.dot_general(load(a_ref, a), load(b_ref, b), dn, preferred_element_type=F32)

        @pl.when(k == nk - 1)
        def _():
            store(o_ref, acc_ref[...])

    if form == "tn":
        a_spec = _view_spec(a, tk, tm, lambda i, j, k: (k, i))
    else:
        a_spec = _view_spec(a, tm, tk, lambda i, j, k: (i, k))
    if form == "nt":
        b_spec = _view_spec(b, tn, tk, lambda i, j, k: (j, k))
    else:
        b_spec = _view_spec(b, tk, tn, lambda i, j, k: (k, j))
    return pl.pallas_call(
        body, name=name, grid=(m // tm, n // tn, nk),
        in_specs=[a_spec, b_spec] + ([] if after is None else [pl.BlockSpec(memory_space=pl.ANY)]),
        out_specs=_view_spec(o, tm, tn, lambda i, j, k: (i, j)), out_shape=o["a"],
        scratch_shapes=[pltpu.VMEM((tm, tn), F32)] if nk > 1 else [],
        compiler_params=pltpu.CompilerParams(dimension_semantics=("parallel", "parallel", "arbitrary"),
                                             vmem_limit_bytes=MM_VMEM_LIMIT),
    )(a["a"], b["a"], *([] if after is None else [after]))


def mm_res_mod(name, a, w, h, gate, gain, shift, scale):
    bsz, t_len, kk = a.shape
    tm = 512
    per = t_len // tm

    def body(a_ref, w_ref, h_ref, gate_ref, gain_ref, shift_ref, scale_ref, y_ref, h1_ref, hn_ref):
        y = jnp.dot(a_ref[...].astype(BF16), w_ref[...].astype(BF16), preferred_element_type=F32)
        h1 = h_ref[...] + gate_ref[...] * y
        y_ref[...] = y.astype(BF16)
        h1_ref[...] = h1
        hn_ref[...] = _mod(h1, gain_ref[...], shift_ref[...], scale_ref[...]).astype(BF16)

    def tile(width):
        return pl.BlockSpec((None, tm, width), lambda i: (i // per, i % per, 0))

    per_ex = pl.BlockSpec((None, 1, D), lambda i: (i // per, 0, 0))
    return pl.pallas_call(
        body, name=name, grid=(bsz * per,),
        in_specs=[tile(kk), pl.BlockSpec((kk, D), lambda i: (0, 0)), tile(D), per_ex, pl.BlockSpec((1, D), lambda i: (0, 0)),
                  per_ex, per_ex],
        out_specs=[tile(D)] * 3,
        out_shape=[jax.ShapeDtypeStruct((bsz, t_len, D), BF16), jax.ShapeDtypeStruct((bsz, t_len, D), F32),
                   jax.ShapeDtypeStruct((bsz, t_len, D), BF16)],
        compiler_params=pltpu.CompilerParams(dimension_semantics=("parallel",), vmem_limit_bytes=MM_VMEM_LIMIT),
    )(a, w, h, gate, gain, shift, scale)


def X(arr, w=None, co=0, ro=0, split=1, planes=False):
    return dict(a=arr, w=arr.shape[-1] if w is None else w, co=co, ro=ro, split=2 if planes else split,
                mode="planes" if planes else "cols")


def P(arr, per_example=False, w=None, split=1, rows=False):
    return dict(a=arr, e=per_example, w=arr.shape[-1] if w is None else w, split=arr.shape[-2] if rows else split,
                mode="rows" if rows else "cols")


def _pieces(ref, s):
    if s["mode"] == "planes":
        return [ref[0], ref[1]]
    if s["mode"] == "rows":
        return [ref[i:i + 1, :] for i in range(s["split"])]
    w = ref.shape[-1] // s["split"]
    return [ref[:, i * w:(i + 1) * w] for i in range(s["split"])]


def _store(ref, pieces, s, accumulate=False):
    w = ref.shape[-1] // len(pieces)
    for i, p in enumerate(pieces):
        at = (i,) if s["mode"] == "planes" else (slice(i, i + 1),) if s["mode"] == "rows" else (slice(None), slice(i * w, (i + 1) * w))
        if accumulate:
            ref[at] += p.astype(ref.dtype)
        else:
            ref[at] = p.astype(ref.dtype)


def rowwise(name, f, xs, ps, *, tm, nt, nc=1, outs=None, douts=None, dx=None, dp=None):
    bsz = xs[0]["a"].shape[0]
    fwd = douts is None
    nx, np_ = len(xs), len(ps)
    douts = [] if fwd else douts
    dx = {} if fwd else dx
    dp = [] if fwd else dp

    def x_spec(s):
        if s["mode"] == "planes":
            return pl.BlockSpec((None, 2, tm, s["w"]), lambda c, b, t, s=s: (b, 0, t + s["ro"], c + s["co"]))
        return pl.BlockSpec((None, tm, s["w"]), lambda c, b, t, s=s: (b, t + s["ro"], c + s["co"]))

    def x_out(s, dt):
        if s["mode"] == "planes":
            return (jax.ShapeDtypeStruct((bsz, 2, nt * tm, nc * s["w"]), dt),
                    pl.BlockSpec((None, 2, tm, s["w"]), lambda c, b, t: (b, 0, t, c)))
        return (jax.ShapeDtypeStruct((bsz, nt * tm, nc * s["w"]), dt), pl.BlockSpec((None, tm, s["w"]), lambda c, b, t: (b, t, c)))

    def p_spec(s):
        r = s["a"].shape[-2]
        if s["e"]:
            return pl.BlockSpec((None, r, s["w"]), lambda c, b, t: (b, 0, c))
        return pl.BlockSpec((r, s["w"]), lambda c, b, t: (0, c))

    in_specs = [x_spec(s) for s in xs] + [p_spec(s) for s in ps] + [x_spec(s) for s in douts]
    operands = [s["a"] for s in xs] + [s["a"] for s in ps] + [s["a"] for s in douts]
    if fwd:
        out_modes = [dict(mode="cols", split=sp) for (_, _, sp) in outs]
        out_shape = [jax.ShapeDtypeStruct((bsz, nt * tm, nc * w), dt) for (w, dt, _) in outs]
        out_specs = [pl.BlockSpec((None, tm, w), lambda c, b, t: (b, t, c)) for (w, _, _) in outs]
    else:
        dx_outs = [x_out(xs[i], dt) for i, dt in dx.items()]
        out_shape, out_specs = [o[0] for o in dx_outs], [o[1] for o in dx_outs]
        for j in dp:
            s = ps[j]
            r = s["a"].shape[-2]
            if s["e"]:
                out_shape.append(jax.ShapeDtypeStruct((bsz, r, nc * s["w"]), F32))
                out_specs.append(pl.BlockSpec((None, r, s["w"]), lambda c, b, t: (b, 0, c)))
            else:
                out_shape.append(jax.ShapeDtypeStruct((r, nc * s["w"]), F32))
                out_specs.append(pl.BlockSpec((r, s["w"]), lambda c, b, t: (0, c)))

    def body(*refs):
        x_refs, p_refs = refs[:nx], refs[nx:nx + np_]
        d_refs = refs[nx + np_:nx + np_ + len(douts)]
        o_refs = refs[nx + np_ + len(douts):]
        xv = [[p.astype(F32) for p in _pieces(r, s)] for r, s in zip(x_refs, xs)]
        pv = [[p.astype(F32) for p in _pieces(r, s)] for r, s in zip(p_refs, ps)]
        if fwd:
            for r, pieces, s in zip(o_refs, f(xv, pv), out_modes):
                _store(r, pieces, s)
            return
        _, vjp = jax.vjp(f, xv, pv)
        cot = [[p.astype(F32) for p in _pieces(r, s)] for r, s in zip(d_refs, douts)]
        dxv, dpv = vjp(cot)
        for r, i in zip(o_refs, dx):
            _store(r, dxv[i], xs[i])
        b, t = pl.program_id(1), pl.program_id(2)
        for r, j in zip(o_refs[len(dx):], dp):
            first = (t == 0) if ps[j]["e"] else jnp.logical_and(b == 0, t == 0)

            @pl.when(first)
            def _(r=r, j=j):
                _store(r, dpv[j], ps[j])

            @pl.when(jnp.logical_not(first))
            def _(r=r, j=j):
                _store(r, dpv[j], ps[j], accumulate=True)

    res = pl.pallas_call(
        body, name=name, grid=(nc, bsz, nt), in_specs=in_specs, out_specs=out_specs, out_shape=out_shape,
        compiler_params=pltpu.CompilerParams(dimension_semantics=("arbitrary", "arbitrary", "arbitrary")),
    )(*operands)
    return res


def _keep_rows(a, shift, keep):
    n = a.shape[0]
    t = lax.broadcasted_iota(jnp.int32, a.shape, 0)
    return jnp.where(keep(t, n), pltpu.roll(a, shift % n, 0), 0.0)


def _shift_pair(step, keep_prev=None, keep_next=None):
    @jax.custom_vjp
    def prev(a):
        if keep_prev is None:
            return jnp.concatenate([jnp.zeros((step,) + a.shape[1:], a.dtype), a[:a.shape[0] - step]], axis=0)
        return _keep_rows(a, step, keep_prev)

    @jax.custom_vjp
    def nxt(a):
        if keep_next is None:
            return jnp.concatenate([a[step:], jnp.zeros((step,) + a.shape[1:], a.dtype)], axis=0)
        return _keep_rows(a, -step, keep_next)

    prev.defvjp(lambda a: (prev(a), None), lambda _, g: (nxt(g),))
    nxt.defvjp(lambda a: (nxt(a), None), lambda _, g: (prev(g),))
    return prev, nxt


prev_tok, next_tok = _shift_pair(1, lambda t, n: t % GRID_W != 0, lambda t, n: t % GRID_W != GRID_W - 1)
prev_row, next_row = _shift_pair(GRID_W)


@jax.custom_vjp
def bdot(a, w):
    return jnp.dot(a.astype(BF16), w.astype(BF16), preferred_element_type=F32)


def _bdot_bwd(res, g):
    a, w = res
    gb = g.astype(BF16)
    da = lax.dot_general(gb, w.astype(BF16), (((1,), (1,)), ((), ())), preferred_element_type=F32)
    dw = lax.dot_general(a.astype(BF16), gb, (((0,), (0,)), ((), ())), preferred_element_type=F32)
    return da, dw


bdot.defvjp(lambda a, w: (bdot(a, w), (a, w)), _bdot_bwd)


@jax.custom_vjp
def log_sigmoid(z):
    return jnp.minimum(z, 0.0) - jnp.log(1.0 + jnp.exp(-jnp.abs(z)))


def _lsig_bwd(z, g):
    e = jnp.exp(-jnp.abs(z))
    return (g * jnp.where(z >= 0, e, 1.0) / (1.0 + e),)


log_sigmoid.defvjp(lambda z: (log_sigmoid(z), z), _lsig_bwd)


def silu(x):
    return x * jax.nn.sigmoid(x)


def _rms(x):
    return x * lax.rsqrt(jnp.mean(x * x, axis=-1, keepdims=True) + EPS)


def _mod(x, gain, shift, scale):
    return _rms(x) * gain * (1.0 + scale) + shift


def f_mod(xs, ps):
    ((h,),), ((gain,), (shift,), (scale,)) = xs, ps
    return [[_mod(h, gain, shift, scale)], [h]]


def f_res_mod(xs, ps):
    ((h,), (y,)), ((gate,), (gain,), (shift,), (scale,)) = xs, ps
    h1 = h + gate * y
    return [[h1], [_mod(h1, gain, shift, scale)]]


def f_ffn_mid(xs, ps):
    ((ua, ug),), ((w0a, w0g), (w1a, w1g), (w2a, w2g), (ba, bg)) = xs, ps
    a = w0a * prev_row(ua) + w1a * ua + w2a * next_row(ua) + ba
    g = w0g * prev_row(ug) + w1g * ug + w2g * next_row(ug) + bg
    return [[a * silu(g)]]


def f_sc_mid(xs, ps):
    ((bg, cg, v),), ((w0,), (w1,), (w2,)) = xs, ps
    z = cg * v
    return [[bg * (w0 * prev_tok(z) + w1 * z + w2 * next_tok(z))]]


def f_decay(xs, ps):
    ((a,),), ((wd,), (bd,)) = xs, ps
    return [[log_sigmoid(bdot(a, wd) + bd) / TAU]]


def f_gla_post(xs, ps):
    (of, ob, g), ((gain,),) = xs, ps
    return [[_rms(a + b) * gain * silu(c) for a, b, c in zip(of, ob, g)]]


NCH = TT // CHUNK
CTX_CH = CTX // CHUNK
_NT = (((1,), (1,)), ((), ()))
_TN = (((0,), (0,)), ((), ()))
_NN = (((1,), (0,)), ((), ()))


def _chunk_of(d, j):
    return jnp.where(d == 0, j, jnp.where(j < CTX_CH, CTX_CH - 1 - j, NCH + CTX_CH - 1 - j))


def _dot(a, b, dn):
    return lax.dot_general(a, b, dn, preferred_element_type=F32)


def _cumsum_rows(g, suffix):
    n = g.shape[0]
    row = lax.broadcasted_iota(jnp.int32, g.shape, 0)
    s = 1
    while s < n:
        if suffix:
            g = g + jnp.where(row < n - s, pltpu.roll(g, n - s, 0), 0.0)
        else:
            g = g + jnp.where(row >= s, pltpu.roll(g, s, 0), 0.0)
        s *= 2
    return g


def _causal(backward):
    row = lax.broadcasted_iota(jnp.int32, (CHUNK, CHUNK), 0)
    col = lax.broadcasted_iota(jnp.int32, (CHUNK, CHUNK), 1)
    return col >= row if backward else col <= row


def _gla_in_specs(bsz, rev):
    def blk(d, j):
        return _chunk_of(d, (NCH - 1 - j) if rev else j)

    return [
        pl.BlockSpec((bsz, CHUNK, KD), lambda d, j: (0, blk(d, j), 0)),
        pl.BlockSpec((bsz, CHUNK, KD), lambda d, j: (0, blk(d, j), 1)),
        pl.BlockSpec((bsz, CHUNK, VD), lambda d, j: (0, blk(d, j), 1)),
        pl.BlockSpec((bsz, CHUNK, KD), lambda d, j: (0, blk(d, j), d)),
    ], blk


def gla_fwd(pcat, la):
    bsz = pcat.shape[0]
    in_specs, blk = _gla_in_specs(bsz, False)

    def body(q_ref, k_ref, v_ref, la_ref, o_ref, s_ref, st):
        d, j = pl.program_id(0), pl.program_id(1)

        @pl.when(j == 0)
        def _():
            st[...] = jnp.zeros_like(st)

        s_ref[...] = st[...]

        def scan(backward):
            causal = _causal(backward)
            for e in range(bsz):
                g_all = la_ref[e]
                b_all = _cumsum_rows(g_all, backward)
                bl_all = jnp.sum(g_all, axis=0, keepdims=True)
                qs_all = (q_ref[e].astype(F32) * (HK ** -0.5) * jnp.exp(b_all)).astype(BF16)
                ks_all = (k_ref[e] * jnp.exp(-b_all)).astype(BF16)
                kd_all = (k_ref[e] * jnp.exp(bl_all - b_all)).astype(BF16)
                el_all = jnp.exp(bl_all)
                for h in range(HEADS):
                    ks_, vs_ = slice(h * HK, (h + 1) * HK), slice(h * HV, (h + 1) * HV)
                    qs, ks, kd, v = qs_all[:, ks_], ks_all[:, ks_], kd_all[:, ks_], v_ref[e, :, vs_].astype(BF16)
                    s = st[e, h]
                    att = jnp.where(causal, _dot(qs, ks, _NT), 0.0).astype(BF16)
                    o_ref[e, :, vs_] = _dot(qs, s.astype(BF16), _NT) + _dot(att, v, _NN)
                    st[e, h] = el_all[:, ks_] * s + _dot(v, kd, _TN)

        @pl.when(d == 0)
        def _():
            scan(False)

        @pl.when(d == 1)
        def _():
            scan(True)

    return pl.pallas_call(
        body, name="gla_fwd", grid=(2, NCH), in_specs=in_specs,
        out_specs=[pl.BlockSpec((bsz, CHUNK, VD), lambda d, j: (0, blk(d, j), d)),
                   pl.BlockSpec((bsz, None, None, HEADS, HV, HK), lambda d, j: (0, d, j, 0, 0, 0))],
        out_shape=[jax.ShapeDtypeStruct((bsz, TT, 2 * VD), F32), jax.ShapeDtypeStruct((bsz, 2, NCH, HEADS, HV, HK), F32)],
        scratch_shapes=[pltpu.VMEM((bsz, HEADS, HV, HK), F32)],
        compiler_params=pltpu.CompilerParams(dimension_semantics=("arbitrary", "arbitrary")),
    )(pcat, pcat, pcat, la)


def gla_bwd(pcat, la, s_all, do):
    bsz = pcat.shape[0]
    in_specs, blk = _gla_in_specs(bsz, True)
    in_specs += [
        pl.BlockSpec((bsz, None, None, HEADS, HV, HK), lambda d, j: (0, d, NCH - 1 - j, 0, 0, 0)),
        pl.BlockSpec((bsz, CHUNK, VD), lambda d, j: (0, jnp.maximum(blk(d, j) - CTX_CH, 0), 0)),
    ]

    def body(q_ref, k_ref, v_ref, la_ref, s_ref, do_ref, dq_ref, dk_ref, dv_ref, dla_ref, dst):
        d, j = pl.program_id(0), pl.program_id(1)

        @pl.when(j == 0)
        def _():
            dst[...] = jnp.zeros_like(dst)

        latent = blk(d, j) >= CTX_CH
        scale = HK ** -0.5

        def scan(backward):
            causal = _causal(backward)
            for e in range(bsz):
                g_all = la_ref[e]
                b_all = _cumsum_rows(g_all, backward)
                bl_all = jnp.sum(g_all, axis=0, keepdims=True)
                ex_all, ei_all, ed_all, el_all = jnp.exp(b_all), jnp.exp(-b_all), jnp.exp(bl_all - b_all), jnp.exp(bl_all)
                qs_all, ks_all, kd_all = q_ref[e].astype(F32) * scale * ex_all, k_ref[e] * ei_all, k_ref[e] * ed_all
                qsb_all, ksb_all, kdb_all = qs_all.astype(BF16), ks_all.astype(BF16), kd_all.astype(BF16)
                db_parts, dbl_parts = [], []
                for h in range(HEADS):
                    ks_, vs_ = slice(h * HK, (h + 1) * HK), slice(h * HV, (h + 1) * HV)
                    qs, ks, kd, el = qs_all[:, ks_], ks_all[:, ks_], kd_all[:, ks_], el_all[:, ks_]
                    qsb, ksb, kdb, v = qsb_all[:, ks_], ksb_all[:, ks_], kdb_all[:, ks_], v_ref[e, :, vs_].astype(BF16)
                    s, ds1 = s_ref[e, h], dst[e, h]
                    sb, ds1b = s.astype(BF16), ds1.astype(BF16)
                    dob = jnp.where(latent, do_ref[e, :, vs_], 0.0).astype(BF16)
                    att = jnp.where(causal, _dot(qsb, ksb, _NT), 0.0).astype(BF16)
                    datt = jnp.where(causal, _dot(dob, v, _NT), 0.0).astype(BF16)
                    dqs = _dot(dob, sb, _NN) + _dot(datt, ksb, _NN)
                    dks = _dot(datt, qsb, _TN)
                    dv_ref[e, :, vs_] = (_dot(att, dob, _TN) + _dot(kdb, ds1b, _NT)).astype(BF16)
                    dkd = _dot(v, ds1b, _NN)
                    dst[e, h] = _dot(dob, qsb, _TN) + el * ds1
                    del_ = jnp.sum(s * ds1, axis=0, keepdims=True)
                    dq_ref[e, :, ks_] = (dqs * ex_all[:, ks_] * scale).astype(BF16)
                    dk_ref[e, :, ks_] = (dks * ei_all[:, ks_] + dkd * ed_all[:, ks_]).astype(BF16)
                    db_parts.append(dqs * qs - dks * ks - dkd * kd)
                    dbl_parts.append(jnp.sum(dkd * kd, axis=0, keepdims=True) + del_ * el)
                dla_ref[e] = _cumsum_rows(jnp.concatenate(db_parts, -1), not backward) + jnp.concatenate(dbl_parts, -1)

        @pl.when(d == 0)
        def _():
            scan(False)

        @pl.when(d == 1)
        def _():
            scan(True)

    return pl.pallas_call(
        body, name="gla_bwd", grid=(2, NCH), in_specs=in_specs,
        out_specs=[pl.BlockSpec((None, bsz, CHUNK, KD), lambda d, j: (d, 0, blk(d, j), 0)),
                   pl.BlockSpec((None, bsz, CHUNK, KD), lambda d, j: (d, 0, blk(d, j), 0)),
                   pl.BlockSpec((None, bsz, CHUNK, VD), lambda d, j: (d, 0, blk(d, j), 0)),
                   pl.BlockSpec((bsz, CHUNK, KD), lambda d, j: (0, blk(d, j), d))],
        out_shape=[jax.ShapeDtypeStruct((2, bsz, TT, KD), BF16), jax.ShapeDtypeStruct((2, bsz, TT, KD), BF16),
                   jax.ShapeDtypeStruct((2, bsz, TT, VD), BF16), jax.ShapeDtypeStruct((bsz, TT, 2 * KD), F32)],
        scratch_shapes=[pltpu.VMEM((bsz, HEADS, HV, HK), F32)],
        compiler_params=pltpu.CompilerParams(dimension_semantics=("arbitrary", "arbitrary")),
    )(pcat, pcat, pcat, la, s_all, do)


def gla_combine(dq2, dk2, dv2, dgate, dpa):
    bsz = dgate.shape[0]
    tm = CTX

    def body(dq_ref, dk_ref, dv_ref, dg_ref, dpa_ref, o_ref):
        t = pl.program_id(1)
        o_ref[:, 0:KD] = (dq_ref[0].astype(F32) + dq_ref[1].astype(F32)).astype(BF16)
        o_ref[:, KD:2 * KD] = (dk_ref[0].astype(F32) + dk_ref[1].astype(F32)).astype(BF16)
        o_ref[:, 2 * KD:2 * KD + VD] = (dv_ref[0].astype(F32) + dv_ref[1].astype(F32)).astype(BF16)
        o_ref[:, 2 * KD + VD:2 * KD + 2 * VD] = jnp.where(t > 0, dg_ref[...], 0).astype(BF16)
        o_ref[:, 2 * KD + 2 * VD:] = dpa_ref[...].astype(BF16)

    return pl.pallas_call(
        body, name="gla_combine", grid=(bsz, TT // tm),
        in_specs=[pl.BlockSpec((2, None, tm, KD), lambda b, t: (0, b, t, 0)),
                  pl.BlockSpec((2, None, tm, KD), lambda b, t: (0, b, t, 0)),
                  pl.BlockSpec((2, None, tm, VD), lambda b, t: (0, b, t, 0)),
                  pl.BlockSpec((None, tm, VD), lambda b, t: (b, jnp.maximum(t - 1, 0), 0)),
                  pl.BlockSpec((None, tm, 128), lambda b, t: (b, t, 0))],
        out_specs=pl.BlockSpec((None, tm, GLA_IN_PAD), lambda b, t: (b, t, 0)),
        out_shape=jax.ShapeDtypeStruct((bsz, TT, GLA_IN_PAD), BF16),
        compiler_params=pltpu.CompilerParams(dimension_semantics=("arbitrary", "arbitrary")),
    )(dq2, dk2, dv2, dgate, dpa)


def final_loss(h1, fo, gate, gain, tgt):
    bsz, t_len, _ = h1.shape
    tm = 512

    def body(h_ref, f_ref, gate_ref, gain_ref, tgt_ref, loss_ref, dh_ref, df_ref, dgate_ref, dgain_ref):
        b, t = pl.program_id(0), pl.program_id(1)
        target = tgt_ref[...]

        def core(h, fo_, gate_, gain_):
            e = _rms(h + gate_ * fo_) * gain_ - target
            return jnp.sum(0.5 * jnp.sum(e * e, axis=-1, keepdims=True) / D, axis=0, keepdims=True)

        loss, vjp = jax.vjp(core, h_ref[...], f_ref[...], gate_ref[...], gain_ref[...])
        dh, df, dgate, dgain = vjp(jnp.ones((1, 1), F32))
        dh_ref[...] = dh
        df_ref[...] = df.astype(BF16)
        first = jnp.logical_and(b == 0, t == 0)

        @pl.when(first)
        def _():
            loss_ref[...] = jnp.broadcast_to(loss, loss_ref.shape)
            dgain_ref[...] = dgain

        @pl.when(jnp.logical_not(first))
        def _():
            loss_ref[...] += jnp.broadcast_to(loss, loss_ref.shape)
            dgain_ref[...] += dgain

        @pl.when(t == 0)
        def _():
            dgate_ref[...] = dgate

        @pl.when(t > 0)
        def _():
            dgate_ref[...] += dgate

    tile = pl.BlockSpec((None, tm, D), lambda b, t: (b, t, 0))
    per_ex = pl.BlockSpec((None, 1, D), lambda b, t: (b, 0, 0))
    shared = pl.BlockSpec((1, D), lambda b, t: (0, 0))
    return pl.pallas_call(
        body, name="final_loss", grid=(bsz, t_len // tm),
        in_specs=[tile, tile, per_ex, shared, tile],
        out_specs=[pl.BlockSpec((8, 128), lambda b, t: (0, 0)), tile, tile, per_ex, shared],
        out_shape=[jax.ShapeDtypeStruct((8, 128), F32), jax.ShapeDtypeStruct(h1.shape, F32),
                   jax.ShapeDtypeStruct(h1.shape, BF16), jax.ShapeDtypeStruct((bsz, 1, D), F32),
                   jax.ShapeDtypeStruct((1, D), F32)],
        compiler_params=pltpu.CompilerParams(dimension_semantics=("arbitrary", "arbitrary")),
    )(h1, fo, gate, gain, tgt)


ADA_ROWS = 24
ADA_CTX_ROW = 16
ADA_COLS = 6 * D // N_DEV


def ada_fwd(cond, w, b):
    def body(c_ref, w_ref, b_ref, o_ref):
        s = silu(c_ref[...]).astype(BF16)
        o_ref[...] = jnp.dot(s, w_ref[...].astype(BF16), preferred_element_type=F32) + b_ref[...]

    return pl.pallas_call(
        body, name="ada_fwd", grid=(2,),
        in_specs=[pl.BlockSpec((ADA_ROWS, D), lambda i: (0, 0)), pl.BlockSpec((None, D, ADA_COLS), lambda i: (i, 0, 0)),
                  pl.BlockSpec((None, 1, ADA_COLS), lambda i: (i, 0, 0))],
        out_specs=pl.BlockSpec((None, ADA_ROWS, ADA_COLS), lambda i: (i, 0, 0)),
        out_shape=jax.ShapeDtypeStruct((2, ADA_ROWS, ADA_COLS), F32),
    )(cond, w, b)


def ada_bwd(cond, dm_mine, dm_full, w):
    def body(c_ref, dm_ref, dmf_ref, w_ref, gw_ref, gb_ref, cp_ref):
        i = pl.program_id(0)
        s = silu(c_ref[...]).astype(BF16)
        dm = dm_ref[...].astype(BF16)
        gw_ref[...] = _dot(s, dm, _TN)
        gb_ref[...] = jnp.sum(dmf_ref[...], axis=0, keepdims=True)

        @pl.when(i == 0)
        def _():
            cp_ref[...] = _dot(dm_ref[ADA_CTX_ROW:, :].astype(BF16), w_ref[...].astype(BF16), _NT)

    return pl.pallas_call(
        body, name="ada_bwd", grid=(2,),
        in_specs=[pl.BlockSpec((ADA_ROWS, D), lambda i: (0, 0)), pl.BlockSpec((None, ADA_ROWS, ADA_COLS), lambda i: (i, 0, 0)),
                  pl.BlockSpec((None, ADA_ROWS, 6 * D), lambda i: (i, 0, 0)), pl.BlockSpec((None, D, ADA_COLS), lambda i: (i, 0, 0))],
        out_specs=[pl.BlockSpec((None, D, ADA_COLS), lambda i: (i, 0, 0)), pl.BlockSpec((None, 1, 6 * D), lambda i: (i, 0, 0)),
                   pl.BlockSpec((ADA_ROWS - ADA_CTX_ROW, D), lambda i: (0, 0))],
        out_shape=[jax.ShapeDtypeStruct((2, D, ADA_COLS), F32), jax.ShapeDtypeStruct((2, 1, 6 * D), F32),
                   jax.ShapeDtypeStruct((ADA_ROWS - ADA_CTX_ROW, D), F32)],
        compiler_params=pltpu.CompilerParams(dimension_semantics=("arbitrary",)),
    )(cond, dm_mine, dm_full, w)


def cctx_grad(parts, c_ctx):
    def body(p_ref, c_ref, o_ref):
        tot = p_ref[0:1, :]
        for i in range(1, N_DEV):
            tot = tot + p_ref[i:i + 1, :]
        c = c_ref[...]
        sg = jax.nn.sigmoid(c)
        o_ref[...] = tot * sg * (1.0 + c * (1.0 - sg))

    return pl.pallas_call(body, name="cctx_grad", out_shape=jax.ShapeDtypeStruct((1, D), F32))(parts, c_ctx)


def _row_tile(r):
    for t in (512, 256, 128, 80, 64, 40, 32, 16, 8):
        if r % t == 0:
            return t
    return r


def _slot_sum(ref):
    tot = ref[0].astype(F32)
    for i in range(1, ref.shape[0]):
        tot = tot + ref[i].astype(F32)
    return tot


def sum_slots(name, x):
    s, r, c = x.shape
    tr = _row_tile(r)

    def body(x_ref, o_ref):
        o_ref[...] = _slot_sum(x_ref)

    return pl.pallas_call(
        body, name=name, grid=(r // tr,), in_specs=[pl.BlockSpec((s, tr, c), lambda i: (0, i, 0))],
        out_specs=pl.BlockSpec((tr, c), lambda i: (i, 0)), out_shape=jax.ShapeDtypeStruct((r, c), F32),
    )(x)


def _adamw_update(gv, w_ref, m_ref, v_ref, go_ref, d_ref, mo_ref, vo_ref):
    mn = B1 * m_ref[...] + (1.0 - B1) * gv
    vn = B2 * v_ref[...] + (1.0 - B2) * jnp.square(gv)
    m_hat = mn / (1.0 - B1 ** STEP)
    v_hat = vn / (1.0 - B2 ** STEP)
    go_ref[...] = gv
    d_ref[...] = -LR * (m_hat / (jnp.sqrt(v_hat) + AEPS) + WD * w_ref[...])
    mo_ref[...] = mn
    vo_ref[...] = vn


def adamw_slots(name, w, land, sent, me1, m, v, layer, into=None):
    r, c = w.shape[-2:]
    tr = _row_tile(r)
    into = [] if into is None else list(into)

    def body(me_ref, w_ref, land_ref, own_ref, m_ref, v_ref, *rest):
        own = own_ref[...].astype(F32)
        gv = jnp.where(me_ref[0] == 0, own, land_ref[0].astype(F32))
        for s in range(1, N_DEV):
            gv = gv + jnp.where(me_ref[0] == s, own, land_ref[s].astype(F32))
        _adamw_update(gv, w_ref, m_ref, v_ref, *rest[len(into):])

    slab = pl.BlockSpec((None, tr, c), lambda i, me: (layer, i, 0))
    return pl.pallas_call(
        body, name=name, out_shape=[jax.ShapeDtypeStruct(w.shape, F32)] * 4,
        grid_spec=pltpu.PrefetchScalarGridSpec(
            num_scalar_prefetch=1, grid=(r // tr,),
            in_specs=[slab, pl.BlockSpec((N_DEV, tr, c), lambda i, me: (0, i, 0)),
                      pl.BlockSpec((None, tr, c), lambda i, me: (me[0], i, 0)), slab, slab]
            + [pl.BlockSpec(memory_space=pl.ANY)] * len(into),
            out_specs=[slab] * 4),
        input_output_aliases={6 + k: k for k in range(len(into))},
    )(me1, w, land, sent, m, v, *into)


def adamw(name, w, g, m, v, layer=None):
    r, c = w.shape[-2:]
    tr = _row_tile(r)
    stacked = g.ndim == 3

    def body(w_ref, g_ref, m_ref, v_ref, *outs):
        _adamw_update(_slot_sum(g_ref) if stacked else g_ref[...], w_ref, m_ref, v_ref, *outs)

    tile = pl.BlockSpec((tr, c), lambda i: (i, 0))
    slab = tile if layer is None else pl.BlockSpec((None, tr, c), lambda i: (layer, i, 0))
    g_spec = pl.BlockSpec((g.shape[0], tr, c), lambda i: (0, i, 0)) if stacked else tile
    return pl.pallas_call(
        body, name=name, grid=(r // tr,), in_specs=[slab, g_spec, slab, slab], out_specs=[tile] * 4,
        out_shape=[jax.ShapeDtypeStruct((r, c), F32)] * 4,
    )(w, g, m, v)


def _place():
    return lax.axis_index("x"), lax.axis_index("y"), lax.axis_index("c")


def all_gather(name, x):
    r, c = x.shape
    space = pltpu.VMEM

    def body(x_ref, out_ref, send_sems, recv_sems, local_sem):
        px, py, pc = _place()
        me, sibling = (px, py, pc), (px, py, 1 - pc)
        chips = [(1 - px, py), (px, 1 - py), (1 - px, 1 - py)]

        def rows(qx, qy, qc):
            return out_ref.at[pl.ds((4 * qx + 2 * qy + qc) * r, r), :]

        def copy(k, block, to, src=None):
            return pltpu.make_async_remote_copy(
                src_ref=rows(*block) if src is None else src, dst_ref=rows(*block),
                send_sem=send_sems.at[k], recv_sem=recv_sems.at[k], device_id=to, device_id_type=MESH)

        mine = pltpu.make_async_copy(x_ref, rows(*me), local_sem)
        mine.start()
        first = [copy(0, me, sibling, src=x_ref)]
        first += [copy(1 + j, me, (*chip, pc), src=x_ref) for j, chip in enumerate(chips)]
        for cp in first:
            cp.start()
        passed = [copy(4 + j, (*chip, pc), sibling) for j, chip in enumerate(chips)]
        for j, chip in enumerate(chips):
            copy(1 + j, (*chip, pc), me).wait_recv()
            passed[j].start()
        copy(0, sibling, me).wait_recv()
        for j, chip in enumerate(chips):
            copy(4 + j, (*chip, 1 - pc), me).wait_recv()
        for cp in first + passed:
            cp.wait_send()
        mine.wait()

    return pl.pallas_call(
        body, name=name, out_shape=jax.ShapeDtypeStruct((N_DEV * r, c), x.dtype),
        in_specs=[pl.BlockSpec(memory_space=space)], out_specs=pl.BlockSpec(memory_space=space),
        scratch_shapes=[pltpu.SemaphoreType.DMA((7,)), pltpu.SemaphoreType.DMA((7,)), pltpu.SemaphoreType.DMA],
    )(x)


_HBM = pl.BlockSpec(memory_space=pltpu.HBM)
_SEM = pl.BlockSpec(memory_space=pltpu.SEMAPHORE)
_EFFECT = pltpu.SideEffectType.DATAFLOW_SIDE_EFFECTING


def _peers():
    px, py, pc = _place()
    return [(1 - px if k & 4 else px, 1 - py if k & 2 else py, 1 - pc if k & 1 else pc) for k in range(1, N_DEV)]


def _slot(dev):
    return 4 * dev[0] + 2 * dev[1] + dev[2]


def _split_copies(src_refs, land_refs, send_sems, recv_sems, gather):
    me = _slot(_place())
    return [pltpu.make_async_remote_copy(
        src_ref=src if gather else src.at[_slot(peer)], dst_ref=land.at[me],
        send_sem=send_sems.at[a * (N_DEV - 1) + k], recv_sem=recv_sems.at[a * (N_DEV - 1) + k],
        device_id=peer, device_id_type=MESH)
        for a, (src, land) in enumerate(zip(src_refs, land_refs)) for k, peer in enumerate(_peers())]


def exchange_start(name, srcs, gather, after):
    n = len(srcs)
    lands = [pltpu.HBM((N_DEV,) + s.shape if gather else s.shape, s.dtype) for s in srcs]

    def body(*refs):
        send_sems, recv_sems = refs[2 * n + 1:2 * n + 3]
        for cp in _split_copies(refs[:n], refs[n:2 * n], send_sems, recv_sems, gather):
            cp.start()
        refs[-1][...] = jnp.zeros_like(refs[-1])

    sems = pltpu.SemaphoreType.DMA((n * (N_DEV - 1),))
    res = pl.pallas_call(
        body, name=name,
        out_shape=(sems, sems, *[pltpu.HBM(s.shape, s.dtype) for s in srcs], *lands, jax.ShapeDtypeStruct((8, 128), F32)),
        in_specs=(_HBM,) * (2 * n) + (pl.BlockSpec(memory_space=pl.ANY),),
        out_specs=(_SEM, _SEM) + (_HBM,) * (2 * n) + (pl.BlockSpec(memory_space=pltpu.VMEM),),
        input_output_aliases={i: 2 + i for i in range(2 * n)},
        compiler_params=pltpu.CompilerParams(has_side_effects=_EFFECT),
    )(*[pltpu.with_memory_space_constraint(s, pltpu.HBM) for s in srcs],
      *[pltpu.with_memory_space_constraint(lax.empty(ld.shape, ld.dtype), pltpu.HBM) for ld in lands], after)
    return res[0], res[1], list(res[2:2 + n]), list(res[2 + n:2 + 2 * n]), res[-1]


def exchange_wait(name, started, after, gather):
    send_sems, recv_sems, srcs, lands, _ = started
    n = len(srcs)
    after = list(after) if isinstance(after, (list, tuple)) else [after]

    def body(*refs):
        send_sems, recv_sems = refs[2 * n:2 * n + 2]
        for cp in _split_copies(refs[:n], refs[n:2 * n], send_sems, recv_sems, gather):
            cp.wait_send()
            cp.wait_recv()

    res = pl.pallas_call(
        body, name=name, out_shape=tuple(pltpu.HBM(a.shape, a.dtype) for a in srcs + lands),
        in_specs=(_HBM,) * (2 * n) + (_SEM, _SEM) + (pl.BlockSpec(memory_space=pl.ANY),) * len(after),
        out_specs=(_HBM,) * (2 * n), input_output_aliases={i: i for i in range(2 * n)},
        compiler_params=pltpu.CompilerParams(has_side_effects=_EFFECT),
    )(*srcs, *lands, send_sems, recv_sems, *after)
    return list(res[:n]), list(res[n:])


NCF = FFN_H // FFN_TC


def _size(shape):
    n = 1
    for s in shape:
        n *= s
    return n


def _padded_rows(n_elems, row_mult):
    return -(-n_elems // (D * row_mult)) * row_mult


def _pack_rows(arrs, dtype, row_mult):
    rows, offs, r0 = [], [], 0
    for a in arrs:
        flat = a.reshape(-1).astype(dtype)
        n = _padded_rows(flat.shape[0], row_mult)
        rows.append(jnp.pad(flat, (0, n * D - flat.shape[0])).reshape(n, D))
        offs.append(r0)
        r0 += n
    return jnp.concatenate(rows, 0), offs


def _unpack_rows(buf, offs, shapes):
    lead, out = buf.shape[:-2], []
    for o, shp in zip(offs, shapes):
        n = _size(shp)
        nr = -(-n // D)
        out.append(buf[..., o:o + nr, :].reshape(lead + (nr * D,))[..., :n].reshape(lead + tuple(shp)))
    return out


def _rows3(w):
    return [w[i:i + 1] for i in range(3)]


def f_mod1(xs, ps):
    return f_mod(xs, ps)[:1]


def kernel(x, c, ctx, c_ctx, ada_w, ada_b, norm_mix, norm_ffn, gla_w_in, gla_w_a2, gla_b_a, gla_head_norm, gla_w_out, sc_w_in, sc_conv_w, sc_w_out, ffn_w_up, ffn_conv_w, ffn_conv_b, ffn_w_down, final_norm, loss_target, m_c_ctx, m_ada_w, m_ada_b, m_norm_mix, m_norm_ffn, m_gla_w_in, m_gla_w_a2, m_gla_b_a, m_gla_head_norm, m_gla_w_out, m_sc_w_in, m_sc_conv_w, m_sc_w_out, m_ffn_w_up, m_ffn_conv_w, m_ffn_conv_b, m_ffn_w_down, m_final_norm, v_c_ctx, v_ada_w, v_ada_b, v_norm_mix, v_norm_ffn, v_gla_w_in, v_gla_w_a2, v_gla_b_a, v_gla_head_norm, v_gla_w_out, v_sc_w_in, v_sc_conv_w, v_sc_w_out, v_ffn_w_up, v_ffn_conv_w, v_ffn_conv_b, v_ffn_w_down, v_final_norm):
    names = ["c_ctx", "ada_w", "ada_b", "norm_mix", "norm_ffn", "gla_w_in", "gla_w_a2", "gla_b_a", "gla_head_norm",
             "gla_w_out", "sc_w_in", "sc_conv_w", "sc_w_out", "ffn_w_up", "ffn_conv_w", "ffn_conv_b", "ffn_w_down",
             "final_norm"]
    w_ = dict(zip(names, [c_ctx, ada_w, ada_b, norm_mix, norm_ffn, gla_w_in, gla_w_a2, gla_b_a, gla_head_norm, gla_w_out,
                          sc_w_in, sc_conv_w, sc_w_out, ffn_w_up, ffn_conv_w, ffn_conv_b, ffn_w_down, final_norm]))
    m_ = dict(zip(names, [m_c_ctx, m_ada_w, m_ada_b, m_norm_mix, m_norm_ffn, m_gla_w_in, m_gla_w_a2, m_gla_b_a,
                          m_gla_head_norm, m_gla_w_out, m_sc_w_in, m_sc_conv_w, m_sc_w_out, m_ffn_w_up, m_ffn_conv_w,
                          m_ffn_conv_b, m_ffn_w_down, m_final_norm]))
    v_ = dict(zip(names, [v_c_ctx, v_ada_w, v_ada_b, v_norm_mix, v_norm_ffn, v_gla_w_in, v_gla_w_a2, v_gla_b_a,
                          v_gla_head_norm, v_gla_w_out, v_sc_w_in, v_sc_conv_w, v_sc_w_out, v_ffn_w_up, v_ffn_conv_w,
                          v_ffn_conv_b, v_ffn_w_down, v_final_norm]))
    me = 4 * lax.axis_index("x") + 2 * lax.axis_index("y") + lax.axis_index("c")
    bsz = x.shape[0]
    tm = 256
    nt = SEQ // tm
    ctx_tiles = CTX // tm
    pe = functools.partial(P, per_example=True)

    groups = {"ffn1": [("ffn_w_up", 1), ("ffn_w_down", 1)], "sc": [("sc_w_in", 0), ("sc_w_out", 0)],
              "ffn0": [("ffn_w_up", 0), ("ffn_w_down", 0)], "gla": [("gla_w_in", 0), ("gla_w_out", 0)]}
    ag_groups = {"gin": [("gla_w_in", 0)], "ffn0": [("gla_w_out", 0), ("ffn_w_up", 0), ("ffn_w_down", 0)],
                 "sc": groups["sc"], "ffn1": groups["ffn1"]}
    ag_started = {}

    def start_gather(g, after):
        ag_started[g] = exchange_start(f"ag_{g}_start", [w_[n][i].astype(BF16) for n, i in ag_groups[g]], True, after)
        return ag_started[g][4]

    small_sharded = [c, gla_w_a2, gla_b_a, sc_conv_w, ffn_conv_w]
    pack0, offs0 = _pack_rows(small_sharded, F32, 8)
    g0 = all_gather("ag_small", pack0).reshape(N_DEV, pack0.shape[0], D)
    c_all, wa2_s, ba_s, scw_s, fcw_s = _unpack_rows(g0, offs0, [a.shape for a in small_sharded])
    w_a2 = wa2_s[:, 0].transpose(1, 2, 0, 3).reshape(2, RANK, KD)
    b_a = ba_s[:, 0].transpose(1, 0, 2).reshape(2, KD)
    sc_cw = scw_s[:, 0].transpose(1, 0, 2).reshape(3, D)
    ffn_cw = fcw_s.transpose(1, 2, 0, 3).reshape(2, 3, 2 * FFN_H)

    cond = jnp.concatenate([c_all.reshape(N_DEV * bsz, D), c_ctx[None], jnp.zeros((ADA_ROWS - N_DEV * bsz - 1, D), F32)], 0)
    b_mine = lax.dynamic_slice(ada_b, (0, me * ADA_COLS), (2, ADA_COLS)).reshape(2, 1, ADA_COLS)
    mod_part = ada_fwd(cond, ada_w, b_mine)
    mod = all_gather("ag_mod", mod_part.reshape(2 * ADA_ROWS, ADA_COLS))
    mod = mod.reshape(N_DEV, 2, ADA_ROWS, ADA_COLS).transpose(1, 2, 0, 3).reshape(2, ADA_ROWS, 6 * D)
    mods = lax.dynamic_slice(mod, (0, bsz * me, 0), (2, bsz, 6 * D))
    md = [[mods[i][:, k * D:(k + 1) * D].reshape(bsz, 1, D) for k in range(6)] for i in range(2)]
    mc = [mod[0, ADA_CTX_ROW, k * D:(k + 1) * D][None] for k in range(2)]

    tok = mod
    for g in ag_groups:
        tok = start_gather(g, tok)
    norm_mix = norm_mix + tok[0, 0]

    def gathered(g, after):
        mine, lands = exchange_wait(f"ag_{g}_wait", ag_started[g], after, True)
        return [lax.dynamic_update_index_in_dim(ld, mn, me, 0) for ld, mn in zip(lands, mine)]

    s_up, w_down = [None, None], [None, None]
    wd = jnp.zeros((128, 2 * KD), F32).at[:RANK, :KD].set(w_a2[0]).at[RANK:2 * RANK, KD:].set(w_a2[1])
    bd = b_a.reshape(1, 2 * KD)
    scw = _rows3(sc_cw)
    head_gain = gla_head_norm.reshape(1, HV)
    gains_mix = [norm_mix[i][None] for i in range(2)]
    gains_ffn = [norm_ffn[i][None] for i in range(2)]

    def tokens(a2d, t_len):
        return a2d.reshape(bsz, t_len, -1)

    def ffn_params(i):
        rows = [ffn_cw[i][t] for t in range(3)] + [ffn_conv_b[i]]
        return [P(a.reshape(2, FFN_H), w=FFN_TC, rows=True) for a in rows]

    def ffn_fwd(i, hn2):
        u = mm(f"ffn_up{i}", V(hn2, "tok"), V(s_up[i], "cols"), out="planes", out_dtype=BF16, planes_t=SEQ)
        act = rowwise(f"ffn_mid{i}", f_ffn_mid, [X(u, w=FFN_TC, planes=True)], ffn_params(i), tm=SEQ, nt=1, nc=NCF,
                      outs=[(FFN_TC, BF16, 1)])[0]
        return u, act

    def arrays(ps):
        return [p["a"] for p in ps]

    ps_in0 = [P(gains_mix[0]), pe(md[0][0]), pe(md[0][1])]
    ps_ctx = [P(gains_mix[0]), P(mc[0]), P(mc[1])]
    hn0 = rowwise("mod_in0", f_mod, [X(x)], ps_in0, tm=tm, nt=nt, outs=[(D, BF16, 1)])[0]
    hnc = rowwise("mod_ctx", f_mod, [X(ctx)], ps_ctx, tm=tm, nt=ctx_tiles, outs=[(D, BF16, 1)])[0]
    hcat = jnp.concatenate([hnc, hn0], axis=1)
    (s_gin,) = gathered("gin", hcat)
    w_gin = V(s_gin, "cols", width=GLA_IN_PAD)
    pcat = tokens(mm("gla_in", V(hcat, "tok"), w_gin, out_dtype=BF16), TT)
    pa_x = X(pcat, w=128, co=(GLA_IN_PAD - 128) // 128)
    la = rowwise("gla_decay", f_decay, [pa_x], [P(wd), P(bd)], tm=tm, nt=TT // tm, outs=[(2 * KD, F32, 1)])[0]
    o2, s_all = gla_fwd(pcat, la)
    post_xs = [X(o2, w=VD, co=0, ro=ctx_tiles, split=HEADS), X(o2, w=VD, co=1, ro=ctx_tiles, split=HEADS),
               X(pcat, w=VD, co=2, ro=ctx_tiles, split=HEADS)]
    yin0 = rowwise("gla_post", f_gla_post, post_xs, [P(head_gain)], tm=tm, nt=nt, outs=[(VD, BF16, HEADS)])[0]
    s_gout, s_up[0], s_down0 = gathered("ffn0", yin0)
    w_gout, w_down[0] = s_gout.reshape(VD, D), s_down0.reshape(FFN_H, D)
    ps_mid0 = [pe(md[0][2]), P(gains_ffn[0]), pe(md[0][3]), pe(md[0][4])]
    y0, h1_0, hn2_0 = mm_res_mod("gla_out", yin0, w_gout, x, *arrays(ps_mid0))
    u0, act0 = ffn_fwd(0, hn2_0)
    ps_in1 = [pe(md[0][5]), P(gains_mix[1]), pe(md[1][0]), pe(md[1][1])]
    fo0, h2_0, hn1 = mm_res_mod("ffn_down0", act0, w_down[0], h1_0, *arrays(ps_in1))

    s_sin, s_sout = gathered("sc", hn1)
    w_sout = s_sout.reshape(D, D)
    p1 = tokens(mm("sc_in", V(hn1, "tok"), V(s_sin, "cols")), SEQ)
    sc_ps = [P(a) for a in scw]
    yin1 = rowwise("sc_mid", f_sc_mid, [X(p1, split=3)], sc_ps, tm=tm, nt=nt, outs=[(D, BF16, 1)])[0]
    ps_mid1 = [pe(md[1][2]), P(gains_ffn[1]), pe(md[1][3]), pe(md[1][4])]
    y1, h1_1, hn2_1 = mm_res_mod("sc_out", yin1, w_sout, h2_0, *arrays(ps_mid1))
    s_up[1], s_down1 = gathered("ffn1", hn2_1)
    w_down[1] = s_down1.reshape(FFN_H, D)
    u1, act1 = ffn_fwd(1, hn2_1)
    fo1 = tokens(mm("ffn_down1", V(act1, "tok"), V(w_down[1])), SEQ)
    loss8, dh1_1, dfo1, dm5_1, g_final = final_loss(h1_1, fo1, md[1][5], final_norm[None], loss_target)

    def ffn_bwd(i, u, act, hn2, dfo):
        dact = tokens(mm(f"ffn_down_dx{i}", V(dfo, "tok"), V(w_down[i]), form="nt", out_dtype=BF16), SEQ)
        g_down = mm(f"ffn_down_dw{i}", V(act, "tok"), V(dfo, "tok"), form="tn", out_dtype=BF16)
        r = rowwise(f"ffn_mid_bwd{i}", f_ffn_mid, [X(u, w=FFN_TC, planes=True)], ffn_params(i), tm=SEQ, nt=1, nc=NCF,
                    douts=[X(dact, w=FFN_TC)], dx={0: BF16}, dp=[0, 1, 2, 3])
        du, g_cw, g_cb = r[0], jnp.stack([a.reshape(2 * FFN_H) for a in r[1:4]]), r[4].reshape(1, 2 * FFN_H)
        dhn2 = tokens(mm(f"ffn_up_dx{i}", V(du, "planes"), V(s_up[i], "cols"), form="nt", out_dtype=BF16), SEQ)
        g_up = mm(f"ffn_up_dw{i}", V(hn2, "tok"), V(du, "planes"), form="tn", out="cols", out_dtype=BF16)
        return dhn2, g_up, row_slots(g_down), g_cw, g_cb

    def res_mod_bwd(name, h, y, ps, dh1, dhn):
        return rowwise(name, f_res_mod, [X(h), X(y)], ps, tm=2 * tm, nt=nt // 2, douts=[X(dh1), X(dhn)],
                       dx={0: F32, 1: BF16}, dp=[0, 1, 2, 3])

    def row_slots(g):
        return g.reshape(N_DEV, -1, g.shape[-1])

    a2a_started = {}

    def send_grads(g, slots, after=None):
        a2a_started[g] = exchange_start(f"a2a_{g}_start", list(slots), False, loss8 if after is None else after)
        return a2a_started[g][4][0, 0]

    def after_start(ps, tok):
        return [dict(ps[0], a=ps[0]["a"] + tok)] + ps[1:]

    dhn2_1, g_up1, g_down1, g_fcw1, g_fcb1 = ffn_bwd(1, u1, act1, hn2_1, dfo1)
    tok = send_grads("ffn1", [g_up1, g_down1])
    dh2_0, dy1, dm2_1, g_nffn1, dm3_1, dm4_1 = res_mod_bwd("res_mod_mid1_bwd", h2_0, y1, after_start(ps_mid1, tok), dh1_1, dhn2_1)
    dyin1 = tokens(mm("sc_out_dx", V(dy1, "tok"), V(w_sout), form="nt", out_dtype=BF16), SEQ)
    g_sout = row_slots(mm("sc_out_dw", V(yin1, "tok"), V(dy1, "tok"), form="tn", out_dtype=BF16))
    r = rowwise("sc_mid_bwd", f_sc_mid, [X(p1, split=3)], sc_ps, tm=tm, nt=nt, douts=[X(dyin1)], dx={0: BF16}, dp=[0, 1, 2])
    dp1, g_scw = r[0], jnp.concatenate(r[1:4], 0)
    dhn1 = tokens(mm("sc_in_dx", V(dp1, "tok"), V(s_sin, "cols"), form="nt", out_dtype=BF16), SEQ)
    g_sin = mm("sc_in_dw", V(hn1, "tok"), V(dp1, "tok"), form="tn", out="cols", out_dtype=BF16)
    tok = send_grads("sc", [g_sin, g_sout])
    dh1_0, dfo0, dm5_0, g_nmix1, dm0_1, dm1_1 = res_mod_bwd("res_mod_in1_bwd", h1_0, fo0, after_start(ps_in1, tok), dh2_0, dhn1)

    dhn2_0, g_up0, g_down0, g_fcw0, g_fcb0 = ffn_bwd(0, u0, act0, hn2_0, dfo0)
    tok = send_grads("ffn0", [g_up0, g_down0])
    dx_res, dy0, dm2_0, g_nffn0, dm3_0, dm4_0 = res_mod_bwd("res_mod_mid0_bwd", x, y0, after_start(ps_mid0, tok), dh1_0, dhn2_0)
    dyin0 = tokens(mm("gla_out_dx", V(dy0, "tok"), V(w_gout), form="nt", out_dtype=BF16), SEQ)
    do, dgate, g_head = rowwise("gla_post_bwd", f_gla_post, post_xs, [P(head_gain)], tm=tm, nt=nt,
                                douts=[X(dyin0, split=HEADS)], dx={0: BF16, 2: BF16}, dp=[0])
    dq2, dk2, dv2, dla = gla_bwd(pcat, la, s_all, do)
    dpa, g_wd, g_bd = rowwise("gla_decay_bwd", f_decay, [pa_x], [P(wd), P(bd)], tm=tm, nt=TT // tm, douts=[X(dla)],
                              dx={0: BF16}, dp=[0, 1])
    dpcat = gla_combine(dq2, dk2, dv2, dgate, dpa)
    dhcat = tokens(mm("gla_in_dx", V(dpcat, "tok"), w_gin, form="nt", out_dtype=BF16), TT)
    grad_x, g_nmix0, dm0_0, dm1_0 = rowwise("mod_in0_bwd", f_mod, [X(x)], ps_in0, tm=tm, nt=nt,
                                            douts=[X(dhcat, ro=ctx_tiles), X(dx_res)], dx={0: F32}, dp=[0, 1, 2])
    g_nmix0c, dmc0, dmc1 = rowwise("mod_ctx_bwd", f_mod1, [X(ctx)], ps_ctx, tm=tm, nt=ctx_tiles, douts=[X(dhcat)],
                                   dx={}, dp=[0, 1, 2])

    zero_row = jnp.zeros((1, 4 * D), F32)
    dmod = [jnp.concatenate([jnp.concatenate([a.reshape(bsz, D) for a in dms], 1), ctx_row], 0)
            for dms, ctx_row in (([dm0_0, dm1_0, dm2_0, dm3_0, dm4_0, dm5_0], jnp.concatenate([dmc0, dmc1, zero_row], 1)),
                                 ([dm0_1, dm1_1, dm2_1, dm3_1, dm4_1, dm5_1], jnp.zeros((1, 6 * D), F32)))]
    g_wa2 = jnp.stack([g_wd[:RANK, :KD], g_wd[RANK:2 * RANK, KD:]])
    small_grads = [jnp.stack(dmod), jnp.concatenate([g_nmix0 + g_nmix0c, g_nmix1], 0), jnp.concatenate([g_nffn0, g_nffn1], 0),
                   g_head, jnp.concatenate([g_fcb0, g_fcb1], 0), g_final, g_wa2, g_bd.reshape(2, KD), g_scw,
                   jnp.stack([g_fcw0, g_fcw1]), loss8[:1]]
    pack1, offs1 = _pack_rows(small_grads, F32, 8)
    ag1 = exchange_start("ag_grads_start", [pack1], True, loss8)
    g_gin = mm("gla_in_dw", V(hcat, "tok"), V(dpcat, "tok"), form="tn", out="cols", out_dtype=BF16, shard_n=GLA_IN // N_DEV,
               after=ag1[4])
    g_gout = row_slots(mm("gla_out_dw", V(yin0, "tok"), V(dy0, "tok"), form="tn", out_dtype=BF16, after=ag1[4]))
    mine1, land1 = exchange_wait("ag_grads_wait", ag1, [g_gin, g_gout], True)
    g1 = lax.dynamic_update_index_in_dim(land1[0], mine1[0], me, 0)
    dmod_all = _unpack_rows(g1, offs1[:1], [small_grads[0].shape])[0]
    tot = _unpack_rows(sum_slots("sum_small", g1), offs1, [a.shape for a in small_grads])
    loss = tot[10][0, 0]
    dm_rows = dmod_all[:, :, :bsz].transpose(1, 0, 2, 3).reshape(2, N_DEV * bsz, 6 * D)
    dm_full = jnp.concatenate([dm_rows, tot[0][:, bsz:], jnp.zeros((2, ADA_ROWS - N_DEV * bsz - 1, 6 * D), F32)], 1)
    dm_mine = lax.dynamic_slice(dm_full, (0, 0, me * ADA_COLS), (2, ADA_ROWS, ADA_COLS))
    g_ada_w, g_ada_b, cpart = ada_bwd(cond, dm_mine, dm_full, ada_w)
    cparts = all_gather("ag_cctx", cpart).reshape(N_DEV, ADA_ROWS - ADA_CTX_ROW, D)[:, 0]
    g_cctx = cctx_grad(cparts, c_ctx[None])[0]
    tok = send_grads("gla", [g_gin, g_gout], after=g_cctx)

    def my_cols(full, n):
        return lax.dynamic_slice_in_dim(full, me * n, n, axis=full.ndim - 1)

    grads = {
        "c_ctx": g_cctx, "ada_b": g_ada_b.reshape(2, 6 * D), "norm_mix": tot[1], "norm_ffn": tot[2],
        "gla_head_norm": tot[3], "ffn_conv_b": tot[4], "final_norm": tot[5].reshape(D),
        "gla_w_a2": my_cols(tot[6], KD // N_DEV)[None], "gla_b_a": my_cols(tot[7], KD // N_DEV)[None],
        "sc_conv_w": my_cols(tot[8], D // N_DEV)[None], "ffn_conv_w": my_cols(tot[9], 2 * FFN_H // N_DEV),
    }

    res_ada = adamw("adamw_ada", *[a.reshape(2 * D, ADA_COLS) for a in (ada_w, g_ada_w, m_ada_w, v_ada_w)])
    grads["c_ctx"] = g_cctx + tok
    big = ["gla_w_in", "gla_w_out", "sc_w_in", "sc_w_out", "ffn_w_up", "ffn_w_down"]
    small = [n for n in names if n not in big and n != "ada_w"]
    g_small = _pack_rows([grads[n] for n in small], F32, 8)[0]
    res_small = adamw("adamw_small", _pack_rows([w_[n] for n in small], F32, 8)[0], g_small,
                      _pack_rows([m_[n] for n in small], F32, 8)[0], _pack_rows([v_[n] for n in small], F32, 8)[0])
    offs_s = _pack_rows([w_[n] for n in small], F32, 8)[1]

    big_res, done, me1 = {}, {"small": res_small[0], "ada_w": res_ada[0]}, jnp.reshape(me, (1,)).astype(jnp.int32)
    for g in groups:
        sent, lands = exchange_wait(f"a2a_{g}_wait", a2a_started[g], list(done.values()), False)
        for (n, i), mine, land in zip(groups[g], sent, lands):
            big_res[n] = adamw_slots(f"adamw_{n}{i}", w_[n], land, mine, me1, m_[n], v_[n], i, into=big_res.get(n))
            done[n] = big_res[n][0]

    out = {}
    for kind, idx in (("grad", 0), ("delta", 1), ("new_m", 2), ("new_v", 3)):
        vals = {n: big_res[n][idx] for n in big}
        vals["ada_w"] = res_ada[idx].reshape(ada_w.shape)
        vals.update(zip(small, _unpack_rows(res_small[idx], offs_s, [w_[n].shape for n in small])))
        out[kind] = [vals[n] for n in names]
    return (loss, grad_x, *out["grad"], *out["delta"], *out["new_m"], *out["new_v"])
```

```python
import functools

import jax
import jax.numpy as jnp
from jax import lax
from jax.experimental import pallas as pl
from jax.experimental.pallas import tpu as pltpu

F32 = jnp.float32
BF16 = jnp.bfloat16

N_DEV = 8
D = 1024
SEQ = 2048
CTX = 256
TT = CTX + SEQ
GRID_W = 64
CHUNK = 64
HEADS = 4
HK = 128
HV = 256
KD = 512
VD = 1024
RANK = 16
TAU = 16.0
GLA_IN = 3104
GLA_IN_PAD = 3200
FFN_H = 2560
FFN_TC = 256
EPS = 1e-6
LR, B1, B2, AEPS, WD, STEP = 0.001, 0.9, 0.999, 1e-08, 0.01, 10
MESH = pl.DeviceIdType.MESH


def _blocks(n):
    return [n] + [t for t in range(n - n % 128, 0, -128) if n % t == 0 and t != n]


def V(arr, kind="flat", width=None):
    if kind == "tok":
        return V(arr.reshape(-1, arr.shape[-1]))
    if kind == "flat":
        r, c = arr.shape
        return dict(a=arr, kind=kind, shape=(r, c), rows=_blocks(r), cols=_blocks(c))
    if kind == "planes":
        bsz, _, t, ch = arr.shape
        return dict(a=arr, kind=kind, shape=(bsz * t, 2 * ch), rows=_blocks(t), cols=[2 * ch] + _blocks(ch), t=t, ch=ch)
    _, r, n = arr.shape
    if width is not None:
        return dict(a=arr, kind=kind, shape=(r, width), rows=_blocks(r), cols=[width], n=n, pad=width - N_DEV * n)
    return dict(a=arr, kind=kind, shape=(r, N_DEV * n), rows=_blocks(r), cols=[8 * n, 4 * n, 2 * n], n=n, pad=0)


def _view_spec(v, br, bc, idx):
    if v["kind"] == "flat":
        return pl.BlockSpec((br, bc), idx)
    if v["kind"] == "planes":
        nt = v["t"] // br
        if bc == 2 * v["ch"]:
            return pl.BlockSpec((None, 2, br, v["ch"]), lambda i, j, k: (idx(i, j, k)[0] // nt, 0, idx(i, j, k)[0] % nt, 0))
        nch = v["ch"] // bc

        def at(i, j, k):
            r, c = idx(i, j, k)
            return r // nt, c // nch, r % nt, c % nch
        return pl.BlockSpec((None, None, br, bc), at)
    return pl.BlockSpec(((bc - v["pad"]) // v["n"], br, v["n"]), lambda i, j, k: (idx(i, j, k)[1], idx(i, j, k)[0], 0))


def _out_view(kind, rows, cols, dtype, planes_t=None, shard_n=None):
    if kind == "flat":
        shape = (rows, cols)
    elif kind == "planes":
        shape = (rows // planes_t, 2, planes_t, cols // 2)
    elif shard_n is not None:
        return V(jax.ShapeDtypeStruct((N_DEV, rows, shard_n), dtype), kind, width=cols)
    else:
        shape = (N_DEV, rows, cols // N_DEV)
    return V(jax.ShapeDtypeStruct(shape, dtype), kind)


MM_VMEM_BUDGET = 40 * 2 ** 20
MM_VMEM_LIMIT = 56 * 2 ** 20
MM_MAX_TILE = 1536


def _mm_tiles(m, n, kk, ms, ns, ks, a_bytes, b_bytes, o_bytes):
    best = None
    for tk in ks:
        for tm in [t for t in ms if t <= MM_MAX_TILE] or ms:
            for tn in [t for t in ns if t <= MM_MAX_TILE] or ns:
                one_k = tk == kk
                need = 2 * (tm * tk * a_bytes + tk * tn * b_bytes + tm * tn * o_bytes) + (0 if one_k else tm * tn * 4)
                if need > MM_VMEM_BUDGET:
                    continue
                steps = (m // tm) * (n // tn) * (kk // tk)
                traffic = (m * kk * a_bytes * (1 if one_k else n // tn)
                           + kk * n * b_bytes * (1 if one_k and n == tn else m // tm) + m * n * o_bytes)
                fill = (tm * tk * a_bytes + tk * tn * b_bytes) / 2.5e12
                cost = max(2.0 * m * n * kk / (9e14 if one_k else 6.5e14), traffic / 2.5e12) + steps * 0.4e-6 + fill
                if best is None or cost < best[0]:
                    best = (cost, tm, tn, tk)
    return best[1:]


def mm(name, a, b, form="nn", out="flat", out_dtype=F32, planes_t=None, shard_n=None, after=None):
    (m, kk) = a["shape"][::-1] if form == "tn" else a["shape"]
    n = b["shape"][0] if form == "nt" else b["shape"][1]
    assert (b["shape"][1] if form == "nt" else b["shape"][0]) == kk, (name, a["shape"], b["shape"])
    o = _out_view(out, m, n, out_dtype, planes_t, shard_n)
    a_m, a_k = (a["cols"], a["rows"]) if form == "tn" else (a["rows"], a["cols"])
    b_k, b_n = (b["cols"], b["rows"]) if form == "nt" else (b["rows"], b["cols"])
    tm, tn, tk = _mm_tiles(m, n, kk, [t for t in a_m if t in o["rows"]], [t for t in b_n if t in o["cols"]],
                           [t for t in a_k if t in b_k], a["a"].dtype.itemsize, b["a"].dtype.itemsize,
                           jnp.dtype(out_dtype).itemsize)
    nk = kk // tk
    dn = (((0 if form == "tn" else 1,), (1 if form == "nt" else 0,)), ((), ()))

    def load(ref, v):
        if len(ref.shape) == 3:
            pieces = [ref[p].astype(BF16) for p in range(ref.shape[0])]
            if v.get("pad"):
                pieces.append(jnp.zeros(ref.shape[1:2] + (v["pad"],), BF16))
            return jnp.concatenate(pieces, axis=-1)
        return ref[...].astype(BF16)

    def store(o_ref, val):
        val = val.astype(out_dtype)
        if len(o_ref.shape) == 3:
            w = o_ref.shape[-1]
            for p in range(o_ref.shape[0]):
                o_ref[p] = val[:, p * w:(p + 1) * w]
        else:
            o_ref[...] = val

    def body(a_ref, b_ref, *rest):
        o_ref, acc = rest[0 if after is None else 1], rest[1 if after is None else 2:]
        if nk == 1:
            store(o_ref, lax.dot_general(load(a_ref, a), load(b_ref, b), dn, preferred_element_type=F32))
            return
        k, acc_ref = pl.program_id(2), acc[0]

        @pl.when(k == 0)
        def _():
            acc_ref[...] = jnp.zeros_like(acc_ref)

        acc_ref[...] += lax.dot_general(load(a_ref, a), load(b_ref, b), dn, preferred_element_type=F32)

        @pl.when(k == nk - 1)
        def _():
            store(o_ref, acc_ref[...])

    if form == "tn":
        a_spec = _view_spec(a, tk, tm, lambda i, j, k: (k, i))
    else:
        a_spec = _view_spec(a, tm, tk, lambda i, j, k: (i, k))
    if form == "nt":
        b_spec = _view_spec(b, tn, tk, lambda i, j, k: (j, k))
    else:
        b_spec = _view_spec(b, tk, tn, lambda i, j, k: (k, j))
    return pl.pallas_call(
        body, name=name, grid=(m // tm, n // tn, nk),
        in_specs=[a_spec, b_spec] + ([] if after is None else [pl.BlockSpec(memory_space=pl.ANY)]),
        out_specs=_view_spec(o, tm, tn, lambda i, j, k: (i, j)), out_shape=o["a"],
        scratch_shapes=[pltpu.VMEM((tm, tn), F32)] if nk > 1 else [],
        compiler_params=pltpu.CompilerParams(dimension_semantics=("parallel", "parallel", "arbitrary"),
                                             vmem_limit_bytes=MM_VMEM_LIMIT),
    )(a["a"], b["a"], *([] if after is None else [after]))


def mm_res_mod(name, a, w, h, gate, gain, shift, scale):
    bsz, t_len, kk = a.shape
    tm = 512
    per = t_len // tm

    def body(a_ref, w_ref, h_ref, gate_ref, gain_ref, shift_ref, scale_ref, y_ref, h1_ref, hn_ref):
        y = jnp.dot(a_ref[...].astype(BF16), w_ref[...].astype(BF16), preferred_element_type=F32)
        h1 = h_ref[...] + gate_ref[...] * y
        y_ref[...] = y.astype(BF16)
        h1_ref[...] = h1
        hn_ref[...] = _mod(h1, gain_ref[...], shift_ref[...], scale_ref[...]).astype(BF16)

    def tile(width):
        return pl.BlockSpec((None, tm, width), lambda i: (i // per, i % per, 0))

    per_ex = pl.BlockSpec((None, 1, D), lambda i: (i // per, 0, 0))
    return pl.pallas_call(
        body, name=name, grid=(bsz * per,),
        in_specs=[tile(kk), pl.BlockSpec((kk, D), lambda i: (0, 0)), tile(D), per_ex, pl.BlockSpec((1, D), lambda i: (0, 0)),
                  per_ex, per_ex],
        out_specs=[tile(D)] * 3,
        out_shape=[jax.ShapeDtypeStruct((bsz, t_len, D), BF16), jax.ShapeDtypeStruct((bsz, t_len, D), F32),
                   jax.ShapeDtypeStruct((bsz, t_len, D), BF16)],
        compiler_params=pltpu.CompilerParams(dimension_semantics=("parallel",), vmem_limit_bytes=MM_VMEM_LIMIT),
    )(a, w, h, gate, gain, shift, scale)


def X(arr, w=None, co=0, ro=0, split=1, planes=False):
    return dict(a=arr, w=arr.shape[-1] if w is None else w, co=co, ro=ro, split=2 if planes else split,
                mode="planes" if planes else "cols")


def P(arr, per_example=False, w=None, split=1, rows=False):
    return dict(a=arr, e=per_example, w=arr.shape[-1] if w is None else w, split=arr.shape[-2] if rows else split,
                mode="rows" if rows else "cols")


def _pieces(ref, s):
    if s["mode"] == "planes":
        return [ref[0], ref[1]]
    if s["mode"] == "rows":
        return [ref[i:i + 1, :] for i in range(s["split"])]
    w = ref.shape[-1] // s["split"]
    return [ref[:, i * w:(i + 1) * w] for i in range(s["split"])]


def _store(ref, pieces, s, accumulate=False):
    w = ref.shape[-1] // len(pieces)
    for i, p in enumerate(pieces):
        at = (i,) if s["mode"] == "planes" else (slice(i, i + 1),) if s["mode"] == "rows" else (slice(None), slice(i * w, (i + 1) * w))
        if accumulate:
            ref[at] += p.astype(ref.dtype)
        else:
            ref[at] = p.astype(ref.dtype)


def rowwise(name, f, xs, ps, *, tm, nt, nc=1, outs=None, douts=None, dx=None, dp=None):
    bsz = xs[0]["a"].shape[0]
    fwd = douts is None
    nx, np_ = len(xs), len(ps)
    douts = [] if fwd else douts
    dx = {} if fwd else dx
    dp = [] if fwd else dp

    def x_spec(s):
        if s["mode"] == "planes":
            return pl.BlockSpec((None, 2, tm, s["w"]), lambda c, b, t, s=s: (b, 0, t + s["ro"], c + s["co"]))
        return pl.BlockSpec((None, tm, s["w"]), lambda c, b, t, s=s: (b, t + s["ro"], c + s["co"]))

    def x_out(s, dt):
        if s["mode"] == "planes":
            return (jax.ShapeDtypeStruct((bsz, 2, nt * tm, nc * s["w"]), dt),
                    pl.BlockSpec((None, 2, tm, s["w"]), lambda c, b, t: (b, 0, t, c)))
        return (jax.ShapeDtypeStruct((bsz, nt * tm, nc * s["w"]), dt), pl.BlockSpec((None, tm, s["w"]), lambda c, b, t: (b, t, c)))

    def p_spec(s):
        r = s["a"].shape[-2]
        if s["e"]:
            return pl.BlockSpec((None, r, s["w"]), lambda c, b, t: (b, 0, c))
        return pl.BlockSpec((r, s["w"]), lambda c, b, t: (0, c))

    in_specs = [x_spec(s) for s in xs] + [p_spec(s) for s in ps] + [x_spec(s) for s in douts]
    operands = [s["a"] for s in xs] + [s["a"] for s in ps] + [s["a"] for s in douts]
    if fwd:
        out_modes = [dict(mode="cols", split=sp) for (_, _, sp) in outs]
        out_shape = [jax.ShapeDtypeStruct((bsz, nt * tm, nc * w), dt) for (w, dt, _) in outs]
        out_specs = [pl.BlockSpec((None, tm, w), lambda c, b, t: (b, t, c)) for (w, _, _) in outs]
    else:
        dx_outs = [x_out(xs[i], dt) for i, dt in dx.items()]
        out_shape, out_specs = [o[0] for o in dx_outs], [o[1] for o in dx_outs]
        for j in dp:
            s = ps[j]
            r = s["a"].shape[-2]
            if s["e"]:
                out_shape.append(jax.ShapeDtypeStruct((bsz, r, nc * s["w"]), F32))
                out_specs.append(pl.BlockSpec((None, r, s["w"]), lambda c, b, t: (b, 0, c)))
            else:
                out_shape.append(jax.ShapeDtypeStruct((r, nc * s["w"]), F32))
                out_specs.append(pl.BlockSpec((r, s["w"]), lambda c, b, t: (0, c)))

    def body(*refs):
        x_refs, p_refs = refs[:nx], refs[nx:nx + np_]
        d_refs = refs[nx + np_:nx + np_ + len(douts)]
        o_refs = refs[nx + np_ + len(douts):]
        xv = [[p.astype(F32) for p in _pieces(r, s)] for r, s in zip(x_refs, xs)]
        pv = [[p.astype(F32) for p in _pieces(r, s)] for r, s in zip(p_refs, ps)]
        if fwd:
            for r, pieces, s in zip(o_refs, f(xv, pv), out_modes):
                _store(r, pieces, s)
            return
        _, vjp = jax.vjp(f, xv, pv)
        cot = [[p.astype(F32) for p in _pieces(r, s)] for r, s in zip(d_refs, douts)]
        dxv, dpv = vjp(cot)
        for r, i in zip(o_refs, dx):
            _store(r, dxv[i], xs[i])
        b, t = pl.program_id(1), pl.program_id(2)
        for r, j in zip(o_refs[len(dx):], dp):
            first = (t == 0) if ps[j]["e"] else jnp.logical_and(b == 0, t == 0)

            @pl.when(first)
            def _(r=r, j=j):
                _store(r, dpv[j], ps[j])

            @pl.when(jnp.logical_not(first))
            def _(r=r, j=j):
                _store(r, dpv[j], ps[j], accumulate=True)

    res = pl.pallas_call(
        body, name=name, grid=(nc, bsz, nt), in_specs=in_specs, out_specs=out_specs, out_shape=out_shape,
        compiler_params=pltpu.CompilerParams(dimension_semantics=("arbitrary", "arbitrary", "arbitrary")),
    )(*operands)
    return res


def _keep_rows(a, shift, keep):
    n = a.shape[0]
    t = lax.broadcasted_iota(jnp.int32, a.shape, 0)
    return jnp.where(keep(t, n), pltpu.roll(a, shift % n, 0), 0.0)


def _shift_pair(step, keep_prev=None, keep_next=None):
    @jax.custom_vjp
    def prev(a):
        if keep_prev is None:
            return jnp.concatenate([jnp.zeros((step,) + a.shape[1:], a.dtype), a[:a.shape[0] - step]], axis=0)
        return _keep_rows(a, step, keep_prev)

    @jax.custom_vjp
    def nxt(a):
        if keep_next is None:
            return jnp.concatenate([a[step:], jnp.zeros((step,) + a.shape[1:], a.dtype)], axis=0)
        return _keep_rows(a, -step, keep_next)

    prev.defvjp(lambda a: (prev(a), None), lambda _, g: (nxt(g),))
    nxt.defvjp(lambda a: (nxt(a), None), lambda _, g: (prev(g),))
    return prev, nxt


prev_tok, next_tok = _shift_pair(1, lambda t, n: t % GRID_W != 0, lambda t, n: t % GRID_W != GRID_W - 1)
prev_row, next_row = _shift_pair(GRID_W)


@jax.custom_vjp
def bdot(a, w):
    return jnp.dot(a.astype(BF16), w.astype(BF16), preferred_element_type=F32)


def _bdot_bwd(res, g):
    a, w = res
    gb = g.astype(BF16)
    da = lax.dot_general(gb, w.astype(BF16), (((1,), (1,)), ((), ())), preferred_element_type=F32)
    dw = lax.dot_general(a.astype(BF16), gb, (((0,), (0,)), ((), ())), preferred_element_type=F32)
    return da, dw


bdot.defvjp(lambda a, w: (bdot(a, w), (a, w)), _bdot_bwd)


@jax.custom_vjp
def log_sigmoid(z):
    return jnp.minimum(z, 0.0) - jnp.log(1.0 + jnp.exp(-jnp.abs(z)))


def _lsig_bwd(z, g):
    e = jnp.exp(-jnp.abs(z))
    return (g * jnp.where(z >= 0, e, 1.0) / (1.0 + e),)


log_sigmoid.defvjp(lambda z: (log_sigmoid(z), z), _lsig_bwd)


def silu(x):
    return x * jax.nn.sigmoid(x)


def _rms(x):
    return x * lax.rsqrt(jnp.mean(x * x, axis=-1, keepdims=True) + EPS)


def _mod(x, gain, shift, scale):
    return _rms(x) * gain * (1.0 + scale) + shift


def f_mod(xs, ps):
    ((h,),), ((gain,), (shift,), (scale,)) = xs, ps
    return [[_mod(h, gain, shift, scale)], [h]]


def f_res_mod(xs, ps):
    ((h,), (y,)), ((gate,), (gain,), (shift,), (scale,)) = xs, ps
    h1 = h + gate * y
    return [[h1], [_mod(h1, gain, shift, scale)]]


def f_ffn_mid(xs, ps):
    ((ua, ug),), ((w0a, w0g), (w1a, w1g), (w2a, w2g), (ba, bg)) = xs, ps
    a = w0a * prev_row(ua) + w1a * ua + w2a * next_row(ua) + ba
    g = w0g * prev_row(ug) + w1g * ug + w2g * next_row(ug) + bg
    return [[a * silu(g)]]


def f_sc_mid(xs, ps):
    ((bg, cg, v),), ((w0,), (w1,), (w2,)) = xs, ps
    z = cg * v
    return [[bg * (w0 * prev_tok(z) + w1 * z + w2 * next_tok(z))]]


def f_decay(xs, ps):
    ((a,),), ((wd,), (bd,)) = xs, ps
    return [[log_sigmoid(bdot(a, wd) + bd) / TAU]]


def f_gla_post(xs, ps):
    (of, ob, g), ((gain,),) = xs, ps
    return [[_rms(a + b) * gain * silu(c) for a, b, c in zip(of, ob, g)]]


NCH = TT // CHUNK
CTX_CH = CTX // CHUNK
_NT = (((1,), (1,)), ((), ()))
_TN = (((0,), (0,)), ((), ()))
_NN = (((1,), (0,)), ((), ()))


def _chunk_of(d, j):
    return jnp.where(d == 0, j, jnp.where(j < CTX_CH, CTX_CH - 1 - j, NCH + CTX_CH - 1 - j))


def _dot(a, b, dn):
    return lax.dot_general(a, b, dn, preferred_element_type=F32)


def _cumsum_rows(g, suffix):
    n = g.shape[0]
    row = lax.broadcasted_iota(jnp.int32, g.shape, 0)
    s = 1
    while s < n:
        if suffix:
            g = g + jnp.where(row < n - s, pltpu.roll(g, n - s, 0), 0.0)
        else:
            g = g + jnp.where(row >= s, pltpu.roll(g, s, 0), 0.0)
        s *= 2
    return g


def _causal(backward):
    row = lax.broadcasted_iota(jnp.int32, (CHUNK, CHUNK), 0)
    col = lax.broadcasted_iota(jnp.int32, (CHUNK, CHUNK), 1)
    return col >= row if backward else col <= row


def _gla_in_specs(bsz, rev):
    def blk(d, j):
        return _chunk_of(d, (NCH - 1 - j) if rev else j)

    return [
        pl.BlockSpec((bsz, CHUNK, KD), lambda d, j: (0, blk(d, j), 0)),
        pl.BlockSpec((bsz, CHUNK, KD), lambda d, j: (0, blk(d, j), 1)),
        pl.BlockSpec((bsz, CHUNK, VD), lambda d, j: (0, blk(d, j), 1)),
        pl.BlockSpec((bsz, CHUNK, KD), lambda d, j: (0, blk(d, j), d)),
    ], blk


def gla_fwd(pcat, la):
    bsz = pcat.shape[0]
    in_specs, blk = _gla_in_specs(bsz, False)

    def body(q_ref, k_ref, v_ref, la_ref, o_ref, s_ref, st):
        d, j = pl.program_id(0), pl.program_id(1)

        @pl.when(j == 0)
        def _():
            st[...] = jnp.zeros_like(st)

        s_ref[...] = st[...]

        def scan(backward):
            causal = _causal(backward)
            for e in range(bsz):
                g_all = la_ref[e]
                b_all = _cumsum_rows(g_all, backward)
                bl_all = jnp.sum(g_all, axis=0, keepdims=True)
                qs_all = (q_ref[e].astype(F32) * (HK ** -0.5) * jnp.exp(b_all)).astype(BF16)
                ks_all = (k_ref[e] * jnp.exp(-b_all)).astype(BF16)
                kd_all = (k_ref[e] * jnp.exp(bl_all - b_all)).astype(BF16)
                el_all = jnp.exp(bl_all)
                for h in range(HEADS):
                    ks_, vs_ = slice(h * HK, (h + 1) * HK), slice(h * HV, (h + 1) * HV)
                    qs, ks, kd, v = qs_all[:, ks_], ks_all[:, ks_], kd_all[:, ks_], v_ref[e, :, vs_].astype(BF16)
                    s = st[e, h]
                    att = jnp.where(causal, _dot(qs, ks, _NT), 0.0).astype(BF16)
                    o_ref[e, :, vs_] = _dot(qs, s.astype(BF16), _NT) + _dot(att, v, _NN)
                    st[e, h] = el_all[:, ks_] * s + _dot(v, kd, _TN)

        @pl.when(d == 0)
        def _():
            scan(False)

        @pl.when(d == 1)
        def _():
            scan(True)

    return pl.pallas_call(
        body, name="gla_fwd", grid=(2, NCH), in_specs=in_specs,
        out_specs=[pl.BlockSpec((bsz, CHUNK, VD), lambda d, j: (0, blk(d, j), d)),
                   pl.BlockSpec((bsz, None, None, HEADS, HV, HK), lambda d, j: (0, d, j, 0, 0, 0))],
        out_shape=[jax.ShapeDtypeStruct((bsz, TT, 2 * VD), F32), jax.ShapeDtypeStruct((bsz, 2, NCH, HEADS, HV, HK), F32)],
        scratch_shapes=[pltpu.VMEM((bsz, HEADS, HV, HK), F32)],
        compiler_params=pltpu.CompilerParams(dimension_semantics=("arbitrary", "arbitrary")),
    )(pcat, pcat, pcat, la)


def gla_bwd(pcat, la, s_all, do):
    bsz = pcat.shape[0]
    in_specs, blk = _gla_in_specs(bsz, True)
    in_specs += [
        pl.BlockSpec((bsz, None, None, HEADS, HV, HK), lambda d, j: (0, d, NCH - 1 - j, 0, 0, 0)),
        pl.BlockSpec((bsz, CHUNK, VD), lambda d, j: (0, jnp.maximum(blk(d, j) - CTX_CH, 0), 0)),
    ]

    def body(q_ref, k_ref, v_ref, la_ref, s_ref, do_ref, dq_ref, dk_ref, dv_ref, dla_ref, dst):
        d, j = pl.program_id(0), pl.program_id(1)

        @pl.when(j == 0)
        def _():
            dst[...] = jnp.zeros_like(dst)

        latent = blk(d, j) >= CTX_CH
        scale = HK ** -0.5

        def scan(backward):
            causal = _causal(backward)
            for e in range(bsz):
                g_all = la_ref[e]
                b_all = _cumsum_rows(g_all, backward)
                bl_all = jnp.sum(g_all, axis=0, keepdims=True)
                ex_all, ei_all, ed_all, el_all = jnp.exp(b_all), jnp.exp(-b_all), jnp.exp(bl_all - b_all), jnp.exp(bl_all)
                qs_all, ks_all, kd_all = q_ref[e].astype(F32) * scale * ex_all, k_ref[e] * ei_all, k_ref[e] * ed_all
                qsb_all, ksb_all, kdb_all = qs_all.astype(BF16), ks_all.astype(BF16), kd_all.astype(BF16)
                db_parts, dbl_parts = [], []
                for h in range(HEADS):
                    ks_, vs_ = slice(h * HK, (h + 1) * HK), slice(h * HV, (h + 1) * HV)
                    qs, ks, kd, el = qs_all[:, ks_], ks_all[:, ks_], kd_all[:, ks_], el_all[:, ks_]
                    qsb, ksb, kdb, v = qsb_all[:, ks_], ksb_all[:, ks_], kdb_all[:, ks_], v_ref[e, :, vs_].astype(BF16)
                    s, ds1 = s_ref[e, h], dst[e, h]
                    sb, ds1b = s.astype(BF16), ds1.astype(BF16)
                    dob = jnp.where(latent, do_ref[e, :, vs_], 0.0).astype(BF16)
                    att = jnp.where(causal, _dot(qsb, ksb, _NT), 0.0).astype(BF16)
                    datt = jnp.where(causal, _dot(dob, v, _NT), 0.0).astype(BF16)
                    dqs = _dot(dob, sb, _NN) + _dot(datt, ksb, _NN)
                    dks = _dot(datt, qsb, _TN)
                    dv_ref[e, :, vs_] = (_dot(att, dob, _TN) + _dot(kdb, ds1b, _NT)).astype(BF16)
                    dkd = _dot(v, ds1b, _NN)
                    dst[e, h] = _dot(dob, qsb, _TN) + el * ds1
                    del_ = jnp.sum(s * ds1, axis=0, keepdims=True)
                    dq_ref[e, :, ks_] = (dqs * ex_all[:, ks_] * scale).astype(BF16)
                    dk_ref[e, :, ks_] = (dks * ei_all[:, ks_] + dkd * ed_all[:, ks_]).astype(BF16)
                    db_parts.append(dqs * qs - dks * ks - dkd * kd)
                    dbl_parts.append(jnp.sum(dkd * kd, axis=0, keepdims=True) + del_ * el)
                dla_ref[e] = _cumsum_rows(jnp.concatenate(db_parts, -1), not backward) + jnp.concatenate(dbl_parts, -1)

        @pl.when(d == 0)
        def _():
            scan(False)

        @pl.when(d == 1)
        def _():
            scan(True)

    return pl.pallas_call(
        body, name="gla_bwd", grid=(2, NCH), in_specs=in_specs,
        out_specs=[pl.BlockSpec((None, bsz, CHUNK, KD), lambda d, j: (d, 0, blk(d, j), 0)),
                   pl.BlockSpec((None, bsz, CHUNK, KD), lambda d, j: (d, 0, blk(d, j), 0)),
                   pl.BlockSpec((None, bsz, CHUNK, VD), lambda d, j: (d, 0, blk(d, j), 0)),
                   pl.BlockSpec((bsz, CHUNK, KD), lambda d, j: (0, blk(d, j), d))],
        out_shape=[jax.ShapeDtypeStruct((2, bsz, TT, KD), BF16), jax.ShapeDtypeStruct((2, bsz, TT, KD), BF16),
                   jax.ShapeDtypeStruct((2, bsz, TT, VD), BF16), jax.ShapeDtypeStruct((bsz, TT, 2 * KD), F32)],
        scratch_shapes=[pltpu.VMEM((bsz, HEADS, HV, HK), F32)],
        compiler_params=pltpu.CompilerParams(dimension_semantics=("arbitrary", "arbitrary")),
    )(pcat, pcat, pcat, la, s_all, do)


def gla_combine(dq2, dk2, dv2, dgate, dpa):
    bsz = dgate.shape[0]
    tm = CTX

    def body(dq_ref, dk_ref, dv_ref, dg_ref, dpa_ref, o_ref):
        t = pl.program_id(1)
        o_ref[:, 0:KD] = (dq_ref[0].astype(F32) + dq_ref[1].astype(F32)).astype(BF16)
        o_ref[:, KD:2 * KD] = (dk_ref[0].astype(F32) + dk_ref[1].astype(F32)).astype(BF16)
        o_ref[:, 2 * KD:2 * KD + VD] = (dv_ref[0].astype(F32) + dv_ref[1].astype(F32)).astype(BF16)
        o_ref[:, 2 * KD + VD:2 * KD + 2 * VD] = jnp.where(t > 0, dg_ref[...], 0).astype(BF16)
        o_ref[:, 2 * KD + 2 * VD:] = dpa_ref[...].astype(BF16)

    return pl.pallas_call(
        body, name="gla_combine", grid=(bsz, TT // tm),
        in_specs=[pl.BlockSpec((2, None, tm, KD), lambda b, t: (0, b, t, 0)),
                  pl.BlockSpec((2, None, tm, KD), lambda b, t: (0, b, t, 0)),
                  pl.BlockSpec((2, None, tm, VD), lambda b, t: (0, b, t, 0)),
                  pl.BlockSpec((None, tm, VD), lambda b, t: (b, jnp.maximum(t - 1, 0), 0)),
                  pl.BlockSpec((None, tm, 128), lambda b, t: (b, t, 0))],
        out_specs=pl.BlockSpec((None, tm, GLA_IN_PAD), lambda b, t: (b, t, 0)),
        out_shape=jax.ShapeDtypeStruct((bsz, TT, GLA_IN_PAD), BF16),
        compiler_params=pltpu.CompilerParams(dimension_semantics=("arbitrary", "arbitrary")),
    )(dq2, dk2, dv2, dgate, dpa)


def final_loss(h1, fo, gate, gain, tgt):
    bsz, t_len, _ = h1.shape
    tm = 512

    def body(h_ref, f_ref, gate_ref, gain_ref, tgt_ref, loss_ref, dh_ref, df_ref, dgate_ref, dgain_ref):
        b, t = pl.program_id(0), pl.program_id(1)
        target = tgt_ref[...]

        def core(h, fo_, gate_, gain_):
            e = _rms(h + gate_ * fo_) * gain_ - target
            return jnp.sum(0.5 * jnp.sum(e * e, axis=-1, keepdims=True) / D, axis=0, keepdims=True)

        loss, vjp = jax.vjp(core, h_ref[...], f_ref[...], gate_ref[...], gain_ref[...])
        dh, df, dgate, dgain = vjp(jnp.ones((1, 1), F32))
        dh_ref[...] = dh
        df_ref[...] = df.astype(BF16)
        first = jnp.logical_and(b == 0, t == 0)

        @pl.when(first)
        def _():
            loss_ref[...] = jnp.broadcast_to(loss, loss_ref.shape)
            dgain_ref[...] = dgain

        @pl.when(jnp.logical_not(first))
        def _():
            loss_ref[...] += jnp.broadcast_to(loss, loss_ref.shape)
            dgain_ref[...] += dgain

        @pl.when(t == 0)
        def _():
            dgate_ref[...] = dgate

        @pl.when(t > 0)
        def _():
            dgate_ref[...] += dgate

    tile = pl.BlockSpec((None, tm, D), lambda b, t: (b, t, 0))
    per_ex = pl.BlockSpec((None, 1, D), lambda b, t: (b, 0, 0))
    shared = pl.BlockSpec((1, D), lambda b, t: (0, 0))
    return pl.pallas_call(
        body, name="final_loss", grid=(bsz, t_len // tm),
        in_specs=[tile, tile, per_ex, shared, tile],
        out_specs=[pl.BlockSpec((8, 128), lambda b, t: (0, 0)), tile, tile, per_ex, shared],
        out_shape=[jax.ShapeDtypeStruct((8, 128), F32), jax.ShapeDtypeStruct(h1.shape, F32),
                   jax.ShapeDtypeStruct(h1.shape, BF16), jax.ShapeDtypeStruct((bsz, 1, D), F32),
                   jax.ShapeDtypeStruct((1, D), F32)],
        compiler_params=pltpu.CompilerParams(dimension_semantics=("arbitrary", "arbitrary")),
    )(h1, fo, gate, gain, tgt)


ADA_ROWS = 24
ADA_CTX_ROW = 16
ADA_COLS = 6 * D // N_DEV


def ada_fwd(cond, w, b):
    def body(c_ref, w_ref, b_ref, o_ref):
        s = silu(c_ref[...]).astype(BF16)
        o_ref[...] = jnp.dot(s, w_ref[...].astype(BF16), preferred_element_type=F32) + b_ref[...]

    return pl.pallas_call(
        body, name="ada_fwd", grid=(2,),
        in_specs=[pl.BlockSpec((ADA_ROWS, D), lambda i: (0, 0)), pl.BlockSpec((None, D, ADA_COLS), lambda i: (i, 0, 0)),
                  pl.BlockSpec((None, 1, ADA_COLS), lambda i: (i, 0, 0))],
        out_specs=pl.BlockSpec((None, ADA_ROWS, ADA_COLS), lambda i: (i, 0, 0)),
        out_shape=jax.ShapeDtypeStruct((2, ADA_ROWS, ADA_COLS), F32),
    )(cond, w, b)


def ada_bwd(cond, dm_mine, dm_full, w):
    def body(c_ref, dm_ref, dmf_ref, w_ref, gw_ref, gb_ref, cp_ref):
        i = pl.program_id(0)
        s = silu(c_ref[...]).astype(BF16)
        dm = dm_ref[...].astype(BF16)
        gw_ref[...] = _dot(s, dm, _TN)
        gb_ref[...] = jnp.sum(dmf_ref[...], axis=0, keepdims=True)

        @pl.when(i == 0)
        def _():
            cp_ref[...] = _dot(dm_ref[ADA_CTX_ROW:, :].astype(BF16), w_ref[...].astype(BF16), _NT)

    return pl.pallas_call(
        body, name="ada_bwd", grid=(2,),
        in_specs=[pl.BlockSpec((ADA_ROWS, D), lambda i: (0, 0)), pl.BlockSpec((None, ADA_ROWS, ADA_COLS), lambda i: (i, 0, 0)),
                  pl.BlockSpec((None, ADA_ROWS, 6 * D), lambda i: (i, 0, 0)), pl.BlockSpec((None, D, ADA_COLS), lambda i: (i, 0, 0))],
        out_specs=[pl.BlockSpec((None, D, ADA_COLS), lambda i: (i, 0, 0)), pl.BlockSpec((None, 1, 6 * D), lambda i: (i, 0, 0)),
                   pl.BlockSpec((ADA_ROWS - ADA_CTX_ROW, D), lambda i: (0, 0))],
        out_shape=[jax.ShapeDtypeStruct((2, D, ADA_COLS), F32), jax.ShapeDtypeStruct((2, 1, 6 * D), F32),
                   jax.ShapeDtypeStruct((ADA_ROWS - ADA_CTX_ROW, D), F32)],
        compiler_params=pltpu.CompilerParams(dimension_semantics=("arbitrary",)),
    )(cond, dm_mine, dm_full, w)


def cctx_grad(parts, c_ctx):
    def body(p_ref, c_ref, o_ref):
        tot = p_ref[0:1, :]
        for i in range(1, N_DEV):
            tot = tot + p_ref[i:i + 1, :]
        c = c_ref[...]
        sg = jax.nn.sigmoid(c)
        o_ref[...] = tot * sg * (1.0 + c * (1.0 - sg))

    return pl.pallas_call(body, name="cctx_grad", out_shape=jax.ShapeDtypeStruct((1, D), F32))(parts, c_ctx)


def _row_tile(r):
    for t in (512, 256, 128, 80, 64, 40, 32, 16, 8):
        if r % t == 0:
            return t
    return r


def _slot_sum(ref):
    tot = ref[0].astype(F32)
    for i in range(1, ref.shape[0]):
        tot = tot + ref[i].astype(F32)
    return tot


def sum_slots(name, x):
    s, r, c = x.shape
    tr = _row_tile(r)

    def body(x_ref, o_ref):
        o_ref[...] = _slot_sum(x_ref)

    return pl.pallas_call(
        body, name=name, grid=(r // tr,), in_specs=[pl.BlockSpec((s, tr, c), lambda i: (0, i, 0))],
        out_specs=pl.BlockSpec((tr, c), lambda i: (i, 0)), out_shape=jax.ShapeDtypeStruct((r, c), F32),
    )(x)


def _adamw_update(gv, w_ref, m_ref, v_ref, go_ref, d_ref, mo_ref, vo_ref):
    mn = B1 * m_ref[...] + (1.0 - B1) * gv
    vn = B2 * v_ref[...] + (1.0 - B2) * jnp.square(gv)
    m_hat = mn / (1.0 - B1 ** STEP)
    v_hat = vn / (1.0 - B2 ** STEP)
    go_ref[...] = gv
    d_ref[...] = -LR * (m_hat / (jnp.sqrt(v_hat) + AEPS) + WD * w_ref[...])
    mo_ref[...] = mn
    vo_ref[...] = vn


def adamw_slots(name, w, land, sent, me1, m, v, layer, into=None):
    r, c = w.shape[-2:]
    tr = _row_tile(r)
    into = [] if into is None else list(into)

    def body(me_ref, w_ref, land_ref, own_ref, m_ref, v_ref, *rest):
        own = own_ref[...].astype(F32)
        gv = jnp.where(me_ref[0] == 0, own, land_ref[0].astype(F32))
        for s in range(1, N_DEV):
            gv = gv + jnp.where(me_ref[0] == s, own, land_ref[s].astype(F32))
        _adamw_update(gv, w_ref, m_ref, v_ref, *rest[len(into):])

    slab = pl.BlockSpec((None, tr, c), lambda i, me: (layer, i, 0))
    return pl.pallas_call(
        body, name=name, out_shape=[jax.ShapeDtypeStruct(w.shape, F32)] * 4,
        grid_spec=pltpu.PrefetchScalarGridSpec(
            num_scalar_prefetch=1, grid=(r // tr,),
            in_specs=[slab, pl.BlockSpec((N_DEV, tr, c), lambda i, me: (0, i, 0)),
                      pl.BlockSpec((None, tr, c), lambda i, me: (me[0], i, 0)), slab, slab]
            + [pl.BlockSpec(memory_space=pl.ANY)] * len(into),
            out_specs=[slab] * 4),
        input_output_aliases={6 + k: k for k in range(len(into))},
    )(me1, w, land, sent, m, v, *into)


def adamw(name, w, g, m, v, layer=None):
    r, c = w.shape[-2:]
    tr = _row_tile(r)
    stacked = g.ndim == 3

    def body(w_ref, g_ref, m_ref, v_ref, *outs):
        _adamw_update(_slot_sum(g_ref) if stacked else g_ref[...], w_ref, m_ref, v_ref, *outs)

    tile = pl.BlockSpec((tr, c), lambda i: (i, 0))
    slab = tile if layer is None else pl.BlockSpec((None, tr, c), lambda i: (layer, i, 0))
    g_spec = pl.BlockSpec((g.shape[0], tr, c), lambda i: (0, i, 0)) if stacked else tile
    return pl.pallas_call(
        body, name=name, grid=(r // tr,), in_specs=[slab, g_spec, slab, slab], out_specs=[tile] * 4,
        out_shape=[jax.ShapeDtypeStruct((r, c), F32)] * 4,
    )(w, g, m, v)


def _place():
    return lax.axis_index("x"), lax.axis_index("y"), lax.axis_index("c")


def all_gather(name, x):
    r, c = x.shape
    space = pltpu.VMEM

    def body(x_ref, out_ref, send_sems, recv_sems, local_sem):
        px, py, pc = _place()
        me, sibling = (px, py, pc), (px, py, 1 - pc)
        chips = [(1 - px, py), (px, 1 - py), (1 - px, 1 - py)]

        def rows(qx, qy, qc):
            return out_ref.at[pl.ds((4 * qx + 2 * qy + qc) * r, r), :]

        def copy(k, block, to, src=None):
            return pltpu.make_async_remote_copy(
                src_ref=rows(*block) if src is None else src, dst_ref=rows(*block),
                send_sem=send_sems.at[k], recv_sem=recv_sems.at[k], device_id=to, device_id_type=MESH)

        mine = pltpu.make_async_copy(x_ref, rows(*me), local_sem)
        mine.start()
        first = [copy(0, me, sibling, src=x_ref)]
        first += [copy(1 + j, me, (*chip, pc), src=x_ref) for j, chip in enumerate(chips)]
        for cp in first:
            cp.start()
        passed = [copy(4 + j, (*chip, pc), sibling) for j, chip in enumerate(chips)]
        for j, chip in enumerate(chips):
            copy(1 + j, (*chip, pc), me).wait_recv()
            passed[j].start()
        copy(0, sibling, me).wait_recv()
        for j, chip in enumerate(chips):
            copy(4 + j, (*chip, 1 - pc), me).wait_recv()
        for cp in first + passed:
            cp.wait_send()
        mine.wait()

    return pl.pallas_call(
        body, name=name, out_shape=jax.ShapeDtypeStruct((N_DEV * r, c), x.dtype),
        in_specs=[pl.BlockSpec(memory_space=space)], out_specs=pl.BlockSpec(memory_space=space),
        scratch_shapes=[pltpu.SemaphoreType.DMA((7,)), pltpu.SemaphoreType.DMA((7,)), pltpu.SemaphoreType.DMA],
    )(x)


_HBM = pl.BlockSpec(memory_space=pltpu.HBM)
_SEM = pl.BlockSpec(memory_space=pltpu.SEMAPHORE)
_EFFECT = pltpu.SideEffectType.DATAFLOW_SIDE_EFFECTING


def _peers():
    px, py, pc = _place()
    return [(1 - px if k & 4 else px, 1 - py if k & 2 else py, 1 - pc if k & 1 else pc) for k in range(1, N_DEV)]


def _slot(dev):
    return 4 * dev[0] + 2 * dev[1] + dev[2]


def _split_copies(src_refs, land_refs, send_sems, recv_sems, gather):
    me = _slot(_place())
    return [pltpu.make_async_remote_copy(
        src_ref=src if gather else src.at[_slot(peer)], dst_ref=land.at[me],
        send_sem=send_sems.at[a * (N_DEV - 1) + k], recv_sem=recv_sems.at[a * (N_DEV - 1) + k],
        device_id=peer, device_id_type=MESH)
        for a, (src, land) in enumerate(zip(src_refs, land_refs)) for k, peer in enumerate(_peers())]


def exchange_start(name, srcs, gather, after):
    n = len(srcs)
    lands = [pltpu.HBM((N_DEV,) + s.shape if gather else s.shape, s.dtype) for s in srcs]

    def body(*refs):
        send_sems, recv_sems = refs[2 * n + 1:2 * n + 3]
        for cp in _split_copies(refs[:n], refs[n:2 * n], send_sems, recv_sems, gather):
            cp.start()
        refs[-1][...] = jnp.zeros_like(refs[-1])

    sems = pltpu.SemaphoreType.DMA((n * (N_DEV - 1),))
    res = pl.pallas_call(
        body, name=name,
        out_shape=(sems, sems, *[pltpu.HBM(s.shape, s.dtype) for s in srcs], *lands, jax.ShapeDtypeStruct((8, 128), F32)),
        in_specs=(_HBM,) * (2 * n) + (pl.BlockSpec(memory_space=pl.ANY),),
        out_specs=(_SEM, _SEM) + (_HBM,) * (2 * n) + (pl.BlockSpec(memory_space=pltpu.VMEM),),
        input_output_aliases={i: 2 + i for i in range(2 * n)},
        compiler_params=pltpu.CompilerParams(has_side_effects=_EFFECT),
    )(*[pltpu.with_memory_space_constraint(s, pltpu.HBM) for s in srcs],
      *[pltpu.with_memory_space_constraint(lax.empty(ld.shape, ld.dtype), pltpu.HBM) for ld in lands], after)
    return res[0], res[1], list(res[2:2 + n]), list(res[2 + n:2 + 2 * n]), res[-1]


def exchange_wait(name, started, after, gather):
    send_sems, recv_sems, srcs, lands, _ = started
    n = len(srcs)
    after = list(after) if isinstance(after, (list, tuple)) else [after]

    def body(*refs):
        send_sems, recv_sems = refs[2 * n:2 * n + 2]
        for cp in _split_copies(refs[:n], refs[n:2 * n], send_sems, recv_sems, gather):
            cp.wait_send()
            cp.wait_recv()

    res = pl.pallas_call(
        body, name=name, out_shape=tuple(pltpu.HBM(a.shape, a.dtype) for a in srcs + lands),
        in_specs=(_HBM,) * (2 * n) + (_SEM, _SEM) + (pl.BlockSpec(memory_space=pl.ANY),) * len(after),
        out_specs=(_HBM,) * (2 * n), input_output_aliases={i: i for i in range(2 * n)},
        compiler_params=pltpu.CompilerParams(has_side_effects=_EFFECT),
    )(*srcs, *lands, send_sems, recv_sems, *after)
    return list(res[:n]), list(res[n:])


NCF = FFN_H // FFN_TC


def _size(shape):
    n = 1
    for s in shape:
        n *= s
    return n


def _padded_rows(n_elems, row_mult):
    return -(-n_elems // (D * row_mult)) * row_mult


def _pack_rows(arrs, dtype, row_mult):
    rows, offs, r0 = [], [], 0
    for a in arrs:
        flat = a.reshape(-1).astype(dtype)
        n = _padded_rows(flat.shape[0], row_mult)
        rows.append(jnp.pad(flat, (0, n * D - flat.shape[0])).reshape(n, D))
        offs.append(r0)
        r0 += n
    return jnp.concatenate(rows, 0), offs


def _unpack_rows(buf, offs, shapes):
    lead, out = buf.shape[:-2], []
    for o, shp in zip(offs, shapes):
        n = _size(shp)
        nr = -(-n // D)
        out.append(buf[..., o:o + nr, :].reshape(lead + (nr * D,))[..., :n].reshape(lead + tuple(shp)))
    return out


def _rows3(w):
    return [w[i:i + 1] for i in range(3)]


def f_mod1(xs, ps):
    return f_mod(xs, ps)[:1]


def kernel(x, c, ctx, c_ctx, ada_w, ada_b, norm_mix, norm_ffn, gla_w_in, gla_w_a2, gla_b_a, gla_head_norm, gla_w_out, sc_w_in, sc_conv_w, sc_w_out, ffn_w_up, ffn_conv_w, ffn_conv_b, ffn_w_down, final_norm, loss_target, m_c_ctx, m_ada_w, m_ada_b, m_norm_mix, m_norm_ffn, m_gla_w_in, m_gla_w_a2, m_gla_b_a, m_gla_head_norm, m_gla_w_out, m_sc_w_in, m_sc_conv_w, m_sc_w_out, m_ffn_w_up, m_ffn_conv_w, m_ffn_conv_b, m_ffn_w_down, m_final_norm, v_c_ctx, v_ada_w, v_ada_b, v_norm_mix, v_norm_ffn, v_gla_w_in, v_gla_w_a2, v_gla_b_a, v_gla_head_norm, v_gla_w_out, v_sc_w_in, v_sc_conv_w, v_sc_w_out, v_ffn_w_up, v_ffn_conv_w, v_ffn_conv_b, v_ffn_w_down, v_final_norm):
    names = ["c_ctx", "ada_w", "ada_b", "norm_mix", "norm_ffn", "gla_w_in", "gla_w_a2", "gla_b_a", "gla_head_norm",
             "gla_w_out", "sc_w_in", "sc_conv_w", "sc_w_out", "ffn_w_up", "ffn_conv_w", "ffn_conv_b", "ffn_w_down",
             "final_norm"]
    w_ = dict(zip(names, [c_ctx, ada_w, ada_b, norm_mix, norm_ffn, gla_w_in, gla_w_a2, gla_b_a, gla_head_norm, gla_w_out,
                          sc_w_in, sc_conv_w, sc_w_out, ffn_w_up, ffn_conv_w, ffn_conv_b, ffn_w_down, final_norm]))
    m_ = dict(zip(names, [m_c_ctx, m_ada_w, m_ada_b, m_norm_mix, m_norm_ffn, m_gla_w_in, m_gla_w_a2, m_gla_b_a,
                          m_gla_head_norm, m_gla_w_out, m_sc_w_in, m_sc_conv_w, m_sc_w_out, m_ffn_w_up, m_ffn_conv_w,
                          m_ffn_conv_b, m_ffn_w_down, m_final_norm]))
    v_ = dict(zip(names, [v_c_ctx, v_ada_w, v_ada_b, v_norm_mix, v_norm_ffn, v_gla_w_in, v_gla_w_a2, v_gla_b_a,
                          v_gla_head_norm, v_gla_w_out, v_sc_w_in, v_sc_conv_w, v_sc_w_out, v_ffn_w_up, v_ffn_conv_w,
                          v_ffn_conv_b, v_ffn_w_down, v_final_norm]))
    me = 4 * lax.axis_index("x") + 2 * lax.axis_index("y") + lax.axis_index("c")
    bsz = x.shape[0]
    tm = 256
    nt = SEQ // tm
    ctx_tiles = CTX // tm
    pe = functools.partial(P, per_example=True)

    groups = {"ffn1": [("ffn_w_up", 1), ("ffn_w_down", 1)], "sc": [("sc_w_in", 0), ("sc_w_out", 0)],
              "ffn0": [("ffn_w_up", 0), ("ffn_w_down", 0)], "gla": [("gla_w_in", 0), ("gla_w_out", 0)]}
    ag_groups = {"gin": [("gla_w_in", 0)], "ffn0": [("gla_w_out", 0), ("ffn_w_up", 0), ("ffn_w_down", 0)],
                 "sc": groups["sc"], "ffn1": groups["ffn1"]}
    ag_started = {}

    def start_gather(g, after):
        ag_started[g] = exchange_start(f"ag_{g}_start", [w_[n][i].astype(BF16) for n, i in ag_groups[g]], True, after)
        return ag_started[g][4]

    small_sharded = [c, gla_w_a2, gla_b_a, sc_conv_w, ffn_conv_w]
    pack0, offs0 = _pack_rows(small_sharded, F32, 8)
    g0 = all_gather("ag_small", pack0).reshape(N_DEV, pack0.shape[0], D)
    c_all, wa2_s, ba_s, scw_s, fcw_s = _unpack_rows(g0, offs0, [a.shape for a in small_sharded])
    w_a2 = wa2_s[:, 0].transpose(1, 2, 0, 3).reshape(2, RANK, KD)
    b_a = ba_s[:, 0].transpose(1, 0, 2).reshape(2, KD)
    sc_cw = scw_s[:, 0].transpose(1, 0, 2).reshape(3, D)
    ffn_cw = fcw_s.transpose(1, 2, 0, 3).reshape(2, 3, 2 * FFN_H)

    cond = jnp.concatenate([c_all.reshape(N_DEV * bsz, D), c_ctx[None], jnp.zeros((ADA_ROWS - N_DEV * bsz - 1, D), F32)], 0)
    b_mine = lax.dynamic_slice(ada_b, (0, me * ADA_COLS), (2, ADA_COLS)).reshape(2, 1, ADA_COLS)
    mod_part = ada_fwd(cond, ada_w, b_mine)
    mod = all_gather("ag_mod", mod_part.reshape(2 * ADA_ROWS, ADA_COLS))
    mod = mod.reshape(N_DEV, 2, ADA_ROWS, ADA_COLS).transpose(1, 2, 0, 3).reshape(2, ADA_ROWS, 6 * D)
    mods = lax.dynamic_slice(mod, (0, bsz * me, 0), (2, bsz, 6 * D))
    md = [[mods[i][:, k * D:(k + 1) * D].reshape(bsz, 1, D) for k in range(6)] for i in range(2)]
    mc = [mod[0, ADA_CTX_ROW, k * D:(k + 1) * D][None] for k in range(2)]

    tok = mod
    for g in ag_groups:
        tok = start_gather(g, tok)
    norm_mix = norm_mix + tok[0, 0]

    def gathered(g, after):
        mine, lands = exchange_wait(f"ag_{g}_wait", ag_started[g], after, True)
        return [lax.dynamic_update_index_in_dim(ld, mn, me, 0) for ld, mn in zip(lands, mine)]

    s_up, w_down = [None, None], [None, None]
    wd = jnp.zeros((128, 2 * KD), F32).at[:RANK, :KD].set(w_a2[0]).at[RANK:2 * RANK, KD:].set(w_a2[1])
    bd = b_a.reshape(1, 2 * KD)
    scw = _rows3(sc_cw)
    head_gain = gla_head_norm.reshape(1, HV)
    gains_mix = [norm_mix[i][None] for i in range(2)]
    gains_ffn = [norm_ffn[i][None] for i in range(2)]

    def tokens(a2d, t_len):
        return a2d.reshape(bsz, t_len, -1)

    def ffn_params(i):
        rows = [ffn_cw[i][t] for t in range(3)] + [ffn_conv_b[i]]
        return [P(a.reshape(2, FFN_H), w=FFN_TC, rows=True) for a in rows]

    def ffn_fwd(i, hn2):
        u = mm(f"ffn_up{i}", V(hn2, "tok"), V(s_up[i], "cols"), out="planes", out_dtype=BF16, planes_t=SEQ)
        act = rowwise(f"ffn_mid{i}", f_ffn_mid, [X(u, w=FFN_TC, planes=True)], ffn_params(i), tm=SEQ, nt=1, nc=NCF,
                      outs=[(FFN_TC, BF16, 1)])[0]
        return u, act

    def arrays(ps):
        return [p["a"] for p in ps]

    ps_in0 = [P(gains_mix[0]), pe(md[0][0]), pe(md[0][1])]
    ps_ctx = [P(gains_mix[0]), P(mc[0]), P(mc[1])]
    hn0 = rowwise("mod_in0", f_mod, [X(x)], ps_in0, tm=2 * tm, nt=nt // 2, outs=[(D, BF16, 1)])[0]
    hnc = rowwise("mod_ctx", f_mod, [X(ctx)], ps_ctx, tm=tm, nt=ctx_tiles, outs=[(D, BF16, 1)])[0]
    hcat = jnp.concatenate([hnc, hn0], axis=1)
    (s_gin,) = gathered("gin", hcat)
    w_gin = V(s_gin, "cols", width=GLA_IN_PAD)
    pcat = tokens(mm("gla_in", V(hcat, "tok"), w_gin, out_dtype=BF16), TT)
    pa_x = X(pcat, w=128, co=(GLA_IN_PAD - 128) // 128)
    la = rowwise("gla_decay", f_decay, [pa_x], [P(wd), P(bd)], tm=3 * tm, nt=TT // (3 * tm), outs=[(2 * KD, F32, 1)])[0]
    o2, s_all = gla_fwd(pcat, la)
    post_xs = [X(o2, w=VD, co=0, ro=ctx_tiles, split=HEADS), X(o2, w=VD, co=1, ro=ctx_tiles, split=HEADS),
               X(pcat, w=VD, co=2, ro=ctx_tiles, split=HEADS)]
    yin0 = rowwise("gla_post", f_gla_post, post_xs, [P(head_gain)], tm=tm, nt=nt, outs=[(VD, BF16, HEADS)])[0]
    s_gout, s_up[0], s_down0 = gathered("ffn0", yin0)
    w_gout, w_down[0] = s_gout.reshape(VD, D), s_down0.reshape(FFN_H, D)
    ps_mid0 = [pe(md[0][2]), P(gains_ffn[0]), pe(md[0][3]), pe(md[0][4])]
    y0, h1_0, hn2_0 = mm_res_mod("gla_out", yin0, w_gout, x, *arrays(ps_mid0))
    u0, act0 = ffn_fwd(0, hn2_0)
    ps_in1 = [pe(md[0][5]), P(gains_mix[1]), pe(md[1][0]), pe(md[1][1])]
    fo0, h2_0, hn1 = mm_res_mod("ffn_down0", act0, w_down[0], h1_0, *arrays(ps_in1))

    s_sin, s_sout = gathered("sc", hn1)
    w_sout = s_sout.reshape(D, D)
    p1 = tokens(mm("sc_in", V(hn1, "tok"), V(s_sin, "cols")), SEQ)
    sc_ps = [P(a) for a in scw]
    yin1 = rowwise("sc_mid", f_sc_mid, [X(p1, split=3)], sc_ps, tm=2 * tm, nt=nt // 2, outs=[(D, BF16, 1)])[0]
    ps_mid1 = [pe(md[1][2]), P(gains_ffn[1]), pe(md[1][3]), pe(md[1][4])]
    y1, h1_1, hn2_1 = mm_res_mod("sc_out", yin1, w_sout, h2_0, *arrays(ps_mid1))
    s_up[1], s_down1 = gathered("ffn1", hn2_1)
    w_down[1] = s_down1.reshape(FFN_H, D)
    u1, act1 = ffn_fwd(1, hn2_1)
    fo1 = tokens(mm("ffn_down1", V(act1, "tok"), V(w_down[1])), SEQ)
    loss8, dh1_1, dfo1, dm5_1, g_final = final_loss(h1_1, fo1, md[1][5], final_norm[None], loss_target)

    def ffn_bwd(i, u, act, hn2, dfo):
        dact = tokens(mm(f"ffn_down_dx{i}", V(dfo, "tok"), V(w_down[i]), form="nt", out_dtype=BF16), SEQ)
        g_down = mm(f"ffn_down_dw{i}", V(act, "tok"), V(dfo, "tok"), form="tn", out_dtype=BF16)
        r = rowwise(f"ffn_mid_bwd{i}", f_ffn_mid, [X(u, w=FFN_TC, planes=True)], ffn_params(i), tm=SEQ, nt=1, nc=NCF,
                    douts=[X(dact, w=FFN_TC)], dx={0: BF16}, dp=[0, 1, 2, 3])
        du, g_cw, g_cb = r[0], jnp.stack([a.reshape(2 * FFN_H) for a in r[1:4]]), r[4].reshape(1, 2 * FFN_H)
        dhn2 = tokens(mm(f"ffn_up_dx{i}", V(du, "planes"), V(s_up[i], "cols"), form="nt", out_dtype=BF16), SEQ)
        g_up = mm(f"ffn_up_dw{i}", V(hn2, "tok"), V(du, "planes"), form="tn", out="cols", out_dtype=BF16)
        return dhn2, g_up, row_slots(g_down), g_cw, g_cb

    def res_mod_bwd(name, h, y, ps, dh1, dhn):
        return rowwise(name, f_res_mod, [X(h), X(y)], ps, tm=2 * tm, nt=nt // 2, douts=[X(dh1), X(dhn)],
                       dx={0: F32, 1: BF16}, dp=[0, 1, 2, 3])

    def row_slots(g):
        return g.reshape(N_DEV, -1, g.shape[-1])

    a2a_started = {}

    def send_grads(g, slots, after=None):
        a2a_started[g] = exchange_start(f"a2a_{g}_start", list(slots), False, loss8 if after is None else after)
        return a2a_started[g][4][0, 0]

    def after_start(ps, tok):
        return [dict(ps[0], a=ps[0]["a"] + tok)] + ps[1:]

    dhn2_1, g_up1, g_down1, g_fcw1, g_fcb1 = ffn_bwd(1, u1, act1, hn2_1, dfo1)
    tok = send_grads("ffn1", [g_up1, g_down1])
    dh2_0, dy1, dm2_1, g_nffn1, dm3_1, dm4_1 = res_mod_bwd("res_mod_mid1_bwd", h2_0, y1, after_start(ps_mid1, tok), dh1_1, dhn2_1)
    dyin1 = tokens(mm("sc_out_dx", V(dy1, "tok"), V(w_sout), form="nt", out_dtype=BF16), SEQ)
    g_sout = row_slots(mm("sc_out_dw", V(yin1, "tok"), V(dy1, "tok"), form="tn", out_dtype=BF16))
    r = rowwise("sc_mid_bwd", f_sc_mid, [X(p1, split=3)], sc_ps, tm=tm, nt=nt, douts=[X(dyin1)], dx={0: BF16}, dp=[0, 1, 2])
    dp1, g_scw = r[0], jnp.concatenate(r[1:4], 0)
    dhn1 = tokens(mm("sc_in_dx", V(dp1, "tok"), V(s_sin, "cols"), form="nt", out_dtype=BF16), SEQ)
    g_sin = mm("sc_in_dw", V(hn1, "tok"), V(dp1, "tok"), form="tn", out="cols", out_dtype=BF16)
    tok = send_grads("sc", [g_sin, g_sout])
    dh1_0, dfo0, dm5_0, g_nmix1, dm0_1, dm1_1 = res_mod_bwd("res_mod_in1_bwd", h1_0, fo0, after_start(ps_in1, tok), dh2_0, dhn1)

    dhn2_0, g_up0, g_down0, g_fcw0, g_fcb0 = ffn_bwd(0, u0, act0, hn2_0, dfo0)
    tok = send_grads("ffn0", [g_up0, g_down0])
    dx_res, dy0, dm2_0, g_nffn0, dm3_0, dm4_0 = res_mod_bwd("res_mod_mid0_bwd", x, y0, after_start(ps_mid0, tok), dh1_0, dhn2_0)
    dyin0 = tokens(mm("gla_out_dx", V(dy0, "tok"), V(w_gout), form="nt", out_dtype=BF16), SEQ)
    do, dgate, g_head = rowwise("gla_post_bwd", f_gla_post, post_xs, [P(head_gain)], tm=tm, nt=nt,
                                douts=[X(dyin0, split=HEADS)], dx={0: BF16, 2: BF16}, dp=[0])
    dq2, dk2, dv2, dla = gla_bwd(pcat, la, s_all, do)
    dpa, g_wd, g_bd = rowwise("gla_decay_bwd", f_decay, [pa_x], [P(wd), P(bd)], tm=3 * tm, nt=TT // (3 * tm), douts=[X(dla)],
                              dx={0: BF16}, dp=[0, 1])
    dpcat = gla_combine(dq2, dk2, dv2, dgate, dpa)
    dhcat = tokens(mm("gla_in_dx", V(dpcat, "tok"), w_gin, form="nt", out_dtype=BF16), TT)
    grad_x, g_nmix0, dm0_0, dm1_0 = rowwise("mod_in0_bwd", f_mod, [X(x)], ps_in0, tm=tm, nt=nt,
                                            douts=[X(dhcat, ro=ctx_tiles), X(dx_res)], dx={0: F32}, dp=[0, 1, 2])
    g_nmix0c, dmc0, dmc1 = rowwise("mod_ctx_bwd", f_mod1, [X(ctx)], ps_ctx, tm=tm, nt=ctx_tiles, douts=[X(dhcat)],
                                   dx={}, dp=[0, 1, 2])

    zero_row = jnp.zeros((1, 4 * D), F32)
    dmod = [jnp.concatenate([jnp.concatenate([a.reshape(bsz, D) for a in dms], 1), ctx_row], 0)
            for dms, ctx_row in (([dm0_0, dm1_0, dm2_0, dm3_0, dm4_0, dm5_0], jnp.concatenate([dmc0, dmc1, zero_row], 1)),
                                 ([dm0_1, dm1_1, dm2_1, dm3_1, dm4_1, dm5_1], jnp.zeros((1, 6 * D), F32)))]
    g_wa2 = jnp.stack([g_wd[:RANK, :KD], g_wd[RANK:2 * RANK, KD:]])
    small_grads = [jnp.stack(dmod), jnp.concatenate([g_nmix0 + g_nmix0c, g_nmix1], 0), jnp.concatenate([g_nffn0, g_nffn1], 0),
                   g_head, jnp.concatenate([g_fcb0, g_fcb1], 0), g_final, g_wa2, g_bd.reshape(2, KD), g_scw,
                   jnp.stack([g_fcw0, g_fcw1]), loss8[:1]]
    pack1, offs1 = _pack_rows(small_grads, F32, 8)
    ag1 = exchange_start("ag_grads_start", [pack1], True, loss8)
    g_gin = mm("gla_in_dw", V(hcat, "tok"), V(dpcat, "tok"), form="tn", out="cols", out_dtype=BF16, shard_n=GLA_IN // N_DEV,
               after=ag1[4])
    g_gout = row_slots(mm("gla_out_dw", V(yin0, "tok"), V(dy0, "tok"), form="tn", out_dtype=BF16, after=ag1[4]))
    mine1, land1 = exchange_wait("ag_grads_wait", ag1, [g_gin, g_gout], True)
    g1 = lax.dynamic_update_index_in_dim(land1[0], mine1[0], me, 0)
    dmod_all = _unpack_rows(g1, offs1[:1], [small_grads[0].shape])[0]
    tot = _unpack_rows(sum_slots("sum_small", g1), offs1, [a.shape for a in small_grads])
    loss = tot[10][0, 0]
    dm_rows = dmod_all[:, :, :bsz].transpose(1, 0, 2, 3).reshape(2, N_DEV * bsz, 6 * D)
    dm_full = jnp.concatenate([dm_rows, tot[0][:, bsz:], jnp.zeros((2, ADA_ROWS - N_DEV * bsz - 1, 6 * D), F32)], 1)
    dm_mine = lax.dynamic_slice(dm_full, (0, 0, me * ADA_COLS), (2, ADA_ROWS, ADA_COLS))
    g_ada_w, g_ada_b, cpart = ada_bwd(cond, dm_mine, dm_full, ada_w)
    cparts = all_gather("ag_cctx", cpart).reshape(N_DEV, ADA_ROWS - ADA_CTX_ROW, D)[:, 0]
    g_cctx = cctx_grad(cparts, c_ctx[None])[0]
    tok = send_grads("gla", [g_gin, g_gout], after=g_cctx)

    def my_cols(full, n):
        return lax.dynamic_slice_in_dim(full, me * n, n, axis=full.ndim - 1)

    grads = {
        "c_ctx": g_cctx, "ada_b": g_ada_b.reshape(2, 6 * D), "norm_mix": tot[1], "norm_ffn": tot[2],
        "gla_head_norm": tot[3], "ffn_conv_b": tot[4], "final_norm": tot[5].reshape(D),
        "gla_w_a2": my_cols(tot[6], KD // N_DEV)[None], "gla_b_a": my_cols(tot[7], KD // N_DEV)[None],
        "sc_conv_w": my_cols(tot[8], D // N_DEV)[None], "ffn_conv_w": my_cols(tot[9], 2 * FFN_H // N_DEV),
    }

    res_ada = adamw("adamw_ada", *[a.reshape(2 * D, ADA_COLS) for a in (ada_w, g_ada_w, m_ada_w, v_ada_w)])
    grads["c_ctx"] = g_cctx + tok
    big = ["gla_w_in", "gla_w_out", "sc_w_in", "sc_w_out", "ffn_w_up", "ffn_w_down"]
    small = [n for n in names if n not in big and n != "ada_w"]
    g_small = _pack_rows([grads[n] for n in small], F32, 8)[0]
    res_small = adamw("adamw_small", _pack_rows([w_[n] for n in small], F32, 8)[0], g_small,
                      _pack_rows([m_[n] for n in small], F32, 8)[0], _pack_rows([v_[n] for n in small], F32, 8)[0])
    offs_s = _pack_rows([w_[n] for n in small], F32, 8)[1]

    big_res, done, me1 = {}, {"small": res_small[0], "ada_w": res_ada[0]}, jnp.reshape(me, (1,)).astype(jnp.int32)
    for g in groups:
        sent, lands = exchange_wait(f"a2a_{g}_wait", a2a_started[g], list(done.values()), False)
        for (n, i), mine, land in zip(groups[g], sent, lands):
            big_res[n] = adamw_slots(f"adamw_{n}{i}", w_[n], land, mine, me1, m_[n], v_[n], i, into=big_res.get(n))
            done[n] = big_res[n][0]

    out = {}
    for kind, idx in (("grad", 0), ("delta", 1), ("new_m", 2), ("new_v", 3)):
        vals = {n: big_res[n][idx] for n in big}
        vals["ada_w"] = res_ada[idx].reshape(ada_w.shape)
        vals.update(zip(small, _unpack_rows(res_small[idx], offs_s, [w_[n].shape for n in small])))
        out[kind] = [vals[n] for n in names]
    return (loss, grad_x, *out["grad"], *out["delta"], *out["new_m"], *out["new_v"])
```

```python
import functools

import jax
import jax.numpy as jnp
from jax import lax
from jax.experimental import pallas as pl
from jax.experimental.pallas import tpu as pltpu

F32 = jnp.float32
BF16 = jnp.bfloat16

N_DEV = 8
D = 1024
SEQ = 2048
CTX = 256
TT = CTX + SEQ
GRID_W = 64
CHUNK = 64
HEADS = 4
HK = 128
HV = 256
KD = 512
VD = 1024
RANK = 16
TAU = 16.0
GLA_IN = 3104
GLA_IN_PAD = 3200
FFN_H = 2560
FFN_TC = 256
EPS = 1e-6
LR, B1, B2, AEPS, WD, STEP = 0.001, 0.9, 0.999, 1e-08, 0.01, 10
MESH = pl.DeviceIdType.MESH


def _blocks(n):
    return [n] + [t for t in range(n - n % 128, 0, -128) if n % t == 0 and t != n]


def V(arr, kind="flat", width=None):
    if kind == "tok":
        return V(arr.reshape(-1, arr.shape[-1]))
    if kind == "flat":
        r, c = arr.shape
        return dict(a=arr, kind=kind, shape=(r, c), rows=_blocks(r), cols=_blocks(c))
    if kind == "planes":
        bsz, _, t, ch = arr.shape
        return dict(a=arr, kind=kind, shape=(bsz * t, 2 * ch), rows=_blocks(t), cols=[2 * ch] + _blocks(ch), t=t, ch=ch)
    _, r, n = arr.shape
    if width is not None:
        return dict(a=arr, kind=kind, shape=(r, width), rows=_blocks(r), cols=[width], n=n, pad=width - N_DEV * n)
    return dict(a=arr, kind=kind, shape=(r, N_DEV * n), rows=_blocks(r), cols=[8 * n, 4 * n, 2 * n], n=n, pad=0)


def _view_spec(v, br, bc, idx):
    if v["kind"] == "flat":
        return pl.BlockSpec((br, bc), idx)
    if v["kind"] == "planes":
        nt = v["t"] // br
        if bc == 2 * v["ch"]:
            return pl.BlockSpec((None, 2, br, v["ch"]), lambda i, j, k: (idx(i, j, k)[0] // nt, 0, idx(i, j, k)[0] % nt, 0))
        nch = v["ch"] // bc

        def at(i, j, k):
            r, c = idx(i, j, k)
            return r // nt, c // nch, r % nt, c % nch
        return pl.BlockSpec((None, None, br, bc), at)
    return pl.BlockSpec(((bc - v["pad"]) // v["n"], br, v["n"]), lambda i, j, k: (idx(i, j, k)[1], idx(i, j, k)[0], 0))


def _out_view(kind, rows, cols, dtype, planes_t=None, shard_n=None):
    if kind == "flat":
        shape = (rows, cols)
    elif kind == "planes":
        shape = (rows // planes_t, 2, planes_t, cols // 2)
    elif shard_n is not None:
        return V(jax.ShapeDtypeStruct((N_DEV, rows, shard_n), dtype), kind, width=cols)
    else:
        shape = (N_DEV, rows, cols // N_DEV)
    return V(jax.ShapeDtypeStruct(shape, dtype), kind)


MM_VMEM_BUDGET = 40 * 2 ** 20
MM_VMEM_LIMIT = 56 * 2 ** 20
MM_MAX_TILE = 1536


def _mm_tiles(m, n, kk, ms, ns, ks, a_bytes, b_bytes, o_bytes):
    best = None
    for tk in ks:
        for tm in [t for t in ms if t <= MM_MAX_TILE] or ms:
            for tn in [t for t in ns if t <= MM_MAX_TILE] or ns:
                one_k = tk == kk
                need = 2 * (tm * tk * a_bytes + tk * tn * b_bytes + tm * tn * o_bytes) + (0 if one_k else tm * tn * 4)
                if need > MM_VMEM_BUDGET:
                    continue
                steps = (m // tm) * (n // tn) * (kk // tk)
                traffic = (m * kk * a_bytes * (1 if one_k else n // tn)
                           + kk * n * b_bytes * (1 if one_k and n == tn else m // tm) + m * n * o_bytes)
                fill = (tm * tk * a_bytes + tk * tn * b_bytes) / 2.5e12
                cost = max(2.0 * m * n * kk / (9e14 if one_k else 6.5e14), traffic / 2.5e12) + steps * 0.4e-6 + fill
                if best is None or cost < best[0]:
                    best = (cost, tm, tn, tk)
    return best[1:]


def mm(name, a, b, form="nn", out="flat", out_dtype=F32, planes_t=None, shard_n=None, after=None):
    (m, kk) = a["shape"][::-1] if form == "tn" else a["shape"]
    n = b["shape"][0] if form == "nt" else b["shape"][1]
    assert (b["shape"][1] if form == "nt" else b["shape"][0]) == kk, (name, a["shape"], b["shape"])
    o = _out_view(out, m, n, out_dtype, planes_t, shard_n)
    a_m, a_k = (a["cols"], a["rows"]) if form == "tn" else (a["rows"], a["cols"])
    b_k, b_n = (b["cols"], b["rows"]) if form == "nt" else (b["rows"], b["cols"])
    tm, tn, tk = _mm_tiles(m, n, kk, [t for t in a_m if t in o["rows"]], [t for t in b_n if t in o["cols"]],
                           [t for t in a_k if t in b_k], a["a"].dtype.itemsize, b["a"].dtype.itemsize,
                           jnp.dtype(out_dtype).itemsize)
    nk = kk // tk
    dn = (((0 if form == "tn" else 1,), (1 if form == "nt" else 0,)), ((), ()))

    def load(ref, v):
        if len(ref.shape) == 3:
            pieces = [ref[p].astype(BF16) for p in range(ref.shape[0])]
            if v.get("pad"):
                pieces.append(jnp.zeros(ref.shape[1:2] + (v["pad"],), BF16))
            return jnp.concatenate(pieces, axis=-1)
        return ref[...].astype(BF16)

    def store(o_ref, val):
        val = val.astype(out_dtype)
        if len(o_ref.shape) == 3:
            w = o_ref.shape[-1]
            for p in range(o_ref.shape[0]):
                o_ref[p] = val[:, p * w:(p + 1) * w]
        else:
            o_ref[...] = val

    def body(a_ref, b_ref, *rest):
        o_ref, acc = rest[0 if after is None else 1], rest[1 if after is None else 2:]
        if nk == 1:
            store(o_ref, lax.dot_general(load(a_ref, a), load(b_ref, b), dn, preferred_element_type=F32))
            return
        k, acc_ref = pl.program_id(2), acc[0]

        @pl.when(k == 0)
        def _():
            acc_ref[...] = jnp.zeros_like(acc_ref)

        acc_ref[...] += lax.dot_general(load(a_ref, a), load(b_ref, b), dn, preferred_element_type=F32)

        @pl.when(k == nk - 1)
        def _():
            store(o_ref, acc_ref[...])

    if form == "tn":
        a_spec = _view_spec(a, tk, tm, lambda i, j, k: (k, i))
    else:
        a_spec = _view_spec(a, tm, tk, lambda i, j, k: (i, k))
    if form == "nt":
        b_spec = _view_spec(b, tn, tk, lambda i, j, k: (j, k))
    else:
        b_spec = _view_spec(b, tk, tn, lambda i, j, k: (k, j))
    return pl.pallas_call(
        body, name=name, grid=(m // tm, n // tn, nk),
        in_specs=[a_spec, b_spec] + ([] if after is None else [pl.BlockSpec(memory_space=pl.ANY)]),
        out_specs=_view_spec(o, tm, tn, lambda i, j, k: (i, j)), out_shape=o["a"],
        scratch_shapes=[pltpu.VMEM((tm, tn), F32)] if nk > 1 else [],
        compiler_params=pltpu.CompilerParams(dimension_semantics=("parallel", "parallel", "arbitrary"),
                                             vmem_limit_bytes=MM_VMEM_LIMIT),
    )(a["a"], b["a"], *([] if after is None else [after]))


def mm_res_mod(name, a, w, h, gate, gain, shift, scale):
    bsz, t_len, kk = a.shape
    tm = 1024 if kk <= D else 512
    per = t_len // tm

    def body(a_ref, w_ref, h_ref, gate_ref, gain_ref, shift_ref, scale_ref, y_ref, h1_ref, hn_ref):
        y = jnp.dot(a_ref[...].astype(BF16), w_ref[...].astype(BF16), preferred_element_type=F32)
        h1 = h_ref[...] + gate_ref[...] * y
        y_ref[...] = y.astype(BF16)
        h1_ref[...] = h1
        hn_ref[...] = _mod(h1, gain_ref[...], shift_ref[...], scale_ref[...]).astype(BF16)

    def tile(width):
        return pl.BlockSpec((None, tm, width), lambda i: (i // per, i % per, 0))

    per_ex = pl.BlockSpec((None, 1, D), lambda i: (i // per, 0, 0))
    return pl.pallas_call(
        body, name=name, grid=(bsz * per,),
        in_specs=[tile(kk), pl.BlockSpec((kk, D), lambda i: (0, 0)), tile(D), per_ex, pl.BlockSpec((1, D), lambda i: (0, 0)),
                  per_ex, per_ex],
        out_specs=[tile(D)] * 3,
        out_shape=[jax.ShapeDtypeStruct((bsz, t_len, D), BF16), jax.ShapeDtypeStruct((bsz, t_len, D), F32),
                   jax.ShapeDtypeStruct((bsz, t_len, D), BF16)],
        compiler_params=pltpu.CompilerParams(dimension_semantics=("parallel",), vmem_limit_bytes=MM_VMEM_LIMIT),
    )(a, w, h, gate, gain, shift, scale)


def X(arr, w=None, co=0, ro=0, split=1, planes=False):
    return dict(a=arr, w=arr.shape[-1] if w is None else w, co=co, ro=ro, split=2 if planes else split,
                mode="planes" if planes else "cols")


def P(arr, per_example=False, w=None, split=1, rows=False):
    return dict(a=arr, e=per_example, w=arr.shape[-1] if w is None else w, split=arr.shape[-2] if rows else split,
                mode="rows" if rows else "cols")


def _pieces(ref, s):
    if s["mode"] == "planes":
        return [ref[0], ref[1]]
    if s["mode"] == "rows":
        return [ref[i:i + 1, :] for i in range(s["split"])]
    w = ref.shape[-1] // s["split"]
    return [ref[:, i * w:(i + 1) * w] for i in range(s["split"])]


def _store(ref, pieces, s, accumulate=False):
    w = ref.shape[-1] // len(pieces)
    for i, p in enumerate(pieces):
        at = (i,) if s["mode"] == "planes" else (slice(i, i + 1),) if s["mode"] == "rows" else (slice(None), slice(i * w, (i + 1) * w))
        if accumulate:
            ref[at] += p.astype(ref.dtype)
        else:
            ref[at] = p.astype(ref.dtype)


def rowwise(name, f, xs, ps, *, tm, nt, nc=1, outs=None, douts=None, dx=None, dp=None):
    bsz = xs[0]["a"].shape[0]
    fwd = douts is None
    nx, np_ = len(xs), len(ps)
    douts = [] if fwd else douts
    dx = {} if fwd else dx
    dp = [] if fwd else dp

    def x_spec(s):
        if s["mode"] == "planes":
            return pl.BlockSpec((None, 2, tm, s["w"]), lambda c, b, t, s=s: (b, 0, t + s["ro"], c + s["co"]))
        return pl.BlockSpec((None, tm, s["w"]), lambda c, b, t, s=s: (b, t + s["ro"], c + s["co"]))

    def x_out(s, dt):
        if s["mode"] == "planes":
            return (jax.ShapeDtypeStruct((bsz, 2, nt * tm, nc * s["w"]), dt),
                    pl.BlockSpec((None, 2, tm, s["w"]), lambda c, b, t: (b, 0, t, c)))
        return (jax.ShapeDtypeStruct((bsz, nt * tm, nc * s["w"]), dt), pl.BlockSpec((None, tm, s["w"]), lambda c, b, t: (b, t, c)))

    def p_spec(s):
        r = s["a"].shape[-2]
        if s["e"]:
            return pl.BlockSpec((None, r, s["w"]), lambda c, b, t: (b, 0, c))
        return pl.BlockSpec((r, s["w"]), lambda c, b, t: (0, c))

    in_specs = [x_spec(s) for s in xs] + [p_spec(s) for s in ps] + [x_spec(s) for s in douts]
    operands = [s["a"] for s in xs] + [s["a"] for s in ps] + [s["a"] for s in douts]
    if fwd:
        out_modes = [dict(mode="cols", split=sp) for (_, _, sp) in outs]
        out_shape = [jax.ShapeDtypeStruct((bsz, nt * tm, nc * w), dt) for (w, dt, _) in outs]
        out_specs = [pl.BlockSpec((None, tm, w), lambda c, b, t: (b, t, c)) for (w, _, _) in outs]
    else:
        dx_outs = [x_out(xs[i], dt) for i, dt in dx.items()]
        out_shape, out_specs = [o[0] for o in dx_outs], [o[1] for o in dx_outs]
        for j in dp:
            s = ps[j]
            r = s["a"].shape[-2]
            if s["e"]:
                out_shape.append(jax.ShapeDtypeStruct((bsz, r, nc * s["w"]), F32))
                out_specs.append(pl.BlockSpec((None, r, s["w"]), lambda c, b, t: (b, 0, c)))
            else:
                out_shape.append(jax.ShapeDtypeStruct((r, nc * s["w"]), F32))
                out_specs.append(pl.BlockSpec((r, s["w"]), lambda c, b, t: (0, c)))

    def body(*refs):
        x_refs, p_refs = refs[:nx], refs[nx:nx + np_]
        d_refs = refs[nx + np_:nx + np_ + len(douts)]
        o_refs = refs[nx + np_ + len(douts):]
        xv = [[p.astype(F32) for p in _pieces(r, s)] for r, s in zip(x_refs, xs)]
        pv = [[p.astype(F32) for p in _pieces(r, s)] for r, s in zip(p_refs, ps)]
        if fwd:
            for r, pieces, s in zip(o_refs, f(xv, pv), out_modes):
                _store(r, pieces, s)
            return
        _, vjp = jax.vjp(f, xv, pv)
        cot = [[p.astype(F32) for p in _pieces(r, s)] for r, s in zip(d_refs, douts)]
        dxv, dpv = vjp(cot)
        for r, i in zip(o_refs, dx):
            _store(r, dxv[i], xs[i])
        b, t = pl.program_id(1), pl.program_id(2)
        for r, j in zip(o_refs[len(dx):], dp):
            first = (t == 0) if ps[j]["e"] else jnp.logical_and(b == 0, t == 0)

            @pl.when(first)
            def _(r=r, j=j):
                _store(r, dpv[j], ps[j])

            @pl.when(jnp.logical_not(first))
            def _(r=r, j=j):
                _store(r, dpv[j], ps[j], accumulate=True)

    res = pl.pallas_call(
        body, name=name, grid=(nc, bsz, nt), in_specs=in_specs, out_specs=out_specs, out_shape=out_shape,
        compiler_params=pltpu.CompilerParams(dimension_semantics=("arbitrary", "arbitrary", "arbitrary")),
    )(*operands)
    return res


def _keep_rows(a, shift, keep):
    n = a.shape[0]
    t = lax.broadcasted_iota(jnp.int32, a.shape, 0)
    return jnp.where(keep(t, n), pltpu.roll(a, shift % n, 0), 0.0)


def _shift_pair(step, keep_prev=None, keep_next=None):
    @jax.custom_vjp
    def prev(a):
        if keep_prev is None:
            return jnp.concatenate([jnp.zeros((step,) + a.shape[1:], a.dtype), a[:a.shape[0] - step]], axis=0)
        return _keep_rows(a, step, keep_prev)

    @jax.custom_vjp
    def nxt(a):
        if keep_next is None:
            return jnp.concatenate([a[step:], jnp.zeros((step,) + a.shape[1:], a.dtype)], axis=0)
        return _keep_rows(a, -step, keep_next)

    prev.defvjp(lambda a: (prev(a), None), lambda _, g: (nxt(g),))
    nxt.defvjp(lambda a: (nxt(a), None), lambda _, g: (prev(g),))
    return prev, nxt


prev_tok, next_tok = _shift_pair(1, lambda t, n: t % GRID_W != 0, lambda t, n: t % GRID_W != GRID_W - 1)
prev_row, next_row = _shift_pair(GRID_W)


@jax.custom_vjp
def bdot(a, w):
    return jnp.dot(a.astype(BF16), w.astype(BF16), preferred_element_type=F32)


def _bdot_bwd(res, g):
    a, w = res
    gb = g.astype(BF16)
    da = lax.dot_general(gb, w.astype(BF16), (((1,), (1,)), ((), ())), preferred_element_type=F32)
    dw = lax.dot_general(a.astype(BF16), gb, (((0,), (0,)), ((), ())), preferred_element_type=F32)
    return da, dw


bdot.defvjp(lambda a, w: (bdot(a, w), (a, w)), _bdot_bwd)


@jax.custom_vjp
def log_sigmoid(z):
    return jnp.minimum(z, 0.0) - jnp.log(1.0 + jnp.exp(-jnp.abs(z)))


def _lsig_bwd(z, g):
    e = jnp.exp(-jnp.abs(z))
    return (g * jnp.where(z >= 0, e, 1.0) / (1.0 + e),)


log_sigmoid.defvjp(lambda z: (log_sigmoid(z), z), _lsig_bwd)


def silu(x):
    return x * jax.nn.sigmoid(x)


def _rms(x):
    return x * lax.rsqrt(jnp.mean(x * x, axis=-1, keepdims=True) + EPS)


def _mod(x, gain, shift, scale):
    return _rms(x) * gain * (1.0 + scale) + shift


def f_mod(xs, ps):
    ((h,),), ((gain,), (shift,), (scale,)) = xs, ps
    return [[_mod(h, gain, shift, scale)], [h]]


def f_res_mod(xs, ps):
    ((h,), (y,)), ((gate,), (gain,), (shift,), (scale,)) = xs, ps
    h1 = h + gate * y
    return [[h1], [_mod(h1, gain, shift, scale)]]


def f_ffn_mid(xs, ps):
    ((ua, ug),), ((w0a, w0g), (w1a, w1g), (w2a, w2g), (ba, bg)) = xs, ps
    a = w0a * prev_row(ua) + w1a * ua + w2a * next_row(ua) + ba
    g = w0g * prev_row(ug) + w1g * ug + w2g * next_row(ug) + bg
    return [[a * silu(g)]]


def f_sc_mid(xs, ps):
    ((bg, cg, v),), ((w0,), (w1,), (w2,)) = xs, ps
    z = cg * v
    return [[bg * (w0 * prev_tok(z) + w1 * z + w2 * next_tok(z))]]


def f_decay(xs, ps):
    ((a,),), ((wd,), (bd,)) = xs, ps
    return [[log_sigmoid(bdot(a, wd) + bd) / TAU]]


def f_gla_post(xs, ps):
    (of, ob, g), ((gain,),) = xs, ps
    return [[_rms(a + b) * gain * silu(c) for a, b, c in zip(of, ob, g)]]


NCH = TT // CHUNK
CTX_CH = CTX // CHUNK
_NT = (((1,), (1,)), ((), ()))
_TN = (((0,), (0,)), ((), ()))
_NN = (((1,), (0,)), ((), ()))


def _chunk_of(d, j):
    return jnp.where(d == 0, j, jnp.where(j < CTX_CH, CTX_CH - 1 - j, NCH + CTX_CH - 1 - j))


def _dot(a, b, dn):
    return lax.dot_general(a, b, dn, preferred_element_type=F32)


def _cumsum_rows(g, suffix):
    n = g.shape[0]
    row = lax.broadcasted_iota(jnp.int32, g.shape, 0)
    s = 1
    while s < n:
        if suffix:
            g = g + jnp.where(row < n - s, pltpu.roll(g, n - s, 0), 0.0)
        else:
            g = g + jnp.where(row >= s, pltpu.roll(g, s, 0), 0.0)
        s *= 2
    return g


def _causal(backward):
    row = lax.broadcasted_iota(jnp.int32, (CHUNK, CHUNK), 0)
    col = lax.broadcasted_iota(jnp.int32, (CHUNK, CHUNK), 1)
    return col >= row if backward else col <= row


def _gla_in_specs(bsz, rev):
    def blk(d, j):
        return _chunk_of(d, (NCH - 1 - j) if rev else j)

    return [
        pl.BlockSpec((bsz, CHUNK, KD), lambda d, j: (0, blk(d, j), 0)),
        pl.BlockSpec((bsz, CHUNK, KD), lambda d, j: (0, blk(d, j), 1)),
        pl.BlockSpec((bsz, CHUNK, VD), lambda d, j: (0, blk(d, j), 1)),
        pl.BlockSpec((bsz, CHUNK, KD), lambda d, j: (0, blk(d, j), d)),
    ], blk


def gla_fwd(pcat, la):
    bsz = pcat.shape[0]
    in_specs, blk = _gla_in_specs(bsz, False)

    def body(q_ref, k_ref, v_ref, la_ref, o_ref, s_ref, st):
        d, j = pl.program_id(0), pl.program_id(1)

        @pl.when(j == 0)
        def _():
            st[...] = jnp.zeros_like(st)

        s_ref[...] = st[...]

        def scan(backward):
            causal = _causal(backward)
            for e in range(bsz):
                g_all = la_ref[e]
                b_all = _cumsum_rows(g_all, backward)
                bl_all = jnp.sum(g_all, axis=0, keepdims=True)
                qs_all = (q_ref[e].astype(F32) * (HK ** -0.5) * jnp.exp(b_all)).astype(BF16)
                ks_all = (k_ref[e] * jnp.exp(-b_all)).astype(BF16)
                kd_all = (k_ref[e] * jnp.exp(bl_all - b_all)).astype(BF16)
                el_all = jnp.exp(bl_all)
                for h in range(HEADS):
                    ks_, vs_ = slice(h * HK, (h + 1) * HK), slice(h * HV, (h + 1) * HV)
                    qs, ks, kd, v = qs_all[:, ks_], ks_all[:, ks_], kd_all[:, ks_], v_ref[e, :, vs_].astype(BF16)
                    s = st[e, h]
                    att = jnp.where(causal, _dot(qs, ks, _NT), 0.0).astype(BF16)
                    o_ref[e, :, vs_] = _dot(qs, s.astype(BF16), _NT) + _dot(att, v, _NN)
                    st[e, h] = el_all[:, ks_] * s + _dot(v, kd, _TN)

        @pl.when(d == 0)
        def _():
            scan(False)

        @pl.when(d == 1)
        def _():
            scan(True)

    return pl.pallas_call(
        body, name="gla_fwd", grid=(2, NCH), in_specs=in_specs,
        out_specs=[pl.BlockSpec((bsz, CHUNK, VD), lambda d, j: (0, blk(d, j), d)),
                   pl.BlockSpec((bsz, None, None, HEADS, HV, HK), lambda d, j: (0, d, j, 0, 0, 0))],
        out_shape=[jax.ShapeDtypeStruct((bsz, TT, 2 * VD), F32), jax.ShapeDtypeStruct((bsz, 2, NCH, HEADS, HV, HK), F32)],
        scratch_shapes=[pltpu.VMEM((bsz, HEADS, HV, HK), F32)],
        compiler_params=pltpu.CompilerParams(dimension_semantics=("arbitrary", "arbitrary")),
    )(pcat, pcat, pcat, la)


def gla_bwd(pcat, la, s_all, do):
    bsz = pcat.shape[0]
    in_specs, blk = _gla_in_specs(bsz, True)
    in_specs += [
        pl.BlockSpec((bsz, None, None, HEADS, HV, HK), lambda d, j: (0, d, NCH - 1 - j, 0, 0, 0)),
        pl.BlockSpec((bsz, CHUNK, VD), lambda d, j: (0, jnp.maximum(blk(d, j) - CTX_CH, 0), 0)),
    ]

    def body(q_ref, k_ref, v_ref, la_ref, s_ref, do_ref, dq_ref, dk_ref, dv_ref, dla_ref, dst):
        d, j = pl.program_id(0), pl.program_id(1)

        @pl.when(j == 0)
        def _():
            dst[...] = jnp.zeros_like(dst)

        latent = blk(d, j) >= CTX_CH
        scale = HK ** -0.5

        def scan(backward):
            causal = _causal(backward)
            for e in range(bsz):
                g_all = la_ref[e]
                b_all = _cumsum_rows(g_all, backward)
                bl_all = jnp.sum(g_all, axis=0, keepdims=True)
                ex_all, ei_all, ed_all, el_all = jnp.exp(b_all), jnp.exp(-b_all), jnp.exp(bl_all - b_all), jnp.exp(bl_all)
                qs_all, ks_all, kd_all = q_ref[e].astype(F32) * scale * ex_all, k_ref[e] * ei_all, k_ref[e] * ed_all
                qsb_all, ksb_all, kdb_all = qs_all.astype(BF16), ks_all.astype(BF16), kd_all.astype(BF16)
                db_parts, dbl_parts = [], []
                for h in range(HEADS):
                    ks_, vs_ = slice(h * HK, (h + 1) * HK), slice(h * HV, (h + 1) * HV)
                    qs, ks, kd, el = qs_all[:, ks_], ks_all[:, ks_], kd_all[:, ks_], el_all[:, ks_]
                    qsb, ksb, kdb, v = qsb_all[:, ks_], ksb_all[:, ks_], kdb_all[:, ks_], v_ref[e, :, vs_].astype(BF16)
                    s, ds1 = s_ref[e, h], dst[e, h]
                    sb, ds1b = s.astype(BF16), ds1.astype(BF16)
                    dob = jnp.where(latent, do_ref[e, :, vs_], 0.0).astype(BF16)
                    att = jnp.where(causal, _dot(qsb, ksb, _NT), 0.0).astype(BF16)
                    datt = jnp.where(causal, _dot(dob, v, _NT), 0.0).astype(BF16)
                    dqs = _dot(dob, sb, _NN) + _dot(datt, ksb, _NN)
                    dks = _dot(datt, qsb, _TN)
                    dv_ref[e, :, vs_] = (_dot(att, dob, _TN) + _dot(kdb, ds1b, _NT)).astype(BF16)
                    dkd = _dot(v, ds1b, _NN)
                    dst[e, h] = _dot(dob, qsb, _TN) + el * ds1
                    del_ = jnp.sum(s * ds1, axis=0, keepdims=True)
                    dq_ref[e, :, ks_] = (dqs * ex_all[:, ks_] * scale).astype(BF16)
                    dk_ref[e, :, ks_] = (dks * ei_all[:, ks_] + dkd * ed_all[:, ks_]).astype(BF16)
                    db_parts.append(dqs * qs - dks * ks - dkd * kd)
                    dbl_parts.append(jnp.sum(dkd * kd, axis=0, keepdims=True) + del_ * el)
                dla_ref[e] = _cumsum_rows(jnp.concatenate(db_parts, -1), not backward) + jnp.concatenate(dbl_parts, -1)

        @pl.when(d == 0)
        def _():
            scan(False)

        @pl.when(d == 1)
        def _():
            scan(True)

    return pl.pallas_call(
        body, name="gla_bwd", grid=(2, NCH), in_specs=in_specs,
        out_specs=[pl.BlockSpec((None, bsz, CHUNK, KD), lambda d, j: (d, 0, blk(d, j), 0)),
                   pl.BlockSpec((None, bsz, CHUNK, KD), lambda d, j: (d, 0, blk(d, j), 0)),
                   pl.BlockSpec((None, bsz, CHUNK, VD), lambda d, j: (d, 0, blk(d, j), 0)),
                   pl.BlockSpec((bsz, CHUNK, KD), lambda d, j: (0, blk(d, j), d))],
        out_shape=[jax.ShapeDtypeStruct((2, bsz, TT, KD), BF16), jax.ShapeDtypeStruct((2, bsz, TT, KD), BF16),
                   jax.ShapeDtypeStruct((2, bsz, TT, VD), BF16), jax.ShapeDtypeStruct((bsz, TT, 2 * KD), F32)],
        scratch_shapes=[pltpu.VMEM((bsz, HEADS, HV, HK), F32)],
        compiler_params=pltpu.CompilerParams(dimension_semantics=("arbitrary", "arbitrary")),
    )(pcat, pcat, pcat, la, s_all, do)


def gla_combine(dq2, dk2, dv2, dgate, dpa):
    bsz = dgate.shape[0]
    tm = CTX

    def body(dq_ref, dk_ref, dv_ref, dg_ref, dpa_ref, o_ref):
        t = pl.program_id(1)
        o_ref[:, 0:KD] = (dq_ref[0].astype(F32) + dq_ref[1].astype(F32)).astype(BF16)
        o_ref[:, KD:2 * KD] = (dk_ref[0].astype(F32) + dk_ref[1].astype(F32)).astype(BF16)
        o_ref[:, 2 * KD:2 * KD + VD] = (dv_ref[0].astype(F32) + dv_ref[1].astype(F32)).astype(BF16)
        o_ref[:, 2 * KD + VD:2 * KD + 2 * VD] = jnp.where(t > 0, dg_ref[...], 0).astype(BF16)
        o_ref[:, 2 * KD + 2 * VD:] = dpa_ref[...].astype(BF16)

    return pl.pallas_call(
        body, name="gla_combine", grid=(bsz, TT // tm),
        in_specs=[pl.BlockSpec((2, None, tm, KD), lambda b, t: (0, b, t, 0)),
                  pl.BlockSpec((2, None, tm, KD), lambda b, t: (0, b, t, 0)),
                  pl.BlockSpec((2, None, tm, VD), lambda b, t: (0, b, t, 0)),
                  pl.BlockSpec((None, tm, VD), lambda b, t: (b, jnp.maximum(t - 1, 0), 0)),
                  pl.BlockSpec((None, tm, 128), lambda b, t: (b, t, 0))],
        out_specs=pl.BlockSpec((None, tm, GLA_IN_PAD), lambda b, t: (b, t, 0)),
        out_shape=jax.ShapeDtypeStruct((bsz, TT, GLA_IN_PAD), BF16),
        compiler_params=pltpu.CompilerParams(dimension_semantics=("arbitrary", "arbitrary")),
    )(dq2, dk2, dv2, dgate, dpa)


def final_loss(h1, fo, gate, gain, tgt):
    bsz, t_len, _ = h1.shape
    tm = 512

    def body(h_ref, f_ref, gate_ref, gain_ref, tgt_ref, loss_ref, dh_ref, df_ref, dgate_ref, dgain_ref):
        b, t = pl.program_id(0), pl.program_id(1)
        target = tgt_ref[...]

        def core(h, fo_, gate_, gain_):
            e = _rms(h + gate_ * fo_) * gain_ - target
            return jnp.sum(0.5 * jnp.sum(e * e, axis=-1, keepdims=True) / D, axis=0, keepdims=True)

        loss, vjp = jax.vjp(core, h_ref[...], f_ref[...], gate_ref[...], gain_ref[...])
        dh, df, dgate, dgain = vjp(jnp.ones((1, 1), F32))
        dh_ref[...] = dh
        df_ref[...] = df.astype(BF16)
        first = jnp.logical_and(b == 0, t == 0)

        @pl.when(first)
        def _():
            loss_ref[...] = jnp.broadcast_to(loss, loss_ref.shape)
            dgain_ref[...] = dgain

        @pl.when(jnp.logical_not(first))
        def _():
            loss_ref[...] += jnp.broadcast_to(loss, loss_ref.shape)
            dgain_ref[...] += dgain

        @pl.when(t == 0)
        def _():
            dgate_ref[...] = dgate

        @pl.when(t > 0)
        def _():
            dgate_ref[...] += dgate

    tile = pl.BlockSpec((None, tm, D), lambda b, t: (b, t, 0))
    per_ex = pl.BlockSpec((None, 1, D), lambda b, t: (b, 0, 0))
    shared = pl.BlockSpec((1, D), lambda b, t: (0, 0))
    return pl.pallas_call(
        body, name="final_loss", grid=(bsz, t_len // tm),
        in_specs=[tile, tile, per_ex, shared, tile],
        out_specs=[pl.BlockSpec((8, 128), lambda b, t: (0, 0)), tile, tile, per_ex, shared],
        out_shape=[jax.ShapeDtypeStruct((8, 128), F32), jax.ShapeDtypeStruct(h1.shape, F32),
                   jax.ShapeDtypeStruct(h1.shape, BF16), jax.ShapeDtypeStruct((bsz, 1, D), F32),
                   jax.ShapeDtypeStruct((1, D), F32)],
        compiler_params=pltpu.CompilerParams(dimension_semantics=("arbitrary", "arbitrary")),
    )(h1, fo, gate, gain, tgt)


ADA_ROWS = 24
ADA_CTX_ROW = 16
ADA_COLS = 6 * D // N_DEV


def ada_fwd(cond, w, b):
    def body(c_ref, w_ref, b_ref, o_ref):
        s = silu(c_ref[...]).astype(BF16)
        o_ref[...] = jnp.dot(s, w_ref[...].astype(BF16), preferred_element_type=F32) + b_ref[...]

    return pl.pallas_call(
        body, name="ada_fwd", grid=(2,),
        in_specs=[pl.BlockSpec((ADA_ROWS, D), lambda i: (0, 0)), pl.BlockSpec((None, D, ADA_COLS), lambda i: (i, 0, 0)),
                  pl.BlockSpec((None, 1, ADA_COLS), lambda i: (i, 0, 0))],
        out_specs=pl.BlockSpec((None, ADA_ROWS, ADA_COLS), lambda i: (i, 0, 0)),
        out_shape=jax.ShapeDtypeStruct((2, ADA_ROWS, ADA_COLS), F32),
    )(cond, w, b)


def ada_bwd(cond, dm_mine, dm_full, w):
    def body(c_ref, dm_ref, dmf_ref, w_ref, gw_ref, gb_ref, cp_ref):
        i = pl.program_id(0)
        s = silu(c_ref[...]).astype(BF16)
        dm = dm_ref[...].astype(BF16)
        gw_ref[...] = _dot(s, dm, _TN)
        gb_ref[...] = jnp.sum(dmf_ref[...], axis=0, keepdims=True)

        @pl.when(i == 0)
        def _():
            cp_ref[...] = _dot(dm_ref[ADA_CTX_ROW:, :].astype(BF16), w_ref[...].astype(BF16), _NT)

    return pl.pallas_call(
        body, name="ada_bwd", grid=(2,),
        in_specs=[pl.BlockSpec((ADA_ROWS, D), lambda i: (0, 0)), pl.BlockSpec((None, ADA_ROWS, ADA_COLS), lambda i: (i, 0, 0)),
                  pl.BlockSpec((None, ADA_ROWS, 6 * D), lambda i: (i, 0, 0)), pl.BlockSpec((None, D, ADA_COLS), lambda i: (i, 0, 0))],
        out_specs=[pl.BlockSpec((None, D, ADA_COLS), lambda i: (i, 0, 0)), pl.BlockSpec((None, 1, 6 * D), lambda i: (i, 0, 0)),
                   pl.BlockSpec((ADA_ROWS - ADA_CTX_ROW, D), lambda i: (0, 0))],
        out_shape=[jax.ShapeDtypeStruct((2, D, ADA_COLS), F32), jax.ShapeDtypeStruct((2, 1, 6 * D), F32),
                   jax.ShapeDtypeStruct((ADA_ROWS - ADA_CTX_ROW, D), F32)],
        compiler_params=pltpu.CompilerParams(dimension_semantics=("arbitrary",)),
    )(cond, dm_mine, dm_full, w)


def cctx_grad(parts, c_ctx):
    def body(p_ref, c_ref, o_ref):
        tot = p_ref[0:1, :]
        for i in range(1, N_DEV):
            tot = tot + p_ref[i:i + 1, :]
        c = c_ref[...]
        sg = jax.nn.sigmoid(c)
        o_ref[...] = tot * sg * (1.0 + c * (1.0 - sg))

    return pl.pallas_call(body, name="cctx_grad", out_shape=jax.ShapeDtypeStruct((1, D), F32))(parts, c_ctx)


def _row_tile(r):
    for t in (512, 256, 128, 80, 64, 40, 32, 16, 8):
        if r % t == 0:
            return t
    return r


def _slot_sum(ref):
    tot = ref[0].astype(F32)
    for i in range(1, ref.shape[0]):
        tot = tot + ref[i].astype(F32)
    return tot


def sum_slots(name, x):
    s, r, c = x.shape
    tr = _row_tile(r)

    def body(x_ref, o_ref):
        o_ref[...] = _slot_sum(x_ref)

    return pl.pallas_call(
        body, name=name, grid=(r // tr,), in_specs=[pl.BlockSpec((s, tr, c), lambda i: (0, i, 0))],
        out_specs=pl.BlockSpec((tr, c), lambda i: (i, 0)), out_shape=jax.ShapeDtypeStruct((r, c), F32),
    )(x)


def _adamw_update(gv, w_ref, m_ref, v_ref, go_ref, d_ref, mo_ref, vo_ref):
    mn = B1 * m_ref[...] + (1.0 - B1) * gv
    vn = B2 * v_ref[...] + (1.0 - B2) * jnp.square(gv)
    m_hat = mn / (1.0 - B1 ** STEP)
    v_hat = vn / (1.0 - B2 ** STEP)
    go_ref[...] = gv
    d_ref[...] = -LR * (m_hat / (jnp.sqrt(v_hat) + AEPS) + WD * w_ref[...])
    mo_ref[...] = mn
    vo_ref[...] = vn


def adamw_slots(name, w, land, sent, me1, m, v, layer, into=None):
    r, c = w.shape[-2:]
    tr = _row_tile(r)
    into = [] if into is None else list(into)

    def body(me_ref, w_ref, land_ref, own_ref, m_ref, v_ref, *rest):
        own = own_ref[...].astype(F32)
        gv = jnp.where(me_ref[0] == 0, own, land_ref[0].astype(F32))
        for s in range(1, N_DEV):
            gv = gv + jnp.where(me_ref[0] == s, own, land_ref[s].astype(F32))
        _adamw_update(gv, w_ref, m_ref, v_ref, *rest[len(into):])

    slab = pl.BlockSpec((None, tr, c), lambda i, me: (layer, i, 0))
    return pl.pallas_call(
        body, name=name, out_shape=[jax.ShapeDtypeStruct(w.shape, F32)] * 4,
        grid_spec=pltpu.PrefetchScalarGridSpec(
            num_scalar_prefetch=1, grid=(r // tr,),
            in_specs=[slab, pl.BlockSpec((N_DEV, tr, c), lambda i, me: (0, i, 0)),
                      pl.BlockSpec((None, tr, c), lambda i, me: (me[0], i, 0)), slab, slab]
            + [pl.BlockSpec(memory_space=pl.ANY)] * len(into),
            out_specs=[slab] * 4),
        input_output_aliases={6 + k: k for k in range(len(into))},
    )(me1, w, land, sent, m, v, *into)


def adamw(name, w, g, m, v, layer=None):
    r, c = w.shape[-2:]
    tr = _row_tile(r)
    stacked = g.ndim == 3

    def body(w_ref, g_ref, m_ref, v_ref, *outs):
        _adamw_update(_slot_sum(g_ref) if stacked else g_ref[...], w_ref, m_ref, v_ref, *outs)

    tile = pl.BlockSpec((tr, c), lambda i: (i, 0))
    slab = tile if layer is None else pl.BlockSpec((None, tr, c), lambda i: (layer, i, 0))
    g_spec = pl.BlockSpec((g.shape[0], tr, c), lambda i: (0, i, 0)) if stacked else tile
    return pl.pallas_call(
        body, name=name, grid=(r // tr,), in_specs=[slab, g_spec, slab, slab], out_specs=[tile] * 4,
        out_shape=[jax.ShapeDtypeStruct((r, c), F32)] * 4,
    )(w, g, m, v)


def _place():
    return lax.axis_index("x"), lax.axis_index("y"), lax.axis_index("c")


def all_gather(name, x):
    r, c = x.shape
    space = pltpu.VMEM

    def body(x_ref, out_ref, send_sems, recv_sems, local_sem):
        px, py, pc = _place()
        me, sibling = (px, py, pc), (px, py, 1 - pc)
        chips = [(1 - px, py), (px, 1 - py), (1 - px, 1 - py)]

        def rows(qx, qy, qc):
            return out_ref.at[pl.ds((4 * qx + 2 * qy + qc) * r, r), :]

        def copy(k, block, to, src=None):
            return pltpu.make_async_remote_copy(
                src_ref=rows(*block) if src is None else src, dst_ref=rows(*block),
                send_sem=send_sems.at[k], recv_sem=recv_sems.at[k], device_id=to, device_id_type=MESH)

        mine = pltpu.make_async_copy(x_ref, rows(*me), local_sem)
        mine.start()
        first = [copy(0, me, sibling, src=x_ref)]
        first += [copy(1 + j, me, (*chip, pc), src=x_ref) for j, chip in enumerate(chips)]
        for cp in first:
            cp.start()
        passed = [copy(4 + j, (*chip, pc), sibling) for j, chip in enumerate(chips)]
        for j, chip in enumerate(chips):
            copy(1 + j, (*chip, pc), me).wait_recv()
            passed[j].start()
        copy(0, sibling, me).wait_recv()
        for j, chip in enumerate(chips):
            copy(4 + j, (*chip, 1 - pc), me).wait_recv()
        for cp in first + passed:
            cp.wait_send()
        mine.wait()

    return pl.pallas_call(
        body, name=name, out_shape=jax.ShapeDtypeStruct((N_DEV * r, c), x.dtype),
        in_specs=[pl.BlockSpec(memory_space=space)], out_specs=pl.BlockSpec(memory_space=space),
        scratch_shapes=[pltpu.SemaphoreType.DMA((7,)), pltpu.SemaphoreType.DMA((7,)), pltpu.SemaphoreType.DMA],
    )(x)


_HBM = pl.BlockSpec(memory_space=pltpu.HBM)
_SEM = pl.BlockSpec(memory_space=pltpu.SEMAPHORE)
_EFFECT = pltpu.SideEffectType.DATAFLOW_SIDE_EFFECTING


def _peers():
    px, py, pc = _place()
    return [(1 - px if k & 4 else px, 1 - py if k & 2 else py, 1 - pc if k & 1 else pc) for k in range(1, N_DEV)]


def _slot(dev):
    return 4 * dev[0] + 2 * dev[1] + dev[2]


def _split_copies(src_refs, land_refs, send_sems, recv_sems, gather):
    me = _slot(_place())
    return [pltpu.make_async_remote_copy(
        src_ref=src if gather else src.at[_slot(peer)], dst_ref=land.at[me],
        send_sem=send_sems.at[a * (N_DEV - 1) + k], recv_sem=recv_sems.at[a * (N_DEV - 1) + k],
        device_id=peer, device_id_type=MESH)
        for a, (src, land) in enumerate(zip(src_refs, land_refs)) for k, peer in enumerate(_peers())]


def exchange_start(name, srcs, gather, after):
    n = len(srcs)
    lands = [pltpu.HBM((N_DEV,) + s.shape if gather else s.shape, s.dtype) for s in srcs]

    def body(*refs):
        send_sems, recv_sems = refs[2 * n + 1:2 * n + 3]
        for cp in _split_copies(refs[:n], refs[n:2 * n], send_sems, recv_sems, gather):
            cp.start()
        refs[-1][...] = jnp.zeros_like(refs[-1])

    sems = pltpu.SemaphoreType.DMA((n * (N_DEV - 1),))
    res = pl.pallas_call(
        body, name=name,
        out_shape=(sems, sems, *[pltpu.HBM(s.shape, s.dtype) for s in srcs], *lands, jax.ShapeDtypeStruct((8, 128), F32)),
        in_specs=(_HBM,) * (2 * n) + (pl.BlockSpec(memory_space=pl.ANY),),
        out_specs=(_SEM, _SEM) + (_HBM,) * (2 * n) + (pl.BlockSpec(memory_space=pltpu.VMEM),),
        input_output_aliases={i: 2 + i for i in range(2 * n)},
        compiler_params=pltpu.CompilerParams(has_side_effects=_EFFECT),
    )(*[pltpu.with_memory_space_constraint(s, pltpu.HBM) for s in srcs],
      *[pltpu.with_memory_space_constraint(lax.empty(ld.shape, ld.dtype), pltpu.HBM) for ld in lands], after)
    return res[0], res[1], list(res[2:2 + n]), list(res[2 + n:2 + 2 * n]), res[-1]


def exchange_wait(name, started, after, gather):
    send_sems, recv_sems, srcs, lands, _ = started
    n = len(srcs)
    after = list(after) if isinstance(after, (list, tuple)) else [after]

    def body(*refs):
        send_sems, recv_sems = refs[2 * n:2 * n + 2]
        for cp in _split_copies(refs[:n], refs[n:2 * n], send_sems, recv_sems, gather):
            cp.wait_send()
            cp.wait_recv()

    res = pl.pallas_call(
        body, name=name, out_shape=tuple(pltpu.HBM(a.shape, a.dtype) for a in srcs + lands),
        in_specs=(_HBM,) * (2 * n) + (_SEM, _SEM) + (pl.BlockSpec(memory_space=pl.ANY),) * len(after),
        out_specs=(_HBM,) * (2 * n), input_output_aliases={i: i for i in range(2 * n)},
        compiler_params=pltpu.CompilerParams(has_side_effects=_EFFECT),
    )(*srcs, *lands, send_sems, recv_sems, *after)
    return list(res[:n]), list(res[n:])


NCF = FFN_H // FFN_TC


def _size(shape):
    n = 1
    for s in shape:
        n *= s
    return n


def _padded_rows(n_elems, row_mult):
    return -(-n_elems // (D * row_mult)) * row_mult


def _pack_rows(arrs, dtype, row_mult):
    rows, offs, r0 = [], [], 0
    for a in arrs:
        flat = a.reshape(-1).astype(dtype)
        n = _padded_rows(flat.shape[0], row_mult)
        rows.append(jnp.pad(flat, (0, n * D - flat.shape[0])).reshape(n, D))
        offs.append(r0)
        r0 += n
    return jnp.concatenate(rows, 0), offs


def _unpack_rows(buf, offs, shapes):
    lead, out = buf.shape[:-2], []
    for o, shp in zip(offs, shapes):
        n = _size(shp)
        nr = -(-n // D)
        out.append(buf[..., o:o + nr, :].reshape(lead + (nr * D,))[..., :n].reshape(lead + tuple(shp)))
    return out


def _rows3(w):
    return [w[i:i + 1] for i in range(3)]


def f_mod1(xs, ps):
    return f_mod(xs, ps)[:1]


def kernel(x, c, ctx, c_ctx, ada_w, ada_b, norm_mix, norm_ffn, gla_w_in, gla_w_a2, gla_b_a, gla_head_norm, gla_w_out, sc_w_in, sc_conv_w, sc_w_out, ffn_w_up, ffn_conv_w, ffn_conv_b, ffn_w_down, final_norm, loss_target, m_c_ctx, m_ada_w, m_ada_b, m_norm_mix, m_norm_ffn, m_gla_w_in, m_gla_w_a2, m_gla_b_a, m_gla_head_norm, m_gla_w_out, m_sc_w_in, m_sc_conv_w, m_sc_w_out, m_ffn_w_up, m_ffn_conv_w, m_ffn_conv_b, m_ffn_w_down, m_final_norm, v_c_ctx, v_ada_w, v_ada_b, v_norm_mix, v_norm_ffn, v_gla_w_in, v_gla_w_a2, v_gla_b_a, v_gla_head_norm, v_gla_w_out, v_sc_w_in, v_sc_conv_w, v_sc_w_out, v_ffn_w_up, v_ffn_conv_w, v_ffn_conv_b, v_ffn_w_down, v_final_norm):
    names = ["c_ctx", "ada_w", "ada_b", "norm_mix", "norm_ffn", "gla_w_in", "gla_w_a2", "gla_b_a", "gla_head_norm",
             "gla_w_out", "sc_w_in", "sc_conv_w", "sc_w_out", "ffn_w_up", "ffn_conv_w", "ffn_conv_b", "ffn_w_down",
             "final_norm"]
    w_ = dict(zip(names, [c_ctx, ada_w, ada_b, norm_mix, norm_ffn, gla_w_in, gla_w_a2, gla_b_a, gla_head_norm, gla_w_out,
                          sc_w_in, sc_conv_w, sc_w_out, ffn_w_up, ffn_conv_w, ffn_conv_b, ffn_w_down, final_norm]))
    m_ = dict(zip(names, [m_c_ctx, m_ada_w, m_ada_b, m_norm_mix, m_norm_ffn, m_gla_w_in, m_gla_w_a2, m_gla_b_a,
                          m_gla_head_norm, m_gla_w_out, m_sc_w_in, m_sc_conv_w, m_sc_w_out, m_ffn_w_up, m_ffn_conv_w,
                          m_ffn_conv_b, m_ffn_w_down, m_final_norm]))
    v_ = dict(zip(names, [v_c_ctx, v_ada_w, v_ada_b, v_norm_mix, v_norm_ffn, v_gla_w_in, v_gla_w_a2, v_gla_b_a,
                          v_gla_head_norm, v_gla_w_out, v_sc_w_in, v_sc_conv_w, v_sc_w_out, v_ffn_w_up, v_ffn_conv_w,
                          v_ffn_conv_b, v_ffn_w_down, v_final_norm]))
    me = 4 * lax.axis_index("x") + 2 * lax.axis_index("y") + lax.axis_index("c")
    bsz = x.shape[0]
    tm = 256
    nt = SEQ // tm
    ctx_tiles = CTX // tm
    pe = functools.partial(P, per_example=True)

    groups = {"ffn1": [("ffn_w_up", 1), ("ffn_w_down", 1)], "sc": [("sc_w_in", 0), ("sc_w_out", 0)],
              "ffn0": [("ffn_w_up", 0), ("ffn_w_down", 0)], "gla": [("gla_w_in", 0), ("gla_w_out", 0)]}
    ag_groups = {"gin": [("gla_w_in", 0)], "ffn0": [("gla_w_out", 0), ("ffn_w_up", 0), ("ffn_w_down", 0)],
                 "sc": groups["sc"], "ffn1": groups["ffn1"]}
    ag_started = {}

    def start_gather(g, after):
        ag_started[g] = exchange_start(f"ag_{g}_start", [w_[n][i].astype(BF16) for n, i in ag_groups[g]], True, after)
        return ag_started[g][4]

    small_sharded = [c, gla_w_a2, gla_b_a, sc_conv_w, ffn_conv_w]
    pack0, offs0 = _pack_rows(small_sharded, F32, 8)
    g0 = all_gather("ag_small", pack0).reshape(N_DEV, pack0.shape[0], D)
    c_all, wa2_s, ba_s, scw_s, fcw_s = _unpack_rows(g0, offs0, [a.shape for a in small_sharded])
    w_a2 = wa2_s[:, 0].transpose(1, 2, 0, 3).reshape(2, RANK, KD)
    b_a = ba_s[:, 0].transpose(1, 0, 2).reshape(2, KD)
    sc_cw = scw_s[:, 0].transpose(1, 0, 2).reshape(3, D)
    ffn_cw = fcw_s.transpose(1, 2, 0, 3).reshape(2, 3, 2 * FFN_H)

    cond = jnp.concatenate([c_all.reshape(N_DEV * bsz, D), c_ctx[None], jnp.zeros((ADA_ROWS - N_DEV * bsz - 1, D), F32)], 0)
    b_mine = lax.dynamic_slice(ada_b, (0, me * ADA_COLS), (2, ADA_COLS)).reshape(2, 1, ADA_COLS)
    mod_part = ada_fwd(cond, ada_w, b_mine)
    mod = all_gather("ag_mod", mod_part.reshape(2 * ADA_ROWS, ADA_COLS))
    mod = mod.reshape(N_DEV, 2, ADA_ROWS, ADA_COLS).transpose(1, 2, 0, 3).reshape(2, ADA_ROWS, 6 * D)
    mods = lax.dynamic_slice(mod, (0, bsz * me, 0), (2, bsz, 6 * D))
    md = [[mods[i][:, k * D:(k + 1) * D].reshape(bsz, 1, D) for k in range(6)] for i in range(2)]
    mc = [mod[0, ADA_CTX_ROW, k * D:(k + 1) * D][None] for k in range(2)]

    tok = mod
    for g in ag_groups:
        tok = start_gather(g, tok)
    norm_mix = norm_mix + tok[0, 0]

    def gathered(g, after):
        mine, lands = exchange_wait(f"ag_{g}_wait", ag_started[g], after, True)
        return [lax.dynamic_update_index_in_dim(ld, mn, me, 0) for ld, mn in zip(lands, mine)]

    s_up, w_down = [None, None], [None, None]
    wd = jnp.zeros((128, 2 * KD), F32).at[:RANK, :KD].set(w_a2[0]).at[RANK:2 * RANK, KD:].set(w_a2[1])
    bd = b_a.reshape(1, 2 * KD)
    scw = _rows3(sc_cw)
    head_gain = gla_head_norm.reshape(1, HV)
    gains_mix = [norm_mix[i][None] for i in range(2)]
    gains_ffn = [norm_ffn[i][None] for i in range(2)]

    def tokens(a2d, t_len):
        return a2d.reshape(bsz, t_len, -1)

    def ffn_params(i):
        rows = [ffn_cw[i][t] for t in range(3)] + [ffn_conv_b[i]]
        return [P(a.reshape(2, FFN_H), w=FFN_TC, rows=True) for a in rows]

    def ffn_fwd(i, hn2):
        u = mm(f"ffn_up{i}", V(hn2, "tok"), V(s_up[i], "cols"), out="planes", out_dtype=BF16, planes_t=SEQ)
        act = rowwise(f"ffn_mid{i}", f_ffn_mid, [X(u, w=FFN_TC, planes=True)], ffn_params(i), tm=SEQ, nt=1, nc=NCF,
                      outs=[(FFN_TC, BF16, 1)])[0]
        return u, act

    def arrays(ps):
        return [p["a"] for p in ps]

    ps_in0 = [P(gains_mix[0]), pe(md[0][0]), pe(md[0][1])]
    ps_ctx = [P(gains_mix[0]), P(mc[0]), P(mc[1])]
    hn0 = rowwise("mod_in0", f_mod, [X(x)], ps_in0, tm=2 * tm, nt=nt // 2, outs=[(D, BF16, 1)])[0]
    hnc = rowwise("mod_ctx", f_mod, [X(ctx)], ps_ctx, tm=tm, nt=ctx_tiles, outs=[(D, BF16, 1)])[0]
    hcat = jnp.concatenate([hnc, hn0], axis=1)
    (s_gin,) = gathered("gin", hcat)
    w_gin = V(s_gin, "cols", width=GLA_IN_PAD)
    pcat = tokens(mm("gla_in", V(hcat, "tok"), w_gin, out_dtype=BF16), TT)
    pa_x = X(pcat, w=128, co=(GLA_IN_PAD - 128) // 128)
    la = rowwise("gla_decay", f_decay, [pa_x], [P(wd), P(bd)], tm=3 * tm, nt=TT // (3 * tm), outs=[(2 * KD, F32, 1)])[0]
    o2, s_all = gla_fwd(pcat, la)
    post_xs = [X(o2, w=VD, co=0, ro=ctx_tiles, split=HEADS), X(o2, w=VD, co=1, ro=ctx_tiles, split=HEADS),
               X(pcat, w=VD, co=2, ro=ctx_tiles, split=HEADS)]
    yin0 = rowwise("gla_post", f_gla_post, post_xs, [P(head_gain)], tm=tm, nt=nt, outs=[(VD, BF16, HEADS)])[0]
    s_gout, s_up[0], s_down0 = gathered("ffn0", yin0)
    w_gout, w_down[0] = s_gout.reshape(VD, D), s_down0.reshape(FFN_H, D)
    ps_mid0 = [pe(md[0][2]), P(gains_ffn[0]), pe(md[0][3]), pe(md[0][4])]
    y0, h1_0, hn2_0 = mm_res_mod("gla_out", yin0, w_gout, x, *arrays(ps_mid0))
    u0, act0 = ffn_fwd(0, hn2_0)
    ps_in1 = [pe(md[0][5]), P(gains_mix[1]), pe(md[1][0]), pe(md[1][1])]
    fo0, h2_0, hn1 = mm_res_mod("ffn_down0", act0, w_down[0], h1_0, *arrays(ps_in1))

    s_sin, s_sout = gathered("sc", hn1)
    w_sout = s_sout.reshape(D, D)
    p1 = tokens(mm("sc_in", V(hn1, "tok"), V(s_sin, "cols")), SEQ)
    sc_ps = [P(a) for a in scw]
    yin1 = rowwise("sc_mid", f_sc_mid, [X(p1, split=3)], sc_ps, tm=2 * tm, nt=nt // 2, outs=[(D, BF16, 1)])[0]
    ps_mid1 = [pe(md[1][2]), P(gains_ffn[1]), pe(md[1][3]), pe(md[1][4])]
    y1, h1_1, hn2_1 = mm_res_mod("sc_out", yin1, w_sout, h2_0, *arrays(ps_mid1))
    s_up[1], s_down1 = gathered("ffn1", hn2_1)
    w_down[1] = s_down1.reshape(FFN_H, D)
    u1, act1 = ffn_fwd(1, hn2_1)
    fo1 = tokens(mm("ffn_down1", V(act1, "tok"), V(w_down[1])), SEQ)
    loss8, dh1_1, dfo1, dm5_1, g_final = final_loss(h1_1, fo1, md[1][5], final_norm[None], loss_target)

    def ffn_bwd(i, u, act, hn2, dfo):
        dact = tokens(mm(f"ffn_down_dx{i}", V(dfo, "tok"), V(w_down[i]), form="nt", out_dtype=BF16), SEQ)
        g_down = mm(f"ffn_down_dw{i}", V(act, "tok"), V(dfo, "tok"), form="tn", out_dtype=BF16)
        r = rowwise(f"ffn_mid_bwd{i}", f_ffn_mid, [X(u, w=FFN_TC, planes=True)], ffn_params(i), tm=SEQ, nt=1, nc=NCF,
                    douts=[X(dact, w=FFN_TC)], dx={0: BF16}, dp=[0, 1, 2, 3])
        du, g_cw, g_cb = r[0], jnp.stack([a.reshape(2 * FFN_H) for a in r[1:4]]), r[4].reshape(1, 2 * FFN_H)
        dhn2 = tokens(mm(f"ffn_up_dx{i}", V(du, "planes"), V(s_up[i], "cols"), form="nt", out_dtype=BF16), SEQ)
        g_up = mm(f"ffn_up_dw{i}", V(hn2, "tok"), V(du, "planes"), form="tn", out="cols", out_dtype=BF16)
        return dhn2, g_up, row_slots(g_down), g_cw, g_cb

    def res_mod_bwd(name, h, y, ps, dh1, dhn):
        return rowwise(name, f_res_mod, [X(h), X(y)], ps, tm=2 * tm, nt=nt // 2, douts=[X(dh1), X(dhn)],
                       dx={0: F32, 1: BF16}, dp=[0, 1, 2, 3])

    def row_slots(g):
        return g.reshape(N_DEV, -1, g.shape[-1])

    a2a_started = {}

    def send_grads(g, slots, after=None):
        a2a_started[g] = exchange_start(f"a2a_{g}_start", list(slots), False, loss8 if after is None else after)
        return a2a_started[g][4][0, 0]

    def after_start(ps, tok):
        return [dict(ps[0], a=ps[0]["a"] + tok)] + ps[1:]

    dhn2_1, g_up1, g_down1, g_fcw1, g_fcb1 = ffn_bwd(1, u1, act1, hn2_1, dfo1)
    tok = send_grads("ffn1", [g_up1, g_down1])
    dh2_0, dy1, dm2_1, g_nffn1, dm3_1, dm4_1 = res_mod_bwd("res_mod_mid1_bwd", h2_0, y1, after_start(ps_mid1, tok), dh1_1, dhn2_1)
    dyin1 = tokens(mm("sc_out_dx", V(dy1, "tok"), V(w_sout), form="nt", out_dtype=BF16), SEQ)
    g_sout = row_slots(mm("sc_out_dw", V(yin1, "tok"), V(dy1, "tok"), form="tn", out_dtype=BF16))
    r = rowwise("sc_mid_bwd", f_sc_mid, [X(p1, split=3)], sc_ps, tm=2 * tm, nt=nt // 2, douts=[X(dyin1)], dx={0: BF16},
                dp=[0, 1, 2])
    dp1, g_scw = r[0], jnp.concatenate(r[1:4], 0)
    dhn1 = tokens(mm("sc_in_dx", V(dp1, "tok"), V(s_sin, "cols"), form="nt", out_dtype=BF16), SEQ)
    g_sin = mm("sc_in_dw", V(hn1, "tok"), V(dp1, "tok"), form="tn", out="cols", out_dtype=BF16)
    tok = send_grads("sc", [g_sin, g_sout])
    dh1_0, dfo0, dm5_0, g_nmix1, dm0_1, dm1_1 = res_mod_bwd("res_mod_in1_bwd", h1_0, fo0, after_start(ps_in1, tok), dh2_0, dhn1)

    dhn2_0, g_up0, g_down0, g_fcw0, g_fcb0 = ffn_bwd(0, u0, act0, hn2_0, dfo0)
    tok = send_grads("ffn0", [g_up0, g_down0])
    dx_res, dy0, dm2_0, g_nffn0, dm3_0, dm4_0 = res_mod_bwd("res_mod_mid0_bwd", x, y0, after_start(ps_mid0, tok), dh1_0, dhn2_0)
    dyin0 = tokens(mm("gla_out_dx", V(dy0, "tok"), V(w_gout), form="nt", out_dtype=BF16), SEQ)
    do, dgate, g_head = rowwise("gla_post_bwd", f_gla_post, post_xs, [P(head_gain)], tm=tm, nt=nt,
                                douts=[X(dyin0, split=HEADS)], dx={0: BF16, 2: BF16}, dp=[0])
    dq2, dk2, dv2, dla = gla_bwd(pcat, la, s_all, do)
    dpa, g_wd, g_bd = rowwise("gla_decay_bwd", f_decay, [pa_x], [P(wd), P(bd)], tm=3 * tm, nt=TT // (3 * tm), douts=[X(dla)],
                              dx={0: BF16}, dp=[0, 1])
    dpcat = gla_combine(dq2, dk2, dv2, dgate, dpa)
    dhcat = tokens(mm("gla_in_dx", V(dpcat, "tok"), w_gin, form="nt", out_dtype=BF16), TT)
    grad_x, g_nmix0, dm0_0, dm1_0 = rowwise("mod_in0_bwd", f_mod, [X(x)], ps_in0, tm=tm, nt=nt,
                                            douts=[X(dhcat, ro=ctx_tiles), X(dx_res)], dx={0: F32}, dp=[0, 1, 2])
    g_nmix0c, dmc0, dmc1 = rowwise("mod_ctx_bwd", f_mod1, [X(ctx)], ps_ctx, tm=tm, nt=ctx_tiles, douts=[X(dhcat)],
                                   dx={}, dp=[0, 1, 2])

    zero_row = jnp.zeros((1, 4 * D), F32)
    dmod = [jnp.concatenate([jnp.concatenate([a.reshape(bsz, D) for a in dms], 1), ctx_row], 0)
            for dms, ctx_row in (([dm0_0, dm1_0, dm2_0, dm3_0, dm4_0, dm5_0], jnp.concatenate([dmc0, dmc1, zero_row], 1)),
                                 ([dm0_1, dm1_1, dm2_1, dm3_1, dm4_1, dm5_1], jnp.zeros((1, 6 * D), F32)))]
    g_wa2 = jnp.stack([g_wd[:RANK, :KD], g_wd[RANK:2 * RANK, KD:]])
    small_grads = [jnp.stack(dmod), jnp.concatenate([g_nmix0 + g_nmix0c, g_nmix1], 0), jnp.concatenate([g_nffn0, g_nffn1], 0),
                   g_head, jnp.concatenate([g_fcb0, g_fcb1], 0), g_final, g_wa2, g_bd.reshape(2, KD), g_scw,
                   jnp.stack([g_fcw0, g_fcw1]), loss8[:1]]
    pack1, offs1 = _pack_rows(small_grads, F32, 8)
    ag1 = exchange_start("ag_grads_start", [pack1], True, loss8)
    g_gin = mm("gla_in_dw", V(hcat, "tok"), V(dpcat, "tok"), form="tn", out="cols", out_dtype=BF16, shard_n=GLA_IN // N_DEV,
               after=ag1[4])
    g_gout = row_slots(mm("gla_out_dw", V(yin0, "tok"), V(dy0, "tok"), form="tn", out_dtype=BF16, after=ag1[4]))
    mine1, land1 = exchange_wait("ag_grads_wait", ag1, [g_gin, g_gout], True)
    g1 = lax.dynamic_update_index_in_dim(land1[0], mine1[0], me, 0)
    dmod_all = _unpack_rows(g1, offs1[:1], [small_grads[0].shape])[0]
    tot = _unpack_rows(sum_slots("sum_small", g1), offs1, [a.shape for a in small_grads])
    loss = tot[10][0, 0]
    dm_rows = dmod_all[:, :, :bsz].transpose(1, 0, 2, 3).reshape(2, N_DEV * bsz, 6 * D)
    dm_full = jnp.concatenate([dm_rows, tot[0][:, bsz:], jnp.zeros((2, ADA_ROWS - N_DEV * bsz - 1, 6 * D), F32)], 1)
    dm_mine = lax.dynamic_slice(dm_full, (0, 0, me * ADA_COLS), (2, ADA_ROWS, ADA_COLS))
    g_ada_w, g_ada_b, cpart = ada_bwd(cond, dm_mine, dm_full, ada_w)
    cparts = all_gather("ag_cctx", cpart).reshape(N_DEV, ADA_ROWS - ADA_CTX_ROW, D)[:, 0]
    g_cctx = cctx_grad(cparts, c_ctx[None])[0]
    tok = send_grads("gla", [g_gin, g_gout], after=g_cctx)

    def my_cols(full, n):
        return lax.dynamic_slice_in_dim(full, me * n, n, axis=full.ndim - 1)

    grads = {
        "c_ctx": g_cctx, "ada_b": g_ada_b.reshape(2, 6 * D), "norm_mix": tot[1], "norm_ffn": tot[2],
        "gla_head_norm": tot[3], "ffn_conv_b": tot[4], "final_norm": tot[5].reshape(D),
        "gla_w_a2": my_cols(tot[6], KD // N_DEV)[None], "gla_b_a": my_cols(tot[7], KD // N_DEV)[None],
        "sc_conv_w": my_cols(tot[8], D // N_DEV)[None], "ffn_conv_w": my_cols(tot[9], 2 * FFN_H // N_DEV),
    }

    res_ada = adamw("adamw_ada", *[a.reshape(2 * D, ADA_COLS) for a in (ada_w, g_ada_w, m_ada_w, v_ada_w)])
    grads["c_ctx"] = g_cctx + tok
    big = ["gla_w_in", "gla_w_out", "sc_w_in", "sc_w_out", "ffn_w_up", "ffn_w_down"]
    small = [n for n in names if n not in big and n != "ada_w"]
    g_small = _pack_rows([grads[n] for n in small], F32, 8)[0]
    res_small = adamw("adamw_small", _pack_rows([w_[n] for n in small], F32, 8)[0], g_small,
                      _pack_rows([m_[n] for n in small], F32, 8)[0], _pack_rows([v_[n] for n in small], F32, 8)[0])
    offs_s = _pack_rows([w_[n] for n in small], F32, 8)[1]

    big_res, done, me1 = {}, {"small": res_small[0], "ada_w": res_ada[0]}, jnp.reshape(me, (1,)).astype(jnp.int32)
    for g in groups:
        sent, lands = exchange_wait(f"a2a_{g}_wait", a2a_started[g], list(done.values()), False)
        for (n, i), mine, land in zip(groups[g], sent, lands):
            big_res[n] = adamw_slots(f"adamw_{n}{i}", w_[n], land, mine, me1, m_[n], v_[n], i, into=big_res.get(n))
            done[n] = big_res[n][0]

    out = {}
    for kind, idx in (("grad", 0), ("delta", 1), ("new_m", 2), ("new_v", 3)):
        vals = {n: big_res[n][idx] for n in big}
        vals["ada_w"] = res_ada[idx].reshape(ada_w.shape)
        vals.update(zip(small, _unpack_rows(res_small[idx], offs_s, [w_[n].shape for n in small])))
        out[kind] = [vals[n] for n in names]
    return (loss, grad_x, *out["grad"], *out["delta"], *out["new_m"], *out["new_v"])
```

```python
import functools

import jax
import jax.numpy as jnp
from jax import lax
from jax.experimental import pallas as pl
from jax.experimental.pallas import tpu as pltpu

F32 = jnp.float32
BF16 = jnp.bfloat16

N_DEV = 8
D = 1024
SEQ = 2048
CTX = 256
TT = CTX + SEQ
GRID_W = 64
CHUNK = 64
HEADS = 4
HK = 128
HV = 256
KD = 512
VD = 1024
RANK = 16
TAU = 16.0
GLA_IN = 3104
GLA_IN_PAD = 3200
FFN_H = 2560
FFN_TC = 256
EPS = 1e-6
LR, B1, B2, AEPS, WD, STEP = 0.001, 0.9, 0.999, 1e-08, 0.01, 10
MESH = pl.DeviceIdType.MESH


def _blocks(n):
    return [n] + [t for t in range(n - n % 128, 0, -128) if n % t == 0 and t != n]


def V(arr, kind="flat", width=None):
    if kind == "tok":
        return V(arr.reshape(-1, arr.shape[-1]))
    if kind == "flat":
        r, c = arr.shape
        return dict(a=arr, kind=kind, shape=(r, c), rows=_blocks(r), cols=_blocks(c))
    if kind == "planes":
        bsz, _, t, ch = arr.shape
        return dict(a=arr, kind=kind, shape=(bsz * t, 2 * ch), rows=_blocks(t), cols=[2 * ch] + _blocks(ch), t=t, ch=ch)
    _, r, n = arr.shape
    if width is not None:
        return dict(a=arr, kind=kind, shape=(r, width), rows=_blocks(r), cols=[width], n=n, pad=width - N_DEV * n)
    return dict(a=arr, kind=kind, shape=(r, N_DEV * n), rows=_blocks(r), cols=[8 * n, 4 * n, 2 * n], n=n, pad=0)


def _view_spec(v, br, bc, idx):
    if v["kind"] == "flat":
        return pl.BlockSpec((br, bc), idx)
    if v["kind"] == "planes":
        nt = v["t"] // br
        if bc == 2 * v["ch"]:
            return pl.BlockSpec((None, 2, br, v["ch"]), lambda i, j, k: (idx(i, j, k)[0] // nt, 0, idx(i, j, k)[0] % nt, 0))
        nch = v["ch"] // bc

        def at(i, j, k):
            r, c = idx(i, j, k)
            return r // nt, c // nch, r % nt, c % nch
        return pl.BlockSpec((None, None, br, bc), at)
    return pl.BlockSpec(((bc - v["pad"]) // v["n"], br, v["n"]), lambda i, j, k: (idx(i, j, k)[1], idx(i, j, k)[0], 0))


def _out_view(kind, rows, cols, dtype, planes_t=None, shard_n=None):
    if kind == "flat":
        shape = (rows, cols)
    elif kind == "planes":
        shape = (rows // planes_t, 2, planes_t, cols // 2)
    elif shard_n is not None:
        return V(jax.ShapeDtypeStruct((N_DEV, rows, shard_n), dtype), kind, width=cols)
    else:
        shape = (N_DEV, rows, cols // N_DEV)
    return V(jax.ShapeDtypeStruct(shape, dtype), kind)


MM_VMEM_BUDGET = 40 * 2 ** 20
MM_VMEM_LIMIT = 56 * 2 ** 20
MM_MAX_TILE = 1536


def _mm_tiles(m, n, kk, ms, ns, ks, a_bytes, b_bytes, o_bytes):
    best = None
    for tk in ks:
        for tm in [t for t in ms if t <= MM_MAX_TILE] or ms:
            for tn in [t for t in ns if t <= MM_MAX_TILE] or ns:
                one_k = tk == kk
                need = 2 * (tm * tk * a_bytes + tk * tn * b_bytes + tm * tn * o_bytes) + (0 if one_k else tm * tn * 4)
                if need > MM_VMEM_BUDGET:
                    continue
                steps = (m // tm) * (n // tn) * (kk // tk)
                traffic = (m * kk * a_bytes * (1 if one_k else n // tn)
                           + kk * n * b_bytes * (1 if one_k and n == tn else m // tm) + m * n * o_bytes)
                fill = (tm * tk * a_bytes + tk * tn * b_bytes) / 2.5e12
                cost = max(2.0 * m * n * kk / (9e14 if one_k else 6.5e14), traffic / 2.5e12) + steps * 0.4e-6 + fill
                if best is None or cost < best[0]:
                    best = (cost, tm, tn, tk)
    return best[1:]


def mm(name, a, b, form="nn", out="flat", out_dtype=F32, planes_t=None, shard_n=None, after=None):
    (m, kk) = a["shape"][::-1] if form == "tn" else a["shape"]
    n = b["shape"][0] if form == "nt" else b["shape"][1]
    assert (b["shape"][1] if form == "nt" else b["shape"][0]) == kk, (name, a["shape"], b["shape"])
    o = _out_view(out, m, n, out_dtype, planes_t, shard_n)
    a_m, a_k = (a["cols"], a["rows"]) if form == "tn" else (a["rows"], a["cols"])
    b_k, b_n = (b["cols"], b["rows"]) if form == "nt" else (b["rows"], b["cols"])
    tm, tn, tk = _mm_tiles(m, n, kk, [t for t in a_m if t in o["rows"]], [t for t in b_n if t in o["cols"]],
                           [t for t in a_k if t in b_k], a["a"].dtype.itemsize, b["a"].dtype.itemsize,
                           jnp.dtype(out_dtype).itemsize)
    nk = kk // tk
    dn = (((0 if form == "tn" else 1,), (1 if form == "nt" else 0,)), ((), ()))

    def load(ref, v):
        if len(ref.shape) == 3:
            pieces = [ref[p].astype(BF16) for p in range(ref.shape[0])]
            if v.get("pad"):
                pieces.append(jnp.zeros(ref.shape[1:2] + (v["pad"],), BF16))
            return jnp.concatenate(pieces, axis=-1)
        return ref[...].astype(BF16)

    def store(o_ref, val):
        val = val.astype(out_dtype)
        if len(o_ref.shape) == 3:
            w = o_ref.shape[-1]
            for p in range(o_ref.shape[0]):
                o_ref[p] = val[:, p * w:(p + 1) * w]
        else:
            o_ref[...] = val

    def body(a_ref, b_ref, *rest):
        o_ref, acc = rest[0 if after is None else 1], rest[1 if after is None else 2:]
        if nk == 1:
            store(o_ref, lax.dot_general(load(a_ref, a), load(b_ref, b), dn, preferred_element_type=F32))
            return
        k, acc_ref = pl.program_id(2), acc[0]

        @pl.when(k == 0)
        def _():
            acc_ref[...] = jnp.zeros_like(acc_ref)

        acc_ref[...] += lax.dot_general(load(a_ref, a), load(b_ref, b), dn, preferred_element_type=F32)

        @pl.when(k == nk - 1)
        def _():
            store(o_ref, acc_ref[...])

    if form == "tn":
        a_spec = _view_spec(a, tk, tm, lambda i, j, k: (k, i))
    else:
        a_spec = _view_spec(a, tm, tk, lambda i, j, k: (i, k))
    if form == "nt":
        b_spec = _view_spec(b, tn, tk, lambda i, j, k: (j, k))
    else:
        b_spec = _view_spec(b, tk, tn, lambda i, j, k: (k, j))
    return pl.pallas_call(
        body, name=name, grid=(m // tm, n // tn, nk),
        in_specs=[a_spec, b_spec] + ([] if after is None else [pl.BlockSpec(memory_space=pl.ANY)]),
        out_specs=_view_spec(o, tm, tn, lambda i, j, k: (i, j)), out_shape=o["a"],
        scratch_shapes=[pltpu.VMEM((tm, tn), F32)] if nk > 1 else [],
        compiler_params=pltpu.CompilerParams(dimension_semantics=("parallel", "parallel", "arbitrary"),
                                             vmem_limit_bytes=MM_VMEM_LIMIT),
    )(a["a"], b["a"], *([] if after is None else [after]))


def mm_res_mod(name, a, w, h, gate, gain, shift, scale):
    bsz, t_len, kk = a.shape
    tm = 1024 if kk <= D else 512
    per = t_len // tm

    def body(a_ref, w_ref, h_ref, gate_ref, gain_ref, shift_ref, scale_ref, y_ref, h1_ref, hn_ref):
        y = jnp.dot(a_ref[...].astype(BF16), w_ref[...].astype(BF16), preferred_element_type=F32)
        h1 = h_ref[...] + gate_ref[...] * y
        y_ref[...] = y.astype(BF16)
        h1_ref[...] = h1
        hn_ref[...] = _mod(h1, gain_ref[...], shift_ref[...], scale_ref[...]).astype(BF16)

    def tile(width):
        return pl.BlockSpec((None, tm, width), lambda i: (i // per, i % per, 0))

    per_ex = pl.BlockSpec((None, 1, D), lambda i: (i // per, 0, 0))
    return pl.pallas_call(
        body, name=name, grid=(bsz * per,),
        in_specs=[tile(kk), pl.BlockSpec((kk, D), lambda i: (0, 0)), tile(D), per_ex, pl.BlockSpec((1, D), lambda i: (0, 0)),
                  per_ex, per_ex],
        out_specs=[tile(D)] * 3,
        out_shape=[jax.ShapeDtypeStruct((bsz, t_len, D), BF16), jax.ShapeDtypeStruct((bsz, t_len, D), F32),
                   jax.ShapeDtypeStruct((bsz, t_len, D), BF16)],
        compiler_params=pltpu.CompilerParams(dimension_semantics=("parallel",), vmem_limit_bytes=MM_VMEM_LIMIT),
    )(a, w, h, gate, gain, shift, scale)


def X(arr, w=None, co=0, ro=0, split=1, planes=False):
    return dict(a=arr, w=arr.shape[-1] if w is None else w, co=co, ro=ro, split=2 if planes else split,
                mode="planes" if planes else "cols")


def P(arr, per_example=False, w=None, split=1, rows=False):
    return dict(a=arr, e=per_example, w=arr.shape[-1] if w is None else w, split=arr.shape[-2] if rows else split,
                mode="rows" if rows else "cols")


def _pieces(ref, s):
    if s["mode"] == "planes":
        return [ref[0], ref[1]]
    if s["mode"] == "rows":
        return [ref[i:i + 1, :] for i in range(s["split"])]
    w = ref.shape[-1] // s["split"]
    return [ref[:, i * w:(i + 1) * w] for i in range(s["split"])]


def _store(ref, pieces, s, accumulate=False):
    w = ref.shape[-1] // len(pieces)
    for i, p in enumerate(pieces):
        at = (i,) if s["mode"] == "planes" else (slice(i, i + 1),) if s["mode"] == "rows" else (slice(None), slice(i * w, (i + 1) * w))
        if accumulate:
            ref[at] += p.astype(ref.dtype)
        else:
            ref[at] = p.astype(ref.dtype)


def rowwise(name, f, xs, ps, *, tm, nt, nc=1, outs=None, douts=None, dx=None, dp=None):
    bsz = xs[0]["a"].shape[0]
    fwd = douts is None
    nx, np_ = len(xs), len(ps)
    douts = [] if fwd else douts
    dx = {} if fwd else dx
    dp = [] if fwd else dp

    def x_spec(s):
        if s["mode"] == "planes":
            return pl.BlockSpec((None, 2, tm, s["w"]), lambda c, b, t, s=s: (b, 0, t + s["ro"], c + s["co"]))
        return pl.BlockSpec((None, tm, s["w"]), lambda c, b, t, s=s: (b, t + s["ro"], c + s["co"]))

    def x_out(s, dt):
        if s["mode"] == "planes":
            return (jax.ShapeDtypeStruct((bsz, 2, nt * tm, nc * s["w"]), dt),
                    pl.BlockSpec((None, 2, tm, s["w"]), lambda c, b, t: (b, 0, t, c)))
        return (jax.ShapeDtypeStruct((bsz, nt * tm, nc * s["w"]), dt), pl.BlockSpec((None, tm, s["w"]), lambda c, b, t: (b, t, c)))

    def p_spec(s):
        r = s["a"].shape[-2]
        if s["e"]:
            return pl.BlockSpec((None, r, s["w"]), lambda c, b, t: (b, 0, c))
        return pl.BlockSpec((r, s["w"]), lambda c, b, t: (0, c))

    in_specs = [x_spec(s) for s in xs] + [p_spec(s) for s in ps] + [x_spec(s) for s in douts]
    operands = [s["a"] for s in xs] + [s["a"] for s in ps] + [s["a"] for s in douts]
    if fwd:
        out_modes = [dict(mode="cols", split=sp) for (_, _, sp) in outs]
        out_shape = [jax.ShapeDtypeStruct((bsz, nt * tm, nc * w), dt) for (w, dt, _) in outs]
        out_specs = [pl.BlockSpec((None, tm, w), lambda c, b, t: (b, t, c)) for (w, _, _) in outs]
    else:
        dx_outs = [x_out(xs[i], dt) for i, dt in dx.items()]
        out_shape, out_specs = [o[0] for o in dx_outs], [o[1] for o in dx_outs]
        for j in dp:
            s = ps[j]
            r = s["a"].shape[-2]
            if s["e"]:
                out_shape.append(jax.ShapeDtypeStruct((bsz, r, nc * s["w"]), F32))
                out_specs.append(pl.BlockSpec((None, r, s["w"]), lambda c, b, t: (b, 0, c)))
            else:
                out_shape.append(jax.ShapeDtypeStruct((r, nc * s["w"]), F32))
                out_specs.append(pl.BlockSpec((r, s["w"]), lambda c, b, t: (0, c)))

    def body(*refs):
        x_refs, p_refs = refs[:nx], refs[nx:nx + np_]
        d_refs = refs[nx + np_:nx + np_ + len(douts)]
        o_refs = refs[nx + np_ + len(douts):]
        xv = [[p.astype(F32) for p in _pieces(r, s)] for r, s in zip(x_refs, xs)]
        pv = [[p.astype(F32) for p in _pieces(r, s)] for r, s in zip(p_refs, ps)]
        if fwd:
            for r, pieces, s in zip(o_refs, f(xv, pv), out_modes):
                _store(r, pieces, s)
            return
        _, vjp = jax.vjp(f, xv, pv)
        cot = [[p.astype(F32) for p in _pieces(r, s)] for r, s in zip(d_refs, douts)]
        dxv, dpv = vjp(cot)
        for r, i in zip(o_refs, dx):
            _store(r, dxv[i], xs[i])
        b, t = pl.program_id(1), pl.program_id(2)
        for r, j in zip(o_refs[len(dx):], dp):
            first = (t == 0) if ps[j]["e"] else jnp.logical_and(b == 0, t == 0)

            @pl.when(first)
            def _(r=r, j=j):
                _store(r, dpv[j], ps[j])

            @pl.when(jnp.logical_not(first))
            def _(r=r, j=j):
                _store(r, dpv[j], ps[j], accumulate=True)

    res = pl.pallas_call(
        body, name=name, grid=(nc, bsz, nt), in_specs=in_specs, out_specs=out_specs, out_shape=out_shape,
        compiler_params=pltpu.CompilerParams(dimension_semantics=("arbitrary", "arbitrary", "arbitrary")),
    )(*operands)
    return res


def _keep_rows(a, shift, keep):
    n = a.shape[0]
    t = lax.broadcasted_iota(jnp.int32, a.shape, 0)
    return jnp.where(keep(t, n), pltpu.roll(a, shift % n, 0), 0.0)


def _shift_pair(step, keep_prev=None, keep_next=None):
    @jax.custom_vjp
    def prev(a):
        if keep_prev is None:
            return jnp.concatenate([jnp.zeros((step,) + a.shape[1:], a.dtype), a[:a.shape[0] - step]], axis=0)
        return _keep_rows(a, step, keep_prev)

    @jax.custom_vjp
    def nxt(a):
        if keep_next is None:
            return jnp.concatenate([a[step:], jnp.zeros((step,) + a.shape[1:], a.dtype)], axis=0)
        return _keep_rows(a, -step, keep_next)

    prev.defvjp(lambda a: (prev(a), None), lambda _, g: (nxt(g),))
    nxt.defvjp(lambda a: (nxt(a), None), lambda _, g: (prev(g),))
    return prev, nxt


prev_tok, next_tok = _shift_pair(1, lambda t, n: t % GRID_W != 0, lambda t, n: t % GRID_W != GRID_W - 1)
prev_row, next_row = _shift_pair(GRID_W)


@jax.custom_vjp
def bdot(a, w):
    return jnp.dot(a.astype(BF16), w.astype(BF16), preferred_element_type=F32)


def _bdot_bwd(res, g):
    a, w = res
    gb = g.astype(BF16)
    da = lax.dot_general(gb, w.astype(BF16), (((1,), (1,)), ((), ())), preferred_element_type=F32)
    dw = lax.dot_general(a.astype(BF16), gb, (((0,), (0,)), ((), ())), preferred_element_type=F32)
    return da, dw


bdot.defvjp(lambda a, w: (bdot(a, w), (a, w)), _bdot_bwd)


@jax.custom_vjp
def log_sigmoid(z):
    return jnp.minimum(z, 0.0) - jnp.log(1.0 + jnp.exp(-jnp.abs(z)))


def _lsig_bwd(z, g):
    e = jnp.exp(-jnp.abs(z))
    return (g * jnp.where(z >= 0, e, 1.0) / (1.0 + e),)


log_sigmoid.defvjp(lambda z: (log_sigmoid(z), z), _lsig_bwd)


def silu(x):
    return x * jax.nn.sigmoid(x)


def _rms(x):
    return x * lax.rsqrt(jnp.mean(x * x, axis=-1, keepdims=True) + EPS)


def _mod(x, gain, shift, scale):
    return _rms(x) * gain * (1.0 + scale) + shift


def f_mod(xs, ps):
    ((h,),), ((gain,), (shift,), (scale,)) = xs, ps
    return [[_mod(h, gain, shift, scale)], [h]]


def f_res_mod(xs, ps):
    ((h,), (y,)), ((gate,), (gain,), (shift,), (scale,)) = xs, ps
    h1 = h + gate * y
    return [[h1], [_mod(h1, gain, shift, scale)]]


def f_ffn_mid(xs, ps):
    ((ua, ug),), ((w0a, w0g), (w1a, w1g), (w2a, w2g), (ba, bg)) = xs, ps
    a = w0a * prev_row(ua) + w1a * ua + w2a * next_row(ua) + ba
    g = w0g * prev_row(ug) + w1g * ug + w2g * next_row(ug) + bg
    return [[a * silu(g)]]


def f_sc_mid(xs, ps):
    ((bg, cg, v),), ((w0,), (w1,), (w2,)) = xs, ps
    z = cg * v
    return [[bg * (w0 * prev_tok(z) + w1 * z + w2 * next_tok(z))]]


def f_decay(xs, ps):
    ((a,),), ((wd,), (bd,)) = xs, ps
    return [[log_sigmoid(bdot(a, wd) + bd) / TAU]]


def f_gla_post(xs, ps):
    (of, ob, g), ((gain,),) = xs, ps
    return [[_rms(a + b) * gain * silu(c) for a, b, c in zip(of, ob, g)]]


NCH = TT // CHUNK
CTX_CH = CTX // CHUNK
_NT = (((1,), (1,)), ((), ()))
_TN = (((0,), (0,)), ((), ()))
_NN = (((1,), (0,)), ((), ()))


def _chunk_of(d, j):
    return jnp.where(d == 0, j, jnp.where(j < CTX_CH, CTX_CH - 1 - j, NCH + CTX_CH - 1 - j))


def _dot(a, b, dn):
    return lax.dot_general(a, b, dn, preferred_element_type=F32)


def _cumsum_rows(g, suffix):
    n = g.shape[0]
    row = lax.broadcasted_iota(jnp.int32, g.shape, 0)
    s = 1
    while s < n:
        if suffix:
            g = g + jnp.where(row < n - s, pltpu.roll(g, n - s, 0), 0.0)
        else:
            g = g + jnp.where(row >= s, pltpu.roll(g, s, 0), 0.0)
        s *= 2
    return g


def _causal(backward):
    row = lax.broadcasted_iota(jnp.int32, (CHUNK, CHUNK), 0)
    col = lax.broadcasted_iota(jnp.int32, (CHUNK, CHUNK), 1)
    return col >= row if backward else col <= row


def _gla_in_specs(bsz, rev):
    def blk(d, j):
        return _chunk_of(d, (NCH - 1 - j) if rev else j)

    return [
        pl.BlockSpec((bsz, CHUNK, KD), lambda d, j: (0, blk(d, j), 0)),
        pl.BlockSpec((bsz, CHUNK, KD), lambda d, j: (0, blk(d, j), 1)),
        pl.BlockSpec((bsz, CHUNK, VD), lambda d, j: (0, blk(d, j), 1)),
        pl.BlockSpec((bsz, CHUNK, KD), lambda d, j: (0, blk(d, j), d)),
    ], blk


def gla_fwd(pcat, la):
    bsz = pcat.shape[0]
    in_specs, blk = _gla_in_specs(bsz, False)

    def body(q_ref, k_ref, v_ref, la_ref, o_ref, s_ref, st):
        d, j = pl.program_id(0), pl.program_id(1)

        @pl.when(j == 0)
        def _():
            st[...] = jnp.zeros_like(st)

        s_ref[...] = st[...]

        def scan(backward):
            causal = _causal(backward)
            for e in range(bsz):
                g_all = la_ref[e]
                b_all = _cumsum_rows(g_all, backward)
                bl_all = jnp.sum(g_all, axis=0, keepdims=True)
                qs_all = (q_ref[e].astype(F32) * (HK ** -0.5) * jnp.exp(b_all)).astype(BF16)
                ks_all = (k_ref[e] * jnp.exp(-b_all)).astype(BF16)
                kd_all = (k_ref[e] * jnp.exp(bl_all - b_all)).astype(BF16)
                el_all = jnp.exp(bl_all)
                for h in range(HEADS):
                    ks_, vs_ = slice(h * HK, (h + 1) * HK), slice(h * HV, (h + 1) * HV)
                    qs, ks, kd, v = qs_all[:, ks_], ks_all[:, ks_], kd_all[:, ks_], v_ref[e, :, vs_].astype(BF16)
                    s = st[e, h]
                    att = jnp.where(causal, _dot(qs, ks, _NT), 0.0).astype(BF16)
                    o_ref[e, :, vs_] = _dot(qs, s.astype(BF16), _NT) + _dot(att, v, _NN)
                    st[e, h] = el_all[:, ks_] * s + _dot(v, kd, _TN)

        @pl.when(d == 0)
        def _():
            scan(False)

        @pl.when(d == 1)
        def _():
            scan(True)

    return pl.pallas_call(
        body, name="gla_fwd", grid=(2, NCH), in_specs=in_specs,
        out_specs=[pl.BlockSpec((bsz, CHUNK, VD), lambda d, j: (0, blk(d, j), d)),
                   pl.BlockSpec((bsz, None, None, HEADS, HV, HK), lambda d, j: (0, d, j, 0, 0, 0))],
        out_shape=[jax.ShapeDtypeStruct((bsz, TT, 2 * VD), F32), jax.ShapeDtypeStruct((bsz, 2, NCH, HEADS, HV, HK), F32)],
        scratch_shapes=[pltpu.VMEM((bsz, HEADS, HV, HK), F32)],
        compiler_params=pltpu.CompilerParams(dimension_semantics=("arbitrary", "arbitrary")),
    )(pcat, pcat, pcat, la)


def gla_bwd(pcat, la, s_all, do):
    bsz = pcat.shape[0]
    in_specs, blk = _gla_in_specs(bsz, True)
    in_specs += [
        pl.BlockSpec((bsz, None, None, HEADS, HV, HK), lambda d, j: (0, d, NCH - 1 - j, 0, 0, 0)),
        pl.BlockSpec((bsz, CHUNK, VD), lambda d, j: (0, jnp.maximum(blk(d, j) - CTX_CH, 0), 0)),
    ]

    def body(q_ref, k_ref, v_ref, la_ref, s_ref, do_ref, dq_ref, dk_ref, dv_ref, dla_ref, dst):
        d, j = pl.program_id(0), pl.program_id(1)

        @pl.when(j == 0)
        def _():
            dst[...] = jnp.zeros_like(dst)

        latent = blk(d, j) >= CTX_CH
        scale = HK ** -0.5

        def scan(backward):
            causal = _causal(backward)
            for e in range(bsz):
                g_all = la_ref[e]
                b_all = _cumsum_rows(g_all, backward)
                bl_all = jnp.sum(g_all, axis=0, keepdims=True)
                ex_all, ei_all, ed_all, el_all = jnp.exp(b_all), jnp.exp(-b_all), jnp.exp(bl_all - b_all), jnp.exp(bl_all)
                qs_all, ks_all, kd_all = q_ref[e].astype(F32) * scale * ex_all, k_ref[e] * ei_all, k_ref[e] * ed_all
                qsb_all, ksb_all, kdb_all = qs_all.astype(BF16), ks_all.astype(BF16), kd_all.astype(BF16)
                db_parts, dbl_parts = [], []
                for h in range(HEADS):
                    ks_, vs_ = slice(h * HK, (h + 1) * HK), slice(h * HV, (h + 1) * HV)
                    qs, ks, kd, el = qs_all[:, ks_], ks_all[:, ks_], kd_all[:, ks_], el_all[:, ks_]
                    qsb, ksb, kdb, v = qsb_all[:, ks_], ksb_all[:, ks_], kdb_all[:, ks_], v_ref[e, :, vs_].astype(BF16)
                    s, ds1 = s_ref[e, h], dst[e, h]
                    sb, ds1b = s.astype(BF16), ds1.astype(BF16)
                    dob = jnp.where(latent, do_ref[e, :, vs_], 0.0).astype(BF16)
                    att = jnp.where(causal, _dot(qsb, ksb, _NT), 0.0).astype(BF16)
                    datt = jnp.where(causal, _dot(dob, v, _NT), 0.0).astype(BF16)
                    dqs = _dot(dob, sb, _NN) + _dot(datt, ksb, _NN)
                    dks = _dot(datt, qsb, _TN)
                    dv_ref[e, :, vs_] = (_dot(att, dob, _TN) + _dot(kdb, ds1b, _NT)).astype(BF16)
                    dkd = _dot(v, ds1b, _NN)
                    dst[e, h] = _dot(dob, qsb, _TN) + el * ds1
                    del_ = jnp.sum(s * ds1, axis=0, keepdims=True)
                    dq_ref[e, :, ks_] = (dqs * ex_all[:, ks_] * scale).astype(BF16)
                    dk_ref[e, :, ks_] = (dks * ei_all[:, ks_] + dkd * ed_all[:, ks_]).astype(BF16)
                    db_parts.append(dqs * qs - dks * ks - dkd * kd)
                    dbl_parts.append(jnp.sum(dkd * kd, axis=0, keepdims=True) + del_ * el)
                dla_ref[e] = _cumsum_rows(jnp.concatenate(db_parts, -1), not backward) + jnp.concatenate(dbl_parts, -1)

        @pl.when(d == 0)
        def _():
            scan(False)

        @pl.when(d == 1)
        def _():
            scan(True)

    return pl.pallas_call(
        body, name="gla_bwd", grid=(2, NCH), in_specs=in_specs,
        out_specs=[pl.BlockSpec((None, bsz, CHUNK, KD), lambda d, j: (d, 0, blk(d, j), 0)),
                   pl.BlockSpec((None, bsz, CHUNK, KD), lambda d, j: (d, 0, blk(d, j), 0)),
                   pl.BlockSpec((None, bsz, CHUNK, VD), lambda d, j: (d, 0, blk(d, j), 0)),
                   pl.BlockSpec((bsz, CHUNK, KD), lambda d, j: (0, blk(d, j), d))],
        out_shape=[jax.ShapeDtypeStruct((2, bsz, TT, KD), BF16), jax.ShapeDtypeStruct((2, bsz, TT, KD), BF16),
                   jax.ShapeDtypeStruct((2, bsz, TT, VD), BF16), jax.ShapeDtypeStruct((bsz, TT, 2 * KD), F32)],
        scratch_shapes=[pltpu.VMEM((bsz, HEADS, HV, HK), F32)],
        compiler_params=pltpu.CompilerParams(dimension_semantics=("arbitrary", "arbitrary")),
    )(pcat, pcat, pcat, la, s_all, do)


def gla_combine(dq2, dk2, dv2, dgate, dpa):
    bsz = dgate.shape[0]
    tm = CTX

    def body(dq_ref, dk_ref, dv_ref, dg_ref, dpa_ref, o_ref):
        t = pl.program_id(1)
        o_ref[:, 0:KD] = (dq_ref[0].astype(F32) + dq_ref[1].astype(F32)).astype(BF16)
        o_ref[:, KD:2 * KD] = (dk_ref[0].astype(F32) + dk_ref[1].astype(F32)).astype(BF16)
        o_ref[:, 2 * KD:2 * KD + VD] = (dv_ref[0].astype(F32) + dv_ref[1].astype(F32)).astype(BF16)
        o_ref[:, 2 * KD + VD:2 * KD + 2 * VD] = jnp.where(t > 0, dg_ref[...], 0).astype(BF16)
        o_ref[:, 2 * KD + 2 * VD:] = dpa_ref[...].astype(BF16)

    return pl.pallas_call(
        body, name="gla_combine", grid=(bsz, TT // tm),
        in_specs=[pl.BlockSpec((2, None, tm, KD), lambda b, t: (0, b, t, 0)),
                  pl.BlockSpec((2, None, tm, KD), lambda b, t: (0, b, t, 0)),
                  pl.BlockSpec((2, None, tm, VD), lambda b, t: (0, b, t, 0)),
                  pl.BlockSpec((None, tm, VD), lambda b, t: (b, jnp.maximum(t - 1, 0), 0)),
                  pl.BlockSpec((None, tm, 128), lambda b, t: (b, t, 0))],
        out_specs=pl.BlockSpec((None, tm, GLA_IN_PAD), lambda b, t: (b, t, 0)),
        out_shape=jax.ShapeDtypeStruct((bsz, TT, GLA_IN_PAD), BF16),
        compiler_params=pltpu.CompilerParams(dimension_semantics=("arbitrary", "arbitrary")),
    )(dq2, dk2, dv2, dgate, dpa)


def final_loss(h1, fo, gate, gain, tgt):
    bsz, t_len, _ = h1.shape
    tm = 512

    def body(h_ref, f_ref, gate_ref, gain_ref, tgt_ref, loss_ref, dh_ref, df_ref, dgate_ref, dgain_ref):
        b, t = pl.program_id(0), pl.program_id(1)
        target = tgt_ref[...]

        def core(h, fo_, gate_, gain_):
            e = _rms(h + gate_ * fo_) * gain_ - target
            return jnp.sum(0.5 * jnp.sum(e * e, axis=-1, keepdims=True) / D, axis=0, keepdims=True)

        loss, vjp = jax.vjp(core, h_ref[...], f_ref[...], gate_ref[...], gain_ref[...])
        dh, df, dgate, dgain = vjp(jnp.ones((1, 1), F32))
        dh_ref[...] = dh
        df_ref[...] = df.astype(BF16)
        first = jnp.logical_and(b == 0, t == 0)

        @pl.when(first)
        def _():
            loss_ref[...] = jnp.broadcast_to(loss, loss_ref.shape)
            dgain_ref[...] = dgain

        @pl.when(jnp.logical_not(first))
        def _():
            loss_ref[...] += jnp.broadcast_to(loss, loss_ref.shape)
            dgain_ref[...] += dgain

        @pl.when(t == 0)
        def _():
            dgate_ref[...] = dgate

        @pl.when(t > 0)
        def _():
            dgate_ref[...] += dgate

    tile = pl.BlockSpec((None, tm, D), lambda b, t: (b, t, 0))
    per_ex = pl.BlockSpec((None, 1, D), lambda b, t: (b, 0, 0))
    shared = pl.BlockSpec((1, D), lambda b, t: (0, 0))
    return pl.pallas_call(
        body, name="final_loss", grid=(bsz, t_len // tm),
        in_specs=[tile, tile, per_ex, shared, tile],
        out_specs=[pl.BlockSpec((8, 128), lambda b, t: (0, 0)), tile, tile, per_ex, shared],
        out_shape=[jax.ShapeDtypeStruct((8, 128), F32), jax.ShapeDtypeStruct(h1.shape, F32),
                   jax.ShapeDtypeStruct(h1.shape, BF16), jax.ShapeDtypeStruct((bsz, 1, D), F32),
                   jax.ShapeDtypeStruct((1, D), F32)],
        compiler_params=pltpu.CompilerParams(dimension_semantics=("arbitrary", "arbitrary")),
    )(h1, fo, gate, gain, tgt)


ADA_ROWS = 24
ADA_CTX_ROW = 16
ADA_COLS = 6 * D // N_DEV


def ada_fwd(cond, w, b):
    def body(c_ref, w_ref, b_ref, o_ref):
        s = silu(c_ref[...]).astype(BF16)
        o_ref[...] = jnp.dot(s, w_ref[...].astype(BF16), preferred_element_type=F32) + b_ref[...]

    return pl.pallas_call(
        body, name="ada_fwd", grid=(2,),
        in_specs=[pl.BlockSpec((ADA_ROWS, D), lambda i: (0, 0)), pl.BlockSpec((None, D, ADA_COLS), lambda i: (i, 0, 0)),
                  pl.BlockSpec((None, 1, ADA_COLS), lambda i: (i, 0, 0))],
        out_specs=pl.BlockSpec((None, ADA_ROWS, ADA_COLS), lambda i: (i, 0, 0)),
        out_shape=jax.ShapeDtypeStruct((2, ADA_ROWS, ADA_COLS), F32),
    )(cond, w, b)


def ada_bwd(cond, dm_mine, dm_full, w):
    def body(c_ref, dm_ref, dmf_ref, w_ref, gw_ref, gb_ref, cp_ref):
        i = pl.program_id(0)
        s = silu(c_ref[...]).astype(BF16)
        dm = dm_ref[...].astype(BF16)
        gw_ref[...] = _dot(s, dm, _TN)
        gb_ref[...] = jnp.sum(dmf_ref[...], axis=0, keepdims=True)

        @pl.when(i == 0)
        def _():
            cp_ref[...] = _dot(dm_ref[ADA_CTX_ROW:, :].astype(BF16), w_ref[...].astype(BF16), _NT)

    return pl.pallas_call(
        body, name="ada_bwd", grid=(2,),
        in_specs=[pl.BlockSpec((ADA_ROWS, D), lambda i: (0, 0)), pl.BlockSpec((None, ADA_ROWS, ADA_COLS), lambda i: (i, 0, 0)),
                  pl.BlockSpec((None, ADA_ROWS, 6 * D), lambda i: (i, 0, 0)), pl.BlockSpec((None, D, ADA_COLS), lambda i: (i, 0, 0))],
        out_specs=[pl.BlockSpec((None, D, ADA_COLS), lambda i: (i, 0, 0)), pl.BlockSpec((None, 1, 6 * D), lambda i: (i, 0, 0)),
                   pl.BlockSpec((ADA_ROWS - ADA_CTX_ROW, D), lambda i: (0, 0))],
        out_shape=[jax.ShapeDtypeStruct((2, D, ADA_COLS), F32), jax.ShapeDtypeStruct((2, 1, 6 * D), F32),
                   jax.ShapeDtypeStruct((ADA_ROWS - ADA_CTX_ROW, D), F32)],
        compiler_params=pltpu.CompilerParams(dimension_semantics=("arbitrary",)),
    )(cond, dm_mine, dm_full, w)


def cctx_grad(parts, c_ctx):
    def body(p_ref, c_ref, o_ref):
        tot = p_ref[0:1, :]
        for i in range(1, N_DEV):
            tot = tot + p_ref[i:i + 1, :]
        c = c_ref[...]
        sg = jax.nn.sigmoid(c)
        o_ref[...] = tot * sg * (1.0 + c * (1.0 - sg))

    return pl.pallas_call(body, name="cctx_grad", out_shape=jax.ShapeDtypeStruct((1, D), F32))(parts, c_ctx)


def _row_tile(r):
    for t in (512, 256, 128, 80, 64, 40, 32, 16, 8):
        if r % t == 0:
            return t
    return r


def _slot_sum(ref):
    tot = ref[0].astype(F32)
    for i in range(1, ref.shape[0]):
        tot = tot + ref[i].astype(F32)
    return tot


def sum_slots(name, x):
    s, r, c = x.shape
    tr = _row_tile(r)

    def body(x_ref, o_ref):
        o_ref[...] = _slot_sum(x_ref)

    return pl.pallas_call(
        body, name=name, grid=(r // tr,), in_specs=[pl.BlockSpec((s, tr, c), lambda i: (0, i, 0))],
        out_specs=pl.BlockSpec((tr, c), lambda i: (i, 0)), out_shape=jax.ShapeDtypeStruct((r, c), F32),
    )(x)


def _adamw_update(gv, w_ref, m_ref, v_ref, go_ref, d_ref, mo_ref, vo_ref):
    mn = B1 * m_ref[...] + (1.0 - B1) * gv
    vn = B2 * v_ref[...] + (1.0 - B2) * jnp.square(gv)
    m_hat = mn / (1.0 - B1 ** STEP)
    v_hat = vn / (1.0 - B2 ** STEP)
    go_ref[...] = gv
    d_ref[...] = -LR * (m_hat / (jnp.sqrt(v_hat) + AEPS) + WD * w_ref[...])
    mo_ref[...] = mn
    vo_ref[...] = vn


def adamw_slots(name, w, land, sent, me1, m, v, layer, into=None):
    r, c = w.shape[-2:]
    tr = _row_tile(r)
    into = [] if into is None else list(into)

    def body(me_ref, w_ref, land_ref, own_ref, m_ref, v_ref, *rest):
        own = own_ref[...].astype(F32)
        gv = jnp.where(me_ref[0] == 0, own, land_ref[0].astype(F32))
        for s in range(1, N_DEV):
            gv = gv + jnp.where(me_ref[0] == s, own, land_ref[s].astype(F32))
        _adamw_update(gv, w_ref, m_ref, v_ref, *rest[len(into):])

    slab = pl.BlockSpec((None, tr, c), lambda i, me: (layer, i, 0))
    return pl.pallas_call(
        body, name=name, out_shape=[jax.ShapeDtypeStruct(w.shape, F32)] * 4,
        grid_spec=pltpu.PrefetchScalarGridSpec(
            num_scalar_prefetch=1, grid=(r // tr,),
            in_specs=[slab, pl.BlockSpec((N_DEV, tr, c), lambda i, me: (0, i, 0)),
                      pl.BlockSpec((None, tr, c), lambda i, me: (me[0], i, 0)), slab, slab]
            + [pl.BlockSpec(memory_space=pl.ANY)] * len(into),
            out_specs=[slab] * 4),
        input_output_aliases={6 + k: k for k in range(len(into))},
    )(me1, w, land, sent, m, v, *into)


def adamw(name, w, g, m, v, layer=None):
    r, c = w.shape[-2:]
    tr = _row_tile(r)
    stacked = g.ndim == 3

    def body(w_ref, g_ref, m_ref, v_ref, *outs):
        _adamw_update(_slot_sum(g_ref) if stacked else g_ref[...], w_ref, m_ref, v_ref, *outs)

    tile = pl.BlockSpec((tr, c), lambda i: (i, 0))
    slab = tile if layer is None else pl.BlockSpec((None, tr, c), lambda i: (layer, i, 0))
    g_spec = pl.BlockSpec((g.shape[0], tr, c), lambda i: (0, i, 0)) if stacked else tile
    return pl.pallas_call(
        body, name=name, grid=(r // tr,), in_specs=[slab, g_spec, slab, slab], out_specs=[tile] * 4,
        out_shape=[jax.ShapeDtypeStruct((r, c), F32)] * 4,
    )(w, g, m, v)


def _place():
    return lax.axis_index("x"), lax.axis_index("y"), lax.axis_index("c")


def all_gather(name, x):
    r, c = x.shape
    space = pltpu.VMEM

    def body(x_ref, out_ref, send_sems, recv_sems, local_sem):
        px, py, pc = _place()
        me, sibling = (px, py, pc), (px, py, 1 - pc)
        chips = [(1 - px, py), (px, 1 - py), (1 - px, 1 - py)]

        def rows(qx, qy, qc):
            return out_ref.at[pl.ds((4 * qx + 2 * qy + qc) * r, r), :]

        def copy(k, block, to, src=None):
            return pltpu.make_async_remote_copy(
                src_ref=rows(*block) if src is None else src, dst_ref=rows(*block),
                send_sem=send_sems.at[k], recv_sem=recv_sems.at[k], device_id=to, device_id_type=MESH)

        mine = pltpu.make_async_copy(x_ref, rows(*me), local_sem)
        mine.start()
        first = [copy(0, me, sibling, src=x_ref)]
        first += [copy(1 + j, me, (*chip, pc), src=x_ref) for j, chip in enumerate(chips)]
        for cp in first:
            cp.start()
        passed = [copy(4 + j, (*chip, pc), sibling) for j, chip in enumerate(chips)]
        for j, chip in enumerate(chips):
            copy(1 + j, (*chip, pc), me).wait_recv()
            passed[j].start()
        copy(0, sibling, me).wait_recv()
        for j, chip in enumerate(chips):
            copy(4 + j, (*chip, 1 - pc), me).wait_recv()
        for cp in first + passed:
            cp.wait_send()
        mine.wait()

    return pl.pallas_call(
        body, name=name, out_shape=jax.ShapeDtypeStruct((N_DEV * r, c), x.dtype),
        in_specs=[pl.BlockSpec(memory_space=space)], out_specs=pl.BlockSpec(memory_space=space),
        scratch_shapes=[pltpu.SemaphoreType.DMA((7,)), pltpu.SemaphoreType.DMA((7,)), pltpu.SemaphoreType.DMA],
    )(x)


_HBM = pl.BlockSpec(memory_space=pltpu.HBM)
_SEM = pl.BlockSpec(memory_space=pltpu.SEMAPHORE)
_EFFECT = pltpu.SideEffectType.DATAFLOW_SIDE_EFFECTING


def _peers():
    px, py, pc = _place()
    return [(1 - px if k & 4 else px, 1 - py if k & 2 else py, 1 - pc if k & 1 else pc) for k in range(1, N_DEV)]


def _slot(dev):
    return 4 * dev[0] + 2 * dev[1] + dev[2]


def _split_copies(src_refs, land_refs, send_sems, recv_sems, gather):
    me = _slot(_place())
    return [pltpu.make_async_remote_copy(
        src_ref=src if gather else src.at[_slot(peer)], dst_ref=land.at[me],
        send_sem=send_sems.at[a * (N_DEV - 1) + k], recv_sem=recv_sems.at[a * (N_DEV - 1) + k],
        device_id=peer, device_id_type=MESH)
        for a, (src, land) in enumerate(zip(src_refs, land_refs)) for k, peer in enumerate(_peers())]


def exchange_start(name, srcs, gather, after):
    n = len(srcs)
    lands = [pltpu.HBM((N_DEV,) + s.shape if gather else s.shape, s.dtype) for s in srcs]

    def body(*refs):
        send_sems, recv_sems = refs[2 * n + 1:2 * n + 3]
        for cp in _split_copies(refs[:n], refs[n:2 * n], send_sems, recv_sems, gather):
            cp.start()
        refs[-1][...] = jnp.zeros_like(refs[-1])

    sems = pltpu.SemaphoreType.DMA((n * (N_DEV - 1),))
    res = pl.pallas_call(
        body, name=name,
        out_shape=(sems, sems, *[pltpu.HBM(s.shape, s.dtype) for s in srcs], *lands, jax.ShapeDtypeStruct((8, 128), F32)),
        in_specs=(_HBM,) * (2 * n) + (pl.BlockSpec(memory_space=pl.ANY),),
        out_specs=(_SEM, _SEM) + (_HBM,) * (2 * n) + (pl.BlockSpec(memory_space=pltpu.VMEM),),
        input_output_aliases={i: 2 + i for i in range(2 * n)},
        compiler_params=pltpu.CompilerParams(has_side_effects=_EFFECT),
    )(*[pltpu.with_memory_space_constraint(s, pltpu.HBM) for s in srcs],
      *[pltpu.with_memory_space_constraint(lax.empty(ld.shape, ld.dtype), pltpu.HBM) for ld in lands], after)
    return res[0], res[1], list(res[2:2 + n]), list(res[2 + n:2 + 2 * n]), res[-1]


def exchange_wait(name, started, after, gather):
    send_sems, recv_sems, srcs, lands, _ = started
    n = len(srcs)
    after = list(after) if isinstance(after, (list, tuple)) else [after]

    def body(*refs):
        send_sems, recv_sems = refs[2 * n:2 * n + 2]
        for cp in _split_copies(refs[:n], refs[n:2 * n], send_sems, recv_sems, gather):
            cp.wait_send()
            cp.wait_recv()

    res = pl.pallas_call(
        body, name=name, out_shape=tuple(pltpu.HBM(a.shape, a.dtype) for a in srcs + lands),
        in_specs=(_HBM,) * (2 * n) + (_SEM, _SEM) + (pl.BlockSpec(memory_space=pl.ANY),) * len(after),
        out_specs=(_HBM,) * (2 * n), input_output_aliases={i: i for i in range(2 * n)},
        compiler_params=pltpu.CompilerParams(has_side_effects=_EFFECT),
    )(*srcs, *lands, send_sems, recv_sems, *after)
    return list(res[:n]), list(res[n:])


NCF = FFN_H // FFN_TC


def _size(shape):
    n = 1
    for s in shape:
        n *= s
    return n


def _padded_rows(n_elems, row_mult):
    return -(-n_elems // (D * row_mult)) * row_mult


def _pack_rows(arrs, dtype, row_mult):
    rows, offs, r0 = [], [], 0
    for a in arrs:
        flat = a.reshape(-1).astype(dtype)
        n = _padded_rows(flat.shape[0], row_mult)
        rows.append(jnp.pad(flat, (0, n * D - flat.shape[0])).reshape(n, D))
        offs.append(r0)
        r0 += n
    return jnp.concatenate(rows, 0), offs


def _unpack_rows(buf, offs, shapes):
    lead, out = buf.shape[:-2], []
    for o, shp in zip(offs, shapes):
        n = _size(shp)
        nr = -(-n // D)
        out.append(buf[..., o:o + nr, :].reshape(lead + (nr * D,))[..., :n].reshape(lead + tuple(shp)))
    return out


def _rows3(w):
    return [w[i:i + 1] for i in range(3)]


def f_mod1(xs, ps):
    return f_mod(xs, ps)[:1]


def kernel(x, c, ctx, c_ctx, ada_w, ada_b, norm_mix, norm_ffn, gla_w_in, gla_w_a2, gla_b_a, gla_head_norm, gla_w_out, sc_w_in, sc_conv_w, sc_w_out, ffn_w_up, ffn_conv_w, ffn_conv_b, ffn_w_down, final_norm, loss_target, m_c_ctx, m_ada_w, m_ada_b, m_norm_mix, m_norm_ffn, m_gla_w_in, m_gla_w_a2, m_gla_b_a, m_gla_head_norm, m_gla_w_out, m_sc_w_in, m_sc_conv_w, m_sc_w_out, m_ffn_w_up, m_ffn_conv_w, m_ffn_conv_b, m_ffn_w_down, m_final_norm, v_c_ctx, v_ada_w, v_ada_b, v_norm_mix, v_norm_ffn, v_gla_w_in, v_gla_w_a2, v_gla_b_a, v_gla_head_norm, v_gla_w_out, v_sc_w_in, v_sc_conv_w, v_sc_w_out, v_ffn_w_up, v_ffn_conv_w, v_ffn_conv_b, v_ffn_w_down, v_final_norm):
    names = ["c_ctx", "ada_w", "ada_b", "norm_mix", "norm_ffn", "gla_w_in", "gla_w_a2", "gla_b_a", "gla_head_norm",
             "gla_w_out", "sc_w_in", "sc_conv_w", "sc_w_out", "ffn_w_up", "ffn_conv_w", "ffn_conv_b", "ffn_w_down",
             "final_norm"]
    w_ = dict(zip(names, [c_ctx, ada_w, ada_b, norm_mix, norm_ffn, gla_w_in, gla_w_a2, gla_b_a, gla_head_norm, gla_w_out,
                          sc_w_in, sc_conv_w, sc_w_out, ffn_w_up, ffn_conv_w, ffn_conv_b, ffn_w_down, final_norm]))
    m_ = dict(zip(names, [m_c_ctx, m_ada_w, m_ada_b, m_norm_mix, m_norm_ffn, m_gla_w_in, m_gla_w_a2, m_gla_b_a,
                          m_gla_head_norm, m_gla_w_out, m_sc_w_in, m_sc_conv_w, m_sc_w_out, m_ffn_w_up, m_ffn_conv_w,
                          m_ffn_conv_b, m_ffn_w_down, m_final_norm]))
    v_ = dict(zip(names, [v_c_ctx, v_ada_w, v_ada_b, v_norm_mix, v_norm_ffn, v_gla_w_in, v_gla_w_a2, v_gla_b_a,
                          v_gla_head_norm, v_gla_w_out, v_sc_w_in, v_sc_conv_w, v_sc_w_out, v_ffn_w_up, v_ffn_conv_w,
                          v_ffn_conv_b, v_ffn_w_down, v_final_norm]))
    me = 4 * lax.axis_index("x") + 2 * lax.axis_index("y") + lax.axis_index("c")
    bsz = x.shape[0]
    tm = 256
    nt = SEQ // tm
    ctx_tiles = CTX // tm
    pe = functools.partial(P, per_example=True)

    groups = {"ffn1": [("ffn_w_up", 1), ("ffn_w_down", 1)], "sc": [("sc_w_in", 0), ("sc_w_out", 0)],
              "ffn0": [("ffn_w_up", 0), ("ffn_w_down", 0)], "gla": [("gla_w_in", 0), ("gla_w_out", 0)]}
    ag_groups = {"gin": [("gla_w_in", 0)], "ffn0": [("gla_w_out", 0), ("ffn_w_up", 0), ("ffn_w_down", 0)],
                 "sc": groups["sc"], "ffn1": groups["ffn1"]}
    ag_started = {}

    def start_gather(g, after):
        ag_started[g] = exchange_start(f"ag_{g}_start", [w_[n][i].astype(BF16) for n, i in ag_groups[g]], True, after)
        return ag_started[g][4]

    small_sharded = [c, gla_w_a2, gla_b_a, sc_conv_w, ffn_conv_w]
    pack0, offs0 = _pack_rows(small_sharded, F32, 8)
    g0 = all_gather("ag_small", pack0).reshape(N_DEV, pack0.shape[0], D)
    c_all, wa2_s, ba_s, scw_s, fcw_s = _unpack_rows(g0, offs0, [a.shape for a in small_sharded])
    w_a2 = wa2_s[:, 0].transpose(1, 2, 0, 3).reshape(2, RANK, KD)
    b_a = ba_s[:, 0].transpose(1, 0, 2).reshape(2, KD)
    sc_cw = scw_s[:, 0].transpose(1, 0, 2).reshape(3, D)
    ffn_cw = fcw_s.transpose(1, 2, 0, 3).reshape(2, 3, 2 * FFN_H)

    cond = jnp.concatenate([c_all.reshape(N_DEV * bsz, D), c_ctx[None], jnp.zeros((ADA_ROWS - N_DEV * bsz - 1, D), F32)], 0)
    b_mine = lax.dynamic_slice(ada_b, (0, me * ADA_COLS), (2, ADA_COLS)).reshape(2, 1, ADA_COLS)
    mod_part = ada_fwd(cond, ada_w, b_mine)
    mod = all_gather("ag_mod", mod_part.reshape(2 * ADA_ROWS, ADA_COLS))
    mod = mod.reshape(N_DEV, 2, ADA_ROWS, ADA_COLS).transpose(1, 2, 0, 3).reshape(2, ADA_ROWS, 6 * D)
    mods = lax.dynamic_slice(mod, (0, bsz * me, 0), (2, bsz, 6 * D))
    md = [[mods[i][:, k * D:(k + 1) * D].reshape(bsz, 1, D) for k in range(6)] for i in range(2)]
    mc = [mod[0, ADA_CTX_ROW, k * D:(k + 1) * D][None] for k in range(2)]

    tok = mod
    for g in ag_groups:
        tok = start_gather(g, tok)
    norm_mix = norm_mix + tok[0, 0]

    def gathered(g, after):
        mine, lands = exchange_wait(f"ag_{g}_wait", ag_started[g], after, True)
        return [lax.dynamic_update_index_in_dim(ld, mn, me, 0) for ld, mn in zip(lands, mine)]

    s_up, w_down = [None, None], [None, None]
    wd = jnp.zeros((128, 2 * KD), F32).at[:RANK, :KD].set(w_a2[0]).at[RANK:2 * RANK, KD:].set(w_a2[1])
    bd = b_a.reshape(1, 2 * KD)
    scw = _rows3(sc_cw)
    head_gain = gla_head_norm.reshape(1, HV)
    gains_mix = [norm_mix[i][None] for i in range(2)]
    gains_ffn = [norm_ffn[i][None] for i in range(2)]

    def tokens(a2d, t_len):
        return a2d.reshape(bsz, t_len, -1)

    def ffn_params(i, tc=FFN_TC):
        rows = [ffn_cw[i][t] for t in range(3)] + [ffn_conv_b[i]]
        return [P(a.reshape(2, FFN_H), w=tc, rows=True) for a in rows]

    def ffn_fwd(i, hn2):
        u = mm(f"ffn_up{i}", V(hn2, "tok"), V(s_up[i], "cols"), out="planes", out_dtype=BF16, planes_t=SEQ)
        tc = 2 * FFN_TC
        act = rowwise(f"ffn_mid{i}", f_ffn_mid, [X(u, w=tc, planes=True)], ffn_params(i, tc), tm=SEQ, nt=1, nc=FFN_H // tc,
                      outs=[(tc, BF16, 1)])[0]
        return u, act

    def arrays(ps):
        return [p["a"] for p in ps]

    ps_in0 = [P(gains_mix[0]), pe(md[0][0]), pe(md[0][1])]
    ps_ctx = [P(gains_mix[0]), P(mc[0]), P(mc[1])]
    hn0 = rowwise("mod_in0", f_mod, [X(x)], ps_in0, tm=2 * tm, nt=nt // 2, outs=[(D, BF16, 1)])[0]
    hnc = rowwise("mod_ctx", f_mod, [X(ctx)], ps_ctx, tm=tm, nt=ctx_tiles, outs=[(D, BF16, 1)])[0]
    hcat = jnp.concatenate([hnc, hn0], axis=1)
    (s_gin,) = gathered("gin", hcat)
    w_gin = V(s_gin, "cols", width=GLA_IN_PAD)
    pcat = tokens(mm("gla_in", V(hcat, "tok"), w_gin, out_dtype=BF16), TT)
    pa_x = X(pcat, w=128, co=(GLA_IN_PAD - 128) // 128)
    la = rowwise("gla_decay", f_decay, [pa_x], [P(wd), P(bd)], tm=3 * tm, nt=TT // (3 * tm), outs=[(2 * KD, F32, 1)])[0]
    o2, s_all = gla_fwd(pcat, la)
    post_xs = [X(o2, w=VD, co=0, ro=ctx_tiles, split=HEADS), X(o2, w=VD, co=1, ro=ctx_tiles, split=HEADS),
               X(pcat, w=VD, co=2, ro=ctx_tiles, split=HEADS)]
    yin0 = rowwise("gla_post", f_gla_post, post_xs, [P(head_gain)], tm=tm, nt=nt, outs=[(VD, BF16, HEADS)])[0]
    s_gout, s_up[0], s_down0 = gathered("ffn0", yin0)
    w_gout, w_down[0] = s_gout.reshape(VD, D), s_down0.reshape(FFN_H, D)
    ps_mid0 = [pe(md[0][2]), P(gains_ffn[0]), pe(md[0][3]), pe(md[0][4])]
    y0, h1_0, hn2_0 = mm_res_mod("gla_out", yin0, w_gout, x, *arrays(ps_mid0))
    u0, act0 = ffn_fwd(0, hn2_0)
    ps_in1 = [pe(md[0][5]), P(gains_mix[1]), pe(md[1][0]), pe(md[1][1])]
    fo0, h2_0, hn1 = mm_res_mod("ffn_down0", act0, w_down[0], h1_0, *arrays(ps_in1))

    s_sin, s_sout = gathered("sc", hn1)
    w_sout = s_sout.reshape(D, D)
    p1 = tokens(mm("sc_in", V(hn1, "tok"), V(s_sin, "cols")), SEQ)
    sc_ps = [P(a) for a in scw]
    yin1 = rowwise("sc_mid", f_sc_mid, [X(p1, split=3)], sc_ps, tm=2 * tm, nt=nt // 2, outs=[(D, BF16, 1)])[0]
    ps_mid1 = [pe(md[1][2]), P(gains_ffn[1]), pe(md[1][3]), pe(md[1][4])]
    y1, h1_1, hn2_1 = mm_res_mod("sc_out", yin1, w_sout, h2_0, *arrays(ps_mid1))
    s_up[1], s_down1 = gathered("ffn1", hn2_1)
    w_down[1] = s_down1.reshape(FFN_H, D)
    u1, act1 = ffn_fwd(1, hn2_1)
    fo1 = tokens(mm("ffn_down1", V(act1, "tok"), V(w_down[1])), SEQ)
    loss8, dh1_1, dfo1, dm5_1, g_final = final_loss(h1_1, fo1, md[1][5], final_norm[None], loss_target)

    def ffn_bwd(i, u, act, hn2, dfo):
        dact = tokens(mm(f"ffn_down_dx{i}", V(dfo, "tok"), V(w_down[i]), form="nt", out_dtype=BF16), SEQ)
        g_down = mm(f"ffn_down_dw{i}", V(act, "tok"), V(dfo, "tok"), form="tn", out_dtype=BF16)
        r = rowwise(f"ffn_mid_bwd{i}", f_ffn_mid, [X(u, w=FFN_TC, planes=True)], ffn_params(i), tm=SEQ, nt=1, nc=NCF,
                    douts=[X(dact, w=FFN_TC)], dx={0: BF16}, dp=[0, 1, 2, 3])
        du, g_cw, g_cb = r[0], jnp.stack([a.reshape(2 * FFN_H) for a in r[1:4]]), r[4].reshape(1, 2 * FFN_H)
        dhn2 = tokens(mm(f"ffn_up_dx{i}", V(du, "planes"), V(s_up[i], "cols"), form="nt", out_dtype=BF16), SEQ)
        g_up = mm(f"ffn_up_dw{i}", V(hn2, "tok"), V(du, "planes"), form="tn", out="cols", out_dtype=BF16)
        return dhn2, g_up, row_slots(g_down), g_cw, g_cb

    def res_mod_bwd(name, h, y, ps, dh1, dhn):
        return rowwise(name, f_res_mod, [X(h), X(y)], ps, tm=2 * tm, nt=nt // 2, douts=[X(dh1), X(dhn)],
                       dx={0: F32, 1: BF16}, dp=[0, 1, 2, 3])

    def row_slots(g):
        return g.reshape(N_DEV, -1, g.shape[-1])

    a2a_started = {}

    def send_grads(g, slots, after=None):
        a2a_started[g] = exchange_start(f"a2a_{g}_start", list(slots), False, loss8 if after is None else after)
        return a2a_started[g][4][0, 0]

    def after_start(ps, tok):
        return [dict(ps[0], a=ps[0]["a"] + tok)] + ps[1:]

    dhn2_1, g_up1, g_down1, g_fcw1, g_fcb1 = ffn_bwd(1, u1, act1, hn2_1, dfo1)
    tok = send_grads("ffn1", [g_up1, g_down1])
    dh2_0, dy1, dm2_1, g_nffn1, dm3_1, dm4_1 = res_mod_bwd("res_mod_mid1_bwd", h2_0, y1, after_start(ps_mid1, tok), dh1_1, dhn2_1)
    dyin1 = tokens(mm("sc_out_dx", V(dy1, "tok"), V(w_sout), form="nt", out_dtype=BF16), SEQ)
    g_sout = row_slots(mm("sc_out_dw", V(yin1, "tok"), V(dy1, "tok"), form="tn", out_dtype=BF16))
    r = rowwise("sc_mid_bwd", f_sc_mid, [X(p1, split=3)], sc_ps, tm=2 * tm, nt=nt // 2, douts=[X(dyin1)], dx={0: BF16},
                dp=[0, 1, 2])
    dp1, g_scw = r[0], jnp.concatenate(r[1:4], 0)
    dhn1 = tokens(mm("sc_in_dx", V(dp1, "tok"), V(s_sin, "cols"), form="nt", out_dtype=BF16), SEQ)
    g_sin = mm("sc_in_dw", V(hn1, "tok"), V(dp1, "tok"), form="tn", out="cols", out_dtype=BF16)
    tok = send_grads("sc", [g_sin, g_sout])
    dh1_0, dfo0, dm5_0, g_nmix1, dm0_1, dm1_1 = res_mod_bwd("res_mod_in1_bwd", h1_0, fo0, after_start(ps_in1, tok), dh2_0, dhn1)

    dhn2_0, g_up0, g_down0, g_fcw0, g_fcb0 = ffn_bwd(0, u0, act0, hn2_0, dfo0)
    tok = send_grads("ffn0", [g_up0, g_down0])
    dx_res, dy0, dm2_0, g_nffn0, dm3_0, dm4_0 = res_mod_bwd("res_mod_mid0_bwd", x, y0, after_start(ps_mid0, tok), dh1_0, dhn2_0)
    dyin0 = tokens(mm("gla_out_dx", V(dy0, "tok"), V(w_gout), form="nt", out_dtype=BF16), SEQ)
    do, dgate, g_head = rowwise("gla_post_bwd", f_gla_post, post_xs, [P(head_gain)], tm=tm, nt=nt,
                                douts=[X(dyin0, split=HEADS)], dx={0: BF16, 2: BF16}, dp=[0])
    dq2, dk2, dv2, dla = gla_bwd(pcat, la, s_all, do)
    dpa, g_wd, g_bd = rowwise("gla_decay_bwd", f_decay, [pa_x], [P(wd), P(bd)], tm=3 * tm, nt=TT // (3 * tm), douts=[X(dla)],
                              dx={0: BF16}, dp=[0, 1])
    dpcat = gla_combine(dq2, dk2, dv2, dgate, dpa)
    dhcat = tokens(mm("gla_in_dx", V(dpcat, "tok"), w_gin, form="nt", out_dtype=BF16), TT)
    grad_x, g_nmix0, dm0_0, dm1_0 = rowwise("mod_in0_bwd", f_mod, [X(x)], ps_in0, tm=tm, nt=nt,
                                            douts=[X(dhcat, ro=ctx_tiles), X(dx_res)], dx={0: F32}, dp=[0, 1, 2])
    g_nmix0c, dmc0, dmc1 = rowwise("mod_ctx_bwd", f_mod1, [X(ctx)], ps_ctx, tm=tm, nt=ctx_tiles, douts=[X(dhcat)],
                                   dx={}, dp=[0, 1, 2])

    zero_row = jnp.zeros((1, 4 * D), F32)
    dmod = [jnp.concatenate([jnp.concatenate([a.reshape(bsz, D) for a in dms], 1), ctx_row], 0)
            for dms, ctx_row in (([dm0_0, dm1_0, dm2_0, dm3_0, dm4_0, dm5_0], jnp.concatenate([dmc0, dmc1, zero_row], 1)),
                                 ([dm0_1, dm1_1, dm2_1, dm3_1, dm4_1, dm5_1], jnp.zeros((1, 6 * D), F32)))]
    g_wa2 = jnp.stack([g_wd[:RANK, :KD], g_wd[RANK:2 * RANK, KD:]])
    small_grads = [jnp.stack(dmod), jnp.concatenate([g_nmix0 + g_nmix0c, g_nmix1], 0), jnp.concatenate([g_nffn0, g_nffn1], 0),
                   g_head, jnp.concatenate([g_fcb0, g_fcb1], 0), g_final, g_wa2, g_bd.reshape(2, KD), g_scw,
                   jnp.stack([g_fcw0, g_fcw1]), loss8[:1]]
    pack1, offs1 = _pack_rows(small_grads, F32, 8)
    ag1 = exchange_start("ag_grads_start", [pack1], True, loss8)
    g_gin = mm("gla_in_dw", V(hcat, "tok"), V(dpcat, "tok"), form="tn", out="cols", out_dtype=BF16, shard_n=GLA_IN // N_DEV,
               after=ag1[4])
    g_gout = row_slots(mm("gla_out_dw", V(yin0, "tok"), V(dy0, "tok"), form="tn", out_dtype=BF16, after=ag1[4]))
    mine1, land1 = exchange_wait("ag_grads_wait", ag1, [g_gin, g_gout], True)
    g1 = lax.dynamic_update_index_in_dim(land1[0], mine1[0], me, 0)
    dmod_all = _unpack_rows(g1, offs1[:1], [small_grads[0].shape])[0]
    tot = _unpack_rows(sum_slots("sum_small", g1), offs1, [a.shape for a in small_grads])
    loss = tot[10][0, 0]
    dm_rows = dmod_all[:, :, :bsz].transpose(1, 0, 2, 3).reshape(2, N_DEV * bsz, 6 * D)
    dm_full = jnp.concatenate([dm_rows, tot[0][:, bsz:], jnp.zeros((2, ADA_ROWS - N_DEV * bsz - 1, 6 * D), F32)], 1)
    dm_mine = lax.dynamic_slice(dm_full, (0, 0, me * ADA_COLS), (2, ADA_ROWS, ADA_COLS))
    g_ada_w, g_ada_b, cpart = ada_bwd(cond, dm_mine, dm_full, ada_w)
    cparts = all_gather("ag_cctx", cpart).reshape(N_DEV, ADA_ROWS - ADA_CTX_ROW, D)[:, 0]
    g_cctx = cctx_grad(cparts, c_ctx[None])[0]
    tok = send_grads("gla", [g_gin, g_gout], after=g_cctx)

    def my_cols(full, n):
        return lax.dynamic_slice_in_dim(full, me * n, n, axis=full.ndim - 1)

    grads = {
        "c_ctx": g_cctx, "ada_b": g_ada_b.reshape(2, 6 * D), "norm_mix": tot[1], "norm_ffn": tot[2],
        "gla_head_norm": tot[3], "ffn_conv_b": tot[4], "final_norm": tot[5].reshape(D),
        "gla_w_a2": my_cols(tot[6], KD // N_DEV)[None], "gla_b_a": my_cols(tot[7], KD // N_DEV)[None],
        "sc_conv_w": my_cols(tot[8], D // N_DEV)[None], "ffn_conv_w": my_cols(tot[9], 2 * FFN_H // N_DEV),
    }

    res_ada = adamw("adamw_ada", *[a.reshape(2 * D, ADA_COLS) for a in (ada_w, g_ada_w, m_ada_w, v_ada_w)])
    grads["c_ctx"] = g_cctx + tok
    big = ["gla_w_in", "gla_w_out", "sc_w_in", "sc_w_out", "ffn_w_up", "ffn_w_down"]
    small = [n for n in names if n not in big and n != "ada_w"]
    g_small = _pack_rows([grads[n] for n in small], F32, 8)[0]
    res_small = adamw("adamw_small", _pack_rows([w_[n] for n in small], F32, 8)[0], g_small,
                      _pack_rows([m_[n] for n in small], F32, 8)[0], _pack_rows([v_[n] for n in small], F32, 8)[0])
    offs_s = _pack_rows([w_[n] for n in small], F32, 8)[1]

    big_res, done, me1 = {}, {"small": res_small[0], "ada_w": res_ada[0]}, jnp.reshape(me, (1,)).astype(jnp.int32)
    for g in groups:
        sent, lands = exchange_wait(f"a2a_{g}_wait", a2a_started[g], list(done.values()), False)
        for (n, i), mine, land in zip(groups[g], sent, lands):
            big_res[n] = adamw_slots(f"adamw_{n}{i}", w_[n], land, mine, me1, m_[n], v_[n], i, into=big_res.get(n))
            done[n] = big_res[n][0]

    out = {}
    for kind, idx in (("grad", 0), ("delta", 1), ("new_m", 2), ("new_v", 3)):
        vals = {n: big_res[n][idx] for n in big}
        vals["ada_w"] = res_ada[idx].reshape(ada_w.shape)
        vals.update(zip(small, _unpack_rows(res_small[idx], offs_s, [w_[n].shape for n in small])))
        out[kind] = [vals[n] for n in names]
    return (loss, grad_x, *out["grad"], *out["delta"], *out["new_m"], *out["new_v"])
```
